```python
import jax, jax.numpy as jnp
from jax import lax
import numpy as np

D_MODEL = 1024
BATCH = 8
SEQ = 2048
DEPTH = 1

CHUNK = 64
D_MIX = D_MODEL
D_HGRN = D_MIX // 2
D_CONV = D_MIX - D_HGRN
HGRN_HEAD_DIM = 128
HGRN_HEADS = D_HGRN // HGRN_HEAD_DIM
CONV_WIDTH = 3
CONV_GROUPS = 8
EPS = 1e-6
IN_COLS = 4 * D_HGRN + 4 * D_CONV

kernel_name = "hgrn2_shortconv_parallel_hybrid"


def rmsnorm(x, g):
    xf = x.astype(jnp.float32)
    y = xf * lax.rsqrt(jnp.mean(xf * xf, axis=-1, keepdims=True) + EPS)
    return (y * g.astype(jnp.float32)).astype(x.dtype)


def grouped_rmsnorm(x, g, n_groups):
    bsz, s, w = x.shape
    xf = x.astype(jnp.float32).reshape(bsz, s, n_groups, w // n_groups)
    y = xf * lax.rsqrt(jnp.mean(xf * xf, axis=-1, keepdims=True) + EPS)
    return (y.reshape(bsz, s, w) * g.astype(jnp.float32)).astype(x.dtype)


def hgrn2_chunkwise(q, log_f, k, v):
    bsz, s, h, dk = q.shape
    dv = v.shape[-1]
    nc = s // CHUNK

    def split(t):
        return t.reshape(bsz, nc, CHUNK, h, t.shape[-1]).transpose(0, 3, 1, 2, 4)

    q, log_f, k, v = (split(t).astype(jnp.float32) for t in (q, log_f, k, v))
    b = jnp.cumsum(log_f, axis=3)
    g = b[..., -1:, :]
    q_dec = q * jnp.exp(b)
    k_inv = k * jnp.exp(-b)
    k_end = k * jnp.exp(g - b)
    causal = jnp.tril(jnp.ones((CHUNK, CHUNK), dtype=bool))
    scores = jnp.einsum('bhnck,bhnsk->bhncs', q_dec, k_inv)
    scores = jnp.where(causal, scores, 0.0)
    o_intra = jnp.einsum('bhncs,bhnsv->bhncv', scores, v)
    chunk_update = jnp.einsum('bhnsk,bhnsv->bhnkv', k_end, v)
    chunk_decay = jnp.exp(g[..., 0, :])

    def step(state, inp):
        dec, upd = inp
        return dec[..., None] * state + upd, state

    s0 = jnp.zeros((bsz, h, dk, dv), jnp.float32)
    _, s_before = lax.scan(step, s0, (jnp.moveaxis(chunk_decay, 2, 0),
                                      jnp.moveaxis(chunk_update, 2, 0)))
    s_before = jnp.moveaxis(s_before, 0, 2)
    o_inter = jnp.einsum('bhnck,bhnkv->bhncv', q_dec, s_before)
    o = o_intra + o_inter
    return o.transpose(0, 2, 3, 1, 4).reshape(bsz, s, h, dv)


def causal_depthwise_conv(u, w):
    c = u.shape[-1]
    return lax.conv_general_dilated(
        u, w.astype(u.dtype)[:, None, :], window_strides=(1,),
        padding=[(CONV_WIDTH - 1, 0)],
        dimension_numbers=('NWC', 'WIO', 'NWC'), feature_group_count=c)


def _fwd_setup_inputs(seed: int = 0) -> dict:
    key = jax.random.key(seed)
    ks = jax.random.split(key, 9)
    f32 = jnp.float32
    x = jax.random.normal(ks[0], (BATCH, SEQ, D_MODEL), f32)
    norm_gain = 1.0 + 0.02 * jax.random.normal(ks[1], (DEPTH, D_MODEL), f32)
    w_in = jax.random.normal(ks[2], (DEPTH, D_MODEL, IN_COLS), f32) * D_MODEL ** -0.5
    lb_logits = 1.0 + 0.1 * jax.random.normal(ks[3], (DEPTH + 1, D_HGRN), f32)
    conv_w = jax.random.normal(ks[4], (DEPTH, CONV_WIDTH, D_CONV), f32) * CONV_WIDTH ** -0.5
    hgrn_norm_gain = 1.0 + 0.02 * jax.random.normal(ks[5], (DEPTH, D_HGRN), f32)
    conv_norm_gain = 1.0 + 0.02 * jax.random.normal(ks[6], (DEPTH, D_CONV), f32)
    w_out = jax.random.normal(ks[7], (DEPTH, D_MIX, D_MODEL), f32) * D_MIX ** -0.5
    final_norm_gain = 1.0 + 0.02 * jax.random.normal(ks[8], (D_MODEL,), f32)
    return {"x": x, "norm_gain": norm_gain, "w_in": w_in, "lb_logits": lb_logits,
            "conv_w": conv_w, "hgrn_norm_gain": hgrn_norm_gain,
            "conv_norm_gain": conv_norm_gain, "w_out": w_out,
            "final_norm_gain": final_norm_gain}


def _fwd_reference(x, norm_gain, w_in, lb_logits, conv_w, hgrn_norm_gain,
              conv_norm_gain, w_out, final_norm_gain):
    bsz, s, _ = x.shape
    lower_bounds = jnp.cumsum(jax.nn.softmax(lb_logits.astype(jnp.float32), axis=0), axis=0)
    splits = [D_HGRN, 2 * D_HGRN, 3 * D_HGRN, 4 * D_HGRN,
              4 * D_HGRN + D_CONV, 4 * D_HGRN + 2 * D_CONV, 4 * D_HGRN + 3 * D_CONV]
    for l in range(DEPTH):
        h = rmsnorm(x, norm_gain[l])
        proj = jnp.einsum('bsd,dc->bsc', h, w_in[l])
        q, f_logit, i_in, z_a, u, gate_b, gate_c, z_b = jnp.split(proj, splits, axis=-1)

        lb = lower_bounds[l]
        f = lb + (1.0 - lb) * jax.nn.sigmoid(f_logit.astype(jnp.float32))
        log_f = jnp.log(f)
        k = 1.0 - f
        hs = (bsz, s, HGRN_HEADS, HGRN_HEAD_DIM)
        o_a = hgrn2_chunkwise(q.reshape(hs), log_f.reshape(hs), k.reshape(hs), i_in.reshape(hs))
        o_a = grouped_rmsnorm(o_a.reshape(bsz, s, D_HGRN).astype(x.dtype),
                              hgrn_norm_gain[l], HGRN_HEADS)
        o_a = o_a * jax.nn.silu(z_a)

        y_b = gate_b * causal_depthwise_conv(gate_c * u, conv_w[l])
        o_b = grouped_rmsnorm(y_b, conv_norm_gain[l], CONV_GROUPS) * jax.nn.silu(z_b)

        mixed = jnp.concatenate([o_a, o_b], axis=-1)
        x = x + jnp.einsum('bsc,cd->bsd', mixed, w_out[l])
    return rmsnorm(x, final_norm_gain)


import jax as _jax
import jax.numpy as _jnp

TWIN_FORMAT = 'train_step'
FWD_PARAMS = ['x', 'norm_gain', 'w_in', 'lb_logits', 'conv_w', 'hgrn_norm_gain', 'conv_norm_gain', 'w_out', 'final_norm_gain']
TWIN_WEIGHTS = ['norm_gain', 'w_in', 'lb_logits', 'conv_w', 'hgrn_norm_gain', 'conv_norm_gain', 'w_out', 'final_norm_gain']
TWIN_DIFF_INPUT = 'x'
TWIN_INPUTS = ['x', 'norm_gain', 'w_in', 'lb_logits', 'conv_w', 'hgrn_norm_gain', 'conv_norm_gain', 'w_out', 'final_norm_gain', 'loss_target', 'm_norm_gain', 'm_w_in', 'm_lb_logits', 'm_conv_w', 'm_hgrn_norm_gain', 'm_conv_norm_gain', 'm_w_out', 'm_final_norm_gain', 'v_norm_gain', 'v_w_in', 'v_lb_logits', 'v_conv_w', 'v_hgrn_norm_gain', 'v_conv_norm_gain', 'v_w_out', 'v_final_norm_gain']
TWIN_OUTPUTS = ['loss', 'grad_x', 'grad_norm_gain', 'grad_w_in', 'grad_lb_logits', 'grad_conv_w', 'grad_hgrn_norm_gain', 'grad_conv_norm_gain', 'grad_w_out', 'grad_final_norm_gain', 'delta_norm_gain', 'delta_w_in', 'delta_lb_logits', 'delta_conv_w', 'delta_hgrn_norm_gain', 'delta_conv_norm_gain', 'delta_w_out', 'delta_final_norm_gain', 'new_m_norm_gain', 'new_m_w_in', 'new_m_lb_logits', 'new_m_conv_w', 'new_m_hgrn_norm_gain', 'new_m_conv_norm_gain', 'new_m_w_out', 'new_m_final_norm_gain', 'new_v_norm_gain', 'new_v_w_in', 'new_v_lb_logits', 'new_v_conv_w', 'new_v_hgrn_norm_gain', 'new_v_conv_norm_gain', 'new_v_w_out', 'new_v_final_norm_gain']
TWIN_LEAF_KINDS = {'loss': 'loss', 'grad_x': 'grad_x', 'grad_norm_gain': 'grad_w', 'grad_w_in': 'grad_w', 'grad_lb_logits': 'grad_w', 'grad_conv_w': 'grad_w', 'grad_hgrn_norm_gain': 'grad_w', 'grad_conv_norm_gain': 'grad_w', 'grad_w_out': 'grad_w', 'grad_final_norm_gain': 'grad_w', 'delta_norm_gain': 'delta_w', 'delta_w_in': 'delta_w', 'delta_lb_logits': 'delta_w', 'delta_conv_w': 'delta_w', 'delta_hgrn_norm_gain': 'delta_w', 'delta_conv_norm_gain': 'delta_w', 'delta_w_out': 'delta_w', 'delta_final_norm_gain': 'delta_w', 'new_m_norm_gain': 'new_m', 'new_m_w_in': 'new_m', 'new_m_lb_logits': 'new_m', 'new_m_conv_w': 'new_m', 'new_m_hgrn_norm_gain': 'new_m', 'new_m_conv_norm_gain': 'new_m', 'new_m_w_out': 'new_m', 'new_m_final_norm_gain': 'new_m', 'new_v_norm_gain': 'new_v', 'new_v_w_in': 'new_v', 'new_v_lb_logits': 'new_v', 'new_v_conv_w': 'new_v', 'new_v_hgrn_norm_gain': 'new_v', 'new_v_conv_norm_gain': 'new_v', 'new_v_w_out': 'new_v', 'new_v_final_norm_gain': 'new_v'}


def _forward(args):
    return _fwd_reference(*[args[k] for k in FWD_PARAMS])


def _output_shape():
    out = _jax.eval_shape(lambda: _forward(_fwd_setup_inputs(0)))
    return out.shape, out.dtype

N_MICROBATCH = 1
ADAM_LR = 0.001
ADAM_B1 = 0.9
ADAM_B2 = 0.999
ADAM_EPS = 1e-08
ADAM_WD = 0.01
ADAM_STEP = 10
PER_EXAMPLE_BATCH_AXIS = {'x': 0, 'loss_target': 0}
SHARED_INPUTS = []
_WEIGHT_DTYPES = {'norm_gain': _jnp.float32, 'w_in': _jnp.float32, 'lb_logits': _jnp.float32, 'conv_w': _jnp.float32, 'hgrn_norm_gain': _jnp.float32, 'conv_norm_gain': _jnp.float32, 'w_out': _jnp.float32, 'final_norm_gain': _jnp.float32}
MOMENT_SCALE = {'norm_gain': 1.413176e-01, 'w_in': 6.839301e-02, 'lb_logits': 4.024497e-02, 'conv_w': 6.714820e-02, 'hgrn_norm_gain': 6.423775e-02, 'conv_norm_gain': 6.526437e-02, 'w_out': 6.385571e-02, 'final_norm_gain': 1.599777e+01}


def _to_microbatches(a, axis):
    t = _jnp.moveaxis(a, axis, 0)
    t = t.reshape((N_MICROBATCH, t.shape[0] // N_MICROBATCH) + t.shape[1:])
    return _jnp.moveaxis(t, 1, axis + 1)


def setup_inputs(seed: int = 0) -> dict:
    inp = _fwd_setup_inputs(seed)
    key = _jax.random.fold_in(_jax.random.key(seed), 7919)
    shape, _ = _output_shape()
    out = dict(inp)
    out["loss_target"] = _jax.random.normal(_jax.random.fold_in(key, 0), shape, _jnp.float32)
    for i, name in enumerate(TWIN_WEIGHTS):
        w = inp[name].astype(_jnp.float32)
        if MOMENT_SCALE is None:
            s = _jnp.sqrt(_jnp.mean(_jnp.square(w)) + 1e-30)
        else:
            s = MOMENT_SCALE[name]
        km, kv = _jax.random.split(_jax.random.fold_in(key, i + 1))
        out[name] = w
        out["m_" + name] = s * _jax.random.normal(km, w.shape, _jnp.float32)
        out["v_" + name] = (s * s) * _jax.random.uniform(kv, w.shape, _jnp.float32, 0.5, 1.5)
    if N_MICROBATCH > 1:
        for name, axis in PER_EXAMPLE_BATCH_AXIS.items():
            out[name] = _to_microbatches(out[name], axis)
    return {'x': out['x'], 'norm_gain': out['norm_gain'], 'w_in': out['w_in'], 'lb_logits': out['lb_logits'], 'conv_w': out['conv_w'], 'hgrn_norm_gain': out['hgrn_norm_gain'], 'conv_norm_gain': out['conv_norm_gain'], 'w_out': out['w_out'], 'final_norm_gain': out['final_norm_gain'], 'loss_target': out['loss_target'], 'm_norm_gain': out['m_norm_gain'], 'm_w_in': out['m_w_in'], 'm_lb_logits': out['m_lb_logits'], 'm_conv_w': out['m_conv_w'], 'm_hgrn_norm_gain': out['m_hgrn_norm_gain'], 'm_conv_norm_gain': out['m_conv_norm_gain'], 'm_w_out': out['m_w_out'], 'm_final_norm_gain': out['m_final_norm_gain'], 'v_norm_gain': out['v_norm_gain'], 'v_w_in': out['v_w_in'], 'v_lb_logits': out['v_lb_logits'], 'v_conv_w': out['v_conv_w'], 'v_hgrn_norm_gain': out['v_hgrn_norm_gain'], 'v_conv_norm_gain': out['v_conv_norm_gain'], 'v_w_out': out['v_w_out'], 'v_final_norm_gain': out['v_final_norm_gain']}


def _loss(weights, diff, rest, loss_target):
    with _jax.named_scope("forward"):
        args = {**rest, TWIN_DIFF_INPUT: diff, **{k: w.astype(_WEIGHT_DTYPES[k]) for k, w in weights.items()}}
        y = _forward(args)
    with _jax.named_scope("loss_head"):
        err = _jnp.square(y.astype(_jnp.float32) - loss_target)
        return 0.5 * _jnp.sum(_jnp.mean(err, axis=-1)) if err.ndim else 0.5 * err


def _adamw(w, g, m, v):
    m = ADAM_B1 * m + (1.0 - ADAM_B1) * g
    v = ADAM_B2 * v + (1.0 - ADAM_B2) * _jnp.square(g)
    m_hat = m / (1.0 - ADAM_B1 ** ADAM_STEP)
    v_hat = v / (1.0 - ADAM_B2 ** ADAM_STEP)
    delta = -ADAM_LR * (m_hat / (_jnp.sqrt(v_hat) + ADAM_EPS) + ADAM_WD * w)
    return delta, m, v


def reference(x, norm_gain, w_in, lb_logits, conv_w, hgrn_norm_gain, conv_norm_gain, w_out, final_norm_gain, loss_target, m_norm_gain, m_w_in, m_lb_logits, m_conv_w, m_hgrn_norm_gain, m_conv_norm_gain, m_w_out, m_final_norm_gain, v_norm_gain, v_w_in, v_lb_logits, v_conv_w, v_hgrn_norm_gain, v_conv_norm_gain, v_w_out, v_final_norm_gain):
    given = dict(x=x, norm_gain=norm_gain, w_in=w_in, lb_logits=lb_logits, conv_w=conv_w, hgrn_norm_gain=hgrn_norm_gain, conv_norm_gain=conv_norm_gain, w_out=w_out, final_norm_gain=final_norm_gain, loss_target=loss_target, m_norm_gain=m_norm_gain, m_w_in=m_w_in, m_lb_logits=m_lb_logits, m_conv_w=m_conv_w, m_hgrn_norm_gain=m_hgrn_norm_gain, m_conv_norm_gain=m_conv_norm_gain, m_w_out=m_w_out, m_final_norm_gain=m_final_norm_gain, v_norm_gain=v_norm_gain, v_w_in=v_w_in, v_lb_logits=v_lb_logits, v_conv_w=v_conv_w, v_hgrn_norm_gain=v_hgrn_norm_gain, v_conv_norm_gain=v_conv_norm_gain, v_w_out=v_w_out, v_final_norm_gain=v_final_norm_gain)
    weights = {n: given[n] for n in TWIN_WEIGHTS}
    shared = {n: given[n] for n in SHARED_INPUTS}
    per_example = {n: given[n] for n in ['x']}
    grad_fn = _jax.value_and_grad(_loss, argnums=(0, 1))

    def one_microbatch(ex, loss_target):
        ex = dict(ex)
        diff = ex.pop(TWIN_DIFF_INPUT)
        return grad_fn(weights, diff, {**shared, **ex}, loss_target)

    if N_MICROBATCH == 1:
        loss, (grad_w, grad_x) = one_microbatch(per_example, given["loss_target"])
    else:
        def body(carry, xs):
            loss_sum, grad_sum = carry
            l_k, (gw_k, gx_k) = one_microbatch(xs[0], xs[1])
            with _jax.named_scope("update"):
                return (loss_sum + l_k, _jax.tree.map(_jnp.add, grad_sum, gw_k)), gx_k

        init = (_jnp.zeros((), _jnp.float32), _jax.tree.map(_jnp.zeros_like, weights))
        (loss, grad_w), grad_x = _jax.lax.scan(body, init, (per_example, given["loss_target"]))
    with _jax.named_scope("update"):
        delta_w, new_m, new_v = {}, {}, {}
        for n in TWIN_WEIGHTS:
            delta_w[n], new_m[n], new_v[n] = _adamw(weights[n], grad_w[n], given["m_" + n], given["v_" + n])
    return (loss, grad_x, *[grad_w[n] for n in TWIN_WEIGHTS], *[delta_w[n] for n in TWIN_WEIGHTS],
            *[new_m[n] for n in TWIN_WEIGHTS], *[new_v[n] for n in TWIN_WEIGHTS])
```

```python
import functools

import jax
import jax.numpy as jnp
from jax import lax
from jax.experimental import pallas as pl
from jax.experimental.pallas import tpu as pltpu

F32 = jnp.float32
BF16 = jnp.bfloat16
MESH = pl.DeviceIdType.MESH

SEQ = 2048
D_MODEL = 1024
D_HGRN = 512
D_CONV = 512
HEAD = 128
N_HEADS = 4
CHUNK = 64
CONV_GROUP = 64
N_SHARD = 4
SHARD_COLS = 1024
WO_ROWS = 256
EPS = 1e-6
TB = 256
NCB = TB // CHUNK
N_CHUNKS = SEQ // CHUNK
N_DEV = 8

ADAM_LR = 0.001
ADAM_B1 = 0.9
ADAM_B2 = 0.999
ADAM_EPS = 1e-08
ADAM_WD = 0.01
ADAM_STEP = 10

VMEM_LIMIT = 56 * 1024 * 1024


def _dot(a, b):
    return jnp.dot(a, b, preferred_element_type=F32)


def _dot_nt(a, b):
    return lax.dot_general(a, b, (((1,), (1,)), ((), ())), preferred_element_type=F32)


def _dot_tn(a, b):
    return lax.dot_general(a, b, (((0,), (0,)), ((), ())), preferred_element_type=F32)


def _split_bf16(x, n):
    parts = []
    r = x
    for _ in range(n):
        p = r.astype(BF16)
        parts.append(p)
        r = r - p.astype(F32)
    return parts


def _exact_left(m, x, n=3):
    acc = None
    for p in _split_bf16(x, n):
        t = _dot(m, p)
        acc = t if acc is None else acc + t
    return acc


def _group_mean(x, gmat, n=2):
    acc = None
    for p in _split_bf16(x, n):
        t = _dot(p, gmat)
        acc = t if acc is None else acc + t
    return acc


def _sigmoid(x):
    return 1.0 / (1.0 + jnp.exp(-x))


def _lower_bound(lbl):
    l0 = lbl[0:1, :]
    l1 = lbl[1:2, :]
    m = jnp.maximum(l0, l1)
    e0 = jnp.exp(l0 - m)
    e1 = jnp.exp(l1 - m)
    return e0 / (e0 + e1)


def _chunk_tri(lower):
    r = lax.broadcasted_iota(jnp.int32, (TB, TB), 0)
    c = lax.broadcasted_iota(jnp.int32, (TB, TB), 1)
    same = (r // CHUNK) == (c // CHUNK)
    tri = (c <= r) if lower else (c >= r)
    return jnp.where(same & tri, 1.0, 0.0).astype(BF16)


def _causal():
    r = lax.broadcasted_iota(jnp.int32, (CHUNK, CHUNK), 0)
    c = lax.broadcasted_iota(jnp.int32, (CHUNK, CHUNK), 1)
    return c <= r


def _shift_down(x, sh, prev_tail):
    r = pltpu.roll(x, sh, 0)
    pt = pltpu.roll(prev_tail, sh, 0)
    rows = lax.broadcasted_iota(jnp.int32, prev_tail.shape, 0)
    top = jnp.where(rows < sh, pt, r[0:8])
    return jnp.concatenate([top, r[8:]], axis=0)


def _shift_up(x, sh, next_head):
    n = x.shape[0]
    r = pltpu.roll(x, n - sh, 0)
    nh = pltpu.roll(next_head, 8 - sh, 0)
    rows = lax.broadcasted_iota(jnp.int32, next_head.shape, 0)
    bot = jnp.where(rows >= 8 - sh, nh, r[n - 8:])
    return jnp.concatenate([r[:n - 8], bot], axis=0)


def _group_matrix(width, group):
    r = jnp.arange(width)[:, None] // group
    c = jnp.arange(width)[None, :] // group
    return jnp.where(r == c, 1.0 / group, 0.0).astype(BF16)


def _gather_weights(w_in, w_out, conv_w):
    half_w = D_MODEL // 2
    half_o = WO_ROWS // 2

    def body(w_ref, wo_ref, cw_ref, wg_ref, wog_ref, cwg_ref, send_sems, recv_sems):
        x, y, c = lax.axis_index("x"), lax.axis_index("y"), lax.axis_index("c")
        k = 2 * x + y
        sibling = (x, y, 1 - c)
        chips = [(1 - x, y), (x, 1 - y), (1 - x, 1 - y)]

        wg_ref[k] = w_ref[0].astype(BF16)
        wog_ref[k] = wo_ref[0].astype(BF16)
        cwg_ref[k] = jnp.zeros((8, HEAD), F32)
        cwg_ref[k, 0:3, :] = cw_ref[0]

        def w_half(kk, cc):
            return wg_ref.at[kk, pl.ds(cc * half_w, half_w), :]

        def wo_half(kk, cc):
            return wog_ref.at[kk, pl.ds(cc * half_o, half_o), :]

        def copy(sem, ref, to):
            return pltpu.make_async_remote_copy(
                src_ref=ref, dst_ref=ref, send_sem=send_sems.at[sem], recv_sem=recv_sems.at[sem],
                device_id=to, device_id_type=MESH)

        first = []
        for j, chip in enumerate(chips):
            first.append(copy(j, w_half(k, c), (*chip, c)))
            first.append(copy(3 + j, wo_half(k, c), (*chip, c)))
            first.append(copy(6 + j, cwg_ref.at[k], (*chip, c)))
        for cp in first:
            cp.start()
        passed = []
        for j, (cx, cy) in enumerate(chips):
            kj = 2 * cx + cy
            copy(j, w_half(kj, c), (cx, cy, c)).wait_recv()
            fw = copy(9 + j, w_half(kj, c), sibling)
            fw.start()
            copy(3 + j, wo_half(kj, c), (cx, cy, c)).wait_recv()
            fo = copy(12 + j, wo_half(kj, c), sibling)
            fo.start()
            copy(6 + j, cwg_ref.at[kj], (cx, cy, c)).wait_recv()
            passed += [fw, fo]
        for j, (cx, cy) in enumerate(chips):
            kj = 2 * cx + cy
            copy(9 + j, w_half(kj, 1 - c), sibling).wait_recv()
            copy(12 + j, wo_half(kj, 1 - c), sibling).wait_recv()
        for cp in first + passed:
            cp.wait_send()

    vm = pl.BlockSpec(memory_space=pltpu.VMEM)
    return pl.pallas_call(
        body, name="gather_weights",
        out_shape=(jax.ShapeDtypeStruct((N_SHARD, D_MODEL, SHARD_COLS), BF16),
                   jax.ShapeDtypeStruct((N_SHARD, WO_ROWS, D_MODEL), BF16),
                   jax.ShapeDtypeStruct((N_SHARD, 8, HEAD), F32)),
        in_specs=[vm, vm, vm], out_specs=(vm, vm, vm),
        scratch_shapes=[pltpu.SemaphoreType.DMA((15,)), pltpu.SemaphoreType.DMA((15,))],
        compiler_params=pltpu.CompilerParams(vmem_limit_bytes=VMEM_LIMIT),
    )(w_in, w_out, conv_w)


def _reduce_grads(gw, gwo, small):
    hw = D_MODEL // 2
    ho = WO_ROWS // 2

    def body(gw_ref, gwo_ref, sm_ref, ow_ref, oo_ref, osm_ref,
             sib_w, sib_o, p_w, p_o, rcv_w, rcv_o, sm_buf, send_sems, recv_sems):
        x, y, c = lax.axis_index("x"), lax.axis_index("y"), lax.axis_index("c")
        k = 2 * x + y
        me = 4 * x + 2 * y + c
        sibling = (x, y, 1 - c)
        chips = [(1 - x, y), (x, 1 - y), (1 - x, 1 - y)]

        def copy(sem, src, dst, to):
            return pltpu.make_async_remote_copy(
                src_ref=src, dst_ref=dst, send_sem=send_sems.at[sem], recv_sem=recv_sems.at[sem],
                device_id=to, device_id_type=MESH)

        to_sib_w = copy(0, gw_ref.at[:, pl.ds((1 - c) * hw, hw), :], sib_w, sibling)
        to_sib_o = copy(1, gwo_ref.at[:, pl.ds((1 - c) * ho, ho), :], sib_o, sibling)
        to_sib_w.start()
        to_sib_o.start()
        sm_buf[me] = sm_ref[...]
        small_sends = []
        for m in range(1, N_DEV):
            px, py, pc = x ^ (m >> 2), y ^ ((m >> 1) & 1), c ^ (m & 1)
            cp = copy(1 + m, sm_buf.at[me], sm_buf.at[me], (px, py, pc))
            cp.start()
            small_sends.append(cp)

        to_sib_w.wait_recv()
        to_sib_o.wait_recv()
        for j in range(N_SHARD):
            p_w[j] = (gw_ref[j, pl.ds(c * hw, hw), :].astype(F32) + sib_w[j].astype(F32)).astype(BF16)
            p_o[j] = (gwo_ref[j, pl.ds(c * ho, ho), :].astype(F32) + sib_o[j].astype(F32)).astype(BF16)
        sends = []
        for j, (cx, cy) in enumerate(chips):
            kj = 2 * cx + cy
            cw = copy(9 + j, p_w.at[kj], rcv_w.at[j], (cx, cy, c))
            co = copy(12 + j, p_o.at[kj], rcv_o.at[j], (cx, cy, c))
            cw.start()
            co.start()
            sends += [cw, co]

        acc_w = gw_ref[k, pl.ds(c * hw, hw), :].astype(F32) + sib_w[k].astype(F32)
        acc_o = gwo_ref[k, pl.ds(c * ho, ho), :].astype(F32) + sib_o[k].astype(F32)
        for j, (cx, cy) in enumerate(chips):
            copy(9 + j, p_w.at[0], rcv_w.at[j], (cx, cy, c)).wait_recv()
            acc_w = acc_w + rcv_w[j].astype(F32)
            copy(12 + j, p_o.at[0], rcv_o.at[j], (cx, cy, c)).wait_recv()
            acc_o = acc_o + rcv_o[j].astype(F32)
        ow_ref[pl.ds(c * hw, hw), :] = acc_w
        oo_ref[pl.ds(c * ho, ho), :] = acc_o
        fin_w = copy(15, ow_ref.at[pl.ds(c * hw, hw), :], ow_ref.at[pl.ds(c * hw, hw), :], sibling)
        fin_o = copy(16, oo_ref.at[pl.ds(c * ho, ho), :], oo_ref.at[pl.ds(c * ho, ho), :], sibling)
        fin_w.start()
        fin_o.start()

        for m in range(1, N_DEV):
            copy(1 + m, sm_buf.at[0], sm_buf.at[0], sibling).wait_recv()
        tot = sm_buf[0]
        for d in range(1, N_DEV):
            tot = tot + sm_buf[d]
        osm_ref[...] = tot

        copy(15, ow_ref.at[pl.ds((1 - c) * hw, hw), :], ow_ref.at[pl.ds((1 - c) * hw, hw), :], sibling).wait_recv()
        copy(16, oo_ref.at[pl.ds((1 - c) * ho, ho), :], oo_ref.at[pl.ds((1 - c) * ho, ho), :], sibling).wait_recv()
        for cp in [to_sib_w, to_sib_o] + small_sends + sends + [fin_w, fin_o]:
            cp.wait_send()

    vm = pl.BlockSpec(memory_space=pltpu.VMEM)
    return pl.pallas_call(
        body, name="reduce_grads",
        out_shape=(jax.ShapeDtypeStruct((D_MODEL, SHARD_COLS), F32),
                   jax.ShapeDtypeStruct((WO_ROWS, D_MODEL), F32),
                   jax.ShapeDtypeStruct((8, D_MODEL), F32)),
        in_specs=[vm, vm, vm], out_specs=(vm, vm, vm),
        scratch_shapes=[
            pltpu.VMEM((N_SHARD, hw, SHARD_COLS), BF16), pltpu.VMEM((N_SHARD, ho, D_MODEL), BF16),
            pltpu.VMEM((N_SHARD, hw, SHARD_COLS), BF16), pltpu.VMEM((N_SHARD, ho, D_MODEL), BF16),
            pltpu.VMEM((3, hw, SHARD_COLS), BF16), pltpu.VMEM((3, ho, D_MODEL), BF16),
            pltpu.VMEM((N_DEV, 8, D_MODEL), F32),
            pltpu.SemaphoreType.DMA((17,)), pltpu.SemaphoreType.DMA((17,)),
        ],
        compiler_params=pltpu.CompilerParams(vmem_limit_bytes=VMEM_LIMIT),
    )(gw, gwo, small)


def _proj_fwd(x2d, g1, wg):
    def body(x_ref, g_ref, w_ref, h_ref, p_ref):
        xv = x_ref[...]
        r = lax.rsqrt(jnp.mean(xv * xv, axis=-1, keepdims=True) + EPS)
        h = (xv * r * g_ref[...]).astype(BF16)
        h_ref[...] = h
        for k in range(N_SHARD):
            p_ref[:, k * SHARD_COLS:(k + 1) * SHARD_COLS] = _dot(h, w_ref[k])

    return pl.pallas_call(
        body, name="proj_fwd", grid=(SEQ // TB,),
        out_shape=(jax.ShapeDtypeStruct((SEQ, D_MODEL), BF16),
                   jax.ShapeDtypeStruct((SEQ, N_SHARD * SHARD_COLS), F32)),
        in_specs=[pl.BlockSpec((TB, D_MODEL), lambda i: (i, 0)),
                  pl.BlockSpec((1, D_MODEL), lambda i: (0, 0)),
                  pl.BlockSpec((N_SHARD, D_MODEL, SHARD_COLS), lambda i: (0, 0, 0))],
        out_specs=(pl.BlockSpec((TB, D_MODEL), lambda i: (i, 0)),
                   pl.BlockSpec((TB, N_SHARD * SHARD_COLS), lambda i: (i, 0))),
        compiler_params=pltpu.CompilerParams(dimension_semantics=("arbitrary",), vmem_limit_bytes=VMEM_LIMIT),
    )(x2d, g1, wg)


def _mix_fwd(proj, lb_logits, cw, ga, gcn, g128, g64):
    def body(p_ref, lbl_ref, cw_ref, ga_ref, gcn_ref, g128_ref, g64_ref,
             mixed_ref, o_ref, cv_ref, sto_ref, st_ref, tail_ref, b_ref, f_ref):
        @pl.when(pl.program_id(0) == 0)
        def _():
            st_ref[...] = jnp.zeros_like(st_ref)
            tail_ref[...] = jnp.zeros_like(tail_ref)

        lb = _lower_bound(lbl_ref[...])
        f = lb + (1.0 - lb) * _sigmoid(p_ref[:, 512:1024])
        f_ref[...] = f
        b_ref[...] = _exact_left(_chunk_tri(True), jnp.log(f))
        causal = _causal()
        for n in range(NCB):
            sl = pl.ds(n * CHUNK, CHUNK)
            bc = b_ref[sl, :]
            g = b_ref[n * CHUNK + CHUNK - 1:n * CHUNK + CHUNK, :]
            kk = 1.0 - f_ref[sl, :]
            qd = (p_ref[sl, 0:512] * jnp.exp(bc)).astype(BF16)
            ki = (kk * jnp.exp(-bc)).astype(BF16)
            ke = (kk * jnp.exp(g - bc)).astype(BF16)
            vb = p_ref[sl, 1024:1536].astype(BF16)
            dec = jnp.exp(g)
            for hd in range(N_HEADS):
                cs = slice(hd * HEAD, (hd + 1) * HEAD)
                st = st_ref[hd]
                sto_ref[n, hd] = st
                sc = jnp.where(causal, _dot_nt(qd[:, cs], ki[:, cs]), 0.0)
                o_ref[sl, cs] = _dot(sc.astype(BF16), vb[:, cs]) + _dot_nt(qd[:, cs], st.astype(BF16))
                st_ref[hd] = st * dec[:, cs] + _dot_tn(vb[:, cs], ke[:, cs])

        o = o_ref[...]
        ra = lax.rsqrt(_group_mean(o * o, g128_ref[...]) + EPS)
        za = p_ref[:, 1536:2048]
        mixed_ref[:, 0:512] = (o * ra * ga_ref[...] * (za * _sigmoid(za))).astype(BF16)

        cu = p_ref[:, 3072:3584] * p_ref[:, 2048:2560]
        tail = tail_ref[...]
        cv = (cw_ref[0:1, :] * _shift_down(cu, 2, tail) + cw_ref[1:2, :] * _shift_down(cu, 1, tail)
              + cw_ref[2:3, :] * cu)
        tail_ref[...] = cu[TB - 8:, :]
        cv_ref[...] = cv
        yb = p_ref[:, 2560:3072] * cv
        rb = lax.rsqrt(_group_mean(yb * yb, g64_ref[...]) + EPS)
        zb = p_ref[:, 3584:4096]
        mixed_ref[:, 512:1024] = (yb * rb * gcn_ref[...] * (zb * _sigmoid(zb))).astype(BF16)

    row = lambda w: pl.BlockSpec((1, w), lambda i: (0, 0))
    return pl.pallas_call(
        body, name="mix_fwd", grid=(SEQ // TB,),
        out_shape=(jax.ShapeDtypeStruct((SEQ, D_MODEL), BF16),
                   jax.ShapeDtypeStruct((SEQ, D_HGRN), F32),
                   jax.ShapeDtypeStruct((SEQ, D_CONV), F32),
                   jax.ShapeDtypeStruct((N_CHUNKS, N_HEADS, HEAD, HEAD), F32)),
        in_specs=[pl.BlockSpec((TB, 4096), lambda i: (i, 0)),
                  pl.BlockSpec((2, D_HGRN), lambda i: (0, 0)),
                  pl.BlockSpec((8, D_CONV), lambda i: (0, 0)),
                  row(D_HGRN), row(D_CONV),
                  pl.BlockSpec((D_HGRN, D_HGRN), lambda i: (0, 0)),
                  pl.BlockSpec((D_CONV, D_CONV), lambda i: (0, 0))],
        out_specs=(pl.BlockSpec((TB, D_MODEL), lambda i: (i, 0)),
                   pl.BlockSpec((TB, D_HGRN), lambda i: (i, 0)),
                   pl.BlockSpec((TB, D_CONV), lambda i: (i, 0)),
                   pl.BlockSpec((NCB, N_HEADS, HEAD, HEAD), lambda i: (i, 0, 0, 0))),
        scratch_shapes=[pltpu.VMEM((N_HEADS, HEAD, HEAD), F32), pltpu.VMEM((8, D_CONV), F32),
                        pltpu.VMEM((TB, D_HGRN), F32), pltpu.VMEM((TB, D_HGRN), F32)],
        compiler_params=pltpu.CompilerParams(dimension_semantics=("arbitrary",), vmem_limit_bytes=VMEM_LIMIT),
    )(proj, lb_logits, cw, ga, gcn, g128, g64)


def _out_loss(x2d, mixed, wog, gf, tgt):
    def body(x_ref, m_ref, wo_ref, gf_ref, t_ref, dx2_ref, dm_ref, gwo_ref, part_ref, acc_ref):
        i = pl.program_id(0)

        @pl.when(i == 0)
        def _():
            acc_ref[...] = jnp.zeros_like(acc_ref)
            part_ref[...] = jnp.zeros_like(part_ref)

        mixed_b = m_ref[...]
        x2 = x_ref[...] + _dot(mixed_b, wo_ref[...])
        r2 = lax.rsqrt(jnp.mean(x2 * x2, axis=-1, keepdims=True) + EPS)
        n2 = x2 * r2
        gfv = gf_ref[...]
        err = n2 * gfv - t_ref[...]
        loss = 0.5 * jnp.sum(jnp.mean(err * err, axis=-1, keepdims=True), axis=0, keepdims=True)
        dy = err * (1.0 / D_MODEL)
        part_ref[0:1, :] += jnp.sum(dy * n2, axis=0, keepdims=True)
        part_ref[1:2, :] += jnp.broadcast_to(loss, (1, D_MODEL))
        dn = dy * gfv
        dx2 = r2 * (dn - n2 * jnp.mean(dn * n2, axis=-1, keepdims=True))
        dx2_ref[...] = dx2
        dx2_b = dx2.astype(BF16)
        dm_ref[...] = _dot_nt(dx2_b, wo_ref[...])
        acc_ref[...] += _dot_tn(mixed_b, dx2_b)

        @pl.when(i == pl.num_programs(0) - 1)
        def _():
            gwo_ref[...] = acc_ref[...].astype(BF16)

    blk = lambda: pl.BlockSpec((TB, D_MODEL), lambda i: (i, 0))
    return pl.pallas_call(
        body, name="out_loss", grid=(SEQ // TB,),
        out_shape=(jax.ShapeDtypeStruct((SEQ, D_MODEL), F32),
                   jax.ShapeDtypeStruct((SEQ, D_MODEL), F32),
                   jax.ShapeDtypeStruct((D_MODEL, D_MODEL), BF16),
                   jax.ShapeDtypeStruct((8, D_MODEL), F32)),
        in_specs=[blk(), blk(), pl.BlockSpec((D_MODEL, D_MODEL), lambda i: (0, 0)),
                  pl.BlockSpec((1, D_MODEL), lambda i: (0, 0)), blk()],
        out_specs=(blk(), blk(), pl.BlockSpec((D_MODEL, D_MODEL), lambda i: (0, 0)),
                   pl.BlockSpec((8, D_MODEL), lambda i: (0, 0))),
        scratch_shapes=[pltpu.VMEM((D_MODEL, D_MODEL), F32)],
        compiler_params=pltpu.CompilerParams(dimension_semantics=("arbitrary",), vmem_limit_bytes=VMEM_LIMIT),
    )(x2d, mixed, wog, gf, tgt)


def _mix_bwd(proj, o, cv, states, dmixed, lb_logits, cw, ga, gcn, g128, g64):
    nblk = SEQ // TB

    def body(p_ref, o_ref, cv_ref, st_ref, dm_ref, lbl_ref, cw_ref, ga_ref, gcn_ref, g128_ref, g64_ref,
             dp_ref, part_ref, dst_ref, head_ref, b_ref, f_ref, do_ref, db_ref, dg_ref, dk_ref, dlb_ref):
        i = pl.program_id(0)

        @pl.when(i == 0)
        def _():
            dst_ref[...] = jnp.zeros_like(dst_ref)
            head_ref[...] = jnp.zeros_like(head_ref)
            part_ref[...] = jnp.zeros_like(part_ref)
            dlb_ref[...] = jnp.zeros_like(dlb_ref)

        ov = o_ref[...]
        ra = lax.rsqrt(_group_mean(ov * ov, g128_ref[...]) + EPS)
        na = ov * ra
        za = p_ref[:, 1536:2048]
        sg = _sigmoid(za)
        dma = dm_ref[:, 0:512]
        gav = ga_ref[...]
        part_ref[3:4, :] += jnp.sum(dma * na * (za * sg), axis=0, keepdims=True)
        dp_ref[:, 1536:2048] = (dma * na * gav * (sg * (1.0 + za * (1.0 - sg)))).astype(BF16)
        dna = dma * gav * (za * sg)
        do_ref[...] = ra * (dna - na * _group_mean(dna * na, g128_ref[...]))

        cvv = cv_ref[...]
        gb = p_ref[:, 2560:3072]
        yb = gb * cvv
        rb = lax.rsqrt(_group_mean(yb * yb, g64_ref[...]) + EPS)
        nb = yb * rb
        zb = p_ref[:, 3584:4096]
        sgb = _sigmoid(zb)
        dmb = dm_ref[:, 512:1024]
        gcv = gcn_ref[...]
        part_ref[4:5, :] += jnp.sum(dmb * nb * (zb * sgb), axis=0, keepdims=True)
        dp_ref[:, 3584:4096] = (dmb * nb * gcv * (sgb * (1.0 + zb * (1.0 - sgb)))).astype(BF16)
        dnb = dmb * gcv * (zb * sgb)
        dyb = rb * (dnb - nb * _group_mean(dnb * nb, g64_ref[...]))
        dp_ref[:, 2560:3072] = (dyb * cvv).astype(BF16)
        dcv = dyb * gb
        head = head_ref[...]
        dcv1 = _shift_up(dcv, 1, head)
        dcv2 = _shift_up(dcv, 2, head)
        head_ref[...] = dcv[0:8, :]
        u = p_ref[:, 2048:2560]
        gc = p_ref[:, 3072:3584]
        cu = gc * u
        part_ref[0:1, :] += jnp.sum(dcv2 * cu, axis=0, keepdims=True)
        part_ref[1:2, :] += jnp.sum(dcv1 * cu, axis=0, keepdims=True)
        part_ref[2:3, :] += jnp.sum(dcv * cu, axis=0, keepdims=True)
        dcu = cw_ref[2:3, :] * dcv + cw_ref[1:2, :] * dcv1 + cw_ref[0:1, :] * dcv2
        dp_ref[:, 3072:3584] = (dcu * u).astype(BF16)
        dp_ref[:, 2048:2560] = (dcu * gc).astype(BF16)

        lb = _lower_bound(lbl_ref[...])
        s = _sigmoid(p_ref[:, 512:1024])
        f = lb + (1.0 - lb) * s
        f_ref[...] = f
        b_ref[...] = _exact_left(_chunk_tri(True), jnp.log(f))
        causal = _causal()
        for n in reversed(range(NCB)):
            sl = pl.ds(n * CHUNK, CHUNK)
            bc = b_ref[sl, :]
            g = b_ref[n * CHUNK + CHUNK - 1:n * CHUNK + CHUNK, :]
            kk = 1.0 - f_ref[sl, :]
            eb = jnp.exp(bc)
            enb = jnp.exp(-bc)
            eg = jnp.exp(g - bc)
            dec = jnp.exp(g)
            qd = p_ref[sl, 0:512] * eb
            ki = kk * enb
            ke = kk * eg
            qd_b = qd.astype(BF16)
            ki_b = ki.astype(BF16)
            ke_b = ke.astype(BF16)
            vb = p_ref[sl, 1024:1536].astype(BF16)
            do_b = do_ref[sl, :].astype(BF16)
            for hd in range(N_HEADS):
                cs = slice(hd * HEAD, (hd + 1) * HEAD)
                st = st_ref[n, hd]
                dst = dst_ref[hd]
                st_b = st.astype(BF16)
                dst_b = dst.astype(BF16)
                sc = jnp.where(causal, _dot_nt(qd_b[:, cs], ki_b[:, cs]), 0.0).astype(BF16)
                am = jnp.where(causal, _dot_nt(do_b[:, cs], vb[:, cs]), 0.0).astype(BF16)
                dqd = _dot(am, ki_b[:, cs]) + _dot(do_b[:, cs], st_b)
                dki = _dot_tn(am, qd_b[:, cs])
                dke = _dot(vb[:, cs], dst_b)
                dv = _dot_tn(sc, do_b[:, cs]) + _dot_nt(ke_b[:, cs], dst_b)
                ddec = jnp.sum(dst * st, axis=0, keepdims=True)
                dst_ref[hd] = dst * dec[:, cs] + _dot_tn(do_b[:, cs], qd_b[:, cs])
                dp_ref[sl, cs] = (dqd * eb[:, cs]).astype(BF16)
                dp_ref[sl, 1024 + hd * HEAD:1024 + (hd + 1) * HEAD] = dv.astype(BF16)
                dk_ref[sl, cs] = dki * enb[:, cs] + dke * eg[:, cs]
                db_ref[sl, cs] = dqd * qd[:, cs] - dki * ki[:, cs] - dke * ke[:, cs]
                dgv = jnp.sum(dke * ke[:, cs], axis=0, keepdims=True) + ddec * dec[:, cs]
                dg_ref[sl, cs] = jnp.broadcast_to(dgv, (CHUNK, HEAD))

        dlogf = _exact_left(_chunk_tri(False), db_ref[...]) + dg_ref[...]
        df = dlogf / f - dk_ref[...]
        dlb_ref[...] += jnp.sum(df * (1.0 - s), axis=0, keepdims=True)
        dp_ref[:, 512:1024] = (df * (1.0 - lb) * s * (1.0 - s)).astype(BF16)

        @pl.when(i == nblk - 1)
        def _():
            row = dlb_ref[...] * lb * (1.0 - lb)
            part_ref[5:6, :] = row
            part_ref[6:7, :] = -row

    rev = lambda w: pl.BlockSpec((TB, w), lambda i: (nblk - 1 - i, 0))
    row = lambda w: pl.BlockSpec((1, w), lambda i: (0, 0))
    return pl.pallas_call(
        body, name="mix_bwd", grid=(nblk,),
        out_shape=(jax.ShapeDtypeStruct((SEQ, 4096), BF16),
                   jax.ShapeDtypeStruct((8, D_HGRN), F32)),
        in_specs=[rev(4096), rev(D_HGRN), rev(D_CONV),
                  pl.BlockSpec((NCB, N_HEADS, HEAD, HEAD), lambda i: (nblk - 1 - i, 0, 0, 0)),
                  rev(D_MODEL),
                  pl.BlockSpec((2, D_HGRN), lambda i: (0, 0)),
                  pl.BlockSpec((8, D_CONV), lambda i: (0, 0)),
                  row(D_HGRN), row(D_CONV),
                  pl.BlockSpec((D_HGRN, D_HGRN), lambda i: (0, 0)),
                  pl.BlockSpec((D_CONV, D_CONV), lambda i: (0, 0))],
        out_specs=(rev(4096), pl.BlockSpec((8, D_HGRN), lambda i: (0, 0))),
        scratch_shapes=[pltpu.VMEM((N_HEADS, HEAD, HEAD), F32), pltpu.VMEM((8, D_CONV), F32),
                        pltpu.VMEM((TB, D_HGRN), F32), pltpu.VMEM((TB, D_HGRN), F32),
                        pltpu.VMEM((TB, D_HGRN), F32), pltpu.VMEM((TB, D_HGRN), F32),
                        pltpu.VMEM((TB, D_HGRN), F32), pltpu.VMEM((TB, D_HGRN), F32),
                        pltpu.VMEM((1, D_HGRN), F32)],
        compiler_params=pltpu.CompilerParams(dimension_semantics=("arbitrary",), vmem_limit_bytes=VMEM_LIMIT),
    )(proj, o, cv, states, dmixed, lb_logits, cw, ga, gcn, g128, g64)


def _grad_w_in(h, dproj):
    tk = 512
    nt = SEQ // tk

    def body(h_ref, dp_ref, out_ref, acc_ref):
        t = pl.program_id(1)

        @pl.when(t == 0)
        def _():
            acc_ref[...] = jnp.zeros_like(acc_ref)

        acc_ref[...] += _dot_tn(h_ref[...], dp_ref[...])

        @pl.when(t == nt - 1)
        def _():
            out_ref[0] = acc_ref[...].astype(BF16)

    return pl.pallas_call(
        body, name="grad_w_in", grid=(N_SHARD, nt),
        out_shape=jax.ShapeDtypeStruct((N_SHARD, D_MODEL, SHARD_COLS), BF16),
        in_specs=[pl.BlockSpec((tk, D_MODEL), lambda k, t: (t, 0)),
                  pl.BlockSpec((tk, SHARD_COLS), lambda k, t: (t, k))],
        out_specs=pl.BlockSpec((1, D_MODEL, SHARD_COLS), lambda k, t: (k, 0, 0)),
        scratch_shapes=[pltpu.VMEM((D_MODEL, SHARD_COLS), F32)],
        compiler_params=pltpu.CompilerParams(dimension_semantics=("arbitrary", "arbitrary"),
                                             vmem_limit_bytes=VMEM_LIMIT),
    )(h, dproj)


def _grad_x(dproj, wg, x2d, g1, dx2):
    def body(dp_ref, w_ref, x_ref, g_ref, dx2_ref, gx_ref, part_ref):
        @pl.when(pl.program_id(0) == 0)
        def _():
            part_ref[...] = jnp.zeros_like(part_ref)

        dh = None
        for k in range(N_SHARD):
            t = _dot_nt(dp_ref[:, k * SHARD_COLS:(k + 1) * SHARD_COLS], w_ref[k])
            dh = t if dh is None else dh + t
        xv = x_ref[...]
        r = lax.rsqrt(jnp.mean(xv * xv, axis=-1, keepdims=True) + EPS)
        xn = xv * r
        part_ref[0:1, :] += jnp.sum(dh * xn, axis=0, keepdims=True)
        dxn = dh * g_ref[...]
        gx_ref[...] = dx2_ref[...] + r * (dxn - xn * jnp.mean(dxn * xn, axis=-1, keepdims=True))

    blk = lambda: pl.BlockSpec((TB, D_MODEL), lambda i: (i, 0))
    return pl.pallas_call(
        body, name="grad_x", grid=(SEQ // TB,),
        out_shape=(jax.ShapeDtypeStruct((SEQ, D_MODEL), F32),
                   jax.ShapeDtypeStruct((8, D_MODEL), F32)),
        in_specs=[pl.BlockSpec((TB, 4096), lambda i: (i, 0)),
                  pl.BlockSpec((N_SHARD, D_MODEL, SHARD_COLS), lambda i: (0, 0, 0)),
                  blk(), pl.BlockSpec((1, D_MODEL), lambda i: (0, 0)), blk()],
        out_specs=(blk(), pl.BlockSpec((8, D_MODEL), lambda i: (0, 0))),
        compiler_params=pltpu.CompilerParams(dimension_semantics=("arbitrary",), vmem_limit_bytes=VMEM_LIMIT),
    )(dproj, wg, x2d, g1, dx2)


def _adamw(w, g, m, v, name):
    rows, cols = w.shape
    tr = rows if rows <= 256 else 256

    def body(w_ref, g_ref, m_ref, v_ref, d_ref, nm_ref, nv_ref):
        gv = g_ref[...]
        nm = ADAM_B1 * m_ref[...] + (1.0 - ADAM_B1) * gv
        nv = ADAM_B2 * v_ref[...] + (1.0 - ADAM_B2) * (gv * gv)
        m_hat = nm / (1.0 - ADAM_B1 ** ADAM_STEP)
        v_hat = nv / (1.0 - ADAM_B2 ** ADAM_STEP)
        d_ref[...] = -ADAM_LR * (m_hat / (jnp.sqrt(v_hat) + ADAM_EPS) + ADAM_WD * w_ref[...])
        nm_ref[...] = nm
        nv_ref[...] = nv

    blk = lambda: pl.BlockSpec((tr, cols), lambda i: (i, 0))
    shp = jax.ShapeDtypeStruct((rows, cols), F32)
    return pl.pallas_call(
        body, name=name, grid=(rows // tr,),
        out_shape=(shp, shp, shp),
        in_specs=[blk(), blk(), blk(), blk()], out_specs=(blk(), blk(), blk()),
        compiler_params=pltpu.CompilerParams(dimension_semantics=("arbitrary",)),
    )(w, g, m, v)


def _local_step(x2d, tgt, g1, wg, lb_logits, cw, ga, gcn, wog, gf):
    g128 = _group_matrix(D_HGRN, HEAD)
    g64 = _group_matrix(D_CONV, CONV_GROUP)
    h, proj = _proj_fwd(x2d, g1, wg)
    mixed, o, cv, states = _mix_fwd(proj, lb_logits, cw, ga, gcn, g128, g64)
    dx2, dmixed, gwo, part_out = _out_loss(x2d, mixed, wog, gf, tgt)
    dproj, part_mix = _mix_bwd(proj, o, cv, states, dmixed, lb_logits, cw, ga, gcn, g128, g64)
    gw = _grad_w_in(h, dproj)
    grad_x, part_x = _grad_x(dproj, wg, x2d, g1, dx2)
    zeros = jnp.zeros((3, D_MODEL - D_CONV), F32)
    small = jnp.concatenate([
        part_x[0:1], part_out[0:1],
        jnp.concatenate([part_mix[3:4], part_mix[4:5]], axis=1),
        jnp.concatenate([part_mix[5:6], part_mix[6:7]], axis=1),
        jnp.concatenate([part_mix[0:3], zeros], axis=1),
        part_out[1:2]], axis=0)
    return grad_x, gw, gwo, small


def kernel(x, norm_gain, w_in, lb_logits, conv_w, hgrn_norm_gain, conv_norm_gain, w_out, final_norm_gain, loss_target, m_norm_gain, m_w_in, m_lb_logits, m_conv_w, m_hgrn_norm_gain, m_conv_norm_gain, m_w_out, m_final_norm_gain, v_norm_gain, v_w_in, v_lb_logits, v_conv_w, v_hgrn_norm_gain, v_conv_norm_gain, v_w_out, v_final_norm_gain):
    k = 2 * lax.axis_index("x") + lax.axis_index("y")
    wg, wog4, cwg = _gather_weights(w_in, w_out, conv_w)
    wog = wog4.reshape(D_MODEL, D_MODEL)
    cw = jnp.transpose(cwg, (1, 0, 2)).reshape(8, D_CONV)

    grad_x, gw, gwo, small = _local_step(
        x[0], loss_target[0], norm_gain, wg, lb_logits, cw, hgrn_norm_gain, conv_norm_gain, wog,
        final_norm_gain.reshape(1, D_MODEL))
    g_w_in, g_w_out, tot = _reduce_grads(gw, gwo.reshape(N_SHARD, WO_ROWS, D_MODEL), small)

    loss = tot[7, 0]
    g_norm_gain = tot[0:1]
    g_final = tot[1]
    g_hgrn = tot[2:3, 0:D_HGRN]
    g_convn = tot[2:3, D_HGRN:]
    g_lb = jnp.concatenate([tot[3:4, 0:D_HGRN], tot[3:4, D_HGRN:]], axis=0)
    g_conv_w = lax.dynamic_slice(tot[4:7, 0:D_CONV], (0, k * HEAD), (3, HEAD))

    d_w_in, nm_w_in, nv_w_in = _adamw(w_in[0], g_w_in, m_w_in[0], v_w_in[0], "adamw_w_in")
    d_w_out, nm_w_out, nv_w_out = _adamw(w_out[0], g_w_out, m_w_out[0], v_w_out[0], "adamw_w_out")
    d_ng, nm_ng, nv_ng = _adamw(norm_gain, g_norm_gain, m_norm_gain, v_norm_gain, "adamw_norm_gain")
    d_lb, nm_lb, nv_lb = _adamw(lb_logits, g_lb, m_lb_logits, v_lb_logits, "adamw_lb_logits")
    d_cw, nm_cw, nv_cw = _adamw(conv_w[0], g_conv_w, m_conv_w[0], v_conv_w[0], "adamw_conv_w")
    d_hg, nm_hg, nv_hg = _adamw(hgrn_norm_gain, g_hgrn, m_hgrn_norm_gain, v_hgrn_norm_gain, "adamw_hgrn_gain")
    d_cg, nm_cg, nv_cg = _adamw(conv_norm_gain, g_convn, m_conv_norm_gain, v_conv_norm_gain, "adamw_conv_gain")
    d_fg, nm_fg, nv_fg = _adamw(final_norm_gain.reshape(1, D_MODEL), g_final.reshape(1, D_MODEL),
                                m_final_norm_gain.reshape(1, D_MODEL), v_final_norm_gain.reshape(1, D_MODEL),
                                "adamw_final_gain")
    flat = lambda a: a.reshape(D_MODEL)
    return (loss, grad_x[None],
            g_norm_gain, g_w_in[None], g_lb, g_conv_w[None], g_hgrn, g_convn, g_w_out[None], g_final,
            d_ng, d_w_in[None], d_lb, d_cw[None], d_hg, d_cg, d_w_out[None], flat(d_fg),
            nm_ng, nm_w_in[None], nm_lb, nm_cw[None], nm_hg, nm_cg, nm_w_out[None], flat(nm_fg),
            nv_ng, nv_w_in[None], nv_lb, nv_cw[None], nv_hg, nv_cg, nv_w_out[None], flat(nv_fg))
```

```python
import functools

import jax
import jax.numpy as jnp
from jax import lax
from jax.experimental import pallas as pl
from jax.experimental.pallas import tpu as pltpu

F32 = jnp.float32
BF16 = jnp.bfloat16
MESH = pl.DeviceIdType.MESH

SEQ = 2048
D_MODEL = 1024
D_HGRN = 512
D_CONV = 512
HEAD = 128
N_HEADS = 4
CHUNK = 64
CONV_GROUP = 64
N_SHARD = 4
SHARD_COLS = 1024
WO_ROWS = 256
EPS = 1e-6
TB = 256
NCB = TB // CHUNK
N_CHUNKS = SEQ // CHUNK
N_DEV = 8

ADAM_LR = 0.001
ADAM_B1 = 0.9
ADAM_B2 = 0.999
ADAM_EPS = 1e-08
ADAM_WD = 0.01
ADAM_STEP = 10

VMEM_LIMIT = 56 * 1024 * 1024


def _dot(a, b):
    return jnp.dot(a, b, preferred_element_type=F32)


def _dot_nt(a, b):
    return lax.dot_general(a, b, (((1,), (1,)), ((), ())), preferred_element_type=F32)


def _dot_tn(a, b):
    return lax.dot_general(a, b, (((0,), (0,)), ((), ())), preferred_element_type=F32)


def _split_bf16(x, n):
    parts = []
    r = x
    for _ in range(n):
        p = r.astype(BF16)
        parts.append(p)
        r = r - p.astype(F32)
    return parts


def _exact_left(m, x, n=3):
    acc = None
    for p in _split_bf16(x, n):
        t = _dot(m, p)
        acc = t if acc is None else acc + t
    return acc


def _group_mean(x, gmat, n=2):
    acc = None
    for p in _split_bf16(x, n):
        t = _dot(p, gmat)
        acc = t if acc is None else acc + t
    return acc


def _sigmoid(x):
    return 1.0 / (1.0 + jnp.exp(-x))


def _lower_bound(lbl):
    l0 = lbl[0:1, :]
    l1 = lbl[1:2, :]
    m = jnp.maximum(l0, l1)
    e0 = jnp.exp(l0 - m)
    e1 = jnp.exp(l1 - m)
    return e0 / (e0 + e1)


def _chunk_tri(lower):
    r = lax.broadcasted_iota(jnp.int32, (TB, TB), 0)
    c = lax.broadcasted_iota(jnp.int32, (TB, TB), 1)
    same = (r // CHUNK) == (c // CHUNK)
    tri = (c <= r) if lower else (c >= r)
    return jnp.where(same & tri, 1.0, 0.0).astype(BF16)


def _causal():
    r = lax.broadcasted_iota(jnp.int32, (CHUNK, CHUNK), 0)
    c = lax.broadcasted_iota(jnp.int32, (CHUNK, CHUNK), 1)
    return c <= r


def _shift_down(x, sh, prev_tail):
    r = pltpu.roll(x, sh, 0)
    pt = pltpu.roll(prev_tail, sh, 0)
    rows = lax.broadcasted_iota(jnp.int32, prev_tail.shape, 0)
    top = jnp.where(rows < sh, pt, r[0:8])
    return jnp.concatenate([top, r[8:]], axis=0)


def _shift_up(x, sh, next_head):
    n = x.shape[0]
    r = pltpu.roll(x, n - sh, 0)
    nh = pltpu.roll(next_head, 8 - sh, 0)
    rows = lax.broadcasted_iota(jnp.int32, next_head.shape, 0)
    bot = jnp.where(rows >= 8 - sh, nh, r[n - 8:])
    return jnp.concatenate([r[:n - 8], bot], axis=0)


def _group_matrix(width, group):
    r = jnp.arange(width)[:, None] // group
    c = jnp.arange(width)[None, :] // group
    return jnp.where(r == c, 1.0 / group, 0.0).astype(BF16)


TP = 512
SEM_W, SEM_WO, SEM_CW, SEM_W_FWD, SEM_WO_FWD, N_SEM = 0, 3, 6, 9, 12, 15


def _gather_proj(kidx, x2d, g1, w_in, w_out, conv_w):
    half_w = D_MODEL // 2
    half_o = WO_ROWS // 2
    nt = SEQ // TP

    def body(k_ref, x_ref, g_ref, w_ref, wo_ref, cw_ref, h_ref, p_ref, wg_out, wog_out, cwg_out,
             wg_v, wog_v, cwg_v, send_sems, recv_sems, out_sems):
        s, t = pl.program_id(0), pl.program_id(1)
        x, y, c = lax.axis_index("x"), lax.axis_index("y"), lax.axis_index("c")
        k = 2 * x + y
        sibling = (x, y, 1 - c)
        chips = [(1 - x, y), (x, 1 - y), (1 - x, 1 - y)]
        kjs = [2 * cx + cy for cx, cy in chips]

        def w_half(kk, cc):
            return wg_v.at[kk, pl.ds(cc * half_w, half_w), :]

        def wo_half(kk, cc):
            return wog_v.at[kk, pl.ds(cc * half_o, half_o), :]

        def copy(sem, ref, to):
            return pltpu.make_async_remote_copy(
                src_ref=ref, dst_ref=ref, send_sem=send_sems.at[sem], recv_sem=recv_sems.at[sem],
                device_id=to, device_id_type=MESH)

        def at_step(sv, tv):
            return pl.when((s == sv) & (t == tv))

        w_direct = [copy(SEM_W + j, w_half(k, c), (*chip, c)) for j, chip in enumerate(chips)]
        wo_direct = [copy(SEM_WO + j, wo_half(k, c), (*chip, c)) for j, chip in enumerate(chips)]
        cw_direct = [copy(SEM_CW + j, cwg_v.at[k], (*chip, c)) for j, chip in enumerate(chips)]
        w_passed = [copy(SEM_W_FWD + j, w_half(kj, c), sibling) for j, kj in enumerate(kjs)]
        wo_passed = [copy(SEM_WO_FWD + j, wo_half(kj, c), sibling) for j, kj in enumerate(kjs)]
        stores = ([pltpu.make_async_copy(wg_v.at[kk], wg_out.at[kk], out_sems.at[i])
                   for i, kk in enumerate([k] + kjs)]
                  + [pltpu.make_async_copy(wog_v, wog_out, out_sems.at[4]),
                     pltpu.make_async_copy(cwg_v, cwg_out, out_sems.at[5])])

        @at_step(0, 0)
        def _():
            wg_v[k] = w_ref[0].astype(BF16)
            wog_v[k] = wo_ref[0].astype(BF16)
            cwg_v[k] = jnp.zeros((8, HEAD), F32)
            cwg_v[k, 0:3, :] = cw_ref[0]
            w_direct[0].start()
            w_direct[1].start()
            for cp in cw_direct:
                cp.start()
            stores[0].start()

        @at_step(1, 0)
        def _():
            for j in range(2):
                copy(SEM_W + j, w_half(kjs[j], c), sibling).wait_recv()
                w_passed[j].start()
            w_direct[2].start()
            for cp in wo_direct:
                cp.start()
            copy(SEM_W_FWD, w_half(kjs[0], 1 - c), sibling).wait_recv()
            stores[1].start()

        @at_step(2, 0)
        def _():
            copy(SEM_W_FWD + 1, w_half(kjs[1], 1 - c), sibling).wait_recv()
            stores[2].start()

        @at_step(3, 0)
        def _():
            copy(SEM_W + 2, w_half(kjs[2], c), sibling).wait_recv()
            w_passed[2].start()
            copy(SEM_W_FWD + 2, w_half(kjs[2], 1 - c), sibling).wait_recv()
            stores[3].start()

        rows = pl.ds(pl.multiple_of(t * TP, TP), TP)

        @pl.when(s == 0)
        def _():
            xv = x_ref[...]
            r = lax.rsqrt(jnp.mean(xv * xv, axis=-1, keepdims=True) + EPS)
            h_ref[rows, :] = (xv * r * g_ref[...]).astype(BF16)

        js = k ^ (((s & 1) << 1) | (s >> 1))
        p_ref[...] = _dot(h_ref[rows, :], wg_v[js])

        @at_step(N_SHARD - 1, nt - 1)
        def _():
            for j in range(3):
                copy(SEM_WO + j, wo_half(kjs[j], c), sibling).wait_recv()
                wo_passed[j].start()
                copy(SEM_CW + j, cwg_v.at[kjs[j]], sibling).wait_recv()
            for j in range(3):
                copy(SEM_WO_FWD + j, wo_half(kjs[j], 1 - c), sibling).wait_recv()
            stores[4].start()
            stores[5].start()
            for cp in w_direct + wo_direct + cw_direct + w_passed + wo_passed:
                cp.wait_send()
            for st in stores:
                st.wait()

    def x_map(s, t, kr):
        return (jnp.where(s == 0, t, nt - 1), 0)

    def p_map(s, t, kr):
        return (t, kr[0] ^ (((s & 1) << 1) | (s >> 1)))

    hbm = pl.BlockSpec(memory_space=pl.ANY)
    grid_spec = pltpu.PrefetchScalarGridSpec(
        num_scalar_prefetch=1, grid=(N_SHARD, nt),
        in_specs=[pl.BlockSpec((TP, D_MODEL), x_map),
                  pl.BlockSpec((1, D_MODEL), lambda s, t, kr: (0, 0)),
                  pl.BlockSpec((1, D_MODEL, SHARD_COLS), lambda s, t, kr: (0, 0, 0)),
                  pl.BlockSpec((1, WO_ROWS, D_MODEL), lambda s, t, kr: (0, 0, 0)),
                  pl.BlockSpec((1, 3, HEAD), lambda s, t, kr: (0, 0, 0))],
        out_specs=(pl.BlockSpec((SEQ, D_MODEL), lambda s, t, kr: (0, 0)),
                   pl.BlockSpec((TP, SHARD_COLS), p_map), hbm, hbm, hbm),
        scratch_shapes=[pltpu.VMEM((N_SHARD, D_MODEL, SHARD_COLS), BF16),
                        pltpu.VMEM((N_SHARD, WO_ROWS, D_MODEL), BF16),
                        pltpu.VMEM((N_SHARD, 8, HEAD), F32),
                        pltpu.SemaphoreType.DMA((N_SEM,)), pltpu.SemaphoreType.DMA((N_SEM,)),
                        pltpu.SemaphoreType.DMA((6,))])
    return pl.pallas_call(
        body, name="gather_proj", grid_spec=grid_spec,
        out_shape=(jax.ShapeDtypeStruct((SEQ, D_MODEL), BF16),
                   jax.ShapeDtypeStruct((SEQ, N_SHARD * SHARD_COLS), F32),
                   jax.ShapeDtypeStruct((N_SHARD, D_MODEL, SHARD_COLS), BF16),
                   jax.ShapeDtypeStruct((N_SHARD, WO_ROWS, D_MODEL), BF16),
                   jax.ShapeDtypeStruct((N_SHARD, 8, HEAD), F32)),
        compiler_params=pltpu.CompilerParams(dimension_semantics=("arbitrary", "arbitrary"),
                                             vmem_limit_bytes=VMEM_LIMIT),
    )(kidx, x2d, g1, w_in, w_out, conv_w)


def _reduce_grads(gw, gwo, small):
    hw = D_MODEL // 2
    ho = WO_ROWS // 2

    def body(gw_ref, gwo_ref, sm_ref, ow_ref, oo_ref, osm_ref,
             sib_w, sib_o, p_w, p_o, rcv_w, rcv_o, sm_buf, send_sems, recv_sems):
        x, y, c = lax.axis_index("x"), lax.axis_index("y"), lax.axis_index("c")
        k = 2 * x + y
        me = 4 * x + 2 * y + c
        sibling = (x, y, 1 - c)
        chips = [(1 - x, y), (x, 1 - y), (1 - x, 1 - y)]

        def copy(sem, src, dst, to):
            return pltpu.make_async_remote_copy(
                src_ref=src, dst_ref=dst, send_sem=send_sems.at[sem], recv_sem=recv_sems.at[sem],
                device_id=to, device_id_type=MESH)

        to_sib_w = copy(0, gw_ref.at[:, pl.ds((1 - c) * hw, hw), :], sib_w, sibling)
        to_sib_o = copy(1, gwo_ref.at[:, pl.ds((1 - c) * ho, ho), :], sib_o, sibling)
        to_sib_w.start()
        to_sib_o.start()
        sm_buf[me] = sm_ref[...]
        small_sends = []
        for m in range(1, N_DEV):
            px, py, pc = x ^ (m >> 2), y ^ ((m >> 1) & 1), c ^ (m & 1)
            cp = copy(1 + m, sm_buf.at[me], sm_buf.at[me], (px, py, pc))
            cp.start()
            small_sends.append(cp)

        to_sib_w.wait_recv()
        to_sib_o.wait_recv()
        for j in range(N_SHARD):
            p_w[j] = (gw_ref[j, pl.ds(c * hw, hw), :].astype(F32) + sib_w[j].astype(F32)).astype(BF16)
            p_o[j] = (gwo_ref[j, pl.ds(c * ho, ho), :].astype(F32) + sib_o[j].astype(F32)).astype(BF16)
        sends = []
        for j, (cx, cy) in enumerate(chips):
            kj = 2 * cx + cy
            cw = copy(9 + j, p_w.at[kj], rcv_w.at[j], (cx, cy, c))
            co = copy(12 + j, p_o.at[kj], rcv_o.at[j], (cx, cy, c))
            cw.start()
            co.start()
            sends += [cw, co]

        acc_w = gw_ref[k, pl.ds(c * hw, hw), :].astype(F32) + sib_w[k].astype(F32)
        acc_o = gwo_ref[k, pl.ds(c * ho, ho), :].astype(F32) + sib_o[k].astype(F32)
        for j, (cx, cy) in enumerate(chips):
            copy(9 + j, p_w.at[0], rcv_w.at[j], (cx, cy, c)).wait_recv()
            acc_w = acc_w + rcv_w[j].astype(F32)
            copy(12 + j, p_o.at[0], rcv_o.at[j], (cx, cy, c)).wait_recv()
            acc_o = acc_o + rcv_o[j].astype(F32)
        ow_ref[pl.ds(c * hw, hw), :] = acc_w
        oo_ref[pl.ds(c * ho, ho), :] = acc_o
        fin_w = copy(15, ow_ref.at[pl.ds(c * hw, hw), :], ow_ref.at[pl.ds(c * hw, hw), :], sibling)
        fin_o = copy(16, oo_ref.at[pl.ds(c * ho, ho), :], oo_ref.at[pl.ds(c * ho, ho), :], sibling)
        fin_w.start()
        fin_o.start()

        for m in range(1, N_DEV):
            copy(1 + m, sm_buf.at[0], sm_buf.at[0], sibling).wait_recv()
        tot = sm_buf[0]
        for d in range(1, N_DEV):
            tot = tot + sm_buf[d]
        osm_ref[...] = tot

        copy(15, ow_ref.at[pl.ds((1 - c) * hw, hw), :], ow_ref.at[pl.ds((1 - c) * hw, hw), :], sibling).wait_recv()
        copy(16, oo_ref.at[pl.ds((1 - c) * ho, ho), :], oo_ref.at[pl.ds((1 - c) * ho, ho), :], sibling).wait_recv()
        for cp in [to_sib_w, to_sib_o] + small_sends + sends + [fin_w, fin_o]:
            cp.wait_send()

    vm = pl.BlockSpec(memory_space=pltpu.VMEM)
    return pl.pallas_call(
        body, name="reduce_grads",
        out_shape=(jax.ShapeDtypeStruct((D_MODEL, SHARD_COLS), F32),
                   jax.ShapeDtypeStruct((WO_ROWS, D_MODEL), F32),
                   jax.ShapeDtypeStruct((8, D_MODEL), F32)),
        in_specs=[vm, vm, vm], out_specs=(vm, vm, vm),
        scratch_shapes=[
            pltpu.VMEM((N_SHARD, hw, SHARD_COLS), BF16), pltpu.VMEM((N_SHARD, ho, D_MODEL), BF16),
            pltpu.VMEM((N_SHARD, hw, SHARD_COLS), BF16), pltpu.VMEM((N_SHARD, ho, D_MODEL), BF16),
            pltpu.VMEM((3, hw, SHARD_COLS), BF16), pltpu.VMEM((3, ho, D_MODEL), BF16),
            pltpu.VMEM((N_DEV, 8, D_MODEL), F32),
            pltpu.SemaphoreType.DMA((17,)), pltpu.SemaphoreType.DMA((17,)),
        ],
        compiler_params=pltpu.CompilerParams(vmem_limit_bytes=VMEM_LIMIT),
    )(gw, gwo, small)


def _mix_fwd(proj, lb_logits, cw, ga, gcn, g128, g64):
    def body(p_ref, lbl_ref, cw_ref, ga_ref, gcn_ref, g128_ref, g64_ref,
             mixed_ref, o_ref, cv_ref, sto_ref, st_ref, tail_ref, b_ref, f_ref):
        @pl.when(pl.program_id(0) == 0)
        def _():
            st_ref[...] = jnp.zeros_like(st_ref)
            tail_ref[...] = jnp.zeros_like(tail_ref)

        lb = _lower_bound(lbl_ref[...])
        f = lb + (1.0 - lb) * _sigmoid(p_ref[:, 512:1024])
        f_ref[...] = f
        b_ref[...] = _exact_left(_chunk_tri(True), jnp.log(f))
        causal = _causal()
        for n in range(NCB):
            sl = pl.ds(n * CHUNK, CHUNK)
            bc = b_ref[sl, :]
            g = b_ref[n * CHUNK + CHUNK - 1:n * CHUNK + CHUNK, :]
            kk = 1.0 - f_ref[sl, :]
            qd = (p_ref[sl, 0:512] * jnp.exp(bc)).astype(BF16)
            ki = (kk * jnp.exp(-bc)).astype(BF16)
            ke = (kk * jnp.exp(g - bc)).astype(BF16)
            vb = p_ref[sl, 1024:1536].astype(BF16)
            dec = jnp.exp(g)
            for hd in range(N_HEADS):
                cs = slice(hd * HEAD, (hd + 1) * HEAD)
                st = st_ref[hd]
                sto_ref[n, hd] = st
                sc = jnp.where(causal, _dot_nt(qd[:, cs], ki[:, cs]), 0.0)
                o_ref[sl, cs] = _dot(sc.astype(BF16), vb[:, cs]) + _dot_nt(qd[:, cs], st.astype(BF16))
                st_ref[hd] = st * dec[:, cs] + _dot_tn(vb[:, cs], ke[:, cs])

        o = o_ref[...]
        ra = lax.rsqrt(_group_mean(o * o, g128_ref[...]) + EPS)
        za = p_ref[:, 1536:2048]
        mixed_ref[:, 0:512] = (o * ra * ga_ref[...] * (za * _sigmoid(za))).astype(BF16)

        cu = p_ref[:, 3072:3584] * p_ref[:, 2048:2560]
        tail = tail_ref[...]
        cv = (cw_ref[0:1, :] * _shift_down(cu, 2, tail) + cw_ref[1:2, :] * _shift_down(cu, 1, tail)
              + cw_ref[2:3, :] * cu)
        tail_ref[...] = cu[TB - 8:, :]
        cv_ref[...] = cv
        yb = p_ref[:, 2560:3072] * cv
        rb = lax.rsqrt(_group_mean(yb * yb, g64_ref[...]) + EPS)
        zb = p_ref[:, 3584:4096]
        mixed_ref[:, 512:1024] = (yb * rb * gcn_ref[...] * (zb * _sigmoid(zb))).astype(BF16)

    row = lambda w: pl.BlockSpec((1, w), lambda i: (0, 0))
    return pl.pallas_call(
        body, name="mix_fwd", grid=(SEQ // TB,),
        out_shape=(jax.ShapeDtypeStruct((SEQ, D_MODEL), BF16),
                   jax.ShapeDtypeStruct((SEQ, D_HGRN), F32),
                   jax.ShapeDtypeStruct((SEQ, D_CONV), F32),
                   jax.ShapeDtypeStruct((N_CHUNKS, N_HEADS, HEAD, HEAD), F32)),
        in_specs=[pl.BlockSpec((TB, 4096), lambda i: (i, 0)),
                  pl.BlockSpec((2, D_HGRN), lambda i: (0, 0)),
                  pl.BlockSpec((8, D_CONV), lambda i: (0, 0)),
                  row(D_HGRN), row(D_CONV),
                  pl.BlockSpec((D_HGRN, D_HGRN), lambda i: (0, 0)),
                  pl.BlockSpec((D_CONV, D_CONV), lambda i: (0, 0))],
        out_specs=(pl.BlockSpec((TB, D_MODEL), lambda i: (i, 0)),
                   pl.BlockSpec((TB, D_HGRN), lambda i: (i, 0)),
                   pl.BlockSpec((TB, D_CONV), lambda i: (i, 0)),
                   pl.BlockSpec((NCB, N_HEADS, HEAD, HEAD), lambda i: (i, 0, 0, 0))),
        scratch_shapes=[pltpu.VMEM((N_HEADS, HEAD, HEAD), F32), pltpu.VMEM((8, D_CONV), F32),
                        pltpu.VMEM((TB, D_HGRN), F32), pltpu.VMEM((TB, D_HGRN), F32)],
        compiler_params=pltpu.CompilerParams(dimension_semantics=("arbitrary",), vmem_limit_bytes=VMEM_LIMIT),
    )(proj, lb_logits, cw, ga, gcn, g128, g64)


def _out_loss(x2d, mixed, wog, gf, tgt):
    def body(x_ref, m_ref, wo_ref, gf_ref, t_ref, dx2_ref, dm_ref, gwo_ref, part_ref, acc_ref):
        i = pl.program_id(0)

        @pl.when(i == 0)
        def _():
            acc_ref[...] = jnp.zeros_like(acc_ref)
            part_ref[...] = jnp.zeros_like(part_ref)

        mixed_b = m_ref[...]
        x2 = x_ref[...] + _dot(mixed_b, wo_ref[...])
        r2 = lax.rsqrt(jnp.mean(x2 * x2, axis=-1, keepdims=True) + EPS)
        n2 = x2 * r2
        gfv = gf_ref[...]
        err = n2 * gfv - t_ref[...]
        loss = 0.5 * jnp.sum(jnp.mean(err * err, axis=-1, keepdims=True), axis=0, keepdims=True)
        dy = err * (1.0 / D_MODEL)
        part_ref[0:1, :] += jnp.sum(dy * n2, axis=0, keepdims=True)
        part_ref[1:2, :] += jnp.broadcast_to(loss, (1, D_MODEL))
        dn = dy * gfv
        dx2 = r2 * (dn - n2 * jnp.mean(dn * n2, axis=-1, keepdims=True))
        dx2_ref[...] = dx2
        dx2_b = dx2.astype(BF16)
        dm_ref[...] = _dot_nt(dx2_b, wo_ref[...])
        acc_ref[...] += _dot_tn(mixed_b, dx2_b)

        @pl.when(i == pl.num_programs(0) - 1)
        def _():
            gwo_ref[...] = acc_ref[...].astype(BF16)

    blk = lambda: pl.BlockSpec((TB, D_MODEL), lambda i: (i, 0))
    return pl.pallas_call(
        body, name="out_loss", grid=(SEQ // TB,),
        out_shape=(jax.ShapeDtypeStruct((SEQ, D_MODEL), F32),
                   jax.ShapeDtypeStruct((SEQ, D_MODEL), F32),
                   jax.ShapeDtypeStruct((D_MODEL, D_MODEL), BF16),
                   jax.ShapeDtypeStruct((8, D_MODEL), F32)),
        in_specs=[blk(), blk(), pl.BlockSpec((D_MODEL, D_MODEL), lambda i: (0, 0)),
                  pl.BlockSpec((1, D_MODEL), lambda i: (0, 0)), blk()],
        out_specs=(blk(), blk(), pl.BlockSpec((D_MODEL, D_MODEL), lambda i: (0, 0)),
                   pl.BlockSpec((8, D_MODEL), lambda i: (0, 0))),
        scratch_shapes=[pltpu.VMEM((D_MODEL, D_MODEL), F32)],
        compiler_params=pltpu.CompilerParams(dimension_semantics=("arbitrary",), vmem_limit_bytes=VMEM_LIMIT),
    )(x2d, mixed, wog, gf, tgt)


def _mix_bwd(proj, o, cv, states, dmixed, lb_logits, cw, ga, gcn, g128, g64):
    nblk = SEQ // TB

    def body(p_ref, o_ref, cv_ref, st_ref, dm_ref, lbl_ref, cw_ref, ga_ref, gcn_ref, g128_ref, g64_ref,
             dp_ref, part_ref, dst_ref, head_ref, b_ref, f_ref, do_ref, db_ref, dg_ref, dk_ref, dlb_ref):
        i = pl.program_id(0)

        @pl.when(i == 0)
        def _():
            dst_ref[...] = jnp.zeros_like(dst_ref)
            head_ref[...] = jnp.zeros_like(head_ref)
            part_ref[...] = jnp.zeros_like(part_ref)
            dlb_ref[...] = jnp.zeros_like(dlb_ref)

        ov = o_ref[...]
        ra = lax.rsqrt(_group_mean(ov * ov, g128_ref[...]) + EPS)
        na = ov * ra
        za = p_ref[:, 1536:2048]
        sg = _sigmoid(za)
        dma = dm_ref[:, 0:512]
        gav = ga_ref[...]
        part_ref[3:4, :] += jnp.sum(dma * na * (za * sg), axis=0, keepdims=True)
        dp_ref[:, 1536:2048] = (dma * na * gav * (sg * (1.0 + za * (1.0 - sg)))).astype(BF16)
        dna = dma * gav * (za * sg)
        do_ref[...] = ra * (dna - na * _group_mean(dna * na, g128_ref[...]))

        cvv = cv_ref[...]
        gb = p_ref[:, 2560:3072]
        yb = gb * cvv
        rb = lax.rsqrt(_group_mean(yb * yb, g64_ref[...]) + EPS)
        nb = yb * rb
        zb = p_ref[:, 3584:4096]
        sgb = _sigmoid(zb)
        dmb = dm_ref[:, 512:1024]
        gcv = gcn_ref[...]
        part_ref[4:5, :] += jnp.sum(dmb * nb * (zb * sgb), axis=0, keepdims=True)
        dp_ref[:, 3584:4096] = (dmb * nb * gcv * (sgb * (1.0 + zb * (1.0 - sgb)))).astype(BF16)
        dnb = dmb * gcv * (zb * sgb)
        dyb = rb * (dnb - nb * _group_mean(dnb * nb, g64_ref[...]))
        dp_ref[:, 2560:3072] = (dyb * cvv).astype(BF16)
        dcv = dyb * gb
        head = head_ref[...]
        dcv1 = _shift_up(dcv, 1, head)
        dcv2 = _shift_up(dcv, 2, head)
        head_ref[...] = dcv[0:8, :]
        u = p_ref[:, 2048:2560]
        gc = p_ref[:, 3072:3584]
        cu = gc * u
        part_ref[0:1, :] += jnp.sum(dcv2 * cu, axis=0, keepdims=True)
        part_ref[1:2, :] += jnp.sum(dcv1 * cu, axis=0, keepdims=True)
        part_ref[2:3, :] += jnp.sum(dcv * cu, axis=0, keepdims=True)
        dcu = cw_ref[2:3, :] * dcv + cw_ref[1:2, :] * dcv1 + cw_ref[0:1, :] * dcv2
        dp_ref[:, 3072:3584] = (dcu * u).astype(BF16)
        dp_ref[:, 2048:2560] = (dcu * gc).astype(BF16)

        lb = _lower_bound(lbl_ref[...])
        s = _sigmoid(p_ref[:, 512:1024])
        f = lb + (1.0 - lb) * s
        f_ref[...] = f
        b_ref[...] = _exact_left(_chunk_tri(True), jnp.log(f))
        causal = _causal()
        for n in reversed(range(NCB)):
            sl = pl.ds(n * CHUNK, CHUNK)
            bc = b_ref[sl, :]
            g = b_ref[n * CHUNK + CHUNK - 1:n * CHUNK + CHUNK, :]
            kk = 1.0 - f_ref[sl, :]
            eb = jnp.exp(bc)
            enb = jnp.exp(-bc)
            eg = jnp.exp(g - bc)
            dec = jnp.exp(g)
            qd = p_ref[sl, 0:512] * eb
            ki = kk * enb
            ke = kk * eg
            qd_b = qd.astype(BF16)
            ki_b = ki.astype(BF16)
            ke_b = ke.astype(BF16)
            vb = p_ref[sl, 1024:1536].astype(BF16)
            do_b = do_ref[sl, :].astype(BF16)
            for hd in range(N_HEADS):
                cs = slice(hd * HEAD, (hd + 1) * HEAD)
                st = st_ref[n, hd]
                dst = dst_ref[hd]
                st_b = st.astype(BF16)
                dst_b = dst.astype(BF16)
                sc = jnp.where(causal, _dot_nt(qd_b[:, cs], ki_b[:, cs]), 0.0).astype(BF16)
                am = jnp.where(causal, _dot_nt(do_b[:, cs], vb[:, cs]), 0.0).astype(BF16)
                dqd = _dot(am, ki_b[:, cs]) + _dot(do_b[:, cs], st_b)
                dki = _dot_tn(am, qd_b[:, cs])
                dke = _dot(vb[:, cs], dst_b)
                dv = _dot_tn(sc, do_b[:, cs]) + _dot_nt(ke_b[:, cs], dst_b)
                ddec = jnp.sum(dst * st, axis=0, keepdims=True)
                dst_ref[hd] = dst * dec[:, cs] + _dot_tn(do_b[:, cs], qd_b[:, cs])
                dp_ref[sl, cs] = (dqd * eb[:, cs]).astype(BF16)
                dp_ref[sl, 1024 + hd * HEAD:1024 + (hd + 1) * HEAD] = dv.astype(BF16)
                dk_ref[sl, cs] = dki * enb[:, cs] + dke * eg[:, cs]
                db_ref[sl, cs] = dqd * qd[:, cs] - dki * ki[:, cs] - dke * ke[:, cs]
                dgv = jnp.sum(dke * ke[:, cs], axis=0, keepdims=True) + ddec * dec[:, cs]
                dg_ref[sl, cs] = jnp.broadcast_to(dgv, (CHUNK, HEAD))

        dlogf = _exact_left(_chunk_tri(False), db_ref[...]) + dg_ref[...]
        df = dlogf / f - dk_ref[...]
        dlb_ref[...] += jnp.sum(df * (1.0 - s), axis=0, keepdims=True)
        dp_ref[:, 512:1024] = (df * (1.0 - lb) * s * (1.0 - s)).astype(BF16)

        @pl.when(i == nblk - 1)
        def _():
            row = dlb_ref[...] * lb * (1.0 - lb)
            part_ref[5:6, :] = row
            part_ref[6:7, :] = -row

    rev = lambda w: pl.BlockSpec((TB, w), lambda i: (nblk - 1 - i, 0))
    row = lambda w: pl.BlockSpec((1, w), lambda i: (0, 0))
    return pl.pallas_call(
        body, name="mix_bwd", grid=(nblk,),
        out_shape=(jax.ShapeDtypeStruct((SEQ, 4096), BF16),
                   jax.ShapeDtypeStruct((8, D_HGRN), F32)),
        in_specs=[rev(4096), rev(D_HGRN), rev(D_CONV),
                  pl.BlockSpec((NCB, N_HEADS, HEAD, HEAD), lambda i: (nblk - 1 - i, 0, 0, 0)),
                  rev(D_MODEL),
                  pl.BlockSpec((2, D_HGRN), lambda i: (0, 0)),
                  pl.BlockSpec((8, D_CONV), lambda i: (0, 0)),
                  row(D_HGRN), row(D_CONV),
                  pl.BlockSpec((D_HGRN, D_HGRN), lambda i: (0, 0)),
                  pl.BlockSpec((D_CONV, D_CONV), lambda i: (0, 0))],
        out_specs=(rev(4096), pl.BlockSpec((8, D_HGRN), lambda i: (0, 0))),
        scratch_shapes=[pltpu.VMEM((N_HEADS, HEAD, HEAD), F32), pltpu.VMEM((8, D_CONV), F32),
                        pltpu.VMEM((TB, D_HGRN), F32), pltpu.VMEM((TB, D_HGRN), F32),
                        pltpu.VMEM((TB, D_HGRN), F32), pltpu.VMEM((TB, D_HGRN), F32),
                        pltpu.VMEM((TB, D_HGRN), F32), pltpu.VMEM((TB, D_HGRN), F32),
                        pltpu.VMEM((1, D_HGRN), F32)],
        compiler_params=pltpu.CompilerParams(dimension_semantics=("arbitrary",), vmem_limit_bytes=VMEM_LIMIT),
    )(proj, o, cv, states, dmixed, lb_logits, cw, ga, gcn, g128, g64)


def _grad_w_in(h, dproj):
    tk = 512
    nt = SEQ // tk

    def body(h_ref, dp_ref, out_ref, acc_ref):
        t = pl.program_id(1)

        @pl.when(t == 0)
        def _():
            acc_ref[...] = jnp.zeros_like(acc_ref)

        acc_ref[...] += _dot_tn(h_ref[...], dp_ref[...])

        @pl.when(t == nt - 1)
        def _():
            out_ref[0] = acc_ref[...].astype(BF16)

    return pl.pallas_call(
        body, name="grad_w_in", grid=(N_SHARD, nt),
        out_shape=jax.ShapeDtypeStruct((N_SHARD, D_MODEL, SHARD_COLS), BF16),
        in_specs=[pl.BlockSpec((tk, D_MODEL), lambda k, t: (t, 0)),
                  pl.BlockSpec((tk, SHARD_COLS), lambda k, t: (t, k))],
        out_specs=pl.BlockSpec((1, D_MODEL, SHARD_COLS), lambda k, t: (k, 0, 0)),
        scratch_shapes=[pltpu.VMEM((D_MODEL, SHARD_COLS), F32)],
        compiler_params=pltpu.CompilerParams(dimension_semantics=("arbitrary", "arbitrary"),
                                             vmem_limit_bytes=VMEM_LIMIT),
    )(h, dproj)


def _grad_x(dproj, wg, x2d, g1, dx2):
    def body(dp_ref, w_ref, x_ref, g_ref, dx2_ref, gx_ref, part_ref):
        @pl.when(pl.program_id(0) == 0)
        def _():
            part_ref[...] = jnp.zeros_like(part_ref)

        dh = None
        for k in range(N_SHARD):
            t = _dot_nt(dp_ref[:, k * SHARD_COLS:(k + 1) * SHARD_COLS], w_ref[k])
            dh = t if dh is None else dh + t
        xv = x_ref[...]
        r = lax.rsqrt(jnp.mean(xv * xv, axis=-1, keepdims=True) + EPS)
        xn = xv * r
        part_ref[0:1, :] += jnp.sum(dh * xn, axis=0, keepdims=True)
        dxn = dh * g_ref[...]
        gx_ref[...] = dx2_ref[...] + r * (dxn - xn * jnp.mean(dxn * xn, axis=-1, keepdims=True))

    blk = lambda: pl.BlockSpec((TB, D_MODEL), lambda i: (i, 0))
    return pl.pallas_call(
        body, name="grad_x", grid=(SEQ // TB,),
        out_shape=(jax.ShapeDtypeStruct((SEQ, D_MODEL), F32),
                   jax.ShapeDtypeStruct((8, D_MODEL), F32)),
        in_specs=[pl.BlockSpec((TB, 4096), lambda i: (i, 0)),
                  pl.BlockSpec((N_SHARD, D_MODEL, SHARD_COLS), lambda i: (0, 0, 0)),
                  blk(), pl.BlockSpec((1, D_MODEL), lambda i: (0, 0)), blk()],
        out_specs=(blk(), pl.BlockSpec((8, D_MODEL), lambda i: (0, 0))),
        compiler_params=pltpu.CompilerParams(dimension_semantics=("arbitrary",), vmem_limit_bytes=VMEM_LIMIT),
    )(dproj, wg, x2d, g1, dx2)


def _adamw(w, g, m, v, name):
    rows, cols = w.shape
    tr = rows if rows <= 256 else 256

    def body(w_ref, g_ref, m_ref, v_ref, d_ref, nm_ref, nv_ref):
        gv = g_ref[...]
        nm = ADAM_B1 * m_ref[...] + (1.0 - ADAM_B1) * gv
        nv = ADAM_B2 * v_ref[...] + (1.0 - ADAM_B2) * (gv * gv)
        m_hat = nm / (1.0 - ADAM_B1 ** ADAM_STEP)
        v_hat = nv / (1.0 - ADAM_B2 ** ADAM_STEP)
        d_ref[...] = -ADAM_LR * (m_hat / (jnp.sqrt(v_hat) + ADAM_EPS) + ADAM_WD * w_ref[...])
        nm_ref[...] = nm
        nv_ref[...] = nv

    blk = lambda: pl.BlockSpec((tr, cols), lambda i: (i, 0))
    shp = jax.ShapeDtypeStruct((rows, cols), F32)
    return pl.pallas_call(
        body, name=name, grid=(rows // tr,),
        out_shape=(shp, shp, shp),
        in_specs=[blk(), blk(), blk(), blk()], out_specs=(blk(), blk(), blk()),
        compiler_params=pltpu.CompilerParams(dimension_semantics=("arbitrary",)),
    )(w, g, m, v)


def _local_step(x2d, tgt, h, proj, g1, wg, lb_logits, cw, ga, gcn, wog, gf):
    g128 = _group_matrix(D_HGRN, HEAD)
    g64 = _group_matrix(D_CONV, CONV_GROUP)
    mixed, o, cv, states = _mix_fwd(proj, lb_logits, cw, ga, gcn, g128, g64)
    dx2, dmixed, gwo, part_out = _out_loss(x2d, mixed, wog, gf, tgt)
    dproj, part_mix = _mix_bwd(proj, o, cv, states, dmixed, lb_logits, cw, ga, gcn, g128, g64)
    gw = _grad_w_in(h, dproj)
    grad_x, part_x = _grad_x(dproj, wg, x2d, g1, dx2)
    zeros = jnp.zeros((3, D_MODEL - D_CONV), F32)
    small = jnp.concatenate([
        part_x[0:1], part_out[0:1],
        jnp.concatenate([part_mix[3:4], part_mix[4:5]], axis=1),
        jnp.concatenate([part_mix[5:6], part_mix[6:7]], axis=1),
        jnp.concatenate([part_mix[0:3], zeros], axis=1),
        part_out[1:2]], axis=0)
    return grad_x, gw, gwo, small


def kernel(x, norm_gain, w_in, lb_logits, conv_w, hgrn_norm_gain, conv_norm_gain, w_out, final_norm_gain, loss_target, m_norm_gain, m_w_in, m_lb_logits, m_conv_w, m_hgrn_norm_gain, m_conv_norm_gain, m_w_out, m_final_norm_gain, v_norm_gain, v_w_in, v_lb_logits, v_conv_w, v_hgrn_norm_gain, v_conv_norm_gain, v_w_out, v_final_norm_gain):
    k = 2 * lax.axis_index("x") + lax.axis_index("y")
    kidx = jnp.reshape(k, (1,)).astype(jnp.int32)
    h, proj, wg, wog4, cwg = _gather_proj(kidx, x[0], norm_gain, w_in, w_out, conv_w)
    wog = wog4.reshape(D_MODEL, D_MODEL)
    cw = jnp.transpose(cwg, (1, 0, 2)).reshape(8, D_CONV)

    grad_x, gw, gwo, small = _local_step(
        x[0], loss_target[0], h, proj, norm_gain, wg, lb_logits, cw, hgrn_norm_gain, conv_norm_gain, wog,
        final_norm_gain.reshape(1, D_MODEL))
    g_w_in, g_w_out, tot = _reduce_grads(gw, gwo.reshape(N_SHARD, WO_ROWS, D_MODEL), small)

    loss = tot[7, 0]
    g_norm_gain = tot[0:1]
    g_final = tot[1]
    g_hgrn = tot[2:3, 0:D_HGRN]
    g_convn = tot[2:3, D_HGRN:]
    g_lb = jnp.concatenate([tot[3:4, 0:D_HGRN], tot[3:4, D_HGRN:]], axis=0)
    g_conv_w = lax.dynamic_slice(tot[4:7, 0:D_CONV], (0, k * HEAD), (3, HEAD))

    d_w_in, nm_w_in, nv_w_in = _adamw(w_in[0], g_w_in, m_w_in[0], v_w_in[0], "adamw_w_in")
    d_w_out, nm_w_out, nv_w_out = _adamw(w_out[0], g_w_out, m_w_out[0], v_w_out[0], "adamw_w_out")
    d_ng, nm_ng, nv_ng = _adamw(norm_gain, g_norm_gain, m_norm_gain, v_norm_gain, "adamw_norm_gain")
    d_lb, nm_lb, nv_lb = _adamw(lb_logits, g_lb, m_lb_logits, v_lb_logits, "adamw_lb_logits")
    d_cw, nm_cw, nv_cw = _adamw(conv_w[0], g_conv_w, m_conv_w[0], v_conv_w[0], "adamw_conv_w")
    d_hg, nm_hg, nv_hg = _adamw(hgrn_norm_gain, g_hgrn, m_hgrn_norm_gain, v_hgrn_norm_gain, "adamw_hgrn_gain")
    d_cg, nm_cg, nv_cg = _adamw(conv_norm_gain, g_convn, m_conv_norm_gain, v_conv_norm_gain, "adamw_conv_gain")
    d_fg, nm_fg, nv_fg = _adamw(final_norm_gain.reshape(1, D_MODEL), g_final.reshape(1, D_MODEL),
                                m_final_norm_gain.reshape(1, D_MODEL), v_final_norm_gain.reshape(1, D_MODEL),
                                "adamw_final_gain")
    flat = lambda a: a.reshape(D_MODEL)
    return (loss, grad_x[None],
            g_norm_gain, g_w_in[None], g_lb, g_conv_w[None], g_hgrn, g_convn, g_w_out[None], g_final,
            d_ng, d_w_in[None], d_lb, d_cw[None], d_hg, d_cg, d_w_out[None], flat(d_fg),
            nm_ng, nm_w_in[None], nm_lb, nm_cw[None], nm_hg, nm_cg, nm_w_out[None], flat(nm_fg),
            nv_ng, nv_w_in[None], nv_lb, nv_cw[None], nv_hg, nv_cg, nv_w_out[None], flat(nv_fg))
```

```python
import functools

import jax
import jax.numpy as jnp
from jax import lax
from jax.experimental import pallas as pl
from jax.experimental.pallas import tpu as pltpu

F32 = jnp.float32
BF16 = jnp.bfloat16
MESH = pl.DeviceIdType.MESH

SEQ = 2048
D_MODEL = 1024
D_HGRN = 512
D_CONV = 512
HEAD = 128
N_HEADS = 4
CHUNK = 64
CONV_GROUP = 64
N_SHARD = 4
SHARD_COLS = 1024
WO_ROWS = 256
EPS = 1e-6
TB = 256
NCB = TB // CHUNK
N_CHUNKS = SEQ // CHUNK
N_DEV = 8

ADAM_LR = 0.001
ADAM_B1 = 0.9
ADAM_B2 = 0.999
ADAM_EPS = 1e-08
ADAM_WD = 0.01
ADAM_STEP = 10

VMEM_LIMIT = 56 * 1024 * 1024


def _dot(a, b):
    return jnp.dot(a, b, preferred_element_type=F32)


def _dot_nt(a, b):
    return lax.dot_general(a, b, (((1,), (1,)), ((), ())), preferred_element_type=F32)


def _dot_tn(a, b):
    return lax.dot_general(a, b, (((0,), (0,)), ((), ())), preferred_element_type=F32)


def _split_bf16(x, n):
    parts = []
    r = x
    for _ in range(n):
        p = r.astype(BF16)
        parts.append(p)
        r = r - p.astype(F32)
    return parts


def _exact_left(m, x, n=3):
    acc = None
    for p in _split_bf16(x, n):
        t = _dot(m, p)
        acc = t if acc is None else acc + t
    return acc


def _group_mean(x, gmat, n=2):
    acc = None
    for p in _split_bf16(x, n):
        t = _dot(p, gmat)
        acc = t if acc is None else acc + t
    return acc


def _sigmoid(x):
    return 1.0 / (1.0 + jnp.exp(-x))


def _lower_bound(lbl):
    l0 = lbl[0:1, :]
    l1 = lbl[1:2, :]
    m = jnp.maximum(l0, l1)
    e0 = jnp.exp(l0 - m)
    e1 = jnp.exp(l1 - m)
    return e0 / (e0 + e1)


def _chunk_tri(lower):
    r = lax.broadcasted_iota(jnp.int32, (TB, TB), 0)
    c = lax.broadcasted_iota(jnp.int32, (TB, TB), 1)
    same = (r // CHUNK) == (c // CHUNK)
    tri = (c <= r) if lower else (c >= r)
    return jnp.where(same & tri, 1.0, 0.0).astype(BF16)


def _causal():
    r = lax.broadcasted_iota(jnp.int32, (CHUNK, CHUNK), 0)
    c = lax.broadcasted_iota(jnp.int32, (CHUNK, CHUNK), 1)
    return c <= r


def _shift_down(x, sh, prev_tail):
    r = pltpu.roll(x, sh, 0)
    pt = pltpu.roll(prev_tail, sh, 0)
    rows = lax.broadcasted_iota(jnp.int32, prev_tail.shape, 0)
    top = jnp.where(rows < sh, pt, r[0:8])
    return jnp.concatenate([top, r[8:]], axis=0)


def _shift_up(x, sh, next_head):
    n = x.shape[0]
    r = pltpu.roll(x, n - sh, 0)
    nh = pltpu.roll(next_head, 8 - sh, 0)
    rows = lax.broadcasted_iota(jnp.int32, next_head.shape, 0)
    bot = jnp.where(rows >= 8 - sh, nh, r[n - 8:])
    return jnp.concatenate([r[:n - 8], bot], axis=0)


def _group_matrix(width, group):
    r = jnp.arange(width)[:, None] // group
    c = jnp.arange(width)[None, :] // group
    return jnp.where(r == c, 1.0 / group, 0.0).astype(BF16)


TP = 512
SEM_W, SEM_WO, SEM_CW, SEM_W_FWD, SEM_WO_FWD, N_SEM = 0, 3, 6, 9, 12, 15


def _gather_proj(kidx, x2d, g1, w_in, w_out, conv_w):
    half_w = D_MODEL // 2
    half_o = WO_ROWS // 2
    nt = SEQ // TP

    def body(k_ref, x_ref, g_ref, w_ref, wo_ref, cw_ref, h_ref, p_ref, wg_out, wog_out, cwg_out,
             wg_v, wog_v, cwg_v, send_sems, recv_sems, out_sems):
        s, t = pl.program_id(0), pl.program_id(1)
        x, y, c = lax.axis_index("x"), lax.axis_index("y"), lax.axis_index("c")
        k = 2 * x + y
        sibling = (x, y, 1 - c)
        chips = [(1 - x, y), (x, 1 - y), (1 - x, 1 - y)]
        kjs = [2 * cx + cy for cx, cy in chips]

        def w_half(kk, cc):
            return wg_v.at[kk, pl.ds(cc * half_w, half_w), :]

        def wo_half(kk, cc):
            return wog_v.at[kk, pl.ds(cc * half_o, half_o), :]

        def copy(sem, ref, to):
            return pltpu.make_async_remote_copy(
                src_ref=ref, dst_ref=ref, send_sem=send_sems.at[sem], recv_sem=recv_sems.at[sem],
                device_id=to, device_id_type=MESH)

        def at_step(sv, tv):
            return pl.when((s == sv) & (t == tv))

        w_direct = [copy(SEM_W + j, w_half(k, c), (*chip, c)) for j, chip in enumerate(chips)]
        wo_direct = [copy(SEM_WO + j, wo_half(k, c), (*chip, c)) for j, chip in enumerate(chips)]
        cw_direct = [copy(SEM_CW + j, cwg_v.at[k], (*chip, c)) for j, chip in enumerate(chips)]
        w_passed = [copy(SEM_W_FWD + j, w_half(kj, c), sibling) for j, kj in enumerate(kjs)]
        wo_passed = [copy(SEM_WO_FWD + j, wo_half(kj, c), sibling) for j, kj in enumerate(kjs)]
        stores = ([pltpu.make_async_copy(wg_v.at[kk], wg_out.at[kk], out_sems.at[i])
                   for i, kk in enumerate([k] + kjs)]
                  + [pltpu.make_async_copy(wog_v, wog_out, out_sems.at[4]),
                     pltpu.make_async_copy(cwg_v, cwg_out, out_sems.at[5])])

        @at_step(0, 0)
        def _():
            wg_v[k] = w_ref[0].astype(BF16)
            wog_v[k] = wo_ref[0].astype(BF16)
            cwg_v[k] = jnp.zeros((8, HEAD), F32)
            cwg_v[k, 0:3, :] = cw_ref[0]
            w_direct[0].start()
            w_direct[1].start()
            for cp in cw_direct:
                cp.start()
            stores[0].start()

        @at_step(1, 0)
        def _():
            for j in range(2):
                copy(SEM_W + j, w_half(kjs[j], c), sibling).wait_recv()
                w_passed[j].start()
            w_direct[2].start()
            for cp in wo_direct:
                cp.start()
            copy(SEM_W_FWD, w_half(kjs[0], 1 - c), sibling).wait_recv()
            stores[1].start()

        @at_step(2, 0)
        def _():
            copy(SEM_W_FWD + 1, w_half(kjs[1], 1 - c), sibling).wait_recv()
            stores[2].start()

        @at_step(3, 0)
        def _():
            copy(SEM_W + 2, w_half(kjs[2], c), sibling).wait_recv()
            w_passed[2].start()
            copy(SEM_W_FWD + 2, w_half(kjs[2], 1 - c), sibling).wait_recv()
            stores[3].start()

        rows = pl.ds(pl.multiple_of(t * TP, TP), TP)

        @pl.when(s == 0)
        def _():
            xv = x_ref[...]
            r = lax.rsqrt(jnp.mean(xv * xv, axis=-1, keepdims=True) + EPS)
            h_ref[rows, :] = (xv * r * g_ref[...]).astype(BF16)

        js = k ^ (((s & 1) << 1) | (s >> 1))
        p_ref[...] = _dot(h_ref[rows, :], wg_v[js])

        @at_step(N_SHARD - 1, nt - 1)
        def _():
            for j in range(3):
                copy(SEM_WO + j, wo_half(kjs[j], c), sibling).wait_recv()
                wo_passed[j].start()
                copy(SEM_CW + j, cwg_v.at[kjs[j]], sibling).wait_recv()
            for j in range(3):
                copy(SEM_WO_FWD + j, wo_half(kjs[j], 1 - c), sibling).wait_recv()
            stores[4].start()
            stores[5].start()
            for cp in w_direct + wo_direct + cw_direct + w_passed + wo_passed:
                cp.wait_send()
            for st in stores:
                st.wait()

    def x_map(s, t, kr):
        return (jnp.where(s == 0, t, nt - 1), 0)

    def p_map(s, t, kr):
        return (t, kr[0] ^ (((s & 1) << 1) | (s >> 1)))

    hbm = pl.BlockSpec(memory_space=pl.ANY)
    grid_spec = pltpu.PrefetchScalarGridSpec(
        num_scalar_prefetch=1, grid=(N_SHARD, nt),
        in_specs=[pl.BlockSpec((TP, D_MODEL), x_map),
                  pl.BlockSpec((1, D_MODEL), lambda s, t, kr: (0, 0)),
                  pl.BlockSpec((1, D_MODEL, SHARD_COLS), lambda s, t, kr: (0, 0, 0)),
                  pl.BlockSpec((1, WO_ROWS, D_MODEL), lambda s, t, kr: (0, 0, 0)),
                  pl.BlockSpec((1, 3, HEAD), lambda s, t, kr: (0, 0, 0))],
        out_specs=(pl.BlockSpec((SEQ, D_MODEL), lambda s, t, kr: (0, 0)),
                   pl.BlockSpec((TP, SHARD_COLS), p_map), hbm, hbm, hbm),
        scratch_shapes=[pltpu.VMEM((N_SHARD, D_MODEL, SHARD_COLS), BF16),
                        pltpu.VMEM((N_SHARD, WO_ROWS, D_MODEL), BF16),
                        pltpu.VMEM((N_SHARD, 8, HEAD), F32),
                        pltpu.SemaphoreType.DMA((N_SEM,)), pltpu.SemaphoreType.DMA((N_SEM,)),
                        pltpu.SemaphoreType.DMA((6,))])
    return pl.pallas_call(
        body, name="gather_proj", grid_spec=grid_spec,
        out_shape=(jax.ShapeDtypeStruct((SEQ, D_MODEL), BF16),
                   jax.ShapeDtypeStruct((SEQ, N_SHARD * SHARD_COLS), F32),
                   jax.ShapeDtypeStruct((N_SHARD, D_MODEL, SHARD_COLS), BF16),
                   jax.ShapeDtypeStruct((N_SHARD, WO_ROWS, D_MODEL), BF16),
                   jax.ShapeDtypeStruct((N_SHARD, 8, HEAD), F32)),
        compiler_params=pltpu.CompilerParams(dimension_semantics=("arbitrary", "arbitrary"),
                                             vmem_limit_bytes=VMEM_LIMIT),
    )(kidx, x2d, g1, w_in, w_out, conv_w)


def _mix_fwd(proj, lb_logits, cw, ga, gcn, g128, g64):
    def body(p_ref, lbl_ref, cw_ref, ga_ref, gcn_ref, g128_ref, g64_ref,
             mixed_ref, o_ref, cv_ref, sto_ref, st_ref, tail_ref, b_ref, f_ref):
        @pl.when(pl.program_id(0) == 0)
        def _():
            st_ref[...] = jnp.zeros_like(st_ref)
            tail_ref[...] = jnp.zeros_like(tail_ref)

        lb = _lower_bound(lbl_ref[...])
        f = lb + (1.0 - lb) * _sigmoid(p_ref[:, 512:1024])
        f_ref[...] = f
        b_ref[...] = _exact_left(_chunk_tri(True), jnp.log(f))
        causal = _causal()
        for n in range(NCB):
            sl = pl.ds(n * CHUNK, CHUNK)
            bc = b_ref[sl, :]
            g = b_ref[n * CHUNK + CHUNK - 1:n * CHUNK + CHUNK, :]
            kk = 1.0 - f_ref[sl, :]
            qd = (p_ref[sl, 0:512] * jnp.exp(bc)).astype(BF16)
            ki = (kk * jnp.exp(-bc)).astype(BF16)
            ke = (kk * jnp.exp(g - bc)).astype(BF16)
            vb = p_ref[sl, 1024:1536].astype(BF16)
            dec = jnp.exp(g)
            for hd in range(N_HEADS):
                cs = slice(hd * HEAD, (hd + 1) * HEAD)
                st = st_ref[hd]
                sto_ref[n, hd] = st
                sc = jnp.where(causal, _dot_nt(qd[:, cs], ki[:, cs]), 0.0)
                o_ref[sl, cs] = _dot(sc.astype(BF16), vb[:, cs]) + _dot_nt(qd[:, cs], st.astype(BF16))
                st_ref[hd] = st * dec[:, cs] + _dot_tn(vb[:, cs], ke[:, cs])

        o = o_ref[...]
        ra = lax.rsqrt(_group_mean(o * o, g128_ref[...]) + EPS)
        za = p_ref[:, 1536:2048]
        mixed_ref[:, 0:512] = (o * ra * ga_ref[...] * (za * _sigmoid(za))).astype(BF16)

        cu = p_ref[:, 3072:3584] * p_ref[:, 2048:2560]
        tail = tail_ref[...]
        cv = (cw_ref[0:1, :] * _shift_down(cu, 2, tail) + cw_ref[1:2, :] * _shift_down(cu, 1, tail)
              + cw_ref[2:3, :] * cu)
        tail_ref[...] = cu[TB - 8:, :]
        cv_ref[...] = cv
        yb = p_ref[:, 2560:3072] * cv
        rb = lax.rsqrt(_group_mean(yb * yb, g64_ref[...]) + EPS)
        zb = p_ref[:, 3584:4096]
        mixed_ref[:, 512:1024] = (yb * rb * gcn_ref[...] * (zb * _sigmoid(zb))).astype(BF16)

    row = lambda w: pl.BlockSpec((1, w), lambda i: (0, 0))
    return pl.pallas_call(
        body, name="mix_fwd", grid=(SEQ // TB,),
        out_shape=(jax.ShapeDtypeStruct((SEQ, D_MODEL), BF16),
                   jax.ShapeDtypeStruct((SEQ, D_HGRN), F32),
                   jax.ShapeDtypeStruct((SEQ, D_CONV), F32),
                   jax.ShapeDtypeStruct((N_CHUNKS, N_HEADS, HEAD, HEAD), F32)),
        in_specs=[pl.BlockSpec((TB, 4096), lambda i: (i, 0)),
                  pl.BlockSpec((2, D_HGRN), lambda i: (0, 0)),
                  pl.BlockSpec((8, D_CONV), lambda i: (0, 0)),
                  row(D_HGRN), row(D_CONV),
                  pl.BlockSpec((D_HGRN, D_HGRN), lambda i: (0, 0)),
                  pl.BlockSpec((D_CONV, D_CONV), lambda i: (0, 0))],
        out_specs=(pl.BlockSpec((TB, D_MODEL), lambda i: (i, 0)),
                   pl.BlockSpec((TB, D_HGRN), lambda i: (i, 0)),
                   pl.BlockSpec((TB, D_CONV), lambda i: (i, 0)),
                   pl.BlockSpec((NCB, N_HEADS, HEAD, HEAD), lambda i: (i, 0, 0, 0))),
        scratch_shapes=[pltpu.VMEM((N_HEADS, HEAD, HEAD), F32), pltpu.VMEM((8, D_CONV), F32),
                        pltpu.VMEM((TB, D_HGRN), F32), pltpu.VMEM((TB, D_HGRN), F32)],
        compiler_params=pltpu.CompilerParams(dimension_semantics=("arbitrary",), vmem_limit_bytes=VMEM_LIMIT),
    )(proj, lb_logits, cw, ga, gcn, g128, g64)


def _out_loss(x2d, mixed, wog, gf, tgt):
    def body(x_ref, m_ref, wo_ref, gf_ref, t_ref, dx2_ref, dm_ref, gwo_ref, part_ref, acc_ref):
        i = pl.program_id(0)

        @pl.when(i == 0)
        def _():
            acc_ref[...] = jnp.zeros_like(acc_ref)
            part_ref[...] = jnp.zeros_like(part_ref)

        mixed_b = m_ref[...]
        x2 = x_ref[...] + _dot(mixed_b, wo_ref[...])
        r2 = lax.rsqrt(jnp.mean(x2 * x2, axis=-1, keepdims=True) + EPS)
        n2 = x2 * r2
        gfv = gf_ref[...]
        err = n2 * gfv - t_ref[...]
        loss = 0.5 * jnp.sum(jnp.mean(err * err, axis=-1, keepdims=True), axis=0, keepdims=True)
        dy = err * (1.0 / D_MODEL)
        part_ref[0:1, :] += jnp.sum(dy * n2, axis=0, keepdims=True)
        part_ref[1:2, :] += jnp.broadcast_to(loss, (1, D_MODEL))
        dn = dy * gfv
        dx2 = r2 * (dn - n2 * jnp.mean(dn * n2, axis=-1, keepdims=True))
        dx2_ref[...] = dx2
        dx2_b = dx2.astype(BF16)
        dm_ref[...] = _dot_nt(dx2_b, wo_ref[...])
        acc_ref[...] += _dot_tn(mixed_b, dx2_b)

        @pl.when(i == pl.num_programs(0) - 1)
        def _():
            gwo_ref[...] = acc_ref[...].astype(BF16)

    blk = lambda: pl.BlockSpec((TB, D_MODEL), lambda i: (i, 0))
    return pl.pallas_call(
        body, name="out_loss", grid=(SEQ // TB,),
        out_shape=(jax.ShapeDtypeStruct((SEQ, D_MODEL), F32),
                   jax.ShapeDtypeStruct((SEQ, D_MODEL), F32),
                   jax.ShapeDtypeStruct((D_MODEL, D_MODEL), BF16),
                   jax.ShapeDtypeStruct((8, D_MODEL), F32)),
        in_specs=[blk(), blk(), pl.BlockSpec((D_MODEL, D_MODEL), lambda i: (0, 0)),
                  pl.BlockSpec((1, D_MODEL), lambda i: (0, 0)), blk()],
        out_specs=(blk(), blk(), pl.BlockSpec((D_MODEL, D_MODEL), lambda i: (0, 0)),
                   pl.BlockSpec((8, D_MODEL), lambda i: (0, 0))),
        scratch_shapes=[pltpu.VMEM((D_MODEL, D_MODEL), F32)],
        compiler_params=pltpu.CompilerParams(dimension_semantics=("arbitrary",), vmem_limit_bytes=VMEM_LIMIT),
    )(x2d, mixed, wog, gf, tgt)


def _mix_bwd(proj, o, cv, states, dmixed, lb_logits, cw, ga, gcn, g128, g64):
    nblk = SEQ // TB

    def body(p_ref, o_ref, cv_ref, st_ref, dm_ref, lbl_ref, cw_ref, ga_ref, gcn_ref, g128_ref, g64_ref,
             dp_ref, part_ref, dst_ref, head_ref, b_ref, f_ref, do_ref, db_ref, dg_ref, dk_ref, dlb_ref):
        i = pl.program_id(0)

        @pl.when(i == 0)
        def _():
            dst_ref[...] = jnp.zeros_like(dst_ref)
            head_ref[...] = jnp.zeros_like(head_ref)
            part_ref[...] = jnp.zeros_like(part_ref)
            dlb_ref[...] = jnp.zeros_like(dlb_ref)

        ov = o_ref[...]
        ra = lax.rsqrt(_group_mean(ov * ov, g128_ref[...]) + EPS)
        na = ov * ra
        za = p_ref[:, 1536:2048]
        sg = _sigmoid(za)
        dma = dm_ref[:, 0:512]
        gav = ga_ref[...]
        part_ref[3:4, :] += jnp.sum(dma * na * (za * sg), axis=0, keepdims=True)
        dp_ref[:, 1536:2048] = (dma * na * gav * (sg * (1.0 + za * (1.0 - sg)))).astype(BF16)
        dna = dma * gav * (za * sg)
        do_ref[...] = ra * (dna - na * _group_mean(dna * na, g128_ref[...]))

        cvv = cv_ref[...]
        gb = p_ref[:, 2560:3072]
        yb = gb * cvv
        rb = lax.rsqrt(_group_mean(yb * yb, g64_ref[...]) + EPS)
        nb = yb * rb
        zb = p_ref[:, 3584:4096]
        sgb = _sigmoid(zb)
        dmb = dm_ref[:, 512:1024]
        gcv = gcn_ref[...]
        part_ref[4:5, :] += jnp.sum(dmb * nb * (zb * sgb), axis=0, keepdims=True)
        dp_ref[:, 3584:4096] = (dmb * nb * gcv * (sgb * (1.0 + zb * (1.0 - sgb)))).astype(BF16)
        dnb = dmb * gcv * (zb * sgb)
        dyb = rb * (dnb - nb * _group_mean(dnb * nb, g64_ref[...]))
        dp_ref[:, 2560:3072] = (dyb * cvv).astype(BF16)
        dcv = dyb * gb
        head = head_ref[...]
        dcv1 = _shift_up(dcv, 1, head)
        dcv2 = _shift_up(dcv, 2, head)
        head_ref[...] = dcv[0:8, :]
        u = p_ref[:, 2048:2560]
        gc = p_ref[:, 3072:3584]
        cu = gc * u
        part_ref[0:1, :] += jnp.sum(dcv2 * cu, axis=0, keepdims=True)
        part_ref[1:2, :] += jnp.sum(dcv1 * cu, axis=0, keepdims=True)
        part_ref[2:3, :] += jnp.sum(dcv * cu, axis=0, keepdims=True)
        dcu = cw_ref[2:3, :] * dcv + cw_ref[1:2, :] * dcv1 + cw_ref[0:1, :] * dcv2
        dp_ref[:, 3072:3584] = (dcu * u).astype(BF16)
        dp_ref[:, 2048:2560] = (dcu * gc).astype(BF16)

        lb = _lower_bound(lbl_ref[...])
        s = _sigmoid(p_ref[:, 512:1024])
        f = lb + (1.0 - lb) * s
        f_ref[...] = f
        b_ref[...] = _exact_left(_chunk_tri(True), jnp.log(f))
        causal = _causal()
        for n in reversed(range(NCB)):
            sl = pl.ds(n * CHUNK, CHUNK)
            bc = b_ref[sl, :]
            g = b_ref[n * CHUNK + CHUNK - 1:n * CHUNK + CHUNK, :]
            kk = 1.0 - f_ref[sl, :]
            eb = jnp.exp(bc)
            enb = jnp.exp(-bc)
            eg = jnp.exp(g - bc)
            dec = jnp.exp(g)
            qd = p_ref[sl, 0:512] * eb
            ki = kk * enb
            ke = kk * eg
            qd_b = qd.astype(BF16)
            ki_b = ki.astype(BF16)
            ke_b = ke.astype(BF16)
            vb = p_ref[sl, 1024:1536].astype(BF16)
            do_b = do_ref[sl, :].astype(BF16)
            for hd in range(N_HEADS):
                cs = slice(hd * HEAD, (hd + 1) * HEAD)
                st = st_ref[n, hd]
                dst = dst_ref[hd]
                st_b = st.astype(BF16)
                dst_b = dst.astype(BF16)
                sc = jnp.where(causal, _dot_nt(qd_b[:, cs], ki_b[:, cs]), 0.0).astype(BF16)
                am = jnp.where(causal, _dot_nt(do_b[:, cs], vb[:, cs]), 0.0).astype(BF16)
                dqd = _dot(am, ki_b[:, cs]) + _dot(do_b[:, cs], st_b)
                dki = _dot_tn(am, qd_b[:, cs])
                dke = _dot(vb[:, cs], dst_b)
                dv = _dot_tn(sc, do_b[:, cs]) + _dot_nt(ke_b[:, cs], dst_b)
                ddec = jnp.sum(dst * st, axis=0, keepdims=True)
                dst_ref[hd] = dst * dec[:, cs] + _dot_tn(do_b[:, cs], qd_b[:, cs])
                dp_ref[sl, cs] = (dqd * eb[:, cs]).astype(BF16)
                dp_ref[sl, 1024 + hd * HEAD:1024 + (hd + 1) * HEAD] = dv.astype(BF16)
                dk_ref[sl, cs] = dki * enb[:, cs] + dke * eg[:, cs]
                db_ref[sl, cs] = dqd * qd[:, cs] - dki * ki[:, cs] - dke * ke[:, cs]
                dgv = jnp.sum(dke * ke[:, cs], axis=0, keepdims=True) + ddec * dec[:, cs]
                dg_ref[sl, cs] = jnp.broadcast_to(dgv, (CHUNK, HEAD))

        dlogf = _exact_left(_chunk_tri(False), db_ref[...]) + dg_ref[...]
        df = dlogf / f - dk_ref[...]
        dlb_ref[...] += jnp.sum(df * (1.0 - s), axis=0, keepdims=True)
        dp_ref[:, 512:1024] = (df * (1.0 - lb) * s * (1.0 - s)).astype(BF16)

        @pl.when(i == nblk - 1)
        def _():
            row = dlb_ref[...] * lb * (1.0 - lb)
            part_ref[5:6, :] = row
            part_ref[6:7, :] = -row

    rev = lambda w: pl.BlockSpec((TB, w), lambda i: (nblk - 1 - i, 0))
    row = lambda w: pl.BlockSpec((1, w), lambda i: (0, 0))
    return pl.pallas_call(
        body, name="mix_bwd", grid=(nblk,),
        out_shape=(jax.ShapeDtypeStruct((SEQ, 4096), BF16),
                   jax.ShapeDtypeStruct((8, D_HGRN), F32)),
        in_specs=[rev(4096), rev(D_HGRN), rev(D_CONV),
                  pl.BlockSpec((NCB, N_HEADS, HEAD, HEAD), lambda i: (nblk - 1 - i, 0, 0, 0)),
                  rev(D_MODEL),
                  pl.BlockSpec((2, D_HGRN), lambda i: (0, 0)),
                  pl.BlockSpec((8, D_CONV), lambda i: (0, 0)),
                  row(D_HGRN), row(D_CONV),
                  pl.BlockSpec((D_HGRN, D_HGRN), lambda i: (0, 0)),
                  pl.BlockSpec((D_CONV, D_CONV), lambda i: (0, 0))],
        out_specs=(rev(4096), pl.BlockSpec((8, D_HGRN), lambda i: (0, 0))),
        scratch_shapes=[pltpu.VMEM((N_HEADS, HEAD, HEAD), F32), pltpu.VMEM((8, D_CONV), F32),
                        pltpu.VMEM((TB, D_HGRN), F32), pltpu.VMEM((TB, D_HGRN), F32),
                        pltpu.VMEM((TB, D_HGRN), F32), pltpu.VMEM((TB, D_HGRN), F32),
                        pltpu.VMEM((TB, D_HGRN), F32), pltpu.VMEM((TB, D_HGRN), F32),
                        pltpu.VMEM((1, D_HGRN), F32)],
        compiler_params=pltpu.CompilerParams(dimension_semantics=("arbitrary",), vmem_limit_bytes=VMEM_LIMIT),
    )(proj, o, cv, states, dmixed, lb_logits, cw, ga, gcn, g128, g64)


TX = 256
(SEM_D2D, SEM_D2D_O, SEM_ICI, SEM_ICI_O, SEM_FIN, SEM_FIN_O, SEM_SMALL, N_SEM_TAIL) = 0, 4, 5, 8, 11, 12, 12, 20


def _bwd_tail(kidx, h, dproj, wg, gwo, x2d, dx2, g1, small):
    hw = D_MODEL // 2
    ho = WO_ROWS // 2
    nt = SEQ // TP
    n_steps = N_SHARD + SEQ // TX // nt

    def body(k_ref, h_ref, dp_ref, w_ref, gwo_ref, x_ref, dx2_ref, g_ref, sm_ref,
             gx_ref, gw_out, gwo_out, osm_ref,
             acc, dh, sendbuf, keep, sibrcv, rcv, sib_o, p_o, rcv_o, res_o, sm_buf, dng,
             send_sems, recv_sems, out_sems):
        s, t = pl.program_id(0), pl.program_id(1)
        x, y, c = lax.axis_index("x"), lax.axis_index("y"), lax.axis_index("c")
        k = 2 * x + y
        me = 4 * x + 2 * y + c
        sibling = (x, y, 1 - c)
        chips = [(1 - x, 1 - y), (1 - x, y), (x, 1 - y)]
        kjs = [2 * cx + cy for cx, cy in chips]
        mine = pl.ds(pl.multiple_of(c * hw, hw), hw)
        other = pl.ds(pl.multiple_of((1 - c) * hw, hw), hw)
        mine_o = pl.ds(pl.multiple_of(c * ho, ho), ho)
        other_o = pl.ds(pl.multiple_of((1 - c) * ho, ho), ho)

        def copy(sem, src, dst, to):
            return pltpu.make_async_remote_copy(
                src_ref=src, dst_ref=dst, send_sem=send_sems.at[sem], recv_sem=recv_sems.at[sem],
                device_id=to, device_id_type=MESH)

        def at_step(sv, tv):
            return pl.when((s == sv) & (t == tv))

        d2d = [copy(SEM_D2D + sv, sendbuf.at[sv], sibrcv.at[sv], sibling) for sv in range(N_SHARD)]
        d2d_o = copy(SEM_D2D_O, gwo_ref.at[:, other_o, :], sib_o, sibling)
        ici = [copy(SEM_ICI + sv, keep.at[sv], rcv.at[sv], (*chips[sv], c)) for sv in range(3)]
        ici_o = [copy(SEM_ICI_O + sv, p_o.at[kjs[sv]], rcv_o.at[sv], (*chips[sv], c)) for sv in range(3)]
        fin = copy(SEM_FIN, acc.at[mine, :], acc.at[mine, :], sibling)
        fin_o = copy(SEM_FIN_O, res_o.at[mine_o, :], res_o.at[mine_o, :], sibling)
        smalls = [copy(SEM_SMALL + m, sm_buf.at[me], sm_buf.at[me],
                       (x ^ (m >> 2), y ^ ((m >> 1) & 1), c ^ (m & 1))) for m in range(1, N_DEV)]
        store_w = pltpu.make_async_copy(acc, gw_out, out_sems.at[0])
        store_o = pltpu.make_async_copy(res_o, gwo_out, out_sems.at[1])

        @at_step(0, 0)
        def _():
            d2d_o.start()

        @at_step(0, 1)
        def _():
            d2d_o.wait_recv()
            for j in range(N_SHARD):
                p_o[j] = (gwo_ref[j, mine_o, :].astype(F32) + sib_o[j].astype(F32)).astype(BF16)
            res_o[mine_o, :] = gwo_ref[k, mine_o, :].astype(F32) + sib_o[k].astype(F32)
            for cp in ici_o:
                cp.start()

        rows = pl.ds(pl.multiple_of(t * TP, TP), TP)

        @pl.when(s < N_SHARD)
        def _():
            dpb = dp_ref[...]
            part = _dot_tn(h_ref[...], dpb)

            @pl.when(t == 0)
            def _():
                acc[...] = part

            @pl.when(t > 0)
            def _():
                acc[...] += part

            d = _dot_nt(dpb, w_ref[0])

            @pl.when(s == 0)
            def _():
                dh[rows, :] = d

            @pl.when(s > 0)
            def _():
                dh[rows, :] += d

        for sv in range(N_SHARD):
            @at_step(sv, nt - 1)
            def _(sv=sv):
                sendbuf[sv] = acc[other, :].astype(BF16)
                if sv < 3:
                    keep[sv] = acc[mine, :].astype(BF16)
                d2d[sv].start()

        for sv in range(3):
            @at_step(sv + 1, 1)
            def _(sv=sv):
                d2d[sv].wait_recv()
                keep[sv] = (keep[sv].astype(F32) + sibrcv[sv].astype(F32)).astype(BF16)
                ici[sv].start()

        @at_step(N_SHARD, 0)
        def _():
            d2d[3].wait_recv()
            ici[0].wait_recv()
            acc[mine, :] += sibrcv[3].astype(F32) + rcv[0].astype(F32)

        @at_step(N_SHARD, 1)
        def _():
            tot = res_o[mine_o, :]
            for sv in range(3):
                ici_o[sv].wait_recv()
                tot = tot + rcv_o[sv].astype(F32)
            res_o[mine_o, :] = tot
            fin_o.start()

        @at_step(N_SHARD, 2)
        def _():
            ici[1].wait_recv()
            acc[mine, :] += rcv[1].astype(F32)

        @at_step(N_SHARD, 0)
        def _():
            dng[...] = jnp.zeros_like(dng)

        @pl.when(s >= N_SHARD)
        def _():
            blk = (s - N_SHARD) * nt + t
            dhv = dh[pl.ds(pl.multiple_of(blk * TX, TX), TX), :]
            xv = x_ref[...]
            r = lax.rsqrt(jnp.mean(xv * xv, axis=-1, keepdims=True) + EPS)
            xn = xv * r
            dng[...] += jnp.sum(dhv * xn, axis=0, keepdims=True)
            dxn = dhv * g_ref[...]
            gx_ref[...] = dx2_ref[...] + r * (dxn - xn * jnp.mean(dxn * xn, axis=-1, keepdims=True))

        @at_step(n_steps - 1, nt - 1)
        def _():
            sm_buf[me] = sm_ref[...]
            sm_buf[me, 0:1, :] = dng[...]
            for cp in smalls:
                cp.start()
            ici[2].wait_recv()
            acc[mine, :] += rcv[2].astype(F32)
            fin.start()
            for m in range(1, N_DEV):
                copy(SEM_SMALL + m, sm_buf.at[0], sm_buf.at[0], sibling).wait_recv()
            tot = sm_buf[0]
            for d in range(1, N_DEV):
                tot = tot + sm_buf[d]
            osm_ref[...] = tot
            fin_o.wait_recv()
            store_o.start()
            fin.wait_recv()
            store_w.start()
            for cp in d2d + [d2d_o] + ici + ici_o + [fin, fin_o] + smalls:
                cp.wait_send()
            store_o.wait()
            store_w.wait()

    def shard_of(s, kr):
        return kr[0] ^ (3 - jnp.minimum(s, 3))

    def tok(s, t):
        return jnp.where(s < N_SHARD, t, nt - 1)

    def blk_map(s, t, kr):
        return (jnp.where(s < N_SHARD, 0, (s - N_SHARD) * nt + t), 0)

    hbm = pl.BlockSpec(memory_space=pl.ANY)
    grid_spec = pltpu.PrefetchScalarGridSpec(
        num_scalar_prefetch=1, grid=(n_steps, nt),
        in_specs=[pl.BlockSpec((TP, D_MODEL), lambda s, t, kr: (tok(s, t), 0)),
                  pl.BlockSpec((TP, SHARD_COLS), lambda s, t, kr: (tok(s, t), shard_of(s, kr))),
                  pl.BlockSpec((1, D_MODEL, SHARD_COLS), lambda s, t, kr: (shard_of(s, kr), 0, 0)),
                  pl.BlockSpec((N_SHARD, WO_ROWS, D_MODEL), lambda s, t, kr: (0, 0, 0)),
                  pl.BlockSpec((TX, D_MODEL), blk_map),
                  pl.BlockSpec((TX, D_MODEL), blk_map),
                  pl.BlockSpec((1, D_MODEL), lambda s, t, kr: (0, 0)),
                  pl.BlockSpec((8, D_MODEL), lambda s, t, kr: (0, 0))],
        out_specs=(pl.BlockSpec((TX, D_MODEL), blk_map), hbm, hbm,
                   pl.BlockSpec((8, D_MODEL), lambda s, t, kr: (0, 0))),
        scratch_shapes=[pltpu.VMEM((D_MODEL, SHARD_COLS), F32), pltpu.VMEM((SEQ, D_MODEL), F32),
                        pltpu.VMEM((N_SHARD, hw, SHARD_COLS), BF16), pltpu.VMEM((3, hw, SHARD_COLS), BF16),
                        pltpu.VMEM((N_SHARD, hw, SHARD_COLS), BF16), pltpu.VMEM((3, hw, SHARD_COLS), BF16),
                        pltpu.VMEM((N_SHARD, ho, D_MODEL), BF16), pltpu.VMEM((N_SHARD, ho, D_MODEL), BF16),
                        pltpu.VMEM((3, ho, D_MODEL), BF16), pltpu.VMEM((WO_ROWS, D_MODEL), F32),
                        pltpu.VMEM((N_DEV, 8, D_MODEL), F32), pltpu.VMEM((1, D_MODEL), F32),
                        pltpu.SemaphoreType.DMA((N_SEM_TAIL,)), pltpu.SemaphoreType.DMA((N_SEM_TAIL,)),
                        pltpu.SemaphoreType.DMA((2,))])
    return pl.pallas_call(
        body, name="bwd_tail", grid_spec=grid_spec,
        out_shape=(jax.ShapeDtypeStruct((SEQ, D_MODEL), F32),
                   jax.ShapeDtypeStruct((D_MODEL, SHARD_COLS), F32),
                   jax.ShapeDtypeStruct((WO_ROWS, D_MODEL), F32),
                   jax.ShapeDtypeStruct((8, D_MODEL), F32)),
        compiler_params=pltpu.CompilerParams(dimension_semantics=("arbitrary", "arbitrary"),
                                             vmem_limit_bytes=60 * 1024 * 1024),
    )(kidx, h, dproj, wg, gwo, x2d, dx2, g1, small)


def _adamw(w, g, m, v, name):
    rows, cols = w.shape
    tr = rows if rows <= 256 else 256

    def body(w_ref, g_ref, m_ref, v_ref, d_ref, nm_ref, nv_ref):
        gv = g_ref[...]
        nm = ADAM_B1 * m_ref[...] + (1.0 - ADAM_B1) * gv
        nv = ADAM_B2 * v_ref[...] + (1.0 - ADAM_B2) * (gv * gv)
        m_hat = nm / (1.0 - ADAM_B1 ** ADAM_STEP)
        v_hat = nv / (1.0 - ADAM_B2 ** ADAM_STEP)
        d_ref[...] = -ADAM_LR * (m_hat / (jnp.sqrt(v_hat) + ADAM_EPS) + ADAM_WD * w_ref[...])
        nm_ref[...] = nm
        nv_ref[...] = nv

    blk = lambda: pl.BlockSpec((tr, cols), lambda i: (i, 0))
    shp = jax.ShapeDtypeStruct((rows, cols), F32)
    return pl.pallas_call(
        body, name=name, grid=(rows // tr,),
        out_shape=(shp, shp, shp),
        in_specs=[blk(), blk(), blk(), blk()], out_specs=(blk(), blk(), blk()),
        compiler_params=pltpu.CompilerParams(dimension_semantics=("arbitrary",)),
    )(w, g, m, v)


def _local_step(x2d, tgt, proj, lb_logits, cw, ga, gcn, wog, gf):
    g128 = _group_matrix(D_HGRN, HEAD)
    g64 = _group_matrix(D_CONV, CONV_GROUP)
    mixed, o, cv, states = _mix_fwd(proj, lb_logits, cw, ga, gcn, g128, g64)
    dx2, dmixed, gwo, part_out = _out_loss(x2d, mixed, wog, gf, tgt)
    dproj, part_mix = _mix_bwd(proj, o, cv, states, dmixed, lb_logits, cw, ga, gcn, g128, g64)
    zeros = jnp.zeros((3, D_MODEL - D_CONV), F32)
    small = jnp.concatenate([
        jnp.zeros((1, D_MODEL), F32), part_out[0:1],
        jnp.concatenate([part_mix[3:4], part_mix[4:5]], axis=1),
        jnp.concatenate([part_mix[5:6], part_mix[6:7]], axis=1),
        jnp.concatenate([part_mix[0:3], zeros], axis=1),
        part_out[1:2]], axis=0)
    return dproj, dx2, gwo, small


def kernel(x, norm_gain, w_in, lb_logits, conv_w, hgrn_norm_gain, conv_norm_gain, w_out, final_norm_gain, loss_target, m_norm_gain, m_w_in, m_lb_logits, m_conv_w, m_hgrn_norm_gain, m_conv_norm_gain, m_w_out, m_final_norm_gain, v_norm_gain, v_w_in, v_lb_logits, v_conv_w, v_hgrn_norm_gain, v_conv_norm_gain, v_w_out, v_final_norm_gain):
    k = 2 * lax.axis_index("x") + lax.axis_index("y")
    kidx = jnp.reshape(k, (1,)).astype(jnp.int32)
    h, proj, wg, wog4, cwg = _gather_proj(kidx, x[0], norm_gain, w_in, w_out, conv_w)
    wog = wog4.reshape(D_MODEL, D_MODEL)
    cw = jnp.transpose(cwg, (1, 0, 2)).reshape(8, D_CONV)

    dproj, dx2, gwo, small = _local_step(
        x[0], loss_target[0], proj, lb_logits, cw, hgrn_norm_gain, conv_norm_gain, wog,
        final_norm_gain.reshape(1, D_MODEL))
    grad_x, g_w_in, g_w_out, tot = _bwd_tail(
        kidx, h, dproj, wg, gwo.reshape(N_SHARD, WO_ROWS, D_MODEL), x[0], dx2, norm_gain, small)

    loss = tot[7, 0]
    g_norm_gain = tot[0:1]
    g_final = tot[1]
    g_hgrn = tot[2:3, 0:D_HGRN]
    g_convn = tot[2:3, D_HGRN:]
    g_lb = jnp.concatenate([tot[3:4, 0:D_HGRN], tot[3:4, D_HGRN:]], axis=0)
    g_conv_w = lax.dynamic_slice(tot[4:7, 0:D_CONV], (0, k * HEAD), (3, HEAD))

    d_w_in, nm_w_in, nv_w_in = _adamw(w_in[0], g_w_in, m_w_in[0], v_w_in[0], "adamw_w_in")
    d_w_out, nm_w_out, nv_w_out = _adamw(w_out[0], g_w_out, m_w_out[0], v_w_out[0], "adamw_w_out")
    d_ng, nm_ng, nv_ng = _adamw(norm_gain, g_norm_gain, m_norm_gain, v_norm_gain, "adamw_norm_gain")
    d_lb, nm_lb, nv_lb = _adamw(lb_logits, g_lb, m_lb_logits, v_lb_logits, "adamw_lb_logits")
    d_cw, nm_cw, nv_cw = _adamw(conv_w[0], g_conv_w, m_conv_w[0], v_conv_w[0], "adamw_conv_w")
    d_hg, nm_hg, nv_hg = _adamw(hgrn_norm_gain, g_hgrn, m_hgrn_norm_gain, v_hgrn_norm_gain, "adamw_hgrn_gain")
    d_cg, nm_cg, nv_cg = _adamw(conv_norm_gain, g_convn, m_conv_norm_gain, v_conv_norm_gain, "adamw_conv_gain")
    d_fg, nm_fg, nv_fg = _adamw(final_norm_gain.reshape(1, D_MODEL), g_final.reshape(1, D_MODEL),
                                m_final_norm_gain.reshape(1, D_MODEL), v_final_norm_gain.reshape(1, D_MODEL),
                                "adamw_final_gain")
    flat = lambda a: a.reshape(D_MODEL)
    return (loss, grad_x[None],
            g_norm_gain, g_w_in[None], g_lb, g_conv_w[None], g_hgrn, g_convn, g_w_out[None], g_final,
            d_ng, d_w_in[None], d_lb, d_cw[None], d_hg, d_cg, d_w_out[None], flat(d_fg),
            nm_ng, nm_w_in[None], nm_lb, nm_cw[None], nm_hg, nm_cg, nm_w_out[None], flat(nm_fg),
            nv_ng, nv_w_in[None], nv_lb, nv_cw[None], nv_hg, nv_cg, nv_w_out[None], flat(nv_fg))
```

```python
import functools

import jax
import jax.numpy as jnp
from jax import lax
from jax.experimental import pallas as pl
from jax.experimental.pallas import tpu as pltpu

F32 = jnp.float32
BF16 = jnp.bfloat16
MESH = pl.DeviceIdType.MESH

SEQ = 2048
D_MODEL = 1024
D_HGRN = 512
D_CONV = 512
HEAD = 128
N_HEADS = 4
CHUNK = 64
CONV_GROUP = 64
N_SHARD = 4
SHARD_COLS = 1024
WO_ROWS = 256
EPS = 1e-6
TB = 256
NCB = TB // CHUNK
N_CHUNKS = SEQ // CHUNK
N_DEV = 8
MXU_WIDTH = 256

ADAM_LR = 0.001
ADAM_B1 = 0.9
ADAM_B2 = 0.999
ADAM_EPS = 1e-08
ADAM_WD = 0.01
ADAM_STEP = 10

VMEM_LIMIT = 56 * 1024 * 1024


def _dot(a, b):
    return jnp.dot(a, b, preferred_element_type=F32)


def _dot_nt(a, b):
    return lax.dot_general(a, b, (((1,), (1,)), ((), ())), preferred_element_type=F32)


def _dot_tn(a, b):
    return lax.dot_general(a, b, (((0,), (0,)), ((), ())), preferred_element_type=F32)


def _split_bf16(x, n):
    parts = []
    r = x
    for _ in range(n):
        p = r.astype(BF16)
        parts.append(p)
        r = r - p.astype(F32)
    return parts


def _exact_left(m, x, n=3):
    acc = None
    for p in _split_bf16(x, n):
        t = _dot(m, p)
        acc = t if acc is None else acc + t
    return acc


def _group_mean(x, gmat, n=2):
    w = gmat.shape[0]
    outs = []
    for c0 in range(0, x.shape[1], w):
        acc = None
        for p in _split_bf16(x[:, c0:c0 + w], n):
            t = _dot(p, gmat)
            acc = t if acc is None else acc + t
        outs.append(acc)
    return jnp.concatenate(outs, axis=1)


def _sigmoid(x):
    return 1.0 / (1.0 + jnp.exp(-x))


def _lower_bound(lbl):
    l0 = lbl[0:1, :]
    l1 = lbl[1:2, :]
    m = jnp.maximum(l0, l1)
    e0 = jnp.exp(l0 - m)
    e1 = jnp.exp(l1 - m)
    return e0 / (e0 + e1)


def _chunk_tri(lower):
    r = lax.broadcasted_iota(jnp.int32, (TB, TB), 0)
    c = lax.broadcasted_iota(jnp.int32, (TB, TB), 1)
    same = (r // CHUNK) == (c // CHUNK)
    tri = (c <= r) if lower else (c >= r)
    return jnp.where(same & tri, 1.0, 0.0).astype(BF16)


def _causal():
    r = lax.broadcasted_iota(jnp.int32, (CHUNK, CHUNK), 0)
    c = lax.broadcasted_iota(jnp.int32, (CHUNK, CHUNK), 1)
    return c <= r


def _shift_down(x, sh, prev_tail):
    r = pltpu.roll(x, sh, 0)
    pt = pltpu.roll(prev_tail, sh, 0)
    rows = lax.broadcasted_iota(jnp.int32, prev_tail.shape, 0)
    top = jnp.where(rows < sh, pt, r[0:8])
    return jnp.concatenate([top, r[8:]], axis=0)


def _shift_up(x, sh, next_head):
    n = x.shape[0]
    r = pltpu.roll(x, n - sh, 0)
    nh = pltpu.roll(next_head, 8 - sh, 0)
    rows = lax.broadcasted_iota(jnp.int32, next_head.shape, 0)
    bot = jnp.where(rows >= 8 - sh, nh, r[n - 8:])
    return jnp.concatenate([r[:n - 8], bot], axis=0)


def _group_matrix(width, group):
    r = jnp.arange(width)[:, None] // group
    c = jnp.arange(width)[None, :] // group
    return jnp.where(r == c, 1.0 / group, 0.0).astype(BF16)


TP = 512
SEM_W, SEM_WO, SEM_CW, SEM_W_FWD, SEM_WO_FWD, N_SEM = 0, 3, 6, 9, 12, 15


def _gather_proj(kidx, x2d, g1, w_in, w_out, conv_w):
    half_w = D_MODEL // 2
    half_o = WO_ROWS // 2
    nt = SEQ // TP

    def body(k_ref, x_ref, g_ref, w_ref, wo_ref, cw_ref, h_ref, p_ref, wg_out, wog_out, cwg_out,
             wg_v, wog_v, cwg_v, send_sems, recv_sems, out_sems):
        s, t = pl.program_id(0), pl.program_id(1)
        x, y, c = lax.axis_index("x"), lax.axis_index("y"), lax.axis_index("c")
        k = 2 * x + y
        sibling = (x, y, 1 - c)
        chips = [(1 - x, y), (x, 1 - y), (1 - x, 1 - y)]
        kjs = [2 * cx + cy for cx, cy in chips]

        def w_half(kk, cc):
            return wg_v.at[kk, pl.ds(cc * half_w, half_w), :]

        def wo_half(kk, cc):
            return wog_v.at[kk, pl.ds(cc * half_o, half_o), :]

        def copy(sem, ref, to):
            return pltpu.make_async_remote_copy(
                src_ref=ref, dst_ref=ref, send_sem=send_sems.at[sem], recv_sem=recv_sems.at[sem],
                device_id=to, device_id_type=MESH)

        def at_step(sv, tv):
            return pl.when((s == sv) & (t == tv))

        w_direct = [copy(SEM_W + j, w_half(k, c), (*chip, c)) for j, chip in enumerate(chips)]
        wo_direct = [copy(SEM_WO + j, wo_half(k, c), (*chip, c)) for j, chip in enumerate(chips)]
        cw_direct = [copy(SEM_CW + j, cwg_v.at[k], (*chip, c)) for j, chip in enumerate(chips)]
        w_passed = [copy(SEM_W_FWD + j, w_half(kj, c), sibling) for j, kj in enumerate(kjs)]
        wo_passed = [copy(SEM_WO_FWD + j, wo_half(kj, c), sibling) for j, kj in enumerate(kjs)]
        stores = ([pltpu.make_async_copy(wg_v.at[kk], wg_out.at[kk], out_sems.at[i])
                   for i, kk in enumerate([k] + kjs)]
                  + [pltpu.make_async_copy(wog_v, wog_out, out_sems.at[4]),
                     pltpu.make_async_copy(cwg_v, cwg_out, out_sems.at[5])])

        @at_step(0, 0)
        def _():
            wg_v[k] = w_ref[0].astype(BF16)
            wog_v[k] = wo_ref[0].astype(BF16)
            cwg_v[k] = jnp.zeros((8, HEAD), F32)
            cwg_v[k, 0:3, :] = cw_ref[0]
            w_direct[0].start()
            w_direct[1].start()
            for cp in cw_direct:
                cp.start()
            stores[0].start()

        @at_step(1, 0)
        def _():
            for j in range(2):
                copy(SEM_W + j, w_half(kjs[j], c), sibling).wait_recv()
                w_passed[j].start()
            w_direct[2].start()
            copy(SEM_W_FWD, w_half(kjs[0], 1 - c), sibling).wait_recv()
            stores[1].start()

        @at_step(2, 0)
        def _():
            copy(SEM_W_FWD + 1, w_half(kjs[1], 1 - c), sibling).wait_recv()
            stores[2].start()

        @at_step(3, 0)
        def _():
            copy(SEM_W + 2, w_half(kjs[2], c), sibling).wait_recv()
            w_passed[2].start()
            for cp in wo_direct:
                cp.start()
            copy(SEM_W_FWD + 2, w_half(kjs[2], 1 - c), sibling).wait_recv()
            stores[3].start()

        rows = pl.ds(pl.multiple_of(t * TP, TP), TP)

        @pl.when(s == 0)
        def _():
            xv = x_ref[...]
            r = lax.rsqrt(jnp.mean(xv * xv, axis=-1, keepdims=True) + EPS)
            h_ref[rows, :] = (xv * r * g_ref[...]).astype(BF16)

        js = k ^ (((s & 1) << 1) | (s >> 1))
        p_ref[...] = _dot(h_ref[rows, :], wg_v[js])

        @at_step(N_SHARD - 1, nt - 1)
        def _():
            for j in range(3):
                copy(SEM_WO + j, wo_half(kjs[j], c), sibling).wait_recv()
                wo_passed[j].start()
                copy(SEM_CW + j, cwg_v.at[kjs[j]], sibling).wait_recv()
            for j in range(3):
                copy(SEM_WO_FWD + j, wo_half(kjs[j], 1 - c), sibling).wait_recv()
            stores[4].start()
            stores[5].start()
            for cp in w_direct + wo_direct + cw_direct + w_passed + wo_passed:
                cp.wait_send()
            for st in stores:
                st.wait()

    def x_map(s, t, kr):
        return (jnp.where(s == 0, t, nt - 1), 0)

    def p_map(s, t, kr):
        return (t, kr[0] ^ (((s & 1) << 1) | (s >> 1)))

    hbm = pl.BlockSpec(memory_space=pl.ANY)
    grid_spec = pltpu.PrefetchScalarGridSpec(
        num_scalar_prefetch=1, grid=(N_SHARD, nt),
        in_specs=[pl.BlockSpec((TP, D_MODEL), x_map),
                  pl.BlockSpec((1, D_MODEL), lambda s, t, kr: (0, 0)),
                  pl.BlockSpec((1, D_MODEL, SHARD_COLS), lambda s, t, kr: (0, 0, 0)),
                  pl.BlockSpec((1, WO_ROWS, D_MODEL), lambda s, t, kr: (0, 0, 0)),
                  pl.BlockSpec((1, 3, HEAD), lambda s, t, kr: (0, 0, 0))],
        out_specs=(pl.BlockSpec((SEQ, D_MODEL), lambda s, t, kr: (0, 0)),
                   pl.BlockSpec((TP, SHARD_COLS), p_map), hbm, hbm, hbm),
        scratch_shapes=[pltpu.VMEM((N_SHARD, D_MODEL, SHARD_COLS), BF16),
                        pltpu.VMEM((N_SHARD, WO_ROWS, D_MODEL), BF16),
                        pltpu.VMEM((N_SHARD, 8, HEAD), F32),
                        pltpu.SemaphoreType.DMA((N_SEM,)), pltpu.SemaphoreType.DMA((N_SEM,)),
                        pltpu.SemaphoreType.DMA((6,))])
    return pl.pallas_call(
        body, name="gather_proj", grid_spec=grid_spec,
        out_shape=(jax.ShapeDtypeStruct((SEQ, D_MODEL), BF16),
                   jax.ShapeDtypeStruct((SEQ, N_SHARD * SHARD_COLS), F32),
                   jax.ShapeDtypeStruct((N_SHARD, D_MODEL, SHARD_COLS), BF16),
                   jax.ShapeDtypeStruct((N_SHARD, WO_ROWS, D_MODEL), BF16),
                   jax.ShapeDtypeStruct((N_SHARD, 8, HEAD), F32)),
        compiler_params=pltpu.CompilerParams(dimension_semantics=("arbitrary", "arbitrary"),
                                             vmem_limit_bytes=VMEM_LIMIT),
    )(kidx, x2d, g1, w_in, w_out, conv_w)


def _mix_fwd(proj, lb_logits, cw, ga, gcn, g128, g64):
    def body(p_ref, lbl_ref, cw_ref, ga_ref, gcn_ref, g128_ref, g64_ref,
             mixed_ref, o_ref, cv_ref, sto_ref, sg_ref, b_ref, st_ref, tail_ref, f_ref):
        @pl.when(pl.program_id(0) == 0)
        def _():
            st_ref[...] = jnp.zeros_like(st_ref)
            tail_ref[...] = jnp.zeros_like(tail_ref)

        lb = _lower_bound(lbl_ref[...])
        sg = _sigmoid(p_ref[:, 512:1024])
        sg_ref[...] = sg
        f = lb + (1.0 - lb) * sg
        f_ref[...] = f
        b_ref[...] = _exact_left(_chunk_tri(True), jnp.log(f))
        causal = _causal()
        for n in range(NCB):
            sl = pl.ds(n * CHUNK, CHUNK)
            bc = b_ref[sl, :]
            g = b_ref[n * CHUNK + CHUNK - 1:n * CHUNK + CHUNK, :]
            kk = 1.0 - f_ref[sl, :]
            qd = (p_ref[sl, 0:512] * jnp.exp(bc)).astype(BF16)
            ki = (kk * jnp.exp(-bc)).astype(BF16)
            ke = (kk * jnp.exp(g - bc)).astype(BF16)
            vb = p_ref[sl, 1024:1536].astype(BF16)
            dec = jnp.exp(g)
            for hd in range(N_HEADS):
                cs = slice(hd * HEAD, (hd + 1) * HEAD)
                st = st_ref[hd]
                sto_ref[n, hd] = st
                sc = jnp.where(causal, _dot_nt(qd[:, cs], ki[:, cs]), 0.0)
                o_ref[sl, cs] = _dot(sc.astype(BF16), vb[:, cs]) + _dot_nt(qd[:, cs], st.astype(BF16))
                st_ref[hd] = st * dec[:, cs] + _dot_tn(vb[:, cs], ke[:, cs])

        o = o_ref[...]
        ra = lax.rsqrt(_group_mean(o * o, g128_ref[...]) + EPS)
        za = p_ref[:, 1536:2048]
        mixed_ref[:, 0:512] = (o * ra * ga_ref[...] * (za * _sigmoid(za))).astype(BF16)

        cu = p_ref[:, 3072:3584] * p_ref[:, 2048:2560]
        tail = tail_ref[...]
        cv = (cw_ref[0:1, :] * _shift_down(cu, 2, tail) + cw_ref[1:2, :] * _shift_down(cu, 1, tail)
              + cw_ref[2:3, :] * cu)
        tail_ref[...] = cu[TB - 8:, :]
        cv_ref[...] = cv
        yb = p_ref[:, 2560:3072] * cv
        rb = lax.rsqrt(_group_mean(yb * yb, g64_ref[...]) + EPS)
        zb = p_ref[:, 3584:4096]
        mixed_ref[:, 512:1024] = (yb * rb * gcn_ref[...] * (zb * _sigmoid(zb))).astype(BF16)

    row = lambda w: pl.BlockSpec((1, w), lambda i: (0, 0))
    return pl.pallas_call(
        body, name="mix_fwd", grid=(SEQ // TB,),
        out_shape=(jax.ShapeDtypeStruct((SEQ, D_MODEL), BF16),
                   jax.ShapeDtypeStruct((SEQ, D_HGRN), F32),
                   jax.ShapeDtypeStruct((SEQ, D_CONV), F32),
                   jax.ShapeDtypeStruct((N_CHUNKS, N_HEADS, HEAD, HEAD), F32),
                   jax.ShapeDtypeStruct((SEQ, D_HGRN), F32),
                   jax.ShapeDtypeStruct((SEQ, D_HGRN), F32)),
        in_specs=[pl.BlockSpec((TB, 4096), lambda i: (i, 0)),
                  pl.BlockSpec((2, D_HGRN), lambda i: (0, 0)),
                  pl.BlockSpec((8, D_CONV), lambda i: (0, 0)),
                  row(D_HGRN), row(D_CONV),
                  pl.BlockSpec((MXU_WIDTH, MXU_WIDTH), lambda i: (0, 0)),
                  pl.BlockSpec((MXU_WIDTH, MXU_WIDTH), lambda i: (0, 0))],
        out_specs=(pl.BlockSpec((TB, D_MODEL), lambda i: (i, 0)),
                   pl.BlockSpec((TB, D_HGRN), lambda i: (i, 0)),
                   pl.BlockSpec((TB, D_CONV), lambda i: (i, 0)),
                   pl.BlockSpec((NCB, N_HEADS, HEAD, HEAD), lambda i: (i, 0, 0, 0)),
                   pl.BlockSpec((TB, D_HGRN), lambda i: (i, 0)),
                   pl.BlockSpec((TB, D_HGRN), lambda i: (i, 0))),
        scratch_shapes=[pltpu.VMEM((N_HEADS, HEAD, HEAD), F32), pltpu.VMEM((8, D_CONV), F32),
                        pltpu.VMEM((TB, D_HGRN), F32)],
        compiler_params=pltpu.CompilerParams(dimension_semantics=("arbitrary",), vmem_limit_bytes=VMEM_LIMIT),
    )(proj, lb_logits, cw, ga, gcn, g128, g64)


def _out_loss(x2d, mixed, wog, gf, tgt):
    def body(x_ref, m_ref, wo_ref, gf_ref, t_ref, dx2_ref, dm_ref, gwo_ref, part_ref, acc_ref):
        i = pl.program_id(0)

        @pl.when(i == 0)
        def _():
            acc_ref[...] = jnp.zeros_like(acc_ref)
            part_ref[...] = jnp.zeros_like(part_ref)

        mixed_b = m_ref[...]
        x2 = x_ref[...] + _dot(mixed_b, wo_ref[...])
        r2 = lax.rsqrt(jnp.mean(x2 * x2, axis=-1, keepdims=True) + EPS)
        n2 = x2 * r2
        gfv = gf_ref[...]
        err = n2 * gfv - t_ref[...]
        loss = 0.5 * jnp.sum(jnp.mean(err * err, axis=-1, keepdims=True), axis=0, keepdims=True)
        dy = err * (1.0 / D_MODEL)
        part_ref[0:1, :] += jnp.sum(dy * n2, axis=0, keepdims=True)
        part_ref[1:2, :] += jnp.broadcast_to(loss, (1, D_MODEL))
        dn = dy * gfv
        dx2 = r2 * (dn - n2 * jnp.mean(dn * n2, axis=-1, keepdims=True))
        dx2_ref[...] = dx2
        dx2_b = dx2.astype(BF16)
        dm_ref[...] = _dot_nt(dx2_b, wo_ref[...])
        acc_ref[...] += _dot_tn(mixed_b, dx2_b)

        @pl.when(i == pl.num_programs(0) - 1)
        def _():
            gwo_ref[...] = acc_ref[...].astype(BF16)

    blk = lambda: pl.BlockSpec((TB, D_MODEL), lambda i: (i, 0))
    return pl.pallas_call(
        body, name="out_loss", grid=(SEQ // TB,),
        out_shape=(jax.ShapeDtypeStruct((SEQ, D_MODEL), F32),
                   jax.ShapeDtypeStruct((SEQ, D_MODEL), F32),
                   jax.ShapeDtypeStruct((D_MODEL, D_MODEL), BF16),
                   jax.ShapeDtypeStruct((8, D_MODEL), F32)),
        in_specs=[blk(), blk(), pl.BlockSpec((D_MODEL, D_MODEL), lambda i: (0, 0)),
                  pl.BlockSpec((1, D_MODEL), lambda i: (0, 0)), blk()],
        out_specs=(blk(), blk(), pl.BlockSpec((D_MODEL, D_MODEL), lambda i: (0, 0)),
                   pl.BlockSpec((8, D_MODEL), lambda i: (0, 0))),
        scratch_shapes=[pltpu.VMEM((D_MODEL, D_MODEL), F32)],
        compiler_params=pltpu.CompilerParams(dimension_semantics=("arbitrary",), vmem_limit_bytes=VMEM_LIMIT),
    )(x2d, mixed, wog, gf, tgt)


def _mix_bwd(proj, o, cv, states, sg, b, dmixed, lb_logits, cw, ga, gcn, g128, g64):
    nblk = SEQ // TB

    def body(p_ref, o_ref, cv_ref, st_ref, sg_ref, b_ref, dm_ref, lbl_ref, cw_ref, ga_ref, gcn_ref,
             g128_ref, g64_ref,
             dp_ref, part_ref, dst_ref, head_ref, f_ref, do_ref, db_ref, dg_ref, dk_ref, dlb_ref):
        i = pl.program_id(0)

        @pl.when(i == 0)
        def _():
            dst_ref[...] = jnp.zeros_like(dst_ref)
            head_ref[...] = jnp.zeros_like(head_ref)
            part_ref[...] = jnp.zeros_like(part_ref)
            dlb_ref[...] = jnp.zeros_like(dlb_ref)

        ov = o_ref[...]
        ra = lax.rsqrt(_group_mean(ov * ov, g128_ref[...]) + EPS)
        na = ov * ra
        za = p_ref[:, 1536:2048]
        sg = _sigmoid(za)
        dma = dm_ref[:, 0:512]
        gav = ga_ref[...]
        part_ref[3:4, :] += jnp.sum(dma * na * (za * sg), axis=0, keepdims=True)
        dp_ref[:, 1536:2048] = (dma * na * gav * (sg * (1.0 + za * (1.0 - sg)))).astype(BF16)
        dna = dma * gav * (za * sg)
        do_ref[...] = ra * (dna - na * _group_mean(dna * na, g128_ref[...]))

        cvv = cv_ref[...]
        gb = p_ref[:, 2560:3072]
        yb = gb * cvv
        rb = lax.rsqrt(_group_mean(yb * yb, g64_ref[...]) + EPS)
        nb = yb * rb
        zb = p_ref[:, 3584:4096]
        sgb = _sigmoid(zb)
        dmb = dm_ref[:, 512:1024]
        gcv = gcn_ref[...]
        part_ref[4:5, :] += jnp.sum(dmb * nb * (zb * sgb), axis=0, keepdims=True)
        dp_ref[:, 3584:4096] = (dmb * nb * gcv * (sgb * (1.0 + zb * (1.0 - sgb)))).astype(BF16)
        dnb = dmb * gcv * (zb * sgb)
        dyb = rb * (dnb - nb * _group_mean(dnb * nb, g64_ref[...]))
        dp_ref[:, 2560:3072] = (dyb * cvv).astype(BF16)
        dcv = dyb * gb
        head = head_ref[...]
        dcv1 = _shift_up(dcv, 1, head)
        dcv2 = _shift_up(dcv, 2, head)
        head_ref[...] = dcv[0:8, :]
        u = p_ref[:, 2048:2560]
        gc = p_ref[:, 3072:3584]
        cu = gc * u
        part_ref[0:1, :] += jnp.sum(dcv2 * cu, axis=0, keepdims=True)
        part_ref[1:2, :] += jnp.sum(dcv1 * cu, axis=0, keepdims=True)
        part_ref[2:3, :] += jnp.sum(dcv * cu, axis=0, keepdims=True)
        dcu = cw_ref[2:3, :] * dcv + cw_ref[1:2, :] * dcv1 + cw_ref[0:1, :] * dcv2
        dp_ref[:, 3072:3584] = (dcu * u).astype(BF16)
        dp_ref[:, 2048:2560] = (dcu * gc).astype(BF16)

        lb = _lower_bound(lbl_ref[...])
        s = sg_ref[...]
        f = lb + (1.0 - lb) * s
        f_ref[...] = f
        causal = _causal()
        for n in reversed(range(NCB)):
            sl = pl.ds(n * CHUNK, CHUNK)
            bc = b_ref[sl, :]
            g = b_ref[n * CHUNK + CHUNK - 1:n * CHUNK + CHUNK, :]
            kk = 1.0 - f_ref[sl, :]
            eb = jnp.exp(bc)
            enb = jnp.exp(-bc)
            eg = jnp.exp(g - bc)
            dec = jnp.exp(g)
            qd = p_ref[sl, 0:512] * eb
            ki = kk * enb
            ke = kk * eg
            qd_b = qd.astype(BF16)
            ki_b = ki.astype(BF16)
            ke_b = ke.astype(BF16)
            vb = p_ref[sl, 1024:1536].astype(BF16)
            do_b = do_ref[sl, :].astype(BF16)
            for hd in range(N_HEADS):
                cs = slice(hd * HEAD, (hd + 1) * HEAD)
                st = st_ref[n, hd]
                dst = dst_ref[hd]
                st_b = st.astype(BF16)
                dst_b = dst.astype(BF16)
                sc = jnp.where(causal, _dot_nt(qd_b[:, cs], ki_b[:, cs]), 0.0).astype(BF16)
                am = jnp.where(causal, _dot_nt(do_b[:, cs], vb[:, cs]), 0.0).astype(BF16)
                dqd = _dot(am, ki_b[:, cs]) + _dot(do_b[:, cs], st_b)
                dki = _dot_tn(am, qd_b[:, cs])
                dke = _dot(vb[:, cs], dst_b)
                dv = _dot_tn(sc, do_b[:, cs]) + _dot_nt(ke_b[:, cs], dst_b)
                ddec = jnp.sum(dst * st, axis=0, keepdims=True)
                dst_ref[hd] = dst * dec[:, cs] + _dot_tn(do_b[:, cs], qd_b[:, cs])
                dp_ref[sl, cs] = (dqd * eb[:, cs]).astype(BF16)
                dp_ref[sl, 1024 + hd * HEAD:1024 + (hd + 1) * HEAD] = dv.astype(BF16)
                dk_ref[sl, cs] = dki * enb[:, cs] + dke * eg[:, cs]
                db_ref[sl, cs] = dqd * qd[:, cs] - dki * ki[:, cs] - dke * ke[:, cs]
                dgv = jnp.sum(dke * ke[:, cs], axis=0, keepdims=True) + ddec * dec[:, cs]
                dg_ref[sl, cs] = jnp.broadcast_to(dgv, (CHUNK, HEAD))

        dlogf = _exact_left(_chunk_tri(False), db_ref[...], 2) + dg_ref[...]
        df = dlogf / f - dk_ref[...]
        dlb_ref[...] += jnp.sum(df * (1.0 - s), axis=0, keepdims=True)
        dp_ref[:, 512:1024] = (df * (1.0 - lb) * s * (1.0 - s)).astype(BF16)

        @pl.when(i == nblk - 1)
        def _():
            row = dlb_ref[...] * lb * (1.0 - lb)
            part_ref[5:6, :] = row
            part_ref[6:7, :] = -row

    rev = lambda w: pl.BlockSpec((TB, w), lambda i: (nblk - 1 - i, 0))
    row = lambda w: pl.BlockSpec((1, w), lambda i: (0, 0))
    return pl.pallas_call(
        body, name="mix_bwd", grid=(nblk,),
        out_shape=(jax.ShapeDtypeStruct((SEQ, 4096), BF16),
                   jax.ShapeDtypeStruct((8, D_HGRN), F32)),
        in_specs=[rev(4096), rev(D_HGRN), rev(D_CONV),
                  pl.BlockSpec((NCB, N_HEADS, HEAD, HEAD), lambda i: (nblk - 1 - i, 0, 0, 0)),
                  rev(D_HGRN), rev(D_HGRN), rev(D_MODEL),
                  pl.BlockSpec((2, D_HGRN), lambda i: (0, 0)),
                  pl.BlockSpec((8, D_CONV), lambda i: (0, 0)),
                  row(D_HGRN), row(D_CONV),
                  pl.BlockSpec((MXU_WIDTH, MXU_WIDTH), lambda i: (0, 0)),
                  pl.BlockSpec((MXU_WIDTH, MXU_WIDTH), lambda i: (0, 0))],
        out_specs=(rev(4096), pl.BlockSpec((8, D_HGRN), lambda i: (0, 0))),
        scratch_shapes=[pltpu.VMEM((N_HEADS, HEAD, HEAD), F32), pltpu.VMEM((8, D_CONV), F32),
                        pltpu.VMEM((TB, D_HGRN), F32), pltpu.VMEM((TB, D_HGRN), F32),
                        pltpu.VMEM((TB, D_HGRN), F32), pltpu.VMEM((TB, D_HGRN), F32),
                        pltpu.VMEM((TB, D_HGRN), F32), pltpu.VMEM((1, D_HGRN), F32)],
        compiler_params=pltpu.CompilerParams(dimension_semantics=("arbitrary",), vmem_limit_bytes=VMEM_LIMIT),
    )(proj, o, cv, states, sg, b, dmixed, lb_logits, cw, ga, gcn, g128, g64)


TX = 256
(SEM_D2D, SEM_D2D_O, SEM_ICI, SEM_ICI_O, SEM_FIN, SEM_FIN_O, SEM_SMALL, N_SEM_TAIL) = 0, 4, 5, 8, 11, 12, 12, 20


def _bwd_tail(kidx, h, dproj, wg, gwo, x2d, dx2, g1, small):
    hw = D_MODEL // 2
    ho = WO_ROWS // 2
    nt = SEQ // TP
    n_steps = N_SHARD + SEQ // TX // nt

    def body(k_ref, h_ref, dp_ref, w_ref, gwo_ref, x_ref, dx2_ref, g_ref, sm_ref,
             gx_ref, gw_out, gwo_out, osm_ref,
             acc, dh, sendbuf, keep, sibrcv, rcv, sib_o, p_o, rcv_o, res_o, sm_buf, dng,
             send_sems, recv_sems, out_sems):
        s, t = pl.program_id(0), pl.program_id(1)
        x, y, c = lax.axis_index("x"), lax.axis_index("y"), lax.axis_index("c")
        k = 2 * x + y
        me = 4 * x + 2 * y + c
        sibling = (x, y, 1 - c)
        chips = [(1 - x, 1 - y), (1 - x, y), (x, 1 - y)]
        kjs = [2 * cx + cy for cx, cy in chips]
        mine = pl.ds(pl.multiple_of(c * hw, hw), hw)
        other = pl.ds(pl.multiple_of((1 - c) * hw, hw), hw)
        mine_o = pl.ds(pl.multiple_of(c * ho, ho), ho)
        other_o = pl.ds(pl.multiple_of((1 - c) * ho, ho), ho)

        def copy(sem, src, dst, to):
            return pltpu.make_async_remote_copy(
                src_ref=src, dst_ref=dst, send_sem=send_sems.at[sem], recv_sem=recv_sems.at[sem],
                device_id=to, device_id_type=MESH)

        def at_step(sv, tv):
            return pl.when((s == sv) & (t == tv))

        d2d = [copy(SEM_D2D + sv, sendbuf.at[sv], sibrcv.at[sv], sibling) for sv in range(N_SHARD)]
        d2d_o = copy(SEM_D2D_O, gwo_ref.at[:, other_o, :], sib_o, sibling)
        ici = [copy(SEM_ICI + sv, keep.at[sv], rcv.at[sv], (*chips[sv], c)) for sv in range(3)]
        ici_o = [copy(SEM_ICI_O + sv, p_o.at[kjs[sv]], rcv_o.at[sv], (*chips[sv], c)) for sv in range(3)]
        fin = copy(SEM_FIN, acc.at[mine, :], acc.at[mine, :], sibling)
        fin_o = copy(SEM_FIN_O, res_o.at[mine_o, :], res_o.at[mine_o, :], sibling)
        smalls = [copy(SEM_SMALL + m, sm_buf.at[me], sm_buf.at[me],
                       (x ^ (m >> 2), y ^ ((m >> 1) & 1), c ^ (m & 1))) for m in range(1, N_DEV)]
        store_w = pltpu.make_async_copy(acc, gw_out, out_sems.at[0])
        store_o = pltpu.make_async_copy(res_o, gwo_out, out_sems.at[1])

        @at_step(0, 0)
        def _():
            d2d_o.start()

        @at_step(0, 1)
        def _():
            d2d_o.wait_recv()
            for j in range(N_SHARD):
                p_o[j] = (gwo_ref[j, mine_o, :].astype(F32) + sib_o[j].astype(F32)).astype(BF16)
            res_o[mine_o, :] = gwo_ref[k, mine_o, :].astype(F32) + sib_o[k].astype(F32)
            for cp in ici_o:
                cp.start()

        rows = pl.ds(pl.multiple_of(t * TP, TP), TP)

        @pl.when(s < N_SHARD)
        def _():
            dpb = dp_ref[...]
            part = _dot_tn(h_ref[...], dpb)

            @pl.when(t == 0)
            def _():
                acc[...] = part

            @pl.when(t > 0)
            def _():
                acc[...] += part

            d = _dot_nt(dpb, w_ref[0])

            @pl.when(s == 0)
            def _():
                dh[rows, :] = d

            @pl.when(s > 0)
            def _():
                dh[rows, :] += d

        for sv in range(N_SHARD):
            @at_step(sv, nt - 1)
            def _(sv=sv):
                sendbuf[sv] = acc[other, :].astype(BF16)
                if sv < 3:
                    keep[sv] = acc[mine, :].astype(BF16)
                d2d[sv].start()

        for sv in range(3):
            @at_step(sv + 1, 1)
            def _(sv=sv):
                d2d[sv].wait_recv()
                keep[sv] = (keep[sv].astype(F32) + sibrcv[sv].astype(F32)).astype(BF16)
                ici[sv].start()

        @at_step(N_SHARD, 0)
        def _():
            d2d[3].wait_recv()
            ici[0].wait_recv()
            acc[mine, :] += sibrcv[3].astype(F32) + rcv[0].astype(F32)

        @at_step(N_SHARD, 1)
        def _():
            tot = res_o[mine_o, :]
            for sv in range(3):
                ici_o[sv].wait_recv()
                tot = tot + rcv_o[sv].astype(F32)
            res_o[mine_o, :] = tot
            fin_o.start()

        @at_step(N_SHARD, 2)
        def _():
            ici[1].wait_recv()
            acc[mine, :] += rcv[1].astype(F32)

        @at_step(N_SHARD, 0)
        def _():
            dng[...] = jnp.zeros_like(dng)

        @pl.when(s >= N_SHARD)
        def _():
            blk = (s - N_SHARD) * nt + t
            dhv = dh[pl.ds(pl.multiple_of(blk * TX, TX), TX), :]
            xv = x_ref[...]
            r = lax.rsqrt(jnp.mean(xv * xv, axis=-1, keepdims=True) + EPS)
            xn = xv * r
            dng[...] += jnp.sum(dhv * xn, axis=0, keepdims=True)
            dxn = dhv * g_ref[...]
            gx_ref[...] = dx2_ref[...] + r * (dxn - xn * jnp.mean(dxn * xn, axis=-1, keepdims=True))

        @at_step(n_steps - 1, nt - 1)
        def _():
            sm_buf[me] = sm_ref[...]
            sm_buf[me, 0:1, :] = dng[...]
            for cp in smalls:
                cp.start()
            ici[2].wait_recv()
            acc[mine, :] += rcv[2].astype(F32)
            fin.start()
            for m in range(1, N_DEV):
                copy(SEM_SMALL + m, sm_buf.at[0], sm_buf.at[0], sibling).wait_recv()
            tot = sm_buf[0]
            for d in range(1, N_DEV):
                tot = tot + sm_buf[d]
            osm_ref[...] = tot
            fin_o.wait_recv()
            store_o.start()
            fin.wait_recv()
            store_w.start()
            for cp in d2d + [d2d_o] + ici + ici_o + [fin, fin_o] + smalls:
                cp.wait_send()
            store_o.wait()
            store_w.wait()

    def shard_of(s, kr):
        return kr[0] ^ (3 - jnp.minimum(s, 3))

    def tok(s, t):
        return jnp.where(s < N_SHARD, t, nt - 1)

    def blk_map(s, t, kr):
        return (jnp.where(s < N_SHARD, 0, (s - N_SHARD) * nt + t), 0)

    hbm = pl.BlockSpec(memory_space=pl.ANY)
    grid_spec = pltpu.PrefetchScalarGridSpec(
        num_scalar_prefetch=1, grid=(n_steps, nt),
        in_specs=[pl.BlockSpec((TP, D_MODEL), lambda s, t, kr: (tok(s, t), 0)),
                  pl.BlockSpec((TP, SHARD_COLS), lambda s, t, kr: (tok(s, t), shard_of(s, kr))),
                  pl.BlockSpec((1, D_MODEL, SHARD_COLS), lambda s, t, kr: (shard_of(s, kr), 0, 0)),
                  pl.BlockSpec((N_SHARD, WO_ROWS, D_MODEL), lambda s, t, kr: (0, 0, 0)),
                  pl.BlockSpec((TX, D_MODEL), blk_map),
                  pl.BlockSpec((TX, D_MODEL), blk_map),
                  pl.BlockSpec((1, D_MODEL), lambda s, t, kr: (0, 0)),
                  pl.BlockSpec((8, D_MODEL), lambda s, t, kr: (0, 0))],
        out_specs=(pl.BlockSpec((TX, D_MODEL), blk_map), hbm, hbm,
                   pl.BlockSpec((8, D_MODEL), lambda s, t, kr: (0, 0))),
        scratch_shapes=[pltpu.VMEM((D_MODEL, SHARD_COLS), F32), pltpu.VMEM((SEQ, D_MODEL), F32),
                        pltpu.VMEM((N_SHARD, hw, SHARD_COLS), BF16), pltpu.VMEM((3, hw, SHARD_COLS), BF16),
                        pltpu.VMEM((N_SHARD, hw, SHARD_COLS), BF16), pltpu.VMEM((3, hw, SHARD_COLS), BF16),
                        pltpu.VMEM((N_SHARD, ho, D_MODEL), BF16), pltpu.VMEM((N_SHARD, ho, D_MODEL), BF16),
                        pltpu.VMEM((3, ho, D_MODEL), BF16), pltpu.VMEM((WO_ROWS, D_MODEL), F32),
                        pltpu.VMEM((N_DEV, 8, D_MODEL), F32), pltpu.VMEM((1, D_MODEL), F32),
                        pltpu.SemaphoreType.DMA((N_SEM_TAIL,)), pltpu.SemaphoreType.DMA((N_SEM_TAIL,)),
                        pltpu.SemaphoreType.DMA((2,))])
    return pl.pallas_call(
        body, name="bwd_tail", grid_spec=grid_spec,
        out_shape=(jax.ShapeDtypeStruct((SEQ, D_MODEL), F32),
                   jax.ShapeDtypeStruct((D_MODEL, SHARD_COLS), F32),
                   jax.ShapeDtypeStruct((WO_ROWS, D_MODEL), F32),
                   jax.ShapeDtypeStruct((8, D_MODEL), F32)),
        compiler_params=pltpu.CompilerParams(dimension_semantics=("arbitrary", "arbitrary"),
                                             vmem_limit_bytes=60 * 1024 * 1024),
    )(kidx, h, dproj, wg, gwo, x2d, dx2, g1, small)


def _adam_update(w, g, m, v):
    nm = ADAM_B1 * m + (1.0 - ADAM_B1) * g
    nv = ADAM_B2 * v + (1.0 - ADAM_B2) * (g * g)
    m_hat = nm / (1.0 - ADAM_B1 ** ADAM_STEP)
    v_hat = nv / (1.0 - ADAM_B2 ** ADAM_STEP)
    return -ADAM_LR * (m_hat / (jnp.sqrt(v_hat) + ADAM_EPS) + ADAM_WD * w), nm, nv


def _adamw_small(tot, params):
    n = len(params)

    def body(tot_ref, *refs):
        ins, outs = refs[:3 * n], refs[3 * n:]
        k = 2 * lax.axis_index("x") + lax.axis_index("y")
        grads = [tot_ref[0:1, :], tot_ref[1:2, :], tot_ref[2:3, 0:D_HGRN], tot_ref[2:3, D_HGRN:],
                 jnp.concatenate([tot_ref[3:4, 0:D_HGRN], tot_ref[3:4, D_HGRN:]], axis=0),
                 tot_ref[4:7, pl.ds(pl.multiple_of(k * HEAD, HEAD), HEAD)]]
        for i, g in enumerate(grads):
            w_ref, m_ref, v_ref = ins[3 * i:3 * i + 3]
            g_ref, d_ref, nm_ref, nv_ref = outs[4 * i:4 * i + 4]
            g_ref[...] = g
            d_ref[...], nm_ref[...], nv_ref[...] = _adam_update(w_ref[...], g, m_ref[...], v_ref[...])

    vm = pl.BlockSpec(memory_space=pltpu.VMEM)
    flat = [a for triple in params for a in triple]
    out_shape = tuple(jax.ShapeDtypeStruct(w.shape, F32) for w, _, _ in params for _ in range(4))
    outs = pl.pallas_call(
        body, name="adamw_small", out_shape=out_shape,
        in_specs=[vm] * (1 + 3 * n), out_specs=tuple([vm] * (4 * n)),
    )(tot, *flat)
    return [outs[4 * i:4 * i + 4] for i in range(n)]


def _adamw(w, g, m, v, name):
    rows, cols = w.shape
    tr = rows if rows <= 256 else 256

    def body(w_ref, g_ref, m_ref, v_ref, d_ref, nm_ref, nv_ref):
        d_ref[...], nm_ref[...], nv_ref[...] = _adam_update(w_ref[...], g_ref[...], m_ref[...], v_ref[...])

    blk = lambda: pl.BlockSpec((tr, cols), lambda i: (i, 0))
    shp = jax.ShapeDtypeStruct((rows, cols), F32)
    return pl.pallas_call(
        body, name=name, grid=(rows // tr,),
        out_shape=(shp, shp, shp),
        in_specs=[blk(), blk(), blk(), blk()], out_specs=(blk(), blk(), blk()),
        compiler_params=pltpu.CompilerParams(dimension_semantics=("arbitrary",)),
    )(w, g, m, v)


def _local_step(x2d, tgt, proj, lb_logits, cw, ga, gcn, wog, gf):
    g128 = _group_matrix(MXU_WIDTH, HEAD)
    g64 = _group_matrix(MXU_WIDTH, CONV_GROUP)
    mixed, o, cv, states, sg, b = _mix_fwd(proj, lb_logits, cw, ga, gcn, g128, g64)
    dx2, dmixed, gwo, part_out = _out_loss(x2d, mixed, wog, gf, tgt)
    dproj, part_mix = _mix_bwd(proj, o, cv, states, sg, b, dmixed, lb_logits, cw, ga, gcn, g128, g64)
    zeros = jnp.zeros((3, D_MODEL - D_CONV), F32)
    small = jnp.concatenate([
        jnp.zeros((1, D_MODEL), F32), part_out[0:1],
        jnp.concatenate([part_mix[3:4], part_mix[4:5]], axis=1),
        jnp.concatenate([part_mix[5:6], part_mix[6:7]], axis=1),
        jnp.concatenate([part_mix[0:3], zeros], axis=1),
        part_out[1:2]], axis=0)
    return dproj, dx2, gwo, small


def kernel(x, norm_gain, w_in, lb_logits, conv_w, hgrn_norm_gain, conv_norm_gain, w_out, final_norm_gain, loss_target, m_norm_gain, m_w_in, m_lb_logits, m_conv_w, m_hgrn_norm_gain, m_conv_norm_gain, m_w_out, m_final_norm_gain, v_norm_gain, v_w_in, v_lb_logits, v_conv_w, v_hgrn_norm_gain, v_conv_norm_gain, v_w_out, v_final_norm_gain):
    k = 2 * lax.axis_index("x") + lax.axis_index("y")
    kidx = jnp.reshape(k, (1,)).astype(jnp.int32)
    h, proj, wg, wog4, cwg = _gather_proj(kidx, x[0], norm_gain, w_in, w_out, conv_w)
    wog = wog4.reshape(D_MODEL, D_MODEL)
    cw = jnp.transpose(cwg, (1, 0, 2)).reshape(8, D_CONV)

    dproj, dx2, gwo, small = _local_step(
        x[0], loss_target[0], proj, lb_logits, cw, hgrn_norm_gain, conv_norm_gain, wog,
        final_norm_gain.reshape(1, D_MODEL))
    grad_x, g_w_in, g_w_out, tot = _bwd_tail(
        kidx, h, dproj, wg, gwo.reshape(N_SHARD, WO_ROWS, D_MODEL), x[0], dx2, norm_gain, small)

    loss = tot[7, 0]
    d_w_in, nm_w_in, nv_w_in = _adamw(w_in[0], g_w_in, m_w_in[0], v_w_in[0], "adamw_w_in")
    d_w_out, nm_w_out, nv_w_out = _adamw(w_out[0], g_w_out, m_w_out[0], v_w_out[0], "adamw_w_out")
    row = lambda a: a.reshape(1, D_MODEL)
    ((g_norm_gain, d_ng, nm_ng, nv_ng), (g_final, d_fg, nm_fg, nv_fg), (g_hgrn, d_hg, nm_hg, nv_hg),
     (g_convn, d_cg, nm_cg, nv_cg), (g_lb, d_lb, nm_lb, nv_lb), (g_conv_w, d_cw, nm_cw, nv_cw)) = _adamw_small(
        tot, [(norm_gain, m_norm_gain, v_norm_gain),
              (row(final_norm_gain), row(m_final_norm_gain), row(v_final_norm_gain)),
              (hgrn_norm_gain, m_hgrn_norm_gain, v_hgrn_norm_gain),
              (conv_norm_gain, m_conv_norm_gain, v_conv_norm_gain),
              (lb_logits, m_lb_logits, v_lb_logits),
              (conv_w[0], m_conv_w[0], v_conv_w[0])])
    flat = lambda a: a.reshape(D_MODEL)
    return (loss, grad_x[None],
            g_norm_gain, g_w_in[None], g_lb, g_conv_w[None], g_hgrn, g_convn, g_w_out[None], flat(g_final),
            d_ng, d_w_in[None], d_lb, d_cw[None], d_hg, d_cg, d_w_out[None], flat(d_fg),
            nm_ng, nm_w_in[None], nm_lb, nm_cw[None], nm_hg, nm_cg, nm_w_out[None], flat(nm_fg),
            nv_ng, nv_w_in[None], nv_lb, nv_cw[None], nv_hg, nv_cg, nv_w_out[None], flat(nv_fg))
```

```python
import functools

import jax
import jax.numpy as jnp
import numpy as np
from jax import lax
from jax.experimental import pallas as pl
from jax.experimental.pallas import tpu as pltpu

F32 = jnp.float32
BF16 = jnp.bfloat16
MESH = pl.DeviceIdType.MESH

SEQ = 2048
D_MODEL = 1024
D_HGRN = 512
D_CONV = 512
HEAD = 128
N_HEADS = 4
CHUNK = 64
CONV_GROUP = 64
N_SHARD = 4
SHARD_COLS = 1024
WO_ROWS = 256
EPS = 1e-6
TB = 256
NCB = TB // CHUNK
N_CHUNKS = SEQ // CHUNK
N_DEV = 8
MXU_WIDTH = 256

ADAM_LR = 0.001
ADAM_B1 = 0.9
ADAM_B2 = 0.999
ADAM_EPS = 1e-08
ADAM_WD = 0.01
ADAM_STEP = 10

VMEM_LIMIT = 56 * 1024 * 1024


def _dot(a, b):
    return jnp.dot(a, b, preferred_element_type=F32)


def _dot_nt(a, b):
    return lax.dot_general(a, b, (((1,), (1,)), ((), ())), preferred_element_type=F32)


def _dot_tn(a, b):
    return lax.dot_general(a, b, (((0,), (0,)), ((), ())), preferred_element_type=F32)


def _split_bf16(x, n):
    parts = []
    r = x
    for _ in range(n):
        p = r.astype(BF16)
        parts.append(p)
        r = r - p.astype(F32)
    return parts


def _exact_left(m, x, n=3):
    acc = None
    for p in _split_bf16(x, n):
        t = _dot(m, p)
        acc = t if acc is None else acc + t
    return acc


def _group_mean(x, gmat, n=2):
    w = gmat.shape[0]
    outs = []
    for c0 in range(0, x.shape[1], w):
        acc = None
        for p in _split_bf16(x[:, c0:c0 + w], n):
            t = _dot(p, gmat)
            acc = t if acc is None else acc + t
        outs.append(acc)
    return jnp.concatenate(outs, axis=1)


def _sigmoid(x):
    return 1.0 / (1.0 + jnp.exp(-x))


def _lower_bound(lbl):
    l0 = lbl[0:1, :]
    l1 = lbl[1:2, :]
    m = jnp.maximum(l0, l1)
    e0 = jnp.exp(l0 - m)
    e1 = jnp.exp(l1 - m)
    return e0 / (e0 + e1)


def _chunk_tri(lower):
    r = lax.broadcasted_iota(jnp.int32, (TB, TB), 0)
    c = lax.broadcasted_iota(jnp.int32, (TB, TB), 1)
    same = (r // CHUNK) == (c // CHUNK)
    tri = (c <= r) if lower else (c >= r)
    return jnp.where(same & tri, 1.0, 0.0).astype(BF16)


def _causal():
    r = lax.broadcasted_iota(jnp.int32, (CHUNK, CHUNK), 0)
    c = lax.broadcasted_iota(jnp.int32, (CHUNK, CHUNK), 1)
    return c <= r


def _shift_down(x, sh, prev_tail):
    r = pltpu.roll(x, sh, 0)
    pt = pltpu.roll(prev_tail, sh, 0)
    rows = lax.broadcasted_iota(jnp.int32, prev_tail.shape, 0)
    top = jnp.where(rows < sh, pt, r[0:8])
    return jnp.concatenate([top, r[8:]], axis=0)


def _shift_up(x, sh, next_head):
    n = x.shape[0]
    r = pltpu.roll(x, n - sh, 0)
    nh = pltpu.roll(next_head, 8 - sh, 0)
    rows = lax.broadcasted_iota(jnp.int32, next_head.shape, 0)
    bot = jnp.where(rows >= 8 - sh, nh, r[n - 8:])
    return jnp.concatenate([r[:n - 8], bot], axis=0)


def _group_matrix(width, group):
    r = np.arange(width)[:, None] // group
    c = np.arange(width)[None, :] // group
    return jnp.asarray(np.where(r == c, 1.0 / group, 0.0), dtype=BF16)


TP = 512
SEM_W, SEM_CW, SEM_W_FWD, N_SEM = 0, 3, 6, 9


def _gather_proj(kidx, x2d, g1, w_in, conv_w):
    half_w = D_MODEL // 2
    nt = SEQ // TP

    def body(k_ref, x_ref, g_ref, w_ref, cw_ref, h_ref, p_ref, wg_out, cwg_out,
             wg_v, cwg_v, send_sems, recv_sems, out_sems):
        s, t = pl.program_id(0), pl.program_id(1)
        x, y, c = lax.axis_index("x"), lax.axis_index("y"), lax.axis_index("c")
        k = 2 * x + y
        sibling = (x, y, 1 - c)
        chips = [(1 - x, y), (x, 1 - y), (1 - x, 1 - y)]
        kjs = [2 * cx + cy for cx, cy in chips]

        def w_half(kk, cc):
            return wg_v.at[kk, pl.ds(cc * half_w, half_w), :]

        def cw_of(kk):
            return cwg_v.at[:, pl.ds(pl.multiple_of(kk * HEAD, HEAD), HEAD)]

        def copy(sem, ref, to):
            return pltpu.make_async_remote_copy(
                src_ref=ref, dst_ref=ref, send_sem=send_sems.at[sem], recv_sem=recv_sems.at[sem],
                device_id=to, device_id_type=MESH)

        def at_step(sv, tv):
            return pl.when((s == sv) & (t == tv))

        w_direct = [copy(SEM_W + j, w_half(k, c), (*chip, c)) for j, chip in enumerate(chips)]
        cw_direct = [copy(SEM_CW + j, cw_of(k), (*chip, c)) for j, chip in enumerate(chips)]
        w_passed = [copy(SEM_W_FWD + j, w_half(kj, c), sibling) for j, kj in enumerate(kjs)]
        stores = ([pltpu.make_async_copy(wg_v.at[kk], wg_out.at[kk], out_sems.at[i])
                   for i, kk in enumerate([k] + kjs)]
                  + [pltpu.make_async_copy(cwg_v, cwg_out, out_sems.at[4])])

        @at_step(0, 0)
        def _():
            wg_v[k] = w_ref[0].astype(BF16)
            mine = pl.ds(pl.multiple_of(k * HEAD, HEAD), HEAD)
            cwg_v[:, mine] = jnp.zeros((8, HEAD), F32)
            for tap in range(3):
                cwg_v[tap:tap + 1, mine] = cw_ref[:, tap * HEAD:(tap + 1) * HEAD]
            w_direct[0].start()
            w_direct[1].start()
            for cp in cw_direct:
                cp.start()
            stores[0].start()

        @at_step(1, 0)
        def _():
            for j in range(2):
                copy(SEM_W + j, w_half(kjs[j], c), sibling).wait_recv()
                w_passed[j].start()
            w_direct[2].start()
            copy(SEM_W_FWD, w_half(kjs[0], 1 - c), sibling).wait_recv()
            stores[1].start()

        @at_step(2, 0)
        def _():
            copy(SEM_W_FWD + 1, w_half(kjs[1], 1 - c), sibling).wait_recv()
            stores[2].start()

        @at_step(3, 0)
        def _():
            copy(SEM_W + 2, w_half(kjs[2], c), sibling).wait_recv()
            w_passed[2].start()
            copy(SEM_W_FWD + 2, w_half(kjs[2], 1 - c), sibling).wait_recv()
            stores[3].start()

        rows = pl.ds(pl.multiple_of(t * TP, TP), TP)

        @pl.when(s == 0)
        def _():
            xv = x_ref[...]
            r = lax.rsqrt(jnp.mean(xv * xv, axis=-1, keepdims=True) + EPS)
            h_ref[rows, :] = (xv * r * g_ref[...]).astype(BF16)

        js = k ^ (((s & 1) << 1) | (s >> 1))
        p_ref[...] = _dot(h_ref[rows, :], wg_v[js])

        @at_step(N_SHARD - 1, nt - 1)
        def _():
            for j in range(3):
                copy(SEM_CW + j, cw_of(kjs[j]), sibling).wait_recv()
            stores[4].start()
            for cp in w_direct + cw_direct + w_passed:
                cp.wait_send()
            for st in stores:
                st.wait()

    def x_map(s, t, kr):
        return (jnp.where(s == 0, t, nt - 1), 0)

    def p_map(s, t, kr):
        return (t, kr[0] ^ (((s & 1) << 1) | (s >> 1)))

    hbm = pl.BlockSpec(memory_space=pl.ANY)
    grid_spec = pltpu.PrefetchScalarGridSpec(
        num_scalar_prefetch=1, grid=(N_SHARD, nt),
        in_specs=[pl.BlockSpec((TP, D_MODEL), x_map),
                  pl.BlockSpec((1, D_MODEL), lambda s, t, kr: (0, 0)),
                  pl.BlockSpec((1, D_MODEL, SHARD_COLS), lambda s, t, kr: (0, 0, 0)),
                  pl.BlockSpec((1, 3 * HEAD), lambda s, t, kr: (0, 0))],
        out_specs=(pl.BlockSpec((SEQ, D_MODEL), lambda s, t, kr: (0, 0)),
                   pl.BlockSpec((TP, SHARD_COLS), p_map), hbm, hbm),
        scratch_shapes=[pltpu.VMEM((N_SHARD, D_MODEL, SHARD_COLS), BF16),
                        pltpu.VMEM((8, D_CONV), F32),
                        pltpu.SemaphoreType.DMA((N_SEM,)), pltpu.SemaphoreType.DMA((N_SEM,)),
                        pltpu.SemaphoreType.DMA((5,))])
    return pl.pallas_call(
        body, name="gather_proj", grid_spec=grid_spec,
        out_shape=(jax.ShapeDtypeStruct((SEQ, D_MODEL), BF16),
                   jax.ShapeDtypeStruct((SEQ, N_SHARD * SHARD_COLS), F32),
                   jax.ShapeDtypeStruct((N_SHARD, D_MODEL, SHARD_COLS), BF16),
                   jax.ShapeDtypeStruct((8, D_CONV), F32)),
        compiler_params=pltpu.CompilerParams(dimension_semantics=("arbitrary", "arbitrary"),
                                             vmem_limit_bytes=VMEM_LIMIT),
    )(kidx, x2d, g1, w_in, conv_w)


def _mix_fwd(proj, lb_logits, cw, ga, gcn, g128, g64, w_out):
    half_o = WO_ROWS // 2
    nblk = SEQ // TB

    def body(p_ref, lbl_ref, cw_ref, ga_ref, gcn_ref, g128_ref, g64_ref, wo_ref,
             mixed_ref, o_ref, cv_ref, sto_ref, sg_ref, b_ref, wog_out,
             st_ref, tail_ref, f_ref, wog_v, send_sems, recv_sems, out_sem):
        i = pl.program_id(0)
        x, y, c = lax.axis_index("x"), lax.axis_index("y"), lax.axis_index("c")
        k = 2 * x + y
        sibling = (x, y, 1 - c)
        chips = [(1 - x, y), (x, 1 - y), (1 - x, 1 - y)]
        kjs = [2 * cx + cy for cx, cy in chips]

        def wo_half(kk, cc):
            return wog_v.at[kk, pl.ds(cc * half_o, half_o), :]

        def copy(sem, ref, to):
            return pltpu.make_async_remote_copy(
                src_ref=ref, dst_ref=ref, send_sem=send_sems.at[sem], recv_sem=recv_sems.at[sem],
                device_id=to, device_id_type=MESH)

        wo_direct = [copy(j, wo_half(k, c), (*chip, c)) for j, chip in enumerate(chips)]
        wo_passed = [copy(3 + j, wo_half(kj, c), sibling) for j, kj in enumerate(kjs)]
        wo_store = pltpu.make_async_copy(wog_v, wog_out, out_sem.at[0])

        @pl.when(i == 0)
        def _():
            st_ref[...] = jnp.zeros_like(st_ref)
            tail_ref[...] = jnp.zeros_like(tail_ref)
            wog_v[k] = wo_ref[0].astype(BF16)
            for cp in wo_direct:
                cp.start()

        @pl.when(i == nblk // 2)
        def _():
            for j in range(3):
                copy(j, wo_half(kjs[j], c), sibling).wait_recv()
                wo_passed[j].start()

        lb = _lower_bound(lbl_ref[...])
        sg = _sigmoid(p_ref[:, 512:1024])
        sg_ref[...] = sg
        f = lb + (1.0 - lb) * sg
        f_ref[...] = f
        b_ref[...] = _exact_left(_chunk_tri(True), jnp.log(f))
        causal = _causal()
        for n in range(NCB):
            sl = pl.ds(n * CHUNK, CHUNK)
            bc = b_ref[sl, :]
            g = b_ref[n * CHUNK + CHUNK - 1:n * CHUNK + CHUNK, :]
            kk = 1.0 - f_ref[sl, :]
            qd = (p_ref[sl, 0:512] * jnp.exp(bc)).astype(BF16)
            ki = (kk * jnp.exp(-bc)).astype(BF16)
            ke = (kk * jnp.exp(g - bc)).astype(BF16)
            vb = p_ref[sl, 1024:1536].astype(BF16)
            dec = jnp.exp(g)
            for hd in range(N_HEADS):
                cs = slice(hd * HEAD, (hd + 1) * HEAD)
                st = st_ref[hd]
                sto_ref[n, hd] = st
                sc = jnp.where(causal, _dot_nt(qd[:, cs], ki[:, cs]), 0.0)
                o_ref[sl, cs] = _dot(sc.astype(BF16), vb[:, cs]) + _dot_nt(qd[:, cs], st.astype(BF16))
                st_ref[hd] = st * dec[:, cs] + _dot_tn(vb[:, cs], ke[:, cs])

        o = o_ref[...]
        ra = lax.rsqrt(_group_mean(o * o, g128_ref[...]) + EPS)
        za = p_ref[:, 1536:2048]
        mixed_ref[:, 0:512] = (o * ra * ga_ref[...] * (za * _sigmoid(za))).astype(BF16)

        cu = p_ref[:, 3072:3584] * p_ref[:, 2048:2560]
        tail = tail_ref[...]
        cv = (cw_ref[0:1, :] * _shift_down(cu, 2, tail) + cw_ref[1:2, :] * _shift_down(cu, 1, tail)
              + cw_ref[2:3, :] * cu)
        tail_ref[...] = cu[TB - 8:, :]
        cv_ref[...] = cv
        yb = p_ref[:, 2560:3072] * cv
        rb = lax.rsqrt(_group_mean(yb * yb, g64_ref[...]) + EPS)
        zb = p_ref[:, 3584:4096]
        mixed_ref[:, 512:1024] = (yb * rb * gcn_ref[...] * (zb * _sigmoid(zb))).astype(BF16)

        @pl.when(i == nblk - 1)
        def _():
            for j in range(3):
                copy(3 + j, wo_half(kjs[j], 1 - c), sibling).wait_recv()
            wo_store.start()
            for cp in wo_direct + wo_passed:
                cp.wait_send()
            wo_store.wait()

    row = lambda w: pl.BlockSpec((1, w), lambda i: (0, 0))
    return pl.pallas_call(
        body, name="mix_fwd", grid=(nblk,),
        out_shape=(jax.ShapeDtypeStruct((SEQ, D_MODEL), BF16),
                   jax.ShapeDtypeStruct((SEQ, D_HGRN), F32),
                   jax.ShapeDtypeStruct((SEQ, D_CONV), F32),
                   jax.ShapeDtypeStruct((N_CHUNKS, N_HEADS, HEAD, HEAD), F32),
                   jax.ShapeDtypeStruct((SEQ, D_HGRN), F32),
                   jax.ShapeDtypeStruct((SEQ, D_HGRN), F32),
                   jax.ShapeDtypeStruct((N_SHARD, WO_ROWS, D_MODEL), BF16)),
        in_specs=[pl.BlockSpec((TB, 4096), lambda i: (i, 0)),
                  pl.BlockSpec((2, D_HGRN), lambda i: (0, 0)),
                  pl.BlockSpec((8, D_CONV), lambda i: (0, 0)),
                  row(D_HGRN), row(D_CONV),
                  pl.BlockSpec((MXU_WIDTH, MXU_WIDTH), lambda i: (0, 0)),
                  pl.BlockSpec((MXU_WIDTH, MXU_WIDTH), lambda i: (0, 0)),
                  pl.BlockSpec((1, WO_ROWS, D_MODEL), lambda i: (0, 0, 0))],
        out_specs=(pl.BlockSpec((TB, D_MODEL), lambda i: (i, 0)),
                   pl.BlockSpec((TB, D_HGRN), lambda i: (i, 0)),
                   pl.BlockSpec((TB, D_CONV), lambda i: (i, 0)),
                   pl.BlockSpec((NCB, N_HEADS, HEAD, HEAD), lambda i: (i, 0, 0, 0)),
                   pl.BlockSpec((TB, D_HGRN), lambda i: (i, 0)),
                   pl.BlockSpec((TB, D_HGRN), lambda i: (i, 0)),
                   pl.BlockSpec(memory_space=pl.ANY)),
        scratch_shapes=[pltpu.VMEM((N_HEADS, HEAD, HEAD), F32), pltpu.VMEM((8, D_CONV), F32),
                        pltpu.VMEM((TB, D_HGRN), F32),
                        pltpu.VMEM((N_SHARD, WO_ROWS, D_MODEL), BF16),
                        pltpu.SemaphoreType.DMA((6,)), pltpu.SemaphoreType.DMA((6,)),
                        pltpu.SemaphoreType.DMA((1,))],
        compiler_params=pltpu.CompilerParams(dimension_semantics=("arbitrary",), vmem_limit_bytes=VMEM_LIMIT),
    )(proj, lb_logits, cw, ga, gcn, g128, g64, w_out)


def _out_loss(x2d, mixed, wog, gf, tgt):
    def body(x_ref, m_ref, wo_ref, gf_ref, t_ref, dx2_ref, dm_ref, gwo_ref, part_ref, acc_ref):
        i = pl.program_id(0)

        @pl.when(i == 0)
        def _():
            acc_ref[...] = jnp.zeros_like(acc_ref)
            part_ref[...] = jnp.zeros_like(part_ref)

        mixed_b = m_ref[...]
        x2 = x_ref[...] + _dot(mixed_b, wo_ref[...])
        r2 = lax.rsqrt(jnp.mean(x2 * x2, axis=-1, keepdims=True) + EPS)
        n2 = x2 * r2
        gfv = gf_ref[...]
        err = n2 * gfv - t_ref[...]
        loss = 0.5 * jnp.sum(jnp.mean(err * err, axis=-1, keepdims=True), axis=0, keepdims=True)
        dy = err * (1.0 / D_MODEL)
        part_ref[1:2, :] += jnp.sum(dy * n2, axis=0, keepdims=True)
        part_ref[7:8, :] += jnp.broadcast_to(loss, (1, D_MODEL))
        dn = dy * gfv
        dx2 = r2 * (dn - n2 * jnp.mean(dn * n2, axis=-1, keepdims=True))
        dx2_ref[...] = dx2
        dx2_b = dx2.astype(BF16)
        dm_ref[...] = _dot_nt(dx2_b, wo_ref[...])
        acc_ref[...] += _dot_tn(mixed_b, dx2_b)

        @pl.when(i == pl.num_programs(0) - 1)
        def _():
            gwo_ref[...] = acc_ref[...].astype(BF16)

    blk = lambda: pl.BlockSpec((TP, D_MODEL), lambda i: (i, 0))
    return pl.pallas_call(
        body, name="out_loss", grid=(SEQ // TP,),
        out_shape=(jax.ShapeDtypeStruct((SEQ, D_MODEL), F32),
                   jax.ShapeDtypeStruct((SEQ, D_MODEL), F32),
                   jax.ShapeDtypeStruct((D_MODEL, D_MODEL), BF16),
                   jax.ShapeDtypeStruct((8, D_MODEL), F32)),
        in_specs=[blk(), blk(), pl.BlockSpec((D_MODEL, D_MODEL), lambda i: (0, 0)),
                  pl.BlockSpec((1, D_MODEL), lambda i: (0, 0)), blk()],
        out_specs=(blk(), blk(), pl.BlockSpec((D_MODEL, D_MODEL), lambda i: (0, 0)),
                   pl.BlockSpec((8, D_MODEL), lambda i: (0, 0))),
        scratch_shapes=[pltpu.VMEM((D_MODEL, D_MODEL), F32)],
        compiler_params=pltpu.CompilerParams(dimension_semantics=("arbitrary",), vmem_limit_bytes=VMEM_LIMIT),
    )(x2d, mixed, wog, gf, tgt)


def _mix_bwd(proj, o, cv, states, sg, b, dmixed, lb_logits, cw, ga, gcn, g128, g64):
    nblk = SEQ // TB

    def body(p_ref, o_ref, cv_ref, st_ref, sg_ref, b_ref, dm_ref, lbl_ref, cw_ref, ga_ref, gcn_ref,
             g128_ref, g64_ref,
             dp_ref, part_ref, dst_ref, head_ref, f_ref, do_ref, db_ref, dg_ref, dk_ref, dlb_ref):
        i = pl.program_id(0)

        @pl.when(i == 0)
        def _():
            dst_ref[...] = jnp.zeros_like(dst_ref)
            head_ref[...] = jnp.zeros_like(head_ref)
            part_ref[...] = jnp.zeros_like(part_ref)
            dlb_ref[...] = jnp.zeros_like(dlb_ref)

        ov = o_ref[...]
        ra = lax.rsqrt(_group_mean(ov * ov, g128_ref[...]) + EPS)
        na = ov * ra
        za = p_ref[:, 1536:2048]
        sg = _sigmoid(za)
        dma = dm_ref[:, 0:512]
        gav = ga_ref[...]
        part_ref[2:3, 0:D_HGRN] += jnp.sum(dma * na * (za * sg), axis=0, keepdims=True)
        dp_ref[:, 1536:2048] = (dma * na * gav * (sg * (1.0 + za * (1.0 - sg)))).astype(BF16)
        dna = dma * gav * (za * sg)
        do_ref[...] = ra * (dna - na * _group_mean(dna * na, g128_ref[...]))

        cvv = cv_ref[...]
        gb = p_ref[:, 2560:3072]
        yb = gb * cvv
        rb = lax.rsqrt(_group_mean(yb * yb, g64_ref[...]) + EPS)
        nb = yb * rb
        zb = p_ref[:, 3584:4096]
        sgb = _sigmoid(zb)
        dmb = dm_ref[:, 512:1024]
        gcv = gcn_ref[...]
        part_ref[2:3, D_HGRN:] += jnp.sum(dmb * nb * (zb * sgb), axis=0, keepdims=True)
        dp_ref[:, 3584:4096] = (dmb * nb * gcv * (sgb * (1.0 + zb * (1.0 - sgb)))).astype(BF16)
        dnb = dmb * gcv * (zb * sgb)
        dyb = rb * (dnb - nb * _group_mean(dnb * nb, g64_ref[...]))
        dp_ref[:, 2560:3072] = (dyb * cvv).astype(BF16)
        dcv = dyb * gb
        head = head_ref[...]
        dcv1 = _shift_up(dcv, 1, head)
        dcv2 = _shift_up(dcv, 2, head)
        head_ref[...] = dcv[0:8, :]
        u = p_ref[:, 2048:2560]
        gc = p_ref[:, 3072:3584]
        cu = gc * u
        part_ref[4:5, 0:D_CONV] += jnp.sum(dcv2 * cu, axis=0, keepdims=True)
        part_ref[5:6, 0:D_CONV] += jnp.sum(dcv1 * cu, axis=0, keepdims=True)
        part_ref[6:7, 0:D_CONV] += jnp.sum(dcv * cu, axis=0, keepdims=True)
        dcu = cw_ref[2:3, :] * dcv + cw_ref[1:2, :] * dcv1 + cw_ref[0:1, :] * dcv2
        dp_ref[:, 3072:3584] = (dcu * u).astype(BF16)
        dp_ref[:, 2048:2560] = (dcu * gc).astype(BF16)

        lb = _lower_bound(lbl_ref[...])
        s = sg_ref[...]
        f = lb + (1.0 - lb) * s
        f_ref[...] = f
        causal = _causal()
        for n in reversed(range(NCB)):
            sl = pl.ds(n * CHUNK, CHUNK)
            bc = b_ref[sl, :]
            g = b_ref[n * CHUNK + CHUNK - 1:n * CHUNK + CHUNK, :]
            kk = 1.0 - f_ref[sl, :]
            eb = jnp.exp(bc)
            enb = jnp.exp(-bc)
            eg = jnp.exp(g - bc)
            dec = jnp.exp(g)
            qd = p_ref[sl, 0:512] * eb
            ki = kk * enb
            ke = kk * eg
            qd_b = qd.astype(BF16)
            ki_b = ki.astype(BF16)
            ke_b = ke.astype(BF16)
            vb = p_ref[sl, 1024:1536].astype(BF16)
            do_b = do_ref[sl, :].astype(BF16)
            for hd in range(N_HEADS):
                cs = slice(hd * HEAD, (hd + 1) * HEAD)
                st = st_ref[n, hd]
                dst = dst_ref[hd]
                st_b = st.astype(BF16)
                dst_b = dst.astype(BF16)
                sc = jnp.where(causal, _dot_nt(qd_b[:, cs], ki_b[:, cs]), 0.0).astype(BF16)
                am = jnp.where(causal, _dot_nt(do_b[:, cs], vb[:, cs]), 0.0).astype(BF16)
                dqd = _dot(am, ki_b[:, cs]) + _dot(do_b[:, cs], st_b)
                dki = _dot_tn(am, qd_b[:, cs])
                dke = _dot(vb[:, cs], dst_b)
                dv = _dot_tn(sc, do_b[:, cs]) + _dot_nt(ke_b[:, cs], dst_b)
                ddec = jnp.sum(dst * st, axis=0, keepdims=True)
                dst_ref[hd] = dst * dec[:, cs] + _dot_tn(do_b[:, cs], qd_b[:, cs])
                dp_ref[sl, cs] = (dqd * eb[:, cs]).astype(BF16)
                dp_ref[sl, 1024 + hd * HEAD:1024 + (hd + 1) * HEAD] = dv.astype(BF16)
                dk_ref[sl, cs] = dki * enb[:, cs] + dke * eg[:, cs]
                db_ref[sl, cs] = dqd * qd[:, cs] - dki * ki[:, cs] - dke * ke[:, cs]
                dgv = jnp.sum(dke * ke[:, cs], axis=0, keepdims=True) + ddec * dec[:, cs]
                dg_ref[sl, cs] = jnp.broadcast_to(dgv, (CHUNK, HEAD))

        dlogf = _exact_left(_chunk_tri(False), db_ref[...], 2) + dg_ref[...]
        df = dlogf / f - dk_ref[...]
        dlb_ref[...] += jnp.sum(df * (1.0 - s), axis=0, keepdims=True)
        dp_ref[:, 512:1024] = (df * (1.0 - lb) * s * (1.0 - s)).astype(BF16)

        @pl.when(i == nblk - 1)
        def _():
            row = dlb_ref[...] * lb * (1.0 - lb)
            part_ref[3:4, 0:D_HGRN] = row
            part_ref[3:4, D_HGRN:] = -row

    rev = lambda w: pl.BlockSpec((TB, w), lambda i: (nblk - 1 - i, 0))
    row = lambda w: pl.BlockSpec((1, w), lambda i: (0, 0))
    return pl.pallas_call(
        body, name="mix_bwd", grid=(nblk,),
        out_shape=(jax.ShapeDtypeStruct((SEQ, 4096), BF16),
                   jax.ShapeDtypeStruct((8, D_MODEL), F32)),
        in_specs=[rev(4096), rev(D_HGRN), rev(D_CONV),
                  pl.BlockSpec((NCB, N_HEADS, HEAD, HEAD), lambda i: (nblk - 1 - i, 0, 0, 0)),
                  rev(D_HGRN), rev(D_HGRN), rev(D_MODEL),
                  pl.BlockSpec((2, D_HGRN), lambda i: (0, 0)),
                  pl.BlockSpec((8, D_CONV), lambda i: (0, 0)),
                  row(D_HGRN), row(D_CONV),
                  pl.BlockSpec((MXU_WIDTH, MXU_WIDTH), lambda i: (0, 0)),
                  pl.BlockSpec((MXU_WIDTH, MXU_WIDTH), lambda i: (0, 0))],
        out_specs=(rev(4096), pl.BlockSpec((8, D_MODEL), lambda i: (0, 0))),
        scratch_shapes=[pltpu.VMEM((N_HEADS, HEAD, HEAD), F32), pltpu.VMEM((8, D_CONV), F32),
                        pltpu.VMEM((TB, D_HGRN), F32), pltpu.VMEM((TB, D_HGRN), F32),
                        pltpu.VMEM((TB, D_HGRN), F32), pltpu.VMEM((TB, D_HGRN), F32),
                        pltpu.VMEM((TB, D_HGRN), F32), pltpu.VMEM((1, D_HGRN), F32)],
        compiler_params=pltpu.CompilerParams(dimension_semantics=("arbitrary",), vmem_limit_bytes=VMEM_LIMIT),
    )(proj, o, cv, states, sg, b, dmixed, lb_logits, cw, ga, gcn, g128, g64)


TX = 256
(SEM_D2D, SEM_D2D_O, SEM_ICI, SEM_ICI_O, SEM_FIN, SEM_FIN_O, SEM_SMALL, N_SEM_TAIL) = 0, 4, 5, 8, 11, 12, 12, 20


def _bwd_tail(kidx, h, dproj, wg, gwo, x2d, dx2, g1, small_a, small_b):
    hw = D_MODEL // 2
    ho = WO_ROWS // 2
    nt = SEQ // TP
    n_steps = N_SHARD + SEQ // TX // nt

    def body(k_ref, h_ref, dp_ref, w_ref, gwo_ref, x_ref, dx2_ref, g_ref, sm_ref, smb_ref,
             gx_ref, gw_out, gwo_out, osm_ref,
             acc, dh, sendbuf, keep, sibrcv, rcv, sib_o, p_o, rcv_o, res_o, sm_buf, dng,
             send_sems, recv_sems, out_sems):
        s, t = pl.program_id(0), pl.program_id(1)
        x, y, c = lax.axis_index("x"), lax.axis_index("y"), lax.axis_index("c")
        k = 2 * x + y
        me = 4 * x + 2 * y + c
        sibling = (x, y, 1 - c)
        chips = [(1 - x, 1 - y), (1 - x, y), (x, 1 - y)]
        kjs = [2 * cx + cy for cx, cy in chips]
        mine = pl.ds(pl.multiple_of(c * hw, hw), hw)
        other = pl.ds(pl.multiple_of((1 - c) * hw, hw), hw)
        mine_o = pl.ds(pl.multiple_of(c * ho, ho), ho)
        other_o = pl.ds(pl.multiple_of((1 - c) * ho, ho), ho)

        def copy(sem, src, dst, to):
            return pltpu.make_async_remote_copy(
                src_ref=src, dst_ref=dst, send_sem=send_sems.at[sem], recv_sem=recv_sems.at[sem],
                device_id=to, device_id_type=MESH)

        def at_step(sv, tv):
            return pl.when((s == sv) & (t == tv))

        d2d = [copy(SEM_D2D + sv, sendbuf.at[sv], sibrcv.at[sv], sibling) for sv in range(N_SHARD)]
        d2d_o = copy(SEM_D2D_O, gwo_ref.at[:, other_o, :], sib_o, sibling)
        ici = [copy(SEM_ICI + sv, keep.at[sv], rcv.at[sv], (*chips[sv], c)) for sv in range(3)]
        ici_o = [copy(SEM_ICI_O + sv, p_o.at[kjs[sv]], rcv_o.at[sv], (*chips[sv], c)) for sv in range(3)]
        fin = copy(SEM_FIN, acc.at[mine, :], acc.at[mine, :], sibling)
        fin_o = copy(SEM_FIN_O, res_o.at[mine_o, :], res_o.at[mine_o, :], sibling)
        smalls = [copy(SEM_SMALL + m, sm_buf.at[me], sm_buf.at[me],
                       (x ^ (m >> 2), y ^ ((m >> 1) & 1), c ^ (m & 1))) for m in range(1, N_DEV)]
        store_w = pltpu.make_async_copy(acc, gw_out, out_sems.at[0])
        store_o = pltpu.make_async_copy(res_o, gwo_out, out_sems.at[1])

        @at_step(0, 0)
        def _():
            d2d_o.start()

        @at_step(0, 1)
        def _():
            d2d_o.wait_recv()
            for j in range(N_SHARD):
                p_o[j] = (gwo_ref[j, mine_o, :].astype(F32) + sib_o[j].astype(F32)).astype(BF16)
            res_o[mine_o, :] = gwo_ref[k, mine_o, :].astype(F32) + sib_o[k].astype(F32)
            for cp in ici_o:
                cp.start()

        rows = pl.ds(pl.multiple_of(t * TP, TP), TP)

        @pl.when(s < N_SHARD)
        def _():
            dpb = dp_ref[...]
            part = _dot_tn(h_ref[...], dpb)

            @pl.when(t == 0)
            def _():
                acc[...] = part

            @pl.when(t > 0)
            def _():
                acc[...] += part

            d = _dot_nt(dpb, w_ref[0])

            @pl.when(s == 0)
            def _():
                dh[rows, :] = d

            @pl.when(s > 0)
            def _():
                dh[rows, :] += d

        for sv in range(N_SHARD):
            @at_step(sv, nt - 1)
            def _(sv=sv):
                sendbuf[sv] = acc[other, :].astype(BF16)
                if sv < 3:
                    keep[sv] = acc[mine, :].astype(BF16)
                d2d[sv].start()

        for sv in range(3):
            @at_step(sv + 1, 1)
            def _(sv=sv):
                d2d[sv].wait_recv()
                keep[sv] = (keep[sv].astype(F32) + sibrcv[sv].astype(F32)).astype(BF16)
                ici[sv].start()

        @at_step(N_SHARD, 0)
        def _():
            d2d[3].wait_recv()
            ici[0].wait_recv()
            acc[mine, :] += sibrcv[3].astype(F32) + rcv[0].astype(F32)

        @at_step(N_SHARD, 1)
        def _():
            tot = res_o[mine_o, :]
            for sv in range(3):
                ici_o[sv].wait_recv()
                tot = tot + rcv_o[sv].astype(F32)
            res_o[mine_o, :] = tot
            fin_o.start()

        @at_step(N_SHARD, 2)
        def _():
            ici[1].wait_recv()
            acc[mine, :] += rcv[1].astype(F32)

        @at_step(N_SHARD, 0)
        def _():
            dng[...] = jnp.zeros_like(dng)

        @pl.when(s >= N_SHARD)
        def _():
            blk = (s - N_SHARD) * nt + t
            dhv = dh[pl.ds(pl.multiple_of(blk * TX, TX), TX), :]
            xv = x_ref[...]
            r = lax.rsqrt(jnp.mean(xv * xv, axis=-1, keepdims=True) + EPS)
            xn = xv * r
            dng[...] += jnp.sum(dhv * xn, axis=0, keepdims=True)
            dxn = dhv * g_ref[...]
            gx_ref[...] = dx2_ref[...] + r * (dxn - xn * jnp.mean(dxn * xn, axis=-1, keepdims=True))

        @at_step(n_steps - 1, nt - 1)
        def _():
            sm_buf[me] = sm_ref[...] + smb_ref[...]
            sm_buf[me, 0:1, :] = dng[...]
            for cp in smalls:
                cp.start()
            ici[2].wait_recv()
            acc[mine, :] += rcv[2].astype(F32)
            fin.start()
            for m in range(1, N_DEV):
                copy(SEM_SMALL + m, sm_buf.at[0], sm_buf.at[0], sibling).wait_recv()
            tot = sm_buf[0]
            for d in range(1, N_DEV):
                tot = tot + sm_buf[d]
            osm_ref[...] = tot
            fin_o.wait_recv()
            store_o.start()
            fin.wait_recv()
            store_w.start()
            for cp in d2d + [d2d_o] + ici + ici_o + [fin, fin_o] + smalls:
                cp.wait_send()
            store_o.wait()
            store_w.wait()

    def shard_of(s, kr):
        return kr[0] ^ (3 - jnp.minimum(s, 3))

    def tok(s, t):
        return jnp.where(s < N_SHARD, t, nt - 1)

    def blk_map(s, t, kr):
        return (jnp.where(s < N_SHARD, 0, (s - N_SHARD) * nt + t), 0)

    hbm = pl.BlockSpec(memory_space=pl.ANY)
    grid_spec = pltpu.PrefetchScalarGridSpec(
        num_scalar_prefetch=1, grid=(n_steps, nt),
        in_specs=[pl.BlockSpec((TP, D_MODEL), lambda s, t, kr: (tok(s, t), 0)),
                  pl.BlockSpec((TP, SHARD_COLS), lambda s, t, kr: (tok(s, t), shard_of(s, kr))),
                  pl.BlockSpec((1, D_MODEL, SHARD_COLS), lambda s, t, kr: (shard_of(s, kr), 0, 0)),
                  pl.BlockSpec((N_SHARD, WO_ROWS, D_MODEL), lambda s, t, kr: (0, 0, 0)),
                  pl.BlockSpec((TX, D_MODEL), blk_map),
                  pl.BlockSpec((TX, D_MODEL), blk_map),
                  pl.BlockSpec((1, D_MODEL), lambda s, t, kr: (0, 0)),
                  pl.BlockSpec((8, D_MODEL), lambda s, t, kr: (0, 0)),
                  pl.BlockSpec((8, D_MODEL), lambda s, t, kr: (0, 0))],
        out_specs=(pl.BlockSpec((TX, D_MODEL), blk_map), hbm, hbm,
                   pl.BlockSpec((8, D_MODEL), lambda s, t, kr: (0, 0))),
        scratch_shapes=[pltpu.VMEM((D_MODEL, SHARD_COLS), F32), pltpu.VMEM((SEQ, D_MODEL), F32),
                        pltpu.VMEM((N_SHARD, hw, SHARD_COLS), BF16), pltpu.VMEM((3, hw, SHARD_COLS), BF16),
                        pltpu.VMEM((N_SHARD, hw, SHARD_COLS), BF16), pltpu.VMEM((3, hw, SHARD_COLS), BF16),
                        pltpu.VMEM((N_SHARD, ho, D_MODEL), BF16), pltpu.VMEM((N_SHARD, ho, D_MODEL), BF16),
                        pltpu.VMEM((3, ho, D_MODEL), BF16), pltpu.VMEM((WO_ROWS, D_MODEL), F32),
                        pltpu.VMEM((N_DEV, 8, D_MODEL), F32), pltpu.VMEM((1, D_MODEL), F32),
                        pltpu.SemaphoreType.DMA((N_SEM_TAIL,)), pltpu.SemaphoreType.DMA((N_SEM_TAIL,)),
                        pltpu.SemaphoreType.DMA((2,))])
    return pl.pallas_call(
        body, name="bwd_tail", grid_spec=grid_spec,
        out_shape=(jax.ShapeDtypeStruct((SEQ, D_MODEL), F32),
                   jax.ShapeDtypeStruct((D_MODEL, SHARD_COLS), F32),
                   jax.ShapeDtypeStruct((WO_ROWS, D_MODEL), F32),
                   jax.ShapeDtypeStruct((8, D_MODEL), F32)),
        compiler_params=pltpu.CompilerParams(dimension_semantics=("arbitrary", "arbitrary"),
                                             vmem_limit_bytes=60 * 1024 * 1024),
    )(kidx, h, dproj, wg, gwo, x2d, dx2, g1, small_a, small_b)


def _adam_update(w, g, m, v):
    nm = ADAM_B1 * m + (1.0 - ADAM_B1) * g
    nv = ADAM_B2 * v + (1.0 - ADAM_B2) * (g * g)
    m_hat = nm / (1.0 - ADAM_B1 ** ADAM_STEP)
    v_hat = nv / (1.0 - ADAM_B2 ** ADAM_STEP)
    return -ADAM_LR * (m_hat / (jnp.sqrt(v_hat) + ADAM_EPS) + ADAM_WD * w), nm, nv


def _adamw_small(tot, params):
    n = len(params)

    def body(tot_ref, *refs):
        ins, loss_ref, outs = refs[:3 * n], refs[3 * n], refs[3 * n + 1:]
        k = 2 * lax.axis_index("x") + lax.axis_index("y")
        mine = pl.ds(pl.multiple_of(k * HEAD, HEAD), HEAD)
        loss_ref[...] = tot_ref[7:8, 0:1]
        grads = [tot_ref[0:1, :], tot_ref[1:2, :], tot_ref[2:3, 0:D_HGRN], tot_ref[2:3, D_HGRN:],
                 jnp.concatenate([tot_ref[3:4, 0:D_HGRN], tot_ref[3:4, D_HGRN:]], axis=0),
                 jnp.concatenate([tot_ref[4 + tap:5 + tap, mine] for tap in range(3)], axis=1)]
        for i, g in enumerate(grads):
            w_ref, m_ref, v_ref = ins[3 * i:3 * i + 3]
            g_ref, d_ref, nm_ref, nv_ref = outs[4 * i:4 * i + 4]
            g_ref[...] = g
            d_ref[...], nm_ref[...], nv_ref[...] = _adam_update(w_ref[...], g, m_ref[...], v_ref[...])

    vm = pl.BlockSpec(memory_space=pltpu.VMEM)
    flat = [a for triple in params for a in triple]
    out_shape = (jax.ShapeDtypeStruct((1, 1), F32),) + tuple(
        jax.ShapeDtypeStruct(w.shape, F32) for w, _, _ in params for _ in range(4))
    outs = pl.pallas_call(
        body, name="adamw_small", out_shape=out_shape,
        in_specs=[vm] * (1 + 3 * n), out_specs=tuple([vm] * (1 + 4 * n)),
    )(tot, *flat)
    return [outs[0]] + [outs[1 + 4 * i:5 + 4 * i] for i in range(n)]


def _adamw(w, g, m, v, name):
    rows, cols = w.shape
    tr = rows if rows <= 256 else 256

    def body(w_ref, g_ref, m_ref, v_ref, d_ref, nm_ref, nv_ref):
        d_ref[...], nm_ref[...], nv_ref[...] = _adam_update(w_ref[...], g_ref[...], m_ref[...], v_ref[...])

    blk = lambda: pl.BlockSpec((tr, cols), lambda i: (i, 0))
    shp = jax.ShapeDtypeStruct((rows, cols), F32)
    return pl.pallas_call(
        body, name=name, grid=(rows // tr,),
        out_shape=(shp, shp, shp),
        in_specs=[blk(), blk(), blk(), blk()], out_specs=(blk(), blk(), blk()),
        compiler_params=pltpu.CompilerParams(dimension_semantics=("arbitrary",)),
    )(w, g, m, v)


def _local_step(x2d, tgt, proj, lb_logits, cw, ga, gcn, w_out, gf):
    g128 = _group_matrix(MXU_WIDTH, HEAD)
    g64 = _group_matrix(MXU_WIDTH, CONV_GROUP)
    mixed, o, cv, states, sg, b, wog = _mix_fwd(proj, lb_logits, cw, ga, gcn, g128, g64, w_out)
    dx2, dmixed, gwo, part_out = _out_loss(x2d, mixed, wog.reshape(D_MODEL, D_MODEL), gf, tgt)
    dproj, part_mix = _mix_bwd(proj, o, cv, states, sg, b, dmixed, lb_logits, cw, ga, gcn, g128, g64)
    return dproj, dx2, gwo.reshape(N_SHARD, WO_ROWS, D_MODEL), part_out, part_mix


def kernel(x, norm_gain, w_in, lb_logits, conv_w, hgrn_norm_gain, conv_norm_gain, w_out, final_norm_gain, loss_target, m_norm_gain, m_w_in, m_lb_logits, m_conv_w, m_hgrn_norm_gain, m_conv_norm_gain, m_w_out, m_final_norm_gain, v_norm_gain, v_w_in, v_lb_logits, v_conv_w, v_hgrn_norm_gain, v_conv_norm_gain, v_w_out, v_final_norm_gain):
    k = 2 * lax.axis_index("x") + lax.axis_index("y")
    kidx = jnp.reshape(k, (1,)).astype(jnp.int32)
    row = lambda a: a.reshape(1, D_MODEL)
    taps = lambda a: a.reshape(1, 3 * HEAD)
    h, proj, wg, cw = _gather_proj(kidx, x[0], norm_gain, w_in, taps(conv_w))
    dproj, dx2, gwo, part_out, part_mix = _local_step(
        x[0], loss_target[0], proj, lb_logits, cw, hgrn_norm_gain, conv_norm_gain, w_out, row(final_norm_gain))
    grad_x, g_w_in, g_w_out, tot = _bwd_tail(kidx, h, dproj, wg, gwo, x[0], dx2, norm_gain, part_out, part_mix)

    d_w_in, nm_w_in, nv_w_in = _adamw(w_in[0], g_w_in, m_w_in[0], v_w_in[0], "adamw_w_in")
    d_w_out, nm_w_out, nv_w_out = _adamw(w_out[0], g_w_out, m_w_out[0], v_w_out[0], "adamw_w_out")
    (loss, (g_norm_gain, d_ng, nm_ng, nv_ng), (g_final, d_fg, nm_fg, nv_fg), (g_hgrn, d_hg, nm_hg, nv_hg),
     (g_convn, d_cg, nm_cg, nv_cg), (g_lb, d_lb, nm_lb, nv_lb), (g_conv_w, d_cw, nm_cw, nv_cw)) = _adamw_small(
        tot, [(norm_gain, m_norm_gain, v_norm_gain),
              (row(final_norm_gain), row(m_final_norm_gain), row(v_final_norm_gain)),
              (hgrn_norm_gain, m_hgrn_norm_gain, v_hgrn_norm_gain),
              (conv_norm_gain, m_conv_norm_gain, v_conv_norm_gain),
              (lb_logits, m_lb_logits, v_lb_logits),
              (taps(conv_w), taps(m_conv_w), taps(v_conv_w))])
    flat = lambda a: a.reshape(D_MODEL)
    untap = lambda a: a.reshape(1, 3, HEAD)
    return (loss.reshape(()), grad_x[None],
            g_norm_gain, g_w_in[None], g_lb, untap(g_conv_w), g_hgrn, g_convn, g_w_out[None], flat(g_final),
            d_ng, d_w_in[None], d_lb, untap(d_cw), d_hg, d_cg, d_w_out[None], flat(d_fg),
            nm_ng, nm_w_in[None], nm_lb, untap(nm_cw), nm_hg, nm_cg, nm_w_out[None], flat(nm_fg),
            nv_ng, nv_w_in[None], nv_lb, untap(nv_cw), nv_hg, nv_cg, nv_w_out[None], flat(nv_fg))
```

```python
import functools

import jax
import jax.numpy as jnp
import numpy as np
from jax import lax
from jax.experimental import pallas as pl
from jax.experimental.pallas import tpu as pltpu

F32 = jnp.float32
BF16 = jnp.bfloat16
MESH = pl.DeviceIdType.MESH

SEQ = 2048
D_MODEL = 1024
D_HGRN = 512
D_CONV = 512
HEAD = 128
N_HEADS = 4
CHUNK = 64
CONV_GROUP = 64
N_SHARD = 4
SHARD_COLS = 1024
WO_ROWS = 256
EPS = 1e-6
TB = 256
NCB = TB // CHUNK
N_CHUNKS = SEQ // CHUNK
N_DEV = 8
MXU_WIDTH = 256

ADAM_LR = 0.001
ADAM_B1 = 0.9
ADAM_B2 = 0.999
ADAM_EPS = 1e-08
ADAM_WD = 0.01
ADAM_STEP = 10

VMEM_LIMIT = 56 * 1024 * 1024


def _dot(a, b):
    return jnp.dot(a, b, preferred_element_type=F32)


def _dot_nt(a, b):
    return lax.dot_general(a, b, (((1,), (1,)), ((), ())), preferred_element_type=F32)


def _dot_tn(a, b):
    return lax.dot_general(a, b, (((0,), (0,)), ((), ())), preferred_element_type=F32)


def _split_bf16(x, n):
    parts = []
    r = x
    for _ in range(n):
        p = r.astype(BF16)
        parts.append(p)
        r = r - p.astype(F32)
    return parts


def _exact_left(m, x, n=3):
    acc = None
    for p in _split_bf16(x, n):
        t = _dot(m, p)
        acc = t if acc is None else acc + t
    return acc


def _group_mean(x, gmat, n=2):
    w = gmat.shape[0]
    outs = []
    for c0 in range(0, x.shape[1], w):
        acc = None
        for p in _split_bf16(x[:, c0:c0 + w], n):
            t = _dot(p, gmat)
            acc = t if acc is None else acc + t
        outs.append(acc)
    return jnp.concatenate(outs, axis=1)


def _sigmoid(x):
    return 1.0 / (1.0 + jnp.exp(-x))


def _lower_bound(lbl):
    l0 = lbl[0:1, :]
    l1 = lbl[1:2, :]
    m = jnp.maximum(l0, l1)
    e0 = jnp.exp(l0 - m)
    e1 = jnp.exp(l1 - m)
    return e0 / (e0 + e1)


def _chunk_tri(lower):
    r = lax.broadcasted_iota(jnp.int32, (TB, TB), 0)
    c = lax.broadcasted_iota(jnp.int32, (TB, TB), 1)
    same = (r // CHUNK) == (c // CHUNK)
    tri = (c <= r) if lower else (c >= r)
    return jnp.where(same & tri, 1.0, 0.0).astype(BF16)


def _causal():
    r = lax.broadcasted_iota(jnp.int32, (CHUNK, CHUNK), 0)
    c = lax.broadcasted_iota(jnp.int32, (CHUNK, CHUNK), 1)
    return c <= r


def _shift_down(x, sh, prev_tail):
    r = pltpu.roll(x, sh, 0)
    pt = pltpu.roll(prev_tail, sh, 0)
    rows = lax.broadcasted_iota(jnp.int32, prev_tail.shape, 0)
    top = jnp.where(rows < sh, pt, r[0:8])
    return jnp.concatenate([top, r[8:]], axis=0)


def _shift_up(x, sh, next_head):
    n = x.shape[0]
    r = pltpu.roll(x, n - sh, 0)
    nh = pltpu.roll(next_head, 8 - sh, 0)
    rows = lax.broadcasted_iota(jnp.int32, next_head.shape, 0)
    bot = jnp.where(rows >= 8 - sh, nh, r[n - 8:])
    return jnp.concatenate([r[:n - 8], bot], axis=0)


def _group_matrix(width, group):
    r = np.arange(width)[:, None] // group
    c = np.arange(width)[None, :] // group
    return jnp.asarray(np.where(r == c, 1.0 / group, 0.0), dtype=BF16)


TP = 512
SEM_W, SEM_CW, SEM_W_FWD, N_SEM = 0, 3, 6, 9


def _gather_proj(kidx, x2d, g1, w_in, conv_w):
    half_w = D_MODEL // 2
    nt = SEQ // TP

    def body(k_ref, x_ref, g_ref, w_ref, cw_ref, h_ref, p_ref, wg_out, cwg_out,
             wg_v, cwg_v, send_sems, recv_sems, out_sems):
        s, t = pl.program_id(0), pl.program_id(1)
        x, y, c = lax.axis_index("x"), lax.axis_index("y"), lax.axis_index("c")
        k = 2 * x + y
        sibling = (x, y, 1 - c)
        chips = [(1 - x, y), (x, 1 - y), (1 - x, 1 - y)]
        kjs = [2 * cx + cy for cx, cy in chips]

        def w_half(kk, cc):
            return wg_v.at[kk, pl.ds(cc * half_w, half_w), :]

        def cw_of(kk):
            return cwg_v.at[:, pl.ds(pl.multiple_of(kk * HEAD, HEAD), HEAD)]

        def copy(sem, ref, to):
            return pltpu.make_async_remote_copy(
                src_ref=ref, dst_ref=ref, send_sem=send_sems.at[sem], recv_sem=recv_sems.at[sem],
                device_id=to, device_id_type=MESH)

        def at_step(sv, tv):
            return pl.when((s == sv) & (t == tv))

        w_direct = [copy(SEM_W + j, w_half(k, c), (*chip, c)) for j, chip in enumerate(chips)]
        cw_direct = [copy(SEM_CW + j, cw_of(k), (*chip, c)) for j, chip in enumerate(chips)]
        w_passed = [copy(SEM_W_FWD + j, w_half(kj, c), sibling) for j, kj in enumerate(kjs)]
        stores = ([pltpu.make_async_copy(wg_v.at[kk], wg_out.at[kk], out_sems.at[i])
                   for i, kk in enumerate([k] + kjs)]
                  + [pltpu.make_async_copy(cwg_v, cwg_out, out_sems.at[4])])

        @at_step(0, 0)
        def _():
            wg_v[k] = w_ref[0].astype(BF16)
            mine = pl.ds(pl.multiple_of(k * HEAD, HEAD), HEAD)
            cwg_v[:, mine] = jnp.zeros((8, HEAD), F32)
            for tap in range(3):
                cwg_v[tap:tap + 1, mine] = cw_ref[:, tap * HEAD:(tap + 1) * HEAD]
            w_direct[0].start()
            w_direct[1].start()
            for cp in cw_direct:
                cp.start()
            stores[0].start()

        @at_step(1, 0)
        def _():
            for j in range(2):
                copy(SEM_W + j, w_half(kjs[j], c), sibling).wait_recv()
                w_passed[j].start()
            w_direct[2].start()
            copy(SEM_W_FWD, w_half(kjs[0], 1 - c), sibling).wait_recv()
            stores[1].start()

        @at_step(2, 0)
        def _():
            copy(SEM_W_FWD + 1, w_half(kjs[1], 1 - c), sibling).wait_recv()
            stores[2].start()

        @at_step(3, 0)
        def _():
            copy(SEM_W + 2, w_half(kjs[2], c), sibling).wait_recv()
            w_passed[2].start()
            copy(SEM_W_FWD + 2, w_half(kjs[2], 1 - c), sibling).wait_recv()
            stores[3].start()

        rows = pl.ds(pl.multiple_of(t * TP, TP), TP)

        @pl.when(s == 0)
        def _():
            xv = x_ref[...]
            r = lax.rsqrt(jnp.mean(xv * xv, axis=-1, keepdims=True) + EPS)
            h_ref[rows, :] = (xv * r * g_ref[...]).astype(BF16)

        js = k ^ (((s & 1) << 1) | (s >> 1))
        p_ref[...] = _dot(h_ref[rows, :], wg_v[js])

        @at_step(N_SHARD - 1, nt - 1)
        def _():
            for j in range(3):
                copy(SEM_CW + j, cw_of(kjs[j]), sibling).wait_recv()
            stores[4].start()
            for cp in w_direct + cw_direct + w_passed:
                cp.wait_send()
            for st in stores:
                st.wait()

    def x_map(s, t, kr):
        return (jnp.where(s == 0, t, nt - 1), 0)

    def p_map(s, t, kr):
        return (t, kr[0] ^ (((s & 1) << 1) | (s >> 1)))

    hbm = pl.BlockSpec(memory_space=pl.ANY)
    grid_spec = pltpu.PrefetchScalarGridSpec(
        num_scalar_prefetch=1, grid=(N_SHARD, nt),
        in_specs=[pl.BlockSpec((TP, D_MODEL), x_map),
                  pl.BlockSpec((1, D_MODEL), lambda s, t, kr: (0, 0)),
                  pl.BlockSpec((1, D_MODEL, SHARD_COLS), lambda s, t, kr: (0, 0, 0)),
                  pl.BlockSpec((1, 3 * HEAD), lambda s, t, kr: (0, 0))],
        out_specs=(pl.BlockSpec((SEQ, D_MODEL), lambda s, t, kr: (0, 0)),
                   pl.BlockSpec((TP, SHARD_COLS), p_map), hbm, hbm),
        scratch_shapes=[pltpu.VMEM((N_SHARD, D_MODEL, SHARD_COLS), BF16),
                        pltpu.VMEM((8, D_CONV), F32),
                        pltpu.SemaphoreType.DMA((N_SEM,)), pltpu.SemaphoreType.DMA((N_SEM,)),
                        pltpu.SemaphoreType.DMA((5,))])
    return pl.pallas_call(
        body, name="gather_proj", grid_spec=grid_spec,
        out_shape=(jax.ShapeDtypeStruct((SEQ, D_MODEL), BF16),
                   jax.ShapeDtypeStruct((SEQ, N_SHARD * SHARD_COLS), F32),
                   jax.ShapeDtypeStruct((N_SHARD, D_MODEL, SHARD_COLS), BF16),
                   jax.ShapeDtypeStruct((8, D_CONV), F32)),
        compiler_params=pltpu.CompilerParams(dimension_semantics=("arbitrary", "arbitrary"),
                                             vmem_limit_bytes=VMEM_LIMIT),
    )(kidx, x2d, g1, w_in, conv_w)


def _mix_fwd(proj, lb_logits, cw, ga, gcn, g128, g64, w_out):
    half_o = WO_ROWS // 2
    nblk = SEQ // TB

    def body(p_ref, lbl_ref, cw_ref, ga_ref, gcn_ref, g128_ref, g64_ref, wo_ref,
             mixed_ref, o_ref, cv_ref, sto_ref, sg_ref, b_ref, wog_out,
             st_ref, tail_ref, f_ref, wog_v, send_sems, recv_sems, out_sem):
        i = pl.program_id(0)
        x, y, c = lax.axis_index("x"), lax.axis_index("y"), lax.axis_index("c")
        k = 2 * x + y
        sibling = (x, y, 1 - c)
        chips = [(1 - x, y), (x, 1 - y), (1 - x, 1 - y)]
        kjs = [2 * cx + cy for cx, cy in chips]

        def wo_half(kk, cc):
            return wog_v.at[kk, pl.ds(cc * half_o, half_o), :]

        def copy(sem, ref, to):
            return pltpu.make_async_remote_copy(
                src_ref=ref, dst_ref=ref, send_sem=send_sems.at[sem], recv_sem=recv_sems.at[sem],
                device_id=to, device_id_type=MESH)

        wo_direct = [copy(j, wo_half(k, c), (*chip, c)) for j, chip in enumerate(chips)]
        wo_passed = [copy(3 + j, wo_half(kj, c), sibling) for j, kj in enumerate(kjs)]
        wo_store = pltpu.make_async_copy(wog_v, wog_out, out_sem.at[0])

        @pl.when(i == 0)
        def _():
            st_ref[...] = jnp.zeros_like(st_ref)
            tail_ref[...] = jnp.zeros_like(tail_ref)
            wog_v[k] = wo_ref[0].astype(BF16)
            for cp in wo_direct:
                cp.start()

        @pl.when(i == nblk - 2)
        def _():
            for j in range(3):
                copy(j, wo_half(kjs[j], c), sibling).wait_recv()
                wo_passed[j].start()

        lb = _lower_bound(lbl_ref[...])
        sg = _sigmoid(p_ref[:, 512:1024])
        sg_ref[...] = sg
        f = lb + (1.0 - lb) * sg
        f_ref[...] = f
        b_ref[...] = _exact_left(_chunk_tri(True), jnp.log(f))
        causal = _causal()
        for n in range(NCB):
            sl = pl.ds(n * CHUNK, CHUNK)
            bc = b_ref[sl, :]
            g = b_ref[n * CHUNK + CHUNK - 1:n * CHUNK + CHUNK, :]
            kk = 1.0 - f_ref[sl, :]
            qd = (p_ref[sl, 0:512] * jnp.exp(bc)).astype(BF16)
            ki = (kk * jnp.exp(-bc)).astype(BF16)
            ke = (kk * jnp.exp(g - bc)).astype(BF16)
            vb = p_ref[sl, 1024:1536].astype(BF16)
            dec = jnp.exp(g)
            for hd in range(N_HEADS):
                cs = slice(hd * HEAD, (hd + 1) * HEAD)
                st = st_ref[hd]
                sto_ref[n, hd] = st
                sc = jnp.where(causal, _dot_nt(qd[:, cs], ki[:, cs]), 0.0)
                o_ref[sl, cs] = _dot(sc.astype(BF16), vb[:, cs]) + _dot_nt(qd[:, cs], st.astype(BF16))
                st_ref[hd] = st * dec[:, cs] + _dot_tn(vb[:, cs], ke[:, cs])

        o = o_ref[...]
        ra = lax.rsqrt(_group_mean(o * o, g128_ref[...]) + EPS)
        za = p_ref[:, 1536:2048]
        mixed_ref[:, 0:512] = (o * ra * ga_ref[...] * (za * _sigmoid(za))).astype(BF16)

        cu = p_ref[:, 3072:3584] * p_ref[:, 2048:2560]
        tail = tail_ref[...]
        cv = (cw_ref[0:1, :] * _shift_down(cu, 2, tail) + cw_ref[1:2, :] * _shift_down(cu, 1, tail)
              + cw_ref[2:3, :] * cu)
        tail_ref[...] = cu[TB - 8:, :]
        cv_ref[...] = cv
        yb = p_ref[:, 2560:3072] * cv
        rb = lax.rsqrt(_group_mean(yb * yb, g64_ref[...]) + EPS)
        zb = p_ref[:, 3584:4096]
        mixed_ref[:, 512:1024] = (yb * rb * gcn_ref[...] * (zb * _sigmoid(zb))).astype(BF16)

        @pl.when(i == nblk - 1)
        def _():
            for j in range(3):
                copy(3 + j, wo_half(kjs[j], 1 - c), sibling).wait_recv()
            wo_store.start()
            for cp in wo_direct + wo_passed:
                cp.wait_send()
            wo_store.wait()

    row = lambda w: pl.BlockSpec((1, w), lambda i: (0, 0))
    return pl.pallas_call(
        body, name="mix_fwd", grid=(nblk,),
        out_shape=(jax.ShapeDtypeStruct((SEQ, D_MODEL), BF16),
                   jax.ShapeDtypeStruct((SEQ, D_HGRN), F32),
                   jax.ShapeDtypeStruct((SEQ, D_CONV), F32),
                   jax.ShapeDtypeStruct((N_CHUNKS, N_HEADS, HEAD, HEAD), F32),
                   jax.ShapeDtypeStruct((SEQ, D_HGRN), F32),
                   jax.ShapeDtypeStruct((SEQ, D_HGRN), F32),
                   jax.ShapeDtypeStruct((N_SHARD, WO_ROWS, D_MODEL), BF16)),
        in_specs=[pl.BlockSpec((TB, 4096), lambda i: (i, 0)),
                  pl.BlockSpec((2, D_HGRN), lambda i: (0, 0)),
                  pl.BlockSpec((8, D_CONV), lambda i: (0, 0)),
                  row(D_HGRN), row(D_CONV),
                  pl.BlockSpec((MXU_WIDTH, MXU_WIDTH), lambda i: (0, 0)),
                  pl.BlockSpec((MXU_WIDTH, MXU_WIDTH), lambda i: (0, 0)),
                  pl.BlockSpec((1, WO_ROWS, D_MODEL), lambda i: (0, 0, 0))],
        out_specs=(pl.BlockSpec((TB, D_MODEL), lambda i: (i, 0)),
                   pl.BlockSpec((TB, D_HGRN), lambda i: (i, 0)),
                   pl.BlockSpec((TB, D_CONV), lambda i: (i, 0)),
                   pl.BlockSpec((NCB, N_HEADS, HEAD, HEAD), lambda i: (i, 0, 0, 0)),
                   pl.BlockSpec((TB, D_HGRN), lambda i: (i, 0)),
                   pl.BlockSpec((TB, D_HGRN), lambda i: (i, 0)),
                   pl.BlockSpec(memory_space=pl.ANY)),
        scratch_shapes=[pltpu.VMEM((N_HEADS, HEAD, HEAD), F32), pltpu.VMEM((8, D_CONV), F32),
                        pltpu.VMEM((TB, D_HGRN), F32),
                        pltpu.VMEM((N_SHARD, WO_ROWS, D_MODEL), BF16),
                        pltpu.SemaphoreType.DMA((6,)), pltpu.SemaphoreType.DMA((6,)),
                        pltpu.SemaphoreType.DMA((1,))],
        compiler_params=pltpu.CompilerParams(dimension_semantics=("arbitrary",), vmem_limit_bytes=VMEM_LIMIT),
    )(proj, lb_logits, cw, ga, gcn, g128, g64, w_out)


def _out_loss(x2d, mixed, wog, gf, tgt):
    def body(x_ref, m_ref, wo_ref, gf_ref, t_ref, dx2_ref, dm_ref, gwo_ref, part_ref, acc_ref):
        i = pl.program_id(0)

        @pl.when(i == 0)
        def _():
            acc_ref[...] = jnp.zeros_like(acc_ref)
            part_ref[...] = jnp.zeros_like(part_ref)

        mixed_b = m_ref[...]
        x2 = x_ref[...] + _dot(mixed_b, wo_ref[...])
        r2 = lax.rsqrt(jnp.mean(x2 * x2, axis=-1, keepdims=True) + EPS)
        n2 = x2 * r2
        gfv = gf_ref[...]
        err = n2 * gfv - t_ref[...]
        loss = 0.5 * jnp.sum(jnp.mean(err * err, axis=-1, keepdims=True), axis=0, keepdims=True)
        dy = err * (1.0 / D_MODEL)
        part_ref[1:2, :] += jnp.sum(dy * n2, axis=0, keepdims=True)
        part_ref[7:8, :] += jnp.broadcast_to(loss, (1, D_MODEL))
        dn = dy * gfv
        dx2 = r2 * (dn - n2 * jnp.mean(dn * n2, axis=-1, keepdims=True))
        dx2_ref[...] = dx2
        dx2_b = dx2.astype(BF16)
        dm_ref[...] = _dot_nt(dx2_b, wo_ref[...])
        acc_ref[...] += _dot_tn(mixed_b, dx2_b)

        @pl.when(i == pl.num_programs(0) - 1)
        def _():
            gwo_ref[...] = acc_ref[...].astype(BF16)

    blk = lambda: pl.BlockSpec((TP, D_MODEL), lambda i: (i, 0))
    return pl.pallas_call(
        body, name="out_loss", grid=(SEQ // TP,),
        out_shape=(jax.ShapeDtypeStruct((SEQ, D_MODEL), F32),
                   jax.ShapeDtypeStruct((SEQ, D_MODEL), F32),
                   jax.ShapeDtypeStruct((D_MODEL, D_MODEL), BF16),
                   jax.ShapeDtypeStruct((8, D_MODEL), F32)),
        in_specs=[blk(), blk(), pl.BlockSpec((D_MODEL, D_MODEL), lambda i: (0, 0)),
                  pl.BlockSpec((1, D_MODEL), lambda i: (0, 0)), blk()],
        out_specs=(blk(), blk(), pl.BlockSpec((D_MODEL, D_MODEL), lambda i: (0, 0)),
                   pl.BlockSpec((8, D_MODEL), lambda i: (0, 0))),
        scratch_shapes=[pltpu.VMEM((D_MODEL, D_MODEL), F32)],
        compiler_params=pltpu.CompilerParams(dimension_semantics=("arbitrary",), vmem_limit_bytes=VMEM_LIMIT),
    )(x2d, mixed, wog, gf, tgt)


def _mix_bwd(proj, o, cv, states, sg, b, dmixed, lb_logits, cw, ga, gcn, g128, g64):
    nblk = SEQ // TB

    def body(p_ref, o_ref, cv_ref, st_ref, sg_ref, b_ref, dm_ref, lbl_ref, cw_ref, ga_ref, gcn_ref,
             g128_ref, g64_ref,
             dp_ref, part_ref, dst_ref, head_ref, f_ref, do_ref, db_ref, dg_ref, dk_ref, dlb_ref):
        i = pl.program_id(0)

        @pl.when(i == 0)
        def _():
            dst_ref[...] = jnp.zeros_like(dst_ref)
            head_ref[...] = jnp.zeros_like(head_ref)
            part_ref[...] = jnp.zeros_like(part_ref)
            dlb_ref[...] = jnp.zeros_like(dlb_ref)

        ov = o_ref[...]
        ra = lax.rsqrt(_group_mean(ov * ov, g128_ref[...]) + EPS)
        na = ov * ra
        za = p_ref[:, 1536:2048]
        sg = _sigmoid(za)
        dma = dm_ref[:, 0:512]
        gav = ga_ref[...]
        part_ref[2:3, 0:D_HGRN] += jnp.sum(dma * na * (za * sg), axis=0, keepdims=True)
        dp_ref[:, 1536:2048] = (dma * na * gav * (sg * (1.0 + za * (1.0 - sg)))).astype(BF16)
        dna = dma * gav * (za * sg)
        do_ref[...] = ra * (dna - na * _group_mean(dna * na, g128_ref[...]))

        cvv = cv_ref[...]
        gb = p_ref[:, 2560:3072]
        yb = gb * cvv
        rb = lax.rsqrt(_group_mean(yb * yb, g64_ref[...]) + EPS)
        nb = yb * rb
        zb = p_ref[:, 3584:4096]
        sgb = _sigmoid(zb)
        dmb = dm_ref[:, 512:1024]
        gcv = gcn_ref[...]
        part_ref[2:3, D_HGRN:] += jnp.sum(dmb * nb * (zb * sgb), axis=0, keepdims=True)
        dp_ref[:, 3584:4096] = (dmb * nb * gcv * (sgb * (1.0 + zb * (1.0 - sgb)))).astype(BF16)
        dnb = dmb * gcv * (zb * sgb)
        dyb = rb * (dnb - nb * _group_mean(dnb * nb, g64_ref[...]))
        dp_ref[:, 2560:3072] = (dyb * cvv).astype(BF16)
        dcv = dyb * gb
        head = head_ref[...]
        dcv1 = _shift_up(dcv, 1, head)
        dcv2 = _shift_up(dcv, 2, head)
        head_ref[...] = dcv[0:8, :]
        u = p_ref[:, 2048:2560]
        gc = p_ref[:, 3072:3584]
        cu = gc * u
        part_ref[4:5, 0:D_CONV] += jnp.sum(dcv2 * cu, axis=0, keepdims=True)
        part_ref[5:6, 0:D_CONV] += jnp.sum(dcv1 * cu, axis=0, keepdims=True)
        part_ref[6:7, 0:D_CONV] += jnp.sum(dcv * cu, axis=0, keepdims=True)
        dcu = cw_ref[2:3, :] * dcv + cw_ref[1:2, :] * dcv1 + cw_ref[0:1, :] * dcv2
        dp_ref[:, 3072:3584] = (dcu * u).astype(BF16)
        dp_ref[:, 2048:2560] = (dcu * gc).astype(BF16)

        lb = _lower_bound(lbl_ref[...])
        s = sg_ref[...]
        f = lb + (1.0 - lb) * s
        f_ref[...] = f
        causal = _causal()
        for n in reversed(range(NCB)):
            sl = pl.ds(n * CHUNK, CHUNK)
            bc = b_ref[sl, :]
            g = b_ref[n * CHUNK + CHUNK - 1:n * CHUNK + CHUNK, :]
            kk = 1.0 - f_ref[sl, :]
            eb = jnp.exp(bc)
            enb = jnp.exp(-bc)
            eg = jnp.exp(g - bc)
            dec = jnp.exp(g)
            qd = p_ref[sl, 0:512] * eb
            ki = kk * enb
            ke = kk * eg
            qd_b = qd.astype(BF16)
            ki_b = ki.astype(BF16)
            ke_b = ke.astype(BF16)
            vb = p_ref[sl, 1024:1536].astype(BF16)
            do_b = do_ref[sl, :].astype(BF16)
            for hd in range(N_HEADS):
                cs = slice(hd * HEAD, (hd + 1) * HEAD)
                st = st_ref[n, hd]
                dst = dst_ref[hd]
                st_b = st.astype(BF16)
                dst_b = dst.astype(BF16)
                sc = jnp.where(causal, _dot_nt(qd_b[:, cs], ki_b[:, cs]), 0.0).astype(BF16)
                am = jnp.where(causal, _dot_nt(do_b[:, cs], vb[:, cs]), 0.0).astype(BF16)
                dqd = _dot(am, ki_b[:, cs]) + _dot(do_b[:, cs], st_b)
                dki = _dot_tn(am, qd_b[:, cs])
                dke = _dot(vb[:, cs], dst_b)
                dv = _dot_tn(sc, do_b[:, cs]) + _dot_nt(ke_b[:, cs], dst_b)
                ddec = jnp.sum(dst * st, axis=0, keepdims=True)
                dst_ref[hd] = dst * dec[:, cs] + _dot_tn(do_b[:, cs], qd_b[:, cs])
                dp_ref[sl, cs] = (dqd * eb[:, cs]).astype(BF16)
                dp_ref[sl, 1024 + hd * HEAD:1024 + (hd + 1) * HEAD] = dv.astype(BF16)
                dk_ref[sl, cs] = dki * enb[:, cs] + dke * eg[:, cs]
                db_ref[sl, cs] = dqd * qd[:, cs] - dki * ki[:, cs] - dke * ke[:, cs]
                dgv = jnp.sum(dke * ke[:, cs], axis=0, keepdims=True) + ddec * dec[:, cs]
                dg_ref[sl, cs] = jnp.broadcast_to(dgv, (CHUNK, HEAD))

        dlogf = _exact_left(_chunk_tri(False), db_ref[...], 2) + dg_ref[...]
        df = dlogf / f - dk_ref[...]
        dlb_ref[...] += jnp.sum(df * (1.0 - s), axis=0, keepdims=True)
        dp_ref[:, 512:1024] = (df * (1.0 - lb) * s * (1.0 - s)).astype(BF16)

        @pl.when(i == nblk - 1)
        def _():
            row = dlb_ref[...] * lb * (1.0 - lb)
            part_ref[3:4, 0:D_HGRN] = row
            part_ref[3:4, D_HGRN:] = -row

    rev = lambda w: pl.BlockSpec((TB, w), lambda i: (nblk - 1 - i, 0))
    row = lambda w: pl.BlockSpec((1, w), lambda i: (0, 0))
    return pl.pallas_call(
        body, name="mix_bwd", grid=(nblk,),
        out_shape=(jax.ShapeDtypeStruct((SEQ, 4096), BF16),
                   jax.ShapeDtypeStruct((8, D_MODEL), F32)),
        in_specs=[rev(4096), rev(D_HGRN), rev(D_CONV),
                  pl.BlockSpec((NCB, N_HEADS, HEAD, HEAD), lambda i: (nblk - 1 - i, 0, 0, 0)),
                  rev(D_HGRN), rev(D_HGRN), rev(D_MODEL),
                  pl.BlockSpec((2, D_HGRN), lambda i: (0, 0)),
                  pl.BlockSpec((8, D_CONV), lambda i: (0, 0)),
                  row(D_HGRN), row(D_CONV),
                  pl.BlockSpec((MXU_WIDTH, MXU_WIDTH), lambda i: (0, 0)),
                  pl.BlockSpec((MXU_WIDTH, MXU_WIDTH), lambda i: (0, 0))],
        out_specs=(rev(4096), pl.BlockSpec((8, D_MODEL), lambda i: (0, 0))),
        scratch_shapes=[pltpu.VMEM((N_HEADS, HEAD, HEAD), F32), pltpu.VMEM((8, D_CONV), F32),
                        pltpu.VMEM((TB, D_HGRN), F32), pltpu.VMEM((TB, D_HGRN), F32),
                        pltpu.VMEM((TB, D_HGRN), F32), pltpu.VMEM((TB, D_HGRN), F32),
                        pltpu.VMEM((TB, D_HGRN), F32), pltpu.VMEM((1, D_HGRN), F32)],
        compiler_params=pltpu.CompilerParams(dimension_semantics=("arbitrary",), vmem_limit_bytes=VMEM_LIMIT),
    )(proj, o, cv, states, sg, b, dmixed, lb_logits, cw, ga, gcn, g128, g64)


TT = 1024
TX = 256
(SEM_D2D, SEM_D2D_O, SEM_ICI, SEM_ICI_O, SEM_FIN, SEM_FIN_O, SEM_SMALL, N_SEM_TAIL) = 0, 4, 5, 8, 11, 12, 12, 20


def _bwd_tail(kidx, h, dproj, wg, gwo, x2d, dx2, g1, small_a, small_b):
    hw = D_MODEL // 2
    ho = WO_ROWS // 2
    nt = SEQ // TT
    n_steps = N_SHARD + SEQ // TX // nt

    def body(k_ref, h_ref, dp_ref, w_ref, gwo_ref, x_ref, dx2_ref, g_ref, sm_ref, smb_ref,
             gx_ref, gw_out, gwo_out, osm_ref,
             acc, dh, sendbuf, keep, sibrcv, rcv, sib_o, p_o, rcv_o, res_o, sm_buf, dng,
             send_sems, recv_sems, out_sems):
        s, t = pl.program_id(0), pl.program_id(1)
        x, y, c = lax.axis_index("x"), lax.axis_index("y"), lax.axis_index("c")
        k = 2 * x + y
        me = 4 * x + 2 * y + c
        sibling = (x, y, 1 - c)
        chips = [(1 - x, 1 - y), (1 - x, y), (x, 1 - y)]
        kjs = [2 * cx + cy for cx, cy in chips]
        mine = pl.ds(pl.multiple_of(c * hw, hw), hw)
        other = pl.ds(pl.multiple_of((1 - c) * hw, hw), hw)
        mine_o = pl.ds(pl.multiple_of(c * ho, ho), ho)
        other_o = pl.ds(pl.multiple_of((1 - c) * ho, ho), ho)

        def copy(sem, src, dst, to):
            return pltpu.make_async_remote_copy(
                src_ref=src, dst_ref=dst, send_sem=send_sems.at[sem], recv_sem=recv_sems.at[sem],
                device_id=to, device_id_type=MESH)

        def at_step(sv, tv):
            return pl.when((s == sv) & (t == tv))

        def at_norm_block(b):
            return at_step(N_SHARD + b // nt, b % nt)

        d2d = [copy(SEM_D2D + sv, sendbuf.at[sv], sibrcv.at[sv], sibling) for sv in range(N_SHARD)]
        d2d_o = copy(SEM_D2D_O, gwo_ref.at[:, other_o, :], sib_o, sibling)
        ici = [copy(SEM_ICI + sv, keep.at[sv], rcv.at[sv], (*chips[sv], c)) for sv in range(3)]
        ici_o = [copy(SEM_ICI_O + sv, p_o.at[kjs[sv]], rcv_o.at[sv], (*chips[sv], c)) for sv in range(3)]
        fin = copy(SEM_FIN, acc.at[mine, :], acc.at[mine, :], sibling)
        fin_o = copy(SEM_FIN_O, res_o.at[mine_o, :], res_o.at[mine_o, :], sibling)
        smalls = [copy(SEM_SMALL + m, sm_buf.at[me], sm_buf.at[me],
                       (x ^ (m >> 2), y ^ ((m >> 1) & 1), c ^ (m & 1))) for m in range(1, N_DEV)]
        store_w = pltpu.make_async_copy(acc, gw_out, out_sems.at[0])
        store_o = pltpu.make_async_copy(res_o, gwo_out, out_sems.at[1])

        @at_step(0, 0)
        def _():
            d2d_o.start()

        @at_step(0, 1)
        def _():
            d2d_o.wait_recv()
            for j in range(N_SHARD):
                p_o[j] = (gwo_ref[j, mine_o, :].astype(F32) + sib_o[j].astype(F32)).astype(BF16)
            res_o[mine_o, :] = gwo_ref[k, mine_o, :].astype(F32) + sib_o[k].astype(F32)
            for cp in ici_o:
                cp.start()

        rows = pl.ds(pl.multiple_of(t * TT, TT), TT)

        @pl.when(s < N_SHARD)
        def _():
            dpb = dp_ref[...]
            part = _dot_tn(h_ref[...], dpb)

            @pl.when(t == 0)
            def _():
                acc[...] = part

            @pl.when(t > 0)
            def _():
                acc[...] += part

            d = _dot_nt(dpb, w_ref[0])

            @pl.when(s == 0)
            def _():
                dh[rows, :] = d

            @pl.when(s > 0)
            def _():
                dh[rows, :] += d

        for sv in range(N_SHARD):
            @at_step(sv, nt - 1)
            def _(sv=sv):
                sendbuf[sv] = acc[other, :].astype(BF16)
                if sv < 3:
                    keep[sv] = acc[mine, :].astype(BF16)
                d2d[sv].start()

        for sv in range(3):
            @at_step(sv + 1, 0)
            def _(sv=sv):
                d2d[sv].wait_recv()
                keep[sv] = (keep[sv].astype(F32) + sibrcv[sv].astype(F32)).astype(BF16)
                ici[sv].start()

        @at_norm_block(0)
        def _():
            d2d[3].wait_recv()
            ici[0].wait_recv()
            acc[mine, :] += sibrcv[3].astype(F32) + rcv[0].astype(F32)

        @at_norm_block(1)
        def _():
            tot = res_o[mine_o, :]
            for sv in range(3):
                ici_o[sv].wait_recv()
                tot = tot + rcv_o[sv].astype(F32)
            res_o[mine_o, :] = tot
            fin_o.start()

        @at_norm_block(2)
        def _():
            ici[1].wait_recv()
            acc[mine, :] += rcv[1].astype(F32)

        @at_norm_block(0)
        def _():
            dng[...] = jnp.zeros_like(dng)

        @pl.when(s >= N_SHARD)
        def _():
            blk = (s - N_SHARD) * nt + t
            dhv = dh[pl.ds(pl.multiple_of(blk * TX, TX), TX), :]
            xv = x_ref[...]
            r = lax.rsqrt(jnp.mean(xv * xv, axis=-1, keepdims=True) + EPS)
            xn = xv * r
            dng[...] += jnp.sum(dhv * xn, axis=0, keepdims=True)
            dxn = dhv * g_ref[...]
            gx_ref[...] = dx2_ref[...] + r * (dxn - xn * jnp.mean(dxn * xn, axis=-1, keepdims=True))

        @at_step(n_steps - 1, nt - 1)
        def _():
            sm_buf[me] = sm_ref[...] + smb_ref[...]
            sm_buf[me, 0:1, :] = dng[...]
            for cp in smalls:
                cp.start()
            ici[2].wait_recv()
            acc[mine, :] += rcv[2].astype(F32)
            fin.start()
            for m in range(1, N_DEV):
                copy(SEM_SMALL + m, sm_buf.at[0], sm_buf.at[0], sibling).wait_recv()
            tot = sm_buf[0]
            for d in range(1, N_DEV):
                tot = tot + sm_buf[d]
            osm_ref[...] = tot
            fin_o.wait_recv()
            store_o.start()
            fin.wait_recv()
            store_w.start()
            for cp in d2d + [d2d_o] + ici + ici_o + [fin, fin_o] + smalls:
                cp.wait_send()
            store_o.wait()
            store_w.wait()

    def shard_of(s, kr):
        return kr[0] ^ (3 - jnp.minimum(s, 3))

    def tok(s, t):
        return jnp.where(s < N_SHARD, t, nt - 1)

    def blk_map(s, t, kr):
        return (jnp.where(s < N_SHARD, 0, (s - N_SHARD) * nt + t), 0)

    hbm = pl.BlockSpec(memory_space=pl.ANY)
    grid_spec = pltpu.PrefetchScalarGridSpec(
        num_scalar_prefetch=1, grid=(n_steps, nt),
        in_specs=[pl.BlockSpec((TT, D_MODEL), lambda s, t, kr: (tok(s, t), 0)),
                  pl.BlockSpec((TT, SHARD_COLS), lambda s, t, kr: (tok(s, t), shard_of(s, kr))),
                  pl.BlockSpec((1, D_MODEL, SHARD_COLS), lambda s, t, kr: (shard_of(s, kr), 0, 0)),
                  pl.BlockSpec((N_SHARD, WO_ROWS, D_MODEL), lambda s, t, kr: (0, 0, 0)),
                  pl.BlockSpec((TX, D_MODEL), blk_map),
                  pl.BlockSpec((TX, D_MODEL), blk_map),
                  pl.BlockSpec((1, D_MODEL), lambda s, t, kr: (0, 0)),
                  pl.BlockSpec((8, D_MODEL), lambda s, t, kr: (0, 0)),
                  pl.BlockSpec((8, D_MODEL), lambda s, t, kr: (0, 0))],
        out_specs=(pl.BlockSpec((TX, D_MODEL), blk_map), hbm, hbm,
                   pl.BlockSpec((8, D_MODEL), lambda s, t, kr: (0, 0))),
        scratch_shapes=[pltpu.VMEM((D_MODEL, SHARD_COLS), F32), pltpu.VMEM((SEQ, D_MODEL), F32),
                        pltpu.VMEM((N_SHARD, hw, SHARD_COLS), BF16), pltpu.VMEM((3, hw, SHARD_COLS), BF16),
                        pltpu.VMEM((N_SHARD, hw, SHARD_COLS), BF16), pltpu.VMEM((3, hw, SHARD_COLS), BF16),
                        pltpu.VMEM((N_SHARD, ho, D_MODEL), BF16), pltpu.VMEM((N_SHARD, ho, D_MODEL), BF16),
                        pltpu.VMEM((3, ho, D_MODEL), BF16), pltpu.VMEM((WO_ROWS, D_MODEL), F32),
                        pltpu.VMEM((N_DEV, 8, D_MODEL), F32), pltpu.VMEM((1, D_MODEL), F32),
                        pltpu.SemaphoreType.DMA((N_SEM_TAIL,)), pltpu.SemaphoreType.DMA((N_SEM_TAIL,)),
                        pltpu.SemaphoreType.DMA((2,))])
    return pl.pallas_call(
        body, name="bwd_tail", grid_spec=grid_spec,
        out_shape=(jax.ShapeDtypeStruct((SEQ, D_MODEL), F32),
                   jax.ShapeDtypeStruct((D_MODEL, SHARD_COLS), F32),
                   jax.ShapeDtypeStruct((WO_ROWS, D_MODEL), F32),
                   jax.ShapeDtypeStruct((8, D_MODEL), F32)),
        compiler_params=pltpu.CompilerParams(dimension_semantics=("arbitrary", "arbitrary"),
                                             vmem_limit_bytes=60 * 1024 * 1024),
    )(kidx, h, dproj, wg, gwo, x2d, dx2, g1, small_a, small_b)


def _adam_update(w, g, m, v):
    nm = ADAM_B1 * m + (1.0 - ADAM_B1) * g
    nv = ADAM_B2 * v + (1.0 - ADAM_B2) * (g * g)
    m_hat = nm / (1.0 - ADAM_B1 ** ADAM_STEP)
    v_hat = nv / (1.0 - ADAM_B2 ** ADAM_STEP)
    return -ADAM_LR * (m_hat / (jnp.sqrt(v_hat) + ADAM_EPS) + ADAM_WD * w), nm, nv


def _adamw_small(tot, params):
    n = len(params)

    def body(tot_ref, *refs):
        ins, loss_ref, outs = refs[:3 * n], refs[3 * n], refs[3 * n + 1:]
        k = 2 * lax.axis_index("x") + lax.axis_index("y")
        mine = pl.ds(pl.multiple_of(k * HEAD, HEAD), HEAD)
        loss_ref[...] = tot_ref[7:8, 0:1]
        grads = [tot_ref[0:1, :], tot_ref[1:2, :], tot_ref[2:3, 0:D_HGRN], tot_ref[2:3, D_HGRN:],
                 jnp.concatenate([tot_ref[3:4, 0:D_HGRN], tot_ref[3:4, D_HGRN:]], axis=0),
                 jnp.concatenate([tot_ref[4 + tap:5 + tap, mine] for tap in range(3)], axis=1)]
        for i, g in enumerate(grads):
            w_ref, m_ref, v_ref = ins[3 * i:3 * i + 3]
            g_ref, d_ref, nm_ref, nv_ref = outs[4 * i:4 * i + 4]
            g_ref[...] = g
            d_ref[...], nm_ref[...], nv_ref[...] = _adam_update(w_ref[...], g, m_ref[...], v_ref[...])

    vm = pl.BlockSpec(memory_space=pltpu.VMEM)
    flat = [a for triple in params for a in triple]
    out_shape = (jax.ShapeDtypeStruct((1, 1), F32),) + tuple(
        jax.ShapeDtypeStruct(w.shape, F32) for w, _, _ in params for _ in range(4))
    outs = pl.pallas_call(
        body, name="adamw_small", out_shape=out_shape,
        in_specs=[vm] * (1 + 3 * n), out_specs=tuple([vm] * (1 + 4 * n)),
    )(tot, *flat)
    return [outs[0]] + [outs[1 + 4 * i:5 + 4 * i] for i in range(n)]


def _adamw(w, g, m, v, name):
    rows, cols = w.shape
    tr = rows if rows <= 256 else 256

    def body(w_ref, g_ref, m_ref, v_ref, d_ref, nm_ref, nv_ref):
        d_ref[...], nm_ref[...], nv_ref[...] = _adam_update(w_ref[...], g_ref[...], m_ref[...], v_ref[...])

    blk = lambda: pl.BlockSpec((tr, cols), lambda i: (i, 0))
    shp = jax.ShapeDtypeStruct((rows, cols), F32)
    return pl.pallas_call(
        body, name=name, grid=(rows // tr,),
        out_shape=(shp, shp, shp),
        in_specs=[blk(), blk(), blk(), blk()], out_specs=(blk(), blk(), blk()),
        compiler_params=pltpu.CompilerParams(dimension_semantics=("arbitrary",)),
    )(w, g, m, v)


def _local_step(x2d, tgt, proj, lb_logits, cw, ga, gcn, w_out, gf):
    g128 = _group_matrix(MXU_WIDTH, HEAD)
    g64 = _group_matrix(MXU_WIDTH, CONV_GROUP)
    mixed, o, cv, states, sg, b, wog = _mix_fwd(proj, lb_logits, cw, ga, gcn, g128, g64, w_out)
    dx2, dmixed, gwo, part_out = _out_loss(x2d, mixed, wog.reshape(D_MODEL, D_MODEL), gf, tgt)
    dproj, part_mix = _mix_bwd(proj, o, cv, states, sg, b, dmixed, lb_logits, cw, ga, gcn, g128, g64)
    return dproj, dx2, gwo.reshape(N_SHARD, WO_ROWS, D_MODEL), part_out, part_mix


def kernel(x, norm_gain, w_in, lb_logits, conv_w, hgrn_norm_gain, conv_norm_gain, w_out, final_norm_gain, loss_target, m_norm_gain, m_w_in, m_lb_logits, m_conv_w, m_hgrn_norm_gain, m_conv_norm_gain, m_w_out, m_final_norm_gain, v_norm_gain, v_w_in, v_lb_logits, v_conv_w, v_hgrn_norm_gain, v_conv_norm_gain, v_w_out, v_final_norm_gain):
    k = 2 * lax.axis_index("x") + lax.axis_index("y")
    kidx = jnp.reshape(k, (1,)).astype(jnp.int32)
    row = lambda a: a.reshape(1, D_MODEL)
    taps = lambda a: a.reshape(1, 3 * HEAD)
    h, proj, wg, cw = _gather_proj(kidx, x[0], norm_gain, w_in, taps(conv_w))
    dproj, dx2, gwo, part_out, part_mix = _local_step(
        x[0], loss_target[0], proj, lb_logits, cw, hgrn_norm_gain, conv_norm_gain, w_out, row(final_norm_gain))
    grad_x, g_w_in, g_w_out, tot = _bwd_tail(kidx, h, dproj, wg, gwo, x[0], dx2, norm_gain, part_out, part_mix)

    d_w_in, nm_w_in, nv_w_in = _adamw(w_in[0], g_w_in, m_w_in[0], v_w_in[0], "adamw_w_in")
    d_w_out, nm_w_out, nv_w_out = _adamw(w_out[0], g_w_out, m_w_out[0], v_w_out[0], "adamw_w_out")
    (loss, (g_norm_gain, d_ng, nm_ng, nv_ng), (g_final, d_fg, nm_fg, nv_fg), (g_hgrn, d_hg, nm_hg, nv_hg),
     (g_convn, d_cg, nm_cg, nv_cg), (g_lb, d_lb, nm_lb, nv_lb), (g_conv_w, d_cw, nm_cw, nv_cw)) = _adamw_small(
        tot, [(norm_gain, m_norm_gain, v_norm_gain),
              (row(final_norm_gain), row(m_final_norm_gain), row(v_final_norm_gain)),
              (hgrn_norm_gain, m_hgrn_norm_gain, v_hgrn_norm_gain),
              (conv_norm_gain, m_conv_norm_gain, v_conv_norm_gain),
              (lb_logits, m_lb_logits, v_lb_logits),
              (taps(conv_w), taps(m_conv_w), taps(v_conv_w))])
    flat = lambda a: a.reshape(D_MODEL)
    untap = lambda a: a.reshape(1, 3, HEAD)
    return (loss.reshape(()), grad_x[None],
            g_norm_gain, g_w_in[None], g_lb, untap(g_conv_w), g_hgrn, g_convn, g_w_out[None], flat(g_final),
            d_ng, d_w_in[None], d_lb, untap(d_cw), d_hg, d_cg, d_w_out[None], flat(d_fg),
            nm_ng, nm_w_in[None], nm_lb, untap(nm_cw), nm_hg, nm_cg, nm_w_out[None], flat(nm_fg),
            nv_ng, nv_w_in[None], nv_lb, untap(nv_cw), nv_hg, nv_cg, nv_w_out[None], flat(nv_fg))
```

```python
import functools

import jax
import jax.numpy as jnp
import numpy as np
from jax import lax
from jax.experimental import pallas as pl
from jax.experimental.pallas import tpu as pltpu

F32 = jnp.float32
BF16 = jnp.bfloat16
MESH = pl.DeviceIdType.MESH

SEQ = 2048
D_MODEL = 1024
D_HGRN = 512
D_CONV = 512
HEAD = 128
N_HEADS = 4
CHUNK = 64
CONV_GROUP = 64
N_SHARD = 4
SHARD_COLS = 1024
WO_ROWS = 256
EPS = 1e-6
TB = 256
NCB = TB // CHUNK
N_CHUNKS = SEQ // CHUNK
N_DEV = 8
MXU_WIDTH = 256

ADAM_LR = 0.001
ADAM_B1 = 0.9
ADAM_B2 = 0.999
ADAM_EPS = 1e-08
ADAM_WD = 0.01
ADAM_STEP = 10

VMEM_LIMIT = 56 * 1024 * 1024


def _dot(a, b):
    return jnp.dot(a, b, preferred_element_type=F32)


def _dot_nt(a, b):
    return lax.dot_general(a, b, (((1,), (1,)), ((), ())), preferred_element_type=F32)


def _dot_tn(a, b):
    return lax.dot_general(a, b, (((0,), (0,)), ((), ())), preferred_element_type=F32)


def _split_bf16(x, n):
    parts = []
    r = x
    for _ in range(n):
        p = r.astype(BF16)
        parts.append(p)
        r = r - p.astype(F32)
    return parts


def _exact_left(m, x, n=3):
    acc = None
    for p in _split_bf16(x, n):
        t = _dot(m, p)
        acc = t if acc is None else acc + t
    return acc


def _group_mean(x, gmat, n=2):
    w = gmat.shape[0]
    outs = []
    for c0 in range(0, x.shape[1], w):
        acc = None
        for p in _split_bf16(x[:, c0:c0 + w], n):
            t = _dot(p, gmat)
            acc = t if acc is None else acc + t
        outs.append(acc)
    return jnp.concatenate(outs, axis=1)


def _sigmoid(x):
    return 1.0 / (1.0 + jnp.exp(-x))


def _lower_bound(lbl):
    l0 = lbl[0:1, :]
    l1 = lbl[1:2, :]
    m = jnp.maximum(l0, l1)
    e0 = jnp.exp(l0 - m)
    e1 = jnp.exp(l1 - m)
    return e0 / (e0 + e1)


def _chunk_tri(lower):
    r = lax.broadcasted_iota(jnp.int32, (TB, TB), 0)
    c = lax.broadcasted_iota(jnp.int32, (TB, TB), 1)
    same = (r // CHUNK) == (c // CHUNK)
    tri = (c <= r) if lower else (c >= r)
    return jnp.where(same & tri, 1.0, 0.0).astype(BF16)


def _causal():
    r = lax.broadcasted_iota(jnp.int32, (CHUNK, CHUNK), 0)
    c = lax.broadcasted_iota(jnp.int32, (CHUNK, CHUNK), 1)
    return c <= r


def _shift_down(x, sh, prev_tail):
    r = pltpu.roll(x, sh, 0)
    pt = pltpu.roll(prev_tail, sh, 0)
    rows = lax.broadcasted_iota(jnp.int32, prev_tail.shape, 0)
    top = jnp.where(rows < sh, pt, r[0:8])
    return jnp.concatenate([top, r[8:]], axis=0)


def _shift_up(x, sh, next_head):
    n = x.shape[0]
    r = pltpu.roll(x, n - sh, 0)
    nh = pltpu.roll(next_head, 8 - sh, 0)
    rows = lax.broadcasted_iota(jnp.int32, next_head.shape, 0)
    bot = jnp.where(rows >= 8 - sh, nh, r[n - 8:])
    return jnp.concatenate([r[:n - 8], bot], axis=0)


def _group_matrix(width, group):
    r = np.arange(width)[:, None] // group
    c = np.arange(width)[None, :] // group
    return jnp.asarray(np.where(r == c, 1.0 / group, 0.0), dtype=BF16)


TP = 512
SEM_W, SEM_CW, SEM_W_FWD, N_SEM = 0, 3, 6, 9


def _gather_proj(kidx, x2d, g1, w_in, conv_w):
    half_w = D_MODEL // 2
    nt = SEQ // TP

    def body(k_ref, x_ref, g_ref, w_ref, cw_ref, h_ref, p_ref, wg_out, cwg_out,
             wg_v, cwg_v, send_sems, recv_sems, out_sems):
        s, t = pl.program_id(0), pl.program_id(1)
        x, y, c = lax.axis_index("x"), lax.axis_index("y"), lax.axis_index("c")
        k = 2 * x + y
        sibling = (x, y, 1 - c)
        chips = [(1 - x, y), (x, 1 - y), (1 - x, 1 - y)]
        kjs = [2 * cx + cy for cx, cy in chips]

        def w_half(kk, cc):
            return wg_v.at[kk, pl.ds(cc * half_w, half_w), :]

        def cw_of(kk):
            return cwg_v.at[:, pl.ds(pl.multiple_of(kk * HEAD, HEAD), HEAD)]

        def copy(sem, ref, to):
            return pltpu.make_async_remote_copy(
                src_ref=ref, dst_ref=ref, send_sem=send_sems.at[sem], recv_sem=recv_sems.at[sem],
                device_id=to, device_id_type=MESH)

        def at_step(sv, tv):
            return pl.when((s == sv) & (t == tv))

        w_direct = [copy(SEM_W + j, w_half(k, c), (*chip, c)) for j, chip in enumerate(chips)]
        cw_direct = [copy(SEM_CW + j, cw_of(k), (*chip, c)) for j, chip in enumerate(chips)]
        w_passed = [copy(SEM_W_FWD + j, w_half(kj, c), sibling) for j, kj in enumerate(kjs)]
        stores = ([pltpu.make_async_copy(wg_v.at[kk], wg_out.at[kk], out_sems.at[i])
                   for i, kk in enumerate([k] + kjs)]
                  + [pltpu.make_async_copy(cwg_v, cwg_out, out_sems.at[4])])

        @at_step(0, 0)
        def _():
            wg_v[k] = w_ref[0].astype(BF16)
            mine = pl.ds(pl.multiple_of(k * HEAD, HEAD), HEAD)
            cwg_v[:, mine] = jnp.zeros((8, HEAD), F32)
            for tap in range(3):
                cwg_v[tap:tap + 1, mine] = cw_ref[:, tap * HEAD:(tap + 1) * HEAD]
            w_direct[0].start()
            w_direct[1].start()
            for cp in cw_direct:
                cp.start()
            stores[0].start()

        @at_step(1, 0)
        def _():
            for j in range(2):
                copy(SEM_W + j, w_half(kjs[j], c), sibling).wait_recv()
                w_passed[j].start()
            w_direct[2].start()
            copy(SEM_W_FWD, w_half(kjs[0], 1 - c), sibling).wait_recv()
            stores[1].start()

        @at_step(2, 0)
        def _():
            copy(SEM_W_FWD + 1, w_half(kjs[1], 1 - c), sibling).wait_recv()
            stores[2].start()

        @at_step(3, 0)
        def _():
            copy(SEM_W + 2, w_half(kjs[2], c), sibling).wait_recv()
            w_passed[2].start()
            copy(SEM_W_FWD + 2, w_half(kjs[2], 1 - c), sibling).wait_recv()
            stores[3].start()

        rows = pl.ds(pl.multiple_of(t * TP, TP), TP)

        @pl.when(s == 0)
        def _():
            xv = x_ref[...]
            r = lax.rsqrt(jnp.mean(xv * xv, axis=-1, keepdims=True) + EPS)
            h_ref[rows, :] = (xv * r * g_ref[...]).astype(BF16)

        js = k ^ (((s & 1) << 1) | (s >> 1))
        p_ref[...] = _dot(h_ref[rows, :], wg_v[js])

        @at_step(N_SHARD - 1, nt - 1)
        def _():
            for j in range(3):
                copy(SEM_CW + j, cw_of(kjs[j]), sibling).wait_recv()
            stores[4].start()
            for cp in w_direct + cw_direct + w_passed:
                cp.wait_send()
            for st in stores:
                st.wait()

    def x_map(s, t, kr):
        return (jnp.where(s == 0, t, nt - 1), 0)

    def p_map(s, t, kr):
        return (t, kr[0] ^ (((s & 1) << 1) | (s >> 1)))

    hbm = pl.BlockSpec(memory_space=pl.ANY)
    grid_spec = pltpu.PrefetchScalarGridSpec(
        num_scalar_prefetch=1, grid=(N_SHARD, nt),
        in_specs=[pl.BlockSpec((TP, D_MODEL), x_map),
                  pl.BlockSpec((1, D_MODEL), lambda s, t, kr: (0, 0)),
                  pl.BlockSpec((1, D_MODEL, SHARD_COLS), lambda s, t, kr: (0, 0, 0)),
                  pl.BlockSpec((1, 3 * HEAD), lambda s, t, kr: (0, 0))],
        out_specs=(pl.BlockSpec((SEQ, D_MODEL), lambda s, t, kr: (0, 0)),
                   pl.BlockSpec((TP, SHARD_COLS), p_map), hbm, hbm),
        scratch_shapes=[pltpu.VMEM((N_SHARD, D_MODEL, SHARD_COLS), BF16),
                        pltpu.VMEM((8, D_CONV), F32),
                        pltpu.SemaphoreType.DMA((N_SEM,)), pltpu.SemaphoreType.DMA((N_SEM,)),
                        pltpu.SemaphoreType.DMA((5,))])
    return pl.pallas_call(
        body, name="gather_proj", grid_spec=grid_spec,
        out_shape=(jax.ShapeDtypeStruct((SEQ, D_MODEL), BF16),
                   jax.ShapeDtypeStruct((SEQ, N_SHARD * SHARD_COLS), F32),
                   jax.ShapeDtypeStruct((N_SHARD, D_MODEL, SHARD_COLS), BF16),
                   jax.ShapeDtypeStruct((8, D_CONV), F32)),
        compiler_params=pltpu.CompilerParams(dimension_semantics=("arbitrary", "arbitrary"),
                                             vmem_limit_bytes=VMEM_LIMIT),
    )(kidx, x2d, g1, w_in, conv_w)


def _mix_fwd(proj, lb_logits, cw, ga, gcn, g128, g64, w_out):
    half_o = WO_ROWS // 2
    nblk = SEQ // TB

    def body(p_ref, lbl_ref, cw_ref, ga_ref, gcn_ref, g128_ref, g64_ref, wo_ref,
             mixed_ref, o_ref, cv_ref, sto_ref, sg_ref, b_ref, wog_out,
             st_ref, tail_ref, f_ref, wog_v, send_sems, recv_sems, out_sem):
        i = pl.program_id(0)
        x, y, c = lax.axis_index("x"), lax.axis_index("y"), lax.axis_index("c")
        k = 2 * x + y
        sibling = (x, y, 1 - c)
        chips = [(1 - x, y), (x, 1 - y), (1 - x, 1 - y)]
        kjs = [2 * cx + cy for cx, cy in chips]

        def wo_half(kk, cc):
            return wog_v.at[kk, pl.ds(cc * half_o, half_o), :]

        def copy(sem, ref, to):
            return pltpu.make_async_remote_copy(
                src_ref=ref, dst_ref=ref, send_sem=send_sems.at[sem], recv_sem=recv_sems.at[sem],
                device_id=to, device_id_type=MESH)

        wo_direct = [copy(j, wo_half(k, c), (*chip, c)) for j, chip in enumerate(chips)]
        wo_passed = [copy(3 + j, wo_half(kj, c), sibling) for j, kj in enumerate(kjs)]
        wo_store = pltpu.make_async_copy(wog_v, wog_out, out_sem.at[0])

        @pl.when(i == 0)
        def _():
            st_ref[...] = jnp.zeros_like(st_ref)
            tail_ref[...] = jnp.zeros_like(tail_ref)
            wog_v[k] = wo_ref[0].astype(BF16)
            for cp in wo_direct:
                cp.start()

        @pl.when(i == nblk - 2)
        def _():
            for j in range(3):
                copy(j, wo_half(kjs[j], c), sibling).wait_recv()
                wo_passed[j].start()

        lb = _lower_bound(lbl_ref[...])
        sg = _sigmoid(p_ref[:, 512:1024])
        sg_ref[...] = sg
        f = lb + (1.0 - lb) * sg
        f_ref[...] = f
        b_ref[...] = _exact_left(_chunk_tri(True), jnp.log(f))
        causal = _causal()
        for n in range(NCB):
            sl = pl.ds(n * CHUNK, CHUNK)
            bc = b_ref[sl, :]
            g = b_ref[n * CHUNK + CHUNK - 1:n * CHUNK + CHUNK, :]
            kk = 1.0 - f_ref[sl, :]
            qd = (p_ref[sl, 0:512] * jnp.exp(bc)).astype(BF16)
            ki = (kk * jnp.exp(-bc)).astype(BF16)
            ke = (kk * jnp.exp(g - bc)).astype(BF16)
            vb = p_ref[sl, 1024:1536].astype(BF16)
            dec = jnp.exp(g)
            for hd in range(N_HEADS):
                cs = slice(hd * HEAD, (hd + 1) * HEAD)
                st = st_ref[hd]
                sto_ref[n, hd] = st
                sc = jnp.where(causal, _dot_nt(qd[:, cs], ki[:, cs]), 0.0)
                o_ref[sl, cs] = _dot(sc.astype(BF16), vb[:, cs]) + _dot_nt(qd[:, cs], st.astype(BF16))
                st_ref[hd] = st * dec[:, cs] + _dot_tn(vb[:, cs], ke[:, cs])

        o = o_ref[...]
        ra = lax.rsqrt(_group_mean(o * o, g128_ref[...]) + EPS)
        za = p_ref[:, 1536:2048]
        mixed_ref[:, 0:512] = (o * ra * ga_ref[...] * (za * _sigmoid(za))).astype(BF16)

        cu = p_ref[:, 3072:3584] * p_ref[:, 2048:2560]
        tail = tail_ref[...]
        cv = (cw_ref[0:1, :] * _shift_down(cu, 2, tail) + cw_ref[1:2, :] * _shift_down(cu, 1, tail)
              + cw_ref[2:3, :] * cu)
        tail_ref[...] = cu[TB - 8:, :]
        cv_ref[...] = cv
        yb = p_ref[:, 2560:3072] * cv
        rb = lax.rsqrt(_group_mean(yb * yb, g64_ref[...]) + EPS)
        zb = p_ref[:, 3584:4096]
        mixed_ref[:, 512:1024] = (yb * rb * gcn_ref[...] * (zb * _sigmoid(zb))).astype(BF16)

        @pl.when(i == nblk - 1)
        def _():
            for j in range(3):
                copy(3 + j, wo_half(kjs[j], 1 - c), sibling).wait_recv()
            wo_store.start()
            for cp in wo_direct + wo_passed:
                cp.wait_send()
            wo_store.wait()

    row = lambda w: pl.BlockSpec((1, w), lambda i: (0, 0))
    return pl.pallas_call(
        body, name="mix_fwd", grid=(nblk,),
        out_shape=(jax.ShapeDtypeStruct((SEQ, D_MODEL), BF16),
                   jax.ShapeDtypeStruct((SEQ, D_HGRN), F32),
                   jax.ShapeDtypeStruct((SEQ, D_CONV), F32),
                   jax.ShapeDtypeStruct((N_CHUNKS, N_HEADS, HEAD, HEAD), F32),
                   jax.ShapeDtypeStruct((SEQ, D_HGRN), F32),
                   jax.ShapeDtypeStruct((SEQ, D_HGRN), F32),
                   jax.ShapeDtypeStruct((N_SHARD, WO_ROWS, D_MODEL), BF16)),
        in_specs=[pl.BlockSpec((TB, 4096), lambda i: (i, 0)),
                  pl.BlockSpec((2, D_HGRN), lambda i: (0, 0)),
                  pl.BlockSpec((8, D_CONV), lambda i: (0, 0)),
                  row(D_HGRN), row(D_CONV),
                  pl.BlockSpec((MXU_WIDTH, MXU_WIDTH), lambda i: (0, 0)),
                  pl.BlockSpec((MXU_WIDTH, MXU_WIDTH), lambda i: (0, 0)),
                  pl.BlockSpec((1, WO_ROWS, D_MODEL), lambda i: (0, 0, 0))],
        out_specs=(pl.BlockSpec((TB, D_MODEL), lambda i: (i, 0)),
                   pl.BlockSpec((TB, D_HGRN), lambda i: (i, 0)),
                   pl.BlockSpec((TB, D_CONV), lambda i: (i, 0)),
                   pl.BlockSpec((NCB, N_HEADS, HEAD, HEAD), lambda i: (i, 0, 0, 0)),
                   pl.BlockSpec((TB, D_HGRN), lambda i: (i, 0)),
                   pl.BlockSpec((TB, D_HGRN), lambda i: (i, 0)),
                   pl.BlockSpec(memory_space=pl.ANY)),
        scratch_shapes=[pltpu.VMEM((N_HEADS, HEAD, HEAD), F32), pltpu.VMEM((8, D_CONV), F32),
                        pltpu.VMEM((TB, D_HGRN), F32),
                        pltpu.VMEM((N_SHARD, WO_ROWS, D_MODEL), BF16),
                        pltpu.SemaphoreType.DMA((6,)), pltpu.SemaphoreType.DMA((6,)),
                        pltpu.SemaphoreType.DMA((1,))],
        compiler_params=pltpu.CompilerParams(dimension_semantics=("arbitrary",), vmem_limit_bytes=VMEM_LIMIT),
    )(proj, lb_logits, cw, ga, gcn, g128, g64, w_out)


def _out_loss(x2d, mixed, wog, gf, tgt):
    def body(x_ref, m_ref, wo_ref, gf_ref, t_ref, dx2_ref, dm_ref, gwo_ref, part_ref, acc_ref):
        i = pl.program_id(0)

        @pl.when(i == 0)
        def _():
            acc_ref[...] = jnp.zeros_like(acc_ref)
            part_ref[...] = jnp.zeros_like(part_ref)

        mixed_b = m_ref[...]
        x2 = x_ref[...] + _dot(mixed_b, wo_ref[...])
        r2 = lax.rsqrt(jnp.mean(x2 * x2, axis=-1, keepdims=True) + EPS)
        n2 = x2 * r2
        gfv = gf_ref[...]
        err = n2 * gfv - t_ref[...]
        loss = 0.5 * jnp.sum(jnp.mean(err * err, axis=-1, keepdims=True), axis=0, keepdims=True)
        dy = err * (1.0 / D_MODEL)
        part_ref[1:2, :] += jnp.sum(dy * n2, axis=0, keepdims=True)
        part_ref[7:8, :] += jnp.broadcast_to(loss, (1, D_MODEL))
        dn = dy * gfv
        dx2 = r2 * (dn - n2 * jnp.mean(dn * n2, axis=-1, keepdims=True))
        dx2_ref[...] = dx2
        dx2_b = dx2.astype(BF16)
        dm_ref[...] = _dot_nt(dx2_b, wo_ref[...])
        acc_ref[...] += _dot_tn(mixed_b, dx2_b)

        @pl.when(i == pl.num_programs(0) - 1)
        def _():
            gwo_ref[...] = acc_ref[...].astype(BF16)

    blk = lambda: pl.BlockSpec((TP, D_MODEL), lambda i: (i, 0))
    return pl.pallas_call(
        body, name="out_loss", grid=(SEQ // TP,),
        out_shape=(jax.ShapeDtypeStruct((SEQ, D_MODEL), F32),
                   jax.ShapeDtypeStruct((SEQ, D_MODEL), F32),
                   jax.ShapeDtypeStruct((D_MODEL, D_MODEL), BF16),
                   jax.ShapeDtypeStruct((8, D_MODEL), F32)),
        in_specs=[blk(), blk(), pl.BlockSpec((D_MODEL, D_MODEL), lambda i: (0, 0)),
                  pl.BlockSpec((1, D_MODEL), lambda i: (0, 0)), blk()],
        out_specs=(blk(), blk(), pl.BlockSpec((D_MODEL, D_MODEL), lambda i: (0, 0)),
                   pl.BlockSpec((8, D_MODEL), lambda i: (0, 0))),
        scratch_shapes=[pltpu.VMEM((D_MODEL, D_MODEL), F32)],
        compiler_params=pltpu.CompilerParams(dimension_semantics=("arbitrary",), vmem_limit_bytes=VMEM_LIMIT),
    )(x2d, mixed, wog, gf, tgt)


def _mix_bwd(proj, o, cv, states, sg, b, dmixed, lb_logits, cw, ga, gcn, g128, g64):
    nblk = SEQ // TB

    def body(p_ref, o_ref, cv_ref, st_ref, sg_ref, b_ref, dm_ref, lbl_ref, cw_ref, ga_ref, gcn_ref,
             g128_ref, g64_ref,
             dp_ref, part_ref, dst_ref, head_ref, f_ref, do_ref, db_ref, dg_ref, dk_ref, dlb_ref):
        i = pl.program_id(0)

        @pl.when(i == 0)
        def _():
            dst_ref[...] = jnp.zeros_like(dst_ref)
            head_ref[...] = jnp.zeros_like(head_ref)
            part_ref[...] = jnp.zeros_like(part_ref)
            dlb_ref[...] = jnp.zeros_like(dlb_ref)

        ov = o_ref[...]
        ra = lax.rsqrt(_group_mean(ov * ov, g128_ref[...]) + EPS)
        na = ov * ra
        za = p_ref[:, 1536:2048]
        sg = _sigmoid(za)
        dma = dm_ref[:, 0:512]
        gav = ga_ref[...]
        part_ref[2:3, 0:D_HGRN] += jnp.sum(dma * na * (za * sg), axis=0, keepdims=True)
        dp_ref[:, 1536:2048] = (dma * na * gav * (sg * (1.0 + za * (1.0 - sg)))).astype(BF16)
        dna = dma * gav * (za * sg)
        do_ref[...] = ra * (dna - na * _group_mean(dna * na, g128_ref[...]))

        cvv = cv_ref[...]
        gb = p_ref[:, 2560:3072]
        yb = gb * cvv
        rb = lax.rsqrt(_group_mean(yb * yb, g64_ref[...]) + EPS)
        nb = yb * rb
        zb = p_ref[:, 3584:4096]
        sgb = _sigmoid(zb)
        dmb = dm_ref[:, 512:1024]
        gcv = gcn_ref[...]
        part_ref[2:3, D_HGRN:] += jnp.sum(dmb * nb * (zb * sgb), axis=0, keepdims=True)
        dp_ref[:, 3584:4096] = (dmb * nb * gcv * (sgb * (1.0 + zb * (1.0 - sgb)))).astype(BF16)
        dnb = dmb * gcv * (zb * sgb)
        dyb = rb * (dnb - nb * _group_mean(dnb * nb, g64_ref[...]))
        dp_ref[:, 2560:3072] = (dyb * cvv).astype(BF16)
        dcv = dyb * gb
        head = head_ref[...]
        dcv1 = _shift_up(dcv, 1, head)
        dcv2 = _shift_up(dcv, 2, head)
        head_ref[...] = dcv[0:8, :]
        u = p_ref[:, 2048:2560]
        gc = p_ref[:, 3072:3584]
        cu = gc * u
        part_ref[4:5, 0:D_CONV] += jnp.sum(dcv2 * cu, axis=0, keepdims=True)
        part_ref[5:6, 0:D_CONV] += jnp.sum(dcv1 * cu, axis=0, keepdims=True)
        part_ref[6:7, 0:D_CONV] += jnp.sum(dcv * cu, axis=0, keepdims=True)
        dcu = cw_ref[2:3, :] * dcv + cw_ref[1:2, :] * dcv1 + cw_ref[0:1, :] * dcv2
        dp_ref[:, 3072:3584] = (dcu * u).astype(BF16)
        dp_ref[:, 2048:2560] = (dcu * gc).astype(BF16)

        lb = _lower_bound(lbl_ref[...])
        s = sg_ref[...]
        f = lb + (1.0 - lb) * s
        f_ref[...] = f
        causal = _causal()
        for n in reversed(range(NCB)):
            sl = pl.ds(n * CHUNK, CHUNK)
            bc = b_ref[sl, :]
            g = b_ref[n * CHUNK + CHUNK - 1:n * CHUNK + CHUNK, :]
            kk = 1.0 - f_ref[sl, :]
            eb = jnp.exp(bc)
            enb = jnp.exp(-bc)
            eg = jnp.exp(g - bc)
            dec = jnp.exp(g)
            qd = p_ref[sl, 0:512] * eb
            ki = kk * enb
            ke = kk * eg
            qd_b = qd.astype(BF16)
            ki_b = ki.astype(BF16)
            ke_b = ke.astype(BF16)
            vb = p_ref[sl, 1024:1536].astype(BF16)
            do_b = do_ref[sl, :].astype(BF16)
            for hd in range(N_HEADS):
                cs = slice(hd * HEAD, (hd + 1) * HEAD)
                st = st_ref[n, hd]
                dst = dst_ref[hd]
                st_b = st.astype(BF16)
                dst_b = dst.astype(BF16)
                sc = jnp.where(causal, _dot_nt(qd_b[:, cs], ki_b[:, cs]), 0.0).astype(BF16)
                am = jnp.where(causal, _dot_nt(do_b[:, cs], vb[:, cs]), 0.0).astype(BF16)
                dqd = _dot(am, ki_b[:, cs]) + _dot(do_b[:, cs], st_b)
                dki = _dot_tn(am, qd_b[:, cs])
                dke = _dot(vb[:, cs], dst_b)
                dv = _dot_tn(sc, do_b[:, cs]) + _dot_nt(ke_b[:, cs], dst_b)
                ddec = jnp.sum(dst * st, axis=0, keepdims=True)
                dst_ref[hd] = dst * dec[:, cs] + _dot_tn(do_b[:, cs], qd_b[:, cs])
                dp_ref[sl, cs] = (dqd * eb[:, cs]).astype(BF16)
                dp_ref[sl, 1024 + hd * HEAD:1024 + (hd + 1) * HEAD] = dv.astype(BF16)
                dk_ref[sl, cs] = dki * enb[:, cs] + dke * eg[:, cs]
                db_ref[sl, cs] = dqd * qd[:, cs] - dki * ki[:, cs] - dke * ke[:, cs]
                dgv = jnp.sum(dke * ke[:, cs], axis=0, keepdims=True) + ddec * dec[:, cs]
                dg_ref[sl, cs] = jnp.broadcast_to(dgv, (CHUNK, HEAD))

        dlogf = _exact_left(_chunk_tri(False), db_ref[...], 2) + dg_ref[...]
        df = dlogf / f - dk_ref[...]
        dlb_ref[...] += jnp.sum(df * (1.0 - s), axis=0, keepdims=True)
        dp_ref[:, 512:1024] = (df * (1.0 - lb) * s * (1.0 - s)).astype(BF16)

        @pl.when(i == nblk - 1)
        def _():
            row = dlb_ref[...] * lb * (1.0 - lb)
            part_ref[3:4, 0:D_HGRN] = row
            part_ref[3:4, D_HGRN:] = -row

    rev = lambda w: pl.BlockSpec((TB, w), lambda i: (nblk - 1 - i, 0))
    row = lambda w: pl.BlockSpec((1, w), lambda i: (0, 0))
    return pl.pallas_call(
        body, name="mix_bwd", grid=(nblk,),
        out_shape=(jax.ShapeDtypeStruct((SEQ, 4096), BF16),
                   jax.ShapeDtypeStruct((8, D_MODEL), F32)),
        in_specs=[rev(4096), rev(D_HGRN), rev(D_CONV),
                  pl.BlockSpec((NCB, N_HEADS, HEAD, HEAD), lambda i: (nblk - 1 - i, 0, 0, 0)),
                  rev(D_HGRN), rev(D_HGRN), rev(D_MODEL),
                  pl.BlockSpec((2, D_HGRN), lambda i: (0, 0)),
                  pl.BlockSpec((8, D_CONV), lambda i: (0, 0)),
                  row(D_HGRN), row(D_CONV),
                  pl.BlockSpec((MXU_WIDTH, MXU_WIDTH), lambda i: (0, 0)),
                  pl.BlockSpec((MXU_WIDTH, MXU_WIDTH), lambda i: (0, 0))],
        out_specs=(rev(4096), pl.BlockSpec((8, D_MODEL), lambda i: (0, 0))),
        scratch_shapes=[pltpu.VMEM((N_HEADS, HEAD, HEAD), F32), pltpu.VMEM((8, D_CONV), F32),
                        pltpu.VMEM((TB, D_HGRN), F32), pltpu.VMEM((TB, D_HGRN), F32),
                        pltpu.VMEM((TB, D_HGRN), F32), pltpu.VMEM((TB, D_HGRN), F32),
                        pltpu.VMEM((TB, D_HGRN), F32), pltpu.VMEM((1, D_HGRN), F32)],
        compiler_params=pltpu.CompilerParams(dimension_semantics=("arbitrary",), vmem_limit_bytes=VMEM_LIMIT),
    )(proj, o, cv, states, sg, b, dmixed, lb_logits, cw, ga, gcn, g128, g64)


TT = 1024
TX = 256
(SEM_D2D, SEM_D2D_O, SEM_ICI, SEM_ICI_O, SEM_FIN, SEM_FIN_O, SEM_SMALL, N_SEM_TAIL) = 0, 4, 5, 8, 11, 12, 12, 20


def _bwd_tail(kidx, h, dproj, wg, gwo, x2d, dx2, g1, small_a, small_b):
    hw = D_MODEL // 2
    ho = WO_ROWS // 2
    nt = SEQ // TT
    norm_step = 2 * N_SHARD
    n_steps = norm_step + SEQ // TX // nt

    def body(k_ref, h_ref, dp_ref, w_ref, gwo_ref, x_ref, dx2_ref, g_ref, sm_ref, smb_ref,
             gx_ref, gw_out, gwo_out, osm_ref,
             acc, dh, sendbuf, keep, sibrcv, rcv, sib_o, p_o, rcv_o, res_o, sm_buf, dng,
             send_sems, recv_sems, out_sems):
        s, t = pl.program_id(0), pl.program_id(1)
        x, y, c = lax.axis_index("x"), lax.axis_index("y"), lax.axis_index("c")
        k = 2 * x + y
        me = 4 * x + 2 * y + c
        sibling = (x, y, 1 - c)
        chips = [(1 - x, 1 - y), (1 - x, y), (x, 1 - y)]
        kjs = [2 * cx + cy for cx, cy in chips]
        mine = pl.ds(pl.multiple_of(c * hw, hw), hw)
        other = pl.ds(pl.multiple_of((1 - c) * hw, hw), hw)
        mine_o = pl.ds(pl.multiple_of(c * ho, ho), ho)
        other_o = pl.ds(pl.multiple_of((1 - c) * ho, ho), ho)

        def copy(sem, src, dst, to):
            return pltpu.make_async_remote_copy(
                src_ref=src, dst_ref=dst, send_sem=send_sems.at[sem], recv_sem=recv_sems.at[sem],
                device_id=to, device_id_type=MESH)

        def at_step(sv, tv):
            return pl.when((s == sv) & (t == tv))

        def at_norm_block(b):
            return at_step(norm_step + b // nt, b % nt)

        d2d = [copy(SEM_D2D + sv, sendbuf.at[sv], sibrcv.at[sv], sibling) for sv in range(N_SHARD)]
        d2d_o = copy(SEM_D2D_O, gwo_ref.at[:, other_o, :], sib_o, sibling)
        ici = [copy(SEM_ICI + sv, keep.at[sv], rcv.at[sv], (*chips[sv], c)) for sv in range(3)]
        ici_o = [copy(SEM_ICI_O + sv, p_o.at[kjs[sv]], rcv_o.at[sv], (*chips[sv], c)) for sv in range(3)]
        fin = copy(SEM_FIN, acc.at[mine, :], acc.at[mine, :], sibling)
        fin_o = copy(SEM_FIN_O, res_o.at[mine_o, :], res_o.at[mine_o, :], sibling)
        smalls = [copy(SEM_SMALL + m, sm_buf.at[me], sm_buf.at[me],
                       (x ^ (m >> 2), y ^ ((m >> 1) & 1), c ^ (m & 1))) for m in range(1, N_DEV)]
        store_w = pltpu.make_async_copy(acc, gw_out, out_sems.at[0])
        store_o = pltpu.make_async_copy(res_o, gwo_out, out_sems.at[1])

        @at_step(0, 0)
        def _():
            d2d_o.start()

        @at_step(0, 1)
        def _():
            d2d_o.wait_recv()
            for j in range(N_SHARD):
                p_o[j] = (gwo_ref[j, mine_o, :].astype(F32) + sib_o[j].astype(F32)).astype(BF16)
            res_o[mine_o, :] = gwo_ref[k, mine_o, :].astype(F32) + sib_o[k].astype(F32)
            for cp in ici_o:
                cp.start()

        rows = pl.ds(pl.multiple_of(t * TT, TT), TT)

        @pl.when(s < N_SHARD)
        def _():
            part = _dot_tn(h_ref[...], dp_ref[...])

            @pl.when(t == 0)
            def _():
                acc[...] = part

            @pl.when(t > 0)
            def _():
                acc[...] += part

        @pl.when((s >= N_SHARD) & (s < norm_step))
        def _():
            d = _dot_nt(dp_ref[...], w_ref[0])

            @pl.when(s == N_SHARD)
            def _():
                dh[rows, :] = d

            @pl.when(s > N_SHARD)
            def _():
                dh[rows, :] += d

        for sv in range(N_SHARD):
            @at_step(sv, nt - 1)
            def _(sv=sv):
                sendbuf[sv] = acc[other, :].astype(BF16)
                if sv < 3:
                    keep[sv] = acc[mine, :].astype(BF16)
                d2d[sv].start()

        for sv in range(3):
            @at_step(sv + 1, 0)
            def _(sv=sv):
                d2d[sv].wait_recv()
                keep[sv] = (keep[sv].astype(F32) + sibrcv[sv].astype(F32)).astype(BF16)
                ici[sv].start()

        @at_step(N_SHARD, 0)
        def _():
            d2d[3].wait_recv()
            acc[mine, :] += sibrcv[3].astype(F32)

        @at_step(N_SHARD + 2, 0)
        def _():
            ici[0].wait_recv()
            acc[mine, :] += rcv[0].astype(F32)

        @at_step(N_SHARD, 1)
        def _():
            tot = res_o[mine_o, :]
            for sv in range(3):
                ici_o[sv].wait_recv()
                tot = tot + rcv_o[sv].astype(F32)
            res_o[mine_o, :] = tot
            fin_o.start()

        @at_step(N_SHARD + 3, 0)
        def _():
            ici[1].wait_recv()
            acc[mine, :] += rcv[1].astype(F32)

        @at_norm_block(0)
        def _():
            dng[...] = jnp.zeros_like(dng)

        @pl.when(s >= norm_step)
        def _():
            blk = (s - norm_step) * nt + t
            dhv = dh[pl.ds(pl.multiple_of(blk * TX, TX), TX), :]
            xv = x_ref[...]
            r = lax.rsqrt(jnp.mean(xv * xv, axis=-1, keepdims=True) + EPS)
            xn = xv * r
            dng[...] += jnp.sum(dhv * xn, axis=0, keepdims=True)
            dxn = dhv * g_ref[...]
            gx_ref[...] = dx2_ref[...] + r * (dxn - xn * jnp.mean(dxn * xn, axis=-1, keepdims=True))

        @at_step(n_steps - 1, nt - 1)
        def _():
            sm_buf[me] = sm_ref[...] + smb_ref[...]
            sm_buf[me, 0:1, :] = dng[...]
            for cp in smalls:
                cp.start()
            ici[2].wait_recv()
            acc[mine, :] += rcv[2].astype(F32)
            fin.start()
            for m in range(1, N_DEV):
                copy(SEM_SMALL + m, sm_buf.at[0], sm_buf.at[0], sibling).wait_recv()
            tot = sm_buf[0]
            for d in range(1, N_DEV):
                tot = tot + sm_buf[d]
            osm_ref[...] = tot
            fin_o.wait_recv()
            store_o.start()
            fin.wait_recv()
            store_w.start()
            for cp in d2d + [d2d_o] + ici + ici_o + [fin, fin_o] + smalls:
                cp.wait_send()
            store_o.wait()
            store_w.wait()

    def shard_of(s, kr):
        order = jnp.where(s < N_SHARD, s, jnp.where(s < norm_step, s - N_SHARD, 3))
        return kr[0] ^ (3 - order)

    def h_map(s, t, kr):
        return (jnp.where(s < N_SHARD, t, nt - 1), 0)

    def dp_map(s, t, kr):
        return (jnp.where(s < norm_step, t, nt - 1), shard_of(s, kr))

    def w_map(s, t, kr):
        return (shard_of(jnp.maximum(s, N_SHARD), kr), 0, 0)

    def blk_map(s, t, kr):
        return (jnp.where(s < norm_step, 0, (s - norm_step) * nt + t), 0)

    hbm = pl.BlockSpec(memory_space=pl.ANY)
    grid_spec = pltpu.PrefetchScalarGridSpec(
        num_scalar_prefetch=1, grid=(n_steps, nt),
        in_specs=[pl.BlockSpec((TT, D_MODEL), h_map),
                  pl.BlockSpec((TT, SHARD_COLS), dp_map),
                  pl.BlockSpec((1, D_MODEL, SHARD_COLS), w_map),
                  pl.BlockSpec((N_SHARD, WO_ROWS, D_MODEL), lambda s, t, kr: (0, 0, 0)),
                  pl.BlockSpec((TX, D_MODEL), blk_map),
                  pl.BlockSpec((TX, D_MODEL), blk_map),
                  pl.BlockSpec((1, D_MODEL), lambda s, t, kr: (0, 0)),
                  pl.BlockSpec((8, D_MODEL), lambda s, t, kr: (0, 0)),
                  pl.BlockSpec((8, D_MODEL), lambda s, t, kr: (0, 0))],
        out_specs=(pl.BlockSpec((TX, D_MODEL), blk_map), hbm, hbm,
                   pl.BlockSpec((8, D_MODEL), lambda s, t, kr: (0, 0))),
        scratch_shapes=[pltpu.VMEM((D_MODEL, SHARD_COLS), F32), pltpu.VMEM((SEQ, D_MODEL), F32),
                        pltpu.VMEM((N_SHARD, hw, SHARD_COLS), BF16), pltpu.VMEM((3, hw, SHARD_COLS), BF16),
                        pltpu.VMEM((N_SHARD, hw, SHARD_COLS), BF16), pltpu.VMEM((3, hw, SHARD_COLS), BF16),
                        pltpu.VMEM((N_SHARD, ho, D_MODEL), BF16), pltpu.VMEM((N_SHARD, ho, D_MODEL), BF16),
                        pltpu.VMEM((3, ho, D_MODEL), BF16), pltpu.VMEM((WO_ROWS, D_MODEL), F32),
                        pltpu.VMEM((N_DEV, 8, D_MODEL), F32), pltpu.VMEM((1, D_MODEL), F32),
                        pltpu.SemaphoreType.DMA((N_SEM_TAIL,)), pltpu.SemaphoreType.DMA((N_SEM_TAIL,)),
                        pltpu.SemaphoreType.DMA((2,))])
    return pl.pallas_call(
        body, name="bwd_tail", grid_spec=grid_spec,
        out_shape=(jax.ShapeDtypeStruct((SEQ, D_MODEL), F32),
                   jax.ShapeDtypeStruct((D_MODEL, SHARD_COLS), F32),
                   jax.ShapeDtypeStruct((WO_ROWS, D_MODEL), F32),
                   jax.ShapeDtypeStruct((8, D_MODEL), F32)),
        compiler_params=pltpu.CompilerParams(dimension_semantics=("arbitrary", "arbitrary"),
                                             vmem_limit_bytes=60 * 1024 * 1024),
    )(kidx, h, dproj, wg, gwo, x2d, dx2, g1, small_a, small_b)


def _adam_update(w, g, m, v):
    nm = ADAM_B1 * m + (1.0 - ADAM_B1) * g
    nv = ADAM_B2 * v + (1.0 - ADAM_B2) * (g * g)
    m_hat = nm / (1.0 - ADAM_B1 ** ADAM_STEP)
    v_hat = nv / (1.0 - ADAM_B2 ** ADAM_STEP)
    return -ADAM_LR * (m_hat / (jnp.sqrt(v_hat) + ADAM_EPS) + ADAM_WD * w), nm, nv


def _adamw_small(tot, params):
    n = len(params)

    def body(tot_ref, *refs):
        ins, loss_ref, outs = refs[:3 * n], refs[3 * n], refs[3 * n + 1:]
        k = 2 * lax.axis_index("x") + lax.axis_index("y")
        mine = pl.ds(pl.multiple_of(k * HEAD, HEAD), HEAD)
        loss_ref[...] = tot_ref[7:8, 0:1]
        grads = [tot_ref[0:1, :], tot_ref[1:2, :], tot_ref[2:3, 0:D_HGRN], tot_ref[2:3, D_HGRN:],
                 jnp.concatenate([tot_ref[3:4, 0:D_HGRN], tot_ref[3:4, D_HGRN:]], axis=0),
                 jnp.concatenate([tot_ref[4 + tap:5 + tap, mine] for tap in range(3)], axis=1)]
        for i, g in enumerate(grads):
            w_ref, m_ref, v_ref = ins[3 * i:3 * i + 3]
            g_ref, d_ref, nm_ref, nv_ref = outs[4 * i:4 * i + 4]
            g_ref[...] = g
            d_ref[...], nm_ref[...], nv_ref[...] = _adam_update(w_ref[...], g, m_ref[...], v_ref[...])

    vm = pl.BlockSpec(memory_space=pltpu.VMEM)
    flat = [a for triple in params for a in triple]
    out_shape = (jax.ShapeDtypeStruct((1, 1), F32),) + tuple(
        jax.ShapeDtypeStruct(w.shape, F32) for w, _, _ in params for _ in range(4))
    outs = pl.pallas_call(
        body, name="adamw_small", out_shape=out_shape,
        in_specs=[vm] * (1 + 3 * n), out_specs=tuple([vm] * (1 + 4 * n)),
    )(tot, *flat)
    return [outs[0]] + [outs[1 + 4 * i:5 + 4 * i] for i in range(n)]


def _adamw(w, g, m, v, name):
    rows, cols = w.shape
    tr = rows if rows <= 256 else 256

    def body(w_ref, g_ref, m_ref, v_ref, d_ref, nm_ref, nv_ref):
        d_ref[...], nm_ref[...], nv_ref[...] = _adam_update(w_ref[...], g_ref[...], m_ref[...], v_ref[...])

    blk = lambda: pl.BlockSpec((tr, cols), lambda i: (i, 0))
    shp = jax.ShapeDtypeStruct((rows, cols), F32)
    return pl.pallas_call(
        body, name=name, grid=(rows // tr,),
        out_shape=(shp, shp, shp),
        in_specs=[blk(), blk(), blk(), blk()], out_specs=(blk(), blk(), blk()),
        compiler_params=pltpu.CompilerParams(dimension_semantics=("arbitrary",)),
    )(w, g, m, v)


def _local_step(x2d, tgt, proj, lb_logits, cw, ga, gcn, w_out, gf):
    g128 = _group_matrix(MXU_WIDTH, HEAD)
    g64 = _group_matrix(MXU_WIDTH, CONV_GROUP)
    mixed, o, cv, states, sg, b, wog = _mix_fwd(proj, lb_logits, cw, ga, gcn, g128, g64, w_out)
    dx2, dmixed, gwo, part_out = _out_loss(x2d, mixed, wog.reshape(D_MODEL, D_MODEL), gf, tgt)
    dproj, part_mix = _mix_bwd(proj, o, cv, states, sg, b, dmixed, lb_logits, cw, ga, gcn, g128, g64)
    return dproj, dx2, gwo.reshape(N_SHARD, WO_ROWS, D_MODEL), part_out, part_mix


def kernel(x, norm_gain, w_in, lb_logits, conv_w, hgrn_norm_gain, conv_norm_gain, w_out, final_norm_gain, loss_target, m_norm_gain, m_w_in, m_lb_logits, m_conv_w, m_hgrn_norm_gain, m_conv_norm_gain, m_w_out, m_final_norm_gain, v_norm_gain, v_w_in, v_lb_logits, v_conv_w, v_hgrn_norm_gain, v_conv_norm_gain, v_w_out, v_final_norm_gain):
    k = 2 * lax.axis_index("x") + lax.axis_index("y")
    kidx = jnp.reshape(k, (1,)).astype(jnp.int32)
    row = lambda a: a.reshape(1, D_MODEL)
    taps = lambda a: a.reshape(1, 3 * HEAD)
    h, proj, wg, cw = _gather_proj(kidx, x[0], norm_gain, w_in, taps(conv_w))
    dproj, dx2, gwo, part_out, part_mix = _local_step(
        x[0], loss_target[0], proj, lb_logits, cw, hgrn_norm_gain, conv_norm_gain, w_out, row(final_norm_gain))
    grad_x, g_w_in, g_w_out, tot = _bwd_tail(kidx, h, dproj, wg, gwo, x[0], dx2, norm_gain, part_out, part_mix)

    d_w_in, nm_w_in, nv_w_in = _adamw(w_in[0], g_w_in, m_w_in[0], v_w_in[0], "adamw_w_in")
    d_w_out, nm_w_out, nv_w_out = _adamw(w_out[0], g_w_out, m_w_out[0], v_w_out[0], "adamw_w_out")
    (loss, (g_norm_gain, d_ng, nm_ng, nv_ng), (g_final, d_fg, nm_fg, nv_fg), (g_hgrn, d_hg, nm_hg, nv_hg),
     (g_convn, d_cg, nm_cg, nv_cg), (g_lb, d_lb, nm_lb, nv_lb), (g_conv_w, d_cw, nm_cw, nv_cw)) = _adamw_small(
        tot, [(norm_gain, m_norm_gain, v_norm_gain),
              (row(final_norm_gain), row(m_final_norm_gain), row(v_final_norm_gain)),
              (hgrn_norm_gain, m_hgrn_norm_gain, v_hgrn_norm_gain),
              (conv_norm_gain, m_conv_norm_gain, v_conv_norm_gain),
              (lb_logits, m_lb_logits, v_lb_logits),
              (taps(conv_w), taps(m_conv_w), taps(v_conv_w))])
    flat = lambda a: a.reshape(D_MODEL)
    untap = lambda a: a.reshape(1, 3, HEAD)
    return (loss.reshape(()), grad_x[None],
            g_norm_gain, g_w_in[None], g_lb, untap(g_conv_w), g_hgrn, g_convn, g_w_out[None], flat(g_final),
            d_ng, d_w_in[None], d_lb, untap(d_cw), d_hg, d_cg, d_w_out[None], flat(d_fg),
            nm_ng, nm_w_in[None], nm_lb, untap(nm_cw), nm_hg, nm_cg, nm_w_out[None], flat(nm_fg),
            nv_ng, nv_w_in[None], nv_lb, untap(nv_cw), nv_hg, nv_cg, nv_w_out[None], flat(nv_fg))
```

```python
import jax
import jax.numpy as jnp
import numpy as np
from jax import lax
from jax.experimental import pallas as pl
from jax.experimental.pallas import tpu as pltpu

F32 = jnp.float32
BF16 = jnp.bfloat16
MESH = pl.DeviceIdType.MESH

SEQ = 2048
D_MODEL = 1024
D_HGRN = 512
D_CONV = 512
HEAD = 128
N_HEADS = 4
CHUNK = 64
CONV_GROUP = 64
N_SHARD = 4
SHARD_COLS = 1024
WO_ROWS = 256
EPS = 1e-6
TB = 256
NCB = TB // CHUNK
N_CHUNKS = SEQ // CHUNK
N_DEV = 8

ADAM_LR = 0.001
ADAM_B1 = 0.9
ADAM_B2 = 0.999
ADAM_EPS = 1e-08
ADAM_WD = 0.01
ADAM_STEP = 10

VMEM_LIMIT = 56 * 1024 * 1024


def _dot(a, b):
    return jnp.dot(a, b, preferred_element_type=F32)


def _dot_nt(a, b):
    return lax.dot_general(a, b, (((1,), (1,)), ((), ())), preferred_element_type=F32)


def _dot_tn(a, b):
    return lax.dot_general(a, b, (((0,), (0,)), ((), ())), preferred_element_type=F32)


def _split_bf16(x, n):
    parts = []
    r = x
    for _ in range(n):
        p = r.astype(BF16)
        parts.append(p)
        r = r - p.astype(F32)
    return parts


def _exact_left(m, x, n=3):
    acc = None
    for p in _split_bf16(x, n):
        t = _dot(m, p)
        acc = t if acc is None else acc + t
    return acc


def _exact_left_many(m, xs, n=3):
    parts = [_split_bf16(x, n) for x in xs]
    accs = [None] * len(xs)
    for i in range(n):
        for j in range(len(xs)):
            t = _dot(m, parts[j][i])
            accs[j] = t if accs[j] is None else accs[j] + t
    return accs


def _group_mean_many(xs, gmat, n=2):
    parts = [_split_bf16(x, n) for x in xs]
    accs = [None] * len(xs)
    for i in range(n):
        for j in range(len(xs)):
            t = _dot(parts[j][i], gmat)
            accs[j] = t if accs[j] is None else accs[j] + t
    return accs


def _group_mean(x, gmat, n=2):
    w = gmat.shape[0]
    outs = []
    for c0 in range(0, x.shape[1], w):
        acc = None
        for p in _split_bf16(x[:, c0:c0 + w], n):
            t = _dot(p, gmat)
            acc = t if acc is None else acc + t
        outs.append(acc)
    return jnp.concatenate(outs, axis=1)


def _sigmoid(x):
    return 1.0 / (1.0 + jnp.exp(-x))


def _lower_bound(lbl):
    l0 = lbl[0:1, :]
    l1 = lbl[1:2, :]
    m = jnp.maximum(l0, l1)
    e0 = jnp.exp(l0 - m)
    e1 = jnp.exp(l1 - m)
    return e0 / (e0 + e1)


def _tri(lower):
    r = lax.broadcasted_iota(jnp.int32, (CHUNK, CHUNK), 0)
    c = lax.broadcasted_iota(jnp.int32, (CHUNK, CHUNK), 1)
    return jnp.where((c <= r) if lower else (c >= r), 1.0, 0.0).astype(BF16)


def _causal():
    r = lax.broadcasted_iota(jnp.int32, (CHUNK, CHUNK), 0)
    c = lax.broadcasted_iota(jnp.int32, (CHUNK, CHUNK), 1)
    return c <= r


def _shift_down(x, sh, prev_tail):
    r = pltpu.roll(x, sh, 0)
    pt = pltpu.roll(prev_tail, sh, 0)
    rows = lax.broadcasted_iota(jnp.int32, prev_tail.shape, 0)
    top = jnp.where(rows < sh, pt, r[0:8])
    return jnp.concatenate([top, r[8:]], axis=0)


def _shift_up(x, sh, next_head):
    n = x.shape[0]
    r = pltpu.roll(x, n - sh, 0)
    nh = pltpu.roll(next_head, 8 - sh, 0)
    rows = lax.broadcasted_iota(jnp.int32, next_head.shape, 0)
    bot = jnp.where(rows >= 8 - sh, nh, r[n - 8:])
    return jnp.concatenate([r[:n - 8], bot], axis=0)


def _group_matrix(width, group):
    r = np.arange(width)[:, None] // group
    c = np.arange(width)[None, :] // group
    return jnp.asarray(np.where(r == c, 1.0 / group, 0.0), dtype=BF16)


TP = 512
SEM_W, SEM_CW, SEM_W_FWD, N_SEM = 0, 3, 6, 9


def _gather_proj(kidx, x2d, g1, w_in, conv_w):
    half_w = D_MODEL // 2
    nt = SEQ // TP

    def body(k_ref, x_ref, g_ref, w_ref, cw_ref, h_ref, p_ref, wg_out, cwg_out,
             wg_v, cwg_v, send_sems, recv_sems, out_sems):
        s, t = pl.program_id(0), pl.program_id(1)
        x, y, c = lax.axis_index("x"), lax.axis_index("y"), lax.axis_index("c")
        k = 2 * x + y
        sibling = (x, y, 1 - c)
        chips = [(1 - x, y), (x, 1 - y), (1 - x, 1 - y)]
        kjs = [2 * cx + cy for cx, cy in chips]

        def w_half(kk, cc):
            return wg_v.at[kk, pl.ds(cc * half_w, half_w), :]

        def cw_of(kk):
            return cwg_v.at[:, pl.ds(pl.multiple_of(kk * HEAD, HEAD), HEAD)]

        def copy(sem, ref, to):
            return pltpu.make_async_remote_copy(
                src_ref=ref, dst_ref=ref, send_sem=send_sems.at[sem], recv_sem=recv_sems.at[sem],
                device_id=to, device_id_type=MESH)

        def at_step(sv, tv):
            return pl.when((s == sv) & (t == tv))

        w_direct = [copy(SEM_W + j, w_half(k, c), (*chip, c)) for j, chip in enumerate(chips)]
        cw_direct = [copy(SEM_CW + j, cw_of(k), (*chip, c)) for j, chip in enumerate(chips)]
        w_passed = [copy(SEM_W_FWD + j, w_half(kj, c), sibling) for j, kj in enumerate(kjs)]
        stores = ([pltpu.make_async_copy(wg_v.at[kk], wg_out.at[kk], out_sems.at[i])
                   for i, kk in enumerate([k] + kjs)]
                  + [pltpu.make_async_copy(cwg_v, cwg_out, out_sems.at[4])])

        @at_step(0, 0)
        def _():
            wg_v[k] = w_ref[0].astype(BF16)
            mine = pl.ds(pl.multiple_of(k * HEAD, HEAD), HEAD)
            cwg_v[:, mine] = jnp.zeros((8, HEAD), F32)
            for tap in range(3):
                cwg_v[tap:tap + 1, mine] = cw_ref[:, tap * HEAD:(tap + 1) * HEAD]
            w_direct[0].start()
            w_direct[1].start()
            for cp in cw_direct:
                cp.start()
            stores[0].start()

        @at_step(1, 0)
        def _():
            for j in range(2):
                copy(SEM_W + j, w_half(kjs[j], c), sibling).wait_recv()
                w_passed[j].start()
            w_direct[2].start()
            copy(SEM_W_FWD, w_half(kjs[0], 1 - c), sibling).wait_recv()
            stores[1].start()

        @at_step(2, 0)
        def _():
            copy(SEM_W_FWD + 1, w_half(kjs[1], 1 - c), sibling).wait_recv()
            stores[2].start()

        @at_step(3, 0)
        def _():
            copy(SEM_W + 2, w_half(kjs[2], c), sibling).wait_recv()
            w_passed[2].start()
            copy(SEM_W_FWD + 2, w_half(kjs[2], 1 - c), sibling).wait_recv()
            stores[3].start()

        rows = pl.ds(pl.multiple_of(t * TP, TP), TP)

        @pl.when(s == 0)
        def _():
            xv = x_ref[...]
            r = lax.rsqrt(jnp.mean(xv * xv, axis=-1, keepdims=True) + EPS)
            h_ref[rows, :] = (xv * r * g_ref[...]).astype(BF16)

        js = k ^ (((s & 1) << 1) | (s >> 1))
        p_ref[...] = _dot(h_ref[rows, :], wg_v[js])

        @at_step(N_SHARD - 1, nt - 1)
        def _():
            for j in range(3):
                copy(SEM_CW + j, cw_of(kjs[j]), sibling).wait_recv()
            stores[4].start()
            for cp in w_direct + cw_direct + w_passed:
                cp.wait_send()
            for st in stores:
                st.wait()

    def x_map(s, t, kr):
        return (jnp.where(s == 0, t, nt - 1), 0)

    def p_map(s, t, kr):
        return (t, kr[0] ^ (((s & 1) << 1) | (s >> 1)))

    hbm = pl.BlockSpec(memory_space=pl.ANY)
    grid_spec = pltpu.PrefetchScalarGridSpec(
        num_scalar_prefetch=1, grid=(N_SHARD, nt),
        in_specs=[pl.BlockSpec((TP, D_MODEL), x_map),
                  pl.BlockSpec((1, D_MODEL), lambda s, t, kr: (0, 0)),
                  pl.BlockSpec((1, D_MODEL, SHARD_COLS), lambda s, t, kr: (0, 0, 0)),
                  pl.BlockSpec((1, 3 * HEAD), lambda s, t, kr: (0, 0))],
        out_specs=(pl.BlockSpec((SEQ, D_MODEL), lambda s, t, kr: (0, 0)),
                   pl.BlockSpec((TP, SHARD_COLS), p_map), hbm, hbm),
        scratch_shapes=[pltpu.VMEM((N_SHARD, D_MODEL, SHARD_COLS), BF16),
                        pltpu.VMEM((8, D_CONV), F32),
                        pltpu.SemaphoreType.DMA((N_SEM,)), pltpu.SemaphoreType.DMA((N_SEM,)),
                        pltpu.SemaphoreType.DMA((5,))])
    return pl.pallas_call(
        body, name="gather_proj", grid_spec=grid_spec,
        out_shape=(jax.ShapeDtypeStruct((SEQ, D_MODEL), BF16),
                   jax.ShapeDtypeStruct((SEQ, N_SHARD * SHARD_COLS), F32),
                   jax.ShapeDtypeStruct((N_SHARD, D_MODEL, SHARD_COLS), BF16),
                   jax.ShapeDtypeStruct((8, D_CONV), F32)),
        compiler_params=pltpu.CompilerParams(dimension_semantics=("arbitrary", "arbitrary"),
                                             vmem_limit_bytes=VMEM_LIMIT),
    )(kidx, x2d, g1, w_in, conv_w)


def _mix_fwd(proj, lb_logits, cw, ga, gcn, g64, w_out):
    half_o = WO_ROWS // 2
    nblk = SEQ // TB

    def body(p_ref, lbl_ref, cw_ref, ga_ref, gcn_ref, g64_ref, wo_ref,
             mixed_ref, o_ref, cv_ref, sto_ref, sg_ref, b_ref, wog_out,
             st_ref, tail_ref, wog_v, send_sems, recv_sems, out_sem):
        i = pl.program_id(0)
        x, y, c = lax.axis_index("x"), lax.axis_index("y"), lax.axis_index("c")
        k = 2 * x + y
        sibling = (x, y, 1 - c)
        chips = [(1 - x, y), (x, 1 - y), (1 - x, 1 - y)]
        kjs = [2 * cx + cy for cx, cy in chips]

        def wo_half(kk, cc):
            return wog_v.at[kk, pl.ds(cc * half_o, half_o), :]

        def copy(sem, ref, to):
            return pltpu.make_async_remote_copy(
                src_ref=ref, dst_ref=ref, send_sem=send_sems.at[sem], recv_sem=recv_sems.at[sem],
                device_id=to, device_id_type=MESH)

        wo_direct = [copy(j, wo_half(k, c), (*chip, c)) for j, chip in enumerate(chips)]
        wo_passed = [copy(3 + j, wo_half(kj, c), sibling) for j, kj in enumerate(kjs)]
        wo_store = pltpu.make_async_copy(wog_v, wog_out, out_sem.at[0])

        @pl.when(i == 0)
        def _():
            st_ref[...] = jnp.zeros_like(st_ref)
            tail_ref[...] = jnp.zeros_like(tail_ref)
            wog_v[k] = wo_ref[0].astype(BF16)
            for cp in wo_direct:
                cp.start()

        @pl.when(i == nblk - 2)
        def _():
            for j in range(3):
                copy(j, wo_half(kjs[j], c), sibling).wait_recv()
                wo_passed[j].start()

        lb = _lower_bound(lbl_ref[...])
        tri = _tri(True)
        causal = _causal()
        g64m = g64_ref[...]
        heads = range(N_HEADS)
        cs = [slice(hd * HEAD, (hd + 1) * HEAD) for hd in heads]
        col = lambda base, hd: slice(base + hd * HEAD, base + (hd + 1) * HEAD)
        for n in range(NCB):
            sl = pl.ds(n * CHUNK, CHUNK)
            sg = [_sigmoid(p_ref[sl, col(512, hd)]) for hd in heads]
            f = [lb[:, cs[hd]] + (1.0 - lb[:, cs[hd]]) * sg[hd] for hd in heads]
            bc = _exact_left_many(tri, [jnp.log(f[hd]) for hd in heads])
            for hd in heads:
                sg_ref[sl, cs[hd]] = sg[hd]
                b_ref[sl, cs[hd]] = bc[hd]
            g = [bc[hd][CHUNK - 1:CHUNK, :] for hd in heads]
            qd = [(p_ref[sl, col(0, hd)] * jnp.exp(bc[hd])).astype(BF16) for hd in heads]
            ki = [((1.0 - f[hd]) * jnp.exp(-bc[hd])).astype(BF16) for hd in heads]
            ke = [((1.0 - f[hd]) * jnp.exp(g[hd] - bc[hd])).astype(BF16) for hd in heads]
            vb = [p_ref[sl, col(1024, hd)].astype(BF16) for hd in heads]
            st = [st_ref[hd] for hd in heads]
            for hd in heads:
                sto_ref[n, hd] = st[hd]
            scm = [_dot_nt(qd[hd], ki[hd]) for hd in heads]
            inter = [_dot_nt(qd[hd], st[hd].astype(BF16)) for hd in heads]
            upd = [_dot_tn(vb[hd], ke[hd]) for hd in heads]
            intra = [_dot(jnp.where(causal, scm[hd], 0.0).astype(BF16), vb[hd]) for hd in heads]
            for hd in heads:
                st_ref[hd] = st[hd] * jnp.exp(g[hd]) + upd[hd]
                o = intra[hd] + inter[hd]
                o_ref[sl, cs[hd]] = o
                ra = lax.rsqrt(jnp.mean(o * o, axis=-1, keepdims=True) + EPS)
                za = p_ref[sl, col(1536, hd)]
                mixed_ref[sl, cs[hd]] = (o * ra * ga_ref[:, cs[hd]] * (za * _sigmoid(za))).astype(BF16)
            yb = []
            for hd in heads:
                cu = p_ref[sl, col(3072, hd)] * p_ref[sl, col(2048, hd)]
                tail = tail_ref[:, cs[hd]]
                cv = (cw_ref[0:1, cs[hd]] * _shift_down(cu, 2, tail) + cw_ref[1:2, cs[hd]] * _shift_down(cu, 1, tail)
                      + cw_ref[2:3, cs[hd]] * cu)
                tail_ref[:, cs[hd]] = cu[CHUNK - 8:, :]
                cv_ref[sl, cs[hd]] = cv
                yb.append(p_ref[sl, col(2560, hd)] * cv)
            ms = _group_mean_many([y * y for y in yb], g64m)
            for hd in heads:
                rb = lax.rsqrt(ms[hd] + EPS)
                zb = p_ref[sl, col(3584, hd)]
                mixed_ref[sl, col(512, hd)] = (yb[hd] * rb * gcn_ref[:, cs[hd]] * (zb * _sigmoid(zb))).astype(BF16)

        @pl.when(i == nblk - 1)
        def _():
            for j in range(3):
                copy(3 + j, wo_half(kjs[j], 1 - c), sibling).wait_recv()
            wo_store.start()
            for cp in wo_direct + wo_passed:
                cp.wait_send()
            wo_store.wait()

    row = lambda w: pl.BlockSpec((1, w), lambda i: (0, 0))
    return pl.pallas_call(
        body, name="mix_fwd", grid=(nblk,),
        out_shape=(jax.ShapeDtypeStruct((SEQ, D_MODEL), BF16),
                   jax.ShapeDtypeStruct((SEQ, D_HGRN), F32),
                   jax.ShapeDtypeStruct((SEQ, D_CONV), F32),
                   jax.ShapeDtypeStruct((N_CHUNKS, N_HEADS, HEAD, HEAD), F32),
                   jax.ShapeDtypeStruct((SEQ, D_HGRN), F32),
                   jax.ShapeDtypeStruct((SEQ, D_HGRN), F32),
                   jax.ShapeDtypeStruct((N_SHARD, WO_ROWS, D_MODEL), BF16)),
        in_specs=[pl.BlockSpec((TB, 4096), lambda i: (i, 0)),
                  pl.BlockSpec((2, D_HGRN), lambda i: (0, 0)),
                  pl.BlockSpec((8, D_CONV), lambda i: (0, 0)),
                  row(D_HGRN), row(D_CONV),
                  pl.BlockSpec((HEAD, HEAD), lambda i: (0, 0)),
                  pl.BlockSpec((1, WO_ROWS, D_MODEL), lambda i: (0, 0, 0))],
        out_specs=(pl.BlockSpec((TB, D_MODEL), lambda i: (i, 0)),
                   pl.BlockSpec((TB, D_HGRN), lambda i: (i, 0)),
                   pl.BlockSpec((TB, D_CONV), lambda i: (i, 0)),
                   pl.BlockSpec((NCB, N_HEADS, HEAD, HEAD), lambda i: (i, 0, 0, 0)),
                   pl.BlockSpec((TB, D_HGRN), lambda i: (i, 0)),
                   pl.BlockSpec((TB, D_HGRN), lambda i: (i, 0)),
                   pl.BlockSpec(memory_space=pl.ANY)),
        scratch_shapes=[pltpu.VMEM((N_HEADS, HEAD, HEAD), F32), pltpu.VMEM((8, D_CONV), F32),
                        pltpu.VMEM((N_SHARD, WO_ROWS, D_MODEL), BF16),
                        pltpu.SemaphoreType.DMA((6,)), pltpu.SemaphoreType.DMA((6,)),
                        pltpu.SemaphoreType.DMA((1,))],
        compiler_params=pltpu.CompilerParams(dimension_semantics=("arbitrary",), vmem_limit_bytes=VMEM_LIMIT),
    )(proj, lb_logits, cw, ga, gcn, g64, w_out)


def _out_loss(x2d, mixed, wog, gf, tgt):
    def body(x_ref, m_ref, wo_ref, gf_ref, t_ref, dx2_ref, dm_ref, gwo_ref, part_ref, acc_ref):
        i = pl.program_id(0)

        @pl.when(i == 0)
        def _():
            acc_ref[...] = jnp.zeros_like(acc_ref)
            part_ref[...] = jnp.zeros_like(part_ref)

        mixed_b = m_ref[...]
        x2 = x_ref[...] + _dot(mixed_b, wo_ref[...])
        r2 = lax.rsqrt(jnp.mean(x2 * x2, axis=-1, keepdims=True) + EPS)
        n2 = x2 * r2
        gfv = gf_ref[...]
        err = n2 * gfv - t_ref[...]
        loss = 0.5 * jnp.sum(jnp.mean(err * err, axis=-1, keepdims=True), axis=0, keepdims=True)
        dy = err * (1.0 / D_MODEL)
        part_ref[1:2, :] += jnp.sum(dy * n2, axis=0, keepdims=True)
        part_ref[7:8, :] += jnp.broadcast_to(loss, (1, D_MODEL))
        dn = dy * gfv
        dx2 = r2 * (dn - n2 * jnp.mean(dn * n2, axis=-1, keepdims=True))
        dx2_ref[...] = dx2
        dx2_b = dx2.astype(BF16)
        dm_ref[...] = _dot_nt(dx2_b, wo_ref[...])
        acc_ref[...] += _dot_tn(mixed_b, dx2_b)

        @pl.when(i == pl.num_programs(0) - 1)
        def _():
            gwo_ref[...] = acc_ref[...].astype(BF16)

    blk = lambda: pl.BlockSpec((TP, D_MODEL), lambda i: (i, 0))
    return pl.pallas_call(
        body, name="out_loss", grid=(SEQ // TP,),
        out_shape=(jax.ShapeDtypeStruct((SEQ, D_MODEL), F32),
                   jax.ShapeDtypeStruct((SEQ, D_MODEL), F32),
                   jax.ShapeDtypeStruct((D_MODEL, D_MODEL), BF16),
                   jax.ShapeDtypeStruct((8, D_MODEL), F32)),
        in_specs=[blk(), blk(), pl.BlockSpec((D_MODEL, D_MODEL), lambda i: (0, 0)),
                  pl.BlockSpec((1, D_MODEL), lambda i: (0, 0)), blk()],
        out_specs=(blk(), blk(), pl.BlockSpec((D_MODEL, D_MODEL), lambda i: (0, 0)),
                   pl.BlockSpec((8, D_MODEL), lambda i: (0, 0))),
        scratch_shapes=[pltpu.VMEM((D_MODEL, D_MODEL), F32)],
        compiler_params=pltpu.CompilerParams(dimension_semantics=("arbitrary",), vmem_limit_bytes=VMEM_LIMIT),
    )(x2d, mixed, wog, gf, tgt)


def _mix_bwd(proj, o, cv, states, sg, b, dmixed, lb_logits, cw, ga, gcn, g64):
    nblk = SEQ // TB

    def body(p_ref, o_ref, cv_ref, st_ref, sg_ref, b_ref, dm_ref, lbl_ref, cw_ref, ga_ref, gcn_ref, g64_ref,
             dp_ref, part_ref, dst_ref, head_ref, dlb_ref):
        i = pl.program_id(0)

        @pl.when(i == 0)
        def _():
            dst_ref[...] = jnp.zeros_like(dst_ref)
            head_ref[...] = jnp.zeros_like(head_ref)
            part_ref[...] = jnp.zeros_like(part_ref)
            dlb_ref[...] = jnp.zeros_like(dlb_ref)

        lb = _lower_bound(lbl_ref[...])
        triu = _tri(False)
        causal = _causal()
        g64m = g64_ref[...]
        rowsum = lambda a: jnp.sum(a, axis=0, keepdims=True)
        heads = range(N_HEADS)
        cs = [slice(hd * HEAD, (hd + 1) * HEAD) for hd in heads]
        col = lambda base, hd: slice(base + hd * HEAD, base + (hd + 1) * HEAD)
        for n in reversed(range(NCB)):
            sl = pl.ds(n * CHUNK, CHUNK)
            cvv = [cv_ref[sl, cs[hd]] for hd in heads]
            gb = [p_ref[sl, col(2560, hd)] for hd in heads]
            yb = [gb[hd] * cvv[hd] for hd in heads]
            ms = _group_mean_many([y * y for y in yb], g64m)
            rb, nb, dnb = [], [], []
            for hd in heads:
                rb.append(lax.rsqrt(ms[hd] + EPS))
                nb.append(yb[hd] * rb[hd])
                zb = p_ref[sl, col(3584, hd)]
                sgb = _sigmoid(zb)
                dmb = dm_ref[sl, col(512, hd)]
                gcv = gcn_ref[:, cs[hd]]
                part_ref[2:3, col(512, hd)] += rowsum(dmb * nb[hd] * (zb * sgb))
                dp_ref[sl, col(3584, hd)] = (dmb * nb[hd] * gcv * (sgb * (1.0 + zb * (1.0 - sgb)))).astype(BF16)
                dnb.append(dmb * gcv * (zb * sgb))
            mdn = _group_mean_many([dnb[hd] * nb[hd] for hd in heads], g64m)
            for hd in heads:
                dyb = rb[hd] * (dnb[hd] - nb[hd] * mdn[hd])
                dp_ref[sl, col(2560, hd)] = (dyb * cvv[hd]).astype(BF16)
                dcv = dyb * gb[hd]
                head = head_ref[:, cs[hd]]
                dcv1 = _shift_up(dcv, 1, head)
                dcv2 = _shift_up(dcv, 2, head)
                head_ref[:, cs[hd]] = dcv[0:8, :]
                u = p_ref[sl, col(2048, hd)]
                gc = p_ref[sl, col(3072, hd)]
                cu = gc * u
                part_ref[4:5, cs[hd]] += rowsum(dcv2 * cu)
                part_ref[5:6, cs[hd]] += rowsum(dcv1 * cu)
                part_ref[6:7, cs[hd]] += rowsum(dcv * cu)
                dcu = cw_ref[2:3, cs[hd]] * dcv + cw_ref[1:2, cs[hd]] * dcv1 + cw_ref[0:1, cs[hd]] * dcv2
                dp_ref[sl, col(3072, hd)] = (dcu * u).astype(BF16)
                dp_ref[sl, col(2048, hd)] = (dcu * gc).astype(BF16)
            do_b = []
            for hd in heads:
                ov = o_ref[sl, cs[hd]]
                ra = lax.rsqrt(jnp.mean(ov * ov, axis=-1, keepdims=True) + EPS)
                na = ov * ra
                za = p_ref[sl, col(1536, hd)]
                sga = _sigmoid(za)
                dma = dm_ref[sl, cs[hd]]
                gav = ga_ref[:, cs[hd]]
                part_ref[2:3, cs[hd]] += rowsum(dma * na * (za * sga))
                dp_ref[sl, col(1536, hd)] = (dma * na * gav * (sga * (1.0 + za * (1.0 - sga)))).astype(BF16)
                dna = dma * gav * (za * sga)
                do_b.append((ra * (dna - na * jnp.mean(dna * na, axis=-1, keepdims=True))).astype(BF16))
            s = [sg_ref[sl, cs[hd]] for hd in heads]
            f = [lb[:, cs[hd]] + (1.0 - lb[:, cs[hd]]) * s[hd] for hd in heads]
            bc = [b_ref[sl, cs[hd]] for hd in heads]
            g = [bc[hd][CHUNK - 1:CHUNK, :] for hd in heads]
            eb = [jnp.exp(bc[hd]) for hd in heads]
            enb = [jnp.exp(-bc[hd]) for hd in heads]
            eg = [jnp.exp(g[hd] - bc[hd]) for hd in heads]
            dec = [jnp.exp(g[hd]) for hd in heads]
            qd = [p_ref[sl, cs[hd]] * eb[hd] for hd in heads]
            ki = [(1.0 - f[hd]) * enb[hd] for hd in heads]
            ke = [(1.0 - f[hd]) * eg[hd] for hd in heads]
            qd_b = [a.astype(BF16) for a in qd]
            ki_b = [a.astype(BF16) for a in ki]
            ke_b = [a.astype(BF16) for a in ke]
            vb = [p_ref[sl, col(1024, hd)].astype(BF16) for hd in heads]
            st = [st_ref[n, hd] for hd in heads]
            dst = [dst_ref[hd] for hd in heads]
            st_b = [a.astype(BF16) for a in st]
            dst_b = [a.astype(BF16) for a in dst]
            scm = [_dot_nt(qd_b[hd], ki_b[hd]) for hd in heads]
            amm = [_dot_nt(do_b[hd], vb[hd]) for hd in heads]
            dqd2 = [_dot(do_b[hd], st_b[hd]) for hd in heads]
            dke = [_dot(vb[hd], dst_b[hd]) for hd in heads]
            dv2 = [_dot_nt(ke_b[hd], dst_b[hd]) for hd in heads]
            dsu = [_dot_tn(do_b[hd], qd_b[hd]) for hd in heads]
            sc = [jnp.where(causal, scm[hd], 0.0).astype(BF16) for hd in heads]
            am = [jnp.where(causal, amm[hd], 0.0).astype(BF16) for hd in heads]
            dqd1 = [_dot(am[hd], ki_b[hd]) for hd in heads]
            dki = [_dot_tn(am[hd], qd_b[hd]) for hd in heads]
            dv1 = [_dot_tn(sc[hd], do_b[hd]) for hd in heads]
            db, dgv = [], []
            for hd in heads:
                dqd = dqd1[hd] + dqd2[hd]
                ddec = rowsum(dst[hd] * st[hd])
                dst_ref[hd] = dst[hd] * dec[hd] + dsu[hd]
                dp_ref[sl, cs[hd]] = (dqd * eb[hd]).astype(BF16)
                dp_ref[sl, col(1024, hd)] = (dv1[hd] + dv2[hd]).astype(BF16)
                db.append(dqd * qd[hd] - dki[hd] * ki[hd] - dke[hd] * ke[hd])
                dgv.append(rowsum(dke[hd] * ke[hd]) + ddec * dec[hd])
            rc = _exact_left_many(triu, db, 2)
            for hd in heads:
                df = (rc[hd] + dgv[hd]) / f[hd] - (dki[hd] * enb[hd] + dke[hd] * eg[hd])
                dlb_ref[:, cs[hd]] += rowsum(df * (1.0 - s[hd]))
                dp_ref[sl, col(512, hd)] = (df * (1.0 - lb[:, cs[hd]]) * s[hd] * (1.0 - s[hd])).astype(BF16)

        @pl.when(i == nblk - 1)
        def _():
            row = dlb_ref[...] * lb * (1.0 - lb)
            part_ref[3:4, 0:D_HGRN] = row
            part_ref[3:4, D_HGRN:] = -row

    rev = lambda w: pl.BlockSpec((TB, w), lambda i: (nblk - 1 - i, 0))
    row = lambda w: pl.BlockSpec((1, w), lambda i: (0, 0))
    return pl.pallas_call(
        body, name="mix_bwd", grid=(nblk,),
        out_shape=(jax.ShapeDtypeStruct((SEQ, 4096), BF16),
                   jax.ShapeDtypeStruct((8, D_MODEL), F32)),
        in_specs=[rev(4096), rev(D_HGRN), rev(D_CONV),
                  pl.BlockSpec((NCB, N_HEADS, HEAD, HEAD), lambda i: (nblk - 1 - i, 0, 0, 0)),
                  rev(D_HGRN), rev(D_HGRN), rev(D_MODEL),
                  pl.BlockSpec((2, D_HGRN), lambda i: (0, 0)),
                  pl.BlockSpec((8, D_CONV), lambda i: (0, 0)),
                  row(D_HGRN), row(D_CONV),
                  pl.BlockSpec((HEAD, HEAD), lambda i: (0, 0))],
        out_specs=(rev(4096), pl.BlockSpec((8, D_MODEL), lambda i: (0, 0))),
        scratch_shapes=[pltpu.VMEM((N_HEADS, HEAD, HEAD), F32), pltpu.VMEM((8, D_CONV), F32),
                        pltpu.VMEM((1, D_HGRN), F32)],
        compiler_params=pltpu.CompilerParams(dimension_semantics=("arbitrary",), vmem_limit_bytes=VMEM_LIMIT),
    )(proj, o, cv, states, sg, b, dmixed, lb_logits, cw, ga, gcn, g64)


TT = 1024
TX = 256
(SEM_D2D, SEM_D2D_O, SEM_ICI, SEM_ICI_O, SEM_FIN, SEM_FIN_O, SEM_SMALL, N_SEM_TAIL) = 0, 4, 5, 8, 11, 12, 12, 20


def _bwd_tail(kidx, h, dproj, wg, gwo, x2d, dx2, g1, small_a, small_b):
    hw = D_MODEL // 2
    ho = WO_ROWS // 2
    nt = SEQ // TT
    n_steps = N_SHARD + SEQ // TX // nt

    def body(k_ref, h_ref, dp_ref, w_ref, gwo_ref, x_ref, dx2_ref, g_ref, sm_ref, smb_ref,
             gx_ref, gw_out, gwo_out, osm_ref,
             acc, dh, sendbuf, keep, sibrcv, rcv, sib_o, p_o, rcv_o, res_o, sm_buf, dng,
             send_sems, recv_sems, out_sems):
        s, t = pl.program_id(0), pl.program_id(1)
        x, y, c = lax.axis_index("x"), lax.axis_index("y"), lax.axis_index("c")
        k = 2 * x + y
        me = 4 * x + 2 * y + c
        sibling = (x, y, 1 - c)
        chips = [(1 - x, 1 - y), (1 - x, y), (x, 1 - y)]
        kjs = [2 * cx + cy for cx, cy in chips]
        mine = pl.ds(pl.multiple_of(c * hw, hw), hw)
        other = pl.ds(pl.multiple_of((1 - c) * hw, hw), hw)
        mine_o = pl.ds(pl.multiple_of(c * ho, ho), ho)
        other_o = pl.ds(pl.multiple_of((1 - c) * ho, ho), ho)

        def copy(sem, src, dst, to):
            return pltpu.make_async_remote_copy(
                src_ref=src, dst_ref=dst, send_sem=send_sems.at[sem], recv_sem=recv_sems.at[sem],
                device_id=to, device_id_type=MESH)

        def at_step(sv, tv):
            return pl.when((s == sv) & (t == tv))

        def at_norm_block(b):
            return at_step(N_SHARD + b // nt, b % nt)

        d2d = [copy(SEM_D2D + sv, sendbuf.at[sv], sibrcv.at[sv], sibling) for sv in range(N_SHARD)]
        d2d_o = copy(SEM_D2D_O, gwo_ref.at[:, other_o, :], sib_o, sibling)
        ici = [copy(SEM_ICI + sv, keep.at[sv], rcv.at[sv], (*chips[sv], c)) for sv in range(3)]
        ici_o = [copy(SEM_ICI_O + sv, p_o.at[kjs[sv]], rcv_o.at[sv], (*chips[sv], c)) for sv in range(3)]
        fin = copy(SEM_FIN, acc.at[mine, :], acc.at[mine, :], sibling)
        fin_o = copy(SEM_FIN_O, res_o.at[mine_o, :], res_o.at[mine_o, :], sibling)
        smalls = [copy(SEM_SMALL + m, sm_buf.at[me], sm_buf.at[me],
                       (x ^ (m >> 2), y ^ ((m >> 1) & 1), c ^ (m & 1))) for m in range(1, N_DEV)]
        store_w = pltpu.make_async_copy(acc, gw_out, out_sems.at[0])
        store_o = pltpu.make_async_copy(res_o, gwo_out, out_sems.at[1])

        @at_step(0, 0)
        def _():
            d2d_o.start()

        @at_step(0, 1)
        def _():
            d2d_o.wait_recv()
            for j in range(N_SHARD):
                p_o[j] = (gwo_ref[j, mine_o, :].astype(F32) + sib_o[j].astype(F32)).astype(BF16)
            res_o[mine_o, :] = gwo_ref[k, mine_o, :].astype(F32) + sib_o[k].astype(F32)
            for cp in ici_o:
                cp.start()

        rows = pl.ds(pl.multiple_of(t * TT, TT), TT)

        @pl.when(s < N_SHARD)
        def _():
            dpb = dp_ref[...]
            part = _dot_tn(h_ref[...], dpb)

            @pl.when(t == 0)
            def _():
                acc[...] = part

            @pl.when(t > 0)
            def _():
                acc[...] += part

            d = _dot_nt(dpb, w_ref[0])

            @pl.when(s == 0)
            def _():
                dh[rows, :] = d

            @pl.when(s > 0)
            def _():
                dh[rows, :] += d

        for sv in range(N_SHARD):
            @at_step(sv, nt - 1)
            def _(sv=sv):
                sendbuf[sv] = acc[other, :].astype(BF16)
                if sv < 3:
                    keep[sv] = acc[mine, :].astype(BF16)
                d2d[sv].start()

        for sv in range(3):
            @at_step(sv + 1, 0)
            def _(sv=sv):
                d2d[sv].wait_recv()
                keep[sv] = (keep[sv].astype(F32) + sibrcv[sv].astype(F32)).astype(BF16)
                ici[sv].start()

        @at_norm_block(0)
        def _():
            d2d[3].wait_recv()
            ici[0].wait_recv()
            acc[mine, :] += sibrcv[3].astype(F32) + rcv[0].astype(F32)

        @at_norm_block(1)
        def _():
            tot = res_o[mine_o, :]
            for sv in range(3):
                ici_o[sv].wait_recv()
                tot = tot + rcv_o[sv].astype(F32)
            res_o[mine_o, :] = tot
            fin_o.start()

        @at_norm_block(2)
        def _():
            ici[1].wait_recv()
            acc[mine, :] += rcv[1].astype(F32)

        @at_norm_block(0)
        def _():
            dng[...] = jnp.zeros_like(dng)

        @pl.when(s >= N_SHARD)
        def _():
            blk = (s - N_SHARD) * nt + t
            dhv = dh[pl.ds(pl.multiple_of(blk * TX, TX), TX), :]
            xv = x_ref[...]
            r = lax.rsqrt(jnp.mean(xv * xv, axis=-1, keepdims=True) + EPS)
            xn = xv * r
            dng[...] += jnp.sum(dhv * xn, axis=0, keepdims=True)
            dxn = dhv * g_ref[...]
            gx_ref[...] = dx2_ref[...] + r * (dxn - xn * jnp.mean(dxn * xn, axis=-1, keepdims=True))

        @at_step(n_steps - 1, nt - 1)
        def _():
            sm_buf[me] = sm_ref[...] + smb_ref[...]
            sm_buf[me, 0:1, :] = dng[...]
            for cp in smalls:
                cp.start()
            ici[2].wait_recv()
            acc[mine, :] += rcv[2].astype(F32)
            fin.start()
            for m in range(1, N_DEV):
                copy(SEM_SMALL + m, sm_buf.at[0], sm_buf.at[0], sibling).wait_recv()
            tot = sm_buf[0]
            for d in range(1, N_DEV):
                tot = tot + sm_buf[d]
            osm_ref[...] = tot
            fin_o.wait_recv()
            store_o.start()
            fin.wait_recv()
            store_w.start()
            for cp in d2d + [d2d_o] + ici + ici_o + [fin, fin_o] + smalls:
                cp.wait_send()
            store_o.wait()
            store_w.wait()

    def shard_of(s, kr):
        return kr[0] ^ (3 - jnp.minimum(s, 3))

    def tok(s, t):
        return jnp.where(s < N_SHARD, t, nt - 1)

    def blk_map(s, t, kr):
        return (jnp.where(s < N_SHARD, 0, (s - N_SHARD) * nt + t), 0)

    hbm = pl.BlockSpec(memory_space=pl.ANY)
    grid_spec = pltpu.PrefetchScalarGridSpec(
        num_scalar_prefetch=1, grid=(n_steps, nt),
        in_specs=[pl.BlockSpec((TT, D_MODEL), lambda s, t, kr: (tok(s, t), 0)),
                  pl.BlockSpec((TT, SHARD_COLS), lambda s, t, kr: (tok(s, t), shard_of(s, kr))),
                  pl.BlockSpec((1, D_MODEL, SHARD_COLS), lambda s, t, kr: (shard_of(s, kr), 0, 0)),
                  pl.BlockSpec((N_SHARD, WO_ROWS, D_MODEL), lambda s, t, kr: (0, 0, 0)),
                  pl.BlockSpec((TX, D_MODEL), blk_map),
                  pl.BlockSpec((TX, D_MODEL), blk_map),
                  pl.BlockSpec((1, D_MODEL), lambda s, t, kr: (0, 0)),
                  pl.BlockSpec((8, D_MODEL), lambda s, t, kr: (0, 0)),
                  pl.BlockSpec((8, D_MODEL), lambda s, t, kr: (0, 0))],
        out_specs=(pl.BlockSpec((TX, D_MODEL), blk_map), hbm, hbm,
                   pl.BlockSpec((8, D_MODEL), lambda s, t, kr: (0, 0))),
        scratch_shapes=[pltpu.VMEM((D_MODEL, SHARD_COLS), F32), pltpu.VMEM((SEQ, D_MODEL), F32),
                        pltpu.VMEM((N_SHARD, hw, SHARD_COLS), BF16), pltpu.VMEM((3, hw, SHARD_COLS), BF16),
                        pltpu.VMEM((N_SHARD, hw, SHARD_COLS), BF16), pltpu.VMEM((3, hw, SHARD_COLS), BF16),
                        pltpu.VMEM((N_SHARD, ho, D_MODEL), BF16), pltpu.VMEM((N_SHARD, ho, D_MODEL), BF16),
                        pltpu.VMEM((3, ho, D_MODEL), BF16), pltpu.VMEM((WO_ROWS, D_MODEL), F32),
                        pltpu.VMEM((N_DEV, 8, D_MODEL), F32), pltpu.VMEM((1, D_MODEL), F32),
                        pltpu.SemaphoreType.DMA((N_SEM_TAIL,)), pltpu.SemaphoreType.DMA((N_SEM_TAIL,)),
                        pltpu.SemaphoreType.DMA((2,))])
    return pl.pallas_call(
        body, name="bwd_tail", grid_spec=grid_spec,
        out_shape=(jax.ShapeDtypeStruct((SEQ, D_MODEL), F32),
                   jax.ShapeDtypeStruct((D_MODEL, SHARD_COLS), F32),
                   jax.ShapeDtypeStruct((WO_ROWS, D_MODEL), F32),
                   jax.ShapeDtypeStruct((8, D_MODEL), F32)),
        compiler_params=pltpu.CompilerParams(dimension_semantics=("arbitrary", "arbitrary"),
                                             vmem_limit_bytes=60 * 1024 * 1024),
    )(kidx, h, dproj, wg, gwo, x2d, dx2, g1, small_a, small_b)


def _adam_update(w, g, m, v):
    nm = ADAM_B1 * m + (1.0 - ADAM_B1) * g
    nv = ADAM_B2 * v + (1.0 - ADAM_B2) * (g * g)
    m_hat = nm / (1.0 - ADAM_B1 ** ADAM_STEP)
    v_hat = nv / (1.0 - ADAM_B2 ** ADAM_STEP)
    return -ADAM_LR * (m_hat / (jnp.sqrt(v_hat) + ADAM_EPS) + ADAM_WD * w), nm, nv


def _adamw_small(tot, params):
    n = len(params)

    def body(tot_ref, *refs):
        ins, loss_ref, outs = refs[:3 * n], refs[3 * n], refs[3 * n + 1:]
        k = 2 * lax.axis_index("x") + lax.axis_index("y")
        mine = pl.ds(pl.multiple_of(k * HEAD, HEAD), HEAD)
        loss_ref[...] = tot_ref[7:8, 0:1]
        grads = [tot_ref[0:1, :], tot_ref[1:2, :], tot_ref[2:3, 0:D_HGRN], tot_ref[2:3, D_HGRN:],
                 jnp.concatenate([tot_ref[3:4, 0:D_HGRN], tot_ref[3:4, D_HGRN:]], axis=0),
                 jnp.concatenate([tot_ref[4 + tap:5 + tap, mine] for tap in range(3)], axis=1)]
        for i, g in enumerate(grads):
            w_ref, m_ref, v_ref = ins[3 * i:3 * i + 3]
            g_ref, d_ref, nm_ref, nv_ref = outs[4 * i:4 * i + 4]
            g_ref[...] = g
            d_ref[...], nm_ref[...], nv_ref[...] = _adam_update(w_ref[...], g, m_ref[...], v_ref[...])

    vm = pl.BlockSpec(memory_space=pltpu.VMEM)
    flat = [a for triple in params for a in triple]
    out_shape = (jax.ShapeDtypeStruct((1, 1), F32),) + tuple(
        jax.ShapeDtypeStruct(w.shape, F32) for w, _, _ in params for _ in range(4))
    outs = pl.pallas_call(
        body, name="adamw_small", out_shape=out_shape,
        in_specs=[vm] * (1 + 3 * n), out_specs=tuple([vm] * (1 + 4 * n)),
    )(tot, *flat)
    return [outs[0]] + [outs[1 + 4 * i:5 + 4 * i] for i in range(n)]


def _adamw(w, g, m, v, name):
    rows, cols = w.shape
    tr = rows if rows <= 256 else 256

    def body(w_ref, g_ref, m_ref, v_ref, d_ref, nm_ref, nv_ref):
        d_ref[...], nm_ref[...], nv_ref[...] = _adam_update(w_ref[...], g_ref[...], m_ref[...], v_ref[...])

    blk = lambda: pl.BlockSpec((tr, cols), lambda i: (i, 0))
    shp = jax.ShapeDtypeStruct((rows, cols), F32)
    return pl.pallas_call(
        body, name=name, grid=(rows // tr,),
        out_shape=(shp, shp, shp),
        in_specs=[blk(), blk(), blk(), blk()], out_specs=(blk(), blk(), blk()),
        compiler_params=pltpu.CompilerParams(dimension_semantics=("arbitrary",)),
    )(w, g, m, v)


def _local_step(x2d, tgt, proj, lb_logits, cw, ga, gcn, w_out, gf):
    g64 = _group_matrix(HEAD, CONV_GROUP)
    mixed, o, cv, states, sg, b, wog = _mix_fwd(proj, lb_logits, cw, ga, gcn, g64, w_out)
    dx2, dmixed, gwo, part_out = _out_loss(x2d, mixed, wog.reshape(D_MODEL, D_MODEL), gf, tgt)
    dproj, part_mix = _mix_bwd(proj, o, cv, states, sg, b, dmixed, lb_logits, cw, ga, gcn, g64)
    return dproj, dx2, gwo.reshape(N_SHARD, WO_ROWS, D_MODEL), part_out, part_mix


def kernel(x, norm_gain, w_in, lb_logits, conv_w, hgrn_norm_gain, conv_norm_gain, w_out, final_norm_gain, loss_target, m_norm_gain, m_w_in, m_lb_logits, m_conv_w, m_hgrn_norm_gain, m_conv_norm_gain, m_w_out, m_final_norm_gain, v_norm_gain, v_w_in, v_lb_logits, v_conv_w, v_hgrn_norm_gain, v_conv_norm_gain, v_w_out, v_final_norm_gain):
    k = 2 * lax.axis_index("x") + lax.axis_index("y")
    kidx = jnp.reshape(k, (1,)).astype(jnp.int32)
    row = lambda a: a.reshape(1, D_MODEL)
    taps = lambda a: a.reshape(1, 3 * HEAD)
    h, proj, wg, cw = _gather_proj(kidx, x[0], norm_gain, w_in, taps(conv_w))
    dproj, dx2, gwo, part_out, part_mix = _local_step(
        x[0], loss_target[0], proj, lb_logits, cw, hgrn_norm_gain, conv_norm_gain, w_out, row(final_norm_gain))
    grad_x, g_w_in, g_w_out, tot = _bwd_tail(kidx, h, dproj, wg, gwo, x[0], dx2, norm_gain, part_out, part_mix)

    d_w_in, nm_w_in, nv_w_in = _adamw(w_in[0], g_w_in, m_w_in[0], v_w_in[0], "adamw_w_in")
    d_w_out, nm_w_out, nv_w_out = _adamw(w_out[0], g_w_out, m_w_out[0], v_w_out[0], "adamw_w_out")
    (loss, (g_norm_gain, d_ng, nm_ng, nv_ng), (g_final, d_fg, nm_fg, nv_fg), (g_hgrn, d_hg, nm_hg, nv_hg),
     (g_convn, d_cg, nm_cg, nv_cg), (g_lb, d_lb, nm_lb, nv_lb), (g_conv_w, d_cw, nm_cw, nv_cw)) = _adamw_small(
        tot, [(norm_gain, m_norm_gain, v_norm_gain),
              (row(final_norm_gain), row(m_final_norm_gain), row(v_final_norm_gain)),
              (hgrn_norm_gain, m_hgrn_norm_gain, v_hgrn_norm_gain),
              (conv_norm_gain, m_conv_norm_gain, v_conv_norm_gain),
              (lb_logits, m_lb_logits, v_lb_logits),
              (taps(conv_w), taps(m_conv_w), taps(v_conv_w))])
    flat = lambda a: a.reshape(D_MODEL)
    untap = lambda a: a.reshape(1, 3, HEAD)
    return (loss.reshape(()), grad_x[None],
            g_norm_gain, g_w_in[None], g_lb, untap(g_conv_w), g_hgrn, g_convn, g_w_out[None], flat(g_final),
            d_ng, d_w_in[None], d_lb, untap(d_cw), d_hg, d_cg, d_w_out[None], flat(d_fg),
            nm_ng, nm_w_in[None], nm_lb, untap(nm_cw), nm_hg, nm_cg, nm_w_out[None], flat(nm_fg),
            nv_ng, nv_w_in[None], nv_lb, untap(nv_cw), nv_hg, nv_cg, nv_w_out[None], flat(nv_fg))
```

```python
import jax
import jax.numpy as jnp
import numpy as np
from jax import lax
from jax.experimental import pallas as pl
from jax.experimental.pallas import tpu as pltpu

F32 = jnp.float32
BF16 = jnp.bfloat16
MESH = pl.DeviceIdType.MESH

SEQ = 2048
D_MODEL = 1024
D_HGRN = 512
D_CONV = 512
HEAD = 128
N_HEADS = 4
CHUNK = 64
CONV_GROUP = 64
N_SHARD = 4
SHARD_COLS = 1024
WO_ROWS = 256
EPS = 1e-6
TB = 256
NCB = TB // CHUNK
N_CHUNKS = SEQ // CHUNK
N_DEV = 8

ADAM_LR = 0.001
ADAM_B1 = 0.9
ADAM_B2 = 0.999
ADAM_EPS = 1e-08
ADAM_WD = 0.01
ADAM_STEP = 10

VMEM_LIMIT = 56 * 1024 * 1024


def _dot(a, b):
    return jnp.dot(a, b, preferred_element_type=F32)


def _dot_nt(a, b):
    return lax.dot_general(a, b, (((1,), (1,)), ((), ())), preferred_element_type=F32)


def _dot_tn(a, b):
    return lax.dot_general(a, b, (((0,), (0,)), ((), ())), preferred_element_type=F32)


def _split_bf16(x, n):
    parts = []
    r = x
    for _ in range(n):
        p = r.astype(BF16)
        parts.append(p)
        r = r - p.astype(F32)
    return parts


def _exact_left(m, x, n=3):
    acc = None
    for p in _split_bf16(x, n):
        t = _dot(m, p)
        acc = t if acc is None else acc + t
    return acc


def _exact_left_many(m, xs, n=3):
    parts = [_split_bf16(x, n) for x in xs]
    accs = [None] * len(xs)
    for i in range(n):
        for j in range(len(xs)):
            t = _dot(m, parts[j][i])
            accs[j] = t if accs[j] is None else accs[j] + t
    return accs


def _group_mean_many(xs, gmat, n=2):
    parts = [_split_bf16(x, n) for x in xs]
    accs = [None] * len(xs)
    for i in range(n):
        for j in range(len(xs)):
            t = _dot(parts[j][i], gmat)
            accs[j] = t if accs[j] is None else accs[j] + t
    return accs


def _group_mean(x, gmat, n=2):
    w = gmat.shape[0]
    outs = []
    for c0 in range(0, x.shape[1], w):
        acc = None
        for p in _split_bf16(x[:, c0:c0 + w], n):
            t = _dot(p, gmat)
            acc = t if acc is None else acc + t
        outs.append(acc)
    return jnp.concatenate(outs, axis=1)


def _sigmoid(x):
    return 1.0 / (1.0 + jnp.exp(-x))


def _lower_bound(lbl):
    l0 = lbl[0:1, :]
    l1 = lbl[1:2, :]
    m = jnp.maximum(l0, l1)
    e0 = jnp.exp(l0 - m)
    e1 = jnp.exp(l1 - m)
    return e0 / (e0 + e1)


def _tri(lower):
    r = lax.broadcasted_iota(jnp.int32, (CHUNK, CHUNK), 0)
    c = lax.broadcasted_iota(jnp.int32, (CHUNK, CHUNK), 1)
    return jnp.where((c <= r) if lower else (c >= r), 1.0, 0.0).astype(BF16)


def _causal():
    r = lax.broadcasted_iota(jnp.int32, (CHUNK, CHUNK), 0)
    c = lax.broadcasted_iota(jnp.int32, (CHUNK, CHUNK), 1)
    return c <= r


def _shift_down(x, sh, prev_tail):
    r = pltpu.roll(x, sh, 0)
    pt = pltpu.roll(prev_tail, sh, 0)
    rows = lax.broadcasted_iota(jnp.int32, prev_tail.shape, 0)
    top = jnp.where(rows < sh, pt, r[0:8])
    return jnp.concatenate([top, r[8:]], axis=0)


def _shift_up(x, sh, next_head):
    n = x.shape[0]
    r = pltpu.roll(x, n - sh, 0)
    nh = pltpu.roll(next_head, 8 - sh, 0)
    rows = lax.broadcasted_iota(jnp.int32, next_head.shape, 0)
    bot = jnp.where(rows >= 8 - sh, nh, r[n - 8:])
    return jnp.concatenate([r[:n - 8], bot], axis=0)


def _group_matrix(width, group):
    r = np.arange(width)[:, None] // group
    c = np.arange(width)[None, :] // group
    return jnp.asarray(np.where(r == c, 1.0 / group, 0.0), dtype=BF16)


TP = 512
SEM_W, SEM_CW, SEM_W_FWD, N_SEM = 0, 3, 6, 9


def _gather_proj(kidx, x2d, g1, w_in, conv_w):
    half_w = D_MODEL // 2
    nt = SEQ // TP

    def body(k_ref, x_ref, g_ref, w_ref, cw_ref, h_ref, p_ref, wg_out, cwg_out,
             wg_v, cwg_v, send_sems, recv_sems, out_sems):
        s, t = pl.program_id(0), pl.program_id(1)
        x, y, c = lax.axis_index("x"), lax.axis_index("y"), lax.axis_index("c")
        k = 2 * x + y
        sibling = (x, y, 1 - c)
        chips = [(1 - x, y), (x, 1 - y), (1 - x, 1 - y)]
        kjs = [2 * cx + cy for cx, cy in chips]

        def w_half(kk, cc):
            return wg_v.at[kk, pl.ds(cc * half_w, half_w), :]

        def cw_of(kk):
            return cwg_v.at[:, pl.ds(pl.multiple_of(kk * HEAD, HEAD), HEAD)]

        def copy(sem, ref, to):
            return pltpu.make_async_remote_copy(
                src_ref=ref, dst_ref=ref, send_sem=send_sems.at[sem], recv_sem=recv_sems.at[sem],
                device_id=to, device_id_type=MESH)

        def at_step(sv, tv):
            return pl.when((s == sv) & (t == tv))

        w_direct = [copy(SEM_W + j, w_half(k, c), (*chip, c)) for j, chip in enumerate(chips)]
        cw_direct = [copy(SEM_CW + j, cw_of(k), (*chip, c)) for j, chip in enumerate(chips)]
        w_passed = [copy(SEM_W_FWD + j, w_half(kj, c), sibling) for j, kj in enumerate(kjs)]
        stores = ([pltpu.make_async_copy(wg_v.at[kk], wg_out.at[kk], out_sems.at[i])
                   for i, kk in enumerate([k] + kjs)]
                  + [pltpu.make_async_copy(cwg_v, cwg_out, out_sems.at[4])])

        @at_step(0, 0)
        def _():
            wg_v[k] = w_ref[0].astype(BF16)
            mine = pl.ds(pl.multiple_of(k * HEAD, HEAD), HEAD)
            cwg_v[:, mine] = jnp.zeros((8, HEAD), F32)
            for tap in range(3):
                cwg_v[tap:tap + 1, mine] = cw_ref[:, tap * HEAD:(tap + 1) * HEAD]
            w_direct[0].start()
            w_direct[1].start()
            for cp in cw_direct:
                cp.start()
            stores[0].start()

        @at_step(1, 0)
        def _():
            for j in range(2):
                copy(SEM_W + j, w_half(kjs[j], c), sibling).wait_recv()
                w_passed[j].start()
            w_direct[2].start()
            copy(SEM_W_FWD, w_half(kjs[0], 1 - c), sibling).wait_recv()
            stores[1].start()

        @at_step(2, 0)
        def _():
            copy(SEM_W_FWD + 1, w_half(kjs[1], 1 - c), sibling).wait_recv()
            stores[2].start()

        @at_step(3, 0)
        def _():
            copy(SEM_W + 2, w_half(kjs[2], c), sibling).wait_recv()
            w_passed[2].start()
            copy(SEM_W_FWD + 2, w_half(kjs[2], 1 - c), sibling).wait_recv()
            stores[3].start()

        rows = pl.ds(pl.multiple_of(t * TP, TP), TP)

        @pl.when(s == 0)
        def _():
            xv = x_ref[...]
            r = lax.rsqrt(jnp.mean(xv * xv, axis=-1, keepdims=True) + EPS)
            h_ref[rows, :] = (xv * r * g_ref[...]).astype(BF16)

        js = k ^ (((s & 1) << 1) | (s >> 1))
        p_ref[...] = _dot(h_ref[rows, :], wg_v[js])

        @at_step(N_SHARD - 1, nt - 1)
        def _():
            for j in range(3):
                copy(SEM_CW + j, cw_of(kjs[j]), sibling).wait_recv()
            stores[4].start()
            for cp in w_direct + cw_direct + w_passed:
                cp.wait_send()
            for st in stores:
                st.wait()

    def x_map(s, t, kr):
        return (jnp.where(s == 0, t, nt - 1), 0)

    def p_map(s, t, kr):
        return (t, kr[0] ^ (((s & 1) << 1) | (s >> 1)))

    hbm = pl.BlockSpec(memory_space=pl.ANY)
    grid_spec = pltpu.PrefetchScalarGridSpec(
        num_scalar_prefetch=1, grid=(N_SHARD, nt),
        in_specs=[pl.BlockSpec((TP, D_MODEL), x_map),
                  pl.BlockSpec((1, D_MODEL), lambda s, t, kr: (0, 0)),
                  pl.BlockSpec((1, D_MODEL, SHARD_COLS), lambda s, t, kr: (0, 0, 0)),
                  pl.BlockSpec((1, 3 * HEAD), lambda s, t, kr: (0, 0))],
        out_specs=(pl.BlockSpec((SEQ, D_MODEL), lambda s, t, kr: (0, 0)),
                   pl.BlockSpec((TP, SHARD_COLS), p_map), hbm, hbm),
        scratch_shapes=[pltpu.VMEM((N_SHARD, D_MODEL, SHARD_COLS), BF16),
                        pltpu.VMEM((8, D_CONV), F32),
                        pltpu.SemaphoreType.DMA((N_SEM,)), pltpu.SemaphoreType.DMA((N_SEM,)),
                        pltpu.SemaphoreType.DMA((5,))])
    return pl.pallas_call(
        body, name="gather_proj", grid_spec=grid_spec,
        out_shape=(jax.ShapeDtypeStruct((SEQ, D_MODEL), BF16),
                   jax.ShapeDtypeStruct((SEQ, N_SHARD * SHARD_COLS), F32),
                   jax.ShapeDtypeStruct((N_SHARD, D_MODEL, SHARD_COLS), BF16),
                   jax.ShapeDtypeStruct((8, D_CONV), F32)),
        compiler_params=pltpu.CompilerParams(dimension_semantics=("arbitrary", "arbitrary"),
                                             vmem_limit_bytes=VMEM_LIMIT),
    )(kidx, x2d, g1, w_in, conv_w)


def _mix_fwd(proj, lb_logits, cw, ga, gcn, g64, w_out):
    half_o = WO_ROWS // 2
    nblk = SEQ // TB

    def body(p_ref, lbl_ref, cw_ref, ga_ref, gcn_ref, g64_ref, wo_ref,
             mixed_ref, o_ref, cv_ref, sto_ref, sg_ref, b_ref, wog_out,
             st_ref, tail_ref, wog_v, send_sems, recv_sems, out_sem):
        i = pl.program_id(0)
        x, y, c = lax.axis_index("x"), lax.axis_index("y"), lax.axis_index("c")
        k = 2 * x + y
        sibling = (x, y, 1 - c)
        chips = [(1 - x, y), (x, 1 - y), (1 - x, 1 - y)]
        kjs = [2 * cx + cy for cx, cy in chips]

        def wo_half(kk, cc):
            return wog_v.at[kk, pl.ds(cc * half_o, half_o), :]

        def copy(sem, ref, to):
            return pltpu.make_async_remote_copy(
                src_ref=ref, dst_ref=ref, send_sem=send_sems.at[sem], recv_sem=recv_sems.at[sem],
                device_id=to, device_id_type=MESH)

        wo_direct = [copy(j, wo_half(k, c), (*chip, c)) for j, chip in enumerate(chips)]
        wo_passed = [copy(3 + j, wo_half(kj, c), sibling) for j, kj in enumerate(kjs)]
        wo_store = pltpu.make_async_copy(wog_v, wog_out, out_sem.at[0])

        @pl.when(i == 0)
        def _():
            st_ref[...] = jnp.zeros_like(st_ref)
            tail_ref[...] = jnp.zeros_like(tail_ref)
            wog_v[k] = wo_ref[0].astype(BF16)
            for cp in wo_direct:
                cp.start()

        @pl.when(i == nblk - 2)
        def _():
            for j in range(3):
                copy(j, wo_half(kjs[j], c), sibling).wait_recv()
                wo_passed[j].start()

        lb = _lower_bound(lbl_ref[...])
        tri = _tri(True)
        causal = _causal()
        g64m = g64_ref[...]
        heads = range(N_HEADS)
        cs = [slice(hd * HEAD, (hd + 1) * HEAD) for hd in heads]
        col = lambda base, hd: slice(base + hd * HEAD, base + (hd + 1) * HEAD)
        for n in range(NCB):
            sl = pl.ds(n * CHUNK, CHUNK)
            sg = [_sigmoid(p_ref[sl, col(512, hd)]) for hd in heads]
            f = [lb[:, cs[hd]] + (1.0 - lb[:, cs[hd]]) * sg[hd] for hd in heads]
            bc = _exact_left_many(tri, [jnp.log(f[hd]) for hd in heads])
            for hd in heads:
                sg_ref[sl, cs[hd]] = sg[hd]
                b_ref[sl, cs[hd]] = bc[hd]
            g = [bc[hd][CHUNK - 1:CHUNK, :] for hd in heads]
            qd = [(p_ref[sl, col(0, hd)] * jnp.exp(bc[hd])).astype(BF16) for hd in heads]
            ki = [((1.0 - f[hd]) * jnp.exp(-bc[hd])).astype(BF16) for hd in heads]
            ke = [((1.0 - f[hd]) * jnp.exp(g[hd] - bc[hd])).astype(BF16) for hd in heads]
            vb = [p_ref[sl, col(1024, hd)].astype(BF16) for hd in heads]
            st = [st_ref[hd] for hd in heads]
            for hd in heads:
                sto_ref[n, hd] = st[hd]
            scm = [_dot_nt(qd[hd], ki[hd]) for hd in heads]
            inter = [_dot_nt(qd[hd], st[hd].astype(BF16)) for hd in heads]
            upd = [_dot_tn(vb[hd], ke[hd]) for hd in heads]
            intra = [_dot(jnp.where(causal, scm[hd], 0.0).astype(BF16), vb[hd]) for hd in heads]
            for hd in heads:
                st_ref[hd] = st[hd] * jnp.exp(g[hd]) + upd[hd]
                o = intra[hd] + inter[hd]
                o_ref[sl, cs[hd]] = o
                ra = lax.rsqrt(jnp.mean(o * o, axis=-1, keepdims=True) + EPS)
                za = p_ref[sl, col(1536, hd)]
                mixed_ref[sl, cs[hd]] = (o * ra * ga_ref[:, cs[hd]] * (za * _sigmoid(za))).astype(BF16)
            yb = []
            for hd in heads:
                cu = p_ref[sl, col(3072, hd)] * p_ref[sl, col(2048, hd)]
                tail = tail_ref[:, cs[hd]]
                cv = (cw_ref[0:1, cs[hd]] * _shift_down(cu, 2, tail) + cw_ref[1:2, cs[hd]] * _shift_down(cu, 1, tail)
                      + cw_ref[2:3, cs[hd]] * cu)
                tail_ref[:, cs[hd]] = cu[CHUNK - 8:, :]
                cv_ref[sl, cs[hd]] = cv
                yb.append(p_ref[sl, col(2560, hd)] * cv)
            ms = _group_mean_many([y * y for y in yb], g64m)
            for hd in heads:
                rb = lax.rsqrt(ms[hd] + EPS)
                zb = p_ref[sl, col(3584, hd)]
                mixed_ref[sl, col(512, hd)] = (yb[hd] * rb * gcn_ref[:, cs[hd]] * (zb * _sigmoid(zb))).astype(BF16)

        @pl.when(i == nblk - 1)
        def _():
            for j in range(3):
                copy(3 + j, wo_half(kjs[j], 1 - c), sibling).wait_recv()
            wo_store.start()
            for cp in wo_direct + wo_passed:
                cp.wait_send()
            wo_store.wait()

    row = lambda w: pl.BlockSpec((1, w), lambda i: (0, 0))
    return pl.pallas_call(
        body, name="mix_fwd", grid=(nblk,),
        out_shape=(jax.ShapeDtypeStruct((SEQ, D_MODEL), BF16),
                   jax.ShapeDtypeStruct((SEQ, D_HGRN), F32),
                   jax.ShapeDtypeStruct((SEQ, D_CONV), F32),
                   jax.ShapeDtypeStruct((N_CHUNKS, N_HEADS, HEAD, HEAD), F32),
                   jax.ShapeDtypeStruct((SEQ, D_HGRN), F32),
                   jax.ShapeDtypeStruct((SEQ, D_HGRN), F32),
                   jax.ShapeDtypeStruct((N_SHARD, WO_ROWS, D_MODEL), BF16)),
        in_specs=[pl.BlockSpec((TB, 4096), lambda i: (i, 0)),
                  pl.BlockSpec((2, D_HGRN), lambda i: (0, 0)),
                  pl.BlockSpec((8, D_CONV), lambda i: (0, 0)),
                  row(D_HGRN), row(D_CONV),
                  pl.BlockSpec((HEAD, HEAD), lambda i: (0, 0)),
                  pl.BlockSpec((1, WO_ROWS, D_MODEL), lambda i: (0, 0, 0))],
        out_specs=(pl.BlockSpec((TB, D_MODEL), lambda i: (i, 0)),
                   pl.BlockSpec((TB, D_HGRN), lambda i: (i, 0)),
                   pl.BlockSpec((TB, D_CONV), lambda i: (i, 0)),
                   pl.BlockSpec((NCB, N_HEADS, HEAD, HEAD), lambda i: (i, 0, 0, 0)),
                   pl.BlockSpec((TB, D_HGRN), lambda i: (i, 0)),
                   pl.BlockSpec((TB, D_HGRN), lambda i: (i, 0)),
                   pl.BlockSpec(memory_space=pl.ANY)),
        scratch_shapes=[pltpu.VMEM((N_HEADS, HEAD, HEAD), F32), pltpu.VMEM((8, D_CONV), F32),
                        pltpu.VMEM((N_SHARD, WO_ROWS, D_MODEL), BF16),
                        pltpu.SemaphoreType.DMA((6,)), pltpu.SemaphoreType.DMA((6,)),
                        pltpu.SemaphoreType.DMA((1,))],
        compiler_params=pltpu.CompilerParams(dimension_semantics=("arbitrary",), vmem_limit_bytes=VMEM_LIMIT),
    )(proj, lb_logits, cw, ga, gcn, g64, w_out)


def _out_loss(x2d, mixed, wog, gf, tgt):
    def body(x_ref, m_ref, wo_ref, gf_ref, t_ref, dx2_ref, dm_ref, gwo_ref, part_ref, acc_ref):
        i = pl.program_id(0)

        @pl.when(i == 0)
        def _():
            acc_ref[...] = jnp.zeros_like(acc_ref)
            part_ref[...] = jnp.zeros_like(part_ref)

        mixed_b = m_ref[...]
        x2 = x_ref[...] + _dot(mixed_b, wo_ref[...])
        r2 = lax.rsqrt(jnp.mean(x2 * x2, axis=-1, keepdims=True) + EPS)
        n2 = x2 * r2
        gfv = gf_ref[...]
        err = n2 * gfv - t_ref[...]
        loss = 0.5 * jnp.sum(jnp.mean(err * err, axis=-1, keepdims=True), axis=0, keepdims=True)
        dy = err * (1.0 / D_MODEL)
        part_ref[1:2, :] += jnp.sum(dy * n2, axis=0, keepdims=True)
        part_ref[7:8, :] += jnp.broadcast_to(loss, (1, D_MODEL))
        dn = dy * gfv
        dx2 = r2 * (dn - n2 * jnp.mean(dn * n2, axis=-1, keepdims=True))
        dx2_ref[...] = dx2
        dx2_b = dx2.astype(BF16)
        dm_ref[...] = _dot_nt(dx2_b, wo_ref[...])
        acc_ref[...] += _dot_tn(mixed_b, dx2_b)

        @pl.when(i == pl.num_programs(0) - 1)
        def _():
            gwo_ref[...] = acc_ref[...].astype(BF16)

    blk = lambda: pl.BlockSpec((TP, D_MODEL), lambda i: (i, 0))
    return pl.pallas_call(
        body, name="out_loss", grid=(SEQ // TP,),
        out_shape=(jax.ShapeDtypeStruct((SEQ, D_MODEL), F32),
                   jax.ShapeDtypeStruct((SEQ, D_MODEL), F32),
                   jax.ShapeDtypeStruct((D_MODEL, D_MODEL), BF16),
                   jax.ShapeDtypeStruct((8, D_MODEL), F32)),
        in_specs=[blk(), blk(), pl.BlockSpec((D_MODEL, D_MODEL), lambda i: (0, 0)),
                  pl.BlockSpec((1, D_MODEL), lambda i: (0, 0)), blk()],
        out_specs=(blk(), blk(), pl.BlockSpec((D_MODEL, D_MODEL), lambda i: (0, 0)),
                   pl.BlockSpec((8, D_MODEL), lambda i: (0, 0))),
        scratch_shapes=[pltpu.VMEM((D_MODEL, D_MODEL), F32)],
        compiler_params=pltpu.CompilerParams(dimension_semantics=("arbitrary",), vmem_limit_bytes=VMEM_LIMIT),
    )(x2d, mixed, wog, gf, tgt)


def _mix_bwd(proj, o, cv, states, sg, b, dmixed, lb_logits, cw, ga, gcn, g64):
    nblk = SEQ // TB

    def body(p_ref, o_ref, cv_ref, st_ref, sg_ref, b_ref, dm_ref, lbl_ref, cw_ref, ga_ref, gcn_ref, g64_ref,
             dp_ref, part_ref, dst_ref, head_ref, dlb_ref):
        i = pl.program_id(0)

        @pl.when(i == 0)
        def _():
            dst_ref[...] = jnp.zeros_like(dst_ref)
            head_ref[...] = jnp.zeros_like(head_ref)
            part_ref[...] = jnp.zeros_like(part_ref)
            dlb_ref[...] = jnp.zeros_like(dlb_ref)

        lb = _lower_bound(lbl_ref[...])
        triu = _tri(False)
        causal = _causal()
        g64m = g64_ref[...]
        rowsum = lambda a: jnp.sum(a, axis=0, keepdims=True)
        heads = range(N_HEADS)
        cs = [slice(hd * HEAD, (hd + 1) * HEAD) for hd in heads]
        col = lambda base, hd: slice(base + hd * HEAD, base + (hd + 1) * HEAD)
        for n in reversed(range(NCB)):
            sl = pl.ds(n * CHUNK, CHUNK)
            cvv = [cv_ref[sl, cs[hd]] for hd in heads]
            gb = [p_ref[sl, col(2560, hd)] for hd in heads]
            yb = [gb[hd] * cvv[hd] for hd in heads]
            ms = _group_mean_many([y * y for y in yb], g64m)
            rb, nb, dnb = [], [], []
            for hd in heads:
                rb.append(lax.rsqrt(ms[hd] + EPS))
                nb.append(yb[hd] * rb[hd])
                zb = p_ref[sl, col(3584, hd)]
                sgb = _sigmoid(zb)
                dmb = dm_ref[sl, col(512, hd)]
                gcv = gcn_ref[:, cs[hd]]
                part_ref[2:3, col(512, hd)] += rowsum(dmb * nb[hd] * (zb * sgb))
                dp_ref[sl, col(3584, hd)] = (dmb * nb[hd] * gcv * (sgb * (1.0 + zb * (1.0 - sgb)))).astype(BF16)
                dnb.append(dmb * gcv * (zb * sgb))
            mdn = _group_mean_many([dnb[hd] * nb[hd] for hd in heads], g64m)
            for hd in heads:
                dyb = rb[hd] * (dnb[hd] - nb[hd] * mdn[hd])
                dp_ref[sl, col(2560, hd)] = (dyb * cvv[hd]).astype(BF16)
                dcv = dyb * gb[hd]
                head = head_ref[:, cs[hd]]
                dcv1 = _shift_up(dcv, 1, head)
                dcv2 = _shift_up(dcv, 2, head)
                head_ref[:, cs[hd]] = dcv[0:8, :]
                u = p_ref[sl, col(2048, hd)]
                gc = p_ref[sl, col(3072, hd)]
                cu = gc * u
                part_ref[4:5, cs[hd]] += rowsum(dcv2 * cu)
                part_ref[5:6, cs[hd]] += rowsum(dcv1 * cu)
                part_ref[6:7, cs[hd]] += rowsum(dcv * cu)
                dcu = cw_ref[2:3, cs[hd]] * dcv + cw_ref[1:2, cs[hd]] * dcv1 + cw_ref[0:1, cs[hd]] * dcv2
                dp_ref[sl, col(3072, hd)] = (dcu * u).astype(BF16)
                dp_ref[sl, col(2048, hd)] = (dcu * gc).astype(BF16)
            do_b = []
            for hd in heads:
                ov = o_ref[sl, cs[hd]]
                ra = lax.rsqrt(jnp.mean(ov * ov, axis=-1, keepdims=True) + EPS)
                na = ov * ra
                za = p_ref[sl, col(1536, hd)]
                sga = _sigmoid(za)
                dma = dm_ref[sl, cs[hd]]
                gav = ga_ref[:, cs[hd]]
                part_ref[2:3, cs[hd]] += rowsum(dma * na * (za * sga))
                dp_ref[sl, col(1536, hd)] = (dma * na * gav * (sga * (1.0 + za * (1.0 - sga)))).astype(BF16)
                dna = dma * gav * (za * sga)
                do_b.append((ra * (dna - na * jnp.mean(dna * na, axis=-1, keepdims=True))).astype(BF16))
            s = [sg_ref[sl, cs[hd]] for hd in heads]
            f = [lb[:, cs[hd]] + (1.0 - lb[:, cs[hd]]) * s[hd] for hd in heads]
            bc = [b_ref[sl, cs[hd]] for hd in heads]
            g = [bc[hd][CHUNK - 1:CHUNK, :] for hd in heads]
            eb = [jnp.exp(bc[hd]) for hd in heads]
            enb = [jnp.exp(-bc[hd]) for hd in heads]
            eg = [jnp.exp(g[hd] - bc[hd]) for hd in heads]
            dec = [jnp.exp(g[hd]) for hd in heads]
            qd = [p_ref[sl, cs[hd]] * eb[hd] for hd in heads]
            ki = [(1.0 - f[hd]) * enb[hd] for hd in heads]
            ke = [(1.0 - f[hd]) * eg[hd] for hd in heads]
            qd_b = [a.astype(BF16) for a in qd]
            ki_b = [a.astype(BF16) for a in ki]
            ke_b = [a.astype(BF16) for a in ke]
            vb = [p_ref[sl, col(1024, hd)].astype(BF16) for hd in heads]
            st = [st_ref[n, hd] for hd in heads]
            dst = [dst_ref[hd] for hd in heads]
            st_b = [a.astype(BF16) for a in st]
            dst_b = [a.astype(BF16) for a in dst]
            scm = [_dot_nt(qd_b[hd], ki_b[hd]) for hd in heads]
            amm = [_dot_nt(do_b[hd], vb[hd]) for hd in heads]
            dqd2 = [_dot(do_b[hd], st_b[hd]) for hd in heads]
            dke = [_dot(vb[hd], dst_b[hd]) for hd in heads]
            dv2 = [_dot_nt(ke_b[hd], dst_b[hd]) for hd in heads]
            dsu = [_dot_tn(do_b[hd], qd_b[hd]) for hd in heads]
            sc = [jnp.where(causal, scm[hd], 0.0).astype(BF16) for hd in heads]
            am = [jnp.where(causal, amm[hd], 0.0).astype(BF16) for hd in heads]
            dqd1 = [_dot(am[hd], ki_b[hd]) for hd in heads]
            dki = [_dot_tn(am[hd], qd_b[hd]) for hd in heads]
            dv1 = [_dot_tn(sc[hd], do_b[hd]) for hd in heads]
            db, dgv = [], []
            for hd in heads:
                dqd = dqd1[hd] + dqd2[hd]
                ddec = rowsum(dst[hd] * st[hd])
                dst_ref[hd] = dst[hd] * dec[hd] + dsu[hd]
                dp_ref[sl, cs[hd]] = (dqd * eb[hd]).astype(BF16)
                dp_ref[sl, col(1024, hd)] = (dv1[hd] + dv2[hd]).astype(BF16)
                db.append(dqd * qd[hd] - dki[hd] * ki[hd] - dke[hd] * ke[hd])
                dgv.append(rowsum(dke[hd] * ke[hd]) + ddec * dec[hd])
            rc = _exact_left_many(triu, db, 2)
            for hd in heads:
                df = (rc[hd] + dgv[hd]) / f[hd] - (dki[hd] * enb[hd] + dke[hd] * eg[hd])
                dlb_ref[:, cs[hd]] += rowsum(df * (1.0 - s[hd]))
                dp_ref[sl, col(512, hd)] = (df * (1.0 - lb[:, cs[hd]]) * s[hd] * (1.0 - s[hd])).astype(BF16)

        @pl.when(i == nblk - 1)
        def _():
            row = dlb_ref[...] * lb * (1.0 - lb)
            part_ref[3:4, 0:D_HGRN] = row
            part_ref[3:4, D_HGRN:] = -row

    rev = lambda w: pl.BlockSpec((TB, w), lambda i: (nblk - 1 - i, 0))
    row = lambda w: pl.BlockSpec((1, w), lambda i: (0, 0))
    return pl.pallas_call(
        body, name="mix_bwd", grid=(nblk,),
        out_shape=(jax.ShapeDtypeStruct((SEQ, 4096), BF16),
                   jax.ShapeDtypeStruct((8, D_MODEL), F32)),
        in_specs=[rev(4096), rev(D_HGRN), rev(D_CONV),
                  pl.BlockSpec((NCB, N_HEADS, HEAD, HEAD), lambda i: (nblk - 1 - i, 0, 0, 0)),
                  rev(D_HGRN), rev(D_HGRN), rev(D_MODEL),
                  pl.BlockSpec((2, D_HGRN), lambda i: (0, 0)),
                  pl.BlockSpec((8, D_CONV), lambda i: (0, 0)),
                  row(D_HGRN), row(D_CONV),
                  pl.BlockSpec((HEAD, HEAD), lambda i: (0, 0))],
        out_specs=(rev(4096), pl.BlockSpec((8, D_MODEL), lambda i: (0, 0))),
        scratch_shapes=[pltpu.VMEM((N_HEADS, HEAD, HEAD), F32), pltpu.VMEM((8, D_CONV), F32),
                        pltpu.VMEM((1, D_HGRN), F32)],
        compiler_params=pltpu.CompilerParams(dimension_semantics=("arbitrary",), vmem_limit_bytes=VMEM_LIMIT),
    )(proj, o, cv, states, sg, b, dmixed, lb_logits, cw, ga, gcn, g64)


TT = 1024
TX = 256
(SEM_D2D, SEM_D2D_O, SEM_ICI, SEM_ICI_O, SEM_FIN, SEM_FIN_O, SEM_SMALL, N_SEM_TAIL) = 0, 4, 5, 8, 11, 12, 12, 20


def _bwd_tail(kidx, h, dproj, wg, gwo, x2d, dx2, g1, small_a, small_b):
    hw = D_MODEL // 2
    ho = WO_ROWS // 2
    nt = SEQ // TT
    assert nt >= 2
    n_steps = N_SHARD + SEQ // TX // nt

    def body(k_ref, h_ref, dp_ref, w_ref, gwo_ref, x_ref, dx2_ref, g_ref, sm_ref, smb_ref,
             gx_ref, gw_out, gwo_out, osm_ref,
             acc, dh, sendbuf, keep, sibrcv, rcv, sib_o, p_o, rcv_o, res_o, sm_buf, dng,
             send_sems, recv_sems, out_sems):
        s, t = pl.program_id(0), pl.program_id(1)
        x, y, c = lax.axis_index("x"), lax.axis_index("y"), lax.axis_index("c")
        k = 2 * x + y
        me = 4 * x + 2 * y + c
        sibling = (x, y, 1 - c)
        chips = [(1 - x, 1 - y), (1 - x, y), (x, 1 - y)]
        kjs = [2 * cx + cy for cx, cy in chips]
        mine = pl.ds(pl.multiple_of(c * hw, hw), hw)
        other = pl.ds(pl.multiple_of((1 - c) * hw, hw), hw)
        mine_o = pl.ds(pl.multiple_of(c * ho, ho), ho)
        other_o = pl.ds(pl.multiple_of((1 - c) * ho, ho), ho)

        def copy(sem, src, dst, to):
            return pltpu.make_async_remote_copy(
                src_ref=src, dst_ref=dst, send_sem=send_sems.at[sem], recv_sem=recv_sems.at[sem],
                device_id=to, device_id_type=MESH)

        def at_step(sv, tv):
            return pl.when((s == sv) & (t == tv))

        def at_norm_block(b):
            return at_step(N_SHARD + b // nt, b % nt)

        d2d = [copy(SEM_D2D + sv, sendbuf.at[sv], sibrcv.at[sv], sibling) for sv in range(N_SHARD)]
        d2d_o = copy(SEM_D2D_O, gwo_ref.at[:, other_o, :], sib_o, sibling)
        ici = [copy(SEM_ICI + sv, keep.at[sv], rcv.at[sv], (*chips[sv], c)) for sv in range(3)]
        ici_o = [copy(SEM_ICI_O + sv, p_o.at[kjs[sv]], rcv_o.at[sv], (*chips[sv], c)) for sv in range(3)]
        fin = copy(SEM_FIN, acc.at[mine, :], acc.at[mine, :], sibling)
        fin_o = copy(SEM_FIN_O, res_o.at[mine_o, :], res_o.at[mine_o, :], sibling)
        smalls = [copy(SEM_SMALL + m, sm_buf.at[me], sm_buf.at[me],
                       (x ^ (m >> 2), y ^ ((m >> 1) & 1), c ^ (m & 1))) for m in range(1, N_DEV)]
        store_w = pltpu.make_async_copy(acc, gw_out, out_sems.at[0])
        store_o = pltpu.make_async_copy(res_o, gwo_out, out_sems.at[1])

        @at_step(0, 0)
        def _():
            d2d_o.start()

        @at_step(0, 1)
        def _():
            d2d_o.wait_recv()
            for j in range(N_SHARD):
                p_o[j] = (gwo_ref[j, mine_o, :].astype(F32) + sib_o[j].astype(F32)).astype(BF16)
            res_o[mine_o, :] = gwo_ref[k, mine_o, :].astype(F32) + sib_o[k].astype(F32)
            for cp in ici_o:
                cp.start()

        rows = pl.ds(pl.multiple_of(t * TT, TT), TT)

        @pl.when((s < N_SHARD) & (t < nt - 1))
        def _():
            part = _dot_tn(h_ref[...], dp_ref[...])

            @pl.when(t == 0)
            def _():
                acc[...] = part

            @pl.when(t > 0)
            def _():
                acc[...] += part

        for sv in range(N_SHARD):
            for cc in range(2):
                @pl.when((s == sv) & (t == nt - 1) & (c == cc))
                def _(sv=sv, cc=cc):
                    theirs = slice((1 - cc) * hw, (2 - cc) * hw)
                    ours = slice(cc * hw, (cc + 1) * hw)
                    acc[theirs, :] += _dot_tn(h_ref[:, theirs], dp_ref[...])
                    sendbuf[sv] = acc[theirs, :].astype(BF16)
                    d2d[sv].start()
                    acc[ours, :] += _dot_tn(h_ref[:, ours], dp_ref[...])
                    if sv < 3:
                        keep[sv] = acc[ours, :].astype(BF16)

        @pl.when(s < N_SHARD)
        def _():
            d = _dot_nt(dp_ref[...], w_ref[0])

            @pl.when(s == 0)
            def _():
                dh[rows, :] = d

            @pl.when(s > 0)
            def _():
                dh[rows, :] += d

        for sv in range(3):
            @at_step(sv, nt - 1)
            def _(sv=sv):
                d2d[sv].wait_recv()
                keep[sv] = (keep[sv].astype(F32) + sibrcv[sv].astype(F32)).astype(BF16)
                ici[sv].start()

        @at_norm_block(0)
        def _():
            d2d[3].wait_recv()
            ici[0].wait_recv()
            acc[mine, :] += sibrcv[3].astype(F32) + rcv[0].astype(F32)

        @at_norm_block(1)
        def _():
            tot = res_o[mine_o, :]
            for sv in range(3):
                ici_o[sv].wait_recv()
                tot = tot + rcv_o[sv].astype(F32)
            res_o[mine_o, :] = tot
            fin_o.start()

        @at_norm_block(2)
        def _():
            ici[1].wait_recv()
            acc[mine, :] += rcv[1].astype(F32)

        @at_norm_block(0)
        def _():
            dng[...] = jnp.zeros_like(dng)

        @pl.when(s >= N_SHARD)
        def _():
            blk = (s - N_SHARD) * nt + t
            dhv = dh[pl.ds(pl.multiple_of(blk * TX, TX), TX), :]
            xv = x_ref[...]
            r = lax.rsqrt(jnp.mean(xv * xv, axis=-1, keepdims=True) + EPS)
            xn = xv * r
            dng[...] += jnp.sum(dhv * xn, axis=0, keepdims=True)
            dxn = dhv * g_ref[...]
            gx_ref[...] = dx2_ref[...] + r * (dxn - xn * jnp.mean(dxn * xn, axis=-1, keepdims=True))

        @at_step(n_steps - 1, nt - 1)
        def _():
            sm_buf[me] = sm_ref[...] + smb_ref[...]
            sm_buf[me, 0:1, :] = dng[...]
            for cp in smalls:
                cp.start()
            ici[2].wait_recv()
            acc[mine, :] += rcv[2].astype(F32)
            fin.start()
            for m in range(1, N_DEV):
                copy(SEM_SMALL + m, sm_buf.at[0], sm_buf.at[0], sibling).wait_recv()
            tot = sm_buf[0]
            for d in range(1, N_DEV):
                tot = tot + sm_buf[d]
            osm_ref[...] = tot
            fin_o.wait_recv()
            store_o.start()
            fin.wait_recv()
            store_w.start()
            for cp in d2d + [d2d_o] + ici + ici_o + [fin, fin_o] + smalls:
                cp.wait_send()
            store_o.wait()
            store_w.wait()

    def shard_of(s, kr):
        return kr[0] ^ (3 - jnp.minimum(s, 3))

    def tok(s, t):
        return jnp.where(s < N_SHARD, t, nt - 1)

    def blk_map(s, t, kr):
        return (jnp.where(s < N_SHARD, 0, (s - N_SHARD) * nt + t), 0)

    hbm = pl.BlockSpec(memory_space=pl.ANY)
    grid_spec = pltpu.PrefetchScalarGridSpec(
        num_scalar_prefetch=1, grid=(n_steps, nt),
        in_specs=[pl.BlockSpec((TT, D_MODEL), lambda s, t, kr: (tok(s, t), 0)),
                  pl.BlockSpec((TT, SHARD_COLS), lambda s, t, kr: (tok(s, t), shard_of(s, kr))),
                  pl.BlockSpec((1, D_MODEL, SHARD_COLS), lambda s, t, kr: (shard_of(s, kr), 0, 0)),
                  pl.BlockSpec((N_SHARD, WO_ROWS, D_MODEL), lambda s, t, kr: (0, 0, 0)),
                  pl.BlockSpec((TX, D_MODEL), blk_map),
                  pl.BlockSpec((TX, D_MODEL), blk_map),
                  pl.BlockSpec((1, D_MODEL), lambda s, t, kr: (0, 0)),
                  pl.BlockSpec((8, D_MODEL), lambda s, t, kr: (0, 0)),
                  pl.BlockSpec((8, D_MODEL), lambda s, t, kr: (0, 0))],
        out_specs=(pl.BlockSpec((TX, D_MODEL), blk_map), hbm, hbm,
                   pl.BlockSpec((8, D_MODEL), lambda s, t, kr: (0, 0))),
        scratch_shapes=[pltpu.VMEM((D_MODEL, SHARD_COLS), F32), pltpu.VMEM((SEQ, D_MODEL), F32),
                        pltpu.VMEM((N_SHARD, hw, SHARD_COLS), BF16), pltpu.VMEM((3, hw, SHARD_COLS), BF16),
                        pltpu.VMEM((N_SHARD, hw, SHARD_COLS), BF16), pltpu.VMEM((3, hw, SHARD_COLS), BF16),
                        pltpu.VMEM((N_SHARD, ho, D_MODEL), BF16), pltpu.VMEM((N_SHARD, ho, D_MODEL), BF16),
                        pltpu.VMEM((3, ho, D_MODEL), BF16), pltpu.VMEM((WO_ROWS, D_MODEL), F32),
                        pltpu.VMEM((N_DEV, 8, D_MODEL), F32), pltpu.VMEM((1, D_MODEL), F32),
                        pltpu.SemaphoreType.DMA((N_SEM_TAIL,)), pltpu.SemaphoreType.DMA((N_SEM_TAIL,)),
                        pltpu.SemaphoreType.DMA((2,))])
    return pl.pallas_call(
        body, name="bwd_tail", grid_spec=grid_spec,
        out_shape=(jax.ShapeDtypeStruct((SEQ, D_MODEL), F32),
                   jax.ShapeDtypeStruct((D_MODEL, SHARD_COLS), F32),
                   jax.ShapeDtypeStruct((WO_ROWS, D_MODEL), F32),
                   jax.ShapeDtypeStruct((8, D_MODEL), F32)),
        compiler_params=pltpu.CompilerParams(dimension_semantics=("arbitrary", "arbitrary"),
                                             vmem_limit_bytes=60 * 1024 * 1024),
    )(kidx, h, dproj, wg, gwo, x2d, dx2, g1, small_a, small_b)


def _adam_update(w, g, m, v):
    nm = ADAM_B1 * m + (1.0 - ADAM_B1) * g
    nv = ADAM_B2 * v + (1.0 - ADAM_B2) * (g * g)
    m_hat = nm / (1.0 - ADAM_B1 ** ADAM_STEP)
    v_hat = nv / (1.0 - ADAM_B2 ** ADAM_STEP)
    return -ADAM_LR * (m_hat / (jnp.sqrt(v_hat) + ADAM_EPS) + ADAM_WD * w), nm, nv


def _adamw_small(tot, params):
    n = len(params)

    def body(tot_ref, *refs):
        ins, loss_ref, outs = refs[:3 * n], refs[3 * n], refs[3 * n + 1:]
        k = 2 * lax.axis_index("x") + lax.axis_index("y")
        mine = pl.ds(pl.multiple_of(k * HEAD, HEAD), HEAD)
        loss_ref[...] = tot_ref[7:8, 0:1]
        grads = [tot_ref[0:1, :], tot_ref[1:2, :], tot_ref[2:3, 0:D_HGRN], tot_ref[2:3, D_HGRN:],
                 jnp.concatenate([tot_ref[3:4, 0:D_HGRN], tot_ref[3:4, D_HGRN:]], axis=0),
                 jnp.concatenate([tot_ref[4 + tap:5 + tap, mine] for tap in range(3)], axis=1)]
        for i, g in enumerate(grads):
            w_ref, m_ref, v_ref = ins[3 * i:3 * i + 3]
            g_ref, d_ref, nm_ref, nv_ref = outs[4 * i:4 * i + 4]
            g_ref[...] = g
            d_ref[...], nm_ref[...], nv_ref[...] = _adam_update(w_ref[...], g, m_ref[...], v_ref[...])

    vm = pl.BlockSpec(memory_space=pltpu.VMEM)
    flat = [a for triple in params for a in triple]
    out_shape = (jax.ShapeDtypeStruct((1, 1), F32),) + tuple(
        jax.ShapeDtypeStruct(w.shape, F32) for w, _, _ in params for _ in range(4))
    outs = pl.pallas_call(
        body, name="adamw_small", out_shape=out_shape,
        in_specs=[vm] * (1 + 3 * n), out_specs=tuple([vm] * (1 + 4 * n)),
    )(tot, *flat)
    return [outs[0]] + [outs[1 + 4 * i:5 + 4 * i] for i in range(n)]


def _adamw(w, g, m, v, name):
    rows, cols = w.shape
    tr = rows if rows <= 256 else 256

    def body(w_ref, g_ref, m_ref, v_ref, d_ref, nm_ref, nv_ref):
        d_ref[...], nm_ref[...], nv_ref[...] = _adam_update(w_ref[...], g_ref[...], m_ref[...], v_ref[...])

    blk = lambda: pl.BlockSpec((tr, cols), lambda i: (i, 0))
    shp = jax.ShapeDtypeStruct((rows, cols), F32)
    return pl.pallas_call(
        body, name=name, grid=(rows // tr,),
        out_shape=(shp, shp, shp),
        in_specs=[blk(), blk(), blk(), blk()], out_specs=(blk(), blk(), blk()),
        compiler_params=pltpu.CompilerParams(dimension_semantics=("arbitrary",)),
    )(w, g, m, v)


def _local_step(x2d, tgt, proj, lb_logits, cw, ga, gcn, w_out, gf):
    g64 = _group_matrix(HEAD, CONV_GROUP)
    mixed, o, cv, states, sg, b, wog = _mix_fwd(proj, lb_logits, cw, ga, gcn, g64, w_out)
    dx2, dmixed, gwo, part_out = _out_loss(x2d, mixed, wog.reshape(D_MODEL, D_MODEL), gf, tgt)
    dproj, part_mix = _mix_bwd(proj, o, cv, states, sg, b, dmixed, lb_logits, cw, ga, gcn, g64)
    return dproj, dx2, gwo.reshape(N_SHARD, WO_ROWS, D_MODEL), part_out, part_mix


def kernel(x, norm_gain, w_in, lb_logits, conv_w, hgrn_norm_gain, conv_norm_gain, w_out, final_norm_gain, loss_target, m_norm_gain, m_w_in, m_lb_logits, m_conv_w, m_hgrn_norm_gain, m_conv_norm_gain, m_w_out, m_final_norm_gain, v_norm_gain, v_w_in, v_lb_logits, v_conv_w, v_hgrn_norm_gain, v_conv_norm_gain, v_w_out, v_final_norm_gain):
    k = 2 * lax.axis_index("x") + lax.axis_index("y")
    kidx = jnp.reshape(k, (1,)).astype(jnp.int32)
    row = lambda a: a.reshape(1, D_MODEL)
    taps = lambda a: a.reshape(1, 3 * HEAD)
    h, proj, wg, cw = _gather_proj(kidx, x[0], norm_gain, w_in, taps(conv_w))
    dproj, dx2, gwo, part_out, part_mix = _local_step(
        x[0], loss_target[0], proj, lb_logits, cw, hgrn_norm_gain, conv_norm_gain, w_out, row(final_norm_gain))
    grad_x, g_w_in, g_w_out, tot = _bwd_tail(kidx, h, dproj, wg, gwo, x[0], dx2, norm_gain, part_out, part_mix)

    d_w_in, nm_w_in, nv_w_in = _adamw(w_in[0], g_w_in, m_w_in[0], v_w_in[0], "adamw_w_in")
    d_w_out, nm_w_out, nv_w_out = _adamw(w_out[0], g_w_out, m_w_out[0], v_w_out[0], "adamw_w_out")
    (loss, (g_norm_gain, d_ng, nm_ng, nv_ng), (g_final, d_fg, nm_fg, nv_fg), (g_hgrn, d_hg, nm_hg, nv_hg),
     (g_convn, d_cg, nm_cg, nv_cg), (g_lb, d_lb, nm_lb, nv_lb), (g_conv_w, d_cw, nm_cw, nv_cw)) = _adamw_small(
        tot, [(norm_gain, m_norm_gain, v_norm_gain),
              (row(final_norm_gain), row(m_final_norm_gain), row(v_final_norm_gain)),
              (hgrn_norm_gain, m_hgrn_norm_gain, v_hgrn_norm_gain),
              (conv_norm_gain, m_conv_norm_gain, v_conv_norm_gain),
              (lb_logits, m_lb_logits, v_lb_logits),
              (taps(conv_w), taps(m_conv_w), taps(v_conv_w))])
    flat = lambda a: a.reshape(D_MODEL)
    untap = lambda a: a.reshape(1, 3, HEAD)
    return (loss.reshape(()), grad_x[None],
            g_norm_gain, g_w_in[None], g_lb, untap(g_conv_w), g_hgrn, g_convn, g_w_out[None], flat(g_final),
            d_ng, d_w_in[None], d_lb, untap(d_cw), d_hg, d_cg, d_w_out[None], flat(d_fg),
            nm_ng, nm_w_in[None], nm_lb, untap(nm_cw), nm_hg, nm_cg, nm_w_out[None], flat(nm_fg),
            nv_ng, nv_w_in[None], nv_lb, untap(nv_cw), nv_hg, nv_cg, nv_w_out[None], flat(nv_fg))
```

```python
import jax
import jax.numpy as jnp
import numpy as np
from jax import lax
from jax.experimental import pallas as pl
from jax.experimental.pallas import tpu as pltpu

F32 = jnp.float32
BF16 = jnp.bfloat16
MESH = pl.DeviceIdType.MESH

SEQ = 2048
D_MODEL = 1024
D_HGRN = 512
D_CONV = 512
HEAD = 128
N_HEADS = 4
CHUNK = 64
CONV_GROUP = 64
N_SHARD = 4
SHARD_COLS = 1024
WO_ROWS = 256
EPS = 1e-6
TB = 256
NCB = TB // CHUNK
N_CHUNKS = SEQ // CHUNK
N_DEV = 8

ADAM_LR = 0.001
ADAM_B1 = 0.9
ADAM_B2 = 0.999
ADAM_EPS = 1e-08
ADAM_WD = 0.01
ADAM_STEP = 10

VMEM_LIMIT = 56 * 1024 * 1024


def _dot(a, b):
    return jnp.dot(a, b, preferred_element_type=F32)


def _dot_nt(a, b):
    return lax.dot_general(a, b, (((1,), (1,)), ((), ())), preferred_element_type=F32)


def _dot_tn(a, b):
    return lax.dot_general(a, b, (((0,), (0,)), ((), ())), preferred_element_type=F32)


def _split_bf16(x, n):
    parts = []
    r = x
    for _ in range(n):
        p = r.astype(BF16)
        parts.append(p)
        r = r - p.astype(F32)
    return parts


def _exact_left(m, x, n=3):
    acc = None
    for p in _split_bf16(x, n):
        t = _dot(m, p)
        acc = t if acc is None else acc + t
    return acc


def _exact_left_many(m, xs, n=3):
    parts = [_split_bf16(x, n) for x in xs]
    accs = [None] * len(xs)
    for i in range(n):
        for j in range(len(xs)):
            t = _dot(m, parts[j][i])
            accs[j] = t if accs[j] is None else accs[j] + t
    return accs


def _group_mean_many(xs, gmat, n=2):
    parts = [_split_bf16(x, n) for x in xs]
    accs = [None] * len(xs)
    for i in range(n):
        for j in range(len(xs)):
            t = _dot(parts[j][i], gmat)
            accs[j] = t if accs[j] is None else accs[j] + t
    return accs


def _group_mean(x, gmat, n=2):
    w = gmat.shape[0]
    outs = []
    for c0 in range(0, x.shape[1], w):
        acc = None
        for p in _split_bf16(x[:, c0:c0 + w], n):
            t = _dot(p, gmat)
            acc = t if acc is None else acc + t
        outs.append(acc)
    return jnp.concatenate(outs, axis=1)


def _sigmoid(x):
    return 1.0 / (1.0 + jnp.exp(-x))


def _lower_bound(lbl):
    l0 = lbl[0:1, :]
    l1 = lbl[1:2, :]
    m = jnp.maximum(l0, l1)
    e0 = jnp.exp(l0 - m)
    e1 = jnp.exp(l1 - m)
    return e0 / (e0 + e1)


def _tri(lower):
    r = lax.broadcasted_iota(jnp.int32, (CHUNK, CHUNK), 0)
    c = lax.broadcasted_iota(jnp.int32, (CHUNK, CHUNK), 1)
    return jnp.where((c <= r) if lower else (c >= r), 1.0, 0.0).astype(BF16)


def _causal():
    r = lax.broadcasted_iota(jnp.int32, (CHUNK, CHUNK), 0)
    c = lax.broadcasted_iota(jnp.int32, (CHUNK, CHUNK), 1)
    return c <= r


def _shift_down(x, sh, prev_tail):
    r = pltpu.roll(x, sh, 0)
    pt = pltpu.roll(prev_tail, sh, 0)
    rows = lax.broadcasted_iota(jnp.int32, prev_tail.shape, 0)
    top = jnp.where(rows < sh, pt, r[0:8])
    return jnp.concatenate([top, r[8:]], axis=0)


def _shift_up(x, sh, next_head):
    n = x.shape[0]
    r = pltpu.roll(x, n - sh, 0)
    nh = pltpu.roll(next_head, 8 - sh, 0)
    rows = lax.broadcasted_iota(jnp.int32, next_head.shape, 0)
    bot = jnp.where(rows >= 8 - sh, nh, r[n - 8:])
    return jnp.concatenate([r[:n - 8], bot], axis=0)


def _group_matrix(width, group):
    r = np.arange(width)[:, None] // group
    c = np.arange(width)[None, :] // group
    return jnp.asarray(np.where(r == c, 1.0 / group, 0.0), dtype=BF16)


TP = 512
SEM_W, SEM_CW, SEM_W_FWD, N_SEM = 0, 3, 6, 9


def _gather_proj(kidx, x2d, g1, w_in, conv_w):
    half_w = D_MODEL // 2
    nt = SEQ // TP

    def body(k_ref, x_ref, g_ref, w_ref, cw_ref, h_ref, p_ref, wg_out, cwg_out,
             wg_v, cwg_v, send_sems, recv_sems, out_sems):
        s, t = pl.program_id(0), pl.program_id(1)
        x, y, c = lax.axis_index("x"), lax.axis_index("y"), lax.axis_index("c")
        k = 2 * x + y
        sibling = (x, y, 1 - c)
        chips = [(1 - x, y), (x, 1 - y), (1 - x, 1 - y)]
        kjs = [2 * cx + cy for cx, cy in chips]

        def w_half(kk, cc):
            return wg_v.at[kk, pl.ds(cc * half_w, half_w), :]

        def cw_of(kk):
            return cwg_v.at[:, pl.ds(pl.multiple_of(kk * HEAD, HEAD), HEAD)]

        def copy(sem, ref, to):
            return pltpu.make_async_remote_copy(
                src_ref=ref, dst_ref=ref, send_sem=send_sems.at[sem], recv_sem=recv_sems.at[sem],
                device_id=to, device_id_type=MESH)

        def at_step(sv, tv):
            return pl.when((s == sv) & (t == tv))

        w_direct = [copy(SEM_W + j, w_half(k, c), (*chip, c)) for j, chip in enumerate(chips)]
        cw_direct = [copy(SEM_CW + j, cw_of(k), (*chip, c)) for j, chip in enumerate(chips)]
        w_passed = [copy(SEM_W_FWD + j, w_half(kj, c), sibling) for j, kj in enumerate(kjs)]
        stores = ([pltpu.make_async_copy(wg_v.at[kk], wg_out.at[kk], out_sems.at[i])
                   for i, kk in enumerate([k] + kjs)]
                  + [pltpu.make_async_copy(cwg_v, cwg_out, out_sems.at[4])])

        @at_step(0, 0)
        def _():
            wg_v[k] = w_ref[0].astype(BF16)
            mine = pl.ds(pl.multiple_of(k * HEAD, HEAD), HEAD)
            cwg_v[:, mine] = jnp.zeros((8, HEAD), F32)
            for tap in range(3):
                cwg_v[tap:tap + 1, mine] = cw_ref[:, tap * HEAD:(tap + 1) * HEAD]
            w_direct[0].start()
            w_direct[1].start()
            for cp in cw_direct:
                cp.start()
            stores[0].start()

        @at_step(1, 0)
        def _():
            for j in range(2):
                copy(SEM_W + j, w_half(kjs[j], c), sibling).wait_recv()
                w_passed[j].start()
            w_direct[2].start()
            copy(SEM_W_FWD, w_half(kjs[0], 1 - c), sibling).wait_recv()
            stores[1].start()

        @at_step(2, 0)
        def _():
            copy(SEM_W_FWD + 1, w_half(kjs[1], 1 - c), sibling).wait_recv()
            stores[2].start()

        @at_step(3, 0)
        def _():
            copy(SEM_W + 2, w_half(kjs[2], c), sibling).wait_recv()
            w_passed[2].start()
            copy(SEM_W_FWD + 2, w_half(kjs[2], 1 - c), sibling).wait_recv()
            stores[3].start()

        rows = pl.ds(pl.multiple_of(t * TP, TP), TP)

        @pl.when(s == 0)
        def _():
            xv = x_ref[...]
            r = lax.rsqrt(jnp.mean(xv * xv, axis=-1, keepdims=True) + EPS)
            h_ref[rows, :] = (xv * r * g_ref[...]).astype(BF16)

        js = k ^ (((s & 1) << 1) | (s >> 1))
        p_ref[...] = _dot(h_ref[rows, :], wg_v[js])

        @at_step(N_SHARD - 1, nt - 1)
        def _():
            for j in range(3):
                copy(SEM_CW + j, cw_of(kjs[j]), sibling).wait_recv()
            stores[4].start()
            for cp in w_direct + cw_direct + w_passed:
                cp.wait_send()
            for st in stores:
                st.wait()

    def x_map(s, t, kr):
        return (jnp.where(s == 0, t, nt - 1), 0)

    def p_map(s, t, kr):
        return (t, kr[0] ^ (((s & 1) << 1) | (s >> 1)))

    hbm = pl.BlockSpec(memory_space=pl.ANY)
    grid_spec = pltpu.PrefetchScalarGridSpec(
        num_scalar_prefetch=1, grid=(N_SHARD, nt),
        in_specs=[pl.BlockSpec((TP, D_MODEL), x_map),
                  pl.BlockSpec((1, D_MODEL), lambda s, t, kr: (0, 0)),
                  pl.BlockSpec((1, D_MODEL, SHARD_COLS), lambda s, t, kr: (0, 0, 0)),
                  pl.BlockSpec((1, 3 * HEAD), lambda s, t, kr: (0, 0))],
        out_specs=(pl.BlockSpec((SEQ, D_MODEL), lambda s, t, kr: (0, 0)),
                   pl.BlockSpec((TP, SHARD_COLS), p_map), hbm, hbm),
        scratch_shapes=[pltpu.VMEM((N_SHARD, D_MODEL, SHARD_COLS), BF16),
                        pltpu.VMEM((8, D_CONV), F32),
                        pltpu.SemaphoreType.DMA((N_SEM,)), pltpu.SemaphoreType.DMA((N_SEM,)),
                        pltpu.SemaphoreType.DMA((5,))])
    return pl.pallas_call(
        body, name="gather_proj", grid_spec=grid_spec,
        out_shape=(jax.ShapeDtypeStruct((SEQ, D_MODEL), BF16),
                   jax.ShapeDtypeStruct((SEQ, N_SHARD * SHARD_COLS), F32),
                   jax.ShapeDtypeStruct((N_SHARD, D_MODEL, SHARD_COLS), BF16),
                   jax.ShapeDtypeStruct((8, D_CONV), F32)),
        compiler_params=pltpu.CompilerParams(dimension_semantics=("arbitrary", "arbitrary"),
                                             vmem_limit_bytes=VMEM_LIMIT),
    )(kidx, x2d, g1, w_in, conv_w)


def _mix_fwd(proj, lb_logits, cw, ga, gcn, g64, w_out):
    half_o = WO_ROWS // 2
    nblk = SEQ // TB

    def body(p_ref, lbl_ref, cw_ref, ga_ref, gcn_ref, g64_ref, wo_ref,
             mixed_ref, o_ref, cv_ref, sto_ref, sg_ref, b_ref, wog_out,
             st_ref, tail_ref, wog_v, send_sems, recv_sems, out_sem):
        i = pl.program_id(0)
        x, y, c = lax.axis_index("x"), lax.axis_index("y"), lax.axis_index("c")
        k = 2 * x + y
        sibling = (x, y, 1 - c)
        chips = [(1 - x, y), (x, 1 - y), (1 - x, 1 - y)]
        kjs = [2 * cx + cy for cx, cy in chips]

        def wo_half(kk, cc):
            return wog_v.at[kk, pl.ds(cc * half_o, half_o), :]

        def copy(sem, ref, to):
            return pltpu.make_async_remote_copy(
                src_ref=ref, dst_ref=ref, send_sem=send_sems.at[sem], recv_sem=recv_sems.at[sem],
                device_id=to, device_id_type=MESH)

        wo_direct = [copy(j, wo_half(k, c), (*chip, c)) for j, chip in enumerate(chips)]
        wo_passed = [copy(3 + j, wo_half(kj, c), sibling) for j, kj in enumerate(kjs)]
        wo_store = pltpu.make_async_copy(wog_v, wog_out, out_sem.at[0])

        @pl.when(i == 0)
        def _():
            st_ref[...] = jnp.zeros_like(st_ref)
            tail_ref[...] = jnp.zeros_like(tail_ref)
            wog_v[k] = wo_ref[0].astype(BF16)
            for cp in wo_direct:
                cp.start()

        @pl.when(i == nblk - 2)
        def _():
            for j in range(3):
                copy(j, wo_half(kjs[j], c), sibling).wait_recv()
                wo_passed[j].start()

        lb = _lower_bound(lbl_ref[...])
        tri = _tri(True)
        causal = _causal()
        g64m = g64_ref[...]
        heads = range(N_HEADS)
        cs = [slice(hd * HEAD, (hd + 1) * HEAD) for hd in heads]
        col = lambda base, hd: slice(base + hd * HEAD, base + (hd + 1) * HEAD)
        for n in range(NCB):
            sl = pl.ds(n * CHUNK, CHUNK)
            sg = [_sigmoid(p_ref[sl, col(512, hd)]) for hd in heads]
            f = [lb[:, cs[hd]] + (1.0 - lb[:, cs[hd]]) * sg[hd] for hd in heads]
            bc = _exact_left_many(tri, [jnp.log(f[hd]) for hd in heads])
            for hd in heads:
                sg_ref[sl, cs[hd]] = sg[hd]
                b_ref[sl, cs[hd]] = bc[hd]
            g = [bc[hd][CHUNK - 1:CHUNK, :] for hd in heads]
            qd = [(p_ref[sl, col(0, hd)] * jnp.exp(bc[hd])).astype(BF16) for hd in heads]
            ki = [((1.0 - f[hd]) * jnp.exp(-bc[hd])).astype(BF16) for hd in heads]
            ke = [((1.0 - f[hd]) * jnp.exp(g[hd] - bc[hd])).astype(BF16) for hd in heads]
            vb = [p_ref[sl, col(1024, hd)].astype(BF16) for hd in heads]
            st = [st_ref[hd] for hd in heads]
            for hd in heads:
                sto_ref[n, hd] = st[hd]
            scm = [_dot_nt(qd[hd], ki[hd]) for hd in heads]
            inter = [_dot_nt(qd[hd], st[hd].astype(BF16)) for hd in heads]
            upd = [_dot_tn(vb[hd], ke[hd]) for hd in heads]
            intra = [_dot(jnp.where(causal, scm[hd], 0.0).astype(BF16), vb[hd]) for hd in heads]
            for hd in heads:
                st_ref[hd] = st[hd] * jnp.exp(g[hd]) + upd[hd]
                o = intra[hd] + inter[hd]
                o_ref[sl, cs[hd]] = o
                ra = lax.rsqrt(jnp.mean(o * o, axis=-1, keepdims=True) + EPS)
                za = p_ref[sl, col(1536, hd)]
                mixed_ref[sl, cs[hd]] = (o * ra * ga_ref[:, cs[hd]] * (za * _sigmoid(za))).astype(BF16)
            yb = []
            for hd in heads:
                cu = p_ref[sl, col(3072, hd)] * p_ref[sl, col(2048, hd)]
                tail = tail_ref[:, cs[hd]]
                cv = (cw_ref[0:1, cs[hd]] * _shift_down(cu, 2, tail) + cw_ref[1:2, cs[hd]] * _shift_down(cu, 1, tail)
                      + cw_ref[2:3, cs[hd]] * cu)
                tail_ref[:, cs[hd]] = cu[CHUNK - 8:, :]
                cv_ref[sl, cs[hd]] = cv
                yb.append(p_ref[sl, col(2560, hd)] * cv)
            ms = _group_mean_many([y * y for y in yb], g64m)
            for hd in heads:
                rb = lax.rsqrt(ms[hd] + EPS)
                zb = p_ref[sl, col(3584, hd)]
                mixed_ref[sl, col(512, hd)] = (yb[hd] * rb * gcn_ref[:, cs[hd]] * (zb * _sigmoid(zb))).astype(BF16)

        @pl.when(i == nblk - 1)
        def _():
            for j in range(3):
                copy(3 + j, wo_half(kjs[j], 1 - c), sibling).wait_recv()
            wo_store.start()
            for cp in wo_direct + wo_passed:
                cp.wait_send()
            wo_store.wait()

    row = lambda w: pl.BlockSpec((1, w), lambda i: (0, 0))
    return pl.pallas_call(
        body, name="mix_fwd", grid=(nblk,),
        out_shape=(jax.ShapeDtypeStruct((SEQ, D_MODEL), BF16),
                   jax.ShapeDtypeStruct((SEQ, D_HGRN), F32),
                   jax.ShapeDtypeStruct((SEQ, D_CONV), F32),
                   jax.ShapeDtypeStruct((N_CHUNKS, N_HEADS, HEAD, HEAD), F32),
                   jax.ShapeDtypeStruct((SEQ, D_HGRN), F32),
                   jax.ShapeDtypeStruct((SEQ, D_HGRN), F32),
                   jax.ShapeDtypeStruct((N_SHARD, WO_ROWS, D_MODEL), BF16)),
        in_specs=[pl.BlockSpec((TB, 4096), lambda i: (i, 0)),
                  pl.BlockSpec((2, D_HGRN), lambda i: (0, 0)),
                  pl.BlockSpec((8, D_CONV), lambda i: (0, 0)),
                  row(D_HGRN), row(D_CONV),
                  pl.BlockSpec((HEAD, HEAD), lambda i: (0, 0)),
                  pl.BlockSpec((1, WO_ROWS, D_MODEL), lambda i: (0, 0, 0))],
        out_specs=(pl.BlockSpec((TB, D_MODEL), lambda i: (i, 0)),
                   pl.BlockSpec((TB, D_HGRN), lambda i: (i, 0)),
                   pl.BlockSpec((TB, D_CONV), lambda i: (i, 0)),
                   pl.BlockSpec((NCB, N_HEADS, HEAD, HEAD), lambda i: (i, 0, 0, 0)),
                   pl.BlockSpec((TB, D_HGRN), lambda i: (i, 0)),
                   pl.BlockSpec((TB, D_HGRN), lambda i: (i, 0)),
                   pl.BlockSpec(memory_space=pl.ANY)),
        scratch_shapes=[pltpu.VMEM((N_HEADS, HEAD, HEAD), F32), pltpu.VMEM((8, D_CONV), F32),
                        pltpu.VMEM((N_SHARD, WO_ROWS, D_MODEL), BF16),
                        pltpu.SemaphoreType.DMA((6,)), pltpu.SemaphoreType.DMA((6,)),
                        pltpu.SemaphoreType.DMA((1,))],
        compiler_params=pltpu.CompilerParams(dimension_semantics=("arbitrary",), vmem_limit_bytes=VMEM_LIMIT),
    )(proj, lb_logits, cw, ga, gcn, g64, w_out)


def _out_loss(x2d, mixed, wog, gf, tgt):
    def body(x_ref, m_ref, wo_ref, gf_ref, t_ref, dx2_ref, dm_ref, gwo_ref, part_ref, acc_ref):
        i = pl.program_id(0)

        @pl.when(i == 0)
        def _():
            acc_ref[...] = jnp.zeros_like(acc_ref)
            part_ref[...] = jnp.zeros_like(part_ref)

        mixed_b = m_ref[...]
        x2 = x_ref[...] + _dot(mixed_b, wo_ref[...])
        r2 = lax.rsqrt(jnp.mean(x2 * x2, axis=-1, keepdims=True) + EPS)
        n2 = x2 * r2
        gfv = gf_ref[...]
        err = n2 * gfv - t_ref[...]
        loss = 0.5 * jnp.sum(jnp.mean(err * err, axis=-1, keepdims=True), axis=0, keepdims=True)
        dy = err * (1.0 / D_MODEL)
        part_ref[1:2, :] += jnp.sum(dy * n2, axis=0, keepdims=True)
        part_ref[7:8, :] += jnp.broadcast_to(loss, (1, D_MODEL))
        dn = dy * gfv
        dx2 = r2 * (dn - n2 * jnp.mean(dn * n2, axis=-1, keepdims=True))
        dx2_ref[...] = dx2
        dx2_b = dx2.astype(BF16)
        dm_ref[...] = _dot_nt(dx2_b, wo_ref[...])
        acc_ref[...] += _dot_tn(mixed_b, dx2_b)

        @pl.when(i == pl.num_programs(0) - 1)
        def _():
            gwo_ref[...] = acc_ref[...].astype(BF16)

    blk = lambda: pl.BlockSpec((TP, D_MODEL), lambda i: (i, 0))
    return pl.pallas_call(
        body, name="out_loss", grid=(SEQ // TP,),
        out_shape=(jax.ShapeDtypeStruct((SEQ, D_MODEL), F32),
                   jax.ShapeDtypeStruct((SEQ, D_MODEL), F32),
                   jax.ShapeDtypeStruct((D_MODEL, D_MODEL), BF16),
                   jax.ShapeDtypeStruct((8, D_MODEL), F32)),
        in_specs=[blk(), blk(), pl.BlockSpec((D_MODEL, D_MODEL), lambda i: (0, 0)),
                  pl.BlockSpec((1, D_MODEL), lambda i: (0, 0)), blk()],
        out_specs=(blk(), blk(), pl.BlockSpec((D_MODEL, D_MODEL), lambda i: (0, 0)),
                   pl.BlockSpec((8, D_MODEL), lambda i: (0, 0))),
        scratch_shapes=[pltpu.VMEM((D_MODEL, D_MODEL), F32)],
        compiler_params=pltpu.CompilerParams(dimension_semantics=("arbitrary",), vmem_limit_bytes=VMEM_LIMIT),
    )(x2d, mixed, wog, gf, tgt)


def _mix_bwd(proj, o, cv, states, sg, b, dmixed, lb_logits, cw, ga, gcn, g64, gwo):
    nblk = SEQ // TB
    ho = WO_ROWS // 2

    def body(p_ref, o_ref, cv_ref, st_ref, sg_ref, b_ref, dm_ref, lbl_ref, cw_ref, ga_ref, gcn_ref, g64_ref,
             gwo_ref, dp_ref, part_ref, gwo_out, dst_ref, head_ref, dlb_ref,
             sib_o, p_o, rcv_o, res_o, send_sems, recv_sems, out_sem):
        i = pl.program_id(0)
        x, y, c = lax.axis_index("x"), lax.axis_index("y"), lax.axis_index("c")
        k = 2 * x + y
        sibling = (x, y, 1 - c)
        chips = [(1 - x, 1 - y), (1 - x, y), (x, 1 - y)]
        kjs = [2 * cx + cy for cx, cy in chips]
        mine_o = pl.ds(pl.multiple_of(c * ho, ho), ho)
        other_o = pl.ds(pl.multiple_of((1 - c) * ho, ho), ho)

        def copy(sem, src, dst, to):
            return pltpu.make_async_remote_copy(
                src_ref=src, dst_ref=dst, send_sem=send_sems.at[sem], recv_sem=recv_sems.at[sem],
                device_id=to, device_id_type=MESH)

        d2d_o = copy(0, gwo_ref.at[:, other_o, :], sib_o, sibling)
        ici_o = [copy(1 + j, p_o.at[kjs[j]], rcv_o.at[j], (*chips[j], c)) for j in range(3)]
        fin_o = copy(4, res_o.at[mine_o, :], res_o.at[mine_o, :], sibling)
        store_o = pltpu.make_async_copy(res_o, gwo_out, out_sem.at[0])

        @pl.when(i == 0)
        def _():
            dst_ref[...] = jnp.zeros_like(dst_ref)
            head_ref[...] = jnp.zeros_like(head_ref)
            part_ref[...] = jnp.zeros_like(part_ref)
            dlb_ref[...] = jnp.zeros_like(dlb_ref)
            d2d_o.start()

        @pl.when(i == 1)
        def _():
            d2d_o.wait_recv()
            for j in range(N_SHARD):
                p_o[j] = (gwo_ref[j, mine_o, :].astype(F32) + sib_o[j].astype(F32)).astype(BF16)
            res_o[mine_o, :] = gwo_ref[k, mine_o, :].astype(F32) + sib_o[k].astype(F32)
            for cp in ici_o:
                cp.start()

        @pl.when(i == nblk - 2)
        def _():
            tot = res_o[mine_o, :]
            for j in range(3):
                ici_o[j].wait_recv()
                tot = tot + rcv_o[j].astype(F32)
            res_o[mine_o, :] = tot
            fin_o.start()

        lb = _lower_bound(lbl_ref[...])
        triu = _tri(False)
        causal = _causal()
        g64m = g64_ref[...]
        rowsum = lambda a: jnp.sum(a, axis=0, keepdims=True)
        heads = range(N_HEADS)
        cs = [slice(hd * HEAD, (hd + 1) * HEAD) for hd in heads]
        col = lambda base, hd: slice(base + hd * HEAD, base + (hd + 1) * HEAD)
        for n in reversed(range(NCB)):
            sl = pl.ds(n * CHUNK, CHUNK)
            cvv = [cv_ref[sl, cs[hd]] for hd in heads]
            gb = [p_ref[sl, col(2560, hd)] for hd in heads]
            yb = [gb[hd] * cvv[hd] for hd in heads]
            ms = _group_mean_many([y * y for y in yb], g64m)
            rb, nb, dnb = [], [], []
            for hd in heads:
                rb.append(lax.rsqrt(ms[hd] + EPS))
                nb.append(yb[hd] * rb[hd])
                zb = p_ref[sl, col(3584, hd)]
                sgb = _sigmoid(zb)
                dmb = dm_ref[sl, col(512, hd)]
                gcv = gcn_ref[:, cs[hd]]
                part_ref[2:3, col(512, hd)] += rowsum(dmb * nb[hd] * (zb * sgb))
                dp_ref[sl, col(3584, hd)] = (dmb * nb[hd] * gcv * (sgb * (1.0 + zb * (1.0 - sgb)))).astype(BF16)
                dnb.append(dmb * gcv * (zb * sgb))
            mdn = _group_mean_many([dnb[hd] * nb[hd] for hd in heads], g64m)
            for hd in heads:
                dyb = rb[hd] * (dnb[hd] - nb[hd] * mdn[hd])
                dp_ref[sl, col(2560, hd)] = (dyb * cvv[hd]).astype(BF16)
                dcv = dyb * gb[hd]
                head = head_ref[:, cs[hd]]
                dcv1 = _shift_up(dcv, 1, head)
                dcv2 = _shift_up(dcv, 2, head)
                head_ref[:, cs[hd]] = dcv[0:8, :]
                u = p_ref[sl, col(2048, hd)]
                gc = p_ref[sl, col(3072, hd)]
                cu = gc * u
                part_ref[4:5, cs[hd]] += rowsum(dcv2 * cu)
                part_ref[5:6, cs[hd]] += rowsum(dcv1 * cu)
                part_ref[6:7, cs[hd]] += rowsum(dcv * cu)
                dcu = cw_ref[2:3, cs[hd]] * dcv + cw_ref[1:2, cs[hd]] * dcv1 + cw_ref[0:1, cs[hd]] * dcv2
                dp_ref[sl, col(3072, hd)] = (dcu * u).astype(BF16)
                dp_ref[sl, col(2048, hd)] = (dcu * gc).astype(BF16)
            do_b = []
            for hd in heads:
                ov = o_ref[sl, cs[hd]]
                ra = lax.rsqrt(jnp.mean(ov * ov, axis=-1, keepdims=True) + EPS)
                na = ov * ra
                za = p_ref[sl, col(1536, hd)]
                sga = _sigmoid(za)
                dma = dm_ref[sl, cs[hd]]
                gav = ga_ref[:, cs[hd]]
                part_ref[2:3, cs[hd]] += rowsum(dma * na * (za * sga))
                dp_ref[sl, col(1536, hd)] = (dma * na * gav * (sga * (1.0 + za * (1.0 - sga)))).astype(BF16)
                dna = dma * gav * (za * sga)
                do_b.append((ra * (dna - na * jnp.mean(dna * na, axis=-1, keepdims=True))).astype(BF16))
            s = [sg_ref[sl, cs[hd]] for hd in heads]
            f = [lb[:, cs[hd]] + (1.0 - lb[:, cs[hd]]) * s[hd] for hd in heads]
            bc = [b_ref[sl, cs[hd]] for hd in heads]
            g = [bc[hd][CHUNK - 1:CHUNK, :] for hd in heads]
            eb = [jnp.exp(bc[hd]) for hd in heads]
            enb = [jnp.exp(-bc[hd]) for hd in heads]
            eg = [jnp.exp(g[hd] - bc[hd]) for hd in heads]
            dec = [jnp.exp(g[hd]) for hd in heads]
            qd = [p_ref[sl, cs[hd]] * eb[hd] for hd in heads]
            ki = [(1.0 - f[hd]) * enb[hd] for hd in heads]
            ke = [(1.0 - f[hd]) * eg[hd] for hd in heads]
            qd_b = [a.astype(BF16) for a in qd]
            ki_b = [a.astype(BF16) for a in ki]
            ke_b = [a.astype(BF16) for a in ke]
            vb = [p_ref[sl, col(1024, hd)].astype(BF16) for hd in heads]
            st = [st_ref[n, hd] for hd in heads]
            dst = [dst_ref[hd] for hd in heads]
            st_b = [a.astype(BF16) for a in st]
            dst_b = [a.astype(BF16) for a in dst]
            scm = [_dot_nt(qd_b[hd], ki_b[hd]) for hd in heads]
            amm = [_dot_nt(do_b[hd], vb[hd]) for hd in heads]
            dqd2 = [_dot(do_b[hd], st_b[hd]) for hd in heads]
            dke = [_dot(vb[hd], dst_b[hd]) for hd in heads]
            dv2 = [_dot_nt(ke_b[hd], dst_b[hd]) for hd in heads]
            dsu = [_dot_tn(do_b[hd], qd_b[hd]) for hd in heads]
            sc = [jnp.where(causal, scm[hd], 0.0).astype(BF16) for hd in heads]
            am = [jnp.where(causal, amm[hd], 0.0).astype(BF16) for hd in heads]
            dqd1 = [_dot(am[hd], ki_b[hd]) for hd in heads]
            dki = [_dot_tn(am[hd], qd_b[hd]) for hd in heads]
            dv1 = [_dot_tn(sc[hd], do_b[hd]) for hd in heads]
            db, dgv = [], []
            for hd in heads:
                dqd = dqd1[hd] + dqd2[hd]
                ddec = rowsum(dst[hd] * st[hd])
                dst_ref[hd] = dst[hd] * dec[hd] + dsu[hd]
                dp_ref[sl, cs[hd]] = (dqd * eb[hd]).astype(BF16)
                dp_ref[sl, col(1024, hd)] = (dv1[hd] + dv2[hd]).astype(BF16)
                db.append(dqd * qd[hd] - dki[hd] * ki[hd] - dke[hd] * ke[hd])
                dgv.append(rowsum(dke[hd] * ke[hd]) + ddec * dec[hd])
            rc = _exact_left_many(triu, db, 2)
            for hd in heads:
                df = (rc[hd] + dgv[hd]) / f[hd] - (dki[hd] * enb[hd] + dke[hd] * eg[hd])
                dlb_ref[:, cs[hd]] += rowsum(df * (1.0 - s[hd]))
                dp_ref[sl, col(512, hd)] = (df * (1.0 - lb[:, cs[hd]]) * s[hd] * (1.0 - s[hd])).astype(BF16)

        @pl.when(i == nblk - 1)
        def _():
            row = dlb_ref[...] * lb * (1.0 - lb)
            part_ref[3:4, 0:D_HGRN] = row
            part_ref[3:4, D_HGRN:] = -row
            fin_o.wait_recv()
            store_o.start()
            for cp in [d2d_o, fin_o] + ici_o:
                cp.wait_send()
            store_o.wait()

    rev = lambda w: pl.BlockSpec((TB, w), lambda i: (nblk - 1 - i, 0))
    row = lambda w: pl.BlockSpec((1, w), lambda i: (0, 0))
    return pl.pallas_call(
        body, name="mix_bwd", grid=(nblk,),
        out_shape=(jax.ShapeDtypeStruct((SEQ, 4096), BF16),
                   jax.ShapeDtypeStruct((8, D_MODEL), F32),
                   jax.ShapeDtypeStruct((WO_ROWS, D_MODEL), F32)),
        in_specs=[rev(4096), rev(D_HGRN), rev(D_CONV),
                  pl.BlockSpec((NCB, N_HEADS, HEAD, HEAD), lambda i: (nblk - 1 - i, 0, 0, 0)),
                  rev(D_HGRN), rev(D_HGRN), rev(D_MODEL),
                  pl.BlockSpec((2, D_HGRN), lambda i: (0, 0)),
                  pl.BlockSpec((8, D_CONV), lambda i: (0, 0)),
                  row(D_HGRN), row(D_CONV),
                  pl.BlockSpec((HEAD, HEAD), lambda i: (0, 0)),
                  pl.BlockSpec((N_SHARD, WO_ROWS, D_MODEL), lambda i: (0, 0, 0))],
        out_specs=(rev(4096), pl.BlockSpec((8, D_MODEL), lambda i: (0, 0)), pl.BlockSpec(memory_space=pl.ANY)),
        scratch_shapes=[pltpu.VMEM((N_HEADS, HEAD, HEAD), F32), pltpu.VMEM((8, D_CONV), F32),
                        pltpu.VMEM((1, D_HGRN), F32),
                        pltpu.VMEM((N_SHARD, ho, D_MODEL), BF16), pltpu.VMEM((N_SHARD, ho, D_MODEL), BF16),
                        pltpu.VMEM((3, ho, D_MODEL), BF16), pltpu.VMEM((WO_ROWS, D_MODEL), F32),
                        pltpu.SemaphoreType.DMA((5,)), pltpu.SemaphoreType.DMA((5,)),
                        pltpu.SemaphoreType.DMA((1,))],
        compiler_params=pltpu.CompilerParams(dimension_semantics=("arbitrary",), vmem_limit_bytes=VMEM_LIMIT),
    )(proj, o, cv, states, sg, b, dmixed, lb_logits, cw, ga, gcn, g64, gwo)


TT = 1024
TX = 256
SEM_D2D, SEM_ICI, SEM_FIN, SEM_SMALL, N_SEM_TAIL = 0, 4, 7, 7, 15


def _bwd_tail(kidx, h, dproj, wg, x2d, dx2, g1, small_a, small_b):
    hw = D_MODEL // 2
    nt = SEQ // TT
    n_steps = N_SHARD + SEQ // TX // nt

    def body(k_ref, h_ref, dp_ref, w_ref, x_ref, dx2_ref, g_ref, sm_ref, smb_ref,
             gx_ref, gw_out, osm_ref,
             acc, dh, sendbuf, keep, sibrcv, rcv, sm_buf, dng,
             send_sems, recv_sems, out_sems):
        s, t = pl.program_id(0), pl.program_id(1)
        x, y, c = lax.axis_index("x"), lax.axis_index("y"), lax.axis_index("c")
        me = 4 * x + 2 * y + c
        sibling = (x, y, 1 - c)
        chips = [(1 - x, 1 - y), (1 - x, y), (x, 1 - y)]
        mine = pl.ds(pl.multiple_of(c * hw, hw), hw)
        other = pl.ds(pl.multiple_of((1 - c) * hw, hw), hw)

        def copy(sem, src, dst, to):
            return pltpu.make_async_remote_copy(
                src_ref=src, dst_ref=dst, send_sem=send_sems.at[sem], recv_sem=recv_sems.at[sem],
                device_id=to, device_id_type=MESH)

        def at_step(sv, tv):
            return pl.when((s == sv) & (t == tv))

        def at_norm_block(b):
            return at_step(N_SHARD + b // nt, b % nt)

        d2d = [copy(SEM_D2D + sv, sendbuf.at[sv], sibrcv.at[sv], sibling) for sv in range(N_SHARD)]
        ici = [copy(SEM_ICI + sv, keep.at[sv], rcv.at[sv], (*chips[sv], c)) for sv in range(3)]
        fin = copy(SEM_FIN, acc.at[mine, :], acc.at[mine, :], sibling)
        smalls = [copy(SEM_SMALL + m, sm_buf.at[me], sm_buf.at[me],
                       (x ^ (m >> 2), y ^ ((m >> 1) & 1), c ^ (m & 1))) for m in range(1, N_DEV)]
        store_w = pltpu.make_async_copy(acc, gw_out, out_sems.at[0])

        rows = pl.ds(pl.multiple_of(t * TT, TT), TT)

        @pl.when(s < N_SHARD)
        def _():
            dpb = dp_ref[...]
            part = _dot_tn(h_ref[...], dpb)

            @pl.when(t == 0)
            def _():
                acc[...] = part

            @pl.when(t > 0)
            def _():
                acc[...] += part

            d = _dot_nt(dpb, w_ref[0])

            @pl.when(s == 0)
            def _():
                dh[rows, :] = d

            @pl.when(s > 0)
            def _():
                dh[rows, :] += d

        for sv in range(N_SHARD):
            @at_step(sv, nt - 1)
            def _(sv=sv):
                sendbuf[sv] = acc[other, :].astype(BF16)
                if sv < 3:
                    keep[sv] = acc[mine, :].astype(BF16)
                d2d[sv].start()

        for sv in range(3):
            @at_step(sv + 1, 0)
            def _(sv=sv):
                d2d[sv].wait_recv()
                keep[sv] = (keep[sv].astype(F32) + sibrcv[sv].astype(F32)).astype(BF16)
                ici[sv].start()

        @at_norm_block(0)
        def _():
            d2d[3].wait_recv()
            ici[0].wait_recv()
            acc[mine, :] += sibrcv[3].astype(F32) + rcv[0].astype(F32)

        @at_norm_block(2)
        def _():
            ici[1].wait_recv()
            acc[mine, :] += rcv[1].astype(F32)

        @at_norm_block(0)
        def _():
            dng[...] = jnp.zeros_like(dng)

        @pl.when(s >= N_SHARD)
        def _():
            blk = (s - N_SHARD) * nt + t
            dhv = dh[pl.ds(pl.multiple_of(blk * TX, TX), TX), :]
            xv = x_ref[...]
            r = lax.rsqrt(jnp.mean(xv * xv, axis=-1, keepdims=True) + EPS)
            xn = xv * r
            dng[...] += jnp.sum(dhv * xn, axis=0, keepdims=True)
            dxn = dhv * g_ref[...]
            gx_ref[...] = dx2_ref[...] + r * (dxn - xn * jnp.mean(dxn * xn, axis=-1, keepdims=True))

        @at_step(n_steps - 1, nt - 1)
        def _():
            sm_buf[me] = sm_ref[...] + smb_ref[...]
            sm_buf[me, 0:1, :] = dng[...]
            for cp in smalls:
                cp.start()
            ici[2].wait_recv()
            acc[mine, :] += rcv[2].astype(F32)
            fin.start()
            for m in range(1, N_DEV):
                copy(SEM_SMALL + m, sm_buf.at[0], sm_buf.at[0], sibling).wait_recv()
            tot = sm_buf[0]
            for d in range(1, N_DEV):
                tot = tot + sm_buf[d]
            osm_ref[...] = tot
            fin.wait_recv()
            store_w.start()
            for cp in d2d + ici + [fin] + smalls:
                cp.wait_send()
            store_w.wait()

    def shard_of(s, kr):
        return kr[0] ^ (3 - jnp.minimum(s, 3))

    def tok(s, t):
        return jnp.where(s < N_SHARD, t, nt - 1)

    def blk_map(s, t, kr):
        return (jnp.where(s < N_SHARD, 0, (s - N_SHARD) * nt + t), 0)

    hbm = pl.BlockSpec(memory_space=pl.ANY)
    grid_spec = pltpu.PrefetchScalarGridSpec(
        num_scalar_prefetch=1, grid=(n_steps, nt),
        in_specs=[pl.BlockSpec((TT, D_MODEL), lambda s, t, kr: (tok(s, t), 0)),
                  pl.BlockSpec((TT, SHARD_COLS), lambda s, t, kr: (tok(s, t), shard_of(s, kr))),
                  pl.BlockSpec((1, D_MODEL, SHARD_COLS), lambda s, t, kr: (shard_of(s, kr), 0, 0)),
                  pl.BlockSpec((TX, D_MODEL), blk_map),
                  pl.BlockSpec((TX, D_MODEL), blk_map),
                  pl.BlockSpec((1, D_MODEL), lambda s, t, kr: (0, 0)),
                  pl.BlockSpec((8, D_MODEL), lambda s, t, kr: (0, 0)),
                  pl.BlockSpec((8, D_MODEL), lambda s, t, kr: (0, 0))],
        out_specs=(pl.BlockSpec((TX, D_MODEL), blk_map), hbm,
                   pl.BlockSpec((8, D_MODEL), lambda s, t, kr: (0, 0))),
        scratch_shapes=[pltpu.VMEM((D_MODEL, SHARD_COLS), F32), pltpu.VMEM((SEQ, D_MODEL), F32),
                        pltpu.VMEM((N_SHARD, hw, SHARD_COLS), BF16), pltpu.VMEM((3, hw, SHARD_COLS), BF16),
                        pltpu.VMEM((N_SHARD, hw, SHARD_COLS), BF16), pltpu.VMEM((3, hw, SHARD_COLS), BF16),
                        pltpu.VMEM((N_DEV, 8, D_MODEL), F32), pltpu.VMEM((1, D_MODEL), F32),
                        pltpu.SemaphoreType.DMA((N_SEM_TAIL,)), pltpu.SemaphoreType.DMA((N_SEM_TAIL,)),
                        pltpu.SemaphoreType.DMA((1,))])
    return pl.pallas_call(
        body, name="bwd_tail", grid_spec=grid_spec,
        out_shape=(jax.ShapeDtypeStruct((SEQ, D_MODEL), F32),
                   jax.ShapeDtypeStruct((D_MODEL, SHARD_COLS), F32),
                   jax.ShapeDtypeStruct((8, D_MODEL), F32)),
        compiler_params=pltpu.CompilerParams(dimension_semantics=("arbitrary", "arbitrary"),
                                             vmem_limit_bytes=VMEM_LIMIT),
    )(kidx, h, dproj, wg, x2d, dx2, g1, small_a, small_b)


def _adam_update(w, g, m, v):
    nm = ADAM_B1 * m + (1.0 - ADAM_B1) * g
    nv = ADAM_B2 * v + (1.0 - ADAM_B2) * (g * g)
    m_hat = nm / (1.0 - ADAM_B1 ** ADAM_STEP)
    v_hat = nv / (1.0 - ADAM_B2 ** ADAM_STEP)
    return -ADAM_LR * (m_hat / (jnp.sqrt(v_hat) + ADAM_EPS) + ADAM_WD * w), nm, nv


def _adamw_small(tot, params):
    n = len(params)

    def body(tot_ref, *refs):
        ins, loss_ref, outs = refs[:3 * n], refs[3 * n], refs[3 * n + 1:]
        k = 2 * lax.axis_index("x") + lax.axis_index("y")
        mine = pl.ds(pl.multiple_of(k * HEAD, HEAD), HEAD)
        loss_ref[...] = tot_ref[7:8, 0:1]
        grads = [tot_ref[0:1, :], tot_ref[1:2, :], tot_ref[2:3, 0:D_HGRN], tot_ref[2:3, D_HGRN:],
                 jnp.concatenate([tot_ref[3:4, 0:D_HGRN], tot_ref[3:4, D_HGRN:]], axis=0),
                 jnp.concatenate([tot_ref[4 + tap:5 + tap, mine] for tap in range(3)], axis=1)]
        for i, g in enumerate(grads):
            w_ref, m_ref, v_ref = ins[3 * i:3 * i + 3]
            g_ref, d_ref, nm_ref, nv_ref = outs[4 * i:4 * i + 4]
            g_ref[...] = g
            d_ref[...], nm_ref[...], nv_ref[...] = _adam_update(w_ref[...], g, m_ref[...], v_ref[...])

    vm = pl.BlockSpec(memory_space=pltpu.VMEM)
    flat = [a for triple in params for a in triple]
    out_shape = (jax.ShapeDtypeStruct((1, 1), F32),) + tuple(
        jax.ShapeDtypeStruct(w.shape, F32) for w, _, _ in params for _ in range(4))
    outs = pl.pallas_call(
        body, name="adamw_small", out_shape=out_shape,
        in_specs=[vm] * (1 + 3 * n), out_specs=tuple([vm] * (1 + 4 * n)),
    )(tot, *flat)
    return [outs[0]] + [outs[1 + 4 * i:5 + 4 * i] for i in range(n)]


def _adamw(w, g, m, v, name):
    rows, cols = w.shape
    tr = rows if rows <= 256 else 256

    def body(w_ref, g_ref, m_ref, v_ref, d_ref, nm_ref, nv_ref):
        d_ref[...], nm_ref[...], nv_ref[...] = _adam_update(w_ref[...], g_ref[...], m_ref[...], v_ref[...])

    blk = lambda: pl.BlockSpec((tr, cols), lambda i: (i, 0))
    shp = jax.ShapeDtypeStruct((rows, cols), F32)
    return pl.pallas_call(
        body, name=name, grid=(rows // tr,),
        out_shape=(shp, shp, shp),
        in_specs=[blk(), blk(), blk(), blk()], out_specs=(blk(), blk(), blk()),
        compiler_params=pltpu.CompilerParams(dimension_semantics=("arbitrary",)),
    )(w, g, m, v)


def _local_step(x2d, tgt, proj, lb_logits, cw, ga, gcn, w_out, gf):
    g64 = _group_matrix(HEAD, CONV_GROUP)
    mixed, o, cv, states, sg, b, wog = _mix_fwd(proj, lb_logits, cw, ga, gcn, g64, w_out)
    dx2, dmixed, gwo, part_out = _out_loss(x2d, mixed, wog.reshape(D_MODEL, D_MODEL), gf, tgt)
    dproj, part_mix, g_w_out = _mix_bwd(proj, o, cv, states, sg, b, dmixed, lb_logits, cw, ga, gcn, g64,
                                        gwo.reshape(N_SHARD, WO_ROWS, D_MODEL))
    return dproj, dx2, g_w_out, part_out, part_mix


def kernel(x, norm_gain, w_in, lb_logits, conv_w, hgrn_norm_gain, conv_norm_gain, w_out, final_norm_gain, loss_target, m_norm_gain, m_w_in, m_lb_logits, m_conv_w, m_hgrn_norm_gain, m_conv_norm_gain, m_w_out, m_final_norm_gain, v_norm_gain, v_w_in, v_lb_logits, v_conv_w, v_hgrn_norm_gain, v_conv_norm_gain, v_w_out, v_final_norm_gain):
    k = 2 * lax.axis_index("x") + lax.axis_index("y")
    kidx = jnp.reshape(k, (1,)).astype(jnp.int32)
    row = lambda a: a.reshape(1, D_MODEL)
    taps = lambda a: a.reshape(1, 3 * HEAD)
    h, proj, wg, cw = _gather_proj(kidx, x[0], norm_gain, w_in, taps(conv_w))
    dproj, dx2, g_w_out, part_out, part_mix = _local_step(
        x[0], loss_target[0], proj, lb_logits, cw, hgrn_norm_gain, conv_norm_gain, w_out, row(final_norm_gain))
    grad_x, g_w_in, tot = _bwd_tail(kidx, h, dproj, wg, x[0], dx2, norm_gain, part_out, part_mix)

    d_w_in, nm_w_in, nv_w_in = _adamw(w_in[0], g_w_in, m_w_in[0], v_w_in[0], "adamw_w_in")
    d_w_out, nm_w_out, nv_w_out = _adamw(w_out[0], g_w_out, m_w_out[0], v_w_out[0], "adamw_w_out")
    (loss, (g_norm_gain, d_ng, nm_ng, nv_ng), (g_final, d_fg, nm_fg, nv_fg), (g_hgrn, d_hg, nm_hg, nv_hg),
     (g_convn, d_cg, nm_cg, nv_cg), (g_lb, d_lb, nm_lb, nv_lb), (g_conv_w, d_cw, nm_cw, nv_cw)) = _adamw_small(
        tot, [(norm_gain, m_norm_gain, v_norm_gain),
              (row(final_norm_gain), row(m_final_norm_gain), row(v_final_norm_gain)),
              (hgrn_norm_gain, m_hgrn_norm_gain, v_hgrn_norm_gain),
              (conv_norm_gain, m_conv_norm_gain, v_conv_norm_gain),
              (lb_logits, m_lb_logits, v_lb_logits),
              (taps(conv_w), taps(m_conv_w), taps(v_conv_w))])
    flat = lambda a: a.reshape(D_MODEL)
    untap = lambda a: a.reshape(1, 3, HEAD)
    return (loss.reshape(()), grad_x[None],
            g_norm_gain, g_w_in[None], g_lb, untap(g_conv_w), g_hgrn, g_convn, g_w_out[None], flat(g_final),
            d_ng, d_w_in[None], d_lb, untap(d_cw), d_hg, d_cg, d_w_out[None], flat(d_fg),
            nm_ng, nm_w_in[None], nm_lb, untap(nm_cw), nm_hg, nm_cg, nm_w_out[None], flat(nm_fg),
            nv_ng, nv_w_in[None], nv_lb, untap(nv_cw), nv_hg, nv_cg, nv_w_out[None], flat(nv_fg))
```

```python
import jax
import jax.numpy as jnp
import numpy as np
from jax import lax
from jax.experimental import pallas as pl
from jax.experimental.pallas import tpu as pltpu

F32 = jnp.float32
BF16 = jnp.bfloat16
MESH = pl.DeviceIdType.MESH

SEQ = 2048
D_MODEL = 1024
D_HGRN = 512
D_CONV = 512
HEAD = 128
N_HEADS = 4
CHUNK = 64
CONV_GROUP = 64
N_SHARD = 4
SHARD_COLS = 1024
WO_ROWS = 256
EPS = 1e-6
TB = 256
NCB = TB // CHUNK
N_CHUNKS = SEQ // CHUNK
N_DEV = 8

ADAM_LR = 0.001
ADAM_B1 = 0.9
ADAM_B2 = 0.999
ADAM_EPS = 1e-08
ADAM_WD = 0.01
ADAM_STEP = 10

VMEM_LIMIT = 56 * 1024 * 1024


def _dot(a, b):
    return jnp.dot(a, b, preferred_element_type=F32)


def _dot_nt(a, b):
    return lax.dot_general(a, b, (((1,), (1,)), ((), ())), preferred_element_type=F32)


def _dot_tn(a, b):
    return lax.dot_general(a, b, (((0,), (0,)), ((), ())), preferred_element_type=F32)


def _split_bf16(x, n):
    parts = []
    r = x
    for _ in range(n):
        p = r.astype(BF16)
        parts.append(p)
        r = r - p.astype(F32)
    return parts


def _exact_left(m, x, n=3):
    acc = None
    for p in _split_bf16(x, n):
        t = _dot(m, p)
        acc = t if acc is None else acc + t
    return acc


def _exact_left_many(m, xs, n=3):
    parts = [_split_bf16(x, n) for x in xs]
    accs = [None] * len(xs)
    for i in range(n):
        for j in range(len(xs)):
            t = _dot(m, parts[j][i])
            accs[j] = t if accs[j] is None else accs[j] + t
    return accs


def _group_mean_many(xs, gmat, n=2):
    parts = [_split_bf16(x, n) for x in xs]
    accs = [None] * len(xs)
    for i in range(n):
        for j in range(len(xs)):
            t = _dot(parts[j][i], gmat)
            accs[j] = t if accs[j] is None else accs[j] + t
    return accs


def _group_mean(x, gmat, n=2):
    w = gmat.shape[0]
    outs = []
    for c0 in range(0, x.shape[1], w):
        acc = None
        for p in _split_bf16(x[:, c0:c0 + w], n):
            t = _dot(p, gmat)
            acc = t if acc is None else acc + t
        outs.append(acc)
    return jnp.concatenate(outs, axis=1)


def _sigmoid(x):
    return 1.0 / (1.0 + jnp.exp(-x))


def _lower_bound(lbl):
    l0 = lbl[0:1, :]
    l1 = lbl[1:2, :]
    m = jnp.maximum(l0, l1)
    e0 = jnp.exp(l0 - m)
    e1 = jnp.exp(l1 - m)
    return e0 / (e0 + e1)


def _tri(lower):
    r = lax.broadcasted_iota(jnp.int32, (CHUNK, CHUNK), 0)
    c = lax.broadcasted_iota(jnp.int32, (CHUNK, CHUNK), 1)
    return jnp.where((c <= r) if lower else (c >= r), 1.0, 0.0).astype(BF16)


def _causal():
    r = lax.broadcasted_iota(jnp.int32, (CHUNK, CHUNK), 0)
    c = lax.broadcasted_iota(jnp.int32, (CHUNK, CHUNK), 1)
    return c <= r


def _shift_down(x, sh, prev_tail):
    r = pltpu.roll(x, sh, 0)
    pt = pltpu.roll(prev_tail, sh, 0)
    rows = lax.broadcasted_iota(jnp.int32, prev_tail.shape, 0)
    top = jnp.where(rows < sh, pt, r[0:8])
    return jnp.concatenate([top, r[8:]], axis=0)


def _shift_up(x, sh, next_head):
    n = x.shape[0]
    r = pltpu.roll(x, n - sh, 0)
    nh = pltpu.roll(next_head, 8 - sh, 0)
    rows = lax.broadcasted_iota(jnp.int32, next_head.shape, 0)
    bot = jnp.where(rows >= 8 - sh, nh, r[n - 8:])
    return jnp.concatenate([r[:n - 8], bot], axis=0)


def _group_matrix(width, group):
    r = np.arange(width)[:, None] // group
    c = np.arange(width)[None, :] // group
    return jnp.asarray(np.where(r == c, 1.0 / group, 0.0), dtype=BF16)


TP = 512
SEM_W, SEM_CW, SEM_W_FWD, N_SEM = 0, 3, 6, 9


def _gather_proj(kidx, x2d, g1, w_in, conv_w):
    half_w = D_MODEL // 2
    nt = SEQ // TP

    def body(k_ref, x_ref, g_ref, w_ref, cw_ref, h_ref, p_ref, wg_out, cwg_out,
             wg_v, cwg_v, send_sems, recv_sems, out_sems):
        s, t = pl.program_id(0), pl.program_id(1)
        x, y, c = lax.axis_index("x"), lax.axis_index("y"), lax.axis_index("c")
        k = 2 * x + y
        sibling = (x, y, 1 - c)
        chips = [(1 - x, y), (x, 1 - y), (1 - x, 1 - y)]
        kjs = [2 * cx + cy for cx, cy in chips]

        def w_half(kk, cc):
            return wg_v.at[kk, pl.ds(cc * half_w, half_w), :]

        def cw_of(kk):
            return cwg_v.at[:, pl.ds(pl.multiple_of(kk * HEAD, HEAD), HEAD)]

        def copy(sem, ref, to):
            return pltpu.make_async_remote_copy(
                src_ref=ref, dst_ref=ref, send_sem=send_sems.at[sem], recv_sem=recv_sems.at[sem],
                device_id=to, device_id_type=MESH)

        def at_step(sv, tv):
            return pl.when((s == sv) & (t == tv))

        w_direct = [copy(SEM_W + j, w_half(k, c), (*chip, c)) for j, chip in enumerate(chips)]
        cw_direct = [copy(SEM_CW + j, cw_of(k), (*chip, c)) for j, chip in enumerate(chips)]
        w_passed = [copy(SEM_W_FWD + j, w_half(kj, c), sibling) for j, kj in enumerate(kjs)]
        stores = ([pltpu.make_async_copy(wg_v.at[kk], wg_out.at[kk], out_sems.at[i])
                   for i, kk in enumerate([k] + kjs)]
                  + [pltpu.make_async_copy(cwg_v, cwg_out, out_sems.at[4])])

        @at_step(0, 0)
        def _():
            wg_v[k] = w_ref[0].astype(BF16)
            mine = pl.ds(pl.multiple_of(k * HEAD, HEAD), HEAD)
            cwg_v[:, mine] = jnp.zeros((8, HEAD), F32)
            for tap in range(3):
                cwg_v[tap:tap + 1, mine] = cw_ref[:, tap * HEAD:(tap + 1) * HEAD]
            w_direct[0].start()
            w_direct[1].start()
            for cp in cw_direct:
                cp.start()
            stores[0].start()

        @at_step(1, 0)
        def _():
            for j in range(2):
                copy(SEM_W + j, w_half(kjs[j], c), sibling).wait_recv()
                w_passed[j].start()
            w_direct[2].start()
            copy(SEM_W_FWD, w_half(kjs[0], 1 - c), sibling).wait_recv()
            stores[1].start()

        @at_step(2, 0)
        def _():
            copy(SEM_W_FWD + 1, w_half(kjs[1], 1 - c), sibling).wait_recv()
            stores[2].start()

        @at_step(3, 0)
        def _():
            copy(SEM_W + 2, w_half(kjs[2], c), sibling).wait_recv()
            w_passed[2].start()
            copy(SEM_W_FWD + 2, w_half(kjs[2], 1 - c), sibling).wait_recv()
            stores[3].start()

        rows = pl.ds(pl.multiple_of(t * TP, TP), TP)

        @pl.when(s == 0)
        def _():
            xv = x_ref[...]
            r = lax.rsqrt(jnp.mean(xv * xv, axis=-1, keepdims=True) + EPS)
            h_ref[rows, :] = (xv * r * g_ref[...]).astype(BF16)

        js = k ^ (((s & 1) << 1) | (s >> 1))
        p_ref[...] = _dot(h_ref[rows, :], wg_v[js])

        @at_step(N_SHARD - 1, nt - 1)
        def _():
            for j in range(3):
                copy(SEM_CW + j, cw_of(kjs[j]), sibling).wait_recv()
            stores[4].start()
            for cp in w_direct + cw_direct + w_passed:
                cp.wait_send()
            for st in stores:
                st.wait()

    def x_map(s, t, kr):
        return (jnp.where(s == 0, t, nt - 1), 0)

    def p_map(s, t, kr):
        return (t, kr[0] ^ (((s & 1) << 1) | (s >> 1)))

    hbm = pl.BlockSpec(memory_space=pl.ANY)
    grid_spec = pltpu.PrefetchScalarGridSpec(
        num_scalar_prefetch=1, grid=(N_SHARD, nt),
        in_specs=[pl.BlockSpec((TP, D_MODEL), x_map),
                  pl.BlockSpec((1, D_MODEL), lambda s, t, kr: (0, 0)),
                  pl.BlockSpec((1, D_MODEL, SHARD_COLS), lambda s, t, kr: (0, 0, 0)),
                  pl.BlockSpec((1, 3 * HEAD), lambda s, t, kr: (0, 0))],
        out_specs=(pl.BlockSpec((SEQ, D_MODEL), lambda s, t, kr: (0, 0)),
                   pl.BlockSpec((TP, SHARD_COLS), p_map), hbm, hbm),
        scratch_shapes=[pltpu.VMEM((N_SHARD, D_MODEL, SHARD_COLS), BF16),
                        pltpu.VMEM((8, D_CONV), F32),
                        pltpu.SemaphoreType.DMA((N_SEM,)), pltpu.SemaphoreType.DMA((N_SEM,)),
                        pltpu.SemaphoreType.DMA((5,))])
    return pl.pallas_call(
        body, name="gather_proj", grid_spec=grid_spec,
        out_shape=(jax.ShapeDtypeStruct((SEQ, D_MODEL), BF16),
                   jax.ShapeDtypeStruct((SEQ, N_SHARD * SHARD_COLS), F32),
                   jax.ShapeDtypeStruct((N_SHARD, D_MODEL, SHARD_COLS), BF16),
                   jax.ShapeDtypeStruct((8, D_CONV), F32)),
        compiler_params=pltpu.CompilerParams(dimension_semantics=("arbitrary", "arbitrary"),
                                             vmem_limit_bytes=VMEM_LIMIT),
    )(kidx, x2d, g1, w_in, conv_w)


def _mix_fwd(proj, lb_logits, cw, ga, gcn, g64, w_out):
    half_o = WO_ROWS // 2
    nblk = SEQ // TB

    def body(p_ref, lbl_ref, cw_ref, ga_ref, gcn_ref, g64_ref, wo_ref,
             mixed_ref, o_ref, cv_ref, sto_ref, sg_ref, b_ref, wog_out,
             st_ref, tail_ref, wog_v, send_sems, recv_sems, out_sem):
        i = pl.program_id(0)
        x, y, c = lax.axis_index("x"), lax.axis_index("y"), lax.axis_index("c")
        k = 2 * x + y
        sibling = (x, y, 1 - c)
        chips = [(1 - x, y), (x, 1 - y), (1 - x, 1 - y)]
        kjs = [2 * cx + cy for cx, cy in chips]

        def wo_half(kk, cc):
            return wog_v.at[kk, pl.ds(cc * half_o, half_o), :]

        def copy(sem, ref, to):
            return pltpu.make_async_remote_copy(
                src_ref=ref, dst_ref=ref, send_sem=send_sems.at[sem], recv_sem=recv_sems.at[sem],
                device_id=to, device_id_type=MESH)

        wo_direct = [copy(j, wo_half(k, c), (*chip, c)) for j, chip in enumerate(chips)]
        wo_passed = [copy(3 + j, wo_half(kj, c), sibling) for j, kj in enumerate(kjs)]
        wo_store = pltpu.make_async_copy(wog_v, wog_out, out_sem.at[0])

        @pl.when(i == 0)
        def _():
            st_ref[...] = jnp.zeros_like(st_ref)
            tail_ref[...] = jnp.zeros_like(tail_ref)
            wog_v[k] = wo_ref[0].astype(BF16)
            for cp in wo_direct:
                cp.start()

        @pl.when(i == nblk - 2)
        def _():
            for j in range(3):
                copy(j, wo_half(kjs[j], c), sibling).wait_recv()
                wo_passed[j].start()

        lb = _lower_bound(lbl_ref[...])
        tri = _tri(True)
        causal = _causal()
        g64m = g64_ref[...]
        heads = range(N_HEADS)
        cs = [slice(hd * HEAD, (hd + 1) * HEAD) for hd in heads]
        col = lambda base, hd: slice(base + hd * HEAD, base + (hd + 1) * HEAD)
        for n in range(NCB):
            sl = pl.ds(n * CHUNK, CHUNK)
            sg = [_sigmoid(p_ref[sl, col(512, hd)]) for hd in heads]
            f = [lb[:, cs[hd]] + (1.0 - lb[:, cs[hd]]) * sg[hd] for hd in heads]
            bc = _exact_left_many(tri, [jnp.log(f[hd]) for hd in heads])
            for hd in heads:
                sg_ref[sl, cs[hd]] = sg[hd]
                b_ref[sl, cs[hd]] = bc[hd]
            g = [bc[hd][CHUNK - 1:CHUNK, :] for hd in heads]
            qd = [(p_ref[sl, col(0, hd)] * jnp.exp(bc[hd])).astype(BF16) for hd in heads]
            ki = [((1.0 - f[hd]) * jnp.exp(-bc[hd])).astype(BF16) for hd in heads]
            ke = [((1.0 - f[hd]) * jnp.exp(g[hd] - bc[hd])).astype(BF16) for hd in heads]
            vb = [p_ref[sl, col(1024, hd)].astype(BF16) for hd in heads]
            st = [st_ref[hd] for hd in heads]
            for hd in heads:
                sto_ref[n, hd] = st[hd]
            scm = [_dot_nt(qd[hd], ki[hd]) for hd in heads]
            inter = [_dot_nt(qd[hd], st[hd].astype(BF16)) for hd in heads]
            upd = [_dot_tn(vb[hd], ke[hd]) for hd in heads]
            intra = [_dot(jnp.where(causal, scm[hd], 0.0).astype(BF16), vb[hd]) for hd in heads]
            for hd in heads:
                st_ref[hd] = st[hd] * jnp.exp(g[hd]) + upd[hd]
                o = intra[hd] + inter[hd]
                o_ref[sl, cs[hd]] = o
                ra = lax.rsqrt(jnp.mean(o * o, axis=-1, keepdims=True) + EPS)
                za = p_ref[sl, col(1536, hd)]
                mixed_ref[sl, cs[hd]] = (o * ra * ga_ref[:, cs[hd]] * (za * _sigmoid(za))).astype(BF16)
            yb = []
            for hd in heads:
                cu = p_ref[sl, col(3072, hd)] * p_ref[sl, col(2048, hd)]
                tail = tail_ref[:, cs[hd]]
                cv = (cw_ref[0:1, cs[hd]] * _shift_down(cu, 2, tail) + cw_ref[1:2, cs[hd]] * _shift_down(cu, 1, tail)
                      + cw_ref[2:3, cs[hd]] * cu)
                tail_ref[:, cs[hd]] = cu[CHUNK - 8:, :]
                cv_ref[sl, cs[hd]] = cv
                yb.append(p_ref[sl, col(2560, hd)] * cv)
            ms = _group_mean_many([y * y for y in yb], g64m)
            for hd in heads:
                rb = lax.rsqrt(ms[hd] + EPS)
                zb = p_ref[sl, col(3584, hd)]
                mixed_ref[sl, col(512, hd)] = (yb[hd] * rb * gcn_ref[:, cs[hd]] * (zb * _sigmoid(zb))).astype(BF16)

        @pl.when(i == nblk - 1)
        def _():
            for j in range(3):
                copy(3 + j, wo_half(kjs[j], 1 - c), sibling).wait_recv()
            wo_store.start()
            for cp in wo_direct + wo_passed:
                cp.wait_send()
            wo_store.wait()

    row = lambda w: pl.BlockSpec((1, w), lambda i: (0, 0))
    return pl.pallas_call(
        body, name="mix_fwd", grid=(nblk,),
        out_shape=(jax.ShapeDtypeStruct((SEQ, D_MODEL), BF16),
                   jax.ShapeDtypeStruct((SEQ, D_HGRN), F32),
                   jax.ShapeDtypeStruct((SEQ, D_CONV), F32),
                   jax.ShapeDtypeStruct((N_CHUNKS, N_HEADS, HEAD, HEAD), F32),
                   jax.ShapeDtypeStruct((SEQ, D_HGRN), F32),
                   jax.ShapeDtypeStruct((SEQ, D_HGRN), F32),
                   jax.ShapeDtypeStruct((N_SHARD, WO_ROWS, D_MODEL), BF16)),
        in_specs=[pl.BlockSpec((TB, 4096), lambda i: (i, 0)),
                  pl.BlockSpec((2, D_HGRN), lambda i: (0, 0)),
                  pl.BlockSpec((8, D_CONV), lambda i: (0, 0)),
                  row(D_HGRN), row(D_CONV),
                  pl.BlockSpec((HEAD, HEAD), lambda i: (0, 0)),
                  pl.BlockSpec((1, WO_ROWS, D_MODEL), lambda i: (0, 0, 0))],
        out_specs=(pl.BlockSpec((TB, D_MODEL), lambda i: (i, 0)),
                   pl.BlockSpec((TB, D_HGRN), lambda i: (i, 0)),
                   pl.BlockSpec((TB, D_CONV), lambda i: (i, 0)),
                   pl.BlockSpec((NCB, N_HEADS, HEAD, HEAD), lambda i: (i, 0, 0, 0)),
                   pl.BlockSpec((TB, D_HGRN), lambda i: (i, 0)),
                   pl.BlockSpec((TB, D_HGRN), lambda i: (i, 0)),
                   pl.BlockSpec(memory_space=pl.ANY)),
        scratch_shapes=[pltpu.VMEM((N_HEADS, HEAD, HEAD), F32), pltpu.VMEM((8, D_CONV), F32),
                        pltpu.VMEM((N_SHARD, WO_ROWS, D_MODEL), BF16),
                        pltpu.SemaphoreType.DMA((6,)), pltpu.SemaphoreType.DMA((6,)),
                        pltpu.SemaphoreType.DMA((1,))],
        compiler_params=pltpu.CompilerParams(dimension_semantics=("arbitrary",), vmem_limit_bytes=VMEM_LIMIT),
    )(proj, lb_logits, cw, ga, gcn, g64, w_out)


def _out_loss(x2d, mixed, wog, gf, tgt):
    def body(x_ref, m_ref, wo_ref, gf_ref, t_ref, dx2_ref, dm_ref, gwo_ref, part_ref, acc_ref):
        i = pl.program_id(0)

        @pl.when(i == 0)
        def _():
            acc_ref[...] = jnp.zeros_like(acc_ref)
            part_ref[...] = jnp.zeros_like(part_ref)

        mixed_b = m_ref[...]
        x2 = x_ref[...] + _dot(mixed_b, wo_ref[...])
        r2 = lax.rsqrt(jnp.mean(x2 * x2, axis=-1, keepdims=True) + EPS)
        n2 = x2 * r2
        gfv = gf_ref[...]
        err = n2 * gfv - t_ref[...]
        loss = 0.5 * jnp.sum(jnp.mean(err * err, axis=-1, keepdims=True), axis=0, keepdims=True)
        dy = err * (1.0 / D_MODEL)
        part_ref[1:2, :] += jnp.sum(dy * n2, axis=0, keepdims=True)
        part_ref[7:8, :] += jnp.broadcast_to(loss, (1, D_MODEL))
        dn = dy * gfv
        dx2 = r2 * (dn - n2 * jnp.mean(dn * n2, axis=-1, keepdims=True))
        dx2_ref[...] = dx2
        dx2_b = dx2.astype(BF16)
        dm_ref[...] = _dot_nt(dx2_b, wo_ref[...])
        acc_ref[...] += _dot_tn(mixed_b, dx2_b)

        @pl.when(i == pl.num_programs(0) - 1)
        def _():
            gwo_ref[...] = acc_ref[...].astype(BF16)

    blk = lambda: pl.BlockSpec((TP, D_MODEL), lambda i: (i, 0))
    return pl.pallas_call(
        body, name="out_loss", grid=(SEQ // TP,),
        out_shape=(jax.ShapeDtypeStruct((SEQ, D_MODEL), F32),
                   jax.ShapeDtypeStruct((SEQ, D_MODEL), F32),
                   jax.ShapeDtypeStruct((D_MODEL, D_MODEL), BF16),
                   jax.ShapeDtypeStruct((8, D_MODEL), F32)),
        in_specs=[blk(), blk(), pl.BlockSpec((D_MODEL, D_MODEL), lambda i: (0, 0)),
                  pl.BlockSpec((1, D_MODEL), lambda i: (0, 0)), blk()],
        out_specs=(blk(), blk(), pl.BlockSpec((D_MODEL, D_MODEL), lambda i: (0, 0)),
                   pl.BlockSpec((8, D_MODEL), lambda i: (0, 0))),
        scratch_shapes=[pltpu.VMEM((D_MODEL, D_MODEL), F32)],
        compiler_params=pltpu.CompilerParams(dimension_semantics=("arbitrary",), vmem_limit_bytes=VMEM_LIMIT),
    )(x2d, mixed, wog, gf, tgt)


def _mix_bwd(proj, o, cv, states, sg, b, dmixed, lb_logits, cw, ga, gcn, g64):
    nblk = SEQ // TB

    def body(p_ref, o_ref, cv_ref, st_ref, sg_ref, b_ref, dm_ref, lbl_ref, cw_ref, ga_ref, gcn_ref, g64_ref,
             dp_ref, part_ref, dst_ref, head_ref, dlb_ref):
        i = pl.program_id(0)

        @pl.when(i == 0)
        def _():
            dst_ref[...] = jnp.zeros_like(dst_ref)
            head_ref[...] = jnp.zeros_like(head_ref)
            part_ref[...] = jnp.zeros_like(part_ref)
            dlb_ref[...] = jnp.zeros_like(dlb_ref)

        lb = _lower_bound(lbl_ref[...])
        triu = _tri(False)
        causal = _causal()
        g64m = g64_ref[...]
        rowsum = lambda a: jnp.sum(a, axis=0, keepdims=True)
        heads = range(N_HEADS)
        cs = [slice(hd * HEAD, (hd + 1) * HEAD) for hd in heads]
        col = lambda base, hd: slice(base + hd * HEAD, base + (hd + 1) * HEAD)
        for n in reversed(range(NCB)):
            sl = pl.ds(n * CHUNK, CHUNK)
            cvv = [cv_ref[sl, cs[hd]] for hd in heads]
            gb = [p_ref[sl, col(2560, hd)] for hd in heads]
            yb = [gb[hd] * cvv[hd] for hd in heads]
            ms = _group_mean_many([y * y for y in yb], g64m)
            rb, nb, dnb = [], [], []
            for hd in heads:
                rb.append(lax.rsqrt(ms[hd] + EPS))
                nb.append(yb[hd] * rb[hd])
                zb = p_ref[sl, col(3584, hd)]
                sgb = _sigmoid(zb)
                dmb = dm_ref[sl, col(512, hd)]
                gcv = gcn_ref[:, cs[hd]]
                part_ref[2:3, col(512, hd)] += rowsum(dmb * nb[hd] * (zb * sgb))
                dp_ref[sl, col(3584, hd)] = (dmb * nb[hd] * gcv * (sgb * (1.0 + zb * (1.0 - sgb)))).astype(BF16)
                dnb.append(dmb * gcv * (zb * sgb))
            mdn = _group_mean_many([dnb[hd] * nb[hd] for hd in heads], g64m)
            for hd in heads:
                dyb = rb[hd] * (dnb[hd] - nb[hd] * mdn[hd])
                dp_ref[sl, col(2560, hd)] = (dyb * cvv[hd]).astype(BF16)
                dcv = dyb * gb[hd]
                head = head_ref[:, cs[hd]]
                dcv1 = _shift_up(dcv, 1, head)
                dcv2 = _shift_up(dcv, 2, head)
                head_ref[:, cs[hd]] = dcv[0:8, :]
                u = p_ref[sl, col(2048, hd)]
                gc = p_ref[sl, col(3072, hd)]
                cu = gc * u
                part_ref[4:5, cs[hd]] += rowsum(dcv2 * cu)
                part_ref[5:6, cs[hd]] += rowsum(dcv1 * cu)
                part_ref[6:7, cs[hd]] += rowsum(dcv * cu)
                dcu = cw_ref[2:3, cs[hd]] * dcv + cw_ref[1:2, cs[hd]] * dcv1 + cw_ref[0:1, cs[hd]] * dcv2
                dp_ref[sl, col(3072, hd)] = (dcu * u).astype(BF16)
                dp_ref[sl, col(2048, hd)] = (dcu * gc).astype(BF16)
            do_b = []
            for hd in heads:
                ov = o_ref[sl, cs[hd]]
                ra = lax.rsqrt(jnp.mean(ov * ov, axis=-1, keepdims=True) + EPS)
                na = ov * ra
                za = p_ref[sl, col(1536, hd)]
                sga = _sigmoid(za)
                dma = dm_ref[sl, cs[hd]]
                gav = ga_ref[:, cs[hd]]
                part_ref[2:3, cs[hd]] += rowsum(dma * na * (za * sga))
                dp_ref[sl, col(1536, hd)] = (dma * na * gav * (sga * (1.0 + za * (1.0 - sga)))).astype(BF16)
                dna = dma * gav * (za * sga)
                do_b.append((ra * (dna - na * jnp.mean(dna * na, axis=-1, keepdims=True))).astype(BF16))
            s = [sg_ref[sl, cs[hd]] for hd in heads]
            f = [lb[:, cs[hd]] + (1.0 - lb[:, cs[hd]]) * s[hd] for hd in heads]
            bc = [b_ref[sl, cs[hd]] for hd in heads]
            g = [bc[hd][CHUNK - 1:CHUNK, :] for hd in heads]
            eb = [jnp.exp(bc[hd]) for hd in heads]
            enb = [jnp.exp(-bc[hd]) for hd in heads]
            eg = [jnp.exp(g[hd] - bc[hd]) for hd in heads]
            dec = [jnp.exp(g[hd]) for hd in heads]
            qd = [p_ref[sl, cs[hd]] * eb[hd] for hd in heads]
            ki = [(1.0 - f[hd]) * enb[hd] for hd in heads]
            ke = [(1.0 - f[hd]) * eg[hd] for hd in heads]
            qd_b = [a.astype(BF16) for a in qd]
            ki_b = [a.astype(BF16) for a in ki]
            ke_b = [a.astype(BF16) for a in ke]
            vb = [p_ref[sl, col(1024, hd)].astype(BF16) for hd in heads]
            st = [st_ref[n, hd] for hd in heads]
            dst = [dst_ref[hd] for hd in heads]
            st_b = [a.astype(BF16) for a in st]
            dst_b = [a.astype(BF16) for a in dst]
            scm = [_dot_nt(qd_b[hd], ki_b[hd]) for hd in heads]
            amm = [_dot_nt(do_b[hd], vb[hd]) for hd in heads]
            dqd2 = [_dot(do_b[hd], st_b[hd]) for hd in heads]
            dke = [_dot(vb[hd], dst_b[hd]) for hd in heads]
            dv2 = [_dot_nt(ke_b[hd], dst_b[hd]) for hd in heads]
            dsu = [_dot_tn(do_b[hd], qd_b[hd]) for hd in heads]
            sc = [jnp.where(causal, scm[hd], 0.0).astype(BF16) for hd in heads]
            am = [jnp.where(causal, amm[hd], 0.0).astype(BF16) for hd in heads]
            dqd1 = [_dot(am[hd], ki_b[hd]) for hd in heads]
            dki = [_dot_tn(am[hd], qd_b[hd]) for hd in heads]
            dv1 = [_dot_tn(sc[hd], do_b[hd]) for hd in heads]
            db, dgv = [], []
            for hd in heads:
                dqd = dqd1[hd] + dqd2[hd]
                ddec = rowsum(dst[hd] * st[hd])
                dst_ref[hd] = dst[hd] * dec[hd] + dsu[hd]
                dp_ref[sl, cs[hd]] = (dqd * eb[hd]).astype(BF16)
                dp_ref[sl, col(1024, hd)] = (dv1[hd] + dv2[hd]).astype(BF16)
                db.append(dqd * qd[hd] - dki[hd] * ki[hd] - dke[hd] * ke[hd])
                dgv.append(rowsum(dke[hd] * ke[hd]) + ddec * dec[hd])
            rc = _exact_left_many(triu, db, 2)
            for hd in heads:
                df = (rc[hd] + dgv[hd]) / f[hd] - (dki[hd] * enb[hd] + dke[hd] * eg[hd])
                dlb_ref[:, cs[hd]] += rowsum(df * (1.0 - s[hd]))
                dp_ref[sl, col(512, hd)] = (df * (1.0 - lb[:, cs[hd]]) * s[hd] * (1.0 - s[hd])).astype(BF16)

        @pl.when(i == nblk - 1)
        def _():
            row = dlb_ref[...] * lb * (1.0 - lb)
            part_ref[3:4, 0:D_HGRN] = row
            part_ref[3:4, D_HGRN:] = -row

    rev = lambda w: pl.BlockSpec((TB, w), lambda i: (nblk - 1 - i, 0))
    row = lambda w: pl.BlockSpec((1, w), lambda i: (0, 0))
    return pl.pallas_call(
        body, name="mix_bwd", grid=(nblk,),
        out_shape=(jax.ShapeDtypeStruct((SEQ, 4096), BF16),
                   jax.ShapeDtypeStruct((8, D_MODEL), F32)),
        in_specs=[rev(4096), rev(D_HGRN), rev(D_CONV),
                  pl.BlockSpec((NCB, N_HEADS, HEAD, HEAD), lambda i: (nblk - 1 - i, 0, 0, 0)),
                  rev(D_HGRN), rev(D_HGRN), rev(D_MODEL),
                  pl.BlockSpec((2, D_HGRN), lambda i: (0, 0)),
                  pl.BlockSpec((8, D_CONV), lambda i: (0, 0)),
                  row(D_HGRN), row(D_CONV),
                  pl.BlockSpec((HEAD, HEAD), lambda i: (0, 0))],
        out_specs=(rev(4096), pl.BlockSpec((8, D_MODEL), lambda i: (0, 0))),
        scratch_shapes=[pltpu.VMEM((N_HEADS, HEAD, HEAD), F32), pltpu.VMEM((8, D_CONV), F32),
                        pltpu.VMEM((1, D_HGRN), F32)],
        compiler_params=pltpu.CompilerParams(dimension_semantics=("arbitrary",), vmem_limit_bytes=VMEM_LIMIT),
    )(proj, o, cv, states, sg, b, dmixed, lb_logits, cw, ga, gcn, g64)


TT = 1024
TX = 256
(SEM_D2D, SEM_D2D_O, SEM_ICI, SEM_ICI_O, SEM_FIN, SEM_FIN_O, SEM_SMALL, N_SEM_TAIL) = 0, 4, 5, 8, 11, 12, 12, 20


def _bwd_tail(kidx, h, dproj, wg, gwo, x2d, dx2, g1, small_a, small_b):
    hw = D_MODEL // 2
    ho = WO_ROWS // 2
    nt = SEQ // TT
    n_steps = N_SHARD + SEQ // TX // nt

    def body(k_ref, h_ref, dp_ref, w_ref, gwo_ref, x_ref, dx2_ref, g_ref, sm_ref, smb_ref,
             gx_ref, gw_out, gwo_out, osm_ref,
             acc, dh, sendbuf, keep, sibrcv, rcv, sib_o, p_o, rcv_o, res_o, sm_buf, dng,
             send_sems, recv_sems, out_sems):
        s, t = pl.program_id(0), pl.program_id(1)
        x, y, c = lax.axis_index("x"), lax.axis_index("y"), lax.axis_index("c")
        k = 2 * x + y
        me = 4 * x + 2 * y + c
        sibling = (x, y, 1 - c)
        chips = [(1 - x, 1 - y), (1 - x, y), (x, 1 - y)]
        kjs = [2 * cx + cy for cx, cy in chips]
        mine = pl.ds(pl.multiple_of(c * hw, hw), hw)
        other = pl.ds(pl.multiple_of((1 - c) * hw, hw), hw)
        mine_o = pl.ds(pl.multiple_of(c * ho, ho), ho)
        other_o = pl.ds(pl.multiple_of((1 - c) * ho, ho), ho)

        def copy(sem, src, dst, to):
            return pltpu.make_async_remote_copy(
                src_ref=src, dst_ref=dst, send_sem=send_sems.at[sem], recv_sem=recv_sems.at[sem],
                device_id=to, device_id_type=MESH)

        def at_step(sv, tv):
            return pl.when((s == sv) & (t == tv))

        def at_norm_block(b):
            return at_step(N_SHARD + b // nt, b % nt)

        d2d = [copy(SEM_D2D + sv, sendbuf.at[sv], sibrcv.at[sv], sibling) for sv in range(N_SHARD)]
        d2d_o = copy(SEM_D2D_O, gwo_ref.at[:, other_o, :], sib_o, sibling)
        ici = [copy(SEM_ICI + sv, keep.at[sv], rcv.at[sv], (*chips[sv], c)) for sv in range(3)]
        ici_o = [copy(SEM_ICI_O + sv, p_o.at[kjs[sv]], rcv_o.at[sv], (*chips[sv], c)) for sv in range(3)]
        fin = copy(SEM_FIN, acc.at[mine, :], acc.at[mine, :], sibling)
        fin_o = copy(SEM_FIN_O, res_o.at[mine_o, :], res_o.at[mine_o, :], sibling)
        smalls = [copy(SEM_SMALL + m, sm_buf.at[me], sm_buf.at[me],
                       (x ^ (m >> 2), y ^ ((m >> 1) & 1), c ^ (m & 1))) for m in range(1, N_DEV)]
        store_w = pltpu.make_async_copy(acc, gw_out, out_sems.at[0])
        store_o = pltpu.make_async_copy(res_o, gwo_out, out_sems.at[1])

        @at_step(0, 0)
        def _():
            d2d_o.start()

        @at_step(0, 1)
        def _():
            d2d_o.wait_recv()
            for j in range(N_SHARD):
                p_o[j] = (gwo_ref[j, mine_o, :].astype(F32) + sib_o[j].astype(F32)).astype(BF16)
            res_o[mine_o, :] = gwo_ref[k, mine_o, :].astype(F32) + sib_o[k].astype(F32)
            for cp in ici_o:
                cp.start()

        rows = pl.ds(pl.multiple_of(t * TT, TT), TT)

        @pl.when(s < N_SHARD)
        def _():
            dpb = dp_ref[...]
            part = _dot_tn(h_ref[...], dpb)

            @pl.when(t == 0)
            def _():
                acc[...] = part

            @pl.when(t > 0)
            def _():
                acc[...] += part

            d = _dot_nt(dpb, w_ref[0])

            @pl.when(s == 0)
            def _():
                dh[rows, :] = d

            @pl.when(s > 0)
            def _():
                dh[rows, :] += d

        for sv in range(N_SHARD):
            @at_step(sv, nt - 1)
            def _(sv=sv):
                sendbuf[sv] = acc[other, :].astype(BF16)
                if sv < 3:
                    keep[sv] = acc[mine, :].astype(BF16)
                d2d[sv].start()

        for sv in range(3):
            @at_step(sv + 1, 0)
            def _(sv=sv):
                d2d[sv].wait_recv()
                keep[sv] = (keep[sv].astype(F32) + sibrcv[sv].astype(F32)).astype(BF16)
                ici[sv].start()

        @at_norm_block(0)
        def _():
            d2d[3].wait_recv()
            ici[0].wait_recv()
            acc[mine, :] += sibrcv[3].astype(F32) + rcv[0].astype(F32)

        @at_norm_block(1)
        def _():
            tot = res_o[mine_o, :]
            for sv in range(3):
                ici_o[sv].wait_recv()
                tot = tot + rcv_o[sv].astype(F32)
            res_o[mine_o, :] = tot
            fin_o.start()

        @at_norm_block(2)
        def _():
            ici[1].wait_recv()
            acc[mine, :] += rcv[1].astype(F32)

        @at_norm_block(0)
        def _():
            dng[...] = jnp.zeros_like(dng)

        @pl.when(s >= N_SHARD)
        def _():
            blk = (s - N_SHARD) * nt + t
            dhv = dh[pl.ds(pl.multiple_of(blk * TX, TX), TX), :]
            xv = x_ref[...]
            r = lax.rsqrt(jnp.mean(xv * xv, axis=-1, keepdims=True) + EPS)
            xn = xv * r
            dng[...] += jnp.sum(dhv * xn, axis=0, keepdims=True)
            dxn = dhv * g_ref[...]
            gx_ref[...] = dx2_ref[...] + r * (dxn - xn * jnp.mean(dxn * xn, axis=-1, keepdims=True))

        @at_step(n_steps - 1, nt - 1)
        def _():
            sm_buf[me] = sm_ref[...] + smb_ref[...]
            sm_buf[me, 0:1, :] = dng[...]
            for cp in smalls:
                cp.start()
            ici[2].wait_recv()
            acc[mine, :] += rcv[2].astype(F32)
            fin.start()
            for m in range(1, N_DEV):
                copy(SEM_SMALL + m, sm_buf.at[0], sm_buf.at[0], sibling).wait_recv()
            tot = sm_buf[0]
            for d in range(1, N_DEV):
                tot = tot + sm_buf[d]
            osm_ref[...] = tot
            fin_o.wait_recv()
            store_o.start()
            fin.wait_recv()
            store_w.start()
            for cp in d2d + [d2d_o] + ici + ici_o + [fin, fin_o] + smalls:
                cp.wait_send()
            store_o.wait()
            store_w.wait()

    def shard_of(s, kr):
        return kr[0] ^ (3 - jnp.minimum(s, 3))

    def tok(s, t):
        return jnp.where(s < N_SHARD, t, nt - 1)

    def blk_map(s, t, kr):
        return (jnp.where(s < N_SHARD, 0, (s - N_SHARD) * nt + t), 0)

    hbm = pl.BlockSpec(memory_space=pl.ANY)
    grid_spec = pltpu.PrefetchScalarGridSpec(
        num_scalar_prefetch=1, grid=(n_steps, nt),
        in_specs=[pl.BlockSpec((TT, D_MODEL), lambda s, t, kr: (tok(s, t), 0)),
                  pl.BlockSpec((TT, SHARD_COLS), lambda s, t, kr: (tok(s, t), shard_of(s, kr))),
                  pl.BlockSpec((1, D_MODEL, SHARD_COLS), lambda s, t, kr: (shard_of(s, kr), 0, 0)),
                  pl.BlockSpec((N_SHARD, WO_ROWS, D_MODEL), lambda s, t, kr: (0, 0, 0)),
                  pl.BlockSpec((TX, D_MODEL), blk_map),
                  pl.BlockSpec((TX, D_MODEL), blk_map),
                  pl.BlockSpec((1, D_MODEL), lambda s, t, kr: (0, 0)),
                  pl.BlockSpec((8, D_MODEL), lambda s, t, kr: (0, 0)),
                  pl.BlockSpec((8, D_MODEL), lambda s, t, kr: (0, 0))],
        out_specs=(pl.BlockSpec((TX, D_MODEL), blk_map), hbm, hbm,
                   pl.BlockSpec((8, D_MODEL), lambda s, t, kr: (0, 0))),
        scratch_shapes=[pltpu.VMEM((D_MODEL, SHARD_COLS), F32), pltpu.VMEM((SEQ, D_MODEL), F32),
                        pltpu.VMEM((N_SHARD, hw, SHARD_COLS), BF16), pltpu.VMEM((3, hw, SHARD_COLS), BF16),
                        pltpu.VMEM((N_SHARD, hw, SHARD_COLS), BF16), pltpu.VMEM((3, hw, SHARD_COLS), BF16),
                        pltpu.VMEM((N_SHARD, ho, D_MODEL), BF16), pltpu.VMEM((N_SHARD, ho, D_MODEL), BF16),
                        pltpu.VMEM((3, ho, D_MODEL), BF16), pltpu.VMEM((WO_ROWS, D_MODEL), F32),
                        pltpu.VMEM((N_DEV, 8, D_MODEL), F32), pltpu.VMEM((1, D_MODEL), F32),
                        pltpu.SemaphoreType.DMA((N_SEM_TAIL,)), pltpu.SemaphoreType.DMA((N_SEM_TAIL,)),
                        pltpu.SemaphoreType.DMA((2,))])
    return pl.pallas_call(
        body, name="bwd_tail", grid_spec=grid_spec,
        out_shape=(jax.ShapeDtypeStruct((SEQ, D_MODEL), F32),
                   jax.ShapeDtypeStruct((D_MODEL, SHARD_COLS), F32),
                   jax.ShapeDtypeStruct((WO_ROWS, D_MODEL), F32),
                   jax.ShapeDtypeStruct((8, D_MODEL), F32)),
        compiler_params=pltpu.CompilerParams(dimension_semantics=("arbitrary", "arbitrary"),
                                             vmem_limit_bytes=60 * 1024 * 1024),
    )(kidx, h, dproj, wg, gwo, x2d, dx2, g1, small_a, small_b)


def _adam_update(w, g, m, v):
    nm = ADAM_B1 * m + (1.0 - ADAM_B1) * g
    nv = ADAM_B2 * v + (1.0 - ADAM_B2) * (g * g)
    m_hat = nm / (1.0 - ADAM_B1 ** ADAM_STEP)
    v_hat = nv / (1.0 - ADAM_B2 ** ADAM_STEP)
    return -ADAM_LR * (m_hat / (jnp.sqrt(v_hat) + ADAM_EPS) + ADAM_WD * w), nm, nv


def _adamw_all(tot, g_w_in, g_w_out, big, small):
    n = len(small)
    rows = WO_ROWS
    steps = D_MODEL // rows

    def body(tot_ref, *refs):
        ins, outs = refs[:2 + 3 * (2 + n)], refs[2 + 3 * (2 + n):]
        g_refs, wmv = ins[:2], ins[2:]
        loss_ref, quads = outs[0], outs[1:]

        def update(j, g):
            w_ref, m_ref, v_ref = wmv[3 * j:3 * j + 3]
            g_ref, d_ref, nm_ref, nv_ref = quads[4 * j:4 * j + 4]
            g_ref[...] = g
            d_ref[...], nm_ref[...], nv_ref[...] = _adam_update(w_ref[...], g, m_ref[...], v_ref[...])

        update(0, g_refs[0][...])

        @pl.when(pl.program_id(0) == 0)
        def _():
            update(1, g_refs[1][...])
            k = 2 * lax.axis_index("x") + lax.axis_index("y")
            mine = pl.ds(pl.multiple_of(k * HEAD, HEAD), HEAD)
            loss_ref[...] = tot_ref[7:8, 0:1]
            grads = [tot_ref[0:1, :], tot_ref[1:2, :], tot_ref[2:3, 0:D_HGRN], tot_ref[2:3, D_HGRN:],
                     jnp.concatenate([tot_ref[3:4, 0:D_HGRN], tot_ref[3:4, D_HGRN:]], axis=0),
                     jnp.concatenate([tot_ref[4 + tap:5 + tap, mine] for tap in range(3)], axis=1)]
            for j, g in enumerate(grads):
                update(2 + j, g)

    whole = lambda a: pl.BlockSpec(a.shape, lambda i: (0, 0))
    blk = pl.BlockSpec((rows, SHARD_COLS), lambda i: (i, 0))
    arrays = [a for triple in big + small for a in triple]
    in_specs = ([whole(tot), blk, whole(g_w_out)] + [blk] * 3 + [whole(a) for a in arrays[3:]])
    shapes = [big[0][0], big[1][0]] + [w for w, _, _ in small]
    out_shape = (jax.ShapeDtypeStruct((1, 1), F32),) + tuple(
        jax.ShapeDtypeStruct(w.shape, F32) for w in shapes for _ in range(4))
    out_specs = (pl.BlockSpec((1, 1), lambda i: (0, 0)),) + (blk,) * 4 + tuple(
        whole(w) for w in shapes[1:] for _ in range(4))
    outs = pl.pallas_call(
        body, name="adamw_all", grid=(steps,), out_shape=out_shape, in_specs=in_specs, out_specs=out_specs,
        compiler_params=pltpu.CompilerParams(dimension_semantics=("arbitrary",), vmem_limit_bytes=VMEM_LIMIT),
    )(tot, g_w_in, g_w_out, *arrays)
    return [outs[0]] + [outs[1 + 4 * j:5 + 4 * j] for j in range(2 + n)]


def _local_step(x2d, tgt, proj, lb_logits, cw, ga, gcn, w_out, gf):
    g64 = _group_matrix(HEAD, CONV_GROUP)
    mixed, o, cv, states, sg, b, wog = _mix_fwd(proj, lb_logits, cw, ga, gcn, g64, w_out)
    dx2, dmixed, gwo, part_out = _out_loss(x2d, mixed, wog.reshape(D_MODEL, D_MODEL), gf, tgt)
    dproj, part_mix = _mix_bwd(proj, o, cv, states, sg, b, dmixed, lb_logits, cw, ga, gcn, g64)
    return dproj, dx2, gwo.reshape(N_SHARD, WO_ROWS, D_MODEL), part_out, part_mix


def kernel(x, norm_gain, w_in, lb_logits, conv_w, hgrn_norm_gain, conv_norm_gain, w_out, final_norm_gain, loss_target, m_norm_gain, m_w_in, m_lb_logits, m_conv_w, m_hgrn_norm_gain, m_conv_norm_gain, m_w_out, m_final_norm_gain, v_norm_gain, v_w_in, v_lb_logits, v_conv_w, v_hgrn_norm_gain, v_conv_norm_gain, v_w_out, v_final_norm_gain):
    k = 2 * lax.axis_index("x") + lax.axis_index("y")
    kidx = jnp.reshape(k, (1,)).astype(jnp.int32)
    row = lambda a: a.reshape(1, D_MODEL)
    taps = lambda a: a.reshape(1, 3 * HEAD)
    h, proj, wg, cw = _gather_proj(kidx, x[0], norm_gain, w_in, taps(conv_w))
    dproj, dx2, gwo, part_out, part_mix = _local_step(
        x[0], loss_target[0], proj, lb_logits, cw, hgrn_norm_gain, conv_norm_gain, w_out, row(final_norm_gain))
    grad_x, rg_w_in, rg_w_out, tot = _bwd_tail(kidx, h, dproj, wg, gwo, x[0], dx2, norm_gain, part_out, part_mix)

    (loss, (g_w_in, d_w_in, nm_w_in, nv_w_in), (g_w_out, d_w_out, nm_w_out, nv_w_out),
     (g_norm_gain, d_ng, nm_ng, nv_ng), (g_final, d_fg, nm_fg, nv_fg), (g_hgrn, d_hg, nm_hg, nv_hg),
     (g_convn, d_cg, nm_cg, nv_cg), (g_lb, d_lb, nm_lb, nv_lb), (g_conv_w, d_cw, nm_cw, nv_cw)) = _adamw_all(
        tot, rg_w_in, rg_w_out,
        [(w_in[0], m_w_in[0], v_w_in[0]), (w_out[0], m_w_out[0], v_w_out[0])],
        [(norm_gain, m_norm_gain, v_norm_gain),
         (row(final_norm_gain), row(m_final_norm_gain), row(v_final_norm_gain)),
         (hgrn_norm_gain, m_hgrn_norm_gain, v_hgrn_norm_gain),
         (conv_norm_gain, m_conv_norm_gain, v_conv_norm_gain),
         (lb_logits, m_lb_logits, v_lb_logits),
         (taps(conv_w), taps(m_conv_w), taps(v_conv_w))])
    flat = lambda a: a.reshape(D_MODEL)
    untap = lambda a: a.reshape(1, 3, HEAD)
    return (loss.reshape(()), grad_x[None],
            g_norm_gain, g_w_in[None], g_lb, untap(g_conv_w), g_hgrn, g_convn, g_w_out[None], flat(g_final),
            d_ng, d_w_in[None], d_lb, untap(d_cw), d_hg, d_cg, d_w_out[None], flat(d_fg),
            nm_ng, nm_w_in[None], nm_lb, untap(nm_cw), nm_hg, nm_cg, nm_w_out[None], flat(nm_fg),
            nv_ng, nv_w_in[None], nv_lb, untap(nv_cw), nv_hg, nv_cg, nv_w_out[None], flat(nv_fg))
```

```python
import jax
import jax.numpy as jnp
import numpy as np
from jax import lax
from jax.experimental import pallas as pl
from jax.experimental.pallas import tpu as pltpu

F32 = jnp.float32
BF16 = jnp.bfloat16
MESH = pl.DeviceIdType.MESH

SEQ = 2048
D_MODEL = 1024
D_HGRN = 512
D_CONV = 512
HEAD = 128
N_HEADS = 4
CHUNK = 64
CONV_GROUP = 64
N_SHARD = 4
SHARD_COLS = 1024
WO_ROWS = 256
EPS = 1e-6
TB = 256
NCB = TB // CHUNK
N_CHUNKS = SEQ // CHUNK
N_DEV = 8

ADAM_LR = 0.001
ADAM_B1 = 0.9
ADAM_B2 = 0.999
ADAM_EPS = 1e-08
ADAM_WD = 0.01
ADAM_STEP = 10

VMEM_LIMIT = 56 * 1024 * 1024


def _dot(a, b):
    return jnp.dot(a, b, preferred_element_type=F32)


def _dot_nt(a, b):
    return lax.dot_general(a, b, (((1,), (1,)), ((), ())), preferred_element_type=F32)


def _dot_tn(a, b):
    return lax.dot_general(a, b, (((0,), (0,)), ((), ())), preferred_element_type=F32)


def _split_bf16(x, n):
    parts = []
    r = x
    for _ in range(n):
        p = r.astype(BF16)
        parts.append(p)
        r = r - p.astype(F32)
    return parts


def _exact_left(m, x, n=3):
    acc = None
    for p in _split_bf16(x, n):
        t = _dot(m, p)
        acc = t if acc is None else acc + t
    return acc


def _exact_left_many(m, xs, n=3):
    parts = [_split_bf16(x, n) for x in xs]
    accs = [None] * len(xs)
    for i in range(n):
        for j in range(len(xs)):
            t = _dot(m, parts[j][i])
            accs[j] = t if accs[j] is None else accs[j] + t
    return accs


def _group_mean_many(xs, gmat, n=2):
    parts = [_split_bf16(x, n) for x in xs]
    accs = [None] * len(xs)
    for i in range(n):
        for j in range(len(xs)):
            t = _dot(parts[j][i], gmat)
            accs[j] = t if accs[j] is None else accs[j] + t
    return accs


def _group_mean(x, gmat, n=2):
    w = gmat.shape[0]
    outs = []
    for c0 in range(0, x.shape[1], w):
        acc = None
        for p in _split_bf16(x[:, c0:c0 + w], n):
            t = _dot(p, gmat)
            acc = t if acc is None else acc + t
        outs.append(acc)
    return jnp.concatenate(outs, axis=1)


def _sigmoid(x):
    return 1.0 / (1.0 + jnp.exp(-x))


def _lower_bound(lbl):
    l0 = lbl[0:1, :]
    l1 = lbl[1:2, :]
    m = jnp.maximum(l0, l1)
    e0 = jnp.exp(l0 - m)
    e1 = jnp.exp(l1 - m)
    return e0 / (e0 + e1)


def _tri(lower):
    r = lax.broadcasted_iota(jnp.int32, (CHUNK, CHUNK), 0)
    c = lax.broadcasted_iota(jnp.int32, (CHUNK, CHUNK), 1)
    return jnp.where((c <= r) if lower else (c >= r), 1.0, 0.0).astype(BF16)


def _causal():
    r = lax.broadcasted_iota(jnp.int32, (CHUNK, CHUNK), 0)
    c = lax.broadcasted_iota(jnp.int32, (CHUNK, CHUNK), 1)
    return c <= r


def _shift_down(x, sh, prev_tail):
    r = pltpu.roll(x, sh, 0)
    pt = pltpu.roll(prev_tail, sh, 0)
    rows = lax.broadcasted_iota(jnp.int32, prev_tail.shape, 0)
    top = jnp.where(rows < sh, pt, r[0:8])
    return jnp.concatenate([top, r[8:]], axis=0)


def _shift_up(x, sh, next_head):
    n = x.shape[0]
    r = pltpu.roll(x, n - sh, 0)
    nh = pltpu.roll(next_head, 8 - sh, 0)
    rows = lax.broadcasted_iota(jnp.int32, next_head.shape, 0)
    bot = jnp.where(rows >= 8 - sh, nh, r[n - 8:])
    return jnp.concatenate([r[:n - 8], bot], axis=0)


def _group_matrix(width, group):
    r = np.arange(width)[:, None] // group
    c = np.arange(width)[None, :] // group
    return jnp.asarray(np.where(r == c, 1.0 / group, 0.0), dtype=BF16)


TP = 512
TG = 1024
SEM_W, SEM_CW, SEM_W_FWD, N_SEM = 0, 4, 7, 11


def _gather_proj(kidx, x2d, g1, w_in, conv_w):
    half_w = D_MODEL // 2
    half_c = SHARD_COLS // 2
    nt = SEQ // TG
    n_steps = 2 * N_SHARD

    def body(k_ref, x_ref, g_ref, w_ref, cw_ref, h_ref, p_ref, wg_out, cwg_out,
             wg_v, cwg_v, send_sems, recv_sems, out_sems):
        s, t = pl.program_id(0), pl.program_id(1)
        x, y, c = lax.axis_index("x"), lax.axis_index("y"), lax.axis_index("c")
        k = 2 * x + y
        sibling = (x, y, 1 - c)
        chips = [(1 - x, y), (x, 1 - y), (1 - x, 1 - y)]
        kjs = [2 * cx + cy for cx, cy in chips]
        diag = (*chips[2], c)

        def w_half(kk, cc):
            return wg_v.at[kk, pl.ds(cc * half_w, half_w), :]

        def w_quarter(kk, cc, piece):
            return wg_v.at[kk, pl.ds(cc * half_w, half_w), piece * half_c:(piece + 1) * half_c]

        def cw_of(kk):
            return cwg_v.at[:, pl.ds(pl.multiple_of(kk * HEAD, HEAD), HEAD)]

        def copy(sem, ref, to):
            return pltpu.make_async_remote_copy(
                src_ref=ref, dst_ref=ref, send_sem=send_sems.at[sem], recv_sem=recv_sems.at[sem],
                device_id=to, device_id_type=MESH)

        def at_step(sv, tv):
            return pl.when((s == sv) & (t == tv))

        w_direct = ([copy(SEM_W + j, w_half(k, c), (*chips[j], c)) for j in range(2)]
                    + [copy(SEM_W + 2 + p, w_quarter(k, c, p), diag) for p in range(2)])
        cw_direct = [copy(SEM_CW + j, cw_of(k), (*chip, c)) for j, chip in enumerate(chips)]
        w_passed = ([copy(SEM_W_FWD + j, w_half(kjs[j], c), sibling) for j in range(2)]
                    + [copy(SEM_W_FWD + 2 + p, w_quarter(kjs[2], c, p), sibling) for p in range(2)])
        stores = ([pltpu.make_async_copy(wg_v.at[kk], wg_out.at[kk], out_sems.at[i])
                   for i, kk in enumerate([k] + kjs)]
                  + [pltpu.make_async_copy(cwg_v, cwg_out, out_sems.at[4])])

        @at_step(0, 0)
        def _():
            wg_v[k] = w_ref[0].astype(BF16)
            mine = pl.ds(pl.multiple_of(k * HEAD, HEAD), HEAD)
            cwg_v[:, mine] = jnp.zeros((8, HEAD), F32)
            for tap in range(3):
                cwg_v[tap:tap + 1, mine] = cw_ref[:, tap * HEAD:(tap + 1) * HEAD]
            w_direct[0].start()
            w_direct[1].start()
            for cp in cw_direct:
                cp.start()
            stores[0].start()

        @at_step(2, 0)
        def _():
            for j in range(2):
                copy(SEM_W + j, w_half(kjs[j], c), sibling).wait_recv()
                w_passed[j].start()
            w_direct[2].start()
            w_direct[3].start()
            copy(SEM_W_FWD, w_half(kjs[0], 1 - c), sibling).wait_recv()
            stores[1].start()

        @at_step(4, 0)
        def _():
            copy(SEM_W_FWD + 1, w_half(kjs[1], 1 - c), sibling).wait_recv()
            stores[2].start()

        for p in range(2):
            @at_step(6 + p, 0)
            def _(p=p):
                copy(SEM_W + 2 + p, w_quarter(kjs[2], c, p), sibling).wait_recv()
                w_passed[2 + p].start()
                copy(SEM_W_FWD + 2 + p, w_quarter(kjs[2], 1 - c, p), sibling).wait_recv()

        rows = pl.ds(pl.multiple_of(t * TG, TG), TG)

        @pl.when(s == 0)
        def _():
            xv = x_ref[...]
            r = lax.rsqrt(jnp.mean(xv * xv, axis=-1, keepdims=True) + EPS)
            h_ref[rows, :] = (xv * r * g_ref[...]).astype(BF16)

        sh = s >> 1
        js = k ^ (((sh & 1) << 1) | (sh >> 1))
        for piece in range(2):
            @pl.when((s & 1) == piece)
            def _(piece=piece):
                p_ref[...] = _dot(h_ref[rows, :], wg_v[js, :, piece * half_c:(piece + 1) * half_c])

        @at_step(n_steps - 1, nt - 1)
        def _():
            stores[3].start()
            for j in range(3):
                copy(SEM_CW + j, cw_of(kjs[j]), sibling).wait_recv()
            stores[4].start()
            for cp in w_direct + cw_direct + w_passed:
                cp.wait_send()
            for st in stores:
                st.wait()

    def x_map(s, t, kr):
        return (jnp.where(s == 0, t, nt - 1), 0)

    def p_map(s, t, kr):
        sh = s >> 1
        return (t, 2 * (kr[0] ^ (((sh & 1) << 1) | (sh >> 1))) + (s & 1))

    hbm = pl.BlockSpec(memory_space=pl.ANY)
    grid_spec = pltpu.PrefetchScalarGridSpec(
        num_scalar_prefetch=1, grid=(n_steps, nt),
        in_specs=[pl.BlockSpec((TG, D_MODEL), x_map),
                  pl.BlockSpec((1, D_MODEL), lambda s, t, kr: (0, 0)),
                  pl.BlockSpec((1, D_MODEL, SHARD_COLS), lambda s, t, kr: (0, 0, 0)),
                  pl.BlockSpec((1, 3 * HEAD), lambda s, t, kr: (0, 0))],
        out_specs=(pl.BlockSpec((SEQ, D_MODEL), lambda s, t, kr: (0, 0)),
                   pl.BlockSpec((TG, half_c), p_map), hbm, hbm),
        scratch_shapes=[pltpu.VMEM((N_SHARD, D_MODEL, SHARD_COLS), BF16),
                        pltpu.VMEM((8, D_CONV), F32),
                        pltpu.SemaphoreType.DMA((N_SEM,)), pltpu.SemaphoreType.DMA((N_SEM,)),
                        pltpu.SemaphoreType.DMA((5,))])
    return pl.pallas_call(
        body, name="gather_proj", grid_spec=grid_spec,
        out_shape=(jax.ShapeDtypeStruct((SEQ, D_MODEL), BF16),
                   jax.ShapeDtypeStruct((SEQ, N_SHARD * SHARD_COLS), F32),
                   jax.ShapeDtypeStruct((N_SHARD, D_MODEL, SHARD_COLS), BF16),
                   jax.ShapeDtypeStruct((8, D_CONV), F32)),
        compiler_params=pltpu.CompilerParams(dimension_semantics=("arbitrary", "arbitrary"),
                                             vmem_limit_bytes=VMEM_LIMIT),
    )(kidx, x2d, g1, w_in, conv_w)


def _mix_fwd(proj, lb_logits, cw, ga, gcn, g64, w_out):
    half_o = WO_ROWS // 2
    nblk = SEQ // TB

    def body(p_ref, lbl_ref, cw_ref, ga_ref, gcn_ref, g64_ref, wo_ref,
             mixed_ref, o_ref, cv_ref, sto_ref, sg_ref, b_ref, wog_out,
             st_ref, tail_ref, wog_v, send_sems, recv_sems, out_sem):
        i = pl.program_id(0)
        x, y, c = lax.axis_index("x"), lax.axis_index("y"), lax.axis_index("c")
        k = 2 * x + y
        sibling = (x, y, 1 - c)
        chips = [(1 - x, y), (x, 1 - y), (1 - x, 1 - y)]
        kjs = [2 * cx + cy for cx, cy in chips]

        def wo_half(kk, cc):
            return wog_v.at[kk, pl.ds(cc * half_o, half_o), :]

        def copy(sem, ref, to):
            return pltpu.make_async_remote_copy(
                src_ref=ref, dst_ref=ref, send_sem=send_sems.at[sem], recv_sem=recv_sems.at[sem],
                device_id=to, device_id_type=MESH)

        wo_direct = [copy(j, wo_half(k, c), (*chip, c)) for j, chip in enumerate(chips)]
        wo_passed = [copy(3 + j, wo_half(kj, c), sibling) for j, kj in enumerate(kjs)]
        wo_store = pltpu.make_async_copy(wog_v, wog_out, out_sem.at[0])

        @pl.when(i == 0)
        def _():
            st_ref[...] = jnp.zeros_like(st_ref)
            tail_ref[...] = jnp.zeros_like(tail_ref)
            wog_v[k] = wo_ref[0].astype(BF16)
            for cp in wo_direct:
                cp.start()

        @pl.when(i == nblk - 2)
        def _():
            for j in range(3):
                copy(j, wo_half(kjs[j], c), sibling).wait_recv()
                wo_passed[j].start()

        lb = _lower_bound(lbl_ref[...])
        tri = _tri(True)
        causal = _causal()
        g64m = g64_ref[...]
        heads = range(N_HEADS)
        cs = [slice(hd * HEAD, (hd + 1) * HEAD) for hd in heads]
        col = lambda base, hd: slice(base + hd * HEAD, base + (hd + 1) * HEAD)
        for n in range(NCB):
            sl = pl.ds(n * CHUNK, CHUNK)
            sg = [_sigmoid(p_ref[sl, col(512, hd)]) for hd in heads]
            f = [lb[:, cs[hd]] + (1.0 - lb[:, cs[hd]]) * sg[hd] for hd in heads]
            bc = _exact_left_many(tri, [jnp.log(f[hd]) for hd in heads])
            for hd in heads:
                sg_ref[sl, cs[hd]] = sg[hd]
                b_ref[sl, cs[hd]] = bc[hd]
            g = [bc[hd][CHUNK - 1:CHUNK, :] for hd in heads]
            qd = [(p_ref[sl, col(0, hd)] * jnp.exp(bc[hd])).astype(BF16) for hd in heads]
            ki = [((1.0 - f[hd]) * jnp.exp(-bc[hd])).astype(BF16) for hd in heads]
            ke = [((1.0 - f[hd]) * jnp.exp(g[hd] - bc[hd])).astype(BF16) for hd in heads]
            vb = [p_ref[sl, col(1024, hd)].astype(BF16) for hd in heads]
            st = [st_ref[hd] for hd in heads]
            for hd in heads:
                sto_ref[n, hd] = st[hd]
            scm = [_dot_nt(qd[hd], ki[hd]) for hd in heads]
            inter = [_dot_nt(qd[hd], st[hd].astype(BF16)) for hd in heads]
            upd = [_dot_tn(vb[hd], ke[hd]) for hd in heads]
            intra = [_dot(jnp.where(causal, scm[hd], 0.0).astype(BF16), vb[hd]) for hd in heads]
            for hd in heads:
                st_ref[hd] = st[hd] * jnp.exp(g[hd]) + upd[hd]
                o = intra[hd] + inter[hd]
                o_ref[sl, cs[hd]] = o
                ra = lax.rsqrt(jnp.mean(o * o, axis=-1, keepdims=True) + EPS)
                za = p_ref[sl, col(1536, hd)]
                mixed_ref[sl, cs[hd]] = (o * ra * ga_ref[:, cs[hd]] * (za * _sigmoid(za))).astype(BF16)
            yb = []
            for hd in heads:
                cu = p_ref[sl, col(3072, hd)] * p_ref[sl, col(2048, hd)]
                tail = tail_ref[:, cs[hd]]
                cv = (cw_ref[0:1, cs[hd]] * _shift_down(cu, 2, tail) + cw_ref[1:2, cs[hd]] * _shift_down(cu, 1, tail)
                      + cw_ref[2:3, cs[hd]] * cu)
                tail_ref[:, cs[hd]] = cu[CHUNK - 8:, :]
                cv_ref[sl, cs[hd]] = cv
                yb.append(p_ref[sl, col(2560, hd)] * cv)
            ms = _group_mean_many([y * y for y in yb], g64m)
            for hd in heads:
                rb = lax.rsqrt(ms[hd] + EPS)
                zb = p_ref[sl, col(3584, hd)]
                mixed_ref[sl, col(512, hd)] = (yb[hd] * rb * gcn_ref[:, cs[hd]] * (zb * _sigmoid(zb))).astype(BF16)

        @pl.when(i == nblk - 1)
        def _():
            for j in range(3):
                copy(3 + j, wo_half(kjs[j], 1 - c), sibling).wait_recv()
            wo_store.start()
            for cp in wo_direct + wo_passed:
                cp.wait_send()
            wo_store.wait()

    row = lambda w: pl.BlockSpec((1, w), lambda i: (0, 0))
    return pl.pallas_call(
        body, name="mix_fwd", grid=(nblk,),
        out_shape=(jax.ShapeDtypeStruct((SEQ, D_MODEL), BF16),
                   jax.ShapeDtypeStruct((SEQ, D_HGRN), F32),
                   jax.ShapeDtypeStruct((SEQ, D_CONV), F32),
                   jax.ShapeDtypeStruct((N_CHUNKS, N_HEADS, HEAD, HEAD), F32),
                   jax.ShapeDtypeStruct((SEQ, D_HGRN), F32),
                   jax.ShapeDtypeStruct((SEQ, D_HGRN), F32),
                   jax.ShapeDtypeStruct((N_SHARD, WO_ROWS, D_MODEL), BF16)),
        in_specs=[pl.BlockSpec((TB, 4096), lambda i: (i, 0)),
                  pl.BlockSpec((2, D_HGRN), lambda i: (0, 0)),
                  pl.BlockSpec((8, D_CONV), lambda i: (0, 0)),
                  row(D_HGRN), row(D_CONV),
                  pl.BlockSpec((HEAD, HEAD), lambda i: (0, 0)),
                  pl.BlockSpec((1, WO_ROWS, D_MODEL), lambda i: (0, 0, 0))],
        out_specs=(pl.BlockSpec((TB, D_MODEL), lambda i: (i, 0)),
                   pl.BlockSpec((TB, D_HGRN), lambda i: (i, 0)),
                   pl.BlockSpec((TB, D_CONV), lambda i: (i, 0)),
                   pl.BlockSpec((NCB, N_HEADS, HEAD, HEAD), lambda i: (i, 0, 0, 0)),
                   pl.BlockSpec((TB, D_HGRN), lambda i: (i, 0)),
                   pl.BlockSpec((TB, D_HGRN), lambda i: (i, 0)),
                   pl.BlockSpec(memory_space=pl.ANY)),
        scratch_shapes=[pltpu.VMEM((N_HEADS, HEAD, HEAD), F32), pltpu.VMEM((8, D_CONV), F32),
                        pltpu.VMEM((N_SHARD, WO_ROWS, D_MODEL), BF16),
                        pltpu.SemaphoreType.DMA((6,)), pltpu.SemaphoreType.DMA((6,)),
                        pltpu.SemaphoreType.DMA((1,))],
        compiler_params=pltpu.CompilerParams(dimension_semantics=("arbitrary",), vmem_limit_bytes=VMEM_LIMIT),
    )(proj, lb_logits, cw, ga, gcn, g64, w_out)


def _out_loss(x2d, mixed, wog, gf, tgt):
    def body(x_ref, m_ref, wo_ref, gf_ref, t_ref, dx2_ref, dm_ref, gwo_ref, part_ref, acc_ref):
        i = pl.program_id(0)

        @pl.when(i == 0)
        def _():
            acc_ref[...] = jnp.zeros_like(acc_ref)
            part_ref[...] = jnp.zeros_like(part_ref)

        mixed_b = m_ref[...]
        x2 = x_ref[...] + _dot(mixed_b, wo_ref[...])
        r2 = lax.rsqrt(jnp.mean(x2 * x2, axis=-1, keepdims=True) + EPS)
        n2 = x2 * r2
        gfv = gf_ref[...]
        err = n2 * gfv - t_ref[...]
        loss = 0.5 * jnp.sum(jnp.mean(err * err, axis=-1, keepdims=True), axis=0, keepdims=True)
        dy = err * (1.0 / D_MODEL)
        part_ref[1:2, :] += jnp.sum(dy * n2, axis=0, keepdims=True)
        part_ref[7:8, :] += jnp.broadcast_to(loss, (1, D_MODEL))
        dn = dy * gfv
        dx2 = r2 * (dn - n2 * jnp.mean(dn * n2, axis=-1, keepdims=True))
        dx2_ref[...] = dx2
        dx2_b = dx2.astype(BF16)
        dm_ref[...] = _dot_nt(dx2_b, wo_ref[...])
        acc_ref[...] += _dot_tn(mixed_b, dx2_b)

        @pl.when(i == pl.num_programs(0) - 1)
        def _():
            gwo_ref[...] = acc_ref[...].astype(BF16)

    blk = lambda: pl.BlockSpec((TP, D_MODEL), lambda i: (i, 0))
    return pl.pallas_call(
        body, name="out_loss", grid=(SEQ // TP,),
        out_shape=(jax.ShapeDtypeStruct((SEQ, D_MODEL), F32),
                   jax.ShapeDtypeStruct((SEQ, D_MODEL), F32),
                   jax.ShapeDtypeStruct((D_MODEL, D_MODEL), BF16),
                   jax.ShapeDtypeStruct((8, D_MODEL), F32)),
        in_specs=[blk(), blk(), pl.BlockSpec((D_MODEL, D_MODEL), lambda i: (0, 0)),
                  pl.BlockSpec((1, D_MODEL), lambda i: (0, 0)), blk()],
        out_specs=(blk(), blk(), pl.BlockSpec((D_MODEL, D_MODEL), lambda i: (0, 0)),
                   pl.BlockSpec((8, D_MODEL), lambda i: (0, 0))),
        scratch_shapes=[pltpu.VMEM((D_MODEL, D_MODEL), F32)],
        compiler_params=pltpu.CompilerParams(dimension_semantics=("arbitrary",), vmem_limit_bytes=VMEM_LIMIT),
    )(x2d, mixed, wog, gf, tgt)


def _mix_bwd(proj, o, cv, states, sg, b, dmixed, lb_logits, cw, ga, gcn, g64):
    nblk = SEQ // TB

    def body(p_ref, o_ref, cv_ref, st_ref, sg_ref, b_ref, dm_ref, lbl_ref, cw_ref, ga_ref, gcn_ref, g64_ref,
             dp_ref, part_ref, dst_ref, head_ref, dlb_ref):
        i = pl.program_id(0)

        @pl.when(i == 0)
        def _():
            dst_ref[...] = jnp.zeros_like(dst_ref)
            head_ref[...] = jnp.zeros_like(head_ref)
            part_ref[...] = jnp.zeros_like(part_ref)
            dlb_ref[...] = jnp.zeros_like(dlb_ref)

        lb = _lower_bound(lbl_ref[...])
        triu = _tri(False)
        causal = _causal()
        g64m = g64_ref[...]
        rowsum = lambda a: jnp.sum(a, axis=0, keepdims=True)
        heads = range(N_HEADS)
        cs = [slice(hd * HEAD, (hd + 1) * HEAD) for hd in heads]
        col = lambda base, hd: slice(base + hd * HEAD, base + (hd + 1) * HEAD)
        for n in reversed(range(NCB)):
            sl = pl.ds(n * CHUNK, CHUNK)
            cvv = [cv_ref[sl, cs[hd]] for hd in heads]
            gb = [p_ref[sl, col(2560, hd)] for hd in heads]
            yb = [gb[hd] * cvv[hd] for hd in heads]
            ms = _group_mean_many([y * y for y in yb], g64m)
            rb, nb, dnb = [], [], []
            for hd in heads:
                rb.append(lax.rsqrt(ms[hd] + EPS))
                nb.append(yb[hd] * rb[hd])
                zb = p_ref[sl, col(3584, hd)]
                sgb = _sigmoid(zb)
                dmb = dm_ref[sl, col(512, hd)]
                gcv = gcn_ref[:, cs[hd]]
                part_ref[2:3, col(512, hd)] += rowsum(dmb * nb[hd] * (zb * sgb))
                dp_ref[sl, col(3584, hd)] = (dmb * nb[hd] * gcv * (sgb * (1.0 + zb * (1.0 - sgb)))).astype(BF16)
                dnb.append(dmb * gcv * (zb * sgb))
            mdn = _group_mean_many([dnb[hd] * nb[hd] for hd in heads], g64m)
            for hd in heads:
                dyb = rb[hd] * (dnb[hd] - nb[hd] * mdn[hd])
                dp_ref[sl, col(2560, hd)] = (dyb * cvv[hd]).astype(BF16)
                dcv = dyb * gb[hd]
                head = head_ref[:, cs[hd]]
                dcv1 = _shift_up(dcv, 1, head)
                dcv2 = _shift_up(dcv, 2, head)
                head_ref[:, cs[hd]] = dcv[0:8, :]
                u = p_ref[sl, col(2048, hd)]
                gc = p_ref[sl, col(3072, hd)]
                cu = gc * u
                part_ref[4:5, cs[hd]] += rowsum(dcv2 * cu)
                part_ref[5:6, cs[hd]] += rowsum(dcv1 * cu)
                part_ref[6:7, cs[hd]] += rowsum(dcv * cu)
                dcu = cw_ref[2:3, cs[hd]] * dcv + cw_ref[1:2, cs[hd]] * dcv1 + cw_ref[0:1, cs[hd]] * dcv2
                dp_ref[sl, col(3072, hd)] = (dcu * u).astype(BF16)
                dp_ref[sl, col(2048, hd)] = (dcu * gc).astype(BF16)
            do_b = []
            for hd in heads:
                ov = o_ref[sl, cs[hd]]
                ra = lax.rsqrt(jnp.mean(ov * ov, axis=-1, keepdims=True) + EPS)
                na = ov * ra
                za = p_ref[sl, col(1536, hd)]
                sga = _sigmoid(za)
                dma = dm_ref[sl, cs[hd]]
                gav = ga_ref[:, cs[hd]]
                part_ref[2:3, cs[hd]] += rowsum(dma * na * (za * sga))
                dp_ref[sl, col(1536, hd)] = (dma * na * gav * (sga * (1.0 + za * (1.0 - sga)))).astype(BF16)
                dna = dma * gav * (za * sga)
                do_b.append((ra * (dna - na * jnp.mean(dna * na, axis=-1, keepdims=True))).astype(BF16))
            s = [sg_ref[sl, cs[hd]] for hd in heads]
            f = [lb[:, cs[hd]] + (1.0 - lb[:, cs[hd]]) * s[hd] for hd in heads]
            bc = [b_ref[sl, cs[hd]] for hd in heads]
            g = [bc[hd][CHUNK - 1:CHUNK, :] for hd in heads]
            eb = [jnp.exp(bc[hd]) for hd in heads]
            enb = [jnp.exp(-bc[hd]) for hd in heads]
            eg = [jnp.exp(g[hd] - bc[hd]) for hd in heads]
            dec = [jnp.exp(g[hd]) for hd in heads]
            qd = [p_ref[sl, cs[hd]] * eb[hd] for hd in heads]
            ki = [(1.0 - f[hd]) * enb[hd] for hd in heads]
            ke = [(1.0 - f[hd]) * eg[hd] for hd in heads]
            qd_b = [a.astype(BF16) for a in qd]
            ki_b = [a.astype(BF16) for a in ki]
            ke_b = [a.astype(BF16) for a in ke]
            vb = [p_ref[sl, col(1024, hd)].astype(BF16) for hd in heads]
            st = [st_ref[n, hd] for hd in heads]
            dst = [dst_ref[hd] for hd in heads]
            st_b = [a.astype(BF16) for a in st]
            dst_b = [a.astype(BF16) for a in dst]
            scm = [_dot_nt(qd_b[hd], ki_b[hd]) for hd in heads]
            amm = [_dot_nt(do_b[hd], vb[hd]) for hd in heads]
            dqd2 = [_dot(do_b[hd], st_b[hd]) for hd in heads]
            dke = [_dot(vb[hd], dst_b[hd]) for hd in heads]
            dv2 = [_dot_nt(ke_b[hd], dst_b[hd]) for hd in heads]
            dsu = [_dot_tn(do_b[hd], qd_b[hd]) for hd in heads]
            sc = [jnp.where(causal, scm[hd], 0.0).astype(BF16) for hd in heads]
            am = [jnp.where(causal, amm[hd], 0.0).astype(BF16) for hd in heads]
            dqd1 = [_dot(am[hd], ki_b[hd]) for hd in heads]
            dki = [_dot_tn(am[hd], qd_b[hd]) for hd in heads]
            dv1 = [_dot_tn(sc[hd], do_b[hd]) for hd in heads]
            db, dgv = [], []
            for hd in heads:
                dqd = dqd1[hd] + dqd2[hd]
                ddec = rowsum(dst[hd] * st[hd])
                dst_ref[hd] = dst[hd] * dec[hd] + dsu[hd]
                dp_ref[sl, cs[hd]] = (dqd * eb[hd]).astype(BF16)
                dp_ref[sl, col(1024, hd)] = (dv1[hd] + dv2[hd]).astype(BF16)
                db.append(dqd * qd[hd] - dki[hd] * ki[hd] - dke[hd] * ke[hd])
                dgv.append(rowsum(dke[hd] * ke[hd]) + ddec * dec[hd])
            rc = _exact_left_many(triu, db, 2)
            for hd in heads:
                df = (rc[hd] + dgv[hd]) / f[hd] - (dki[hd] * enb[hd] + dke[hd] * eg[hd])
                dlb_ref[:, cs[hd]] += rowsum(df * (1.0 - s[hd]))
                dp_ref[sl, col(512, hd)] = (df * (1.0 - lb[:, cs[hd]]) * s[hd] * (1.0 - s[hd])).astype(BF16)

        @pl.when(i == nblk - 1)
        def _():
            row = dlb_ref[...] * lb * (1.0 - lb)
            part_ref[3:4, 0:D_HGRN] = row
            part_ref[3:4, D_HGRN:] = -row

    rev = lambda w: pl.BlockSpec((TB, w), lambda i: (nblk - 1 - i, 0))
    row = lambda w: pl.BlockSpec((1, w), lambda i: (0, 0))
    return pl.pallas_call(
        body, name="mix_bwd", grid=(nblk,),
        out_shape=(jax.ShapeDtypeStruct((SEQ, 4096), BF16),
                   jax.ShapeDtypeStruct((8, D_MODEL), F32)),
        in_specs=[rev(4096), rev(D_HGRN), rev(D_CONV),
                  pl.BlockSpec((NCB, N_HEADS, HEAD, HEAD), lambda i: (nblk - 1 - i, 0, 0, 0)),
                  rev(D_HGRN), rev(D_HGRN), rev(D_MODEL),
                  pl.BlockSpec((2, D_HGRN), lambda i: (0, 0)),
                  pl.BlockSpec((8, D_CONV), lambda i: (0, 0)),
                  row(D_HGRN), row(D_CONV),
                  pl.BlockSpec((HEAD, HEAD), lambda i: (0, 0))],
        out_specs=(rev(4096), pl.BlockSpec((8, D_MODEL), lambda i: (0, 0))),
        scratch_shapes=[pltpu.VMEM((N_HEADS, HEAD, HEAD), F32), pltpu.VMEM((8, D_CONV), F32),
                        pltpu.VMEM((1, D_HGRN), F32)],
        compiler_params=pltpu.CompilerParams(dimension_semantics=("arbitrary",), vmem_limit_bytes=VMEM_LIMIT),
    )(proj, o, cv, states, sg, b, dmixed, lb_logits, cw, ga, gcn, g64)


TT = 1024
TX = 256
(SEM_D2D, SEM_D2D_O, SEM_ICI, SEM_ICI_O, SEM_FIN, SEM_FIN_O, SEM_SMALL, N_SEM_TAIL) = 0, 4, 5, 8, 11, 12, 12, 20


def _bwd_tail(kidx, h, dproj, wg, gwo, x2d, dx2, g1, small_a, small_b):
    hw = D_MODEL // 2
    ho = WO_ROWS // 2
    nt = SEQ // TT
    n_steps = N_SHARD + SEQ // TX // nt

    def body(k_ref, h_ref, dp_ref, w_ref, gwo_ref, x_ref, dx2_ref, g_ref, sm_ref, smb_ref,
             gx_ref, gw_out, gwo_out, osm_ref,
             acc, dh, sendbuf, keep, sibrcv, rcv, sib_o, p_o, rcv_o, res_o, sm_buf, dng,
             send_sems, recv_sems, out_sems):
        s, t = pl.program_id(0), pl.program_id(1)
        x, y, c = lax.axis_index("x"), lax.axis_index("y"), lax.axis_index("c")
        k = 2 * x + y
        me = 4 * x + 2 * y + c
        sibling = (x, y, 1 - c)
        chips = [(1 - x, 1 - y), (1 - x, y), (x, 1 - y)]
        kjs = [2 * cx + cy for cx, cy in chips]
        mine = pl.ds(pl.multiple_of(c * hw, hw), hw)
        other = pl.ds(pl.multiple_of((1 - c) * hw, hw), hw)
        mine_o = pl.ds(pl.multiple_of(c * ho, ho), ho)
        other_o = pl.ds(pl.multiple_of((1 - c) * ho, ho), ho)

        def copy(sem, src, dst, to):
            return pltpu.make_async_remote_copy(
                src_ref=src, dst_ref=dst, send_sem=send_sems.at[sem], recv_sem=recv_sems.at[sem],
                device_id=to, device_id_type=MESH)

        def at_step(sv, tv):
            return pl.when((s == sv) & (t == tv))

        def at_norm_block(b):
            return at_step(N_SHARD + b // nt, b % nt)

        d2d = [copy(SEM_D2D + sv, sendbuf.at[sv], sibrcv.at[sv], sibling) for sv in range(N_SHARD)]
        d2d_o = copy(SEM_D2D_O, gwo_ref.at[:, other_o, :], sib_o, sibling)
        ici = [copy(SEM_ICI + sv, keep.at[sv], rcv.at[sv], (*chips[sv], c)) for sv in range(3)]
        ici_o = [copy(SEM_ICI_O + sv, p_o.at[kjs[sv]], rcv_o.at[sv], (*chips[sv], c)) for sv in range(3)]
        fin = copy(SEM_FIN, acc.at[mine, :], gw_out.at[mine, :], sibling)
        fin_o = copy(SEM_FIN_O, res_o.at[mine_o, :], res_o.at[mine_o, :], sibling)
        smalls = [copy(SEM_SMALL + m, sm_buf.at[me], sm_buf.at[me],
                       (x ^ (m >> 2), y ^ ((m >> 1) & 1), c ^ (m & 1))) for m in range(1, N_DEV)]
        store_w = pltpu.make_async_copy(acc.at[mine, :], gw_out.at[mine, :], out_sems.at[0])
        store_o = pltpu.make_async_copy(res_o, gwo_out, out_sems.at[1])

        @at_step(0, 0)
        def _():
            d2d_o.start()

        @at_step(0, 1)
        def _():
            d2d_o.wait_recv()
            for j in range(N_SHARD):
                p_o[j] = (gwo_ref[j, mine_o, :].astype(F32) + sib_o[j].astype(F32)).astype(BF16)
            res_o[mine_o, :] = gwo_ref[k, mine_o, :].astype(F32) + sib_o[k].astype(F32)
            for cp in ici_o:
                cp.start()

        rows = pl.ds(pl.multiple_of(t * TT, TT), TT)

        @pl.when(s < N_SHARD)
        def _():
            dpb = dp_ref[...]
            part = _dot_tn(h_ref[...], dpb)

            @pl.when(t == 0)
            def _():
                acc[...] = part

            @pl.when(t > 0)
            def _():
                acc[...] += part

            d = _dot_nt(dpb, w_ref[0])

            @pl.when(s == 0)
            def _():
                dh[rows, :] = d

            @pl.when(s > 0)
            def _():
                dh[rows, :] += d

        for sv in range(N_SHARD):
            @at_step(sv, nt - 1)
            def _(sv=sv):
                sendbuf[sv] = acc[other, :].astype(BF16)
                if sv < 3:
                    keep[sv] = acc[mine, :].astype(BF16)
                d2d[sv].start()

        for sv in range(3):
            @at_step(sv + 1, 0)
            def _(sv=sv):
                d2d[sv].wait_recv()
                keep[sv] = (keep[sv].astype(F32) + sibrcv[sv].astype(F32)).astype(BF16)
                ici[sv].start()

        @at_norm_block(0)
        def _():
            d2d[3].wait_recv()
            ici[0].wait_recv()
            acc[mine, :] += sibrcv[3].astype(F32) + rcv[0].astype(F32)

        @at_norm_block(1)
        def _():
            tot = res_o[mine_o, :]
            for sv in range(3):
                ici_o[sv].wait_recv()
                tot = tot + rcv_o[sv].astype(F32)
            res_o[mine_o, :] = tot
            fin_o.start()

        @at_norm_block(2)
        def _():
            ici[1].wait_recv()
            acc[mine, :] += rcv[1].astype(F32)

        @at_norm_block(0)
        def _():
            dng[...] = jnp.zeros_like(dng)

        @pl.when(s >= N_SHARD)
        def _():
            blk = (s - N_SHARD) * nt + t
            dhv = dh[pl.ds(pl.multiple_of(blk * TX, TX), TX), :]
            xv = x_ref[...]
            r = lax.rsqrt(jnp.mean(xv * xv, axis=-1, keepdims=True) + EPS)
            xn = xv * r
            dng[...] += jnp.sum(dhv * xn, axis=0, keepdims=True)
            dxn = dhv * g_ref[...]
            gx_ref[...] = dx2_ref[...] + r * (dxn - xn * jnp.mean(dxn * xn, axis=-1, keepdims=True))

        @at_step(n_steps - 1, nt - 1)
        def _():
            sm_buf[me] = sm_ref[...] + smb_ref[...]
            sm_buf[me, 0:1, :] = dng[...]
            for cp in smalls:
                cp.start()
            ici[2].wait_recv()
            acc[mine, :] += rcv[2].astype(F32)
            fin.start()
            store_w.start()
            for m in range(1, N_DEV):
                copy(SEM_SMALL + m, sm_buf.at[0], sm_buf.at[0], sibling).wait_recv()
            tot = sm_buf[0]
            for d in range(1, N_DEV):
                tot = tot + sm_buf[d]
            osm_ref[...] = tot
            fin_o.wait_recv()
            store_o.start()
            fin.wait_recv()
            for cp in d2d + [d2d_o] + ici + ici_o + [fin, fin_o] + smalls:
                cp.wait_send()
            store_o.wait()
            store_w.wait()

    def shard_of(s, kr):
        return kr[0] ^ (3 - jnp.minimum(s, 3))

    def tok(s, t):
        return jnp.where(s < N_SHARD, t, nt - 1)

    def blk_map(s, t, kr):
        return (jnp.where(s < N_SHARD, 0, (s - N_SHARD) * nt + t), 0)

    hbm = pl.BlockSpec(memory_space=pl.ANY)
    grid_spec = pltpu.PrefetchScalarGridSpec(
        num_scalar_prefetch=1, grid=(n_steps, nt),
        in_specs=[pl.BlockSpec((TT, D_MODEL), lambda s, t, kr: (tok(s, t), 0)),
                  pl.BlockSpec((TT, SHARD_COLS), lambda s, t, kr: (tok(s, t), shard_of(s, kr))),
                  pl.BlockSpec((1, D_MODEL, SHARD_COLS), lambda s, t, kr: (shard_of(s, kr), 0, 0)),
                  pl.BlockSpec((N_SHARD, WO_ROWS, D_MODEL), lambda s, t, kr: (0, 0, 0)),
                  pl.BlockSpec((TX, D_MODEL), blk_map),
                  pl.BlockSpec((TX, D_MODEL), blk_map),
                  pl.BlockSpec((1, D_MODEL), lambda s, t, kr: (0, 0)),
                  pl.BlockSpec((8, D_MODEL), lambda s, t, kr: (0, 0)),
                  pl.BlockSpec((8, D_MODEL), lambda s, t, kr: (0, 0))],
        out_specs=(pl.BlockSpec((TX, D_MODEL), blk_map), hbm, hbm,
                   pl.BlockSpec((8, D_MODEL), lambda s, t, kr: (0, 0))),
        scratch_shapes=[pltpu.VMEM((D_MODEL, SHARD_COLS), F32), pltpu.VMEM((SEQ, D_MODEL), F32),
                        pltpu.VMEM((N_SHARD, hw, SHARD_COLS), BF16), pltpu.VMEM((3, hw, SHARD_COLS), BF16),
                        pltpu.VMEM((N_SHARD, hw, SHARD_COLS), BF16), pltpu.VMEM((3, hw, SHARD_COLS), BF16),
                        pltpu.VMEM((N_SHARD, ho, D_MODEL), BF16), pltpu.VMEM((N_SHARD, ho, D_MODEL), BF16),
                        pltpu.VMEM((3, ho, D_MODEL), BF16), pltpu.VMEM((WO_ROWS, D_MODEL), F32),
                        pltpu.VMEM((N_DEV, 8, D_MODEL), F32), pltpu.VMEM((1, D_MODEL), F32),
                        pltpu.SemaphoreType.DMA((N_SEM_TAIL,)), pltpu.SemaphoreType.DMA((N_SEM_TAIL,)),
                        pltpu.SemaphoreType.DMA((2,))])
    return pl.pallas_call(
        body, name="bwd_tail", grid_spec=grid_spec,
        out_shape=(jax.ShapeDtypeStruct((SEQ, D_MODEL), F32),
                   jax.ShapeDtypeStruct((D_MODEL, SHARD_COLS), F32),
                   jax.ShapeDtypeStruct((WO_ROWS, D_MODEL), F32),
                   jax.ShapeDtypeStruct((8, D_MODEL), F32)),
        compiler_params=pltpu.CompilerParams(dimension_semantics=("arbitrary", "arbitrary"),
                                             vmem_limit_bytes=60 * 1024 * 1024),
    )(kidx, h, dproj, wg, gwo, x2d, dx2, g1, small_a, small_b)


def _adam_update(w, g, m, v):
    nm = ADAM_B1 * m + (1.0 - ADAM_B1) * g
    nv = ADAM_B2 * v + (1.0 - ADAM_B2) * (g * g)
    m_hat = nm / (1.0 - ADAM_B1 ** ADAM_STEP)
    v_hat = nv / (1.0 - ADAM_B2 ** ADAM_STEP)
    return -ADAM_LR * (m_hat / (jnp.sqrt(v_hat) + ADAM_EPS) + ADAM_WD * w), nm, nv


def _adamw_all(tot, g_w_in, g_w_out, big, small, grad_x):
    n = len(small)
    rows = WO_ROWS
    steps = D_MODEL // rows

    def body(tot_ref, *refs):
        gx_ref, gx_out = refs[2 + 3 * (2 + n)], refs[-1]
        gx_out[...] = gx_ref[...]
        ins, outs = refs[:2 + 3 * (2 + n)], refs[3 + 3 * (2 + n):-1]
        g_refs, wmv = ins[:2], ins[2:]
        loss_ref, quads = outs[0], outs[1:]

        def update(j, g):
            w_ref, m_ref, v_ref = wmv[3 * j:3 * j + 3]
            g_ref, d_ref, nm_ref, nv_ref = quads[4 * j:4 * j + 4]
            g_ref[...] = g
            d_ref[...], nm_ref[...], nv_ref[...] = _adam_update(w_ref[...], g, m_ref[...], v_ref[...])

        update(0, g_refs[0][...])

        @pl.when(pl.program_id(0) == 0)
        def _():
            update(1, g_refs[1][...])
            k = 2 * lax.axis_index("x") + lax.axis_index("y")
            mine = pl.ds(pl.multiple_of(k * HEAD, HEAD), HEAD)
            loss_ref[...] = tot_ref[7:8, 0:1]
            grads = [tot_ref[0:1, :], tot_ref[1:2, :], tot_ref[2:3, 0:D_HGRN], tot_ref[2:3, D_HGRN:],
                     jnp.concatenate([tot_ref[3:4, 0:D_HGRN], tot_ref[3:4, D_HGRN:]], axis=0),
                     jnp.concatenate([tot_ref[4 + tap:5 + tap, mine] for tap in range(3)], axis=1)]
            for j, g in enumerate(grads):
                update(2 + j, g)

    whole = lambda a: pl.BlockSpec(a.shape, lambda i: (0, 0))
    blk = pl.BlockSpec((rows, SHARD_COLS), lambda i: (i, 0))
    arrays = [a for triple in big + small for a in triple]
    in_specs = ([whole(tot), blk, whole(g_w_out)] + [blk] * 3 + [whole(a) for a in arrays[3:]])
    shapes = [big[0][0], big[1][0]] + [w for w, _, _ in small]
    out_shape = (jax.ShapeDtypeStruct((1, 1), F32),) + tuple(
        jax.ShapeDtypeStruct(w.shape, F32) for w in shapes for _ in range(4))
    out_specs = (pl.BlockSpec((1, 1), lambda i: (0, 0)),) + (blk,) * 4 + tuple(
        whole(w) for w in shapes[1:] for _ in range(4))
    gx_blk = pl.BlockSpec((SEQ // steps, D_MODEL), lambda i: (i, 0))
    outs = pl.pallas_call(
        body, name="adamw_all", grid=(steps,),
        out_shape=out_shape + (jax.ShapeDtypeStruct(grad_x.shape, F32),),
        in_specs=in_specs + [gx_blk], out_specs=out_specs + (gx_blk,),
        compiler_params=pltpu.CompilerParams(dimension_semantics=("arbitrary",), vmem_limit_bytes=VMEM_LIMIT),
    )(tot, g_w_in, g_w_out, *arrays, grad_x)
    return [outs[0]] + [outs[1 + 4 * j:5 + 4 * j] for j in range(2 + n)] + [outs[-1]]


def _local_step(x2d, tgt, proj, lb_logits, cw, ga, gcn, w_out, gf):
    g64 = _group_matrix(HEAD, CONV_GROUP)
    mixed, o, cv, states, sg, b, wog = _mix_fwd(proj, lb_logits, cw, ga, gcn, g64, w_out)
    dx2, dmixed, gwo, part_out = _out_loss(x2d, mixed, wog.reshape(D_MODEL, D_MODEL), gf, tgt)
    dproj, part_mix = _mix_bwd(proj, o, cv, states, sg, b, dmixed, lb_logits, cw, ga, gcn, g64)
    return dproj, dx2, gwo.reshape(N_SHARD, WO_ROWS, D_MODEL), part_out, part_mix


def kernel(x, norm_gain, w_in, lb_logits, conv_w, hgrn_norm_gain, conv_norm_gain, w_out, final_norm_gain, loss_target, m_norm_gain, m_w_in, m_lb_logits, m_conv_w, m_hgrn_norm_gain, m_conv_norm_gain, m_w_out, m_final_norm_gain, v_norm_gain, v_w_in, v_lb_logits, v_conv_w, v_hgrn_norm_gain, v_conv_norm_gain, v_w_out, v_final_norm_gain):
    k = 2 * lax.axis_index("x") + lax.axis_index("y")
    kidx = jnp.reshape(k, (1,)).astype(jnp.int32)
    row = lambda a: a.reshape(1, D_MODEL)
    taps = lambda a: a.reshape(1, 3 * HEAD)
    h, proj, wg, cw = _gather_proj(kidx, x[0], norm_gain, w_in, taps(conv_w))
    dproj, dx2, gwo, part_out, part_mix = _local_step(
        x[0], loss_target[0], proj, lb_logits, cw, hgrn_norm_gain, conv_norm_gain, w_out, row(final_norm_gain))
    rgrad_x, rg_w_in, rg_w_out, tot = _bwd_tail(kidx, h, dproj, wg, gwo, x[0], dx2, norm_gain, part_out, part_mix)

    (loss, (g_w_in, d_w_in, nm_w_in, nv_w_in), (g_w_out, d_w_out, nm_w_out, nv_w_out),
     (g_norm_gain, d_ng, nm_ng, nv_ng), (g_final, d_fg, nm_fg, nv_fg), (g_hgrn, d_hg, nm_hg, nv_hg),
     (g_convn, d_cg, nm_cg, nv_cg), (g_lb, d_lb, nm_lb, nv_lb), (g_conv_w, d_cw, nm_cw, nv_cw),
     grad_x) = _adamw_all(
        tot, rg_w_in, rg_w_out,
        [(w_in[0], m_w_in[0], v_w_in[0]), (w_out[0], m_w_out[0], v_w_out[0])],
        [(norm_gain, m_norm_gain, v_norm_gain),
         (row(final_norm_gain), row(m_final_norm_gain), row(v_final_norm_gain)),
         (hgrn_norm_gain, m_hgrn_norm_gain, v_hgrn_norm_gain),
         (conv_norm_gain, m_conv_norm_gain, v_conv_norm_gain),
         (lb_logits, m_lb_logits, v_lb_logits),
         (taps(conv_w), taps(m_conv_w), taps(v_conv_w))],
        rgrad_x)
    flat = lambda a: a.reshape(D_MODEL)
    untap = lambda a: a.reshape(1, 3, HEAD)
    return (loss.reshape(()), grad_x[None],
            g_norm_gain, g_w_in[None], g_lb, untap(g_conv_w), g_hgrn, g_convn, g_w_out[None], flat(g_final),
            d_ng, d_w_in[None], d_lb, untap(d_cw), d_hg, d_cg, d_w_out[None], flat(d_fg),
            nm_ng, nm_w_in[None], nm_lb, untap(nm_cw), nm_hg, nm_cg, nm_w_out[None], flat(nm_fg),
            nv_ng, nv_w_in[None], nv_lb, untap(nv_cw), nv_hg, nv_cg, nv_w_out[None], flat(nv_fg))
```

```python
import jax
import jax.numpy as jnp
import numpy as np
from jax import lax
from jax.experimental import pallas as pl
from jax.experimental.pallas import tpu as pltpu

F32 = jnp.float32
BF16 = jnp.bfloat16
MESH = pl.DeviceIdType.MESH

SEQ = 2048
D_MODEL = 1024
D_HGRN = 512
D_CONV = 512
HEAD = 128
N_HEADS = 4
CHUNK = 64
CONV_GROUP = 64
N_SHARD = 4
SHARD_COLS = 1024
WO_ROWS = 256
EPS = 1e-6
TB = 256
NCB = TB // CHUNK
N_CHUNKS = SEQ // CHUNK
N_DEV = 8
AUX_O, AUX_CV, AUX_B, AUX_COLS = 0, 512, 1024, 1536

ADAM_LR = 0.001
ADAM_B1 = 0.9
ADAM_B2 = 0.999
ADAM_EPS = 1e-08
ADAM_WD = 0.01
ADAM_STEP = 10

VMEM_LIMIT = 56 * 1024 * 1024


def _dot(a, b):
    return jnp.dot(a, b, preferred_element_type=F32)


def _dot_nt(a, b):
    return lax.dot_general(a, b, (((1,), (1,)), ((), ())), preferred_element_type=F32)


def _dot_tn(a, b):
    return lax.dot_general(a, b, (((0,), (0,)), ((), ())), preferred_element_type=F32)


def _split_bf16(x, n):
    parts = []
    r = x
    for _ in range(n):
        p = r.astype(BF16)
        parts.append(p)
        r = r - p.astype(F32)
    return parts


def _exact_left(m, x, n=3):
    acc = None
    for p in _split_bf16(x, n):
        t = _dot(m, p)
        acc = t if acc is None else acc + t
    return acc


def _exact_left_many(m, xs, n=3):
    parts = [_split_bf16(x, n) for x in xs]
    accs = [None] * len(xs)
    for i in range(n):
        for j in range(len(xs)):
            t = _dot(m, parts[j][i])
            accs[j] = t if accs[j] is None else accs[j] + t
    return accs


def _group_mean_many(xs, gmat, n=2):
    parts = [_split_bf16(x, n) for x in xs]
    accs = [None] * len(xs)
    for i in range(n):
        for j in range(len(xs)):
            t = _dot(parts[j][i], gmat)
            accs[j] = t if accs[j] is None else accs[j] + t
    return accs


def _group_mean(x, gmat, n=2):
    w = gmat.shape[0]
    outs = []
    for c0 in range(0, x.shape[1], w):
        acc = None
        for p in _split_bf16(x[:, c0:c0 + w], n):
            t = _dot(p, gmat)
            acc = t if acc is None else acc + t
        outs.append(acc)
    return jnp.concatenate(outs, axis=1)


def _sigmoid(x):
    return 1.0 / (1.0 + jnp.exp(-x))


def _lower_bound(lbl):
    l0 = lbl[0:1, :]
    l1 = lbl[1:2, :]
    m = jnp.maximum(l0, l1)
    e0 = jnp.exp(l0 - m)
    e1 = jnp.exp(l1 - m)
    return e0 / (e0 + e1)


def _tri(lower):
    r = lax.broadcasted_iota(jnp.int32, (CHUNK, CHUNK), 0)
    c = lax.broadcasted_iota(jnp.int32, (CHUNK, CHUNK), 1)
    return jnp.where((c <= r) if lower else (c >= r), 1.0, 0.0).astype(BF16)


def _causal():
    r = lax.broadcasted_iota(jnp.int32, (CHUNK, CHUNK), 0)
    c = lax.broadcasted_iota(jnp.int32, (CHUNK, CHUNK), 1)
    return c <= r


def _shift_down(x, sh, prev_tail):
    r = pltpu.roll(x, sh, 0)
    pt = pltpu.roll(prev_tail, sh, 0)
    rows = lax.broadcasted_iota(jnp.int32, prev_tail.shape, 0)
    top = jnp.where(rows < sh, pt, r[0:8])
    return jnp.concatenate([top, r[8:]], axis=0)


def _shift_up(x, sh, next_head):
    n = x.shape[0]
    r = pltpu.roll(x, n - sh, 0)
    nh = pltpu.roll(next_head, 8 - sh, 0)
    rows = lax.broadcasted_iota(jnp.int32, next_head.shape, 0)
    bot = jnp.where(rows >= 8 - sh, nh, r[n - 8:])
    return jnp.concatenate([r[:n - 8], bot], axis=0)


def _group_matrix(width, group):
    r = np.arange(width)[:, None] // group
    c = np.arange(width)[None, :] // group
    return jnp.asarray(np.where(r == c, 1.0 / group, 0.0), dtype=BF16)


TP = 512
TG = 1024
SEM_W, SEM_CW, SEM_W_FWD, N_SEM = 0, 4, 7, 11


def _gather_proj(kidx, x2d, g1, w_in, conv_w):
    half_w = D_MODEL // 2
    half_c = SHARD_COLS // 2
    nt = SEQ // TG
    n_steps = 2 * N_SHARD

    def body(k_ref, x_ref, g_ref, w_ref, cw_ref, h_ref, p_ref, wg_out, cwg_out,
             wg_v, cwg_v, send_sems, recv_sems, out_sems):
        s, t = pl.program_id(0), pl.program_id(1)
        x, y, c = lax.axis_index("x"), lax.axis_index("y"), lax.axis_index("c")
        k = 2 * x + y
        sibling = (x, y, 1 - c)
        chips = [(1 - x, y), (x, 1 - y), (1 - x, 1 - y)]
        kjs = [2 * cx + cy for cx, cy in chips]
        diag = (*chips[2], c)

        def w_half(kk, cc):
            return wg_v.at[kk, pl.ds(cc * half_w, half_w), :]

        def w_quarter(kk, cc, piece):
            return wg_v.at[kk, pl.ds(cc * half_w, half_w), piece * half_c:(piece + 1) * half_c]

        def cw_of(kk):
            return cwg_v.at[:, pl.ds(pl.multiple_of(kk * HEAD, HEAD), HEAD)]

        def copy(sem, ref, to):
            return pltpu.make_async_remote_copy(
                src_ref=ref, dst_ref=ref, send_sem=send_sems.at[sem], recv_sem=recv_sems.at[sem],
                device_id=to, device_id_type=MESH)

        def at_step(sv, tv):
            return pl.when((s == sv) & (t == tv))

        w_direct = ([copy(SEM_W + j, w_half(k, c), (*chips[j], c)) for j in range(2)]
                    + [copy(SEM_W + 2 + p, w_quarter(k, c, p), diag) for p in range(2)])
        cw_direct = [copy(SEM_CW + j, cw_of(k), (*chip, c)) for j, chip in enumerate(chips)]
        w_passed = ([copy(SEM_W_FWD + j, w_half(kjs[j], c), sibling) for j in range(2)]
                    + [copy(SEM_W_FWD + 2 + p, w_quarter(kjs[2], c, p), sibling) for p in range(2)])
        stores = ([pltpu.make_async_copy(wg_v.at[kk], wg_out.at[kk], out_sems.at[i])
                   for i, kk in enumerate([k] + kjs)]
                  + [pltpu.make_async_copy(cwg_v, cwg_out, out_sems.at[4])])

        @at_step(0, 0)
        def _():
            wg_v[k] = w_ref[0].astype(BF16)
            mine = pl.ds(pl.multiple_of(k * HEAD, HEAD), HEAD)
            cwg_v[:, mine] = jnp.zeros((8, HEAD), F32)
            for tap in range(3):
                cwg_v[tap:tap + 1, mine] = cw_ref[:, tap * HEAD:(tap + 1) * HEAD]
            w_direct[0].start()
            w_direct[1].start()
            for cp in cw_direct:
                cp.start()
            stores[0].start()

        @at_step(2, 0)
        def _():
            for j in range(2):
                copy(SEM_W + j, w_half(kjs[j], c), sibling).wait_recv()
                w_passed[j].start()
            w_direct[2].start()
            w_direct[3].start()
            copy(SEM_W_FWD, w_half(kjs[0], 1 - c), sibling).wait_recv()
            stores[1].start()

        @at_step(4, 0)
        def _():
            copy(SEM_W_FWD + 1, w_half(kjs[1], 1 - c), sibling).wait_recv()
            stores[2].start()

        for p in range(2):
            @at_step(6 + p, 0)
            def _(p=p):
                copy(SEM_W + 2 + p, w_quarter(kjs[2], c, p), sibling).wait_recv()
                w_passed[2 + p].start()
                copy(SEM_W_FWD + 2 + p, w_quarter(kjs[2], 1 - c, p), sibling).wait_recv()

        rows = pl.ds(pl.multiple_of(t * TG, TG), TG)

        @pl.when(s == 0)
        def _():
            xv = x_ref[...]
            r = lax.rsqrt(jnp.mean(xv * xv, axis=-1, keepdims=True) + EPS)
            h_ref[rows, :] = (xv * r * g_ref[...]).astype(BF16)

        sh = s >> 1
        js = k ^ (((sh & 1) << 1) | (sh >> 1))
        for piece in range(2):
            @pl.when((s & 1) == piece)
            def _(piece=piece):
                p_ref[...] = _dot(h_ref[rows, :], wg_v[js, :, piece * half_c:(piece + 1) * half_c])

        @at_step(n_steps - 1, nt - 1)
        def _():
            stores[3].start()
            for j in range(3):
                copy(SEM_CW + j, cw_of(kjs[j]), sibling).wait_recv()
            stores[4].start()
            for cp in w_direct + cw_direct + w_passed:
                cp.wait_send()
            for st in stores:
                st.wait()

    def x_map(s, t, kr):
        return (jnp.where(s == 0, t, nt - 1), 0)

    def p_map(s, t, kr):
        sh = s >> 1
        return (t, 2 * (kr[0] ^ (((sh & 1) << 1) | (sh >> 1))) + (s & 1))

    hbm = pl.BlockSpec(memory_space=pl.ANY)
    grid_spec = pltpu.PrefetchScalarGridSpec(
        num_scalar_prefetch=1, grid=(n_steps, nt),
        in_specs=[pl.BlockSpec((TG, D_MODEL), x_map),
                  pl.BlockSpec((1, D_MODEL), lambda s, t, kr: (0, 0)),
                  pl.BlockSpec((1, D_MODEL, SHARD_COLS), lambda s, t, kr: (0, 0, 0)),
                  pl.BlockSpec((1, 3 * HEAD), lambda s, t, kr: (0, 0))],
        out_specs=(pl.BlockSpec((SEQ, D_MODEL), lambda s, t, kr: (0, 0)),
                   pl.BlockSpec((TG, half_c), p_map), hbm, hbm),
        scratch_shapes=[pltpu.VMEM((N_SHARD, D_MODEL, SHARD_COLS), BF16),
                        pltpu.VMEM((8, D_CONV), F32),
                        pltpu.SemaphoreType.DMA((N_SEM,)), pltpu.SemaphoreType.DMA((N_SEM,)),
                        pltpu.SemaphoreType.DMA((5,))])
    return pl.pallas_call(
        body, name="gather_proj", grid_spec=grid_spec,
        out_shape=(jax.ShapeDtypeStruct((SEQ, D_MODEL), BF16),
                   jax.ShapeDtypeStruct((SEQ, N_SHARD * SHARD_COLS), F32),
                   jax.ShapeDtypeStruct((N_SHARD, D_MODEL, SHARD_COLS), BF16),
                   jax.ShapeDtypeStruct((8, D_CONV), F32)),
        compiler_params=pltpu.CompilerParams(dimension_semantics=("arbitrary", "arbitrary"),
                                             vmem_limit_bytes=VMEM_LIMIT),
    )(kidx, x2d, g1, w_in, conv_w)


def _mix_fwd(proj, lb_logits, cw, ga, gcn, g64, w_out):
    half_o = WO_ROWS // 2
    nblk = SEQ // TB

    def body(p_ref, lbl_ref, cw_ref, ga_ref, gcn_ref, g64_ref, wo_ref,
             mixed_ref, aux_ref, sto_ref, wog_out,
             st_ref, tail_ref, wog_v, send_sems, recv_sems, out_sem):
        i = pl.program_id(0)
        x, y, c = lax.axis_index("x"), lax.axis_index("y"), lax.axis_index("c")
        k = 2 * x + y
        sibling = (x, y, 1 - c)
        chips = [(1 - x, y), (x, 1 - y), (1 - x, 1 - y)]
        kjs = [2 * cx + cy for cx, cy in chips]

        def wo_half(kk, cc):
            return wog_v.at[kk, pl.ds(cc * half_o, half_o), :]

        def copy(sem, ref, to):
            return pltpu.make_async_remote_copy(
                src_ref=ref, dst_ref=ref, send_sem=send_sems.at[sem], recv_sem=recv_sems.at[sem],
                device_id=to, device_id_type=MESH)

        wo_direct = [copy(j, wo_half(k, c), (*chip, c)) for j, chip in enumerate(chips)]
        wo_passed = [copy(3 + j, wo_half(kj, c), sibling) for j, kj in enumerate(kjs)]
        wo_store = pltpu.make_async_copy(wog_v, wog_out, out_sem.at[0])

        @pl.when(i == 0)
        def _():
            st_ref[...] = jnp.zeros_like(st_ref)
            tail_ref[...] = jnp.zeros_like(tail_ref)
            wog_v[k] = wo_ref[0].astype(BF16)
            for cp in wo_direct:
                cp.start()

        @pl.when(i == nblk - 2)
        def _():
            for j in range(3):
                copy(j, wo_half(kjs[j], c), sibling).wait_recv()
                wo_passed[j].start()

        lb = _lower_bound(lbl_ref[...])
        tri = _tri(True)
        causal = _causal()
        g64m = g64_ref[...]
        heads = range(N_HEADS)
        cs = [slice(hd * HEAD, (hd + 1) * HEAD) for hd in heads]
        col = lambda base, hd: slice(base + hd * HEAD, base + (hd + 1) * HEAD)
        for n in range(NCB):
            sl = pl.ds(n * CHUNK, CHUNK)
            sg = [_sigmoid(p_ref[sl, col(512, hd)]) for hd in heads]
            f = [lb[:, cs[hd]] + (1.0 - lb[:, cs[hd]]) * sg[hd] for hd in heads]
            bc = _exact_left_many(tri, [jnp.log(f[hd]) for hd in heads])
            for hd in heads:
                aux_ref[sl, col(AUX_B, hd)] = bc[hd]
            g = [bc[hd][CHUNK - 1:CHUNK, :] for hd in heads]
            qd = [(p_ref[sl, col(0, hd)] * jnp.exp(bc[hd])).astype(BF16) for hd in heads]
            ki = [((1.0 - f[hd]) * jnp.exp(-bc[hd])).astype(BF16) for hd in heads]
            ke = [((1.0 - f[hd]) * jnp.exp(g[hd] - bc[hd])).astype(BF16) for hd in heads]
            vb = [p_ref[sl, col(1024, hd)].astype(BF16) for hd in heads]
            st = [st_ref[hd] for hd in heads]
            st_b = [a.astype(BF16) for a in st]
            for hd in heads:
                sto_ref[n, hd] = st_b[hd]
            scm = [_dot_nt(qd[hd], ki[hd]) for hd in heads]
            inter = [_dot_nt(qd[hd], st_b[hd]) for hd in heads]
            upd = [_dot_tn(vb[hd], ke[hd]) for hd in heads]
            intra = [_dot(jnp.where(causal, scm[hd], 0.0).astype(BF16), vb[hd]) for hd in heads]
            for hd in heads:
                st_ref[hd] = st[hd] * jnp.exp(g[hd]) + upd[hd]
                o = intra[hd] + inter[hd]
                aux_ref[sl, col(AUX_O, hd)] = o
                ra = lax.rsqrt(jnp.mean(o * o, axis=-1, keepdims=True) + EPS)
                za = p_ref[sl, col(1536, hd)]
                mixed_ref[sl, cs[hd]] = (o * ra * ga_ref[:, cs[hd]] * (za * _sigmoid(za))).astype(BF16)
            yb = []
            for hd in heads:
                cu = p_ref[sl, col(3072, hd)] * p_ref[sl, col(2048, hd)]
                tail = tail_ref[:, cs[hd]]
                cv = (cw_ref[0:1, cs[hd]] * _shift_down(cu, 2, tail) + cw_ref[1:2, cs[hd]] * _shift_down(cu, 1, tail)
                      + cw_ref[2:3, cs[hd]] * cu)
                tail_ref[:, cs[hd]] = cu[CHUNK - 8:, :]
                aux_ref[sl, col(AUX_CV, hd)] = cv
                yb.append(p_ref[sl, col(2560, hd)] * cv)
            ms = _group_mean_many([y * y for y in yb], g64m)
            for hd in heads:
                rb = lax.rsqrt(ms[hd] + EPS)
                zb = p_ref[sl, col(3584, hd)]
                mixed_ref[sl, col(512, hd)] = (yb[hd] * rb * gcn_ref[:, cs[hd]] * (zb * _sigmoid(zb))).astype(BF16)

        @pl.when(i == nblk - 1)
        def _():
            for j in range(3):
                copy(3 + j, wo_half(kjs[j], 1 - c), sibling).wait_recv()
            wo_store.start()
            for cp in wo_direct + wo_passed:
                cp.wait_send()
            wo_store.wait()

    row = lambda w: pl.BlockSpec((1, w), lambda i: (0, 0))
    return pl.pallas_call(
        body, name="mix_fwd", grid=(nblk,),
        out_shape=(jax.ShapeDtypeStruct((SEQ, D_MODEL), BF16),
                   jax.ShapeDtypeStruct((SEQ, AUX_COLS), F32),
                   jax.ShapeDtypeStruct((N_CHUNKS, N_HEADS, HEAD, HEAD), BF16),
                   jax.ShapeDtypeStruct((N_SHARD, WO_ROWS, D_MODEL), BF16)),
        in_specs=[pl.BlockSpec((TB, 4096), lambda i: (i, 0)),
                  pl.BlockSpec((2, D_HGRN), lambda i: (0, 0)),
                  pl.BlockSpec((8, D_CONV), lambda i: (0, 0)),
                  row(D_HGRN), row(D_CONV),
                  pl.BlockSpec((HEAD, HEAD), lambda i: (0, 0)),
                  pl.BlockSpec((1, WO_ROWS, D_MODEL), lambda i: (0, 0, 0))],
        out_specs=(pl.BlockSpec((TB, D_MODEL), lambda i: (i, 0)),
                   pl.BlockSpec((TB, AUX_COLS), lambda i: (i, 0)),
                   pl.BlockSpec((NCB, N_HEADS, HEAD, HEAD), lambda i: (i, 0, 0, 0)),
                   pl.BlockSpec(memory_space=pl.ANY)),
        scratch_shapes=[pltpu.VMEM((N_HEADS, HEAD, HEAD), F32), pltpu.VMEM((8, D_CONV), F32),
                        pltpu.VMEM((N_SHARD, WO_ROWS, D_MODEL), BF16),
                        pltpu.SemaphoreType.DMA((6,)), pltpu.SemaphoreType.DMA((6,)),
                        pltpu.SemaphoreType.DMA((1,))],
        compiler_params=pltpu.CompilerParams(dimension_semantics=("arbitrary",), vmem_limit_bytes=VMEM_LIMIT),
    )(proj, lb_logits, cw, ga, gcn, g64, w_out)


def _out_loss(x2d, mixed, wog, gf, tgt):
    def body(x_ref, m_ref, wo_ref, gf_ref, t_ref, dx2_ref, dm_ref, gwo_ref, part_ref, acc_ref):
        i = pl.program_id(0)

        @pl.when(i == 0)
        def _():
            acc_ref[...] = jnp.zeros_like(acc_ref)
            part_ref[...] = jnp.zeros_like(part_ref)

        mixed_b = m_ref[...]
        x2 = x_ref[...] + _dot(mixed_b, wo_ref[...])
        r2 = lax.rsqrt(jnp.mean(x2 * x2, axis=-1, keepdims=True) + EPS)
        n2 = x2 * r2
        gfv = gf_ref[...]
        err = n2 * gfv - t_ref[...]
        loss = 0.5 * jnp.sum(jnp.mean(err * err, axis=-1, keepdims=True), axis=0, keepdims=True)
        dy = err * (1.0 / D_MODEL)
        part_ref[1:2, :] += jnp.sum(dy * n2, axis=0, keepdims=True)
        part_ref[7:8, :] += jnp.broadcast_to(loss, (1, D_MODEL))
        dn = dy * gfv
        dx2 = r2 * (dn - n2 * jnp.mean(dn * n2, axis=-1, keepdims=True))
        dx2_ref[...] = dx2
        dx2_b = dx2.astype(BF16)
        dm_ref[...] = _dot_nt(dx2_b, wo_ref[...])
        acc_ref[...] += _dot_tn(mixed_b, dx2_b)

        @pl.when(i == pl.num_programs(0) - 1)
        def _():
            gwo_ref[...] = acc_ref[...].astype(BF16)

    blk = lambda: pl.BlockSpec((TP, D_MODEL), lambda i: (i, 0))
    return pl.pallas_call(
        body, name="out_loss", grid=(SEQ // TP,),
        out_shape=(jax.ShapeDtypeStruct((SEQ, D_MODEL), F32),
                   jax.ShapeDtypeStruct((SEQ, D_MODEL), F32),
                   jax.ShapeDtypeStruct((D_MODEL, D_MODEL), BF16),
                   jax.ShapeDtypeStruct((8, D_MODEL), F32)),
        in_specs=[blk(), blk(), pl.BlockSpec((D_MODEL, D_MODEL), lambda i: (0, 0)),
                  pl.BlockSpec((1, D_MODEL), lambda i: (0, 0)), blk()],
        out_specs=(blk(), blk(), pl.BlockSpec((D_MODEL, D_MODEL), lambda i: (0, 0)),
                   pl.BlockSpec((8, D_MODEL), lambda i: (0, 0))),
        scratch_shapes=[pltpu.VMEM((D_MODEL, D_MODEL), F32)],
        compiler_params=pltpu.CompilerParams(dimension_semantics=("arbitrary",), vmem_limit_bytes=VMEM_LIMIT),
    )(x2d, mixed, wog, gf, tgt)


def _mix_bwd(proj, aux, states, dmixed, lb_logits, cw, ga, gcn, g64):
    nblk = SEQ // TB

    def body(p_ref, aux_ref, st_ref, dm_ref, lbl_ref, cw_ref, ga_ref, gcn_ref, g64_ref,
             dp_ref, part_ref, dst_ref, head_ref, dlb_ref):
        i = pl.program_id(0)

        @pl.when(i == 0)
        def _():
            dst_ref[...] = jnp.zeros_like(dst_ref)
            head_ref[...] = jnp.zeros_like(head_ref)
            part_ref[...] = jnp.zeros_like(part_ref)
            dlb_ref[...] = jnp.zeros_like(dlb_ref)

        lb = _lower_bound(lbl_ref[...])
        triu = _tri(False)
        causal = _causal()
        g64m = g64_ref[...]
        rowsum = lambda a: jnp.sum(a, axis=0, keepdims=True)
        heads = range(N_HEADS)
        cs = [slice(hd * HEAD, (hd + 1) * HEAD) for hd in heads]
        col = lambda base, hd: slice(base + hd * HEAD, base + (hd + 1) * HEAD)
        for n in reversed(range(NCB)):
            sl = pl.ds(n * CHUNK, CHUNK)
            cvv = [aux_ref[sl, col(AUX_CV, hd)] for hd in heads]
            gb = [p_ref[sl, col(2560, hd)] for hd in heads]
            yb = [gb[hd] * cvv[hd] for hd in heads]
            ms = _group_mean_many([y * y for y in yb], g64m)
            rb, nb, dnb = [], [], []
            for hd in heads:
                rb.append(lax.rsqrt(ms[hd] + EPS))
                nb.append(yb[hd] * rb[hd])
                zb = p_ref[sl, col(3584, hd)]
                sgb = _sigmoid(zb)
                dmb = dm_ref[sl, col(512, hd)]
                gcv = gcn_ref[:, cs[hd]]
                part_ref[2:3, col(512, hd)] += rowsum(dmb * nb[hd] * (zb * sgb))
                dp_ref[sl, col(3584, hd)] = (dmb * nb[hd] * gcv * (sgb * (1.0 + zb * (1.0 - sgb)))).astype(BF16)
                dnb.append(dmb * gcv * (zb * sgb))
            mdn = _group_mean_many([dnb[hd] * nb[hd] for hd in heads], g64m)
            for hd in heads:
                dyb = rb[hd] * (dnb[hd] - nb[hd] * mdn[hd])
                dp_ref[sl, col(2560, hd)] = (dyb * cvv[hd]).astype(BF16)
                dcv = dyb * gb[hd]
                head = head_ref[:, cs[hd]]
                dcv1 = _shift_up(dcv, 1, head)
                dcv2 = _shift_up(dcv, 2, head)
                head_ref[:, cs[hd]] = dcv[0:8, :]
                u = p_ref[sl, col(2048, hd)]
                gc = p_ref[sl, col(3072, hd)]
                cu = gc * u
                part_ref[4:5, cs[hd]] += rowsum(dcv2 * cu)
                part_ref[5:6, cs[hd]] += rowsum(dcv1 * cu)
                part_ref[6:7, cs[hd]] += rowsum(dcv * cu)
                dcu = cw_ref[2:3, cs[hd]] * dcv + cw_ref[1:2, cs[hd]] * dcv1 + cw_ref[0:1, cs[hd]] * dcv2
                dp_ref[sl, col(3072, hd)] = (dcu * u).astype(BF16)
                dp_ref[sl, col(2048, hd)] = (dcu * gc).astype(BF16)
            do_b = []
            for hd in heads:
                ov = aux_ref[sl, col(AUX_O, hd)]
                ra = lax.rsqrt(jnp.mean(ov * ov, axis=-1, keepdims=True) + EPS)
                na = ov * ra
                za = p_ref[sl, col(1536, hd)]
                sga = _sigmoid(za)
                dma = dm_ref[sl, cs[hd]]
                gav = ga_ref[:, cs[hd]]
                part_ref[2:3, cs[hd]] += rowsum(dma * na * (za * sga))
                dp_ref[sl, col(1536, hd)] = (dma * na * gav * (sga * (1.0 + za * (1.0 - sga)))).astype(BF16)
                dna = dma * gav * (za * sga)
                do_b.append((ra * (dna - na * jnp.mean(dna * na, axis=-1, keepdims=True))).astype(BF16))
            s = [_sigmoid(p_ref[sl, col(512, hd)]) for hd in heads]
            f = [lb[:, cs[hd]] + (1.0 - lb[:, cs[hd]]) * s[hd] for hd in heads]
            bc = [aux_ref[sl, col(AUX_B, hd)] for hd in heads]
            g = [bc[hd][CHUNK - 1:CHUNK, :] for hd in heads]
            eb = [jnp.exp(bc[hd]) for hd in heads]
            enb = [jnp.exp(-bc[hd]) for hd in heads]
            eg = [jnp.exp(g[hd] - bc[hd]) for hd in heads]
            dec = [jnp.exp(g[hd]) for hd in heads]
            qd = [p_ref[sl, cs[hd]] * eb[hd] for hd in heads]
            ki = [(1.0 - f[hd]) * enb[hd] for hd in heads]
            ke = [(1.0 - f[hd]) * eg[hd] for hd in heads]
            qd_b = [a.astype(BF16) for a in qd]
            ki_b = [a.astype(BF16) for a in ki]
            ke_b = [a.astype(BF16) for a in ke]
            vb = [p_ref[sl, col(1024, hd)].astype(BF16) for hd in heads]
            st_b = [st_ref[n, hd] for hd in heads]
            dst = [dst_ref[hd] for hd in heads]
            dst_b = [a.astype(BF16) for a in dst]
            scm = [_dot_nt(qd_b[hd], ki_b[hd]) for hd in heads]
            amm = [_dot_nt(do_b[hd], vb[hd]) for hd in heads]
            dqd2 = [_dot(do_b[hd], st_b[hd]) for hd in heads]
            dke = [_dot(vb[hd], dst_b[hd]) for hd in heads]
            dv2 = [_dot_nt(ke_b[hd], dst_b[hd]) for hd in heads]
            dsu = [_dot_tn(do_b[hd], qd_b[hd]) for hd in heads]
            sc = [jnp.where(causal, scm[hd], 0.0).astype(BF16) for hd in heads]
            am = [jnp.where(causal, amm[hd], 0.0).astype(BF16) for hd in heads]
            dqd1 = [_dot(am[hd], ki_b[hd]) for hd in heads]
            dki = [_dot_tn(am[hd], qd_b[hd]) for hd in heads]
            dv1 = [_dot_tn(sc[hd], do_b[hd]) for hd in heads]
            db, dgv = [], []
            for hd in heads:
                dqd = dqd1[hd] + dqd2[hd]
                ddec = rowsum(dst[hd] * st_b[hd].astype(F32))
                dst_ref[hd] = dst[hd] * dec[hd] + dsu[hd]
                dp_ref[sl, cs[hd]] = (dqd * eb[hd]).astype(BF16)
                dp_ref[sl, col(1024, hd)] = (dv1[hd] + dv2[hd]).astype(BF16)
                db.append(dqd * qd[hd] - dki[hd] * ki[hd] - dke[hd] * ke[hd])
                dgv.append(rowsum(dke[hd] * ke[hd]) + ddec * dec[hd])
            rc = _exact_left_many(triu, db, 2)
            for hd in heads:
                df = (rc[hd] + dgv[hd]) / f[hd] - (dki[hd] * enb[hd] + dke[hd] * eg[hd])
                dlb_ref[:, cs[hd]] += rowsum(df * (1.0 - s[hd]))
                dp_ref[sl, col(512, hd)] = (df * (1.0 - lb[:, cs[hd]]) * s[hd] * (1.0 - s[hd])).astype(BF16)

        @pl.when(i == nblk - 1)
        def _():
            row = dlb_ref[...] * lb * (1.0 - lb)
            part_ref[3:4, 0:D_HGRN] = row
            part_ref[3:4, D_HGRN:] = -row

    rev = lambda w: pl.BlockSpec((TB, w), lambda i: (nblk - 1 - i, 0))
    row = lambda w: pl.BlockSpec((1, w), lambda i: (0, 0))
    return pl.pallas_call(
        body, name="mix_bwd", grid=(nblk,),
        out_shape=(jax.ShapeDtypeStruct((SEQ, 4096), BF16),
                   jax.ShapeDtypeStruct((8, D_MODEL), F32)),
        in_specs=[rev(4096), rev(AUX_COLS),
                  pl.BlockSpec((NCB, N_HEADS, HEAD, HEAD), lambda i: (nblk - 1 - i, 0, 0, 0)),
                  rev(D_MODEL),
                  pl.BlockSpec((2, D_HGRN), lambda i: (0, 0)),
                  pl.BlockSpec((8, D_CONV), lambda i: (0, 0)),
                  row(D_HGRN), row(D_CONV),
                  pl.BlockSpec((HEAD, HEAD), lambda i: (0, 0))],
        out_specs=(rev(4096), pl.BlockSpec((8, D_MODEL), lambda i: (0, 0))),
        scratch_shapes=[pltpu.VMEM((N_HEADS, HEAD, HEAD), F32), pltpu.VMEM((8, D_CONV), F32),
                        pltpu.VMEM((1, D_HGRN), F32)],
        compiler_params=pltpu.CompilerParams(dimension_semantics=("arbitrary",), vmem_limit_bytes=VMEM_LIMIT),
    )(proj, aux, states, dmixed, lb_logits, cw, ga, gcn, g64)


TT = 1024
TX = 256
(SEM_D2D, SEM_D2D_O, SEM_ICI, SEM_ICI_O, SEM_FIN, SEM_FIN_O, SEM_SMALL, N_SEM_TAIL) = 0, 4, 5, 8, 11, 12, 12, 20


def _bwd_tail(kidx, h, dproj, wg, gwo, x2d, dx2, g1, small_a, small_b):
    hw = D_MODEL // 2
    ho = WO_ROWS // 2
    nt = SEQ // TT
    n_steps = N_SHARD + SEQ // TX // nt

    def body(k_ref, h_ref, dp_ref, w_ref, gwo_ref, x_ref, dx2_ref, g_ref, sm_ref, smb_ref,
             gx_ref, gw_out, gwo_out, osm_ref,
             acc, dh, sendbuf, keep, sibrcv, rcv, sib_o, p_o, rcv_o, res_o, sm_buf, dng,
             send_sems, recv_sems, out_sems):
        s, t = pl.program_id(0), pl.program_id(1)
        x, y, c = lax.axis_index("x"), lax.axis_index("y"), lax.axis_index("c")
        k = 2 * x + y
        me = 4 * x + 2 * y + c
        sibling = (x, y, 1 - c)
        chips = [(1 - x, 1 - y), (1 - x, y), (x, 1 - y)]
        kjs = [2 * cx + cy for cx, cy in chips]
        mine = pl.ds(pl.multiple_of(c * hw, hw), hw)
        other = pl.ds(pl.multiple_of((1 - c) * hw, hw), hw)
        mine_o = pl.ds(pl.multiple_of(c * ho, ho), ho)
        other_o = pl.ds(pl.multiple_of((1 - c) * ho, ho), ho)

        def copy(sem, src, dst, to):
            return pltpu.make_async_remote_copy(
                src_ref=src, dst_ref=dst, send_sem=send_sems.at[sem], recv_sem=recv_sems.at[sem],
                device_id=to, device_id_type=MESH)

        def at_step(sv, tv):
            return pl.when((s == sv) & (t == tv))

        def at_norm_block(b):
            return at_step(N_SHARD + b // nt, b % nt)

        d2d = [copy(SEM_D2D + sv, sendbuf.at[sv], sibrcv.at[sv], sibling) for sv in range(N_SHARD)]
        d2d_o = copy(SEM_D2D_O, gwo_ref.at[:, other_o, :], sib_o, sibling)
        ici = [copy(SEM_ICI + sv, keep.at[sv], rcv.at[sv], (*chips[sv], c)) for sv in range(3)]
        ici_o = [copy(SEM_ICI_O + sv, p_o.at[kjs[sv]], rcv_o.at[sv], (*chips[sv], c)) for sv in range(3)]
        fin = copy(SEM_FIN, acc.at[mine, :], gw_out.at[mine, :], sibling)
        fin_o = copy(SEM_FIN_O, res_o.at[mine_o, :], res_o.at[mine_o, :], sibling)
        smalls = [copy(SEM_SMALL + m, sm_buf.at[me], sm_buf.at[me],
                       (x ^ (m >> 2), y ^ ((m >> 1) & 1), c ^ (m & 1))) for m in range(1, N_DEV)]
        store_w = pltpu.make_async_copy(acc.at[mine, :], gw_out.at[mine, :], out_sems.at[0])
        store_o = pltpu.make_async_copy(res_o, gwo_out, out_sems.at[1])

        @at_step(0, 0)
        def _():
            d2d_o.start()

        @at_step(0, 1)
        def _():
            d2d_o.wait_recv()
            for j in range(N_SHARD):
                p_o[j] = (gwo_ref[j, mine_o, :].astype(F32) + sib_o[j].astype(F32)).astype(BF16)
            res_o[mine_o, :] = gwo_ref[k, mine_o, :].astype(F32) + sib_o[k].astype(F32)
            for cp in ici_o:
                cp.start()

        rows = pl.ds(pl.multiple_of(t * TT, TT), TT)

        @pl.when(s < N_SHARD)
        def _():
            dpb = dp_ref[...]
            part = _dot_tn(h_ref[...], dpb)

            @pl.when(t == 0)
            def _():
                acc[...] = part

            @pl.when(t > 0)
            def _():
                acc[...] += part

            d = _dot_nt(dpb, w_ref[0])

            @pl.when(s == 0)
            def _():
                dh[rows, :] = d

            @pl.when(s > 0)
            def _():
                dh[rows, :] += d

        for sv in range(N_SHARD):
            @at_step(sv, nt - 1)
            def _(sv=sv):
                sendbuf[sv] = acc[other, :].astype(BF16)
                if sv < 3:
                    keep[sv] = acc[mine, :].astype(BF16)
                d2d[sv].start()

        for sv in range(3):
            @at_step(sv + 1, 0)
            def _(sv=sv):
                d2d[sv].wait_recv()
                keep[sv] = (keep[sv].astype(F32) + sibrcv[sv].astype(F32)).astype(BF16)
                ici[sv].start()

        @at_norm_block(0)
        def _():
            d2d[3].wait_recv()
            ici[0].wait_recv()
            acc[mine, :] += sibrcv[3].astype(F32) + rcv[0].astype(F32)

        @at_norm_block(1)
        def _():
            tot = res_o[mine_o, :]
            for sv in range(3):
                ici_o[sv].wait_recv()
                tot = tot + rcv_o[sv].astype(F32)
            res_o[mine_o, :] = tot
            fin_o.start()

        @at_norm_block(2)
        def _():
            ici[1].wait_recv()
            acc[mine, :] += rcv[1].astype(F32)

        @at_norm_block(0)
        def _():
            dng[...] = jnp.zeros_like(dng)

        @pl.when(s >= N_SHARD)
        def _():
            blk = (s - N_SHARD) * nt + t
            dhv = dh[pl.ds(pl.multiple_of(blk * TX, TX), TX), :]
            xv = x_ref[...]
            r = lax.rsqrt(jnp.mean(xv * xv, axis=-1, keepdims=True) + EPS)
            xn = xv * r
            dng[...] += jnp.sum(dhv * xn, axis=0, keepdims=True)
            dxn = dhv * g_ref[...]
            gx_ref[...] = dx2_ref[...] + r * (dxn - xn * jnp.mean(dxn * xn, axis=-1, keepdims=True))

        @at_step(n_steps - 1, nt - 1)
        def _():
            sm_buf[me] = sm_ref[...] + smb_ref[...]
            sm_buf[me, 0:1, :] = dng[...]
            for cp in smalls:
                cp.start()
            ici[2].wait_recv()
            acc[mine, :] += rcv[2].astype(F32)
            fin.start()
            store_w.start()
            for m in range(1, N_DEV):
                copy(SEM_SMALL + m, sm_buf.at[0], sm_buf.at[0], sibling).wait_recv()
            tot = sm_buf[0]
            for d in range(1, N_DEV):
                tot = tot + sm_buf[d]
            osm_ref[...] = tot
            fin_o.wait_recv()
            store_o.start()
            fin.wait_recv()
            for cp in d2d + [d2d_o] + ici + ici_o + [fin, fin_o] + smalls:
                cp.wait_send()
            store_o.wait()
            store_w.wait()

    def shard_of(s, kr):
        return kr[0] ^ (3 - jnp.minimum(s, 3))

    def tok(s, t):
        return jnp.where(s < N_SHARD, t, nt - 1)

    def blk_map(s, t, kr):
        return (jnp.where(s < N_SHARD, 0, (s - N_SHARD) * nt + t), 0)

    hbm = pl.BlockSpec(memory_space=pl.ANY)
    grid_spec = pltpu.PrefetchScalarGridSpec(
        num_scalar_prefetch=1, grid=(n_steps, nt),
        in_specs=[pl.BlockSpec((TT, D_MODEL), lambda s, t, kr: (tok(s, t), 0)),
                  pl.BlockSpec((TT, SHARD_COLS), lambda s, t, kr: (tok(s, t), shard_of(s, kr))),
                  pl.BlockSpec((1, D_MODEL, SHARD_COLS), lambda s, t, kr: (shard_of(s, kr), 0, 0)),
                  pl.BlockSpec((N_SHARD, WO_ROWS, D_MODEL), lambda s, t, kr: (0, 0, 0)),
                  pl.BlockSpec((TX, D_MODEL), blk_map),
                  pl.BlockSpec((TX, D_MODEL), blk_map),
                  pl.BlockSpec((1, D_MODEL), lambda s, t, kr: (0, 0)),
                  pl.BlockSpec((8, D_MODEL), lambda s, t, kr: (0, 0)),
                  pl.BlockSpec((8, D_MODEL), lambda s, t, kr: (0, 0))],
        out_specs=(pl.BlockSpec((TX, D_MODEL), blk_map), hbm, hbm,
                   pl.BlockSpec((8, D_MODEL), lambda s, t, kr: (0, 0))),
        scratch_shapes=[pltpu.VMEM((D_MODEL, SHARD_COLS), F32), pltpu.VMEM((SEQ, D_MODEL), F32),
                        pltpu.VMEM((N_SHARD, hw, SHARD_COLS), BF16), pltpu.VMEM((3, hw, SHARD_COLS), BF16),
                        pltpu.VMEM((N_SHARD, hw, SHARD_COLS), BF16), pltpu.VMEM((3, hw, SHARD_COLS), BF16),
                        pltpu.VMEM((N_SHARD, ho, D_MODEL), BF16), pltpu.VMEM((N_SHARD, ho, D_MODEL), BF16),
                        pltpu.VMEM((3, ho, D_MODEL), BF16), pltpu.VMEM((WO_ROWS, D_MODEL), F32),
                        pltpu.VMEM((N_DEV, 8, D_MODEL), F32), pltpu.VMEM((1, D_MODEL), F32),
                        pltpu.SemaphoreType.DMA((N_SEM_TAIL,)), pltpu.SemaphoreType.DMA((N_SEM_TAIL,)),
                        pltpu.SemaphoreType.DMA((2,))])
    return pl.pallas_call(
        body, name="bwd_tail", grid_spec=grid_spec,
        out_shape=(jax.ShapeDtypeStruct((SEQ, D_MODEL), F32),
                   jax.ShapeDtypeStruct((D_MODEL, SHARD_COLS), F32),
                   jax.ShapeDtypeStruct((WO_ROWS, D_MODEL), F32),
                   jax.ShapeDtypeStruct((8, D_MODEL), F32)),
        compiler_params=pltpu.CompilerParams(dimension_semantics=("arbitrary", "arbitrary"),
                                             vmem_limit_bytes=60 * 1024 * 1024),
    )(kidx, h, dproj, wg, gwo, x2d, dx2, g1, small_a, small_b)


def _adam_update(w, g, m, v):
    nm = ADAM_B1 * m + (1.0 - ADAM_B1) * g
    nv = ADAM_B2 * v + (1.0 - ADAM_B2) * (g * g)
    m_hat = nm / (1.0 - ADAM_B1 ** ADAM_STEP)
    v_hat = nv / (1.0 - ADAM_B2 ** ADAM_STEP)
    return -ADAM_LR * (m_hat / (jnp.sqrt(v_hat) + ADAM_EPS) + ADAM_WD * w), nm, nv


def _adamw_all(tot, g_w_in, g_w_out, big, small, grad_x):
    n = len(small)
    rows = WO_ROWS
    steps = D_MODEL // rows

    def body(tot_ref, *refs):
        gx_ref, gx_out = refs[2 + 3 * (2 + n)], refs[-1]
        gx_out[...] = gx_ref[...]
        ins, outs = refs[:2 + 3 * (2 + n)], refs[3 + 3 * (2 + n):-1]
        g_refs, wmv = ins[:2], ins[2:]
        loss_ref, quads = outs[0], outs[1:]

        def update(j, g):
            w_ref, m_ref, v_ref = wmv[3 * j:3 * j + 3]
            g_ref, d_ref, nm_ref, nv_ref = quads[4 * j:4 * j + 4]
            g_ref[...] = g
            d_ref[...], nm_ref[...], nv_ref[...] = _adam_update(w_ref[...], g, m_ref[...], v_ref[...])

        update(0, g_refs[0][...])

        @pl.when(pl.program_id(0) == 0)
        def _():
            update(1, g_refs[1][...])
            k = 2 * lax.axis_index("x") + lax.axis_index("y")
            mine = pl.ds(pl.multiple_of(k * HEAD, HEAD), HEAD)
            loss_ref[...] = tot_ref[7:8, 0:1]
            grads = [tot_ref[0:1, :], tot_ref[1:2, :], tot_ref[2:3, 0:D_HGRN], tot_ref[2:3, D_HGRN:],
                     jnp.concatenate([tot_ref[3:4, 0:D_HGRN], tot_ref[3:4, D_HGRN:]], axis=0),
                     jnp.concatenate([tot_ref[4 + tap:5 + tap, mine] for tap in range(3)], axis=1)]
            for j, g in enumerate(grads):
                update(2 + j, g)

    whole = lambda a: pl.BlockSpec(a.shape, lambda i: (0, 0))
    blk = pl.BlockSpec((rows, SHARD_COLS), lambda i: (i, 0))
    arrays = [a for triple in big + small for a in triple]
    in_specs = ([whole(tot), blk, whole(g_w_out)] + [blk] * 3 + [whole(a) for a in arrays[3:]])
    shapes = [big[0][0], big[1][0]] + [w for w, _, _ in small]
    out_shape = (jax.ShapeDtypeStruct((1, 1), F32),) + tuple(
        jax.ShapeDtypeStruct(w.shape, F32) for w in shapes for _ in range(4))
    out_specs = (pl.BlockSpec((1, 1), lambda i: (0, 0)),) + (blk,) * 4 + tuple(
        whole(w) for w in shapes[1:] for _ in range(4))
    gx_blk = pl.BlockSpec((SEQ // steps, D_MODEL), lambda i: (i, 0))
    outs = pl.pallas_call(
        body, name="adamw_all", grid=(steps,),
        out_shape=out_shape + (jax.ShapeDtypeStruct(grad_x.shape, F32),),
        in_specs=in_specs + [gx_blk], out_specs=out_specs + (gx_blk,),
        compiler_params=pltpu.CompilerParams(dimension_semantics=("arbitrary",), vmem_limit_bytes=VMEM_LIMIT),
    )(tot, g_w_in, g_w_out, *arrays, grad_x)
    return [outs[0]] + [outs[1 + 4 * j:5 + 4 * j] for j in range(2 + n)] + [outs[-1]]


def _local_step(x2d, tgt, proj, lb_logits, cw, ga, gcn, w_out, gf):
    g64 = _group_matrix(HEAD, CONV_GROUP)
    mixed, aux, states, wog = _mix_fwd(proj, lb_logits, cw, ga, gcn, g64, w_out)
    dx2, dmixed, gwo, part_out = _out_loss(x2d, mixed, wog.reshape(D_MODEL, D_MODEL), gf, tgt)
    dproj, part_mix = _mix_bwd(proj, aux, states, dmixed, lb_logits, cw, ga, gcn, g64)
    return dproj, dx2, gwo.reshape(N_SHARD, WO_ROWS, D_MODEL), part_out, part_mix


def kernel(x, norm_gain, w_in, lb_logits, conv_w, hgrn_norm_gain, conv_norm_gain, w_out, final_norm_gain, loss_target, m_norm_gain, m_w_in, m_lb_logits, m_conv_w, m_hgrn_norm_gain, m_conv_norm_gain, m_w_out, m_final_norm_gain, v_norm_gain, v_w_in, v_lb_logits, v_conv_w, v_hgrn_norm_gain, v_conv_norm_gain, v_w_out, v_final_norm_gain):
    k = 2 * lax.axis_index("x") + lax.axis_index("y")
    kidx = jnp.reshape(k, (1,)).astype(jnp.int32)
    row = lambda a: a.reshape(1, D_MODEL)
    taps = lambda a: a.reshape(1, 3 * HEAD)
    h, proj, wg, cw = _gather_proj(kidx, x[0], norm_gain, w_in, taps(conv_w))
    dproj, dx2, gwo, part_out, part_mix = _local_step(
        x[0], loss_target[0], proj, lb_logits, cw, hgrn_norm_gain, conv_norm_gain, w_out, row(final_norm_gain))
    rgrad_x, rg_w_in, rg_w_out, tot = _bwd_tail(kidx, h, dproj, wg, gwo, x[0], dx2, norm_gain, part_out, part_mix)

    (loss, (g_w_in, d_w_in, nm_w_in, nv_w_in), (g_w_out, d_w_out, nm_w_out, nv_w_out),
     (g_norm_gain, d_ng, nm_ng, nv_ng), (g_final, d_fg, nm_fg, nv_fg), (g_hgrn, d_hg, nm_hg, nv_hg),
     (g_convn, d_cg, nm_cg, nv_cg), (g_lb, d_lb, nm_lb, nv_lb), (g_conv_w, d_cw, nm_cw, nv_cw),
     grad_x) = _adamw_all(
        tot, rg_w_in, rg_w_out,
        [(w_in[0], m_w_in[0], v_w_in[0]), (w_out[0], m_w_out[0], v_w_out[0])],
        [(norm_gain, m_norm_gain, v_norm_gain),
         (row(final_norm_gain), row(m_final_norm_gain), row(v_final_norm_gain)),
         (hgrn_norm_gain, m_hgrn_norm_gain, v_hgrn_norm_gain),
         (conv_norm_gain, m_conv_norm_gain, v_conv_norm_gain),
         (lb_logits, m_lb_logits, v_lb_logits),
         (taps(conv_w), taps(m_conv_w), taps(v_conv_w))],
        rgrad_x)
    flat = lambda a: a.reshape(D_MODEL)
    untap = lambda a: a.reshape(1, 3, HEAD)
    return (loss.reshape(()), grad_x[None],
            g_norm_gain, g_w_in[None], g_lb, untap(g_conv_w), g_hgrn, g_convn, g_w_out[None], flat(g_final),
            d_ng, d_w_in[None], d_lb, untap(d_cw), d_hg, d_cg, d_w_out[None], flat(d_fg),
            nm_ng, nm_w_in[None], nm_lb, untap(nm_cw), nm_hg, nm_cg, nm_w_out[None], flat(nm_fg),
            nv_ng, nv_w_in[None], nv_lb, untap(nv_cw), nv_hg, nv_cg, nv_w_out[None], flat(nv_fg))
```

```python
import jax
import jax.numpy as jnp
import numpy as np
from jax import lax
from jax.experimental import pallas as pl
from jax.experimental.pallas import tpu as pltpu

F32 = jnp.float32
BF16 = jnp.bfloat16
MESH = pl.DeviceIdType.MESH

SEQ = 2048
D_MODEL = 1024
D_HGRN = 512
D_CONV = 512
HEAD = 128
N_HEADS = 4
CHUNK = 64
CONV_GROUP = 64
N_SHARD = 4
SHARD_COLS = 1024
WO_ROWS = 256
EPS = 1e-6
TB = 256
NCB = TB // CHUNK
N_CHUNKS = SEQ // CHUNK
N_DEV = 8
AUX_O, AUX_CV, AUX_B, AUX_COLS = 0, 512, 1024, 1536

ADAM_LR = 0.001
ADAM_B1 = 0.9
ADAM_B2 = 0.999
ADAM_EPS = 1e-08
ADAM_WD = 0.01
ADAM_STEP = 10

VMEM_LIMIT = 56 * 1024 * 1024


def _dot(a, b):
    return jnp.dot(a, b, preferred_element_type=F32)


def _dot_nt(a, b):
    return lax.dot_general(a, b, (((1,), (1,)), ((), ())), preferred_element_type=F32)


def _dot_tn(a, b):
    return lax.dot_general(a, b, (((0,), (0,)), ((), ())), preferred_element_type=F32)


def _split_bf16(x, n):
    parts = []
    r = x
    for _ in range(n):
        p = r.astype(BF16)
        parts.append(p)
        r = r - p.astype(F32)
    return parts


def _exact_left(m, x, n=3):
    acc = None
    for p in _split_bf16(x, n):
        t = _dot(m, p)
        acc = t if acc is None else acc + t
    return acc


def _exact_left_many(m, xs, n=3):
    parts = [_split_bf16(x, n) for x in xs]
    accs = [None] * len(xs)
    for i in range(n):
        for j in range(len(xs)):
            t = _dot(m, parts[j][i])
            accs[j] = t if accs[j] is None else accs[j] + t
    return accs


def _group_mean_many(xs, gmat, n=2):
    parts = [_split_bf16(x, n) for x in xs]
    accs = [None] * len(xs)
    for i in range(n):
        for j in range(len(xs)):
            t = _dot(parts[j][i], gmat)
            accs[j] = t if accs[j] is None else accs[j] + t
    return accs


def _group_mean(x, gmat, n=2):
    w = gmat.shape[0]
    outs = []
    for c0 in range(0, x.shape[1], w):
        acc = None
        for p in _split_bf16(x[:, c0:c0 + w], n):
            t = _dot(p, gmat)
            acc = t if acc is None else acc + t
        outs.append(acc)
    return jnp.concatenate(outs, axis=1)


def _sigmoid(x):
    return 1.0 / (1.0 + jnp.exp(-x))


def _lower_bound(lbl):
    l0 = lbl[0:1, :]
    l1 = lbl[1:2, :]
    m = jnp.maximum(l0, l1)
    e0 = jnp.exp(l0 - m)
    e1 = jnp.exp(l1 - m)
    return e0 / (e0 + e1)


def _tri(lower):
    r = lax.broadcasted_iota(jnp.int32, (CHUNK, CHUNK), 0)
    c = lax.broadcasted_iota(jnp.int32, (CHUNK, CHUNK), 1)
    return jnp.where((c <= r) if lower else (c >= r), 1.0, 0.0).astype(BF16)


def _causal():
    r = lax.broadcasted_iota(jnp.int32, (CHUNK, CHUNK), 0)
    c = lax.broadcasted_iota(jnp.int32, (CHUNK, CHUNK), 1)
    return c <= r


def _shift_down(x, sh, prev_tail):
    r = pltpu.roll(x, sh, 0)
    pt = pltpu.roll(prev_tail, sh, 0)
    rows = lax.broadcasted_iota(jnp.int32, prev_tail.shape, 0)
    top = jnp.where(rows < sh, pt, r[0:8])
    return jnp.concatenate([top, r[8:]], axis=0)


def _shift_up(x, sh, next_head):
    n = x.shape[0]
    r = pltpu.roll(x, n - sh, 0)
    nh = pltpu.roll(next_head, 8 - sh, 0)
    rows = lax.broadcasted_iota(jnp.int32, next_head.shape, 0)
    bot = jnp.where(rows >= 8 - sh, nh, r[n - 8:])
    return jnp.concatenate([r[:n - 8], bot], axis=0)


def _group_matrix(width, group):
    r = np.arange(width)[:, None] // group
    c = np.arange(width)[None, :] // group
    return jnp.asarray(np.where(r == c, 1.0 / group, 0.0), dtype=BF16)


TP = 512
TG = 2048
SEM_W, SEM_CW, SEM_W_FWD, N_SEM = 0, 4, 7, 11


def _gather_proj(kidx, x2d, g1, w_in, conv_w):
    half_w = D_MODEL // 2
    half_c = SHARD_COLS // 2
    nt = SEQ // TG
    n_steps = 2 * N_SHARD

    def body(k_ref, x_ref, g_ref, w_ref, cw_ref, h_ref, p_ref, wg_out, cwg_out,
             wg_v, cwg_v, send_sems, recv_sems, out_sems):
        s, t = pl.program_id(0), pl.program_id(1)
        x, y, c = lax.axis_index("x"), lax.axis_index("y"), lax.axis_index("c")
        k = 2 * x + y
        sibling = (x, y, 1 - c)
        chips = [(1 - x, y), (x, 1 - y), (1 - x, 1 - y)]
        kjs = [2 * cx + cy for cx, cy in chips]
        diag = (*chips[2], c)

        def w_half(kk, cc):
            return wg_v.at[kk, pl.ds(cc * half_w, half_w), :]

        def w_quarter(kk, cc, piece):
            return wg_v.at[kk, pl.ds(cc * half_w, half_w), piece * half_c:(piece + 1) * half_c]

        def cw_of(kk):
            return cwg_v.at[:, pl.ds(pl.multiple_of(kk * HEAD, HEAD), HEAD)]

        def copy(sem, ref, to):
            return pltpu.make_async_remote_copy(
                src_ref=ref, dst_ref=ref, send_sem=send_sems.at[sem], recv_sem=recv_sems.at[sem],
                device_id=to, device_id_type=MESH)

        def at_step(sv, tv):
            return pl.when((s == sv) & (t == tv))

        w_direct = ([copy(SEM_W + j, w_half(k, c), (*chips[j], c)) for j in range(2)]
                    + [copy(SEM_W + 2 + p, w_quarter(k, c, p), diag) for p in range(2)])
        cw_direct = [copy(SEM_CW + j, cw_of(k), (*chip, c)) for j, chip in enumerate(chips)]
        w_passed = ([copy(SEM_W_FWD + j, w_half(kjs[j], c), sibling) for j in range(2)]
                    + [copy(SEM_W_FWD + 2 + p, w_quarter(kjs[2], c, p), sibling) for p in range(2)])
        stores = ([pltpu.make_async_copy(wg_v.at[kk], wg_out.at[kk], out_sems.at[i])
                   for i, kk in enumerate([k] + kjs)]
                  + [pltpu.make_async_copy(cwg_v, cwg_out, out_sems.at[4])])

        @at_step(0, 0)
        def _():
            wg_v[k] = w_ref[0].astype(BF16)
            mine = pl.ds(pl.multiple_of(k * HEAD, HEAD), HEAD)
            cwg_v[:, mine] = jnp.zeros((8, HEAD), F32)
            for tap in range(3):
                cwg_v[tap:tap + 1, mine] = cw_ref[:, tap * HEAD:(tap + 1) * HEAD]
            w_direct[0].start()
            w_direct[1].start()
            for cp in cw_direct:
                cp.start()
            stores[0].start()

        @at_step(2, 0)
        def _():
            for j in range(2):
                copy(SEM_W + j, w_half(kjs[j], c), sibling).wait_recv()
                w_passed[j].start()
            w_direct[2].start()
            w_direct[3].start()
            copy(SEM_W_FWD, w_half(kjs[0], 1 - c), sibling).wait_recv()
            stores[1].start()

        @at_step(4, 0)
        def _():
            copy(SEM_W_FWD + 1, w_half(kjs[1], 1 - c), sibling).wait_recv()
            stores[2].start()

        for p in range(2):
            @at_step(6 + p, 0)
            def _(p=p):
                copy(SEM_W + 2 + p, w_quarter(kjs[2], c, p), sibling).wait_recv()
                w_passed[2 + p].start()
                copy(SEM_W_FWD + 2 + p, w_quarter(kjs[2], 1 - c, p), sibling).wait_recv()

        rows = pl.ds(pl.multiple_of(t * TG, TG), TG)

        @pl.when(s == 0)
        def _():
            for r0 in range(0, TG, TB):
                xv = x_ref[r0:r0 + TB, :]
                r = lax.rsqrt(jnp.mean(xv * xv, axis=-1, keepdims=True) + EPS)
                h_ref[pl.ds(pl.multiple_of(t * TG + r0, TB), TB), :] = (xv * r * g_ref[...]).astype(BF16)

        sh = s >> 1
        js = k ^ (((sh & 1) << 1) | (sh >> 1))
        for piece in range(2):
            @pl.when((s & 1) == piece)
            def _(piece=piece):
                p_ref[...] = _dot(h_ref[rows, :], wg_v[js, :, piece * half_c:(piece + 1) * half_c])

        @at_step(n_steps - 1, nt - 1)
        def _():
            stores[3].start()
            for j in range(3):
                copy(SEM_CW + j, cw_of(kjs[j]), sibling).wait_recv()
            stores[4].start()
            for cp in w_direct + cw_direct + w_passed:
                cp.wait_send()
            for st in stores:
                st.wait()

    def x_map(s, t, kr):
        return (jnp.where(s == 0, t, nt - 1), 0)

    def p_map(s, t, kr):
        sh = s >> 1
        return (t, 2 * (kr[0] ^ (((sh & 1) << 1) | (sh >> 1))) + (s & 1))

    hbm = pl.BlockSpec(memory_space=pl.ANY)
    grid_spec = pltpu.PrefetchScalarGridSpec(
        num_scalar_prefetch=1, grid=(n_steps, nt),
        in_specs=[pl.BlockSpec((TG, D_MODEL), x_map),
                  pl.BlockSpec((1, D_MODEL), lambda s, t, kr: (0, 0)),
                  pl.BlockSpec((1, D_MODEL, SHARD_COLS), lambda s, t, kr: (0, 0, 0)),
                  pl.BlockSpec((1, 3 * HEAD), lambda s, t, kr: (0, 0))],
        out_specs=(pl.BlockSpec((SEQ, D_MODEL), lambda s, t, kr: (0, 0)),
                   pl.BlockSpec((TG, half_c), p_map), hbm, hbm),
        scratch_shapes=[pltpu.VMEM((N_SHARD, D_MODEL, SHARD_COLS), BF16),
                        pltpu.VMEM((8, D_CONV), F32),
                        pltpu.SemaphoreType.DMA((N_SEM,)), pltpu.SemaphoreType.DMA((N_SEM,)),
                        pltpu.SemaphoreType.DMA((5,))])
    return pl.pallas_call(
        body, name="gather_proj", grid_spec=grid_spec,
        out_shape=(jax.ShapeDtypeStruct((SEQ, D_MODEL), BF16),
                   jax.ShapeDtypeStruct((SEQ, N_SHARD * SHARD_COLS), F32),
                   jax.ShapeDtypeStruct((N_SHARD, D_MODEL, SHARD_COLS), BF16),
                   jax.ShapeDtypeStruct((8, D_CONV), F32)),
        compiler_params=pltpu.CompilerParams(dimension_semantics=("arbitrary", "arbitrary"),
                                             vmem_limit_bytes=VMEM_LIMIT),
    )(kidx, x2d, g1, w_in, conv_w)


def _mix_fwd(proj, lb_logits, cw, ga, gcn, g64, w_out):
    half_o = WO_ROWS // 2
    nblk = SEQ // TB

    def body(p_ref, lbl_ref, cw_ref, ga_ref, gcn_ref, g64_ref, wo_ref,
             mixed_ref, aux_ref, sto_ref, wog_out,
             st_ref, tail_ref, wog_v, send_sems, recv_sems, out_sem):
        i = pl.program_id(0)
        x, y, c = lax.axis_index("x"), lax.axis_index("y"), lax.axis_index("c")
        k = 2 * x + y
        sibling = (x, y, 1 - c)
        chips = [(1 - x, y), (x, 1 - y), (1 - x, 1 - y)]
        kjs = [2 * cx + cy for cx, cy in chips]

        def wo_half(kk, cc):
            return wog_v.at[kk, pl.ds(cc * half_o, half_o), :]

        def copy(sem, ref, to):
            return pltpu.make_async_remote_copy(
                src_ref=ref, dst_ref=ref, send_sem=send_sems.at[sem], recv_sem=recv_sems.at[sem],
                device_id=to, device_id_type=MESH)

        wo_direct = [copy(j, wo_half(k, c), (*chip, c)) for j, chip in enumerate(chips)]
        wo_passed = [copy(3 + j, wo_half(kj, c), sibling) for j, kj in enumerate(kjs)]
        wo_store = pltpu.make_async_copy(wog_v, wog_out, out_sem.at[0])

        @pl.when(i == 0)
        def _():
            st_ref[...] = jnp.zeros_like(st_ref)
            tail_ref[...] = jnp.zeros_like(tail_ref)
            wog_v[k] = wo_ref[0].astype(BF16)
            for cp in wo_direct:
                cp.start()

        @pl.when(i == nblk - 2)
        def _():
            for j in range(3):
                copy(j, wo_half(kjs[j], c), sibling).wait_recv()
                wo_passed[j].start()

        lb = _lower_bound(lbl_ref[...])
        tri = _tri(True)
        causal = _causal()
        g64m = g64_ref[...]
        heads = range(N_HEADS)
        cs = [slice(hd * HEAD, (hd + 1) * HEAD) for hd in heads]
        col = lambda base, hd: slice(base + hd * HEAD, base + (hd + 1) * HEAD)
        for n in range(NCB):
            sl = pl.ds(n * CHUNK, CHUNK)
            sg = [_sigmoid(p_ref[sl, col(512, hd)]) for hd in heads]
            f = [lb[:, cs[hd]] + (1.0 - lb[:, cs[hd]]) * sg[hd] for hd in heads]
            bc = _exact_left_many(tri, [jnp.log(f[hd]) for hd in heads])
            for hd in heads:
                aux_ref[sl, col(AUX_B, hd)] = bc[hd]
            g = [bc[hd][CHUNK - 1:CHUNK, :] for hd in heads]
            qd = [(p_ref[sl, col(0, hd)] * jnp.exp(bc[hd])).astype(BF16) for hd in heads]
            ki = [((1.0 - f[hd]) * jnp.exp(-bc[hd])).astype(BF16) for hd in heads]
            ke = [((1.0 - f[hd]) * jnp.exp(g[hd] - bc[hd])).astype(BF16) for hd in heads]
            vb = [p_ref[sl, col(1024, hd)].astype(BF16) for hd in heads]
            st = [st_ref[hd] for hd in heads]
            st_b = [a.astype(BF16) for a in st]
            for hd in heads:
                sto_ref[n, hd] = st_b[hd]
            scm = [_dot_nt(qd[hd], ki[hd]) for hd in heads]
            inter = [_dot_nt(qd[hd], st_b[hd]) for hd in heads]
            upd = [_dot_tn(vb[hd], ke[hd]) for hd in heads]
            intra = [_dot(jnp.where(causal, scm[hd], 0.0).astype(BF16), vb[hd]) for hd in heads]
            for hd in heads:
                st_ref[hd] = st[hd] * jnp.exp(g[hd]) + upd[hd]
                o = intra[hd] + inter[hd]
                aux_ref[sl, col(AUX_O, hd)] = o
                ra = lax.rsqrt(jnp.mean(o * o, axis=-1, keepdims=True) + EPS)
                za = p_ref[sl, col(1536, hd)]
                mixed_ref[sl, cs[hd]] = (o * ra * ga_ref[:, cs[hd]] * (za * _sigmoid(za))).astype(BF16)
            yb = []
            for hd in heads:
                cu = p_ref[sl, col(3072, hd)] * p_ref[sl, col(2048, hd)]
                tail = tail_ref[:, cs[hd]]
                cv = (cw_ref[0:1, cs[hd]] * _shift_down(cu, 2, tail) + cw_ref[1:2, cs[hd]] * _shift_down(cu, 1, tail)
                      + cw_ref[2:3, cs[hd]] * cu)
                tail_ref[:, cs[hd]] = cu[CHUNK - 8:, :]
                aux_ref[sl, col(AUX_CV, hd)] = cv
                yb.append(p_ref[sl, col(2560, hd)] * cv)
            ms = _group_mean_many([y * y for y in yb], g64m)
            for hd in heads:
                rb = lax.rsqrt(ms[hd] + EPS)
                zb = p_ref[sl, col(3584, hd)]
                mixed_ref[sl, col(512, hd)] = (yb[hd] * rb * gcn_ref[:, cs[hd]] * (zb * _sigmoid(zb))).astype(BF16)

        @pl.when(i == nblk - 1)
        def _():
            for j in range(3):
                copy(3 + j, wo_half(kjs[j], 1 - c), sibling).wait_recv()
            wo_store.start()
            for cp in wo_direct + wo_passed:
                cp.wait_send()
            wo_store.wait()

    row = lambda w: pl.BlockSpec((1, w), lambda i: (0, 0))
    return pl.pallas_call(
        body, name="mix_fwd", grid=(nblk,),
        out_shape=(jax.ShapeDtypeStruct((SEQ, D_MODEL), BF16),
                   jax.ShapeDtypeStruct((SEQ, AUX_COLS), F32),
                   jax.ShapeDtypeStruct((N_CHUNKS, N_HEADS, HEAD, HEAD), BF16),
                   jax.ShapeDtypeStruct((N_SHARD, WO_ROWS, D_MODEL), BF16)),
        in_specs=[pl.BlockSpec((TB, 4096), lambda i: (i, 0)),
                  pl.BlockSpec((2, D_HGRN), lambda i: (0, 0)),
                  pl.BlockSpec((8, D_CONV), lambda i: (0, 0)),
                  row(D_HGRN), row(D_CONV),
                  pl.BlockSpec((HEAD, HEAD), lambda i: (0, 0)),
                  pl.BlockSpec((1, WO_ROWS, D_MODEL), lambda i: (0, 0, 0))],
        out_specs=(pl.BlockSpec((TB, D_MODEL), lambda i: (i, 0)),
                   pl.BlockSpec((TB, AUX_COLS), lambda i: (i, 0)),
                   pl.BlockSpec((NCB, N_HEADS, HEAD, HEAD), lambda i: (i, 0, 0, 0)),
                   pl.BlockSpec(memory_space=pl.ANY)),
        scratch_shapes=[pltpu.VMEM((N_HEADS, HEAD, HEAD), F32), pltpu.VMEM((8, D_CONV), F32),
                        pltpu.VMEM((N_SHARD, WO_ROWS, D_MODEL), BF16),
                        pltpu.SemaphoreType.DMA((6,)), pltpu.SemaphoreType.DMA((6,)),
                        pltpu.SemaphoreType.DMA((1,))],
        compiler_params=pltpu.CompilerParams(dimension_semantics=("arbitrary",), vmem_limit_bytes=VMEM_LIMIT),
    )(proj, lb_logits, cw, ga, gcn, g64, w_out)


def _out_loss(x2d, mixed, wog, gf, tgt):
    def body(x_ref, m_ref, wo_ref, gf_ref, t_ref, dx2_ref, dm_ref, gwo_ref, part_ref, acc_ref):
        i = pl.program_id(0)

        @pl.when(i == 0)
        def _():
            acc_ref[...] = jnp.zeros_like(acc_ref)
            part_ref[...] = jnp.zeros_like(part_ref)

        mixed_b = m_ref[...]
        x2 = x_ref[...] + _dot(mixed_b, wo_ref[...])
        r2 = lax.rsqrt(jnp.mean(x2 * x2, axis=-1, keepdims=True) + EPS)
        n2 = x2 * r2
        gfv = gf_ref[...]
        err = n2 * gfv - t_ref[...]
        loss = 0.5 * jnp.sum(jnp.mean(err * err, axis=-1, keepdims=True), axis=0, keepdims=True)
        dy = err * (1.0 / D_MODEL)
        part_ref[1:2, :] += jnp.sum(dy * n2, axis=0, keepdims=True)
        part_ref[7:8, :] += jnp.broadcast_to(loss, (1, D_MODEL))
        dn = dy * gfv
        dx2 = r2 * (dn - n2 * jnp.mean(dn * n2, axis=-1, keepdims=True))
        dx2_ref[...] = dx2
        dx2_b = dx2.astype(BF16)
        dm_ref[...] = _dot_nt(dx2_b, wo_ref[...])
        acc_ref[...] += _dot_tn(mixed_b, dx2_b)

        @pl.when(i == pl.num_programs(0) - 1)
        def _():
            gwo_ref[...] = acc_ref[...].astype(BF16)

    blk = lambda: pl.BlockSpec((TP, D_MODEL), lambda i: (i, 0))
    return pl.pallas_call(
        body, name="out_loss", grid=(SEQ // TP,),
        out_shape=(jax.ShapeDtypeStruct((SEQ, D_MODEL), F32),
                   jax.ShapeDtypeStruct((SEQ, D_MODEL), F32),
                   jax.ShapeDtypeStruct((D_MODEL, D_MODEL), BF16),
                   jax.ShapeDtypeStruct((8, D_MODEL), F32)),
        in_specs=[blk(), blk(), pl.BlockSpec((D_MODEL, D_MODEL), lambda i: (0, 0)),
                  pl.BlockSpec((1, D_MODEL), lambda i: (0, 0)), blk()],
        out_specs=(blk(), blk(), pl.BlockSpec((D_MODEL, D_MODEL), lambda i: (0, 0)),
                   pl.BlockSpec((8, D_MODEL), lambda i: (0, 0))),
        scratch_shapes=[pltpu.VMEM((D_MODEL, D_MODEL), F32)],
        compiler_params=pltpu.CompilerParams(dimension_semantics=("arbitrary",), vmem_limit_bytes=VMEM_LIMIT),
    )(x2d, mixed, wog, gf, tgt)


def _mix_bwd(proj, aux, states, dmixed, lb_logits, cw, ga, gcn, g64):
    nblk = SEQ // TB

    def body(p_ref, aux_ref, st_ref, dm_ref, lbl_ref, cw_ref, ga_ref, gcn_ref, g64_ref,
             dp_ref, part_ref, dst_ref, head_ref, dlb_ref):
        i = pl.program_id(0)

        @pl.when(i == 0)
        def _():
            dst_ref[...] = jnp.zeros_like(dst_ref)
            head_ref[...] = jnp.zeros_like(head_ref)
            part_ref[...] = jnp.zeros_like(part_ref)
            dlb_ref[...] = jnp.zeros_like(dlb_ref)

        lb = _lower_bound(lbl_ref[...])
        triu = _tri(False)
        causal = _causal()
        g64m = g64_ref[...]
        rowsum = lambda a: jnp.sum(a, axis=0, keepdims=True)
        heads = range(N_HEADS)
        cs = [slice(hd * HEAD, (hd + 1) * HEAD) for hd in heads]
        col = lambda base, hd: slice(base + hd * HEAD, base + (hd + 1) * HEAD)
        for n in reversed(range(NCB)):
            sl = pl.ds(n * CHUNK, CHUNK)
            cvv = [aux_ref[sl, col(AUX_CV, hd)] for hd in heads]
            gb = [p_ref[sl, col(2560, hd)] for hd in heads]
            yb = [gb[hd] * cvv[hd] for hd in heads]
            ms = _group_mean_many([y * y for y in yb], g64m)
            rb, nb, dnb = [], [], []
            for hd in heads:
                rb.append(lax.rsqrt(ms[hd] + EPS))
                nb.append(yb[hd] * rb[hd])
                zb = p_ref[sl, col(3584, hd)]
                sgb = _sigmoid(zb)
                dmb = dm_ref[sl, col(512, hd)]
                gcv = gcn_ref[:, cs[hd]]
                part_ref[2:3, col(512, hd)] += rowsum(dmb * nb[hd] * (zb * sgb))
                dp_ref[sl, col(3584, hd)] = (dmb * nb[hd] * gcv * (sgb * (1.0 + zb * (1.0 - sgb)))).astype(BF16)
                dnb.append(dmb * gcv * (zb * sgb))
            mdn = _group_mean_many([dnb[hd] * nb[hd] for hd in heads], g64m)
            for hd in heads:
                dyb = rb[hd] * (dnb[hd] - nb[hd] * mdn[hd])
                dp_ref[sl, col(2560, hd)] = (dyb * cvv[hd]).astype(BF16)
                dcv = dyb * gb[hd]
                head = head_ref[:, cs[hd]]
                dcv1 = _shift_up(dcv, 1, head)
                dcv2 = _shift_up(dcv, 2, head)
                head_ref[:, cs[hd]] = dcv[0:8, :]
                u = p_ref[sl, col(2048, hd)]
                gc = p_ref[sl, col(3072, hd)]
                cu = gc * u
                part_ref[4:5, cs[hd]] += rowsum(dcv2 * cu)
                part_ref[5:6, cs[hd]] += rowsum(dcv1 * cu)
                part_ref[6:7, cs[hd]] += rowsum(dcv * cu)
                dcu = cw_ref[2:3, cs[hd]] * dcv + cw_ref[1:2, cs[hd]] * dcv1 + cw_ref[0:1, cs[hd]] * dcv2
                dp_ref[sl, col(3072, hd)] = (dcu * u).astype(BF16)
                dp_ref[sl, col(2048, hd)] = (dcu * gc).astype(BF16)
            do_b = []
            for hd in heads:
                ov = aux_ref[sl, col(AUX_O, hd)]
                ra = lax.rsqrt(jnp.mean(ov * ov, axis=-1, keepdims=True) + EPS)
                na = ov * ra
                za = p_ref[sl, col(1536, hd)]
                sga = _sigmoid(za)
                dma = dm_ref[sl, cs[hd]]
                gav = ga_ref[:, cs[hd]]
                part_ref[2:3, cs[hd]] += rowsum(dma * na * (za * sga))
                dp_ref[sl, col(1536, hd)] = (dma * na * gav * (sga * (1.0 + za * (1.0 - sga)))).astype(BF16)
                dna = dma * gav * (za * sga)
                do_b.append((ra * (dna - na * jnp.mean(dna * na, axis=-1, keepdims=True))).astype(BF16))
            s = [_sigmoid(p_ref[sl, col(512, hd)]) for hd in heads]
            f = [lb[:, cs[hd]] + (1.0 - lb[:, cs[hd]]) * s[hd] for hd in heads]
            bc = [aux_ref[sl, col(AUX_B, hd)] for hd in heads]
            g = [bc[hd][CHUNK - 1:CHUNK, :] for hd in heads]
            eb = [jnp.exp(bc[hd]) for hd in heads]
            enb = [jnp.exp(-bc[hd]) for hd in heads]
            eg = [jnp.exp(g[hd] - bc[hd]) for hd in heads]
            dec = [jnp.exp(g[hd]) for hd in heads]
            qd = [p_ref[sl, cs[hd]] * eb[hd] for hd in heads]
            ki = [(1.0 - f[hd]) * enb[hd] for hd in heads]
            ke = [(1.0 - f[hd]) * eg[hd] for hd in heads]
            qd_b = [a.astype(BF16) for a in qd]
            ki_b = [a.astype(BF16) for a in ki]
            ke_b = [a.astype(BF16) for a in ke]
            vb = [p_ref[sl, col(1024, hd)].astype(BF16) for hd in heads]
            st_b = [st_ref[n, hd] for hd in heads]
            dst = [dst_ref[hd] for hd in heads]
            dst_b = [a.astype(BF16) for a in dst]
            scm = [_dot_nt(qd_b[hd], ki_b[hd]) for hd in heads]
            amm = [_dot_nt(do_b[hd], vb[hd]) for hd in heads]
            dqd2 = [_dot(do_b[hd], st_b[hd]) for hd in heads]
            dke = [_dot(vb[hd], dst_b[hd]) for hd in heads]
            dv2 = [_dot_nt(ke_b[hd], dst_b[hd]) for hd in heads]
            dsu = [_dot_tn(do_b[hd], qd_b[hd]) for hd in heads]
            sc = [jnp.where(causal, scm[hd], 0.0).astype(BF16) for hd in heads]
            am = [jnp.where(causal, amm[hd], 0.0).astype(BF16) for hd in heads]
            dqd1 = [_dot(am[hd], ki_b[hd]) for hd in heads]
            dki = [_dot_tn(am[hd], qd_b[hd]) for hd in heads]
            dv1 = [_dot_tn(sc[hd], do_b[hd]) for hd in heads]
            db, dgv = [], []
            for hd in heads:
                dqd = dqd1[hd] + dqd2[hd]
                ddec = rowsum(dst[hd] * st_b[hd].astype(F32))
                dst_ref[hd] = dst[hd] * dec[hd] + dsu[hd]
                dp_ref[sl, cs[hd]] = (dqd * eb[hd]).astype(BF16)
                dp_ref[sl, col(1024, hd)] = (dv1[hd] + dv2[hd]).astype(BF16)
                db.append(dqd * qd[hd] - dki[hd] * ki[hd] - dke[hd] * ke[hd])
                dgv.append(rowsum(dke[hd] * ke[hd]) + ddec * dec[hd])
            rc = _exact_left_many(triu, db, 2)
            for hd in heads:
                df = (rc[hd] + dgv[hd]) / f[hd] - (dki[hd] * enb[hd] + dke[hd] * eg[hd])
                dlb_ref[:, cs[hd]] += rowsum(df * (1.0 - s[hd]))
                dp_ref[sl, col(512, hd)] = (df * (1.0 - lb[:, cs[hd]]) * s[hd] * (1.0 - s[hd])).astype(BF16)

        @pl.when(i == nblk - 1)
        def _():
            row = dlb_ref[...] * lb * (1.0 - lb)
            part_ref[3:4, 0:D_HGRN] = row
            part_ref[3:4, D_HGRN:] = -row

    rev = lambda w: pl.BlockSpec((TB, w), lambda i: (nblk - 1 - i, 0))
    row = lambda w: pl.BlockSpec((1, w), lambda i: (0, 0))
    return pl.pallas_call(
        body, name="mix_bwd", grid=(nblk,),
        out_shape=(jax.ShapeDtypeStruct((SEQ, 4096), BF16),
                   jax.ShapeDtypeStruct((8, D_MODEL), F32)),
        in_specs=[rev(4096), rev(AUX_COLS),
                  pl.BlockSpec((NCB, N_HEADS, HEAD, HEAD), lambda i: (nblk - 1 - i, 0, 0, 0)),
                  rev(D_MODEL),
                  pl.BlockSpec((2, D_HGRN), lambda i: (0, 0)),
                  pl.BlockSpec((8, D_CONV), lambda i: (0, 0)),
                  row(D_HGRN), row(D_CONV),
                  pl.BlockSpec((HEAD, HEAD), lambda i: (0, 0))],
        out_specs=(rev(4096), pl.BlockSpec((8, D_MODEL), lambda i: (0, 0))),
        scratch_shapes=[pltpu.VMEM((N_HEADS, HEAD, HEAD), F32), pltpu.VMEM((8, D_CONV), F32),
                        pltpu.VMEM((1, D_HGRN), F32)],
        compiler_params=pltpu.CompilerParams(dimension_semantics=("arbitrary",), vmem_limit_bytes=VMEM_LIMIT),
    )(proj, aux, states, dmixed, lb_logits, cw, ga, gcn, g64)


TT = 1024
TX = 256
(SEM_D2D, SEM_D2D_O, SEM_ICI, SEM_ICI_O, SEM_FIN, SEM_FIN_O, SEM_SMALL, N_SEM_TAIL) = 0, 4, 5, 8, 11, 12, 12, 20


def _bwd_tail(kidx, h, dproj, wg, gwo, x2d, dx2, g1, small_a, small_b):
    hw = D_MODEL // 2
    ho = WO_ROWS // 2
    nt = SEQ // TT
    n_steps = N_SHARD + SEQ // TX // nt

    def body(k_ref, h_ref, dp_ref, w_ref, gwo_ref, x_ref, dx2_ref, g_ref, sm_ref, smb_ref,
             gx_ref, gw_out, gwo_out, osm_ref,
             acc, dh, sendbuf, keep, sibrcv, rcv, sib_o, p_o, rcv_o, res_o, sm_buf, dng,
             send_sems, recv_sems, out_sems):
        s, t = pl.program_id(0), pl.program_id(1)
        x, y, c = lax.axis_index("x"), lax.axis_index("y"), lax.axis_index("c")
        k = 2 * x + y
        me = 4 * x + 2 * y + c
        sibling = (x, y, 1 - c)
        chips = [(1 - x, 1 - y), (1 - x, y), (x, 1 - y)]
        kjs = [2 * cx + cy for cx, cy in chips]
        mine = pl.ds(pl.multiple_of(c * hw, hw), hw)
        other = pl.ds(pl.multiple_of((1 - c) * hw, hw), hw)
        mine_o = pl.ds(pl.multiple_of(c * ho, ho), ho)
        other_o = pl.ds(pl.multiple_of((1 - c) * ho, ho), ho)

        def copy(sem, src, dst, to):
            return pltpu.make_async_remote_copy(
                src_ref=src, dst_ref=dst, send_sem=send_sems.at[sem], recv_sem=recv_sems.at[sem],
                device_id=to, device_id_type=MESH)

        def at_step(sv, tv):
            return pl.when((s == sv) & (t == tv))

        def at_norm_block(b):
            return at_step(N_SHARD + b // nt, b % nt)

        d2d = [copy(SEM_D2D + sv, sendbuf.at[sv], sibrcv.at[sv], sibling) for sv in range(N_SHARD)]
        d2d_o = copy(SEM_D2D_O, gwo_ref.at[:, other_o, :], sib_o, sibling)
        ici = [copy(SEM_ICI + sv, keep.at[sv], rcv.at[sv], (*chips[sv], c)) for sv in range(3)]
        ici_o = [copy(SEM_ICI_O + sv, p_o.at[kjs[sv]], rcv_o.at[sv], (*chips[sv], c)) for sv in range(3)]
        fin = copy(SEM_FIN, acc.at[mine, :], gw_out.at[mine, :], sibling)
        fin_o = copy(SEM_FIN_O, res_o.at[mine_o, :], res_o.at[mine_o, :], sibling)
        smalls = [copy(SEM_SMALL + m, sm_buf.at[me], sm_buf.at[me],
                       (x ^ (m >> 2), y ^ ((m >> 1) & 1), c ^ (m & 1))) for m in range(1, N_DEV)]
        store_w = pltpu.make_async_copy(acc.at[mine, :], gw_out.at[mine, :], out_sems.at[0])
        store_o = pltpu.make_async_copy(res_o, gwo_out, out_sems.at[1])

        @at_step(0, 0)
        def _():
            d2d_o.start()

        @at_step(0, 1)
        def _():
            d2d_o.wait_recv()
            for j in range(N_SHARD):
                p_o[j] = (gwo_ref[j, mine_o, :].astype(F32) + sib_o[j].astype(F32)).astype(BF16)
            res_o[mine_o, :] = gwo_ref[k, mine_o, :].astype(F32) + sib_o[k].astype(F32)
            for cp in ici_o:
                cp.start()

        rows = pl.ds(pl.multiple_of(t * TT, TT), TT)

        @pl.when(s < N_SHARD)
        def _():
            dpb = dp_ref[...]
            part = _dot_tn(h_ref[...], dpb)

            @pl.when(t == 0)
            def _():
                acc[...] = part

            @pl.when(t > 0)
            def _():
                acc[...] += part

            d = _dot_nt(dpb, w_ref[0])

            @pl.when(s == 0)
            def _():
                dh[rows, :] = d

            @pl.when(s > 0)
            def _():
                dh[rows, :] += d

        for sv in range(N_SHARD):
            @at_step(sv, nt - 1)
            def _(sv=sv):
                sendbuf[sv] = acc[other, :].astype(BF16)
                if sv < 3:
                    keep[sv] = acc[mine, :].astype(BF16)
                d2d[sv].start()

        for sv in range(3):
            @at_step(sv + 1, 0)
            def _(sv=sv):
                d2d[sv].wait_recv()
                keep[sv] = (keep[sv].astype(F32) + sibrcv[sv].astype(F32)).astype(BF16)
                ici[sv].start()

        @at_norm_block(0)
        def _():
            d2d[3].wait_recv()
            ici[0].wait_recv()
            acc[mine, :] += sibrcv[3].astype(F32) + rcv[0].astype(F32)

        @at_norm_block(1)
        def _():
            tot = res_o[mine_o, :]
            for sv in range(3):
                ici_o[sv].wait_recv()
                tot = tot + rcv_o[sv].astype(F32)
            res_o[mine_o, :] = tot
            fin_o.start()

        @at_norm_block(2)
        def _():
            ici[1].wait_recv()
            acc[mine, :] += rcv[1].astype(F32)

        @at_norm_block(0)
        def _():
            dng[...] = jnp.zeros_like(dng)

        @pl.when(s >= N_SHARD)
        def _():
            blk = (s - N_SHARD) * nt + t
            dhv = dh[pl.ds(pl.multiple_of(blk * TX, TX), TX), :]
            xv = x_ref[...]
            r = lax.rsqrt(jnp.mean(xv * xv, axis=-1, keepdims=True) + EPS)
            xn = xv * r
            dng[...] += jnp.sum(dhv * xn, axis=0, keepdims=True)
            dxn = dhv * g_ref[...]
            gx_ref[...] = dx2_ref[...] + r * (dxn - xn * jnp.mean(dxn * xn, axis=-1, keepdims=True))

        @at_step(n_steps - 1, nt - 1)
        def _():
            sm_buf[me] = sm_ref[...] + smb_ref[...]
            sm_buf[me, 0:1, :] = dng[...]
            for cp in smalls:
                cp.start()
            ici[2].wait_recv()
            acc[mine, :] += rcv[2].astype(F32)
            fin.start()
            store_w.start()
            for m in range(1, N_DEV):
                copy(SEM_SMALL + m, sm_buf.at[0], sm_buf.at[0], sibling).wait_recv()
            tot = sm_buf[0]
            for d in range(1, N_DEV):
                tot = tot + sm_buf[d]
            osm_ref[...] = tot
            fin_o.wait_recv()
            store_o.start()
            fin.wait_recv()
            for cp in d2d + [d2d_o] + ici + ici_o + [fin, fin_o] + smalls:
                cp.wait_send()
            store_o.wait()
            store_w.wait()

    def shard_of(s, kr):
        return kr[0] ^ (3 - jnp.minimum(s, 3))

    def tok(s, t):
        return jnp.where(s < N_SHARD, t, nt - 1)

    def blk_map(s, t, kr):
        return (jnp.where(s < N_SHARD, 0, (s - N_SHARD) * nt + t), 0)

    hbm = pl.BlockSpec(memory_space=pl.ANY)
    grid_spec = pltpu.PrefetchScalarGridSpec(
        num_scalar_prefetch=1, grid=(n_steps, nt),
        in_specs=[pl.BlockSpec((TT, D_MODEL), lambda s, t, kr: (tok(s, t), 0)),
                  pl.BlockSpec((TT, SHARD_COLS), lambda s, t, kr: (tok(s, t), shard_of(s, kr))),
                  pl.BlockSpec((1, D_MODEL, SHARD_COLS), lambda s, t, kr: (shard_of(s, kr), 0, 0)),
                  pl.BlockSpec((N_SHARD, WO_ROWS, D_MODEL), lambda s, t, kr: (0, 0, 0)),
                  pl.BlockSpec((TX, D_MODEL), blk_map),
                  pl.BlockSpec((TX, D_MODEL), blk_map),
                  pl.BlockSpec((1, D_MODEL), lambda s, t, kr: (0, 0)),
                  pl.BlockSpec((8, D_MODEL), lambda s, t, kr: (0, 0)),
                  pl.BlockSpec((8, D_MODEL), lambda s, t, kr: (0, 0))],
        out_specs=(pl.BlockSpec((TX, D_MODEL), blk_map), hbm, hbm,
                   pl.BlockSpec((8, D_MODEL), lambda s, t, kr: (0, 0))),
        scratch_shapes=[pltpu.VMEM((D_MODEL, SHARD_COLS), F32), pltpu.VMEM((SEQ, D_MODEL), F32),
                        pltpu.VMEM((N_SHARD, hw, SHARD_COLS), BF16), pltpu.VMEM((3, hw, SHARD_COLS), BF16),
                        pltpu.VMEM((N_SHARD, hw, SHARD_COLS), BF16), pltpu.VMEM((3, hw, SHARD_COLS), BF16),
                        pltpu.VMEM((N_SHARD, ho, D_MODEL), BF16), pltpu.VMEM((N_SHARD, ho, D_MODEL), BF16),
                        pltpu.VMEM((3, ho, D_MODEL), BF16), pltpu.VMEM((WO_ROWS, D_MODEL), F32),
                        pltpu.VMEM((N_DEV, 8, D_MODEL), F32), pltpu.VMEM((1, D_MODEL), F32),
                        pltpu.SemaphoreType.DMA((N_SEM_TAIL,)), pltpu.SemaphoreType.DMA((N_SEM_TAIL,)),
                        pltpu.SemaphoreType.DMA((2,))])
    return pl.pallas_call(
        body, name="bwd_tail", grid_spec=grid_spec,
        out_shape=(jax.ShapeDtypeStruct((SEQ, D_MODEL), F32),
                   jax.ShapeDtypeStruct((D_MODEL, SHARD_COLS), F32),
                   jax.ShapeDtypeStruct((WO_ROWS, D_MODEL), F32),
                   jax.ShapeDtypeStruct((8, D_MODEL), F32)),
        compiler_params=pltpu.CompilerParams(dimension_semantics=("arbitrary", "arbitrary"),
                                             vmem_limit_bytes=60 * 1024 * 1024),
    )(kidx, h, dproj, wg, gwo, x2d, dx2, g1, small_a, small_b)


def _adam_update(w, g, m, v):
    nm = ADAM_B1 * m + (1.0 - ADAM_B1) * g
    nv = ADAM_B2 * v + (1.0 - ADAM_B2) * (g * g)
    m_hat = nm / (1.0 - ADAM_B1 ** ADAM_STEP)
    v_hat = nv / (1.0 - ADAM_B2 ** ADAM_STEP)
    return -ADAM_LR * (m_hat / (jnp.sqrt(v_hat) + ADAM_EPS) + ADAM_WD * w), nm, nv


def _adamw_all(tot, g_w_in, g_w_out, big, small, grad_x):
    n = len(small)
    rows = WO_ROWS
    steps = D_MODEL // rows

    def body(tot_ref, *refs):
        gx_ref, gx_out = refs[2 + 3 * (2 + n)], refs[-1]
        gx_out[...] = gx_ref[...]
        ins, outs = refs[:2 + 3 * (2 + n)], refs[3 + 3 * (2 + n):-1]
        g_refs, wmv = ins[:2], ins[2:]
        loss_ref, quads = outs[0], outs[1:]

        def update(j, g):
            w_ref, m_ref, v_ref = wmv[3 * j:3 * j + 3]
            g_ref, d_ref, nm_ref, nv_ref = quads[4 * j:4 * j + 4]
            g_ref[...] = g
            d_ref[...], nm_ref[...], nv_ref[...] = _adam_update(w_ref[...], g, m_ref[...], v_ref[...])

        update(0, g_refs[0][...])

        @pl.when(pl.program_id(0) == 0)
        def _():
            update(1, g_refs[1][...])
            k = 2 * lax.axis_index("x") + lax.axis_index("y")
            mine = pl.ds(pl.multiple_of(k * HEAD, HEAD), HEAD)
            loss_ref[...] = tot_ref[7:8, 0:1]
            grads = [tot_ref[0:1, :], tot_ref[1:2, :], tot_ref[2:3, 0:D_HGRN], tot_ref[2:3, D_HGRN:],
                     jnp.concatenate([tot_ref[3:4, 0:D_HGRN], tot_ref[3:4, D_HGRN:]], axis=0),
                     jnp.concatenate([tot_ref[4 + tap:5 + tap, mine] for tap in range(3)], axis=1)]
            for j, g in enumerate(grads):
                update(2 + j, g)

    whole = lambda a: pl.BlockSpec(a.shape, lambda i: (0, 0))
    blk = pl.BlockSpec((rows, SHARD_COLS), lambda i: (i, 0))
    arrays = [a for triple in big + small for a in triple]
    in_specs = ([whole(tot), blk, whole(g_w_out)] + [blk] * 3 + [whole(a) for a in arrays[3:]])
    shapes = [big[0][0], big[1][0]] + [w for w, _, _ in small]
    out_shape = (jax.ShapeDtypeStruct((1, 1), F32),) + tuple(
        jax.ShapeDtypeStruct(w.shape, F32) for w in shapes for _ in range(4))
    out_specs = (pl.BlockSpec((1, 1), lambda i: (0, 0)),) + (blk,) * 4 + tuple(
        whole(w) for w in shapes[1:] for _ in range(4))
    gx_blk = pl.BlockSpec((SEQ // steps, D_MODEL), lambda i: (i, 0))
    outs = pl.pallas_call(
        body, name="adamw_all", grid=(steps,),
        out_shape=out_shape + (jax.ShapeDtypeStruct(grad_x.shape, F32),),
        in_specs=in_specs + [gx_blk], out_specs=out_specs + (gx_blk,),
        compiler_params=pltpu.CompilerParams(dimension_semantics=("arbitrary",), vmem_limit_bytes=VMEM_LIMIT),
    )(tot, g_w_in, g_w_out, *arrays, grad_x)
    return [outs[0]] + [outs[1 + 4 * j:5 + 4 * j] for j in range(2 + n)] + [outs[-1]]


def _local_step(x2d, tgt, proj, lb_logits, cw, ga, gcn, w_out, gf):
    g64 = _group_matrix(HEAD, CONV_GROUP)
    mixed, aux, states, wog = _mix_fwd(proj, lb_logits, cw, ga, gcn, g64, w_out)
    dx2, dmixed, gwo, part_out = _out_loss(x2d, mixed, wog.reshape(D_MODEL, D_MODEL), gf, tgt)
    dproj, part_mix = _mix_bwd(proj, aux, states, dmixed, lb_logits, cw, ga, gcn, g64)
    return dproj, dx2, gwo.reshape(N_SHARD, WO_ROWS, D_MODEL), part_out, part_mix


def kernel(x, norm_gain, w_in, lb_logits, conv_w, hgrn_norm_gain, conv_norm_gain, w_out, final_norm_gain, loss_target, m_norm_gain, m_w_in, m_lb_logits, m_conv_w, m_hgrn_norm_gain, m_conv_norm_gain, m_w_out, m_final_norm_gain, v_norm_gain, v_w_in, v_lb_logits, v_conv_w, v_hgrn_norm_gain, v_conv_norm_gain, v_w_out, v_final_norm_gain):
    k = 2 * lax.axis_index("x") + lax.axis_index("y")
    kidx = jnp.reshape(k, (1,)).astype(jnp.int32)
    row = lambda a: a.reshape(1, D_MODEL)
    taps = lambda a: a.reshape(1, 3 * HEAD)
    h, proj, wg, cw = _gather_proj(kidx, x[0], norm_gain, w_in, taps(conv_w))
    dproj, dx2, gwo, part_out, part_mix = _local_step(
        x[0], loss_target[0], proj, lb_logits, cw, hgrn_norm_gain, conv_norm_gain, w_out, row(final_norm_gain))
    rgrad_x, rg_w_in, rg_w_out, tot = _bwd_tail(kidx, h, dproj, wg, gwo, x[0], dx2, norm_gain, part_out, part_mix)

    (loss, (g_w_in, d_w_in, nm_w_in, nv_w_in), (g_w_out, d_w_out, nm_w_out, nv_w_out),
     (g_norm_gain, d_ng, nm_ng, nv_ng), (g_final, d_fg, nm_fg, nv_fg), (g_hgrn, d_hg, nm_hg, nv_hg),
     (g_convn, d_cg, nm_cg, nv_cg), (g_lb, d_lb, nm_lb, nv_lb), (g_conv_w, d_cw, nm_cw, nv_cw),
     grad_x) = _adamw_all(
        tot, rg_w_in, rg_w_out,
        [(w_in[0], m_w_in[0], v_w_in[0]), (w_out[0], m_w_out[0], v_w_out[0])],
        [(norm_gain, m_norm_gain, v_norm_gain),
         (row(final_norm_gain), row(m_final_norm_gain), row(v_final_norm_gain)),
         (hgrn_norm_gain, m_hgrn_norm_gain, v_hgrn_norm_gain),
         (conv_norm_gain, m_conv_norm_gain, v_conv_norm_gain),
         (lb_logits, m_lb_logits, v_lb_logits),
         (taps(conv_w), taps(m_conv_w), taps(v_conv_w))],
        rgrad_x)
    flat = lambda a: a.reshape(D_MODEL)
    untap = lambda a: a.reshape(1, 3, HEAD)
    return (loss.reshape(()), grad_x[None],
            g_norm_gain, g_w_in[None], g_lb, untap(g_conv_w), g_hgrn, g_convn, g_w_out[None], flat(g_final),
            d_ng, d_w_in[None], d_lb, untap(d_cw), d_hg, d_cg, d_w_out[None], flat(d_fg),
            nm_ng, nm_w_in[None], nm_lb, untap(nm_cw), nm_hg, nm_cg, nm_w_out[None], flat(nm_fg),
            nv_ng, nv_w_in[None], nv_lb, untap(nv_cw), nv_hg, nv_cg, nv_w_out[None], flat(nv_fg))
```

```python
import jax
import jax.numpy as jnp
import numpy as np
from jax import lax
from jax.experimental import pallas as pl
from jax.experimental.pallas import tpu as pltpu

F32 = jnp.float32
BF16 = jnp.bfloat16
MESH = pl.DeviceIdType.MESH

SEQ = 2048
D_MODEL = 1024
D_HGRN = 512
D_CONV = 512
HEAD = 128
N_HEADS = 4
CHUNK = 64
CONV_GROUP = 64
N_SHARD = 4
SHARD_COLS = 1024
WO_ROWS = 256
EPS = 1e-6
TB = 256
NCB = TB // CHUNK
N_CHUNKS = SEQ // CHUNK
N_DEV = 8
COLLECTIVE_MIX_FWD = 0
AUX_O, AUX_CV, AUX_B, AUX_COLS = 0, 512, 1024, 1536

ADAM_LR = 0.001
ADAM_B1 = 0.9
ADAM_B2 = 0.999
ADAM_EPS = 1e-08
ADAM_WD = 0.01
ADAM_STEP = 10

VMEM_LIMIT = 56 * 1024 * 1024


def _dot(a, b):
    return jnp.dot(a, b, preferred_element_type=F32)


def _dot_nt(a, b):
    return lax.dot_general(a, b, (((1,), (1,)), ((), ())), preferred_element_type=F32)


def _dot_tn(a, b):
    return lax.dot_general(a, b, (((0,), (0,)), ((), ())), preferred_element_type=F32)


def _split_bf16(x, n):
    parts = []
    r = x
    for _ in range(n):
        p = r.astype(BF16)
        parts.append(p)
        r = r - p.astype(F32)
    return parts


def _exact_left(m, x, n=3):
    acc = None
    for p in _split_bf16(x, n):
        t = _dot(m, p)
        acc = t if acc is None else acc + t
    return acc


def _exact_left_many(m, xs, n=3):
    parts = [_split_bf16(x, n) for x in xs]
    accs = [None] * len(xs)
    for i in range(n):
        for j in range(len(xs)):
            t = _dot(m, parts[j][i])
            accs[j] = t if accs[j] is None else accs[j] + t
    return accs


def _group_mean_many(xs, gmat, n=2):
    parts = [_split_bf16(x, n) for x in xs]
    accs = [None] * len(xs)
    for i in range(n):
        for j in range(len(xs)):
            t = _dot(parts[j][i], gmat)
            accs[j] = t if accs[j] is None else accs[j] + t
    return accs


def _group_mean(x, gmat, n=2):
    w = gmat.shape[0]
    outs = []
    for c0 in range(0, x.shape[1], w):
        acc = None
        for p in _split_bf16(x[:, c0:c0 + w], n):
            t = _dot(p, gmat)
            acc = t if acc is None else acc + t
        outs.append(acc)
    return jnp.concatenate(outs, axis=1)


def _sigmoid(x):
    return 1.0 / (1.0 + jnp.exp(-x))


def _lower_bound(lbl):
    l0 = lbl[0:1, :]
    l1 = lbl[1:2, :]
    m = jnp.maximum(l0, l1)
    e0 = jnp.exp(l0 - m)
    e1 = jnp.exp(l1 - m)
    return e0 / (e0 + e1)


def _tri(lower):
    r = lax.broadcasted_iota(jnp.int32, (CHUNK, CHUNK), 0)
    c = lax.broadcasted_iota(jnp.int32, (CHUNK, CHUNK), 1)
    return jnp.where((c <= r) if lower else (c >= r), 1.0, 0.0).astype(BF16)


def _causal():
    r = lax.broadcasted_iota(jnp.int32, (CHUNK, CHUNK), 0)
    c = lax.broadcasted_iota(jnp.int32, (CHUNK, CHUNK), 1)
    return c <= r


def _shift_down(x, sh, prev_tail):
    r = pltpu.roll(x, sh, 0)
    pt = pltpu.roll(prev_tail, sh, 0)
    rows = lax.broadcasted_iota(jnp.int32, prev_tail.shape, 0)
    top = jnp.where(rows < sh, pt, r[0:8])
    return jnp.concatenate([top, r[8:]], axis=0)


def _shift_up(x, sh, next_head):
    n = x.shape[0]
    r = pltpu.roll(x, n - sh, 0)
    nh = pltpu.roll(next_head, 8 - sh, 0)
    rows = lax.broadcasted_iota(jnp.int32, next_head.shape, 0)
    bot = jnp.where(rows >= 8 - sh, nh, r[n - 8:])
    return jnp.concatenate([r[:n - 8], bot], axis=0)


def _group_matrix(width, group):
    r = np.arange(width)[:, None] // group
    c = np.arange(width)[None, :] // group
    return jnp.asarray(np.where(r == c, 1.0 / group, 0.0), dtype=BF16)


TP = 512
TG = 1024
SEM_W, SEM_CW, SEM_W_FWD, N_SEM = 0, 4, 7, 11


def _gather_proj(kidx, x2d, g1, w_in, conv_w):
    half_w = D_MODEL // 2
    half_c = SHARD_COLS // 2
    nt = SEQ // TG
    n_steps = 2 * N_SHARD

    def body(k_ref, x_ref, g_ref, w_ref, cw_ref, h_ref, p_ref, wg_out, cwg_out,
             wg_v, cwg_v, send_sems, recv_sems, out_sems):
        s, t = pl.program_id(0), pl.program_id(1)
        x, y, c = lax.axis_index("x"), lax.axis_index("y"), lax.axis_index("c")
        k = 2 * x + y
        sibling = (x, y, 1 - c)
        chips = [(1 - x, y), (x, 1 - y), (1 - x, 1 - y)]
        kjs = [2 * cx + cy for cx, cy in chips]
        diag = (*chips[2], c)

        def w_half(kk, cc):
            return wg_v.at[kk, pl.ds(cc * half_w, half_w), :]

        def w_quarter(kk, cc, piece):
            return wg_v.at[kk, pl.ds(cc * half_w, half_w), piece * half_c:(piece + 1) * half_c]

        def cw_of(kk):
            return cwg_v.at[:, pl.ds(pl.multiple_of(kk * HEAD, HEAD), HEAD)]

        def copy(sem, ref, to):
            return pltpu.make_async_remote_copy(
                src_ref=ref, dst_ref=ref, send_sem=send_sems.at[sem], recv_sem=recv_sems.at[sem],
                device_id=to, device_id_type=MESH)

        def at_step(sv, tv):
            return pl.when((s == sv) & (t == tv))

        w_direct = ([copy(SEM_W + j, w_half(k, c), (*chips[j], c)) for j in range(2)]
                    + [copy(SEM_W + 2 + p, w_quarter(k, c, p), diag) for p in range(2)])
        cw_direct = [copy(SEM_CW + j, cw_of(k), (*chip, c)) for j, chip in enumerate(chips)]
        w_passed = ([copy(SEM_W_FWD + j, w_half(kjs[j], c), sibling) for j in range(2)]
                    + [copy(SEM_W_FWD + 2 + p, w_quarter(kjs[2], c, p), sibling) for p in range(2)])
        stores = ([pltpu.make_async_copy(wg_v.at[kk], wg_out.at[kk], out_sems.at[i])
                   for i, kk in enumerate([k] + kjs)]
                  + [pltpu.make_async_copy(cwg_v, cwg_out, out_sems.at[4])])

        @at_step(0, 0)
        def _():
            wg_v[k] = w_ref[0].astype(BF16)
            mine = pl.ds(pl.multiple_of(k * HEAD, HEAD), HEAD)
            cwg_v[:, mine] = jnp.zeros((8, HEAD), F32)
            for tap in range(3):
                cwg_v[tap:tap + 1, mine] = cw_ref[:, tap * HEAD:(tap + 1) * HEAD]
            w_direct[0].start()
            w_direct[1].start()
            for cp in cw_direct:
                cp.start()
            stores[0].start()

        @at_step(2, 0)
        def _():
            for j in range(2):
                copy(SEM_W + j, w_half(kjs[j], c), sibling).wait_recv()
                w_passed[j].start()
            w_direct[2].start()
            w_direct[3].start()
            copy(SEM_W_FWD, w_half(kjs[0], 1 - c), sibling).wait_recv()
            stores[1].start()

        @at_step(4, 0)
        def _():
            copy(SEM_W_FWD + 1, w_half(kjs[1], 1 - c), sibling).wait_recv()
            stores[2].start()

        for p in range(2):
            @at_step(6 + p, 0)
            def _(p=p):
                copy(SEM_W + 2 + p, w_quarter(kjs[2], c, p), sibling).wait_recv()
                w_passed[2 + p].start()
                copy(SEM_W_FWD + 2 + p, w_quarter(kjs[2], 1 - c, p), sibling).wait_recv()

        rows = pl.ds(pl.multiple_of(t * TG, TG), TG)

        @pl.when(s == 0)
        def _():
            xv = x_ref[...]
            r = lax.rsqrt(jnp.mean(xv * xv, axis=-1, keepdims=True) + EPS)
            h_ref[rows, :] = (xv * r * g_ref[...]).astype(BF16)

        sh = s >> 1
        js = k ^ (((sh & 1) << 1) | (sh >> 1))
        for piece in range(2):
            @pl.when((s & 1) == piece)
            def _(piece=piece):
                p_ref[...] = _dot(h_ref[rows, :], wg_v[js, :, piece * half_c:(piece + 1) * half_c])

        @at_step(n_steps - 1, nt - 1)
        def _():
            stores[3].start()
            for j in range(3):
                copy(SEM_CW + j, cw_of(kjs[j]), sibling).wait_recv()
            stores[4].start()
            for cp in w_direct + cw_direct + w_passed:
                cp.wait_send()
            for st in stores:
                st.wait()

    def x_map(s, t, kr):
        return (jnp.where(s == 0, t, nt - 1), 0)

    def p_map(s, t, kr):
        sh = s >> 1
        return (t, 2 * (kr[0] ^ (((sh & 1) << 1) | (sh >> 1))) + (s & 1))

    hbm = pl.BlockSpec(memory_space=pl.ANY)
    grid_spec = pltpu.PrefetchScalarGridSpec(
        num_scalar_prefetch=1, grid=(n_steps, nt),
        in_specs=[pl.BlockSpec((TG, D_MODEL), x_map),
                  pl.BlockSpec((1, D_MODEL), lambda s, t, kr: (0, 0)),
                  pl.BlockSpec((1, D_MODEL, SHARD_COLS), lambda s, t, kr: (0, 0, 0)),
                  pl.BlockSpec((1, 3 * HEAD), lambda s, t, kr: (0, 0))],
        out_specs=(pl.BlockSpec((SEQ, D_MODEL), lambda s, t, kr: (0, 0)),
                   pl.BlockSpec((TG, half_c), p_map), hbm, hbm),
        scratch_shapes=[pltpu.VMEM((N_SHARD, D_MODEL, SHARD_COLS), BF16),
                        pltpu.VMEM((8, D_CONV), F32),
                        pltpu.SemaphoreType.DMA((N_SEM,)), pltpu.SemaphoreType.DMA((N_SEM,)),
                        pltpu.SemaphoreType.DMA((5,))])
    return pl.pallas_call(
        body, name="gather_proj", grid_spec=grid_spec,
        out_shape=(jax.ShapeDtypeStruct((SEQ, D_MODEL), BF16),
                   jax.ShapeDtypeStruct((SEQ, N_SHARD * SHARD_COLS), F32),
                   jax.ShapeDtypeStruct((N_SHARD, D_MODEL, SHARD_COLS), BF16),
                   jax.ShapeDtypeStruct((8, D_CONV), F32)),
        compiler_params=pltpu.CompilerParams(dimension_semantics=("arbitrary", "arbitrary"),
                                             vmem_limit_bytes=VMEM_LIMIT),
    )(kidx, x2d, g1, w_in, conv_w)


def _mix_fwd(proj, lb_logits, cw, ga, gcn, g64, w_out):
    half_o = WO_ROWS // 2
    nblk = SEQ // TB

    def body(p_ref, lbl_ref, cw_ref, ga_ref, gcn_ref, g64_ref, wo_ref,
             mixed_ref, aux_ref, sto_ref, wog_out,
             st_ref, tail_ref, wog_v, send_sems, recv_sems, out_sem):
        i = pl.program_id(0)
        x, y, c = lax.axis_index("x"), lax.axis_index("y"), lax.axis_index("c")
        k = 2 * x + y
        sibling = (x, y, 1 - c)
        chips = [(1 - x, y), (x, 1 - y), (1 - x, 1 - y)]
        kjs = [2 * cx + cy for cx, cy in chips]

        def wo_half(kk, cc):
            return wog_v.at[kk, pl.ds(cc * half_o, half_o), :]

        def copy(sem, ref, to):
            return pltpu.make_async_remote_copy(
                src_ref=ref, dst_ref=ref, send_sem=send_sems.at[sem], recv_sem=recv_sems.at[sem],
                device_id=to, device_id_type=MESH)

        wo_direct = [copy(j, wo_half(k, c), (*chip, c)) for j, chip in enumerate(chips)]
        wo_passed = [copy(3 + j, wo_half(kj, c), sibling) for j, kj in enumerate(kjs)]
        wo_store = pltpu.make_async_copy(wog_v, wog_out, out_sem.at[0])

        @pl.when(i == 0)
        def _():
            barrier = pltpu.get_barrier_semaphore()
            for peer in [sibling] + [(*chip, c) for chip in chips]:
                pl.semaphore_signal(barrier, inc=1, device_id=peer, device_id_type=MESH)
            st_ref[...] = jnp.zeros_like(st_ref)
            tail_ref[...] = jnp.zeros_like(tail_ref)
            wog_v[k] = wo_ref[0].astype(BF16)
            pl.semaphore_wait(barrier, 4)
            for cp in wo_direct:
                cp.start()

        @pl.when(i == nblk - 2)
        def _():
            for j in range(3):
                copy(j, wo_half(kjs[j], c), sibling).wait_recv()
                wo_passed[j].start()

        lb = _lower_bound(lbl_ref[...])
        tri = _tri(True)
        causal = _causal()
        g64m = g64_ref[...]
        heads = range(N_HEADS)
        cs = [slice(hd * HEAD, (hd + 1) * HEAD) for hd in heads]
        col = lambda base, hd: slice(base + hd * HEAD, base + (hd + 1) * HEAD)
        for n in range(NCB):
            sl = pl.ds(n * CHUNK, CHUNK)
            sg = [_sigmoid(p_ref[sl, col(512, hd)]) for hd in heads]
            f = [lb[:, cs[hd]] + (1.0 - lb[:, cs[hd]]) * sg[hd] for hd in heads]
            bc = _exact_left_many(tri, [jnp.log(f[hd]) for hd in heads])
            for hd in heads:
                aux_ref[sl, col(AUX_B, hd)] = bc[hd]
            g = [bc[hd][CHUNK - 1:CHUNK, :] for hd in heads]
            qd = [(p_ref[sl, col(0, hd)] * jnp.exp(bc[hd])).astype(BF16) for hd in heads]
            ki = [((1.0 - f[hd]) * jnp.exp(-bc[hd])).astype(BF16) for hd in heads]
            ke = [((1.0 - f[hd]) * jnp.exp(g[hd] - bc[hd])).astype(BF16) for hd in heads]
            vb = [p_ref[sl, col(1024, hd)].astype(BF16) for hd in heads]
            st = [st_ref[hd] for hd in heads]
            st_b = [a.astype(BF16) for a in st]
            for hd in heads:
                sto_ref[n, hd] = st_b[hd]
            scm = [_dot_nt(qd[hd], ki[hd]) for hd in heads]
            inter = [_dot_nt(qd[hd], st_b[hd]) for hd in heads]
            upd = [_dot_tn(vb[hd], ke[hd]) for hd in heads]
            intra = [_dot(jnp.where(causal, scm[hd], 0.0).astype(BF16), vb[hd]) for hd in heads]
            for hd in heads:
                st_ref[hd] = st[hd] * jnp.exp(g[hd]) + upd[hd]
                o = intra[hd] + inter[hd]
                aux_ref[sl, col(AUX_O, hd)] = o
                ra = lax.rsqrt(jnp.mean(o * o, axis=-1, keepdims=True) + EPS)
                za = p_ref[sl, col(1536, hd)]
                mixed_ref[sl, cs[hd]] = (o * ra * ga_ref[:, cs[hd]] * (za * _sigmoid(za))).astype(BF16)
            yb = []
            for hd in heads:
                cu = p_ref[sl, col(3072, hd)] * p_ref[sl, col(2048, hd)]
                tail = tail_ref[:, cs[hd]]
                cv = (cw_ref[0:1, cs[hd]] * _shift_down(cu, 2, tail) + cw_ref[1:2, cs[hd]] * _shift_down(cu, 1, tail)
                      + cw_ref[2:3, cs[hd]] * cu)
                tail_ref[:, cs[hd]] = cu[CHUNK - 8:, :]
                aux_ref[sl, col(AUX_CV, hd)] = cv
                yb.append(p_ref[sl, col(2560, hd)] * cv)
            ms = _group_mean_many([y * y for y in yb], g64m)
            for hd in heads:
                rb = lax.rsqrt(ms[hd] + EPS)
                zb = p_ref[sl, col(3584, hd)]
                mixed_ref[sl, col(512, hd)] = (yb[hd] * rb * gcn_ref[:, cs[hd]] * (zb * _sigmoid(zb))).astype(BF16)

        @pl.when(i == nblk - 1)
        def _():
            for j in range(3):
                copy(3 + j, wo_half(kjs[j], 1 - c), sibling).wait_recv()
            wo_store.start()
            for cp in wo_direct + wo_passed:
                cp.wait_send()
            wo_store.wait()

    row = lambda w: pl.BlockSpec((1, w), lambda i: (0, 0))
    return pl.pallas_call(
        body, name="mix_fwd", grid=(nblk,),
        out_shape=(jax.ShapeDtypeStruct((SEQ, D_MODEL), BF16),
                   jax.ShapeDtypeStruct((SEQ, AUX_COLS), F32),
                   jax.ShapeDtypeStruct((N_CHUNKS, N_HEADS, HEAD, HEAD), BF16),
                   jax.ShapeDtypeStruct((N_SHARD, WO_ROWS, D_MODEL), BF16)),
        in_specs=[pl.BlockSpec((TB, 4096), lambda i: (i, 0)),
                  pl.BlockSpec((2, D_HGRN), lambda i: (0, 0)),
                  pl.BlockSpec((8, D_CONV), lambda i: (0, 0)),
                  row(D_HGRN), row(D_CONV),
                  pl.BlockSpec((HEAD, HEAD), lambda i: (0, 0)),
                  pl.BlockSpec((1, WO_ROWS, D_MODEL), lambda i: (0, 0, 0))],
        out_specs=(pl.BlockSpec((TB, D_MODEL), lambda i: (i, 0)),
                   pl.BlockSpec((TB, AUX_COLS), lambda i: (i, 0)),
                   pl.BlockSpec((NCB, N_HEADS, HEAD, HEAD), lambda i: (i, 0, 0, 0)),
                   pl.BlockSpec(memory_space=pl.ANY)),
        scratch_shapes=[pltpu.VMEM((N_HEADS, HEAD, HEAD), F32), pltpu.VMEM((8, D_CONV), F32),
                        pltpu.VMEM((N_SHARD, WO_ROWS, D_MODEL), BF16),
                        pltpu.SemaphoreType.DMA((6,)), pltpu.SemaphoreType.DMA((6,)),
                        pltpu.SemaphoreType.DMA((1,))],
        compiler_params=pltpu.CompilerParams(dimension_semantics=("arbitrary",), vmem_limit_bytes=VMEM_LIMIT,
                                             collective_id=COLLECTIVE_MIX_FWD),
    )(proj, lb_logits, cw, ga, gcn, g64, w_out)


def _out_loss(x2d, mixed, wog, gf, tgt):
    def body(x_ref, m_ref, wo_ref, gf_ref, t_ref, dx2_ref, dm_ref, gwo_ref, part_ref, acc_ref):
        i = pl.program_id(0)

        @pl.when(i == 0)
        def _():
            acc_ref[...] = jnp.zeros_like(acc_ref)
            part_ref[...] = jnp.zeros_like(part_ref)

        mixed_b = m_ref[...]
        x2 = x_ref[...] + _dot(mixed_b, wo_ref[...])
        r2 = lax.rsqrt(jnp.mean(x2 * x2, axis=-1, keepdims=True) + EPS)
        n2 = x2 * r2
        gfv = gf_ref[...]
        err = n2 * gfv - t_ref[...]
        loss = 0.5 * jnp.sum(jnp.mean(err * err, axis=-1, keepdims=True), axis=0, keepdims=True)
        dy = err * (1.0 / D_MODEL)
        part_ref[1:2, :] += jnp.sum(dy * n2, axis=0, keepdims=True)
        part_ref[7:8, :] += jnp.broadcast_to(loss, (1, D_MODEL))
        dn = dy * gfv
        dx2 = r2 * (dn - n2 * jnp.mean(dn * n2, axis=-1, keepdims=True))
        dx2_ref[...] = dx2
        dx2_b = dx2.astype(BF16)
        dm_ref[...] = _dot_nt(dx2_b, wo_ref[...])
        acc_ref[...] += _dot_tn(mixed_b, dx2_b)

        @pl.when(i == pl.num_programs(0) - 1)
        def _():
            gwo_ref[...] = acc_ref[...].astype(BF16)

    blk = lambda: pl.BlockSpec((TP, D_MODEL), lambda i: (i, 0))
    return pl.pallas_call(
        body, name="out_loss", grid=(SEQ // TP,),
        out_shape=(jax.ShapeDtypeStruct((SEQ, D_MODEL), F32),
                   jax.ShapeDtypeStruct((SEQ, D_MODEL), F32),
                   jax.ShapeDtypeStruct((D_MODEL, D_MODEL), BF16),
                   jax.ShapeDtypeStruct((8, D_MODEL), F32)),
        in_specs=[blk(), blk(), pl.BlockSpec((D_MODEL, D_MODEL), lambda i: (0, 0)),
                  pl.BlockSpec((1, D_MODEL), lambda i: (0, 0)), blk()],
        out_specs=(blk(), blk(), pl.BlockSpec((D_MODEL, D_MODEL), lambda i: (0, 0)),
                   pl.BlockSpec((8, D_MODEL), lambda i: (0, 0))),
        scratch_shapes=[pltpu.VMEM((D_MODEL, D_MODEL), F32)],
        compiler_params=pltpu.CompilerParams(dimension_semantics=("arbitrary",), vmem_limit_bytes=VMEM_LIMIT),
    )(x2d, mixed, wog, gf, tgt)


def _mix_bwd(proj, aux, states, dmixed, lb_logits, cw, ga, gcn, g64):
    nblk = SEQ // TB

    def body(p_ref, aux_ref, st_ref, dm_ref, lbl_ref, cw_ref, ga_ref, gcn_ref, g64_ref,
             dp_ref, part_ref, dst_ref, head_ref, dlb_ref):
        i = pl.program_id(0)

        @pl.when(i == 0)
        def _():
            dst_ref[...] = jnp.zeros_like(dst_ref)
            head_ref[...] = jnp.zeros_like(head_ref)
            part_ref[...] = jnp.zeros_like(part_ref)
            dlb_ref[...] = jnp.zeros_like(dlb_ref)

        lb = _lower_bound(lbl_ref[...])
        triu = _tri(False)
        causal = _causal()
        g64m = g64_ref[...]
        rowsum = lambda a: jnp.sum(a, axis=0, keepdims=True)
        heads = range(N_HEADS)
        cs = [slice(hd * HEAD, (hd + 1) * HEAD) for hd in heads]
        col = lambda base, hd: slice(base + hd * HEAD, base + (hd + 1) * HEAD)
        for n in reversed(range(NCB)):
            sl = pl.ds(n * CHUNK, CHUNK)
            cvv = [aux_ref[sl, col(AUX_CV, hd)] for hd in heads]
            gb = [p_ref[sl, col(2560, hd)] for hd in heads]
            yb = [gb[hd] * cvv[hd] for hd in heads]
            ms = _group_mean_many([y * y for y in yb], g64m)
            rb, nb, dnb = [], [], []
            for hd in heads:
                rb.append(lax.rsqrt(ms[hd] + EPS))
                nb.append(yb[hd] * rb[hd])
                zb = p_ref[sl, col(3584, hd)]
                sgb = _sigmoid(zb)
                dmb = dm_ref[sl, col(512, hd)]
                gcv = gcn_ref[:, cs[hd]]
                part_ref[2:3, col(512, hd)] += rowsum(dmb * nb[hd] * (zb * sgb))
                dp_ref[sl, col(3584, hd)] = (dmb * nb[hd] * gcv * (sgb * (1.0 + zb * (1.0 - sgb)))).astype(BF16)
                dnb.append(dmb * gcv * (zb * sgb))
            mdn = _group_mean_many([dnb[hd] * nb[hd] for hd in heads], g64m)
            for hd in heads:
                dyb = rb[hd] * (dnb[hd] - nb[hd] * mdn[hd])
                dp_ref[sl, col(2560, hd)] = (dyb * cvv[hd]).astype(BF16)
                dcv = dyb * gb[hd]
                head = head_ref[:, cs[hd]]
                dcv1 = _shift_up(dcv, 1, head)
                dcv2 = _shift_up(dcv, 2, head)
                head_ref[:, cs[hd]] = dcv[0:8, :]
                u = p_ref[sl, col(2048, hd)]
                gc = p_ref[sl, col(3072, hd)]
                cu = gc * u
                part_ref[4:5, cs[hd]] += rowsum(dcv2 * cu)
                part_ref[5:6, cs[hd]] += rowsum(dcv1 * cu)
                part_ref[6:7, cs[hd]] += rowsum(dcv * cu)
                dcu = cw_ref[2:3, cs[hd]] * dcv + cw_ref[1:2, cs[hd]] * dcv1 + cw_ref[0:1, cs[hd]] * dcv2
                dp_ref[sl, col(3072, hd)] = (dcu * u).astype(BF16)
                dp_ref[sl, col(2048, hd)] = (dcu * gc).astype(BF16)
            do_b = []
            for hd in heads:
                ov = aux_ref[sl, col(AUX_O, hd)]
                ra = lax.rsqrt(jnp.mean(ov * ov, axis=-1, keepdims=True) + EPS)
                na = ov * ra
                za = p_ref[sl, col(1536, hd)]
                sga = _sigmoid(za)
                dma = dm_ref[sl, cs[hd]]
                gav = ga_ref[:, cs[hd]]
                part_ref[2:3, cs[hd]] += rowsum(dma * na * (za * sga))
                dp_ref[sl, col(1536, hd)] = (dma * na * gav * (sga * (1.0 + za * (1.0 - sga)))).astype(BF16)
                dna = dma * gav * (za * sga)
                do_b.append((ra * (dna - na * jnp.mean(dna * na, axis=-1, keepdims=True))).astype(BF16))
            s = [_sigmoid(p_ref[sl, col(512, hd)]) for hd in heads]
            f = [lb[:, cs[hd]] + (1.0 - lb[:, cs[hd]]) * s[hd] for hd in heads]
            bc = [aux_ref[sl, col(AUX_B, hd)] for hd in heads]
            g = [bc[hd][CHUNK - 1:CHUNK, :] for hd in heads]
            eb = [jnp.exp(bc[hd]) for hd in heads]
            enb = [jnp.exp(-bc[hd]) for hd in heads]
            eg = [jnp.exp(g[hd] - bc[hd]) for hd in heads]
            dec = [jnp.exp(g[hd]) for hd in heads]
            qd = [p_ref[sl, cs[hd]] * eb[hd] for hd in heads]
            ki = [(1.0 - f[hd]) * enb[hd] for hd in heads]
            ke = [(1.0 - f[hd]) * eg[hd] for hd in heads]
            qd_b = [a.astype(BF16) for a in qd]
            ki_b = [a.astype(BF16) for a in ki]
            ke_b = [a.astype(BF16) for a in ke]
            vb = [p_ref[sl, col(1024, hd)].astype(BF16) for hd in heads]
            st_b = [st_ref[n, hd] for hd in heads]
            dst = [dst_ref[hd] for hd in heads]
            dst_b = [a.astype(BF16) for a in dst]
            scm = [_dot_nt(qd_b[hd], ki_b[hd]) for hd in heads]
            amm = [_dot_nt(do_b[hd], vb[hd]) for hd in heads]
            dqd2 = [_dot(do_b[hd], st_b[hd]) for hd in heads]
            dke = [_dot(vb[hd], dst_b[hd]) for hd in heads]
            dv2 = [_dot_nt(ke_b[hd], dst_b[hd]) for hd in heads]
            dsu = [_dot_tn(do_b[hd], qd_b[hd]) for hd in heads]
            sc = [jnp.where(causal, scm[hd], 0.0).astype(BF16) for hd in heads]
            am = [jnp.where(causal, amm[hd], 0.0).astype(BF16) for hd in heads]
            dqd1 = [_dot(am[hd], ki_b[hd]) for hd in heads]
            dki = [_dot_tn(am[hd], qd_b[hd]) for hd in heads]
            dv1 = [_dot_tn(sc[hd], do_b[hd]) for hd in heads]
            db, dgv = [], []
            for hd in heads:
                dqd = dqd1[hd] + dqd2[hd]
                ddec = rowsum(dst[hd] * st_b[hd].astype(F32))
                dst_ref[hd] = dst[hd] * dec[hd] + dsu[hd]
                dp_ref[sl, cs[hd]] = (dqd * eb[hd]).astype(BF16)
                dp_ref[sl, col(1024, hd)] = (dv1[hd] + dv2[hd]).astype(BF16)
                db.append(dqd * qd[hd] - dki[hd] * ki[hd] - dke[hd] * ke[hd])
                dgv.append(rowsum(dke[hd] * ke[hd]) + ddec * dec[hd])
            rc = _exact_left_many(triu, db, 2)
            for hd in heads:
                df = (rc[hd] + dgv[hd]) / f[hd] - (dki[hd] * enb[hd] + dke[hd] * eg[hd])
                dlb_ref[:, cs[hd]] += rowsum(df * (1.0 - s[hd]))
                dp_ref[sl, col(512, hd)] = (df * (1.0 - lb[:, cs[hd]]) * s[hd] * (1.0 - s[hd])).astype(BF16)

        @pl.when(i == nblk - 1)
        def _():
            row = dlb_ref[...] * lb * (1.0 - lb)
            part_ref[3:4, 0:D_HGRN] = row
            part_ref[3:4, D_HGRN:] = -row

    rev = lambda w: pl.BlockSpec((TB, w), lambda i: (nblk - 1 - i, 0))
    row = lambda w: pl.BlockSpec((1, w), lambda i: (0, 0))
    return pl.pallas_call(
        body, name="mix_bwd", grid=(nblk,),
        out_shape=(jax.ShapeDtypeStruct((SEQ, 4096), BF16),
                   jax.ShapeDtypeStruct((8, D_MODEL), F32)),
        in_specs=[rev(4096), rev(AUX_COLS),
                  pl.BlockSpec((NCB, N_HEADS, HEAD, HEAD), lambda i: (nblk - 1 - i, 0, 0, 0)),
                  rev(D_MODEL),
                  pl.BlockSpec((2, D_HGRN), lambda i: (0, 0)),
                  pl.BlockSpec((8, D_CONV), lambda i: (0, 0)),
                  row(D_HGRN), row(D_CONV),
                  pl.BlockSpec((HEAD, HEAD), lambda i: (0, 0))],
        out_specs=(rev(4096), pl.BlockSpec((8, D_MODEL), lambda i: (0, 0))),
        scratch_shapes=[pltpu.VMEM((N_HEADS, HEAD, HEAD), F32), pltpu.VMEM((8, D_CONV), F32),
                        pltpu.VMEM((1, D_HGRN), F32)],
        compiler_params=pltpu.CompilerParams(dimension_semantics=("arbitrary",), vmem_limit_bytes=VMEM_LIMIT),
    )(proj, aux, states, dmixed, lb_logits, cw, ga, gcn, g64)


TT = 1024
TX = 256
(SEM_D2D, SEM_D2D_O, SEM_ICI, SEM_ICI_O, SEM_FIN, SEM_FIN_O, SEM_SMALL, N_SEM_TAIL) = 0, 4, 5, 8, 11, 12, 12, 20


def _bwd_tail(kidx, h, dproj, wg, gwo, x2d, dx2, g1, small_a, small_b):
    hw = D_MODEL // 2
    ho = WO_ROWS // 2
    nt = SEQ // TT
    n_steps = N_SHARD + SEQ // TX // nt

    def body(k_ref, h_ref, dp_ref, w_ref, gwo_ref, x_ref, dx2_ref, g_ref, sm_ref, smb_ref,
             gx_ref, gw_out, gwo_out, osm_ref,
             acc, dh, sendbuf, keep, sibrcv, rcv, sib_o, p_o, rcv_o, res_o, sm_buf, dng,
             send_sems, recv_sems, out_sems):
        s, t = pl.program_id(0), pl.program_id(1)
        x, y, c = lax.axis_index("x"), lax.axis_index("y"), lax.axis_index("c")
        k = 2 * x + y
        me = 4 * x + 2 * y + c
        sibling = (x, y, 1 - c)
        chips = [(1 - x, 1 - y), (1 - x, y), (x, 1 - y)]
        kjs = [2 * cx + cy for cx, cy in chips]
        mine = pl.ds(pl.multiple_of(c * hw, hw), hw)
        other = pl.ds(pl.multiple_of((1 - c) * hw, hw), hw)
        mine_o = pl.ds(pl.multiple_of(c * ho, ho), ho)
        other_o = pl.ds(pl.multiple_of((1 - c) * ho, ho), ho)

        def copy(sem, src, dst, to):
            return pltpu.make_async_remote_copy(
                src_ref=src, dst_ref=dst, send_sem=send_sems.at[sem], recv_sem=recv_sems.at[sem],
                device_id=to, device_id_type=MESH)

        def at_step(sv, tv):
            return pl.when((s == sv) & (t == tv))

        def at_norm_block(b):
            return at_step(N_SHARD + b // nt, b % nt)

        d2d = [copy(SEM_D2D + sv, sendbuf.at[sv], sibrcv.at[sv], sibling) for sv in range(N_SHARD)]
        d2d_o = copy(SEM_D2D_O, gwo_ref.at[:, other_o, :], sib_o, sibling)
        ici = [copy(SEM_ICI + sv, keep.at[sv], rcv.at[sv], (*chips[sv], c)) for sv in range(3)]
        ici_o = [copy(SEM_ICI_O + sv, p_o.at[kjs[sv]], rcv_o.at[sv], (*chips[sv], c)) for sv in range(3)]
        fin = copy(SEM_FIN, acc.at[mine, :], gw_out.at[mine, :], sibling)
        fin_o = copy(SEM_FIN_O, res_o.at[mine_o, :], res_o.at[mine_o, :], sibling)
        smalls = [copy(SEM_SMALL + m, sm_buf.at[me], sm_buf.at[me],
                       (x ^ (m >> 2), y ^ ((m >> 1) & 1), c ^ (m & 1))) for m in range(1, N_DEV)]
        store_w = pltpu.make_async_copy(acc.at[mine, :], gw_out.at[mine, :], out_sems.at[0])
        store_o = pltpu.make_async_copy(res_o, gwo_out, out_sems.at[1])

        @at_step(0, 0)
        def _():
            d2d_o.start()

        @at_step(0, 1)
        def _():
            d2d_o.wait_recv()
            for j in range(N_SHARD):
                p_o[j] = (gwo_ref[j, mine_o, :].astype(F32) + sib_o[j].astype(F32)).astype(BF16)
            res_o[mine_o, :] = gwo_ref[k, mine_o, :].astype(F32) + sib_o[k].astype(F32)
            for cp in ici_o:
                cp.start()

        rows = pl.ds(pl.multiple_of(t * TT, TT), TT)

        @pl.when(s < N_SHARD)
        def _():
            dpb = dp_ref[...]
            part = _dot_tn(h_ref[...], dpb)

            @pl.when(t == 0)
            def _():
                acc[...] = part

            @pl.when(t > 0)
            def _():
                acc[...] += part

            d = _dot_nt(dpb, w_ref[0])

            @pl.when(s == 0)
            def _():
                dh[rows, :] = d

            @pl.when(s > 0)
            def _():
                dh[rows, :] += d

        for sv in range(N_SHARD):
            @at_step(sv, nt - 1)
            def _(sv=sv):
                sendbuf[sv] = acc[other, :].astype(BF16)
                if sv < 3:
                    keep[sv] = acc[mine, :].astype(BF16)
                d2d[sv].start()

        for sv in range(3):
            @at_step(sv + 1, 0)
            def _(sv=sv):
                d2d[sv].wait_recv()
                keep[sv] = (keep[sv].astype(F32) + sibrcv[sv].astype(F32)).astype(BF16)
                ici[sv].start()

        @at_norm_block(0)
        def _():
            d2d[3].wait_recv()
            ici[0].wait_recv()
            acc[mine, :] += sibrcv[3].astype(F32) + rcv[0].astype(F32)

        @at_norm_block(1)
        def _():
            tot = res_o[mine_o, :]
            for sv in range(3):
                ici_o[sv].wait_recv()
                tot = tot + rcv_o[sv].astype(F32)
            res_o[mine_o, :] = tot
            fin_o.start()

        @at_norm_block(2)
        def _():
            ici[1].wait_recv()
            acc[mine, :] += rcv[1].astype(F32)

        @at_norm_block(0)
        def _():
            dng[...] = jnp.zeros_like(dng)

        @pl.when(s >= N_SHARD)
        def _():
            blk = (s - N_SHARD) * nt + t
            dhv = dh[pl.ds(pl.multiple_of(blk * TX, TX), TX), :]
            xv = x_ref[...]
            r = lax.rsqrt(jnp.mean(xv * xv, axis=-1, keepdims=True) + EPS)
            xn = xv * r
            dng[...] += jnp.sum(dhv * xn, axis=0, keepdims=True)
            dxn = dhv * g_ref[...]
            gx_ref[...] = dx2_ref[...] + r * (dxn - xn * jnp.mean(dxn * xn, axis=-1, keepdims=True))

        @at_step(n_steps - 1, nt - 1)
        def _():
            sm_buf[me] = sm_ref[...] + smb_ref[...]
            sm_buf[me, 0:1, :] = dng[...]
            for cp in smalls:
                cp.start()
            ici[2].wait_recv()
            acc[mine, :] += rcv[2].astype(F32)
            fin.start()
            store_w.start()
            for m in range(1, N_DEV):
                copy(SEM_SMALL + m, sm_buf.at[0], sm_buf.at[0], sibling).wait_recv()
            tot = sm_buf[0]
            for d in range(1, N_DEV):
                tot = tot + sm_buf[d]
            osm_ref[...] = tot
            fin_o.wait_recv()
            store_o.start()
            fin.wait_recv()
            for cp in d2d + [d2d_o] + ici + ici_o + [fin, fin_o] + smalls:
                cp.wait_send()
            store_o.wait()
            store_w.wait()

    def shard_of(s, kr):
        return kr[0] ^ (3 - jnp.minimum(s, 3))

    def tok(s, t):
        return jnp.where(s < N_SHARD, t, nt - 1)

    def blk_map(s, t, kr):
        return (jnp.where(s < N_SHARD, 0, (s - N_SHARD) * nt + t), 0)

    hbm = pl.BlockSpec(memory_space=pl.ANY)
    grid_spec = pltpu.PrefetchScalarGridSpec(
        num_scalar_prefetch=1, grid=(n_steps, nt),
        in_specs=[pl.BlockSpec((TT, D_MODEL), lambda s, t, kr: (tok(s, t), 0)),
                  pl.BlockSpec((TT, SHARD_COLS), lambda s, t, kr: (tok(s, t), shard_of(s, kr))),
                  pl.BlockSpec((1, D_MODEL, SHARD_COLS), lambda s, t, kr: (shard_of(s, kr), 0, 0)),
                  pl.BlockSpec((N_SHARD, WO_ROWS, D_MODEL), lambda s, t, kr: (0, 0, 0)),
                  pl.BlockSpec((TX, D_MODEL), blk_map),
                  pl.BlockSpec((TX, D_MODEL), blk_map),
                  pl.BlockSpec((1, D_MODEL), lambda s, t, kr: (0, 0)),
                  pl.BlockSpec((8, D_MODEL), lambda s, t, kr: (0, 0)),
                  pl.BlockSpec((8, D_MODEL), lambda s, t, kr: (0, 0))],
        out_specs=(pl.BlockSpec((TX, D_MODEL), blk_map), hbm, hbm,
                   pl.BlockSpec((8, D_MODEL), lambda s, t, kr: (0, 0))),
        scratch_shapes=[pltpu.VMEM((D_MODEL, SHARD_COLS), F32), pltpu.VMEM((SEQ, D_MODEL), F32),
                        pltpu.VMEM((N_SHARD, hw, SHARD_COLS), BF16), pltpu.VMEM((3, hw, SHARD_COLS), BF16),
                        pltpu.VMEM((N_SHARD, hw, SHARD_COLS), BF16), pltpu.VMEM((3, hw, SHARD_COLS), BF16),
                        pltpu.VMEM((N_SHARD, ho, D_MODEL), BF16), pltpu.VMEM((N_SHARD, ho, D_MODEL), BF16),
                        pltpu.VMEM((3, ho, D_MODEL), BF16), pltpu.VMEM((WO_ROWS, D_MODEL), F32),
                        pltpu.VMEM((N_DEV, 8, D_MODEL), F32), pltpu.VMEM((1, D_MODEL), F32),
                        pltpu.SemaphoreType.DMA((N_SEM_TAIL,)), pltpu.SemaphoreType.DMA((N_SEM_TAIL,)),
                        pltpu.SemaphoreType.DMA((2,))])
    return pl.pallas_call(
        body, name="bwd_tail", grid_spec=grid_spec,
        out_shape=(jax.ShapeDtypeStruct((SEQ, D_MODEL), F32),
                   jax.ShapeDtypeStruct((D_MODEL, SHARD_COLS), F32),
                   jax.ShapeDtypeStruct((WO_ROWS, D_MODEL), F32),
                   jax.ShapeDtypeStruct((8, D_MODEL), F32)),
        compiler_params=pltpu.CompilerParams(dimension_semantics=("arbitrary", "arbitrary"),
                                             vmem_limit_bytes=60 * 1024 * 1024),
    )(kidx, h, dproj, wg, gwo, x2d, dx2, g1, small_a, small_b)


def _adam_update(w, g, m, v):
    nm = ADAM_B1 * m + (1.0 - ADAM_B1) * g
    nv = ADAM_B2 * v + (1.0 - ADAM_B2) * (g * g)
    m_hat = nm / (1.0 - ADAM_B1 ** ADAM_STEP)
    v_hat = nv / (1.0 - ADAM_B2 ** ADAM_STEP)
    return -ADAM_LR * (m_hat / (jnp.sqrt(v_hat) + ADAM_EPS) + ADAM_WD * w), nm, nv


def _adamw_all(tot, g_w_in, g_w_out, big, small, grad_x):
    n = len(small)
    rows = WO_ROWS
    steps = D_MODEL // rows

    def body(tot_ref, *refs):
        gx_ref, gx_out = refs[2 + 3 * (2 + n)], refs[-1]
        gx_out[...] = gx_ref[...]
        ins, outs = refs[:2 + 3 * (2 + n)], refs[3 + 3 * (2 + n):-1]
        g_refs, wmv = ins[:2], ins[2:]
        loss_ref, quads = outs[0], outs[1:]

        def update(j, g):
            w_ref, m_ref, v_ref = wmv[3 * j:3 * j + 3]
            g_ref, d_ref, nm_ref, nv_ref = quads[4 * j:4 * j + 4]
            g_ref[...] = g
            d_ref[...], nm_ref[...], nv_ref[...] = _adam_update(w_ref[...], g, m_ref[...], v_ref[...])

        update(0, g_refs[0][...])

        @pl.when(pl.program_id(0) == 0)
        def _():
            update(1, g_refs[1][...])
            k = 2 * lax.axis_index("x") + lax.axis_index("y")
            mine = pl.ds(pl.multiple_of(k * HEAD, HEAD), HEAD)
            loss_ref[...] = tot_ref[7:8, 0:1]
            grads = [tot_ref[0:1, :], tot_ref[1:2, :], tot_ref[2:3, 0:D_HGRN], tot_ref[2:3, D_HGRN:],
                     jnp.concatenate([tot_ref[3:4, 0:D_HGRN], tot_ref[3:4, D_HGRN:]], axis=0),
                     jnp.concatenate([tot_ref[4 + tap:5 + tap, mine] for tap in range(3)], axis=1)]
            for j, g in enumerate(grads):
                update(2 + j, g)

    whole = lambda a: pl.BlockSpec(a.shape, lambda i: (0, 0))
    blk = pl.BlockSpec((rows, SHARD_COLS), lambda i: (i, 0))
    arrays = [a for triple in big + small for a in triple]
    in_specs = ([whole(tot), blk, whole(g_w_out)] + [blk] * 3 + [whole(a) for a in arrays[3:]])
    shapes = [big[0][0], big[1][0]] + [w for w, _, _ in small]
    out_shape = (jax.ShapeDtypeStruct((1, 1), F32),) + tuple(
        jax.ShapeDtypeStruct(w.shape, F32) for w in shapes for _ in range(4))
    out_specs = (pl.BlockSpec((1, 1), lambda i: (0, 0)),) + (blk,) * 4 + tuple(
        whole(w) for w in shapes[1:] for _ in range(4))
    gx_blk = pl.BlockSpec((SEQ // steps, D_MODEL), lambda i: (i, 0))
    outs = pl.pallas_call(
        body, name="adamw_all", grid=(steps,),
        out_shape=out_shape + (jax.ShapeDtypeStruct(grad_x.shape, F32),),
        in_specs=in_specs + [gx_blk], out_specs=out_specs + (gx_blk,),
        compiler_params=pltpu.CompilerParams(dimension_semantics=("arbitrary",), vmem_limit_bytes=VMEM_LIMIT),
    )(tot, g_w_in, g_w_out, *arrays, grad_x)
    return [outs[0]] + [outs[1 + 4 * j:5 + 4 * j] for j in range(2 + n)] + [outs[-1]]


def _local_step(x2d, tgt, proj, lb_logits, cw, ga, gcn, w_out, gf):
    g64 = _group_matrix(HEAD, CONV_GROUP)
    mixed, aux, states, wog = _mix_fwd(proj, lb_logits, cw, ga, gcn, g64, w_out)
    dx2, dmixed, gwo, part_out = _out_loss(x2d, mixed, wog.reshape(D_MODEL, D_MODEL), gf, tgt)
    dproj, part_mix = _mix_bwd(proj, aux, states, dmixed, lb_logits, cw, ga, gcn, g64)
    return dproj, dx2, gwo.reshape(N_SHARD, WO_ROWS, D_MODEL), part_out, part_mix


def kernel(x, norm_gain, w_in, lb_logits, conv_w, hgrn_norm_gain, conv_norm_gain, w_out, final_norm_gain, loss_target, m_norm_gain, m_w_in, m_lb_logits, m_conv_w, m_hgrn_norm_gain, m_conv_norm_gain, m_w_out, m_final_norm_gain, v_norm_gain, v_w_in, v_lb_logits, v_conv_w, v_hgrn_norm_gain, v_conv_norm_gain, v_w_out, v_final_norm_gain):
    k = 2 * lax.axis_index("x") + lax.axis_index("y")
    kidx = jnp.reshape(k, (1,)).astype(jnp.int32)
    row = lambda a: a.reshape(1, D_MODEL)
    taps = lambda a: a.reshape(1, 3 * HEAD)
    h, proj, wg, cw = _gather_proj(kidx, x[0], norm_gain, w_in, taps(conv_w))
    dproj, dx2, gwo, part_out, part_mix = _local_step(
        x[0], loss_target[0], proj, lb_logits, cw, hgrn_norm_gain, conv_norm_gain, w_out, row(final_norm_gain))
    rgrad_x, rg_w_in, rg_w_out, tot = _bwd_tail(kidx, h, dproj, wg, gwo, x[0], dx2, norm_gain, part_out, part_mix)

    (loss, (g_w_in, d_w_in, nm_w_in, nv_w_in), (g_w_out, d_w_out, nm_w_out, nv_w_out),
     (g_norm_gain, d_ng, nm_ng, nv_ng), (g_final, d_fg, nm_fg, nv_fg), (g_hgrn, d_hg, nm_hg, nv_hg),
     (g_convn, d_cg, nm_cg, nv_cg), (g_lb, d_lb, nm_lb, nv_lb), (g_conv_w, d_cw, nm_cw, nv_cw),
     grad_x) = _adamw_all(
        tot, rg_w_in, rg_w_out,
        [(w_in[0], m_w_in[0], v_w_in[0]), (w_out[0], m_w_out[0], v_w_out[0])],
        [(norm_gain, m_norm_gain, v_norm_gain),
         (row(final_norm_gain), row(m_final_norm_gain), row(v_final_norm_gain)),
         (hgrn_norm_gain, m_hgrn_norm_gain, v_hgrn_norm_gain),
         (conv_norm_gain, m_conv_norm_gain, v_conv_norm_gain),
         (lb_logits, m_lb_logits, v_lb_logits),
         (taps(conv_w), taps(m_conv_w), taps(v_conv_w))],
        rgrad_x)
    flat = lambda a: a.reshape(D_MODEL)
    untap = lambda a: a.reshape(1, 3, HEAD)
    return (loss.reshape(()), grad_x[None],
            g_norm_gain, g_w_in[None], g_lb, untap(g_conv_w), g_hgrn, g_convn, g_w_out[None], flat(g_final),
            d_ng, d_w_in[None], d_lb, untap(d_cw), d_hg, d_cg, d_w_out[None], flat(d_fg),
            nm_ng, nm_w_in[None], nm_lb, untap(nm_cw), nm_hg, nm_cg, nm_w_out[None], flat(nm_fg),
            nv_ng, nv_w_in[None], nv_lb, untap(nv_cw), nv_hg, nv_cg, nv_w_out[None], flat(nv_fg))
```

```python
import jax
import jax.numpy as jnp
import numpy as np
from jax import lax
from jax.experimental import pallas as pl
from jax.experimental.pallas import tpu as pltpu

F32 = jnp.float32
BF16 = jnp.bfloat16
MESH = pl.DeviceIdType.MESH

SEQ = 2048
D_MODEL = 1024
D_HGRN = 512
D_CONV = 512
HEAD = 128
N_HEADS = 4
CHUNK = 64
CONV_GROUP = 64
N_SHARD = 4
SHARD_COLS = 1024
WO_ROWS = 256
EPS = 1e-6
TB = 256
NCB = TB // CHUNK
N_CHUNKS = SEQ // CHUNK
N_DEV = 8
COLLECTIVE_GATHER, COLLECTIVE_MIX_FWD, COLLECTIVE_TAIL = 1, 0, 2
AUX_O, AUX_CV, AUX_B, AUX_COLS = 0, 512, 1024, 1536

ADAM_LR = 0.001
ADAM_B1 = 0.9
ADAM_B2 = 0.999
ADAM_EPS = 1e-08
ADAM_WD = 0.01
ADAM_STEP = 10

VMEM_LIMIT = 56 * 1024 * 1024


def _dot(a, b):
    return jnp.dot(a, b, preferred_element_type=F32)


def _dot_nt(a, b):
    return lax.dot_general(a, b, (((1,), (1,)), ((), ())), preferred_element_type=F32)


def _dot_tn(a, b):
    return lax.dot_general(a, b, (((0,), (0,)), ((), ())), preferred_element_type=F32)


def _split_bf16(x, n):
    parts = []
    r = x
    for _ in range(n):
        p = r.astype(BF16)
        parts.append(p)
        r = r - p.astype(F32)
    return parts


def _exact_left(m, x, n=3):
    acc = None
    for p in _split_bf16(x, n):
        t = _dot(m, p)
        acc = t if acc is None else acc + t
    return acc


def _exact_left_many(m, xs, n=3):
    parts = [_split_bf16(x, n) for x in xs]
    accs = [None] * len(xs)
    for i in range(n):
        for j in range(len(xs)):
            t = _dot(m, parts[j][i])
            accs[j] = t if accs[j] is None else accs[j] + t
    return accs


def _group_mean_many(xs, gmat, n=2):
    parts = [_split_bf16(x, n) for x in xs]
    accs = [None] * len(xs)
    for i in range(n):
        for j in range(len(xs)):
            t = _dot(parts[j][i], gmat)
            accs[j] = t if accs[j] is None else accs[j] + t
    return accs


def _group_mean(x, gmat, n=2):
    w = gmat.shape[0]
    outs = []
    for c0 in range(0, x.shape[1], w):
        acc = None
        for p in _split_bf16(x[:, c0:c0 + w], n):
            t = _dot(p, gmat)
            acc = t if acc is None else acc + t
        outs.append(acc)
    return jnp.concatenate(outs, axis=1)


def _sigmoid(x):
    return 1.0 / (1.0 + jnp.exp(-x))


def _lower_bound(lbl):
    l0 = lbl[0:1, :]
    l1 = lbl[1:2, :]
    m = jnp.maximum(l0, l1)
    e0 = jnp.exp(l0 - m)
    e1 = jnp.exp(l1 - m)
    return e0 / (e0 + e1)


def _tri(lower):
    r = lax.broadcasted_iota(jnp.int32, (CHUNK, CHUNK), 0)
    c = lax.broadcasted_iota(jnp.int32, (CHUNK, CHUNK), 1)
    return jnp.where((c <= r) if lower else (c >= r), 1.0, 0.0).astype(BF16)


def _causal():
    r = lax.broadcasted_iota(jnp.int32, (CHUNK, CHUNK), 0)
    c = lax.broadcasted_iota(jnp.int32, (CHUNK, CHUNK), 1)
    return c <= r


def _shift_down(x, sh, prev_tail):
    r = pltpu.roll(x, sh, 0)
    pt = pltpu.roll(prev_tail, sh, 0)
    rows = lax.broadcasted_iota(jnp.int32, prev_tail.shape, 0)
    top = jnp.where(rows < sh, pt, r[0:8])
    return jnp.concatenate([top, r[8:]], axis=0)


def _shift_up(x, sh, next_head):
    n = x.shape[0]
    r = pltpu.roll(x, n - sh, 0)
    nh = pltpu.roll(next_head, 8 - sh, 0)
    rows = lax.broadcasted_iota(jnp.int32, next_head.shape, 0)
    bot = jnp.where(rows >= 8 - sh, nh, r[n - 8:])
    return jnp.concatenate([r[:n - 8], bot], axis=0)


def _group_matrix(width, group):
    r = np.arange(width)[:, None] // group
    c = np.arange(width)[None, :] // group
    return jnp.asarray(np.where(r == c, 1.0 / group, 0.0), dtype=BF16)


TP = 512
TG = 1024
SEM_W, SEM_CW, SEM_W_FWD, N_SEM = 0, 4, 7, 11


def _gather_proj(kidx, x2d, g1, w_in, conv_w):
    half_w = D_MODEL // 2
    half_c = SHARD_COLS // 2
    nt = SEQ // TG
    n_steps = 2 * N_SHARD

    def body(k_ref, x_ref, g_ref, w_ref, cw_ref, h_ref, p_ref, wg_out, cwg_out,
             wg_v, cwg_v, send_sems, recv_sems, out_sems):
        s, t = pl.program_id(0), pl.program_id(1)
        x, y, c = lax.axis_index("x"), lax.axis_index("y"), lax.axis_index("c")
        k = 2 * x + y
        sibling = (x, y, 1 - c)
        chips = [(1 - x, y), (x, 1 - y), (1 - x, 1 - y)]
        kjs = [2 * cx + cy for cx, cy in chips]
        diag = (*chips[2], c)

        def w_half(kk, cc):
            return wg_v.at[kk, pl.ds(cc * half_w, half_w), :]

        def w_quarter(kk, cc, piece):
            return wg_v.at[kk, pl.ds(cc * half_w, half_w), piece * half_c:(piece + 1) * half_c]

        def cw_of(kk):
            return cwg_v.at[:, pl.ds(pl.multiple_of(kk * HEAD, HEAD), HEAD)]

        def copy(sem, ref, to):
            return pltpu.make_async_remote_copy(
                src_ref=ref, dst_ref=ref, send_sem=send_sems.at[sem], recv_sem=recv_sems.at[sem],
                device_id=to, device_id_type=MESH)

        def at_step(sv, tv):
            return pl.when((s == sv) & (t == tv))

        w_direct = ([copy(SEM_W + j, w_half(k, c), (*chips[j], c)) for j in range(2)]
                    + [copy(SEM_W + 2 + p, w_quarter(k, c, p), diag) for p in range(2)])
        cw_direct = [copy(SEM_CW + j, cw_of(k), (*chip, c)) for j, chip in enumerate(chips)]
        w_passed = ([copy(SEM_W_FWD + j, w_half(kjs[j], c), sibling) for j in range(2)]
                    + [copy(SEM_W_FWD + 2 + p, w_quarter(kjs[2], c, p), sibling) for p in range(2)])
        stores = ([pltpu.make_async_copy(wg_v.at[kk], wg_out.at[kk], out_sems.at[i])
                   for i, kk in enumerate([k] + kjs)]
                  + [pltpu.make_async_copy(cwg_v, cwg_out, out_sems.at[4])])

        @at_step(0, 0)
        def _():
            barrier = pltpu.get_barrier_semaphore()
            for peer in [sibling] + [(*chip, c) for chip in chips]:
                pl.semaphore_signal(barrier, inc=1, device_id=peer, device_id_type=MESH)
            wg_v[k] = w_ref[0].astype(BF16)
            mine = pl.ds(pl.multiple_of(k * HEAD, HEAD), HEAD)
            cwg_v[:, mine] = jnp.zeros((8, HEAD), F32)
            for tap in range(3):
                cwg_v[tap:tap + 1, mine] = cw_ref[:, tap * HEAD:(tap + 1) * HEAD]
            pl.semaphore_wait(barrier, 4)
            w_direct[0].start()
            w_direct[1].start()
            for cp in cw_direct:
                cp.start()
            stores[0].start()

        @at_step(2, 0)
        def _():
            for j in range(2):
                copy(SEM_W + j, w_half(kjs[j], c), sibling).wait_recv()
                w_passed[j].start()
            w_direct[2].start()
            w_direct[3].start()
            copy(SEM_W_FWD, w_half(kjs[0], 1 - c), sibling).wait_recv()
            stores[1].start()

        @at_step(4, 0)
        def _():
            copy(SEM_W_FWD + 1, w_half(kjs[1], 1 - c), sibling).wait_recv()
            stores[2].start()

        for p in range(2):
            @at_step(6 + p, 0)
            def _(p=p):
                copy(SEM_W + 2 + p, w_quarter(kjs[2], c, p), sibling).wait_recv()
                w_passed[2 + p].start()
                copy(SEM_W_FWD + 2 + p, w_quarter(kjs[2], 1 - c, p), sibling).wait_recv()

        rows = pl.ds(pl.multiple_of(t * TG, TG), TG)

        @pl.when(s == 0)
        def _():
            xv = x_ref[...]
            r = lax.rsqrt(jnp.mean(xv * xv, axis=-1, keepdims=True) + EPS)
            h_ref[rows, :] = (xv * r * g_ref[...]).astype(BF16)

        sh = s >> 1
        js = k ^ (((sh & 1) << 1) | (sh >> 1))
        for piece in range(2):
            @pl.when((s & 1) == piece)
            def _(piece=piece):
                p_ref[...] = _dot(h_ref[rows, :], wg_v[js, :, piece * half_c:(piece + 1) * half_c])

        @at_step(n_steps - 1, nt - 1)
        def _():
            stores[3].start()
            for j in range(3):
                copy(SEM_CW + j, cw_of(kjs[j]), sibling).wait_recv()
            stores[4].start()
            for cp in w_direct + cw_direct + w_passed:
                cp.wait_send()
            for st in stores:
                st.wait()

    def x_map(s, t, kr):
        return (jnp.where(s == 0, t, nt - 1), 0)

    def p_map(s, t, kr):
        sh = s >> 1
        return (t, 2 * (kr[0] ^ (((sh & 1) << 1) | (sh >> 1))) + (s & 1))

    hbm = pl.BlockSpec(memory_space=pl.ANY)
    grid_spec = pltpu.PrefetchScalarGridSpec(
        num_scalar_prefetch=1, grid=(n_steps, nt),
        in_specs=[pl.BlockSpec((TG, D_MODEL), x_map),
                  pl.BlockSpec((1, D_MODEL), lambda s, t, kr: (0, 0)),
                  pl.BlockSpec((1, D_MODEL, SHARD_COLS), lambda s, t, kr: (0, 0, 0)),
                  pl.BlockSpec((1, 3 * HEAD), lambda s, t, kr: (0, 0))],
        out_specs=(pl.BlockSpec((SEQ, D_MODEL), lambda s, t, kr: (0, 0)),
                   pl.BlockSpec((TG, half_c), p_map), hbm, hbm),
        scratch_shapes=[pltpu.VMEM((N_SHARD, D_MODEL, SHARD_COLS), BF16),
                        pltpu.VMEM((8, D_CONV), F32),
                        pltpu.SemaphoreType.DMA((N_SEM,)), pltpu.SemaphoreType.DMA((N_SEM,)),
                        pltpu.SemaphoreType.DMA((5,))])
    return pl.pallas_call(
        body, name="gather_proj", grid_spec=grid_spec,
        out_shape=(jax.ShapeDtypeStruct((SEQ, D_MODEL), BF16),
                   jax.ShapeDtypeStruct((SEQ, N_SHARD * SHARD_COLS), F32),
                   jax.ShapeDtypeStruct((N_SHARD, D_MODEL, SHARD_COLS), BF16),
                   jax.ShapeDtypeStruct((8, D_CONV), F32)),
        compiler_params=pltpu.CompilerParams(dimension_semantics=("arbitrary", "arbitrary"),
                                             vmem_limit_bytes=VMEM_LIMIT, collective_id=COLLECTIVE_GATHER),
    )(kidx, x2d, g1, w_in, conv_w)


def _mix_fwd(proj, lb_logits, cw, ga, gcn, g64, w_out):
    half_o = WO_ROWS // 2
    nblk = SEQ // TB

    def body(p_ref, lbl_ref, cw_ref, ga_ref, gcn_ref, g64_ref, wo_ref,
             mixed_ref, aux_ref, sto_ref, wog_out,
             st_ref, tail_ref, wog_v, send_sems, recv_sems, out_sem):
        i = pl.program_id(0)
        x, y, c = lax.axis_index("x"), lax.axis_index("y"), lax.axis_index("c")
        k = 2 * x + y
        sibling = (x, y, 1 - c)
        chips = [(1 - x, y), (x, 1 - y), (1 - x, 1 - y)]
        kjs = [2 * cx + cy for cx, cy in chips]

        def wo_half(kk, cc):
            return wog_v.at[kk, pl.ds(cc * half_o, half_o), :]

        def copy(sem, ref, to):
            return pltpu.make_async_remote_copy(
                src_ref=ref, dst_ref=ref, send_sem=send_sems.at[sem], recv_sem=recv_sems.at[sem],
                device_id=to, device_id_type=MESH)

        wo_direct = [copy(j, wo_half(k, c), (*chip, c)) for j, chip in enumerate(chips)]
        wo_passed = [copy(3 + j, wo_half(kj, c), sibling) for j, kj in enumerate(kjs)]
        wo_store = pltpu.make_async_copy(wog_v, wog_out, out_sem.at[0])

        @pl.when(i == 0)
        def _():
            barrier = pltpu.get_barrier_semaphore()
            for peer in [sibling] + [(*chip, c) for chip in chips]:
                pl.semaphore_signal(barrier, inc=1, device_id=peer, device_id_type=MESH)
            st_ref[...] = jnp.zeros_like(st_ref)
            tail_ref[...] = jnp.zeros_like(tail_ref)
            wog_v[k] = wo_ref[0].astype(BF16)
            pl.semaphore_wait(barrier, 4)
            for cp in wo_direct:
                cp.start()

        @pl.when(i == nblk - 2)
        def _():
            for j in range(3):
                copy(j, wo_half(kjs[j], c), sibling).wait_recv()
                wo_passed[j].start()

        lb = _lower_bound(lbl_ref[...])
        tri = _tri(True)
        causal = _causal()
        g64m = g64_ref[...]
        heads = range(N_HEADS)
        cs = [slice(hd * HEAD, (hd + 1) * HEAD) for hd in heads]
        col = lambda base, hd: slice(base + hd * HEAD, base + (hd + 1) * HEAD)
        for n in range(NCB):
            sl = pl.ds(n * CHUNK, CHUNK)
            sg = [_sigmoid(p_ref[sl, col(512, hd)]) for hd in heads]
            f = [lb[:, cs[hd]] + (1.0 - lb[:, cs[hd]]) * sg[hd] for hd in heads]
            bc = _exact_left_many(tri, [jnp.log(f[hd]) for hd in heads])
            for hd in heads:
                aux_ref[sl, col(AUX_B, hd)] = bc[hd]
            g = [bc[hd][CHUNK - 1:CHUNK, :] for hd in heads]
            qd = [(p_ref[sl, col(0, hd)] * jnp.exp(bc[hd])).astype(BF16) for hd in heads]
            ki = [((1.0 - f[hd]) * jnp.exp(-bc[hd])).astype(BF16) for hd in heads]
            ke = [((1.0 - f[hd]) * jnp.exp(g[hd] - bc[hd])).astype(BF16) for hd in heads]
            vb = [p_ref[sl, col(1024, hd)].astype(BF16) for hd in heads]
            st = [st_ref[hd] for hd in heads]
            st_b = [a.astype(BF16) for a in st]
            for hd in heads:
                sto_ref[n, hd] = st_b[hd]
            scm = [_dot_nt(qd[hd], ki[hd]) for hd in heads]
            inter = [_dot_nt(qd[hd], st_b[hd]) for hd in heads]
            upd = [_dot_tn(vb[hd], ke[hd]) for hd in heads]
            intra = [_dot(jnp.where(causal, scm[hd], 0.0).astype(BF16), vb[hd]) for hd in heads]
            for hd in heads:
                st_ref[hd] = st[hd] * jnp.exp(g[hd]) + upd[hd]
                o = intra[hd] + inter[hd]
                aux_ref[sl, col(AUX_O, hd)] = o
                ra = lax.rsqrt(jnp.mean(o * o, axis=-1, keepdims=True) + EPS)
                za = p_ref[sl, col(1536, hd)]
                mixed_ref[sl, cs[hd]] = (o * ra * ga_ref[:, cs[hd]] * (za * _sigmoid(za))).astype(BF16)
            yb = []
            for hd in heads:
                cu = p_ref[sl, col(3072, hd)] * p_ref[sl, col(2048, hd)]
                tail = tail_ref[:, cs[hd]]
                cv = (cw_ref[0:1, cs[hd]] * _shift_down(cu, 2, tail) + cw_ref[1:2, cs[hd]] * _shift_down(cu, 1, tail)
                      + cw_ref[2:3, cs[hd]] * cu)
                tail_ref[:, cs[hd]] = cu[CHUNK - 8:, :]
                aux_ref[sl, col(AUX_CV, hd)] = cv
                yb.append(p_ref[sl, col(2560, hd)] * cv)
            ms = _group_mean_many([y * y for y in yb], g64m)
            for hd in heads:
                rb = lax.rsqrt(ms[hd] + EPS)
                zb = p_ref[sl, col(3584, hd)]
                mixed_ref[sl, col(512, hd)] = (yb[hd] * rb * gcn_ref[:, cs[hd]] * (zb * _sigmoid(zb))).astype(BF16)

        @pl.when(i == nblk - 1)
        def _():
            for j in range(3):
                copy(3 + j, wo_half(kjs[j], 1 - c), sibling).wait_recv()
            wo_store.start()
            for cp in wo_direct + wo_passed:
                cp.wait_send()
            wo_store.wait()

    row = lambda w: pl.BlockSpec((1, w), lambda i: (0, 0))
    return pl.pallas_call(
        body, name="mix_fwd", grid=(nblk,),
        out_shape=(jax.ShapeDtypeStruct((SEQ, D_MODEL), BF16),
                   jax.ShapeDtypeStruct((SEQ, AUX_COLS), F32),
                   jax.ShapeDtypeStruct((N_CHUNKS, N_HEADS, HEAD, HEAD), BF16),
                   jax.ShapeDtypeStruct((N_SHARD, WO_ROWS, D_MODEL), BF16)),
        in_specs=[pl.BlockSpec((TB, 4096), lambda i: (i, 0)),
                  pl.BlockSpec((2, D_HGRN), lambda i: (0, 0)),
                  pl.BlockSpec((8, D_CONV), lambda i: (0, 0)),
                  row(D_HGRN), row(D_CONV),
                  pl.BlockSpec((HEAD, HEAD), lambda i: (0, 0)),
                  pl.BlockSpec((1, WO_ROWS, D_MODEL), lambda i: (0, 0, 0))],
        out_specs=(pl.BlockSpec((TB, D_MODEL), lambda i: (i, 0)),
                   pl.BlockSpec((TB, AUX_COLS), lambda i: (i, 0)),
                   pl.BlockSpec((NCB, N_HEADS, HEAD, HEAD), lambda i: (i, 0, 0, 0)),
                   pl.BlockSpec(memory_space=pl.ANY)),
        scratch_shapes=[pltpu.VMEM((N_HEADS, HEAD, HEAD), F32), pltpu.VMEM((8, D_CONV), F32),
                        pltpu.VMEM((N_SHARD, WO_ROWS, D_MODEL), BF16),
                        pltpu.SemaphoreType.DMA((6,)), pltpu.SemaphoreType.DMA((6,)),
                        pltpu.SemaphoreType.DMA((1,))],
        compiler_params=pltpu.CompilerParams(dimension_semantics=("arbitrary",), vmem_limit_bytes=VMEM_LIMIT,
                                             collective_id=COLLECTIVE_MIX_FWD),
    )(proj, lb_logits, cw, ga, gcn, g64, w_out)


def _out_loss(x2d, mixed, wog, gf, tgt):
    def body(x_ref, m_ref, wo_ref, gf_ref, t_ref, dx2_ref, dm_ref, gwo_ref, part_ref, acc_ref):
        i = pl.program_id(0)

        @pl.when(i == 0)
        def _():
            acc_ref[...] = jnp.zeros_like(acc_ref)
            part_ref[...] = jnp.zeros_like(part_ref)

        mixed_b = m_ref[...]
        x2 = x_ref[...] + _dot(mixed_b, wo_ref[...])
        r2 = lax.rsqrt(jnp.mean(x2 * x2, axis=-1, keepdims=True) + EPS)
        n2 = x2 * r2
        gfv = gf_ref[...]
        err = n2 * gfv - t_ref[...]
        loss = 0.5 * jnp.sum(jnp.mean(err * err, axis=-1, keepdims=True), axis=0, keepdims=True)
        dy = err * (1.0 / D_MODEL)
        part_ref[1:2, :] += jnp.sum(dy * n2, axis=0, keepdims=True)
        part_ref[7:8, :] += jnp.broadcast_to(loss, (1, D_MODEL))
        dn = dy * gfv
        dx2 = r2 * (dn - n2 * jnp.mean(dn * n2, axis=-1, keepdims=True))
        dx2_ref[...] = dx2
        dx2_b = dx2.astype(BF16)
        dm_ref[...] = _dot_nt(dx2_b, wo_ref[...])
        acc_ref[...] += _dot_tn(mixed_b, dx2_b)

        @pl.when(i == pl.num_programs(0) - 1)
        def _():
            gwo_ref[...] = acc_ref[...].astype(BF16)

    blk = lambda: pl.BlockSpec((TP, D_MODEL), lambda i: (i, 0))
    return pl.pallas_call(
        body, name="out_loss", grid=(SEQ // TP,),
        out_shape=(jax.ShapeDtypeStruct((SEQ, D_MODEL), F32),
                   jax.ShapeDtypeStruct((SEQ, D_MODEL), F32),
                   jax.ShapeDtypeStruct((D_MODEL, D_MODEL), BF16),
                   jax.ShapeDtypeStruct((8, D_MODEL), F32)),
        in_specs=[blk(), blk(), pl.BlockSpec((D_MODEL, D_MODEL), lambda i: (0, 0)),
                  pl.BlockSpec((1, D_MODEL), lambda i: (0, 0)), blk()],
        out_specs=(blk(), blk(), pl.BlockSpec((D_MODEL, D_MODEL), lambda i: (0, 0)),
                   pl.BlockSpec((8, D_MODEL), lambda i: (0, 0))),
        scratch_shapes=[pltpu.VMEM((D_MODEL, D_MODEL), F32)],
        compiler_params=pltpu.CompilerParams(dimension_semantics=("arbitrary",), vmem_limit_bytes=VMEM_LIMIT),
    )(x2d, mixed, wog, gf, tgt)


def _mix_bwd(proj, aux, states, dmixed, lb_logits, cw, ga, gcn, g64):
    nblk = SEQ // TB

    def body(p_ref, aux_ref, st_ref, dm_ref, lbl_ref, cw_ref, ga_ref, gcn_ref, g64_ref,
             dp_ref, part_ref, dst_ref, head_ref, dlb_ref):
        i = pl.program_id(0)

        @pl.when(i == 0)
        def _():
            dst_ref[...] = jnp.zeros_like(dst_ref)
            head_ref[...] = jnp.zeros_like(head_ref)
            part_ref[...] = jnp.zeros_like(part_ref)
            dlb_ref[...] = jnp.zeros_like(dlb_ref)

        lb = _lower_bound(lbl_ref[...])
        triu = _tri(False)
        causal = _causal()
        g64m = g64_ref[...]
        rowsum = lambda a: jnp.sum(a, axis=0, keepdims=True)
        heads = range(N_HEADS)
        cs = [slice(hd * HEAD, (hd + 1) * HEAD) for hd in heads]
        col = lambda base, hd: slice(base + hd * HEAD, base + (hd + 1) * HEAD)
        for n in reversed(range(NCB)):
            sl = pl.ds(n * CHUNK, CHUNK)
            cvv = [aux_ref[sl, col(AUX_CV, hd)] for hd in heads]
            gb = [p_ref[sl, col(2560, hd)] for hd in heads]
            yb = [gb[hd] * cvv[hd] for hd in heads]
            ms = _group_mean_many([y * y for y in yb], g64m)
            rb, nb, dnb = [], [], []
            for hd in heads:
                rb.append(lax.rsqrt(ms[hd] + EPS))
                nb.append(yb[hd] * rb[hd])
                zb = p_ref[sl, col(3584, hd)]
                sgb = _sigmoid(zb)
                dmb = dm_ref[sl, col(512, hd)]
                gcv = gcn_ref[:, cs[hd]]
                part_ref[2:3, col(512, hd)] += rowsum(dmb * nb[hd] * (zb * sgb))
                dp_ref[sl, col(3584, hd)] = (dmb * nb[hd] * gcv * (sgb * (1.0 + zb * (1.0 - sgb)))).astype(BF16)
                dnb.append(dmb * gcv * (zb * sgb))
            mdn = _group_mean_many([dnb[hd] * nb[hd] for hd in heads], g64m)
            for hd in heads:
                dyb = rb[hd] * (dnb[hd] - nb[hd] * mdn[hd])
                dp_ref[sl, col(2560, hd)] = (dyb * cvv[hd]).astype(BF16)
                dcv = dyb * gb[hd]
                head = head_ref[:, cs[hd]]
                dcv1 = _shift_up(dcv, 1, head)
                dcv2 = _shift_up(dcv, 2, head)
                head_ref[:, cs[hd]] = dcv[0:8, :]
                u = p_ref[sl, col(2048, hd)]
                gc = p_ref[sl, col(3072, hd)]
                cu = gc * u
                part_ref[4:5, cs[hd]] += rowsum(dcv2 * cu)
                part_ref[5:6, cs[hd]] += rowsum(dcv1 * cu)
                part_ref[6:7, cs[hd]] += rowsum(dcv * cu)
                dcu = cw_ref[2:3, cs[hd]] * dcv + cw_ref[1:2, cs[hd]] * dcv1 + cw_ref[0:1, cs[hd]] * dcv2
                dp_ref[sl, col(3072, hd)] = (dcu * u).astype(BF16)
                dp_ref[sl, col(2048, hd)] = (dcu * gc).astype(BF16)
            do_b = []
            for hd in heads:
                ov = aux_ref[sl, col(AUX_O, hd)]
                ra = lax.rsqrt(jnp.mean(ov * ov, axis=-1, keepdims=True) + EPS)
                na = ov * ra
                za = p_ref[sl, col(1536, hd)]
                sga = _sigmoid(za)
                dma = dm_ref[sl, cs[hd]]
                gav = ga_ref[:, cs[hd]]
                part_ref[2:3, cs[hd]] += rowsum(dma * na * (za * sga))
                dp_ref[sl, col(1536, hd)] = (dma * na * gav * (sga * (1.0 + za * (1.0 - sga)))).astype(BF16)
                dna = dma * gav * (za * sga)
                do_b.append((ra * (dna - na * jnp.mean(dna * na, axis=-1, keepdims=True))).astype(BF16))
            s = [_sigmoid(p_ref[sl, col(512, hd)]) for hd in heads]
            f = [lb[:, cs[hd]] + (1.0 - lb[:, cs[hd]]) * s[hd] for hd in heads]
            bc = [aux_ref[sl, col(AUX_B, hd)] for hd in heads]
            g = [bc[hd][CHUNK - 1:CHUNK, :] for hd in heads]
            eb = [jnp.exp(bc[hd]) for hd in heads]
            enb = [jnp.exp(-bc[hd]) for hd in heads]
            eg = [jnp.exp(g[hd] - bc[hd]) for hd in heads]
            dec = [jnp.exp(g[hd]) for hd in heads]
            qd = [p_ref[sl, cs[hd]] * eb[hd] for hd in heads]
            ki = [(1.0 - f[hd]) * enb[hd] for hd in heads]
            ke = [(1.0 - f[hd]) * eg[hd] for hd in heads]
            qd_b = [a.astype(BF16) for a in qd]
            ki_b = [a.astype(BF16) for a in ki]
            ke_b = [a.astype(BF16) for a in ke]
            vb = [p_ref[sl, col(1024, hd)].astype(BF16) for hd in heads]
            st_b = [st_ref[n, hd] for hd in heads]
            dst = [dst_ref[hd] for hd in heads]
            dst_b = [a.astype(BF16) for a in dst]
            scm = [_dot_nt(qd_b[hd], ki_b[hd]) for hd in heads]
            amm = [_dot_nt(do_b[hd], vb[hd]) for hd in heads]
            dqd2 = [_dot(do_b[hd], st_b[hd]) for hd in heads]
            dke = [_dot(vb[hd], dst_b[hd]) for hd in heads]
            dv2 = [_dot_nt(ke_b[hd], dst_b[hd]) for hd in heads]
            dsu = [_dot_tn(do_b[hd], qd_b[hd]) for hd in heads]
            sc = [jnp.where(causal, scm[hd], 0.0).astype(BF16) for hd in heads]
            am = [jnp.where(causal, amm[hd], 0.0).astype(BF16) for hd in heads]
            dqd1 = [_dot(am[hd], ki_b[hd]) for hd in heads]
            dki = [_dot_tn(am[hd], qd_b[hd]) for hd in heads]
            dv1 = [_dot_tn(sc[hd], do_b[hd]) for hd in heads]
            db, dgv = [], []
            for hd in heads:
                dqd = dqd1[hd] + dqd2[hd]
                ddec = rowsum(dst[hd] * st_b[hd].astype(F32))
                dst_ref[hd] = dst[hd] * dec[hd] + dsu[hd]
                dp_ref[sl, cs[hd]] = (dqd * eb[hd]).astype(BF16)
                dp_ref[sl, col(1024, hd)] = (dv1[hd] + dv2[hd]).astype(BF16)
                db.append(dqd * qd[hd] - dki[hd] * ki[hd] - dke[hd] * ke[hd])
                dgv.append(rowsum(dke[hd] * ke[hd]) + ddec * dec[hd])
            rc = _exact_left_many(triu, db, 2)
            for hd in heads:
                df = (rc[hd] + dgv[hd]) / f[hd] - (dki[hd] * enb[hd] + dke[hd] * eg[hd])
                dlb_ref[:, cs[hd]] += rowsum(df * (1.0 - s[hd]))
                dp_ref[sl, col(512, hd)] = (df * (1.0 - lb[:, cs[hd]]) * s[hd] * (1.0 - s[hd])).astype(BF16)

        @pl.when(i == nblk - 1)
        def _():
            row = dlb_ref[...] * lb * (1.0 - lb)
            part_ref[3:4, 0:D_HGRN] = row
            part_ref[3:4, D_HGRN:] = -row

    rev = lambda w: pl.BlockSpec((TB, w), lambda i: (nblk - 1 - i, 0))
    row = lambda w: pl.BlockSpec((1, w), lambda i: (0, 0))
    return pl.pallas_call(
        body, name="mix_bwd", grid=(nblk,),
        out_shape=(jax.ShapeDtypeStruct((SEQ, 4096), BF16),
                   jax.ShapeDtypeStruct((8, D_MODEL), F32)),
        in_specs=[rev(4096), rev(AUX_COLS),
                  pl.BlockSpec((NCB, N_HEADS, HEAD, HEAD), lambda i: (nblk - 1 - i, 0, 0, 0)),
                  rev(D_MODEL),
                  pl.BlockSpec((2, D_HGRN), lambda i: (0, 0)),
                  pl.BlockSpec((8, D_CONV), lambda i: (0, 0)),
                  row(D_HGRN), row(D_CONV),
                  pl.BlockSpec((HEAD, HEAD), lambda i: (0, 0))],
        out_specs=(rev(4096), pl.BlockSpec((8, D_MODEL), lambda i: (0, 0))),
        scratch_shapes=[pltpu.VMEM((N_HEADS, HEAD, HEAD), F32), pltpu.VMEM((8, D_CONV), F32),
                        pltpu.VMEM((1, D_HGRN), F32)],
        compiler_params=pltpu.CompilerParams(dimension_semantics=("arbitrary",), vmem_limit_bytes=VMEM_LIMIT),
    )(proj, aux, states, dmixed, lb_logits, cw, ga, gcn, g64)


TT = 1024
TX = 256
(SEM_D2D, SEM_D2D_O, SEM_ICI, SEM_ICI_O, SEM_FIN, SEM_FIN_O, SEM_SMALL, N_SEM_TAIL) = 0, 4, 5, 8, 11, 12, 12, 20


def _bwd_tail(kidx, h, dproj, wg, gwo, x2d, dx2, g1, small_a, small_b):
    hw = D_MODEL // 2
    ho = WO_ROWS // 2
    nt = SEQ // TT
    n_steps = N_SHARD + SEQ // TX // nt

    def body(k_ref, h_ref, dp_ref, w_ref, gwo_ref, x_ref, dx2_ref, g_ref, sm_ref, smb_ref,
             gx_ref, gw_out, gwo_out, osm_ref,
             acc, dh, sendbuf, keep, sibrcv, rcv, sib_o, p_o, rcv_o, res_o, sm_buf, dng,
             send_sems, recv_sems, out_sems):
        s, t = pl.program_id(0), pl.program_id(1)
        x, y, c = lax.axis_index("x"), lax.axis_index("y"), lax.axis_index("c")
        k = 2 * x + y
        me = 4 * x + 2 * y + c
        sibling = (x, y, 1 - c)
        chips = [(1 - x, 1 - y), (1 - x, y), (x, 1 - y)]
        kjs = [2 * cx + cy for cx, cy in chips]
        mine = pl.ds(pl.multiple_of(c * hw, hw), hw)
        other = pl.ds(pl.multiple_of((1 - c) * hw, hw), hw)
        mine_o = pl.ds(pl.multiple_of(c * ho, ho), ho)
        other_o = pl.ds(pl.multiple_of((1 - c) * ho, ho), ho)

        def copy(sem, src, dst, to):
            return pltpu.make_async_remote_copy(
                src_ref=src, dst_ref=dst, send_sem=send_sems.at[sem], recv_sem=recv_sems.at[sem],
                device_id=to, device_id_type=MESH)

        def at_step(sv, tv):
            return pl.when((s == sv) & (t == tv))

        def at_norm_block(b):
            return at_step(N_SHARD + b // nt, b % nt)

        d2d = [copy(SEM_D2D + sv, sendbuf.at[sv], sibrcv.at[sv], sibling) for sv in range(N_SHARD)]
        d2d_o = copy(SEM_D2D_O, gwo_ref.at[:, other_o, :], sib_o, sibling)
        ici = [copy(SEM_ICI + sv, keep.at[sv], rcv.at[sv], (*chips[sv], c)) for sv in range(3)]
        ici_o = [copy(SEM_ICI_O + sv, p_o.at[kjs[sv]], rcv_o.at[sv], (*chips[sv], c)) for sv in range(3)]
        fin = copy(SEM_FIN, acc.at[mine, :], gw_out.at[mine, :], sibling)
        fin_o = copy(SEM_FIN_O, res_o.at[mine_o, :], res_o.at[mine_o, :], sibling)
        smalls = [copy(SEM_SMALL + m, sm_buf.at[me], sm_buf.at[me],
                       (x ^ (m >> 2), y ^ ((m >> 1) & 1), c ^ (m & 1))) for m in range(1, N_DEV)]
        store_w = pltpu.make_async_copy(acc.at[mine, :], gw_out.at[mine, :], out_sems.at[0])
        store_o = pltpu.make_async_copy(res_o, gwo_out, out_sems.at[1])

        @at_step(0, 0)
        def _():
            barrier = pltpu.get_barrier_semaphore()
            for m in range(1, N_DEV):
                pl.semaphore_signal(barrier, inc=1, device_id=(x ^ (m >> 2), y ^ ((m >> 1) & 1), c ^ (m & 1)),
                                    device_id_type=MESH)
            pl.semaphore_wait(barrier, N_DEV - 1)
            d2d_o.start()

        @at_step(0, 1)
        def _():
            d2d_o.wait_recv()
            for j in range(N_SHARD):
                p_o[j] = (gwo_ref[j, mine_o, :].astype(F32) + sib_o[j].astype(F32)).astype(BF16)
            res_o[mine_o, :] = gwo_ref[k, mine_o, :].astype(F32) + sib_o[k].astype(F32)
            for cp in ici_o:
                cp.start()

        rows = pl.ds(pl.multiple_of(t * TT, TT), TT)

        @pl.when(s < N_SHARD)
        def _():
            dpb = dp_ref[...]
            part = _dot_tn(h_ref[...], dpb)

            @pl.when(t == 0)
            def _():
                acc[...] = part

            @pl.when(t > 0)
            def _():
                acc[...] += part

            d = _dot_nt(dpb, w_ref[0])

            @pl.when(s == 0)
            def _():
                dh[rows, :] = d

            @pl.when(s > 0)
            def _():
                dh[rows, :] += d

        for sv in range(N_SHARD):
            @at_step(sv, nt - 1)
            def _(sv=sv):
                sendbuf[sv] = acc[other, :].astype(BF16)
                if sv < 3:
                    keep[sv] = acc[mine, :].astype(BF16)
                d2d[sv].start()

        for sv in range(3):
            @at_step(sv + 1, 0)
            def _(sv=sv):
                d2d[sv].wait_recv()
                keep[sv] = (keep[sv].astype(F32) + sibrcv[sv].astype(F32)).astype(BF16)
                ici[sv].start()

        @at_norm_block(0)
        def _():
            d2d[3].wait_recv()
            ici[0].wait_recv()
            acc[mine, :] += sibrcv[3].astype(F32) + rcv[0].astype(F32)

        @at_norm_block(1)
        def _():
            tot = res_o[mine_o, :]
            for sv in range(3):
                ici_o[sv].wait_recv()
                tot = tot + rcv_o[sv].astype(F32)
            res_o[mine_o, :] = tot
            fin_o.start()

        @at_norm_block(2)
        def _():
            ici[1].wait_recv()
            acc[mine, :] += rcv[1].astype(F32)

        @at_norm_block(0)
        def _():
            dng[...] = jnp.zeros_like(dng)

        @pl.when(s >= N_SHARD)
        def _():
            blk = (s - N_SHARD) * nt + t
            dhv = dh[pl.ds(pl.multiple_of(blk * TX, TX), TX), :]
            xv = x_ref[...]
            r = lax.rsqrt(jnp.mean(xv * xv, axis=-1, keepdims=True) + EPS)
            xn = xv * r
            dng[...] += jnp.sum(dhv * xn, axis=0, keepdims=True)
            dxn = dhv * g_ref[...]
            gx_ref[...] = dx2_ref[...] + r * (dxn - xn * jnp.mean(dxn * xn, axis=-1, keepdims=True))

        @at_step(n_steps - 1, nt - 1)
        def _():
            sm_buf[me] = sm_ref[...] + smb_ref[...]
            sm_buf[me, 0:1, :] = dng[...]
            for cp in smalls:
                cp.start()
            ici[2].wait_recv()
            acc[mine, :] += rcv[2].astype(F32)
            fin.start()
            store_w.start()
            for m in range(1, N_DEV):
                copy(SEM_SMALL + m, sm_buf.at[0], sm_buf.at[0], sibling).wait_recv()
            tot = sm_buf[0]
            for d in range(1, N_DEV):
                tot = tot + sm_buf[d]
            osm_ref[...] = tot
            fin_o.wait_recv()
            store_o.start()
            fin.wait_recv()
            for cp in d2d + [d2d_o] + ici + ici_o + [fin, fin_o] + smalls:
                cp.wait_send()
            store_o.wait()
            store_w.wait()

    def shard_of(s, kr):
        return kr[0] ^ (3 - jnp.minimum(s, 3))

    def tok(s, t):
        return jnp.where(s < N_SHARD, t, nt - 1)

    def blk_map(s, t, kr):
        return (jnp.where(s < N_SHARD, 0, (s - N_SHARD) * nt + t), 0)

    hbm = pl.BlockSpec(memory_space=pl.ANY)
    grid_spec = pltpu.PrefetchScalarGridSpec(
        num_scalar_prefetch=1, grid=(n_steps, nt),
        in_specs=[pl.BlockSpec((TT, D_MODEL), lambda s, t, kr: (tok(s, t), 0)),
                  pl.BlockSpec((TT, SHARD_COLS), lambda s, t, kr: (tok(s, t), shard_of(s, kr))),
                  pl.BlockSpec((1, D_MODEL, SHARD_COLS), lambda s, t, kr: (shard_of(s, kr), 0, 0)),
                  pl.BlockSpec((N_SHARD, WO_ROWS, D_MODEL), lambda s, t, kr: (0, 0, 0)),
                  pl.BlockSpec((TX, D_MODEL), blk_map),
                  pl.BlockSpec((TX, D_MODEL), blk_map),
                  pl.BlockSpec((1, D_MODEL), lambda s, t, kr: (0, 0)),
                  pl.BlockSpec((8, D_MODEL), lambda s, t, kr: (0, 0)),
                  pl.BlockSpec((8, D_MODEL), lambda s, t, kr: (0, 0))],
        out_specs=(pl.BlockSpec((TX, D_MODEL), blk_map), hbm, hbm,
                   pl.BlockSpec((8, D_MODEL), lambda s, t, kr: (0, 0))),
        scratch_shapes=[pltpu.VMEM((D_MODEL, SHARD_COLS), F32), pltpu.VMEM((SEQ, D_MODEL), F32),
                        pltpu.VMEM((N_SHARD, hw, SHARD_COLS), BF16), pltpu.VMEM((3, hw, SHARD_COLS), BF16),
                        pltpu.VMEM((N_SHARD, hw, SHARD_COLS), BF16), pltpu.VMEM((3, hw, SHARD_COLS), BF16),
                        pltpu.VMEM((N_SHARD, ho, D_MODEL), BF16), pltpu.VMEM((N_SHARD, ho, D_MODEL), BF16),
                        pltpu.VMEM((3, ho, D_MODEL), BF16), pltpu.VMEM((WO_ROWS, D_MODEL), F32),
                        pltpu.VMEM((N_DEV, 8, D_MODEL), F32), pltpu.VMEM((1, D_MODEL), F32),
                        pltpu.SemaphoreType.DMA((N_SEM_TAIL,)), pltpu.SemaphoreType.DMA((N_SEM_TAIL,)),
                        pltpu.SemaphoreType.DMA((2,))])
    return pl.pallas_call(
        body, name="bwd_tail", grid_spec=grid_spec,
        out_shape=(jax.ShapeDtypeStruct((SEQ, D_MODEL), F32),
                   jax.ShapeDtypeStruct((D_MODEL, SHARD_COLS), F32),
                   jax.ShapeDtypeStruct((WO_ROWS, D_MODEL), F32),
                   jax.ShapeDtypeStruct((8, D_MODEL), F32)),
        compiler_params=pltpu.CompilerParams(dimension_semantics=("arbitrary", "arbitrary"),
                                             vmem_limit_bytes=60 * 1024 * 1024, collective_id=COLLECTIVE_TAIL),
    )(kidx, h, dproj, wg, gwo, x2d, dx2, g1, small_a, small_b)


def _adam_update(w, g, m, v):
    nm = ADAM_B1 * m + (1.0 - ADAM_B1) * g
    nv = ADAM_B2 * v + (1.0 - ADAM_B2) * (g * g)
    m_hat = nm / (1.0 - ADAM_B1 ** ADAM_STEP)
    v_hat = nv / (1.0 - ADAM_B2 ** ADAM_STEP)
    return -ADAM_LR * (m_hat / (jnp.sqrt(v_hat) + ADAM_EPS) + ADAM_WD * w), nm, nv


def _adamw_all(tot, g_w_in, g_w_out, big, small, grad_x):
    n = len(small)
    rows = WO_ROWS
    steps = D_MODEL // rows

    def body(tot_ref, *refs):
        gx_ref, gx_out = refs[2 + 3 * (2 + n)], refs[-1]
        gx_out[...] = gx_ref[...]
        ins, outs = refs[:2 + 3 * (2 + n)], refs[3 + 3 * (2 + n):-1]
        g_refs, wmv = ins[:2], ins[2:]
        loss_ref, quads = outs[0], outs[1:]

        def update(j, g):
            w_ref, m_ref, v_ref = wmv[3 * j:3 * j + 3]
            g_ref, d_ref, nm_ref, nv_ref = quads[4 * j:4 * j + 4]
            g_ref[...] = g
            d_ref[...], nm_ref[...], nv_ref[...] = _adam_update(w_ref[...], g, m_ref[...], v_ref[...])

        update(0, g_refs[0][...])

        @pl.when(pl.program_id(0) == 0)
        def _():
            update(1, g_refs[1][...])
            k = 2 * lax.axis_index("x") + lax.axis_index("y")
            mine = pl.ds(pl.multiple_of(k * HEAD, HEAD), HEAD)
            loss_ref[...] = tot_ref[7:8, 0:1]
            grads = [tot_ref[0:1, :], tot_ref[1:2, :], tot_ref[2:3, 0:D_HGRN], tot_ref[2:3, D_HGRN:],
                     jnp.concatenate([tot_ref[3:4, 0:D_HGRN], tot_ref[3:4, D_HGRN:]], axis=0),
                     jnp.concatenate([tot_ref[4 + tap:5 + tap, mine] for tap in range(3)], axis=1)]
            for j, g in enumerate(grads):
                update(2 + j, g)

    whole = lambda a: pl.BlockSpec(a.shape, lambda i: (0, 0))
    blk = pl.BlockSpec((rows, SHARD_COLS), lambda i: (i, 0))
    arrays = [a for triple in big + small for a in triple]
    in_specs = ([whole(tot), blk, whole(g_w_out)] + [blk] * 3 + [whole(a) for a in arrays[3:]])
    shapes = [big[0][0], big[1][0]] + [w for w, _, _ in small]
    out_shape = (jax.ShapeDtypeStruct((1, 1), F32),) + tuple(
        jax.ShapeDtypeStruct(w.shape, F32) for w in shapes for _ in range(4))
    out_specs = (pl.BlockSpec((1, 1), lambda i: (0, 0)),) + (blk,) * 4 + tuple(
        whole(w) for w in shapes[1:] for _ in range(4))
    gx_blk = pl.BlockSpec((SEQ // steps, D_MODEL), lambda i: (i, 0))
    outs = pl.pallas_call(
        body, name="adamw_all", grid=(steps,),
        out_shape=out_shape + (jax.ShapeDtypeStruct(grad_x.shape, F32),),
        in_specs=in_specs + [gx_blk], out_specs=out_specs + (gx_blk,),
        compiler_params=pltpu.CompilerParams(dimension_semantics=("arbitrary",), vmem_limit_bytes=VMEM_LIMIT),
    )(tot, g_w_in, g_w_out, *arrays, grad_x)
    return [outs[0]] + [outs[1 + 4 * j:5 + 4 * j] for j in range(2 + n)] + [outs[-1]]


def _local_step(x2d, tgt, proj, lb_logits, cw, ga, gcn, w_out, gf):
    g64 = _group_matrix(HEAD, CONV_GROUP)
    mixed, aux, states, wog = _mix_fwd(proj, lb_logits, cw, ga, gcn, g64, w_out)
    dx2, dmixed, gwo, part_out = _out_loss(x2d, mixed, wog.reshape(D_MODEL, D_MODEL), gf, tgt)
    dproj, part_mix = _mix_bwd(proj, aux, states, dmixed, lb_logits, cw, ga, gcn, g64)
    return dproj, dx2, gwo.reshape(N_SHARD, WO_ROWS, D_MODEL), part_out, part_mix


def kernel(x, norm_gain, w_in, lb_logits, conv_w, hgrn_norm_gain, conv_norm_gain, w_out, final_norm_gain, loss_target, m_norm_gain, m_w_in, m_lb_logits, m_conv_w, m_hgrn_norm_gain, m_conv_norm_gain, m_w_out, m_final_norm_gain, v_norm_gain, v_w_in, v_lb_logits, v_conv_w, v_hgrn_norm_gain, v_conv_norm_gain, v_w_out, v_final_norm_gain):
    k = 2 * lax.axis_index("x") + lax.axis_index("y")
    kidx = jnp.reshape(k, (1,)).astype(jnp.int32)
    row = lambda a: a.reshape(1, D_MODEL)
    taps = lambda a: a.reshape(1, 3 * HEAD)
    h, proj, wg, cw = _gather_proj(kidx, x[0], norm_gain, w_in, taps(conv_w))
    dproj, dx2, gwo, part_out, part_mix = _local_step(
        x[0], loss_target[0], proj, lb_logits, cw, hgrn_norm_gain, conv_norm_gain, w_out, row(final_norm_gain))
    rgrad_x, rg_w_in, rg_w_out, tot = _bwd_tail(kidx, h, dproj, wg, gwo, x[0], dx2, norm_gain, part_out, part_mix)

    (loss, (g_w_in, d_w_in, nm_w_in, nv_w_in), (g_w_out, d_w_out, nm_w_out, nv_w_out),
     (g_norm_gain, d_ng, nm_ng, nv_ng), (g_final, d_fg, nm_fg, nv_fg), (g_hgrn, d_hg, nm_hg, nv_hg),
     (g_convn, d_cg, nm_cg, nv_cg), (g_lb, d_lb, nm_lb, nv_lb), (g_conv_w, d_cw, nm_cw, nv_cw),
     grad_x) = _adamw_all(
        tot, rg_w_in, rg_w_out,
        [(w_in[0], m_w_in[0], v_w_in[0]), (w_out[0], m_w_out[0], v_w_out[0])],
        [(norm_gain, m_norm_gain, v_norm_gain),
         (row(final_norm_gain), row(m_final_norm_gain), row(v_final_norm_gain)),
         (hgrn_norm_gain, m_hgrn_norm_gain, v_hgrn_norm_gain),
         (conv_norm_gain, m_conv_norm_gain, v_conv_norm_gain),
         (lb_logits, m_lb_logits, v_lb_logits),
         (taps(conv_w), taps(m_conv_w), taps(v_conv_w))],
        rgrad_x)
    flat = lambda a: a.reshape(D_MODEL)
    untap = lambda a: a.reshape(1, 3, HEAD)
    return (loss.reshape(()), grad_x[None],
            g_norm_gain, g_w_in[None], g_lb, untap(g_conv_w), g_hgrn, g_convn, g_w_out[None], flat(g_final),
            d_ng, d_w_in[None], d_lb, untap(d_cw), d_hg, d_cg, d_w_out[None], flat(d_fg),
            nm_ng, nm_w_in[None], nm_lb, untap(nm_cw), nm_hg, nm_cg, nm_w_out[None], flat(nm_fg),
            nv_ng, nv_w_in[None], nv_lb, untap(nv_cw), nv_hg, nv_cg, nv_w_out[None], flat(nv_fg))
```

```python
import jax
import jax.numpy as jnp
import numpy as np
from jax import lax
from jax.experimental import pallas as pl
from jax.experimental.pallas import tpu as pltpu

F32 = jnp.float32
BF16 = jnp.bfloat16
MESH = pl.DeviceIdType.MESH

SEQ = 2048
D_MODEL = 1024
D_HGRN = 512
D_CONV = 512
HEAD = 128
N_HEADS = 4
CHUNK = 64
CONV_GROUP = 64
N_SHARD = 4
SHARD_COLS = 1024
WO_ROWS = 256
EPS = 1e-6
TB = 256
NCB = TB // CHUNK
N_CHUNKS = SEQ // CHUNK
N_DEV = 8
COLLECTIVE_GATHER, COLLECTIVE_MIX_OUT, COLLECTIVE_TAIL = 1, 0, 2
AUX_O, AUX_CV, AUX_B, AUX_COLS = 0, 512, 1024, 1536

ADAM_LR = 0.001
ADAM_B1 = 0.9
ADAM_B2 = 0.999
ADAM_EPS = 1e-08
ADAM_WD = 0.01
ADAM_STEP = 10

VMEM_LIMIT = 56 * 1024 * 1024


def _dot(a, b):
    return jnp.dot(a, b, preferred_element_type=F32)


def _dot_nt(a, b):
    return lax.dot_general(a, b, (((1,), (1,)), ((), ())), preferred_element_type=F32)


def _dot_tn(a, b):
    return lax.dot_general(a, b, (((0,), (0,)), ((), ())), preferred_element_type=F32)


def _split_bf16(x, n):
    parts = []
    r = x
    for _ in range(n):
        p = r.astype(BF16)
        parts.append(p)
        r = r - p.astype(F32)
    return parts


def _exact_left(m, x, n=3):
    acc = None
    for p in _split_bf16(x, n):
        t = _dot(m, p)
        acc = t if acc is None else acc + t
    return acc


def _exact_left_many(m, xs, n=3):
    parts = [_split_bf16(x, n) for x in xs]
    accs = [None] * len(xs)
    for i in range(n):
        for j in range(len(xs)):
            t = _dot(m, parts[j][i])
            accs[j] = t if accs[j] is None else accs[j] + t
    return accs


def _group_mean_many(xs, gmat, n=2):
    parts = [_split_bf16(x, n) for x in xs]
    accs = [None] * len(xs)
    for i in range(n):
        for j in range(len(xs)):
            t = _dot(parts[j][i], gmat)
            accs[j] = t if accs[j] is None else accs[j] + t
    return accs


def _group_mean(x, gmat, n=2):
    w = gmat.shape[0]
    outs = []
    for c0 in range(0, x.shape[1], w):
        acc = None
        for p in _split_bf16(x[:, c0:c0 + w], n):
            t = _dot(p, gmat)
            acc = t if acc is None else acc + t
        outs.append(acc)
    return jnp.concatenate(outs, axis=1)


def _sigmoid(x):
    return 1.0 / (1.0 + jnp.exp(-x))


def _lower_bound(lbl):
    l0 = lbl[0:1, :]
    l1 = lbl[1:2, :]
    m = jnp.maximum(l0, l1)
    e0 = jnp.exp(l0 - m)
    e1 = jnp.exp(l1 - m)
    return e0 / (e0 + e1)


def _tri(lower):
    r = lax.broadcasted_iota(jnp.int32, (CHUNK, CHUNK), 0)
    c = lax.broadcasted_iota(jnp.int32, (CHUNK, CHUNK), 1)
    return jnp.where((c <= r) if lower else (c >= r), 1.0, 0.0).astype(BF16)


def _causal():
    r = lax.broadcasted_iota(jnp.int32, (CHUNK, CHUNK), 0)
    c = lax.broadcasted_iota(jnp.int32, (CHUNK, CHUNK), 1)
    return c <= r


def _shift_down(x, sh, prev_tail):
    r = pltpu.roll(x, sh, 0)
    pt = pltpu.roll(prev_tail, sh, 0)
    rows = lax.broadcasted_iota(jnp.int32, prev_tail.shape, 0)
    top = jnp.where(rows < sh, pt, r[0:8])
    return jnp.concatenate([top, r[8:]], axis=0)


def _shift_up(x, sh, next_head):
    n = x.shape[0]
    r = pltpu.roll(x, n - sh, 0)
    nh = pltpu.roll(next_head, 8 - sh, 0)
    rows = lax.broadcasted_iota(jnp.int32, next_head.shape, 0)
    bot = jnp.where(rows >= 8 - sh, nh, r[n - 8:])
    return jnp.concatenate([r[:n - 8], bot], axis=0)


def _group_matrix(width, group):
    r = np.arange(width)[:, None] // group
    c = np.arange(width)[None, :] // group
    return jnp.asarray(np.where(r == c, 1.0 / group, 0.0), dtype=BF16)


TG = 1024
SEM_W, SEM_CW, SEM_W_FWD, N_SEM = 0, 4, 7, 11


def _gather_proj(kidx, x2d, g1, w_in, conv_w):
    half_w = D_MODEL // 2
    half_c = SHARD_COLS // 2
    nt = SEQ // TG
    n_steps = 2 * N_SHARD

    def body(k_ref, x_ref, g_ref, w_ref, cw_ref, h_ref, p_ref, wg_out, cwg_out,
             wg_v, cwg_v, send_sems, recv_sems, out_sems):
        s, t = pl.program_id(0), pl.program_id(1)
        x, y, c = lax.axis_index("x"), lax.axis_index("y"), lax.axis_index("c")
        k = 2 * x + y
        sibling = (x, y, 1 - c)
        chips = [(1 - x, y), (x, 1 - y), (1 - x, 1 - y)]
        kjs = [2 * cx + cy for cx, cy in chips]
        diag = (*chips[2], c)

        def w_half(kk, cc):
            return wg_v.at[kk, pl.ds(cc * half_w, half_w), :]

        def w_quarter(kk, cc, piece):
            return wg_v.at[kk, pl.ds(cc * half_w, half_w), piece * half_c:(piece + 1) * half_c]

        def cw_of(kk):
            return cwg_v.at[:, pl.ds(pl.multiple_of(kk * HEAD, HEAD), HEAD)]

        def copy(sem, ref, to):
            return pltpu.make_async_remote_copy(
                src_ref=ref, dst_ref=ref, send_sem=send_sems.at[sem], recv_sem=recv_sems.at[sem],
                device_id=to, device_id_type=MESH)

        def at_step(sv, tv):
            return pl.when((s == sv) & (t == tv))

        w_direct = ([copy(SEM_W + j, w_half(k, c), (*chips[j], c)) for j in range(2)]
                    + [copy(SEM_W + 2 + p, w_quarter(k, c, p), diag) for p in range(2)])
        cw_direct = [copy(SEM_CW + j, cw_of(k), (*chip, c)) for j, chip in enumerate(chips)]
        w_passed = ([copy(SEM_W_FWD + j, w_half(kjs[j], c), sibling) for j in range(2)]
                    + [copy(SEM_W_FWD + 2 + p, w_quarter(kjs[2], c, p), sibling) for p in range(2)])
        stores = ([pltpu.make_async_copy(wg_v.at[kk], wg_out.at[kk], out_sems.at[i])
                   for i, kk in enumerate([k] + kjs)]
                  + [pltpu.make_async_copy(cwg_v, cwg_out, out_sems.at[4])])

        @at_step(0, 0)
        def _():
            barrier = pltpu.get_barrier_semaphore()
            for peer in [sibling] + [(*chip, c) for chip in chips]:
                pl.semaphore_signal(barrier, inc=1, device_id=peer, device_id_type=MESH)
            wg_v[k] = w_ref[0].astype(BF16)
            mine = pl.ds(pl.multiple_of(k * HEAD, HEAD), HEAD)
            cwg_v[:, mine] = jnp.zeros((8, HEAD), F32)
            for tap in range(3):
                cwg_v[tap:tap + 1, mine] = cw_ref[:, tap * HEAD:(tap + 1) * HEAD]
            pl.semaphore_wait(barrier, 4)
            w_direct[0].start()
            w_direct[1].start()
            for cp in cw_direct:
                cp.start()
            stores[0].start()

        @at_step(2, 0)
        def _():
            for j in range(2):
                copy(SEM_W + j, w_half(kjs[j], c), sibling).wait_recv()
                w_passed[j].start()
            w_direct[2].start()
            w_direct[3].start()
            copy(SEM_W_FWD, w_half(kjs[0], 1 - c), sibling).wait_recv()
            stores[1].start()

        @at_step(4, 0)
        def _():
            copy(SEM_W_FWD + 1, w_half(kjs[1], 1 - c), sibling).wait_recv()
            stores[2].start()

        for p in range(2):
            @at_step(6 + p, 0)
            def _(p=p):
                copy(SEM_W + 2 + p, w_quarter(kjs[2], c, p), sibling).wait_recv()
                w_passed[2 + p].start()
                copy(SEM_W_FWD + 2 + p, w_quarter(kjs[2], 1 - c, p), sibling).wait_recv()

        rows = pl.ds(pl.multiple_of(t * TG, TG), TG)

        @pl.when(s == 0)
        def _():
            xv = x_ref[...]
            r = lax.rsqrt(jnp.mean(xv * xv, axis=-1, keepdims=True) + EPS)
            h_ref[rows, :] = (xv * r * g_ref[...]).astype(BF16)

        sh = s >> 1
        js = k ^ (((sh & 1) << 1) | (sh >> 1))
        for piece in range(2):
            @pl.when((s & 1) == piece)
            def _(piece=piece):
                p_ref[...] = _dot(h_ref[rows, :], wg_v[js, :, piece * half_c:(piece + 1) * half_c])

        @at_step(n_steps - 1, nt - 1)
        def _():
            stores[3].start()
            for j in range(3):
                copy(SEM_CW + j, cw_of(kjs[j]), sibling).wait_recv()
            stores[4].start()
            for cp in w_direct + cw_direct + w_passed:
                cp.wait_send()
            for st in stores:
                st.wait()

    def x_map(s, t, kr):
        return (jnp.where(s == 0, t, nt - 1), 0)

    def p_map(s, t, kr):
        sh = s >> 1
        return (t, 2 * (kr[0] ^ (((sh & 1) << 1) | (sh >> 1))) + (s & 1))

    hbm = pl.BlockSpec(memory_space=pl.ANY)
    grid_spec = pltpu.PrefetchScalarGridSpec(
        num_scalar_prefetch=1, grid=(n_steps, nt),
        in_specs=[pl.BlockSpec((TG, D_MODEL), x_map),
                  pl.BlockSpec((1, D_MODEL), lambda s, t, kr: (0, 0)),
                  pl.BlockSpec((1, D_MODEL, SHARD_COLS), lambda s, t, kr: (0, 0, 0)),
                  pl.BlockSpec((1, 3 * HEAD), lambda s, t, kr: (0, 0))],
        out_specs=(pl.BlockSpec((SEQ, D_MODEL), lambda s, t, kr: (0, 0)),
                   pl.BlockSpec((TG, half_c), p_map), hbm, hbm),
        scratch_shapes=[pltpu.VMEM((N_SHARD, D_MODEL, SHARD_COLS), BF16),
                        pltpu.VMEM((8, D_CONV), F32),
                        pltpu.SemaphoreType.DMA((N_SEM,)), pltpu.SemaphoreType.DMA((N_SEM,)),
                        pltpu.SemaphoreType.DMA((5,))])
    return pl.pallas_call(
        body, name="gather_proj", grid_spec=grid_spec,
        out_shape=(jax.ShapeDtypeStruct((SEQ, D_MODEL), BF16),
                   jax.ShapeDtypeStruct((SEQ, N_SHARD * SHARD_COLS), F32),
                   jax.ShapeDtypeStruct((N_SHARD, D_MODEL, SHARD_COLS), BF16),
                   jax.ShapeDtypeStruct((8, D_CONV), F32)),
        compiler_params=pltpu.CompilerParams(dimension_semantics=("arbitrary", "arbitrary"),
                                             vmem_limit_bytes=VMEM_LIMIT, collective_id=COLLECTIVE_GATHER),
    )(kidx, x2d, g1, w_in, conv_w)


LAG = 4


def _mix_out(proj, lb_logits, cw, ga, gcn, g64, w_out, x2d, gf, tgt):
    half_o = WO_ROWS // 2
    nblk = SEQ // TB
    n_steps = nblk + LAG

    def body(p_ref, lbl_ref, cw_ref, ga_ref, gcn_ref, g64_ref, wo_ref, x_ref, gf_ref, t_ref,
             aux_ref, sto_ref, dx2_ref, dm_ref, gwo_ref, part_ref,
             st_ref, tail_ref, wog_v, stage, ring, acc_ref, send_sems, recv_sems):
        i = pl.program_id(0)
        x, y, c = lax.axis_index("x"), lax.axis_index("y"), lax.axis_index("c")
        k = 2 * x + y
        sibling = (x, y, 1 - c)
        chips = [(1 - x, y), (x, 1 - y), (1 - x, 1 - y)]
        kjs = [2 * cx + cy for cx, cy in chips]

        def wo_half(kk, cc):
            return wog_v.at[pl.ds(pl.multiple_of(kk * WO_ROWS + cc * half_o, half_o), half_o), :]

        def copy(sem, ref, to):
            return pltpu.make_async_remote_copy(
                src_ref=ref, dst_ref=ref, send_sem=send_sems.at[sem], recv_sem=recv_sems.at[sem],
                device_id=to, device_id_type=MESH)

        wo_direct = [copy(j, wo_half(k, c), (*chip, c)) for j, chip in enumerate(chips)]
        wo_passed = [copy(3 + j, wo_half(kj, c), sibling) for j, kj in enumerate(kjs)]

        @pl.when(i == 0)
        def _():
            barrier = pltpu.get_barrier_semaphore()
            for peer in [sibling] + [(*chip, c) for chip in chips]:
                pl.semaphore_signal(barrier, inc=1, device_id=peer, device_id_type=MESH)
            st_ref[...] = jnp.zeros_like(st_ref)
            tail_ref[...] = jnp.zeros_like(tail_ref)
            acc_ref[...] = jnp.zeros_like(acc_ref)
            part_ref[...] = jnp.zeros_like(part_ref)
            wog_v[pl.ds(pl.multiple_of(k * WO_ROWS, WO_ROWS), WO_ROWS), :] = wo_ref[0].astype(BF16)
            pl.semaphore_wait(barrier, 4)
            for cp in wo_direct:
                cp.start()

        @pl.when(i == LAG - 1)
        def _():
            for j in range(3):
                copy(j, wo_half(kjs[j], c), sibling).wait_recv()
                wo_passed[j].start()

        @pl.when(i == LAG)
        def _():
            for j in range(3):
                copy(3 + j, wo_half(kjs[j], 1 - c), sibling).wait_recv()

        lb = _lower_bound(lbl_ref[...])
        tri = _tri(True)
        causal = _causal()
        g64m = g64_ref[...]
        heads = range(N_HEADS)
        cs = [slice(hd * HEAD, (hd + 1) * HEAD) for hd in heads]
        col = lambda base, hd: slice(base + hd * HEAD, base + (hd + 1) * HEAD)

        def mix_chunk(n):
            sl = pl.ds(n * CHUNK, CHUNK)
            sg = [_sigmoid(p_ref[sl, col(512, hd)]) for hd in heads]
            f = [lb[:, cs[hd]] + (1.0 - lb[:, cs[hd]]) * sg[hd] for hd in heads]
            bc = _exact_left_many(tri, [jnp.log(f[hd]) for hd in heads])
            for hd in heads:
                aux_ref[sl, col(AUX_B, hd)] = bc[hd]
            g = [bc[hd][CHUNK - 1:CHUNK, :] for hd in heads]
            qd = [(p_ref[sl, col(0, hd)] * jnp.exp(bc[hd])).astype(BF16) for hd in heads]
            ki = [((1.0 - f[hd]) * jnp.exp(-bc[hd])).astype(BF16) for hd in heads]
            ke = [((1.0 - f[hd]) * jnp.exp(g[hd] - bc[hd])).astype(BF16) for hd in heads]
            vb = [p_ref[sl, col(1024, hd)].astype(BF16) for hd in heads]
            st = [st_ref[hd] for hd in heads]
            st_b = [a.astype(BF16) for a in st]
            for hd in heads:
                sto_ref[n, hd] = st_b[hd]
            scm = [_dot_nt(qd[hd], ki[hd]) for hd in heads]
            inter = [_dot_nt(qd[hd], st_b[hd]) for hd in heads]
            upd = [_dot_tn(vb[hd], ke[hd]) for hd in heads]
            intra = [_dot(jnp.where(causal, scm[hd], 0.0).astype(BF16), vb[hd]) for hd in heads]
            for hd in heads:
                st_ref[hd] = st[hd] * jnp.exp(g[hd]) + upd[hd]
                o = intra[hd] + inter[hd]
                aux_ref[sl, col(AUX_O, hd)] = o
                ra = lax.rsqrt(jnp.mean(o * o, axis=-1, keepdims=True) + EPS)
                za = p_ref[sl, col(1536, hd)]
                stage[sl, cs[hd]] = (o * ra * ga_ref[:, cs[hd]] * (za * _sigmoid(za))).astype(BF16)
            yb = []
            for hd in heads:
                cu = p_ref[sl, col(3072, hd)] * p_ref[sl, col(2048, hd)]
                tail = tail_ref[:, cs[hd]]
                cv = (cw_ref[0:1, cs[hd]] * _shift_down(cu, 2, tail) + cw_ref[1:2, cs[hd]] * _shift_down(cu, 1, tail)
                      + cw_ref[2:3, cs[hd]] * cu)
                tail_ref[:, cs[hd]] = cu[CHUNK - 8:, :]
                aux_ref[sl, col(AUX_CV, hd)] = cv
                yb.append(p_ref[sl, col(2560, hd)] * cv)
            ms = _group_mean_many([y * y for y in yb], g64m)
            for hd in heads:
                rb = lax.rsqrt(ms[hd] + EPS)
                zb = p_ref[sl, col(3584, hd)]
                stage[sl, col(512, hd)] = (yb[hd] * rb * gcn_ref[:, cs[hd]] * (zb * _sigmoid(zb))).astype(BF16)

        def step(mix, project):
            if project:
                mixed_b = ring[pl.ds(pl.multiple_of((i - LAG) * TB, TB), TB), :]
                y = _dot(mixed_b, wog_v[...])
            if mix:
                mix_chunk(0)
            if project:
                x2 = x_ref[...] + y
                r2 = lax.rsqrt(jnp.mean(x2 * x2, axis=-1, keepdims=True) + EPS)
                n2 = x2 * r2
                gfv = gf_ref[...]
                err = n2 * gfv - t_ref[...]
                loss = 0.5 * jnp.sum(jnp.mean(err * err, axis=-1, keepdims=True), axis=0, keepdims=True)
                dy = err * (1.0 / D_MODEL)
                part_ref[1:2, :] += jnp.sum(dy * n2, axis=0, keepdims=True)
                part_ref[7:8, :] += jnp.broadcast_to(loss, (1, D_MODEL))
                dn = dy * gfv
                dx2 = r2 * (dn - n2 * jnp.mean(dn * n2, axis=-1, keepdims=True))
                dx2_ref[...] = dx2
                dx2_b = dx2.astype(BF16)
            if mix:
                mix_chunk(1)
            if project:
                dm_ref[...] = _dot_nt(dx2_b, wog_v[...])
            if mix:
                mix_chunk(2)
            if project:
                acc_ref[...] += _dot_tn(mixed_b, dx2_b)
            if mix:
                mix_chunk(3)
                ring[pl.ds(pl.multiple_of(i * TB, TB), TB), :] = stage[...]

        @pl.when(i < LAG)
        def _():
            step(True, False)

        @pl.when((i >= LAG) & (i < nblk))
        def _():
            step(True, True)

        @pl.when(i >= nblk)
        def _():
            step(False, True)

        @pl.when(i == n_steps - 1)
        def _():
            gwo_ref[...] = acc_ref[...].astype(BF16)
            for cp in wo_direct + wo_passed:
                cp.wait_send()

    assert NCB == 4
    row = lambda w: pl.BlockSpec((1, w), lambda i: (0, 0))
    mix_blk = lambda i: jnp.minimum(i, nblk - 1)
    out_blk = lambda i: jnp.clip(i - LAG, 0, nblk - 1)
    tok = lambda: pl.BlockSpec((TB, D_MODEL), lambda i: (out_blk(i), 0))
    return pl.pallas_call(
        body, name="mix_out", grid=(n_steps,),
        out_shape=(jax.ShapeDtypeStruct((SEQ, AUX_COLS), F32),
                   jax.ShapeDtypeStruct((N_CHUNKS, N_HEADS, HEAD, HEAD), BF16),
                   jax.ShapeDtypeStruct((SEQ, D_MODEL), F32),
                   jax.ShapeDtypeStruct((SEQ, D_MODEL), F32),
                   jax.ShapeDtypeStruct((D_MODEL, D_MODEL), BF16),
                   jax.ShapeDtypeStruct((8, D_MODEL), F32)),
        in_specs=[pl.BlockSpec((TB, 4096), lambda i: (jnp.minimum(i, nblk - 1), 0)),
                  pl.BlockSpec((2, D_HGRN), lambda i: (0, 0)),
                  pl.BlockSpec((8, D_CONV), lambda i: (0, 0)),
                  row(D_HGRN), row(D_CONV),
                  pl.BlockSpec((HEAD, HEAD), lambda i: (0, 0)),
                  pl.BlockSpec((1, WO_ROWS, D_MODEL), lambda i: (0, 0, 0)),
                  tok(), row(D_MODEL), tok()],
        out_specs=(pl.BlockSpec((TB, AUX_COLS), lambda i: (mix_blk(i), 0)),
                   pl.BlockSpec((NCB, N_HEADS, HEAD, HEAD), lambda i: (mix_blk(i), 0, 0, 0)),
                   tok(), tok(),
                   pl.BlockSpec((D_MODEL, D_MODEL), lambda i: (0, 0)),
                   pl.BlockSpec((8, D_MODEL), lambda i: (0, 0))),
        scratch_shapes=[pltpu.VMEM((N_HEADS, HEAD, HEAD), F32), pltpu.VMEM((8, D_CONV), F32),
                        pltpu.VMEM((D_MODEL, D_MODEL), BF16), pltpu.VMEM((TB, D_MODEL), BF16),
                        pltpu.VMEM((SEQ, D_MODEL), BF16), pltpu.VMEM((D_MODEL, D_MODEL), F32),
                        pltpu.SemaphoreType.DMA((6,)), pltpu.SemaphoreType.DMA((6,))],
        compiler_params=pltpu.CompilerParams(dimension_semantics=("arbitrary",), vmem_limit_bytes=VMEM_LIMIT,
                                             collective_id=COLLECTIVE_MIX_OUT),
    )(proj, lb_logits, cw, ga, gcn, g64, w_out, x2d, gf, tgt)


def _mix_bwd(proj, aux, states, dmixed, lb_logits, cw, ga, gcn, g64):
    nblk = SEQ // TB

    def body(p_ref, aux_ref, st_ref, dm_ref, lbl_ref, cw_ref, ga_ref, gcn_ref, g64_ref,
             dp_ref, part_ref, dst_ref, head_ref, dlb_ref):
        i = pl.program_id(0)

        @pl.when(i == 0)
        def _():
            dst_ref[...] = jnp.zeros_like(dst_ref)
            head_ref[...] = jnp.zeros_like(head_ref)
            part_ref[...] = jnp.zeros_like(part_ref)
            dlb_ref[...] = jnp.zeros_like(dlb_ref)

        lb = _lower_bound(lbl_ref[...])
        triu = _tri(False)
        causal = _causal()
        g64m = g64_ref[...]
        rowsum = lambda a: jnp.sum(a, axis=0, keepdims=True)
        heads = range(N_HEADS)
        cs = [slice(hd * HEAD, (hd + 1) * HEAD) for hd in heads]
        col = lambda base, hd: slice(base + hd * HEAD, base + (hd + 1) * HEAD)
        for n in reversed(range(NCB)):
            sl = pl.ds(n * CHUNK, CHUNK)
            cvv = [aux_ref[sl, col(AUX_CV, hd)] for hd in heads]
            gb = [p_ref[sl, col(2560, hd)] for hd in heads]
            yb = [gb[hd] * cvv[hd] for hd in heads]
            ms = _group_mean_many([y * y for y in yb], g64m)
            rb, nb, dnb = [], [], []
            for hd in heads:
                rb.append(lax.rsqrt(ms[hd] + EPS))
                nb.append(yb[hd] * rb[hd])
                zb = p_ref[sl, col(3584, hd)]
                sgb = _sigmoid(zb)
                dmb = dm_ref[sl, col(512, hd)]
                gcv = gcn_ref[:, cs[hd]]
                part_ref[2:3, col(512, hd)] += rowsum(dmb * nb[hd] * (zb * sgb))
                dp_ref[sl, col(3584, hd)] = (dmb * nb[hd] * gcv * (sgb * (1.0 + zb * (1.0 - sgb)))).astype(BF16)
                dnb.append(dmb * gcv * (zb * sgb))
            mdn = _group_mean_many([dnb[hd] * nb[hd] for hd in heads], g64m)
            for hd in heads:
                dyb = rb[hd] * (dnb[hd] - nb[hd] * mdn[hd])
                dp_ref[sl, col(2560, hd)] = (dyb * cvv[hd]).astype(BF16)
                dcv = dyb * gb[hd]
                head = head_ref[:, cs[hd]]
                dcv1 = _shift_up(dcv, 1, head)
                dcv2 = _shift_up(dcv, 2, head)
                head_ref[:, cs[hd]] = dcv[0:8, :]
                u = p_ref[sl, col(2048, hd)]
                gc = p_ref[sl, col(3072, hd)]
                cu = gc * u
                part_ref[4:5, cs[hd]] += rowsum(dcv2 * cu)
                part_ref[5:6, cs[hd]] += rowsum(dcv1 * cu)
                part_ref[6:7, cs[hd]] += rowsum(dcv * cu)
                dcu = cw_ref[2:3, cs[hd]] * dcv + cw_ref[1:2, cs[hd]] * dcv1 + cw_ref[0:1, cs[hd]] * dcv2
                dp_ref[sl, col(3072, hd)] = (dcu * u).astype(BF16)
                dp_ref[sl, col(2048, hd)] = (dcu * gc).astype(BF16)
            do_b = []
            for hd in heads:
                ov = aux_ref[sl, col(AUX_O, hd)]
                ra = lax.rsqrt(jnp.mean(ov * ov, axis=-1, keepdims=True) + EPS)
                na = ov * ra
                za = p_ref[sl, col(1536, hd)]
                sga = _sigmoid(za)
                dma = dm_ref[sl, cs[hd]]
                gav = ga_ref[:, cs[hd]]
                part_ref[2:3, cs[hd]] += rowsum(dma * na * (za * sga))
                dp_ref[sl, col(1536, hd)] = (dma * na * gav * (sga * (1.0 + za * (1.0 - sga)))).astype(BF16)
                dna = dma * gav * (za * sga)
                do_b.append((ra * (dna - na * jnp.mean(dna * na, axis=-1, keepdims=True))).astype(BF16))
            s = [_sigmoid(p_ref[sl, col(512, hd)]) for hd in heads]
            f = [lb[:, cs[hd]] + (1.0 - lb[:, cs[hd]]) * s[hd] for hd in heads]
            bc = [aux_ref[sl, col(AUX_B, hd)] for hd in heads]
            g = [bc[hd][CHUNK - 1:CHUNK, :] for hd in heads]
            eb = [jnp.exp(bc[hd]) for hd in heads]
            enb = [jnp.exp(-bc[hd]) for hd in heads]
            eg = [jnp.exp(g[hd] - bc[hd]) for hd in heads]
            dec = [jnp.exp(g[hd]) for hd in heads]
            qd = [p_ref[sl, cs[hd]] * eb[hd] for hd in heads]
            ki = [(1.0 - f[hd]) * enb[hd] for hd in heads]
            ke = [(1.0 - f[hd]) * eg[hd] for hd in heads]
            qd_b = [a.astype(BF16) for a in qd]
            ki_b = [a.astype(BF16) for a in ki]
            ke_b = [a.astype(BF16) for a in ke]
            vb = [p_ref[sl, col(1024, hd)].astype(BF16) for hd in heads]
            st_b = [st_ref[n, hd] for hd in heads]
            dst = [dst_ref[hd] for hd in heads]
            dst_b = [a.astype(BF16) for a in dst]
            scm = [_dot_nt(qd_b[hd], ki_b[hd]) for hd in heads]
            amm = [_dot_nt(do_b[hd], vb[hd]) for hd in heads]
            dqd2 = [_dot(do_b[hd], st_b[hd]) for hd in heads]
            dke = [_dot(vb[hd], dst_b[hd]) for hd in heads]
            dv2 = [_dot_nt(ke_b[hd], dst_b[hd]) for hd in heads]
            dsu = [_dot_tn(do_b[hd], qd_b[hd]) for hd in heads]
            sc = [jnp.where(causal, scm[hd], 0.0).astype(BF16) for hd in heads]
            am = [jnp.where(causal, amm[hd], 0.0).astype(BF16) for hd in heads]
            dqd1 = [_dot(am[hd], ki_b[hd]) for hd in heads]
            dki = [_dot_tn(am[hd], qd_b[hd]) for hd in heads]
            dv1 = [_dot_tn(sc[hd], do_b[hd]) for hd in heads]
            db, dgv = [], []
            for hd in heads:
                dqd = dqd1[hd] + dqd2[hd]
                ddec = rowsum(dst[hd] * st_b[hd].astype(F32))
                dst_ref[hd] = dst[hd] * dec[hd] + dsu[hd]
                dp_ref[sl, cs[hd]] = (dqd * eb[hd]).astype(BF16)
                dp_ref[sl, col(1024, hd)] = (dv1[hd] + dv2[hd]).astype(BF16)
                db.append(dqd * qd[hd] - dki[hd] * ki[hd] - dke[hd] * ke[hd])
                dgv.append(rowsum(dke[hd] * ke[hd]) + ddec * dec[hd])
            rc = _exact_left_many(triu, db, 2)
            for hd in heads:
                df = (rc[hd] + dgv[hd]) / f[hd] - (dki[hd] * enb[hd] + dke[hd] * eg[hd])
                dlb_ref[:, cs[hd]] += rowsum(df * (1.0 - s[hd]))
                dp_ref[sl, col(512, hd)] = (df * (1.0 - lb[:, cs[hd]]) * s[hd] * (1.0 - s[hd])).astype(BF16)

        @pl.when(i == nblk - 1)
        def _():
            row = dlb_ref[...] * lb * (1.0 - lb)
            part_ref[3:4, 0:D_HGRN] = row
            part_ref[3:4, D_HGRN:] = -row

    rev = lambda w: pl.BlockSpec((TB, w), lambda i: (nblk - 1 - i, 0))
    row = lambda w: pl.BlockSpec((1, w), lambda i: (0, 0))
    return pl.pallas_call(
        body, name="mix_bwd", grid=(nblk,),
        out_shape=(jax.ShapeDtypeStruct((SEQ, 4096), BF16),
                   jax.ShapeDtypeStruct((8, D_MODEL), F32)),
        in_specs=[rev(4096), rev(AUX_COLS),
                  pl.BlockSpec((NCB, N_HEADS, HEAD, HEAD), lambda i: (nblk - 1 - i, 0, 0, 0)),
                  rev(D_MODEL),
                  pl.BlockSpec((2, D_HGRN), lambda i: (0, 0)),
                  pl.BlockSpec((8, D_CONV), lambda i: (0, 0)),
                  row(D_HGRN), row(D_CONV),
                  pl.BlockSpec((HEAD, HEAD), lambda i: (0, 0))],
        out_specs=(rev(4096), pl.BlockSpec((8, D_MODEL), lambda i: (0, 0))),
        scratch_shapes=[pltpu.VMEM((N_HEADS, HEAD, HEAD), F32), pltpu.VMEM((8, D_CONV), F32),
                        pltpu.VMEM((1, D_HGRN), F32)],
        compiler_params=pltpu.CompilerParams(dimension_semantics=("arbitrary",), vmem_limit_bytes=VMEM_LIMIT),
    )(proj, aux, states, dmixed, lb_logits, cw, ga, gcn, g64)


TT = 1024
TX = 256
(SEM_D2D, SEM_D2D_O, SEM_ICI, SEM_ICI_O, SEM_FIN, SEM_FIN_O, SEM_SMALL, N_SEM_TAIL) = 0, 4, 5, 8, 11, 12, 12, 20


def _bwd_tail(kidx, h, dproj, wg, gwo, x2d, dx2, g1, small_a, small_b):
    hw = D_MODEL // 2
    ho = WO_ROWS // 2
    nt = SEQ // TT
    n_steps = N_SHARD + SEQ // TX // nt

    def body(k_ref, h_ref, dp_ref, w_ref, gwo_ref, x_ref, dx2_ref, g_ref, sm_ref, smb_ref,
             gx_ref, gw_out, gwo_out, osm_ref,
             acc, dh, sendbuf, keep, sibrcv, rcv, sib_o, p_o, rcv_o, res_o, sm_buf, dng,
             send_sems, recv_sems, out_sems):
        s, t = pl.program_id(0), pl.program_id(1)
        x, y, c = lax.axis_index("x"), lax.axis_index("y"), lax.axis_index("c")
        k = 2 * x + y
        me = 4 * x + 2 * y + c
        sibling = (x, y, 1 - c)
        chips = [(1 - x, 1 - y), (1 - x, y), (x, 1 - y)]
        kjs = [2 * cx + cy for cx, cy in chips]
        mine = pl.ds(pl.multiple_of(c * hw, hw), hw)
        other = pl.ds(pl.multiple_of((1 - c) * hw, hw), hw)
        mine_o = pl.ds(pl.multiple_of(c * ho, ho), ho)
        other_o = pl.ds(pl.multiple_of((1 - c) * ho, ho), ho)

        def copy(sem, src, dst, to):
            return pltpu.make_async_remote_copy(
                src_ref=src, dst_ref=dst, send_sem=send_sems.at[sem], recv_sem=recv_sems.at[sem],
                device_id=to, device_id_type=MESH)

        def at_step(sv, tv):
            return pl.when((s == sv) & (t == tv))

        def at_norm_block(b):
            return at_step(N_SHARD + b // nt, b % nt)

        d2d = [copy(SEM_D2D + sv, sendbuf.at[sv], sibrcv.at[sv], sibling) for sv in range(N_SHARD)]
        d2d_o = copy(SEM_D2D_O, gwo_ref.at[:, other_o, :], sib_o, sibling)
        ici = [copy(SEM_ICI + sv, keep.at[sv], rcv.at[sv], (*chips[sv], c)) for sv in range(3)]
        ici_o = [copy(SEM_ICI_O + sv, p_o.at[kjs[sv]], rcv_o.at[sv], (*chips[sv], c)) for sv in range(3)]
        fin = copy(SEM_FIN, acc.at[mine, :], gw_out.at[mine, :], sibling)
        fin_o = copy(SEM_FIN_O, res_o.at[mine_o, :], res_o.at[mine_o, :], sibling)
        smalls = [copy(SEM_SMALL + m, sm_buf.at[me], sm_buf.at[me],
                       (x ^ (m >> 2), y ^ ((m >> 1) & 1), c ^ (m & 1))) for m in range(1, N_DEV)]
        store_w = pltpu.make_async_copy(acc.at[mine, :], gw_out.at[mine, :], out_sems.at[0])
        store_o = pltpu.make_async_copy(res_o, gwo_out, out_sems.at[1])

        @at_step(0, 0)
        def _():
            barrier = pltpu.get_barrier_semaphore()
            for m in range(1, N_DEV):
                pl.semaphore_signal(barrier, inc=1, device_id=(x ^ (m >> 2), y ^ ((m >> 1) & 1), c ^ (m & 1)),
                                    device_id_type=MESH)
            pl.semaphore_wait(barrier, N_DEV - 1)
            d2d_o.start()

        @at_step(0, 1)
        def _():
            d2d_o.wait_recv()
            for j in range(N_SHARD):
                p_o[j] = (gwo_ref[j, mine_o, :].astype(F32) + sib_o[j].astype(F32)).astype(BF16)
            res_o[mine_o, :] = gwo_ref[k, mine_o, :].astype(F32) + sib_o[k].astype(F32)
            for cp in ici_o:
                cp.start()

        rows = pl.ds(pl.multiple_of(t * TT, TT), TT)

        @pl.when(s < N_SHARD)
        def _():
            dpb = dp_ref[...]
            part = _dot_tn(h_ref[...], dpb)

            @pl.when(t == 0)
            def _():
                acc[...] = part

            @pl.when(t > 0)
            def _():
                acc[...] += part

            d = _dot_nt(dpb, w_ref[0])

            @pl.when(s == 0)
            def _():
                dh[rows, :] = d

            @pl.when(s > 0)
            def _():
                dh[rows, :] += d

        for sv in range(N_SHARD):
            @at_step(sv, nt - 1)
            def _(sv=sv):
                sendbuf[sv] = acc[other, :].astype(BF16)
                if sv < 3:
                    keep[sv] = acc[mine, :].astype(BF16)
                d2d[sv].start()

        for sv in range(3):
            @at_step(sv + 1, 0)
            def _(sv=sv):
                d2d[sv].wait_recv()
                keep[sv] = (keep[sv].astype(F32) + sibrcv[sv].astype(F32)).astype(BF16)
                ici[sv].start()

        @at_norm_block(0)
        def _():
            d2d[3].wait_recv()
            ici[0].wait_recv()
            acc[mine, :] += sibrcv[3].astype(F32) + rcv[0].astype(F32)

        @at_norm_block(1)
        def _():
            tot = res_o[mine_o, :]
            for sv in range(3):
                ici_o[sv].wait_recv()
                tot = tot + rcv_o[sv].astype(F32)
            res_o[mine_o, :] = tot
            fin_o.start()

        @at_norm_block(2)
        def _():
            ici[1].wait_recv()
            acc[mine, :] += rcv[1].astype(F32)

        @at_norm_block(0)
        def _():
            dng[...] = jnp.zeros_like(dng)

        @pl.when(s >= N_SHARD)
        def _():
            blk = (s - N_SHARD) * nt + t
            dhv = dh[pl.ds(pl.multiple_of(blk * TX, TX), TX), :]
            xv = x_ref[...]
            r = lax.rsqrt(jnp.mean(xv * xv, axis=-1, keepdims=True) + EPS)
            xn = xv * r
            dng[...] += jnp.sum(dhv * xn, axis=0, keepdims=True)
            dxn = dhv * g_ref[...]
            gx_ref[...] = dx2_ref[...] + r * (dxn - xn * jnp.mean(dxn * xn, axis=-1, keepdims=True))

        @at_step(n_steps - 1, nt - 1)
        def _():
            sm_buf[me] = sm_ref[...] + smb_ref[...]
            sm_buf[me, 0:1, :] = dng[...]
            for cp in smalls:
                cp.start()
            ici[2].wait_recv()
            acc[mine, :] += rcv[2].astype(F32)
            fin.start()
            store_w.start()
            for m in range(1, N_DEV):
                copy(SEM_SMALL + m, sm_buf.at[0], sm_buf.at[0], sibling).wait_recv()
            tot = sm_buf[0]
            for d in range(1, N_DEV):
                tot = tot + sm_buf[d]
            osm_ref[...] = tot
            fin_o.wait_recv()
            store_o.start()
            fin.wait_recv()
            for cp in d2d + [d2d_o] + ici + ici_o + [fin, fin_o] + smalls:
                cp.wait_send()
            store_o.wait()
            store_w.wait()

    def shard_of(s, kr):
        return kr[0] ^ (3 - jnp.minimum(s, 3))

    def tok(s, t):
        return jnp.where(s < N_SHARD, t, nt - 1)

    def blk_map(s, t, kr):
        return (jnp.where(s < N_SHARD, 0, (s - N_SHARD) * nt + t), 0)

    hbm = pl.BlockSpec(memory_space=pl.ANY)
    grid_spec = pltpu.PrefetchScalarGridSpec(
        num_scalar_prefetch=1, grid=(n_steps, nt),
        in_specs=[pl.BlockSpec((TT, D_MODEL), lambda s, t, kr: (tok(s, t), 0)),
                  pl.BlockSpec((TT, SHARD_COLS), lambda s, t, kr: (tok(s, t), shard_of(s, kr))),
                  pl.BlockSpec((1, D_MODEL, SHARD_COLS), lambda s, t, kr: (shard_of(s, kr), 0, 0)),
                  pl.BlockSpec((N_SHARD, WO_ROWS, D_MODEL), lambda s, t, kr: (0, 0, 0)),
                  pl.BlockSpec((TX, D_MODEL), blk_map),
                  pl.BlockSpec((TX, D_MODEL), blk_map),
                  pl.BlockSpec((1, D_MODEL), lambda s, t, kr: (0, 0)),
                  pl.BlockSpec((8, D_MODEL), lambda s, t, kr: (0, 0)),
                  pl.BlockSpec((8, D_MODEL), lambda s, t, kr: (0, 0))],
        out_specs=(pl.BlockSpec((TX, D_MODEL), blk_map), hbm, hbm,
                   pl.BlockSpec((8, D_MODEL), lambda s, t, kr: (0, 0))),
        scratch_shapes=[pltpu.VMEM((D_MODEL, SHARD_COLS), F32), pltpu.VMEM((SEQ, D_MODEL), F32),
                        pltpu.VMEM((N_SHARD, hw, SHARD_COLS), BF16), pltpu.VMEM((3, hw, SHARD_COLS), BF16),
                        pltpu.VMEM((N_SHARD, hw, SHARD_COLS), BF16), pltpu.VMEM((3, hw, SHARD_COLS), BF16),
                        pltpu.VMEM((N_SHARD, ho, D_MODEL), BF16), pltpu.VMEM((N_SHARD, ho, D_MODEL), BF16),
                        pltpu.VMEM((3, ho, D_MODEL), BF16), pltpu.VMEM((WO_ROWS, D_MODEL), F32),
                        pltpu.VMEM((N_DEV, 8, D_MODEL), F32), pltpu.VMEM((1, D_MODEL), F32),
                        pltpu.SemaphoreType.DMA((N_SEM_TAIL,)), pltpu.SemaphoreType.DMA((N_SEM_TAIL,)),
                        pltpu.SemaphoreType.DMA((2,))])
    return pl.pallas_call(
        body, name="bwd_tail", grid_spec=grid_spec,
        out_shape=(jax.ShapeDtypeStruct((SEQ, D_MODEL), F32),
                   jax.ShapeDtypeStruct((D_MODEL, SHARD_COLS), F32),
                   jax.ShapeDtypeStruct((WO_ROWS, D_MODEL), F32),
                   jax.ShapeDtypeStruct((8, D_MODEL), F32)),
        compiler_params=pltpu.CompilerParams(dimension_semantics=("arbitrary", "arbitrary"),
                                             vmem_limit_bytes=60 * 1024 * 1024, collective_id=COLLECTIVE_TAIL),
    )(kidx, h, dproj, wg, gwo, x2d, dx2, g1, small_a, small_b)


def _adam_update(w, g, m, v):
    nm = ADAM_B1 * m + (1.0 - ADAM_B1) * g
    nv = ADAM_B2 * v + (1.0 - ADAM_B2) * (g * g)
    m_hat = nm / (1.0 - ADAM_B1 ** ADAM_STEP)
    v_hat = nv / (1.0 - ADAM_B2 ** ADAM_STEP)
    return -ADAM_LR * (m_hat / (jnp.sqrt(v_hat) + ADAM_EPS) + ADAM_WD * w), nm, nv


def _adamw_all(tot, g_w_in, g_w_out, big, small, grad_x):
    n = len(small)
    rows = WO_ROWS
    steps = D_MODEL // rows

    def body(tot_ref, *refs):
        gx_ref, gx_out = refs[2 + 3 * (2 + n)], refs[-1]
        gx_out[...] = gx_ref[...]
        ins, outs = refs[:2 + 3 * (2 + n)], refs[3 + 3 * (2 + n):-1]
        g_refs, wmv = ins[:2], ins[2:]
        loss_ref, quads = outs[0], outs[1:]

        def update(j, g):
            w_ref, m_ref, v_ref = wmv[3 * j:3 * j + 3]
            g_ref, d_ref, nm_ref, nv_ref = quads[4 * j:4 * j + 4]
            g_ref[...] = g
            d_ref[...], nm_ref[...], nv_ref[...] = _adam_update(w_ref[...], g, m_ref[...], v_ref[...])

        update(0, g_refs[0][...])

        @pl.when(pl.program_id(0) == 0)
        def _():
            update(1, g_refs[1][...])
            k = 2 * lax.axis_index("x") + lax.axis_index("y")
            mine = pl.ds(pl.multiple_of(k * HEAD, HEAD), HEAD)
            loss_ref[...] = tot_ref[7:8, 0:1]
            grads = [tot_ref[0:1, :], tot_ref[1:2, :], tot_ref[2:3, 0:D_HGRN], tot_ref[2:3, D_HGRN:],
                     jnp.concatenate([tot_ref[3:4, 0:D_HGRN], tot_ref[3:4, D_HGRN:]], axis=0),
                     jnp.concatenate([tot_ref[4 + tap:5 + tap, mine] for tap in range(3)], axis=1)]
            for j, g in enumerate(grads):
                update(2 + j, g)

    whole = lambda a: pl.BlockSpec(a.shape, lambda i: (0, 0))
    blk = pl.BlockSpec((rows, SHARD_COLS), lambda i: (i, 0))
    arrays = [a for triple in big + small for a in triple]
    in_specs = ([whole(tot), blk, whole(g_w_out)] + [blk] * 3 + [whole(a) for a in arrays[3:]])
    shapes = [big[0][0], big[1][0]] + [w for w, _, _ in small]
    out_shape = (jax.ShapeDtypeStruct((1, 1), F32),) + tuple(
        jax.ShapeDtypeStruct(w.shape, F32) for w in shapes for _ in range(4))
    out_specs = (pl.BlockSpec((1, 1), lambda i: (0, 0)),) + (blk,) * 4 + tuple(
        whole(w) for w in shapes[1:] for _ in range(4))
    gx_blk = pl.BlockSpec((SEQ // steps, D_MODEL), lambda i: (i, 0))
    outs = pl.pallas_call(
        body, name="adamw_all", grid=(steps,),
        out_shape=out_shape + (jax.ShapeDtypeStruct(grad_x.shape, F32),),
        in_specs=in_specs + [gx_blk], out_specs=out_specs + (gx_blk,),
        compiler_params=pltpu.CompilerParams(dimension_semantics=("arbitrary",), vmem_limit_bytes=VMEM_LIMIT),
    )(tot, g_w_in, g_w_out, *arrays, grad_x)
    return [outs[0]] + [outs[1 + 4 * j:5 + 4 * j] for j in range(2 + n)] + [outs[-1]]


def _local_step(x2d, tgt, proj, lb_logits, cw, ga, gcn, w_out, gf):
    g64 = _group_matrix(HEAD, CONV_GROUP)
    aux, states, dx2, dmixed, gwo, part_out = _mix_out(proj, lb_logits, cw, ga, gcn, g64, w_out, x2d, gf, tgt)
    dproj, part_mix = _mix_bwd(proj, aux, states, dmixed, lb_logits, cw, ga, gcn, g64)
    return dproj, dx2, gwo.reshape(N_SHARD, WO_ROWS, D_MODEL), part_out, part_mix


def kernel(x, norm_gain, w_in, lb_logits, conv_w, hgrn_norm_gain, conv_norm_gain, w_out, final_norm_gain, loss_target, m_norm_gain, m_w_in, m_lb_logits, m_conv_w, m_hgrn_norm_gain, m_conv_norm_gain, m_w_out, m_final_norm_gain, v_norm_gain, v_w_in, v_lb_logits, v_conv_w, v_hgrn_norm_gain, v_conv_norm_gain, v_w_out, v_final_norm_gain):
    k = 2 * lax.axis_index("x") + lax.axis_index("y")
    kidx = jnp.reshape(k, (1,)).astype(jnp.int32)
    row = lambda a: a.reshape(1, D_MODEL)
    taps = lambda a: a.reshape(1, 3 * HEAD)
    h, proj, wg, cw = _gather_proj(kidx, x[0], norm_gain, w_in, taps(conv_w))
    dproj, dx2, gwo, part_out, part_mix = _local_step(
        x[0], loss_target[0], proj, lb_logits, cw, hgrn_norm_gain, conv_norm_gain, w_out, row(final_norm_gain))
    rgrad_x, rg_w_in, rg_w_out, tot = _bwd_tail(kidx, h, dproj, wg, gwo, x[0], dx2, norm_gain, part_out, part_mix)

    (loss, (g_w_in, d_w_in, nm_w_in, nv_w_in), (g_w_out, d_w_out, nm_w_out, nv_w_out),
     (g_norm_gain, d_ng, nm_ng, nv_ng), (g_final, d_fg, nm_fg, nv_fg), (g_hgrn, d_hg, nm_hg, nv_hg),
     (g_convn, d_cg, nm_cg, nv_cg), (g_lb, d_lb, nm_lb, nv_lb), (g_conv_w, d_cw, nm_cw, nv_cw),
     grad_x) = _adamw_all(
        tot, rg_w_in, rg_w_out,
        [(w_in[0], m_w_in[0], v_w_in[0]), (w_out[0], m_w_out[0], v_w_out[0])],
        [(norm_gain, m_norm_gain, v_norm_gain),
         (row(final_norm_gain), row(m_final_norm_gain), row(v_final_norm_gain)),
         (hgrn_norm_gain, m_hgrn_norm_gain, v_hgrn_norm_gain),
         (conv_norm_gain, m_conv_norm_gain, v_conv_norm_gain),
         (lb_logits, m_lb_logits, v_lb_logits),
         (taps(conv_w), taps(m_conv_w), taps(v_conv_w))],
        rgrad_x)
    flat = lambda a: a.reshape(D_MODEL)
    untap = lambda a: a.reshape(1, 3, HEAD)
    return (loss.reshape(()), grad_x[None],
            g_norm_gain, g_w_in[None], g_lb, untap(g_conv_w), g_hgrn, g_convn, g_w_out[None], flat(g_final),
            d_ng, d_w_in[None], d_lb, untap(d_cw), d_hg, d_cg, d_w_out[None], flat(d_fg),
            nm_ng, nm_w_in[None], nm_lb, untap(nm_cw), nm_hg, nm_cg, nm_w_out[None], flat(nm_fg),
            nv_ng, nv_w_in[None], nv_lb, untap(nv_cw), nv_hg, nv_cg, nv_w_out[None], flat(nv_fg))
```

```python
import jax
import jax.numpy as jnp
import numpy as np
from jax import lax
from jax.experimental import pallas as pl
from jax.experimental.pallas import tpu as pltpu

F32 = jnp.float32
BF16 = jnp.bfloat16
MESH = pl.DeviceIdType.MESH

SEQ = 2048
D_MODEL = 1024
D_HGRN = 512
D_CONV = 512
HEAD = 128
N_HEADS = 4
CHUNK = 64
CONV_GROUP = 64
N_SHARD = 4
SHARD_COLS = 1024
WO_ROWS = 256
EPS = 1e-6
TB = 256
NCB = TB // CHUNK
N_CHUNKS = SEQ // CHUNK
N_DEV = 8
COLLECTIVE_GATHER, COLLECTIVE_MIX_OUT, COLLECTIVE_TAIL = 1, 0, 2
AUX_O, AUX_CV, AUX_B, AUX_COLS = 0, 512, 1024, 1536

ADAM_LR = 0.001
ADAM_B1 = 0.9
ADAM_B2 = 0.999
ADAM_EPS = 1e-08
ADAM_WD = 0.01
ADAM_STEP = 10

VMEM_LIMIT = 56 * 1024 * 1024


def _dot(a, b):
    return jnp.dot(a, b, preferred_element_type=F32)


def _dot_nt(a, b):
    return lax.dot_general(a, b, (((1,), (1,)), ((), ())), preferred_element_type=F32)


def _dot_tn(a, b):
    return lax.dot_general(a, b, (((0,), (0,)), ((), ())), preferred_element_type=F32)


def _split_bf16(x, n):
    parts = []
    r = x
    for _ in range(n):
        p = r.astype(BF16)
        parts.append(p)
        r = r - p.astype(F32)
    return parts


def _exact_left(m, x, n=3):
    acc = None
    for p in _split_bf16(x, n):
        t = _dot(m, p)
        acc = t if acc is None else acc + t
    return acc


def _exact_left_many(m, xs, n=3):
    parts = [_split_bf16(x, n) for x in xs]
    accs = [None] * len(xs)
    for i in range(n):
        for j in range(len(xs)):
            t = _dot(m, parts[j][i])
            accs[j] = t if accs[j] is None else accs[j] + t
    return accs


def _group_mean_many(xs, gmat, n=2):
    parts = [_split_bf16(x, n) for x in xs]
    accs = [None] * len(xs)
    for i in range(n):
        for j in range(len(xs)):
            t = _dot(parts[j][i], gmat)
            accs[j] = t if accs[j] is None else accs[j] + t
    return accs


def _group_mean(x, gmat, n=2):
    w = gmat.shape[0]
    outs = []
    for c0 in range(0, x.shape[1], w):
        acc = None
        for p in _split_bf16(x[:, c0:c0 + w], n):
            t = _dot(p, gmat)
            acc = t if acc is None else acc + t
        outs.append(acc)
    return jnp.concatenate(outs, axis=1)


def _sigmoid(x):
    return 1.0 / (1.0 + jnp.exp(-x))


def _lower_bound(lbl):
    l0 = lbl[0:1, :]
    l1 = lbl[1:2, :]
    m = jnp.maximum(l0, l1)
    e0 = jnp.exp(l0 - m)
    e1 = jnp.exp(l1 - m)
    return e0 / (e0 + e1)


def _tri(lower):
    r = lax.broadcasted_iota(jnp.int32, (CHUNK, CHUNK), 0)
    c = lax.broadcasted_iota(jnp.int32, (CHUNK, CHUNK), 1)
    return jnp.where((c <= r) if lower else (c >= r), 1.0, 0.0).astype(BF16)


def _causal():
    r = lax.broadcasted_iota(jnp.int32, (CHUNK, CHUNK), 0)
    c = lax.broadcasted_iota(jnp.int32, (CHUNK, CHUNK), 1)
    return c <= r


def _shift_down(x, sh, prev_tail):
    r = pltpu.roll(x, sh, 0)
    pt = pltpu.roll(prev_tail, sh, 0)
    rows = lax.broadcasted_iota(jnp.int32, prev_tail.shape, 0)
    top = jnp.where(rows < sh, pt, r[0:8])
    return jnp.concatenate([top, r[8:]], axis=0)


def _shift_up(x, sh, next_head):
    n = x.shape[0]
    r = pltpu.roll(x, n - sh, 0)
    nh = pltpu.roll(next_head, 8 - sh, 0)
    rows = lax.broadcasted_iota(jnp.int32, next_head.shape, 0)
    bot = jnp.where(rows >= 8 - sh, nh, r[n - 8:])
    return jnp.concatenate([r[:n - 8], bot], axis=0)


def _group_matrix(width, group):
    r = np.arange(width)[:, None] // group
    c = np.arange(width)[None, :] // group
    return jnp.asarray(np.where(r == c, 1.0 / group, 0.0), dtype=BF16)


TG = 1024
SEM_W, SEM_CW, SEM_W_FWD, N_SEM = 0, 4, 7, 11


def _gather_proj(kidx, x2d, g1, w_in, conv_w):
    half_w = D_MODEL // 2
    half_c = SHARD_COLS // 2
    nt = SEQ // TG
    n_steps = 2 * N_SHARD

    def body(k_ref, x_ref, g_ref, w_ref, cw_ref, h_ref, p_ref, wg_out, cwg_out,
             wg_v, cwg_v, send_sems, recv_sems, out_sems):
        s, t = pl.program_id(0), pl.program_id(1)
        x, y, c = lax.axis_index("x"), lax.axis_index("y"), lax.axis_index("c")
        k = 2 * x + y
        sibling = (x, y, 1 - c)
        chips = [(1 - x, y), (x, 1 - y), (1 - x, 1 - y)]
        kjs = [2 * cx + cy for cx, cy in chips]
        diag = (*chips[2], c)

        def w_half(kk, cc):
            return wg_v.at[kk, pl.ds(cc * half_w, half_w), :]

        def w_quarter(kk, cc, piece):
            return wg_v.at[kk, pl.ds(cc * half_w, half_w), piece * half_c:(piece + 1) * half_c]

        def cw_of(kk):
            return cwg_v.at[:, pl.ds(pl.multiple_of(kk * HEAD, HEAD), HEAD)]

        def copy(sem, ref, to):
            return pltpu.make_async_remote_copy(
                src_ref=ref, dst_ref=ref, send_sem=send_sems.at[sem], recv_sem=recv_sems.at[sem],
                device_id=to, device_id_type=MESH)

        def at_step(sv, tv):
            return pl.when((s == sv) & (t == tv))

        w_direct = ([copy(SEM_W + j, w_half(k, c), (*chips[j], c)) for j in range(2)]
                    + [copy(SEM_W + 2 + p, w_quarter(k, c, p), diag) for p in range(2)])
        cw_direct = [copy(SEM_CW + j, cw_of(k), (*chip, c)) for j, chip in enumerate(chips)]
        w_passed = ([copy(SEM_W_FWD + j, w_half(kjs[j], c), sibling) for j in range(2)]
                    + [copy(SEM_W_FWD + 2 + p, w_quarter(kjs[2], c, p), sibling) for p in range(2)])
        stores = ([pltpu.make_async_copy(wg_v.at[kk], wg_out.at[kk], out_sems.at[i])
                   for i, kk in enumerate([k] + kjs)]
                  + [pltpu.make_async_copy(cwg_v, cwg_out, out_sems.at[4])])

        @at_step(0, 0)
        def _():
            barrier = pltpu.get_barrier_semaphore()
            for peer in [sibling] + [(*chip, c) for chip in chips]:
                pl.semaphore_signal(barrier, inc=1, device_id=peer, device_id_type=MESH)
            wg_v[k] = w_ref[0].astype(BF16)
            mine = pl.ds(pl.multiple_of(k * HEAD, HEAD), HEAD)
            cwg_v[:, mine] = jnp.zeros((8, HEAD), F32)
            for tap in range(3):
                cwg_v[tap:tap + 1, mine] = cw_ref[:, tap * HEAD:(tap + 1) * HEAD]
            pl.semaphore_wait(barrier, 4)
            w_direct[0].start()
            w_direct[1].start()
            for cp in cw_direct:
                cp.start()
            stores[0].start()

        @at_step(2, 0)
        def _():
            for j in range(2):
                copy(SEM_W + j, w_half(kjs[j], c), sibling).wait_recv()
                w_passed[j].start()
            w_direct[2].start()
            w_direct[3].start()
            copy(SEM_W_FWD, w_half(kjs[0], 1 - c), sibling).wait_recv()
            stores[1].start()

        @at_step(4, 0)
        def _():
            copy(SEM_W_FWD + 1, w_half(kjs[1], 1 - c), sibling).wait_recv()
            stores[2].start()

        for p in range(2):
            @at_step(6 + p, 0)
            def _(p=p):
                copy(SEM_W + 2 + p, w_quarter(kjs[2], c, p), sibling).wait_recv()
                w_passed[2 + p].start()
                copy(SEM_W_FWD + 2 + p, w_quarter(kjs[2], 1 - c, p), sibling).wait_recv()

        rows = pl.ds(pl.multiple_of(t * TG, TG), TG)

        @pl.when(s == 0)
        def _():
            xv = x_ref[...]
            r = lax.rsqrt(jnp.mean(xv * xv, axis=-1, keepdims=True) + EPS)
            h_ref[rows, :] = (xv * r * g_ref[...]).astype(BF16)

        sh = s >> 1
        js = k ^ (((sh & 1) << 1) | (sh >> 1))
        for piece in range(2):
            @pl.when((s & 1) == piece)
            def _(piece=piece):
                p_ref[...] = _dot(h_ref[rows, :], wg_v[js, :, piece * half_c:(piece + 1) * half_c])

        @at_step(n_steps - 1, nt - 1)
        def _():
            stores[3].start()
            for j in range(3):
                copy(SEM_CW + j, cw_of(kjs[j]), sibling).wait_recv()
            stores[4].start()
            for cp in w_direct + cw_direct + w_passed:
                cp.wait_send()
            for st in stores:
                st.wait()

    def x_map(s, t, kr):
        return (jnp.where(s == 0, t, nt - 1), 0)

    def p_map(s, t, kr):
        sh = s >> 1
        return (t, 2 * (kr[0] ^ (((sh & 1) << 1) | (sh >> 1))) + (s & 1))

    hbm = pl.BlockSpec(memory_space=pl.ANY)
    grid_spec = pltpu.PrefetchScalarGridSpec(
        num_scalar_prefetch=1, grid=(n_steps, nt),
        in_specs=[pl.BlockSpec((TG, D_MODEL), x_map),
                  pl.BlockSpec((1, D_MODEL), lambda s, t, kr: (0, 0)),
                  pl.BlockSpec((1, D_MODEL, SHARD_COLS), lambda s, t, kr: (0, 0, 0)),
                  pl.BlockSpec((1, 3 * HEAD), lambda s, t, kr: (0, 0))],
        out_specs=(pl.BlockSpec((SEQ, D_MODEL), lambda s, t, kr: (0, 0)),
                   pl.BlockSpec((TG, half_c), p_map), hbm, hbm),
        scratch_shapes=[pltpu.VMEM((N_SHARD, D_MODEL, SHARD_COLS), BF16),
                        pltpu.VMEM((8, D_CONV), F32),
                        pltpu.SemaphoreType.DMA((N_SEM,)), pltpu.SemaphoreType.DMA((N_SEM,)),
                        pltpu.SemaphoreType.DMA((5,))])
    return pl.pallas_call(
        body, name="gather_proj", grid_spec=grid_spec,
        out_shape=(jax.ShapeDtypeStruct((SEQ, D_MODEL), BF16),
                   jax.ShapeDtypeStruct((SEQ, N_SHARD * SHARD_COLS), F32),
                   jax.ShapeDtypeStruct((N_SHARD, D_MODEL, SHARD_COLS), BF16),
                   jax.ShapeDtypeStruct((8, D_CONV), F32)),
        compiler_params=pltpu.CompilerParams(dimension_semantics=("arbitrary", "arbitrary"),
                                             vmem_limit_bytes=VMEM_LIMIT, collective_id=COLLECTIVE_GATHER),
    )(kidx, x2d, g1, w_in, conv_w)


LAG = 6


def _mix_out(proj, lb_logits, cw, ga, gcn, g64, w_out, x2d, gf, tgt):
    half_o = WO_ROWS // 2
    nblk = SEQ // TB
    n_steps = nblk + LAG

    def body(p_ref, lbl_ref, cw_ref, ga_ref, gcn_ref, g64_ref, wo_ref, x_ref, gf_ref, t_ref,
             aux_ref, sto_ref, dx2_ref, dm_ref, gwo_ref, part_ref,
             st_ref, tail_ref, wog_v, stage, ring, acc_ref, send_sems, recv_sems):
        i = pl.program_id(0)
        x, y, c = lax.axis_index("x"), lax.axis_index("y"), lax.axis_index("c")
        k = 2 * x + y
        sibling = (x, y, 1 - c)
        chips = [(1 - x, y), (x, 1 - y), (1 - x, 1 - y)]
        kjs = [2 * cx + cy for cx, cy in chips]

        def wo_half(kk, cc):
            return wog_v.at[pl.ds(pl.multiple_of(kk * WO_ROWS + cc * half_o, half_o), half_o), :]

        def copy(sem, ref, to):
            return pltpu.make_async_remote_copy(
                src_ref=ref, dst_ref=ref, send_sem=send_sems.at[sem], recv_sem=recv_sems.at[sem],
                device_id=to, device_id_type=MESH)

        wo_direct = [copy(j, wo_half(k, c), (*chip, c)) for j, chip in enumerate(chips)]
        wo_passed = [copy(3 + j, wo_half(kj, c), sibling) for j, kj in enumerate(kjs)]

        @pl.when(i == 0)
        def _():
            barrier = pltpu.get_barrier_semaphore()
            for peer in [sibling] + [(*chip, c) for chip in chips]:
                pl.semaphore_signal(barrier, inc=1, device_id=peer, device_id_type=MESH)
            st_ref[...] = jnp.zeros_like(st_ref)
            tail_ref[...] = jnp.zeros_like(tail_ref)
            acc_ref[...] = jnp.zeros_like(acc_ref)
            part_ref[...] = jnp.zeros_like(part_ref)
            wog_v[pl.ds(pl.multiple_of(k * WO_ROWS, WO_ROWS), WO_ROWS), :] = wo_ref[0].astype(BF16)
            pl.semaphore_wait(barrier, 4)
            for cp in wo_direct:
                cp.start()

        @pl.when(i == LAG - 1)
        def _():
            for j in range(3):
                copy(j, wo_half(kjs[j], c), sibling).wait_recv()
                wo_passed[j].start()

        @pl.when(i == LAG)
        def _():
            for j in range(3):
                copy(3 + j, wo_half(kjs[j], 1 - c), sibling).wait_recv()

        lb = _lower_bound(lbl_ref[...])
        tri = _tri(True)
        causal = _causal()
        g64m = g64_ref[...]
        heads = range(N_HEADS)
        cs = [slice(hd * HEAD, (hd + 1) * HEAD) for hd in heads]
        col = lambda base, hd: slice(base + hd * HEAD, base + (hd + 1) * HEAD)

        def mix_chunk(n):
            sl = pl.ds(n * CHUNK, CHUNK)
            sg = [_sigmoid(p_ref[sl, col(512, hd)]) for hd in heads]
            f = [lb[:, cs[hd]] + (1.0 - lb[:, cs[hd]]) * sg[hd] for hd in heads]
            bc = _exact_left_many(tri, [jnp.log(f[hd]) for hd in heads])
            for hd in heads:
                aux_ref[sl, col(AUX_B, hd)] = bc[hd]
            g = [bc[hd][CHUNK - 1:CHUNK, :] for hd in heads]
            qd = [(p_ref[sl, col(0, hd)] * jnp.exp(bc[hd])).astype(BF16) for hd in heads]
            ki = [((1.0 - f[hd]) * jnp.exp(-bc[hd])).astype(BF16) for hd in heads]
            ke = [((1.0 - f[hd]) * jnp.exp(g[hd] - bc[hd])).astype(BF16) for hd in heads]
            vb = [p_ref[sl, col(1024, hd)].astype(BF16) for hd in heads]
            st = [st_ref[hd] for hd in heads]
            st_b = [a.astype(BF16) for a in st]
            for hd in heads:
                sto_ref[n, hd] = st_b[hd]
            scm = [_dot_nt(qd[hd], ki[hd]) for hd in heads]
            inter = [_dot_nt(qd[hd], st_b[hd]) for hd in heads]
            upd = [_dot_tn(vb[hd], ke[hd]) for hd in heads]
            intra = [_dot(jnp.where(causal, scm[hd], 0.0).astype(BF16), vb[hd]) for hd in heads]
            for hd in heads:
                st_ref[hd] = st[hd] * jnp.exp(g[hd]) + upd[hd]
                o = intra[hd] + inter[hd]
                aux_ref[sl, col(AUX_O, hd)] = o
                ra = lax.rsqrt(jnp.mean(o * o, axis=-1, keepdims=True) + EPS)
                za = p_ref[sl, col(1536, hd)]
                stage[sl, cs[hd]] = (o * ra * ga_ref[:, cs[hd]] * (za * _sigmoid(za))).astype(BF16)
            yb = []
            for hd in heads:
                cu = p_ref[sl, col(3072, hd)] * p_ref[sl, col(2048, hd)]
                tail = tail_ref[:, cs[hd]]
                cv = (cw_ref[0:1, cs[hd]] * _shift_down(cu, 2, tail) + cw_ref[1:2, cs[hd]] * _shift_down(cu, 1, tail)
                      + cw_ref[2:3, cs[hd]] * cu)
                tail_ref[:, cs[hd]] = cu[CHUNK - 8:, :]
                aux_ref[sl, col(AUX_CV, hd)] = cv
                yb.append(p_ref[sl, col(2560, hd)] * cv)
            ms = _group_mean_many([y * y for y in yb], g64m)
            for hd in heads:
                rb = lax.rsqrt(ms[hd] + EPS)
                zb = p_ref[sl, col(3584, hd)]
                stage[sl, col(512, hd)] = (yb[hd] * rb * gcn_ref[:, cs[hd]] * (zb * _sigmoid(zb))).astype(BF16)

        def step(mix, project):
            if project:
                mixed_b = ring[pl.ds(pl.multiple_of((i - LAG) * TB, TB), TB), :]
                y = _dot(mixed_b, wog_v[...])
            if mix:
                mix_chunk(0)
            if project:
                x2 = x_ref[...] + y
                r2 = lax.rsqrt(jnp.mean(x2 * x2, axis=-1, keepdims=True) + EPS)
                n2 = x2 * r2
                gfv = gf_ref[...]
                err = n2 * gfv - t_ref[...]
                loss = 0.5 * jnp.sum(jnp.mean(err * err, axis=-1, keepdims=True), axis=0, keepdims=True)
                dy = err * (1.0 / D_MODEL)
                part_ref[1:2, :] += jnp.sum(dy * n2, axis=0, keepdims=True)
                part_ref[7:8, :] += jnp.broadcast_to(loss, (1, D_MODEL))
                dn = dy * gfv
                dx2 = r2 * (dn - n2 * jnp.mean(dn * n2, axis=-1, keepdims=True))
                dx2_ref[...] = dx2
                dx2_b = dx2.astype(BF16)
            if mix:
                mix_chunk(1)
            if project:
                dm_ref[...] = _dot_nt(dx2_b, wog_v[...])
            if mix:
                mix_chunk(2)
            if project:
                acc_ref[...] += _dot_tn(mixed_b, dx2_b)
            if mix:
                mix_chunk(3)
                ring[pl.ds(pl.multiple_of(i * TB, TB), TB), :] = stage[...]

        @pl.when(i < LAG)
        def _():
            step(True, False)

        @pl.when((i >= LAG) & (i < nblk))
        def _():
            step(True, True)

        @pl.when(i >= nblk)
        def _():
            step(False, True)

        @pl.when(i == n_steps - 1)
        def _():
            gwo_ref[...] = acc_ref[...].astype(BF16)
            for cp in wo_direct + wo_passed:
                cp.wait_send()

    assert NCB == 4
    row = lambda w: pl.BlockSpec((1, w), lambda i: (0, 0))
    mix_blk = lambda i: jnp.minimum(i, nblk - 1)
    out_blk = lambda i: jnp.clip(i - LAG, 0, nblk - 1)
    tok = lambda: pl.BlockSpec((TB, D_MODEL), lambda i: (out_blk(i), 0))
    return pl.pallas_call(
        body, name="mix_out", grid=(n_steps,),
        out_shape=(jax.ShapeDtypeStruct((SEQ, AUX_COLS), F32),
                   jax.ShapeDtypeStruct((N_CHUNKS, N_HEADS, HEAD, HEAD), BF16),
                   jax.ShapeDtypeStruct((SEQ, D_MODEL), F32),
                   jax.ShapeDtypeStruct((SEQ, D_MODEL), F32),
                   jax.ShapeDtypeStruct((D_MODEL, D_MODEL), BF16),
                   jax.ShapeDtypeStruct((8, D_MODEL), F32)),
        in_specs=[pl.BlockSpec((TB, 4096), lambda i: (jnp.minimum(i, nblk - 1), 0)),
                  pl.BlockSpec((2, D_HGRN), lambda i: (0, 0)),
                  pl.BlockSpec((8, D_CONV), lambda i: (0, 0)),
                  row(D_HGRN), row(D_CONV),
                  pl.BlockSpec((HEAD, HEAD), lambda i: (0, 0)),
                  pl.BlockSpec((1, WO_ROWS, D_MODEL), lambda i: (0, 0, 0)),
                  tok(), row(D_MODEL), tok()],
        out_specs=(pl.BlockSpec((TB, AUX_COLS), lambda i: (mix_blk(i), 0)),
                   pl.BlockSpec((NCB, N_HEADS, HEAD, HEAD), lambda i: (mix_blk(i), 0, 0, 0)),
                   tok(), tok(),
                   pl.BlockSpec((D_MODEL, D_MODEL), lambda i: (0, 0)),
                   pl.BlockSpec((8, D_MODEL), lambda i: (0, 0))),
        scratch_shapes=[pltpu.VMEM((N_HEADS, HEAD, HEAD), F32), pltpu.VMEM((8, D_CONV), F32),
                        pltpu.VMEM((D_MODEL, D_MODEL), BF16), pltpu.VMEM((TB, D_MODEL), BF16),
                        pltpu.VMEM((SEQ, D_MODEL), BF16), pltpu.VMEM((D_MODEL, D_MODEL), F32),
                        pltpu.SemaphoreType.DMA((6,)), pltpu.SemaphoreType.DMA((6,))],
        compiler_params=pltpu.CompilerParams(dimension_semantics=("arbitrary",), vmem_limit_bytes=VMEM_LIMIT,
                                             collective_id=COLLECTIVE_MIX_OUT),
    )(proj, lb_logits, cw, ga, gcn, g64, w_out, x2d, gf, tgt)


def _mix_bwd(proj, aux, states, dmixed, lb_logits, cw, ga, gcn, g64):
    nblk = SEQ // TB

    def body(p_ref, aux_ref, st_ref, dm_ref, lbl_ref, cw_ref, ga_ref, gcn_ref, g64_ref,
             dp_ref, part_ref, dst_ref, head_ref, dlb_ref):
        i = pl.program_id(0)

        @pl.when(i == 0)
        def _():
            dst_ref[...] = jnp.zeros_like(dst_ref)
            head_ref[...] = jnp.zeros_like(head_ref)
            part_ref[...] = jnp.zeros_like(part_ref)
            dlb_ref[...] = jnp.zeros_like(dlb_ref)

        lb = _lower_bound(lbl_ref[...])
        triu = _tri(False)
        causal = _causal()
        g64m = g64_ref[...]
        rowsum = lambda a: jnp.sum(a, axis=0, keepdims=True)
        heads = range(N_HEADS)
        cs = [slice(hd * HEAD, (hd + 1) * HEAD) for hd in heads]
        col = lambda base, hd: slice(base + hd * HEAD, base + (hd + 1) * HEAD)
        for n in reversed(range(NCB)):
            sl = pl.ds(n * CHUNK, CHUNK)
            cvv = [aux_ref[sl, col(AUX_CV, hd)] for hd in heads]
            gb = [p_ref[sl, col(2560, hd)] for hd in heads]
            yb = [gb[hd] * cvv[hd] for hd in heads]
            ms = _group_mean_many([y * y for y in yb], g64m)
            rb, nb, dnb = [], [], []
            for hd in heads:
                rb.append(lax.rsqrt(ms[hd] + EPS))
                nb.append(yb[hd] * rb[hd])
                zb = p_ref[sl, col(3584, hd)]
                sgb = _sigmoid(zb)
                dmb = dm_ref[sl, col(512, hd)]
                gcv = gcn_ref[:, cs[hd]]
                part_ref[2:3, col(512, hd)] += rowsum(dmb * nb[hd] * (zb * sgb))
                dp_ref[sl, col(3584, hd)] = (dmb * nb[hd] * gcv * (sgb * (1.0 + zb * (1.0 - sgb)))).astype(BF16)
                dnb.append(dmb * gcv * (zb * sgb))
            mdn = _group_mean_many([dnb[hd] * nb[hd] for hd in heads], g64m)
            for hd in heads:
                dyb = rb[hd] * (dnb[hd] - nb[hd] * mdn[hd])
                dp_ref[sl, col(2560, hd)] = (dyb * cvv[hd]).astype(BF16)
                dcv = dyb * gb[hd]
                head = head_ref[:, cs[hd]]
                dcv1 = _shift_up(dcv, 1, head)
                dcv2 = _shift_up(dcv, 2, head)
                head_ref[:, cs[hd]] = dcv[0:8, :]
                u = p_ref[sl, col(2048, hd)]
                gc = p_ref[sl, col(3072, hd)]
                cu = gc * u
                part_ref[4:5, cs[hd]] += rowsum(dcv2 * cu)
                part_ref[5:6, cs[hd]] += rowsum(dcv1 * cu)
                part_ref[6:7, cs[hd]] += rowsum(dcv * cu)
                dcu = cw_ref[2:3, cs[hd]] * dcv + cw_ref[1:2, cs[hd]] * dcv1 + cw_ref[0:1, cs[hd]] * dcv2
                dp_ref[sl, col(3072, hd)] = (dcu * u).astype(BF16)
                dp_ref[sl, col(2048, hd)] = (dcu * gc).astype(BF16)
            do_b = []
            for hd in heads:
                ov = aux_ref[sl, col(AUX_O, hd)]
                ra = lax.rsqrt(jnp.mean(ov * ov, axis=-1, keepdims=True) + EPS)
                na = ov * ra
                za = p_ref[sl, col(1536, hd)]
                sga = _sigmoid(za)
                dma = dm_ref[sl, cs[hd]]
                gav = ga_ref[:, cs[hd]]
                part_ref[2:3, cs[hd]] += rowsum(dma * na * (za * sga))
                dp_ref[sl, col(1536, hd)] = (dma * na * gav * (sga * (1.0 + za * (1.0 - sga)))).astype(BF16)
                dna = dma * gav * (za * sga)
                do_b.append((ra * (dna - na * jnp.mean(dna * na, axis=-1, keepdims=True))).astype(BF16))
            s = [_sigmoid(p_ref[sl, col(512, hd)]) for hd in heads]
            f = [lb[:, cs[hd]] + (1.0 - lb[:, cs[hd]]) * s[hd] for hd in heads]
            bc = [aux_ref[sl, col(AUX_B, hd)] for hd in heads]
            g = [bc[hd][CHUNK - 1:CHUNK, :] for hd in heads]
            eb = [jnp.exp(bc[hd]) for hd in heads]
            enb = [jnp.exp(-bc[hd]) for hd in heads]
            eg = [jnp.exp(g[hd] - bc[hd]) for hd in heads]
            dec = [jnp.exp(g[hd]) for hd in heads]
            qd = [p_ref[sl, cs[hd]] * eb[hd] for hd in heads]
            ki = [(1.0 - f[hd]) * enb[hd] for hd in heads]
            ke = [(1.0 - f[hd]) * eg[hd] for hd in heads]
            qd_b = [a.astype(BF16) for a in qd]
            ki_b = [a.astype(BF16) for a in ki]
            ke_b = [a.astype(BF16) for a in ke]
            vb = [p_ref[sl, col(1024, hd)].astype(BF16) for hd in heads]
            st_b = [st_ref[n, hd] for hd in heads]
            dst = [dst_ref[hd] for hd in heads]
            dst_b = [a.astype(BF16) for a in dst]
            scm = [_dot_nt(qd_b[hd], ki_b[hd]) for hd in heads]
            amm = [_dot_nt(do_b[hd], vb[hd]) for hd in heads]
            dqd2 = [_dot(do_b[hd], st_b[hd]) for hd in heads]
            dke = [_dot(vb[hd], dst_b[hd]) for hd in heads]
            dv2 = [_dot_nt(ke_b[hd], dst_b[hd]) for hd in heads]
            dsu = [_dot_tn(do_b[hd], qd_b[hd]) for hd in heads]
            sc = [jnp.where(causal, scm[hd], 0.0).astype(BF16) for hd in heads]
            am = [jnp.where(causal, amm[hd], 0.0).astype(BF16) for hd in heads]
            dqd1 = [_dot(am[hd], ki_b[hd]) for hd in heads]
            dki = [_dot_tn(am[hd], qd_b[hd]) for hd in heads]
            dv1 = [_dot_tn(sc[hd], do_b[hd]) for hd in heads]
            db, dgv = [], []
            for hd in heads:
                dqd = dqd1[hd] + dqd2[hd]
                ddec = rowsum(dst[hd] * st_b[hd].astype(F32))
                dst_ref[hd] = dst[hd] * dec[hd] + dsu[hd]
                dp_ref[sl, cs[hd]] = (dqd * eb[hd]).astype(BF16)
                dp_ref[sl, col(1024, hd)] = (dv1[hd] + dv2[hd]).astype(BF16)
                db.append(dqd * qd[hd] - dki[hd] * ki[hd] - dke[hd] * ke[hd])
                dgv.append(rowsum(dke[hd] * ke[hd]) + ddec * dec[hd])
            rc = _exact_left_many(triu, db, 2)
            for hd in heads:
                df = (rc[hd] + dgv[hd]) / f[hd] - (dki[hd] * enb[hd] + dke[hd] * eg[hd])
                dlb_ref[:, cs[hd]] += rowsum(df * (1.0 - s[hd]))
                dp_ref[sl, col(512, hd)] = (df * (1.0 - lb[:, cs[hd]]) * s[hd] * (1.0 - s[hd])).astype(BF16)

        @pl.when(i == nblk - 1)
        def _():
            row = dlb_ref[...] * lb * (1.0 - lb)
            part_ref[3:4, 0:D_HGRN] = row
            part_ref[3:4, D_HGRN:] = -row

    rev = lambda w: pl.BlockSpec((TB, w), lambda i: (nblk - 1 - i, 0))
    row = lambda w: pl.BlockSpec((1, w), lambda i: (0, 0))
    return pl.pallas_call(
        body, name="mix_bwd", grid=(nblk,),
        out_shape=(jax.ShapeDtypeStruct((SEQ, 4096), BF16),
                   jax.ShapeDtypeStruct((8, D_MODEL), F32)),
        in_specs=[rev(4096), rev(AUX_COLS),
                  pl.BlockSpec((NCB, N_HEADS, HEAD, HEAD), lambda i: (nblk - 1 - i, 0, 0, 0)),
                  rev(D_MODEL),
                  pl.BlockSpec((2, D_HGRN), lambda i: (0, 0)),
                  pl.BlockSpec((8, D_CONV), lambda i: (0, 0)),
                  row(D_HGRN), row(D_CONV),
                  pl.BlockSpec((HEAD, HEAD), lambda i: (0, 0))],
        out_specs=(rev(4096), pl.BlockSpec((8, D_MODEL), lambda i: (0, 0))),
        scratch_shapes=[pltpu.VMEM((N_HEADS, HEAD, HEAD), F32), pltpu.VMEM((8, D_CONV), F32),
                        pltpu.VMEM((1, D_HGRN), F32)],
        compiler_params=pltpu.CompilerParams(dimension_semantics=("arbitrary",), vmem_limit_bytes=VMEM_LIMIT),
    )(proj, aux, states, dmixed, lb_logits, cw, ga, gcn, g64)


TT = 1024
TX = 256
(SEM_D2D, SEM_D2D_O, SEM_ICI, SEM_ICI_O, SEM_FIN, SEM_FIN_O, SEM_SMALL, N_SEM_TAIL) = 0, 4, 5, 8, 11, 12, 12, 20


def _bwd_tail(kidx, h, dproj, wg, gwo, x2d, dx2, g1, small_a, small_b):
    hw = D_MODEL // 2
    ho = WO_ROWS // 2
    nt = SEQ // TT
    n_steps = N_SHARD + SEQ // TX // nt

    def body(k_ref, h_ref, dp_ref, w_ref, gwo_ref, x_ref, dx2_ref, g_ref, sm_ref, smb_ref,
             gx_ref, gw_out, gwo_out, osm_ref,
             acc, dh, sendbuf, keep, sibrcv, rcv, sib_o, p_o, rcv_o, res_o, sm_buf, dng,
             send_sems, recv_sems, out_sems):
        s, t = pl.program_id(0), pl.program_id(1)
        x, y, c = lax.axis_index("x"), lax.axis_index("y"), lax.axis_index("c")
        k = 2 * x + y
        me = 4 * x + 2 * y + c
        sibling = (x, y, 1 - c)
        chips = [(1 - x, 1 - y), (1 - x, y), (x, 1 - y)]
        kjs = [2 * cx + cy for cx, cy in chips]
        mine = pl.ds(pl.multiple_of(c * hw, hw), hw)
        other = pl.ds(pl.multiple_of((1 - c) * hw, hw), hw)
        mine_o = pl.ds(pl.multiple_of(c * ho, ho), ho)
        other_o = pl.ds(pl.multiple_of((1 - c) * ho, ho), ho)

        def copy(sem, src, dst, to):
            return pltpu.make_async_remote_copy(
                src_ref=src, dst_ref=dst, send_sem=send_sems.at[sem], recv_sem=recv_sems.at[sem],
                device_id=to, device_id_type=MESH)

        def at_step(sv, tv):
            return pl.when((s == sv) & (t == tv))

        def at_norm_block(b):
            return at_step(N_SHARD + b // nt, b % nt)

        d2d = [copy(SEM_D2D + sv, sendbuf.at[sv], sibrcv.at[sv], sibling) for sv in range(N_SHARD)]
        d2d_o = copy(SEM_D2D_O, gwo_ref.at[:, other_o, :], sib_o, sibling)
        ici = [copy(SEM_ICI + sv, keep.at[sv], rcv.at[sv], (*chips[sv], c)) for sv in range(3)]
        ici_o = [copy(SEM_ICI_O + sv, p_o.at[kjs[sv]], rcv_o.at[sv], (*chips[sv], c)) for sv in range(3)]
        fin = copy(SEM_FIN, acc.at[mine, :], gw_out.at[mine, :], sibling)
        fin_o = copy(SEM_FIN_O, res_o.at[mine_o, :], res_o.at[mine_o, :], sibling)
        smalls = [copy(SEM_SMALL + m, sm_buf.at[me], sm_buf.at[me],
                       (x ^ (m >> 2), y ^ ((m >> 1) & 1), c ^ (m & 1))) for m in range(1, N_DEV)]
        store_w = pltpu.make_async_copy(acc.at[mine, :], gw_out.at[mine, :], out_sems.at[0])
        store_o = pltpu.make_async_copy(res_o, gwo_out, out_sems.at[1])

        @at_step(0, 0)
        def _():
            barrier = pltpu.get_barrier_semaphore()
            for m in range(1, N_DEV):
                pl.semaphore_signal(barrier, inc=1, device_id=(x ^ (m >> 2), y ^ ((m >> 1) & 1), c ^ (m & 1)),
                                    device_id_type=MESH)
            pl.semaphore_wait(barrier, N_DEV - 1)
            d2d_o.start()

        @at_step(0, 1)
        def _():
            d2d_o.wait_recv()
            for j in range(N_SHARD):
                p_o[j] = (gwo_ref[j, mine_o, :].astype(F32) + sib_o[j].astype(F32)).astype(BF16)
            res_o[mine_o, :] = gwo_ref[k, mine_o, :].astype(F32) + sib_o[k].astype(F32)
            for cp in ici_o:
                cp.start()

        rows = pl.ds(pl.multiple_of(t * TT, TT), TT)

        @pl.when(s < N_SHARD)
        def _():
            dpb = dp_ref[...]
            part = _dot_tn(h_ref[...], dpb)

            @pl.when(t == 0)
            def _():
                acc[...] = part

            @pl.when(t > 0)
            def _():
                acc[...] += part

            d = _dot_nt(dpb, w_ref[0])

            @pl.when(s == 0)
            def _():
                dh[rows, :] = d

            @pl.when(s > 0)
            def _():
                dh[rows, :] += d

        for sv in range(N_SHARD):
            @at_step(sv, nt - 1)
            def _(sv=sv):
                sendbuf[sv] = acc[other, :].astype(BF16)
                if sv < 3:
                    keep[sv] = acc[mine, :].astype(BF16)
                d2d[sv].start()

        for sv in range(3):
            @at_step(sv + 1, 0)
            def _(sv=sv):
                d2d[sv].wait_recv()
                keep[sv] = (keep[sv].astype(F32) + sibrcv[sv].astype(F32)).astype(BF16)
                ici[sv].start()

        @at_norm_block(0)
        def _():
            d2d[3].wait_recv()
            ici[0].wait_recv()
            acc[mine, :] += sibrcv[3].astype(F32) + rcv[0].astype(F32)

        @at_norm_block(1)
        def _():
            tot = res_o[mine_o, :]
            for sv in range(3):
                ici_o[sv].wait_recv()
                tot = tot + rcv_o[sv].astype(F32)
            res_o[mine_o, :] = tot
            fin_o.start()

        @at_norm_block(2)
        def _():
            ici[1].wait_recv()
            acc[mine, :] += rcv[1].astype(F32)

        @at_norm_block(0)
        def _():
            dng[...] = jnp.zeros_like(dng)

        @pl.when(s >= N_SHARD)
        def _():
            blk = (s - N_SHARD) * nt + t
            dhv = dh[pl.ds(pl.multiple_of(blk * TX, TX), TX), :]
            xv = x_ref[...]
            r = lax.rsqrt(jnp.mean(xv * xv, axis=-1, keepdims=True) + EPS)
            xn = xv * r
            dng[...] += jnp.sum(dhv * xn, axis=0, keepdims=True)
            dxn = dhv * g_ref[...]
            gx_ref[...] = dx2_ref[...] + r * (dxn - xn * jnp.mean(dxn * xn, axis=-1, keepdims=True))

        @at_step(n_steps - 1, nt - 1)
        def _():
            sm_buf[me] = sm_ref[...] + smb_ref[...]
            sm_buf[me, 0:1, :] = dng[...]
            for cp in smalls:
                cp.start()
            ici[2].wait_recv()
            acc[mine, :] += rcv[2].astype(F32)
            fin.start()
            store_w.start()
            for m in range(1, N_DEV):
                copy(SEM_SMALL + m, sm_buf.at[0], sm_buf.at[0], sibling).wait_recv()
            tot = sm_buf[0]
            for d in range(1, N_DEV):
                tot = tot + sm_buf[d]
            osm_ref[...] = tot
            fin_o.wait_recv()
            store_o.start()
            fin.wait_recv()
            for cp in d2d + [d2d_o] + ici + ici_o + [fin, fin_o] + smalls:
                cp.wait_send()
            store_o.wait()
            store_w.wait()

    def shard_of(s, kr):
        return kr[0] ^ (3 - jnp.minimum(s, 3))

    def tok(s, t):
        return jnp.where(s < N_SHARD, t, nt - 1)

    def blk_map(s, t, kr):
        return (jnp.where(s < N_SHARD, 0, (s - N_SHARD) * nt + t), 0)

    hbm = pl.BlockSpec(memory_space=pl.ANY)
    grid_spec = pltpu.PrefetchScalarGridSpec(
        num_scalar_prefetch=1, grid=(n_steps, nt),
        in_specs=[pl.BlockSpec((TT, D_MODEL), lambda s, t, kr: (tok(s, t), 0)),
                  pl.BlockSpec((TT, SHARD_COLS), lambda s, t, kr: (tok(s, t), shard_of(s, kr))),
                  pl.BlockSpec((1, D_MODEL, SHARD_COLS), lambda s, t, kr: (shard_of(s, kr), 0, 0)),
                  pl.BlockSpec((N_SHARD, WO_ROWS, D_MODEL), lambda s, t, kr: (0, 0, 0)),
                  pl.BlockSpec((TX, D_MODEL), blk_map),
                  pl.BlockSpec((TX, D_MODEL), blk_map),
                  pl.BlockSpec((1, D_MODEL), lambda s, t, kr: (0, 0)),
                  pl.BlockSpec((8, D_MODEL), lambda s, t, kr: (0, 0)),
                  pl.BlockSpec((8, D_MODEL), lambda s, t, kr: (0, 0))],
        out_specs=(pl.BlockSpec((TX, D_MODEL), blk_map), hbm, hbm,
                   pl.BlockSpec((8, D_MODEL), lambda s, t, kr: (0, 0))),
        scratch_shapes=[pltpu.VMEM((D_MODEL, SHARD_COLS), F32), pltpu.VMEM((SEQ, D_MODEL), F32),
                        pltpu.VMEM((N_SHARD, hw, SHARD_COLS), BF16), pltpu.VMEM((3, hw, SHARD_COLS), BF16),
                        pltpu.VMEM((N_SHARD, hw, SHARD_COLS), BF16), pltpu.VMEM((3, hw, SHARD_COLS), BF16),
                        pltpu.VMEM((N_SHARD, ho, D_MODEL), BF16), pltpu.VMEM((N_SHARD, ho, D_MODEL), BF16),
                        pltpu.VMEM((3, ho, D_MODEL), BF16), pltpu.VMEM((WO_ROWS, D_MODEL), F32),
                        pltpu.VMEM((N_DEV, 8, D_MODEL), F32), pltpu.VMEM((1, D_MODEL), F32),
                        pltpu.SemaphoreType.DMA((N_SEM_TAIL,)), pltpu.SemaphoreType.DMA((N_SEM_TAIL,)),
                        pltpu.SemaphoreType.DMA((2,))])
    return pl.pallas_call(
        body, name="bwd_tail", grid_spec=grid_spec,
        out_shape=(jax.ShapeDtypeStruct((SEQ, D_MODEL), F32),
                   jax.ShapeDtypeStruct((D_MODEL, SHARD_COLS), F32),
                   jax.ShapeDtypeStruct((WO_ROWS, D_MODEL), F32),
                   jax.ShapeDtypeStruct((8, D_MODEL), F32)),
        compiler_params=pltpu.CompilerParams(dimension_semantics=("arbitrary", "arbitrary"),
                                             vmem_limit_bytes=60 * 1024 * 1024, collective_id=COLLECTIVE_TAIL),
    )(kidx, h, dproj, wg, gwo, x2d, dx2, g1, small_a, small_b)


def _adam_update(w, g, m, v):
    nm = ADAM_B1 * m + (1.0 - ADAM_B1) * g
    nv = ADAM_B2 * v + (1.0 - ADAM_B2) * (g * g)
    m_hat = nm / (1.0 - ADAM_B1 ** ADAM_STEP)
    v_hat = nv / (1.0 - ADAM_B2 ** ADAM_STEP)
    return -ADAM_LR * (m_hat / (jnp.sqrt(v_hat) + ADAM_EPS) + ADAM_WD * w), nm, nv


def _adamw_all(tot, g_w_in, g_w_out, big, small, grad_x):
    n = len(small)
    rows = WO_ROWS
    steps = D_MODEL // rows

    def body(tot_ref, *refs):
        gx_ref, gx_out = refs[2 + 3 * (2 + n)], refs[-1]
        gx_out[...] = gx_ref[...]
        ins, outs = refs[:2 + 3 * (2 + n)], refs[3 + 3 * (2 + n):-1]
        g_refs, wmv = ins[:2], ins[2:]
        loss_ref, quads = outs[0], outs[1:]

        def update(j, g):
            w_ref, m_ref, v_ref = wmv[3 * j:3 * j + 3]
            g_ref, d_ref, nm_ref, nv_ref = quads[4 * j:4 * j + 4]
            g_ref[...] = g
            d_ref[...], nm_ref[...], nv_ref[...] = _adam_update(w_ref[...], g, m_ref[...], v_ref[...])

        update(0, g_refs[0][...])

        @pl.when(pl.program_id(0) == 0)
        def _():
            update(1, g_refs[1][...])
            k = 2 * lax.axis_index("x") + lax.axis_index("y")
            mine = pl.ds(pl.multiple_of(k * HEAD, HEAD), HEAD)
            loss_ref[...] = tot_ref[7:8, 0:1]
            grads = [tot_ref[0:1, :], tot_ref[1:2, :], tot_ref[2:3, 0:D_HGRN], tot_ref[2:3, D_HGRN:],
                     jnp.concatenate([tot_ref[3:4, 0:D_HGRN], tot_ref[3:4, D_HGRN:]], axis=0),
                     jnp.concatenate([tot_ref[4 + tap:5 + tap, mine] for tap in range(3)], axis=1)]
            for j, g in enumerate(grads):
                update(2 + j, g)

    whole = lambda a: pl.BlockSpec(a.shape, lambda i: (0, 0))
    blk = pl.BlockSpec((rows, SHARD_COLS), lambda i: (i, 0))
    arrays = [a for triple in big + small for a in triple]
    in_specs = ([whole(tot), blk, whole(g_w_out)] + [blk] * 3 + [whole(a) for a in arrays[3:]])
    shapes = [big[0][0], big[1][0]] + [w for w, _, _ in small]
    out_shape = (jax.ShapeDtypeStruct((1, 1), F32),) + tuple(
        jax.ShapeDtypeStruct(w.shape, F32) for w in shapes for _ in range(4))
    out_specs = (pl.BlockSpec((1, 1), lambda i: (0, 0)),) + (blk,) * 4 + tuple(
        whole(w) for w in shapes[1:] for _ in range(4))
    gx_blk = pl.BlockSpec((SEQ // steps, D_MODEL), lambda i: (i, 0))
    outs = pl.pallas_call(
        body, name="adamw_all", grid=(steps,),
        out_shape=out_shape + (jax.ShapeDtypeStruct(grad_x.shape, F32),),
        in_specs=in_specs + [gx_blk], out_specs=out_specs + (gx_blk,),
        compiler_params=pltpu.CompilerParams(dimension_semantics=("arbitrary",), vmem_limit_bytes=VMEM_LIMIT),
    )(tot, g_w_in, g_w_out, *arrays, grad_x)
    return [outs[0]] + [outs[1 + 4 * j:5 + 4 * j] for j in range(2 + n)] + [outs[-1]]


def _local_step(x2d, tgt, proj, lb_logits, cw, ga, gcn, w_out, gf):
    g64 = _group_matrix(HEAD, CONV_GROUP)
    aux, states, dx2, dmixed, gwo, part_out = _mix_out(proj, lb_logits, cw, ga, gcn, g64, w_out, x2d, gf, tgt)
    dproj, part_mix = _mix_bwd(proj, aux, states, dmixed, lb_logits, cw, ga, gcn, g64)
    return dproj, dx2, gwo.reshape(N_SHARD, WO_ROWS, D_MODEL), part_out, part_mix


def kernel(x, norm_gain, w_in, lb_logits, conv_w, hgrn_norm_gain, conv_norm_gain, w_out, final_norm_gain, loss_target, m_norm_gain, m_w_in, m_lb_logits, m_conv_w, m_hgrn_norm_gain, m_conv_norm_gain, m_w_out, m_final_norm_gain, v_norm_gain, v_w_in, v_lb_logits, v_conv_w, v_hgrn_norm_gain, v_conv_norm_gain, v_w_out, v_final_norm_gain):
    k = 2 * lax.axis_index("x") + lax.axis_index("y")
    kidx = jnp.reshape(k, (1,)).astype(jnp.int32)
    row = lambda a: a.reshape(1, D_MODEL)
    taps = lambda a: a.reshape(1, 3 * HEAD)
    h, proj, wg, cw = _gather_proj(kidx, x[0], norm_gain, w_in, taps(conv_w))
    dproj, dx2, gwo, part_out, part_mix = _local_step(
        x[0], loss_target[0], proj, lb_logits, cw, hgrn_norm_gain, conv_norm_gain, w_out, row(final_norm_gain))
    rgrad_x, rg_w_in, rg_w_out, tot = _bwd_tail(kidx, h, dproj, wg, gwo, x[0], dx2, norm_gain, part_out, part_mix)

    (loss, (g_w_in, d_w_in, nm_w_in, nv_w_in), (g_w_out, d_w_out, nm_w_out, nv_w_out),
     (g_norm_gain, d_ng, nm_ng, nv_ng), (g_final, d_fg, nm_fg, nv_fg), (g_hgrn, d_hg, nm_hg, nv_hg),
     (g_convn, d_cg, nm_cg, nv_cg), (g_lb, d_lb, nm_lb, nv_lb), (g_conv_w, d_cw, nm_cw, nv_cw),
     grad_x) = _adamw_all(
        tot, rg_w_in, rg_w_out,
        [(w_in[0], m_w_in[0], v_w_in[0]), (w_out[0], m_w_out[0], v_w_out[0])],
        [(norm_gain, m_norm_gain, v_norm_gain),
         (row(final_norm_gain), row(m_final_norm_gain), row(v_final_norm_gain)),
         (hgrn_norm_gain, m_hgrn_norm_gain, v_hgrn_norm_gain),
         (conv_norm_gain, m_conv_norm_gain, v_conv_norm_gain),
         (lb_logits, m_lb_logits, v_lb_logits),
         (taps(conv_w), taps(m_conv_w), taps(v_conv_w))],
        rgrad_x)
    flat = lambda a: a.reshape(D_MODEL)
    untap = lambda a: a.reshape(1, 3, HEAD)
    return (loss.reshape(()), grad_x[None],
            g_norm_gain, g_w_in[None], g_lb, untap(g_conv_w), g_hgrn, g_convn, g_w_out[None], flat(g_final),
            d_ng, d_w_in[None], d_lb, untap(d_cw), d_hg, d_cg, d_w_out[None], flat(d_fg),
            nm_ng, nm_w_in[None], nm_lb, untap(nm_cw), nm_hg, nm_cg, nm_w_out[None], flat(nm_fg),
            nv_ng, nv_w_in[None], nv_lb, untap(nv_cw), nv_hg, nv_cg, nv_w_out[None], flat(nv_fg))
```

```python
import jax
import jax.numpy as jnp
import numpy as np
from jax import lax
from jax.experimental import pallas as pl
from jax.experimental.pallas import tpu as pltpu

F32 = jnp.float32
BF16 = jnp.bfloat16
MESH = pl.DeviceIdType.MESH

SEQ = 2048
D_MODEL = 1024
D_HGRN = 512
D_CONV = 512
HEAD = 128
N_HEADS = 4
CHUNK = 64
CONV_GROUP = 64
N_SHARD = 4
SHARD_COLS = 1024
WO_ROWS = 256
EPS = 1e-6
TB = 256
NCB = TB // CHUNK
N_CHUNKS = SEQ // CHUNK
N_DEV = 8
COLLECTIVE_GATHER, COLLECTIVE_MIX_OUT, COLLECTIVE_TAIL = 1, 0, 2
AUX_O, AUX_CV, AUX_B, AUX_COLS = 0, 512, 1024, 1536

ADAM_LR = 0.001
ADAM_B1 = 0.9
ADAM_B2 = 0.999
ADAM_EPS = 1e-08
ADAM_WD = 0.01
ADAM_STEP = 10

VMEM_LIMIT = 56 * 1024 * 1024


def _dot(a, b):
    return jnp.dot(a, b, preferred_element_type=F32)


def _dot_nt(a, b):
    return lax.dot_general(a, b, (((1,), (1,)), ((), ())), preferred_element_type=F32)


def _dot_tn(a, b):
    return lax.dot_general(a, b, (((0,), (0,)), ((), ())), preferred_element_type=F32)


def _split_bf16(x, n):
    parts = []
    r = x
    for _ in range(n):
        p = r.astype(BF16)
        parts.append(p)
        r = r - p.astype(F32)
    return parts


def _exact_left(m, x, n=3):
    acc = None
    for p in _split_bf16(x, n):
        t = _dot(m, p)
        acc = t if acc is None else acc + t
    return acc


def _exact_left_many(m, xs, n=3):
    parts = [_split_bf16(x, n) for x in xs]
    accs = [None] * len(xs)
    for i in range(n):
        for j in range(len(xs)):
            t = _dot(m, parts[j][i])
            accs[j] = t if accs[j] is None else accs[j] + t
    return accs


def _group_mean_many(xs, gmat, n=2):
    parts = [_split_bf16(x, n) for x in xs]
    accs = [None] * len(xs)
    for i in range(n):
        for j in range(len(xs)):
            t = _dot(parts[j][i], gmat)
            accs[j] = t if accs[j] is None else accs[j] + t
    return accs


def _group_mean(x, gmat, n=2):
    w = gmat.shape[0]
    outs = []
    for c0 in range(0, x.shape[1], w):
        acc = None
        for p in _split_bf16(x[:, c0:c0 + w], n):
            t = _dot(p, gmat)
            acc = t if acc is None else acc + t
        outs.append(acc)
    return jnp.concatenate(outs, axis=1)


def _sigmoid(x):
    return 1.0 / (1.0 + jnp.exp(-x))


def _lower_bound(lbl):
    l0 = lbl[0:1, :]
    l1 = lbl[1:2, :]
    m = jnp.maximum(l0, l1)
    e0 = jnp.exp(l0 - m)
    e1 = jnp.exp(l1 - m)
    return e0 / (e0 + e1)


def _tri(lower):
    r = lax.broadcasted_iota(jnp.int32, (CHUNK, CHUNK), 0)
    c = lax.broadcasted_iota(jnp.int32, (CHUNK, CHUNK), 1)
    return jnp.where((c <= r) if lower else (c >= r), 1.0, 0.0).astype(BF16)


def _causal():
    r = lax.broadcasted_iota(jnp.int32, (CHUNK, CHUNK), 0)
    c = lax.broadcasted_iota(jnp.int32, (CHUNK, CHUNK), 1)
    return c <= r


def _shift_down(x, sh, prev_tail):
    r = pltpu.roll(x, sh, 0)
    pt = pltpu.roll(prev_tail, sh, 0)
    rows = lax.broadcasted_iota(jnp.int32, prev_tail.shape, 0)
    top = jnp.where(rows < sh, pt, r[0:8])
    return jnp.concatenate([top, r[8:]], axis=0)


def _shift_up(x, sh, next_head):
    n = x.shape[0]
    r = pltpu.roll(x, n - sh, 0)
    nh = pltpu.roll(next_head, 8 - sh, 0)
    rows = lax.broadcasted_iota(jnp.int32, next_head.shape, 0)
    bot = jnp.where(rows >= 8 - sh, nh, r[n - 8:])
    return jnp.concatenate([r[:n - 8], bot], axis=0)


def _group_matrix(width, group):
    r = np.arange(width)[:, None] // group
    c = np.arange(width)[None, :] // group
    return jnp.asarray(np.where(r == c, 1.0 / group, 0.0), dtype=BF16)


TG = 1024
SEM_W, SEM_CW, SEM_W_FWD, N_SEM = 0, 4, 7, 11


def _gather_proj(kidx, x2d, g1, w_in, conv_w):
    half_w = D_MODEL // 2
    half_c = SHARD_COLS // 2
    nt = SEQ // TG
    n_steps = 2 * N_SHARD

    def body(k_ref, x_ref, g_ref, w_ref, cw_ref, h_ref, p_ref, wg_out, cwg_out,
             wg_v, cwg_v, send_sems, recv_sems, out_sems):
        s, t = pl.program_id(0), pl.program_id(1)
        x, y, c = lax.axis_index("x"), lax.axis_index("y"), lax.axis_index("c")
        k = 2 * x + y
        sibling = (x, y, 1 - c)
        chips = [(1 - x, y), (x, 1 - y), (1 - x, 1 - y)]
        kjs = [2 * cx + cy for cx, cy in chips]
        diag = (*chips[2], c)

        def w_half(kk, cc):
            return wg_v.at[kk, pl.ds(cc * half_w, half_w), :]

        def w_quarter(kk, cc, piece):
            return wg_v.at[kk, pl.ds(cc * half_w, half_w), piece * half_c:(piece + 1) * half_c]

        def cw_of(kk):
            return cwg_v.at[:, pl.ds(pl.multiple_of(kk * HEAD, HEAD), HEAD)]

        def copy(sem, ref, to):
            return pltpu.make_async_remote_copy(
                src_ref=ref, dst_ref=ref, send_sem=send_sems.at[sem], recv_sem=recv_sems.at[sem],
                device_id=to, device_id_type=MESH)

        def at_step(sv, tv):
            return pl.when((s == sv) & (t == tv))

        w_direct = ([copy(SEM_W + j, w_half(k, c), (*chips[j], c)) for j in range(2)]
                    + [copy(SEM_W + 2 + p, w_quarter(k, c, p), diag) for p in range(2)])
        cw_direct = [copy(SEM_CW + j, cw_of(k), (*chip, c)) for j, chip in enumerate(chips)]
        w_passed = ([copy(SEM_W_FWD + j, w_half(kjs[j], c), sibling) for j in range(2)]
                    + [copy(SEM_W_FWD + 2 + p, w_quarter(kjs[2], c, p), sibling) for p in range(2)])
        stores = ([pltpu.make_async_copy(wg_v.at[kk], wg_out.at[kk], out_sems.at[i])
                   for i, kk in enumerate([k] + kjs)]
                  + [pltpu.make_async_copy(cwg_v, cwg_out, out_sems.at[4])])

        @at_step(0, 0)
        def _():
            barrier = pltpu.get_barrier_semaphore()
            for peer in [sibling] + [(*chip, c) for chip in chips]:
                pl.semaphore_signal(barrier, inc=1, device_id=peer, device_id_type=MESH)
            wg_v[k] = w_ref[0].astype(BF16)
            mine = pl.ds(pl.multiple_of(k * HEAD, HEAD), HEAD)
            cwg_v[:, mine] = jnp.zeros((8, HEAD), F32)
            for tap in range(3):
                cwg_v[tap:tap + 1, mine] = cw_ref[:, tap * HEAD:(tap + 1) * HEAD]
            pl.semaphore_wait(barrier, 4)
            w_direct[0].start()
            w_direct[1].start()
            for cp in cw_direct:
                cp.start()
            stores[0].start()

        @at_step(2, 0)
        def _():
            for j in range(2):
                copy(SEM_W + j, w_half(kjs[j], c), sibling).wait_recv()
                w_passed[j].start()
            w_direct[2].start()
            w_direct[3].start()
            copy(SEM_W_FWD, w_half(kjs[0], 1 - c), sibling).wait_recv()
            stores[1].start()

        @at_step(4, 0)
        def _():
            copy(SEM_W_FWD + 1, w_half(kjs[1], 1 - c), sibling).wait_recv()
            stores[2].start()

        for p in range(2):
            @at_step(6 + p, 0)
            def _(p=p):
                copy(SEM_W + 2 + p, w_quarter(kjs[2], c, p), sibling).wait_recv()
                w_passed[2 + p].start()
                copy(SEM_W_FWD + 2 + p, w_quarter(kjs[2], 1 - c, p), sibling).wait_recv()

        rows = pl.ds(pl.multiple_of(t * TG, TG), TG)

        @pl.when(s == 0)
        def _():
            xv = x_ref[...]
            r = lax.rsqrt(jnp.mean(xv * xv, axis=-1, keepdims=True) + EPS)
            h_ref[rows, :] = (xv * r * g_ref[...]).astype(BF16)

        sh = s >> 1
        js = k ^ (((sh & 1) << 1) | (sh >> 1))
        for piece in range(2):
            @pl.when((s & 1) == piece)
            def _(piece=piece):
                p_ref[...] = _dot(h_ref[rows, :], wg_v[js, :, piece * half_c:(piece + 1) * half_c])

        @at_step(n_steps - 1, nt - 1)
        def _():
            stores[3].start()
            for j in range(3):
                copy(SEM_CW + j, cw_of(kjs[j]), sibling).wait_recv()
            stores[4].start()
            for cp in w_direct + cw_direct + w_passed:
                cp.wait_send()
            for st in stores:
                st.wait()

    def x_map(s, t, kr):
        return (jnp.where(s == 0, t, nt - 1), 0)

    def p_map(s, t, kr):
        sh = s >> 1
        return (t, 2 * (kr[0] ^ (((sh & 1) << 1) | (sh >> 1))) + (s & 1))

    hbm = pl.BlockSpec(memory_space=pl.ANY)
    grid_spec = pltpu.PrefetchScalarGridSpec(
        num_scalar_prefetch=1, grid=(n_steps, nt),
        in_specs=[pl.BlockSpec((TG, D_MODEL), x_map),
                  pl.BlockSpec((1, D_MODEL), lambda s, t, kr: (0, 0)),
                  pl.BlockSpec((1, D_MODEL, SHARD_COLS), lambda s, t, kr: (0, 0, 0)),
                  pl.BlockSpec((1, 3 * HEAD), lambda s, t, kr: (0, 0))],
        out_specs=(pl.BlockSpec((SEQ, D_MODEL), lambda s, t, kr: (0, 0)),
                   pl.BlockSpec((TG, half_c), p_map), hbm, hbm),
        scratch_shapes=[pltpu.VMEM((N_SHARD, D_MODEL, SHARD_COLS), BF16),
                        pltpu.VMEM((8, D_CONV), F32),
                        pltpu.SemaphoreType.DMA((N_SEM,)), pltpu.SemaphoreType.DMA((N_SEM,)),
                        pltpu.SemaphoreType.DMA((5,))])
    return pl.pallas_call(
        body, name="gather_proj", grid_spec=grid_spec,
        out_shape=(jax.ShapeDtypeStruct((SEQ, D_MODEL), BF16),
                   jax.ShapeDtypeStruct((SEQ, N_SHARD * SHARD_COLS), F32),
                   jax.ShapeDtypeStruct((N_SHARD, D_MODEL, SHARD_COLS), BF16),
                   jax.ShapeDtypeStruct((8, D_CONV), F32)),
        compiler_params=pltpu.CompilerParams(dimension_semantics=("arbitrary", "arbitrary"),
                                             vmem_limit_bytes=VMEM_LIMIT, collective_id=COLLECTIVE_GATHER),
    )(kidx, x2d, g1, w_in, conv_w)


LAG = 7


def _mix_out(proj, lb_logits, cw, ga, gcn, g64, w_out, x2d, gf, tgt):
    half_o = WO_ROWS // 2
    nblk = SEQ // TB
    n_steps = nblk + LAG

    def body(p_ref, lbl_ref, cw_ref, ga_ref, gcn_ref, g64_ref, wo_ref, x_ref, gf_ref, t_ref,
             aux_ref, sto_ref, dx2_ref, dm_ref, gwo_ref, part_ref,
             st_ref, tail_ref, wog_v, stage, ring, acc_ref, send_sems, recv_sems):
        i = pl.program_id(0)
        x, y, c = lax.axis_index("x"), lax.axis_index("y"), lax.axis_index("c")
        k = 2 * x + y
        sibling = (x, y, 1 - c)
        chips = [(1 - x, y), (x, 1 - y), (1 - x, 1 - y)]
        kjs = [2 * cx + cy for cx, cy in chips]

        def wo_half(kk, cc):
            return wog_v.at[pl.ds(pl.multiple_of(kk * WO_ROWS + cc * half_o, half_o), half_o), :]

        def copy(sem, ref, to):
            return pltpu.make_async_remote_copy(
                src_ref=ref, dst_ref=ref, send_sem=send_sems.at[sem], recv_sem=recv_sems.at[sem],
                device_id=to, device_id_type=MESH)

        wo_direct = [copy(j, wo_half(k, c), (*chip, c)) for j, chip in enumerate(chips)]
        wo_passed = [copy(3 + j, wo_half(kj, c), sibling) for j, kj in enumerate(kjs)]

        @pl.when(i == 0)
        def _():
            barrier = pltpu.get_barrier_semaphore()
            for peer in [sibling] + [(*chip, c) for chip in chips]:
                pl.semaphore_signal(barrier, inc=1, device_id=peer, device_id_type=MESH)
            st_ref[...] = jnp.zeros_like(st_ref)
            tail_ref[...] = jnp.zeros_like(tail_ref)
            acc_ref[...] = jnp.zeros_like(acc_ref)
            part_ref[...] = jnp.zeros_like(part_ref)
            wog_v[pl.ds(pl.multiple_of(k * WO_ROWS, WO_ROWS), WO_ROWS), :] = wo_ref[0].astype(BF16)
            pl.semaphore_wait(barrier, 4)
            for cp in wo_direct:
                cp.start()

        @pl.when(i == LAG - 1)
        def _():
            for j in range(3):
                copy(j, wo_half(kjs[j], c), sibling).wait_recv()
                wo_passed[j].start()

        @pl.when(i == LAG)
        def _():
            for j in range(3):
                copy(3 + j, wo_half(kjs[j], 1 - c), sibling).wait_recv()

        lb = _lower_bound(lbl_ref[...])
        tri = _tri(True)
        causal = _causal()
        g64m = g64_ref[...]
        heads = range(N_HEADS)
        cs = [slice(hd * HEAD, (hd + 1) * HEAD) for hd in heads]
        col = lambda base, hd: slice(base + hd * HEAD, base + (hd + 1) * HEAD)

        def mix_chunk(n):
            sl = pl.ds(n * CHUNK, CHUNK)
            sg = [_sigmoid(p_ref[sl, col(512, hd)]) for hd in heads]
            f = [lb[:, cs[hd]] + (1.0 - lb[:, cs[hd]]) * sg[hd] for hd in heads]
            bc = _exact_left_many(tri, [jnp.log(f[hd]) for hd in heads])
            for hd in heads:
                aux_ref[sl, col(AUX_B, hd)] = bc[hd]
            g = [bc[hd][CHUNK - 1:CHUNK, :] for hd in heads]
            qd = [(p_ref[sl, col(0, hd)] * jnp.exp(bc[hd])).astype(BF16) for hd in heads]
            ki = [((1.0 - f[hd]) * jnp.exp(-bc[hd])).astype(BF16) for hd in heads]
            ke = [((1.0 - f[hd]) * jnp.exp(g[hd] - bc[hd])).astype(BF16) for hd in heads]
            vb = [p_ref[sl, col(1024, hd)].astype(BF16) for hd in heads]
            st = [st_ref[hd] for hd in heads]
            st_b = [a.astype(BF16) for a in st]
            for hd in heads:
                sto_ref[n, hd] = st_b[hd]
            scm = [_dot_nt(qd[hd], ki[hd]) for hd in heads]
            inter = [_dot_nt(qd[hd], st_b[hd]) for hd in heads]
            upd = [_dot_tn(vb[hd], ke[hd]) for hd in heads]
            intra = [_dot(jnp.where(causal, scm[hd], 0.0).astype(BF16), vb[hd]) for hd in heads]
            for hd in heads:
                st_ref[hd] = st[hd] * jnp.exp(g[hd]) + upd[hd]
                o = intra[hd] + inter[hd]
                aux_ref[sl, col(AUX_O, hd)] = o
                ra = lax.rsqrt(jnp.mean(o * o, axis=-1, keepdims=True) + EPS)
                za = p_ref[sl, col(1536, hd)]
                stage[sl, cs[hd]] = (o * ra * ga_ref[:, cs[hd]] * (za * _sigmoid(za))).astype(BF16)
            yb = []
            for hd in heads:
                cu = p_ref[sl, col(3072, hd)] * p_ref[sl, col(2048, hd)]
                tail = tail_ref[:, cs[hd]]
                cv = (cw_ref[0:1, cs[hd]] * _shift_down(cu, 2, tail) + cw_ref[1:2, cs[hd]] * _shift_down(cu, 1, tail)
                      + cw_ref[2:3, cs[hd]] * cu)
                tail_ref[:, cs[hd]] = cu[CHUNK - 8:, :]
                aux_ref[sl, col(AUX_CV, hd)] = cv
                yb.append(p_ref[sl, col(2560, hd)] * cv)
            ms = _group_mean_many([y * y for y in yb], g64m)
            for hd in heads:
                rb = lax.rsqrt(ms[hd] + EPS)
                zb = p_ref[sl, col(3584, hd)]
                stage[sl, col(512, hd)] = (yb[hd] * rb * gcn_ref[:, cs[hd]] * (zb * _sigmoid(zb))).astype(BF16)

        def step(mix, project):
            if project:
                mixed_b = ring[pl.ds(pl.multiple_of((i - LAG) * TB, TB), TB), :]
                y = _dot(mixed_b, wog_v[...])
            if mix:
                mix_chunk(0)
            if project:
                x2 = x_ref[...] + y
                r2 = lax.rsqrt(jnp.mean(x2 * x2, axis=-1, keepdims=True) + EPS)
                n2 = x2 * r2
                gfv = gf_ref[...]
                err = n2 * gfv - t_ref[...]
                loss = 0.5 * jnp.sum(jnp.mean(err * err, axis=-1, keepdims=True), axis=0, keepdims=True)
                dy = err * (1.0 / D_MODEL)
                part_ref[1:2, :] += jnp.sum(dy * n2, axis=0, keepdims=True)
                part_ref[7:8, :] += jnp.broadcast_to(loss, (1, D_MODEL))
                dn = dy * gfv
                dx2 = r2 * (dn - n2 * jnp.mean(dn * n2, axis=-1, keepdims=True))
                dx2_ref[...] = dx2
                dx2_b = dx2.astype(BF16)
            if mix:
                mix_chunk(1)
            if project:
                dm_ref[...] = _dot_nt(dx2_b, wog_v[...])
            if mix:
                mix_chunk(2)
            if project:
                acc_ref[...] += _dot_tn(mixed_b, dx2_b)
            if mix:
                mix_chunk(3)
                ring[pl.ds(pl.multiple_of(i * TB, TB), TB), :] = stage[...]

        @pl.when(i < LAG)
        def _():
            step(True, False)

        @pl.when((i >= LAG) & (i < nblk))
        def _():
            step(True, True)

        @pl.when(i >= nblk)
        def _():
            step(False, True)

        @pl.when(i == n_steps - 1)
        def _():
            gwo_ref[...] = acc_ref[...].astype(BF16)
            for cp in wo_direct + wo_passed:
                cp.wait_send()

    assert NCB == 4
    row = lambda w: pl.BlockSpec((1, w), lambda i: (0, 0))
    mix_blk = lambda i: jnp.minimum(i, nblk - 1)
    out_blk = lambda i: jnp.clip(i - LAG, 0, nblk - 1)
    tok = lambda: pl.BlockSpec((TB, D_MODEL), lambda i: (out_blk(i), 0))
    return pl.pallas_call(
        body, name="mix_out", grid=(n_steps,),
        out_shape=(jax.ShapeDtypeStruct((SEQ, AUX_COLS), F32),
                   jax.ShapeDtypeStruct((N_CHUNKS, N_HEADS, HEAD, HEAD), BF16),
                   jax.ShapeDtypeStruct((SEQ, D_MODEL), F32),
                   jax.ShapeDtypeStruct((SEQ, D_MODEL), F32),
                   jax.ShapeDtypeStruct((D_MODEL, D_MODEL), BF16),
                   jax.ShapeDtypeStruct((8, D_MODEL), F32)),
        in_specs=[pl.BlockSpec((TB, 4096), lambda i: (jnp.minimum(i, nblk - 1), 0)),
                  pl.BlockSpec((2, D_HGRN), lambda i: (0, 0)),
                  pl.BlockSpec((8, D_CONV), lambda i: (0, 0)),
                  row(D_HGRN), row(D_CONV),
                  pl.BlockSpec((HEAD, HEAD), lambda i: (0, 0)),
                  pl.BlockSpec((1, WO_ROWS, D_MODEL), lambda i: (0, 0, 0)),
                  tok(), row(D_MODEL), tok()],
        out_specs=(pl.BlockSpec((TB, AUX_COLS), lambda i: (mix_blk(i), 0)),
                   pl.BlockSpec((NCB, N_HEADS, HEAD, HEAD), lambda i: (mix_blk(i), 0, 0, 0)),
                   tok(), tok(),
                   pl.BlockSpec((D_MODEL, D_MODEL), lambda i: (0, 0)),
                   pl.BlockSpec((8, D_MODEL), lambda i: (0, 0))),
        scratch_shapes=[pltpu.VMEM((N_HEADS, HEAD, HEAD), F32), pltpu.VMEM((8, D_CONV), F32),
                        pltpu.VMEM((D_MODEL, D_MODEL), BF16), pltpu.VMEM((TB, D_MODEL), BF16),
                        pltpu.VMEM((SEQ, D_MODEL), BF16), pltpu.VMEM((D_MODEL, D_MODEL), F32),
                        pltpu.SemaphoreType.DMA((6,)), pltpu.SemaphoreType.DMA((6,))],
        compiler_params=pltpu.CompilerParams(dimension_semantics=("arbitrary",), vmem_limit_bytes=VMEM_LIMIT,
                                             collective_id=COLLECTIVE_MIX_OUT),
    )(proj, lb_logits, cw, ga, gcn, g64, w_out, x2d, gf, tgt)


def _mix_bwd(proj, aux, states, dmixed, lb_logits, cw, ga, gcn, g64):
    nblk = SEQ // TB

    def body(p_ref, aux_ref, st_ref, dm_ref, lbl_ref, cw_ref, ga_ref, gcn_ref, g64_ref,
             dp_ref, part_ref, dst_ref, head_ref, dlb_ref):
        i = pl.program_id(0)

        @pl.when(i == 0)
        def _():
            dst_ref[...] = jnp.zeros_like(dst_ref)
            head_ref[...] = jnp.zeros_like(head_ref)
            part_ref[...] = jnp.zeros_like(part_ref)
            dlb_ref[...] = jnp.zeros_like(dlb_ref)

        lb = _lower_bound(lbl_ref[...])
        triu = _tri(False)
        causal = _causal()
        g64m = g64_ref[...]
        rowsum = lambda a: jnp.sum(a, axis=0, keepdims=True)
        heads = range(N_HEADS)
        cs = [slice(hd * HEAD, (hd + 1) * HEAD) for hd in heads]
        col = lambda base, hd: slice(base + hd * HEAD, base + (hd + 1) * HEAD)
        for n in reversed(range(NCB)):
            sl = pl.ds(n * CHUNK, CHUNK)
            cvv = [aux_ref[sl, col(AUX_CV, hd)] for hd in heads]
            gb = [p_ref[sl, col(2560, hd)] for hd in heads]
            yb = [gb[hd] * cvv[hd] for hd in heads]
            ms = _group_mean_many([y * y for y in yb], g64m)
            rb, nb, dnb = [], [], []
            for hd in heads:
                rb.append(lax.rsqrt(ms[hd] + EPS))
                nb.append(yb[hd] * rb[hd])
                zb = p_ref[sl, col(3584, hd)]
                sgb = _sigmoid(zb)
                dmb = dm_ref[sl, col(512, hd)]
                gcv = gcn_ref[:, cs[hd]]
                part_ref[2:3, col(512, hd)] += rowsum(dmb * nb[hd] * (zb * sgb))
                dp_ref[sl, col(3584, hd)] = (dmb * nb[hd] * gcv * (sgb * (1.0 + zb * (1.0 - sgb)))).astype(BF16)
                dnb.append(dmb * gcv * (zb * sgb))
            mdn = _group_mean_many([dnb[hd] * nb[hd] for hd in heads], g64m)
            for hd in heads:
                dyb = rb[hd] * (dnb[hd] - nb[hd] * mdn[hd])
                dp_ref[sl, col(2560, hd)] = (dyb * cvv[hd]).astype(BF16)
                dcv = dyb * gb[hd]
                head = head_ref[:, cs[hd]]
                dcv1 = _shift_up(dcv, 1, head)
                dcv2 = _shift_up(dcv, 2, head)
                head_ref[:, cs[hd]] = dcv[0:8, :]
                u = p_ref[sl, col(2048, hd)]
                gc = p_ref[sl, col(3072, hd)]
                cu = gc * u
                part_ref[4:5, cs[hd]] += rowsum(dcv2 * cu)
                part_ref[5:6, cs[hd]] += rowsum(dcv1 * cu)
                part_ref[6:7, cs[hd]] += rowsum(dcv * cu)
                dcu = cw_ref[2:3, cs[hd]] * dcv + cw_ref[1:2, cs[hd]] * dcv1 + cw_ref[0:1, cs[hd]] * dcv2
                dp_ref[sl, col(3072, hd)] = (dcu * u).astype(BF16)
                dp_ref[sl, col(2048, hd)] = (dcu * gc).astype(BF16)
            do_b = []
            for hd in heads:
                ov = aux_ref[sl, col(AUX_O, hd)]
                ra = lax.rsqrt(jnp.mean(ov * ov, axis=-1, keepdims=True) + EPS)
                na = ov * ra
                za = p_ref[sl, col(1536, hd)]
                sga = _sigmoid(za)
                dma = dm_ref[sl, cs[hd]]
                gav = ga_ref[:, cs[hd]]
                part_ref[2:3, cs[hd]] += rowsum(dma * na * (za * sga))
                dp_ref[sl, col(1536, hd)] = (dma * na * gav * (sga * (1.0 + za * (1.0 - sga)))).astype(BF16)
                dna = dma * gav * (za * sga)
                do_b.append((ra * (dna - na * jnp.mean(dna * na, axis=-1, keepdims=True))).astype(BF16))
            s = [_sigmoid(p_ref[sl, col(512, hd)]) for hd in heads]
            f = [lb[:, cs[hd]] + (1.0 - lb[:, cs[hd]]) * s[hd] for hd in heads]
            bc = [aux_ref[sl, col(AUX_B, hd)] for hd in heads]
            g = [bc[hd][CHUNK - 1:CHUNK, :] for hd in heads]
            eb = [jnp.exp(bc[hd]) for hd in heads]
            enb = [jnp.exp(-bc[hd]) for hd in heads]
            eg = [jnp.exp(g[hd] - bc[hd]) for hd in heads]
            dec = [jnp.exp(g[hd]) for hd in heads]
            qd = [p_ref[sl, cs[hd]] * eb[hd] for hd in heads]
            ki = [(1.0 - f[hd]) * enb[hd] for hd in heads]
            ke = [(1.0 - f[hd]) * eg[hd] for hd in heads]
            qd_b = [a.astype(BF16) for a in qd]
            ki_b = [a.astype(BF16) for a in ki]
            ke_b = [a.astype(BF16) for a in ke]
            vb = [p_ref[sl, col(1024, hd)].astype(BF16) for hd in heads]
            st_b = [st_ref[n, hd] for hd in heads]
            dst = [dst_ref[hd] for hd in heads]
            dst_b = [a.astype(BF16) for a in dst]
            scm = [_dot_nt(qd_b[hd], ki_b[hd]) for hd in heads]
            amm = [_dot_nt(do_b[hd], vb[hd]) for hd in heads]
            dqd2 = [_dot(do_b[hd], st_b[hd]) for hd in heads]
            dke = [_dot(vb[hd], dst_b[hd]) for hd in heads]
            dv2 = [_dot_nt(ke_b[hd], dst_b[hd]) for hd in heads]
            dsu = [_dot_tn(do_b[hd], qd_b[hd]) for hd in heads]
            sc = [jnp.where(causal, scm[hd], 0.0).astype(BF16) for hd in heads]
            am = [jnp.where(causal, amm[hd], 0.0).astype(BF16) for hd in heads]
            dqd1 = [_dot(am[hd], ki_b[hd]) for hd in heads]
            dki = [_dot_tn(am[hd], qd_b[hd]) for hd in heads]
            dv1 = [_dot_tn(sc[hd], do_b[hd]) for hd in heads]
            db, dgv = [], []
            for hd in heads:
                dqd = dqd1[hd] + dqd2[hd]
                ddec = rowsum(dst[hd] * st_b[hd].astype(F32))
                dst_ref[hd] = dst[hd] * dec[hd] + dsu[hd]
                dp_ref[sl, cs[hd]] = (dqd * eb[hd]).astype(BF16)
                dp_ref[sl, col(1024, hd)] = (dv1[hd] + dv2[hd]).astype(BF16)
                db.append(dqd * qd[hd] - dki[hd] * ki[hd] - dke[hd] * ke[hd])
                dgv.append(rowsum(dke[hd] * ke[hd]) + ddec * dec[hd])
            rc = _exact_left_many(triu, db, 2)
            for hd in heads:
                df = (rc[hd] + dgv[hd]) / f[hd] - (dki[hd] * enb[hd] + dke[hd] * eg[hd])
                dlb_ref[:, cs[hd]] += rowsum(df * (1.0 - s[hd]))
                dp_ref[sl, col(512, hd)] = (df * (1.0 - lb[:, cs[hd]]) * s[hd] * (1.0 - s[hd])).astype(BF16)

        @pl.when(i == nblk - 1)
        def _():
            row = dlb_ref[...] * lb * (1.0 - lb)
            part_ref[3:4, 0:D_HGRN] = row
            part_ref[3:4, D_HGRN:] = -row

    rev = lambda w: pl.BlockSpec((TB, w), lambda i: (nblk - 1 - i, 0))
    row = lambda w: pl.BlockSpec((1, w), lambda i: (0, 0))
    return pl.pallas_call(
        body, name="mix_bwd", grid=(nblk,),
        out_shape=(jax.ShapeDtypeStruct((SEQ, 4096), BF16),
                   jax.ShapeDtypeStruct((8, D_MODEL), F32)),
        in_specs=[rev(4096), rev(AUX_COLS),
                  pl.BlockSpec((NCB, N_HEADS, HEAD, HEAD), lambda i: (nblk - 1 - i, 0, 0, 0)),
                  rev(D_MODEL),
                  pl.BlockSpec((2, D_HGRN), lambda i: (0, 0)),
                  pl.BlockSpec((8, D_CONV), lambda i: (0, 0)),
                  row(D_HGRN), row(D_CONV),
                  pl.BlockSpec((HEAD, HEAD), lambda i: (0, 0))],
        out_specs=(rev(4096), pl.BlockSpec((8, D_MODEL), lambda i: (0, 0))),
        scratch_shapes=[pltpu.VMEM((N_HEADS, HEAD, HEAD), F32), pltpu.VMEM((8, D_CONV), F32),
                        pltpu.VMEM((1, D_HGRN), F32)],
        compiler_params=pltpu.CompilerParams(dimension_semantics=("arbitrary",), vmem_limit_bytes=VMEM_LIMIT),
    )(proj, aux, states, dmixed, lb_logits, cw, ga, gcn, g64)


TT = 1024
TX = 256
(SEM_D2D, SEM_D2D_O, SEM_ICI, SEM_ICI_O, SEM_FIN, SEM_FIN_O, SEM_SMALL, N_SEM_TAIL) = 0, 4, 5, 8, 11, 12, 12, 20


def _bwd_tail(kidx, h, dproj, wg, gwo, x2d, dx2, g1, small_a, small_b):
    hw = D_MODEL // 2
    ho = WO_ROWS // 2
    nt = SEQ // TT
    n_steps = N_SHARD + SEQ // TX // nt

    def body(k_ref, h_ref, dp_ref, w_ref, gwo_ref, x_ref, dx2_ref, g_ref, sm_ref, smb_ref,
             gx_ref, gw_out, gwo_out, osm_ref,
             acc, dh, sendbuf, keep, sibrcv, rcv, sib_o, p_o, rcv_o, res_o, sm_buf, dng,
             send_sems, recv_sems, out_sems):
        s, t = pl.program_id(0), pl.program_id(1)
        x, y, c = lax.axis_index("x"), lax.axis_index("y"), lax.axis_index("c")
        k = 2 * x + y
        me = 4 * x + 2 * y + c
        sibling = (x, y, 1 - c)
        chips = [(1 - x, 1 - y), (1 - x, y), (x, 1 - y)]
        kjs = [2 * cx + cy for cx, cy in chips]
        mine = pl.ds(pl.multiple_of(c * hw, hw), hw)
        other = pl.ds(pl.multiple_of((1 - c) * hw, hw), hw)
        mine_o = pl.ds(pl.multiple_of(c * ho, ho), ho)
        other_o = pl.ds(pl.multiple_of((1 - c) * ho, ho), ho)

        def copy(sem, src, dst, to):
            return pltpu.make_async_remote_copy(
                src_ref=src, dst_ref=dst, send_sem=send_sems.at[sem], recv_sem=recv_sems.at[sem],
                device_id=to, device_id_type=MESH)

        def at_step(sv, tv):
            return pl.when((s == sv) & (t == tv))

        def at_norm_block(b):
            return at_step(N_SHARD + b // nt, b % nt)

        d2d = [copy(SEM_D2D + sv, sendbuf.at[sv], sibrcv.at[sv], sibling) for sv in range(N_SHARD)]
        d2d_o = copy(SEM_D2D_O, gwo_ref.at[:, other_o, :], sib_o, sibling)
        ici = [copy(SEM_ICI + sv, keep.at[sv], rcv.at[sv], (*chips[sv], c)) for sv in range(3)]
        ici_o = [copy(SEM_ICI_O + sv, p_o.at[kjs[sv]], rcv_o.at[sv], (*chips[sv], c)) for sv in range(3)]
        fin = copy(SEM_FIN, acc.at[mine, :], gw_out.at[mine, :], sibling)
        fin_o = copy(SEM_FIN_O, res_o.at[mine_o, :], res_o.at[mine_o, :], sibling)
        smalls = [copy(SEM_SMALL + m, sm_buf.at[me], sm_buf.at[me],
                       (x ^ (m >> 2), y ^ ((m >> 1) & 1), c ^ (m & 1))) for m in range(1, N_DEV)]
        store_w = pltpu.make_async_copy(acc.at[mine, :], gw_out.at[mine, :], out_sems.at[0])
        store_o = pltpu.make_async_copy(res_o, gwo_out, out_sems.at[1])

        @at_step(0, 0)
        def _():
            barrier = pltpu.get_barrier_semaphore()
            for m in range(1, N_DEV):
                pl.semaphore_signal(barrier, inc=1, device_id=(x ^ (m >> 2), y ^ ((m >> 1) & 1), c ^ (m & 1)),
                                    device_id_type=MESH)
            pl.semaphore_wait(barrier, N_DEV - 1)
            d2d_o.start()

        @at_step(0, 1)
        def _():
            d2d_o.wait_recv()
            for j in range(N_SHARD):
                p_o[j] = (gwo_ref[j, mine_o, :].astype(F32) + sib_o[j].astype(F32)).astype(BF16)
            res_o[mine_o, :] = gwo_ref[k, mine_o, :].astype(F32) + sib_o[k].astype(F32)
            for cp in ici_o:
                cp.start()

        rows = pl.ds(pl.multiple_of(t * TT, TT), TT)

        @pl.when(s < N_SHARD)
        def _():
            dpb = dp_ref[...]
            part = _dot_tn(h_ref[...], dpb)

            @pl.when(t == 0)
            def _():
                acc[...] = part

            @pl.when(t > 0)
            def _():
                acc[...] += part

            d = _dot_nt(dpb, w_ref[0])

            @pl.when(s == 0)
            def _():
                dh[rows, :] = d

            @pl.when(s > 0)
            def _():
                dh[rows, :] += d

        for sv in range(N_SHARD):
            @at_step(sv, nt - 1)
            def _(sv=sv):
                sendbuf[sv] = acc[other, :].astype(BF16)
                if sv < 3:
                    keep[sv] = acc[mine, :].astype(BF16)
                d2d[sv].start()

        for sv in range(3):
            @at_step(sv + 1, 0)
            def _(sv=sv):
                d2d[sv].wait_recv()
                keep[sv] = (keep[sv].astype(F32) + sibrcv[sv].astype(F32)).astype(BF16)
                ici[sv].start()

        @at_norm_block(0)
        def _():
            d2d[3].wait_recv()
            ici[0].wait_recv()
            acc[mine, :] += sibrcv[3].astype(F32) + rcv[0].astype(F32)

        @at_norm_block(1)
        def _():
            tot = res_o[mine_o, :]
            for sv in range(3):
                ici_o[sv].wait_recv()
                tot = tot + rcv_o[sv].astype(F32)
            res_o[mine_o, :] = tot
            fin_o.start()

        @at_norm_block(2)
        def _():
            ici[1].wait_recv()
            acc[mine, :] += rcv[1].astype(F32)

        @at_norm_block(0)
        def _():
            dng[...] = jnp.zeros_like(dng)

        @pl.when(s >= N_SHARD)
        def _():
            blk = (s - N_SHARD) * nt + t
            dhv = dh[pl.ds(pl.multiple_of(blk * TX, TX), TX), :]
            xv = x_ref[...]
            r = lax.rsqrt(jnp.mean(xv * xv, axis=-1, keepdims=True) + EPS)
            xn = xv * r
            dng[...] += jnp.sum(dhv * xn, axis=0, keepdims=True)
            dxn = dhv * g_ref[...]
            gx_ref[...] = dx2_ref[...] + r * (dxn - xn * jnp.mean(dxn * xn, axis=-1, keepdims=True))

        @at_step(n_steps - 1, nt - 1)
        def _():
            sm_buf[me] = sm_ref[...] + smb_ref[...]
            sm_buf[me, 0:1, :] = dng[...]
            for cp in smalls:
                cp.start()
            ici[2].wait_recv()
            acc[mine, :] += rcv[2].astype(F32)
            fin.start()
            store_w.start()
            for m in range(1, N_DEV):
                copy(SEM_SMALL + m, sm_buf.at[0], sm_buf.at[0], sibling).wait_recv()
            tot = sm_buf[0]
            for d in range(1, N_DEV):
                tot = tot + sm_buf[d]
            osm_ref[...] = tot
            fin_o.wait_recv()
            store_o.start()
            fin.wait_recv()
            for cp in d2d + [d2d_o] + ici + ici_o + [fin, fin_o] + smalls:
                cp.wait_send()
            store_o.wait()
            store_w.wait()

    def shard_of(s, kr):
        return kr[0] ^ (3 - jnp.minimum(s, 3))

    def tok(s, t):
        return jnp.where(s < N_SHARD, t, nt - 1)

    def blk_map(s, t, kr):
        return (jnp.where(s < N_SHARD, 0, (s - N_SHARD) * nt + t), 0)

    hbm = pl.BlockSpec(memory_space=pl.ANY)
    grid_spec = pltpu.PrefetchScalarGridSpec(
        num_scalar_prefetch=1, grid=(n_steps, nt),
        in_specs=[pl.BlockSpec((TT, D_MODEL), lambda s, t, kr: (tok(s, t), 0)),
                  pl.BlockSpec((TT, SHARD_COLS), lambda s, t, kr: (tok(s, t), shard_of(s, kr))),
                  pl.BlockSpec((1, D_MODEL, SHARD_COLS), lambda s, t, kr: (shard_of(s, kr), 0, 0)),
                  pl.BlockSpec((N_SHARD, WO_ROWS, D_MODEL), lambda s, t, kr: (0, 0, 0)),
                  pl.BlockSpec((TX, D_MODEL), blk_map),
                  pl.BlockSpec((TX, D_MODEL), blk_map),
                  pl.BlockSpec((1, D_MODEL), lambda s, t, kr: (0, 0)),
                  pl.BlockSpec((8, D_MODEL), lambda s, t, kr: (0, 0)),
                  pl.BlockSpec((8, D_MODEL), lambda s, t, kr: (0, 0))],
        out_specs=(pl.BlockSpec((TX, D_MODEL), blk_map), hbm, hbm,
                   pl.BlockSpec((8, D_MODEL), lambda s, t, kr: (0, 0))),
        scratch_shapes=[pltpu.VMEM((D_MODEL, SHARD_COLS), F32), pltpu.VMEM((SEQ, D_MODEL), F32),
                        pltpu.VMEM((N_SHARD, hw, SHARD_COLS), BF16), pltpu.VMEM((3, hw, SHARD_COLS), BF16),
                        pltpu.VMEM((N_SHARD, hw, SHARD_COLS), BF16), pltpu.VMEM((3, hw, SHARD_COLS), BF16),
                        pltpu.VMEM((N_SHARD, ho, D_MODEL), BF16), pltpu.VMEM((N_SHARD, ho, D_MODEL), BF16),
                        pltpu.VMEM((3, ho, D_MODEL), BF16), pltpu.VMEM((WO_ROWS, D_MODEL), F32),
                        pltpu.VMEM((N_DEV, 8, D_MODEL), F32), pltpu.VMEM((1, D_MODEL), F32),
                        pltpu.SemaphoreType.DMA((N_SEM_TAIL,)), pltpu.SemaphoreType.DMA((N_SEM_TAIL,)),
                        pltpu.SemaphoreType.DMA((2,))])
    return pl.pallas_call(
        body, name="bwd_tail", grid_spec=grid_spec,
        out_shape=(jax.ShapeDtypeStruct((SEQ, D_MODEL), F32),
                   jax.ShapeDtypeStruct((D_MODEL, SHARD_COLS), F32),
                   jax.ShapeDtypeStruct((WO_ROWS, D_MODEL), F32),
                   jax.ShapeDtypeStruct((8, D_MODEL), F32)),
        compiler_params=pltpu.CompilerParams(dimension_semantics=("arbitrary", "arbitrary"),
                                             vmem_limit_bytes=60 * 1024 * 1024, collective_id=COLLECTIVE_TAIL),
    )(kidx, h, dproj, wg, gwo, x2d, dx2, g1, small_a, small_b)


def _adam_update(w, g, m, v):
    nm = ADAM_B1 * m + (1.0 - ADAM_B1) * g
    nv = ADAM_B2 * v + (1.0 - ADAM_B2) * (g * g)
    m_hat = nm / (1.0 - ADAM_B1 ** ADAM_STEP)
    v_hat = nv / (1.0 - ADAM_B2 ** ADAM_STEP)
    return -ADAM_LR * (m_hat / (jnp.sqrt(v_hat) + ADAM_EPS) + ADAM_WD * w), nm, nv


def _adamw_all(tot, g_w_in, g_w_out, big, small, grad_x):
    n = len(small)
    rows = WO_ROWS
    steps = D_MODEL // rows

    def body(tot_ref, *refs):
        gx_ref, gx_out = refs[2 + 3 * (2 + n)], refs[-1]
        gx_out[...] = gx_ref[...]
        ins, outs = refs[:2 + 3 * (2 + n)], refs[3 + 3 * (2 + n):-1]
        g_refs, wmv = ins[:2], ins[2:]
        loss_ref, quads = outs[0], outs[1:]

        def update(j, g):
            w_ref, m_ref, v_ref = wmv[3 * j:3 * j + 3]
            g_ref, d_ref, nm_ref, nv_ref = quads[4 * j:4 * j + 4]
            g_ref[...] = g
            d_ref[...], nm_ref[...], nv_ref[...] = _adam_update(w_ref[...], g, m_ref[...], v_ref[...])

        update(0, g_refs[0][...])

        @pl.when(pl.program_id(0) == 0)
        def _():
            update(1, g_refs[1][...])
            k = 2 * lax.axis_index("x") + lax.axis_index("y")
            mine = pl.ds(pl.multiple_of(k * HEAD, HEAD), HEAD)
            loss_ref[...] = tot_ref[7:8, 0:1]
            grads = [tot_ref[0:1, :], tot_ref[1:2, :], tot_ref[2:3, 0:D_HGRN], tot_ref[2:3, D_HGRN:],
                     jnp.concatenate([tot_ref[3:4, 0:D_HGRN], tot_ref[3:4, D_HGRN:]], axis=0),
                     jnp.concatenate([tot_ref[4 + tap:5 + tap, mine] for tap in range(3)], axis=1)]
            for j, g in enumerate(grads):
                update(2 + j, g)

    whole = lambda a: pl.BlockSpec(a.shape, lambda i: (0, 0))
    blk = pl.BlockSpec((rows, SHARD_COLS), lambda i: (i, 0))
    arrays = [a for triple in big + small for a in triple]
    in_specs = ([whole(tot), blk, whole(g_w_out)] + [blk] * 3 + [whole(a) for a in arrays[3:]])
    shapes = [big[0][0], big[1][0]] + [w for w, _, _ in small]
    out_shape = (jax.ShapeDtypeStruct((1, 1), F32),) + tuple(
        jax.ShapeDtypeStruct(w.shape, F32) for w in shapes for _ in range(4))
    out_specs = (pl.BlockSpec((1, 1), lambda i: (0, 0)),) + (blk,) * 4 + tuple(
        whole(w) for w in shapes[1:] for _ in range(4))
    gx_blk = pl.BlockSpec((SEQ // steps, D_MODEL), lambda i: (i, 0))
    outs = pl.pallas_call(
        body, name="adamw_all", grid=(steps,),
        out_shape=out_shape + (jax.ShapeDtypeStruct(grad_x.shape, F32),),
        in_specs=in_specs + [gx_blk], out_specs=out_specs + (gx_blk,),
        compiler_params=pltpu.CompilerParams(dimension_semantics=("arbitrary",), vmem_limit_bytes=VMEM_LIMIT),
    )(tot, g_w_in, g_w_out, *arrays, grad_x)
    return [outs[0]] + [outs[1 + 4 * j:5 + 4 * j] for j in range(2 + n)] + [outs[-1]]


def _local_step(x2d, tgt, proj, lb_logits, cw, ga, gcn, w_out, gf):
    g64 = _group_matrix(HEAD, CONV_GROUP)
    aux, states, dx2, dmixed, gwo, part_out = _mix_out(proj, lb_logits, cw, ga, gcn, g64, w_out, x2d, gf, tgt)
    dproj, part_mix = _mix_bwd(proj, aux, states, dmixed, lb_logits, cw, ga, gcn, g64)
    return dproj, dx2, gwo.reshape(N_SHARD, WO_ROWS, D_MODEL), part_out, part_mix


def kernel(x, norm_gain, w_in, lb_logits, conv_w, hgrn_norm_gain, conv_norm_gain, w_out, final_norm_gain, loss_target, m_norm_gain, m_w_in, m_lb_logits, m_conv_w, m_hgrn_norm_gain, m_conv_norm_gain, m_w_out, m_final_norm_gain, v_norm_gain, v_w_in, v_lb_logits, v_conv_w, v_hgrn_norm_gain, v_conv_norm_gain, v_w_out, v_final_norm_gain):
    k = 2 * lax.axis_index("x") + lax.axis_index("y")
    kidx = jnp.reshape(k, (1,)).astype(jnp.int32)
    row = lambda a: a.reshape(1, D_MODEL)
    taps = lambda a: a.reshape(1, 3 * HEAD)
    h, proj, wg, cw = _gather_proj(kidx, x[0], norm_gain, w_in, taps(conv_w))
    dproj, dx2, gwo, part_out, part_mix = _local_step(
        x[0], loss_target[0], proj, lb_logits, cw, hgrn_norm_gain, conv_norm_gain, w_out, row(final_norm_gain))
    rgrad_x, rg_w_in, rg_w_out, tot = _bwd_tail(kidx, h, dproj, wg, gwo, x[0], dx2, norm_gain, part_out, part_mix)

    (loss, (g_w_in, d_w_in, nm_w_in, nv_w_in), (g_w_out, d_w_out, nm_w_out, nv_w_out),
     (g_norm_gain, d_ng, nm_ng, nv_ng), (g_final, d_fg, nm_fg, nv_fg), (g_hgrn, d_hg, nm_hg, nv_hg),
     (g_convn, d_cg, nm_cg, nv_cg), (g_lb, d_lb, nm_lb, nv_lb), (g_conv_w, d_cw, nm_cw, nv_cw),
     grad_x) = _adamw_all(
        tot, rg_w_in, rg_w_out,
        [(w_in[0], m_w_in[0], v_w_in[0]), (w_out[0], m_w_out[0], v_w_out[0])],
        [(norm_gain, m_norm_gain, v_norm_gain),
         (row(final_norm_gain), row(m_final_norm_gain), row(v_final_norm_gain)),
         (hgrn_norm_gain, m_hgrn_norm_gain, v_hgrn_norm_gain),
         (conv_norm_gain, m_conv_norm_gain, v_conv_norm_gain),
         (lb_logits, m_lb_logits, v_lb_logits),
         (taps(conv_w), taps(m_conv_w), taps(v_conv_w))],
        rgrad_x)
    flat = lambda a: a.reshape(D_MODEL)
    untap = lambda a: a.reshape(1, 3, HEAD)
    return (loss.reshape(()), grad_x[None],
            g_norm_gain, g_w_in[None], g_lb, untap(g_conv_w), g_hgrn, g_convn, g_w_out[None], flat(g_final),
            d_ng, d_w_in[None], d_lb, untap(d_cw), d_hg, d_cg, d_w_out[None], flat(d_fg),
            nm_ng, nm_w_in[None], nm_lb, untap(nm_cw), nm_hg, nm_cg, nm_w_out[None], flat(nm_fg),
            nv_ng, nv_w_in[None], nv_lb, untap(nv_cw), nv_hg, nv_cg, nv_w_out[None], flat(nv_fg))
```

```python
import jax
import jax.numpy as jnp
import numpy as np
from jax import lax
from jax.experimental import pallas as pl
from jax.experimental.pallas import tpu as pltpu

F32 = jnp.float32
BF16 = jnp.bfloat16
MESH = pl.DeviceIdType.MESH

SEQ = 2048
D_MODEL = 1024
D_HGRN = 512
D_CONV = 512
HEAD = 128
N_HEADS = 4
CHUNK = 64
CONV_GROUP = 64
N_SHARD = 4
SHARD_COLS = 1024
WO_ROWS = 256
EPS = 1e-6
TB = 256
NCB = TB // CHUNK
N_CHUNKS = SEQ // CHUNK
N_DEV = 8
COLLECTIVE_GATHER, COLLECTIVE_MIX_OUT, COLLECTIVE_TAIL = 1, 0, 2
AUX_O, AUX_CV, AUX_B, AUX_COLS = 0, 512, 1024, 1536

ADAM_LR = 0.001
ADAM_B1 = 0.9
ADAM_B2 = 0.999
ADAM_EPS = 1e-08
ADAM_WD = 0.01
ADAM_STEP = 10

VMEM_LIMIT = 56 * 1024 * 1024


def _dot(a, b):
    return jnp.dot(a, b, preferred_element_type=F32)


def _dot_nt(a, b):
    return lax.dot_general(a, b, (((1,), (1,)), ((), ())), preferred_element_type=F32)


def _dot_tn(a, b):
    return lax.dot_general(a, b, (((0,), (0,)), ((), ())), preferred_element_type=F32)


def _split_bf16(x, n):
    parts = []
    r = x
    for _ in range(n):
        p = r.astype(BF16)
        parts.append(p)
        r = r - p.astype(F32)
    return parts


def _exact_left(m, x, n=3):
    acc = None
    for p in _split_bf16(x, n):
        t = _dot(m, p)
        acc = t if acc is None else acc + t
    return acc


def _exact_left_many(m, xs, n=3):
    parts = [_split_bf16(x, n) for x in xs]
    accs = [None] * len(xs)
    for i in range(n):
        for j in range(len(xs)):
            t = _dot(m, parts[j][i])
            accs[j] = t if accs[j] is None else accs[j] + t
    return accs


def _group_mean_many(xs, gmat, n=2):
    parts = [_split_bf16(x, n) for x in xs]
    accs = [None] * len(xs)
    for i in range(n):
        for j in range(len(xs)):
            t = _dot(parts[j][i], gmat)
            accs[j] = t if accs[j] is None else accs[j] + t
    return accs


def _group_mean(x, gmat, n=2):
    w = gmat.shape[0]
    outs = []
    for c0 in range(0, x.shape[1], w):
        acc = None
        for p in _split_bf16(x[:, c0:c0 + w], n):
            t = _dot(p, gmat)
            acc = t if acc is None else acc + t
        outs.append(acc)
    return jnp.concatenate(outs, axis=1)


def _sigmoid(x):
    return 1.0 / (1.0 + jnp.exp(-x))


def _lower_bound(lbl):
    l0 = lbl[0:1, :]
    l1 = lbl[1:2, :]
    m = jnp.maximum(l0, l1)
    e0 = jnp.exp(l0 - m)
    e1 = jnp.exp(l1 - m)
    return e0 / (e0 + e1)


def _tri(lower):
    r = lax.broadcasted_iota(jnp.int32, (CHUNK, CHUNK), 0)
    c = lax.broadcasted_iota(jnp.int32, (CHUNK, CHUNK), 1)
    return jnp.where((c <= r) if lower else (c >= r), 1.0, 0.0).astype(BF16)


def _causal():
    r = lax.broadcasted_iota(jnp.int32, (CHUNK, CHUNK), 0)
    c = lax.broadcasted_iota(jnp.int32, (CHUNK, CHUNK), 1)
    return c <= r


def _shift_down(x, sh, prev_tail):
    r = pltpu.roll(x, sh, 0)
    pt = pltpu.roll(prev_tail, sh, 0)
    rows = lax.broadcasted_iota(jnp.int32, prev_tail.shape, 0)
    top = jnp.where(rows < sh, pt, r[0:8])
    return jnp.concatenate([top, r[8:]], axis=0)


def _shift_up(x, sh, next_head):
    n = x.shape[0]
    r = pltpu.roll(x, n - sh, 0)
    nh = pltpu.roll(next_head, 8 - sh, 0)
    rows = lax.broadcasted_iota(jnp.int32, next_head.shape, 0)
    bot = jnp.where(rows >= 8 - sh, nh, r[n - 8:])
    return jnp.concatenate([r[:n - 8], bot], axis=0)


def _group_matrix(width, group):
    r = np.arange(width)[:, None] // group
    c = np.arange(width)[None, :] // group
    return jnp.asarray(np.where(r == c, 1.0 / group, 0.0), dtype=BF16)


TG = 1024
SEM_W, SEM_CW, SEM_W_FWD, N_SEM = 0, 4, 7, 11


def _gather_proj(kidx, x2d, g1, w_in, conv_w):
    half_w = D_MODEL // 2
    half_c = SHARD_COLS // 2
    nt = SEQ // TG
    n_steps = 2 * N_SHARD

    def body(k_ref, x_ref, g_ref, w_ref, cw_ref, h_ref, p_ref, wg_out, cwg_out,
             wg_v, cwg_v, send_sems, recv_sems, out_sems):
        s, t = pl.program_id(0), pl.program_id(1)
        x, y, c = lax.axis_index("x"), lax.axis_index("y"), lax.axis_index("c")
        k = 2 * x + y
        sibling = (x, y, 1 - c)
        chips = [(1 - x, y), (x, 1 - y), (1 - x, 1 - y)]
        kjs = [2 * cx + cy for cx, cy in chips]
        diag = (*chips[2], c)

        def w_half(kk, cc):
            return wg_v.at[kk, pl.ds(cc * half_w, half_w), :]

        def w_quarter(kk, cc, piece):
            return wg_v.at[kk, pl.ds(cc * half_w, half_w), piece * half_c:(piece + 1) * half_c]

        def cw_of(kk):
            return cwg_v.at[:, pl.ds(pl.multiple_of(kk * HEAD, HEAD), HEAD)]

        def copy(sem, ref, to):
            return pltpu.make_async_remote_copy(
                src_ref=ref, dst_ref=ref, send_sem=send_sems.at[sem], recv_sem=recv_sems.at[sem],
                device_id=to, device_id_type=MESH)

        def at_step(sv, tv):
            return pl.when((s == sv) & (t == tv))

        w_direct = ([copy(SEM_W + j, w_half(k, c), (*chips[j], c)) for j in range(2)]
                    + [copy(SEM_W + 2 + p, w_quarter(k, c, p), diag) for p in range(2)])
        cw_direct = [copy(SEM_CW + j, cw_of(k), (*chip, c)) for j, chip in enumerate(chips)]
        w_passed = ([copy(SEM_W_FWD + j, w_half(kjs[j], c), sibling) for j in range(2)]
                    + [copy(SEM_W_FWD + 2 + p, w_quarter(kjs[2], c, p), sibling) for p in range(2)])
        stores = ([pltpu.make_async_copy(wg_v.at[kk], wg_out.at[kk], out_sems.at[i])
                   for i, kk in enumerate([k] + kjs)]
                  + [pltpu.make_async_copy(cwg_v, cwg_out, out_sems.at[4])])

        @at_step(0, 0)
        def _():
            barrier = pltpu.get_barrier_semaphore()
            for peer in [sibling] + [(*chip, c) for chip in chips]:
                pl.semaphore_signal(barrier, inc=1, device_id=peer, device_id_type=MESH)
            wg_v[k] = w_ref[0].astype(BF16)
            mine = pl.ds(pl.multiple_of(k * HEAD, HEAD), HEAD)
            cwg_v[:, mine] = jnp.zeros((8, HEAD), F32)
            for tap in range(3):
                cwg_v[tap:tap + 1, mine] = cw_ref[:, tap * HEAD:(tap + 1) * HEAD]
            pl.semaphore_wait(barrier, 4)
            w_direct[0].start()
            w_direct[1].start()
            for cp in cw_direct:
                cp.start()
            stores[0].start()

        @at_step(2, 0)
        def _():
            for j in range(2):
                copy(SEM_W + j, w_half(kjs[j], c), sibling).wait_recv()
                w_passed[j].start()
            w_direct[2].start()
            w_direct[3].start()
            copy(SEM_W_FWD, w_half(kjs[0], 1 - c), sibling).wait_recv()
            stores[1].start()

        @at_step(4, 0)
        def _():
            copy(SEM_W_FWD + 1, w_half(kjs[1], 1 - c), sibling).wait_recv()
            stores[2].start()

        for p in range(2):
            @at_step(6 + p, 0)
            def _(p=p):
                copy(SEM_W + 2 + p, w_quarter(kjs[2], c, p), sibling).wait_recv()
                w_passed[2 + p].start()
                copy(SEM_W_FWD + 2 + p, w_quarter(kjs[2], 1 - c, p), sibling).wait_recv()

        rows = pl.ds(pl.multiple_of(t * TG, TG), TG)

        @pl.when(s == 0)
        def _():
            xv = x_ref[...]
            r = lax.rsqrt(jnp.mean(xv * xv, axis=-1, keepdims=True) + EPS)
            h_ref[rows, :] = (xv * r * g_ref[...]).astype(BF16)

        sh = s >> 1
        js = k ^ (((sh & 1) << 1) | (sh >> 1))
        for piece in range(2):
            @pl.when((s & 1) == piece)
            def _(piece=piece):
                p_ref[...] = _dot(h_ref[rows, :], wg_v[js, :, piece * half_c:(piece + 1) * half_c])

        @at_step(n_steps - 1, nt - 1)
        def _():
            stores[3].start()
            for j in range(3):
                copy(SEM_CW + j, cw_of(kjs[j]), sibling).wait_recv()
            stores[4].start()
            for cp in w_direct + cw_direct + w_passed:
                cp.wait_send()
            for st in stores:
                st.wait()

    def x_map(s, t, kr):
        return (jnp.where(s == 0, t, nt - 1), 0)

    def p_map(s, t, kr):
        sh = s >> 1
        return (t, 2 * (kr[0] ^ (((sh & 1) << 1) | (sh >> 1))) + (s & 1))

    hbm = pl.BlockSpec(memory_space=pl.ANY)
    grid_spec = pltpu.PrefetchScalarGridSpec(
        num_scalar_prefetch=1, grid=(n_steps, nt),
        in_specs=[pl.BlockSpec((TG, D_MODEL), x_map),
                  pl.BlockSpec((1, D_MODEL), lambda s, t, kr: (0, 0)),
                  pl.BlockSpec((1, D_MODEL, SHARD_COLS), lambda s, t, kr: (0, 0, 0)),
                  pl.BlockSpec((1, 3 * HEAD), lambda s, t, kr: (0, 0))],
        out_specs=(pl.BlockSpec((SEQ, D_MODEL), lambda s, t, kr: (0, 0)),
                   pl.BlockSpec((TG, half_c), p_map), hbm, hbm),
        scratch_shapes=[pltpu.VMEM((N_SHARD, D_MODEL, SHARD_COLS), BF16),
                        pltpu.VMEM((8, D_CONV), F32),
                        pltpu.SemaphoreType.DMA((N_SEM,)), pltpu.SemaphoreType.DMA((N_SEM,)),
                        pltpu.SemaphoreType.DMA((5,))])
    return pl.pallas_call(
        body, name="gather_proj", grid_spec=grid_spec,
        out_shape=(jax.ShapeDtypeStruct((SEQ, D_MODEL), BF16),
                   jax.ShapeDtypeStruct((SEQ, N_SHARD * SHARD_COLS), F32),
                   jax.ShapeDtypeStruct((N_SHARD, D_MODEL, SHARD_COLS), BF16),
                   jax.ShapeDtypeStruct((8, D_CONV), F32)),
        compiler_params=pltpu.CompilerParams(dimension_semantics=("arbitrary", "arbitrary"),
                                             vmem_limit_bytes=VMEM_LIMIT, collective_id=COLLECTIVE_GATHER),
    )(kidx, x2d, g1, w_in, conv_w)


LAG = 6


def _mix_out(proj, lb_logits, cw, ga, gcn, g64, w_out, x2d, gf, tgt):
    half_o = WO_ROWS // 2
    nblk = SEQ // TB
    n_steps = nblk + LAG

    def body(p_ref, lbl_ref, cw_ref, ga_ref, gcn_ref, g64_ref, wo_ref, x_ref, gf_ref, t_ref,
             aux_ref, sto_ref, dx2_ref, dm_ref, gwo_ref, part_ref,
             st_ref, tail_ref, wog_v, stage, ring, acc_ref, send_sems, recv_sems):
        i = pl.program_id(0)
        x, y, c = lax.axis_index("x"), lax.axis_index("y"), lax.axis_index("c")
        k = 2 * x + y
        sibling = (x, y, 1 - c)
        chips = [(1 - x, y), (x, 1 - y), (1 - x, 1 - y)]
        kjs = [2 * cx + cy for cx, cy in chips]

        def wo_half(kk, cc):
            return wog_v.at[pl.ds(pl.multiple_of(kk * WO_ROWS + cc * half_o, half_o), half_o), :]

        def copy(sem, ref, to):
            return pltpu.make_async_remote_copy(
                src_ref=ref, dst_ref=ref, send_sem=send_sems.at[sem], recv_sem=recv_sems.at[sem],
                device_id=to, device_id_type=MESH)

        wo_direct = [copy(j, wo_half(k, c), (*chip, c)) for j, chip in enumerate(chips)]
        wo_passed = [copy(3 + j, wo_half(kj, c), sibling) for j, kj in enumerate(kjs)]

        @pl.when(i == 0)
        def _():
            barrier = pltpu.get_barrier_semaphore()
            for peer in [sibling] + [(*chip, c) for chip in chips]:
                pl.semaphore_signal(barrier, inc=1, device_id=peer, device_id_type=MESH)
            st_ref[...] = jnp.zeros_like(st_ref)
            tail_ref[...] = jnp.zeros_like(tail_ref)
            acc_ref[...] = jnp.zeros_like(acc_ref)
            part_ref[...] = jnp.zeros_like(part_ref)
            wog_v[pl.ds(pl.multiple_of(k * WO_ROWS, WO_ROWS), WO_ROWS), :] = wo_ref[0].astype(BF16)
            pl.semaphore_wait(barrier, 4)
            for cp in wo_direct:
                cp.start()

        @pl.when(i == LAG - 1)
        def _():
            for j in range(3):
                copy(j, wo_half(kjs[j], c), sibling).wait_recv()
                wo_passed[j].start()

        @pl.when(i == LAG)
        def _():
            for j in range(3):
                copy(3 + j, wo_half(kjs[j], 1 - c), sibling).wait_recv()

        lb = _lower_bound(lbl_ref[...])
        tri = _tri(True)
        causal = _causal()
        g64m = g64_ref[...]
        heads = range(N_HEADS)
        cs = [slice(hd * HEAD, (hd + 1) * HEAD) for hd in heads]
        col = lambda base, hd: slice(base + hd * HEAD, base + (hd + 1) * HEAD)

        def mix_chunk(n):
            sl = pl.ds(n * CHUNK, CHUNK)
            sg = [_sigmoid(p_ref[sl, col(512, hd)]) for hd in heads]
            f = [lb[:, cs[hd]] + (1.0 - lb[:, cs[hd]]) * sg[hd] for hd in heads]
            bc = _exact_left_many(tri, [jnp.log(f[hd]) for hd in heads])
            for hd in heads:
                aux_ref[sl, col(AUX_B, hd)] = bc[hd]
            g = [bc[hd][CHUNK - 1:CHUNK, :] for hd in heads]
            qd = [(p_ref[sl, col(0, hd)] * jnp.exp(bc[hd])).astype(BF16) for hd in heads]
            ki = [((1.0 - f[hd]) * jnp.exp(-bc[hd])).astype(BF16) for hd in heads]
            ke = [((1.0 - f[hd]) * jnp.exp(g[hd] - bc[hd])).astype(BF16) for hd in heads]
            vb = [p_ref[sl, col(1024, hd)].astype(BF16) for hd in heads]
            st = [st_ref[hd] for hd in heads]
            st_b = [a.astype(BF16) for a in st]
            for hd in heads:
                sto_ref[n, hd] = st_b[hd]
            scm = [_dot_nt(qd[hd], ki[hd]) for hd in heads]
            inter = [_dot_nt(qd[hd], st_b[hd]) for hd in heads]
            upd = [_dot_tn(vb[hd], ke[hd]) for hd in heads]
            intra = [_dot(jnp.where(causal, scm[hd], 0.0).astype(BF16), vb[hd]) for hd in heads]
            for hd in heads:
                st_ref[hd] = st[hd] * jnp.exp(g[hd]) + upd[hd]
                o = intra[hd] + inter[hd]
                aux_ref[sl, col(AUX_O, hd)] = o
                ra = lax.rsqrt(jnp.mean(o * o, axis=-1, keepdims=True) + EPS)
                za = p_ref[sl, col(1536, hd)]
                stage[sl, cs[hd]] = (o * ra * ga_ref[:, cs[hd]] * (za * _sigmoid(za))).astype(BF16)
            yb = []
            for hd in heads:
                cu = p_ref[sl, col(3072, hd)] * p_ref[sl, col(2048, hd)]
                tail = tail_ref[:, cs[hd]]
                cv = (cw_ref[0:1, cs[hd]] * _shift_down(cu, 2, tail) + cw_ref[1:2, cs[hd]] * _shift_down(cu, 1, tail)
                      + cw_ref[2:3, cs[hd]] * cu)
                tail_ref[:, cs[hd]] = cu[CHUNK - 8:, :]
                aux_ref[sl, col(AUX_CV, hd)] = cv
                yb.append(p_ref[sl, col(2560, hd)] * cv)
            ms = _group_mean_many([y * y for y in yb], g64m)
            for hd in heads:
                rb = lax.rsqrt(ms[hd] + EPS)
                zb = p_ref[sl, col(3584, hd)]
                stage[sl, col(512, hd)] = (yb[hd] * rb * gcn_ref[:, cs[hd]] * (zb * _sigmoid(zb))).astype(BF16)

        def step(mix, project):
            if project:
                mixed_b = ring[pl.ds(pl.multiple_of((i - LAG) * TB, TB), TB), :]
                y = _dot(mixed_b, wog_v[...])
            if mix:
                mix_chunk(0)
            if project:
                x2 = x_ref[...] + y
                r2 = lax.rsqrt(jnp.mean(x2 * x2, axis=-1, keepdims=True) + EPS)
                n2 = x2 * r2
                gfv = gf_ref[...]
                err = n2 * gfv - t_ref[...]
                loss = 0.5 * jnp.sum(jnp.mean(err * err, axis=-1, keepdims=True), axis=0, keepdims=True)
                dy = err * (1.0 / D_MODEL)
                part_ref[1:2, :] += jnp.sum(dy * n2, axis=0, keepdims=True)
                part_ref[7:8, :] += jnp.broadcast_to(loss, (1, D_MODEL))
                dn = dy * gfv
                dx2 = r2 * (dn - n2 * jnp.mean(dn * n2, axis=-1, keepdims=True))
                dx2_ref[...] = dx2
                dx2_b = dx2.astype(BF16)
            if mix:
                mix_chunk(1)
            if project:
                dm_ref[...] = _dot_nt(dx2_b, wog_v[...])
            if mix:
                mix_chunk(2)
            if project:
                acc_ref[...] += _dot_tn(mixed_b, dx2_b)
            if mix:
                mix_chunk(3)
                ring[pl.ds(pl.multiple_of(i * TB, TB), TB), :] = stage[...]

        @pl.when(i < LAG)
        def _():
            step(True, False)

        @pl.when((i >= LAG) & (i < nblk))
        def _():
            step(True, True)

        @pl.when(i >= nblk)
        def _():
            step(False, True)

        @pl.when(i == n_steps - 1)
        def _():
            gwo_ref[...] = acc_ref[...].astype(BF16)
            for cp in wo_direct + wo_passed:
                cp.wait_send()

    assert NCB == 4
    row = lambda w: pl.BlockSpec((1, w), lambda i: (0, 0))
    mix_blk = lambda i: jnp.minimum(i, nblk - 1)
    out_blk = lambda i: jnp.clip(i - LAG, 0, nblk - 1)
    tok = lambda: pl.BlockSpec((TB, D_MODEL), lambda i: (out_blk(i), 0))
    return pl.pallas_call(
        body, name="mix_out", grid=(n_steps,),
        out_shape=(jax.ShapeDtypeStruct((SEQ, AUX_COLS), F32),
                   jax.ShapeDtypeStruct((N_CHUNKS, N_HEADS, HEAD, HEAD), BF16),
                   jax.ShapeDtypeStruct((SEQ, D_MODEL), F32),
                   jax.ShapeDtypeStruct((SEQ, D_MODEL), F32),
                   jax.ShapeDtypeStruct((D_MODEL, D_MODEL), BF16),
                   jax.ShapeDtypeStruct((8, D_MODEL), F32)),
        in_specs=[pl.BlockSpec((TB, 4096), lambda i: (jnp.minimum(i, nblk - 1), 0)),
                  pl.BlockSpec((2, D_HGRN), lambda i: (0, 0)),
                  pl.BlockSpec((8, D_CONV), lambda i: (0, 0)),
                  row(D_HGRN), row(D_CONV),
                  pl.BlockSpec((HEAD, HEAD), lambda i: (0, 0)),
                  pl.BlockSpec((1, WO_ROWS, D_MODEL), lambda i: (0, 0, 0)),
                  tok(), row(D_MODEL), tok()],
        out_specs=(pl.BlockSpec((TB, AUX_COLS), lambda i: (mix_blk(i), 0)),
                   pl.BlockSpec((NCB, N_HEADS, HEAD, HEAD), lambda i: (mix_blk(i), 0, 0, 0)),
                   tok(), tok(),
                   pl.BlockSpec((D_MODEL, D_MODEL), lambda i: (0, 0)),
                   pl.BlockSpec((8, D_MODEL), lambda i: (0, 0))),
        scratch_shapes=[pltpu.VMEM((N_HEADS, HEAD, HEAD), F32), pltpu.VMEM((8, D_CONV), F32),
                        pltpu.VMEM((D_MODEL, D_MODEL), BF16), pltpu.VMEM((TB, D_MODEL), BF16),
                        pltpu.VMEM((SEQ, D_MODEL), BF16), pltpu.VMEM((D_MODEL, D_MODEL), F32),
                        pltpu.SemaphoreType.DMA((6,)), pltpu.SemaphoreType.DMA((6,))],
        compiler_params=pltpu.CompilerParams(dimension_semantics=("arbitrary",), vmem_limit_bytes=VMEM_LIMIT,
                                             collective_id=COLLECTIVE_MIX_OUT),
    )(proj, lb_logits, cw, ga, gcn, g64, w_out, x2d, gf, tgt)


def _mix_bwd(proj, aux, states, dmixed, lb_logits, cw, ga, gcn, g64):
    nblk = SEQ // TB

    def body(p_ref, aux_ref, st_ref, dm_ref, lbl_ref, cw_ref, ga_ref, gcn_ref, g64_ref,
             dp_ref, part_ref, dst_ref, head_ref, dlb_ref):
        i = pl.program_id(0)

        @pl.when(i == 0)
        def _():
            dst_ref[...] = jnp.zeros_like(dst_ref)
            head_ref[...] = jnp.zeros_like(head_ref)
            part_ref[...] = jnp.zeros_like(part_ref)
            dlb_ref[...] = jnp.zeros_like(dlb_ref)

        lb = _lower_bound(lbl_ref[...])
        triu = _tri(False)
        causal = _causal()
        g64m = g64_ref[...]
        rowsum = lambda a: jnp.sum(a, axis=0, keepdims=True)
        heads = range(N_HEADS)
        cs = [slice(hd * HEAD, (hd + 1) * HEAD) for hd in heads]
        col = lambda base, hd: slice(base + hd * HEAD, base + (hd + 1) * HEAD)
        for n in reversed(range(NCB)):
            sl = pl.ds(n * CHUNK, CHUNK)
            cvv = [aux_ref[sl, col(AUX_CV, hd)] for hd in heads]
            gb = [p_ref[sl, col(2560, hd)] for hd in heads]
            yb = [gb[hd] * cvv[hd] for hd in heads]
            ms = _group_mean_many([y * y for y in yb], g64m)
            rb, nb, dnb = [], [], []
            for hd in heads:
                rb.append(lax.rsqrt(ms[hd] + EPS))
                nb.append(yb[hd] * rb[hd])
                zb = p_ref[sl, col(3584, hd)]
                sgb = _sigmoid(zb)
                dmb = dm_ref[sl, col(512, hd)]
                gcv = gcn_ref[:, cs[hd]]
                part_ref[2:3, col(512, hd)] += rowsum(dmb * nb[hd] * (zb * sgb))
                dp_ref[sl, col(3584, hd)] = (dmb * nb[hd] * gcv * (sgb * (1.0 + zb * (1.0 - sgb)))).astype(BF16)
                dnb.append(dmb * gcv * (zb * sgb))
            mdn = _group_mean_many([dnb[hd] * nb[hd] for hd in heads], g64m)
            for hd in heads:
                dyb = rb[hd] * (dnb[hd] - nb[hd] * mdn[hd])
                dp_ref[sl, col(2560, hd)] = (dyb * cvv[hd]).astype(BF16)
                dcv = dyb * gb[hd]
                head = head_ref[:, cs[hd]]
                dcv1 = _shift_up(dcv, 1, head)
                dcv2 = _shift_up(dcv, 2, head)
                head_ref[:, cs[hd]] = dcv[0:8, :]
                u = p_ref[sl, col(2048, hd)]
                gc = p_ref[sl, col(3072, hd)]
                cu = gc * u
                part_ref[4:5, cs[hd]] += rowsum(dcv2 * cu)
                part_ref[5:6, cs[hd]] += rowsum(dcv1 * cu)
                part_ref[6:7, cs[hd]] += rowsum(dcv * cu)
                dcu = cw_ref[2:3, cs[hd]] * dcv + cw_ref[1:2, cs[hd]] * dcv1 + cw_ref[0:1, cs[hd]] * dcv2
                dp_ref[sl, col(3072, hd)] = (dcu * u).astype(BF16)
                dp_ref[sl, col(2048, hd)] = (dcu * gc).astype(BF16)
            do_b = []
            for hd in heads:
                ov = aux_ref[sl, col(AUX_O, hd)]
                ra = lax.rsqrt(jnp.mean(ov * ov, axis=-1, keepdims=True) + EPS)
                na = ov * ra
                za = p_ref[sl, col(1536, hd)]
                sga = _sigmoid(za)
                dma = dm_ref[sl, cs[hd]]
                gav = ga_ref[:, cs[hd]]
                part_ref[2:3, cs[hd]] += rowsum(dma * na * (za * sga))
                dp_ref[sl, col(1536, hd)] = (dma * na * gav * (sga * (1.0 + za * (1.0 - sga)))).astype(BF16)
                dna = dma * gav * (za * sga)
                do_b.append((ra * (dna - na * jnp.mean(dna * na, axis=-1, keepdims=True))).astype(BF16))
            s = [_sigmoid(p_ref[sl, col(512, hd)]) for hd in heads]
            f = [lb[:, cs[hd]] + (1.0 - lb[:, cs[hd]]) * s[hd] for hd in heads]
            bc = [aux_ref[sl, col(AUX_B, hd)] for hd in heads]
            g = [bc[hd][CHUNK - 1:CHUNK, :] for hd in heads]
            eb = [jnp.exp(bc[hd]) for hd in heads]
            enb = [jnp.exp(-bc[hd]) for hd in heads]
            eg = [jnp.exp(g[hd] - bc[hd]) for hd in heads]
            dec = [jnp.exp(g[hd]) for hd in heads]
            qd = [p_ref[sl, cs[hd]] * eb[hd] for hd in heads]
            ki = [(1.0 - f[hd]) * enb[hd] for hd in heads]
            ke = [(1.0 - f[hd]) * eg[hd] for hd in heads]
            qd_b = [a.astype(BF16) for a in qd]
            ki_b = [a.astype(BF16) for a in ki]
            ke_b = [a.astype(BF16) for a in ke]
            vb = [p_ref[sl, col(1024, hd)].astype(BF16) for hd in heads]
            st_b = [st_ref[n, hd] for hd in heads]
            dst = [dst_ref[hd] for hd in heads]
            dst_b = [a.astype(BF16) for a in dst]
            scm = [_dot_nt(qd_b[hd], ki_b[hd]) for hd in heads]
            amm = [_dot_nt(do_b[hd], vb[hd]) for hd in heads]
            dqd2 = [_dot(do_b[hd], st_b[hd]) for hd in heads]
            dke = [_dot(vb[hd], dst_b[hd]) for hd in heads]
            dv2 = [_dot_nt(ke_b[hd], dst_b[hd]) for hd in heads]
            dsu = [_dot_tn(do_b[hd], qd_b[hd]) for hd in heads]
            sc = [jnp.where(causal, scm[hd], 0.0).astype(BF16) for hd in heads]
            am = [jnp.where(causal, amm[hd], 0.0).astype(BF16) for hd in heads]
            dqd1 = [_dot(am[hd], ki_b[hd]) for hd in heads]
            dki = [_dot_tn(am[hd], qd_b[hd]) for hd in heads]
            dv1 = [_dot_tn(sc[hd], do_b[hd]) for hd in heads]
            db, dgv = [], []
            for hd in heads:
                dqd = dqd1[hd] + dqd2[hd]
                ddec = rowsum(dst[hd] * st_b[hd].astype(F32))
                dst_ref[hd] = dst[hd] * dec[hd] + dsu[hd]
                dp_ref[sl, cs[hd]] = (dqd * eb[hd]).astype(BF16)
                dp_ref[sl, col(1024, hd)] = (dv1[hd] + dv2[hd]).astype(BF16)
                db.append(dqd * qd[hd] - dki[hd] * ki[hd] - dke[hd] * ke[hd])
                dgv.append(rowsum(dke[hd] * ke[hd]) + ddec * dec[hd])
            rc = _exact_left_many(triu, db, 2)
            for hd in heads:
                df = (rc[hd] + dgv[hd]) / f[hd] - (dki[hd] * enb[hd] + dke[hd] * eg[hd])
                dlb_ref[:, cs[hd]] += rowsum(df * (1.0 - s[hd]))
                dp_ref[sl, col(512, hd)] = (df * (1.0 - lb[:, cs[hd]]) * s[hd] * (1.0 - s[hd])).astype(BF16)

        @pl.when(i == nblk - 1)
        def _():
            row = dlb_ref[...] * lb * (1.0 - lb)
            part_ref[3:4, 0:D_HGRN] = row
            part_ref[3:4, D_HGRN:] = -row

    rev = lambda w: pl.BlockSpec((TB, w), lambda i: (nblk - 1 - i, 0))
    row = lambda w: pl.BlockSpec((1, w), lambda i: (0, 0))
    return pl.pallas_call(
        body, name="mix_bwd", grid=(nblk,),
        out_shape=(jax.ShapeDtypeStruct((SEQ, 4096), BF16),
                   jax.ShapeDtypeStruct((8, D_MODEL), F32)),
        in_specs=[rev(4096), rev(AUX_COLS),
                  pl.BlockSpec((NCB, N_HEADS, HEAD, HEAD), lambda i: (nblk - 1 - i, 0, 0, 0)),
                  rev(D_MODEL),
                  pl.BlockSpec((2, D_HGRN), lambda i: (0, 0)),
                  pl.BlockSpec((8, D_CONV), lambda i: (0, 0)),
                  row(D_HGRN), row(D_CONV),
                  pl.BlockSpec((HEAD, HEAD), lambda i: (0, 0))],
        out_specs=(rev(4096), pl.BlockSpec((8, D_MODEL), lambda i: (0, 0))),
        scratch_shapes=[pltpu.VMEM((N_HEADS, HEAD, HEAD), F32), pltpu.VMEM((8, D_CONV), F32),
                        pltpu.VMEM((1, D_HGRN), F32)],
        compiler_params=pltpu.CompilerParams(dimension_semantics=("arbitrary",), vmem_limit_bytes=VMEM_LIMIT),
    )(proj, aux, states, dmixed, lb_logits, cw, ga, gcn, g64)


TT = 1024
TX = 256
(SEM_D2D, SEM_D2D_O, SEM_ICI, SEM_ICI_O, SEM_FIN, SEM_FIN_O, SEM_SMALL, N_SEM_TAIL) = 0, 4, 5, 8, 11, 12, 12, 20


def _bwd_tail(kidx, h, dproj, wg, gwo, x2d, dx2, g1, small_a, small_b):
    hw = D_MODEL // 2
    ho = WO_ROWS // 2
    nt = SEQ // TT
    n_steps = N_SHARD + SEQ // TX // nt

    def body(k_ref, h_ref, dp_ref, w_ref, gwo_ref, x_ref, dx2_ref, g_ref, sm_ref, smb_ref,
             gx_ref, gw_out, gwo_out, osm_ref,
             acc, dh, sendbuf, keep, sibrcv, rcv, sib_o, p_o, rcv_o, res_o, sm_buf, dng,
             send_sems, recv_sems, out_sems):
        s, t = pl.program_id(0), pl.program_id(1)
        x, y, c = lax.axis_index("x"), lax.axis_index("y"), lax.axis_index("c")
        k = 2 * x + y
        me = 4 * x + 2 * y + c
        sibling = (x, y, 1 - c)
        chips = [(1 - x, 1 - y), (1 - x, y), (x, 1 - y)]
        kjs = [2 * cx + cy for cx, cy in chips]
        mine = pl.ds(pl.multiple_of(c * hw, hw), hw)
        other = pl.ds(pl.multiple_of((1 - c) * hw, hw), hw)
        mine_o = pl.ds(pl.multiple_of(c * ho, ho), ho)
        other_o = pl.ds(pl.multiple_of((1 - c) * ho, ho), ho)

        def copy(sem, src, dst, to):
            return pltpu.make_async_remote_copy(
                src_ref=src, dst_ref=dst, send_sem=send_sems.at[sem], recv_sem=recv_sems.at[sem],
                device_id=to, device_id_type=MESH)

        def at_step(sv, tv):
            return pl.when((s == sv) & (t == tv))

        def at_norm_block(b):
            return at_step(N_SHARD + b // nt, b % nt)

        d2d = [copy(SEM_D2D + sv, sendbuf.at[sv], sibrcv.at[sv], sibling) for sv in range(N_SHARD)]
        d2d_o = copy(SEM_D2D_O, gwo_ref.at[:, other_o, :], sib_o, sibling)
        ici = [copy(SEM_ICI + sv, keep.at[sv], rcv.at[sv], (*chips[sv], c)) for sv in range(3)]
        ici_o = [copy(SEM_ICI_O + sv, p_o.at[kjs[sv]], rcv_o.at[sv], (*chips[sv], c)) for sv in range(3)]
        fin = copy(SEM_FIN, acc.at[mine, :], gw_out.at[mine, :], sibling)
        fin_o = copy(SEM_FIN_O, res_o.at[mine_o, :], res_o.at[mine_o, :], sibling)
        smalls = [copy(SEM_SMALL + m, sm_buf.at[me], sm_buf.at[me],
                       (x ^ (m >> 2), y ^ ((m >> 1) & 1), c ^ (m & 1))) for m in range(1, N_DEV)]
        store_w = pltpu.make_async_copy(acc.at[mine, :], gw_out.at[mine, :], out_sems.at[0])
        store_o = pltpu.make_async_copy(res_o, gwo_out, out_sems.at[1])

        @at_step(0, 0)
        def _():
            barrier = pltpu.get_barrier_semaphore()
            for m in range(1, N_DEV):
                pl.semaphore_signal(barrier, inc=1, device_id=(x ^ (m >> 2), y ^ ((m >> 1) & 1), c ^ (m & 1)),
                                    device_id_type=MESH)
            pl.semaphore_wait(barrier, N_DEV - 1)
            d2d_o.start()

        @at_step(0, 1)
        def _():
            d2d_o.wait_recv()
            for j in range(N_SHARD):
                p_o[j] = (gwo_ref[j, mine_o, :].astype(F32) + sib_o[j].astype(F32)).astype(BF16)
            res_o[mine_o, :] = gwo_ref[k, mine_o, :].astype(F32) + sib_o[k].astype(F32)
            for cp in ici_o:
                cp.start()

        rows = pl.ds(pl.multiple_of(t * TT, TT), TT)

        @pl.when(s < N_SHARD)
        def _():
            part = _dot_tn(h_ref[...], dp_ref[...])

            @pl.when(t == 0)
            def _():
                acc[...] = part

            @pl.when(t > 0)
            def _():
                acc[...] += part

        for sv in range(N_SHARD):
            @at_step(sv, nt - 1)
            def _(sv=sv):
                sendbuf[sv] = acc[other, :].astype(BF16)
                if sv < 3:
                    keep[sv] = acc[mine, :].astype(BF16)
                d2d[sv].start()

        @pl.when(s < N_SHARD)
        def _():
            d = _dot_nt(dp_ref[...], w_ref[0])

            @pl.when(s == 0)
            def _():
                dh[rows, :] = d

            @pl.when(s > 0)
            def _():
                dh[rows, :] += d

        for sv in range(3):
            @at_step(sv, nt - 1)
            def _(sv=sv):
                d2d[sv].wait_recv()
                keep[sv] = (keep[sv].astype(F32) + sibrcv[sv].astype(F32)).astype(BF16)
                ici[sv].start()

        @at_norm_block(0)
        def _():
            d2d[3].wait_recv()
            ici[0].wait_recv()
            acc[mine, :] += sibrcv[3].astype(F32) + rcv[0].astype(F32)

        @at_norm_block(1)
        def _():
            tot = res_o[mine_o, :]
            for sv in range(3):
                ici_o[sv].wait_recv()
                tot = tot + rcv_o[sv].astype(F32)
            res_o[mine_o, :] = tot
            fin_o.start()

        @at_norm_block(2)
        def _():
            ici[1].wait_recv()
            acc[mine, :] += rcv[1].astype(F32)

        @at_norm_block(0)
        def _():
            dng[...] = jnp.zeros_like(dng)

        @pl.when(s >= N_SHARD)
        def _():
            blk = (s - N_SHARD) * nt + t
            dhv = dh[pl.ds(pl.multiple_of(blk * TX, TX), TX), :]
            xv = x_ref[...]
            r = lax.rsqrt(jnp.mean(xv * xv, axis=-1, keepdims=True) + EPS)
            xn = xv * r
            dng[...] += jnp.sum(dhv * xn, axis=0, keepdims=True)
            dxn = dhv * g_ref[...]
            gx_ref[...] = dx2_ref[...] + r * (dxn - xn * jnp.mean(dxn * xn, axis=-1, keepdims=True))

        @at_step(n_steps - 1, nt - 1)
        def _():
            sm_buf[me] = sm_ref[...] + smb_ref[...]
            sm_buf[me, 0:1, :] = dng[...]
            for cp in smalls:
                cp.start()
            ici[2].wait_recv()
            acc[mine, :] += rcv[2].astype(F32)
            fin.start()
            store_w.start()
            for m in range(1, N_DEV):
                copy(SEM_SMALL + m, sm_buf.at[0], sm_buf.at[0], sibling).wait_recv()
            tot = sm_buf[0]
            for d in range(1, N_DEV):
                tot = tot + sm_buf[d]
            osm_ref[...] = tot
            fin_o.wait_recv()
            store_o.start()
            fin.wait_recv()
            for cp in d2d + [d2d_o] + ici + ici_o + [fin, fin_o] + smalls:
                cp.wait_send()
            store_o.wait()
            store_w.wait()

    def shard_of(s, kr):
        return kr[0] ^ (3 - jnp.minimum(s, 3))

    def tok(s, t):
        return jnp.where(s < N_SHARD, t, nt - 1)

    def blk_map(s, t, kr):
        return (jnp.where(s < N_SHARD, 0, (s - N_SHARD) * nt + t), 0)

    hbm = pl.BlockSpec(memory_space=pl.ANY)
    grid_spec = pltpu.PrefetchScalarGridSpec(
        num_scalar_prefetch=1, grid=(n_steps, nt),
        in_specs=[pl.BlockSpec((TT, D_MODEL), lambda s, t, kr: (tok(s, t), 0)),
                  pl.BlockSpec((TT, SHARD_COLS), lambda s, t, kr: (tok(s, t), shard_of(s, kr))),
                  pl.BlockSpec((1, D_MODEL, SHARD_COLS), lambda s, t, kr: (shard_of(s, kr), 0, 0)),
                  pl.BlockSpec((N_SHARD, WO_ROWS, D_MODEL), lambda s, t, kr: (0, 0, 0)),
                  pl.BlockSpec((TX, D_MODEL), blk_map),
                  pl.BlockSpec((TX, D_MODEL), blk_map),
                  pl.BlockSpec((1, D_MODEL), lambda s, t, kr: (0, 0)),
                  pl.BlockSpec((8, D_MODEL), lambda s, t, kr: (0, 0)),
                  pl.BlockSpec((8, D_MODEL), lambda s, t, kr: (0, 0))],
        out_specs=(pl.BlockSpec((TX, D_MODEL), blk_map), hbm, hbm,
                   pl.BlockSpec((8, D_MODEL), lambda s, t, kr: (0, 0))),
        scratch_shapes=[pltpu.VMEM((D_MODEL, SHARD_COLS), F32), pltpu.VMEM((SEQ, D_MODEL), F32),
                        pltpu.VMEM((N_SHARD, hw, SHARD_COLS), BF16), pltpu.VMEM((3, hw, SHARD_COLS), BF16),
                        pltpu.VMEM((N_SHARD, hw, SHARD_COLS), BF16), pltpu.VMEM((3, hw, SHARD_COLS), BF16),
                        pltpu.VMEM((N_SHARD, ho, D_MODEL), BF16), pltpu.VMEM((N_SHARD, ho, D_MODEL), BF16),
                        pltpu.VMEM((3, ho, D_MODEL), BF16), pltpu.VMEM((WO_ROWS, D_MODEL), F32),
                        pltpu.VMEM((N_DEV, 8, D_MODEL), F32), pltpu.VMEM((1, D_MODEL), F32),
                        pltpu.SemaphoreType.DMA((N_SEM_TAIL,)), pltpu.SemaphoreType.DMA((N_SEM_TAIL,)),
                        pltpu.SemaphoreType.DMA((2,))])
    return pl.pallas_call(
        body, name="bwd_tail", grid_spec=grid_spec,
        out_shape=(jax.ShapeDtypeStruct((SEQ, D_MODEL), F32),
                   jax.ShapeDtypeStruct((D_MODEL, SHARD_COLS), F32),
                   jax.ShapeDtypeStruct((WO_ROWS, D_MODEL), F32),
                   jax.ShapeDtypeStruct((8, D_MODEL), F32)),
        compiler_params=pltpu.CompilerParams(dimension_semantics=("arbitrary", "arbitrary"),
                                             vmem_limit_bytes=60 * 1024 * 1024, collective_id=COLLECTIVE_TAIL),
    )(kidx, h, dproj, wg, gwo, x2d, dx2, g1, small_a, small_b)


def _adam_update(w, g, m, v):
    nm = ADAM_B1 * m + (1.0 - ADAM_B1) * g
    nv = ADAM_B2 * v + (1.0 - ADAM_B2) * (g * g)
    m_hat = nm / (1.0 - ADAM_B1 ** ADAM_STEP)
    v_hat = nv / (1.0 - ADAM_B2 ** ADAM_STEP)
    return -ADAM_LR * (m_hat / (jnp.sqrt(v_hat) + ADAM_EPS) + ADAM_WD * w), nm, nv


def _adamw_all(tot, g_w_in, g_w_out, big, small, grad_x):
    n = len(small)
    rows = WO_ROWS
    steps = D_MODEL // rows

    def body(tot_ref, *refs):
        gx_ref, gx_out = refs[2 + 3 * (2 + n)], refs[-1]
        gx_out[...] = gx_ref[...]
        ins, outs = refs[:2 + 3 * (2 + n)], refs[3 + 3 * (2 + n):-1]
        g_refs, wmv = ins[:2], ins[2:]
        loss_ref, quads = outs[0], outs[1:]

        def update(j, g):
            w_ref, m_ref, v_ref = wmv[3 * j:3 * j + 3]
            g_ref, d_ref, nm_ref, nv_ref = quads[4 * j:4 * j + 4]
            g_ref[...] = g
            d_ref[...], nm_ref[...], nv_ref[...] = _adam_update(w_ref[...], g, m_ref[...], v_ref[...])

        update(0, g_refs[0][...])

        @pl.when(pl.program_id(0) == 0)
        def _():
            update(1, g_refs[1][...])
            k = 2 * lax.axis_index("x") + lax.axis_index("y")
            mine = pl.ds(pl.multiple_of(k * HEAD, HEAD), HEAD)
            loss_ref[...] = tot_ref[7:8, 0:1]
            grads = [tot_ref[0:1, :], tot_ref[1:2, :], tot_ref[2:3, 0:D_HGRN], tot_ref[2:3, D_HGRN:],
                     jnp.concatenate([tot_ref[3:4, 0:D_HGRN], tot_ref[3:4, D_HGRN:]], axis=0),
                     jnp.concatenate([tot_ref[4 + tap:5 + tap, mine] for tap in range(3)], axis=1)]
            for j, g in enumerate(grads):
                update(2 + j, g)

    whole = lambda a: pl.BlockSpec(a.shape, lambda i: (0, 0))
    blk = pl.BlockSpec((rows, SHARD_COLS), lambda i: (i, 0))
    arrays = [a for triple in big + small for a in triple]
    in_specs = ([whole(tot), blk, whole(g_w_out)] + [blk] * 3 + [whole(a) for a in arrays[3:]])
    shapes = [big[0][0], big[1][0]] + [w for w, _, _ in small]
    out_shape = (jax.ShapeDtypeStruct((1, 1), F32),) + tuple(
        jax.ShapeDtypeStruct(w.shape, F32) for w in shapes for _ in range(4))
    out_specs = (pl.BlockSpec((1, 1), lambda i: (0, 0)),) + (blk,) * 4 + tuple(
        whole(w) for w in shapes[1:] for _ in range(4))
    gx_blk = pl.BlockSpec((SEQ // steps, D_MODEL), lambda i: (i, 0))
    outs = pl.pallas_call(
        body, name="adamw_all", grid=(steps,),
        out_shape=out_shape + (jax.ShapeDtypeStruct(grad_x.shape, F32),),
        in_specs=in_specs + [gx_blk], out_specs=out_specs + (gx_blk,),
        compiler_params=pltpu.CompilerParams(dimension_semantics=("arbitrary",), vmem_limit_bytes=VMEM_LIMIT),
    )(tot, g_w_in, g_w_out, *arrays, grad_x)
    return [outs[0]] + [outs[1 + 4 * j:5 + 4 * j] for j in range(2 + n)] + [outs[-1]]


def _local_step(x2d, tgt, proj, lb_logits, cw, ga, gcn, w_out, gf):
    g64 = _group_matrix(HEAD, CONV_GROUP)
    aux, states, dx2, dmixed, gwo, part_out = _mix_out(proj, lb_logits, cw, ga, gcn, g64, w_out, x2d, gf, tgt)
    dproj, part_mix = _mix_bwd(proj, aux, states, dmixed, lb_logits, cw, ga, gcn, g64)
    return dproj, dx2, gwo.reshape(N_SHARD, WO_ROWS, D_MODEL), part_out, part_mix


def kernel(x, norm_gain, w_in, lb_logits, conv_w, hgrn_norm_gain, conv_norm_gain, w_out, final_norm_gain, loss_target, m_norm_gain, m_w_in, m_lb_logits, m_conv_w, m_hgrn_norm_gain, m_conv_norm_gain, m_w_out, m_final_norm_gain, v_norm_gain, v_w_in, v_lb_logits, v_conv_w, v_hgrn_norm_gain, v_conv_norm_gain, v_w_out, v_final_norm_gain):
    k = 2 * lax.axis_index("x") + lax.axis_index("y")
    kidx = jnp.reshape(k, (1,)).astype(jnp.int32)
    row = lambda a: a.reshape(1, D_MODEL)
    taps = lambda a: a.reshape(1, 3 * HEAD)
    h, proj, wg, cw = _gather_proj(kidx, x[0], norm_gain, w_in, taps(conv_w))
    dproj, dx2, gwo, part_out, part_mix = _local_step(
        x[0], loss_target[0], proj, lb_logits, cw, hgrn_norm_gain, conv_norm_gain, w_out, row(final_norm_gain))
    rgrad_x, rg_w_in, rg_w_out, tot = _bwd_tail(kidx, h, dproj, wg, gwo, x[0], dx2, norm_gain, part_out, part_mix)

    (loss, (g_w_in, d_w_in, nm_w_in, nv_w_in), (g_w_out, d_w_out, nm_w_out, nv_w_out),
     (g_norm_gain, d_ng, nm_ng, nv_ng), (g_final, d_fg, nm_fg, nv_fg), (g_hgrn, d_hg, nm_hg, nv_hg),
     (g_convn, d_cg, nm_cg, nv_cg), (g_lb, d_lb, nm_lb, nv_lb), (g_conv_w, d_cw, nm_cw, nv_cw),
     grad_x) = _adamw_all(
        tot, rg_w_in, rg_w_out,
        [(w_in[0], m_w_in[0], v_w_in[0]), (w_out[0], m_w_out[0], v_w_out[0])],
        [(norm_gain, m_norm_gain, v_norm_gain),
         (row(final_norm_gain), row(m_final_norm_gain), row(v_final_norm_gain)),
         (hgrn_norm_gain, m_hgrn_norm_gain, v_hgrn_norm_gain),
         (conv_norm_gain, m_conv_norm_gain, v_conv_norm_gain),
         (lb_logits, m_lb_logits, v_lb_logits),
         (taps(conv_w), taps(m_conv_w), taps(v_conv_w))],
        rgrad_x)
    flat = lambda a: a.reshape(D_MODEL)
    untap = lambda a: a.reshape(1, 3, HEAD)
    return (loss.reshape(()), grad_x[None],
            g_norm_gain, g_w_in[None], g_lb, untap(g_conv_w), g_hgrn, g_convn, g_w_out[None], flat(g_final),
            d_ng, d_w_in[None], d_lb, untap(d_cw), d_hg, d_cg, d_w_out[None], flat(d_fg),
            nm_ng, nm_w_in[None], nm_lb, untap(nm_cw), nm_hg, nm_cg, nm_w_out[None], flat(nm_fg),
            nv_ng, nv_w_in[None], nv_lb, untap(nv_cw), nv_hg, nv_cg, nv_w_out[None], flat(nv_fg))
```

```python
import jax
import jax.numpy as jnp
import numpy as np
from jax import lax
from jax.experimental import pallas as pl
from jax.experimental.pallas import tpu as pltpu

F32 = jnp.float32
BF16 = jnp.bfloat16
MESH = pl.DeviceIdType.MESH

SEQ = 2048
D_MODEL = 1024
D_HGRN = 512
D_CONV = 512
HEAD = 128
N_HEADS = 4
CHUNK = 64
CONV_GROUP = 64
N_SHARD = 4
SHARD_COLS = 1024
WO_ROWS = 256
EPS = 1e-6
TB = 256
NCB = TB // CHUNK
N_CHUNKS = SEQ // CHUNK
N_DEV = 8
COLLECTIVE_GATHER, COLLECTIVE_MIX_OUT, COLLECTIVE_TAIL = 1, 0, 2
AUX_O, AUX_CV, AUX_B, AUX_COLS = 0, 512, 1024, 1536

ADAM_LR = 0.001
ADAM_B1 = 0.9
ADAM_B2 = 0.999
ADAM_EPS = 1e-08
ADAM_WD = 0.01
ADAM_STEP = 10

VMEM_LIMIT = 56 * 1024 * 1024


def _dot(a, b):
    return jnp.dot(a, b, preferred_element_type=F32)


def _dot_nt(a, b):
    return lax.dot_general(a, b, (((1,), (1,)), ((), ())), preferred_element_type=F32)


def _dot_tn(a, b):
    return lax.dot_general(a, b, (((0,), (0,)), ((), ())), preferred_element_type=F32)


def _split_bf16(x, n):
    parts = []
    r = x
    for _ in range(n):
        p = r.astype(BF16)
        parts.append(p)
        r = r - p.astype(F32)
    return parts


def _exact_left(m, x, n=3):
    acc = None
    for p in _split_bf16(x, n):
        t = _dot(m, p)
        acc = t if acc is None else acc + t
    return acc


def _exact_left_many(m, xs, n=3):
    parts = [_split_bf16(x, n) for x in xs]
    accs = [None] * len(xs)
    for i in range(n):
        for j in range(len(xs)):
            t = _dot(m, parts[j][i])
            accs[j] = t if accs[j] is None else accs[j] + t
    return accs


def _group_mean_many(xs, gmat, n=2):
    parts = [_split_bf16(x, n) for x in xs]
    accs = [None] * len(xs)
    for i in range(n):
        for j in range(len(xs)):
            t = _dot(parts[j][i], gmat)
            accs[j] = t if accs[j] is None else accs[j] + t
    return accs


def _group_mean(x, gmat, n=2):
    w = gmat.shape[0]
    outs = []
    for c0 in range(0, x.shape[1], w):
        acc = None
        for p in _split_bf16(x[:, c0:c0 + w], n):
            t = _dot(p, gmat)
            acc = t if acc is None else acc + t
        outs.append(acc)
    return jnp.concatenate(outs, axis=1)


def _sigmoid(x):
    return 1.0 / (1.0 + jnp.exp(-x))


def _lower_bound(lbl):
    l0 = lbl[0:1, :]
    l1 = lbl[1:2, :]
    m = jnp.maximum(l0, l1)
    e0 = jnp.exp(l0 - m)
    e1 = jnp.exp(l1 - m)
    return e0 / (e0 + e1)


def _tri(lower):
    r = lax.broadcasted_iota(jnp.int32, (CHUNK, CHUNK), 0)
    c = lax.broadcasted_iota(jnp.int32, (CHUNK, CHUNK), 1)
    return jnp.where((c <= r) if lower else (c >= r), 1.0, 0.0).astype(BF16)


def _causal():
    r = lax.broadcasted_iota(jnp.int32, (CHUNK, CHUNK), 0)
    c = lax.broadcasted_iota(jnp.int32, (CHUNK, CHUNK), 1)
    return c <= r


def _shift_down(x, sh, prev_tail):
    r = pltpu.roll(x, sh, 0)
    pt = pltpu.roll(prev_tail, sh, 0)
    rows = lax.broadcasted_iota(jnp.int32, prev_tail.shape, 0)
    top = jnp.where(rows < sh, pt, r[0:8])
    return jnp.concatenate([top, r[8:]], axis=0)


def _shift_up(x, sh, next_head):
    n = x.shape[0]
    r = pltpu.roll(x, n - sh, 0)
    nh = pltpu.roll(next_head, 8 - sh, 0)
    rows = lax.broadcasted_iota(jnp.int32, next_head.shape, 0)
    bot = jnp.where(rows >= 8 - sh, nh, r[n - 8:])
    return jnp.concatenate([r[:n - 8], bot], axis=0)


def _group_matrix(width, group):
    r = np.arange(width)[:, None] // group
    c = np.arange(width)[None, :] // group
    return jnp.asarray(np.where(r == c, 1.0 / group, 0.0), dtype=BF16)


TG = 1024
SEM_W, SEM_CW, SEM_W_FWD, N_SEM = 0, 4, 7, 11


def _gather_proj(kidx, x2d, g1, w_in, conv_w):
    half_w = D_MODEL // 2
    half_c = SHARD_COLS // 2
    nt = SEQ // TG
    n_steps = 2 * N_SHARD

    def body(k_ref, x_ref, g_ref, w_ref, cw_ref, h_ref, p_ref, wg_out, cwg_out,
             wg_v, cwg_v, send_sems, recv_sems, out_sems):
        s, t = pl.program_id(0), pl.program_id(1)
        x, y, c = lax.axis_index("x"), lax.axis_index("y"), lax.axis_index("c")
        k = 2 * x + y
        sibling = (x, y, 1 - c)
        chips = [(1 - x, y), (x, 1 - y), (1 - x, 1 - y)]
        kjs = [2 * cx + cy for cx, cy in chips]
        diag = (*chips[2], c)

        def w_half(kk, cc):
            return wg_v.at[kk, pl.ds(cc * half_w, half_w), :]

        def w_quarter(kk, cc, piece):
            return wg_v.at[kk, pl.ds(cc * half_w, half_w), piece * half_c:(piece + 1) * half_c]

        def cw_of(kk):
            return cwg_v.at[:, pl.ds(pl.multiple_of(kk * HEAD, HEAD), HEAD)]

        def copy(sem, ref, to):
            return pltpu.make_async_remote_copy(
                src_ref=ref, dst_ref=ref, send_sem=send_sems.at[sem], recv_sem=recv_sems.at[sem],
                device_id=to, device_id_type=MESH)

        def at_step(sv, tv):
            return pl.when((s == sv) & (t == tv))

        w_direct = ([copy(SEM_W + j, w_half(k, c), (*chips[j], c)) for j in range(2)]
                    + [copy(SEM_W + 2 + p, w_quarter(k, c, p), diag) for p in range(2)])
        cw_direct = [copy(SEM_CW + j, cw_of(k), (*chip, c)) for j, chip in enumerate(chips)]
        w_passed = ([copy(SEM_W_FWD + j, w_half(kjs[j], c), sibling) for j in range(2)]
                    + [copy(SEM_W_FWD + 2 + p, w_quarter(kjs[2], c, p), sibling) for p in range(2)])
        stores = ([pltpu.make_async_copy(wg_v.at[kk], wg_out.at[kk], out_sems.at[i])
                   for i, kk in enumerate([k] + kjs)]
                  + [pltpu.make_async_copy(cwg_v, cwg_out, out_sems.at[4])])

        @at_step(0, 0)
        def _():
            barrier = pltpu.get_barrier_semaphore()
            for peer in [sibling] + [(*chip, c) for chip in chips]:
                pl.semaphore_signal(barrier, inc=1, device_id=peer, device_id_type=MESH)
            wg_v[k] = w_ref[0].astype(BF16)
            mine = pl.ds(pl.multiple_of(k * HEAD, HEAD), HEAD)
            cwg_v[:, mine] = jnp.zeros((8, HEAD), F32)
            for tap in range(3):
                cwg_v[tap:tap + 1, mine] = cw_ref[:, tap * HEAD:(tap + 1) * HEAD]
            pl.semaphore_wait(barrier, 4)
            w_direct[0].start()
            w_direct[1].start()
            for cp in cw_direct:
                cp.start()
            stores[0].start()

        @at_step(2, 0)
        def _():
            for j in range(2):
                copy(SEM_W + j, w_half(kjs[j], c), sibling).wait_recv()
                w_passed[j].start()
            w_direct[2].start()
            w_direct[3].start()
            copy(SEM_W_FWD, w_half(kjs[0], 1 - c), sibling).wait_recv()
            stores[1].start()

        @at_step(4, 0)
        def _():
            copy(SEM_W_FWD + 1, w_half(kjs[1], 1 - c), sibling).wait_recv()
            stores[2].start()

        for p in range(2):
            @at_step(6 + p, 0)
            def _(p=p):
                copy(SEM_W + 2 + p, w_quarter(kjs[2], c, p), sibling).wait_recv()
                w_passed[2 + p].start()
                copy(SEM_W_FWD + 2 + p, w_quarter(kjs[2], 1 - c, p), sibling).wait_recv()

        rows = pl.ds(pl.multiple_of(t * TG, TG), TG)

        @pl.when(s == 0)
        def _():
            xv = x_ref[...]
            r = lax.rsqrt(jnp.mean(xv * xv, axis=-1, keepdims=True) + EPS)
            h_ref[rows, :] = (xv * r * g_ref[...]).astype(BF16)

        sh = s >> 1
        js = k ^ (((sh & 1) << 1) | (sh >> 1))
        for piece in range(2):
            @pl.when((s & 1) == piece)
            def _(piece=piece):
                p_ref[...] = _dot(h_ref[rows, :], wg_v[js, :, piece * half_c:(piece + 1) * half_c])

        @at_step(n_steps - 1, nt - 1)
        def _():
            stores[3].start()
            for j in range(3):
                copy(SEM_CW + j, cw_of(kjs[j]), sibling).wait_recv()
            stores[4].start()
            for cp in w_direct + cw_direct + w_passed:
                cp.wait_send()
            for st in stores:
                st.wait()

    def x_map(s, t, kr):
        return (jnp.where(s == 0, t, nt - 1), 0)

    def p_map(s, t, kr):
        sh = s >> 1
        return (t, 2 * (kr[0] ^ (((sh & 1) << 1) | (sh >> 1))) + (s & 1))

    hbm = pl.BlockSpec(memory_space=pl.ANY)
    grid_spec = pltpu.PrefetchScalarGridSpec(
        num_scalar_prefetch=1, grid=(n_steps, nt),
        in_specs=[pl.BlockSpec((TG, D_MODEL), x_map),
                  pl.BlockSpec((1, D_MODEL), lambda s, t, kr: (0, 0)),
                  pl.BlockSpec((1, D_MODEL, SHARD_COLS), lambda s, t, kr: (0, 0, 0)),
                  pl.BlockSpec((1, 3 * HEAD), lambda s, t, kr: (0, 0))],
        out_specs=(pl.BlockSpec((SEQ, D_MODEL), lambda s, t, kr: (0, 0)),
                   pl.BlockSpec((TG, half_c), p_map), hbm, hbm),
        scratch_shapes=[pltpu.VMEM((N_SHARD, D_MODEL, SHARD_COLS), BF16),
                        pltpu.VMEM((8, D_CONV), F32),
                        pltpu.SemaphoreType.DMA((N_SEM,)), pltpu.SemaphoreType.DMA((N_SEM,)),
                        pltpu.SemaphoreType.DMA((5,))])
    return pl.pallas_call(
        body, name="gather_proj", grid_spec=grid_spec,
        out_shape=(jax.ShapeDtypeStruct((SEQ, D_MODEL), BF16),
                   jax.ShapeDtypeStruct((SEQ, N_SHARD * SHARD_COLS), F32),
                   jax.ShapeDtypeStruct((N_SHARD, D_MODEL, SHARD_COLS), BF16),
                   jax.ShapeDtypeStruct((8, D_CONV), F32)),
        compiler_params=pltpu.CompilerParams(dimension_semantics=("arbitrary", "arbitrary"),
                                             vmem_limit_bytes=VMEM_LIMIT, collective_id=COLLECTIVE_GATHER),
    )(kidx, x2d, g1, w_in, conv_w)


LAG = 6


def _mix_out(proj, lb_logits, cw, ga, gcn, g64, w_out, x2d, gf, tgt):
    half_o = WO_ROWS // 2
    nblk = SEQ // TB
    n_steps = nblk + LAG

    def body(p_ref, lbl_ref, cw_ref, ga_ref, gcn_ref, g64_ref, wo_ref, x_ref, gf_ref, t_ref,
             aux_ref, sto_ref, dx2_ref, dm_ref, gwo_ref, part_ref,
             st_ref, tail_ref, wog_v, stage, ring, acc_ref, send_sems, recv_sems):
        i = pl.program_id(0)
        x, y, c = lax.axis_index("x"), lax.axis_index("y"), lax.axis_index("c")
        k = 2 * x + y
        sibling = (x, y, 1 - c)
        chips = [(1 - x, y), (x, 1 - y), (1 - x, 1 - y)]
        kjs = [2 * cx + cy for cx, cy in chips]

        def wo_half(kk, cc):
            return wog_v.at[pl.ds(pl.multiple_of(kk * WO_ROWS + cc * half_o, half_o), half_o), :]

        def copy(sem, ref, to):
            return pltpu.make_async_remote_copy(
                src_ref=ref, dst_ref=ref, send_sem=send_sems.at[sem], recv_sem=recv_sems.at[sem],
                device_id=to, device_id_type=MESH)

        wo_direct = [copy(j, wo_half(k, c), (*chip, c)) for j, chip in enumerate(chips)]
        wo_passed = [copy(3 + j, wo_half(kj, c), sibling) for j, kj in enumerate(kjs)]

        @pl.when(i == 0)
        def _():
            barrier = pltpu.get_barrier_semaphore()
            for peer in [sibling] + [(*chip, c) for chip in chips]:
                pl.semaphore_signal(barrier, inc=1, device_id=peer, device_id_type=MESH)
            st_ref[...] = jnp.zeros_like(st_ref)
            tail_ref[...] = jnp.zeros_like(tail_ref)
            acc_ref[...] = jnp.zeros_like(acc_ref)
            part_ref[...] = jnp.zeros_like(part_ref)
            wog_v[pl.ds(pl.multiple_of(k * WO_ROWS, WO_ROWS), WO_ROWS), :] = wo_ref[0].astype(BF16)
            pl.semaphore_wait(barrier, 4)
            for cp in wo_direct:
                cp.start()

        @pl.when(i == LAG - 1)
        def _():
            for j in range(3):
                copy(j, wo_half(kjs[j], c), sibling).wait_recv()
                wo_passed[j].start()

        @pl.when(i == LAG)
        def _():
            for j in range(3):
                copy(3 + j, wo_half(kjs[j], 1 - c), sibling).wait_recv()

        lb = _lower_bound(lbl_ref[...])
        tri = _tri(True)
        causal = _causal()
        g64m = g64_ref[...]
        heads = range(N_HEADS)
        cs = [slice(hd * HEAD, (hd + 1) * HEAD) for hd in heads]
        col = lambda base, hd: slice(base + hd * HEAD, base + (hd + 1) * HEAD)

        def mix_chunk(n):
            sl = pl.ds(n * CHUNK, CHUNK)
            sg = [_sigmoid(p_ref[sl, col(512, hd)]) for hd in heads]
            f = [lb[:, cs[hd]] + (1.0 - lb[:, cs[hd]]) * sg[hd] for hd in heads]
            bc = _exact_left_many(tri, [jnp.log(f[hd]) for hd in heads])
            for hd in heads:
                aux_ref[sl, col(AUX_B, hd)] = bc[hd]
            g = [bc[hd][CHUNK - 1:CHUNK, :] for hd in heads]
            qd = [(p_ref[sl, col(0, hd)] * jnp.exp(bc[hd])).astype(BF16) for hd in heads]
            kk = [1.0 - f[hd] for hd in heads]
            ki = [(kk[hd] * jnp.exp(-bc[hd])).astype(BF16) for hd in heads]
            ke = [(kk[hd] * jnp.exp(g[hd] - bc[hd])).astype(BF16) for hd in heads]
            vb = [p_ref[sl, col(1024, hd)].astype(BF16) for hd in heads]
            st = [st_ref[hd] for hd in heads]
            st_b = [a.astype(BF16) for a in st]
            for hd in heads:
                sto_ref[n, hd] = st_b[hd]
            scm = [_dot_nt(qd[hd], ki[hd]) for hd in heads]
            inter = [_dot_nt(qd[hd], st_b[hd]) for hd in heads]
            upd = [_dot_tn(vb[hd], ke[hd]) for hd in heads]
            intra = [_dot(jnp.where(causal, scm[hd], 0.0).astype(BF16), vb[hd]) for hd in heads]
            for hd in heads:
                st_ref[hd] = st[hd] * jnp.exp(g[hd]) + upd[hd]
                o = intra[hd] + inter[hd]
                aux_ref[sl, col(AUX_O, hd)] = o
                ra = lax.rsqrt(jnp.mean(o * o, axis=-1, keepdims=True) + EPS)
                za = p_ref[sl, col(1536, hd)]
                stage[sl, cs[hd]] = (o * ra * ga_ref[:, cs[hd]] * (za * _sigmoid(za))).astype(BF16)
            yb = []
            for hd in heads:
                cu = p_ref[sl, col(3072, hd)] * p_ref[sl, col(2048, hd)]
                tail = tail_ref[:, cs[hd]]
                cv = (cw_ref[0:1, cs[hd]] * _shift_down(cu, 2, tail) + cw_ref[1:2, cs[hd]] * _shift_down(cu, 1, tail)
                      + cw_ref[2:3, cs[hd]] * cu)
                tail_ref[:, cs[hd]] = cu[CHUNK - 8:, :]
                aux_ref[sl, col(AUX_CV, hd)] = cv
                yb.append(p_ref[sl, col(2560, hd)] * cv)
            ms = _group_mean_many([y * y for y in yb], g64m)
            for hd in heads:
                rb = lax.rsqrt(ms[hd] + EPS)
                zb = p_ref[sl, col(3584, hd)]
                stage[sl, col(512, hd)] = (yb[hd] * rb * gcn_ref[:, cs[hd]] * (zb * _sigmoid(zb))).astype(BF16)

        def step(mix, project):
            if project:
                mixed_b = ring[pl.ds(pl.multiple_of((i - LAG) * TB, TB), TB), :]
                y = _dot(mixed_b, wog_v[...])
            if mix:
                mix_chunk(0)
            if project:
                x2 = x_ref[...] + y
                r2 = lax.rsqrt(jnp.mean(x2 * x2, axis=-1, keepdims=True) + EPS)
                n2 = x2 * r2
                gfv = gf_ref[...]
                err = n2 * gfv - t_ref[...]
                loss = 0.5 * jnp.sum(jnp.mean(err * err, axis=-1, keepdims=True), axis=0, keepdims=True)
                dy = err * (1.0 / D_MODEL)
                part_ref[1:2, :] += jnp.sum(dy * n2, axis=0, keepdims=True)
                part_ref[7:8, :] += jnp.broadcast_to(loss, (1, D_MODEL))
                dn = dy * gfv
                dx2 = r2 * (dn - n2 * jnp.mean(dn * n2, axis=-1, keepdims=True))
                dx2_ref[...] = dx2
                dx2_b = dx2.astype(BF16)
            if mix:
                mix_chunk(1)
            if project:
                dm_ref[...] = _dot_nt(dx2_b, wog_v[...])
            if mix:
                mix_chunk(2)
            if project:
                acc_ref[...] += _dot_tn(mixed_b, dx2_b)
            if mix:
                mix_chunk(3)
                ring[pl.ds(pl.multiple_of(i * TB, TB), TB), :] = stage[...]

        @pl.when(i < LAG)
        def _():
            step(True, False)

        @pl.when((i >= LAG) & (i < nblk))
        def _():
            step(True, True)

        @pl.when(i >= nblk)
        def _():
            step(False, True)

        @pl.when(i == n_steps - 1)
        def _():
            gwo_ref[...] = acc_ref[...].astype(BF16)
            for cp in wo_direct + wo_passed:
                cp.wait_send()

    assert NCB == 4
    row = lambda w: pl.BlockSpec((1, w), lambda i: (0, 0))
    mix_blk = lambda i: jnp.minimum(i, nblk - 1)
    out_blk = lambda i: jnp.clip(i - LAG, 0, nblk - 1)
    tok = lambda: pl.BlockSpec((TB, D_MODEL), lambda i: (out_blk(i), 0))
    return pl.pallas_call(
        body, name="mix_out", grid=(n_steps,),
        out_shape=(jax.ShapeDtypeStruct((SEQ, AUX_COLS), F32),
                   jax.ShapeDtypeStruct((N_CHUNKS, N_HEADS, HEAD, HEAD), BF16),
                   jax.ShapeDtypeStruct((SEQ, D_MODEL), F32),
                   jax.ShapeDtypeStruct((SEQ, D_MODEL), F32),
                   jax.ShapeDtypeStruct((D_MODEL, D_MODEL), BF16),
                   jax.ShapeDtypeStruct((8, D_MODEL), F32)),
        in_specs=[pl.BlockSpec((TB, 4096), lambda i: (jnp.minimum(i, nblk - 1), 0)),
                  pl.BlockSpec((2, D_HGRN), lambda i: (0, 0)),
                  pl.BlockSpec((8, D_CONV), lambda i: (0, 0)),
                  row(D_HGRN), row(D_CONV),
                  pl.BlockSpec((HEAD, HEAD), lambda i: (0, 0)),
                  pl.BlockSpec((1, WO_ROWS, D_MODEL), lambda i: (0, 0, 0)),
                  tok(), row(D_MODEL), tok()],
        out_specs=(pl.BlockSpec((TB, AUX_COLS), lambda i: (mix_blk(i), 0)),
                   pl.BlockSpec((NCB, N_HEADS, HEAD, HEAD), lambda i: (mix_blk(i), 0, 0, 0)),
                   tok(), tok(),
                   pl.BlockSpec((D_MODEL, D_MODEL), lambda i: (0, 0)),
                   pl.BlockSpec((8, D_MODEL), lambda i: (0, 0))),
        scratch_shapes=[pltpu.VMEM((N_HEADS, HEAD, HEAD), F32), pltpu.VMEM((8, D_CONV), F32),
                        pltpu.VMEM((D_MODEL, D_MODEL), BF16), pltpu.VMEM((TB, D_MODEL), BF16),
                        pltpu.VMEM((SEQ, D_MODEL), BF16), pltpu.VMEM((D_MODEL, D_MODEL), F32),
                        pltpu.SemaphoreType.DMA((6,)), pltpu.SemaphoreType.DMA((6,))],
        compiler_params=pltpu.CompilerParams(dimension_semantics=("arbitrary",), vmem_limit_bytes=VMEM_LIMIT,
                                             collective_id=COLLECTIVE_MIX_OUT),
    )(proj, lb_logits, cw, ga, gcn, g64, w_out, x2d, gf, tgt)


def _mix_bwd(proj, aux, states, dmixed, lb_logits, cw, ga, gcn, g64):
    nblk = SEQ // TB

    def body(p_ref, aux_ref, st_ref, dm_ref, lbl_ref, cw_ref, ga_ref, gcn_ref, g64_ref,
             dp_ref, part_ref, dst_ref, head_ref, dlb_ref):
        i = pl.program_id(0)

        @pl.when(i == 0)
        def _():
            dst_ref[...] = jnp.zeros_like(dst_ref)
            head_ref[...] = jnp.zeros_like(head_ref)
            part_ref[...] = jnp.zeros_like(part_ref)
            dlb_ref[...] = jnp.zeros_like(dlb_ref)

        lb = _lower_bound(lbl_ref[...])
        triu = _tri(False)
        causal = _causal()
        g64m = g64_ref[...]
        rowsum = lambda a: jnp.sum(a, axis=0, keepdims=True)
        heads = range(N_HEADS)
        cs = [slice(hd * HEAD, (hd + 1) * HEAD) for hd in heads]
        col = lambda base, hd: slice(base + hd * HEAD, base + (hd + 1) * HEAD)
        for n in reversed(range(NCB)):
            sl = pl.ds(n * CHUNK, CHUNK)
            cvv = [aux_ref[sl, col(AUX_CV, hd)] for hd in heads]
            gb = [p_ref[sl, col(2560, hd)] for hd in heads]
            yb = [gb[hd] * cvv[hd] for hd in heads]
            ms = _group_mean_many([y * y for y in yb], g64m)
            rb, nb, dnb = [], [], []
            for hd in heads:
                rb.append(lax.rsqrt(ms[hd] + EPS))
                nb.append(yb[hd] * rb[hd])
                zb = p_ref[sl, col(3584, hd)]
                sgb = _sigmoid(zb)
                dmb = dm_ref[sl, col(512, hd)]
                silu = zb * sgb
                dgate = dmb * gcn_ref[:, cs[hd]]
                part_ref[2:3, col(512, hd)] += rowsum(dmb * nb[hd] * silu)
                dp_ref[sl, col(3584, hd)] = (dgate * nb[hd] * (sgb + silu * (1.0 - sgb))).astype(BF16)
                dnb.append(dgate * silu)
            mdn = _group_mean_many([dnb[hd] * nb[hd] for hd in heads], g64m)
            for hd in heads:
                dyb = rb[hd] * (dnb[hd] - nb[hd] * mdn[hd])
                dp_ref[sl, col(2560, hd)] = (dyb * cvv[hd]).astype(BF16)
                dcv = dyb * gb[hd]
                head = head_ref[:, cs[hd]]
                dcv1 = _shift_up(dcv, 1, head)
                dcv2 = _shift_up(dcv, 2, head)
                head_ref[:, cs[hd]] = dcv[0:8, :]
                u = p_ref[sl, col(2048, hd)]
                gc = p_ref[sl, col(3072, hd)]
                cu = gc * u
                part_ref[4:5, cs[hd]] += rowsum(dcv2 * cu)
                part_ref[5:6, cs[hd]] += rowsum(dcv1 * cu)
                part_ref[6:7, cs[hd]] += rowsum(dcv * cu)
                dcu = cw_ref[2:3, cs[hd]] * dcv + cw_ref[1:2, cs[hd]] * dcv1 + cw_ref[0:1, cs[hd]] * dcv2
                dp_ref[sl, col(3072, hd)] = (dcu * u).astype(BF16)
                dp_ref[sl, col(2048, hd)] = (dcu * gc).astype(BF16)
            do_b = []
            for hd in heads:
                ov = aux_ref[sl, col(AUX_O, hd)]
                ra = lax.rsqrt(jnp.mean(ov * ov, axis=-1, keepdims=True) + EPS)
                na = ov * ra
                za = p_ref[sl, col(1536, hd)]
                sga = _sigmoid(za)
                dma = dm_ref[sl, cs[hd]]
                silu = za * sga
                dgate = dma * ga_ref[:, cs[hd]]
                part_ref[2:3, cs[hd]] += rowsum(dma * na * silu)
                dp_ref[sl, col(1536, hd)] = (dgate * na * (sga + silu * (1.0 - sga))).astype(BF16)
                dna = dgate * silu
                do_b.append((ra * (dna - na * jnp.mean(dna * na, axis=-1, keepdims=True))).astype(BF16))
            s = [_sigmoid(p_ref[sl, col(512, hd)]) for hd in heads]
            f = [lb[:, cs[hd]] + (1.0 - lb[:, cs[hd]]) * s[hd] for hd in heads]
            bc = [aux_ref[sl, col(AUX_B, hd)] for hd in heads]
            g = [bc[hd][CHUNK - 1:CHUNK, :] for hd in heads]
            eb = [jnp.exp(bc[hd]) for hd in heads]
            enb = [jnp.exp(-bc[hd]) for hd in heads]
            eg = [jnp.exp(g[hd] - bc[hd]) for hd in heads]
            dec = [jnp.exp(g[hd]) for hd in heads]
            qd = [p_ref[sl, cs[hd]] * eb[hd] for hd in heads]
            kk = [1.0 - f[hd] for hd in heads]
            ki = [kk[hd] * enb[hd] for hd in heads]
            ke = [kk[hd] * eg[hd] for hd in heads]
            qd_b = [a.astype(BF16) for a in qd]
            ki_b = [a.astype(BF16) for a in ki]
            ke_b = [a.astype(BF16) for a in ke]
            vb = [p_ref[sl, col(1024, hd)].astype(BF16) for hd in heads]
            st_b = [st_ref[n, hd] for hd in heads]
            dst = [dst_ref[hd] for hd in heads]
            dst_b = [a.astype(BF16) for a in dst]
            scm = [_dot_nt(qd_b[hd], ki_b[hd]) for hd in heads]
            amm = [_dot_nt(do_b[hd], vb[hd]) for hd in heads]
            dqd2 = [_dot(do_b[hd], st_b[hd]) for hd in heads]
            dke = [_dot(vb[hd], dst_b[hd]) for hd in heads]
            dv2 = [_dot_nt(ke_b[hd], dst_b[hd]) for hd in heads]
            dsu = [_dot_tn(do_b[hd], qd_b[hd]) for hd in heads]
            sc = [jnp.where(causal, scm[hd], 0.0).astype(BF16) for hd in heads]
            am = [jnp.where(causal, amm[hd], 0.0).astype(BF16) for hd in heads]
            dqd1 = [_dot(am[hd], ki_b[hd]) for hd in heads]
            dki = [_dot_tn(am[hd], qd_b[hd]) for hd in heads]
            dv1 = [_dot_tn(sc[hd], do_b[hd]) for hd in heads]
            db, dgv, dkk = [], [], []
            for hd in heads:
                dqd = dqd1[hd] + dqd2[hd]
                ddec = rowsum(dst[hd] * st_b[hd].astype(F32))
                dst_ref[hd] = dst[hd] * dec[hd] + dsu[hd]
                dp_ref[sl, cs[hd]] = (dqd * eb[hd]).astype(BF16)
                dp_ref[sl, col(1024, hd)] = (dv1[hd] + dv2[hd]).astype(BF16)
                dke_eg = dke[hd] * eg[hd]
                dkk.append(dki[hd] * enb[hd] + dke_eg)
                db.append(dqd * qd[hd] - kk[hd] * dkk[hd])
                dgv.append(rowsum(kk[hd] * dke_eg) + ddec * dec[hd])
            rc = _exact_left_many(triu, db, 2)
            for hd in heads:
                df = (rc[hd] + dgv[hd]) / f[hd] - dkk[hd]
                one_s = 1.0 - s[hd]
                dlb_ref[:, cs[hd]] += rowsum(df * one_s)
                dp_ref[sl, col(512, hd)] = (df * (1.0 - lb[:, cs[hd]]) * s[hd] * one_s).astype(BF16)

        @pl.when(i == nblk - 1)
        def _():
            row = dlb_ref[...] * lb * (1.0 - lb)
            part_ref[3:4, 0:D_HGRN] = row
            part_ref[3:4, D_HGRN:] = -row

    rev = lambda w: pl.BlockSpec((TB, w), lambda i: (nblk - 1 - i, 0))
    row = lambda w: pl.BlockSpec((1, w), lambda i: (0, 0))
    return pl.pallas_call(
        body, name="mix_bwd", grid=(nblk,),
        out_shape=(jax.ShapeDtypeStruct((SEQ, 4096), BF16),
                   jax.ShapeDtypeStruct((8, D_MODEL), F32)),
        in_specs=[rev(4096), rev(AUX_COLS),
                  pl.BlockSpec((NCB, N_HEADS, HEAD, HEAD), lambda i: (nblk - 1 - i, 0, 0, 0)),
                  rev(D_MODEL),
                  pl.BlockSpec((2, D_HGRN), lambda i: (0, 0)),
                  pl.BlockSpec((8, D_CONV), lambda i: (0, 0)),
                  row(D_HGRN), row(D_CONV),
                  pl.BlockSpec((HEAD, HEAD), lambda i: (0, 0))],
        out_specs=(rev(4096), pl.BlockSpec((8, D_MODEL), lambda i: (0, 0))),
        scratch_shapes=[pltpu.VMEM((N_HEADS, HEAD, HEAD), F32), pltpu.VMEM((8, D_CONV), F32),
                        pltpu.VMEM((1, D_HGRN), F32)],
        compiler_params=pltpu.CompilerParams(dimension_semantics=("arbitrary",), vmem_limit_bytes=VMEM_LIMIT),
    )(proj, aux, states, dmixed, lb_logits, cw, ga, gcn, g64)


TT = 1024
TX = 256
(SEM_D2D, SEM_D2D_O, SEM_ICI, SEM_ICI_O, SEM_FIN, SEM_FIN_O, SEM_SMALL, N_SEM_TAIL) = 0, 4, 5, 8, 11, 12, 12, 20


def _bwd_tail(kidx, h, dproj, wg, gwo, x2d, dx2, g1, small_a, small_b):
    hw = D_MODEL // 2
    ho = WO_ROWS // 2
    nt = SEQ // TT
    n_steps = N_SHARD + SEQ // TX // nt

    def body(k_ref, h_ref, dp_ref, w_ref, gwo_ref, x_ref, dx2_ref, g_ref, sm_ref, smb_ref,
             gx_ref, gw_out, gwo_out, osm_ref,
             acc, dh, sendbuf, keep, sibrcv, rcv, sib_o, p_o, rcv_o, res_o, sm_buf, dng,
             send_sems, recv_sems, out_sems):
        s, t = pl.program_id(0), pl.program_id(1)
        x, y, c = lax.axis_index("x"), lax.axis_index("y"), lax.axis_index("c")
        k = 2 * x + y
        me = 4 * x + 2 * y + c
        sibling = (x, y, 1 - c)
        chips = [(1 - x, 1 - y), (1 - x, y), (x, 1 - y)]
        kjs = [2 * cx + cy for cx, cy in chips]
        mine = pl.ds(pl.multiple_of(c * hw, hw), hw)
        other = pl.ds(pl.multiple_of((1 - c) * hw, hw), hw)
        mine_o = pl.ds(pl.multiple_of(c * ho, ho), ho)
        other_o = pl.ds(pl.multiple_of((1 - c) * ho, ho), ho)

        def copy(sem, src, dst, to):
            return pltpu.make_async_remote_copy(
                src_ref=src, dst_ref=dst, send_sem=send_sems.at[sem], recv_sem=recv_sems.at[sem],
                device_id=to, device_id_type=MESH)

        def at_step(sv, tv):
            return pl.when((s == sv) & (t == tv))

        def at_norm_block(b):
            return at_step(N_SHARD + b // nt, b % nt)

        d2d = [copy(SEM_D2D + sv, sendbuf.at[sv], sibrcv.at[sv], sibling) for sv in range(N_SHARD)]
        d2d_o = copy(SEM_D2D_O, gwo_ref.at[:, other_o, :], sib_o, sibling)
        ici = [copy(SEM_ICI + sv, keep.at[sv], rcv.at[sv], (*chips[sv], c)) for sv in range(3)]
        ici_o = [copy(SEM_ICI_O + sv, p_o.at[kjs[sv]], rcv_o.at[sv], (*chips[sv], c)) for sv in range(3)]
        fin = copy(SEM_FIN, acc.at[mine, :], gw_out.at[mine, :], sibling)
        fin_o = copy(SEM_FIN_O, res_o.at[mine_o, :], res_o.at[mine_o, :], sibling)
        smalls = [copy(SEM_SMALL + m, sm_buf.at[me], sm_buf.at[me],
                       (x ^ (m >> 2), y ^ ((m >> 1) & 1), c ^ (m & 1))) for m in range(1, N_DEV)]
        store_w = pltpu.make_async_copy(acc.at[mine, :], gw_out.at[mine, :], out_sems.at[0])
        store_o = pltpu.make_async_copy(res_o, gwo_out, out_sems.at[1])

        @at_step(0, 0)
        def _():
            barrier = pltpu.get_barrier_semaphore()
            for m in range(1, N_DEV):
                pl.semaphore_signal(barrier, inc=1, device_id=(x ^ (m >> 2), y ^ ((m >> 1) & 1), c ^ (m & 1)),
                                    device_id_type=MESH)
            pl.semaphore_wait(barrier, N_DEV - 1)
            d2d_o.start()

        @at_step(0, 1)
        def _():
            d2d_o.wait_recv()
            for j in range(N_SHARD):
                p_o[j] = (gwo_ref[j, mine_o, :].astype(F32) + sib_o[j].astype(F32)).astype(BF16)
            res_o[mine_o, :] = gwo_ref[k, mine_o, :].astype(F32) + sib_o[k].astype(F32)
            for cp in ici_o:
                cp.start()

        rows = pl.ds(pl.multiple_of(t * TT, TT), TT)

        @pl.when(s < N_SHARD)
        def _():
            part = _dot_tn(h_ref[...], dp_ref[...])

            @pl.when(t == 0)
            def _():
                acc[...] = part

            @pl.when(t > 0)
            def _():
                acc[...] += part

        for sv in range(N_SHARD):
            @at_step(sv, nt - 1)
            def _(sv=sv):
                sendbuf[sv] = acc[other, :].astype(BF16)
                if sv < 3:
                    keep[sv] = acc[mine, :].astype(BF16)
                d2d[sv].start()

        @pl.when(s < N_SHARD)
        def _():
            d = _dot_nt(dp_ref[...], w_ref[0])

            @pl.when(s == 0)
            def _():
                dh[rows, :] = d

            @pl.when(s > 0)
            def _():
                dh[rows, :] += d

        for sv in range(3):
            @at_step(sv, nt - 1)
            def _(sv=sv):
                d2d[sv].wait_recv()
                keep[sv] = (keep[sv].astype(F32) + sibrcv[sv].astype(F32)).astype(BF16)
                ici[sv].start()

        @at_norm_block(0)
        def _():
            d2d[3].wait_recv()
            ici[0].wait_recv()
            acc[mine, :] += sibrcv[3].astype(F32) + rcv[0].astype(F32)

        @at_norm_block(1)
        def _():
            tot = res_o[mine_o, :]
            for sv in range(3):
                ici_o[sv].wait_recv()
                tot = tot + rcv_o[sv].astype(F32)
            res_o[mine_o, :] = tot
            fin_o.start()

        @at_norm_block(2)
        def _():
            ici[1].wait_recv()
            acc[mine, :] += rcv[1].astype(F32)

        @at_norm_block(0)
        def _():
            dng[...] = jnp.zeros_like(dng)

        @pl.when(s >= N_SHARD)
        def _():
            blk = (s - N_SHARD) * nt + t
            dhv = dh[pl.ds(pl.multiple_of(blk * TX, TX), TX), :]
            xv = x_ref[...]
            r = lax.rsqrt(jnp.mean(xv * xv, axis=-1, keepdims=True) + EPS)
            xn = xv * r
            dng[...] += jnp.sum(dhv * xn, axis=0, keepdims=True)
            dxn = dhv * g_ref[...]
            gx_ref[...] = dx2_ref[...] + r * (dxn - xn * jnp.mean(dxn * xn, axis=-1, keepdims=True))

        @at_step(n_steps - 1, nt - 1)
        def _():
            sm_buf[me] = sm_ref[...] + smb_ref[...]
            sm_buf[me, 0:1, :] = dng[...]
            for cp in smalls:
                cp.start()
            ici[2].wait_recv()
            acc[mine, :] += rcv[2].astype(F32)
            fin.start()
            store_w.start()
            for m in range(1, N_DEV):
                copy(SEM_SMALL + m, sm_buf.at[0], sm_buf.at[0], sibling).wait_recv()
            tot = sm_buf[0]
            for d in range(1, N_DEV):
                tot = tot + sm_buf[d]
            osm_ref[...] = tot
            fin_o.wait_recv()
            store_o.start()
            fin.wait_recv()
            for cp in d2d + [d2d_o] + ici + ici_o + [fin, fin_o] + smalls:
                cp.wait_send()
            store_o.wait()
            store_w.wait()

    def shard_of(s, kr):
        return kr[0] ^ (3 - jnp.minimum(s, 3))

    def tok(s, t):
        return jnp.where(s < N_SHARD, t, nt - 1)

    def blk_map(s, t, kr):
        return (jnp.where(s < N_SHARD, 0, (s - N_SHARD) * nt + t), 0)

    hbm = pl.BlockSpec(memory_space=pl.ANY)
    grid_spec = pltpu.PrefetchScalarGridSpec(
        num_scalar_prefetch=1, grid=(n_steps, nt),
        in_specs=[pl.BlockSpec((TT, D_MODEL), lambda s, t, kr: (tok(s, t), 0)),
                  pl.BlockSpec((TT, SHARD_COLS), lambda s, t, kr: (tok(s, t), shard_of(s, kr))),
                  pl.BlockSpec((1, D_MODEL, SHARD_COLS), lambda s, t, kr: (shard_of(s, kr), 0, 0)),
                  pl.BlockSpec((N_SHARD, WO_ROWS, D_MODEL), lambda s, t, kr: (0, 0, 0)),
                  pl.BlockSpec((TX, D_MODEL), blk_map),
                  pl.BlockSpec((TX, D_MODEL), blk_map),
                  pl.BlockSpec((1, D_MODEL), lambda s, t, kr: (0, 0)),
                  pl.BlockSpec((8, D_MODEL), lambda s, t, kr: (0, 0)),
                  pl.BlockSpec((8, D_MODEL), lambda s, t, kr: (0, 0))],
        out_specs=(pl.BlockSpec((TX, D_MODEL), blk_map), hbm, hbm,
                   pl.BlockSpec((8, D_MODEL), lambda s, t, kr: (0, 0))),
        scratch_shapes=[pltpu.VMEM((D_MODEL, SHARD_COLS), F32), pltpu.VMEM((SEQ, D_MODEL), F32),
                        pltpu.VMEM((N_SHARD, hw, SHARD_COLS), BF16), pltpu.VMEM((3, hw, SHARD_COLS), BF16),
                        pltpu.VMEM((N_SHARD, hw, SHARD_COLS), BF16), pltpu.VMEM((3, hw, SHARD_COLS), BF16),
                        pltpu.VMEM((N_SHARD, ho, D_MODEL), BF16), pltpu.VMEM((N_SHARD, ho, D_MODEL), BF16),
                        pltpu.VMEM((3, ho, D_MODEL), BF16), pltpu.VMEM((WO_ROWS, D_MODEL), F32),
                        pltpu.VMEM((N_DEV, 8, D_MODEL), F32), pltpu.VMEM((1, D_MODEL), F32),
                        pltpu.SemaphoreType.DMA((N_SEM_TAIL,)), pltpu.SemaphoreType.DMA((N_SEM_TAIL,)),
                        pltpu.SemaphoreType.DMA((2,))])
    return pl.pallas_call(
        body, name="bwd_tail", grid_spec=grid_spec,
        out_shape=(jax.ShapeDtypeStruct((SEQ, D_MODEL), F32),
                   jax.ShapeDtypeStruct((D_MODEL, SHARD_COLS), F32),
                   jax.ShapeDtypeStruct((WO_ROWS, D_MODEL), F32),
                   jax.ShapeDtypeStruct((8, D_MODEL), F32)),
        compiler_params=pltpu.CompilerParams(dimension_semantics=("arbitrary", "arbitrary"),
                                             vmem_limit_bytes=60 * 1024 * 1024, collective_id=COLLECTIVE_TAIL),
    )(kidx, h, dproj, wg, gwo, x2d, dx2, g1, small_a, small_b)


def _adam_update(w, g, m, v):
    nm = ADAM_B1 * m + (1.0 - ADAM_B1) * g
    nv = ADAM_B2 * v + (1.0 - ADAM_B2) * (g * g)
    m_hat = nm / (1.0 - ADAM_B1 ** ADAM_STEP)
    v_hat = nv / (1.0 - ADAM_B2 ** ADAM_STEP)
    return -ADAM_LR * (m_hat / (jnp.sqrt(v_hat) + ADAM_EPS) + ADAM_WD * w), nm, nv


def _adamw_all(tot, g_w_in, g_w_out, big, small, grad_x):
    n = len(small)
    rows = WO_ROWS
    steps = D_MODEL // rows

    def body(tot_ref, *refs):
        gx_ref, gx_out = refs[2 + 3 * (2 + n)], refs[-1]
        gx_out[...] = gx_ref[...]
        ins, outs = refs[:2 + 3 * (2 + n)], refs[3 + 3 * (2 + n):-1]
        g_refs, wmv = ins[:2], ins[2:]
        loss_ref, quads = outs[0], outs[1:]

        def update(j, g):
            w_ref, m_ref, v_ref = wmv[3 * j:3 * j + 3]
            g_ref, d_ref, nm_ref, nv_ref = quads[4 * j:4 * j + 4]
            g_ref[...] = g
            d_ref[...], nm_ref[...], nv_ref[...] = _adam_update(w_ref[...], g, m_ref[...], v_ref[...])

        update(0, g_refs[0][...])

        @pl.when(pl.program_id(0) == 0)
        def _():
            update(1, g_refs[1][...])
            k = 2 * lax.axis_index("x") + lax.axis_index("y")
            mine = pl.ds(pl.multiple_of(k * HEAD, HEAD), HEAD)
            loss_ref[...] = tot_ref[7:8, 0:1]
            grads = [tot_ref[0:1, :], tot_ref[1:2, :], tot_ref[2:3, 0:D_HGRN], tot_ref[2:3, D_HGRN:],
                     jnp.concatenate([tot_ref[3:4, 0:D_HGRN], tot_ref[3:4, D_HGRN:]], axis=0),
                     jnp.concatenate([tot_ref[4 + tap:5 + tap, mine] for tap in range(3)], axis=1)]
            for j, g in enumerate(grads):
                update(2 + j, g)

    whole = lambda a: pl.BlockSpec(a.shape, lambda i: (0, 0))
    blk = pl.BlockSpec((rows, SHARD_COLS), lambda i: (i, 0))
    arrays = [a for triple in big + small for a in triple]
    in_specs = ([whole(tot), blk, whole(g_w_out)] + [blk] * 3 + [whole(a) for a in arrays[3:]])
    shapes = [big[0][0], big[1][0]] + [w for w, _, _ in small]
    out_shape = (jax.ShapeDtypeStruct((1, 1), F32),) + tuple(
        jax.ShapeDtypeStruct(w.shape, F32) for w in shapes for _ in range(4))
    out_specs = (pl.BlockSpec((1, 1), lambda i: (0, 0)),) + (blk,) * 4 + tuple(
        whole(w) for w in shapes[1:] for _ in range(4))
    gx_blk = pl.BlockSpec((SEQ // steps, D_MODEL), lambda i: (i, 0))
    outs = pl.pallas_call(
        body, name="adamw_all", grid=(steps,),
        out_shape=out_shape + (jax.ShapeDtypeStruct(grad_x.shape, F32),),
        in_specs=in_specs + [gx_blk], out_specs=out_specs + (gx_blk,),
        compiler_params=pltpu.CompilerParams(dimension_semantics=("arbitrary",), vmem_limit_bytes=VMEM_LIMIT),
    )(tot, g_w_in, g_w_out, *arrays, grad_x)
    return [outs[0]] + [outs[1 + 4 * j:5 + 4 * j] for j in range(2 + n)] + [outs[-1]]


def _local_step(x2d, tgt, proj, lb_logits, cw, ga, gcn, w_out, gf):
    g64 = _group_matrix(HEAD, CONV_GROUP)
    aux, states, dx2, dmixed, gwo, part_out = _mix_out(proj, lb_logits, cw, ga, gcn, g64, w_out, x2d, gf, tgt)
    dproj, part_mix = _mix_bwd(proj, aux, states, dmixed, lb_logits, cw, ga, gcn, g64)
    return dproj, dx2, gwo.reshape(N_SHARD, WO_ROWS, D_MODEL), part_out, part_mix


def kernel(x, norm_gain, w_in, lb_logits, conv_w, hgrn_norm_gain, conv_norm_gain, w_out, final_norm_gain, loss_target, m_norm_gain, m_w_in, m_lb_logits, m_conv_w, m_hgrn_norm_gain, m_conv_norm_gain, m_w_out, m_final_norm_gain, v_norm_gain, v_w_in, v_lb_logits, v_conv_w, v_hgrn_norm_gain, v_conv_norm_gain, v_w_out, v_final_norm_gain):
    k = 2 * lax.axis_index("x") + lax.axis_index("y")
    kidx = jnp.reshape(k, (1,)).astype(jnp.int32)
    row = lambda a: a.reshape(1, D_MODEL)
    taps = lambda a: a.reshape(1, 3 * HEAD)
    h, proj, wg, cw = _gather_proj(kidx, x[0], norm_gain, w_in, taps(conv_w))
    dproj, dx2, gwo, part_out, part_mix = _local_step(
        x[0], loss_target[0], proj, lb_logits, cw, hgrn_norm_gain, conv_norm_gain, w_out, row(final_norm_gain))
    rgrad_x, rg_w_in, rg_w_out, tot = _bwd_tail(kidx, h, dproj, wg, gwo, x[0], dx2, norm_gain, part_out, part_mix)

    (loss, (g_w_in, d_w_in, nm_w_in, nv_w_in), (g_w_out, d_w_out, nm_w_out, nv_w_out),
     (g_norm_gain, d_ng, nm_ng, nv_ng), (g_final, d_fg, nm_fg, nv_fg), (g_hgrn, d_hg, nm_hg, nv_hg),
     (g_convn, d_cg, nm_cg, nv_cg), (g_lb, d_lb, nm_lb, nv_lb), (g_conv_w, d_cw, nm_cw, nv_cw),
     grad_x) = _adamw_all(
        tot, rg_w_in, rg_w_out,
        [(w_in[0], m_w_in[0], v_w_in[0]), (w_out[0], m_w_out[0], v_w_out[0])],
        [(norm_gain, m_norm_gain, v_norm_gain),
         (row(final_norm_gain), row(m_final_norm_gain), row(v_final_norm_gain)),
         (hgrn_norm_gain, m_hgrn_norm_gain, v_hgrn_norm_gain),
         (conv_norm_gain, m_conv_norm_gain, v_conv_norm_gain),
         (lb_logits, m_lb_logits, v_lb_logits),
         (taps(conv_w), taps(m_conv_w), taps(v_conv_w))],
        rgrad_x)
    flat = lambda a: a.reshape(D_MODEL)
    untap = lambda a: a.reshape(1, 3, HEAD)
    return (loss.reshape(()), grad_x[None],
            g_norm_gain, g_w_in[None], g_lb, untap(g_conv_w), g_hgrn, g_convn, g_w_out[None], flat(g_final),
            d_ng, d_w_in[None], d_lb, untap(d_cw), d_hg, d_cg, d_w_out[None], flat(d_fg),
            nm_ng, nm_w_in[None], nm_lb, untap(nm_cw), nm_hg, nm_cg, nm_w_out[None], flat(nm_fg),
            nv_ng, nv_w_in[None], nv_lb, untap(nv_cw), nv_hg, nv_cg, nv_w_out[None], flat(nv_fg))
```

```python
import jax
import jax.numpy as jnp
import numpy as np
from jax import lax
from jax.experimental import pallas as pl
from jax.experimental.pallas import tpu as pltpu

F32 = jnp.float32
BF16 = jnp.bfloat16
MESH = pl.DeviceIdType.MESH

SEQ = 2048
D_MODEL = 1024
D_HGRN = 512
D_CONV = 512
HEAD = 128
N_HEADS = 4
CHUNK = 64
CONV_GROUP = 64
N_SHARD = 4
SHARD_COLS = 1024
WO_ROWS = 256
EPS = 1e-6
TB = 256
NCB = TB // CHUNK
N_CHUNKS = SEQ // CHUNK
N_DEV = 8
COLLECTIVE_GATHER, COLLECTIVE_MIX_OUT, COLLECTIVE_TAIL = 1, 0, 2
AUX_O, AUX_CV, AUX_B, AUX_COLS = 0, 512, 1024, 1536

ADAM_LR = 0.001
ADAM_B1 = 0.9
ADAM_B2 = 0.999
ADAM_EPS = 1e-08
ADAM_WD = 0.01
ADAM_STEP = 10

VMEM_LIMIT = 56 * 1024 * 1024


def _dot(a, b):
    return jnp.dot(a, b, preferred_element_type=F32)


def _dot_nt(a, b):
    return lax.dot_general(a, b, (((1,), (1,)), ((), ())), preferred_element_type=F32)


def _dot_tn(a, b):
    return lax.dot_general(a, b, (((0,), (0,)), ((), ())), preferred_element_type=F32)


def _split_bf16(x, n):
    parts = []
    r = x
    for _ in range(n):
        p = r.astype(BF16)
        parts.append(p)
        r = r - p.astype(F32)
    return parts


def _exact_left(m, x, n=3):
    acc = None
    for p in _split_bf16(x, n):
        t = _dot(m, p)
        acc = t if acc is None else acc + t
    return acc


def _exact_left_many(m, xs, n=3):
    parts = [_split_bf16(x, n) for x in xs]
    accs = [None] * len(xs)
    for i in range(n):
        for j in range(len(xs)):
            t = _dot(m, parts[j][i])
            accs[j] = t if accs[j] is None else accs[j] + t
    return accs


def _group_mean_many(xs, gmat, n=2):
    parts = [_split_bf16(x, n) for x in xs]
    accs = [None] * len(xs)
    for i in range(n):
        for j in range(len(xs)):
            t = _dot(parts[j][i], gmat)
            accs[j] = t if accs[j] is None else accs[j] + t
    return accs


def _group_mean(x, gmat, n=2):
    w = gmat.shape[0]
    outs = []
    for c0 in range(0, x.shape[1], w):
        acc = None
        for p in _split_bf16(x[:, c0:c0 + w], n):
            t = _dot(p, gmat)
            acc = t if acc is None else acc + t
        outs.append(acc)
    return jnp.concatenate(outs, axis=1)


def _sigmoid(x):
    return 1.0 / (1.0 + jnp.exp(-x))


def _lower_bound(lbl):
    l0 = lbl[0:1, :]
    l1 = lbl[1:2, :]
    m = jnp.maximum(l0, l1)
    e0 = jnp.exp(l0 - m)
    e1 = jnp.exp(l1 - m)
    return e0 / (e0 + e1)


def _tri(lower):
    r = lax.broadcasted_iota(jnp.int32, (CHUNK, CHUNK), 0)
    c = lax.broadcasted_iota(jnp.int32, (CHUNK, CHUNK), 1)
    return jnp.where((c <= r) if lower else (c >= r), 1.0, 0.0).astype(BF16)


def _causal():
    r = lax.broadcasted_iota(jnp.int32, (CHUNK, CHUNK), 0)
    c = lax.broadcasted_iota(jnp.int32, (CHUNK, CHUNK), 1)
    return c <= r


def _shift_down(x, sh, prev_tail):
    r = pltpu.roll(x, sh, 0)
    pt = pltpu.roll(prev_tail, sh, 0)
    rows = lax.broadcasted_iota(jnp.int32, prev_tail.shape, 0)
    top = jnp.where(rows < sh, pt, r[0:8])
    return jnp.concatenate([top, r[8:]], axis=0)


def _shift_up(x, sh, next_head):
    n = x.shape[0]
    r = pltpu.roll(x, n - sh, 0)
    nh = pltpu.roll(next_head, 8 - sh, 0)
    rows = lax.broadcasted_iota(jnp.int32, next_head.shape, 0)
    bot = jnp.where(rows >= 8 - sh, nh, r[n - 8:])
    return jnp.concatenate([r[:n - 8], bot], axis=0)


def _group_matrix(width, group):
    r = np.arange(width)[:, None] // group
    c = np.arange(width)[None, :] // group
    return jnp.asarray(np.where(r == c, 1.0 / group, 0.0), dtype=BF16)


TG = 1024
SEM_W, SEM_CW, SEM_W_FWD, N_SEM = 0, 4, 7, 11


def _gather_proj(kidx, x2d, g1, w_in, conv_w):
    half_w = D_MODEL // 2
    half_c = SHARD_COLS // 2
    nt = SEQ // TG
    n_steps = 2 * N_SHARD

    def body(k_ref, x_ref, g_ref, w_ref, cw_ref, h_ref, p_ref, wg_out, cwg_out,
             wg_v, cwg_v, send_sems, recv_sems, out_sems):
        s, t = pl.program_id(0), pl.program_id(1)
        x, y, c = lax.axis_index("x"), lax.axis_index("y"), lax.axis_index("c")
        k = 2 * x + y
        sibling = (x, y, 1 - c)
        chips = [(1 - x, y), (x, 1 - y), (1 - x, 1 - y)]
        kjs = [2 * cx + cy for cx, cy in chips]
        diag = (*chips[2], c)

        def w_half(kk, cc):
            return wg_v.at[kk, pl.ds(cc * half_w, half_w), :]

        def w_quarter(kk, cc, piece):
            return wg_v.at[kk, pl.ds(cc * half_w, half_w), piece * half_c:(piece + 1) * half_c]

        def cw_of(kk):
            return cwg_v.at[:, pl.ds(pl.multiple_of(kk * HEAD, HEAD), HEAD)]

        def copy(sem, ref, to):
            return pltpu.make_async_remote_copy(
                src_ref=ref, dst_ref=ref, send_sem=send_sems.at[sem], recv_sem=recv_sems.at[sem],
                device_id=to, device_id_type=MESH)

        def at_step(sv, tv):
            return pl.when((s == sv) & (t == tv))

        w_direct = ([copy(SEM_W + j, w_half(k, c), (*chips[j], c)) for j in range(2)]
                    + [copy(SEM_W + 2 + p, w_quarter(k, c, p), diag) for p in range(2)])
        cw_direct = [copy(SEM_CW + j, cw_of(k), (*chip, c)) for j, chip in enumerate(chips)]
        w_passed = ([copy(SEM_W_FWD + j, w_half(kjs[j], c), sibling) for j in range(2)]
                    + [copy(SEM_W_FWD + 2 + p, w_quarter(kjs[2], c, p), sibling) for p in range(2)])
        stores = ([pltpu.make_async_copy(wg_v.at[kk], wg_out.at[kk], out_sems.at[i])
                   for i, kk in enumerate([k] + kjs)]
                  + [pltpu.make_async_copy(cwg_v, cwg_out, out_sems.at[4])])

        @at_step(0, 0)
        def _():
            barrier = pltpu.get_barrier_semaphore()
            for peer in [sibling] + [(*chip, c) for chip in chips]:
                pl.semaphore_signal(barrier, inc=1, device_id=peer, device_id_type=MESH)
            wg_v[k] = w_ref[0].astype(BF16)
            mine = pl.ds(pl.multiple_of(k * HEAD, HEAD), HEAD)
            cwg_v[:, mine] = jnp.zeros((8, HEAD), F32)
            for tap in range(3):
                cwg_v[tap:tap + 1, mine] = cw_ref[:, tap * HEAD:(tap + 1) * HEAD]
            pl.semaphore_wait(barrier, 4)
            w_direct[0].start()
            w_direct[1].start()
            for cp in cw_direct:
                cp.start()
            stores[0].start()

        @at_step(2, 0)
        def _():
            for j in range(2):
                copy(SEM_W + j, w_half(kjs[j], c), sibling).wait_recv()
                w_passed[j].start()
            w_direct[2].start()
            w_direct[3].start()
            copy(SEM_W_FWD, w_half(kjs[0], 1 - c), sibling).wait_recv()
            stores[1].start()

        @at_step(4, 0)
        def _():
            copy(SEM_W_FWD + 1, w_half(kjs[1], 1 - c), sibling).wait_recv()
            stores[2].start()

        for p in range(2):
            @at_step(6 + p, 0)
            def _(p=p):
                copy(SEM_W + 2 + p, w_quarter(kjs[2], c, p), sibling).wait_recv()
                w_passed[2 + p].start()
                copy(SEM_W_FWD + 2 + p, w_quarter(kjs[2], 1 - c, p), sibling).wait_recv()

        rows = pl.ds(pl.multiple_of(t * TG, TG), TG)

        @pl.when(s == 0)
        def _():
            xv = x_ref[...]
            r = lax.rsqrt(jnp.mean(xv * xv, axis=-1, keepdims=True) + EPS)
            h_ref[rows, :] = (xv * r * g_ref[...]).astype(BF16)

        sh = s >> 1
        js = k ^ (((sh & 1) << 1) | (sh >> 1))
        for piece in range(2):
            @pl.when((s & 1) == piece)
            def _(piece=piece):
                p_ref[...] = _dot(h_ref[rows, :], wg_v[js, :, piece * half_c:(piece + 1) * half_c])

        @at_step(n_steps - 1, nt - 1)
        def _():
            stores[3].start()
            for j in range(3):
                copy(SEM_CW + j, cw_of(kjs[j]), sibling).wait_recv()
            stores[4].start()
            for cp in w_direct + cw_direct + w_passed:
                cp.wait_send()
            for st in stores:
                st.wait()

    def x_map(s, t, kr):
        return (jnp.where(s == 0, t, nt - 1), 0)

    def p_map(s, t, kr):
        sh = s >> 1
        return (t, 2 * (kr[0] ^ (((sh & 1) << 1) | (sh >> 1))) + (s & 1))

    hbm = pl.BlockSpec(memory_space=pl.ANY)
    grid_spec = pltpu.PrefetchScalarGridSpec(
        num_scalar_prefetch=1, grid=(n_steps, nt),
        in_specs=[pl.BlockSpec((TG, D_MODEL), x_map),
                  pl.BlockSpec((1, D_MODEL), lambda s, t, kr: (0, 0)),
                  pl.BlockSpec((1, D_MODEL, SHARD_COLS), lambda s, t, kr: (0, 0, 0)),
                  pl.BlockSpec((1, 3 * HEAD), lambda s, t, kr: (0, 0))],
        out_specs=(pl.BlockSpec((SEQ, D_MODEL), lambda s, t, kr: (0, 0)),
                   pl.BlockSpec((TG, half_c), p_map), hbm, hbm),
        scratch_shapes=[pltpu.VMEM((N_SHARD, D_MODEL, SHARD_COLS), BF16),
                        pltpu.VMEM((8, D_CONV), F32),
                        pltpu.SemaphoreType.DMA((N_SEM,)), pltpu.SemaphoreType.DMA((N_SEM,)),
                        pltpu.SemaphoreType.DMA((5,))])
    return pl.pallas_call(
        body, name="gather_proj", grid_spec=grid_spec,
        out_shape=(jax.ShapeDtypeStruct((SEQ, D_MODEL), BF16),
                   jax.ShapeDtypeStruct((SEQ, N_SHARD * SHARD_COLS), F32),
                   jax.ShapeDtypeStruct((N_SHARD, D_MODEL, SHARD_COLS), BF16),
                   jax.ShapeDtypeStruct((8, D_CONV), F32)),
        compiler_params=pltpu.CompilerParams(dimension_semantics=("arbitrary", "arbitrary"),
                                             vmem_limit_bytes=VMEM_LIMIT, collective_id=COLLECTIVE_GATHER),
    )(kidx, x2d, g1, w_in, conv_w)


LAG = 6


def _mix_out(proj, lb_logits, cw, ga, gcn, g64, w_out, x2d, gf, tgt):
    half_o = WO_ROWS // 2
    nblk = SEQ // TB
    n_steps = nblk + LAG

    def body(p_ref, lbl_ref, cw_ref, ga_ref, gcn_ref, g64_ref, wo_ref, x_ref, gf_ref, t_ref,
             aux_ref, sto_ref, dx2_ref, dm_ref, gwo_ref, part_ref,
             st_ref, tail_ref, wog_v, stage, ring, acc_ref, send_sems, recv_sems):
        i = pl.program_id(0)
        x, y, c = lax.axis_index("x"), lax.axis_index("y"), lax.axis_index("c")
        k = 2 * x + y
        sibling = (x, y, 1 - c)
        chips = [(1 - x, y), (x, 1 - y), (1 - x, 1 - y)]
        kjs = [2 * cx + cy for cx, cy in chips]

        def wo_half(kk, cc):
            return wog_v.at[pl.ds(pl.multiple_of(kk * WO_ROWS + cc * half_o, half_o), half_o), :]

        def copy(sem, ref, to):
            return pltpu.make_async_remote_copy(
                src_ref=ref, dst_ref=ref, send_sem=send_sems.at[sem], recv_sem=recv_sems.at[sem],
                device_id=to, device_id_type=MESH)

        wo_direct = [copy(j, wo_half(k, c), (*chip, c)) for j, chip in enumerate(chips)]
        wo_passed = [copy(3 + j, wo_half(kj, c), sibling) for j, kj in enumerate(kjs)]

        @pl.when(i == 0)
        def _():
            barrier = pltpu.get_barrier_semaphore()
            for peer in [sibling] + [(*chip, c) for chip in chips]:
                pl.semaphore_signal(barrier, inc=1, device_id=peer, device_id_type=MESH)
            st_ref[...] = jnp.zeros_like(st_ref)
            tail_ref[...] = jnp.zeros_like(tail_ref)
            acc_ref[...] = jnp.zeros_like(acc_ref)
            part_ref[...] = jnp.zeros_like(part_ref)
            wog_v[pl.ds(pl.multiple_of(k * WO_ROWS, WO_ROWS), WO_ROWS), :] = wo_ref[0].astype(BF16)
            pl.semaphore_wait(barrier, 4)
            for cp in wo_direct:
                cp.start()

        @pl.when(i == LAG - 1)
        def _():
            for j in range(3):
                copy(j, wo_half(kjs[j], c), sibling).wait_recv()
                wo_passed[j].start()

        @pl.when(i == LAG)
        def _():
            for j in range(3):
                copy(3 + j, wo_half(kjs[j], 1 - c), sibling).wait_recv()

        lb = _lower_bound(lbl_ref[...])
        tri = _tri(True)
        causal = _causal()
        g64m = g64_ref[...]
        heads = range(N_HEADS)
        cs = [slice(hd * HEAD, (hd + 1) * HEAD) for hd in heads]
        col = lambda base, hd: slice(base + hd * HEAD, base + (hd + 1) * HEAD)

        def mix_chunk(n):
            sl = pl.ds(n * CHUNK, CHUNK)
            sg = [_sigmoid(p_ref[sl, col(512, hd)]) for hd in heads]
            f = [lb[:, cs[hd]] + (1.0 - lb[:, cs[hd]]) * sg[hd] for hd in heads]
            bc = _exact_left_many(tri, [jnp.log(f[hd]) for hd in heads])
            for hd in heads:
                aux_ref[sl, col(AUX_B, hd)] = bc[hd]
            g = [bc[hd][CHUNK - 1:CHUNK, :] for hd in heads]
            qd = [(p_ref[sl, col(0, hd)] * jnp.exp(bc[hd])).astype(BF16) for hd in heads]
            kk = [1.0 - f[hd] for hd in heads]
            ki = [(kk[hd] * jnp.exp(-bc[hd])).astype(BF16) for hd in heads]
            ke = [(kk[hd] * jnp.exp(g[hd] - bc[hd])).astype(BF16) for hd in heads]
            vb = [p_ref[sl, col(1024, hd)].astype(BF16) for hd in heads]
            st = [st_ref[hd] for hd in heads]
            st_b = [a.astype(BF16) for a in st]
            for hd in heads:
                sto_ref[n, hd] = st_b[hd]
            scm = [_dot_nt(qd[hd], ki[hd]) for hd in heads]
            inter = [_dot_nt(qd[hd], st_b[hd]) for hd in heads]
            upd = [_dot_tn(vb[hd], ke[hd]) for hd in heads]
            intra = [_dot(jnp.where(causal, scm[hd], 0.0).astype(BF16), vb[hd]) for hd in heads]
            for hd in heads:
                st_ref[hd] = st[hd] * jnp.exp(g[hd]) + upd[hd]
                o = intra[hd] + inter[hd]
                aux_ref[sl, col(AUX_O, hd)] = o
                ra = lax.rsqrt(jnp.mean(o * o, axis=-1, keepdims=True) + EPS)
                za = p_ref[sl, col(1536, hd)]
                stage[sl, cs[hd]] = (o * ra * ga_ref[:, cs[hd]] * (za * _sigmoid(za))).astype(BF16)
            yb = []
            for hd in heads:
                cu = p_ref[sl, col(3072, hd)] * p_ref[sl, col(2048, hd)]
                tail = tail_ref[:, cs[hd]]
                cv = (cw_ref[0:1, cs[hd]] * _shift_down(cu, 2, tail) + cw_ref[1:2, cs[hd]] * _shift_down(cu, 1, tail)
                      + cw_ref[2:3, cs[hd]] * cu)
                tail_ref[:, cs[hd]] = cu[CHUNK - 8:, :]
                aux_ref[sl, col(AUX_CV, hd)] = cv
                yb.append(p_ref[sl, col(2560, hd)] * cv)
            ms = _group_mean_many([y * y for y in yb], g64m)
            for hd in heads:
                rb = lax.rsqrt(ms[hd] + EPS)
                zb = p_ref[sl, col(3584, hd)]
                stage[sl, col(512, hd)] = (yb[hd] * rb * gcn_ref[:, cs[hd]] * (zb * _sigmoid(zb))).astype(BF16)

        def step(mix, project):
            if project:
                mixed_b = ring[pl.ds(pl.multiple_of((i - LAG) * TB, TB), TB), :]
                y = _dot(mixed_b, wog_v[...])
            if mix:
                mix_chunk(0)
            if project:
                x2 = x_ref[...] + y
                r2 = lax.rsqrt(jnp.mean(x2 * x2, axis=-1, keepdims=True) + EPS)
                n2 = x2 * r2
                gfv = gf_ref[...]
                err = n2 * gfv - t_ref[...]
                loss = 0.5 * jnp.sum(jnp.mean(err * err, axis=-1, keepdims=True), axis=0, keepdims=True)
                dy = err * (1.0 / D_MODEL)
                part_ref[1:2, :] += jnp.sum(dy * n2, axis=0, keepdims=True)
                part_ref[7:8, :] += jnp.broadcast_to(loss, (1, D_MODEL))
                dn = dy * gfv
                dx2 = r2 * (dn - n2 * jnp.mean(dn * n2, axis=-1, keepdims=True))
                dx2_ref[...] = dx2
                dx2_b = dx2.astype(BF16)
            if mix:
                mix_chunk(1)
            if project:
                dm_ref[...] = _dot_nt(dx2_b, wog_v[...])
            if mix:
                mix_chunk(2)
            if project:
                acc_ref[...] += _dot_tn(mixed_b, dx2_b)
            if mix:
                mix_chunk(3)
                ring[pl.ds(pl.multiple_of(i * TB, TB), TB), :] = stage[...]

        @pl.when(i < LAG)
        def _():
            step(True, False)

        @pl.when((i >= LAG) & (i < nblk))
        def _():
            step(True, True)

        @pl.when(i >= nblk)
        def _():
            step(False, True)

        @pl.when(i == n_steps - 1)
        def _():
            gwo_ref[...] = acc_ref[...].astype(BF16)
            for cp in wo_direct + wo_passed:
                cp.wait_send()

    assert NCB == 4
    row = lambda w: pl.BlockSpec((1, w), lambda i: (0, 0))
    mix_blk = lambda i: jnp.minimum(i, nblk - 1)
    out_blk = lambda i: jnp.clip(i - LAG, 0, nblk - 1)
    tok = lambda: pl.BlockSpec((TB, D_MODEL), lambda i: (out_blk(i), 0))
    return pl.pallas_call(
        body, name="mix_out", grid=(n_steps,),
        out_shape=(jax.ShapeDtypeStruct((SEQ, AUX_COLS), F32),
                   jax.ShapeDtypeStruct((N_CHUNKS, N_HEADS, HEAD, HEAD), BF16),
                   jax.ShapeDtypeStruct((SEQ, D_MODEL), F32),
                   jax.ShapeDtypeStruct((SEQ, D_MODEL), F32),
                   jax.ShapeDtypeStruct((D_MODEL, D_MODEL), BF16),
                   jax.ShapeDtypeStruct((8, D_MODEL), F32)),
        in_specs=[pl.BlockSpec((TB, 4096), lambda i: (jnp.minimum(i, nblk - 1), 0)),
                  pl.BlockSpec((2, D_HGRN), lambda i: (0, 0)),
                  pl.BlockSpec((8, D_CONV), lambda i: (0, 0)),
                  row(D_HGRN), row(D_CONV),
                  pl.BlockSpec((HEAD, HEAD), lambda i: (0, 0)),
                  pl.BlockSpec((1, WO_ROWS, D_MODEL), lambda i: (0, 0, 0)),
                  tok(), row(D_MODEL), tok()],
        out_specs=(pl.BlockSpec((TB, AUX_COLS), lambda i: (mix_blk(i), 0)),
                   pl.BlockSpec((NCB, N_HEADS, HEAD, HEAD), lambda i: (mix_blk(i), 0, 0, 0)),
                   tok(), tok(),
                   pl.BlockSpec((D_MODEL, D_MODEL), lambda i: (0, 0)),
                   pl.BlockSpec((8, D_MODEL), lambda i: (0, 0))),
        scratch_shapes=[pltpu.VMEM((N_HEADS, HEAD, HEAD), F32), pltpu.VMEM((8, D_CONV), F32),
                        pltpu.VMEM((D_MODEL, D_MODEL), BF16), pltpu.VMEM((TB, D_MODEL), BF16),
                        pltpu.VMEM((SEQ, D_MODEL), BF16), pltpu.VMEM((D_MODEL, D_MODEL), F32),
                        pltpu.SemaphoreType.DMA((6,)), pltpu.SemaphoreType.DMA((6,))],
        compiler_params=pltpu.CompilerParams(dimension_semantics=("arbitrary",), vmem_limit_bytes=VMEM_LIMIT,
                                             collective_id=COLLECTIVE_MIX_OUT),
    )(proj, lb_logits, cw, ga, gcn, g64, w_out, x2d, gf, tgt)


def _mix_bwd(proj, aux, states, dmixed, lb_logits, cw, ga, gcn, g64):
    nblk = SEQ // TB

    def body(p_ref, aux_ref, st_ref, dm_ref, lbl_ref, cw_ref, ga_ref, gcn_ref, g64_ref,
             dp_ref, part_ref, dst_ref, head_ref, dlb_ref):
        i = pl.program_id(0)

        @pl.when(i == 0)
        def _():
            dst_ref[...] = jnp.zeros_like(dst_ref)
            head_ref[...] = jnp.zeros_like(head_ref)
            part_ref[...] = jnp.zeros_like(part_ref)
            dlb_ref[...] = jnp.zeros_like(dlb_ref)

        lb = _lower_bound(lbl_ref[...])
        triu = _tri(False)
        causal = _causal()
        g64m = g64_ref[...]
        rowsum = lambda a: jnp.sum(a, axis=0, keepdims=True)
        heads = range(N_HEADS)
        cs = [slice(hd * HEAD, (hd + 1) * HEAD) for hd in heads]
        col = lambda base, hd: slice(base + hd * HEAD, base + (hd + 1) * HEAD)
        for n in reversed(range(NCB)):
            sl = pl.ds(n * CHUNK, CHUNK)
            cvv = [aux_ref[sl, col(AUX_CV, hd)] for hd in heads]
            gb = [p_ref[sl, col(2560, hd)] for hd in heads]
            yb = [gb[hd] * cvv[hd] for hd in heads]
            ms = _group_mean_many([y * y for y in yb], g64m)
            rb, nb, dnb = [], [], []
            for hd in heads:
                rb.append(lax.rsqrt(ms[hd] + EPS))
                nb.append(yb[hd] * rb[hd])
                zb = p_ref[sl, col(3584, hd)]
                sgb = _sigmoid(zb)
                dmb = dm_ref[sl, col(512, hd)]
                silu = zb * sgb
                dgate = dmb * gcn_ref[:, cs[hd]]
                part_ref[2:3, col(512, hd)] += rowsum(dmb * nb[hd] * silu)
                dp_ref[sl, col(3584, hd)] = (dgate * nb[hd] * (sgb + silu * (1.0 - sgb))).astype(BF16)
                dnb.append(dgate * silu)
            mdn = _group_mean_many([dnb[hd] * nb[hd] for hd in heads], g64m)
            for hd in heads:
                dyb = rb[hd] * (dnb[hd] - nb[hd] * mdn[hd])
                dp_ref[sl, col(2560, hd)] = (dyb * cvv[hd]).astype(BF16)
                dcv = dyb * gb[hd]
                head = head_ref[:, cs[hd]]
                dcv1 = _shift_up(dcv, 1, head)
                dcv2 = _shift_up(dcv, 2, head)
                head_ref[:, cs[hd]] = dcv[0:8, :]
                u = p_ref[sl, col(2048, hd)]
                gc = p_ref[sl, col(3072, hd)]
                cu = gc * u
                part_ref[4:5, cs[hd]] += rowsum(dcv2 * cu)
                part_ref[5:6, cs[hd]] += rowsum(dcv1 * cu)
                part_ref[6:7, cs[hd]] += rowsum(dcv * cu)
                dcu = cw_ref[2:3, cs[hd]] * dcv + cw_ref[1:2, cs[hd]] * dcv1 + cw_ref[0:1, cs[hd]] * dcv2
                dp_ref[sl, col(3072, hd)] = (dcu * u).astype(BF16)
                dp_ref[sl, col(2048, hd)] = (dcu * gc).astype(BF16)
            do_b = []
            for hd in heads:
                ov = aux_ref[sl, col(AUX_O, hd)]
                ra = lax.rsqrt(jnp.mean(ov * ov, axis=-1, keepdims=True) + EPS)
                na = ov * ra
                za = p_ref[sl, col(1536, hd)]
                sga = _sigmoid(za)
                dma = dm_ref[sl, cs[hd]]
                silu = za * sga
                dgate = dma * ga_ref[:, cs[hd]]
                part_ref[2:3, cs[hd]] += rowsum(dma * na * silu)
                dp_ref[sl, col(1536, hd)] = (dgate * na * (sga + silu * (1.0 - sga))).astype(BF16)
                dna = dgate * silu
                do_b.append((ra * (dna - na * jnp.mean(dna * na, axis=-1, keepdims=True))).astype(BF16))
            s = [_sigmoid(p_ref[sl, col(512, hd)]) for hd in heads]
            f = [lb[:, cs[hd]] + (1.0 - lb[:, cs[hd]]) * s[hd] for hd in heads]
            bc = [aux_ref[sl, col(AUX_B, hd)] for hd in heads]
            g = [bc[hd][CHUNK - 1:CHUNK, :] for hd in heads]
            eb = [jnp.exp(bc[hd]) for hd in heads]
            enb = [jnp.exp(-bc[hd]) for hd in heads]
            eg = [jnp.exp(g[hd] - bc[hd]) for hd in heads]
            dec = [jnp.exp(g[hd]) for hd in heads]
            qd = [p_ref[sl, cs[hd]] * eb[hd] for hd in heads]
            kk = [1.0 - f[hd] for hd in heads]
            ki = [kk[hd] * enb[hd] for hd in heads]
            ke = [kk[hd] * eg[hd] for hd in heads]
            qd_b = [a.astype(BF16) for a in qd]
            ki_b = [a.astype(BF16) for a in ki]
            ke_b = [a.astype(BF16) for a in ke]
            vb = [p_ref[sl, col(1024, hd)].astype(BF16) for hd in heads]
            st_b = [st_ref[n, hd] for hd in heads]
            dst = [dst_ref[hd] for hd in heads]
            dst_b = [a.astype(BF16) for a in dst]
            scm = [_dot_nt(qd_b[hd], ki_b[hd]) for hd in heads]
            amm = [_dot_nt(do_b[hd], vb[hd]) for hd in heads]
            dqd2 = [_dot(do_b[hd], st_b[hd]) for hd in heads]
            dke = [_dot(vb[hd], dst_b[hd]) for hd in heads]
            dv2 = [_dot_nt(ke_b[hd], dst_b[hd]) for hd in heads]
            dsu = [_dot_tn(do_b[hd], qd_b[hd]) for hd in heads]
            sc = [jnp.where(causal, scm[hd], 0.0).astype(BF16) for hd in heads]
            am = [jnp.where(causal, amm[hd], 0.0).astype(BF16) for hd in heads]
            dqd1 = [_dot(am[hd], ki_b[hd]) for hd in heads]
            dki = [_dot_tn(am[hd], qd_b[hd]) for hd in heads]
            dv1 = [_dot_tn(sc[hd], do_b[hd]) for hd in heads]
            db, dgv, dkk = [], [], []
            for hd in heads:
                dqd = dqd1[hd] + dqd2[hd]
                ddec = rowsum(dst[hd] * st_b[hd].astype(F32))
                dst_ref[hd] = dst[hd] * dec[hd] + dsu[hd]
                dp_ref[sl, cs[hd]] = (dqd * eb[hd]).astype(BF16)
                dp_ref[sl, col(1024, hd)] = (dv1[hd] + dv2[hd]).astype(BF16)
                dke_eg = dke[hd] * eg[hd]
                dkk.append(dki[hd] * enb[hd] + dke_eg)
                db.append(dqd * qd[hd] - kk[hd] * dkk[hd])
                dgv.append(rowsum(kk[hd] * dke_eg) + ddec * dec[hd])
            rc = _exact_left_many(triu, db, 2)
            for hd in heads:
                df = (rc[hd] + dgv[hd]) / f[hd] - dkk[hd]
                one_s = 1.0 - s[hd]
                dlb_ref[:, cs[hd]] += rowsum(df * one_s)
                dp_ref[sl, col(512, hd)] = (df * (1.0 - lb[:, cs[hd]]) * s[hd] * one_s).astype(BF16)

        @pl.when(i == nblk - 1)
        def _():
            row = dlb_ref[...] * lb * (1.0 - lb)
            part_ref[3:4, 0:D_HGRN] = row
            part_ref[3:4, D_HGRN:] = -row

    rev = lambda w: pl.BlockSpec((TB, w), lambda i: (nblk - 1 - i, 0))
    row = lambda w: pl.BlockSpec((1, w), lambda i: (0, 0))
    return pl.pallas_call(
        body, name="mix_bwd", grid=(nblk,),
        out_shape=(jax.ShapeDtypeStruct((SEQ, 4096), BF16),
                   jax.ShapeDtypeStruct((8, D_MODEL), F32)),
        in_specs=[rev(4096), rev(AUX_COLS),
                  pl.BlockSpec((NCB, N_HEADS, HEAD, HEAD), lambda i: (nblk - 1 - i, 0, 0, 0)),
                  rev(D_MODEL),
                  pl.BlockSpec((2, D_HGRN), lambda i: (0, 0)),
                  pl.BlockSpec((8, D_CONV), lambda i: (0, 0)),
                  row(D_HGRN), row(D_CONV),
                  pl.BlockSpec((HEAD, HEAD), lambda i: (0, 0))],
        out_specs=(rev(4096), pl.BlockSpec((8, D_MODEL), lambda i: (0, 0))),
        scratch_shapes=[pltpu.VMEM((N_HEADS, HEAD, HEAD), F32), pltpu.VMEM((8, D_CONV), F32),
                        pltpu.VMEM((1, D_HGRN), F32)],
        compiler_params=pltpu.CompilerParams(dimension_semantics=("arbitrary",), vmem_limit_bytes=VMEM_LIMIT),
    )(proj, aux, states, dmixed, lb_logits, cw, ga, gcn, g64)


TT = 1024
TX = 256
(SEM_D2D, SEM_D2D_O, SEM_ICI, SEM_ICI_O, SEM_FIN, SEM_FIN_O, SEM_SMALL, N_SEM_TAIL) = 0, 4, 5, 8, 11, 12, 12, 20


def _bwd_tail(kidx, h, dproj, wg, gwo, x2d, dx2, g1, small_a, small_b):
    hw = D_MODEL // 2
    ho = WO_ROWS // 2
    nt = SEQ // TT
    n_steps = N_SHARD + SEQ // TX // nt

    def body(k_ref, h_ref, dp_ref, w_ref, gwo_ref, x_ref, dx2_ref, g_ref, sm_ref, smb_ref,
             gx_ref, gw_out, gwo_out, osm_ref,
             acc, dh, sendbuf, keep, sibrcv, rcv, sib_o, p_o, rcv_o, res_o, sm_buf, dng,
             send_sems, recv_sems, out_sems):
        s, t = pl.program_id(0), pl.program_id(1)
        x, y, c = lax.axis_index("x"), lax.axis_index("y"), lax.axis_index("c")
        k = 2 * x + y
        me = 4 * x + 2 * y + c
        sibling = (x, y, 1 - c)
        chips = [(1 - x, 1 - y), (1 - x, y), (x, 1 - y)]
        kjs = [2 * cx + cy for cx, cy in chips]
        mine = pl.ds(pl.multiple_of(c * hw, hw), hw)
        other = pl.ds(pl.multiple_of((1 - c) * hw, hw), hw)
        mine_o = pl.ds(pl.multiple_of(c * ho, ho), ho)
        other_o = pl.ds(pl.multiple_of((1 - c) * ho, ho), ho)

        def copy(sem, src, dst, to):
            return pltpu.make_async_remote_copy(
                src_ref=src, dst_ref=dst, send_sem=send_sems.at[sem], recv_sem=recv_sems.at[sem],
                device_id=to, device_id_type=MESH)

        def at_step(sv, tv):
            return pl.when((s == sv) & (t == tv))

        def at_norm_block(b):
            return at_step(N_SHARD + b // nt, b % nt)

        d2d = [copy(SEM_D2D + sv, sendbuf.at[sv], sibrcv.at[sv], sibling) for sv in range(N_SHARD)]
        d2d_o = copy(SEM_D2D_O, gwo_ref.at[:, other_o, :], sib_o, sibling)
        ici = [copy(SEM_ICI + sv, keep.at[sv], rcv.at[sv], (*chips[sv], c)) for sv in range(3)]
        ici_o = [copy(SEM_ICI_O + sv, p_o.at[kjs[sv]], rcv_o.at[sv], (*chips[sv], c)) for sv in range(3)]
        fin = copy(SEM_FIN, acc.at[mine, :], gw_out.at[mine, :], sibling)
        fin_o = copy(SEM_FIN_O, res_o.at[mine_o, :], res_o.at[mine_o, :], sibling)
        smalls = [copy(SEM_SMALL + m, sm_buf.at[me], sm_buf.at[me],
                       (x ^ (m >> 2), y ^ ((m >> 1) & 1), c ^ (m & 1))) for m in range(1, N_DEV)]
        store_w = pltpu.make_async_copy(acc.at[mine, :], gw_out.at[mine, :], out_sems.at[0])
        store_o = pltpu.make_async_copy(res_o, gwo_out, out_sems.at[1])

        @at_step(0, 0)
        def _():
            barrier = pltpu.get_barrier_semaphore()
            for m in range(1, N_DEV):
                pl.semaphore_signal(barrier, inc=1, device_id=(x ^ (m >> 2), y ^ ((m >> 1) & 1), c ^ (m & 1)),
                                    device_id_type=MESH)
            pl.semaphore_wait(barrier, N_DEV - 1)
            d2d_o.start()

        @at_step(0, 1)
        def _():
            d2d_o.wait_recv()
            for j in range(N_SHARD):
                p_o[j] = (gwo_ref[j, mine_o, :].astype(F32) + sib_o[j].astype(F32)).astype(BF16)
            res_o[mine_o, :] = gwo_ref[k, mine_o, :].astype(F32) + sib_o[k].astype(F32)
            for cp in ici_o:
                cp.start()

        rows = pl.ds(pl.multiple_of(t * TT, TT), TT)

        @pl.when((s < N_SHARD) & (t == 0))
        def _():
            acc[...] = _dot_tn(h_ref[...], dp_ref[...])

        @pl.when((s < N_SHARD) & (t > 0))
        def _():
            acc[...] += _dot_tn(h_ref[...], dp_ref[...])

        for sv in range(N_SHARD):
            @at_step(sv, nt - 1)
            def _(sv=sv):
                sendbuf[sv] = acc[other, :].astype(BF16)
                if sv < 3:
                    keep[sv] = acc[mine, :].astype(BF16)
                d2d[sv].start()

        @pl.when(s == 0)
        def _():
            dh[rows, :] = _dot_nt(dp_ref[...], w_ref[0])

        @pl.when((s > 0) & (s < N_SHARD))
        def _():
            dh[rows, :] += _dot_nt(dp_ref[...], w_ref[0])

        for sv in range(3):
            @at_step(sv, nt - 1)
            def _(sv=sv):
                d2d[sv].wait_recv()
                keep[sv] = (keep[sv].astype(F32) + sibrcv[sv].astype(F32)).astype(BF16)
                ici[sv].start()

        @at_norm_block(0)
        def _():
            d2d[3].wait_recv()
            ici[0].wait_recv()
            acc[mine, :] += sibrcv[3].astype(F32) + rcv[0].astype(F32)

        @at_norm_block(1)
        def _():
            tot = res_o[mine_o, :]
            for sv in range(3):
                ici_o[sv].wait_recv()
                tot = tot + rcv_o[sv].astype(F32)
            res_o[mine_o, :] = tot
            fin_o.start()

        @at_norm_block(2)
        def _():
            ici[1].wait_recv()
            acc[mine, :] += rcv[1].astype(F32)

        @at_norm_block(0)
        def _():
            dng[...] = jnp.zeros_like(dng)

        @pl.when(s >= N_SHARD)
        def _():
            blk = (s - N_SHARD) * nt + t
            dhv = dh[pl.ds(pl.multiple_of(blk * TX, TX), TX), :]
            xv = x_ref[...]
            r = lax.rsqrt(jnp.mean(xv * xv, axis=-1, keepdims=True) + EPS)
            xn = xv * r
            dng[...] += jnp.sum(dhv * xn, axis=0, keepdims=True)
            dxn = dhv * g_ref[...]
            gx_ref[...] = dx2_ref[...] + r * (dxn - xn * jnp.mean(dxn * xn, axis=-1, keepdims=True))

        @at_step(n_steps - 1, nt - 1)
        def _():
            sm_buf[me] = sm_ref[...] + smb_ref[...]
            sm_buf[me, 0:1, :] = dng[...]
            for cp in smalls:
                cp.start()
            ici[2].wait_recv()
            acc[mine, :] += rcv[2].astype(F32)
            fin.start()
            store_w.start()
            for m in range(1, N_DEV):
                copy(SEM_SMALL + m, sm_buf.at[0], sm_buf.at[0], sibling).wait_recv()
            tot = sm_buf[0]
            for d in range(1, N_DEV):
                tot = tot + sm_buf[d]
            osm_ref[...] = tot
            fin_o.wait_recv()
            store_o.start()
            fin.wait_recv()
            for cp in d2d + [d2d_o] + ici + ici_o + [fin, fin_o] + smalls:
                cp.wait_send()
            store_o.wait()
            store_w.wait()

    def shard_of(s, kr):
        return kr[0] ^ (3 - jnp.minimum(s, 3))

    def tok(s, t):
        return jnp.where(s < N_SHARD, t, nt - 1)

    def blk_map(s, t, kr):
        return (jnp.where(s < N_SHARD, 0, (s - N_SHARD) * nt + t), 0)

    hbm = pl.BlockSpec(memory_space=pl.ANY)
    grid_spec = pltpu.PrefetchScalarGridSpec(
        num_scalar_prefetch=1, grid=(n_steps, nt),
        in_specs=[pl.BlockSpec((TT, D_MODEL), lambda s, t, kr: (tok(s, t), 0)),
                  pl.BlockSpec((TT, SHARD_COLS), lambda s, t, kr: (tok(s, t), shard_of(s, kr))),
                  pl.BlockSpec((1, D_MODEL, SHARD_COLS), lambda s, t, kr: (shard_of(s, kr), 0, 0)),
                  pl.BlockSpec((N_SHARD, WO_ROWS, D_MODEL), lambda s, t, kr: (0, 0, 0)),
                  pl.BlockSpec((TX, D_MODEL), blk_map),
                  pl.BlockSpec((TX, D_MODEL), blk_map),
                  pl.BlockSpec((1, D_MODEL), lambda s, t, kr: (0, 0)),
                  pl.BlockSpec((8, D_MODEL), lambda s, t, kr: (0, 0)),
                  pl.BlockSpec((8, D_MODEL), lambda s, t, kr: (0, 0))],
        out_specs=(pl.BlockSpec((TX, D_MODEL), blk_map), hbm, hbm,
                   pl.BlockSpec((8, D_MODEL), lambda s, t, kr: (0, 0))),
        scratch_shapes=[pltpu.VMEM((D_MODEL, SHARD_COLS), F32), pltpu.VMEM((SEQ, D_MODEL), F32),
                        pltpu.VMEM((N_SHARD, hw, SHARD_COLS), BF16), pltpu.VMEM((3, hw, SHARD_COLS), BF16),
                        pltpu.VMEM((N_SHARD, hw, SHARD_COLS), BF16), pltpu.VMEM((3, hw, SHARD_COLS), BF16),
                        pltpu.VMEM((N_SHARD, ho, D_MODEL), BF16), pltpu.VMEM((N_SHARD, ho, D_MODEL), BF16),
                        pltpu.VMEM((3, ho, D_MODEL), BF16), pltpu.VMEM((WO_ROWS, D_MODEL), F32),
                        pltpu.VMEM((N_DEV, 8, D_MODEL), F32), pltpu.VMEM((1, D_MODEL), F32),
                        pltpu.SemaphoreType.DMA((N_SEM_TAIL,)), pltpu.SemaphoreType.DMA((N_SEM_TAIL,)),
                        pltpu.SemaphoreType.DMA((2,))])
    return pl.pallas_call(
        body, name="bwd_tail", grid_spec=grid_spec,
        out_shape=(jax.ShapeDtypeStruct((SEQ, D_MODEL), F32),
                   jax.ShapeDtypeStruct((D_MODEL, SHARD_COLS), F32),
                   jax.ShapeDtypeStruct((WO_ROWS, D_MODEL), F32),
                   jax.ShapeDtypeStruct((8, D_MODEL), F32)),
        compiler_params=pltpu.CompilerParams(dimension_semantics=("arbitrary", "arbitrary"),
                                             vmem_limit_bytes=60 * 1024 * 1024, collective_id=COLLECTIVE_TAIL),
    )(kidx, h, dproj, wg, gwo, x2d, dx2, g1, small_a, small_b)


def _adam_update(w, g, m, v):
    nm = ADAM_B1 * m + (1.0 - ADAM_B1) * g
    nv = ADAM_B2 * v + (1.0 - ADAM_B2) * (g * g)
    m_hat = nm / (1.0 - ADAM_B1 ** ADAM_STEP)
    v_hat = nv / (1.0 - ADAM_B2 ** ADAM_STEP)
    return -ADAM_LR * (m_hat / (jnp.sqrt(v_hat) + ADAM_EPS) + ADAM_WD * w), nm, nv


def _adamw_all(tot, g_w_in, g_w_out, big, small, grad_x):
    n = len(small)
    rows = WO_ROWS
    steps = D_MODEL // rows

    def body(tot_ref, *refs):
        gx_ref, gx_out = refs[2 + 3 * (2 + n)], refs[-1]
        gx_out[...] = gx_ref[...]
        ins, outs = refs[:2 + 3 * (2 + n)], refs[3 + 3 * (2 + n):-1]
        g_refs, wmv = ins[:2], ins[2:]
        loss_ref, quads = outs[0], outs[1:]

        def update(j, g):
            w_ref, m_ref, v_ref = wmv[3 * j:3 * j + 3]
            g_ref, d_ref, nm_ref, nv_ref = quads[4 * j:4 * j + 4]
            g_ref[...] = g
            d_ref[...], nm_ref[...], nv_ref[...] = _adam_update(w_ref[...], g, m_ref[...], v_ref[...])

        update(0, g_refs[0][...])

        @pl.when(pl.program_id(0) == 0)
        def _():
            update(1, g_refs[1][...])
            k = 2 * lax.axis_index("x") + lax.axis_index("y")
            mine = pl.ds(pl.multiple_of(k * HEAD, HEAD), HEAD)
            loss_ref[...] = tot_ref[7:8, 0:1]
            grads = [tot_ref[0:1, :], tot_ref[1:2, :], tot_ref[2:3, 0:D_HGRN], tot_ref[2:3, D_HGRN:],
                     jnp.concatenate([tot_ref[3:4, 0:D_HGRN], tot_ref[3:4, D_HGRN:]], axis=0),
                     jnp.concatenate([tot_ref[4 + tap:5 + tap, mine] for tap in range(3)], axis=1)]
            for j, g in enumerate(grads):
                update(2 + j, g)

    whole = lambda a: pl.BlockSpec(a.shape, lambda i: (0, 0))
    blk = pl.BlockSpec((rows, SHARD_COLS), lambda i: (i, 0))
    arrays = [a for triple in big + small for a in triple]
    in_specs = ([whole(tot), blk, whole(g_w_out)] + [blk] * 3 + [whole(a) for a in arrays[3:]])
    shapes = [big[0][0], big[1][0]] + [w for w, _, _ in small]
    out_shape = (jax.ShapeDtypeStruct((1, 1), F32),) + tuple(
        jax.ShapeDtypeStruct(w.shape, F32) for w in shapes for _ in range(4))
    out_specs = (pl.BlockSpec((1, 1), lambda i: (0, 0)),) + (blk,) * 4 + tuple(
        whole(w) for w in shapes[1:] for _ in range(4))
    gx_blk = pl.BlockSpec((SEQ // steps, D_MODEL), lambda i: (i, 0))
    outs = pl.pallas_call(
        body, name="adamw_all", grid=(steps,),
        out_shape=out_shape + (jax.ShapeDtypeStruct(grad_x.shape, F32),),
        in_specs=in_specs + [gx_blk], out_specs=out_specs + (gx_blk,),
        compiler_params=pltpu.CompilerParams(dimension_semantics=("arbitrary",), vmem_limit_bytes=VMEM_LIMIT),
    )(tot, g_w_in, g_w_out, *arrays, grad_x)
    return [outs[0]] + [outs[1 + 4 * j:5 + 4 * j] for j in range(2 + n)] + [outs[-1]]


def _local_step(x2d, tgt, proj, lb_logits, cw, ga, gcn, w_out, gf):
    g64 = _group_matrix(HEAD, CONV_GROUP)
    aux, states, dx2, dmixed, gwo, part_out = _mix_out(proj, lb_logits, cw, ga, gcn, g64, w_out, x2d, gf, tgt)
    dproj, part_mix = _mix_bwd(proj, aux, states, dmixed, lb_logits, cw, ga, gcn, g64)
    return dproj, dx2, gwo.reshape(N_SHARD, WO_ROWS, D_MODEL), part_out, part_mix


def kernel(x, norm_gain, w_in, lb_logits, conv_w, hgrn_norm_gain, conv_norm_gain, w_out, final_norm_gain, loss_target, m_norm_gain, m_w_in, m_lb_logits, m_conv_w, m_hgrn_norm_gain, m_conv_norm_gain, m_w_out, m_final_norm_gain, v_norm_gain, v_w_in, v_lb_logits, v_conv_w, v_hgrn_norm_gain, v_conv_norm_gain, v_w_out, v_final_norm_gain):
    k = 2 * lax.axis_index("x") + lax.axis_index("y")
    kidx = jnp.reshape(k, (1,)).astype(jnp.int32)
    row = lambda a: a.reshape(1, D_MODEL)
    taps = lambda a: a.reshape(1, 3 * HEAD)
    h, proj, wg, cw = _gather_proj(kidx, x[0], norm_gain, w_in, taps(conv_w))
    dproj, dx2, gwo, part_out, part_mix = _local_step(
        x[0], loss_target[0], proj, lb_logits, cw, hgrn_norm_gain, conv_norm_gain, w_out, row(final_norm_gain))
    rgrad_x, rg_w_in, rg_w_out, tot = _bwd_tail(kidx, h, dproj, wg, gwo, x[0], dx2, norm_gain, part_out, part_mix)

    (loss, (g_w_in, d_w_in, nm_w_in, nv_w_in), (g_w_out, d_w_out, nm_w_out, nv_w_out),
     (g_norm_gain, d_ng, nm_ng, nv_ng), (g_final, d_fg, nm_fg, nv_fg), (g_hgrn, d_hg, nm_hg, nv_hg),
     (g_convn, d_cg, nm_cg, nv_cg), (g_lb, d_lb, nm_lb, nv_lb), (g_conv_w, d_cw, nm_cw, nv_cw),
     grad_x) = _adamw_all(
        tot, rg_w_in, rg_w_out,
        [(w_in[0], m_w_in[0], v_w_in[0]), (w_out[0], m_w_out[0], v_w_out[0])],
        [(norm_gain, m_norm_gain, v_norm_gain),
         (row(final_norm_gain), row(m_final_norm_gain), row(v_final_norm_gain)),
         (hgrn_norm_gain, m_hgrn_norm_gain, v_hgrn_norm_gain),
         (conv_norm_gain, m_conv_norm_gain, v_conv_norm_gain),
         (lb_logits, m_lb_logits, v_lb_logits),
         (taps(conv_w), taps(m_conv_w), taps(v_conv_w))],
        rgrad_x)
    flat = lambda a: a.reshape(D_MODEL)
    untap = lambda a: a.reshape(1, 3, HEAD)
    return (loss.reshape(()), grad_x[None],
            g_norm_gain, g_w_in[None], g_lb, untap(g_conv_w), g_hgrn, g_convn, g_w_out[None], flat(g_final),
            d_ng, d_w_in[None], d_lb, untap(d_cw), d_hg, d_cg, d_w_out[None], flat(d_fg),
            nm_ng, nm_w_in[None], nm_lb, untap(nm_cw), nm_hg, nm_cg, nm_w_out[None], flat(nm_fg),
            nv_ng, nv_w_in[None], nv_lb, untap(nv_cw), nv_hg, nv_cg, nv_w_out[None], flat(nv_fg))
```

```python
import jax
import jax.numpy as jnp
import numpy as np
from jax import lax
from jax.experimental import pallas as pl
from jax.experimental.pallas import tpu as pltpu

F32 = jnp.float32
BF16 = jnp.bfloat16
MESH = pl.DeviceIdType.MESH

SEQ = 2048
D_MODEL = 1024
D_HGRN = 512
D_CONV = 512
HEAD = 128
N_HEADS = 4
CHUNK = 64
CONV_GROUP = 64
N_SHARD = 4
SHARD_COLS = 1024
WO_ROWS = 256
EPS = 1e-6
TB = 256
NCB = TB // CHUNK
N_CHUNKS = SEQ // CHUNK
N_DEV = 8
COLLECTIVE_GATHER, COLLECTIVE_MIX_OUT, COLLECTIVE_TAIL = 1, 0, 2
AUX_O, AUX_CV, AUX_B, AUX_COLS = 0, 512, 1024, 1536

ADAM_LR = 0.001
ADAM_B1 = 0.9
ADAM_B2 = 0.999
ADAM_EPS = 1e-08
ADAM_WD = 0.01
ADAM_STEP = 10

VMEM_LIMIT = 56 * 1024 * 1024


def _dot(a, b):
    return jnp.dot(a, b, preferred_element_type=F32)


def _dot_nt(a, b):
    return lax.dot_general(a, b, (((1,), (1,)), ((), ())), preferred_element_type=F32)


def _dot_tn(a, b):
    return lax.dot_general(a, b, (((0,), (0,)), ((), ())), preferred_element_type=F32)


def _split_bf16(x, n):
    parts = []
    r = x
    for _ in range(n):
        p = r.astype(BF16)
        parts.append(p)
        r = r - p.astype(F32)
    return parts


def _exact_left(m, x, n=3):
    acc = None
    for p in _split_bf16(x, n):
        t = _dot(m, p)
        acc = t if acc is None else acc + t
    return acc


def _exact_left_many(m, xs, n=3):
    parts = [_split_bf16(x, n) for x in xs]
    accs = [None] * len(xs)
    for i in range(n):
        for j in range(len(xs)):
            t = _dot(m, parts[j][i])
            accs[j] = t if accs[j] is None else accs[j] + t
    return accs


def _group_mean_many(xs, gmat, n=2):
    parts = [_split_bf16(x, n) for x in xs]
    accs = [None] * len(xs)
    for i in range(n):
        for j in range(len(xs)):
            t = _dot(parts[j][i], gmat)
            accs[j] = t if accs[j] is None else accs[j] + t
    return accs


def _group_mean(x, gmat, n=2):
    w = gmat.shape[0]
    outs = []
    for c0 in range(0, x.shape[1], w):
        acc = None
        for p in _split_bf16(x[:, c0:c0 + w], n):
            t = _dot(p, gmat)
            acc = t if acc is None else acc + t
        outs.append(acc)
    return jnp.concatenate(outs, axis=1)


def _sigmoid(x):
    return 1.0 / (1.0 + jnp.exp(-x))


def _lower_bound(lbl):
    l0 = lbl[0:1, :]
    l1 = lbl[1:2, :]
    m = jnp.maximum(l0, l1)
    e0 = jnp.exp(l0 - m)
    e1 = jnp.exp(l1 - m)
    return e0 / (e0 + e1)


def _tri(lower):
    r = lax.broadcasted_iota(jnp.int32, (CHUNK, CHUNK), 0)
    c = lax.broadcasted_iota(jnp.int32, (CHUNK, CHUNK), 1)
    return jnp.where((c <= r) if lower else (c >= r), 1.0, 0.0).astype(BF16)


def _causal():
    r = lax.broadcasted_iota(jnp.int32, (CHUNK, CHUNK), 0)
    c = lax.broadcasted_iota(jnp.int32, (CHUNK, CHUNK), 1)
    return c <= r


def _shift_down(x, sh, prev_tail):
    r = pltpu.roll(x, sh, 0)
    pt = pltpu.roll(prev_tail, sh, 0)
    rows = lax.broadcasted_iota(jnp.int32, prev_tail.shape, 0)
    top = jnp.where(rows < sh, pt, r[0:8])
    return jnp.concatenate([top, r[8:]], axis=0)


def _shift_up(x, sh, next_head):
    n = x.shape[0]
    r = pltpu.roll(x, n - sh, 0)
    nh = pltpu.roll(next_head, 8 - sh, 0)
    rows = lax.broadcasted_iota(jnp.int32, next_head.shape, 0)
    bot = jnp.where(rows >= 8 - sh, nh, r[n - 8:])
    return jnp.concatenate([r[:n - 8], bot], axis=0)


def _group_matrix(width, group):
    r = np.arange(width)[:, None] // group
    c = np.arange(width)[None, :] // group
    return jnp.asarray(np.where(r == c, 1.0 / group, 0.0), dtype=BF16)


TG = 1024
SEM_W, SEM_CW, SEM_W_FWD, N_SEM = 0, 4, 7, 11


def _gather_proj(kidx, x2d, g1, w_in, conv_w):
    half_w = D_MODEL // 2
    half_c = SHARD_COLS // 2
    nt = SEQ // TG
    n_steps = 2 * N_SHARD

    def body(k_ref, x_ref, g_ref, w_ref, cw_ref, h_ref, p_ref, wg_out, cwg_out,
             wg_v, cwg_v, send_sems, recv_sems, out_sems):
        s, t = pl.program_id(0), pl.program_id(1)
        x, y, c = lax.axis_index("x"), lax.axis_index("y"), lax.axis_index("c")
        k = 2 * x + y
        sibling = (x, y, 1 - c)
        chips = [(1 - x, y), (x, 1 - y), (1 - x, 1 - y)]
        kjs = [2 * cx + cy for cx, cy in chips]
        diag = (*chips[2], c)

        def w_half(kk, cc):
            return wg_v.at[kk, pl.ds(cc * half_w, half_w), :]

        def w_quarter(kk, cc, piece):
            return wg_v.at[kk, pl.ds(cc * half_w, half_w), piece * half_c:(piece + 1) * half_c]

        def cw_of(kk):
            return cwg_v.at[:, pl.ds(pl.multiple_of(kk * HEAD, HEAD), HEAD)]

        def copy(sem, ref, to):
            return pltpu.make_async_remote_copy(
                src_ref=ref, dst_ref=ref, send_sem=send_sems.at[sem], recv_sem=recv_sems.at[sem],
                device_id=to, device_id_type=MESH)

        def at_step(sv, tv):
            return pl.when((s == sv) & (t == tv))

        w_direct = ([copy(SEM_W + j, w_half(k, c), (*chips[j], c)) for j in range(2)]
                    + [copy(SEM_W + 2 + p, w_quarter(k, c, p), diag) for p in range(2)])
        cw_direct = [copy(SEM_CW + j, cw_of(k), (*chip, c)) for j, chip in enumerate(chips)]
        w_passed = ([copy(SEM_W_FWD + j, w_half(kjs[j], c), sibling) for j in range(2)]
                    + [copy(SEM_W_FWD + 2 + p, w_quarter(kjs[2], c, p), sibling) for p in range(2)])
        stores = ([pltpu.make_async_copy(wg_v.at[kk], wg_out.at[kk], out_sems.at[i])
                   for i, kk in enumerate([k] + kjs)]
                  + [pltpu.make_async_copy(cwg_v, cwg_out, out_sems.at[4])])

        @at_step(0, 0)
        def _():
            barrier = pltpu.get_barrier_semaphore()
            for peer in [sibling] + [(*chip, c) for chip in chips]:
                pl.semaphore_signal(barrier, inc=1, device_id=peer, device_id_type=MESH)
            wg_v[k] = w_ref[0].astype(BF16)
            mine = pl.ds(pl.multiple_of(k * HEAD, HEAD), HEAD)
            cwg_v[:, mine] = jnp.zeros((8, HEAD), F32)
            for tap in range(3):
                cwg_v[tap:tap + 1, mine] = cw_ref[:, tap * HEAD:(tap + 1) * HEAD]
            pl.semaphore_wait(barrier, 4)
            w_direct[0].start()
            w_direct[1].start()
            for cp in cw_direct:
                cp.start()
            stores[0].start()

        @at_step(2, 0)
        def _():
            for j in range(2):
                copy(SEM_W + j, w_half(kjs[j], c), sibling).wait_recv()
                w_passed[j].start()
            w_direct[2].start()
            w_direct[3].start()
            copy(SEM_W_FWD, w_half(kjs[0], 1 - c), sibling).wait_recv()
            stores[1].start()

        @at_step(4, 0)
        def _():
            copy(SEM_W_FWD + 1, w_half(kjs[1], 1 - c), sibling).wait_recv()
            stores[2].start()

        for p in range(2):
            @at_step(6 + p, 0)
            def _(p=p):
                copy(SEM_W + 2 + p, w_quarter(kjs[2], c, p), sibling).wait_recv()
                w_passed[2 + p].start()
                copy(SEM_W_FWD + 2 + p, w_quarter(kjs[2], 1 - c, p), sibling).wait_recv()

        rows = pl.ds(pl.multiple_of(t * TG, TG), TG)

        @pl.when(s == 0)
        def _():
            xv = x_ref[...]
            r = lax.rsqrt(jnp.mean(xv * xv, axis=-1, keepdims=True) + EPS)
            h_ref[rows, :] = (xv * r * g_ref[...]).astype(BF16)

        sh = s >> 1
        js = k ^ (((sh & 1) << 1) | (sh >> 1))
        for piece in range(2):
            @pl.when((s & 1) == piece)
            def _(piece=piece):
                p_ref[...] = _dot(h_ref[rows, :], wg_v[js, :, piece * half_c:(piece + 1) * half_c])

        @at_step(n_steps - 1, nt - 1)
        def _():
            stores[3].start()
            for j in range(3):
                copy(SEM_CW + j, cw_of(kjs[j]), sibling).wait_recv()
            stores[4].start()
            for cp in w_direct + cw_direct + w_passed:
                cp.wait_send()
            for st in stores:
                st.wait()

    def x_map(s, t, kr):
        return (jnp.where(s == 0, t, nt - 1), 0)

    def p_map(s, t, kr):
        sh = s >> 1
        return (t, 2 * (kr[0] ^ (((sh & 1) << 1) | (sh >> 1))) + (s & 1))

    hbm = pl.BlockSpec(memory_space=pl.ANY)
    grid_spec = pltpu.PrefetchScalarGridSpec(
        num_scalar_prefetch=1, grid=(n_steps, nt),
        in_specs=[pl.BlockSpec((TG, D_MODEL), x_map),
                  pl.BlockSpec((1, D_MODEL), lambda s, t, kr: (0, 0)),
                  pl.BlockSpec((1, D_MODEL, SHARD_COLS), lambda s, t, kr: (0, 0, 0)),
                  pl.BlockSpec((1, 3 * HEAD), lambda s, t, kr: (0, 0))],
        out_specs=(pl.BlockSpec((SEQ, D_MODEL), lambda s, t, kr: (0, 0)),
                   pl.BlockSpec((TG, half_c), p_map), hbm, hbm),
        scratch_shapes=[pltpu.VMEM((N_SHARD, D_MODEL, SHARD_COLS), BF16),
                        pltpu.VMEM((8, D_CONV), F32),
                        pltpu.SemaphoreType.DMA((N_SEM,)), pltpu.SemaphoreType.DMA((N_SEM,)),
                        pltpu.SemaphoreType.DMA((5,))])
    return pl.pallas_call(
        body, name="gather_proj", grid_spec=grid_spec,
        out_shape=(jax.ShapeDtypeStruct((SEQ, D_MODEL), BF16),
                   jax.ShapeDtypeStruct((SEQ, N_SHARD * SHARD_COLS), F32),
                   jax.ShapeDtypeStruct((N_SHARD, D_MODEL, SHARD_COLS), BF16),
                   jax.ShapeDtypeStruct((8, D_CONV), F32)),
        compiler_params=pltpu.CompilerParams(dimension_semantics=("arbitrary", "arbitrary"),
                                             vmem_limit_bytes=VMEM_LIMIT, collective_id=COLLECTIVE_GATHER),
    )(kidx, x2d, g1, w_in, conv_w)


LAG = 6


def _mix_out(proj, lb_logits, cw, ga, gcn, g64, w_out, x2d, gf, tgt):
    half_o = WO_ROWS // 2
    nblk = SEQ // TB
    n_steps = nblk + LAG

    def body(p_ref, lbl_ref, cw_ref, ga_ref, gcn_ref, g64_ref, wo_ref, x_ref, gf_ref, t_ref,
             aux_ref, sto_ref, dx2_ref, dm_ref, gwo_ref, part_ref,
             st_ref, tail_ref, wog_v, stage, ring, acc_ref, send_sems, recv_sems):
        i = pl.program_id(0)
        x, y, c = lax.axis_index("x"), lax.axis_index("y"), lax.axis_index("c")
        k = 2 * x + y
        sibling = (x, y, 1 - c)
        chips = [(1 - x, y), (x, 1 - y), (1 - x, 1 - y)]
        kjs = [2 * cx + cy for cx, cy in chips]

        def wo_half(kk, cc):
            return wog_v.at[pl.ds(pl.multiple_of(kk * WO_ROWS + cc * half_o, half_o), half_o), :]

        def copy(sem, ref, to):
            return pltpu.make_async_remote_copy(
                src_ref=ref, dst_ref=ref, send_sem=send_sems.at[sem], recv_sem=recv_sems.at[sem],
                device_id=to, device_id_type=MESH)

        wo_direct = [copy(j, wo_half(k, c), (*chip, c)) for j, chip in enumerate(chips)]
        wo_passed = [copy(3 + j, wo_half(kj, c), sibling) for j, kj in enumerate(kjs)]

        @pl.when(i == 0)
        def _():
            barrier = pltpu.get_barrier_semaphore()
            for peer in [sibling] + [(*chip, c) for chip in chips]:
                pl.semaphore_signal(barrier, inc=1, device_id=peer, device_id_type=MESH)
            st_ref[...] = jnp.zeros_like(st_ref)
            tail_ref[...] = jnp.zeros_like(tail_ref)
            acc_ref[...] = jnp.zeros_like(acc_ref)
            part_ref[...] = jnp.zeros_like(part_ref)
            wog_v[pl.ds(pl.multiple_of(k * WO_ROWS, WO_ROWS), WO_ROWS), :] = wo_ref[0].astype(BF16)
            pl.semaphore_wait(barrier, 4)
            for cp in wo_direct:
                cp.start()

        @pl.when(i == LAG - 1)
        def _():
            for j in range(3):
                copy(j, wo_half(kjs[j], c), sibling).wait_recv()
                wo_passed[j].start()

        @pl.when(i == LAG)
        def _():
            for j in range(3):
                copy(3 + j, wo_half(kjs[j], 1 - c), sibling).wait_recv()

        lb = _lower_bound(lbl_ref[...])
        tri = _tri(True)
        causal = _causal()
        g64m = g64_ref[...]
        heads = range(N_HEADS)
        cs = [slice(hd * HEAD, (hd + 1) * HEAD) for hd in heads]
        col = lambda base, hd: slice(base + hd * HEAD, base + (hd + 1) * HEAD)

        def mix_chunk(n):
            sl = pl.ds(n * CHUNK, CHUNK)
            sg = [_sigmoid(p_ref[sl, col(512, hd)]) for hd in heads]
            f = [lb[:, cs[hd]] + (1.0 - lb[:, cs[hd]]) * sg[hd] for hd in heads]
            bc = _exact_left_many(tri, [jnp.log(f[hd]) for hd in heads])
            for hd in heads:
                aux_ref[sl, col(AUX_B, hd)] = bc[hd]
            g = [bc[hd][CHUNK - 1:CHUNK, :] for hd in heads]
            qd = [(p_ref[sl, col(0, hd)] * jnp.exp(bc[hd])).astype(BF16) for hd in heads]
            kk = [1.0 - f[hd] for hd in heads]
            ki = [(kk[hd] * jnp.exp(-bc[hd])).astype(BF16) for hd in heads]
            ke = [(kk[hd] * jnp.exp(g[hd] - bc[hd])).astype(BF16) for hd in heads]
            vb = [p_ref[sl, col(1024, hd)].astype(BF16) for hd in heads]
            st = [st_ref[hd] for hd in heads]
            st_b = [a.astype(BF16) for a in st]
            for hd in heads:
                sto_ref[n, hd] = st_b[hd]
            scm = [_dot_nt(qd[hd], ki[hd]) for hd in heads]
            inter = [_dot_nt(qd[hd], st_b[hd]) for hd in heads]
            upd = [_dot_tn(vb[hd], ke[hd]) for hd in heads]
            intra = [_dot(jnp.where(causal, scm[hd], 0.0).astype(BF16), vb[hd]) for hd in heads]
            for hd in heads:
                st_ref[hd] = st[hd] * jnp.exp(g[hd]) + upd[hd]
                o = intra[hd] + inter[hd]
                aux_ref[sl, col(AUX_O, hd)] = o
                ra = lax.rsqrt(jnp.mean(o * o, axis=-1, keepdims=True) + EPS)
                za = p_ref[sl, col(1536, hd)]
                stage[sl, cs[hd]] = (o * ra * ga_ref[:, cs[hd]] * (za * _sigmoid(za))).astype(BF16)
            yb = []
            for hd in heads:
                cu = p_ref[sl, col(3072, hd)] * p_ref[sl, col(2048, hd)]
                tail = tail_ref[:, cs[hd]]
                cv = (cw_ref[0:1, cs[hd]] * _shift_down(cu, 2, tail) + cw_ref[1:2, cs[hd]] * _shift_down(cu, 1, tail)
                      + cw_ref[2:3, cs[hd]] * cu)
                tail_ref[:, cs[hd]] = cu[CHUNK - 8:, :]
                aux_ref[sl, col(AUX_CV, hd)] = cv
                yb.append(p_ref[sl, col(2560, hd)] * cv)
            ms = _group_mean_many([y * y for y in yb], g64m)
            for hd in heads:
                rb = lax.rsqrt(ms[hd] + EPS)
                zb = p_ref[sl, col(3584, hd)]
                stage[sl, col(512, hd)] = (yb[hd] * rb * gcn_ref[:, cs[hd]] * (zb * _sigmoid(zb))).astype(BF16)

        def step(mix, project):
            if project:
                mixed_b = ring[pl.ds(pl.multiple_of((i - LAG) * TB, TB), TB), :]
                y = _dot(mixed_b, wog_v[...])
            if mix:
                mix_chunk(0)
            if project:
                x2 = x_ref[...] + y
                r2 = lax.rsqrt(jnp.mean(x2 * x2, axis=-1, keepdims=True) + EPS)
                n2 = x2 * r2
                gfv = gf_ref[...]
                err = n2 * gfv - t_ref[...]
                loss = 0.5 * jnp.sum(jnp.mean(err * err, axis=-1, keepdims=True), axis=0, keepdims=True)
                dy = err * (1.0 / D_MODEL)
                part_ref[1:2, :] += jnp.sum(dy * n2, axis=0, keepdims=True)
                part_ref[7:8, :] += jnp.broadcast_to(loss, (1, D_MODEL))
                dn = dy * gfv
                dx2 = r2 * (dn - n2 * jnp.mean(dn * n2, axis=-1, keepdims=True))
                dx2_ref[...] = dx2
                dx2_b = dx2.astype(BF16)
            if mix:
                mix_chunk(1)
            if project:
                dm_ref[...] = _dot_nt(dx2_b, wog_v[...])
            if mix:
                mix_chunk(2)
            if project:
                acc_ref[...] += _dot_tn(mixed_b, dx2_b)
            if mix:
                mix_chunk(3)
                ring[pl.ds(pl.multiple_of(i * TB, TB), TB), :] = stage[...]

        @pl.when(i < LAG)
        def _():
            step(True, False)

        @pl.when((i >= LAG) & (i < nblk))
        def _():
            step(True, True)

        @pl.when(i >= nblk)
        def _():
            step(False, True)

        @pl.when(i == n_steps - 1)
        def _():
            gwo_ref[...] = acc_ref[...].astype(BF16)
            for cp in wo_direct + wo_passed:
                cp.wait_send()

    assert NCB == 4
    row = lambda w: pl.BlockSpec((1, w), lambda i: (0, 0))
    mix_blk = lambda i: jnp.minimum(i, nblk - 1)
    out_blk = lambda i: jnp.clip(i - LAG, 0, nblk - 1)
    tok = lambda: pl.BlockSpec((TB, D_MODEL), lambda i: (out_blk(i), 0))
    return pl.pallas_call(
        body, name="mix_out", grid=(n_steps,),
        out_shape=(jax.ShapeDtypeStruct((SEQ, AUX_COLS), F32),
                   jax.ShapeDtypeStruct((N_CHUNKS, N_HEADS, HEAD, HEAD), BF16),
                   jax.ShapeDtypeStruct((SEQ, D_MODEL), F32),
                   jax.ShapeDtypeStruct((SEQ, D_MODEL), F32),
                   jax.ShapeDtypeStruct((D_MODEL, D_MODEL), BF16),
                   jax.ShapeDtypeStruct((8, D_MODEL), F32)),
        in_specs=[pl.BlockSpec((TB, 4096), lambda i: (jnp.minimum(i, nblk - 1), 0)),
                  pl.BlockSpec((2, D_HGRN), lambda i: (0, 0)),
                  pl.BlockSpec((8, D_CONV), lambda i: (0, 0)),
                  row(D_HGRN), row(D_CONV),
                  pl.BlockSpec((HEAD, HEAD), lambda i: (0, 0)),
                  pl.BlockSpec((1, WO_ROWS, D_MODEL), lambda i: (0, 0, 0)),
                  tok(), row(D_MODEL), tok()],
        out_specs=(pl.BlockSpec((TB, AUX_COLS), lambda i: (mix_blk(i), 0)),
                   pl.BlockSpec((NCB, N_HEADS, HEAD, HEAD), lambda i: (mix_blk(i), 0, 0, 0)),
                   tok(), tok(),
                   pl.BlockSpec((D_MODEL, D_MODEL), lambda i: (0, 0)),
                   pl.BlockSpec((8, D_MODEL), lambda i: (0, 0))),
        scratch_shapes=[pltpu.VMEM((N_HEADS, HEAD, HEAD), F32), pltpu.VMEM((8, D_CONV), F32),
                        pltpu.VMEM((D_MODEL, D_MODEL), BF16), pltpu.VMEM((TB, D_MODEL), BF16),
                        pltpu.VMEM((SEQ, D_MODEL), BF16), pltpu.VMEM((D_MODEL, D_MODEL), F32),
                        pltpu.SemaphoreType.DMA((6,)), pltpu.SemaphoreType.DMA((6,))],
        compiler_params=pltpu.CompilerParams(dimension_semantics=("arbitrary",), vmem_limit_bytes=VMEM_LIMIT,
                                             collective_id=COLLECTIVE_MIX_OUT),
    )(proj, lb_logits, cw, ga, gcn, g64, w_out, x2d, gf, tgt)


def _mix_bwd(proj, aux, states, dmixed, lb_logits, cw, ga, gcn, g64):
    nblk = SEQ // TB

    def body(p_ref, aux_ref, st_ref, dm_ref, lbl_ref, cw_ref, ga_ref, gcn_ref, g64_ref,
             dp_ref, part_ref, dst_ref, head_ref, dlb_ref):
        i = pl.program_id(0)

        @pl.when(i == 0)
        def _():
            dst_ref[...] = jnp.zeros_like(dst_ref)
            head_ref[...] = jnp.zeros_like(head_ref)
            part_ref[...] = jnp.zeros_like(part_ref)
            dlb_ref[...] = jnp.zeros_like(dlb_ref)

        lb = _lower_bound(lbl_ref[...])
        triu = _tri(False)
        causal = _causal()
        g64m = g64_ref[...]
        rowsum = lambda a: jnp.sum(a, axis=0, keepdims=True)
        heads = range(N_HEADS)
        cs = [slice(hd * HEAD, (hd + 1) * HEAD) for hd in heads]
        col = lambda base, hd: slice(base + hd * HEAD, base + (hd + 1) * HEAD)
        for n in reversed(range(NCB)):
            sl = pl.ds(n * CHUNK, CHUNK)
            cvv = [aux_ref[sl, col(AUX_CV, hd)] for hd in heads]
            gb = [p_ref[sl, col(2560, hd)] for hd in heads]
            yb = [gb[hd] * cvv[hd] for hd in heads]
            ms = _group_mean_many([y * y for y in yb], g64m)
            rb, nb, dnb = [], [], []
            for hd in heads:
                rb.append(lax.rsqrt(ms[hd] + EPS))
                nb.append(yb[hd] * rb[hd])
                zb = p_ref[sl, col(3584, hd)]
                sgb = _sigmoid(zb)
                dmb = dm_ref[sl, col(512, hd)]
                silu = zb * sgb
                dgate = dmb * gcn_ref[:, cs[hd]]
                part_ref[2:3, col(512, hd)] += rowsum(dmb * nb[hd] * silu)
                dp_ref[sl, col(3584, hd)] = (dgate * nb[hd] * (sgb + silu * (1.0 - sgb))).astype(BF16)
                dnb.append(dgate * silu)
            mdn = _group_mean_many([dnb[hd] * nb[hd] for hd in heads], g64m)
            for hd in heads:
                dyb = rb[hd] * (dnb[hd] - nb[hd] * mdn[hd])
                dp_ref[sl, col(2560, hd)] = (dyb * cvv[hd]).astype(BF16)
                dcv = dyb * gb[hd]
                head = head_ref[:, cs[hd]]
                dcv1 = _shift_up(dcv, 1, head)
                dcv2 = _shift_up(dcv, 2, head)
                head_ref[:, cs[hd]] = dcv[0:8, :]
                u = p_ref[sl, col(2048, hd)]
                gc = p_ref[sl, col(3072, hd)]
                cu = gc * u
                part_ref[4:5, cs[hd]] += rowsum(dcv2 * cu)
                part_ref[5:6, cs[hd]] += rowsum(dcv1 * cu)
                part_ref[6:7, cs[hd]] += rowsum(dcv * cu)
                dcu = cw_ref[2:3, cs[hd]] * dcv + cw_ref[1:2, cs[hd]] * dcv1 + cw_ref[0:1, cs[hd]] * dcv2
                dp_ref[sl, col(3072, hd)] = (dcu * u).astype(BF16)
                dp_ref[sl, col(2048, hd)] = (dcu * gc).astype(BF16)
            do_b = []
            for hd in heads:
                ov = aux_ref[sl, col(AUX_O, hd)]
                ra = lax.rsqrt(jnp.mean(ov * ov, axis=-1, keepdims=True) + EPS)
                na = ov * ra
                za = p_ref[sl, col(1536, hd)]
                sga = _sigmoid(za)
                dma = dm_ref[sl, cs[hd]]
                silu = za * sga
                dgate = dma * ga_ref[:, cs[hd]]
                part_ref[2:3, cs[hd]] += rowsum(dma * na * silu)
                dp_ref[sl, col(1536, hd)] = (dgate * na * (sga + silu * (1.0 - sga))).astype(BF16)
                dna = dgate * silu
                do_b.append((ra * (dna - na * jnp.mean(dna * na, axis=-1, keepdims=True))).astype(BF16))
            s = [_sigmoid(p_ref[sl, col(512, hd)]) for hd in heads]
            f = [lb[:, cs[hd]] + (1.0 - lb[:, cs[hd]]) * s[hd] for hd in heads]
            bc = [aux_ref[sl, col(AUX_B, hd)] for hd in heads]
            g = [bc[hd][CHUNK - 1:CHUNK, :] for hd in heads]
            eb = [jnp.exp(bc[hd]) for hd in heads]
            enb = [jnp.exp(-bc[hd]) for hd in heads]
            eg = [jnp.exp(g[hd] - bc[hd]) for hd in heads]
            dec = [jnp.exp(g[hd]) for hd in heads]
            qd = [p_ref[sl, cs[hd]] * eb[hd] for hd in heads]
            kk = [1.0 - f[hd] for hd in heads]
            ki = [kk[hd] * enb[hd] for hd in heads]
            ke = [kk[hd] * eg[hd] for hd in heads]
            qd_b = [a.astype(BF16) for a in qd]
            ki_b = [a.astype(BF16) for a in ki]
            ke_b = [a.astype(BF16) for a in ke]
            vb = [p_ref[sl, col(1024, hd)].astype(BF16) for hd in heads]
            st_b = [st_ref[n, hd] for hd in heads]
            dst = [dst_ref[hd] for hd in heads]
            dst_b = [a.astype(BF16) for a in dst]
            scm = [_dot_nt(qd_b[hd], ki_b[hd]) for hd in heads]
            amm = [_dot_nt(do_b[hd], vb[hd]) for hd in heads]
            dqd2 = [_dot(do_b[hd], st_b[hd]) for hd in heads]
            dke = [_dot(vb[hd], dst_b[hd]) for hd in heads]
            dv2 = [_dot_nt(ke_b[hd], dst_b[hd]) for hd in heads]
            dsu = [_dot_tn(do_b[hd], qd_b[hd]) for hd in heads]
            sc = [jnp.where(causal, scm[hd], 0.0).astype(BF16) for hd in heads]
            am = [jnp.where(causal, amm[hd], 0.0).astype(BF16) for hd in heads]
            dqd1 = [_dot(am[hd], ki_b[hd]) for hd in heads]
            dki = [_dot_tn(am[hd], qd_b[hd]) for hd in heads]
            dv1 = [_dot_tn(sc[hd], do_b[hd]) for hd in heads]
            db, dgv, dkk = [], [], []
            for hd in heads:
                dqd = dqd1[hd] + dqd2[hd]
                ddec = rowsum(dst[hd] * st_b[hd].astype(F32))
                dst_ref[hd] = dst[hd] * dec[hd] + dsu[hd]
                dp_ref[sl, cs[hd]] = (dqd * eb[hd]).astype(BF16)
                dp_ref[sl, col(1024, hd)] = (dv1[hd] + dv2[hd]).astype(BF16)
                dke_eg = dke[hd] * eg[hd]
                dkk.append(dki[hd] * enb[hd] + dke_eg)
                db.append(dqd * qd[hd] - kk[hd] * dkk[hd])
                dgv.append(rowsum(kk[hd] * dke_eg) + ddec * dec[hd])
            rc = _exact_left_many(triu, db, 2)
            for hd in heads:
                df = (rc[hd] + dgv[hd]) / f[hd] - dkk[hd]
                one_s = 1.0 - s[hd]
                dlb_ref[:, cs[hd]] += rowsum(df * one_s)
                dp_ref[sl, col(512, hd)] = (df * (1.0 - lb[:, cs[hd]]) * s[hd] * one_s).astype(BF16)

        @pl.when(i == nblk - 1)
        def _():
            row = dlb_ref[...] * lb * (1.0 - lb)
            part_ref[3:4, 0:D_HGRN] = row
            part_ref[3:4, D_HGRN:] = -row

    rev = lambda w: pl.BlockSpec((TB, w), lambda i: (nblk - 1 - i, 0))
    row = lambda w: pl.BlockSpec((1, w), lambda i: (0, 0))
    return pl.pallas_call(
        body, name="mix_bwd", grid=(nblk,),
        out_shape=(jax.ShapeDtypeStruct((SEQ, 4096), BF16),
                   jax.ShapeDtypeStruct((8, D_MODEL), F32)),
        in_specs=[rev(4096), rev(AUX_COLS),
                  pl.BlockSpec((NCB, N_HEADS, HEAD, HEAD), lambda i: (nblk - 1 - i, 0, 0, 0)),
                  rev(D_MODEL),
                  pl.BlockSpec((2, D_HGRN), lambda i: (0, 0)),
                  pl.BlockSpec((8, D_CONV), lambda i: (0, 0)),
                  row(D_HGRN), row(D_CONV),
                  pl.BlockSpec((HEAD, HEAD), lambda i: (0, 0))],
        out_specs=(rev(4096), pl.BlockSpec((8, D_MODEL), lambda i: (0, 0))),
        scratch_shapes=[pltpu.VMEM((N_HEADS, HEAD, HEAD), F32), pltpu.VMEM((8, D_CONV), F32),
                        pltpu.VMEM((1, D_HGRN), F32)],
        compiler_params=pltpu.CompilerParams(dimension_semantics=("arbitrary",), vmem_limit_bytes=VMEM_LIMIT),
    )(proj, aux, states, dmixed, lb_logits, cw, ga, gcn, g64)


TT = 1024
TX = 256
(SEM_D2D, SEM_D2D_O, SEM_ICI, SEM_ICI_O, SEM_FIN, SEM_FIN_O, SEM_SMALL, N_SEM_TAIL) = 0, 4, 5, 8, 11, 12, 12, 20


def _bwd_tail(kidx, h, dproj, wg, gwo, x2d, dx2, g1, small_a, small_b):
    hw = D_MODEL // 2
    ho = WO_ROWS // 2
    nt = SEQ // TT
    norm_step = 2 * N_SHARD
    n_steps = norm_step + SEQ // TX // nt

    def body(k_ref, h_ref, dp_ref, w_ref, gwo_ref, x_ref, dx2_ref, g_ref, sm_ref, smb_ref,
             gx_ref, gw_out, gwo_out, osm_ref,
             acc, dh, sendbuf, keep, sibrcv, rcv, sib_o, p_o, rcv_o, res_o, sm_buf, dng,
             send_sems, recv_sems, out_sems):
        s, t = pl.program_id(0), pl.program_id(1)
        x, y, c = lax.axis_index("x"), lax.axis_index("y"), lax.axis_index("c")
        k = 2 * x + y
        me = 4 * x + 2 * y + c
        sibling = (x, y, 1 - c)
        chips = [(1 - x, 1 - y), (1 - x, y), (x, 1 - y)]
        kjs = [2 * cx + cy for cx, cy in chips]
        mine = pl.ds(pl.multiple_of(c * hw, hw), hw)
        other = pl.ds(pl.multiple_of((1 - c) * hw, hw), hw)
        mine_o = pl.ds(pl.multiple_of(c * ho, ho), ho)
        other_o = pl.ds(pl.multiple_of((1 - c) * ho, ho), ho)

        def copy(sem, src, dst, to):
            return pltpu.make_async_remote_copy(
                src_ref=src, dst_ref=dst, send_sem=send_sems.at[sem], recv_sem=recv_sems.at[sem],
                device_id=to, device_id_type=MESH)

        def at_step(sv, tv):
            return pl.when((s == sv) & (t == tv))

        def at_norm_block(b):
            return at_step(norm_step + b // nt, b % nt)

        d2d = [copy(SEM_D2D + sv, sendbuf.at[sv], sibrcv.at[sv], sibling) for sv in range(N_SHARD)]
        d2d_o = copy(SEM_D2D_O, gwo_ref.at[:, other_o, :], sib_o, sibling)
        ici = [copy(SEM_ICI + sv, keep.at[sv], rcv.at[sv], (*chips[sv], c)) for sv in range(3)]
        ici_o = [copy(SEM_ICI_O + sv, p_o.at[kjs[sv]], rcv_o.at[sv], (*chips[sv], c)) for sv in range(3)]
        fin = copy(SEM_FIN, acc.at[mine, :], gw_out.at[mine, :], sibling)
        fin_o = copy(SEM_FIN_O, res_o.at[mine_o, :], res_o.at[mine_o, :], sibling)
        smalls = [copy(SEM_SMALL + m, sm_buf.at[me], sm_buf.at[me],
                       (x ^ (m >> 2), y ^ ((m >> 1) & 1), c ^ (m & 1))) for m in range(1, N_DEV)]
        store_w = pltpu.make_async_copy(acc.at[mine, :], gw_out.at[mine, :], out_sems.at[0])
        store_o = pltpu.make_async_copy(res_o, gwo_out, out_sems.at[1])

        @at_step(0, 0)
        def _():
            barrier = pltpu.get_barrier_semaphore()
            for m in range(1, N_DEV):
                pl.semaphore_signal(barrier, inc=1, device_id=(x ^ (m >> 2), y ^ ((m >> 1) & 1), c ^ (m & 1)),
                                    device_id_type=MESH)
            pl.semaphore_wait(barrier, N_DEV - 1)
            d2d_o.start()

        @at_step(0, 1)
        def _():
            d2d_o.wait_recv()
            for j in range(N_SHARD):
                p_o[j] = (gwo_ref[j, mine_o, :].astype(F32) + sib_o[j].astype(F32)).astype(BF16)
            res_o[mine_o, :] = gwo_ref[k, mine_o, :].astype(F32) + sib_o[k].astype(F32)
            for cp in ici_o:
                cp.start()

        rows = pl.ds(pl.multiple_of(t * TT, TT), TT)

        @pl.when((s < N_SHARD) & (t == 0))
        def _():
            acc[...] = _dot_tn(h_ref[...], dp_ref[...])

        @pl.when((s < N_SHARD) & (t > 0))
        def _():
            acc[...] += _dot_tn(h_ref[...], dp_ref[...])

        for sv in range(N_SHARD):
            @at_step(sv, nt - 1)
            def _(sv=sv):
                sendbuf[sv] = acc[other, :].astype(BF16)
                if sv < 3:
                    keep[sv] = acc[mine, :].astype(BF16)
                d2d[sv].start()

        for sv in range(3):
            @at_step(sv + 1, 0)
            def _(sv=sv):
                d2d[sv].wait_recv()
                keep[sv] = (keep[sv].astype(F32) + sibrcv[sv].astype(F32)).astype(BF16)
                ici[sv].start()

        @pl.when(s == N_SHARD)
        def _():
            dh[rows, :] = _dot_nt(dp_ref[...], w_ref[0])

        @pl.when((s > N_SHARD) & (s < norm_step))
        def _():
            dh[rows, :] += _dot_nt(dp_ref[...], w_ref[0])

        @at_norm_block(0)
        def _():
            d2d[3].wait_recv()
            ici[0].wait_recv()
            acc[mine, :] += sibrcv[3].astype(F32) + rcv[0].astype(F32)

        @at_norm_block(1)
        def _():
            tot = res_o[mine_o, :]
            for sv in range(3):
                ici_o[sv].wait_recv()
                tot = tot + rcv_o[sv].astype(F32)
            res_o[mine_o, :] = tot
            fin_o.start()

        @at_norm_block(2)
        def _():
            ici[1].wait_recv()
            acc[mine, :] += rcv[1].astype(F32)

        @at_norm_block(0)
        def _():
            dng[...] = jnp.zeros_like(dng)

        @pl.when(s >= norm_step)
        def _():
            blk = (s - norm_step) * nt + t
            dhv = dh[pl.ds(pl.multiple_of(blk * TX, TX), TX), :]
            xv = x_ref[...]
            r = lax.rsqrt(jnp.mean(xv * xv, axis=-1, keepdims=True) + EPS)
            xn = xv * r
            dng[...] += jnp.sum(dhv * xn, axis=0, keepdims=True)
            dxn = dhv * g_ref[...]
            gx_ref[...] = dx2_ref[...] + r * (dxn - xn * jnp.mean(dxn * xn, axis=-1, keepdims=True))

        @at_step(n_steps - 1, nt - 1)
        def _():
            sm_buf[me] = sm_ref[...] + smb_ref[...]
            sm_buf[me, 0:1, :] = dng[...]
            for cp in smalls:
                cp.start()
            ici[2].wait_recv()
            acc[mine, :] += rcv[2].astype(F32)
            fin.start()
            store_w.start()
            for m in range(1, N_DEV):
                copy(SEM_SMALL + m, sm_buf.at[0], sm_buf.at[0], sibling).wait_recv()
            tot = sm_buf[0]
            for d in range(1, N_DEV):
                tot = tot + sm_buf[d]
            osm_ref[...] = tot
            fin_o.wait_recv()
            store_o.start()
            fin.wait_recv()
            for cp in d2d + [d2d_o] + ici + ici_o + [fin, fin_o] + smalls:
                cp.wait_send()
            store_o.wait()
            store_w.wait()

    def shard_of(s, kr):
        order = jnp.where(s < N_SHARD, s, jnp.where(s < norm_step, s - N_SHARD, 3))
        return kr[0] ^ (3 - order)

    def h_map(s, t, kr):
        return (jnp.where(s < N_SHARD, t, nt - 1), 0)

    def dp_map(s, t, kr):
        return (jnp.where(s < norm_step, t, nt - 1), shard_of(s, kr))

    def w_map(s, t, kr):
        return (shard_of(jnp.maximum(s, N_SHARD), kr), 0, 0)

    def blk_map(s, t, kr):
        return (jnp.where(s < norm_step, 0, (s - norm_step) * nt + t), 0)

    hbm = pl.BlockSpec(memory_space=pl.ANY)
    grid_spec = pltpu.PrefetchScalarGridSpec(
        num_scalar_prefetch=1, grid=(n_steps, nt),
        in_specs=[pl.BlockSpec((TT, D_MODEL), h_map),
                  pl.BlockSpec((TT, SHARD_COLS), dp_map),
                  pl.BlockSpec((1, D_MODEL, SHARD_COLS), w_map),
                  pl.BlockSpec((N_SHARD, WO_ROWS, D_MODEL), lambda s, t, kr: (0, 0, 0)),
                  pl.BlockSpec((TX, D_MODEL), blk_map),
                  pl.BlockSpec((TX, D_MODEL), blk_map),
                  pl.BlockSpec((1, D_MODEL), lambda s, t, kr: (0, 0)),
                  pl.BlockSpec((8, D_MODEL), lambda s, t, kr: (0, 0)),
                  pl.BlockSpec((8, D_MODEL), lambda s, t, kr: (0, 0))],
        out_specs=(pl.BlockSpec((TX, D_MODEL), blk_map), hbm, hbm,
                   pl.BlockSpec((8, D_MODEL), lambda s, t, kr: (0, 0))),
        scratch_shapes=[pltpu.VMEM((D_MODEL, SHARD_COLS), F32), pltpu.VMEM((SEQ, D_MODEL), F32),
                        pltpu.VMEM((N_SHARD, hw, SHARD_COLS), BF16), pltpu.VMEM((3, hw, SHARD_COLS), BF16),
                        pltpu.VMEM((N_SHARD, hw, SHARD_COLS), BF16), pltpu.VMEM((3, hw, SHARD_COLS), BF16),
                        pltpu.VMEM((N_SHARD, ho, D_MODEL), BF16), pltpu.VMEM((N_SHARD, ho, D_MODEL), BF16),
                        pltpu.VMEM((3, ho, D_MODEL), BF16), pltpu.VMEM((WO_ROWS, D_MODEL), F32),
                        pltpu.VMEM((N_DEV, 8, D_MODEL), F32), pltpu.VMEM((1, D_MODEL), F32),
                        pltpu.SemaphoreType.DMA((N_SEM_TAIL,)), pltpu.SemaphoreType.DMA((N_SEM_TAIL,)),
                        pltpu.SemaphoreType.DMA((2,))])
    return pl.pallas_call(
        body, name="bwd_tail", grid_spec=grid_spec,
        out_shape=(jax.ShapeDtypeStruct((SEQ, D_MODEL), F32),
                   jax.ShapeDtypeStruct((D_MODEL, SHARD_COLS), F32),
                   jax.ShapeDtypeStruct((WO_ROWS, D_MODEL), F32),
                   jax.ShapeDtypeStruct((8, D_MODEL), F32)),
        compiler_params=pltpu.CompilerParams(dimension_semantics=("arbitrary", "arbitrary"),
                                             vmem_limit_bytes=60 * 1024 * 1024, collective_id=COLLECTIVE_TAIL),
    )(kidx, h, dproj, wg, gwo, x2d, dx2, g1, small_a, small_b)


def _adam_update(w, g, m, v):
    nm = ADAM_B1 * m + (1.0 - ADAM_B1) * g
    nv = ADAM_B2 * v + (1.0 - ADAM_B2) * (g * g)
    m_hat = nm / (1.0 - ADAM_B1 ** ADAM_STEP)
    v_hat = nv / (1.0 - ADAM_B2 ** ADAM_STEP)
    return -ADAM_LR * (m_hat / (jnp.sqrt(v_hat) + ADAM_EPS) + ADAM_WD * w), nm, nv


def _adamw_all(tot, g_w_in, g_w_out, big, small, grad_x):
    n = len(small)
    rows = WO_ROWS
    steps = D_MODEL // rows

    def body(tot_ref, *refs):
        gx_ref, gx_out = refs[2 + 3 * (2 + n)], refs[-1]
        gx_out[...] = gx_ref[...]
        ins, outs = refs[:2 + 3 * (2 + n)], refs[3 + 3 * (2 + n):-1]
        g_refs, wmv = ins[:2], ins[2:]
        loss_ref, quads = outs[0], outs[1:]

        def update(j, g):
            w_ref, m_ref, v_ref = wmv[3 * j:3 * j + 3]
            g_ref, d_ref, nm_ref, nv_ref = quads[4 * j:4 * j + 4]
            g_ref[...] = g
            d_ref[...], nm_ref[...], nv_ref[...] = _adam_update(w_ref[...], g, m_ref[...], v_ref[...])

        update(0, g_refs[0][...])

        @pl.when(pl.program_id(0) == 0)
        def _():
            update(1, g_refs[1][...])
            k = 2 * lax.axis_index("x") + lax.axis_index("y")
            mine = pl.ds(pl.multiple_of(k * HEAD, HEAD), HEAD)
            loss_ref[...] = tot_ref[7:8, 0:1]
            grads = [tot_ref[0:1, :], tot_ref[1:2, :], tot_ref[2:3, 0:D_HGRN], tot_ref[2:3, D_HGRN:],
                     jnp.concatenate([tot_ref[3:4, 0:D_HGRN], tot_ref[3:4, D_HGRN:]], axis=0),
                     jnp.concatenate([tot_ref[4 + tap:5 + tap, mine] for tap in range(3)], axis=1)]
            for j, g in enumerate(grads):
                update(2 + j, g)

    whole = lambda a: pl.BlockSpec(a.shape, lambda i: (0, 0))
    blk = pl.BlockSpec((rows, SHARD_COLS), lambda i: (i, 0))
    arrays = [a for triple in big + small for a in triple]
    in_specs = ([whole(tot), blk, whole(g_w_out)] + [blk] * 3 + [whole(a) for a in arrays[3:]])
    shapes = [big[0][0], big[1][0]] + [w for w, _, _ in small]
    out_shape = (jax.ShapeDtypeStruct((1, 1), F32),) + tuple(
        jax.ShapeDtypeStruct(w.shape, F32) for w in shapes for _ in range(4))
    out_specs = (pl.BlockSpec((1, 1), lambda i: (0, 0)),) + (blk,) * 4 + tuple(
        whole(w) for w in shapes[1:] for _ in range(4))
    gx_blk = pl.BlockSpec((SEQ // steps, D_MODEL), lambda i: (i, 0))
    outs = pl.pallas_call(
        body, name="adamw_all", grid=(steps,),
        out_shape=out_shape + (jax.ShapeDtypeStruct(grad_x.shape, F32),),
        in_specs=in_specs + [gx_blk], out_specs=out_specs + (gx_blk,),
        compiler_params=pltpu.CompilerParams(dimension_semantics=("arbitrary",), vmem_limit_bytes=VMEM_LIMIT),
    )(tot, g_w_in, g_w_out, *arrays, grad_x)
    return [outs[0]] + [outs[1 + 4 * j:5 + 4 * j] for j in range(2 + n)] + [outs[-1]]


def _local_step(x2d, tgt, proj, lb_logits, cw, ga, gcn, w_out, gf):
    g64 = _group_matrix(HEAD, CONV_GROUP)
    aux, states, dx2, dmixed, gwo, part_out = _mix_out(proj, lb_logits, cw, ga, gcn, g64, w_out, x2d, gf, tgt)
    dproj, part_mix = _mix_bwd(proj, aux, states, dmixed, lb_logits, cw, ga, gcn, g64)
    return dproj, dx2, gwo.reshape(N_SHARD, WO_ROWS, D_MODEL), part_out, part_mix


def kernel(x, norm_gain, w_in, lb_logits, conv_w, hgrn_norm_gain, conv_norm_gain, w_out, final_norm_gain, loss_target, m_norm_gain, m_w_in, m_lb_logits, m_conv_w, m_hgrn_norm_gain, m_conv_norm_gain, m_w_out, m_final_norm_gain, v_norm_gain, v_w_in, v_lb_logits, v_conv_w, v_hgrn_norm_gain, v_conv_norm_gain, v_w_out, v_final_norm_gain):
    k = 2 * lax.axis_index("x") + lax.axis_index("y")
    kidx = jnp.reshape(k, (1,)).astype(jnp.int32)
    row = lambda a: a.reshape(1, D_MODEL)
    taps = lambda a: a.reshape(1, 3 * HEAD)
    h, proj, wg, cw = _gather_proj(kidx, x[0], norm_gain, w_in, taps(conv_w))
    dproj, dx2, gwo, part_out, part_mix = _local_step(
        x[0], loss_target[0], proj, lb_logits, cw, hgrn_norm_gain, conv_norm_gain, w_out, row(final_norm_gain))
    rgrad_x, rg_w_in, rg_w_out, tot = _bwd_tail(kidx, h, dproj, wg, gwo, x[0], dx2, norm_gain, part_out, part_mix)

    (loss, (g_w_in, d_w_in, nm_w_in, nv_w_in), (g_w_out, d_w_out, nm_w_out, nv_w_out),
     (g_norm_gain, d_ng, nm_ng, nv_ng), (g_final, d_fg, nm_fg, nv_fg), (g_hgrn, d_hg, nm_hg, nv_hg),
     (g_convn, d_cg, nm_cg, nv_cg), (g_lb, d_lb, nm_lb, nv_lb), (g_conv_w, d_cw, nm_cw, nv_cw),
     grad_x) = _adamw_all(
        tot, rg_w_in, rg_w_out,
        [(w_in[0], m_w_in[0], v_w_in[0]), (w_out[0], m_w_out[0], v_w_out[0])],
        [(norm_gain, m_norm_gain, v_norm_gain),
         (row(final_norm_gain), row(m_final_norm_gain), row(v_final_norm_gain)),
         (hgrn_norm_gain, m_hgrn_norm_gain, v_hgrn_norm_gain),
         (conv_norm_gain, m_conv_norm_gain, v_conv_norm_gain),
         (lb_logits, m_lb_logits, v_lb_logits),
         (taps(conv_w), taps(m_conv_w), taps(v_conv_w))],
        rgrad_x)
    flat = lambda a: a.reshape(D_MODEL)
    untap = lambda a: a.reshape(1, 3, HEAD)
    return (loss.reshape(()), grad_x[None],
            g_norm_gain, g_w_in[None], g_lb, untap(g_conv_w), g_hgrn, g_convn, g_w_out[None], flat(g_final),
            d_ng, d_w_in[None], d_lb, untap(d_cw), d_hg, d_cg, d_w_out[None], flat(d_fg),
            nm_ng, nm_w_in[None], nm_lb, untap(nm_cw), nm_hg, nm_cg, nm_w_out[None], flat(nm_fg),
            nv_ng, nv_w_in[None], nv_lb, untap(nv_cw), nv_hg, nv_cg, nv_w_out[None], flat(nv_fg))
```

```python
import jax
import jax.numpy as jnp
import numpy as np
from jax import lax
from jax.experimental import pallas as pl
from jax.experimental.pallas import tpu as pltpu

F32 = jnp.float32
BF16 = jnp.bfloat16
MESH = pl.DeviceIdType.MESH

SEQ = 2048
D_MODEL = 1024
D_HGRN = 512
D_CONV = 512
HEAD = 128
N_HEADS = 4
CHUNK = 64
CONV_GROUP = 64
N_SHARD = 4
SHARD_COLS = 1024
WO_ROWS = 256
EPS = 1e-6
TB = 256
NCB = TB // CHUNK
N_CHUNKS = SEQ // CHUNK
N_DEV = 8
COLLECTIVE_GATHER, COLLECTIVE_MIX_OUT, COLLECTIVE_TAIL = 1, 0, 2
AUX_O, AUX_CV, AUX_B, AUX_COLS = 0, 512, 1024, 1536

ADAM_LR = 0.001
ADAM_B1 = 0.9
ADAM_B2 = 0.999
ADAM_EPS = 1e-08
ADAM_WD = 0.01
ADAM_STEP = 10

VMEM_LIMIT = 56 * 1024 * 1024


def _dot(a, b):
    return jnp.dot(a, b, preferred_element_type=F32)


def _dot_nt(a, b):
    return lax.dot_general(a, b, (((1,), (1,)), ((), ())), preferred_element_type=F32)


def _dot_tn(a, b):
    return lax.dot_general(a, b, (((0,), (0,)), ((), ())), preferred_element_type=F32)


def _split_bf16(x, n):
    parts = []
    r = x
    for _ in range(n):
        p = r.astype(BF16)
        parts.append(p)
        r = r - p.astype(F32)
    return parts


def _exact_left(m, x, n=3):
    acc = None
    for p in _split_bf16(x, n):
        t = _dot(m, p)
        acc = t if acc is None else acc + t
    return acc


def _exact_left_many(m, xs, n=3):
    parts = [_split_bf16(x, n) for x in xs]
    accs = [None] * len(xs)
    for i in range(n):
        for j in range(len(xs)):
            t = _dot(m, parts[j][i])
            accs[j] = t if accs[j] is None else accs[j] + t
    return accs


def _group_mean_many(xs, gmat, n=2):
    parts = [_split_bf16(x, n) for x in xs]
    accs = [None] * len(xs)
    for i in range(n):
        for j in range(len(xs)):
            t = _dot(parts[j][i], gmat)
            accs[j] = t if accs[j] is None else accs[j] + t
    return accs


def _group_mean(x, gmat, n=2):
    w = gmat.shape[0]
    outs = []
    for c0 in range(0, x.shape[1], w):
        acc = None
        for p in _split_bf16(x[:, c0:c0 + w], n):
            t = _dot(p, gmat)
            acc = t if acc is None else acc + t
        outs.append(acc)
    return jnp.concatenate(outs, axis=1)


def _sigmoid(x):
    return 1.0 / (1.0 + jnp.exp(-x))


def _lower_bound(lbl):
    l0 = lbl[0:1, :]
    l1 = lbl[1:2, :]
    m = jnp.maximum(l0, l1)
    e0 = jnp.exp(l0 - m)
    e1 = jnp.exp(l1 - m)
    return e0 / (e0 + e1)


def _tri(lower):
    r = lax.broadcasted_iota(jnp.int32, (CHUNK, CHUNK), 0)
    c = lax.broadcasted_iota(jnp.int32, (CHUNK, CHUNK), 1)
    return jnp.where((c <= r) if lower else (c >= r), 1.0, 0.0).astype(BF16)


def _causal():
    r = lax.broadcasted_iota(jnp.int32, (CHUNK, CHUNK), 0)
    c = lax.broadcasted_iota(jnp.int32, (CHUNK, CHUNK), 1)
    return c <= r


def _shift_down(x, sh, prev_tail):
    r = pltpu.roll(x, sh, 0)
    pt = pltpu.roll(prev_tail, sh, 0)
    rows = lax.broadcasted_iota(jnp.int32, prev_tail.shape, 0)
    top = jnp.where(rows < sh, pt, r[0:8])
    return jnp.concatenate([top, r[8:]], axis=0)


def _shift_up(x, sh, next_head):
    n = x.shape[0]
    r = pltpu.roll(x, n - sh, 0)
    nh = pltpu.roll(next_head, 8 - sh, 0)
    rows = lax.broadcasted_iota(jnp.int32, next_head.shape, 0)
    bot = jnp.where(rows >= 8 - sh, nh, r[n - 8:])
    return jnp.concatenate([r[:n - 8], bot], axis=0)


def _group_matrix(width, group):
    r = np.arange(width)[:, None] // group
    c = np.arange(width)[None, :] // group
    return jnp.asarray(np.where(r == c, 1.0 / group, 0.0), dtype=BF16)


TG = 1024
SEM_W, SEM_CW, SEM_W_FWD, N_SEM = 0, 4, 7, 11


def _gather_proj(kidx, x2d, g1, w_in, conv_w):
    half_w = D_MODEL // 2
    half_c = SHARD_COLS // 2
    nt = SEQ // TG
    n_steps = 2 * N_SHARD

    def body(k_ref, x_ref, g_ref, w_ref, cw_ref, h_ref, p_ref, wg_out, cwg_out,
             wg_v, cwg_v, send_sems, recv_sems, out_sems):
        s, t = pl.program_id(0), pl.program_id(1)
        x, y, c = lax.axis_index("x"), lax.axis_index("y"), lax.axis_index("c")
        k = 2 * x + y
        sibling = (x, y, 1 - c)
        chips = [(1 - x, y), (x, 1 - y), (1 - x, 1 - y)]
        kjs = [2 * cx + cy for cx, cy in chips]
        diag = (*chips[2], c)

        def w_half(kk, cc):
            return wg_v.at[kk, pl.ds(cc * half_w, half_w), :]

        def w_quarter(kk, cc, piece):
            return wg_v.at[kk, pl.ds(cc * half_w, half_w), piece * half_c:(piece + 1) * half_c]

        def cw_of(kk):
            return cwg_v.at[:, pl.ds(pl.multiple_of(kk * HEAD, HEAD), HEAD)]

        def copy(sem, ref, to):
            return pltpu.make_async_remote_copy(
                src_ref=ref, dst_ref=ref, send_sem=send_sems.at[sem], recv_sem=recv_sems.at[sem],
                device_id=to, device_id_type=MESH)

        def at_step(sv, tv):
            return pl.when((s == sv) & (t == tv))

        w_direct = ([copy(SEM_W + j, w_half(k, c), (*chips[j], c)) for j in range(2)]
                    + [copy(SEM_W + 2 + p, w_quarter(k, c, p), diag) for p in range(2)])
        cw_direct = [copy(SEM_CW + j, cw_of(k), (*chip, c)) for j, chip in enumerate(chips)]
        w_passed = ([copy(SEM_W_FWD + j, w_half(kjs[j], c), sibling) for j in range(2)]
                    + [copy(SEM_W_FWD + 2 + p, w_quarter(kjs[2], c, p), sibling) for p in range(2)])
        stores = ([pltpu.make_async_copy(wg_v.at[kk], wg_out.at[kk], out_sems.at[i])
                   for i, kk in enumerate([k] + kjs)]
                  + [pltpu.make_async_copy(cwg_v, cwg_out, out_sems.at[4])])

        @at_step(0, 0)
        def _():
            barrier = pltpu.get_barrier_semaphore()
            for peer in [sibling] + [(*chip, c) for chip in chips]:
                pl.semaphore_signal(barrier, inc=1, device_id=peer, device_id_type=MESH)
            wg_v[k] = w_ref[0].astype(BF16)
            mine = pl.ds(pl.multiple_of(k * HEAD, HEAD), HEAD)
            cwg_v[:, mine] = jnp.zeros((8, HEAD), F32)
            for tap in range(3):
                cwg_v[tap:tap + 1, mine] = cw_ref[:, tap * HEAD:(tap + 1) * HEAD]
            pl.semaphore_wait(barrier, 4)
            w_direct[0].start()
            w_direct[1].start()
            for cp in cw_direct:
                cp.start()
            stores[0].start()

        @at_step(2, 0)
        def _():
            for j in range(2):
                copy(SEM_W + j, w_half(kjs[j], c), sibling).wait_recv()
                w_passed[j].start()
            w_direct[2].start()
            w_direct[3].start()
            copy(SEM_W_FWD, w_half(kjs[0], 1 - c), sibling).wait_recv()
            stores[1].start()

        @at_step(4, 0)
        def _():
            copy(SEM_W_FWD + 1, w_half(kjs[1], 1 - c), sibling).wait_recv()
            stores[2].start()

        for p in range(2):
            @at_step(6 + p, 0)
            def _(p=p):
                copy(SEM_W + 2 + p, w_quarter(kjs[2], c, p), sibling).wait_recv()
                w_passed[2 + p].start()
                copy(SEM_W_FWD + 2 + p, w_quarter(kjs[2], 1 - c, p), sibling).wait_recv()

        rows = pl.ds(pl.multiple_of(t * TG, TG), TG)

        @pl.when(s == 0)
        def _():
            xv = x_ref[...]
            r = lax.rsqrt(jnp.mean(xv * xv, axis=-1, keepdims=True) + EPS)
            h_ref[rows, :] = (xv * r * g_ref[...]).astype(BF16)

        sh = s >> 1
        js = k ^ (((sh & 1) << 1) | (sh >> 1))
        for piece in range(2):
            @pl.when((s & 1) == piece)
            def _(piece=piece):
                p_ref[...] = _dot(h_ref[rows, :], wg_v[js, :, piece * half_c:(piece + 1) * half_c])

        @at_step(n_steps - 1, nt - 1)
        def _():
            stores[3].start()
            for j in range(3):
                copy(SEM_CW + j, cw_of(kjs[j]), sibling).wait_recv()
            stores[4].start()
            for cp in w_direct + cw_direct + w_passed:
                cp.wait_send()
            for st in stores:
                st.wait()

    def x_map(s, t, kr):
        return (jnp.where(s == 0, t, nt - 1), 0)

    def p_map(s, t, kr):
        sh = s >> 1
        return (t, 2 * (kr[0] ^ (((sh & 1) << 1) | (sh >> 1))) + (s & 1))

    hbm = pl.BlockSpec(memory_space=pl.ANY)
    grid_spec = pltpu.PrefetchScalarGridSpec(
        num_scalar_prefetch=1, grid=(n_steps, nt),
        in_specs=[pl.BlockSpec((TG, D_MODEL), x_map),
                  pl.BlockSpec((1, D_MODEL), lambda s, t, kr: (0, 0)),
                  pl.BlockSpec((1, D_MODEL, SHARD_COLS), lambda s, t, kr: (0, 0, 0)),
                  pl.BlockSpec((1, 3 * HEAD), lambda s, t, kr: (0, 0))],
        out_specs=(pl.BlockSpec((SEQ, D_MODEL), lambda s, t, kr: (0, 0)),
                   pl.BlockSpec((TG, half_c), p_map), hbm, hbm),
        scratch_shapes=[pltpu.VMEM((N_SHARD, D_MODEL, SHARD_COLS), BF16),
                        pltpu.VMEM((8, D_CONV), F32),
                        pltpu.SemaphoreType.DMA((N_SEM,)), pltpu.SemaphoreType.DMA((N_SEM,)),
                        pltpu.SemaphoreType.DMA((5,))])
    return pl.pallas_call(
        body, name="gather_proj", grid_spec=grid_spec,
        out_shape=(jax.ShapeDtypeStruct((SEQ, D_MODEL), BF16),
                   jax.ShapeDtypeStruct((SEQ, N_SHARD * SHARD_COLS), F32),
                   jax.ShapeDtypeStruct((N_SHARD, D_MODEL, SHARD_COLS), BF16),
                   jax.ShapeDtypeStruct((8, D_CONV), F32)),
        compiler_params=pltpu.CompilerParams(dimension_semantics=("arbitrary", "arbitrary"),
                                             vmem_limit_bytes=VMEM_LIMIT, collective_id=COLLECTIVE_GATHER),
    )(kidx, x2d, g1, w_in, conv_w)


LAG = 6


def _mix_out(proj, lb_logits, cw, ga, gcn, g64, w_out, x2d, gf, tgt):
    half_o = WO_ROWS // 2
    nblk = SEQ // TB
    n_steps = nblk + LAG

    def body(p_ref, lbl_ref, cw_ref, ga_ref, gcn_ref, g64_ref, wo_ref, x_ref, gf_ref, t_ref,
             aux_ref, sto_ref, dx2_ref, dm_ref, gwo_ref, part_ref,
             st_ref, tail_ref, wog_v, stage, ring, acc_ref, send_sems, recv_sems):
        i = pl.program_id(0)
        x, y, c = lax.axis_index("x"), lax.axis_index("y"), lax.axis_index("c")
        k = 2 * x + y
        sibling = (x, y, 1 - c)
        chips = [(1 - x, y), (x, 1 - y), (1 - x, 1 - y)]
        kjs = [2 * cx + cy for cx, cy in chips]

        def wo_half(kk, cc):
            return wog_v.at[pl.ds(pl.multiple_of(kk * WO_ROWS + cc * half_o, half_o), half_o), :]

        def copy(sem, ref, to):
            return pltpu.make_async_remote_copy(
                src_ref=ref, dst_ref=ref, send_sem=send_sems.at[sem], recv_sem=recv_sems.at[sem],
                device_id=to, device_id_type=MESH)

        wo_direct = [copy(j, wo_half(k, c), (*chip, c)) for j, chip in enumerate(chips)]
        wo_passed = [copy(3 + j, wo_half(kj, c), sibling) for j, kj in enumerate(kjs)]

        @pl.when(i == 0)
        def _():
            barrier = pltpu.get_barrier_semaphore()
            for peer in [sibling] + [(*chip, c) for chip in chips]:
                pl.semaphore_signal(barrier, inc=1, device_id=peer, device_id_type=MESH)
            st_ref[...] = jnp.zeros_like(st_ref)
            tail_ref[...] = jnp.zeros_like(tail_ref)
            acc_ref[...] = jnp.zeros_like(acc_ref)
            part_ref[...] = jnp.zeros_like(part_ref)
            wog_v[pl.ds(pl.multiple_of(k * WO_ROWS, WO_ROWS), WO_ROWS), :] = wo_ref[0].astype(BF16)
            pl.semaphore_wait(barrier, 4)
            for cp in wo_direct:
                cp.start()

        @pl.when(i == LAG - 1)
        def _():
            for j in range(3):
                copy(j, wo_half(kjs[j], c), sibling).wait_recv()
                wo_passed[j].start()

        @pl.when(i == LAG)
        def _():
            for j in range(3):
                copy(3 + j, wo_half(kjs[j], 1 - c), sibling).wait_recv()

        lb = _lower_bound(lbl_ref[...])
        tri = _tri(True)
        causal = _causal()
        g64m = g64_ref[...]
        heads = range(N_HEADS)
        cs = [slice(hd * HEAD, (hd + 1) * HEAD) for hd in heads]
        col = lambda base, hd: slice(base + hd * HEAD, base + (hd + 1) * HEAD)

        def mix_chunk(n):
            sl = pl.ds(n * CHUNK, CHUNK)
            sg = [_sigmoid(p_ref[sl, col(512, hd)]) for hd in heads]
            f = [lb[:, cs[hd]] + (1.0 - lb[:, cs[hd]]) * sg[hd] for hd in heads]
            bc = _exact_left_many(tri, [jnp.log(f[hd]) for hd in heads])
            for hd in heads:
                aux_ref[sl, col(AUX_B, hd)] = bc[hd]
            g = [bc[hd][CHUNK - 1:CHUNK, :] for hd in heads]
            qd = [(p_ref[sl, col(0, hd)] * jnp.exp(bc[hd])).astype(BF16) for hd in heads]
            kk = [1.0 - f[hd] for hd in heads]
            ki = [(kk[hd] * jnp.exp(-bc[hd])).astype(BF16) for hd in heads]
            ke = [(kk[hd] * jnp.exp(g[hd] - bc[hd])).astype(BF16) for hd in heads]
            vb = [p_ref[sl, col(1024, hd)].astype(BF16) for hd in heads]
            st = [st_ref[hd] for hd in heads]
            st_b = [a.astype(BF16) for a in st]
            for hd in heads:
                sto_ref[n, hd] = st_b[hd]
            scm = [_dot_nt(qd[hd], ki[hd]) for hd in heads]
            inter = [_dot_nt(qd[hd], st_b[hd]) for hd in heads]
            upd = [_dot_tn(vb[hd], ke[hd]) for hd in heads]
            intra = [_dot(jnp.where(causal, scm[hd], 0.0).astype(BF16), vb[hd]) for hd in heads]
            for hd in heads:
                st_ref[hd] = st[hd] * jnp.exp(g[hd]) + upd[hd]
                o = intra[hd] + inter[hd]
                aux_ref[sl, col(AUX_O, hd)] = o
                ra = lax.rsqrt(jnp.mean(o * o, axis=-1, keepdims=True) + EPS)
                za = p_ref[sl, col(1536, hd)]
                stage[sl, cs[hd]] = (o * ra * ga_ref[:, cs[hd]] * (za * _sigmoid(za))).astype(BF16)
            yb = []
            for hd in heads:
                cu = p_ref[sl, col(3072, hd)] * p_ref[sl, col(2048, hd)]
                tail = tail_ref[:, cs[hd]]
                cv = (cw_ref[0:1, cs[hd]] * _shift_down(cu, 2, tail) + cw_ref[1:2, cs[hd]] * _shift_down(cu, 1, tail)
                      + cw_ref[2:3, cs[hd]] * cu)
                tail_ref[:, cs[hd]] = cu[CHUNK - 8:, :]
                aux_ref[sl, col(AUX_CV, hd)] = cv
                yb.append(p_ref[sl, col(2560, hd)] * cv)
            ms = _group_mean_many([y * y for y in yb], g64m)
            for hd in heads:
                rb = lax.rsqrt(ms[hd] + EPS)
                zb = p_ref[sl, col(3584, hd)]
                stage[sl, col(512, hd)] = (yb[hd] * rb * gcn_ref[:, cs[hd]] * (zb * _sigmoid(zb))).astype(BF16)

        def step(mix, project):
            if project:
                mixed_b = ring[pl.ds(pl.multiple_of((i - LAG) * TB, TB), TB), :]
                y = _dot(mixed_b, wog_v[...])
            if mix:
                mix_chunk(0)
            if project:
                x2 = x_ref[...] + y
                r2 = lax.rsqrt(jnp.mean(x2 * x2, axis=-1, keepdims=True) + EPS)
                n2 = x2 * r2
                gfv = gf_ref[...]
                err = n2 * gfv - t_ref[...]
                loss = 0.5 * jnp.sum(jnp.mean(err * err, axis=-1, keepdims=True), axis=0, keepdims=True)
                dy = err * (1.0 / D_MODEL)
                part_ref[1:2, :] += jnp.sum(dy * n2, axis=0, keepdims=True)
                part_ref[7:8, :] += jnp.broadcast_to(loss, (1, D_MODEL))
                dn = dy * gfv
                dx2 = r2 * (dn - n2 * jnp.mean(dn * n2, axis=-1, keepdims=True))
                dx2_ref[...] = dx2
                dx2_b = dx2.astype(BF16)
            if mix:
                mix_chunk(1)
            if project:
                dm_ref[...] = _dot_nt(dx2_b, wog_v[...])
            if mix:
                mix_chunk(2)
            if project:
                acc_ref[...] += _dot_tn(mixed_b, dx2_b)
            if mix:
                mix_chunk(3)
                ring[pl.ds(pl.multiple_of(i * TB, TB), TB), :] = stage[...]

        @pl.when(i < LAG)
        def _():
            step(True, False)

        @pl.when((i >= LAG) & (i < nblk))
        def _():
            step(True, True)

        @pl.when(i >= nblk)
        def _():
            step(False, True)

        @pl.when(i == n_steps - 1)
        def _():
            gwo_ref[...] = acc_ref[...].astype(BF16)
            for cp in wo_direct + wo_passed:
                cp.wait_send()

    assert NCB == 4
    row = lambda w: pl.BlockSpec((1, w), lambda i: (0, 0))
    mix_blk = lambda i: jnp.minimum(i, nblk - 1)
    out_blk = lambda i: jnp.clip(i - LAG, 0, nblk - 1)
    tok = lambda: pl.BlockSpec((TB, D_MODEL), lambda i: (out_blk(i), 0))
    return pl.pallas_call(
        body, name="mix_out", grid=(n_steps,),
        out_shape=(jax.ShapeDtypeStruct((SEQ, AUX_COLS), F32),
                   jax.ShapeDtypeStruct((N_CHUNKS, N_HEADS, HEAD, HEAD), BF16),
                   jax.ShapeDtypeStruct((SEQ, D_MODEL), F32),
                   jax.ShapeDtypeStruct((SEQ, D_MODEL), F32),
                   jax.ShapeDtypeStruct((D_MODEL, D_MODEL), BF16),
                   jax.ShapeDtypeStruct((8, D_MODEL), F32)),
        in_specs=[pl.BlockSpec((TB, 4096), lambda i: (jnp.minimum(i, nblk - 1), 0)),
                  pl.BlockSpec((2, D_HGRN), lambda i: (0, 0)),
                  pl.BlockSpec((8, D_CONV), lambda i: (0, 0)),
                  row(D_HGRN), row(D_CONV),
                  pl.BlockSpec((HEAD, HEAD), lambda i: (0, 0)),
                  pl.BlockSpec((1, WO_ROWS, D_MODEL), lambda i: (0, 0, 0)),
                  tok(), row(D_MODEL), tok()],
        out_specs=(pl.BlockSpec((TB, AUX_COLS), lambda i: (mix_blk(i), 0)),
                   pl.BlockSpec((NCB, N_HEADS, HEAD, HEAD), lambda i: (mix_blk(i), 0, 0, 0)),
                   tok(), tok(),
                   pl.BlockSpec((D_MODEL, D_MODEL), lambda i: (0, 0)),
                   pl.BlockSpec((8, D_MODEL), lambda i: (0, 0))),
        scratch_shapes=[pltpu.VMEM((N_HEADS, HEAD, HEAD), F32), pltpu.VMEM((8, D_CONV), F32),
                        pltpu.VMEM((D_MODEL, D_MODEL), BF16), pltpu.VMEM((TB, D_MODEL), BF16),
                        pltpu.VMEM((SEQ, D_MODEL), BF16), pltpu.VMEM((D_MODEL, D_MODEL), F32),
                        pltpu.SemaphoreType.DMA((6,)), pltpu.SemaphoreType.DMA((6,))],
        compiler_params=pltpu.CompilerParams(dimension_semantics=("arbitrary",), vmem_limit_bytes=VMEM_LIMIT,
                                             collective_id=COLLECTIVE_MIX_OUT),
    )(proj, lb_logits, cw, ga, gcn, g64, w_out, x2d, gf, tgt)


def _mix_bwd(proj, aux, states, dmixed, lb_logits, cw, ga, gcn, g64):
    nblk = SEQ // TB

    def body(p_ref, aux_ref, st_ref, dm_ref, lbl_ref, cw_ref, ga_ref, gcn_ref, g64_ref,
             dp_ref, part_ref, dst_ref, head_ref, dlb_ref):
        i = pl.program_id(0)

        @pl.when(i == 0)
        def _():
            dst_ref[...] = jnp.zeros_like(dst_ref)
            head_ref[...] = jnp.zeros_like(head_ref)
            part_ref[...] = jnp.zeros_like(part_ref)
            dlb_ref[...] = jnp.zeros_like(dlb_ref)

        lb = _lower_bound(lbl_ref[...])
        triu = _tri(False)
        causal = _causal()
        g64m = g64_ref[...]
        rowsum = lambda a: jnp.sum(a, axis=0, keepdims=True)
        heads = range(N_HEADS)
        cs = [slice(hd * HEAD, (hd + 1) * HEAD) for hd in heads]
        col = lambda base, hd: slice(base + hd * HEAD, base + (hd + 1) * HEAD)
        for n in reversed(range(NCB)):
            sl = pl.ds(n * CHUNK, CHUNK)
            cvv = [aux_ref[sl, col(AUX_CV, hd)] for hd in heads]
            gb = [p_ref[sl, col(2560, hd)] for hd in heads]
            yb = [gb[hd] * cvv[hd] for hd in heads]
            ms = _group_mean_many([y * y for y in yb], g64m)
            rb, nb, dnb = [], [], []
            for hd in heads:
                rb.append(lax.rsqrt(ms[hd] + EPS))
                nb.append(yb[hd] * rb[hd])
                zb = p_ref[sl, col(3584, hd)]
                sgb = _sigmoid(zb)
                dmb = dm_ref[sl, col(512, hd)]
                silu = zb * sgb
                dgate = dmb * gcn_ref[:, cs[hd]]
                part_ref[2:3, col(512, hd)] += rowsum(dmb * nb[hd] * silu)
                dp_ref[sl, col(3584, hd)] = (dgate * nb[hd] * (sgb + silu * (1.0 - sgb))).astype(BF16)
                dnb.append(dgate * silu)
            mdn = _group_mean_many([dnb[hd] * nb[hd] for hd in heads], g64m)
            for hd in heads:
                dyb = rb[hd] * (dnb[hd] - nb[hd] * mdn[hd])
                dp_ref[sl, col(2560, hd)] = (dyb * cvv[hd]).astype(BF16)
                dcv = dyb * gb[hd]
                head = head_ref[:, cs[hd]]
                dcv1 = _shift_up(dcv, 1, head)
                dcv2 = _shift_up(dcv, 2, head)
                head_ref[:, cs[hd]] = dcv[0:8, :]
                u = p_ref[sl, col(2048, hd)]
                gc = p_ref[sl, col(3072, hd)]
                cu = gc * u
                part_ref[4:5, cs[hd]] += rowsum(dcv2 * cu)
                part_ref[5:6, cs[hd]] += rowsum(dcv1 * cu)
                part_ref[6:7, cs[hd]] += rowsum(dcv * cu)
                dcu = cw_ref[2:3, cs[hd]] * dcv + cw_ref[1:2, cs[hd]] * dcv1 + cw_ref[0:1, cs[hd]] * dcv2
                dp_ref[sl, col(3072, hd)] = (dcu * u).astype(BF16)
                dp_ref[sl, col(2048, hd)] = (dcu * gc).astype(BF16)
            do_b = []
            for hd in heads:
                ov = aux_ref[sl, col(AUX_O, hd)]
                ra = lax.rsqrt(jnp.mean(ov * ov, axis=-1, keepdims=True) + EPS)
                na = ov * ra
                za = p_ref[sl, col(1536, hd)]
                sga = _sigmoid(za)
                dma = dm_ref[sl, cs[hd]]
                silu = za * sga
                dgate = dma * ga_ref[:, cs[hd]]
                part_ref[2:3, cs[hd]] += rowsum(dma * na * silu)
                dp_ref[sl, col(1536, hd)] = (dgate * na * (sga + silu * (1.0 - sga))).astype(BF16)
                dna = dgate * silu
                do_b.append((ra * (dna - na * jnp.mean(dna * na, axis=-1, keepdims=True))).astype(BF16))
            s = [_sigmoid(p_ref[sl, col(512, hd)]) for hd in heads]
            f = [lb[:, cs[hd]] + (1.0 - lb[:, cs[hd]]) * s[hd] for hd in heads]
            bc = [aux_ref[sl, col(AUX_B, hd)] for hd in heads]
            g = [bc[hd][CHUNK - 1:CHUNK, :] for hd in heads]
            eb = [jnp.exp(bc[hd]) for hd in heads]
            enb = [jnp.exp(-bc[hd]) for hd in heads]
            eg = [jnp.exp(g[hd] - bc[hd]) for hd in heads]
            dec = [jnp.exp(g[hd]) for hd in heads]
            qd = [p_ref[sl, cs[hd]] * eb[hd] for hd in heads]
            kk = [1.0 - f[hd] for hd in heads]
            ki = [kk[hd] * enb[hd] for hd in heads]
            ke = [kk[hd] * eg[hd] for hd in heads]
            qd_b = [a.astype(BF16) for a in qd]
            ki_b = [a.astype(BF16) for a in ki]
            ke_b = [a.astype(BF16) for a in ke]
            vb = [p_ref[sl, col(1024, hd)].astype(BF16) for hd in heads]
            st_b = [st_ref[n, hd] for hd in heads]
            dst = [dst_ref[hd] for hd in heads]
            dst_b = [a.astype(BF16) for a in dst]
            scm = [_dot_nt(qd_b[hd], ki_b[hd]) for hd in heads]
            amm = [_dot_nt(do_b[hd], vb[hd]) for hd in heads]
            dqd2 = [_dot(do_b[hd], st_b[hd]) for hd in heads]
            dke = [_dot(vb[hd], dst_b[hd]) for hd in heads]
            dv2 = [_dot_nt(ke_b[hd], dst_b[hd]) for hd in heads]
            dsu = [_dot_tn(do_b[hd], qd_b[hd]) for hd in heads]
            sc = [jnp.where(causal, scm[hd], 0.0).astype(BF16) for hd in heads]
            am = [jnp.where(causal, amm[hd], 0.0).astype(BF16) for hd in heads]
            dqd1 = [_dot(am[hd], ki_b[hd]) for hd in heads]
            dki = [_dot_tn(am[hd], qd_b[hd]) for hd in heads]
            dv1 = [_dot_tn(sc[hd], do_b[hd]) for hd in heads]
            db, dgv, dkk = [], [], []
            for hd in heads:
                dqd = dqd1[hd] + dqd2[hd]
                ddec = rowsum(dst[hd] * st_b[hd].astype(F32))
                dst_ref[hd] = dst[hd] * dec[hd] + dsu[hd]
                dp_ref[sl, cs[hd]] = (dqd * eb[hd]).astype(BF16)
                dp_ref[sl, col(1024, hd)] = (dv1[hd] + dv2[hd]).astype(BF16)
                dke_eg = dke[hd] * eg[hd]
                dkk.append(dki[hd] * enb[hd] + dke_eg)
                db.append(dqd * qd[hd] - kk[hd] * dkk[hd])
                dgv.append(rowsum(kk[hd] * dke_eg) + ddec * dec[hd])
            rc = _exact_left_many(triu, db, 2)
            for hd in heads:
                df = (rc[hd] + dgv[hd]) / f[hd] - dkk[hd]
                one_s = 1.0 - s[hd]
                dlb_ref[:, cs[hd]] += rowsum(df * one_s)
                dp_ref[sl, col(512, hd)] = (df * (1.0 - lb[:, cs[hd]]) * s[hd] * one_s).astype(BF16)

        @pl.when(i == nblk - 1)
        def _():
            row = dlb_ref[...] * lb * (1.0 - lb)
            part_ref[3:4, 0:D_HGRN] = row
            part_ref[3:4, D_HGRN:] = -row

    rev = lambda w: pl.BlockSpec((TB, w), lambda i: (nblk - 1 - i, 0))
    row = lambda w: pl.BlockSpec((1, w), lambda i: (0, 0))
    return pl.pallas_call(
        body, name="mix_bwd", grid=(nblk,),
        out_shape=(jax.ShapeDtypeStruct((SEQ, 4096), BF16),
                   jax.ShapeDtypeStruct((8, D_MODEL), F32)),
        in_specs=[rev(4096), rev(AUX_COLS),
                  pl.BlockSpec((NCB, N_HEADS, HEAD, HEAD), lambda i: (nblk - 1 - i, 0, 0, 0)),
                  rev(D_MODEL),
                  pl.BlockSpec((2, D_HGRN), lambda i: (0, 0)),
                  pl.BlockSpec((8, D_CONV), lambda i: (0, 0)),
                  row(D_HGRN), row(D_CONV),
                  pl.BlockSpec((HEAD, HEAD), lambda i: (0, 0))],
        out_specs=(rev(4096), pl.BlockSpec((8, D_MODEL), lambda i: (0, 0))),
        scratch_shapes=[pltpu.VMEM((N_HEADS, HEAD, HEAD), F32), pltpu.VMEM((8, D_CONV), F32),
                        pltpu.VMEM((1, D_HGRN), F32)],
        compiler_params=pltpu.CompilerParams(dimension_semantics=("arbitrary",), vmem_limit_bytes=VMEM_LIMIT),
    )(proj, aux, states, dmixed, lb_logits, cw, ga, gcn, g64)


TT = 1024
TX = 256
(SEM_D2D, SEM_D2D_O, SEM_ICI, SEM_ICI_O, SEM_FIN, SEM_FIN_O, SEM_SMALL, N_SEM_TAIL) = 0, 4, 5, 8, 11, 12, 12, 20


def _bwd_tail(kidx, h, dproj, wg, gwo, x2d, dx2, g1, small_a, small_b):
    hw = D_MODEL // 2
    ho = WO_ROWS // 2
    nt = SEQ // TT
    norm_step = 2 * N_SHARD
    n_steps = norm_step + SEQ // TX // nt

    def body(k_ref, h_ref, dp_ref, w_ref, gwo_ref, x_ref, dx2_ref, g_ref, sm_ref, smb_ref,
             gx_ref, gw_out, gwo_out, osm_ref,
             acc, dh, sendbuf, keep, sibrcv, rcv, sib_o, p_o, rcv_o, res_o, sm_buf, dng,
             send_sems, recv_sems, out_sems):
        s, t = pl.program_id(0), pl.program_id(1)
        x, y, c = lax.axis_index("x"), lax.axis_index("y"), lax.axis_index("c")
        k = 2 * x + y
        me = 4 * x + 2 * y + c
        sibling = (x, y, 1 - c)
        chips = [(1 - x, 1 - y), (1 - x, y), (x, 1 - y)]
        kjs = [2 * cx + cy for cx, cy in chips]
        mine = pl.ds(pl.multiple_of(c * hw, hw), hw)
        other = pl.ds(pl.multiple_of((1 - c) * hw, hw), hw)
        mine_o = pl.ds(pl.multiple_of(c * ho, ho), ho)
        other_o = pl.ds(pl.multiple_of((1 - c) * ho, ho), ho)

        def copy(sem, src, dst, to):
            return pltpu.make_async_remote_copy(
                src_ref=src, dst_ref=dst, send_sem=send_sems.at[sem], recv_sem=recv_sems.at[sem],
                device_id=to, device_id_type=MESH)

        def at_step(sv, tv):
            return pl.when((s == sv) & (t == tv))

        def at_norm_block(b):
            return at_step(norm_step + b // nt, b % nt)

        d2d = [copy(SEM_D2D + sv, sendbuf.at[sv], sibrcv.at[sv], sibling) for sv in range(N_SHARD)]
        d2d_o = copy(SEM_D2D_O, gwo_ref.at[:, other_o, :], sib_o, sibling)
        ici = [copy(SEM_ICI + sv, keep.at[sv], rcv.at[sv], (*chips[sv], c)) for sv in range(3)]
        ici_o = [copy(SEM_ICI_O + sv, p_o.at[kjs[sv]], rcv_o.at[sv], (*chips[sv], c)) for sv in range(3)]
        fin = copy(SEM_FIN, acc.at[mine, :], gw_out.at[mine, :], sibling)
        fin_o = copy(SEM_FIN_O, res_o.at[mine_o, :], res_o.at[mine_o, :], sibling)
        smalls = [copy(SEM_SMALL + m, sm_buf.at[me], sm_buf.at[me],
                       (x ^ (m >> 2), y ^ ((m >> 1) & 1), c ^ (m & 1))) for m in range(1, N_DEV)]
        store_w = pltpu.make_async_copy(acc.at[mine, :], gw_out.at[mine, :], out_sems.at[0])
        store_o = pltpu.make_async_copy(res_o, gwo_out, out_sems.at[1])

        @at_step(0, 0)
        def _():
            barrier = pltpu.get_barrier_semaphore()
            for m in range(1, N_DEV):
                pl.semaphore_signal(barrier, inc=1, device_id=(x ^ (m >> 2), y ^ ((m >> 1) & 1), c ^ (m & 1)),
                                    device_id_type=MESH)
            pl.semaphore_wait(barrier, N_DEV - 1)
            d2d_o.start()

        @at_step(0, 1)
        def _():
            d2d_o.wait_recv()
            for j in range(N_SHARD):
                p_o[j] = (gwo_ref[j, mine_o, :].astype(F32) + sib_o[j].astype(F32)).astype(BF16)
            res_o[mine_o, :] = gwo_ref[k, mine_o, :].astype(F32) + sib_o[k].astype(F32)
            for cp in ici_o:
                cp.start()

        rows = pl.ds(pl.multiple_of(t * TT, TT), TT)

        @pl.when((s < N_SHARD) & (t == 0))
        def _():
            acc[...] = _dot_tn(h_ref[...], dp_ref[...])

        @pl.when((s < N_SHARD) & (t > 0))
        def _():
            acc[...] += _dot_tn(h_ref[...], dp_ref[...])

        for sv in range(N_SHARD):
            @at_step(sv, nt - 1)
            def _(sv=sv):
                sendbuf[sv] = acc[other, :].astype(BF16)
                if sv < 3:
                    keep[sv] = acc[mine, :].astype(BF16)
                d2d[sv].start()

        for sv in range(3):
            @at_step(sv + 1, 0)
            def _(sv=sv):
                d2d[sv].wait_recv()
                keep[sv] = (keep[sv].astype(F32) + sibrcv[sv].astype(F32)).astype(BF16)
                ici[sv].start()

        @pl.when(s == N_SHARD)
        def _():
            dh[rows, :] = _dot_nt(dp_ref[...], w_ref[0])

        @pl.when((s > N_SHARD) & (s < norm_step))
        def _():
            dh[rows, :] += _dot_nt(dp_ref[...], w_ref[0])

        @at_norm_block(0)
        def _():
            d2d[3].wait_recv()
            ici[0].wait_recv()
            acc[mine, :] += sibrcv[3].astype(F32) + rcv[0].astype(F32)

        @at_norm_block(1)
        def _():
            tot = res_o[mine_o, :]
            for sv in range(3):
                ici_o[sv].wait_recv()
                tot = tot + rcv_o[sv].astype(F32)
            res_o[mine_o, :] = tot
            fin_o.start()

        @at_norm_block(2)
        def _():
            ici[1].wait_recv()
            acc[mine, :] += rcv[1].astype(F32)

        @at_norm_block(SEQ // TX - 2)
        def _():
            ici[2].wait_recv()
            acc[mine, :] += rcv[2].astype(F32)
            fin.start()
            store_w.start()
            fin_o.wait_recv()
            store_o.start()

        @at_norm_block(0)
        def _():
            dng[...] = jnp.zeros_like(dng)

        @pl.when(s >= norm_step)
        def _():
            blk = (s - norm_step) * nt + t
            dhv = dh[pl.ds(pl.multiple_of(blk * TX, TX), TX), :]
            xv = x_ref[...]
            r = lax.rsqrt(jnp.mean(xv * xv, axis=-1, keepdims=True) + EPS)
            xn = xv * r
            dng[...] += jnp.sum(dhv * xn, axis=0, keepdims=True)
            dxn = dhv * g_ref[...]
            gx_ref[...] = dx2_ref[...] + r * (dxn - xn * jnp.mean(dxn * xn, axis=-1, keepdims=True))

        @at_step(n_steps - 1, nt - 1)
        def _():
            sm_buf[me] = sm_ref[...] + smb_ref[...]
            sm_buf[me, 0:1, :] = dng[...]
            for cp in smalls:
                cp.start()
            for m in range(1, N_DEV):
                copy(SEM_SMALL + m, sm_buf.at[0], sm_buf.at[0], sibling).wait_recv()
            tot = sm_buf[0]
            for d in range(1, N_DEV):
                tot = tot + sm_buf[d]
            osm_ref[...] = tot
            fin.wait_recv()
            for cp in d2d + [d2d_o] + ici + ici_o + [fin, fin_o] + smalls:
                cp.wait_send()
            store_o.wait()
            store_w.wait()

    def shard_of(s, kr):
        order = jnp.where(s < N_SHARD, s, jnp.where(s < norm_step, s - N_SHARD, 3))
        return kr[0] ^ (3 - order)

    def h_map(s, t, kr):
        return (jnp.where(s < N_SHARD, t, nt - 1), 0)

    def dp_map(s, t, kr):
        return (jnp.where(s < norm_step, t, nt - 1), shard_of(s, kr))

    def w_map(s, t, kr):
        return (shard_of(jnp.maximum(s, N_SHARD), kr), 0, 0)

    def blk_map(s, t, kr):
        return (jnp.where(s < norm_step, 0, (s - norm_step) * nt + t), 0)

    hbm = pl.BlockSpec(memory_space=pl.ANY)
    grid_spec = pltpu.PrefetchScalarGridSpec(
        num_scalar_prefetch=1, grid=(n_steps, nt),
        in_specs=[pl.BlockSpec((TT, D_MODEL), h_map),
                  pl.BlockSpec((TT, SHARD_COLS), dp_map),
                  pl.BlockSpec((1, D_MODEL, SHARD_COLS), w_map),
                  pl.BlockSpec((N_SHARD, WO_ROWS, D_MODEL), lambda s, t, kr: (0, 0, 0)),
                  pl.BlockSpec((TX, D_MODEL), blk_map),
                  pl.BlockSpec((TX, D_MODEL), blk_map),
                  pl.BlockSpec((1, D_MODEL), lambda s, t, kr: (0, 0)),
                  pl.BlockSpec((8, D_MODEL), lambda s, t, kr: (0, 0)),
                  pl.BlockSpec((8, D_MODEL), lambda s, t, kr: (0, 0))],
        out_specs=(pl.BlockSpec((TX, D_MODEL), blk_map), hbm, hbm,
                   pl.BlockSpec((8, D_MODEL), lambda s, t, kr: (0, 0))),
        scratch_shapes=[pltpu.VMEM((D_MODEL, SHARD_COLS), F32), pltpu.VMEM((SEQ, D_MODEL), F32),
                        pltpu.VMEM((N_SHARD, hw, SHARD_COLS), BF16), pltpu.VMEM((3, hw, SHARD_COLS), BF16),
                        pltpu.VMEM((N_SHARD, hw, SHARD_COLS), BF16), pltpu.VMEM((3, hw, SHARD_COLS), BF16),
                        pltpu.VMEM((N_SHARD, ho, D_MODEL), BF16), pltpu.VMEM((N_SHARD, ho, D_MODEL), BF16),
                        pltpu.VMEM((3, ho, D_MODEL), BF16), pltpu.VMEM((WO_ROWS, D_MODEL), F32),
                        pltpu.VMEM((N_DEV, 8, D_MODEL), F32), pltpu.VMEM((1, D_MODEL), F32),
                        pltpu.SemaphoreType.DMA((N_SEM_TAIL,)), pltpu.SemaphoreType.DMA((N_SEM_TAIL,)),
                        pltpu.SemaphoreType.DMA((2,))])
    return pl.pallas_call(
        body, name="bwd_tail", grid_spec=grid_spec,
        out_shape=(jax.ShapeDtypeStruct((SEQ, D_MODEL), F32),
                   jax.ShapeDtypeStruct((D_MODEL, SHARD_COLS), F32),
                   jax.ShapeDtypeStruct((WO_ROWS, D_MODEL), F32),
                   jax.ShapeDtypeStruct((8, D_MODEL), F32)),
        compiler_params=pltpu.CompilerParams(dimension_semantics=("arbitrary", "arbitrary"),
                                             vmem_limit_bytes=60 * 1024 * 1024, collective_id=COLLECTIVE_TAIL),
    )(kidx, h, dproj, wg, gwo, x2d, dx2, g1, small_a, small_b)


def _adam_update(w, g, m, v):
    nm = ADAM_B1 * m + (1.0 - ADAM_B1) * g
    nv = ADAM_B2 * v + (1.0 - ADAM_B2) * (g * g)
    m_hat = nm / (1.0 - ADAM_B1 ** ADAM_STEP)
    v_hat = nv / (1.0 - ADAM_B2 ** ADAM_STEP)
    return -ADAM_LR * (m_hat / (jnp.sqrt(v_hat) + ADAM_EPS) + ADAM_WD * w), nm, nv


def _adamw_all(tot, g_w_in, g_w_out, big, small, grad_x):
    n = len(small)
    rows = WO_ROWS
    steps = D_MODEL // rows

    def body(tot_ref, *refs):
        gx_ref, gx_out = refs[2 + 3 * (2 + n)], refs[-1]
        gx_out[...] = gx_ref[...]
        ins, outs = refs[:2 + 3 * (2 + n)], refs[3 + 3 * (2 + n):-1]
        g_refs, wmv = ins[:2], ins[2:]
        loss_ref, quads = outs[0], outs[1:]

        def update(j, g):
            w_ref, m_ref, v_ref = wmv[3 * j:3 * j + 3]
            g_ref, d_ref, nm_ref, nv_ref = quads[4 * j:4 * j + 4]
            g_ref[...] = g
            d_ref[...], nm_ref[...], nv_ref[...] = _adam_update(w_ref[...], g, m_ref[...], v_ref[...])

        update(0, g_refs[0][...])

        @pl.when(pl.program_id(0) == 0)
        def _():
            update(1, g_refs[1][...])
            k = 2 * lax.axis_index("x") + lax.axis_index("y")
            mine = pl.ds(pl.multiple_of(k * HEAD, HEAD), HEAD)
            loss_ref[...] = tot_ref[7:8, 0:1]
            grads = [tot_ref[0:1, :], tot_ref[1:2, :], tot_ref[2:3, 0:D_HGRN], tot_ref[2:3, D_HGRN:],
                     jnp.concatenate([tot_ref[3:4, 0:D_HGRN], tot_ref[3:4, D_HGRN:]], axis=0),
                     jnp.concatenate([tot_ref[4 + tap:5 + tap, mine] for tap in range(3)], axis=1)]
            for j, g in enumerate(grads):
                update(2 + j, g)

    whole = lambda a: pl.BlockSpec(a.shape, lambda i: (0, 0))
    blk = pl.BlockSpec((rows, SHARD_COLS), lambda i: (i, 0))
    arrays = [a for triple in big + small for a in triple]
    in_specs = ([whole(tot), blk, whole(g_w_out)] + [blk] * 3 + [whole(a) for a in arrays[3:]])
    shapes = [big[0][0], big[1][0]] + [w for w, _, _ in small]
    out_shape = (jax.ShapeDtypeStruct((1, 1), F32),) + tuple(
        jax.ShapeDtypeStruct(w.shape, F32) for w in shapes for _ in range(4))
    out_specs = (pl.BlockSpec((1, 1), lambda i: (0, 0)),) + (blk,) * 4 + tuple(
        whole(w) for w in shapes[1:] for _ in range(4))
    gx_blk = pl.BlockSpec((SEQ // steps, D_MODEL), lambda i: (i, 0))
    outs = pl.pallas_call(
        body, name="adamw_all", grid=(steps,),
        out_shape=out_shape + (jax.ShapeDtypeStruct(grad_x.shape, F32),),
        in_specs=in_specs + [gx_blk], out_specs=out_specs + (gx_blk,),
        compiler_params=pltpu.CompilerParams(dimension_semantics=("arbitrary",), vmem_limit_bytes=VMEM_LIMIT),
    )(tot, g_w_in, g_w_out, *arrays, grad_x)
    return [outs[0]] + [outs[1 + 4 * j:5 + 4 * j] for j in range(2 + n)] + [outs[-1]]


def _local_step(x2d, tgt, proj, lb_logits, cw, ga, gcn, w_out, gf):
    g64 = _group_matrix(HEAD, CONV_GROUP)
    aux, states, dx2, dmixed, gwo, part_out = _mix_out(proj, lb_logits, cw, ga, gcn, g64, w_out, x2d, gf, tgt)
    dproj, part_mix = _mix_bwd(proj, aux, states, dmixed, lb_logits, cw, ga, gcn, g64)
    return dproj, dx2, gwo.reshape(N_SHARD, WO_ROWS, D_MODEL), part_out, part_mix


def kernel(x, norm_gain, w_in, lb_logits, conv_w, hgrn_norm_gain, conv_norm_gain, w_out, final_norm_gain, loss_target, m_norm_gain, m_w_in, m_lb_logits, m_conv_w, m_hgrn_norm_gain, m_conv_norm_gain, m_w_out, m_final_norm_gain, v_norm_gain, v_w_in, v_lb_logits, v_conv_w, v_hgrn_norm_gain, v_conv_norm_gain, v_w_out, v_final_norm_gain):
    k = 2 * lax.axis_index("x") + lax.axis_index("y")
    kidx = jnp.reshape(k, (1,)).astype(jnp.int32)
    row = lambda a: a.reshape(1, D_MODEL)
    taps = lambda a: a.reshape(1, 3 * HEAD)
    h, proj, wg, cw = _gather_proj(kidx, x[0], norm_gain, w_in, taps(conv_w))
    dproj, dx2, gwo, part_out, part_mix = _local_step(
        x[0], loss_target[0], proj, lb_logits, cw, hgrn_norm_gain, conv_norm_gain, w_out, row(final_norm_gain))
    rgrad_x, rg_w_in, rg_w_out, tot = _bwd_tail(kidx, h, dproj, wg, gwo, x[0], dx2, norm_gain, part_out, part_mix)

    (loss, (g_w_in, d_w_in, nm_w_in, nv_w_in), (g_w_out, d_w_out, nm_w_out, nv_w_out),
     (g_norm_gain, d_ng, nm_ng, nv_ng), (g_final, d_fg, nm_fg, nv_fg), (g_hgrn, d_hg, nm_hg, nv_hg),
     (g_convn, d_cg, nm_cg, nv_cg), (g_lb, d_lb, nm_lb, nv_lb), (g_conv_w, d_cw, nm_cw, nv_cw),
     grad_x) = _adamw_all(
        tot, rg_w_in, rg_w_out,
        [(w_in[0], m_w_in[0], v_w_in[0]), (w_out[0], m_w_out[0], v_w_out[0])],
        [(norm_gain, m_norm_gain, v_norm_gain),
         (row(final_norm_gain), row(m_final_norm_gain), row(v_final_norm_gain)),
         (hgrn_norm_gain, m_hgrn_norm_gain, v_hgrn_norm_gain),
         (conv_norm_gain, m_conv_norm_gain, v_conv_norm_gain),
         (lb_logits, m_lb_logits, v_lb_logits),
         (taps(conv_w), taps(m_conv_w), taps(v_conv_w))],
        rgrad_x)
    flat = lambda a: a.reshape(D_MODEL)
    untap = lambda a: a.reshape(1, 3, HEAD)
    return (loss.reshape(()), grad_x[None],
            g_norm_gain, g_w_in[None], g_lb, untap(g_conv_w), g_hgrn, g_convn, g_w_out[None], flat(g_final),
            d_ng, d_w_in[None], d_lb, untap(d_cw), d_hg, d_cg, d_w_out[None], flat(d_fg),
            nm_ng, nm_w_in[None], nm_lb, untap(nm_cw), nm_hg, nm_cg, nm_w_out[None], flat(nm_fg),
            nv_ng, nv_w_in[None], nv_lb, untap(nv_cw), nv_hg, nv_cg, nv_w_out[None], flat(nv_fg))
```

```python
import jax
import jax.numpy as jnp
import numpy as np
from jax import lax
from jax.experimental import pallas as pl
from jax.experimental.pallas import tpu as pltpu

F32 = jnp.float32
BF16 = jnp.bfloat16
MESH = pl.DeviceIdType.MESH

SEQ = 2048
D_MODEL = 1024
D_HGRN = 512
D_CONV = 512
HEAD = 128
N_HEADS = 4
CHUNK = 64
CONV_GROUP = 64
N_SHARD = 4
SHARD_COLS = 1024
WO_ROWS = 256
EPS = 1e-6
TB = 256
NCB = TB // CHUNK
N_CHUNKS = SEQ // CHUNK
N_DEV = 8
COLLECTIVE_GATHER, COLLECTIVE_MIX_OUT, COLLECTIVE_TAIL = 1, 0, 2
AUX_O, AUX_CV, AUX_B, AUX_COLS = 0, 512, 1024, 1536

ADAM_LR = 0.001
ADAM_B1 = 0.9
ADAM_B2 = 0.999
ADAM_EPS = 1e-08
ADAM_WD = 0.01
ADAM_STEP = 10

VMEM_LIMIT = 56 * 1024 * 1024


def _dot(a, b):
    return jnp.dot(a, b, preferred_element_type=F32)


def _dot_nt(a, b):
    return lax.dot_general(a, b, (((1,), (1,)), ((), ())), preferred_element_type=F32)


def _dot_tn(a, b):
    return lax.dot_general(a, b, (((0,), (0,)), ((), ())), preferred_element_type=F32)


def _split_bf16(x, n):
    parts = []
    r = x
    for _ in range(n):
        p = r.astype(BF16)
        parts.append(p)
        r = r - p.astype(F32)
    return parts


def _exact_left(m, x, n=3):
    acc = None
    for p in _split_bf16(x, n):
        t = _dot(m, p)
        acc = t if acc is None else acc + t
    return acc


def _exact_left_many(m, xs, n=3):
    parts = [_split_bf16(x, n) for x in xs]
    accs = [None] * len(xs)
    for i in range(n):
        for j in range(len(xs)):
            t = _dot(m, parts[j][i])
            accs[j] = t if accs[j] is None else accs[j] + t
    return accs


def _group_mean_many(xs, gmat, n=2):
    parts = [_split_bf16(x, n) for x in xs]
    accs = [None] * len(xs)
    for i in range(n):
        for j in range(len(xs)):
            t = _dot(parts[j][i], gmat)
            accs[j] = t if accs[j] is None else accs[j] + t
    return accs


def _group_mean(x, gmat, n=2):
    w = gmat.shape[0]
    outs = []
    for c0 in range(0, x.shape[1], w):
        acc = None
        for p in _split_bf16(x[:, c0:c0 + w], n):
            t = _dot(p, gmat)
            acc = t if acc is None else acc + t
        outs.append(acc)
    return jnp.concatenate(outs, axis=1)


def _sigmoid(x):
    return 1.0 / (1.0 + jnp.exp(-x))


def _lower_bound(lbl):
    l0 = lbl[0:1, :]
    l1 = lbl[1:2, :]
    m = jnp.maximum(l0, l1)
    e0 = jnp.exp(l0 - m)
    e1 = jnp.exp(l1 - m)
    return e0 / (e0 + e1)


def _tri(lower):
    r = lax.broadcasted_iota(jnp.int32, (CHUNK, CHUNK), 0)
    c = lax.broadcasted_iota(jnp.int32, (CHUNK, CHUNK), 1)
    return jnp.where((c <= r) if lower else (c >= r), 1.0, 0.0).astype(BF16)


def _causal():
    r = lax.broadcasted_iota(jnp.int32, (CHUNK, CHUNK), 0)
    c = lax.broadcasted_iota(jnp.int32, (CHUNK, CHUNK), 1)
    return c <= r


def _shift_down(x, sh, prev_tail):
    r = pltpu.roll(x, sh, 0)
    pt = pltpu.roll(prev_tail, sh, 0)
    rows = lax.broadcasted_iota(jnp.int32, prev_tail.shape, 0)
    top = jnp.where(rows < sh, pt, r[0:8])
    return jnp.concatenate([top, r[8:]], axis=0)


def _shift_up(x, sh, next_head):
    n = x.shape[0]
    r = pltpu.roll(x, n - sh, 0)
    nh = pltpu.roll(next_head, 8 - sh, 0)
    rows = lax.broadcasted_iota(jnp.int32, next_head.shape, 0)
    bot = jnp.where(rows >= 8 - sh, nh, r[n - 8:])
    return jnp.concatenate([r[:n - 8], bot], axis=0)


def _group_matrix(width, group):
    r = np.arange(width)[:, None] // group
    c = np.arange(width)[None, :] // group
    return jnp.asarray(np.where(r == c, 1.0 / group, 0.0), dtype=BF16)


TG = 1024
SEM_W, SEM_CW, SEM_W_FWD, N_SEM = 0, 4, 7, 11


def _gather_proj(kidx, x2d, g1, w_in, conv_w):
    half_w = D_MODEL // 2
    half_c = SHARD_COLS // 2
    nt = SEQ // TG
    n_steps = 2 * N_SHARD

    def body(k_ref, x_ref, g_ref, w_ref, cw_ref, h_ref, p_ref, wg_out, cwg_out,
             wg_v, cwg_v, send_sems, recv_sems, out_sems):
        s, t = pl.program_id(0), pl.program_id(1)
        x, y, c = lax.axis_index("x"), lax.axis_index("y"), lax.axis_index("c")
        k = 2 * x + y
        sibling = (x, y, 1 - c)
        chips = [(1 - x, y), (x, 1 - y), (1 - x, 1 - y)]
        kjs = [2 * cx + cy for cx, cy in chips]
        diag = (*chips[2], c)

        def w_half(kk, cc):
            return wg_v.at[kk, pl.ds(cc * half_w, half_w), :]

        def w_quarter(kk, cc, piece):
            return wg_v.at[kk, pl.ds(cc * half_w, half_w), piece * half_c:(piece + 1) * half_c]

        def cw_of(kk):
            return cwg_v.at[:, pl.ds(pl.multiple_of(kk * HEAD, HEAD), HEAD)]

        def copy(sem, ref, to):
            return pltpu.make_async_remote_copy(
                src_ref=ref, dst_ref=ref, send_sem=send_sems.at[sem], recv_sem=recv_sems.at[sem],
                device_id=to, device_id_type=MESH)

        def at_step(sv, tv):
            return pl.when((s == sv) & (t == tv))

        w_direct = ([copy(SEM_W + j, w_half(k, c), (*chips[j], c)) for j in range(2)]
                    + [copy(SEM_W + 2 + p, w_quarter(k, c, p), diag) for p in range(2)])
        cw_direct = [copy(SEM_CW + j, cw_of(k), (*chip, c)) for j, chip in enumerate(chips)]
        w_passed = ([copy(SEM_W_FWD + j, w_half(kjs[j], c), sibling) for j in range(2)]
                    + [copy(SEM_W_FWD + 2 + p, w_quarter(kjs[2], c, p), sibling) for p in range(2)])
        stores = ([pltpu.make_async_copy(wg_v.at[kk], wg_out.at[kk], out_sems.at[i])
                   for i, kk in enumerate([k] + kjs)]
                  + [pltpu.make_async_copy(cwg_v, cwg_out, out_sems.at[4])])

        @at_step(0, 0)
        def _():
            barrier = pltpu.get_barrier_semaphore()
            for peer in [sibling] + [(*chip, c) for chip in chips]:
                pl.semaphore_signal(barrier, inc=1, device_id=peer, device_id_type=MESH)
            wg_v[k] = w_ref[0].astype(BF16)
            mine = pl.ds(pl.multiple_of(k * HEAD, HEAD), HEAD)
            cwg_v[:, mine] = jnp.zeros((8, HEAD), F32)
            for tap in range(3):
                cwg_v[tap:tap + 1, mine] = cw_ref[:, tap * HEAD:(tap + 1) * HEAD]
            pl.semaphore_wait(barrier, 4)
            for cp in w_direct + cw_direct:
                cp.start()
            stores[0].start()

        @at_step(2, 0)
        def _():
            for j in range(2):
                copy(SEM_W + j, w_half(kjs[j], c), sibling).wait_recv()
                w_passed[j].start()
            copy(SEM_W_FWD, w_half(kjs[0], 1 - c), sibling).wait_recv()
            stores[1].start()

        @at_step(4, 0)
        def _():
            copy(SEM_W_FWD + 1, w_half(kjs[1], 1 - c), sibling).wait_recv()
            stores[2].start()

        for p in range(2):
            @at_step(6 + p, 0)
            def _(p=p):
                copy(SEM_W + 2 + p, w_quarter(kjs[2], c, p), sibling).wait_recv()
                w_passed[2 + p].start()
                copy(SEM_W_FWD + 2 + p, w_quarter(kjs[2], 1 - c, p), sibling).wait_recv()

        rows = pl.ds(pl.multiple_of(t * TG, TG), TG)

        @pl.when(s == 0)
        def _():
            xv = x_ref[...]
            r = lax.rsqrt(jnp.mean(xv * xv, axis=-1, keepdims=True) + EPS)
            h_ref[rows, :] = (xv * r * g_ref[...]).astype(BF16)

        sh = s >> 1
        js = k ^ (((sh & 1) << 1) | (sh >> 1))
        for piece in range(2):
            @pl.when((s & 1) == piece)
            def _(piece=piece):
                p_ref[...] = _dot(h_ref[rows, :], wg_v[js, :, piece * half_c:(piece + 1) * half_c])

        @at_step(n_steps - 1, nt - 1)
        def _():
            stores[3].start()
            for j in range(3):
                copy(SEM_CW + j, cw_of(kjs[j]), sibling).wait_recv()
            stores[4].start()
            for cp in w_direct + cw_direct + w_passed:
                cp.wait_send()
            for st in stores:
                st.wait()

    def x_map(s, t, kr):
        return (jnp.where(s == 0, t, nt - 1), 0)

    def p_map(s, t, kr):
        sh = s >> 1
        return (t, 2 * (kr[0] ^ (((sh & 1) << 1) | (sh >> 1))) + (s & 1))

    hbm = pl.BlockSpec(memory_space=pl.ANY)
    grid_spec = pltpu.PrefetchScalarGridSpec(
        num_scalar_prefetch=1, grid=(n_steps, nt),
        in_specs=[pl.BlockSpec((TG, D_MODEL), x_map),
                  pl.BlockSpec((1, D_MODEL), lambda s, t, kr: (0, 0)),
                  pl.BlockSpec((1, D_MODEL, SHARD_COLS), lambda s, t, kr: (0, 0, 0)),
                  pl.BlockSpec((1, 3 * HEAD), lambda s, t, kr: (0, 0))],
        out_specs=(pl.BlockSpec((SEQ, D_MODEL), lambda s, t, kr: (0, 0)),
                   pl.BlockSpec((TG, half_c), p_map), hbm, hbm),
        scratch_shapes=[pltpu.VMEM((N_SHARD, D_MODEL, SHARD_COLS), BF16),
                        pltpu.VMEM((8, D_CONV), F32),
                        pltpu.SemaphoreType.DMA((N_SEM,)), pltpu.SemaphoreType.DMA((N_SEM,)),
                        pltpu.SemaphoreType.DMA((5,))])
    return pl.pallas_call(
        body, name="gather_proj", grid_spec=grid_spec,
        out_shape=(jax.ShapeDtypeStruct((SEQ, D_MODEL), BF16),
                   jax.ShapeDtypeStruct((SEQ, N_SHARD * SHARD_COLS), F32),
                   jax.ShapeDtypeStruct((N_SHARD, D_MODEL, SHARD_COLS), BF16),
                   jax.ShapeDtypeStruct((8, D_CONV), F32)),
        compiler_params=pltpu.CompilerParams(dimension_semantics=("arbitrary", "arbitrary"),
                                             vmem_limit_bytes=VMEM_LIMIT, collective_id=COLLECTIVE_GATHER),
    )(kidx, x2d, g1, w_in, conv_w)


LAG = 6


def _mix_out(proj, lb_logits, cw, ga, gcn, g64, w_out, x2d, gf, tgt):
    half_o = WO_ROWS // 2
    nblk = SEQ // TB
    n_steps = nblk + LAG

    def body(p_ref, lbl_ref, cw_ref, ga_ref, gcn_ref, g64_ref, wo_ref, x_ref, gf_ref, t_ref,
             aux_ref, sto_ref, dx2_ref, dm_ref, gwo_ref, part_ref,
             st_ref, tail_ref, wog_v, stage, ring, acc_ref, send_sems, recv_sems):
        i = pl.program_id(0)
        x, y, c = lax.axis_index("x"), lax.axis_index("y"), lax.axis_index("c")
        k = 2 * x + y
        sibling = (x, y, 1 - c)
        chips = [(1 - x, y), (x, 1 - y), (1 - x, 1 - y)]
        kjs = [2 * cx + cy for cx, cy in chips]

        def wo_half(kk, cc):
            return wog_v.at[pl.ds(pl.multiple_of(kk * WO_ROWS + cc * half_o, half_o), half_o), :]

        def copy(sem, ref, to):
            return pltpu.make_async_remote_copy(
                src_ref=ref, dst_ref=ref, send_sem=send_sems.at[sem], recv_sem=recv_sems.at[sem],
                device_id=to, device_id_type=MESH)

        wo_direct = [copy(j, wo_half(k, c), (*chip, c)) for j, chip in enumerate(chips)]
        wo_passed = [copy(3 + j, wo_half(kj, c), sibling) for j, kj in enumerate(kjs)]

        @pl.when(i == 0)
        def _():
            barrier = pltpu.get_barrier_semaphore()
            for peer in [sibling] + [(*chip, c) for chip in chips]:
                pl.semaphore_signal(barrier, inc=1, device_id=peer, device_id_type=MESH)
            st_ref[...] = jnp.zeros_like(st_ref)
            tail_ref[...] = jnp.zeros_like(tail_ref)
            acc_ref[...] = jnp.zeros_like(acc_ref)
            part_ref[...] = jnp.zeros_like(part_ref)
            wog_v[pl.ds(pl.multiple_of(k * WO_ROWS, WO_ROWS), WO_ROWS), :] = wo_ref[0].astype(BF16)
            pl.semaphore_wait(barrier, 4)
            for cp in wo_direct:
                cp.start()

        @pl.when(i == LAG - 1)
        def _():
            for j in range(3):
                copy(j, wo_half(kjs[j], c), sibling).wait_recv()
                wo_passed[j].start()

        @pl.when(i == LAG)
        def _():
            for j in range(3):
                copy(3 + j, wo_half(kjs[j], 1 - c), sibling).wait_recv()

        lb = _lower_bound(lbl_ref[...])
        tri = _tri(True)
        causal = _causal()
        g64m = g64_ref[...]
        heads = range(N_HEADS)
        cs = [slice(hd * HEAD, (hd + 1) * HEAD) for hd in heads]
        col = lambda base, hd: slice(base + hd * HEAD, base + (hd + 1) * HEAD)

        def mix_chunk(n):
            sl = pl.ds(n * CHUNK, CHUNK)
            sg = [_sigmoid(p_ref[sl, col(512, hd)]) for hd in heads]
            f = [lb[:, cs[hd]] + (1.0 - lb[:, cs[hd]]) * sg[hd] for hd in heads]
            bc = _exact_left_many(tri, [jnp.log(f[hd]) for hd in heads])
            for hd in heads:
                aux_ref[sl, col(AUX_B, hd)] = bc[hd]
            g = [bc[hd][CHUNK - 1:CHUNK, :] for hd in heads]
            qd = [(p_ref[sl, col(0, hd)] * jnp.exp(bc[hd])).astype(BF16) for hd in heads]
            kk = [1.0 - f[hd] for hd in heads]
            ki = [(kk[hd] * jnp.exp(-bc[hd])).astype(BF16) for hd in heads]
            ke = [(kk[hd] * jnp.exp(g[hd] - bc[hd])).astype(BF16) for hd in heads]
            vb = [p_ref[sl, col(1024, hd)].astype(BF16) for hd in heads]
            st = [st_ref[hd] for hd in heads]
            st_b = [a.astype(BF16) for a in st]
            for hd in heads:
                sto_ref[n, hd] = st_b[hd]
            scm = [_dot_nt(qd[hd], ki[hd]) for hd in heads]
            inter = [_dot_nt(qd[hd], st_b[hd]) for hd in heads]
            upd = [_dot_tn(vb[hd], ke[hd]) for hd in heads]
            intra = [_dot(jnp.where(causal, scm[hd], 0.0).astype(BF16), vb[hd]) for hd in heads]
            for hd in heads:
                st_ref[hd] = st[hd] * jnp.exp(g[hd]) + upd[hd]
                o = intra[hd] + inter[hd]
                aux_ref[sl, col(AUX_O, hd)] = o
                ra = lax.rsqrt(jnp.mean(o * o, axis=-1, keepdims=True) + EPS)
                za = p_ref[sl, col(1536, hd)]
                stage[sl, cs[hd]] = (o * ra * ga_ref[:, cs[hd]] * (za * _sigmoid(za))).astype(BF16)
            yb = []
            for hd in heads:
                cu = p_ref[sl, col(3072, hd)] * p_ref[sl, col(2048, hd)]
                tail = tail_ref[:, cs[hd]]
                cv = (cw_ref[0:1, cs[hd]] * _shift_down(cu, 2, tail) + cw_ref[1:2, cs[hd]] * _shift_down(cu, 1, tail)
                      + cw_ref[2:3, cs[hd]] * cu)
                tail_ref[:, cs[hd]] = cu[CHUNK - 8:, :]
                aux_ref[sl, col(AUX_CV, hd)] = cv
                yb.append(p_ref[sl, col(2560, hd)] * cv)
            ms = _group_mean_many([y * y for y in yb], g64m)
            for hd in heads:
                rb = lax.rsqrt(ms[hd] + EPS)
                zb = p_ref[sl, col(3584, hd)]
                stage[sl, col(512, hd)] = (yb[hd] * rb * gcn_ref[:, cs[hd]] * (zb * _sigmoid(zb))).astype(BF16)

        def step(mix, project):
            if project:
                mixed_b = ring[pl.ds(pl.multiple_of((i - LAG) * TB, TB), TB), :]
                y = _dot(mixed_b, wog_v[...])
            if mix:
                mix_chunk(0)
            if project:
                x2 = x_ref[...] + y
                r2 = lax.rsqrt(jnp.mean(x2 * x2, axis=-1, keepdims=True) + EPS)
                n2 = x2 * r2
                gfv = gf_ref[...]
                err = n2 * gfv - t_ref[...]
                loss = 0.5 * jnp.sum(jnp.mean(err * err, axis=-1, keepdims=True), axis=0, keepdims=True)
                dy = err * (1.0 / D_MODEL)
                part_ref[1:2, :] += jnp.sum(dy * n2, axis=0, keepdims=True)
                part_ref[7:8, :] += jnp.broadcast_to(loss, (1, D_MODEL))
                dn = dy * gfv
                dx2 = r2 * (dn - n2 * jnp.mean(dn * n2, axis=-1, keepdims=True))
                dx2_ref[...] = dx2
                dx2_b = dx2.astype(BF16)
            if mix:
                mix_chunk(1)
            if project:
                dm_ref[...] = _dot_nt(dx2_b, wog_v[...])
            if mix:
                mix_chunk(2)
            if project:
                acc_ref[...] += _dot_tn(mixed_b, dx2_b)
            if mix:
                mix_chunk(3)
                ring[pl.ds(pl.multiple_of(i * TB, TB), TB), :] = stage[...]

        @pl.when(i < LAG)
        def _():
            step(True, False)

        @pl.when((i >= LAG) & (i < nblk))
        def _():
            step(True, True)

        @pl.when(i >= nblk)
        def _():
            step(False, True)

        @pl.when(i == n_steps - 1)
        def _():
            gwo_ref[...] = acc_ref[...].astype(BF16)
            for cp in wo_direct + wo_passed:
                cp.wait_send()

    assert NCB == 4
    row = lambda w: pl.BlockSpec((1, w), lambda i: (0, 0))
    mix_blk = lambda i: jnp.minimum(i, nblk - 1)
    out_blk = lambda i: jnp.clip(i - LAG, 0, nblk - 1)
    tok = lambda: pl.BlockSpec((TB, D_MODEL), lambda i: (out_blk(i), 0))
    return pl.pallas_call(
        body, name="mix_out", grid=(n_steps,),
        out_shape=(jax.ShapeDtypeStruct((SEQ, AUX_COLS), F32),
                   jax.ShapeDtypeStruct((N_CHUNKS, N_HEADS, HEAD, HEAD), BF16),
                   jax.ShapeDtypeStruct((SEQ, D_MODEL), F32),
                   jax.ShapeDtypeStruct((SEQ, D_MODEL), F32),
                   jax.ShapeDtypeStruct((D_MODEL, D_MODEL), BF16),
                   jax.ShapeDtypeStruct((8, D_MODEL), F32)),
        in_specs=[pl.BlockSpec((TB, 4096), lambda i: (jnp.minimum(i, nblk - 1), 0)),
                  pl.BlockSpec((2, D_HGRN), lambda i: (0, 0)),
                  pl.BlockSpec((8, D_CONV), lambda i: (0, 0)),
                  row(D_HGRN), row(D_CONV),
                  pl.BlockSpec((HEAD, HEAD), lambda i: (0, 0)),
                  pl.BlockSpec((1, WO_ROWS, D_MODEL), lambda i: (0, 0, 0)),
                  tok(), row(D_MODEL), tok()],
        out_specs=(pl.BlockSpec((TB, AUX_COLS), lambda i: (mix_blk(i), 0)),
                   pl.BlockSpec((NCB, N_HEADS, HEAD, HEAD), lambda i: (mix_blk(i), 0, 0, 0)),
                   tok(), tok(),
                   pl.BlockSpec((D_MODEL, D_MODEL), lambda i: (0, 0)),
                   pl.BlockSpec((8, D_MODEL), lambda i: (0, 0))),
        scratch_shapes=[pltpu.VMEM((N_HEADS, HEAD, HEAD), F32), pltpu.VMEM((8, D_CONV), F32),
                        pltpu.VMEM((D_MODEL, D_MODEL), BF16), pltpu.VMEM((TB, D_MODEL), BF16),
                        pltpu.VMEM((SEQ, D_MODEL), BF16), pltpu.VMEM((D_MODEL, D_MODEL), F32),
                        pltpu.SemaphoreType.DMA((6,)), pltpu.SemaphoreType.DMA((6,))],
        compiler_params=pltpu.CompilerParams(dimension_semantics=("arbitrary",), vmem_limit_bytes=VMEM_LIMIT,
                                             collective_id=COLLECTIVE_MIX_OUT),
    )(proj, lb_logits, cw, ga, gcn, g64, w_out, x2d, gf, tgt)


def _mix_bwd(proj, aux, states, dmixed, lb_logits, cw, ga, gcn, g64):
    nblk = SEQ // TB

    def body(p_ref, aux_ref, st_ref, dm_ref, lbl_ref, cw_ref, ga_ref, gcn_ref, g64_ref,
             dp_ref, part_ref, dst_ref, head_ref, dlb_ref):
        i = pl.program_id(0)

        @pl.when(i == 0)
        def _():
            dst_ref[...] = jnp.zeros_like(dst_ref)
            head_ref[...] = jnp.zeros_like(head_ref)
            part_ref[...] = jnp.zeros_like(part_ref)
            dlb_ref[...] = jnp.zeros_like(dlb_ref)

        lb = _lower_bound(lbl_ref[...])
        triu = _tri(False)
        causal = _causal()
        g64m = g64_ref[...]
        rowsum = lambda a: jnp.sum(a, axis=0, keepdims=True)
        heads = range(N_HEADS)
        cs = [slice(hd * HEAD, (hd + 1) * HEAD) for hd in heads]
        col = lambda base, hd: slice(base + hd * HEAD, base + (hd + 1) * HEAD)
        for n in reversed(range(NCB)):
            sl = pl.ds(n * CHUNK, CHUNK)
            cvv = [aux_ref[sl, col(AUX_CV, hd)] for hd in heads]
            gb = [p_ref[sl, col(2560, hd)] for hd in heads]
            yb = [gb[hd] * cvv[hd] for hd in heads]
            ms = _group_mean_many([y * y for y in yb], g64m)
            rb, nb, dnb = [], [], []
            for hd in heads:
                rb.append(lax.rsqrt(ms[hd] + EPS))
                nb.append(yb[hd] * rb[hd])
                zb = p_ref[sl, col(3584, hd)]
                sgb = _sigmoid(zb)
                dmb = dm_ref[sl, col(512, hd)]
                silu = zb * sgb
                dgate = dmb * gcn_ref[:, cs[hd]]
                part_ref[2:3, col(512, hd)] += rowsum(dmb * nb[hd] * silu)
                dp_ref[sl, col(3584, hd)] = (dgate * nb[hd] * (sgb + silu * (1.0 - sgb))).astype(BF16)
                dnb.append(dgate * silu)
            mdn = _group_mean_many([dnb[hd] * nb[hd] for hd in heads], g64m)
            for hd in heads:
                dyb = rb[hd] * (dnb[hd] - nb[hd] * mdn[hd])
                dp_ref[sl, col(2560, hd)] = (dyb * cvv[hd]).astype(BF16)
                dcv = dyb * gb[hd]
                head = head_ref[:, cs[hd]]
                dcv1 = _shift_up(dcv, 1, head)
                dcv2 = _shift_up(dcv, 2, head)
                head_ref[:, cs[hd]] = dcv[0:8, :]
                u = p_ref[sl, col(2048, hd)]
                gc = p_ref[sl, col(3072, hd)]
                cu = gc * u
                part_ref[4:5, cs[hd]] += rowsum(dcv2 * cu)
                part_ref[5:6, cs[hd]] += rowsum(dcv1 * cu)
                part_ref[6:7, cs[hd]] += rowsum(dcv * cu)
                dcu = cw_ref[2:3, cs[hd]] * dcv + cw_ref[1:2, cs[hd]] * dcv1 + cw_ref[0:1, cs[hd]] * dcv2
                dp_ref[sl, col(3072, hd)] = (dcu * u).astype(BF16)
                dp_ref[sl, col(2048, hd)] = (dcu * gc).astype(BF16)
            do_b = []
            for hd in heads:
                ov = aux_ref[sl, col(AUX_O, hd)]
                ra = lax.rsqrt(jnp.mean(ov * ov, axis=-1, keepdims=True) + EPS)
                na = ov * ra
                za = p_ref[sl, col(1536, hd)]
                sga = _sigmoid(za)
                dma = dm_ref[sl, cs[hd]]
                silu = za * sga
                dgate = dma * ga_ref[:, cs[hd]]
                part_ref[2:3, cs[hd]] += rowsum(dma * na * silu)
                dp_ref[sl, col(1536, hd)] = (dgate * na * (sga + silu * (1.0 - sga))).astype(BF16)
                dna = dgate * silu
                do_b.append((ra * (dna - na * jnp.mean(dna * na, axis=-1, keepdims=True))).astype(BF16))
            s = [_sigmoid(p_ref[sl, col(512, hd)]) for hd in heads]
            f = [lb[:, cs[hd]] + (1.0 - lb[:, cs[hd]]) * s[hd] for hd in heads]
            bc = [aux_ref[sl, col(AUX_B, hd)] for hd in heads]
            g = [bc[hd][CHUNK - 1:CHUNK, :] for hd in heads]
            eb = [jnp.exp(bc[hd]) for hd in heads]
            enb = [jnp.exp(-bc[hd]) for hd in heads]
            eg = [jnp.exp(g[hd] - bc[hd]) for hd in heads]
            dec = [jnp.exp(g[hd]) for hd in heads]
            qd = [p_ref[sl, cs[hd]] * eb[hd] for hd in heads]
            kk = [1.0 - f[hd] for hd in heads]
            ki = [kk[hd] * enb[hd] for hd in heads]
            ke = [kk[hd] * eg[hd] for hd in heads]
            qd_b = [a.astype(BF16) for a in qd]
            ki_b = [a.astype(BF16) for a in ki]
            ke_b = [a.astype(BF16) for a in ke]
            vb = [p_ref[sl, col(1024, hd)].astype(BF16) for hd in heads]
            st_b = [st_ref[n, hd] for hd in heads]
            dst = [dst_ref[hd] for hd in heads]
            dst_b = [a.astype(BF16) for a in dst]
            scm = [_dot_nt(qd_b[hd], ki_b[hd]) for hd in heads]
            amm = [_dot_nt(do_b[hd], vb[hd]) for hd in heads]
            dqd2 = [_dot(do_b[hd], st_b[hd]) for hd in heads]
            dke = [_dot(vb[hd], dst_b[hd]) for hd in heads]
            dv2 = [_dot_nt(ke_b[hd], dst_b[hd]) for hd in heads]
            dsu = [_dot_tn(do_b[hd], qd_b[hd]) for hd in heads]
            sc = [jnp.where(causal, scm[hd], 0.0).astype(BF16) for hd in heads]
            am = [jnp.where(causal, amm[hd], 0.0).astype(BF16) for hd in heads]
            dqd1 = [_dot(am[hd], ki_b[hd]) for hd in heads]
            dki = [_dot_tn(am[hd], qd_b[hd]) for hd in heads]
            dv1 = [_dot_tn(sc[hd], do_b[hd]) for hd in heads]
            db, dgv, dkk = [], [], []
            for hd in heads:
                dqd = dqd1[hd] + dqd2[hd]
                ddec = rowsum(dst[hd] * st_b[hd].astype(F32))
                dst_ref[hd] = dst[hd] * dec[hd] + dsu[hd]
                dp_ref[sl, cs[hd]] = (dqd * eb[hd]).astype(BF16)
                dp_ref[sl, col(1024, hd)] = (dv1[hd] + dv2[hd]).astype(BF16)
                dke_eg = dke[hd] * eg[hd]
                dkk.append(dki[hd] * enb[hd] + dke_eg)
                db.append(dqd * qd[hd] - kk[hd] * dkk[hd])
                dgv.append(rowsum(kk[hd] * dke_eg) + ddec * dec[hd])
            rc = _exact_left_many(triu, db, 2)
            for hd in heads:
                df = (rc[hd] + dgv[hd]) / f[hd] - dkk[hd]
                one_s = 1.0 - s[hd]
                dlb_ref[:, cs[hd]] += rowsum(df * one_s)
                dp_ref[sl, col(512, hd)] = (df * (1.0 - lb[:, cs[hd]]) * s[hd] * one_s).astype(BF16)

        @pl.when(i == nblk - 1)
        def _():
            row = dlb_ref[...] * lb * (1.0 - lb)
            part_ref[3:4, 0:D_HGRN] = row
            part_ref[3:4, D_HGRN:] = -row

    rev = lambda w: pl.BlockSpec((TB, w), lambda i: (nblk - 1 - i, 0))
    row = lambda w: pl.BlockSpec((1, w), lambda i: (0, 0))
    return pl.pallas_call(
        body, name="mix_bwd", grid=(nblk,),
        out_shape=(jax.ShapeDtypeStruct((SEQ, 4096), BF16),
                   jax.ShapeDtypeStruct((8, D_MODEL), F32)),
        in_specs=[rev(4096), rev(AUX_COLS),
                  pl.BlockSpec((NCB, N_HEADS, HEAD, HEAD), lambda i: (nblk - 1 - i, 0, 0, 0)),
                  rev(D_MODEL),
                  pl.BlockSpec((2, D_HGRN), lambda i: (0, 0)),
                  pl.BlockSpec((8, D_CONV), lambda i: (0, 0)),
                  row(D_HGRN), row(D_CONV),
                  pl.BlockSpec((HEAD, HEAD), lambda i: (0, 0))],
        out_specs=(rev(4096), pl.BlockSpec((8, D_MODEL), lambda i: (0, 0))),
        scratch_shapes=[pltpu.VMEM((N_HEADS, HEAD, HEAD), F32), pltpu.VMEM((8, D_CONV), F32),
                        pltpu.VMEM((1, D_HGRN), F32)],
        compiler_params=pltpu.CompilerParams(dimension_semantics=("arbitrary",), vmem_limit_bytes=VMEM_LIMIT),
    )(proj, aux, states, dmixed, lb_logits, cw, ga, gcn, g64)


TT = 1024
TX = 256
(SEM_D2D, SEM_D2D_O, SEM_ICI, SEM_ICI_O, SEM_FIN, SEM_FIN_O, SEM_SMALL, N_SEM_TAIL) = 0, 4, 5, 8, 11, 12, 12, 20


def _bwd_tail(kidx, h, dproj, wg, gwo, x2d, dx2, g1, small_a, small_b):
    hw = D_MODEL // 2
    ho = WO_ROWS // 2
    nt = SEQ // TT
    norm_step = 2 * N_SHARD
    n_steps = norm_step + SEQ // TX // nt

    def body(k_ref, h_ref, dp_ref, w_ref, gwo_ref, x_ref, dx2_ref, g_ref, sm_ref, smb_ref,
             gx_ref, gw_out, gwo_out, osm_ref,
             acc, dh, sendbuf, keep, sibrcv, rcv, sib_o, p_o, rcv_o, res_o, sm_buf, dng,
             send_sems, recv_sems, out_sems):
        s, t = pl.program_id(0), pl.program_id(1)
        x, y, c = lax.axis_index("x"), lax.axis_index("y"), lax.axis_index("c")
        k = 2 * x + y
        me = 4 * x + 2 * y + c
        sibling = (x, y, 1 - c)
        chips = [(1 - x, 1 - y), (1 - x, y), (x, 1 - y)]
        kjs = [2 * cx + cy for cx, cy in chips]
        mine = pl.ds(pl.multiple_of(c * hw, hw), hw)
        other = pl.ds(pl.multiple_of((1 - c) * hw, hw), hw)
        mine_o = pl.ds(pl.multiple_of(c * ho, ho), ho)
        other_o = pl.ds(pl.multiple_of((1 - c) * ho, ho), ho)

        def copy(sem, src, dst, to):
            return pltpu.make_async_remote_copy(
                src_ref=src, dst_ref=dst, send_sem=send_sems.at[sem], recv_sem=recv_sems.at[sem],
                device_id=to, device_id_type=MESH)

        def at_step(sv, tv):
            return pl.when((s == sv) & (t == tv))

        def at_norm_block(b):
            return at_step(norm_step + b // nt, b % nt)

        d2d = [copy(SEM_D2D + sv, sendbuf.at[sv], sibrcv.at[sv], sibling) for sv in range(N_SHARD)]
        d2d_o = copy(SEM_D2D_O, gwo_ref.at[:, other_o, :], sib_o, sibling)
        ici = [copy(SEM_ICI + sv, keep.at[sv], rcv.at[sv], (*chips[sv], c)) for sv in range(3)]
        ici_o = [copy(SEM_ICI_O + sv, p_o.at[kjs[sv]], rcv_o.at[sv], (*chips[sv], c)) for sv in range(3)]
        fin = copy(SEM_FIN, acc.at[mine, :], gw_out.at[mine, :], sibling)
        fin_o = copy(SEM_FIN_O, res_o.at[mine_o, :], res_o.at[mine_o, :], sibling)
        smalls = [copy(SEM_SMALL + m, sm_buf.at[me], sm_buf.at[me],
                       (x ^ (m >> 2), y ^ ((m >> 1) & 1), c ^ (m & 1))) for m in range(1, N_DEV)]
        store_w = pltpu.make_async_copy(acc.at[mine, :], gw_out.at[mine, :], out_sems.at[0])
        store_o = pltpu.make_async_copy(res_o, gwo_out, out_sems.at[1])

        @at_step(0, 0)
        def _():
            barrier = pltpu.get_barrier_semaphore()
            for m in range(1, N_DEV):
                pl.semaphore_signal(barrier, inc=1, device_id=(x ^ (m >> 2), y ^ ((m >> 1) & 1), c ^ (m & 1)),
                                    device_id_type=MESH)
            pl.semaphore_wait(barrier, N_DEV - 1)
            d2d_o.start()

        @at_step(0, 1)
        def _():
            d2d_o.wait_recv()
            for j in range(N_SHARD):
                p_o[j] = (gwo_ref[j, mine_o, :].astype(F32) + sib_o[j].astype(F32)).astype(BF16)
            res_o[mine_o, :] = gwo_ref[k, mine_o, :].astype(F32) + sib_o[k].astype(F32)
            for cp in ici_o:
                cp.start()

        rows = pl.ds(pl.multiple_of(t * TT, TT), TT)

        @pl.when((s < N_SHARD) & (t == 0))
        def _():
            acc[...] = _dot_tn(h_ref[...], dp_ref[...])

        @pl.when((s < N_SHARD) & (t > 0))
        def _():
            acc[...] += _dot_tn(h_ref[...], dp_ref[...])

        for sv in range(N_SHARD):
            @at_step(sv, nt - 1)
            def _(sv=sv):
                sendbuf[sv] = acc[other, :].astype(BF16)
                if sv < 3:
                    keep[sv] = acc[mine, :].astype(BF16)
                d2d[sv].start()

        for sv in range(3):
            @at_step(sv + 1, 0)
            def _(sv=sv):
                d2d[sv].wait_recv()
                keep[sv] = (keep[sv].astype(F32) + sibrcv[sv].astype(F32)).astype(BF16)
                ici[sv].start()

        @pl.when(s == N_SHARD)
        def _():
            dh[rows, :] = _dot_nt(dp_ref[...], w_ref[0])

        @pl.when((s > N_SHARD) & (s < norm_step))
        def _():
            dh[rows, :] += _dot_nt(dp_ref[...], w_ref[0])

        @at_norm_block(0)
        def _():
            d2d[3].wait_recv()
            ici[0].wait_recv()
            acc[mine, :] += sibrcv[3].astype(F32) + rcv[0].astype(F32)

        @at_norm_block(1)
        def _():
            tot = res_o[mine_o, :]
            for sv in range(3):
                ici_o[sv].wait_recv()
                tot = tot + rcv_o[sv].astype(F32)
            res_o[mine_o, :] = tot
            fin_o.start()

        @at_norm_block(2)
        def _():
            ici[1].wait_recv()
            acc[mine, :] += rcv[1].astype(F32)

        @at_norm_block(SEQ // TX - 2)
        def _():
            ici[2].wait_recv()
            acc[mine, :] += rcv[2].astype(F32)
            fin.start()
            store_w.start()
            fin_o.wait_recv()
            store_o.start()

        @at_norm_block(0)
        def _():
            dng[...] = jnp.zeros_like(dng)

        @pl.when(s >= norm_step)
        def _():
            blk = (s - norm_step) * nt + t
            dhv = dh[pl.ds(pl.multiple_of(blk * TX, TX), TX), :]
            xv = x_ref[...]
            r = lax.rsqrt(jnp.mean(xv * xv, axis=-1, keepdims=True) + EPS)
            xn = xv * r
            dng[...] += jnp.sum(dhv * xn, axis=0, keepdims=True)
            dxn = dhv * g_ref[...]
            gx_ref[...] = dx2_ref[...] + r * (dxn - xn * jnp.mean(dxn * xn, axis=-1, keepdims=True))

        @at_step(n_steps - 1, nt - 1)
        def _():
            sm_buf[me] = sm_ref[...] + smb_ref[...]
            sm_buf[me, 0:1, :] = dng[...]
            for cp in smalls:
                cp.start()
            for m in range(1, N_DEV):
                copy(SEM_SMALL + m, sm_buf.at[0], sm_buf.at[0], sibling).wait_recv()
            tot = sm_buf[0]
            for d in range(1, N_DEV):
                tot = tot + sm_buf[d]
            osm_ref[...] = tot
            fin.wait_recv()
            for cp in d2d + [d2d_o] + ici + ici_o + [fin, fin_o] + smalls:
                cp.wait_send()
            store_o.wait()
            store_w.wait()

    def shard_of(s, kr):
        order = jnp.where(s < N_SHARD, s, jnp.where(s < norm_step, s - N_SHARD, 3))
        return kr[0] ^ (3 - order)

    def h_map(s, t, kr):
        return (jnp.where(s < N_SHARD, t, nt - 1), 0)

    def dp_map(s, t, kr):
        return (jnp.where(s < norm_step, t, nt - 1), shard_of(s, kr))

    def w_map(s, t, kr):
        return (shard_of(jnp.maximum(s, N_SHARD), kr), 0, 0)

    def blk_map(s, t, kr):
        return (jnp.where(s < norm_step, 0, (s - norm_step) * nt + t), 0)

    hbm = pl.BlockSpec(memory_space=pl.ANY)
    grid_spec = pltpu.PrefetchScalarGridSpec(
        num_scalar_prefetch=1, grid=(n_steps, nt),
        in_specs=[pl.BlockSpec((TT, D_MODEL), h_map),
                  pl.BlockSpec((TT, SHARD_COLS), dp_map),
                  pl.BlockSpec((1, D_MODEL, SHARD_COLS), w_map),
                  pl.BlockSpec((N_SHARD, WO_ROWS, D_MODEL), lambda s, t, kr: (0, 0, 0)),
                  pl.BlockSpec((TX, D_MODEL), blk_map),
                  pl.BlockSpec((TX, D_MODEL), blk_map),
                  pl.BlockSpec((1, D_MODEL), lambda s, t, kr: (0, 0)),
                  pl.BlockSpec((8, D_MODEL), lambda s, t, kr: (0, 0)),
                  pl.BlockSpec((8, D_MODEL), lambda s, t, kr: (0, 0))],
        out_specs=(pl.BlockSpec((TX, D_MODEL), blk_map), hbm, hbm,
                   pl.BlockSpec((8, D_MODEL), lambda s, t, kr: (0, 0))),
        scratch_shapes=[pltpu.VMEM((D_MODEL, SHARD_COLS), F32), pltpu.VMEM((SEQ, D_MODEL), F32),
                        pltpu.VMEM((N_SHARD, hw, SHARD_COLS), BF16), pltpu.VMEM((3, hw, SHARD_COLS), BF16),
                        pltpu.VMEM((N_SHARD, hw, SHARD_COLS), BF16), pltpu.VMEM((3, hw, SHARD_COLS), BF16),
                        pltpu.VMEM((N_SHARD, ho, D_MODEL), BF16), pltpu.VMEM((N_SHARD, ho, D_MODEL), BF16),
                        pltpu.VMEM((3, ho, D_MODEL), BF16), pltpu.VMEM((WO_ROWS, D_MODEL), F32),
                        pltpu.VMEM((N_DEV, 8, D_MODEL), F32), pltpu.VMEM((1, D_MODEL), F32),
                        pltpu.SemaphoreType.DMA((N_SEM_TAIL,)), pltpu.SemaphoreType.DMA((N_SEM_TAIL,)),
                        pltpu.SemaphoreType.DMA((2,))])
    return pl.pallas_call(
        body, name="bwd_tail", grid_spec=grid_spec,
        out_shape=(jax.ShapeDtypeStruct((SEQ, D_MODEL), F32),
                   jax.ShapeDtypeStruct((D_MODEL, SHARD_COLS), F32),
                   jax.ShapeDtypeStruct((WO_ROWS, D_MODEL), F32),
                   jax.ShapeDtypeStruct((8, D_MODEL), F32)),
        compiler_params=pltpu.CompilerParams(dimension_semantics=("arbitrary", "arbitrary"),
                                             vmem_limit_bytes=60 * 1024 * 1024, collective_id=COLLECTIVE_TAIL),
    )(kidx, h, dproj, wg, gwo, x2d, dx2, g1, small_a, small_b)


def _adam_update(w, g, m, v):
    nm = ADAM_B1 * m + (1.0 - ADAM_B1) * g
    nv = ADAM_B2 * v + (1.0 - ADAM_B2) * (g * g)
    m_hat = nm / (1.0 - ADAM_B1 ** ADAM_STEP)
    v_hat = nv / (1.0 - ADAM_B2 ** ADAM_STEP)
    return -ADAM_LR * (m_hat / (jnp.sqrt(v_hat) + ADAM_EPS) + ADAM_WD * w), nm, nv


def _adamw_all(tot, g_w_in, g_w_out, big, small, grad_x):
    n = len(small)
    rows = WO_ROWS
    steps = D_MODEL // rows

    def body(tot_ref, *refs):
        gx_ref, gx_out = refs[2 + 3 * (2 + n)], refs[-1]
        gx_out[...] = gx_ref[...]
        ins, outs = refs[:2 + 3 * (2 + n)], refs[3 + 3 * (2 + n):-1]
        g_refs, wmv = ins[:2], ins[2:]
        loss_ref, quads = outs[0], outs[1:]

        def update(j, g):
            w_ref, m_ref, v_ref = wmv[3 * j:3 * j + 3]
            g_ref, d_ref, nm_ref, nv_ref = quads[4 * j:4 * j + 4]
            g_ref[...] = g
            d_ref[...], nm_ref[...], nv_ref[...] = _adam_update(w_ref[...], g, m_ref[...], v_ref[...])

        update(0, g_refs[0][...])

        @pl.when(pl.program_id(0) == 0)
        def _():
            update(1, g_refs[1][...])
            k = 2 * lax.axis_index("x") + lax.axis_index("y")
            mine = pl.ds(pl.multiple_of(k * HEAD, HEAD), HEAD)
            loss_ref[...] = tot_ref[7:8, 0:1]
            grads = [tot_ref[0:1, :], tot_ref[1:2, :], tot_ref[2:3, 0:D_HGRN], tot_ref[2:3, D_HGRN:],
                     jnp.concatenate([tot_ref[3:4, 0:D_HGRN], tot_ref[3:4, D_HGRN:]], axis=0),
                     jnp.concatenate([tot_ref[4 + tap:5 + tap, mine] for tap in range(3)], axis=1)]
            for j, g in enumerate(grads):
                update(2 + j, g)

    whole = lambda a: pl.BlockSpec(a.shape, lambda i: (0, 0))
    blk = pl.BlockSpec((rows, SHARD_COLS), lambda i: (i, 0))
    arrays = [a for triple in big + small for a in triple]
    in_specs = ([whole(tot), blk, whole(g_w_out)] + [blk] * 3 + [whole(a) for a in arrays[3:]])
    shapes = [big[0][0], big[1][0]] + [w for w, _, _ in small]
    out_shape = (jax.ShapeDtypeStruct((1, 1), F32),) + tuple(
        jax.ShapeDtypeStruct(w.shape, F32) for w in shapes for _ in range(4))
    out_specs = (pl.BlockSpec((1, 1), lambda i: (0, 0)),) + (blk,) * 4 + tuple(
        whole(w) for w in shapes[1:] for _ in range(4))
    gx_blk = pl.BlockSpec((SEQ // steps, D_MODEL), lambda i: (i, 0))
    outs = pl.pallas_call(
        body, name="adamw_all", grid=(steps,),
        out_shape=out_shape + (jax.ShapeDtypeStruct(grad_x.shape, F32),),
        in_specs=in_specs + [gx_blk], out_specs=out_specs + (gx_blk,),
        compiler_params=pltpu.CompilerParams(dimension_semantics=("arbitrary",), vmem_limit_bytes=VMEM_LIMIT),
    )(tot, g_w_in, g_w_out, *arrays, grad_x)
    return [outs[0]] + [outs[1 + 4 * j:5 + 4 * j] for j in range(2 + n)] + [outs[-1]]


def _local_step(x2d, tgt, proj, lb_logits, cw, ga, gcn, w_out, gf):
    g64 = _group_matrix(HEAD, CONV_GROUP)
    aux, states, dx2, dmixed, gwo, part_out = _mix_out(proj, lb_logits, cw, ga, gcn, g64, w_out, x2d, gf, tgt)
    dproj, part_mix = _mix_bwd(proj, aux, states, dmixed, lb_logits, cw, ga, gcn, g64)
    return dproj, dx2, gwo.reshape(N_SHARD, WO_ROWS, D_MODEL), part_out, part_mix


def kernel(x, norm_gain, w_in, lb_logits, conv_w, hgrn_norm_gain, conv_norm_gain, w_out, final_norm_gain, loss_target, m_norm_gain, m_w_in, m_lb_logits, m_conv_w, m_hgrn_norm_gain, m_conv_norm_gain, m_w_out, m_final_norm_gain, v_norm_gain, v_w_in, v_lb_logits, v_conv_w, v_hgrn_norm_gain, v_conv_norm_gain, v_w_out, v_final_norm_gain):
    k = 2 * lax.axis_index("x") + lax.axis_index("y")
    kidx = jnp.reshape(k, (1,)).astype(jnp.int32)
    row = lambda a: a.reshape(1, D_MODEL)
    taps = lambda a: a.reshape(1, 3 * HEAD)
    h, proj, wg, cw = _gather_proj(kidx, x[0], norm_gain, w_in, taps(conv_w))
    dproj, dx2, gwo, part_out, part_mix = _local_step(
        x[0], loss_target[0], proj, lb_logits, cw, hgrn_norm_gain, conv_norm_gain, w_out, row(final_norm_gain))
    rgrad_x, rg_w_in, rg_w_out, tot = _bwd_tail(kidx, h, dproj, wg, gwo, x[0], dx2, norm_gain, part_out, part_mix)

    (loss, (g_w_in, d_w_in, nm_w_in, nv_w_in), (g_w_out, d_w_out, nm_w_out, nv_w_out),
     (g_norm_gain, d_ng, nm_ng, nv_ng), (g_final, d_fg, nm_fg, nv_fg), (g_hgrn, d_hg, nm_hg, nv_hg),
     (g_convn, d_cg, nm_cg, nv_cg), (g_lb, d_lb, nm_lb, nv_lb), (g_conv_w, d_cw, nm_cw, nv_cw),
     grad_x) = _adamw_all(
        tot, rg_w_in, rg_w_out,
        [(w_in[0], m_w_in[0], v_w_in[0]), (w_out[0], m_w_out[0], v_w_out[0])],
        [(norm_gain, m_norm_gain, v_norm_gain),
         (row(final_norm_gain), row(m_final_norm_gain), row(v_final_norm_gain)),
         (hgrn_norm_gain, m_hgrn_norm_gain, v_hgrn_norm_gain),
         (conv_norm_gain, m_conv_norm_gain, v_conv_norm_gain),
         (lb_logits, m_lb_logits, v_lb_logits),
         (taps(conv_w), taps(m_conv_w), taps(v_conv_w))],
        rgrad_x)
    flat = lambda a: a.reshape(D_MODEL)
    untap = lambda a: a.reshape(1, 3, HEAD)
    return (loss.reshape(()), grad_x[None],
            g_norm_gain, g_w_in[None], g_lb, untap(g_conv_w), g_hgrn, g_convn, g_w_out[None], flat(g_final),
            d_ng, d_w_in[None], d_lb, untap(d_cw), d_hg, d_cg, d_w_out[None], flat(d_fg),
            nm_ng, nm_w_in[None], nm_lb, untap(nm_cw), nm_hg, nm_cg, nm_w_out[None], flat(nm_fg),
            nv_ng, nv_w_in[None], nv_lb, untap(nv_cw), nv_hg, nv_cg, nv_w_out[None], flat(nv_fg))
```

```python
import jax
import jax.numpy as jnp
import numpy as np
from jax import lax
from jax.experimental import pallas as pl
from jax.experimental.pallas import tpu as pltpu

F32 = jnp.float32
BF16 = jnp.bfloat16
MESH = pl.DeviceIdType.MESH

SEQ = 2048
D_MODEL = 1024
D_HGRN = 512
D_CONV = 512
HEAD = 128
N_HEADS = 4
CHUNK = 64
CONV_GROUP = 64
N_SHARD = 4
SHARD_COLS = 1024
WO_ROWS = 256
EPS = 1e-6
TB = 256
NCB = TB // CHUNK
N_CHUNKS = SEQ // CHUNK
N_DEV = 8
COLLECTIVE_GATHER, COLLECTIVE_MIX_OUT, COLLECTIVE_TAIL = 1, 0, 2
AUX_O, AUX_CV, AUX_B, AUX_COLS = 0, 512, 1024, 1536

ADAM_LR = 0.001
ADAM_B1 = 0.9
ADAM_B2 = 0.999
ADAM_EPS = 1e-08
ADAM_WD = 0.01
ADAM_STEP = 10

VMEM_LIMIT = 56 * 1024 * 1024


def _dot(a, b):
    return jnp.dot(a, b, preferred_element_type=F32)


def _dot_nt(a, b):
    return lax.dot_general(a, b, (((1,), (1,)), ((), ())), preferred_element_type=F32)


def _dot_tn(a, b):
    return lax.dot_general(a, b, (((0,), (0,)), ((), ())), preferred_element_type=F32)


def _split_bf16(x, n):
    parts = []
    r = x
    for _ in range(n):
        p = r.astype(BF16)
        parts.append(p)
        r = r - p.astype(F32)
    return parts


def _exact_left(m, x, n=3):
    acc = None
    for p in _split_bf16(x, n):
        t = _dot(m, p)
        acc = t if acc is None else acc + t
    return acc


def _exact_left_many(m, xs, n=3):
    parts = [_split_bf16(x, n) for x in xs]
    accs = [None] * len(xs)
    for i in range(n):
        for j in range(len(xs)):
            t = _dot(m, parts[j][i])
            accs[j] = t if accs[j] is None else accs[j] + t
    return accs


def _group_mean_many(xs, gmat, n=2):
    parts = [_split_bf16(x, n) for x in xs]
    accs = [None] * len(xs)
    for i in range(n):
        for j in range(len(xs)):
            t = _dot(parts[j][i], gmat)
            accs[j] = t if accs[j] is None else accs[j] + t
    return accs


def _group_mean(x, gmat, n=2):
    w = gmat.shape[0]
    outs = []
    for c0 in range(0, x.shape[1], w):
        acc = None
        for p in _split_bf16(x[:, c0:c0 + w], n):
            t = _dot(p, gmat)
            acc = t if acc is None else acc + t
        outs.append(acc)
    return jnp.concatenate(outs, axis=1)


def _sigmoid(x):
    return 1.0 / (1.0 + jnp.exp(-x))


def _lower_bound(lbl):
    l0 = lbl[0:1, :]
    l1 = lbl[1:2, :]
    m = jnp.maximum(l0, l1)
    e0 = jnp.exp(l0 - m)
    e1 = jnp.exp(l1 - m)
    return e0 / (e0 + e1)


def _tri(lower):
    r = lax.broadcasted_iota(jnp.int32, (CHUNK, CHUNK), 0)
    c = lax.broadcasted_iota(jnp.int32, (CHUNK, CHUNK), 1)
    return jnp.where((c <= r) if lower else (c >= r), 1.0, 0.0).astype(BF16)


def _causal():
    r = lax.broadcasted_iota(jnp.int32, (CHUNK, CHUNK), 0)
    c = lax.broadcasted_iota(jnp.int32, (CHUNK, CHUNK), 1)
    return c <= r


def _shift_down(x, sh, prev_tail):
    r = pltpu.roll(x, sh, 0)
    pt = pltpu.roll(prev_tail, sh, 0)
    rows = lax.broadcasted_iota(jnp.int32, prev_tail.shape, 0)
    top = jnp.where(rows < sh, pt, r[0:8])
    return jnp.concatenate([top, r[8:]], axis=0)


def _shift_up(x, sh, next_head):
    n = x.shape[0]
    r = pltpu.roll(x, n - sh, 0)
    nh = pltpu.roll(next_head, 8 - sh, 0)
    rows = lax.broadcasted_iota(jnp.int32, next_head.shape, 0)
    bot = jnp.where(rows >= 8 - sh, nh, r[n - 8:])
    return jnp.concatenate([r[:n - 8], bot], axis=0)


def _group_matrix(width, group):
    r = np.arange(width)[:, None] // group
    c = np.arange(width)[None, :] // group
    return jnp.asarray(np.where(r == c, 1.0 / group, 0.0), dtype=BF16)


TG = 1024
SEM_W, SEM_CW, SEM_W_FWD, N_SEM = 0, 4, 7, 11


def _gather_proj(kidx, x2d, g1, w_in, conv_w):
    half_w = D_MODEL // 2
    half_c = SHARD_COLS // 2
    nt = SEQ // TG
    n_steps = 2 * N_SHARD

    def body(k_ref, x_ref, g_ref, w_ref, cw_ref, h_ref, p_ref, wg_out, cwg_out,
             wg_v, cwg_v, send_sems, recv_sems, out_sems):
        s, t = pl.program_id(0), pl.program_id(1)
        x, y, c = lax.axis_index("x"), lax.axis_index("y"), lax.axis_index("c")
        k = 2 * x + y
        sibling = (x, y, 1 - c)
        chips = [(1 - x, y), (x, 1 - y), (1 - x, 1 - y)]
        kjs = [2 * cx + cy for cx, cy in chips]
        diag = (*chips[2], c)

        def w_half(kk, cc):
            return wg_v.at[kk, pl.ds(cc * half_w, half_w), :]

        def w_quarter(kk, cc, piece):
            return wg_v.at[kk, pl.ds(cc * half_w, half_w), piece * half_c:(piece + 1) * half_c]

        def cw_of(kk):
            return cwg_v.at[:, pl.ds(pl.multiple_of(kk * HEAD, HEAD), HEAD)]

        def copy(sem, ref, to):
            return pltpu.make_async_remote_copy(
                src_ref=ref, dst_ref=ref, send_sem=send_sems.at[sem], recv_sem=recv_sems.at[sem],
                device_id=to, device_id_type=MESH)

        def at_step(sv, tv):
            return pl.when((s == sv) & (t == tv))

        w_direct = ([copy(SEM_W + j, w_half(k, c), (*chips[j], c)) for j in range(2)]
                    + [copy(SEM_W + 2 + p, w_quarter(k, c, p), diag) for p in range(2)])
        cw_direct = [copy(SEM_CW + j, cw_of(k), (*chip, c)) for j, chip in enumerate(chips)]
        w_passed = ([copy(SEM_W_FWD + j, w_half(kjs[j], c), sibling) for j in range(2)]
                    + [copy(SEM_W_FWD + 2 + p, w_quarter(kjs[2], c, p), sibling) for p in range(2)])
        stores = ([pltpu.make_async_copy(wg_v.at[kk], wg_out.at[kk], out_sems.at[i])
                   for i, kk in enumerate([k] + kjs)]
                  + [pltpu.make_async_copy(cwg_v, cwg_out, out_sems.at[4])])

        @at_step(0, 0)
        def _():
            barrier = pltpu.get_barrier_semaphore()
            for peer in [sibling] + [(*chip, c) for chip in chips]:
                pl.semaphore_signal(barrier, inc=1, device_id=peer, device_id_type=MESH)
            wg_v[k] = w_ref[0].astype(BF16)
            mine = pl.ds(pl.multiple_of(k * HEAD, HEAD), HEAD)
            cwg_v[:, mine] = jnp.zeros((8, HEAD), F32)
            for tap in range(3):
                cwg_v[tap:tap + 1, mine] = cw_ref[:, tap * HEAD:(tap + 1) * HEAD]
            pl.semaphore_wait(barrier, 4)
            for cp in w_direct + cw_direct:
                cp.start()
            stores[0].start()

        @at_step(2, 0)
        def _():
            for j in range(2):
                copy(SEM_W + j, w_half(kjs[j], c), sibling).wait_recv()
                w_passed[j].start()
            copy(SEM_W_FWD, w_half(kjs[0], 1 - c), sibling).wait_recv()
            stores[1].start()

        @at_step(4, 0)
        def _():
            copy(SEM_W_FWD + 1, w_half(kjs[1], 1 - c), sibling).wait_recv()
            stores[2].start()

        for p in range(2):
            @at_step(6 + p, 0)
            def _(p=p):
                copy(SEM_W + 2 + p, w_quarter(kjs[2], c, p), sibling).wait_recv()
                w_passed[2 + p].start()
                copy(SEM_W_FWD + 2 + p, w_quarter(kjs[2], 1 - c, p), sibling).wait_recv()

        rows = pl.ds(pl.multiple_of(t * TG, TG), TG)

        @pl.when(s == 0)
        def _():
            xv = x_ref[...]
            r = lax.rsqrt(jnp.mean(xv * xv, axis=-1, keepdims=True) + EPS)
            h_ref[rows, :] = (xv * r * g_ref[...]).astype(BF16)

        sh = s >> 1
        js = k ^ (((sh & 1) << 1) | (sh >> 1))
        for piece in range(2):
            @pl.when((s & 1) == piece)
            def _(piece=piece):
                p_ref[...] = _dot(h_ref[rows, :], wg_v[js, :, piece * half_c:(piece + 1) * half_c])

        @at_step(n_steps - 1, nt - 1)
        def _():
            stores[3].start()
            for j in range(3):
                copy(SEM_CW + j, cw_of(kjs[j]), sibling).wait_recv()
            stores[4].start()
            for cp in w_direct + cw_direct + w_passed:
                cp.wait_send()
            for st in stores:
                st.wait()

    def x_map(s, t, kr):
        return (jnp.where(s == 0, t, nt - 1), 0)

    def p_map(s, t, kr):
        sh = s >> 1
        return (t, 2 * (kr[0] ^ (((sh & 1) << 1) | (sh >> 1))) + (s & 1))

    hbm = pl.BlockSpec(memory_space=pl.ANY)
    grid_spec = pltpu.PrefetchScalarGridSpec(
        num_scalar_prefetch=1, grid=(n_steps, nt),
        in_specs=[pl.BlockSpec((TG, D_MODEL), x_map),
                  pl.BlockSpec((1, D_MODEL), lambda s, t, kr: (0, 0)),
                  pl.BlockSpec((1, D_MODEL, SHARD_COLS), lambda s, t, kr: (0, 0, 0)),
                  pl.BlockSpec((1, 3 * HEAD), lambda s, t, kr: (0, 0))],
        out_specs=(pl.BlockSpec((SEQ, D_MODEL), lambda s, t, kr: (0, 0)),
                   pl.BlockSpec((TG, half_c), p_map), hbm, hbm),
        scratch_shapes=[pltpu.VMEM((N_SHARD, D_MODEL, SHARD_COLS), BF16),
                        pltpu.VMEM((8, D_CONV), F32),
                        pltpu.SemaphoreType.DMA((N_SEM,)), pltpu.SemaphoreType.DMA((N_SEM,)),
                        pltpu.SemaphoreType.DMA((5,))])
    return pl.pallas_call(
        body, name="gather_proj", grid_spec=grid_spec,
        out_shape=(jax.ShapeDtypeStruct((SEQ, D_MODEL), BF16),
                   jax.ShapeDtypeStruct((SEQ, N_SHARD * SHARD_COLS), F32),
                   jax.ShapeDtypeStruct((N_SHARD, D_MODEL, SHARD_COLS), BF16),
                   jax.ShapeDtypeStruct((8, D_CONV), F32)),
        compiler_params=pltpu.CompilerParams(dimension_semantics=("arbitrary", "arbitrary"),
                                             vmem_limit_bytes=VMEM_LIMIT, collective_id=COLLECTIVE_GATHER),
    )(kidx, x2d, g1, w_in, conv_w)


LAG = 6


def _mix_out(proj, lb_logits, cw, ga, gcn, g64, w_out, x2d, gf, tgt):
    half_o = WO_ROWS // 2
    nblk = SEQ // TB
    n_steps = nblk + LAG

    def body(p_ref, lbl_ref, cw_ref, ga_ref, gcn_ref, g64_ref, wo_ref, x_ref, gf_ref, t_ref,
             aux_ref, sto_ref, dx2_ref, dm_ref, gwo_ref, part_ref,
             st_ref, tail_ref, wog_v, stage, ring, acc_ref, send_sems, recv_sems):
        i = pl.program_id(0)
        x, y, c = lax.axis_index("x"), lax.axis_index("y"), lax.axis_index("c")
        k = 2 * x + y
        sibling = (x, y, 1 - c)
        chips = [(1 - x, y), (x, 1 - y), (1 - x, 1 - y)]
        kjs = [2 * cx + cy for cx, cy in chips]

        def wo_half(kk, cc):
            return wog_v.at[pl.ds(pl.multiple_of(kk * WO_ROWS + cc * half_o, half_o), half_o), :]

        def copy(sem, ref, to):
            return pltpu.make_async_remote_copy(
                src_ref=ref, dst_ref=ref, send_sem=send_sems.at[sem], recv_sem=recv_sems.at[sem],
                device_id=to, device_id_type=MESH)

        wo_direct = [copy(j, wo_half(k, c), (*chip, c)) for j, chip in enumerate(chips)]
        wo_passed = [copy(3 + j, wo_half(kj, c), sibling) for j, kj in enumerate(kjs)]

        @pl.when(i == 0)
        def _():
            barrier = pltpu.get_barrier_semaphore()
            for peer in [sibling] + [(*chip, c) for chip in chips]:
                pl.semaphore_signal(barrier, inc=1, device_id=peer, device_id_type=MESH)
            st_ref[...] = jnp.zeros_like(st_ref)
            tail_ref[...] = jnp.zeros_like(tail_ref)
            acc_ref[...] = jnp.zeros_like(acc_ref)
            part_ref[...] = jnp.zeros_like(part_ref)
            wog_v[pl.ds(pl.multiple_of(k * WO_ROWS, WO_ROWS), WO_ROWS), :] = wo_ref[0].astype(BF16)
            pl.semaphore_wait(barrier, 4)
            for cp in wo_direct:
                cp.start()

        @pl.when(i == LAG - 1)
        def _():
            for j in range(3):
                copy(j, wo_half(kjs[j], c), sibling).wait_recv()
                wo_passed[j].start()

        @pl.when(i == LAG)
        def _():
            for j in range(3):
                copy(3 + j, wo_half(kjs[j], 1 - c), sibling).wait_recv()

        lb = _lower_bound(lbl_ref[...])
        tri = _tri(True)
        causal = _causal()
        g64m = g64_ref[...]
        heads = range(N_HEADS)
        cs = [slice(hd * HEAD, (hd + 1) * HEAD) for hd in heads]
        col = lambda base, hd: slice(base + hd * HEAD, base + (hd + 1) * HEAD)

        def mix_chunk(n):
            sl = pl.ds(n * CHUNK, CHUNK)
            sg = [_sigmoid(p_ref[sl, col(512, hd)]) for hd in heads]
            f = [lb[:, cs[hd]] + (1.0 - lb[:, cs[hd]]) * sg[hd] for hd in heads]
            bc = _exact_left_many(tri, [jnp.log(f[hd]) for hd in heads])
            for hd in heads:
                aux_ref[sl, col(AUX_B, hd)] = bc[hd]
            g = [bc[hd][CHUNK - 1:CHUNK, :] for hd in heads]
            qd = [(p_ref[sl, col(0, hd)] * jnp.exp(bc[hd])).astype(BF16) for hd in heads]
            kk = [1.0 - f[hd] for hd in heads]
            ki = [(kk[hd] * jnp.exp(-bc[hd])).astype(BF16) for hd in heads]
            ke = [(kk[hd] * jnp.exp(g[hd] - bc[hd])).astype(BF16) for hd in heads]
            vb = [p_ref[sl, col(1024, hd)].astype(BF16) for hd in heads]
            st = [st_ref[hd] for hd in heads]
            st_b = [a.astype(BF16) for a in st]
            for hd in heads:
                sto_ref[n, hd] = st_b[hd]
            scm = [_dot_nt(qd[hd], ki[hd]) for hd in heads]
            inter = [_dot_nt(qd[hd], st_b[hd]) for hd in heads]
            upd = [_dot_tn(vb[hd], ke[hd]) for hd in heads]
            intra = [_dot(jnp.where(causal, scm[hd], 0.0).astype(BF16), vb[hd]) for hd in heads]
            for hd in heads:
                st_ref[hd] = st[hd] * jnp.exp(g[hd]) + upd[hd]
                o = intra[hd] + inter[hd]
                aux_ref[sl, col(AUX_O, hd)] = o
                ra = lax.rsqrt(jnp.mean(o * o, axis=-1, keepdims=True) + EPS)
                za = p_ref[sl, col(1536, hd)]
                stage[sl, cs[hd]] = (o * ra * ga_ref[:, cs[hd]] * (za * _sigmoid(za))).astype(BF16)
            yb = []
            for hd in heads:
                cu = p_ref[sl, col(3072, hd)] * p_ref[sl, col(2048, hd)]
                tail = tail_ref[:, cs[hd]]
                cv = (cw_ref[0:1, cs[hd]] * _shift_down(cu, 2, tail) + cw_ref[1:2, cs[hd]] * _shift_down(cu, 1, tail)
                      + cw_ref[2:3, cs[hd]] * cu)
                tail_ref[:, cs[hd]] = cu[CHUNK - 8:, :]
                aux_ref[sl, col(AUX_CV, hd)] = cv
                yb.append(p_ref[sl, col(2560, hd)] * cv)
            ms = _group_mean_many([y * y for y in yb], g64m)
            for hd in heads:
                rb = lax.rsqrt(ms[hd] + EPS)
                zb = p_ref[sl, col(3584, hd)]
                stage[sl, col(512, hd)] = (yb[hd] * rb * gcn_ref[:, cs[hd]] * (zb * _sigmoid(zb))).astype(BF16)

        def step(mix, project):
            if project:
                mixed_b = ring[pl.ds(pl.multiple_of((i - LAG) * TB, TB), TB), :]
                y = _dot(mixed_b, wog_v[...])
            if mix:
                mix_chunk(0)
            if project:
                x2 = x_ref[...] + y
                r2 = lax.rsqrt(jnp.mean(x2 * x2, axis=-1, keepdims=True) + EPS)
                n2 = x2 * r2
                gfv = gf_ref[...]
                err = n2 * gfv - t_ref[...]
                loss = 0.5 * jnp.sum(jnp.mean(err * err, axis=-1, keepdims=True), axis=0, keepdims=True)
                dy = err * (1.0 / D_MODEL)
                part_ref[1:2, :] += jnp.sum(dy * n2, axis=0, keepdims=True)
                part_ref[7:8, :] += jnp.broadcast_to(loss, (1, D_MODEL))
                dn = dy * gfv
                dx2 = r2 * (dn - n2 * jnp.mean(dn * n2, axis=-1, keepdims=True))
                dx2_ref[...] = dx2
                dx2_b = dx2.astype(BF16)
            if mix:
                mix_chunk(1)
            if project:
                dm_ref[...] = _dot_nt(dx2_b, wog_v[...])
            if mix:
                mix_chunk(2)
            if project:
                acc_ref[...] += _dot_tn(mixed_b, dx2_b)
            if mix:
                mix_chunk(3)
                ring[pl.ds(pl.multiple_of(i * TB, TB), TB), :] = stage[...]

        @pl.when(i < LAG)
        def _():
            step(True, False)

        @pl.when((i >= LAG) & (i < nblk))
        def _():
            step(True, True)

        @pl.when(i >= nblk)
        def _():
            step(False, True)

        @pl.when(i == n_steps - 1)
        def _():
            gwo_ref[...] = acc_ref[...].astype(BF16)
            for cp in wo_direct + wo_passed:
                cp.wait_send()

    assert NCB == 4
    row = lambda w: pl.BlockSpec((1, w), lambda i: (0, 0))
    mix_blk = lambda i: jnp.minimum(i, nblk - 1)
    out_blk = lambda i: jnp.clip(i - LAG, 0, nblk - 1)
    tok = lambda: pl.BlockSpec((TB, D_MODEL), lambda i: (out_blk(i), 0))
    return pl.pallas_call(
        body, name="mix_out", grid=(n_steps,),
        out_shape=(jax.ShapeDtypeStruct((SEQ, AUX_COLS), F32),
                   jax.ShapeDtypeStruct((N_CHUNKS, N_HEADS, HEAD, HEAD), BF16),
                   jax.ShapeDtypeStruct((SEQ, D_MODEL), F32),
                   jax.ShapeDtypeStruct((SEQ, D_MODEL), F32),
                   jax.ShapeDtypeStruct((D_MODEL, D_MODEL), BF16),
                   jax.ShapeDtypeStruct((8, D_MODEL), F32)),
        in_specs=[pl.BlockSpec((TB, 4096), lambda i: (jnp.minimum(i, nblk - 1), 0)),
                  pl.BlockSpec((2, D_HGRN), lambda i: (0, 0)),
                  pl.BlockSpec((8, D_CONV), lambda i: (0, 0)),
                  row(D_HGRN), row(D_CONV),
                  pl.BlockSpec((HEAD, HEAD), lambda i: (0, 0)),
                  pl.BlockSpec((1, WO_ROWS, D_MODEL), lambda i: (0, 0, 0)),
                  tok(), row(D_MODEL), tok()],
        out_specs=(pl.BlockSpec((TB, AUX_COLS), lambda i: (mix_blk(i), 0)),
                   pl.BlockSpec((NCB, N_HEADS, HEAD, HEAD), lambda i: (mix_blk(i), 0, 0, 0)),
                   tok(), tok(),
                   pl.BlockSpec((D_MODEL, D_MODEL), lambda i: (0, 0)),
                   pl.BlockSpec((8, D_MODEL), lambda i: (0, 0))),
        scratch_shapes=[pltpu.VMEM((N_HEADS, HEAD, HEAD), F32), pltpu.VMEM((8, D_CONV), F32),
                        pltpu.VMEM((D_MODEL, D_MODEL), BF16), pltpu.VMEM((TB, D_MODEL), BF16),
                        pltpu.VMEM((SEQ, D_MODEL), BF16), pltpu.VMEM((D_MODEL, D_MODEL), F32),
                        pltpu.SemaphoreType.DMA((6,)), pltpu.SemaphoreType.DMA((6,))],
        compiler_params=pltpu.CompilerParams(dimension_semantics=("arbitrary",), vmem_limit_bytes=VMEM_LIMIT,
                                             collective_id=COLLECTIVE_MIX_OUT),
    )(proj, lb_logits, cw, ga, gcn, g64, w_out, x2d, gf, tgt)


def _mix_bwd(proj, aux, states, dmixed, lb_logits, cw, ga, gcn, g64):
    nblk = SEQ // TB

    def body(p_ref, aux_ref, st_ref, dm_ref, lbl_ref, cw_ref, ga_ref, gcn_ref, g64_ref,
             dp_ref, part_ref, dst_ref, head_ref, dlb_ref):
        i = pl.program_id(0)

        @pl.when(i == 0)
        def _():
            dst_ref[...] = jnp.zeros_like(dst_ref)
            head_ref[...] = jnp.zeros_like(head_ref)
            part_ref[...] = jnp.zeros_like(part_ref)
            dlb_ref[...] = jnp.zeros_like(dlb_ref)

        lb = _lower_bound(lbl_ref[...])
        triu = _tri(False)
        causal = _causal()
        g64m = g64_ref[...]
        rowsum = lambda a: jnp.sum(a, axis=0, keepdims=True)
        heads = range(N_HEADS)
        cs = [slice(hd * HEAD, (hd + 1) * HEAD) for hd in heads]
        col = lambda base, hd: slice(base + hd * HEAD, base + (hd + 1) * HEAD)
        for n in reversed(range(NCB)):
            sl = pl.ds(n * CHUNK, CHUNK)
            cvv = [aux_ref[sl, col(AUX_CV, hd)] for hd in heads]
            gb = [p_ref[sl, col(2560, hd)] for hd in heads]
            yb = [gb[hd] * cvv[hd] for hd in heads]
            ms = _group_mean_many([y * y for y in yb], g64m)
            rb, nb, dnb = [], [], []
            for hd in heads:
                rb.append(lax.rsqrt(ms[hd] + EPS))
                nb.append(yb[hd] * rb[hd])
                zb = p_ref[sl, col(3584, hd)]
                sgb = _sigmoid(zb)
                dmb = dm_ref[sl, col(512, hd)]
                silu = zb * sgb
                dgate = dmb * gcn_ref[:, cs[hd]]
                part_ref[2:3, col(512, hd)] += rowsum(dmb * nb[hd] * silu)
                dp_ref[sl, col(3584, hd)] = (dgate * nb[hd] * (sgb + silu * (1.0 - sgb))).astype(BF16)
                dnb.append(dgate * silu)
            mdn = _group_mean_many([dnb[hd] * nb[hd] for hd in heads], g64m)
            for hd in heads:
                dyb = rb[hd] * (dnb[hd] - nb[hd] * mdn[hd])
                dp_ref[sl, col(2560, hd)] = (dyb * cvv[hd]).astype(BF16)
                dcv = dyb * gb[hd]
                head = head_ref[:, cs[hd]]
                dcv1 = _shift_up(dcv, 1, head)
                dcv2 = _shift_up(dcv, 2, head)
                head_ref[:, cs[hd]] = dcv[0:8, :]
                u = p_ref[sl, col(2048, hd)]
                gc = p_ref[sl, col(3072, hd)]
                cu = gc * u
                part_ref[4:5, cs[hd]] += rowsum(dcv2 * cu)
                part_ref[5:6, cs[hd]] += rowsum(dcv1 * cu)
                part_ref[6:7, cs[hd]] += rowsum(dcv * cu)
                dcu = cw_ref[2:3, cs[hd]] * dcv + cw_ref[1:2, cs[hd]] * dcv1 + cw_ref[0:1, cs[hd]] * dcv2
                dp_ref[sl, col(3072, hd)] = (dcu * u).astype(BF16)
                dp_ref[sl, col(2048, hd)] = (dcu * gc).astype(BF16)
            do_b = []
            for hd in heads:
                ov = aux_ref[sl, col(AUX_O, hd)]
                ra = lax.rsqrt(jnp.mean(ov * ov, axis=-1, keepdims=True) + EPS)
                na = ov * ra
                za = p_ref[sl, col(1536, hd)]
                sga = _sigmoid(za)
                dma = dm_ref[sl, cs[hd]]
                silu = za * sga
                dgate = dma * ga_ref[:, cs[hd]]
                part_ref[2:3, cs[hd]] += rowsum(dma * na * silu)
                dp_ref[sl, col(1536, hd)] = (dgate * na * (sga + silu * (1.0 - sga))).astype(BF16)
                dna = dgate * silu
                do_b.append((ra * (dna - na * jnp.mean(dna * na, axis=-1, keepdims=True))).astype(BF16))
            s = [_sigmoid(p_ref[sl, col(512, hd)]) for hd in heads]
            f = [lb[:, cs[hd]] + (1.0 - lb[:, cs[hd]]) * s[hd] for hd in heads]
            bc = [aux_ref[sl, col(AUX_B, hd)] for hd in heads]
            g = [bc[hd][CHUNK - 1:CHUNK, :] for hd in heads]
            eb = [jnp.exp(bc[hd]) for hd in heads]
            enb = [jnp.exp(-bc[hd]) for hd in heads]
            eg = [jnp.exp(g[hd] - bc[hd]) for hd in heads]
            dec = [jnp.exp(g[hd]) for hd in heads]
            qd = [p_ref[sl, cs[hd]] * eb[hd] for hd in heads]
            kk = [1.0 - f[hd] for hd in heads]
            ki = [kk[hd] * enb[hd] for hd in heads]
            ke = [kk[hd] * eg[hd] for hd in heads]
            qd_b = [a.astype(BF16) for a in qd]
            ki_b = [a.astype(BF16) for a in ki]
            ke_b = [a.astype(BF16) for a in ke]
            vb = [p_ref[sl, col(1024, hd)].astype(BF16) for hd in heads]
            st_b = [st_ref[n, hd] for hd in heads]
            dst = [dst_ref[hd] for hd in heads]
            dst_b = [a.astype(BF16) for a in dst]
            scm = [_dot_nt(qd_b[hd], ki_b[hd]) for hd in heads]
            amm = [_dot_nt(do_b[hd], vb[hd]) for hd in heads]
            dqd2 = [_dot(do_b[hd], st_b[hd]) for hd in heads]
            dke = [_dot(vb[hd], dst_b[hd]) for hd in heads]
            dv2 = [_dot_nt(ke_b[hd], dst_b[hd]) for hd in heads]
            dsu = [_dot_tn(do_b[hd], qd_b[hd]) for hd in heads]
            sc = [jnp.where(causal, scm[hd], 0.0).astype(BF16) for hd in heads]
            am = [jnp.where(causal, amm[hd], 0.0).astype(BF16) for hd in heads]
            dqd1 = [_dot(am[hd], ki_b[hd]) for hd in heads]
            dki = [_dot_tn(am[hd], qd_b[hd]) for hd in heads]
            dv1 = [_dot_tn(sc[hd], do_b[hd]) for hd in heads]
            db, dgv, dkk = [], [], []
            for hd in heads:
                dqd = dqd1[hd] + dqd2[hd]
                ddec = rowsum(dst[hd] * st_b[hd].astype(F32))
                dst_ref[hd] = dst[hd] * dec[hd] + dsu[hd]
                dp_ref[sl, cs[hd]] = (dqd * eb[hd]).astype(BF16)
                dp_ref[sl, col(1024, hd)] = (dv1[hd] + dv2[hd]).astype(BF16)
                dke_eg = dke[hd] * eg[hd]
                dkk.append(dki[hd] * enb[hd] + dke_eg)
                db.append(dqd * qd[hd] - kk[hd] * dkk[hd])
                dgv.append(rowsum(kk[hd] * dke_eg) + ddec * dec[hd])
            rc = _exact_left_many(triu, db, 2)
            for hd in heads:
                df = (rc[hd] + dgv[hd]) / f[hd] - dkk[hd]
                one_s = 1.0 - s[hd]
                dlb_ref[:, cs[hd]] += rowsum(df * one_s)
                dp_ref[sl, col(512, hd)] = (df * (1.0 - lb[:, cs[hd]]) * s[hd] * one_s).astype(BF16)

        @pl.when(i == nblk - 1)
        def _():
            row = dlb_ref[...] * lb * (1.0 - lb)
            part_ref[3:4, 0:D_HGRN] = row
            part_ref[3:4, D_HGRN:] = -row

    rev = lambda w: pl.BlockSpec((TB, w), lambda i: (nblk - 1 - i, 0))
    row = lambda w: pl.BlockSpec((1, w), lambda i: (0, 0))
    return pl.pallas_call(
        body, name="mix_bwd", grid=(nblk,),
        out_shape=(jax.ShapeDtypeStruct((SEQ, 4096), BF16),
                   jax.ShapeDtypeStruct((8, D_MODEL), F32)),
        in_specs=[rev(4096), rev(AUX_COLS),
                  pl.BlockSpec((NCB, N_HEADS, HEAD, HEAD), lambda i: (nblk - 1 - i, 0, 0, 0)),
                  rev(D_MODEL),
                  pl.BlockSpec((2, D_HGRN), lambda i: (0, 0)),
                  pl.BlockSpec((8, D_CONV), lambda i: (0, 0)),
                  row(D_HGRN), row(D_CONV),
                  pl.BlockSpec((HEAD, HEAD), lambda i: (0, 0))],
        out_specs=(rev(4096), pl.BlockSpec((8, D_MODEL), lambda i: (0, 0))),
        scratch_shapes=[pltpu.VMEM((N_HEADS, HEAD, HEAD), F32), pltpu.VMEM((8, D_CONV), F32),
                        pltpu.VMEM((1, D_HGRN), F32)],
        compiler_params=pltpu.CompilerParams(dimension_semantics=("arbitrary",), vmem_limit_bytes=VMEM_LIMIT),
    )(proj, aux, states, dmixed, lb_logits, cw, ga, gcn, g64)


TT = 1024
TX = 256
(SEM_D2D, SEM_D2D_O, SEM_ICI, SEM_ICI_O, SEM_FIN, SEM_FIN_O, SEM_SMALL, SEM_VIA, N_SEM_TAIL) = (
    0, 4, 5, 8, 11, 12, 12, 20, 22)


def _bwd_tail(kidx, h, dproj, wg, gwo, x2d, dx2, g1, small_a, small_b):
    hw = D_MODEL // 2
    ho = WO_ROWS // 2
    nt = SEQ // TT
    norm_step = 2 * N_SHARD
    n_steps = norm_step + SEQ // TX // nt

    def body(k_ref, h_ref, dp_ref, w_ref, gwo_ref, x_ref, dx2_ref, g_ref, sm_ref, smb_ref,
             gx_ref, gw_out, gwo_out, osm_ref,
             acc, dh, sendbuf, keep, sibrcv, rcv, merge, sib_o, p_o, rcv_o, res_o, sm_buf, dng,
             send_sems, recv_sems, out_sems):
        s, t = pl.program_id(0), pl.program_id(1)
        x, y, c = lax.axis_index("x"), lax.axis_index("y"), lax.axis_index("c")
        k = 2 * x + y
        me = 4 * x + 2 * y + c
        sibling = (x, y, 1 - c)
        chips = [(1 - x, 1 - y), (1 - x, y), (x, 1 - y)]
        kjs = [2 * cx + cy for cx, cy in chips]
        mine = pl.ds(pl.multiple_of(c * hw, hw), hw)
        other = pl.ds(pl.multiple_of((1 - c) * hw, hw), hw)
        mine_o = pl.ds(pl.multiple_of(c * ho, ho), ho)
        other_o = pl.ds(pl.multiple_of((1 - c) * ho, ho), ho)

        def copy(sem, src, dst, to):
            return pltpu.make_async_remote_copy(
                src_ref=src, dst_ref=dst, send_sem=send_sems.at[sem], recv_sem=recv_sems.at[sem],
                device_id=to, device_id_type=MESH)

        def at_step(sv, tv):
            return pl.when((s == sv) & (t == tv))

        def at_norm_block(b):
            return at_step(norm_step + b // nt, b % nt)

        d2d = [copy(SEM_D2D + sv, sendbuf.at[sv], sibrcv.at[sv], sibling) for sv in range(N_SHARD)]
        d2d_o = copy(SEM_D2D_O, gwo_ref.at[:, other_o, :], sib_o, sibling)
        ici = [copy(SEM_ICI + sv, keep.at[sv], rcv.at[sv], (*chips[sv], c)) for sv in range(3)]
        qh = hw // 2
        via = [copy(SEM_VIA, keep.at[0, 0:qh, :], merge.at[1], (*chips[1], c)),
               copy(SEM_VIA + 1, keep.at[0, qh:hw, :], merge.at[0], (*chips[2], c))]
        merged_rows = [slice(qh, hw), slice(0, qh)]
        ici_o = [copy(SEM_ICI_O + sv, p_o.at[kjs[sv]], rcv_o.at[sv], (*chips[sv], c)) for sv in range(3)]
        fin = copy(SEM_FIN, acc.at[mine, :], gw_out.at[mine, :], sibling)
        fin_o = copy(SEM_FIN_O, res_o.at[mine_o, :], res_o.at[mine_o, :], sibling)
        smalls = [copy(SEM_SMALL + m, sm_buf.at[me], sm_buf.at[me],
                       (x ^ (m >> 2), y ^ ((m >> 1) & 1), c ^ (m & 1))) for m in range(1, N_DEV)]
        store_w = pltpu.make_async_copy(acc.at[mine, :], gw_out.at[mine, :], out_sems.at[0])
        store_o = pltpu.make_async_copy(res_o, gwo_out, out_sems.at[1])

        @at_step(0, 0)
        def _():
            barrier = pltpu.get_barrier_semaphore()
            for m in range(1, N_DEV):
                pl.semaphore_signal(barrier, inc=1, device_id=(x ^ (m >> 2), y ^ ((m >> 1) & 1), c ^ (m & 1)),
                                    device_id_type=MESH)
            pl.semaphore_wait(barrier, N_DEV - 1)
            d2d_o.start()

        @at_step(0, 1)
        def _():
            d2d_o.wait_recv()
            for j in range(N_SHARD):
                p_o[j] = (gwo_ref[j, mine_o, :].astype(F32) + sib_o[j].astype(F32)).astype(BF16)
            res_o[mine_o, :] = gwo_ref[k, mine_o, :].astype(F32) + sib_o[k].astype(F32)
            for cp in ici_o:
                cp.start()

        rows = pl.ds(pl.multiple_of(t * TT, TT), TT)

        @pl.when((s < N_SHARD) & (t == 0))
        def _():
            acc[...] = _dot_tn(h_ref[...], dp_ref[...])

        @pl.when((s < N_SHARD) & (t > 0))
        def _():
            acc[...] += _dot_tn(h_ref[...], dp_ref[...])

        for sv in range(N_SHARD):
            @at_step(sv, nt - 1)
            def _(sv=sv):
                sendbuf[sv] = acc[other, :].astype(BF16)
                if sv < 3:
                    keep[sv] = acc[mine, :].astype(BF16)
                d2d[sv].start()

        @at_step(1, 0)
        def _():
            d2d[0].wait_recv()
            keep[0] = (keep[0].astype(F32) + sibrcv[0].astype(F32)).astype(BF16)
            for cp in via:
                cp.start()

        for sv in (1, 2):
            @at_step(sv + 2, 0)
            def _(sv=sv):
                d2d[sv].wait_recv()
                keep[sv] = (keep[sv].astype(F32) + sibrcv[sv].astype(F32)).astype(BF16)
                via[2 - sv].wait_recv()
                rows_m = merged_rows[sv - 1]
                keep[sv, rows_m, :] = (keep[sv, rows_m, :].astype(F32) + merge[sv - 1].astype(F32)).astype(BF16)
                ici[sv].start()

        @pl.when(s == N_SHARD)
        def _():
            dh[rows, :] = _dot_nt(dp_ref[...], w_ref[0])

        @pl.when((s > N_SHARD) & (s < norm_step))
        def _():
            dh[rows, :] += _dot_nt(dp_ref[...], w_ref[0])

        @at_norm_block(0)
        def _():
            d2d[3].wait_recv()
            acc[mine, :] += sibrcv[3].astype(F32)

        @at_norm_block(1)
        def _():
            tot = res_o[mine_o, :]
            for sv in range(3):
                ici_o[sv].wait_recv()
                tot = tot + rcv_o[sv].astype(F32)
            res_o[mine_o, :] = tot
            fin_o.start()

        @at_norm_block(2)
        def _():
            ici[1].wait_recv()
            acc[mine, :] += rcv[1].astype(F32)

        @at_norm_block(SEQ // TX - 2)
        def _():
            ici[2].wait_recv()
            acc[mine, :] += rcv[2].astype(F32)
            fin.start()
            store_w.start()
            fin_o.wait_recv()
            store_o.start()

        @at_norm_block(0)
        def _():
            dng[...] = jnp.zeros_like(dng)

        @pl.when(s >= norm_step)
        def _():
            blk = (s - norm_step) * nt + t
            dhv = dh[pl.ds(pl.multiple_of(blk * TX, TX), TX), :]
            xv = x_ref[...]
            r = lax.rsqrt(jnp.mean(xv * xv, axis=-1, keepdims=True) + EPS)
            xn = xv * r
            dng[...] += jnp.sum(dhv * xn, axis=0, keepdims=True)
            dxn = dhv * g_ref[...]
            gx_ref[...] = dx2_ref[...] + r * (dxn - xn * jnp.mean(dxn * xn, axis=-1, keepdims=True))

        @at_step(n_steps - 1, nt - 1)
        def _():
            sm_buf[me] = sm_ref[...] + smb_ref[...]
            sm_buf[me, 0:1, :] = dng[...]
            for cp in smalls:
                cp.start()
            for m in range(1, N_DEV):
                copy(SEM_SMALL + m, sm_buf.at[0], sm_buf.at[0], sibling).wait_recv()
            tot = sm_buf[0]
            for d in range(1, N_DEV):
                tot = tot + sm_buf[d]
            osm_ref[...] = tot
            fin.wait_recv()
            for cp in d2d + [d2d_o] + via + ici[1:] + ici_o + [fin, fin_o] + smalls:
                cp.wait_send()
            store_o.wait()
            store_w.wait()

    def shard_of(s, kr):
        order = jnp.where(s < N_SHARD, s, jnp.where(s < norm_step, s - N_SHARD, 3))
        return kr[0] ^ (3 - order)

    def h_map(s, t, kr):
        return (jnp.where(s < N_SHARD, t, nt - 1), 0)

    def dp_map(s, t, kr):
        return (jnp.where(s < norm_step, t, nt - 1), shard_of(s, kr))

    def w_map(s, t, kr):
        return (shard_of(jnp.maximum(s, N_SHARD), kr), 0, 0)

    def blk_map(s, t, kr):
        return (jnp.where(s < norm_step, 0, (s - norm_step) * nt + t), 0)

    hbm = pl.BlockSpec(memory_space=pl.ANY)
    grid_spec = pltpu.PrefetchScalarGridSpec(
        num_scalar_prefetch=1, grid=(n_steps, nt),
        in_specs=[pl.BlockSpec((TT, D_MODEL), h_map),
                  pl.BlockSpec((TT, SHARD_COLS), dp_map),
                  pl.BlockSpec((1, D_MODEL, SHARD_COLS), w_map),
                  pl.BlockSpec((N_SHARD, WO_ROWS, D_MODEL), lambda s, t, kr: (0, 0, 0)),
                  pl.BlockSpec((TX, D_MODEL), blk_map),
                  pl.BlockSpec((TX, D_MODEL), blk_map),
                  pl.BlockSpec((1, D_MODEL), lambda s, t, kr: (0, 0)),
                  pl.BlockSpec((8, D_MODEL), lambda s, t, kr: (0, 0)),
                  pl.BlockSpec((8, D_MODEL), lambda s, t, kr: (0, 0))],
        out_specs=(pl.BlockSpec((TX, D_MODEL), blk_map), hbm, hbm,
                   pl.BlockSpec((8, D_MODEL), lambda s, t, kr: (0, 0))),
        scratch_shapes=[pltpu.VMEM((D_MODEL, SHARD_COLS), F32), pltpu.VMEM((SEQ, D_MODEL), F32),
                        pltpu.VMEM((N_SHARD, hw, SHARD_COLS), BF16), pltpu.VMEM((3, hw, SHARD_COLS), BF16),
                        pltpu.VMEM((N_SHARD, hw, SHARD_COLS), BF16), pltpu.VMEM((3, hw, SHARD_COLS), BF16),
                        pltpu.VMEM((2, hw // 2, SHARD_COLS), BF16),
                        pltpu.VMEM((N_SHARD, ho, D_MODEL), BF16), pltpu.VMEM((N_SHARD, ho, D_MODEL), BF16),
                        pltpu.VMEM((3, ho, D_MODEL), BF16), pltpu.VMEM((WO_ROWS, D_MODEL), F32),
                        pltpu.VMEM((N_DEV, 8, D_MODEL), F32), pltpu.VMEM((1, D_MODEL), F32),
                        pltpu.SemaphoreType.DMA((N_SEM_TAIL,)), pltpu.SemaphoreType.DMA((N_SEM_TAIL,)),
                        pltpu.SemaphoreType.DMA((2,))])
    return pl.pallas_call(
        body, name="bwd_tail", grid_spec=grid_spec,
        out_shape=(jax.ShapeDtypeStruct((SEQ, D_MODEL), F32),
                   jax.ShapeDtypeStruct((D_MODEL, SHARD_COLS), F32),
                   jax.ShapeDtypeStruct((WO_ROWS, D_MODEL), F32),
                   jax.ShapeDtypeStruct((8, D_MODEL), F32)),
        compiler_params=pltpu.CompilerParams(dimension_semantics=("arbitrary", "arbitrary"),
                                             vmem_limit_bytes=60 * 1024 * 1024, collective_id=COLLECTIVE_TAIL),
    )(kidx, h, dproj, wg, gwo, x2d, dx2, g1, small_a, small_b)


def _adam_update(w, g, m, v):
    nm = ADAM_B1 * m + (1.0 - ADAM_B1) * g
    nv = ADAM_B2 * v + (1.0 - ADAM_B2) * (g * g)
    m_hat = nm / (1.0 - ADAM_B1 ** ADAM_STEP)
    v_hat = nv / (1.0 - ADAM_B2 ** ADAM_STEP)
    return -ADAM_LR * (m_hat / (jnp.sqrt(v_hat) + ADAM_EPS) + ADAM_WD * w), nm, nv


def _adamw_all(tot, g_w_in, g_w_out, big, small, grad_x):
    n = len(small)
    rows = WO_ROWS
    steps = D_MODEL // rows

    def body(tot_ref, *refs):
        gx_ref, gx_out = refs[2 + 3 * (2 + n)], refs[-1]
        gx_out[...] = gx_ref[...]
        ins, outs = refs[:2 + 3 * (2 + n)], refs[3 + 3 * (2 + n):-1]
        g_refs, wmv = ins[:2], ins[2:]
        loss_ref, quads = outs[0], outs[1:]

        def update(j, g):
            w_ref, m_ref, v_ref = wmv[3 * j:3 * j + 3]
            g_ref, d_ref, nm_ref, nv_ref = quads[4 * j:4 * j + 4]
            g_ref[...] = g
            d_ref[...], nm_ref[...], nv_ref[...] = _adam_update(w_ref[...], g, m_ref[...], v_ref[...])

        update(0, g_refs[0][...])

        @pl.when(pl.program_id(0) == 0)
        def _():
            update(1, g_refs[1][...])
            k = 2 * lax.axis_index("x") + lax.axis_index("y")
            mine = pl.ds(pl.multiple_of(k * HEAD, HEAD), HEAD)
            loss_ref[...] = tot_ref[7:8, 0:1]
            grads = [tot_ref[0:1, :], tot_ref[1:2, :], tot_ref[2:3, 0:D_HGRN], tot_ref[2:3, D_HGRN:],
                     jnp.concatenate([tot_ref[3:4, 0:D_HGRN], tot_ref[3:4, D_HGRN:]], axis=0),
                     jnp.concatenate([tot_ref[4 + tap:5 + tap, mine] for tap in range(3)], axis=1)]
            for j, g in enumerate(grads):
                update(2 + j, g)

    whole = lambda a: pl.BlockSpec(a.shape, lambda i: (0, 0))
    blk = pl.BlockSpec((rows, SHARD_COLS), lambda i: (i, 0))
    arrays = [a for triple in big + small for a in triple]
    in_specs = ([whole(tot), blk, whole(g_w_out)] + [blk] * 3 + [whole(a) for a in arrays[3:]])
    shapes = [big[0][0], big[1][0]] + [w for w, _, _ in small]
    out_shape = (jax.ShapeDtypeStruct((1, 1), F32),) + tuple(
        jax.ShapeDtypeStruct(w.shape, F32) for w in shapes for _ in range(4))
    out_specs = (pl.BlockSpec((1, 1), lambda i: (0, 0)),) + (blk,) * 4 + tuple(
        whole(w) for w in shapes[1:] for _ in range(4))
    gx_blk = pl.BlockSpec((SEQ // steps, D_MODEL), lambda i: (i, 0))
    outs = pl.pallas_call(
        body, name="adamw_all", grid=(steps,),
        out_shape=out_shape + (jax.ShapeDtypeStruct(grad_x.shape, F32),),
        in_specs=in_specs + [gx_blk], out_specs=out_specs + (gx_blk,),
        compiler_params=pltpu.CompilerParams(dimension_semantics=("arbitrary",), vmem_limit_bytes=VMEM_LIMIT),
    )(tot, g_w_in, g_w_out, *arrays, grad_x)
    return [outs[0]] + [outs[1 + 4 * j:5 + 4 * j] for j in range(2 + n)] + [outs[-1]]


def _local_step(x2d, tgt, proj, lb_logits, cw, ga, gcn, w_out, gf):
    g64 = _group_matrix(HEAD, CONV_GROUP)
    aux, states, dx2, dmixed, gwo, part_out = _mix_out(proj, lb_logits, cw, ga, gcn, g64, w_out, x2d, gf, tgt)
    dproj, part_mix = _mix_bwd(proj, aux, states, dmixed, lb_logits, cw, ga, gcn, g64)
    return dproj, dx2, gwo.reshape(N_SHARD, WO_ROWS, D_MODEL), part_out, part_mix


def kernel(x, norm_gain, w_in, lb_logits, conv_w, hgrn_norm_gain, conv_norm_gain, w_out, final_norm_gain, loss_target, m_norm_gain, m_w_in, m_lb_logits, m_conv_w, m_hgrn_norm_gain, m_conv_norm_gain, m_w_out, m_final_norm_gain, v_norm_gain, v_w_in, v_lb_logits, v_conv_w, v_hgrn_norm_gain, v_conv_norm_gain, v_w_out, v_final_norm_gain):
    k = 2 * lax.axis_index("x") + lax.axis_index("y")
    kidx = jnp.reshape(k, (1,)).astype(jnp.int32)
    row = lambda a: a.reshape(1, D_MODEL)
    taps = lambda a: a.reshape(1, 3 * HEAD)
    h, proj, wg, cw = _gather_proj(kidx, x[0], norm_gain, w_in, taps(conv_w))
    dproj, dx2, gwo, part_out, part_mix = _local_step(
        x[0], loss_target[0], proj, lb_logits, cw, hgrn_norm_gain, conv_norm_gain, w_out, row(final_norm_gain))
    rgrad_x, rg_w_in, rg_w_out, tot = _bwd_tail(kidx, h, dproj, wg, gwo, x[0], dx2, norm_gain, part_out, part_mix)

    (loss, (g_w_in, d_w_in, nm_w_in, nv_w_in), (g_w_out, d_w_out, nm_w_out, nv_w_out),
     (g_norm_gain, d_ng, nm_ng, nv_ng), (g_final, d_fg, nm_fg, nv_fg), (g_hgrn, d_hg, nm_hg, nv_hg),
     (g_convn, d_cg, nm_cg, nv_cg), (g_lb, d_lb, nm_lb, nv_lb), (g_conv_w, d_cw, nm_cw, nv_cw),
     grad_x) = _adamw_all(
        tot, rg_w_in, rg_w_out,
        [(w_in[0], m_w_in[0], v_w_in[0]), (w_out[0], m_w_out[0], v_w_out[0])],
        [(norm_gain, m_norm_gain, v_norm_gain),
         (row(final_norm_gain), row(m_final_norm_gain), row(v_final_norm_gain)),
         (hgrn_norm_gain, m_hgrn_norm_gain, v_hgrn_norm_gain),
         (conv_norm_gain, m_conv_norm_gain, v_conv_norm_gain),
         (lb_logits, m_lb_logits, v_lb_logits),
         (taps(conv_w), taps(m_conv_w), taps(v_conv_w))],
        rgrad_x)
    flat = lambda a: a.reshape(D_MODEL)
    untap = lambda a: a.reshape(1, 3, HEAD)
    return (loss.reshape(()), grad_x[None],
            g_norm_gain, g_w_in[None], g_lb, untap(g_conv_w), g_hgrn, g_convn, g_w_out[None], flat(g_final),
            d_ng, d_w_in[None], d_lb, untap(d_cw), d_hg, d_cg, d_w_out[None], flat(d_fg),
            nm_ng, nm_w_in[None], nm_lb, untap(nm_cw), nm_hg, nm_cg, nm_w_out[None], flat(nm_fg),
            nv_ng, nv_w_in[None], nv_lb, untap(nv_cw), nv_hg, nv_cg, nv_w_out[None], flat(nv_fg))
```

```python
import jax
import jax.numpy as jnp
import numpy as np
from jax import lax
from jax.experimental import pallas as pl
from jax.experimental.pallas import tpu as pltpu

F32 = jnp.float32
BF16 = jnp.bfloat16
MESH = pl.DeviceIdType.MESH

SEQ = 2048
D_MODEL = 1024
D_HGRN = 512
D_CONV = 512
HEAD = 128
N_HEADS = 4
CHUNK = 64
CONV_GROUP = 64
N_SHARD = 4
SHARD_COLS = 1024
WO_ROWS = 256
EPS = 1e-6
TB = 256
NCB = TB // CHUNK
N_CHUNKS = SEQ // CHUNK
N_DEV = 8
COLLECTIVE_GATHER, COLLECTIVE_MIX_OUT, COLLECTIVE_TAIL = 1, 0, 2
AUX_O, AUX_CV, AUX_B, AUX_COLS = 0, 512, 1024, 1536

ADAM_LR = 0.001
ADAM_B1 = 0.9
ADAM_B2 = 0.999
ADAM_EPS = 1e-08
ADAM_WD = 0.01
ADAM_STEP = 10

VMEM_LIMIT = 56 * 1024 * 1024


def _dot(a, b):
    return jnp.dot(a, b, preferred_element_type=F32)


def _dot_nt(a, b):
    return lax.dot_general(a, b, (((1,), (1,)), ((), ())), preferred_element_type=F32)


def _dot_tn(a, b):
    return lax.dot_general(a, b, (((0,), (0,)), ((), ())), preferred_element_type=F32)


def _split_bf16(x, n):
    parts = []
    r = x
    for _ in range(n):
        p = r.astype(BF16)
        parts.append(p)
        r = r - p.astype(F32)
    return parts


def _exact_left(m, x, n=3):
    acc = None
    for p in _split_bf16(x, n):
        t = _dot(m, p)
        acc = t if acc is None else acc + t
    return acc


def _exact_left_many(m, xs, n=3):
    parts = [_split_bf16(x, n) for x in xs]
    accs = [None] * len(xs)
    for i in range(n):
        for j in range(len(xs)):
            t = _dot(m, parts[j][i])
            accs[j] = t if accs[j] is None else accs[j] + t
    return accs


def _group_mean_many(xs, gmat, n=2):
    parts = [_split_bf16(x, n) for x in xs]
    accs = [None] * len(xs)
    for i in range(n):
        for j in range(len(xs)):
            t = _dot(parts[j][i], gmat)
            accs[j] = t if accs[j] is None else accs[j] + t
    return accs


def _group_mean(x, gmat, n=2):
    w = gmat.shape[0]
    outs = []
    for c0 in range(0, x.shape[1], w):
        acc = None
        for p in _split_bf16(x[:, c0:c0 + w], n):
            t = _dot(p, gmat)
            acc = t if acc is None else acc + t
        outs.append(acc)
    return jnp.concatenate(outs, axis=1)


def _sigmoid(x):
    return 1.0 / (1.0 + jnp.exp(-x))


def _lower_bound(lbl):
    l0 = lbl[0:1, :]
    l1 = lbl[1:2, :]
    m = jnp.maximum(l0, l1)
    e0 = jnp.exp(l0 - m)
    e1 = jnp.exp(l1 - m)
    return e0 / (e0 + e1)


def _tri(lower):
    r = lax.broadcasted_iota(jnp.int32, (CHUNK, CHUNK), 0)
    c = lax.broadcasted_iota(jnp.int32, (CHUNK, CHUNK), 1)
    return jnp.where((c <= r) if lower else (c >= r), 1.0, 0.0).astype(BF16)


def _causal():
    r = lax.broadcasted_iota(jnp.int32, (CHUNK, CHUNK), 0)
    c = lax.broadcasted_iota(jnp.int32, (CHUNK, CHUNK), 1)
    return c <= r


def _shift_down(x, sh, prev_tail):
    r = pltpu.roll(x, sh, 0)
    pt = pltpu.roll(prev_tail, sh, 0)
    rows = lax.broadcasted_iota(jnp.int32, prev_tail.shape, 0)
    top = jnp.where(rows < sh, pt, r[0:8])
    return jnp.concatenate([top, r[8:]], axis=0)


def _shift_up(x, sh, next_head):
    n = x.shape[0]
    r = pltpu.roll(x, n - sh, 0)
    nh = pltpu.roll(next_head, 8 - sh, 0)
    rows = lax.broadcasted_iota(jnp.int32, next_head.shape, 0)
    bot = jnp.where(rows >= 8 - sh, nh, r[n - 8:])
    return jnp.concatenate([r[:n - 8], bot], axis=0)


def _group_matrix(width, group):
    r = np.arange(width)[:, None] // group
    c = np.arange(width)[None, :] // group
    return jnp.asarray(np.where(r == c, 1.0 / group, 0.0), dtype=BF16)


TG = 1024
SEM_W, SEM_CW, SEM_W_FWD, N_SEM = 0, 4, 7, 11


def _gather_proj(kidx, x2d, g1, w_in, conv_w):
    half_w = D_MODEL // 2
    half_c = SHARD_COLS // 2
    nt = SEQ // TG
    n_steps = 2 * N_SHARD

    def body(k_ref, x_ref, g_ref, w_ref, cw_ref, h_ref, p_ref, wg_out, cwg_out,
             wg_v, cwg_v, send_sems, recv_sems, out_sems):
        s, t = pl.program_id(0), pl.program_id(1)
        x, y, c = lax.axis_index("x"), lax.axis_index("y"), lax.axis_index("c")
        k = 2 * x + y
        sibling = (x, y, 1 - c)
        chips = [(1 - x, y), (x, 1 - y), (1 - x, 1 - y)]
        kjs = [2 * cx + cy for cx, cy in chips]
        diag = (*chips[2], c)

        def w_half(kk, cc):
            return wg_v.at[kk, pl.ds(cc * half_w, half_w), :]

        def w_quarter(kk, cc, piece):
            return wg_v.at[kk, pl.ds(cc * half_w, half_w), piece * half_c:(piece + 1) * half_c]

        def cw_of(kk):
            return cwg_v.at[:, pl.ds(pl.multiple_of(kk * HEAD, HEAD), HEAD)]

        def copy(sem, ref, to):
            return pltpu.make_async_remote_copy(
                src_ref=ref, dst_ref=ref, send_sem=send_sems.at[sem], recv_sem=recv_sems.at[sem],
                device_id=to, device_id_type=MESH)

        def at_step(sv, tv):
            return pl.when((s == sv) & (t == tv))

        w_direct = ([copy(SEM_W + j, w_half(k, c), (*chips[j], c)) for j in range(2)]
                    + [copy(SEM_W + 2 + p, w_quarter(k, c, p), diag) for p in range(2)])
        cw_direct = [copy(SEM_CW + j, cw_of(k), (*chip, c)) for j, chip in enumerate(chips)]
        w_passed = ([copy(SEM_W_FWD + j, w_half(kjs[j], c), sibling) for j in range(2)]
                    + [copy(SEM_W_FWD + 2 + p, w_quarter(kjs[2], c, p), sibling) for p in range(2)])
        stores = ([pltpu.make_async_copy(wg_v.at[kk], wg_out.at[kk], out_sems.at[i])
                   for i, kk in enumerate([k] + kjs)]
                  + [pltpu.make_async_copy(cwg_v, cwg_out, out_sems.at[4])])

        @at_step(0, 0)
        def _():
            barrier = pltpu.get_barrier_semaphore()
            for peer in [sibling] + [(*chip, c) for chip in chips]:
                pl.semaphore_signal(barrier, inc=1, device_id=peer, device_id_type=MESH)
            wg_v[k] = w_ref[0].astype(BF16)
            mine = pl.ds(pl.multiple_of(k * HEAD, HEAD), HEAD)
            cwg_v[:, mine] = jnp.zeros((8, HEAD), F32)
            for tap in range(3):
                cwg_v[tap:tap + 1, mine] = cw_ref[:, tap * HEAD:(tap + 1) * HEAD]
            pl.semaphore_wait(barrier, 4)
            for cp in w_direct + cw_direct:
                cp.start()
            stores[0].start()

        @at_step(2, 0)
        def _():
            for j in range(2):
                copy(SEM_W + j, w_half(kjs[j], c), sibling).wait_recv()
                w_passed[j].start()
            copy(SEM_W_FWD, w_half(kjs[0], 1 - c), sibling).wait_recv()
            stores[1].start()

        @at_step(4, 0)
        def _():
            copy(SEM_W_FWD + 1, w_half(kjs[1], 1 - c), sibling).wait_recv()
            stores[2].start()

        for p in range(2):
            @at_step(6 + p, 0)
            def _(p=p):
                copy(SEM_W + 2 + p, w_quarter(kjs[2], c, p), sibling).wait_recv()
                w_passed[2 + p].start()
                copy(SEM_W_FWD + 2 + p, w_quarter(kjs[2], 1 - c, p), sibling).wait_recv()

        rows = pl.ds(pl.multiple_of(t * TG, TG), TG)

        @pl.when(s == 0)
        def _():
            xv = x_ref[...]
            r = lax.rsqrt(jnp.mean(xv * xv, axis=-1, keepdims=True) + EPS)
            h_ref[rows, :] = (xv * r * g_ref[...]).astype(BF16)

        sh = s >> 1
        js = k ^ (((sh & 1) << 1) | (sh >> 1))
        for piece in range(2):
            @pl.when((s & 1) == piece)
            def _(piece=piece):
                p_ref[...] = _dot(h_ref[rows, :], wg_v[js, :, piece * half_c:(piece + 1) * half_c])

        @at_step(n_steps - 1, nt - 1)
        def _():
            stores[3].start()
            for j in range(3):
                copy(SEM_CW + j, cw_of(kjs[j]), sibling).wait_recv()
            stores[4].start()
            for cp in w_direct + cw_direct + w_passed:
                cp.wait_send()
            for st in stores:
                st.wait()

    def x_map(s, t, kr):
        return (jnp.where(s == 0, t, nt - 1), 0)

    def p_map(s, t, kr):
        sh = s >> 1
        return (t, 2 * (kr[0] ^ (((sh & 1) << 1) | (sh >> 1))) + (s & 1))

    hbm = pl.BlockSpec(memory_space=pl.ANY)
    grid_spec = pltpu.PrefetchScalarGridSpec(
        num_scalar_prefetch=1, grid=(n_steps, nt),
        in_specs=[pl.BlockSpec((TG, D_MODEL), x_map),
                  pl.BlockSpec((1, D_MODEL), lambda s, t, kr: (0, 0)),
                  pl.BlockSpec((1, D_MODEL, SHARD_COLS), lambda s, t, kr: (0, 0, 0)),
                  pl.BlockSpec((1, 3 * HEAD), lambda s, t, kr: (0, 0))],
        out_specs=(pl.BlockSpec((SEQ, D_MODEL), lambda s, t, kr: (0, 0)),
                   pl.BlockSpec((TG, half_c), p_map), hbm, hbm),
        scratch_shapes=[pltpu.VMEM((N_SHARD, D_MODEL, SHARD_COLS), BF16),
                        pltpu.VMEM((8, D_CONV), F32),
                        pltpu.SemaphoreType.DMA((N_SEM,)), pltpu.SemaphoreType.DMA((N_SEM,)),
                        pltpu.SemaphoreType.DMA((5,))])
    return pl.pallas_call(
        body, name="gather_proj", grid_spec=grid_spec,
        out_shape=(jax.ShapeDtypeStruct((SEQ, D_MODEL), BF16),
                   jax.ShapeDtypeStruct((SEQ, N_SHARD * SHARD_COLS), F32),
                   jax.ShapeDtypeStruct((N_SHARD, D_MODEL, SHARD_COLS), BF16),
                   jax.ShapeDtypeStruct((8, D_CONV), F32)),
        compiler_params=pltpu.CompilerParams(dimension_semantics=("arbitrary", "arbitrary"),
                                             vmem_limit_bytes=VMEM_LIMIT, collective_id=COLLECTIVE_GATHER),
    )(kidx, x2d, g1, w_in, conv_w)


LAG = 6


def _mix_out(proj, lb_logits, cw, ga, gcn, g64, w_out, x2d, gf, tgt):
    half_o = WO_ROWS // 2
    nblk = SEQ // TB
    n_steps = nblk + LAG

    def body(p_ref, lbl_ref, cw_ref, ga_ref, gcn_ref, g64_ref, wo_ref, x_ref, gf_ref, t_ref,
             aux_ref, sto_ref, dx2_ref, dm_ref, gwo_ref, part_ref,
             st_ref, tail_ref, wog_v, stage, ring, acc_ref, send_sems, recv_sems):
        i = pl.program_id(0)
        x, y, c = lax.axis_index("x"), lax.axis_index("y"), lax.axis_index("c")
        k = 2 * x + y
        sibling = (x, y, 1 - c)
        chips = [(1 - x, y), (x, 1 - y), (1 - x, 1 - y)]
        kjs = [2 * cx + cy for cx, cy in chips]

        def wo_half(kk, cc):
            return wog_v.at[pl.ds(pl.multiple_of(kk * WO_ROWS + cc * half_o, half_o), half_o), :]

        def copy(sem, ref, to):
            return pltpu.make_async_remote_copy(
                src_ref=ref, dst_ref=ref, send_sem=send_sems.at[sem], recv_sem=recv_sems.at[sem],
                device_id=to, device_id_type=MESH)

        wo_direct = [copy(j, wo_half(k, c), (*chip, c)) for j, chip in enumerate(chips)]
        wo_passed = [copy(3 + j, wo_half(kj, c), sibling) for j, kj in enumerate(kjs)]

        @pl.when(i == 0)
        def _():
            barrier = pltpu.get_barrier_semaphore()
            for peer in [sibling] + [(*chip, c) for chip in chips]:
                pl.semaphore_signal(barrier, inc=1, device_id=peer, device_id_type=MESH)
            st_ref[...] = jnp.zeros_like(st_ref)
            tail_ref[...] = jnp.zeros_like(tail_ref)
            acc_ref[...] = jnp.zeros_like(acc_ref)
            part_ref[...] = jnp.zeros_like(part_ref)
            wog_v[pl.ds(pl.multiple_of(k * WO_ROWS, WO_ROWS), WO_ROWS), :] = wo_ref[0].astype(BF16)
            pl.semaphore_wait(barrier, 4)
            for cp in wo_direct:
                cp.start()

        @pl.when(i == LAG - 1)
        def _():
            for j in range(3):
                copy(j, wo_half(kjs[j], c), sibling).wait_recv()
                wo_passed[j].start()

        @pl.when(i == LAG)
        def _():
            for j in range(3):
                copy(3 + j, wo_half(kjs[j], 1 - c), sibling).wait_recv()

        lb = _lower_bound(lbl_ref[...])
        tri = _tri(True)
        causal = _causal()
        g64m = g64_ref[...]
        heads = range(N_HEADS)
        cs = [slice(hd * HEAD, (hd + 1) * HEAD) for hd in heads]
        col = lambda base, hd: slice(base + hd * HEAD, base + (hd + 1) * HEAD)

        def mix_chunk(n):
            sl = pl.ds(n * CHUNK, CHUNK)
            sg = [_sigmoid(p_ref[sl, col(512, hd)]) for hd in heads]
            f = [lb[:, cs[hd]] + (1.0 - lb[:, cs[hd]]) * sg[hd] for hd in heads]
            bc = _exact_left_many(tri, [jnp.log(f[hd]) for hd in heads])
            for hd in heads:
                aux_ref[sl, col(AUX_B, hd)] = bc[hd]
            g = [bc[hd][CHUNK - 1:CHUNK, :] for hd in heads]
            qd = [(p_ref[sl, col(0, hd)] * jnp.exp(bc[hd])).astype(BF16) for hd in heads]
            kk = [1.0 - f[hd] for hd in heads]
            ki = [(kk[hd] * jnp.exp(-bc[hd])).astype(BF16) for hd in heads]
            ke = [(kk[hd] * jnp.exp(g[hd] - bc[hd])).astype(BF16) for hd in heads]
            vb = [p_ref[sl, col(1024, hd)].astype(BF16) for hd in heads]
            st = [st_ref[hd] for hd in heads]
            st_b = [a.astype(BF16) for a in st]
            for hd in heads:
                sto_ref[n, hd] = st_b[hd]
            scm = [_dot_nt(qd[hd], ki[hd]) for hd in heads]
            inter = [_dot_nt(qd[hd], st_b[hd]) for hd in heads]
            upd = [_dot_tn(vb[hd], ke[hd]) for hd in heads]
            intra = [_dot(jnp.where(causal, scm[hd], 0.0).astype(BF16), vb[hd]) for hd in heads]
            for hd in heads:
                st_ref[hd] = st[hd] * jnp.exp(g[hd]) + upd[hd]
                o = intra[hd] + inter[hd]
                aux_ref[sl, col(AUX_O, hd)] = o
                ra = lax.rsqrt(jnp.mean(o * o, axis=-1, keepdims=True) + EPS)
                za = p_ref[sl, col(1536, hd)]
                stage[sl, cs[hd]] = (o * ra * ga_ref[:, cs[hd]] * (za * _sigmoid(za))).astype(BF16)
            yb = []
            for hd in heads:
                cu = p_ref[sl, col(3072, hd)] * p_ref[sl, col(2048, hd)]
                tail = tail_ref[:, cs[hd]]
                cv = (cw_ref[0:1, cs[hd]] * _shift_down(cu, 2, tail) + cw_ref[1:2, cs[hd]] * _shift_down(cu, 1, tail)
                      + cw_ref[2:3, cs[hd]] * cu)
                tail_ref[:, cs[hd]] = cu[CHUNK - 8:, :]
                aux_ref[sl, col(AUX_CV, hd)] = cv
                yb.append(p_ref[sl, col(2560, hd)] * cv)
            ms = _group_mean_many([y * y for y in yb], g64m)
            for hd in heads:
                rb = lax.rsqrt(ms[hd] + EPS)
                zb = p_ref[sl, col(3584, hd)]
                stage[sl, col(512, hd)] = (yb[hd] * rb * gcn_ref[:, cs[hd]] * (zb * _sigmoid(zb))).astype(BF16)

        def step(mix, project):
            if project:
                mixed_b = ring[pl.ds(pl.multiple_of((i - LAG) * TB, TB), TB), :]
                y = _dot(mixed_b, wog_v[...])
            if mix:
                mix_chunk(0)
            if project:
                x2 = x_ref[...] + y
                r2 = lax.rsqrt(jnp.mean(x2 * x2, axis=-1, keepdims=True) + EPS)
                n2 = x2 * r2
                gfv = gf_ref[...]
                err = n2 * gfv - t_ref[...]
                loss = 0.5 * jnp.sum(jnp.mean(err * err, axis=-1, keepdims=True), axis=0, keepdims=True)
                dy = err * (1.0 / D_MODEL)
                part_ref[1:2, :] += jnp.sum(dy * n2, axis=0, keepdims=True)
                part_ref[7:8, :] += jnp.broadcast_to(loss, (1, D_MODEL))
                dn = dy * gfv
                dx2 = r2 * (dn - n2 * jnp.mean(dn * n2, axis=-1, keepdims=True))
                dx2_ref[...] = dx2
                dx2_b = dx2.astype(BF16)
            if mix:
                mix_chunk(1)
            if project:
                dm_ref[...] = _dot_nt(dx2_b, wog_v[...])
            if mix:
                mix_chunk(2)
            if project:
                acc_ref[...] += _dot_tn(mixed_b, dx2_b)
            if mix:
                mix_chunk(3)
                ring[pl.ds(pl.multiple_of(i * TB, TB), TB), :] = stage[...]

        @pl.when(i < LAG)
        def _():
            step(True, False)

        @pl.when((i >= LAG) & (i < nblk))
        def _():
            step(True, True)

        @pl.when(i >= nblk)
        def _():
            step(False, True)

        @pl.when(i == n_steps - 1)
        def _():
            gwo_ref[...] = acc_ref[...].astype(BF16)
            for cp in wo_direct + wo_passed:
                cp.wait_send()

    assert NCB == 4
    row = lambda w: pl.BlockSpec((1, w), lambda i: (0, 0))
    mix_blk = lambda i: jnp.minimum(i, nblk - 1)
    out_blk = lambda i: jnp.clip(i - LAG, 0, nblk - 1)
    tok = lambda: pl.BlockSpec((TB, D_MODEL), lambda i: (out_blk(i), 0))
    return pl.pallas_call(
        body, name="mix_out", grid=(n_steps,),
        out_shape=(jax.ShapeDtypeStruct((SEQ, AUX_COLS), F32),
                   jax.ShapeDtypeStruct((N_CHUNKS, N_HEADS, HEAD, HEAD), BF16),
                   jax.ShapeDtypeStruct((SEQ, D_MODEL), F32),
                   jax.ShapeDtypeStruct((SEQ, D_MODEL), F32),
                   jax.ShapeDtypeStruct((D_MODEL, D_MODEL), BF16),
                   jax.ShapeDtypeStruct((8, D_MODEL), F32)),
        in_specs=[pl.BlockSpec((TB, 4096), lambda i: (jnp.minimum(i, nblk - 1), 0)),
                  pl.BlockSpec((2, D_HGRN), lambda i: (0, 0)),
                  pl.BlockSpec((8, D_CONV), lambda i: (0, 0)),
                  row(D_HGRN), row(D_CONV),
                  pl.BlockSpec((HEAD, HEAD), lambda i: (0, 0)),
                  pl.BlockSpec((1, WO_ROWS, D_MODEL), lambda i: (0, 0, 0)),
                  tok(), row(D_MODEL), tok()],
        out_specs=(pl.BlockSpec((TB, AUX_COLS), lambda i: (mix_blk(i), 0)),
                   pl.BlockSpec((NCB, N_HEADS, HEAD, HEAD), lambda i: (mix_blk(i), 0, 0, 0)),
                   tok(), tok(),
                   pl.BlockSpec((D_MODEL, D_MODEL), lambda i: (0, 0)),
                   pl.BlockSpec((8, D_MODEL), lambda i: (0, 0))),
        scratch_shapes=[pltpu.VMEM((N_HEADS, HEAD, HEAD), F32), pltpu.VMEM((8, D_CONV), F32),
                        pltpu.VMEM((D_MODEL, D_MODEL), BF16), pltpu.VMEM((TB, D_MODEL), BF16),
                        pltpu.VMEM((SEQ, D_MODEL), BF16), pltpu.VMEM((D_MODEL, D_MODEL), F32),
                        pltpu.SemaphoreType.DMA((6,)), pltpu.SemaphoreType.DMA((6,))],
        compiler_params=pltpu.CompilerParams(dimension_semantics=("arbitrary",), vmem_limit_bytes=VMEM_LIMIT,
                                             collective_id=COLLECTIVE_MIX_OUT),
    )(proj, lb_logits, cw, ga, gcn, g64, w_out, x2d, gf, tgt)


def _mix_bwd(proj, aux, states, dmixed, lb_logits, cw, ga, gcn, g64):
    nblk = SEQ // TB

    def body(p_ref, aux_ref, st_ref, dm_ref, lbl_ref, cw_ref, ga_ref, gcn_ref, g64_ref,
             dp_ref, part_ref, dst_ref, head_ref, dlb_ref):
        i = pl.program_id(0)

        @pl.when(i == 0)
        def _():
            dst_ref[...] = jnp.zeros_like(dst_ref)
            head_ref[...] = jnp.zeros_like(head_ref)
            part_ref[...] = jnp.zeros_like(part_ref)
            dlb_ref[...] = jnp.zeros_like(dlb_ref)

        lb = _lower_bound(lbl_ref[...])
        triu = _tri(False)
        causal = _causal()
        g64m = g64_ref[...]
        rowsum = lambda a: jnp.sum(a, axis=0, keepdims=True)
        heads = range(N_HEADS)
        cs = [slice(hd * HEAD, (hd + 1) * HEAD) for hd in heads]
        col = lambda base, hd: slice(base + hd * HEAD, base + (hd + 1) * HEAD)
        for n in reversed(range(NCB)):
            sl = pl.ds(n * CHUNK, CHUNK)
            cvv = [aux_ref[sl, col(AUX_CV, hd)] for hd in heads]
            gb = [p_ref[sl, col(2560, hd)] for hd in heads]
            yb = [gb[hd] * cvv[hd] for hd in heads]
            ms = _group_mean_many([y * y for y in yb], g64m)
            rb, nb, dnb = [], [], []
            for hd in heads:
                rb.append(lax.rsqrt(ms[hd] + EPS))
                nb.append(yb[hd] * rb[hd])
                zb = p_ref[sl, col(3584, hd)]
                sgb = _sigmoid(zb)
                dmb = dm_ref[sl, col(512, hd)]
                silu = zb * sgb
                dgate = dmb * gcn_ref[:, cs[hd]]
                part_ref[2:3, col(512, hd)] += rowsum(dmb * nb[hd] * silu)
                dp_ref[sl, col(3584, hd)] = (dgate * nb[hd] * (sgb + silu * (1.0 - sgb))).astype(BF16)
                dnb.append(dgate * silu)
            mdn = _group_mean_many([dnb[hd] * nb[hd] for hd in heads], g64m)
            for hd in heads:
                dyb = rb[hd] * (dnb[hd] - nb[hd] * mdn[hd])
                dp_ref[sl, col(2560, hd)] = (dyb * cvv[hd]).astype(BF16)
                dcv = dyb * gb[hd]
                head = head_ref[:, cs[hd]]
                dcv1 = _shift_up(dcv, 1, head)
                dcv2 = _shift_up(dcv, 2, head)
                head_ref[:, cs[hd]] = dcv[0:8, :]
                u = p_ref[sl, col(2048, hd)]
                gc = p_ref[sl, col(3072, hd)]
                cu = gc * u
                part_ref[4:5, cs[hd]] += rowsum(dcv2 * cu)
                part_ref[5:6, cs[hd]] += rowsum(dcv1 * cu)
                part_ref[6:7, cs[hd]] += rowsum(dcv * cu)
                dcu = cw_ref[2:3, cs[hd]] * dcv + cw_ref[1:2, cs[hd]] * dcv1 + cw_ref[0:1, cs[hd]] * dcv2
                dp_ref[sl, col(3072, hd)] = (dcu * u).astype(BF16)
                dp_ref[sl, col(2048, hd)] = (dcu * gc).astype(BF16)
            do_b = []
            for hd in heads:
                ov = aux_ref[sl, col(AUX_O, hd)]
                ra = lax.rsqrt(jnp.mean(ov * ov, axis=-1, keepdims=True) + EPS)
                na = ov * ra
                za = p_ref[sl, col(1536, hd)]
                sga = _sigmoid(za)
                dma = dm_ref[sl, cs[hd]]
                silu = za * sga
                dgate = dma * ga_ref[:, cs[hd]]
                part_ref[2:3, cs[hd]] += rowsum(dma * na * silu)
                dp_ref[sl, col(1536, hd)] = (dgate * na * (sga + silu * (1.0 - sga))).astype(BF16)
                dna = dgate * silu
                do_b.append((ra * (dna - na * jnp.mean(dna * na, axis=-1, keepdims=True))).astype(BF16))
            s = [_sigmoid(p_ref[sl, col(512, hd)]) for hd in heads]
            f = [lb[:, cs[hd]] + (1.0 - lb[:, cs[hd]]) * s[hd] for hd in heads]
            bc = [aux_ref[sl, col(AUX_B, hd)] for hd in heads]
            g = [bc[hd][CHUNK - 1:CHUNK, :] for hd in heads]
            eb = [jnp.exp(bc[hd]) for hd in heads]
            enb = [jnp.exp(-bc[hd]) for hd in heads]
            eg = [jnp.exp(g[hd] - bc[hd]) for hd in heads]
            dec = [jnp.exp(g[hd]) for hd in heads]
            qd = [p_ref[sl, cs[hd]] * eb[hd] for hd in heads]
            kk = [1.0 - f[hd] for hd in heads]
            ki = [kk[hd] * enb[hd] for hd in heads]
            ke = [kk[hd] * eg[hd] for hd in heads]
            qd_b = [a.astype(BF16) for a in qd]
            ki_b = [a.astype(BF16) for a in ki]
            ke_b = [a.astype(BF16) for a in ke]
            vb = [p_ref[sl, col(1024, hd)].astype(BF16) for hd in heads]
            st_b = [st_ref[n, hd] for hd in heads]
            dst = [dst_ref[hd] for hd in heads]
            dst_b = [a.astype(BF16) for a in dst]
            scm = [_dot_nt(qd_b[hd], ki_b[hd]) for hd in heads]
            amm = [_dot_nt(do_b[hd], vb[hd]) for hd in heads]
            dqd2 = [_dot(do_b[hd], st_b[hd]) for hd in heads]
            dke = [_dot(vb[hd], dst_b[hd]) for hd in heads]
            dv2 = [_dot_nt(ke_b[hd], dst_b[hd]) for hd in heads]
            dsu = [_dot_tn(do_b[hd], qd_b[hd]) for hd in heads]
            sc = [jnp.where(causal, scm[hd], 0.0).astype(BF16) for hd in heads]
            am = [jnp.where(causal, amm[hd], 0.0).astype(BF16) for hd in heads]
            dqd1 = [_dot(am[hd], ki_b[hd]) for hd in heads]
            dki = [_dot_tn(am[hd], qd_b[hd]) for hd in heads]
            dv1 = [_dot_tn(sc[hd], do_b[hd]) for hd in heads]
            db, dgv, dkk = [], [], []
            for hd in heads:
                dqd = dqd1[hd] + dqd2[hd]
                ddec = rowsum(dst[hd] * st_b[hd].astype(F32))
                dst_ref[hd] = dst[hd] * dec[hd] + dsu[hd]
                dp_ref[sl, cs[hd]] = (dqd * eb[hd]).astype(BF16)
                dp_ref[sl, col(1024, hd)] = (dv1[hd] + dv2[hd]).astype(BF16)
                dke_eg = dke[hd] * eg[hd]
                dkk.append(dki[hd] * enb[hd] + dke_eg)
                db.append(dqd * qd[hd] - kk[hd] * dkk[hd])
                dgv.append(rowsum(kk[hd] * dke_eg) + ddec * dec[hd])
            rc = _exact_left_many(triu, db, 2)
            for hd in heads:
                df = (rc[hd] + dgv[hd]) / f[hd] - dkk[hd]
                one_s = 1.0 - s[hd]
                dlb_ref[:, cs[hd]] += rowsum(df * one_s)
                dp_ref[sl, col(512, hd)] = (df * (1.0 - lb[:, cs[hd]]) * s[hd] * one_s).astype(BF16)

        @pl.when(i == nblk - 1)
        def _():
            row = dlb_ref[...] * lb * (1.0 - lb)
            part_ref[3:4, 0:D_HGRN] = row
            part_ref[3:4, D_HGRN:] = -row

    rev = lambda w: pl.BlockSpec((TB, w), lambda i: (nblk - 1 - i, 0))
    row = lambda w: pl.BlockSpec((1, w), lambda i: (0, 0))
    return pl.pallas_call(
        body, name="mix_bwd", grid=(nblk,),
        out_shape=(jax.ShapeDtypeStruct((SEQ, 4096), BF16),
                   jax.ShapeDtypeStruct((8, D_MODEL), F32)),
        in_specs=[rev(4096), rev(AUX_COLS),
                  pl.BlockSpec((NCB, N_HEADS, HEAD, HEAD), lambda i: (nblk - 1 - i, 0, 0, 0)),
                  rev(D_MODEL),
                  pl.BlockSpec((2, D_HGRN), lambda i: (0, 0)),
                  pl.BlockSpec((8, D_CONV), lambda i: (0, 0)),
                  row(D_HGRN), row(D_CONV),
                  pl.BlockSpec((HEAD, HEAD), lambda i: (0, 0))],
        out_specs=(rev(4096), pl.BlockSpec((8, D_MODEL), lambda i: (0, 0))),
        scratch_shapes=[pltpu.VMEM((N_HEADS, HEAD, HEAD), F32), pltpu.VMEM((8, D_CONV), F32),
                        pltpu.VMEM((1, D_HGRN), F32)],
        compiler_params=pltpu.CompilerParams(dimension_semantics=("arbitrary",), vmem_limit_bytes=VMEM_LIMIT),
    )(proj, aux, states, dmixed, lb_logits, cw, ga, gcn, g64)


TT = 1024
TX = 512
(SEM_D2D, SEM_D2D_O, SEM_ICI, SEM_ICI_O, SEM_FIN, SEM_FIN_O, SEM_SMALL, SEM_VIA, N_SEM_TAIL) = (
    0, 4, 5, 8, 11, 12, 12, 20, 22)


def _bwd_tail(kidx, h, dproj, wg, gwo, x2d, dx2, g1, small_a, small_b):
    hw = D_MODEL // 2
    ho = WO_ROWS // 2
    nt = SEQ // TT
    norm_step = 2 * N_SHARD
    n_steps = norm_step + SEQ // TX // nt

    def body(k_ref, h_ref, dp_ref, w_ref, gwo_ref, x_ref, dx2_ref, g_ref, sm_ref, smb_ref,
             gx_ref, gw_out, gwo_out, osm_ref,
             acc, dh, sendbuf, keep, sibrcv, rcv, merge, sib_o, p_o, rcv_o, res_o, sm_buf, dng,
             send_sems, recv_sems, out_sems):
        s, t = pl.program_id(0), pl.program_id(1)
        x, y, c = lax.axis_index("x"), lax.axis_index("y"), lax.axis_index("c")
        k = 2 * x + y
        me = 4 * x + 2 * y + c
        sibling = (x, y, 1 - c)
        chips = [(1 - x, 1 - y), (1 - x, y), (x, 1 - y)]
        kjs = [2 * cx + cy for cx, cy in chips]
        mine = pl.ds(pl.multiple_of(c * hw, hw), hw)
        other = pl.ds(pl.multiple_of((1 - c) * hw, hw), hw)
        mine_o = pl.ds(pl.multiple_of(c * ho, ho), ho)
        other_o = pl.ds(pl.multiple_of((1 - c) * ho, ho), ho)

        def copy(sem, src, dst, to):
            return pltpu.make_async_remote_copy(
                src_ref=src, dst_ref=dst, send_sem=send_sems.at[sem], recv_sem=recv_sems.at[sem],
                device_id=to, device_id_type=MESH)

        def at_step(sv, tv):
            return pl.when((s == sv) & (t == tv))

        def at_norm_block(b):
            return at_step(norm_step + b // nt, b % nt)

        d2d = [copy(SEM_D2D + sv, sendbuf.at[sv], sibrcv.at[sv], sibling) for sv in range(N_SHARD)]
        d2d_o = copy(SEM_D2D_O, gwo_ref.at[:, other_o, :], sib_o, sibling)
        ici = {sv: copy(SEM_ICI + sv, keep.at[sv], rcv.at[sv - 1], (*chips[sv], c)) for sv in (1, 2)}
        qh = hw // 2
        via = [copy(SEM_VIA, keep.at[0, 0:qh, :], merge.at[1], (*chips[1], c)),
               copy(SEM_VIA + 1, keep.at[0, qh:hw, :], merge.at[0], (*chips[2], c))]
        merged_rows = [slice(qh, hw), slice(0, qh)]
        ici_o = [copy(SEM_ICI_O + sv, p_o.at[kjs[sv]], rcv_o.at[sv], (*chips[sv], c)) for sv in range(3)]
        fin = copy(SEM_FIN, acc.at[mine, :], gw_out.at[mine, :], sibling)
        fin_o = copy(SEM_FIN_O, res_o.at[mine_o, :], res_o.at[mine_o, :], sibling)
        smalls = [copy(SEM_SMALL + m, sm_buf.at[me], sm_buf.at[me],
                       (x ^ (m >> 2), y ^ ((m >> 1) & 1), c ^ (m & 1))) for m in range(1, N_DEV)]
        store_w = pltpu.make_async_copy(acc.at[mine, :], gw_out.at[mine, :], out_sems.at[0])
        store_o = pltpu.make_async_copy(res_o, gwo_out, out_sems.at[1])

        @at_step(0, 0)
        def _():
            barrier = pltpu.get_barrier_semaphore()
            for m in range(1, N_DEV):
                pl.semaphore_signal(barrier, inc=1, device_id=(x ^ (m >> 2), y ^ ((m >> 1) & 1), c ^ (m & 1)),
                                    device_id_type=MESH)
            pl.semaphore_wait(barrier, N_DEV - 1)
            d2d_o.start()

        @at_step(0, 1)
        def _():
            d2d_o.wait_recv()
            for j in range(N_SHARD):
                p_o[j] = (gwo_ref[j, mine_o, :].astype(F32) + sib_o[j].astype(F32)).astype(BF16)
            res_o[mine_o, :] = gwo_ref[k, mine_o, :].astype(F32) + sib_o[k].astype(F32)
            for cp in ici_o:
                cp.start()

        rows = pl.ds(pl.multiple_of(t * TT, TT), TT)

        @pl.when((s < N_SHARD) & (t == 0))
        def _():
            acc[...] = _dot_tn(h_ref[...], dp_ref[...])

        @pl.when((s < N_SHARD) & (t > 0))
        def _():
            acc[...] += _dot_tn(h_ref[...], dp_ref[...])

        for sv in range(N_SHARD):
            @at_step(sv, nt - 1)
            def _(sv=sv):
                sendbuf[sv] = acc[other, :].astype(BF16)
                if sv < 3:
                    keep[sv] = acc[mine, :].astype(BF16)
                d2d[sv].start()

        @at_step(1, 0)
        def _():
            d2d[0].wait_recv()
            keep[0] = (keep[0].astype(F32) + sibrcv[0].astype(F32)).astype(BF16)
            for cp in via:
                cp.start()

        for sv in (1, 2):
            @at_step(sv + 2, 0)
            def _(sv=sv):
                d2d[sv].wait_recv()
                keep[sv] = (keep[sv].astype(F32) + sibrcv[sv].astype(F32)).astype(BF16)
                via[2 - sv].wait_recv()
                rows_m = merged_rows[sv - 1]
                keep[sv, rows_m, :] = (keep[sv, rows_m, :].astype(F32) + merge[sv - 1].astype(F32)).astype(BF16)
                ici[sv].start()

        @pl.when(s == N_SHARD)
        def _():
            dh[rows, :] = _dot_nt(dp_ref[...], w_ref[0])

        @pl.when((s > N_SHARD) & (s < norm_step))
        def _():
            dh[rows, :] += _dot_nt(dp_ref[...], w_ref[0])

        @at_norm_block(0)
        def _():
            d2d[3].wait_recv()
            acc[mine, :] += sibrcv[3].astype(F32)

        @at_norm_block(1)
        def _():
            tot = res_o[mine_o, :]
            for sv in range(3):
                ici_o[sv].wait_recv()
                tot = tot + rcv_o[sv].astype(F32)
            res_o[mine_o, :] = tot
            fin_o.start()

        @at_norm_block(2)
        def _():
            ici[1].wait_recv()
            acc[mine, :] += rcv[0].astype(F32)

        @at_norm_block(SEQ // TX - 2)
        def _():
            ici[2].wait_recv()
            acc[mine, :] += rcv[1].astype(F32)
            fin.start()
            store_w.start()
            fin_o.wait_recv()
            store_o.start()

        @at_norm_block(0)
        def _():
            dng[...] = jnp.zeros_like(dng)

        @pl.when(s >= norm_step)
        def _():
            blk = (s - norm_step) * nt + t
            dhv = dh[pl.ds(pl.multiple_of(blk * TX, TX), TX), :]
            xv = x_ref[...]
            r = lax.rsqrt(jnp.mean(xv * xv, axis=-1, keepdims=True) + EPS)
            xn = xv * r
            dng[...] += jnp.sum(dhv * xn, axis=0, keepdims=True)
            dxn = dhv * g_ref[...]
            gx_ref[...] = dx2_ref[...] + r * (dxn - xn * jnp.mean(dxn * xn, axis=-1, keepdims=True))

        @at_step(n_steps - 1, nt - 1)
        def _():
            sm_buf[me] = sm_ref[...] + smb_ref[...]
            sm_buf[me, 0:1, :] = dng[...]
            for cp in smalls:
                cp.start()
            for m in range(1, N_DEV):
                copy(SEM_SMALL + m, sm_buf.at[0], sm_buf.at[0], sibling).wait_recv()
            tot = sm_buf[0]
            for d in range(1, N_DEV):
                tot = tot + sm_buf[d]
            osm_ref[...] = tot
            fin.wait_recv()
            for cp in d2d + [d2d_o] + via + list(ici.values()) + ici_o + [fin, fin_o] + smalls:
                cp.wait_send()
            store_o.wait()
            store_w.wait()

    def shard_of(s, kr):
        order = jnp.where(s < N_SHARD, s, jnp.where(s < norm_step, s - N_SHARD, 3))
        return kr[0] ^ (3 - order)

    def h_map(s, t, kr):
        return (jnp.where(s < N_SHARD, t, nt - 1), 0)

    def dp_map(s, t, kr):
        return (jnp.where(s < norm_step, t, nt - 1), shard_of(s, kr))

    def w_map(s, t, kr):
        return (shard_of(jnp.maximum(s, N_SHARD), kr), 0, 0)

    def blk_map(s, t, kr):
        return (jnp.where(s < norm_step, 0, (s - norm_step) * nt + t), 0)

    hbm = pl.BlockSpec(memory_space=pl.ANY)
    grid_spec = pltpu.PrefetchScalarGridSpec(
        num_scalar_prefetch=1, grid=(n_steps, nt),
        in_specs=[pl.BlockSpec((TT, D_MODEL), h_map),
                  pl.BlockSpec((TT, SHARD_COLS), dp_map),
                  pl.BlockSpec((1, D_MODEL, SHARD_COLS), w_map),
                  pl.BlockSpec((N_SHARD, WO_ROWS, D_MODEL), lambda s, t, kr: (0, 0, 0),
                               pipeline_mode=pl.Buffered(1)),
                  pl.BlockSpec((TX, D_MODEL), blk_map),
                  pl.BlockSpec((TX, D_MODEL), blk_map),
                  pl.BlockSpec((1, D_MODEL), lambda s, t, kr: (0, 0)),
                  pl.BlockSpec((8, D_MODEL), lambda s, t, kr: (0, 0)),
                  pl.BlockSpec((8, D_MODEL), lambda s, t, kr: (0, 0))],
        out_specs=(pl.BlockSpec((TX, D_MODEL), blk_map), hbm, hbm,
                   pl.BlockSpec((8, D_MODEL), lambda s, t, kr: (0, 0))),
        scratch_shapes=[pltpu.VMEM((D_MODEL, SHARD_COLS), F32), pltpu.VMEM((SEQ, D_MODEL), F32),
                        pltpu.VMEM((N_SHARD, hw, SHARD_COLS), BF16), pltpu.VMEM((3, hw, SHARD_COLS), BF16),
                        pltpu.VMEM((N_SHARD, hw, SHARD_COLS), BF16), pltpu.VMEM((2, hw, SHARD_COLS), BF16),
                        pltpu.VMEM((2, hw // 2, SHARD_COLS), BF16),
                        pltpu.VMEM((N_SHARD, ho, D_MODEL), BF16), pltpu.VMEM((N_SHARD, ho, D_MODEL), BF16),
                        pltpu.VMEM((3, ho, D_MODEL), BF16), pltpu.VMEM((WO_ROWS, D_MODEL), F32),
                        pltpu.VMEM((N_DEV, 8, D_MODEL), F32), pltpu.VMEM((1, D_MODEL), F32),
                        pltpu.SemaphoreType.DMA((N_SEM_TAIL,)), pltpu.SemaphoreType.DMA((N_SEM_TAIL,)),
                        pltpu.SemaphoreType.DMA((2,))])
    return pl.pallas_call(
        body, name="bwd_tail", grid_spec=grid_spec,
        out_shape=(jax.ShapeDtypeStruct((SEQ, D_MODEL), F32),
                   jax.ShapeDtypeStruct((D_MODEL, SHARD_COLS), F32),
                   jax.ShapeDtypeStruct((WO_ROWS, D_MODEL), F32),
                   jax.ShapeDtypeStruct((8, D_MODEL), F32)),
        compiler_params=pltpu.CompilerParams(dimension_semantics=("arbitrary", "arbitrary"),
                                             vmem_limit_bytes=61 * 1024 * 1024, collective_id=COLLECTIVE_TAIL),
    )(kidx, h, dproj, wg, gwo, x2d, dx2, g1, small_a, small_b)


def _adam_update(w, g, m, v):
    nm = ADAM_B1 * m + (1.0 - ADAM_B1) * g
    nv = ADAM_B2 * v + (1.0 - ADAM_B2) * (g * g)
    m_hat = nm / (1.0 - ADAM_B1 ** ADAM_STEP)
    v_hat = nv / (1.0 - ADAM_B2 ** ADAM_STEP)
    return -ADAM_LR * (m_hat / (jnp.sqrt(v_hat) + ADAM_EPS) + ADAM_WD * w), nm, nv


def _adamw_all(tot, g_w_in, g_w_out, big, small, grad_x):
    n = len(small)
    rows = WO_ROWS
    steps = D_MODEL // rows

    def body(tot_ref, *refs):
        gx_ref, gx_out = refs[2 + 3 * (2 + n)], refs[-1]
        gx_out[...] = gx_ref[...]
        ins, outs = refs[:2 + 3 * (2 + n)], refs[3 + 3 * (2 + n):-1]
        g_refs, wmv = ins[:2], ins[2:]
        loss_ref, quads = outs[0], outs[1:]

        def update(j, g):
            w_ref, m_ref, v_ref = wmv[3 * j:3 * j + 3]
            g_ref, d_ref, nm_ref, nv_ref = quads[4 * j:4 * j + 4]
            g_ref[...] = g
            d_ref[...], nm_ref[...], nv_ref[...] = _adam_update(w_ref[...], g, m_ref[...], v_ref[...])

        update(0, g_refs[0][...])

        @pl.when(pl.program_id(0) == 0)
        def _():
            update(1, g_refs[1][...])
            k = 2 * lax.axis_index("x") + lax.axis_index("y")
            mine = pl.ds(pl.multiple_of(k * HEAD, HEAD), HEAD)
            loss_ref[...] = tot_ref[7:8, 0:1]
            grads = [tot_ref[0:1, :], tot_ref[1:2, :], tot_ref[2:3, 0:D_HGRN], tot_ref[2:3, D_HGRN:],
                     jnp.concatenate([tot_ref[3:4, 0:D_HGRN], tot_ref[3:4, D_HGRN:]], axis=0),
                     jnp.concatenate([tot_ref[4 + tap:5 + tap, mine] for tap in range(3)], axis=1)]
            for j, g in enumerate(grads):
                update(2 + j, g)

    whole = lambda a: pl.BlockSpec(a.shape, lambda i: (0, 0))
    blk = pl.BlockSpec((rows, SHARD_COLS), lambda i: (i, 0))
    arrays = [a for triple in big + small for a in triple]
    in_specs = ([whole(tot), blk, whole(g_w_out)] + [blk] * 3 + [whole(a) for a in arrays[3:]])
    shapes = [big[0][0], big[1][0]] + [w for w, _, _ in small]
    out_shape = (jax.ShapeDtypeStruct((1, 1), F32),) + tuple(
        jax.ShapeDtypeStruct(w.shape, F32) for w in shapes for _ in range(4))
    out_specs = (pl.BlockSpec((1, 1), lambda i: (0, 0)),) + (blk,) * 4 + tuple(
        whole(w) for w in shapes[1:] for _ in range(4))
    gx_blk = pl.BlockSpec((SEQ // steps, D_MODEL), lambda i: (i, 0))
    outs = pl.pallas_call(
        body, name="adamw_all", grid=(steps,),
        out_shape=out_shape + (jax.ShapeDtypeStruct(grad_x.shape, F32),),
        in_specs=in_specs + [gx_blk], out_specs=out_specs + (gx_blk,),
        compiler_params=pltpu.CompilerParams(dimension_semantics=("arbitrary",), vmem_limit_bytes=VMEM_LIMIT),
    )(tot, g_w_in, g_w_out, *arrays, grad_x)
    return [outs[0]] + [outs[1 + 4 * j:5 + 4 * j] for j in range(2 + n)] + [outs[-1]]


def _local_step(x2d, tgt, proj, lb_logits, cw, ga, gcn, w_out, gf):
    g64 = _group_matrix(HEAD, CONV_GROUP)
    aux, states, dx2, dmixed, gwo, part_out = _mix_out(proj, lb_logits, cw, ga, gcn, g64, w_out, x2d, gf, tgt)
    dproj, part_mix = _mix_bwd(proj, aux, states, dmixed, lb_logits, cw, ga, gcn, g64)
    return dproj, dx2, gwo.reshape(N_SHARD, WO_ROWS, D_MODEL), part_out, part_mix


def kernel(x, norm_gain, w_in, lb_logits, conv_w, hgrn_norm_gain, conv_norm_gain, w_out, final_norm_gain, loss_target, m_norm_gain, m_w_in, m_lb_logits, m_conv_w, m_hgrn_norm_gain, m_conv_norm_gain, m_w_out, m_final_norm_gain, v_norm_gain, v_w_in, v_lb_logits, v_conv_w, v_hgrn_norm_gain, v_conv_norm_gain, v_w_out, v_final_norm_gain):
    k = 2 * lax.axis_index("x") + lax.axis_index("y")
    kidx = jnp.reshape(k, (1,)).astype(jnp.int32)
    row = lambda a: a.reshape(1, D_MODEL)
    taps = lambda a: a.reshape(1, 3 * HEAD)
    h, proj, wg, cw = _gather_proj(kidx, x[0], norm_gain, w_in, taps(conv_w))
    dproj, dx2, gwo, part_out, part_mix = _local_step(
        x[0], loss_target[0], proj, lb_logits, cw, hgrn_norm_gain, conv_norm_gain, w_out, row(final_norm_gain))
    rgrad_x, rg_w_in, rg_w_out, tot = _bwd_tail(kidx, h, dproj, wg, gwo, x[0], dx2, norm_gain, part_out, part_mix)

    (loss, (g_w_in, d_w_in, nm_w_in, nv_w_in), (g_w_out, d_w_out, nm_w_out, nv_w_out),
     (g_norm_gain, d_ng, nm_ng, nv_ng), (g_final, d_fg, nm_fg, nv_fg), (g_hgrn, d_hg, nm_hg, nv_hg),
     (g_convn, d_cg, nm_cg, nv_cg), (g_lb, d_lb, nm_lb, nv_lb), (g_conv_w, d_cw, nm_cw, nv_cw),
     grad_x) = _adamw_all(
        tot, rg_w_in, rg_w_out,
        [(w_in[0], m_w_in[0], v_w_in[0]), (w_out[0], m_w_out[0], v_w_out[0])],
        [(norm_gain, m_norm_gain, v_norm_gain),
         (row(final_norm_gain), row(m_final_norm_gain), row(v_final_norm_gain)),
         (hgrn_norm_gain, m_hgrn_norm_gain, v_hgrn_norm_gain),
         (conv_norm_gain, m_conv_norm_gain, v_conv_norm_gain),
         (lb_logits, m_lb_logits, v_lb_logits),
         (taps(conv_w), taps(m_conv_w), taps(v_conv_w))],
        rgrad_x)
    flat = lambda a: a.reshape(D_MODEL)
    untap = lambda a: a.reshape(1, 3, HEAD)
    return (loss.reshape(()), grad_x[None],
            g_norm_gain, g_w_in[None], g_lb, untap(g_conv_w), g_hgrn, g_convn, g_w_out[None], flat(g_final),
            d_ng, d_w_in[None], d_lb, untap(d_cw), d_hg, d_cg, d_w_out[None], flat(d_fg),
            nm_ng, nm_w_in[None], nm_lb, untap(nm_cw), nm_hg, nm_cg, nm_w_out[None], flat(nm_fg),
            nv_ng, nv_w_in[None], nv_lb, untap(nv_cw), nv_hg, nv_cg, nv_w_out[None], flat(nv_fg))
```

```python
import jax
import jax.numpy as jnp
import numpy as np
from jax import lax
from jax.experimental import pallas as pl
from jax.experimental.pallas import tpu as pltpu

F32 = jnp.float32
BF16 = jnp.bfloat16
MESH = pl.DeviceIdType.MESH

SEQ = 2048
D_MODEL = 1024
D_HGRN = 512
D_CONV = 512
HEAD = 128
N_HEADS = 4
CHUNK = 64
CONV_GROUP = 64
N_SHARD = 4
SHARD_COLS = 1024
WO_ROWS = 256
EPS = 1e-6
TB = 256
NCB = TB // CHUNK
N_CHUNKS = SEQ // CHUNK
N_DEV = 8
COLLECTIVE_GATHER, COLLECTIVE_MIX_OUT, COLLECTIVE_TAIL = 1, 0, 2
AUX_O, AUX_CV, AUX_B, AUX_COLS = 0, 512, 1024, 1536

ADAM_LR = 0.001
ADAM_B1 = 0.9
ADAM_B2 = 0.999
ADAM_EPS = 1e-08
ADAM_WD = 0.01
ADAM_STEP = 10

VMEM_LIMIT = 56 * 1024 * 1024


def _dot(a, b):
    return jnp.dot(a, b, preferred_element_type=F32)


def _dot_nt(a, b):
    return lax.dot_general(a, b, (((1,), (1,)), ((), ())), preferred_element_type=F32)


def _dot_tn(a, b):
    return lax.dot_general(a, b, (((0,), (0,)), ((), ())), preferred_element_type=F32)


def _split_bf16(x, n):
    parts = []
    r = x
    for _ in range(n):
        p = r.astype(BF16)
        parts.append(p)
        r = r - p.astype(F32)
    return parts


def _exact_left(m, x, n=3):
    acc = None
    for p in _split_bf16(x, n):
        t = _dot(m, p)
        acc = t if acc is None else acc + t
    return acc


def _exact_left_many(m, xs, n=3):
    parts = [_split_bf16(x, n) for x in xs]
    accs = [None] * len(xs)
    for i in range(n):
        for j in range(len(xs)):
            t = _dot(m, parts[j][i])
            accs[j] = t if accs[j] is None else accs[j] + t
    return accs


def _group_mean_many(xs, gmat, n=2):
    parts = [_split_bf16(x, n) for x in xs]
    accs = [None] * len(xs)
    for i in range(n):
        for j in range(len(xs)):
            t = _dot(parts[j][i], gmat)
            accs[j] = t if accs[j] is None else accs[j] + t
    return accs


def _group_mean(x, gmat, n=2):
    w = gmat.shape[0]
    outs = []
    for c0 in range(0, x.shape[1], w):
        acc = None
        for p in _split_bf16(x[:, c0:c0 + w], n):
            t = _dot(p, gmat)
            acc = t if acc is None else acc + t
        outs.append(acc)
    return jnp.concatenate(outs, axis=1)


def _sigmoid(x):
    return 1.0 / (1.0 + jnp.exp(-x))


def _lower_bound(lbl):
    l0 = lbl[0:1, :]
    l1 = lbl[1:2, :]
    m = jnp.maximum(l0, l1)
    e0 = jnp.exp(l0 - m)
    e1 = jnp.exp(l1 - m)
    return e0 / (e0 + e1)


def _tri(lower):
    r = lax.broadcasted_iota(jnp.int32, (CHUNK, CHUNK), 0)
    c = lax.broadcasted_iota(jnp.int32, (CHUNK, CHUNK), 1)
    return jnp.where((c <= r) if lower else (c >= r), 1.0, 0.0).astype(BF16)


def _causal():
    r = lax.broadcasted_iota(jnp.int32, (CHUNK, CHUNK), 0)
    c = lax.broadcasted_iota(jnp.int32, (CHUNK, CHUNK), 1)
    return c <= r


def _shift_down(x, sh, prev_tail):
    r = pltpu.roll(x, sh, 0)
    pt = pltpu.roll(prev_tail, sh, 0)
    rows = lax.broadcasted_iota(jnp.int32, prev_tail.shape, 0)
    top = jnp.where(rows < sh, pt, r[0:8])
    return jnp.concatenate([top, r[8:]], axis=0)


def _shift_up(x, sh, next_head):
    n = x.shape[0]
    r = pltpu.roll(x, n - sh, 0)
    nh = pltpu.roll(next_head, 8 - sh, 0)
    rows = lax.broadcasted_iota(jnp.int32, next_head.shape, 0)
    bot = jnp.where(rows >= 8 - sh, nh, r[n - 8:])
    return jnp.concatenate([r[:n - 8], bot], axis=0)


def _group_matrix(width, group):
    r = np.arange(width)[:, None] // group
    c = np.arange(width)[None, :] // group
    return jnp.asarray(np.where(r == c, 1.0 / group, 0.0), dtype=BF16)


TG = 1024
SEM_W, SEM_CW, SEM_W_FWD, N_SEM = 0, 4, 7, 11


def _gather_proj(kidx, x2d, g1, w_in, conv_w):
    half_w = D_MODEL // 2
    half_c = SHARD_COLS // 2
    nt = SEQ // TG
    n_steps = 2 * N_SHARD

    def body(k_ref, x_ref, g_ref, w_ref, cw_ref, h_ref, p_ref, wg_out, cwg_out,
             wg_v, cwg_v, send_sems, recv_sems, out_sems):
        s, t = pl.program_id(0), pl.program_id(1)
        x, y, c = lax.axis_index("x"), lax.axis_index("y"), lax.axis_index("c")
        k = 2 * x + y
        sibling = (x, y, 1 - c)
        chips = [(1 - x, y), (x, 1 - y), (1 - x, 1 - y)]
        kjs = [2 * cx + cy for cx, cy in chips]
        diag = (*chips[2], c)

        def w_half(kk, cc):
            return wg_v.at[kk, pl.ds(cc * half_w, half_w), :]

        def w_quarter(kk, cc, piece):
            return wg_v.at[kk, pl.ds(cc * half_w, half_w), piece * half_c:(piece + 1) * half_c]

        def cw_of(kk):
            return cwg_v.at[:, pl.ds(pl.multiple_of(kk * HEAD, HEAD), HEAD)]

        def copy(sem, ref, to):
            return pltpu.make_async_remote_copy(
                src_ref=ref, dst_ref=ref, send_sem=send_sems.at[sem], recv_sem=recv_sems.at[sem],
                device_id=to, device_id_type=MESH)

        def at_step(sv, tv):
            return pl.when((s == sv) & (t == tv))

        w_direct = ([copy(SEM_W + j, w_half(k, c), (*chips[j], c)) for j in range(2)]
                    + [copy(SEM_W + 2 + p, w_quarter(k, c, p), diag) for p in range(2)])
        cw_direct = [copy(SEM_CW + j, cw_of(k), (*chip, c)) for j, chip in enumerate(chips)]
        w_passed = ([copy(SEM_W_FWD + j, w_half(kjs[j], c), sibling) for j in range(2)]
                    + [copy(SEM_W_FWD + 2 + p, w_quarter(kjs[2], c, p), sibling) for p in range(2)])
        stores = ([pltpu.make_async_copy(wg_v.at[kk], wg_out.at[kk], out_sems.at[i])
                   for i, kk in enumerate([k] + kjs)]
                  + [pltpu.make_async_copy(cwg_v, cwg_out, out_sems.at[4])])

        @at_step(0, 0)
        def _():
            barrier = pltpu.get_barrier_semaphore()
            for peer in [sibling] + [(*chip, c) for chip in chips]:
                pl.semaphore_signal(barrier, inc=1, device_id=peer, device_id_type=MESH)
            wg_v[k] = w_ref[0].astype(BF16)
            mine = pl.ds(pl.multiple_of(k * HEAD, HEAD), HEAD)
            cwg_v[:, mine] = jnp.zeros((8, HEAD), F32)
            for tap in range(3):
                cwg_v[tap:tap + 1, mine] = cw_ref[:, tap * HEAD:(tap + 1) * HEAD]
            pl.semaphore_wait(barrier, 4)
            for cp in w_direct + cw_direct:
                cp.start()
            stores[0].start()

        @at_step(2, 0)
        def _():
            for j in range(2):
                copy(SEM_W + j, w_half(kjs[j], c), sibling).wait_recv()
                w_passed[j].start()

        @at_step(4, 0)
        def _():
            copy(SEM_W_FWD + 1, w_half(kjs[1], 1 - c), sibling).wait_recv()
            stores[2].start()

        for p in range(2):
            @at_step(6 + p, 0)
            def _(p=p):
                copy(SEM_W + 2 + p, w_quarter(kjs[2], c, p), sibling).wait_recv()
                w_passed[2 + p].start()

        rows = pl.ds(pl.multiple_of(t * TG, TG), TG)

        @pl.when(s == 0)
        def _():
            xv = x_ref[...]
            r = lax.rsqrt(jnp.mean(xv * xv, axis=-1, keepdims=True) + EPS)
            h_ref[rows, :] = (xv * r * g_ref[...]).astype(BF16)

        sh = s >> 1
        js = k ^ (((sh & 1) << 1) | (sh >> 1))
        landing = [(2, SEM_W_FWD, lambda cc: w_half(kjs[0], cc)),
                   (6, SEM_W_FWD + 2, lambda cc: w_quarter(kjs[2], cc, 0)),
                   (7, SEM_W_FWD + 3, lambda cc: w_quarter(kjs[2], cc, 1))]
        for sv, sem, part in landing:
            for cc in range(2):
                @pl.when((s == sv) & (t == 0) & (c == cc))
                def _(sv=sv, sem=sem, part=part, cc=cc):
                    cols = slice((sv & 1) * half_c, ((sv & 1) + 1) * half_c)
                    mine, other = slice(cc * half_w, (cc + 1) * half_w), slice((1 - cc) * half_w, (2 - cc) * half_w)
                    p_ref[...] = _dot(h_ref[rows, mine], wg_v[js, mine, cols])
                    copy(sem, part(1 - cc), sibling).wait_recv()
                    p_ref[...] += _dot(h_ref[rows, other], wg_v[js, other, cols])

        @at_step(2, 0)
        def _():
            stores[1].start()

        is_landing = (t == 0) & ((s == 2) | (s == 6) | (s == 7))
        for piece in range(2):
            @pl.when(((s & 1) == piece) & jnp.logical_not(is_landing))
            def _(piece=piece):
                p_ref[...] = _dot(h_ref[rows, :], wg_v[js, :, piece * half_c:(piece + 1) * half_c])

        @at_step(n_steps - 1, nt - 1)
        def _():
            stores[3].start()
            for j in range(3):
                copy(SEM_CW + j, cw_of(kjs[j]), sibling).wait_recv()
            stores[4].start()
            for cp in w_direct + cw_direct + w_passed:
                cp.wait_send()
            for st in stores:
                st.wait()

    def x_map(s, t, kr):
        return (jnp.where(s == 0, t, nt - 1), 0)

    def p_map(s, t, kr):
        sh = s >> 1
        return (t, 2 * (kr[0] ^ (((sh & 1) << 1) | (sh >> 1))) + (s & 1))

    hbm = pl.BlockSpec(memory_space=pl.ANY)
    grid_spec = pltpu.PrefetchScalarGridSpec(
        num_scalar_prefetch=1, grid=(n_steps, nt),
        in_specs=[pl.BlockSpec((TG, D_MODEL), x_map),
                  pl.BlockSpec((1, D_MODEL), lambda s, t, kr: (0, 0)),
                  pl.BlockSpec((1, D_MODEL, SHARD_COLS), lambda s, t, kr: (0, 0, 0)),
                  pl.BlockSpec((1, 3 * HEAD), lambda s, t, kr: (0, 0))],
        out_specs=(pl.BlockSpec((SEQ, D_MODEL), lambda s, t, kr: (0, 0)),
                   pl.BlockSpec((TG, half_c), p_map), hbm, hbm),
        scratch_shapes=[pltpu.VMEM((N_SHARD, D_MODEL, SHARD_COLS), BF16),
                        pltpu.VMEM((8, D_CONV), F32),
                        pltpu.SemaphoreType.DMA((N_SEM,)), pltpu.SemaphoreType.DMA((N_SEM,)),
                        pltpu.SemaphoreType.DMA((5,))])
    return pl.pallas_call(
        body, name="gather_proj", grid_spec=grid_spec,
        out_shape=(jax.ShapeDtypeStruct((SEQ, D_MODEL), BF16),
                   jax.ShapeDtypeStruct((SEQ, N_SHARD * SHARD_COLS), F32),
                   jax.ShapeDtypeStruct((N_SHARD, D_MODEL, SHARD_COLS), BF16),
                   jax.ShapeDtypeStruct((8, D_CONV), F32)),
        compiler_params=pltpu.CompilerParams(dimension_semantics=("arbitrary", "arbitrary"),
                                             vmem_limit_bytes=VMEM_LIMIT, collective_id=COLLECTIVE_GATHER),
    )(kidx, x2d, g1, w_in, conv_w)


LAG = 6


def _mix_out(proj, lb_logits, cw, ga, gcn, g64, w_out, x2d, gf, tgt):
    half_o = WO_ROWS // 2
    nblk = SEQ // TB
    n_steps = nblk + LAG

    def body(p_ref, lbl_ref, cw_ref, ga_ref, gcn_ref, g64_ref, wo_ref, x_ref, gf_ref, t_ref,
             aux_ref, sto_ref, dx2_ref, dm_ref, gwo_ref, part_ref,
             st_ref, tail_ref, wog_v, stage, ring, acc_ref, send_sems, recv_sems):
        i = pl.program_id(0)
        x, y, c = lax.axis_index("x"), lax.axis_index("y"), lax.axis_index("c")
        k = 2 * x + y
        sibling = (x, y, 1 - c)
        chips = [(1 - x, y), (x, 1 - y), (1 - x, 1 - y)]
        kjs = [2 * cx + cy for cx, cy in chips]

        def wo_half(kk, cc):
            return wog_v.at[pl.ds(pl.multiple_of(kk * WO_ROWS + cc * half_o, half_o), half_o), :]

        def copy(sem, ref, to):
            return pltpu.make_async_remote_copy(
                src_ref=ref, dst_ref=ref, send_sem=send_sems.at[sem], recv_sem=recv_sems.at[sem],
                device_id=to, device_id_type=MESH)

        wo_direct = [copy(j, wo_half(k, c), (*chip, c)) for j, chip in enumerate(chips)]
        wo_passed = [copy(3 + j, wo_half(kj, c), sibling) for j, kj in enumerate(kjs)]

        @pl.when(i == 0)
        def _():
            barrier = pltpu.get_barrier_semaphore()
            for peer in [sibling] + [(*chip, c) for chip in chips]:
                pl.semaphore_signal(barrier, inc=1, device_id=peer, device_id_type=MESH)
            st_ref[...] = jnp.zeros_like(st_ref)
            tail_ref[...] = jnp.zeros_like(tail_ref)
            acc_ref[...] = jnp.zeros_like(acc_ref)
            part_ref[...] = jnp.zeros_like(part_ref)
            wog_v[pl.ds(pl.multiple_of(k * WO_ROWS, WO_ROWS), WO_ROWS), :] = wo_ref[0].astype(BF16)
            pl.semaphore_wait(barrier, 4)
            for cp in wo_direct:
                cp.start()

        @pl.when(i == LAG - 1)
        def _():
            for j in range(3):
                copy(j, wo_half(kjs[j], c), sibling).wait_recv()
                wo_passed[j].start()

        @pl.when(i == LAG)
        def _():
            for j in range(3):
                copy(3 + j, wo_half(kjs[j], 1 - c), sibling).wait_recv()

        lb = _lower_bound(lbl_ref[...])
        tri = _tri(True)
        causal = _causal()
        g64m = g64_ref[...]
        heads = range(N_HEADS)
        cs = [slice(hd * HEAD, (hd + 1) * HEAD) for hd in heads]
        col = lambda base, hd: slice(base + hd * HEAD, base + (hd + 1) * HEAD)

        def mix_chunk(n):
            sl = pl.ds(n * CHUNK, CHUNK)
            sg = [_sigmoid(p_ref[sl, col(512, hd)]) for hd in heads]
            f = [lb[:, cs[hd]] + (1.0 - lb[:, cs[hd]]) * sg[hd] for hd in heads]
            bc = _exact_left_many(tri, [jnp.log(f[hd]) for hd in heads])
            for hd in heads:
                aux_ref[sl, col(AUX_B, hd)] = bc[hd]
            g = [bc[hd][CHUNK - 1:CHUNK, :] for hd in heads]
            qd = [(p_ref[sl, col(0, hd)] * jnp.exp(bc[hd])).astype(BF16) for hd in heads]
            kk = [1.0 - f[hd] for hd in heads]
            ki = [(kk[hd] * jnp.exp(-bc[hd])).astype(BF16) for hd in heads]
            ke = [(kk[hd] * jnp.exp(g[hd] - bc[hd])).astype(BF16) for hd in heads]
            vb = [p_ref[sl, col(1024, hd)].astype(BF16) for hd in heads]
            st = [st_ref[hd] for hd in heads]
            st_b = [a.astype(BF16) for a in st]
            for hd in heads:
                sto_ref[n, hd] = st_b[hd]
            scm = [_dot_nt(qd[hd], ki[hd]) for hd in heads]
            inter = [_dot_nt(qd[hd], st_b[hd]) for hd in heads]
            upd = [_dot_tn(vb[hd], ke[hd]) for hd in heads]
            intra = [_dot(jnp.where(causal, scm[hd], 0.0).astype(BF16), vb[hd]) for hd in heads]
            for hd in heads:
                st_ref[hd] = st[hd] * jnp.exp(g[hd]) + upd[hd]
                o = intra[hd] + inter[hd]
                aux_ref[sl, col(AUX_O, hd)] = o
                ra = lax.rsqrt(jnp.mean(o * o, axis=-1, keepdims=True) + EPS)
                za = p_ref[sl, col(1536, hd)]
                stage[sl, cs[hd]] = (o * ra * ga_ref[:, cs[hd]] * (za * _sigmoid(za))).astype(BF16)
            yb = []
            for hd in heads:
                cu = p_ref[sl, col(3072, hd)] * p_ref[sl, col(2048, hd)]
                tail = tail_ref[:, cs[hd]]
                cv = (cw_ref[0:1, cs[hd]] * _shift_down(cu, 2, tail) + cw_ref[1:2, cs[hd]] * _shift_down(cu, 1, tail)
                      + cw_ref[2:3, cs[hd]] * cu)
                tail_ref[:, cs[hd]] = cu[CHUNK - 8:, :]
                aux_ref[sl, col(AUX_CV, hd)] = cv
                yb.append(p_ref[sl, col(2560, hd)] * cv)
            ms = _group_mean_many([y * y for y in yb], g64m)
            for hd in heads:
                rb = lax.rsqrt(ms[hd] + EPS)
                zb = p_ref[sl, col(3584, hd)]
                stage[sl, col(512, hd)] = (yb[hd] * rb * gcn_ref[:, cs[hd]] * (zb * _sigmoid(zb))).astype(BF16)

        def step(mix, project):
            if project:
                mixed_b = ring[pl.ds(pl.multiple_of((i - LAG) * TB, TB), TB), :]
                y = _dot(mixed_b, wog_v[...])
            if mix:
                mix_chunk(0)
            if project:
                x2 = x_ref[...] + y
                r2 = lax.rsqrt(jnp.mean(x2 * x2, axis=-1, keepdims=True) + EPS)
                n2 = x2 * r2
                gfv = gf_ref[...]
                err = n2 * gfv - t_ref[...]
                loss = 0.5 * jnp.sum(jnp.mean(err * err, axis=-1, keepdims=True), axis=0, keepdims=True)
                dy = err * (1.0 / D_MODEL)
                part_ref[1:2, :] += jnp.sum(dy * n2, axis=0, keepdims=True)
                part_ref[7:8, :] += jnp.broadcast_to(loss, (1, D_MODEL))
                dn = dy * gfv
                dx2 = r2 * (dn - n2 * jnp.mean(dn * n2, axis=-1, keepdims=True))
                dx2_ref[...] = dx2
                dx2_b = dx2.astype(BF16)
            if mix:
                mix_chunk(1)
            if project:
                dm_ref[...] = _dot_nt(dx2_b, wog_v[...])
            if mix:
                mix_chunk(2)
            if project:
                acc_ref[...] += _dot_tn(mixed_b, dx2_b)
            if mix:
                mix_chunk(3)
                ring[pl.ds(pl.multiple_of(i * TB, TB), TB), :] = stage[...]

        @pl.when(i < LAG)
        def _():
            step(True, False)

        @pl.when((i >= LAG) & (i < nblk))
        def _():
            step(True, True)

        @pl.when(i >= nblk)
        def _():
            step(False, True)

        @pl.when(i == n_steps - 1)
        def _():
            gwo_ref[...] = acc_ref[...].astype(BF16)
            for cp in wo_direct + wo_passed:
                cp.wait_send()

    assert NCB == 4
    row = lambda w: pl.BlockSpec((1, w), lambda i: (0, 0))
    mix_blk = lambda i: jnp.minimum(i, nblk - 1)
    out_blk = lambda i: jnp.clip(i - LAG, 0, nblk - 1)
    tok = lambda: pl.BlockSpec((TB, D_MODEL), lambda i: (out_blk(i), 0))
    return pl.pallas_call(
        body, name="mix_out", grid=(n_steps,),
        out_shape=(jax.ShapeDtypeStruct((SEQ, AUX_COLS), F32),
                   jax.ShapeDtypeStruct((N_CHUNKS, N_HEADS, HEAD, HEAD), BF16),
                   jax.ShapeDtypeStruct((SEQ, D_MODEL), F32),
                   jax.ShapeDtypeStruct((SEQ, D_MODEL), F32),
                   jax.ShapeDtypeStruct((D_MODEL, D_MODEL), BF16),
                   jax.ShapeDtypeStruct((8, D_MODEL), F32)),
        in_specs=[pl.BlockSpec((TB, 4096), lambda i: (jnp.minimum(i, nblk - 1), 0)),
                  pl.BlockSpec((2, D_HGRN), lambda i: (0, 0)),
                  pl.BlockSpec((8, D_CONV), lambda i: (0, 0)),
                  row(D_HGRN), row(D_CONV),
                  pl.BlockSpec((HEAD, HEAD), lambda i: (0, 0)),
                  pl.BlockSpec((1, WO_ROWS, D_MODEL), lambda i: (0, 0, 0)),
                  tok(), row(D_MODEL), tok()],
        out_specs=(pl.BlockSpec((TB, AUX_COLS), lambda i: (mix_blk(i), 0)),
                   pl.BlockSpec((NCB, N_HEADS, HEAD, HEAD), lambda i: (mix_blk(i), 0, 0, 0)),
                   tok(), tok(),
                   pl.BlockSpec((D_MODEL, D_MODEL), lambda i: (0, 0)),
                   pl.BlockSpec((8, D_MODEL), lambda i: (0, 0))),
        scratch_shapes=[pltpu.VMEM((N_HEADS, HEAD, HEAD), F32), pltpu.VMEM((8, D_CONV), F32),
                        pltpu.VMEM((D_MODEL, D_MODEL), BF16), pltpu.VMEM((TB, D_MODEL), BF16),
                        pltpu.VMEM((SEQ, D_MODEL), BF16), pltpu.VMEM((D_MODEL, D_MODEL), F32),
                        pltpu.SemaphoreType.DMA((6,)), pltpu.SemaphoreType.DMA((6,))],
        compiler_params=pltpu.CompilerParams(dimension_semantics=("arbitrary",), vmem_limit_bytes=VMEM_LIMIT,
                                             collective_id=COLLECTIVE_MIX_OUT),
    )(proj, lb_logits, cw, ga, gcn, g64, w_out, x2d, gf, tgt)


def _mix_bwd(proj, aux, states, dmixed, lb_logits, cw, ga, gcn, g64):
    nblk = SEQ // TB

    def body(p_ref, aux_ref, st_ref, dm_ref, lbl_ref, cw_ref, ga_ref, gcn_ref, g64_ref,
             dp_ref, part_ref, dst_ref, head_ref, dlb_ref):
        i = pl.program_id(0)

        @pl.when(i == 0)
        def _():
            dst_ref[...] = jnp.zeros_like(dst_ref)
            head_ref[...] = jnp.zeros_like(head_ref)
            part_ref[...] = jnp.zeros_like(part_ref)
            dlb_ref[...] = jnp.zeros_like(dlb_ref)

        lb = _lower_bound(lbl_ref[...])
        triu = _tri(False)
        causal = _causal()
        g64m = g64_ref[...]
        rowsum = lambda a: jnp.sum(a, axis=0, keepdims=True)
        heads = range(N_HEADS)
        cs = [slice(hd * HEAD, (hd + 1) * HEAD) for hd in heads]
        col = lambda base, hd: slice(base + hd * HEAD, base + (hd + 1) * HEAD)
        for n in reversed(range(NCB)):
            sl = pl.ds(n * CHUNK, CHUNK)
            cvv = [aux_ref[sl, col(AUX_CV, hd)] for hd in heads]
            gb = [p_ref[sl, col(2560, hd)] for hd in heads]
            yb = [gb[hd] * cvv[hd] for hd in heads]
            ms = _group_mean_many([y * y for y in yb], g64m)
            rb, nb, dnb = [], [], []
            for hd in heads:
                rb.append(lax.rsqrt(ms[hd] + EPS))
                nb.append(yb[hd] * rb[hd])
                zb = p_ref[sl, col(3584, hd)]
                sgb = _sigmoid(zb)
                dmb = dm_ref[sl, col(512, hd)]
                silu = zb * sgb
                dgate = dmb * gcn_ref[:, cs[hd]]
                part_ref[2:3, col(512, hd)] += rowsum(dmb * nb[hd] * silu)
                dp_ref[sl, col(3584, hd)] = (dgate * nb[hd] * (sgb + silu * (1.0 - sgb))).astype(BF16)
                dnb.append(dgate * silu)
            mdn = _group_mean_many([dnb[hd] * nb[hd] for hd in heads], g64m)
            for hd in heads:
                dyb = rb[hd] * (dnb[hd] - nb[hd] * mdn[hd])
                dp_ref[sl, col(2560, hd)] = (dyb * cvv[hd]).astype(BF16)
                dcv = dyb * gb[hd]
                head = head_ref[:, cs[hd]]
                dcv1 = _shift_up(dcv, 1, head)
                dcv2 = _shift_up(dcv, 2, head)
                head_ref[:, cs[hd]] = dcv[0:8, :]
                u = p_ref[sl, col(2048, hd)]
                gc = p_ref[sl, col(3072, hd)]
                cu = gc * u
                part_ref[4:5, cs[hd]] += rowsum(dcv2 * cu)
                part_ref[5:6, cs[hd]] += rowsum(dcv1 * cu)
                part_ref[6:7, cs[hd]] += rowsum(dcv * cu)
                dcu = cw_ref[2:3, cs[hd]] * dcv + cw_ref[1:2, cs[hd]] * dcv1 + cw_ref[0:1, cs[hd]] * dcv2
                dp_ref[sl, col(3072, hd)] = (dcu * u).astype(BF16)
                dp_ref[sl, col(2048, hd)] = (dcu * gc).astype(BF16)
            do_b = []
            for hd in heads:
                ov = aux_ref[sl, col(AUX_O, hd)]
                ra = lax.rsqrt(jnp.mean(ov * ov, axis=-1, keepdims=True) + EPS)
                na = ov * ra
                za = p_ref[sl, col(1536, hd)]
                sga = _sigmoid(za)
                dma = dm_ref[sl, cs[hd]]
                silu = za * sga
                dgate = dma * ga_ref[:, cs[hd]]
                part_ref[2:3, cs[hd]] += rowsum(dma * na * silu)
                dp_ref[sl, col(1536, hd)] = (dgate * na * (sga + silu * (1.0 - sga))).astype(BF16)
                dna = dgate * silu
                do_b.append((ra * (dna - na * jnp.mean(dna * na, axis=-1, keepdims=True))).astype(BF16))
            s = [_sigmoid(p_ref[sl, col(512, hd)]) for hd in heads]
            f = [lb[:, cs[hd]] + (1.0 - lb[:, cs[hd]]) * s[hd] for hd in heads]
            bc = [aux_ref[sl, col(AUX_B, hd)] for hd in heads]
            g = [bc[hd][CHUNK - 1:CHUNK, :] for hd in heads]
            eb = [jnp.exp(bc[hd]) for hd in heads]
            enb = [jnp.exp(-bc[hd]) for hd in heads]
            eg = [jnp.exp(g[hd] - bc[hd]) for hd in heads]
            dec = [jnp.exp(g[hd]) for hd in heads]
            qd = [p_ref[sl, cs[hd]] * eb[hd] for hd in heads]
            kk = [1.0 - f[hd] for hd in heads]
            ki = [kk[hd] * enb[hd] for hd in heads]
            ke = [kk[hd] * eg[hd] for hd in heads]
            qd_b = [a.astype(BF16) for a in qd]
            ki_b = [a.astype(BF16) for a in ki]
            ke_b = [a.astype(BF16) for a in ke]
            vb = [p_ref[sl, col(1024, hd)].astype(BF16) for hd in heads]
            st_b = [st_ref[n, hd] for hd in heads]
            dst = [dst_ref[hd] for hd in heads]
            dst_b = [a.astype(BF16) for a in dst]
            scm = [_dot_nt(qd_b[hd], ki_b[hd]) for hd in heads]
            amm = [_dot_nt(do_b[hd], vb[hd]) for hd in heads]
            dqd2 = [_dot(do_b[hd], st_b[hd]) for hd in heads]
            dke = [_dot(vb[hd], dst_b[hd]) for hd in heads]
            dv2 = [_dot_nt(ke_b[hd], dst_b[hd]) for hd in heads]
            dsu = [_dot_tn(do_b[hd], qd_b[hd]) for hd in heads]
            sc = [jnp.where(causal, scm[hd], 0.0).astype(BF16) for hd in heads]
            am = [jnp.where(causal, amm[hd], 0.0).astype(BF16) for hd in heads]
            dqd1 = [_dot(am[hd], ki_b[hd]) for hd in heads]
            dki = [_dot_tn(am[hd], qd_b[hd]) for hd in heads]
            dv1 = [_dot_tn(sc[hd], do_b[hd]) for hd in heads]
            db, dgv, dkk = [], [], []
            for hd in heads:
                dqd = dqd1[hd] + dqd2[hd]
                ddec = rowsum(dst[hd] * st_b[hd].astype(F32))
                dst_ref[hd] = dst[hd] * dec[hd] + dsu[hd]
                dp_ref[sl, cs[hd]] = (dqd * eb[hd]).astype(BF16)
                dp_ref[sl, col(1024, hd)] = (dv1[hd] + dv2[hd]).astype(BF16)
                dke_eg = dke[hd] * eg[hd]
                dkk.append(dki[hd] * enb[hd] + dke_eg)
                db.append(dqd * qd[hd] - kk[hd] * dkk[hd])
                dgv.append(rowsum(kk[hd] * dke_eg) + ddec * dec[hd])
            rc = _exact_left_many(triu, db, 2)
            for hd in heads:
                df = (rc[hd] + dgv[hd]) / f[hd] - dkk[hd]
                one_s = 1.0 - s[hd]
                dlb_ref[:, cs[hd]] += rowsum(df * one_s)
                dp_ref[sl, col(512, hd)] = (df * (1.0 - lb[:, cs[hd]]) * s[hd] * one_s).astype(BF16)

        @pl.when(i == nblk - 1)
        def _():
            row = dlb_ref[...] * lb * (1.0 - lb)
            part_ref[3:4, 0:D_HGRN] = row
            part_ref[3:4, D_HGRN:] = -row

    rev = lambda w: pl.BlockSpec((TB, w), lambda i: (nblk - 1 - i, 0))
    row = lambda w: pl.BlockSpec((1, w), lambda i: (0, 0))
    return pl.pallas_call(
        body, name="mix_bwd", grid=(nblk,),
        out_shape=(jax.ShapeDtypeStruct((SEQ, 4096), BF16),
                   jax.ShapeDtypeStruct((8, D_MODEL), F32)),
        in_specs=[rev(4096), rev(AUX_COLS),
                  pl.BlockSpec((NCB, N_HEADS, HEAD, HEAD), lambda i: (nblk - 1 - i, 0, 0, 0)),
                  rev(D_MODEL),
                  pl.BlockSpec((2, D_HGRN), lambda i: (0, 0)),
                  pl.BlockSpec((8, D_CONV), lambda i: (0, 0)),
                  row(D_HGRN), row(D_CONV),
                  pl.BlockSpec((HEAD, HEAD), lambda i: (0, 0))],
        out_specs=(rev(4096), pl.BlockSpec((8, D_MODEL), lambda i: (0, 0))),
        scratch_shapes=[pltpu.VMEM((N_HEADS, HEAD, HEAD), F32), pltpu.VMEM((8, D_CONV), F32),
                        pltpu.VMEM((1, D_HGRN), F32)],
        compiler_params=pltpu.CompilerParams(dimension_semantics=("arbitrary",), vmem_limit_bytes=VMEM_LIMIT),
    )(proj, aux, states, dmixed, lb_logits, cw, ga, gcn, g64)


TT = 1024
TX = 512
(SEM_D2D, SEM_D2D_O, SEM_ICI, SEM_ICI_O, SEM_FIN, SEM_FIN_O, SEM_SMALL, SEM_VIA, N_SEM_TAIL) = (
    0, 4, 5, 8, 11, 12, 12, 20, 22)


def _bwd_tail(kidx, h, dproj, wg, gwo, x2d, dx2, g1, small_a, small_b):
    hw = D_MODEL // 2
    ho = WO_ROWS // 2
    nt = SEQ // TT
    norm_step = 2 * N_SHARD
    n_steps = norm_step + SEQ // TX // nt

    def body(k_ref, h_ref, dp_ref, w_ref, gwo_ref, x_ref, dx2_ref, g_ref, sm_ref, smb_ref,
             gx_ref, gw_out, gwo_out, osm_ref,
             acc, dh, sendbuf, keep, sibrcv, rcv, merge, sib_o, p_o, rcv_o, res_o, sm_buf, dng,
             send_sems, recv_sems, out_sems):
        s, t = pl.program_id(0), pl.program_id(1)
        x, y, c = lax.axis_index("x"), lax.axis_index("y"), lax.axis_index("c")
        k = 2 * x + y
        me = 4 * x + 2 * y + c
        sibling = (x, y, 1 - c)
        chips = [(1 - x, 1 - y), (1 - x, y), (x, 1 - y)]
        kjs = [2 * cx + cy for cx, cy in chips]
        mine = pl.ds(pl.multiple_of(c * hw, hw), hw)
        other = pl.ds(pl.multiple_of((1 - c) * hw, hw), hw)
        mine_o = pl.ds(pl.multiple_of(c * ho, ho), ho)
        other_o = pl.ds(pl.multiple_of((1 - c) * ho, ho), ho)

        def copy(sem, src, dst, to):
            return pltpu.make_async_remote_copy(
                src_ref=src, dst_ref=dst, send_sem=send_sems.at[sem], recv_sem=recv_sems.at[sem],
                device_id=to, device_id_type=MESH)

        def at_step(sv, tv):
            return pl.when((s == sv) & (t == tv))

        def at_norm_block(b):
            return at_step(norm_step + b // nt, b % nt)

        d2d = [copy(SEM_D2D + sv, sendbuf.at[sv], sibrcv.at[sv], sibling) for sv in range(N_SHARD)]
        d2d_o = copy(SEM_D2D_O, gwo_ref.at[:, other_o, :], sib_o, sibling)
        ici = {sv: copy(SEM_ICI + sv, keep.at[sv], rcv.at[sv - 1], (*chips[sv], c)) for sv in (1, 2)}
        qh = hw // 2
        via = [copy(SEM_VIA, keep.at[0, 0:qh, :], merge.at[1], (*chips[1], c)),
               copy(SEM_VIA + 1, keep.at[0, qh:hw, :], merge.at[0], (*chips[2], c))]
        merged_rows = [slice(qh, hw), slice(0, qh)]
        ici_o = [copy(SEM_ICI_O + sv, p_o.at[kjs[sv]], rcv_o.at[sv], (*chips[sv], c)) for sv in range(3)]
        fin = copy(SEM_FIN, acc.at[mine, :], gw_out.at[mine, :], sibling)
        fin_o = copy(SEM_FIN_O, res_o.at[mine_o, :], res_o.at[mine_o, :], sibling)
        smalls = [copy(SEM_SMALL + m, sm_buf.at[me], sm_buf.at[me],
                       (x ^ (m >> 2), y ^ ((m >> 1) & 1), c ^ (m & 1))) for m in range(1, N_DEV)]
        store_w = pltpu.make_async_copy(acc.at[mine, :], gw_out.at[mine, :], out_sems.at[0])
        store_o = pltpu.make_async_copy(res_o, gwo_out, out_sems.at[1])

        @at_step(0, 0)
        def _():
            barrier = pltpu.get_barrier_semaphore()
            for m in range(1, N_DEV):
                pl.semaphore_signal(barrier, inc=1, device_id=(x ^ (m >> 2), y ^ ((m >> 1) & 1), c ^ (m & 1)),
                                    device_id_type=MESH)
            pl.semaphore_wait(barrier, N_DEV - 1)
            d2d_o.start()

        @at_step(0, 1)
        def _():
            d2d_o.wait_recv()
            for j in range(N_SHARD):
                p_o[j] = (gwo_ref[j, mine_o, :].astype(F32) + sib_o[j].astype(F32)).astype(BF16)
            res_o[mine_o, :] = gwo_ref[k, mine_o, :].astype(F32) + sib_o[k].astype(F32)
            for cp in ici_o:
                cp.start()

        rows = pl.ds(pl.multiple_of(t * TT, TT), TT)

        @pl.when((s < N_SHARD) & (t == 0))
        def _():
            acc[...] = _dot_tn(h_ref[...], dp_ref[...])

        @pl.when((s < N_SHARD) & (t > 0))
        def _():
            acc[...] += _dot_tn(h_ref[...], dp_ref[...])

        for sv in range(N_SHARD):
            @at_step(sv, nt - 1)
            def _(sv=sv):
                sendbuf[sv] = acc[other, :].astype(BF16)
                if sv < 3:
                    keep[sv] = acc[mine, :].astype(BF16)
                d2d[sv].start()

        @at_step(1, 0)
        def _():
            d2d[0].wait_recv()
            keep[0] = (keep[0].astype(F32) + sibrcv[0].astype(F32)).astype(BF16)
            for cp in via:
                cp.start()

        for sv in (1, 2):
            @at_step(sv + 2, 0)
            def _(sv=sv):
                d2d[sv].wait_recv()
                keep[sv] = (keep[sv].astype(F32) + sibrcv[sv].astype(F32)).astype(BF16)
                via[2 - sv].wait_recv()
                rows_m = merged_rows[sv - 1]
                keep[sv, rows_m, :] = (keep[sv, rows_m, :].astype(F32) + merge[sv - 1].astype(F32)).astype(BF16)
                ici[sv].start()

        @pl.when(s == N_SHARD)
        def _():
            dh[rows, :] = _dot_nt(dp_ref[...], w_ref[0])

        @pl.when((s > N_SHARD) & (s < norm_step))
        def _():
            dh[rows, :] += _dot_nt(dp_ref[...], w_ref[0])

        @at_norm_block(0)
        def _():
            d2d[3].wait_recv()
            acc[mine, :] += sibrcv[3].astype(F32)

        @at_norm_block(1)
        def _():
            tot = res_o[mine_o, :]
            for sv in range(3):
                ici_o[sv].wait_recv()
                tot = tot + rcv_o[sv].astype(F32)
            res_o[mine_o, :] = tot
            fin_o.start()

        @at_norm_block(2)
        def _():
            ici[1].wait_recv()
            acc[mine, :] += rcv[0].astype(F32)

        @at_norm_block(SEQ // TX - 2)
        def _():
            ici[2].wait_recv()
            acc[mine, :] += rcv[1].astype(F32)
            fin.start()
            store_w.start()
            fin_o.wait_recv()
            store_o.start()

        @at_norm_block(0)
        def _():
            dng[...] = jnp.zeros_like(dng)

        @pl.when(s >= norm_step)
        def _():
            blk = (s - norm_step) * nt + t
            dhv = dh[pl.ds(pl.multiple_of(blk * TX, TX), TX), :]
            xv = x_ref[...]
            r = lax.rsqrt(jnp.mean(xv * xv, axis=-1, keepdims=True) + EPS)
            xn = xv * r
            dng[...] += jnp.sum(dhv * xn, axis=0, keepdims=True)
            dxn = dhv * g_ref[...]
            gx_ref[...] = dx2_ref[...] + r * (dxn - xn * jnp.mean(dxn * xn, axis=-1, keepdims=True))

        @at_step(n_steps - 1, nt - 1)
        def _():
            sm_buf[me] = sm_ref[...] + smb_ref[...]
            sm_buf[me, 0:1, :] = dng[...]
            for cp in smalls:
                cp.start()
            for m in range(1, N_DEV):
                copy(SEM_SMALL + m, sm_buf.at[0], sm_buf.at[0], sibling).wait_recv()
            tot = sm_buf[0]
            for d in range(1, N_DEV):
                tot = tot + sm_buf[d]
            osm_ref[...] = tot
            fin.wait_recv()
            for cp in d2d + [d2d_o] + via + list(ici.values()) + ici_o + [fin, fin_o] + smalls:
                cp.wait_send()
            store_o.wait()
            store_w.wait()

    def shard_of(s, kr):
        order = jnp.where(s < N_SHARD, s, jnp.where(s < norm_step, s - N_SHARD, 3))
        return kr[0] ^ (3 - order)

    def h_map(s, t, kr):
        return (jnp.where(s < N_SHARD, t, nt - 1), 0)

    def dp_map(s, t, kr):
        return (jnp.where(s < norm_step, t, nt - 1), shard_of(s, kr))

    def w_map(s, t, kr):
        return (shard_of(jnp.maximum(s, N_SHARD), kr), 0, 0)

    def blk_map(s, t, kr):
        return (jnp.where(s < norm_step, 0, (s - norm_step) * nt + t), 0)

    hbm = pl.BlockSpec(memory_space=pl.ANY)
    grid_spec = pltpu.PrefetchScalarGridSpec(
        num_scalar_prefetch=1, grid=(n_steps, nt),
        in_specs=[pl.BlockSpec((TT, D_MODEL), h_map),
                  pl.BlockSpec((TT, SHARD_COLS), dp_map),
                  pl.BlockSpec((1, D_MODEL, SHARD_COLS), w_map),
                  pl.BlockSpec((N_SHARD, WO_ROWS, D_MODEL), lambda s, t, kr: (0, 0, 0),
                               pipeline_mode=pl.Buffered(1)),
                  pl.BlockSpec((TX, D_MODEL), blk_map),
                  pl.BlockSpec((TX, D_MODEL), blk_map),
                  pl.BlockSpec((1, D_MODEL), lambda s, t, kr: (0, 0)),
                  pl.BlockSpec((8, D_MODEL), lambda s, t, kr: (0, 0)),
                  pl.BlockSpec((8, D_MODEL), lambda s, t, kr: (0, 0))],
        out_specs=(pl.BlockSpec((TX, D_MODEL), blk_map), hbm, hbm,
                   pl.BlockSpec((8, D_MODEL), lambda s, t, kr: (0, 0))),
        scratch_shapes=[pltpu.VMEM((D_MODEL, SHARD_COLS), F32), pltpu.VMEM((SEQ, D_MODEL), F32),
                        pltpu.VMEM((N_SHARD, hw, SHARD_COLS), BF16), pltpu.VMEM((3, hw, SHARD_COLS), BF16),
                        pltpu.VMEM((N_SHARD, hw, SHARD_COLS), BF16), pltpu.VMEM((2, hw, SHARD_COLS), BF16),
                        pltpu.VMEM((2, hw // 2, SHARD_COLS), BF16),
                        pltpu.VMEM((N_SHARD, ho, D_MODEL), BF16), pltpu.VMEM((N_SHARD, ho, D_MODEL), BF16),
                        pltpu.VMEM((3, ho, D_MODEL), BF16), pltpu.VMEM((WO_ROWS, D_MODEL), F32),
                        pltpu.VMEM((N_DEV, 8, D_MODEL), F32), pltpu.VMEM((1, D_MODEL), F32),
                        pltpu.SemaphoreType.DMA((N_SEM_TAIL,)), pltpu.SemaphoreType.DMA((N_SEM_TAIL,)),
                        pltpu.SemaphoreType.DMA((2,))])
    return pl.pallas_call(
        body, name="bwd_tail", grid_spec=grid_spec,
        out_shape=(jax.ShapeDtypeStruct((SEQ, D_MODEL), F32),
                   jax.ShapeDtypeStruct((D_MODEL, SHARD_COLS), F32),
                   jax.ShapeDtypeStruct((WO_ROWS, D_MODEL), F32),
                   jax.ShapeDtypeStruct((8, D_MODEL), F32)),
        compiler_params=pltpu.CompilerParams(dimension_semantics=("arbitrary", "arbitrary"),
                                             vmem_limit_bytes=61 * 1024 * 1024, collective_id=COLLECTIVE_TAIL),
    )(kidx, h, dproj, wg, gwo, x2d, dx2, g1, small_a, small_b)


def _adam_update(w, g, m, v):
    nm = ADAM_B1 * m + (1.0 - ADAM_B1) * g
    nv = ADAM_B2 * v + (1.0 - ADAM_B2) * (g * g)
    m_hat = nm / (1.0 - ADAM_B1 ** ADAM_STEP)
    v_hat = nv / (1.0 - ADAM_B2 ** ADAM_STEP)
    return -ADAM_LR * (m_hat / (jnp.sqrt(v_hat) + ADAM_EPS) + ADAM_WD * w), nm, nv


def _adamw_all(tot, g_w_in, g_w_out, big, small, grad_x):
    n = len(small)
    rows = WO_ROWS
    steps = D_MODEL // rows

    def body(tot_ref, *refs):
        gx_ref, gx_out = refs[2 + 3 * (2 + n)], refs[-1]
        gx_out[...] = gx_ref[...]
        ins, outs = refs[:2 + 3 * (2 + n)], refs[3 + 3 * (2 + n):-1]
        g_refs, wmv = ins[:2], ins[2:]
        loss_ref, quads = outs[0], outs[1:]

        def update(j, g):
            w_ref, m_ref, v_ref = wmv[3 * j:3 * j + 3]
            g_ref, d_ref, nm_ref, nv_ref = quads[4 * j:4 * j + 4]
            g_ref[...] = g
            d_ref[...], nm_ref[...], nv_ref[...] = _adam_update(w_ref[...], g, m_ref[...], v_ref[...])

        update(0, g_refs[0][...])

        @pl.when(pl.program_id(0) == 0)
        def _():
            update(1, g_refs[1][...])
            k = 2 * lax.axis_index("x") + lax.axis_index("y")
            mine = pl.ds(pl.multiple_of(k * HEAD, HEAD), HEAD)
            loss_ref[...] = tot_ref[7:8, 0:1]
            grads = [tot_ref[0:1, :], tot_ref[1:2, :], tot_ref[2:3, 0:D_HGRN], tot_ref[2:3, D_HGRN:],
                     jnp.concatenate([tot_ref[3:4, 0:D_HGRN], tot_ref[3:4, D_HGRN:]], axis=0),
                     jnp.concatenate([tot_ref[4 + tap:5 + tap, mine] for tap in range(3)], axis=1)]
            for j, g in enumerate(grads):
                update(2 + j, g)

    whole = lambda a: pl.BlockSpec(a.shape, lambda i: (0, 0))
    blk = pl.BlockSpec((rows, SHARD_COLS), lambda i: (i, 0))
    arrays = [a for triple in big + small for a in triple]
    in_specs = ([whole(tot), blk, whole(g_w_out)] + [blk] * 3 + [whole(a) for a in arrays[3:]])
    shapes = [big[0][0], big[1][0]] + [w for w, _, _ in small]
    out_shape = (jax.ShapeDtypeStruct((1, 1), F32),) + tuple(
        jax.ShapeDtypeStruct(w.shape, F32) for w in shapes for _ in range(4))
    out_specs = (pl.BlockSpec((1, 1), lambda i: (0, 0)),) + (blk,) * 4 + tuple(
        whole(w) for w in shapes[1:] for _ in range(4))
    gx_blk = pl.BlockSpec((SEQ // steps, D_MODEL), lambda i: (i, 0))
    outs = pl.pallas_call(
        body, name="adamw_all", grid=(steps,),
        out_shape=out_shape + (jax.ShapeDtypeStruct(grad_x.shape, F32),),
        in_specs=in_specs + [gx_blk], out_specs=out_specs + (gx_blk,),
        compiler_params=pltpu.CompilerParams(dimension_semantics=("arbitrary",), vmem_limit_bytes=VMEM_LIMIT),
    )(tot, g_w_in, g_w_out, *arrays, grad_x)
    return [outs[0]] + [outs[1 + 4 * j:5 + 4 * j] for j in range(2 + n)] + [outs[-1]]


def _local_step(x2d, tgt, proj, lb_logits, cw, ga, gcn, w_out, gf):
    g64 = _group_matrix(HEAD, CONV_GROUP)
    aux, states, dx2, dmixed, gwo, part_out = _mix_out(proj, lb_logits, cw, ga, gcn, g64, w_out, x2d, gf, tgt)
    dproj, part_mix = _mix_bwd(proj, aux, states, dmixed, lb_logits, cw, ga, gcn, g64)
    return dproj, dx2, gwo.reshape(N_SHARD, WO_ROWS, D_MODEL), part_out, part_mix


def kernel(x, norm_gain, w_in, lb_logits, conv_w, hgrn_norm_gain, conv_norm_gain, w_out, final_norm_gain, loss_target, m_norm_gain, m_w_in, m_lb_logits, m_conv_w, m_hgrn_norm_gain, m_conv_norm_gain, m_w_out, m_final_norm_gain, v_norm_gain, v_w_in, v_lb_logits, v_conv_w, v_hgrn_norm_gain, v_conv_norm_gain, v_w_out, v_final_norm_gain):
    k = 2 * lax.axis_index("x") + lax.axis_index("y")
    kidx = jnp.reshape(k, (1,)).astype(jnp.int32)
    row = lambda a: a.reshape(1, D_MODEL)
    taps = lambda a: a.reshape(1, 3 * HEAD)
    h, proj, wg, cw = _gather_proj(kidx, x[0], norm_gain, w_in, taps(conv_w))
    dproj, dx2, gwo, part_out, part_mix = _local_step(
        x[0], loss_target[0], proj, lb_logits, cw, hgrn_norm_gain, conv_norm_gain, w_out, row(final_norm_gain))
    rgrad_x, rg_w_in, rg_w_out, tot = _bwd_tail(kidx, h, dproj, wg, gwo, x[0], dx2, norm_gain, part_out, part_mix)

    (loss, (g_w_in, d_w_in, nm_w_in, nv_w_in), (g_w_out, d_w_out, nm_w_out, nv_w_out),
     (g_norm_gain, d_ng, nm_ng, nv_ng), (g_final, d_fg, nm_fg, nv_fg), (g_hgrn, d_hg, nm_hg, nv_hg),
     (g_convn, d_cg, nm_cg, nv_cg), (g_lb, d_lb, nm_lb, nv_lb), (g_conv_w, d_cw, nm_cw, nv_cw),
     grad_x) = _adamw_all(
        tot, rg_w_in, rg_w_out,
        [(w_in[0], m_w_in[0], v_w_in[0]), (w_out[0], m_w_out[0], v_w_out[0])],
        [(norm_gain, m_norm_gain, v_norm_gain),
         (row(final_norm_gain), row(m_final_norm_gain), row(v_final_norm_gain)),
         (hgrn_norm_gain, m_hgrn_norm_gain, v_hgrn_norm_gain),
         (conv_norm_gain, m_conv_norm_gain, v_conv_norm_gain),
         (lb_logits, m_lb_logits, v_lb_logits),
         (taps(conv_w), taps(m_conv_w), taps(v_conv_w))],
        rgrad_x)
    flat = lambda a: a.reshape(D_MODEL)
    untap = lambda a: a.reshape(1, 3, HEAD)
    return (loss.reshape(()), grad_x[None],
            g_norm_gain, g_w_in[None], g_lb, untap(g_conv_w), g_hgrn, g_convn, g_w_out[None], flat(g_final),
            d_ng, d_w_in[None], d_lb, untap(d_cw), d_hg, d_cg, d_w_out[None], flat(d_fg),
            nm_ng, nm_w_in[None], nm_lb, untap(nm_cw), nm_hg, nm_cg, nm_w_out[None], flat(nm_fg),
            nv_ng, nv_w_in[None], nv_lb, untap(nv_cw), nv_hg, nv_cg, nv_w_out[None], flat(nv_fg))
```

```python
import jax
import jax.numpy as jnp
import numpy as np
from jax import lax
from jax.experimental import pallas as pl
from jax.experimental.pallas import tpu as pltpu

F32 = jnp.float32
BF16 = jnp.bfloat16
MESH = pl.DeviceIdType.MESH

SEQ = 2048
D_MODEL = 1024
D_HGRN = 512
D_CONV = 512
HEAD = 128
N_HEADS = 4
CHUNK = 64
CONV_GROUP = 64
N_SHARD = 4
SHARD_COLS = 1024
WO_ROWS = 256
EPS = 1e-6
TB = 256
NCB = TB // CHUNK
N_CHUNKS = SEQ // CHUNK
N_DEV = 8
COLLECTIVE_GATHER, COLLECTIVE_MIX_OUT, COLLECTIVE_TAIL = 1, 0, 2
AUX_O, AUX_CV, AUX_B, AUX_COLS = 0, 512, 1024, 1536

ADAM_LR = 0.001
ADAM_B1 = 0.9
ADAM_B2 = 0.999
ADAM_EPS = 1e-08
ADAM_WD = 0.01
ADAM_STEP = 10

VMEM_LIMIT = 56 * 1024 * 1024


def _dot(a, b):
    return jnp.dot(a, b, preferred_element_type=F32)


def _dot_nt(a, b):
    return lax.dot_general(a, b, (((1,), (1,)), ((), ())), preferred_element_type=F32)


def _dot_tn(a, b):
    return lax.dot_general(a, b, (((0,), (0,)), ((), ())), preferred_element_type=F32)


def _split_bf16(x, n):
    parts = []
    r = x
    for _ in range(n):
        p = r.astype(BF16)
        parts.append(p)
        r = r - p.astype(F32)
    return parts


def _exact_left(m, x, n=3):
    acc = None
    for p in _split_bf16(x, n):
        t = _dot(m, p)
        acc = t if acc is None else acc + t
    return acc


def _exact_left_many(m, xs, n=3):
    parts = [_split_bf16(x, n) for x in xs]
    accs = [None] * len(xs)
    for i in range(n):
        for j in range(len(xs)):
            t = _dot(m, parts[j][i])
            accs[j] = t if accs[j] is None else accs[j] + t
    return accs


def _group_mean_many(xs, gmat, n=2):
    parts = [_split_bf16(x, n) for x in xs]
    accs = [None] * len(xs)
    for i in range(n):
        for j in range(len(xs)):
            t = _dot(parts[j][i], gmat)
            accs[j] = t if accs[j] is None else accs[j] + t
    return accs


def _group_mean(x, gmat, n=2):
    w = gmat.shape[0]
    outs = []
    for c0 in range(0, x.shape[1], w):
        acc = None
        for p in _split_bf16(x[:, c0:c0 + w], n):
            t = _dot(p, gmat)
            acc = t if acc is None else acc + t
        outs.append(acc)
    return jnp.concatenate(outs, axis=1)


def _sigmoid(x):
    return 1.0 / (1.0 + jnp.exp(-x))


def _lower_bound(lbl):
    l0 = lbl[0:1, :]
    l1 = lbl[1:2, :]
    m = jnp.maximum(l0, l1)
    e0 = jnp.exp(l0 - m)
    e1 = jnp.exp(l1 - m)
    return e0 / (e0 + e1)


def _tri(lower):
    r = lax.broadcasted_iota(jnp.int32, (CHUNK, CHUNK), 0)
    c = lax.broadcasted_iota(jnp.int32, (CHUNK, CHUNK), 1)
    return jnp.where((c <= r) if lower else (c >= r), 1.0, 0.0).astype(BF16)


def _causal():
    r = lax.broadcasted_iota(jnp.int32, (CHUNK, CHUNK), 0)
    c = lax.broadcasted_iota(jnp.int32, (CHUNK, CHUNK), 1)
    return c <= r


def _shift_down(x, sh, prev_tail):
    r = pltpu.roll(x, sh, 0)
    pt = pltpu.roll(prev_tail, sh, 0)
    rows = lax.broadcasted_iota(jnp.int32, prev_tail.shape, 0)
    top = jnp.where(rows < sh, pt, r[0:8])
    return jnp.concatenate([top, r[8:]], axis=0)


def _shift_up(x, sh, next_head):
    n = x.shape[0]
    r = pltpu.roll(x, n - sh, 0)
    nh = pltpu.roll(next_head, 8 - sh, 0)
    rows = lax.broadcasted_iota(jnp.int32, next_head.shape, 0)
    bot = jnp.where(rows >= 8 - sh, nh, r[n - 8:])
    return jnp.concatenate([r[:n - 8], bot], axis=0)


def _group_matrix(width, group):
    r = np.arange(width)[:, None] // group
    c = np.arange(width)[None, :] // group
    return jnp.asarray(np.where(r == c, 1.0 / group, 0.0), dtype=BF16)


TG = 1024
SEM_W, SEM_CW, SEM_W_FWD, N_SEM = 0, 4, 7, 11


def _gather_proj(kidx, x2d, g1, w_in, conv_w):
    half_w = D_MODEL // 2
    half_c = SHARD_COLS // 2
    nt = SEQ // TG
    n_steps = 2 * N_SHARD

    def body(k_ref, x_ref, g_ref, w_ref, cw_ref, h_ref, p_ref, wg_out, cwg_out,
             wg_v, cwg_v, send_sems, recv_sems, out_sems):
        s, t = pl.program_id(0), pl.program_id(1)
        x, y, c = lax.axis_index("x"), lax.axis_index("y"), lax.axis_index("c")
        k = 2 * x + y
        sibling = (x, y, 1 - c)
        chips = [(1 - x, y), (x, 1 - y), (1 - x, 1 - y)]
        kjs = [2 * cx + cy for cx, cy in chips]
        diag = (*chips[2], c)

        def w_half(kk, cc):
            return wg_v.at[kk, pl.ds(cc * half_w, half_w), :]

        def w_quarter(kk, cc, piece):
            return wg_v.at[kk, pl.ds(cc * half_w, half_w), piece * half_c:(piece + 1) * half_c]

        def cw_of(kk):
            return cwg_v.at[:, pl.ds(pl.multiple_of(kk * HEAD, HEAD), HEAD)]

        def copy(sem, ref, to):
            return pltpu.make_async_remote_copy(
                src_ref=ref, dst_ref=ref, send_sem=send_sems.at[sem], recv_sem=recv_sems.at[sem],
                device_id=to, device_id_type=MESH)

        def at_step(sv, tv):
            return pl.when((s == sv) & (t == tv))

        w_direct = ([copy(SEM_W + j, w_half(k, c), (*chips[j], c)) for j in range(2)]
                    + [copy(SEM_W + 2 + p, w_quarter(k, c, p), diag) for p in range(2)])
        cw_direct = [copy(SEM_CW + j, cw_of(k), (*chip, c)) for j, chip in enumerate(chips)]
        w_passed = ([copy(SEM_W_FWD + j, w_half(kjs[j], c), sibling) for j in range(2)]
                    + [copy(SEM_W_FWD + 2 + p, w_quarter(kjs[2], c, p), sibling) for p in range(2)])
        stores = ([pltpu.make_async_copy(wg_v.at[kk], wg_out.at[kk], out_sems.at[i])
                   for i, kk in enumerate([k] + kjs)]
                  + [pltpu.make_async_copy(cwg_v, cwg_out, out_sems.at[4])])

        @at_step(0, 0)
        def _():
            barrier = pltpu.get_barrier_semaphore()
            for peer in [sibling] + [(*chip, c) for chip in chips]:
                pl.semaphore_signal(barrier, inc=1, device_id=peer, device_id_type=MESH)
            wg_v[k] = w_ref[0].astype(BF16)
            mine = pl.ds(pl.multiple_of(k * HEAD, HEAD), HEAD)
            cwg_v[:, mine] = jnp.zeros((8, HEAD), F32)
            for tap in range(3):
                cwg_v[tap:tap + 1, mine] = cw_ref[:, tap * HEAD:(tap + 1) * HEAD]
            pl.semaphore_wait(barrier, 4)
            for cp in w_direct + cw_direct:
                cp.start()
            stores[0].start()

        @at_step(2, 0)
        def _():
            for j in range(2):
                copy(SEM_W + j, w_half(kjs[j], c), sibling).wait_recv()
                w_passed[j].start()
            copy(SEM_W_FWD, w_half(kjs[0], 1 - c), sibling).wait_recv()
            stores[1].start()

        @at_step(4, 0)
        def _():
            copy(SEM_W_FWD + 1, w_half(kjs[1], 1 - c), sibling).wait_recv()
            stores[2].start()

        for p in range(2):
            @at_step(6 + p, 0)
            def _(p=p):
                copy(SEM_W + 2 + p, w_quarter(kjs[2], c, p), sibling).wait_recv()
                w_passed[2 + p].start()
                copy(SEM_W_FWD + 2 + p, w_quarter(kjs[2], 1 - c, p), sibling).wait_recv()

        rows = pl.ds(pl.multiple_of(t * TG, TG), TG)

        @pl.when(s == 0)
        def _():
            xv = x_ref[...]
            r = lax.rsqrt(jnp.mean(xv * xv, axis=-1, keepdims=True) + EPS)
            h_ref[rows, :] = (xv * r * g_ref[...]).astype(BF16)

        sh = s >> 1
        js = k ^ (((sh & 1) << 1) | (sh >> 1))
        for piece in range(2):
            @pl.when((s & 1) == piece)
            def _(piece=piece):
                p_ref[...] = _dot(h_ref[rows, :], wg_v[js, :, piece * half_c:(piece + 1) * half_c])

        @at_step(n_steps - 1, nt - 1)
        def _():
            stores[3].start()
            for j in range(3):
                copy(SEM_CW + j, cw_of(kjs[j]), sibling).wait_recv()
            stores[4].start()
            for cp in w_direct + cw_direct + w_passed:
                cp.wait_send()
            for st in stores:
                st.wait()

    def x_map(s, t, kr):
        return (jnp.where(s == 0, t, nt - 1), 0)

    def p_map(s, t, kr):
        sh = s >> 1
        return (t, 2 * (kr[0] ^ (((sh & 1) << 1) | (sh >> 1))) + (s & 1))

    hbm = pl.BlockSpec(memory_space=pl.ANY)
    grid_spec = pltpu.PrefetchScalarGridSpec(
        num_scalar_prefetch=1, grid=(n_steps, nt),
        in_specs=[pl.BlockSpec((TG, D_MODEL), x_map),
                  pl.BlockSpec((1, D_MODEL), lambda s, t, kr: (0, 0)),
                  pl.BlockSpec((1, D_MODEL, SHARD_COLS), lambda s, t, kr: (0, 0, 0)),
                  pl.BlockSpec((1, 3 * HEAD), lambda s, t, kr: (0, 0))],
        out_specs=(pl.BlockSpec((SEQ, D_MODEL), lambda s, t, kr: (0, 0)),
                   pl.BlockSpec((TG, half_c), p_map), hbm, hbm),
        scratch_shapes=[pltpu.VMEM((N_SHARD, D_MODEL, SHARD_COLS), BF16),
                        pltpu.VMEM((8, D_CONV), F32),
                        pltpu.SemaphoreType.DMA((N_SEM,)), pltpu.SemaphoreType.DMA((N_SEM,)),
                        pltpu.SemaphoreType.DMA((5,))])
    return pl.pallas_call(
        body, name="gather_proj", grid_spec=grid_spec,
        out_shape=(jax.ShapeDtypeStruct((SEQ, D_MODEL), BF16),
                   jax.ShapeDtypeStruct((SEQ, N_SHARD * SHARD_COLS), F32),
                   jax.ShapeDtypeStruct((N_SHARD, D_MODEL, SHARD_COLS), BF16),
                   jax.ShapeDtypeStruct((8, D_CONV), F32)),
        compiler_params=pltpu.CompilerParams(dimension_semantics=("arbitrary", "arbitrary"),
                                             vmem_limit_bytes=VMEM_LIMIT, collective_id=COLLECTIVE_GATHER),
    )(kidx, x2d, g1, w_in, conv_w)


LAG = 6


def _mix_out(proj, lb_logits, cw, ga, gcn, g64, w_out, x2d, gf, tgt):
    half_o = WO_ROWS // 2
    nblk = SEQ // TB
    n_steps = nblk + LAG + 1

    def body(p_ref, lbl_ref, cw_ref, ga_ref, gcn_ref, g64_ref, wo_ref, x_ref, gf_ref, t_ref,
             aux_ref, sto_ref, dx2_ref, dm_ref, gwo_ref, part_ref,
             st_ref, tail_ref, wog_v, stage, ring, y_ref, acc_ref, send_sems, recv_sems):
        i = pl.program_id(0)
        x, y, c = lax.axis_index("x"), lax.axis_index("y"), lax.axis_index("c")
        k = 2 * x + y
        sibling = (x, y, 1 - c)
        chips = [(1 - x, y), (x, 1 - y), (1 - x, 1 - y)]
        kjs = [2 * cx + cy for cx, cy in chips]

        def wo_half(kk, cc):
            return wog_v.at[pl.ds(pl.multiple_of(kk * WO_ROWS + cc * half_o, half_o), half_o), :]

        def copy(sem, ref, to):
            return pltpu.make_async_remote_copy(
                src_ref=ref, dst_ref=ref, send_sem=send_sems.at[sem], recv_sem=recv_sems.at[sem],
                device_id=to, device_id_type=MESH)

        wo_direct = [copy(j, wo_half(k, c), (*chip, c)) for j, chip in enumerate(chips)]
        wo_passed = [copy(3 + j, wo_half(kj, c), sibling) for j, kj in enumerate(kjs)]

        @pl.when(i == 0)
        def _():
            barrier = pltpu.get_barrier_semaphore()
            for peer in [sibling] + [(*chip, c) for chip in chips]:
                pl.semaphore_signal(barrier, inc=1, device_id=peer, device_id_type=MESH)
            st_ref[...] = jnp.zeros_like(st_ref)
            tail_ref[...] = jnp.zeros_like(tail_ref)
            acc_ref[...] = jnp.zeros_like(acc_ref)
            part_ref[...] = jnp.zeros_like(part_ref)
            wog_v[pl.ds(pl.multiple_of(k * WO_ROWS, WO_ROWS), WO_ROWS), :] = wo_ref[0].astype(BF16)
            pl.semaphore_wait(barrier, 4)
            for cp in wo_direct:
                cp.start()

        @pl.when(i == LAG - 1)
        def _():
            for j in range(3):
                copy(j, wo_half(kjs[j], c), sibling).wait_recv()
                wo_passed[j].start()

        @pl.when(i == LAG)
        def _():
            for j in range(3):
                copy(3 + j, wo_half(kjs[j], 1 - c), sibling).wait_recv()

        lb = _lower_bound(lbl_ref[...])
        tri = _tri(True)
        causal = _causal()
        g64m = g64_ref[...]
        heads = range(N_HEADS)
        cs = [slice(hd * HEAD, (hd + 1) * HEAD) for hd in heads]
        col = lambda base, hd: slice(base + hd * HEAD, base + (hd + 1) * HEAD)

        def mix_chunk(n):
            sl = pl.ds(n * CHUNK, CHUNK)
            sg = [_sigmoid(p_ref[sl, col(512, hd)]) for hd in heads]
            f = [lb[:, cs[hd]] + (1.0 - lb[:, cs[hd]]) * sg[hd] for hd in heads]
            bc = _exact_left_many(tri, [jnp.log(f[hd]) for hd in heads])
            for hd in heads:
                aux_ref[sl, col(AUX_B, hd)] = bc[hd]
            g = [bc[hd][CHUNK - 1:CHUNK, :] for hd in heads]
            qd = [(p_ref[sl, col(0, hd)] * jnp.exp(bc[hd])).astype(BF16) for hd in heads]
            kk = [1.0 - f[hd] for hd in heads]
            ki = [(kk[hd] * jnp.exp(-bc[hd])).astype(BF16) for hd in heads]
            ke = [(kk[hd] * jnp.exp(g[hd] - bc[hd])).astype(BF16) for hd in heads]
            vb = [p_ref[sl, col(1024, hd)].astype(BF16) for hd in heads]
            st = [st_ref[hd] for hd in heads]
            st_b = [a.astype(BF16) for a in st]
            for hd in heads:
                sto_ref[n, hd] = st_b[hd]
            scm = [_dot_nt(qd[hd], ki[hd]) for hd in heads]
            inter = [_dot_nt(qd[hd], st_b[hd]) for hd in heads]
            upd = [_dot_tn(vb[hd], ke[hd]) for hd in heads]
            intra = [_dot(jnp.where(causal, scm[hd], 0.0).astype(BF16), vb[hd]) for hd in heads]
            for hd in heads:
                st_ref[hd] = st[hd] * jnp.exp(g[hd]) + upd[hd]
                o = intra[hd] + inter[hd]
                aux_ref[sl, col(AUX_O, hd)] = o
                ra = lax.rsqrt(jnp.mean(o * o, axis=-1, keepdims=True) + EPS)
                za = p_ref[sl, col(1536, hd)]
                stage[sl, cs[hd]] = (o * ra * ga_ref[:, cs[hd]] * (za * _sigmoid(za))).astype(BF16)
            yb = []
            for hd in heads:
                cu = p_ref[sl, col(3072, hd)] * p_ref[sl, col(2048, hd)]
                tail = tail_ref[:, cs[hd]]
                cv = (cw_ref[0:1, cs[hd]] * _shift_down(cu, 2, tail) + cw_ref[1:2, cs[hd]] * _shift_down(cu, 1, tail)
                      + cw_ref[2:3, cs[hd]] * cu)
                tail_ref[:, cs[hd]] = cu[CHUNK - 8:, :]
                aux_ref[sl, col(AUX_CV, hd)] = cv
                yb.append(p_ref[sl, col(2560, hd)] * cv)
            ms = _group_mean_many([y * y for y in yb], g64m)
            for hd in heads:
                rb = lax.rsqrt(ms[hd] + EPS)
                zb = p_ref[sl, col(3584, hd)]
                stage[sl, col(512, hd)] = (yb[hd] * rb * gcn_ref[:, cs[hd]] * (zb * _sigmoid(zb))).astype(BF16)

        def step(mix, first, rest):
            if first:
                y_ref[i % 2] = _dot(ring[pl.ds(pl.multiple_of((i - LAG) * TB, TB), TB), :], wog_v[...])
            if mix:
                mix_chunk(0)
            if rest:
                mixed_b = ring[pl.ds(pl.multiple_of((i - LAG - 1) * TB, TB), TB), :]
                x2 = x_ref[...] + y_ref[(i - 1) % 2]
                r2 = lax.rsqrt(jnp.mean(x2 * x2, axis=-1, keepdims=True) + EPS)
                n2 = x2 * r2
                gfv = gf_ref[...]
                err = n2 * gfv - t_ref[...]
                loss = 0.5 * jnp.sum(jnp.mean(err * err, axis=-1, keepdims=True), axis=0, keepdims=True)
                dy = err * (1.0 / D_MODEL)
                part_ref[1:2, :] += jnp.sum(dy * n2, axis=0, keepdims=True)
                part_ref[7:8, :] += jnp.broadcast_to(loss, (1, D_MODEL))
                dn = dy * gfv
                dx2 = r2 * (dn - n2 * jnp.mean(dn * n2, axis=-1, keepdims=True))
                dx2_ref[...] = dx2
                dx2_b = dx2.astype(BF16)
            if mix:
                mix_chunk(1)
            if rest:
                dm_ref[...] = _dot_nt(dx2_b, wog_v[...])
            if mix:
                mix_chunk(2)
            if rest:
                acc_ref[...] += _dot_tn(mixed_b, dx2_b)
            if mix:
                mix_chunk(3)
                ring[pl.ds(pl.multiple_of(i * TB, TB), TB), :] = stage[...]

        assert LAG + 1 < nblk
        for lo, hi, work in [(0, LAG, (True, False, False)), (LAG, LAG + 1, (True, True, False)),
                             (LAG + 1, nblk, (True, True, True)), (nblk, nblk + LAG, (False, True, True)),
                             (nblk + LAG, n_steps, (False, False, True))]:
            @pl.when((i >= lo) & (i < hi))
            def _(work=work):
                step(*work)

        @pl.when(i == n_steps - 1)
        def _():
            gwo_ref[...] = acc_ref[...].astype(BF16)
            for cp in wo_direct + wo_passed:
                cp.wait_send()

    assert NCB == 4
    row = lambda w: pl.BlockSpec((1, w), lambda i: (0, 0))
    mix_blk = lambda i: jnp.minimum(i, nblk - 1)
    out_blk = lambda i: jnp.clip(i - LAG - 1, 0, nblk - 1)
    tok = lambda: pl.BlockSpec((TB, D_MODEL), lambda i: (out_blk(i), 0))
    return pl.pallas_call(
        body, name="mix_out", grid=(n_steps,),
        out_shape=(jax.ShapeDtypeStruct((SEQ, AUX_COLS), F32),
                   jax.ShapeDtypeStruct((N_CHUNKS, N_HEADS, HEAD, HEAD), BF16),
                   jax.ShapeDtypeStruct((SEQ, D_MODEL), F32),
                   jax.ShapeDtypeStruct((SEQ, D_MODEL), F32),
                   jax.ShapeDtypeStruct((D_MODEL, D_MODEL), BF16),
                   jax.ShapeDtypeStruct((8, D_MODEL), F32)),
        in_specs=[pl.BlockSpec((TB, 4096), lambda i: (jnp.minimum(i, nblk - 1), 0)),
                  pl.BlockSpec((2, D_HGRN), lambda i: (0, 0)),
                  pl.BlockSpec((8, D_CONV), lambda i: (0, 0)),
                  row(D_HGRN), row(D_CONV),
                  pl.BlockSpec((HEAD, HEAD), lambda i: (0, 0)),
                  pl.BlockSpec((1, WO_ROWS, D_MODEL), lambda i: (0, 0, 0)),
                  tok(), row(D_MODEL), tok()],
        out_specs=(pl.BlockSpec((TB, AUX_COLS), lambda i: (mix_blk(i), 0)),
                   pl.BlockSpec((NCB, N_HEADS, HEAD, HEAD), lambda i: (mix_blk(i), 0, 0, 0)),
                   tok(), tok(),
                   pl.BlockSpec((D_MODEL, D_MODEL), lambda i: (0, 0)),
                   pl.BlockSpec((8, D_MODEL), lambda i: (0, 0))),
        scratch_shapes=[pltpu.VMEM((N_HEADS, HEAD, HEAD), F32), pltpu.VMEM((8, D_CONV), F32),
                        pltpu.VMEM((D_MODEL, D_MODEL), BF16), pltpu.VMEM((TB, D_MODEL), BF16),
                        pltpu.VMEM((SEQ, D_MODEL), BF16), pltpu.VMEM((2, TB, D_MODEL), F32),
                        pltpu.VMEM((D_MODEL, D_MODEL), F32),
                        pltpu.SemaphoreType.DMA((6,)), pltpu.SemaphoreType.DMA((6,))],
        compiler_params=pltpu.CompilerParams(dimension_semantics=("arbitrary",), vmem_limit_bytes=VMEM_LIMIT,
                                             collective_id=COLLECTIVE_MIX_OUT),
    )(proj, lb_logits, cw, ga, gcn, g64, w_out, x2d, gf, tgt)


def _mix_bwd(proj, aux, states, dmixed, lb_logits, cw, ga, gcn, g64):
    nblk = SEQ // TB

    def body(p_ref, aux_ref, st_ref, dm_ref, lbl_ref, cw_ref, ga_ref, gcn_ref, g64_ref,
             dp_ref, part_ref, dst_ref, head_ref, dlb_ref):
        i = pl.program_id(0)

        @pl.when(i == 0)
        def _():
            dst_ref[...] = jnp.zeros_like(dst_ref)
            head_ref[...] = jnp.zeros_like(head_ref)
            part_ref[...] = jnp.zeros_like(part_ref)
            dlb_ref[...] = jnp.zeros_like(dlb_ref)

        lb = _lower_bound(lbl_ref[...])
        triu = _tri(False)
        causal = _causal()
        g64m = g64_ref[...]
        rowsum = lambda a: jnp.sum(a, axis=0, keepdims=True)
        heads = range(N_HEADS)
        cs = [slice(hd * HEAD, (hd + 1) * HEAD) for hd in heads]
        col = lambda base, hd: slice(base + hd * HEAD, base + (hd + 1) * HEAD)
        for n in reversed(range(NCB)):
            sl = pl.ds(n * CHUNK, CHUNK)
            cvv = [aux_ref[sl, col(AUX_CV, hd)] for hd in heads]
            gb = [p_ref[sl, col(2560, hd)] for hd in heads]
            yb = [gb[hd] * cvv[hd] for hd in heads]
            ms = _group_mean_many([y * y for y in yb], g64m)
            rb, nb, dnb = [], [], []
            for hd in heads:
                rb.append(lax.rsqrt(ms[hd] + EPS))
                nb.append(yb[hd] * rb[hd])
                zb = p_ref[sl, col(3584, hd)]
                sgb = _sigmoid(zb)
                dmb = dm_ref[sl, col(512, hd)]
                silu = zb * sgb
                dgate = dmb * gcn_ref[:, cs[hd]]
                part_ref[2:3, col(512, hd)] += rowsum(dmb * nb[hd] * silu)
                dp_ref[sl, col(3584, hd)] = (dgate * nb[hd] * (sgb + silu * (1.0 - sgb))).astype(BF16)
                dnb.append(dgate * silu)
            mdn = _group_mean_many([dnb[hd] * nb[hd] for hd in heads], g64m)
            for hd in heads:
                dyb = rb[hd] * (dnb[hd] - nb[hd] * mdn[hd])
                dp_ref[sl, col(2560, hd)] = (dyb * cvv[hd]).astype(BF16)
                dcv = dyb * gb[hd]
                head = head_ref[:, cs[hd]]
                dcv1 = _shift_up(dcv, 1, head)
                dcv2 = _shift_up(dcv, 2, head)
                head_ref[:, cs[hd]] = dcv[0:8, :]
                u = p_ref[sl, col(2048, hd)]
                gc = p_ref[sl, col(3072, hd)]
                cu = gc * u
                part_ref[4:5, cs[hd]] += rowsum(dcv2 * cu)
                part_ref[5:6, cs[hd]] += rowsum(dcv1 * cu)
                part_ref[6:7, cs[hd]] += rowsum(dcv * cu)
                dcu = cw_ref[2:3, cs[hd]] * dcv + cw_ref[1:2, cs[hd]] * dcv1 + cw_ref[0:1, cs[hd]] * dcv2
                dp_ref[sl, col(3072, hd)] = (dcu * u).astype(BF16)
                dp_ref[sl, col(2048, hd)] = (dcu * gc).astype(BF16)
            do_b = []
            for hd in heads:
                ov = aux_ref[sl, col(AUX_O, hd)]
                ra = lax.rsqrt(jnp.mean(ov * ov, axis=-1, keepdims=True) + EPS)
                na = ov * ra
                za = p_ref[sl, col(1536, hd)]
                sga = _sigmoid(za)
                dma = dm_ref[sl, cs[hd]]
                silu = za * sga
                dgate = dma * ga_ref[:, cs[hd]]
                part_ref[2:3, cs[hd]] += rowsum(dma * na * silu)
                dp_ref[sl, col(1536, hd)] = (dgate * na * (sga + silu * (1.0 - sga))).astype(BF16)
                dna = dgate * silu
                do_b.append((ra * (dna - na * jnp.mean(dna * na, axis=-1, keepdims=True))).astype(BF16))
            s = [_sigmoid(p_ref[sl, col(512, hd)]) for hd in heads]
            f = [lb[:, cs[hd]] + (1.0 - lb[:, cs[hd]]) * s[hd] for hd in heads]
            bc = [aux_ref[sl, col(AUX_B, hd)] for hd in heads]
            g = [bc[hd][CHUNK - 1:CHUNK, :] for hd in heads]
            eb = [jnp.exp(bc[hd]) for hd in heads]
            enb = [jnp.exp(-bc[hd]) for hd in heads]
            eg = [jnp.exp(g[hd] - bc[hd]) for hd in heads]
            dec = [jnp.exp(g[hd]) for hd in heads]
            qd = [p_ref[sl, cs[hd]] * eb[hd] for hd in heads]
            kk = [1.0 - f[hd] for hd in heads]
            ki = [kk[hd] * enb[hd] for hd in heads]
            ke = [kk[hd] * eg[hd] for hd in heads]
            qd_b = [a.astype(BF16) for a in qd]
            ki_b = [a.astype(BF16) for a in ki]
            ke_b = [a.astype(BF16) for a in ke]
            vb = [p_ref[sl, col(1024, hd)].astype(BF16) for hd in heads]
            st_b = [st_ref[n, hd] for hd in heads]
            dst = [dst_ref[hd] for hd in heads]
            dst_b = [a.astype(BF16) for a in dst]
            scm = [_dot_nt(qd_b[hd], ki_b[hd]) for hd in heads]
            amm = [_dot_nt(do_b[hd], vb[hd]) for hd in heads]
            dqd2 = [_dot(do_b[hd], st_b[hd]) for hd in heads]
            dke = [_dot(vb[hd], dst_b[hd]) for hd in heads]
            dv2 = [_dot_nt(ke_b[hd], dst_b[hd]) for hd in heads]
            dsu = [_dot_tn(do_b[hd], qd_b[hd]) for hd in heads]
            sc = [jnp.where(causal, scm[hd], 0.0).astype(BF16) for hd in heads]
            am = [jnp.where(causal, amm[hd], 0.0).astype(BF16) for hd in heads]
            dqd1 = [_dot(am[hd], ki_b[hd]) for hd in heads]
            dki = [_dot_tn(am[hd], qd_b[hd]) for hd in heads]
            dv1 = [_dot_tn(sc[hd], do_b[hd]) for hd in heads]
            db, dgv, dkk = [], [], []
            for hd in heads:
                dqd = dqd1[hd] + dqd2[hd]
                ddec = rowsum(dst[hd] * st_b[hd].astype(F32))
                dst_ref[hd] = dst[hd] * dec[hd] + dsu[hd]
                dp_ref[sl, cs[hd]] = (dqd * eb[hd]).astype(BF16)
                dp_ref[sl, col(1024, hd)] = (dv1[hd] + dv2[hd]).astype(BF16)
                dke_eg = dke[hd] * eg[hd]
                dkk.append(dki[hd] * enb[hd] + dke_eg)
                db.append(dqd * qd[hd] - kk[hd] * dkk[hd])
                dgv.append(rowsum(kk[hd] * dke_eg) + ddec * dec[hd])
            rc = _exact_left_many(triu, db, 2)
            for hd in heads:
                df = (rc[hd] + dgv[hd]) / f[hd] - dkk[hd]
                one_s = 1.0 - s[hd]
                dlb_ref[:, cs[hd]] += rowsum(df * one_s)
                dp_ref[sl, col(512, hd)] = (df * (1.0 - lb[:, cs[hd]]) * s[hd] * one_s).astype(BF16)

        @pl.when(i == nblk - 1)
        def _():
            row = dlb_ref[...] * lb * (1.0 - lb)
            part_ref[3:4, 0:D_HGRN] = row
            part_ref[3:4, D_HGRN:] = -row

    rev = lambda w: pl.BlockSpec((TB, w), lambda i: (nblk - 1 - i, 0))
    row = lambda w: pl.BlockSpec((1, w), lambda i: (0, 0))
    return pl.pallas_call(
        body, name="mix_bwd", grid=(nblk,),
        out_shape=(jax.ShapeDtypeStruct((SEQ, 4096), BF16),
                   jax.ShapeDtypeStruct((8, D_MODEL), F32)),
        in_specs=[rev(4096), rev(AUX_COLS),
                  pl.BlockSpec((NCB, N_HEADS, HEAD, HEAD), lambda i: (nblk - 1 - i, 0, 0, 0)),
                  rev(D_MODEL),
                  pl.BlockSpec((2, D_HGRN), lambda i: (0, 0)),
                  pl.BlockSpec((8, D_CONV), lambda i: (0, 0)),
                  row(D_HGRN), row(D_CONV),
                  pl.BlockSpec((HEAD, HEAD), lambda i: (0, 0))],
        out_specs=(rev(4096), pl.BlockSpec((8, D_MODEL), lambda i: (0, 0))),
        scratch_shapes=[pltpu.VMEM((N_HEADS, HEAD, HEAD), F32), pltpu.VMEM((8, D_CONV), F32),
                        pltpu.VMEM((1, D_HGRN), F32)],
        compiler_params=pltpu.CompilerParams(dimension_semantics=("arbitrary",), vmem_limit_bytes=VMEM_LIMIT),
    )(proj, aux, states, dmixed, lb_logits, cw, ga, gcn, g64)


TT = 1024
TX = 512
(SEM_D2D, SEM_D2D_O, SEM_ICI, SEM_ICI_O, SEM_FIN, SEM_FIN_O, SEM_SMALL, SEM_VIA, N_SEM_TAIL) = (
    0, 4, 5, 8, 11, 12, 12, 20, 22)


def _bwd_tail(kidx, h, dproj, wg, gwo, x2d, dx2, g1, small_a, small_b):
    hw = D_MODEL // 2
    ho = WO_ROWS // 2
    nt = SEQ // TT
    norm_step = 2 * N_SHARD
    n_steps = norm_step + SEQ // TX // nt

    def body(k_ref, h_ref, dp_ref, w_ref, gwo_ref, x_ref, dx2_ref, g_ref, sm_ref, smb_ref,
             gx_ref, gw_out, gwo_out, osm_ref,
             acc, dh, sendbuf, keep, sibrcv, rcv, merge, sib_o, p_o, rcv_o, res_o, sm_buf, dng,
             send_sems, recv_sems, out_sems):
        s, t = pl.program_id(0), pl.program_id(1)
        x, y, c = lax.axis_index("x"), lax.axis_index("y"), lax.axis_index("c")
        k = 2 * x + y
        me = 4 * x + 2 * y + c
        sibling = (x, y, 1 - c)
        chips = [(1 - x, 1 - y), (1 - x, y), (x, 1 - y)]
        kjs = [2 * cx + cy for cx, cy in chips]
        mine = pl.ds(pl.multiple_of(c * hw, hw), hw)
        other = pl.ds(pl.multiple_of((1 - c) * hw, hw), hw)
        mine_o = pl.ds(pl.multiple_of(c * ho, ho), ho)
        other_o = pl.ds(pl.multiple_of((1 - c) * ho, ho), ho)

        def copy(sem, src, dst, to):
            return pltpu.make_async_remote_copy(
                src_ref=src, dst_ref=dst, send_sem=send_sems.at[sem], recv_sem=recv_sems.at[sem],
                device_id=to, device_id_type=MESH)

        def at_step(sv, tv):
            return pl.when((s == sv) & (t == tv))

        def at_norm_block(b):
            return at_step(norm_step + b // nt, b % nt)

        d2d = [copy(SEM_D2D + sv, sendbuf.at[sv], sibrcv.at[sv], sibling) for sv in range(N_SHARD)]
        d2d_o = copy(SEM_D2D_O, gwo_ref.at[:, other_o, :], sib_o, sibling)
        ici = {sv: copy(SEM_ICI + sv, keep.at[sv], rcv.at[sv - 1], (*chips[sv], c)) for sv in (1, 2)}
        qh = hw // 2
        via = [copy(SEM_VIA, keep.at[0, 0:qh, :], merge.at[1], (*chips[1], c)),
               copy(SEM_VIA + 1, keep.at[0, qh:hw, :], merge.at[0], (*chips[2], c))]
        merged_rows = [slice(qh, hw), slice(0, qh)]
        ici_o = [copy(SEM_ICI_O + sv, p_o.at[kjs[sv]], rcv_o.at[sv], (*chips[sv], c)) for sv in range(3)]
        fin = copy(SEM_FIN, acc.at[mine, :], gw_out.at[mine, :], sibling)
        fin_o = copy(SEM_FIN_O, res_o.at[mine_o, :], res_o.at[mine_o, :], sibling)
        smalls = [copy(SEM_SMALL + m, sm_buf.at[me], sm_buf.at[me],
                       (x ^ (m >> 2), y ^ ((m >> 1) & 1), c ^ (m & 1))) for m in range(1, N_DEV)]
        store_w = pltpu.make_async_copy(acc.at[mine, :], gw_out.at[mine, :], out_sems.at[0])
        store_o = pltpu.make_async_copy(res_o, gwo_out, out_sems.at[1])

        @at_step(0, 0)
        def _():
            barrier = pltpu.get_barrier_semaphore()
            for m in range(1, N_DEV):
                pl.semaphore_signal(barrier, inc=1, device_id=(x ^ (m >> 2), y ^ ((m >> 1) & 1), c ^ (m & 1)),
                                    device_id_type=MESH)
            pl.semaphore_wait(barrier, N_DEV - 1)
            d2d_o.start()

        @at_step(0, 1)
        def _():
            d2d_o.wait_recv()
            for j in range(N_SHARD):
                p_o[j] = (gwo_ref[j, mine_o, :].astype(F32) + sib_o[j].astype(F32)).astype(BF16)
            res_o[mine_o, :] = gwo_ref[k, mine_o, :].astype(F32) + sib_o[k].astype(F32)
            for cp in ici_o:
                cp.start()

        rows = pl.ds(pl.multiple_of(t * TT, TT), TT)

        @pl.when((s < N_SHARD) & (t == 0))
        def _():
            acc[...] = _dot_tn(h_ref[...], dp_ref[...])

        @pl.when((s < N_SHARD) & (t > 0))
        def _():
            acc[...] += _dot_tn(h_ref[...], dp_ref[...])

        for sv in range(N_SHARD):
            @at_step(sv, nt - 1)
            def _(sv=sv):
                sendbuf[sv] = acc[other, :].astype(BF16)
                if sv < 3:
                    keep[sv] = acc[mine, :].astype(BF16)
                d2d[sv].start()

        @at_step(1, 0)
        def _():
            d2d[0].wait_recv()
            keep[0] = (keep[0].astype(F32) + sibrcv[0].astype(F32)).astype(BF16)
            for cp in via:
                cp.start()

        for sv in (1, 2):
            @at_step(sv + 2, 0)
            def _(sv=sv):
                d2d[sv].wait_recv()
                keep[sv] = (keep[sv].astype(F32) + sibrcv[sv].astype(F32)).astype(BF16)
                via[2 - sv].wait_recv()
                rows_m = merged_rows[sv - 1]
                keep[sv, rows_m, :] = (keep[sv, rows_m, :].astype(F32) + merge[sv - 1].astype(F32)).astype(BF16)
                ici[sv].start()

        @pl.when(s == N_SHARD)
        def _():
            dh[rows, :] = _dot_nt(dp_ref[...], w_ref[0])

        @pl.when((s > N_SHARD) & (s < norm_step))
        def _():
            dh[rows, :] += _dot_nt(dp_ref[...], w_ref[0])

        @at_norm_block(0)
        def _():
            d2d[3].wait_recv()
            acc[mine, :] += sibrcv[3].astype(F32)

        @at_norm_block(1)
        def _():
            tot = res_o[mine_o, :]
            for sv in range(3):
                ici_o[sv].wait_recv()
                tot = tot + rcv_o[sv].astype(F32)
            res_o[mine_o, :] = tot
            fin_o.start()

        @at_norm_block(2)
        def _():
            ici[1].wait_recv()
            acc[mine, :] += rcv[0].astype(F32)

        @at_norm_block(SEQ // TX - 2)
        def _():
            ici[2].wait_recv()
            acc[mine, :] += rcv[1].astype(F32)
            fin.start()
            store_w.start()
            fin_o.wait_recv()
            store_o.start()

        @at_norm_block(0)
        def _():
            dng[...] = jnp.zeros_like(dng)

        @pl.when(s >= norm_step)
        def _():
            blk = (s - norm_step) * nt + t
            dhv = dh[pl.ds(pl.multiple_of(blk * TX, TX), TX), :]
            xv = x_ref[...]
            r = lax.rsqrt(jnp.mean(xv * xv, axis=-1, keepdims=True) + EPS)
            xn = xv * r
            dng[...] += jnp.sum(dhv * xn, axis=0, keepdims=True)
            dxn = dhv * g_ref[...]
            gx_ref[...] = dx2_ref[...] + r * (dxn - xn * jnp.mean(dxn * xn, axis=-1, keepdims=True))

        @at_step(n_steps - 1, nt - 1)
        def _():
            sm_buf[me] = sm_ref[...] + smb_ref[...]
            sm_buf[me, 0:1, :] = dng[...]
            for cp in smalls:
                cp.start()
            for m in range(1, N_DEV):
                copy(SEM_SMALL + m, sm_buf.at[0], sm_buf.at[0], sibling).wait_recv()
            tot = sm_buf[0]
            for d in range(1, N_DEV):
                tot = tot + sm_buf[d]
            osm_ref[...] = tot
            fin.wait_recv()
            for cp in d2d + [d2d_o] + via + list(ici.values()) + ici_o + [fin, fin_o] + smalls:
                cp.wait_send()
            store_o.wait()
            store_w.wait()

    def shard_of(s, kr):
        order = jnp.where(s < N_SHARD, s, jnp.where(s < norm_step, s - N_SHARD, 3))
        return kr[0] ^ (3 - order)

    def h_map(s, t, kr):
        return (jnp.where(s < N_SHARD, t, nt - 1), 0)

    def dp_map(s, t, kr):
        return (jnp.where(s < norm_step, t, nt - 1), shard_of(s, kr))

    def w_map(s, t, kr):
        return (shard_of(jnp.maximum(s, N_SHARD), kr), 0, 0)

    def blk_map(s, t, kr):
        return (jnp.where(s < norm_step, 0, (s - norm_step) * nt + t), 0)

    hbm = pl.BlockSpec(memory_space=pl.ANY)
    grid_spec = pltpu.PrefetchScalarGridSpec(
        num_scalar_prefetch=1, grid=(n_steps, nt),
        in_specs=[pl.BlockSpec((TT, D_MODEL), h_map),
                  pl.BlockSpec((TT, SHARD_COLS), dp_map),
                  pl.BlockSpec((1, D_MODEL, SHARD_COLS), w_map),
                  pl.BlockSpec((N_SHARD, WO_ROWS, D_MODEL), lambda s, t, kr: (0, 0, 0),
                               pipeline_mode=pl.Buffered(1)),
                  pl.BlockSpec((TX, D_MODEL), blk_map),
                  pl.BlockSpec((TX, D_MODEL), blk_map),
                  pl.BlockSpec((1, D_MODEL), lambda s, t, kr: (0, 0)),
                  pl.BlockSpec((8, D_MODEL), lambda s, t, kr: (0, 0)),
                  pl.BlockSpec((8, D_MODEL), lambda s, t, kr: (0, 0))],
        out_specs=(pl.BlockSpec((TX, D_MODEL), blk_map), hbm, hbm,
                   pl.BlockSpec((8, D_MODEL), lambda s, t, kr: (0, 0))),
        scratch_shapes=[pltpu.VMEM((D_MODEL, SHARD_COLS), F32), pltpu.VMEM((SEQ, D_MODEL), F32),
                        pltpu.VMEM((N_SHARD, hw, SHARD_COLS), BF16), pltpu.VMEM((3, hw, SHARD_COLS), BF16),
                        pltpu.VMEM((N_SHARD, hw, SHARD_COLS), BF16), pltpu.VMEM((2, hw, SHARD_COLS), BF16),
                        pltpu.VMEM((2, hw // 2, SHARD_COLS), BF16),
                        pltpu.VMEM((N_SHARD, ho, D_MODEL), BF16), pltpu.VMEM((N_SHARD, ho, D_MODEL), BF16),
                        pltpu.VMEM((3, ho, D_MODEL), BF16), pltpu.VMEM((WO_ROWS, D_MODEL), F32),
                        pltpu.VMEM((N_DEV, 8, D_MODEL), F32), pltpu.VMEM((1, D_MODEL), F32),
                        pltpu.SemaphoreType.DMA((N_SEM_TAIL,)), pltpu.SemaphoreType.DMA((N_SEM_TAIL,)),
                        pltpu.SemaphoreType.DMA((2,))])
    return pl.pallas_call(
        body, name="bwd_tail", grid_spec=grid_spec,
        out_shape=(jax.ShapeDtypeStruct((SEQ, D_MODEL), F32),
                   jax.ShapeDtypeStruct((D_MODEL, SHARD_COLS), F32),
                   jax.ShapeDtypeStruct((WO_ROWS, D_MODEL), F32),
                   jax.ShapeDtypeStruct((8, D_MODEL), F32)),
        compiler_params=pltpu.CompilerParams(dimension_semantics=("arbitrary", "arbitrary"),
                                             vmem_limit_bytes=61 * 1024 * 1024, collective_id=COLLECTIVE_TAIL),
    )(kidx, h, dproj, wg, gwo, x2d, dx2, g1, small_a, small_b)


def _adam_update(w, g, m, v):
    nm = ADAM_B1 * m + (1.0 - ADAM_B1) * g
    nv = ADAM_B2 * v + (1.0 - ADAM_B2) * (g * g)
    m_hat = nm / (1.0 - ADAM_B1 ** ADAM_STEP)
    v_hat = nv / (1.0 - ADAM_B2 ** ADAM_STEP)
    return -ADAM_LR * (m_hat / (jnp.sqrt(v_hat) + ADAM_EPS) + ADAM_WD * w), nm, nv


def _adamw_all(tot, g_w_in, g_w_out, big, small, grad_x):
    n = len(small)
    rows = WO_ROWS
    steps = D_MODEL // rows

    def body(tot_ref, *refs):
        gx_ref, gx_out = refs[2 + 3 * (2 + n)], refs[-1]
        gx_out[...] = gx_ref[...]
        ins, outs = refs[:2 + 3 * (2 + n)], refs[3 + 3 * (2 + n):-1]
        g_refs, wmv = ins[:2], ins[2:]
        loss_ref, quads = outs[0], outs[1:]

        def update(j, g):
            w_ref, m_ref, v_ref = wmv[3 * j:3 * j + 3]
            g_ref, d_ref, nm_ref, nv_ref = quads[4 * j:4 * j + 4]
            g_ref[...] = g
            d_ref[...], nm_ref[...], nv_ref[...] = _adam_update(w_ref[...], g, m_ref[...], v_ref[...])

        update(0, g_refs[0][...])

        @pl.when(pl.program_id(0) == 0)
        def _():
            update(1, g_refs[1][...])
            k = 2 * lax.axis_index("x") + lax.axis_index("y")
            mine = pl.ds(pl.multiple_of(k * HEAD, HEAD), HEAD)
            loss_ref[...] = tot_ref[7:8, 0:1]
            grads = [tot_ref[0:1, :], tot_ref[1:2, :], tot_ref[2:3, 0:D_HGRN], tot_ref[2:3, D_HGRN:],
                     jnp.concatenate([tot_ref[3:4, 0:D_HGRN], tot_ref[3:4, D_HGRN:]], axis=0),
                     jnp.concatenate([tot_ref[4 + tap:5 + tap, mine] for tap in range(3)], axis=1)]
            for j, g in enumerate(grads):
                update(2 + j, g)

    whole = lambda a: pl.BlockSpec(a.shape, lambda i: (0, 0))
    blk = pl.BlockSpec((rows, SHARD_COLS), lambda i: (i, 0))
    arrays = [a for triple in big + small for a in triple]
    in_specs = ([whole(tot), blk, whole(g_w_out)] + [blk] * 3 + [whole(a) for a in arrays[3:]])
    shapes = [big[0][0], big[1][0]] + [w for w, _, _ in small]
    out_shape = (jax.ShapeDtypeStruct((1, 1), F32),) + tuple(
        jax.ShapeDtypeStruct(w.shape, F32) for w in shapes for _ in range(4))
    out_specs = (pl.BlockSpec((1, 1), lambda i: (0, 0)),) + (blk,) * 4 + tuple(
        whole(w) for w in shapes[1:] for _ in range(4))
    gx_blk = pl.BlockSpec((SEQ // steps, D_MODEL), lambda i: (i, 0))
    outs = pl.pallas_call(
        body, name="adamw_all", grid=(steps,),
        out_shape=out_shape + (jax.ShapeDtypeStruct(grad_x.shape, F32),),
        in_specs=in_specs + [gx_blk], out_specs=out_specs + (gx_blk,),
        compiler_params=pltpu.CompilerParams(dimension_semantics=("arbitrary",), vmem_limit_bytes=VMEM_LIMIT),
    )(tot, g_w_in, g_w_out, *arrays, grad_x)
    return [outs[0]] + [outs[1 + 4 * j:5 + 4 * j] for j in range(2 + n)] + [outs[-1]]


def _local_step(x2d, tgt, proj, lb_logits, cw, ga, gcn, w_out, gf):
    g64 = _group_matrix(HEAD, CONV_GROUP)
    aux, states, dx2, dmixed, gwo, part_out = _mix_out(proj, lb_logits, cw, ga, gcn, g64, w_out, x2d, gf, tgt)
    dproj, part_mix = _mix_bwd(proj, aux, states, dmixed, lb_logits, cw, ga, gcn, g64)
    return dproj, dx2, gwo.reshape(N_SHARD, WO_ROWS, D_MODEL), part_out, part_mix


def kernel(x, norm_gain, w_in, lb_logits, conv_w, hgrn_norm_gain, conv_norm_gain, w_out, final_norm_gain, loss_target, m_norm_gain, m_w_in, m_lb_logits, m_conv_w, m_hgrn_norm_gain, m_conv_norm_gain, m_w_out, m_final_norm_gain, v_norm_gain, v_w_in, v_lb_logits, v_conv_w, v_hgrn_norm_gain, v_conv_norm_gain, v_w_out, v_final_norm_gain):
    k = 2 * lax.axis_index("x") + lax.axis_index("y")
    kidx = jnp.reshape(k, (1,)).astype(jnp.int32)
    row = lambda a: a.reshape(1, D_MODEL)
    taps = lambda a: a.reshape(1, 3 * HEAD)
    h, proj, wg, cw = _gather_proj(kidx, x[0], norm_gain, w_in, taps(conv_w))
    dproj, dx2, gwo, part_out, part_mix = _local_step(
        x[0], loss_target[0], proj, lb_logits, cw, hgrn_norm_gain, conv_norm_gain, w_out, row(final_norm_gain))
    rgrad_x, rg_w_in, rg_w_out, tot = _bwd_tail(kidx, h, dproj, wg, gwo, x[0], dx2, norm_gain, part_out, part_mix)

    (loss, (g_w_in, d_w_in, nm_w_in, nv_w_in), (g_w_out, d_w_out, nm_w_out, nv_w_out),
     (g_norm_gain, d_ng, nm_ng, nv_ng), (g_final, d_fg, nm_fg, nv_fg), (g_hgrn, d_hg, nm_hg, nv_hg),
     (g_convn, d_cg, nm_cg, nv_cg), (g_lb, d_lb, nm_lb, nv_lb), (g_conv_w, d_cw, nm_cw, nv_cw),
     grad_x) = _adamw_all(
        tot, rg_w_in, rg_w_out,
        [(w_in[0], m_w_in[0], v_w_in[0]), (w_out[0], m_w_out[0], v_w_out[0])],
        [(norm_gain, m_norm_gain, v_norm_gain),
         (row(final_norm_gain), row(m_final_norm_gain), row(v_final_norm_gain)),
         (hgrn_norm_gain, m_hgrn_norm_gain, v_hgrn_norm_gain),
         (conv_norm_gain, m_conv_norm_gain, v_conv_norm_gain),
         (lb_logits, m_lb_logits, v_lb_logits),
         (taps(conv_w), taps(m_conv_w), taps(v_conv_w))],
        rgrad_x)
    flat = lambda a: a.reshape(D_MODEL)
    untap = lambda a: a.reshape(1, 3, HEAD)
    return (loss.reshape(()), grad_x[None],
            g_norm_gain, g_w_in[None], g_lb, untap(g_conv_w), g_hgrn, g_convn, g_w_out[None], flat(g_final),
            d_ng, d_w_in[None], d_lb, untap(d_cw), d_hg, d_cg, d_w_out[None], flat(d_fg),
            nm_ng, nm_w_in[None], nm_lb, untap(nm_cw), nm_hg, nm_cg, nm_w_out[None], flat(nm_fg),
            nv_ng, nv_w_in[None], nv_lb, untap(nv_cw), nv_hg, nv_cg, nv_w_out[None], flat(nv_fg))
```

```python
import jax
import jax.numpy as jnp
import numpy as np
from jax import lax
from jax.experimental import pallas as pl
from jax.experimental.pallas import tpu as pltpu

F32 = jnp.float32
BF16 = jnp.bfloat16
MESH = pl.DeviceIdType.MESH

SEQ = 2048
D_MODEL = 1024
D_HGRN = 512
D_CONV = 512
HEAD = 128
N_HEADS = 4
CHUNK = 64
CONV_GROUP = 64
N_SHARD = 4
SHARD_COLS = 1024
WO_ROWS = 256
EPS = 1e-6
TB = 256
NCB = TB // CHUNK
N_CHUNKS = SEQ // CHUNK
N_DEV = 8
COLLECTIVE_GATHER, COLLECTIVE_MIX_OUT, COLLECTIVE_TAIL = 1, 0, 2
AUX_O, AUX_CV, AUX_B, AUX_COLS = 0, 512, 1024, 1536

ADAM_LR = 0.001
ADAM_B1 = 0.9
ADAM_B2 = 0.999
ADAM_EPS = 1e-08
ADAM_WD = 0.01
ADAM_STEP = 10

VMEM_LIMIT = 56 * 1024 * 1024


def _dot(a, b):
    return jnp.dot(a, b, preferred_element_type=F32)


def _dot_nt(a, b):
    return lax.dot_general(a, b, (((1,), (1,)), ((), ())), preferred_element_type=F32)


def _dot_tn(a, b):
    return lax.dot_general(a, b, (((0,), (0,)), ((), ())), preferred_element_type=F32)


def _split_bf16(x, n):
    parts = []
    r = x
    for _ in range(n):
        p = r.astype(BF16)
        parts.append(p)
        r = r - p.astype(F32)
    return parts


def _exact_left(m, x, n=3):
    acc = None
    for p in _split_bf16(x, n):
        t = _dot(m, p)
        acc = t if acc is None else acc + t
    return acc


def _exact_left_many(m, xs, n=3):
    parts = [_split_bf16(x, n) for x in xs]
    accs = [None] * len(xs)
    for i in range(n):
        for j in range(len(xs)):
            t = _dot(m, parts[j][i])
            accs[j] = t if accs[j] is None else accs[j] + t
    return accs


def _group_mean_many(xs, gmat, n=2):
    parts = [_split_bf16(x, n) for x in xs]
    accs = [None] * len(xs)
    for i in range(n):
        for j in range(len(xs)):
            t = _dot(parts[j][i], gmat)
            accs[j] = t if accs[j] is None else accs[j] + t
    return accs


def _group_mean(x, gmat, n=2):
    w = gmat.shape[0]
    outs = []
    for c0 in range(0, x.shape[1], w):
        acc = None
        for p in _split_bf16(x[:, c0:c0 + w], n):
            t = _dot(p, gmat)
            acc = t if acc is None else acc + t
        outs.append(acc)
    return jnp.concatenate(outs, axis=1)


def _sigmoid(x):
    return 1.0 / (1.0 + jnp.exp(-x))


def _lower_bound(lbl):
    l0 = lbl[0:1, :]
    l1 = lbl[1:2, :]
    m = jnp.maximum(l0, l1)
    e0 = jnp.exp(l0 - m)
    e1 = jnp.exp(l1 - m)
    return e0 / (e0 + e1)


def _tri(lower):
    r = lax.broadcasted_iota(jnp.int32, (CHUNK, CHUNK), 0)
    c = lax.broadcasted_iota(jnp.int32, (CHUNK, CHUNK), 1)
    return jnp.where((c <= r) if lower else (c >= r), 1.0, 0.0).astype(BF16)


def _causal():
    r = lax.broadcasted_iota(jnp.int32, (CHUNK, CHUNK), 0)
    c = lax.broadcasted_iota(jnp.int32, (CHUNK, CHUNK), 1)
    return c <= r


def _shift_down(x, sh, prev_tail):
    r = pltpu.roll(x, sh, 0)
    pt = pltpu.roll(prev_tail, sh, 0)
    rows = lax.broadcasted_iota(jnp.int32, prev_tail.shape, 0)
    top = jnp.where(rows < sh, pt, r[0:8])
    return jnp.concatenate([top, r[8:]], axis=0)


def _shift_up(x, sh, next_head):
    n = x.shape[0]
    r = pltpu.roll(x, n - sh, 0)
    nh = pltpu.roll(next_head, 8 - sh, 0)
    rows = lax.broadcasted_iota(jnp.int32, next_head.shape, 0)
    bot = jnp.where(rows >= 8 - sh, nh, r[n - 8:])
    return jnp.concatenate([r[:n - 8], bot], axis=0)


def _group_matrix(width, group):
    r = np.arange(width)[:, None] // group
    c = np.arange(width)[None, :] // group
    return jnp.asarray(np.where(r == c, 1.0 / group, 0.0), dtype=BF16)


TG = 1024
SEM_W, SEM_CW, SEM_W_FWD, N_SEM = 0, 4, 7, 11


def _gather_proj(kidx, x2d, g1, w_in, conv_w):
    half_w = D_MODEL // 2
    half_c = SHARD_COLS // 2
    nt = SEQ // TG
    n_steps = 2 * N_SHARD

    def body(k_ref, x_ref, g_ref, w_ref, cw_ref, h_ref, p_ref, wg_out, cwg_out,
             wg_v, cwg_v, send_sems, recv_sems, out_sems):
        s, t = pl.program_id(0), pl.program_id(1)
        x, y, c = lax.axis_index("x"), lax.axis_index("y"), lax.axis_index("c")
        k = 2 * x + y
        sibling = (x, y, 1 - c)
        chips = [(1 - x, y), (x, 1 - y), (1 - x, 1 - y)]
        kjs = [2 * cx + cy for cx, cy in chips]
        diag = (*chips[2], c)

        def w_half(kk, cc):
            return wg_v.at[kk, pl.ds(cc * half_w, half_w), :]

        def w_quarter(kk, cc, piece):
            return wg_v.at[kk, pl.ds(cc * half_w, half_w), piece * half_c:(piece + 1) * half_c]

        def cw_of(kk):
            return cwg_v.at[:, pl.ds(pl.multiple_of(kk * HEAD, HEAD), HEAD)]

        def copy(sem, ref, to):
            return pltpu.make_async_remote_copy(
                src_ref=ref, dst_ref=ref, send_sem=send_sems.at[sem], recv_sem=recv_sems.at[sem],
                device_id=to, device_id_type=MESH)

        def at_step(sv, tv):
            return pl.when((s == sv) & (t == tv))

        w_direct = ([copy(SEM_W + j, w_half(k, c), (*chips[j], c)) for j in range(2)]
                    + [copy(SEM_W + 2 + p, w_quarter(k, c, p), diag) for p in range(2)])
        cw_direct = [copy(SEM_CW + j, cw_of(k), (*chip, c)) for j, chip in enumerate(chips)]
        w_passed = ([copy(SEM_W_FWD + j, w_half(kjs[j], c), sibling) for j in range(2)]
                    + [copy(SEM_W_FWD + 2 + p, w_quarter(kjs[2], c, p), sibling) for p in range(2)])
        stores = ([pltpu.make_async_copy(wg_v.at[kk], wg_out.at[kk], out_sems.at[i])
                   for i, kk in enumerate([k] + kjs)]
                  + [pltpu.make_async_copy(cwg_v, cwg_out, out_sems.at[4])])

        @at_step(0, 0)
        def _():
            barrier = pltpu.get_barrier_semaphore()
            for peer in [sibling] + [(*chip, c) for chip in chips]:
                pl.semaphore_signal(barrier, inc=1, device_id=peer, device_id_type=MESH)
            wg_v[k] = w_ref[0].astype(BF16)
            mine = pl.ds(pl.multiple_of(k * HEAD, HEAD), HEAD)
            cwg_v[:, mine] = jnp.zeros((8, HEAD), F32)
            for tap in range(3):
                cwg_v[tap:tap + 1, mine] = cw_ref[:, tap * HEAD:(tap + 1) * HEAD]
            pl.semaphore_wait(barrier, 4)
            for cp in w_direct + cw_direct:
                cp.start()
            stores[0].start()

        @at_step(2, 0)
        def _():
            for j in range(2):
                copy(SEM_W + j, w_half(kjs[j], c), sibling).wait_recv()
                w_passed[j].start()
            copy(SEM_W_FWD, w_half(kjs[0], 1 - c), sibling).wait_recv()
            stores[1].start()

        @at_step(4, 0)
        def _():
            copy(SEM_W_FWD + 1, w_half(kjs[1], 1 - c), sibling).wait_recv()
            stores[2].start()

        for p in range(2):
            @at_step(6 + p, 0)
            def _(p=p):
                copy(SEM_W + 2 + p, w_quarter(kjs[2], c, p), sibling).wait_recv()
                w_passed[2 + p].start()
                copy(SEM_W_FWD + 2 + p, w_quarter(kjs[2], 1 - c, p), sibling).wait_recv()

        rows = pl.ds(pl.multiple_of(t * TG, TG), TG)

        @pl.when(s == 0)
        def _():
            xv = x_ref[...]
            r = lax.rsqrt(jnp.mean(xv * xv, axis=-1, keepdims=True) + EPS)
            h_ref[rows, :] = (xv * r * g_ref[...]).astype(BF16)

        sh = s >> 1
        js = k ^ (((sh & 1) << 1) | (sh >> 1))
        for piece in range(2):
            @pl.when((s & 1) == piece)
            def _(piece=piece):
                p_ref[...] = _dot(h_ref[rows, :], wg_v[js, :, piece * half_c:(piece + 1) * half_c])

        @at_step(n_steps - 1, nt - 1)
        def _():
            stores[3].start()
            for j in range(3):
                copy(SEM_CW + j, cw_of(kjs[j]), sibling).wait_recv()
            stores[4].start()
            for cp in w_direct + cw_direct + w_passed:
                cp.wait_send()
            for st in stores:
                st.wait()

    def x_map(s, t, kr):
        return (jnp.where(s == 0, t, nt - 1), 0)

    def p_map(s, t, kr):
        sh = s >> 1
        return (t, 2 * (kr[0] ^ (((sh & 1) << 1) | (sh >> 1))) + (s & 1))

    hbm = pl.BlockSpec(memory_space=pl.ANY)
    grid_spec = pltpu.PrefetchScalarGridSpec(
        num_scalar_prefetch=1, grid=(n_steps, nt),
        in_specs=[pl.BlockSpec((TG, D_MODEL), x_map),
                  pl.BlockSpec((1, D_MODEL), lambda s, t, kr: (0, 0)),
                  pl.BlockSpec((1, D_MODEL, SHARD_COLS), lambda s, t, kr: (0, 0, 0)),
                  pl.BlockSpec((1, 3 * HEAD), lambda s, t, kr: (0, 0))],
        out_specs=(pl.BlockSpec((SEQ, D_MODEL), lambda s, t, kr: (0, 0)),
                   pl.BlockSpec((TG, half_c), p_map), hbm, hbm),
        scratch_shapes=[pltpu.VMEM((N_SHARD, D_MODEL, SHARD_COLS), BF16),
                        pltpu.VMEM((8, D_CONV), F32),
                        pltpu.SemaphoreType.DMA((N_SEM,)), pltpu.SemaphoreType.DMA((N_SEM,)),
                        pltpu.SemaphoreType.DMA((5,))])
    return pl.pallas_call(
        body, name="gather_proj", grid_spec=grid_spec,
        out_shape=(jax.ShapeDtypeStruct((SEQ, D_MODEL), BF16),
                   jax.ShapeDtypeStruct((SEQ, N_SHARD * SHARD_COLS), F32),
                   jax.ShapeDtypeStruct((N_SHARD, D_MODEL, SHARD_COLS), BF16),
                   jax.ShapeDtypeStruct((8, D_CONV), F32)),
        compiler_params=pltpu.CompilerParams(dimension_semantics=("arbitrary", "arbitrary"),
                                             vmem_limit_bytes=VMEM_LIMIT, collective_id=COLLECTIVE_GATHER),
    )(kidx, x2d, g1, w_in, conv_w)


LAG = 6


def _mix_out(proj, lb_logits, cw, ga, gcn, g64, w_out, x2d, gf, tgt):
    half_o = WO_ROWS // 2
    nblk = SEQ // TB
    n_steps = nblk + LAG

    def body(p_ref, lbl_ref, cw_ref, ga_ref, gcn_ref, g64_ref, wo_ref, x_ref, gf_ref, t_ref,
             aux_ref, sto_ref, dx2_ref, dm_ref, gwo_ref, part_ref,
             st_ref, tail_ref, wog_v, stage, ring, acc_ref, send_sems, recv_sems):
        i = pl.program_id(0)
        x, y, c = lax.axis_index("x"), lax.axis_index("y"), lax.axis_index("c")
        k = 2 * x + y
        sibling = (x, y, 1 - c)
        chips = [(1 - x, y), (x, 1 - y), (1 - x, 1 - y)]
        kjs = [2 * cx + cy for cx, cy in chips]

        def wo_half(kk, cc):
            return wog_v.at[pl.ds(pl.multiple_of(kk * WO_ROWS + cc * half_o, half_o), half_o), :]

        def copy(sem, ref, to):
            return pltpu.make_async_remote_copy(
                src_ref=ref, dst_ref=ref, send_sem=send_sems.at[sem], recv_sem=recv_sems.at[sem],
                device_id=to, device_id_type=MESH)

        wo_direct = [copy(j, wo_half(k, c), (*chip, c)) for j, chip in enumerate(chips)]
        wo_passed = [copy(3 + j, wo_half(kj, c), sibling) for j, kj in enumerate(kjs)]

        @pl.when(i == 0)
        def _():
            barrier = pltpu.get_barrier_semaphore()
            for peer in [sibling] + [(*chip, c) for chip in chips]:
                pl.semaphore_signal(barrier, inc=1, device_id=peer, device_id_type=MESH)
            st_ref[...] = jnp.zeros_like(st_ref)
            tail_ref[...] = jnp.zeros_like(tail_ref)
            acc_ref[...] = jnp.zeros_like(acc_ref)
            part_ref[...] = jnp.zeros_like(part_ref)
            wog_v[pl.ds(pl.multiple_of(k * WO_ROWS, WO_ROWS), WO_ROWS), :] = wo_ref[0].astype(BF16)
            pl.semaphore_wait(barrier, 4)
            for cp in wo_direct:
                cp.start()

        @pl.when(i == LAG - 1)
        def _():
            for j in range(3):
                copy(j, wo_half(kjs[j], c), sibling).wait_recv()
                wo_passed[j].start()

        @pl.when(i == LAG)
        def _():
            for j in range(3):
                copy(3 + j, wo_half(kjs[j], 1 - c), sibling).wait_recv()

        lb = _lower_bound(lbl_ref[...])
        tri = _tri(True)
        causal = _causal()
        g64m = g64_ref[...]
        heads = range(N_HEADS)
        cs = [slice(hd * HEAD, (hd + 1) * HEAD) for hd in heads]
        col = lambda base, hd: slice(base + hd * HEAD, base + (hd + 1) * HEAD)

        def mix_chunk(n):
            sl = pl.ds(n * CHUNK, CHUNK)
            sg = [_sigmoid(p_ref[sl, col(512, hd)]) for hd in heads]
            f = [lb[:, cs[hd]] + (1.0 - lb[:, cs[hd]]) * sg[hd] for hd in heads]
            bc = _exact_left_many(tri, [jnp.log(f[hd]) for hd in heads])
            for hd in heads:
                aux_ref[sl, col(AUX_B, hd)] = bc[hd]
            g = [bc[hd][CHUNK - 1:CHUNK, :] for hd in heads]
            qd = [(p_ref[sl, col(0, hd)] * jnp.exp(bc[hd])).astype(BF16) for hd in heads]
            kk = [1.0 - f[hd] for hd in heads]
            ki = [(kk[hd] * jnp.exp(-bc[hd])).astype(BF16) for hd in heads]
            ke = [(kk[hd] * jnp.exp(g[hd] - bc[hd])).astype(BF16) for hd in heads]
            vb = [p_ref[sl, col(1024, hd)].astype(BF16) for hd in heads]
            st = [st_ref[hd] for hd in heads]
            st_b = [a.astype(BF16) for a in st]
            for hd in heads:
                sto_ref[n, hd] = st_b[hd]
            scm = [_dot_nt(qd[hd], ki[hd]) for hd in heads]
            inter = [_dot_nt(qd[hd], st_b[hd]) for hd in heads]
            upd = [_dot_tn(vb[hd], ke[hd]) for hd in heads]
            intra = [_dot(jnp.where(causal, scm[hd], 0.0).astype(BF16), vb[hd]) for hd in heads]
            for hd in heads:
                st_ref[hd] = st[hd] * jnp.exp(g[hd]) + upd[hd]
                o = intra[hd] + inter[hd]
                aux_ref[sl, col(AUX_O, hd)] = o
                ra = lax.rsqrt(jnp.mean(o * o, axis=-1, keepdims=True) + EPS)
                za = p_ref[sl, col(1536, hd)]
                stage[sl, cs[hd]] = (o * ra * ga_ref[:, cs[hd]] * (za * _sigmoid(za))).astype(BF16)
            yb = []
            for hd in heads:
                cu = p_ref[sl, col(3072, hd)] * p_ref[sl, col(2048, hd)]
                tail = tail_ref[:, cs[hd]]
                cv = (cw_ref[0:1, cs[hd]] * _shift_down(cu, 2, tail) + cw_ref[1:2, cs[hd]] * _shift_down(cu, 1, tail)
                      + cw_ref[2:3, cs[hd]] * cu)
                tail_ref[:, cs[hd]] = cu[CHUNK - 8:, :]
                aux_ref[sl, col(AUX_CV, hd)] = cv
                yb.append(p_ref[sl, col(2560, hd)] * cv)
            ms = _group_mean_many([y * y for y in yb], g64m)
            for hd in heads:
                rb = lax.rsqrt(ms[hd] + EPS)
                zb = p_ref[sl, col(3584, hd)]
                stage[sl, col(512, hd)] = (yb[hd] * rb * gcn_ref[:, cs[hd]] * (zb * _sigmoid(zb))).astype(BF16)

        def step(mix, project):
            if project:
                mixed_b = ring[pl.ds(pl.multiple_of((i - LAG) * TB, TB), TB), :]
                y = _dot(mixed_b, wog_v[...])
            if mix:
                mix_chunk(0)
            if project:
                x2 = x_ref[...] + y
                r2 = lax.rsqrt(jnp.mean(x2 * x2, axis=-1, keepdims=True) + EPS)
                n2 = x2 * r2
                gfv = gf_ref[...]
                err = n2 * gfv - t_ref[...]
                loss = 0.5 * jnp.sum(jnp.mean(err * err, axis=-1, keepdims=True), axis=0, keepdims=True)
                dy = err * (1.0 / D_MODEL)
                part_ref[1:2, :] += jnp.sum(dy * n2, axis=0, keepdims=True)
                part_ref[7:8, :] += jnp.broadcast_to(loss, (1, D_MODEL))
                dn = dy * gfv
                dx2 = r2 * (dn - n2 * jnp.mean(dn * n2, axis=-1, keepdims=True))
                dx2_ref[...] = dx2
                dx2_b = dx2.astype(BF16)
            if mix:
                mix_chunk(1)
            if project:
                dm_ref[...] = _dot_nt(dx2_b, wog_v[...])
            if mix:
                mix_chunk(2)
            if project:
                acc_ref[...] += _dot_tn(mixed_b, dx2_b)
            if mix:
                mix_chunk(3)
                ring[pl.ds(pl.multiple_of(i * TB, TB), TB), :] = stage[...]

        @pl.when(i < LAG)
        def _():
            step(True, False)

        @pl.when((i >= LAG) & (i < nblk))
        def _():
            step(True, True)

        @pl.when(i >= nblk)
        def _():
            step(False, True)

        @pl.when(i == n_steps - 1)
        def _():
            gwo_ref[...] = acc_ref[...].astype(BF16)
            for cp in wo_direct + wo_passed:
                cp.wait_send()

    assert NCB == 4
    row = lambda w: pl.BlockSpec((1, w), lambda i: (0, 0))
    mix_blk = lambda i: jnp.minimum(i, nblk - 1)
    out_blk = lambda i: jnp.clip(i - LAG, 0, nblk - 1)
    tok = lambda: pl.BlockSpec((TB, D_MODEL), lambda i: (out_blk(i), 0))
    return pl.pallas_call(
        body, name="mix_out", grid=(n_steps,),
        out_shape=(jax.ShapeDtypeStruct((SEQ, AUX_COLS), F32),
                   jax.ShapeDtypeStruct((N_CHUNKS, N_HEADS, HEAD, HEAD), BF16),
                   jax.ShapeDtypeStruct((SEQ, D_MODEL), F32),
                   jax.ShapeDtypeStruct((SEQ, D_MODEL), F32),
                   jax.ShapeDtypeStruct((D_MODEL, D_MODEL), BF16),
                   jax.ShapeDtypeStruct((8, D_MODEL), F32)),
        in_specs=[pl.BlockSpec((TB, 4096), lambda i: (jnp.minimum(i, nblk - 1), 0)),
                  pl.BlockSpec((2, D_HGRN), lambda i: (0, 0)),
                  pl.BlockSpec((8, D_CONV), lambda i: (0, 0)),
                  row(D_HGRN), row(D_CONV),
                  pl.BlockSpec((HEAD, HEAD), lambda i: (0, 0)),
                  pl.BlockSpec((1, WO_ROWS, D_MODEL), lambda i: (0, 0, 0)),
                  tok(), row(D_MODEL), tok()],
        out_specs=(pl.BlockSpec((TB, AUX_COLS), lambda i: (mix_blk(i), 0)),
                   pl.BlockSpec((NCB, N_HEADS, HEAD, HEAD), lambda i: (mix_blk(i), 0, 0, 0)),
                   tok(), tok(),
                   pl.BlockSpec((D_MODEL, D_MODEL), lambda i: (0, 0)),
                   pl.BlockSpec((8, D_MODEL), lambda i: (0, 0))),
        scratch_shapes=[pltpu.VMEM((N_HEADS, HEAD, HEAD), F32), pltpu.VMEM((8, D_CONV), F32),
                        pltpu.VMEM((D_MODEL, D_MODEL), BF16), pltpu.VMEM((TB, D_MODEL), BF16),
                        pltpu.VMEM((SEQ, D_MODEL), BF16), pltpu.VMEM((D_MODEL, D_MODEL), F32),
                        pltpu.SemaphoreType.DMA((6,)), pltpu.SemaphoreType.DMA((6,))],
        compiler_params=pltpu.CompilerParams(dimension_semantics=("arbitrary",), vmem_limit_bytes=VMEM_LIMIT,
                                             collective_id=COLLECTIVE_MIX_OUT),
    )(proj, lb_logits, cw, ga, gcn, g64, w_out, x2d, gf, tgt)


def _mix_bwd(proj, aux, states, dmixed, lb_logits, cw, ga, gcn, g64):
    nblk = SEQ // TB

    def body(p_ref, aux_ref, st_ref, dm_ref, lbl_ref, cw_ref, ga_ref, gcn_ref, g64_ref,
             dp_ref, part_ref, dst_ref, head_ref, dlb_ref):
        i = pl.program_id(0)

        @pl.when(i == 0)
        def _():
            dst_ref[...] = jnp.zeros_like(dst_ref)
            head_ref[...] = jnp.zeros_like(head_ref)
            part_ref[...] = jnp.zeros_like(part_ref)
            dlb_ref[...] = jnp.zeros_like(dlb_ref)

        lb = _lower_bound(lbl_ref[...])
        triu = _tri(False)
        causal = _causal()
        g64m = g64_ref[...]
        rowsum = lambda a: jnp.sum(a, axis=0, keepdims=True)
        heads = range(N_HEADS)
        cs = [slice(hd * HEAD, (hd + 1) * HEAD) for hd in heads]
        col = lambda base, hd: slice(base + hd * HEAD, base + (hd + 1) * HEAD)
        for n in reversed(range(NCB)):
            sl = pl.ds(n * CHUNK, CHUNK)
            cvv = [aux_ref[sl, col(AUX_CV, hd)] for hd in heads]
            gb = [p_ref[sl, col(2560, hd)] for hd in heads]
            yb = [gb[hd] * cvv[hd] for hd in heads]
            ms = _group_mean_many([y * y for y in yb], g64m)
            rb, nb, dnb = [], [], []
            for hd in heads:
                rb.append(lax.rsqrt(ms[hd] + EPS))
                nb.append(yb[hd] * rb[hd])
                zb = p_ref[sl, col(3584, hd)]
                sgb = _sigmoid(zb)
                dmb = dm_ref[sl, col(512, hd)]
                silu = zb * sgb
                dgate = dmb * gcn_ref[:, cs[hd]]
                part_ref[2:3, col(512, hd)] += rowsum(dmb * nb[hd] * silu)
                dp_ref[sl, col(3584, hd)] = (dgate * nb[hd] * (sgb + silu * (1.0 - sgb))).astype(BF16)
                dnb.append(dgate * silu)
            mdn = _group_mean_many([dnb[hd] * nb[hd] for hd in heads], g64m)
            for hd in heads:
                dyb = rb[hd] * (dnb[hd] - nb[hd] * mdn[hd])
                dp_ref[sl, col(2560, hd)] = (dyb * cvv[hd]).astype(BF16)
                dcv = dyb * gb[hd]
                head = head_ref[:, cs[hd]]
                dcv1 = _shift_up(dcv, 1, head)
                dcv2 = _shift_up(dcv, 2, head)
                head_ref[:, cs[hd]] = dcv[0:8, :]
                u = p_ref[sl, col(2048, hd)]
                gc = p_ref[sl, col(3072, hd)]
                cu = gc * u
                part_ref[4:5, cs[hd]] += rowsum(dcv2 * cu)
                part_ref[5:6, cs[hd]] += rowsum(dcv1 * cu)
                part_ref[6:7, cs[hd]] += rowsum(dcv * cu)
                dcu = cw_ref[2:3, cs[hd]] * dcv + cw_ref[1:2, cs[hd]] * dcv1 + cw_ref[0:1, cs[hd]] * dcv2
                dp_ref[sl, col(3072, hd)] = (dcu * u).astype(BF16)
                dp_ref[sl, col(2048, hd)] = (dcu * gc).astype(BF16)
            do_b = []
            for hd in heads:
                ov = aux_ref[sl, col(AUX_O, hd)]
                ra = lax.rsqrt(jnp.mean(ov * ov, axis=-1, keepdims=True) + EPS)
                na = ov * ra
                za = p_ref[sl, col(1536, hd)]
                sga = _sigmoid(za)
                dma = dm_ref[sl, cs[hd]]
                silu = za * sga
                dgate = dma * ga_ref[:, cs[hd]]
                part_ref[2:3, cs[hd]] += rowsum(dma * na * silu)
                dp_ref[sl, col(1536, hd)] = (dgate * na * (sga + silu * (1.0 - sga))).astype(BF16)
                dna = dgate * silu
                do_b.append((ra * (dna - na * jnp.mean(dna * na, axis=-1, keepdims=True))).astype(BF16))
            s = [_sigmoid(p_ref[sl, col(512, hd)]) for hd in heads]
            f = [lb[:, cs[hd]] + (1.0 - lb[:, cs[hd]]) * s[hd] for hd in heads]
            bc = [aux_ref[sl, col(AUX_B, hd)] for hd in heads]
            g = [bc[hd][CHUNK - 1:CHUNK, :] for hd in heads]
            eb = [jnp.exp(bc[hd]) for hd in heads]
            enb = [jnp.exp(-bc[hd]) for hd in heads]
            eg = [jnp.exp(g[hd] - bc[hd]) for hd in heads]
            dec = [jnp.exp(g[hd]) for hd in heads]
            qd = [p_ref[sl, cs[hd]] * eb[hd] for hd in heads]
            kk = [1.0 - f[hd] for hd in heads]
            ki = [kk[hd] * enb[hd] for hd in heads]
            ke = [kk[hd] * eg[hd] for hd in heads]
            qd_b = [a.astype(BF16) for a in qd]
            ki_b = [a.astype(BF16) for a in ki]
            ke_b = [a.astype(BF16) for a in ke]
            vb = [p_ref[sl, col(1024, hd)].astype(BF16) for hd in heads]
            st_b = [st_ref[n, hd] for hd in heads]
            dst = [dst_ref[hd] for hd in heads]
            dst_b = [a.astype(BF16) for a in dst]
            scm = [_dot_nt(qd_b[hd], ki_b[hd]) for hd in heads]
            amm = [_dot_nt(do_b[hd], vb[hd]) for hd in heads]
            dqd2 = [_dot(do_b[hd], st_b[hd]) for hd in heads]
            dke = [_dot(vb[hd], dst_b[hd]) for hd in heads]
            dv2 = [_dot_nt(ke_b[hd], dst_b[hd]) for hd in heads]
            dsu = [_dot_tn(do_b[hd], qd_b[hd]) for hd in heads]
            sc = [jnp.where(causal, scm[hd], 0.0).astype(BF16) for hd in heads]
            am = [jnp.where(causal, amm[hd], 0.0).astype(BF16) for hd in heads]
            dqd1 = [_dot(am[hd], ki_b[hd]) for hd in heads]
            dki = [_dot_tn(am[hd], qd_b[hd]) for hd in heads]
            dv1 = [_dot_tn(sc[hd], do_b[hd]) for hd in heads]
            db, dgv, dkk = [], [], []
            for hd in heads:
                dqd = dqd1[hd] + dqd2[hd]
                ddec = rowsum(dst[hd] * st_b[hd].astype(F32))
                dst_ref[hd] = dst[hd] * dec[hd] + dsu[hd]
                dp_ref[sl, cs[hd]] = (dqd * eb[hd]).astype(BF16)
                dp_ref[sl, col(1024, hd)] = (dv1[hd] + dv2[hd]).astype(BF16)
                dke_eg = dke[hd] * eg[hd]
                dkk.append(dki[hd] * enb[hd] + dke_eg)
                db.append(dqd * qd[hd] - kk[hd] * dkk[hd])
                dgv.append(rowsum(kk[hd] * dke_eg) + ddec * dec[hd])
            rc = _exact_left_many(triu, db, 2)
            for hd in heads:
                df = (rc[hd] + dgv[hd]) / f[hd] - dkk[hd]
                one_s = 1.0 - s[hd]
                dlb_ref[:, cs[hd]] += rowsum(df * one_s)
                dp_ref[sl, col(512, hd)] = (df * (1.0 - lb[:, cs[hd]]) * s[hd] * one_s).astype(BF16)

        @pl.when(i == nblk - 1)
        def _():
            row = dlb_ref[...] * lb * (1.0 - lb)
            part_ref[3:4, 0:D_HGRN] = row
            part_ref[3:4, D_HGRN:] = -row

    rev = lambda w: pl.BlockSpec((TB, w), lambda i: (nblk - 1 - i, 0))
    row = lambda w: pl.BlockSpec((1, w), lambda i: (0, 0))
    return pl.pallas_call(
        body, name="mix_bwd", grid=(nblk,),
        out_shape=(jax.ShapeDtypeStruct((SEQ, 4096), BF16),
                   jax.ShapeDtypeStruct((8, D_MODEL), F32)),
        in_specs=[rev(4096), rev(AUX_COLS),
                  pl.BlockSpec((NCB, N_HEADS, HEAD, HEAD), lambda i: (nblk - 1 - i, 0, 0, 0)),
                  rev(D_MODEL),
                  pl.BlockSpec((2, D_HGRN), lambda i: (0, 0)),
                  pl.BlockSpec((8, D_CONV), lambda i: (0, 0)),
                  row(D_HGRN), row(D_CONV),
                  pl.BlockSpec((HEAD, HEAD), lambda i: (0, 0))],
        out_specs=(rev(4096), pl.BlockSpec((8, D_MODEL), lambda i: (0, 0))),
        scratch_shapes=[pltpu.VMEM((N_HEADS, HEAD, HEAD), F32), pltpu.VMEM((8, D_CONV), F32),
                        pltpu.VMEM((1, D_HGRN), F32)],
        compiler_params=pltpu.CompilerParams(dimension_semantics=("arbitrary",), vmem_limit_bytes=VMEM_LIMIT),
    )(proj, aux, states, dmixed, lb_logits, cw, ga, gcn, g64)


TT = 1024
TX = 512
(SEM_D2D, SEM_D2D_O, SEM_ICI, SEM_ICI_O, SEM_FIN, SEM_FIN_O, SEM_SMALL, SEM_VIA, N_SEM_TAIL) = (
    0, 4, 5, 8, 11, 12, 12, 20, 22)


def _bwd_tail(kidx, h, dproj, wg, gwo, x2d, dx2, g1, small_a, small_b):
    hw = D_MODEL // 2
    ho = WO_ROWS // 2
    nt = SEQ // TT
    norm_step = 2 * N_SHARD
    n_steps = norm_step + SEQ // TX // nt

    def body(k_ref, h_ref, dp_ref, w_ref, gwo_ref, x_ref, dx2_ref, g_ref, sm_ref, smb_ref,
             gx_ref, gw_out, gwo_out, osm_ref,
             acc, dh, sendbuf, keep, sibrcv, rcv, merge, sib_o, p_o, rcv_o, res_o, sm_buf, dng,
             send_sems, recv_sems, out_sems):
        s, t = pl.program_id(0), pl.program_id(1)
        x, y, c = lax.axis_index("x"), lax.axis_index("y"), lax.axis_index("c")
        k = 2 * x + y
        me = 4 * x + 2 * y + c
        sibling = (x, y, 1 - c)
        chips = [(1 - x, 1 - y), (1 - x, y), (x, 1 - y)]
        kjs = [2 * cx + cy for cx, cy in chips]
        mine = pl.ds(pl.multiple_of(c * hw, hw), hw)
        other = pl.ds(pl.multiple_of((1 - c) * hw, hw), hw)
        mine_o = pl.ds(pl.multiple_of(c * ho, ho), ho)
        other_o = pl.ds(pl.multiple_of((1 - c) * ho, ho), ho)

        def copy(sem, src, dst, to):
            return pltpu.make_async_remote_copy(
                src_ref=src, dst_ref=dst, send_sem=send_sems.at[sem], recv_sem=recv_sems.at[sem],
                device_id=to, device_id_type=MESH)

        def at_step(sv, tv):
            return pl.when((s == sv) & (t == tv))

        def at_norm_block(b):
            return at_step(norm_step + b // nt, b % nt)

        d2d = [copy(SEM_D2D + sv, sendbuf.at[sv], sibrcv.at[sv], sibling) for sv in range(N_SHARD)]
        d2d_o = copy(SEM_D2D_O, gwo_ref.at[:, other_o, :], sib_o, sibling)
        ici = {sv: copy(SEM_ICI + sv, keep.at[sv], rcv.at[sv - 1], (*chips[sv], c)) for sv in (1, 2)}
        qh = hw // 2
        via = [copy(SEM_VIA, keep.at[0, 0:qh, :], merge.at[1], (*chips[1], c)),
               copy(SEM_VIA + 1, keep.at[0, qh:hw, :], merge.at[0], (*chips[2], c))]
        merged_rows = [slice(qh, hw), slice(0, qh)]
        ici_o = [copy(SEM_ICI_O + sv, p_o.at[kjs[sv]], rcv_o.at[sv], (*chips[sv], c)) for sv in range(3)]
        fin = copy(SEM_FIN, acc.at[mine, :], gw_out.at[mine, :], sibling)
        fin_o = copy(SEM_FIN_O, res_o.at[mine_o, :], res_o.at[mine_o, :], sibling)
        smalls = [copy(SEM_SMALL + m, sm_buf.at[me], sm_buf.at[me],
                       (x ^ (m >> 2), y ^ ((m >> 1) & 1), c ^ (m & 1))) for m in range(1, N_DEV)]
        store_w = pltpu.make_async_copy(acc.at[mine, :], gw_out.at[mine, :], out_sems.at[0])
        store_o = pltpu.make_async_copy(res_o, gwo_out, out_sems.at[1])

        @at_step(0, 0)
        def _():
            barrier = pltpu.get_barrier_semaphore()
            for m in range(1, N_DEV):
                pl.semaphore_signal(barrier, inc=1, device_id=(x ^ (m >> 2), y ^ ((m >> 1) & 1), c ^ (m & 1)),
                                    device_id_type=MESH)
            pl.semaphore_wait(barrier, N_DEV - 1)
            d2d_o.start()

        @at_step(0, 1)
        def _():
            d2d_o.wait_recv()
            for j in range(N_SHARD):
                p_o[j] = (gwo_ref[j, mine_o, :].astype(F32) + sib_o[j].astype(F32)).astype(BF16)
            res_o[mine_o, :] = gwo_ref[k, mine_o, :].astype(F32) + sib_o[k].astype(F32)
            for cp in ici_o:
                cp.start()

        rows = pl.ds(pl.multiple_of(t * TT, TT), TT)

        @pl.when((s < N_SHARD) & (t == 0))
        def _():
            acc[...] = _dot_tn(h_ref[...], dp_ref[...])

        @pl.when((s < N_SHARD) & (t > 0))
        def _():
            acc[...] += _dot_tn(h_ref[...], dp_ref[...])

        for sv in range(N_SHARD):
            @at_step(sv, nt - 1)
            def _(sv=sv):
                sendbuf[sv] = acc[other, :].astype(BF16)
                if sv < 3:
                    keep[sv] = acc[mine, :].astype(BF16)
                d2d[sv].start()

        @at_step(1, 0)
        def _():
            d2d[0].wait_recv()
            keep[0] = (keep[0].astype(F32) + sibrcv[0].astype(F32)).astype(BF16)
            for cp in via:
                cp.start()

        for sv in (1, 2):
            @at_step(sv + 2, 0)
            def _(sv=sv):
                d2d[sv].wait_recv()
                keep[sv] = (keep[sv].astype(F32) + sibrcv[sv].astype(F32)).astype(BF16)
                via[2 - sv].wait_recv()
                rows_m = merged_rows[sv - 1]
                keep[sv, rows_m, :] = (keep[sv, rows_m, :].astype(F32) + merge[sv - 1].astype(F32)).astype(BF16)
                ici[sv].start()

        @pl.when(s == N_SHARD)
        def _():
            dh[rows, :] = _dot_nt(dp_ref[...], w_ref[0])

        @pl.when((s > N_SHARD) & (s < norm_step))
        def _():
            dh[rows, :] += _dot_nt(dp_ref[...], w_ref[0])

        @at_norm_block(0)
        def _():
            d2d[3].wait_recv()
            acc[mine, :] += sibrcv[3].astype(F32)

        @at_norm_block(1)
        def _():
            tot = res_o[mine_o, :]
            for sv in range(3):
                ici_o[sv].wait_recv()
                tot = tot + rcv_o[sv].astype(F32)
            res_o[mine_o, :] = tot
            fin_o.start()

        @at_norm_block(2)
        def _():
            ici[1].wait_recv()
            acc[mine, :] += rcv[0].astype(F32)

        @at_norm_block(SEQ // TX - 2)
        def _():
            ici[2].wait_recv()
            acc[mine, :] += rcv[1].astype(F32)
            fin.start()
            store_w.start()
            fin_o.wait_recv()
            store_o.start()

        @at_norm_block(0)
        def _():
            dng[...] = jnp.zeros_like(dng)

        @pl.when(s >= norm_step)
        def _():
            blk = (s - norm_step) * nt + t
            dhv = dh[pl.ds(pl.multiple_of(blk * TX, TX), TX), :]
            xv = x_ref[...]
            r = lax.rsqrt(jnp.mean(xv * xv, axis=-1, keepdims=True) + EPS)
            xn = xv * r
            dng[...] += jnp.sum(dhv * xn, axis=0, keepdims=True)
            dxn = dhv * g_ref[...]
            gx_ref[...] = dx2_ref[...] + r * (dxn - xn * jnp.mean(dxn * xn, axis=-1, keepdims=True))

        @at_step(n_steps - 1, nt - 1)
        def _():
            sm_buf[me] = sm_ref[...] + smb_ref[...]
            sm_buf[me, 0:1, :] = dng[...]
            for cp in smalls:
                cp.start()
            for m in range(1, N_DEV):
                copy(SEM_SMALL + m, sm_buf.at[0], sm_buf.at[0], sibling).wait_recv()
            tot = sm_buf[0]
            for d in range(1, N_DEV):
                tot = tot + sm_buf[d]
            osm_ref[...] = tot
            fin.wait_recv()
            for cp in d2d + [d2d_o] + via + list(ici.values()) + ici_o + [fin, fin_o] + smalls:
                cp.wait_send()
            store_o.wait()
            store_w.wait()

    def shard_of(s, kr):
        order = jnp.where(s < N_SHARD, s, jnp.where(s < norm_step, s - N_SHARD, 3))
        return kr[0] ^ (3 - order)

    def h_map(s, t, kr):
        return (jnp.where(s < N_SHARD, t, nt - 1), 0)

    def dp_map(s, t, kr):
        return (jnp.where(s < norm_step, t, nt - 1), shard_of(s, kr))

    def w_map(s, t, kr):
        return (shard_of(jnp.maximum(s, N_SHARD), kr), 0, 0)

    def blk_map(s, t, kr):
        return (jnp.where(s < norm_step, 0, (s - norm_step) * nt + t), 0)

    hbm = pl.BlockSpec(memory_space=pl.ANY)
    grid_spec = pltpu.PrefetchScalarGridSpec(
        num_scalar_prefetch=1, grid=(n_steps, nt),
        in_specs=[pl.BlockSpec((TT, D_MODEL), h_map),
                  pl.BlockSpec((TT, SHARD_COLS), dp_map),
                  pl.BlockSpec((1, D_MODEL, SHARD_COLS), w_map),
                  pl.BlockSpec((N_SHARD, WO_ROWS, D_MODEL), lambda s, t, kr: (0, 0, 0),
                               pipeline_mode=pl.Buffered(1)),
                  pl.BlockSpec((TX, D_MODEL), blk_map),
                  pl.BlockSpec((TX, D_MODEL), blk_map),
                  pl.BlockSpec((1, D_MODEL), lambda s, t, kr: (0, 0)),
                  pl.BlockSpec((8, D_MODEL), lambda s, t, kr: (0, 0)),
                  pl.BlockSpec((8, D_MODEL), lambda s, t, kr: (0, 0))],
        out_specs=(pl.BlockSpec((TX, D_MODEL), blk_map), hbm, hbm,
                   pl.BlockSpec((8, D_MODEL), lambda s, t, kr: (0, 0))),
        scratch_shapes=[pltpu.VMEM((D_MODEL, SHARD_COLS), F32), pltpu.VMEM((SEQ, D_MODEL), F32),
                        pltpu.VMEM((N_SHARD, hw, SHARD_COLS), BF16), pltpu.VMEM((3, hw, SHARD_COLS), BF16),
                        pltpu.VMEM((N_SHARD, hw, SHARD_COLS), BF16), pltpu.VMEM((2, hw, SHARD_COLS), BF16),
                        pltpu.VMEM((2, hw // 2, SHARD_COLS), BF16),
                        pltpu.VMEM((N_SHARD, ho, D_MODEL), BF16), pltpu.VMEM((N_SHARD, ho, D_MODEL), BF16),
                        pltpu.VMEM((3, ho, D_MODEL), BF16), pltpu.VMEM((WO_ROWS, D_MODEL), F32),
                        pltpu.VMEM((N_DEV, 8, D_MODEL), F32), pltpu.VMEM((1, D_MODEL), F32),
                        pltpu.SemaphoreType.DMA((N_SEM_TAIL,)), pltpu.SemaphoreType.DMA((N_SEM_TAIL,)),
                        pltpu.SemaphoreType.DMA((2,))])
    return pl.pallas_call(
        body, name="bwd_tail", grid_spec=grid_spec,
        out_shape=(jax.ShapeDtypeStruct((SEQ, D_MODEL), F32),
                   jax.ShapeDtypeStruct((D_MODEL, SHARD_COLS), F32),
                   jax.ShapeDtypeStruct((WO_ROWS, D_MODEL), F32),
                   jax.ShapeDtypeStruct((8, D_MODEL), F32)),
        compiler_params=pltpu.CompilerParams(dimension_semantics=("arbitrary", "arbitrary"),
                                             vmem_limit_bytes=61 * 1024 * 1024, collective_id=COLLECTIVE_TAIL),
    )(kidx, h, dproj, wg, gwo, x2d, dx2, g1, small_a, small_b)


def _adam_update(w, g, m, v):
    nm = ADAM_B1 * m + (1.0 - ADAM_B1) * g
    nv = ADAM_B2 * v + (1.0 - ADAM_B2) * (g * g)
    m_hat = nm / (1.0 - ADAM_B1 ** ADAM_STEP)
    v_hat = nv / (1.0 - ADAM_B2 ** ADAM_STEP)
    return -ADAM_LR * (m_hat / (jnp.sqrt(v_hat) + ADAM_EPS) + ADAM_WD * w), nm, nv


def _adamw_all(tot, g_w_in, g_w_out, big, small, grad_x):
    n = len(small)
    rows = WO_ROWS // 2
    steps = D_MODEL // rows

    def body(tot_ref, *refs):
        gx_ref, gx_out = refs[2 + 3 * (2 + n)], refs[-1]
        gx_out[...] = gx_ref[...]
        ins, outs = refs[:2 + 3 * (2 + n)], refs[3 + 3 * (2 + n):-1]
        g_refs, wmv = ins[:2], ins[2:]
        loss_ref, quads = outs[0], outs[1:]

        def update(j, g):
            w_ref, m_ref, v_ref = wmv[3 * j:3 * j + 3]
            g_ref, d_ref, nm_ref, nv_ref = quads[4 * j:4 * j + 4]
            g_ref[...] = g
            d_ref[...], nm_ref[...], nv_ref[...] = _adam_update(w_ref[...], g, m_ref[...], v_ref[...])

        update(0, g_refs[0][...])

        @pl.when(pl.program_id(0) == 0)
        def _():
            update(1, g_refs[1][...])
            k = 2 * lax.axis_index("x") + lax.axis_index("y")
            mine = pl.ds(pl.multiple_of(k * HEAD, HEAD), HEAD)
            loss_ref[...] = tot_ref[7:8, 0:1]
            grads = [tot_ref[0:1, :], tot_ref[1:2, :], tot_ref[2:3, 0:D_HGRN], tot_ref[2:3, D_HGRN:],
                     jnp.concatenate([tot_ref[3:4, 0:D_HGRN], tot_ref[3:4, D_HGRN:]], axis=0),
                     jnp.concatenate([tot_ref[4 + tap:5 + tap, mine] for tap in range(3)], axis=1)]
            for j, g in enumerate(grads):
                update(2 + j, g)

    whole = lambda a: pl.BlockSpec(a.shape, lambda i: (0, 0))
    blk = pl.BlockSpec((rows, SHARD_COLS), lambda i: (i, 0))
    arrays = [a for triple in big + small for a in triple]
    in_specs = ([whole(tot), blk, whole(g_w_out)] + [blk] * 3 + [whole(a) for a in arrays[3:]])
    shapes = [big[0][0], big[1][0]] + [w for w, _, _ in small]
    out_shape = (jax.ShapeDtypeStruct((1, 1), F32),) + tuple(
        jax.ShapeDtypeStruct(w.shape, F32) for w in shapes for _ in range(4))
    out_specs = (pl.BlockSpec((1, 1), lambda i: (0, 0)),) + (blk,) * 4 + tuple(
        whole(w) for w in shapes[1:] for _ in range(4))
    gx_blk = pl.BlockSpec((SEQ // steps, D_MODEL), lambda i: (i, 0))
    outs = pl.pallas_call(
        body, name="adamw_all", grid=(steps,),
        out_shape=out_shape + (jax.ShapeDtypeStruct(grad_x.shape, F32),),
        in_specs=in_specs + [gx_blk], out_specs=out_specs + (gx_blk,),
        compiler_params=pltpu.CompilerParams(dimension_semantics=("arbitrary",), vmem_limit_bytes=VMEM_LIMIT),
    )(tot, g_w_in, g_w_out, *arrays, grad_x)
    return [outs[0]] + [outs[1 + 4 * j:5 + 4 * j] for j in range(2 + n)] + [outs[-1]]


def _local_step(x2d, tgt, proj, lb_logits, cw, ga, gcn, w_out, gf):
    g64 = _group_matrix(HEAD, CONV_GROUP)
    aux, states, dx2, dmixed, gwo, part_out = _mix_out(proj, lb_logits, cw, ga, gcn, g64, w_out, x2d, gf, tgt)
    dproj, part_mix = _mix_bwd(proj, aux, states, dmixed, lb_logits, cw, ga, gcn, g64)
    return dproj, dx2, gwo.reshape(N_SHARD, WO_ROWS, D_MODEL), part_out, part_mix


def kernel(x, norm_gain, w_in, lb_logits, conv_w, hgrn_norm_gain, conv_norm_gain, w_out, final_norm_gain, loss_target, m_norm_gain, m_w_in, m_lb_logits, m_conv_w, m_hgrn_norm_gain, m_conv_norm_gain, m_w_out, m_final_norm_gain, v_norm_gain, v_w_in, v_lb_logits, v_conv_w, v_hgrn_norm_gain, v_conv_norm_gain, v_w_out, v_final_norm_gain):
    k = 2 * lax.axis_index("x") + lax.axis_index("y")
    kidx = jnp.reshape(k, (1,)).astype(jnp.int32)
    row = lambda a: a.reshape(1, D_MODEL)
    taps = lambda a: a.reshape(1, 3 * HEAD)
    h, proj, wg, cw = _gather_proj(kidx, x[0], norm_gain, w_in, taps(conv_w))
    dproj, dx2, gwo, part_out, part_mix = _local_step(
        x[0], loss_target[0], proj, lb_logits, cw, hgrn_norm_gain, conv_norm_gain, w_out, row(final_norm_gain))
    rgrad_x, rg_w_in, rg_w_out, tot = _bwd_tail(kidx, h, dproj, wg, gwo, x[0], dx2, norm_gain, part_out, part_mix)

    (loss, (g_w_in, d_w_in, nm_w_in, nv_w_in), (g_w_out, d_w_out, nm_w_out, nv_w_out),
     (g_norm_gain, d_ng, nm_ng, nv_ng), (g_final, d_fg, nm_fg, nv_fg), (g_hgrn, d_hg, nm_hg, nv_hg),
     (g_convn, d_cg, nm_cg, nv_cg), (g_lb, d_lb, nm_lb, nv_lb), (g_conv_w, d_cw, nm_cw, nv_cw),
     grad_x) = _adamw_all(
        tot, rg_w_in, rg_w_out,
        [(w_in[0], m_w_in[0], v_w_in[0]), (w_out[0], m_w_out[0], v_w_out[0])],
        [(norm_gain, m_norm_gain, v_norm_gain),
         (row(final_norm_gain), row(m_final_norm_gain), row(v_final_norm_gain)),
         (hgrn_norm_gain, m_hgrn_norm_gain, v_hgrn_norm_gain),
         (conv_norm_gain, m_conv_norm_gain, v_conv_norm_gain),
         (lb_logits, m_lb_logits, v_lb_logits),
         (taps(conv_w), taps(m_conv_w), taps(v_conv_w))],
        rgrad_x)
    flat = lambda a: a.reshape(D_MODEL)
    untap = lambda a: a.reshape(1, 3, HEAD)
    return (loss.reshape(()), grad_x[None],
            g_norm_gain, g_w_in[None], g_lb, untap(g_conv_w), g_hgrn, g_convn, g_w_out[None], flat(g_final),
            d_ng, d_w_in[None], d_lb, untap(d_cw), d_hg, d_cg, d_w_out[None], flat(d_fg),
            nm_ng, nm_w_in[None], nm_lb, untap(nm_cw), nm_hg, nm_cg, nm_w_out[None], flat(nm_fg),
            nv_ng, nv_w_in[None], nv_lb, untap(nv_cw), nv_hg, nv_cg, nv_w_out[None], flat(nv_fg))
```

```python
import jax
import jax.numpy as jnp
import numpy as np
from jax import lax
from jax.experimental import pallas as pl
from jax.experimental.pallas import tpu as pltpu

F32 = jnp.float32
BF16 = jnp.bfloat16
MESH = pl.DeviceIdType.MESH

SEQ = 2048
D_MODEL = 1024
D_HGRN = 512
D_CONV = 512
HEAD = 128
N_HEADS = 4
CHUNK = 64
CONV_GROUP = 64
N_SHARD = 4
SHARD_COLS = 1024
WO_ROWS = 256
EPS = 1e-6
TB = 256
NCB = TB // CHUNK
N_CHUNKS = SEQ // CHUNK
N_DEV = 8
COLLECTIVE_GATHER, COLLECTIVE_MIX_OUT, COLLECTIVE_TAIL = 1, 0, 2
AUX_O, AUX_CV, AUX_B, AUX_COLS = 0, 512, 1024, 1536

ADAM_LR = 0.001
ADAM_B1 = 0.9
ADAM_B2 = 0.999
ADAM_EPS = 1e-08
ADAM_WD = 0.01
ADAM_STEP = 10

VMEM_LIMIT = 56 * 1024 * 1024


def _dot(a, b):
    return jnp.dot(a, b, preferred_element_type=F32)


def _dot_nt(a, b):
    return lax.dot_general(a, b, (((1,), (1,)), ((), ())), preferred_element_type=F32)


def _dot_tn(a, b):
    return lax.dot_general(a, b, (((0,), (0,)), ((), ())), preferred_element_type=F32)


def _split_bf16(x, n):
    parts = []
    r = x
    for _ in range(n):
        p = r.astype(BF16)
        parts.append(p)
        r = r - p.astype(F32)
    return parts


def _exact_left(m, x, n=3):
    acc = None
    for p in _split_bf16(x, n):
        t = _dot(m, p)
        acc = t if acc is None else acc + t
    return acc


def _exact_left_many(m, xs, n=3):
    parts = [_split_bf16(x, n) for x in xs]
    accs = [None] * len(xs)
    for i in range(n):
        for j in range(len(xs)):
            t = _dot(m, parts[j][i])
            accs[j] = t if accs[j] is None else accs[j] + t
    return accs


def _group_mean_many(xs, gmat, n=2):
    parts = [_split_bf16(x, n) for x in xs]
    accs = [None] * len(xs)
    for i in range(n):
        for j in range(len(xs)):
            t = _dot(parts[j][i], gmat)
            accs[j] = t if accs[j] is None else accs[j] + t
    return accs


def _group_mean(x, gmat, n=2):
    w = gmat.shape[0]
    outs = []
    for c0 in range(0, x.shape[1], w):
        acc = None
        for p in _split_bf16(x[:, c0:c0 + w], n):
            t = _dot(p, gmat)
            acc = t if acc is None else acc + t
        outs.append(acc)
    return jnp.concatenate(outs, axis=1)


def _sigmoid(x):
    return 1.0 / (1.0 + jnp.exp(-x))


def _lower_bound(lbl):
    l0 = lbl[0:1, :]
    l1 = lbl[1:2, :]
    m = jnp.maximum(l0, l1)
    e0 = jnp.exp(l0 - m)
    e1 = jnp.exp(l1 - m)
    return e0 / (e0 + e1)


def _tri(lower):
    r = lax.broadcasted_iota(jnp.int32, (CHUNK, CHUNK), 0)
    c = lax.broadcasted_iota(jnp.int32, (CHUNK, CHUNK), 1)
    return jnp.where((c <= r) if lower else (c >= r), 1.0, 0.0).astype(BF16)


def _causal():
    r = lax.broadcasted_iota(jnp.int32, (CHUNK, CHUNK), 0)
    c = lax.broadcasted_iota(jnp.int32, (CHUNK, CHUNK), 1)
    return c <= r


def _shift_down(x, sh, prev_tail):
    r = pltpu.roll(x, sh, 0)
    pt = pltpu.roll(prev_tail, sh, 0)
    rows = lax.broadcasted_iota(jnp.int32, prev_tail.shape, 0)
    top = jnp.where(rows < sh, pt, r[0:8])
    return jnp.concatenate([top, r[8:]], axis=0)


def _shift_up(x, sh, next_head):
    n = x.shape[0]
    r = pltpu.roll(x, n - sh, 0)
    nh = pltpu.roll(next_head, 8 - sh, 0)
    rows = lax.broadcasted_iota(jnp.int32, next_head.shape, 0)
    bot = jnp.where(rows >= 8 - sh, nh, r[n - 8:])
    return jnp.concatenate([r[:n - 8], bot], axis=0)


def _group_matrix(width, group):
    r = np.arange(width)[:, None] // group
    c = np.arange(width)[None, :] // group
    return jnp.asarray(np.where(r == c, 1.0 / group, 0.0), dtype=BF16)


TG = 1024
SEM_W, SEM_CW, SEM_W_FWD, N_SEM = 0, 4, 7, 11


def _gather_proj(kidx, x2d, g1, w_in, conv_w):
    half_w = D_MODEL // 2
    half_c = SHARD_COLS // 2
    nt = SEQ // TG
    n_steps = 2 * N_SHARD

    def body(k_ref, x_ref, g_ref, w_ref, cw_ref, h_ref, p_ref, wg_out, cwg_out,
             wg_v, cwg_v, w_f32, send_sems, recv_sems, out_sems, load_sem):
        s, t = pl.program_id(0), pl.program_id(1)
        x, y, c = lax.axis_index("x"), lax.axis_index("y"), lax.axis_index("c")
        k = 2 * x + y
        sibling = (x, y, 1 - c)
        chips = [(1 - x, y), (x, 1 - y), (1 - x, 1 - y)]
        kjs = [2 * cx + cy for cx, cy in chips]
        diag = (*chips[2], c)

        def w_half(kk, cc):
            return wg_v.at[kk, pl.ds(cc * half_w, half_w), :]

        def w_quarter(kk, cc, piece):
            return wg_v.at[kk, pl.ds(cc * half_w, half_w), piece * half_c:(piece + 1) * half_c]

        def cw_of(kk):
            return cwg_v.at[:, pl.ds(pl.multiple_of(kk * HEAD, HEAD), HEAD)]

        def copy(sem, ref, to):
            return pltpu.make_async_remote_copy(
                src_ref=ref, dst_ref=ref, send_sem=send_sems.at[sem], recv_sem=recv_sems.at[sem],
                device_id=to, device_id_type=MESH)

        def at_step(sv, tv):
            return pl.when((s == sv) & (t == tv))

        w_direct = ([copy(SEM_W + j, w_half(k, c), (*chips[j], c)) for j in range(2)]
                    + [copy(SEM_W + 2 + p, w_quarter(k, c, p), diag) for p in range(2)])
        cw_direct = [copy(SEM_CW + j, cw_of(k), (*chip, c)) for j, chip in enumerate(chips)]
        w_passed = ([copy(SEM_W_FWD + j, w_half(kjs[j], c), sibling) for j in range(2)]
                    + [copy(SEM_W_FWD + 2 + p, w_quarter(kjs[2], c, p), sibling) for p in range(2)])
        stores = ([pltpu.make_async_copy(wg_v.at[kk], wg_out.at[kk], out_sems.at[i])
                   for i, kk in enumerate([k] + kjs)]
                  + [pltpu.make_async_copy(cwg_v, cwg_out, out_sems.at[4])])

        @at_step(0, 0)
        def _():
            barrier = pltpu.get_barrier_semaphore()
            for peer in [sibling] + [(*chip, c) for chip in chips]:
                pl.semaphore_signal(barrier, inc=1, device_id=peer, device_id_type=MESH)
            load = pltpu.make_async_copy(w_ref.at[0], w_f32, load_sem.at[0])
            load.start()
            load.wait()
            wg_v[k] = w_f32[...].astype(BF16)
            mine = pl.ds(pl.multiple_of(k * HEAD, HEAD), HEAD)
            cwg_v[:, mine] = jnp.zeros((8, HEAD), F32)
            for tap in range(3):
                cwg_v[tap:tap + 1, mine] = cw_ref[:, tap * HEAD:(tap + 1) * HEAD]
            pl.semaphore_wait(barrier, 4)
            for cp in w_direct + cw_direct:
                cp.start()
            stores[0].start()

        @at_step(2, 0)
        def _():
            for j in range(2):
                copy(SEM_W + j, w_half(kjs[j], c), sibling).wait_recv()
                w_passed[j].start()
            copy(SEM_W_FWD, w_half(kjs[0], 1 - c), sibling).wait_recv()
            stores[1].start()

        @at_step(4, 0)
        def _():
            copy(SEM_W_FWD + 1, w_half(kjs[1], 1 - c), sibling).wait_recv()
            stores[2].start()

        for p in range(2):
            @at_step(6 + p, 0)
            def _(p=p):
                copy(SEM_W + 2 + p, w_quarter(kjs[2], c, p), sibling).wait_recv()
                w_passed[2 + p].start()
                copy(SEM_W_FWD + 2 + p, w_quarter(kjs[2], 1 - c, p), sibling).wait_recv()

        rows = pl.ds(pl.multiple_of(t * TG, TG), TG)

        @pl.when(s == 0)
        def _():
            xv = x_ref[...]
            r = lax.rsqrt(jnp.mean(xv * xv, axis=-1, keepdims=True) + EPS)
            h_ref[rows, :] = (xv * r * g_ref[...]).astype(BF16)

        sh = s >> 1
        js = k ^ (((sh & 1) << 1) | (sh >> 1))
        for piece in range(2):
            @pl.when((s & 1) == piece)
            def _(piece=piece):
                p_ref[...] = _dot(h_ref[rows, :], wg_v[js, :, piece * half_c:(piece + 1) * half_c])

        @at_step(n_steps - 1, nt - 1)
        def _():
            stores[3].start()
            for j in range(3):
                copy(SEM_CW + j, cw_of(kjs[j]), sibling).wait_recv()
            stores[4].start()
            for cp in w_direct + cw_direct + w_passed:
                cp.wait_send()
            for st in stores:
                st.wait()

    def x_map(s, t, kr):
        return (jnp.where(s == 0, t, nt - 1), 0)

    def p_map(s, t, kr):
        sh = s >> 1
        return (t, 2 * (kr[0] ^ (((sh & 1) << 1) | (sh >> 1))) + (s & 1))

    hbm = pl.BlockSpec(memory_space=pl.ANY)
    grid_spec = pltpu.PrefetchScalarGridSpec(
        num_scalar_prefetch=1, grid=(n_steps, nt),
        in_specs=[pl.BlockSpec((TG, D_MODEL), x_map),
                  pl.BlockSpec((1, D_MODEL), lambda s, t, kr: (0, 0)),
                  hbm,
                  pl.BlockSpec((1, 3 * HEAD), lambda s, t, kr: (0, 0))],
        out_specs=(pl.BlockSpec((SEQ, D_MODEL), lambda s, t, kr: (0, 0)),
                   pl.BlockSpec((TG, half_c), p_map), hbm, hbm),
        scratch_shapes=[pltpu.VMEM((N_SHARD, D_MODEL, SHARD_COLS), BF16),
                        pltpu.VMEM((8, D_CONV), F32), pltpu.VMEM((D_MODEL, SHARD_COLS), F32),
                        pltpu.SemaphoreType.DMA((N_SEM,)), pltpu.SemaphoreType.DMA((N_SEM,)),
                        pltpu.SemaphoreType.DMA((5,)), pltpu.SemaphoreType.DMA((1,))])
    return pl.pallas_call(
        body, name="gather_proj", grid_spec=grid_spec,
        out_shape=(jax.ShapeDtypeStruct((SEQ, D_MODEL), BF16),
                   jax.ShapeDtypeStruct((SEQ, N_SHARD * SHARD_COLS), F32),
                   jax.ShapeDtypeStruct((N_SHARD, D_MODEL, SHARD_COLS), BF16),
                   jax.ShapeDtypeStruct((8, D_CONV), F32)),
        compiler_params=pltpu.CompilerParams(dimension_semantics=("arbitrary", "arbitrary"),
                                             vmem_limit_bytes=VMEM_LIMIT, collective_id=COLLECTIVE_GATHER),
    )(kidx, x2d, g1, w_in, conv_w)


LAG = 6


def _mix_out(proj, lb_logits, cw, ga, gcn, g64, w_out, x2d, gf, tgt):
    half_o = WO_ROWS // 2
    nblk = SEQ // TB
    n_steps = nblk + LAG

    def body(p_ref, lbl_ref, cw_ref, ga_ref, gcn_ref, g64_ref, wo_ref, x_ref, gf_ref, t_ref,
             aux_ref, sto_ref, dx2_ref, dm_ref, gwo_ref, part_ref,
             st_ref, tail_ref, wog_v, stage, ring, acc_ref, send_sems, recv_sems):
        i = pl.program_id(0)
        x, y, c = lax.axis_index("x"), lax.axis_index("y"), lax.axis_index("c")
        k = 2 * x + y
        sibling = (x, y, 1 - c)
        chips = [(1 - x, y), (x, 1 - y), (1 - x, 1 - y)]
        kjs = [2 * cx + cy for cx, cy in chips]

        def wo_half(kk, cc):
            return wog_v.at[pl.ds(pl.multiple_of(kk * WO_ROWS + cc * half_o, half_o), half_o), :]

        def copy(sem, ref, to):
            return pltpu.make_async_remote_copy(
                src_ref=ref, dst_ref=ref, send_sem=send_sems.at[sem], recv_sem=recv_sems.at[sem],
                device_id=to, device_id_type=MESH)

        wo_direct = [copy(j, wo_half(k, c), (*chip, c)) for j, chip in enumerate(chips)]
        wo_passed = [copy(3 + j, wo_half(kj, c), sibling) for j, kj in enumerate(kjs)]

        @pl.when(i == 0)
        def _():
            barrier = pltpu.get_barrier_semaphore()
            for peer in [sibling] + [(*chip, c) for chip in chips]:
                pl.semaphore_signal(barrier, inc=1, device_id=peer, device_id_type=MESH)
            st_ref[...] = jnp.zeros_like(st_ref)
            tail_ref[...] = jnp.zeros_like(tail_ref)
            acc_ref[...] = jnp.zeros_like(acc_ref)
            part_ref[...] = jnp.zeros_like(part_ref)
            wog_v[pl.ds(pl.multiple_of(k * WO_ROWS, WO_ROWS), WO_ROWS), :] = wo_ref[0].astype(BF16)
            pl.semaphore_wait(barrier, 4)
            for cp in wo_direct:
                cp.start()

        @pl.when(i == LAG - 1)
        def _():
            for j in range(3):
                copy(j, wo_half(kjs[j], c), sibling).wait_recv()
                wo_passed[j].start()

        @pl.when(i == LAG)
        def _():
            for j in range(3):
                copy(3 + j, wo_half(kjs[j], 1 - c), sibling).wait_recv()

        lb = _lower_bound(lbl_ref[...])
        tri = _tri(True)
        causal = _causal()
        g64m = g64_ref[...]
        heads = range(N_HEADS)
        cs = [slice(hd * HEAD, (hd + 1) * HEAD) for hd in heads]
        col = lambda base, hd: slice(base + hd * HEAD, base + (hd + 1) * HEAD)

        def mix_chunk(n):
            sl = pl.ds(n * CHUNK, CHUNK)
            sg = [_sigmoid(p_ref[sl, col(512, hd)]) for hd in heads]
            f = [lb[:, cs[hd]] + (1.0 - lb[:, cs[hd]]) * sg[hd] for hd in heads]
            bc = _exact_left_many(tri, [jnp.log(f[hd]) for hd in heads])
            for hd in heads:
                aux_ref[sl, col(AUX_B, hd)] = bc[hd]
            g = [bc[hd][CHUNK - 1:CHUNK, :] for hd in heads]
            qd = [(p_ref[sl, col(0, hd)] * jnp.exp(bc[hd])).astype(BF16) for hd in heads]
            kk = [1.0 - f[hd] for hd in heads]
            ki = [(kk[hd] * jnp.exp(-bc[hd])).astype(BF16) for hd in heads]
            ke = [(kk[hd] * jnp.exp(g[hd] - bc[hd])).astype(BF16) for hd in heads]
            vb = [p_ref[sl, col(1024, hd)].astype(BF16) for hd in heads]
            st = [st_ref[hd] for hd in heads]
            st_b = [a.astype(BF16) for a in st]
            for hd in heads:
                sto_ref[n, hd] = st_b[hd]
            scm = [_dot_nt(qd[hd], ki[hd]) for hd in heads]
            inter = [_dot_nt(qd[hd], st_b[hd]) for hd in heads]
            upd = [_dot_tn(vb[hd], ke[hd]) for hd in heads]
            intra = [_dot(jnp.where(causal, scm[hd], 0.0).astype(BF16), vb[hd]) for hd in heads]
            for hd in heads:
                st_ref[hd] = st[hd] * jnp.exp(g[hd]) + upd[hd]
                o = intra[hd] + inter[hd]
                aux_ref[sl, col(AUX_O, hd)] = o
                ra = lax.rsqrt(jnp.mean(o * o, axis=-1, keepdims=True) + EPS)
                za = p_ref[sl, col(1536, hd)]
                stage[sl, cs[hd]] = (o * ra * ga_ref[:, cs[hd]] * (za * _sigmoid(za))).astype(BF16)
            yb = []
            for hd in heads:
                cu = p_ref[sl, col(3072, hd)] * p_ref[sl, col(2048, hd)]
                tail = tail_ref[:, cs[hd]]
                cv = (cw_ref[0:1, cs[hd]] * _shift_down(cu, 2, tail) + cw_ref[1:2, cs[hd]] * _shift_down(cu, 1, tail)
                      + cw_ref[2:3, cs[hd]] * cu)
                tail_ref[:, cs[hd]] = cu[CHUNK - 8:, :]
                aux_ref[sl, col(AUX_CV, hd)] = cv
                yb.append(p_ref[sl, col(2560, hd)] * cv)
            ms = _group_mean_many([y * y for y in yb], g64m)
            for hd in heads:
                rb = lax.rsqrt(ms[hd] + EPS)
                zb = p_ref[sl, col(3584, hd)]
                stage[sl, col(512, hd)] = (yb[hd] * rb * gcn_ref[:, cs[hd]] * (zb * _sigmoid(zb))).astype(BF16)

        def step(mix, project):
            if project:
                mixed_b = ring[pl.ds(pl.multiple_of((i - LAG) * TB, TB), TB), :]
                y = _dot(mixed_b, wog_v[...])
            if mix:
                mix_chunk(0)
            if project:
                x2 = x_ref[...] + y
                r2 = lax.rsqrt(jnp.mean(x2 * x2, axis=-1, keepdims=True) + EPS)
                n2 = x2 * r2
                gfv = gf_ref[...]
                err = n2 * gfv - t_ref[...]
                loss = 0.5 * jnp.sum(jnp.mean(err * err, axis=-1, keepdims=True), axis=0, keepdims=True)
                dy = err * (1.0 / D_MODEL)
                part_ref[1:2, :] += jnp.sum(dy * n2, axis=0, keepdims=True)
                part_ref[7:8, :] += jnp.broadcast_to(loss, (1, D_MODEL))
                dn = dy * gfv
                dx2 = r2 * (dn - n2 * jnp.mean(dn * n2, axis=-1, keepdims=True))
                dx2_ref[...] = dx2
                dx2_b = dx2.astype(BF16)
            if mix:
                mix_chunk(1)
            if project:
                dm_ref[...] = _dot_nt(dx2_b, wog_v[...])
            if mix:
                mix_chunk(2)
            if project:
                acc_ref[...] += _dot_tn(mixed_b, dx2_b)
            if mix:
                mix_chunk(3)
                ring[pl.ds(pl.multiple_of(i * TB, TB), TB), :] = stage[...]

        @pl.when(i < LAG)
        def _():
            step(True, False)

        @pl.when((i >= LAG) & (i < nblk))
        def _():
            step(True, True)

        @pl.when(i >= nblk)
        def _():
            step(False, True)

        @pl.when(i == n_steps - 1)
        def _():
            gwo_ref[...] = acc_ref[...].astype(BF16)
            for cp in wo_direct + wo_passed:
                cp.wait_send()

    assert NCB == 4
    row = lambda w: pl.BlockSpec((1, w), lambda i: (0, 0))
    mix_blk = lambda i: jnp.minimum(i, nblk - 1)
    out_blk = lambda i: jnp.clip(i - LAG, 0, nblk - 1)
    tok = lambda: pl.BlockSpec((TB, D_MODEL), lambda i: (out_blk(i), 0))
    return pl.pallas_call(
        body, name="mix_out", grid=(n_steps,),
        out_shape=(jax.ShapeDtypeStruct((SEQ, AUX_COLS), F32),
                   jax.ShapeDtypeStruct((N_CHUNKS, N_HEADS, HEAD, HEAD), BF16),
                   jax.ShapeDtypeStruct((SEQ, D_MODEL), F32),
                   jax.ShapeDtypeStruct((SEQ, D_MODEL), F32),
                   jax.ShapeDtypeStruct((D_MODEL, D_MODEL), BF16),
                   jax.ShapeDtypeStruct((8, D_MODEL), F32)),
        in_specs=[pl.BlockSpec((TB, 4096), lambda i: (jnp.minimum(i, nblk - 1), 0)),
                  pl.BlockSpec((2, D_HGRN), lambda i: (0, 0)),
                  pl.BlockSpec((8, D_CONV), lambda i: (0, 0)),
                  row(D_HGRN), row(D_CONV),
                  pl.BlockSpec((HEAD, HEAD), lambda i: (0, 0)),
                  pl.BlockSpec((1, WO_ROWS, D_MODEL), lambda i: (0, 0, 0)),
                  tok(), row(D_MODEL), tok()],
        out_specs=(pl.BlockSpec((TB, AUX_COLS), lambda i: (mix_blk(i), 0)),
                   pl.BlockSpec((NCB, N_HEADS, HEAD, HEAD), lambda i: (mix_blk(i), 0, 0, 0)),
                   tok(), tok(),
                   pl.BlockSpec((D_MODEL, D_MODEL), lambda i: (0, 0)),
                   pl.BlockSpec((8, D_MODEL), lambda i: (0, 0))),
        scratch_shapes=[pltpu.VMEM((N_HEADS, HEAD, HEAD), F32), pltpu.VMEM((8, D_CONV), F32),
                        pltpu.VMEM((D_MODEL, D_MODEL), BF16), pltpu.VMEM((TB, D_MODEL), BF16),
                        pltpu.VMEM((SEQ, D_MODEL), BF16), pltpu.VMEM((D_MODEL, D_MODEL), F32),
                        pltpu.SemaphoreType.DMA((6,)), pltpu.SemaphoreType.DMA((6,))],
        compiler_params=pltpu.CompilerParams(dimension_semantics=("arbitrary",), vmem_limit_bytes=VMEM_LIMIT,
                                             collective_id=COLLECTIVE_MIX_OUT),
    )(proj, lb_logits, cw, ga, gcn, g64, w_out, x2d, gf, tgt)


def _mix_bwd(proj, aux, states, dmixed, lb_logits, cw, ga, gcn, g64):
    nblk = SEQ // TB

    def body(p_ref, aux_ref, st_ref, dm_ref, lbl_ref, cw_ref, ga_ref, gcn_ref, g64_ref,
             dp_ref, part_ref, dst_ref, head_ref, dlb_ref):
        i = pl.program_id(0)

        @pl.when(i == 0)
        def _():
            dst_ref[...] = jnp.zeros_like(dst_ref)
            head_ref[...] = jnp.zeros_like(head_ref)
            part_ref[...] = jnp.zeros_like(part_ref)
            dlb_ref[...] = jnp.zeros_like(dlb_ref)

        lb = _lower_bound(lbl_ref[...])
        triu = _tri(False)
        causal = _causal()
        g64m = g64_ref[...]
        rowsum = lambda a: jnp.sum(a, axis=0, keepdims=True)
        heads = range(N_HEADS)
        cs = [slice(hd * HEAD, (hd + 1) * HEAD) for hd in heads]
        col = lambda base, hd: slice(base + hd * HEAD, base + (hd + 1) * HEAD)
        for n in reversed(range(NCB)):
            sl = pl.ds(n * CHUNK, CHUNK)
            cvv = [aux_ref[sl, col(AUX_CV, hd)] for hd in heads]
            gb = [p_ref[sl, col(2560, hd)] for hd in heads]
            yb = [gb[hd] * cvv[hd] for hd in heads]
            ms = _group_mean_many([y * y for y in yb], g64m)
            rb, nb, dnb = [], [], []
            for hd in heads:
                rb.append(lax.rsqrt(ms[hd] + EPS))
                nb.append(yb[hd] * rb[hd])
                zb = p_ref[sl, col(3584, hd)]
                sgb = _sigmoid(zb)
                dmb = dm_ref[sl, col(512, hd)]
                silu = zb * sgb
                dgate = dmb * gcn_ref[:, cs[hd]]
                part_ref[2:3, col(512, hd)] += rowsum(dmb * nb[hd] * silu)
                dp_ref[sl, col(3584, hd)] = (dgate * nb[hd] * (sgb + silu * (1.0 - sgb))).astype(BF16)
                dnb.append(dgate * silu)
            mdn = _group_mean_many([dnb[hd] * nb[hd] for hd in heads], g64m)
            for hd in heads:
                dyb = rb[hd] * (dnb[hd] - nb[hd] * mdn[hd])
                dp_ref[sl, col(2560, hd)] = (dyb * cvv[hd]).astype(BF16)
                dcv = dyb * gb[hd]
                head = head_ref[:, cs[hd]]
                dcv1 = _shift_up(dcv, 1, head)
                dcv2 = _shift_up(dcv, 2, head)
                head_ref[:, cs[hd]] = dcv[0:8, :]
                u = p_ref[sl, col(2048, hd)]
                gc = p_ref[sl, col(3072, hd)]
                cu = gc * u
                part_ref[4:5, cs[hd]] += rowsum(dcv2 * cu)
                part_ref[5:6, cs[hd]] += rowsum(dcv1 * cu)
                part_ref[6:7, cs[hd]] += rowsum(dcv * cu)
                dcu = cw_ref[2:3, cs[hd]] * dcv + cw_ref[1:2, cs[hd]] * dcv1 + cw_ref[0:1, cs[hd]] * dcv2
                dp_ref[sl, col(3072, hd)] = (dcu * u).astype(BF16)
                dp_ref[sl, col(2048, hd)] = (dcu * gc).astype(BF16)
            do_b = []
            for hd in heads:
                ov = aux_ref[sl, col(AUX_O, hd)]
                ra = lax.rsqrt(jnp.mean(ov * ov, axis=-1, keepdims=True) + EPS)
                na = ov * ra
                za = p_ref[sl, col(1536, hd)]
                sga = _sigmoid(za)
                dma = dm_ref[sl, cs[hd]]
                silu = za * sga
                dgate = dma * ga_ref[:, cs[hd]]
                part_ref[2:3, cs[hd]] += rowsum(dma * na * silu)
                dp_ref[sl, col(1536, hd)] = (dgate * na * (sga + silu * (1.0 - sga))).astype(BF16)
                dna = dgate * silu
                do_b.append((ra * (dna - na * jnp.mean(dna * na, axis=-1, keepdims=True))).astype(BF16))
            s = [_sigmoid(p_ref[sl, col(512, hd)]) for hd in heads]
            f = [lb[:, cs[hd]] + (1.0 - lb[:, cs[hd]]) * s[hd] for hd in heads]
            bc = [aux_ref[sl, col(AUX_B, hd)] for hd in heads]
            g = [bc[hd][CHUNK - 1:CHUNK, :] for hd in heads]
            eb = [jnp.exp(bc[hd]) for hd in heads]
            enb = [jnp.exp(-bc[hd]) for hd in heads]
            eg = [jnp.exp(g[hd] - bc[hd]) for hd in heads]
            dec = [jnp.exp(g[hd]) for hd in heads]
            qd = [p_ref[sl, cs[hd]] * eb[hd] for hd in heads]
            kk = [1.0 - f[hd] for hd in heads]
            ki = [kk[hd] * enb[hd] for hd in heads]
            ke = [kk[hd] * eg[hd] for hd in heads]
            qd_b = [a.astype(BF16) for a in qd]
            ki_b = [a.astype(BF16) for a in ki]
            ke_b = [a.astype(BF16) for a in ke]
            vb = [p_ref[sl, col(1024, hd)].astype(BF16) for hd in heads]
            st_b = [st_ref[n, hd] for hd in heads]
            dst = [dst_ref[hd] for hd in heads]
            dst_b = [a.astype(BF16) for a in dst]
            scm = [_dot_nt(qd_b[hd], ki_b[hd]) for hd in heads]
            amm = [_dot_nt(do_b[hd], vb[hd]) for hd in heads]
            dqd2 = [_dot(do_b[hd], st_b[hd]) for hd in heads]
            dke = [_dot(vb[hd], dst_b[hd]) for hd in heads]
            dv2 = [_dot_nt(ke_b[hd], dst_b[hd]) for hd in heads]
            dsu = [_dot_tn(do_b[hd], qd_b[hd]) for hd in heads]
            sc = [jnp.where(causal, scm[hd], 0.0).astype(BF16) for hd in heads]
            am = [jnp.where(causal, amm[hd], 0.0).astype(BF16) for hd in heads]
            dqd1 = [_dot(am[hd], ki_b[hd]) for hd in heads]
            dki = [_dot_tn(am[hd], qd_b[hd]) for hd in heads]
            dv1 = [_dot_tn(sc[hd], do_b[hd]) for hd in heads]
            db, dgv, dkk = [], [], []
            for hd in heads:
                dqd = dqd1[hd] + dqd2[hd]
                ddec = rowsum(dst[hd] * st_b[hd].astype(F32))
                dst_ref[hd] = dst[hd] * dec[hd] + dsu[hd]
                dp_ref[sl, cs[hd]] = (dqd * eb[hd]).astype(BF16)
                dp_ref[sl, col(1024, hd)] = (dv1[hd] + dv2[hd]).astype(BF16)
                dke_eg = dke[hd] * eg[hd]
                dkk.append(dki[hd] * enb[hd] + dke_eg)
                db.append(dqd * qd[hd] - kk[hd] * dkk[hd])
                dgv.append(rowsum(kk[hd] * dke_eg) + ddec * dec[hd])
            rc = _exact_left_many(triu, db, 2)
            for hd in heads:
                df = (rc[hd] + dgv[hd]) / f[hd] - dkk[hd]
                one_s = 1.0 - s[hd]
                dlb_ref[:, cs[hd]] += rowsum(df * one_s)
                dp_ref[sl, col(512, hd)] = (df * (1.0 - lb[:, cs[hd]]) * s[hd] * one_s).astype(BF16)

        @pl.when(i == nblk - 1)
        def _():
            row = dlb_ref[...] * lb * (1.0 - lb)
            part_ref[3:4, 0:D_HGRN] = row
            part_ref[3:4, D_HGRN:] = -row

    rev = lambda w: pl.BlockSpec((TB, w), lambda i: (nblk - 1 - i, 0))
    row = lambda w: pl.BlockSpec((1, w), lambda i: (0, 0))
    return pl.pallas_call(
        body, name="mix_bwd", grid=(nblk,),
        out_shape=(jax.ShapeDtypeStruct((SEQ, 4096), BF16),
                   jax.ShapeDtypeStruct((8, D_MODEL), F32)),
        in_specs=[rev(4096), rev(AUX_COLS),
                  pl.BlockSpec((NCB, N_HEADS, HEAD, HEAD), lambda i: (nblk - 1 - i, 0, 0, 0)),
                  rev(D_MODEL),
                  pl.BlockSpec((2, D_HGRN), lambda i: (0, 0)),
                  pl.BlockSpec((8, D_CONV), lambda i: (0, 0)),
                  row(D_HGRN), row(D_CONV),
                  pl.BlockSpec((HEAD, HEAD), lambda i: (0, 0))],
        out_specs=(rev(4096), pl.BlockSpec((8, D_MODEL), lambda i: (0, 0))),
        scratch_shapes=[pltpu.VMEM((N_HEADS, HEAD, HEAD), F32), pltpu.VMEM((8, D_CONV), F32),
                        pltpu.VMEM((1, D_HGRN), F32)],
        compiler_params=pltpu.CompilerParams(dimension_semantics=("arbitrary",), vmem_limit_bytes=VMEM_LIMIT),
    )(proj, aux, states, dmixed, lb_logits, cw, ga, gcn, g64)


TT = 1024
TX = 512
(SEM_D2D, SEM_D2D_O, SEM_ICI, SEM_ICI_O, SEM_FIN, SEM_FIN_O, SEM_SMALL, SEM_VIA, N_SEM_TAIL) = (
    0, 4, 5, 8, 11, 12, 12, 20, 22)


def _bwd_tail(kidx, h, dproj, wg, gwo, x2d, dx2, g1, small_a, small_b):
    hw = D_MODEL // 2
    ho = WO_ROWS // 2
    nt = SEQ // TT
    norm_step = 2 * N_SHARD
    n_steps = norm_step + SEQ // TX // nt

    def body(k_ref, h_ref, dp_ref, w_ref, gwo_ref, x_ref, dx2_ref, g_ref, sm_ref, smb_ref,
             gx_ref, gw_out, gwo_out, osm_ref,
             acc, dh, sendbuf, keep, sibrcv, rcv, merge, sib_o, p_o, rcv_o, res_o, sm_buf, dng,
             send_sems, recv_sems, out_sems):
        s, t = pl.program_id(0), pl.program_id(1)
        x, y, c = lax.axis_index("x"), lax.axis_index("y"), lax.axis_index("c")
        k = 2 * x + y
        me = 4 * x + 2 * y + c
        sibling = (x, y, 1 - c)
        chips = [(1 - x, 1 - y), (1 - x, y), (x, 1 - y)]
        kjs = [2 * cx + cy for cx, cy in chips]
        mine = pl.ds(pl.multiple_of(c * hw, hw), hw)
        other = pl.ds(pl.multiple_of((1 - c) * hw, hw), hw)
        mine_o = pl.ds(pl.multiple_of(c * ho, ho), ho)
        other_o = pl.ds(pl.multiple_of((1 - c) * ho, ho), ho)

        def copy(sem, src, dst, to):
            return pltpu.make_async_remote_copy(
                src_ref=src, dst_ref=dst, send_sem=send_sems.at[sem], recv_sem=recv_sems.at[sem],
                device_id=to, device_id_type=MESH)

        def at_step(sv, tv):
            return pl.when((s == sv) & (t == tv))

        def at_norm_block(b):
            return at_step(norm_step + b // nt, b % nt)

        d2d = [copy(SEM_D2D + sv, sendbuf.at[sv], sibrcv.at[sv], sibling) for sv in range(N_SHARD)]
        d2d_o = copy(SEM_D2D_O, gwo_ref.at[:, other_o, :], sib_o, sibling)
        ici = {sv: copy(SEM_ICI + sv, keep.at[sv], rcv.at[sv - 1], (*chips[sv], c)) for sv in (1, 2)}
        qh = hw // 2
        via = [copy(SEM_VIA, keep.at[0, 0:qh, :], merge.at[1], (*chips[1], c)),
               copy(SEM_VIA + 1, keep.at[0, qh:hw, :], merge.at[0], (*chips[2], c))]
        merged_rows = [slice(qh, hw), slice(0, qh)]
        ici_o = [copy(SEM_ICI_O + sv, p_o.at[kjs[sv]], rcv_o.at[sv], (*chips[sv], c)) for sv in range(3)]
        fin = copy(SEM_FIN, acc.at[mine, :], gw_out.at[mine, :], sibling)
        fin_o = copy(SEM_FIN_O, res_o.at[mine_o, :], res_o.at[mine_o, :], sibling)
        smalls = [copy(SEM_SMALL + m, sm_buf.at[me], sm_buf.at[me],
                       (x ^ (m >> 2), y ^ ((m >> 1) & 1), c ^ (m & 1))) for m in range(1, N_DEV)]
        store_w = pltpu.make_async_copy(acc.at[mine, :], gw_out.at[mine, :], out_sems.at[0])
        store_o = pltpu.make_async_copy(res_o, gwo_out, out_sems.at[1])

        @at_step(0, 0)
        def _():
            barrier = pltpu.get_barrier_semaphore()
            for m in range(1, N_DEV):
                pl.semaphore_signal(barrier, inc=1, device_id=(x ^ (m >> 2), y ^ ((m >> 1) & 1), c ^ (m & 1)),
                                    device_id_type=MESH)
            pl.semaphore_wait(barrier, N_DEV - 1)
            d2d_o.start()

        @at_step(0, 1)
        def _():
            d2d_o.wait_recv()
            for j in range(N_SHARD):
                p_o[j] = (gwo_ref[j, mine_o, :].astype(F32) + sib_o[j].astype(F32)).astype(BF16)
            res_o[mine_o, :] = gwo_ref[k, mine_o, :].astype(F32) + sib_o[k].astype(F32)
            for cp in ici_o:
                cp.start()

        rows = pl.ds(pl.multiple_of(t * TT, TT), TT)

        @pl.when((s < N_SHARD) & (t == 0))
        def _():
            acc[...] = _dot_tn(h_ref[...], dp_ref[...])

        @pl.when((s < N_SHARD) & (t > 0))
        def _():
            acc[...] += _dot_tn(h_ref[...], dp_ref[...])

        for sv in range(N_SHARD):
            @at_step(sv, nt - 1)
            def _(sv=sv):
                sendbuf[sv] = acc[other, :].astype(BF16)
                if sv < 3:
                    keep[sv] = acc[mine, :].astype(BF16)
                d2d[sv].start()

        @at_step(1, 0)
        def _():
            d2d[0].wait_recv()
            keep[0] = (keep[0].astype(F32) + sibrcv[0].astype(F32)).astype(BF16)
            for cp in via:
                cp.start()

        for sv in (1, 2):
            @at_step(sv + 2, 0)
            def _(sv=sv):
                d2d[sv].wait_recv()
                keep[sv] = (keep[sv].astype(F32) + sibrcv[sv].astype(F32)).astype(BF16)
                via[2 - sv].wait_recv()
                rows_m = merged_rows[sv - 1]
                keep[sv, rows_m, :] = (keep[sv, rows_m, :].astype(F32) + merge[sv - 1].astype(F32)).astype(BF16)
                ici[sv].start()

        @pl.when(s == N_SHARD)
        def _():
            dh[rows, :] = _dot_nt(dp_ref[...], w_ref[0])

        @pl.when((s > N_SHARD) & (s < norm_step))
        def _():
            dh[rows, :] += _dot_nt(dp_ref[...], w_ref[0])

        @at_norm_block(0)
        def _():
            d2d[3].wait_recv()
            acc[mine, :] += sibrcv[3].astype(F32)

        @at_norm_block(1)
        def _():
            tot = res_o[mine_o, :]
            for sv in range(3):
                ici_o[sv].wait_recv()
                tot = tot + rcv_o[sv].astype(F32)
            res_o[mine_o, :] = tot
            fin_o.start()

        @at_norm_block(2)
        def _():
            ici[1].wait_recv()
            acc[mine, :] += rcv[0].astype(F32)

        @at_norm_block(SEQ // TX - 2)
        def _():
            ici[2].wait_recv()
            acc[mine, :] += rcv[1].astype(F32)
            fin.start()
            store_w.start()
            fin_o.wait_recv()
            store_o.start()

        @at_norm_block(0)
        def _():
            dng[...] = jnp.zeros_like(dng)

        @pl.when(s >= norm_step)
        def _():
            blk = (s - norm_step) * nt + t
            dhv = dh[pl.ds(pl.multiple_of(blk * TX, TX), TX), :]
            xv = x_ref[...]
            r = lax.rsqrt(jnp.mean(xv * xv, axis=-1, keepdims=True) + EPS)
            xn = xv * r
            dng[...] += jnp.sum(dhv * xn, axis=0, keepdims=True)
            dxn = dhv * g_ref[...]
            gx_ref[...] = dx2_ref[...] + r * (dxn - xn * jnp.mean(dxn * xn, axis=-1, keepdims=True))

        @at_step(n_steps - 1, nt - 1)
        def _():
            sm_buf[me] = sm_ref[...] + smb_ref[...]
            sm_buf[me, 0:1, :] = dng[...]
            for cp in smalls:
                cp.start()
            for m in range(1, N_DEV):
                copy(SEM_SMALL + m, sm_buf.at[0], sm_buf.at[0], sibling).wait_recv()
            tot = sm_buf[0]
            for d in range(1, N_DEV):
                tot = tot + sm_buf[d]
            osm_ref[...] = tot
            fin.wait_recv()
            for cp in d2d + [d2d_o] + via + list(ici.values()) + ici_o + [fin, fin_o] + smalls:
                cp.wait_send()
            store_o.wait()
            store_w.wait()

    def shard_of(s, kr):
        order = jnp.where(s < N_SHARD, s, jnp.where(s < norm_step, s - N_SHARD, 3))
        return kr[0] ^ (3 - order)

    def h_map(s, t, kr):
        return (jnp.where(s < N_SHARD, t, nt - 1), 0)

    def dp_map(s, t, kr):
        return (jnp.where(s < norm_step, t, nt - 1), shard_of(s, kr))

    def w_map(s, t, kr):
        return (shard_of(jnp.maximum(s, N_SHARD), kr), 0, 0)

    def blk_map(s, t, kr):
        return (jnp.where(s < norm_step, 0, (s - norm_step) * nt + t), 0)

    hbm = pl.BlockSpec(memory_space=pl.ANY)
    grid_spec = pltpu.PrefetchScalarGridSpec(
        num_scalar_prefetch=1, grid=(n_steps, nt),
        in_specs=[pl.BlockSpec((TT, D_MODEL), h_map),
                  pl.BlockSpec((TT, SHARD_COLS), dp_map),
                  pl.BlockSpec((1, D_MODEL, SHARD_COLS), w_map),
                  pl.BlockSpec((N_SHARD, WO_ROWS, D_MODEL), lambda s, t, kr: (0, 0, 0),
                               pipeline_mode=pl.Buffered(1)),
                  pl.BlockSpec((TX, D_MODEL), blk_map),
                  pl.BlockSpec((TX, D_MODEL), blk_map),
                  pl.BlockSpec((1, D_MODEL), lambda s, t, kr: (0, 0)),
                  pl.BlockSpec((8, D_MODEL), lambda s, t, kr: (0, 0)),
                  pl.BlockSpec((8, D_MODEL), lambda s, t, kr: (0, 0))],
        out_specs=(pl.BlockSpec((TX, D_MODEL), blk_map), hbm, hbm,
                   pl.BlockSpec((8, D_MODEL), lambda s, t, kr: (0, 0))),
        scratch_shapes=[pltpu.VMEM((D_MODEL, SHARD_COLS), F32), pltpu.VMEM((SEQ, D_MODEL), F32),
                        pltpu.VMEM((N_SHARD, hw, SHARD_COLS), BF16), pltpu.VMEM((3, hw, SHARD_COLS), BF16),
                        pltpu.VMEM((N_SHARD, hw, SHARD_COLS), BF16), pltpu.VMEM((2, hw, SHARD_COLS), BF16),
                        pltpu.VMEM((2, hw // 2, SHARD_COLS), BF16),
                        pltpu.VMEM((N_SHARD, ho, D_MODEL), BF16), pltpu.VMEM((N_SHARD, ho, D_MODEL), BF16),
                        pltpu.VMEM((3, ho, D_MODEL), BF16), pltpu.VMEM((WO_ROWS, D_MODEL), F32),
                        pltpu.VMEM((N_DEV, 8, D_MODEL), F32), pltpu.VMEM((1, D_MODEL), F32),
                        pltpu.SemaphoreType.DMA((N_SEM_TAIL,)), pltpu.SemaphoreType.DMA((N_SEM_TAIL,)),
                        pltpu.SemaphoreType.DMA((2,))])
    return pl.pallas_call(
        body, name="bwd_tail", grid_spec=grid_spec,
        out_shape=(jax.ShapeDtypeStruct((SEQ, D_MODEL), F32),
                   jax.ShapeDtypeStruct((D_MODEL, SHARD_COLS), F32),
                   jax.ShapeDtypeStruct((WO_ROWS, D_MODEL), F32),
                   jax.ShapeDtypeStruct((8, D_MODEL), F32)),
        compiler_params=pltpu.CompilerParams(dimension_semantics=("arbitrary", "arbitrary"),
                                             vmem_limit_bytes=61 * 1024 * 1024, collective_id=COLLECTIVE_TAIL),
    )(kidx, h, dproj, wg, gwo, x2d, dx2, g1, small_a, small_b)


def _adam_update(w, g, m, v):
    nm = ADAM_B1 * m + (1.0 - ADAM_B1) * g
    nv = ADAM_B2 * v + (1.0 - ADAM_B2) * (g * g)
    m_hat = nm / (1.0 - ADAM_B1 ** ADAM_STEP)
    v_hat = nv / (1.0 - ADAM_B2 ** ADAM_STEP)
    return -ADAM_LR * (m_hat / (jnp.sqrt(v_hat) + ADAM_EPS) + ADAM_WD * w), nm, nv


def _adamw_all(tot, g_w_in, g_w_out, big, small, grad_x):
    n = len(small)
    rows = WO_ROWS
    steps = D_MODEL // rows

    def body(tot_ref, *refs):
        gx_ref, gx_out = refs[2 + 3 * (2 + n)], refs[-1]
        gx_out[...] = gx_ref[...]
        ins, outs = refs[:2 + 3 * (2 + n)], refs[3 + 3 * (2 + n):-1]
        g_refs, wmv = ins[:2], ins[2:]
        loss_ref, quads = outs[0], outs[1:]

        def update(j, g):
            w_ref, m_ref, v_ref = wmv[3 * j:3 * j + 3]
            g_ref, d_ref, nm_ref, nv_ref = quads[4 * j:4 * j + 4]
            g_ref[...] = g
            d_ref[...], nm_ref[...], nv_ref[...] = _adam_update(w_ref[...], g, m_ref[...], v_ref[...])

        update(0, g_refs[0][...])

        @pl.when(pl.program_id(0) == 0)
        def _():
            update(1, g_refs[1][...])
            k = 2 * lax.axis_index("x") + lax.axis_index("y")
            mine = pl.ds(pl.multiple_of(k * HEAD, HEAD), HEAD)
            loss_ref[...] = tot_ref[7:8, 0:1]
            grads = [tot_ref[0:1, :], tot_ref[1:2, :], tot_ref[2:3, 0:D_HGRN], tot_ref[2:3, D_HGRN:],
                     jnp.concatenate([tot_ref[3:4, 0:D_HGRN], tot_ref[3:4, D_HGRN:]], axis=0),
                     jnp.concatenate([tot_ref[4 + tap:5 + tap, mine] for tap in range(3)], axis=1)]
            for j, g in enumerate(grads):
                update(2 + j, g)

    whole = lambda a: pl.BlockSpec(a.shape, lambda i: (0, 0))
    blk = pl.BlockSpec((rows, SHARD_COLS), lambda i: (i, 0))
    arrays = [a for triple in big + small for a in triple]
    in_specs = ([whole(tot), blk, whole(g_w_out)] + [blk] * 3 + [whole(a) for a in arrays[3:]])
    shapes = [big[0][0], big[1][0]] + [w for w, _, _ in small]
    out_shape = (jax.ShapeDtypeStruct((1, 1), F32),) + tuple(
        jax.ShapeDtypeStruct(w.shape, F32) for w in shapes for _ in range(4))
    out_specs = (pl.BlockSpec((1, 1), lambda i: (0, 0)),) + (blk,) * 4 + tuple(
        whole(w) for w in shapes[1:] for _ in range(4))
    gx_blk = pl.BlockSpec((SEQ // steps, D_MODEL), lambda i: (i, 0))
    outs = pl.pallas_call(
        body, name="adamw_all", grid=(steps,),
        out_shape=out_shape + (jax.ShapeDtypeStruct(grad_x.shape, F32),),
        in_specs=in_specs + [gx_blk], out_specs=out_specs + (gx_blk,),
        compiler_params=pltpu.CompilerParams(dimension_semantics=("arbitrary",), vmem_limit_bytes=VMEM_LIMIT),
    )(tot, g_w_in, g_w_out, *arrays, grad_x)
    return [outs[0]] + [outs[1 + 4 * j:5 + 4 * j] for j in range(2 + n)] + [outs[-1]]


def _local_step(x2d, tgt, proj, lb_logits, cw, ga, gcn, w_out, gf):
    g64 = _group_matrix(HEAD, CONV_GROUP)
    aux, states, dx2, dmixed, gwo, part_out = _mix_out(proj, lb_logits, cw, ga, gcn, g64, w_out, x2d, gf, tgt)
    dproj, part_mix = _mix_bwd(proj, aux, states, dmixed, lb_logits, cw, ga, gcn, g64)
    return dproj, dx2, gwo.reshape(N_SHARD, WO_ROWS, D_MODEL), part_out, part_mix


def kernel(x, norm_gain, w_in, lb_logits, conv_w, hgrn_norm_gain, conv_norm_gain, w_out, final_norm_gain, loss_target, m_norm_gain, m_w_in, m_lb_logits, m_conv_w, m_hgrn_norm_gain, m_conv_norm_gain, m_w_out, m_final_norm_gain, v_norm_gain, v_w_in, v_lb_logits, v_conv_w, v_hgrn_norm_gain, v_conv_norm_gain, v_w_out, v_final_norm_gain):
    k = 2 * lax.axis_index("x") + lax.axis_index("y")
    kidx = jnp.reshape(k, (1,)).astype(jnp.int32)
    row = lambda a: a.reshape(1, D_MODEL)
    taps = lambda a: a.reshape(1, 3 * HEAD)
    h, proj, wg, cw = _gather_proj(kidx, x[0], norm_gain, w_in, taps(conv_w))
    dproj, dx2, gwo, part_out, part_mix = _local_step(
        x[0], loss_target[0], proj, lb_logits, cw, hgrn_norm_gain, conv_norm_gain, w_out, row(final_norm_gain))
    rgrad_x, rg_w_in, rg_w_out, tot = _bwd_tail(kidx, h, dproj, wg, gwo, x[0], dx2, norm_gain, part_out, part_mix)

    (loss, (g_w_in, d_w_in, nm_w_in, nv_w_in), (g_w_out, d_w_out, nm_w_out, nv_w_out),
     (g_norm_gain, d_ng, nm_ng, nv_ng), (g_final, d_fg, nm_fg, nv_fg), (g_hgrn, d_hg, nm_hg, nv_hg),
     (g_convn, d_cg, nm_cg, nv_cg), (g_lb, d_lb, nm_lb, nv_lb), (g_conv_w, d_cw, nm_cw, nv_cw),
     grad_x) = _adamw_all(
        tot, rg_w_in, rg_w_out,
        [(w_in[0], m_w_in[0], v_w_in[0]), (w_out[0], m_w_out[0], v_w_out[0])],
        [(norm_gain, m_norm_gain, v_norm_gain),
         (row(final_norm_gain), row(m_final_norm_gain), row(v_final_norm_gain)),
         (hgrn_norm_gain, m_hgrn_norm_gain, v_hgrn_norm_gain),
         (conv_norm_gain, m_conv_norm_gain, v_conv_norm_gain),
         (lb_logits, m_lb_logits, v_lb_logits),
         (taps(conv_w), taps(m_conv_w), taps(v_conv_w))],
        rgrad_x)
    flat = lambda a: a.reshape(D_MODEL)
    untap = lambda a: a.reshape(1, 3, HEAD)
    return (loss.reshape(()), grad_x[None],
            g_norm_gain, g_w_in[None], g_lb, untap(g_conv_w), g_hgrn, g_convn, g_w_out[None], flat(g_final),
            d_ng, d_w_in[None], d_lb, untap(d_cw), d_hg, d_cg, d_w_out[None], flat(d_fg),
            nm_ng, nm_w_in[None], nm_lb, untap(nm_cw), nm_hg, nm_cg, nm_w_out[None], flat(nm_fg),
            nv_ng, nv_w_in[None], nv_lb, untap(nv_cw), nv_hg, nv_cg, nv_w_out[None], flat(nv_fg))
```

```python
import jax
import jax.numpy as jnp
import numpy as np
from jax import lax
from jax.experimental import pallas as pl
from jax.experimental.pallas import tpu as pltpu

F32 = jnp.float32
BF16 = jnp.bfloat16
MESH = pl.DeviceIdType.MESH

SEQ = 2048
D_MODEL = 1024
D_HGRN = 512
D_CONV = 512
HEAD = 128
N_HEADS = 4
CHUNK = 64
CONV_GROUP = 64
N_SHARD = 4
SHARD_COLS = 1024
WO_ROWS = 256
EPS = 1e-6
TB = 256
NCB = TB // CHUNK
N_CHUNKS = SEQ // CHUNK
N_DEV = 8
COLLECTIVE_GATHER, COLLECTIVE_MIX_OUT, COLLECTIVE_TAIL = 1, 0, 2
AUX_O, AUX_CV, AUX_B, AUX_COLS = 0, 512, 1024, 1536

ADAM_LR = 0.001
ADAM_B1 = 0.9
ADAM_B2 = 0.999
ADAM_EPS = 1e-08
ADAM_WD = 0.01
ADAM_STEP = 10

VMEM_LIMIT = 56 * 1024 * 1024


def _dot(a, b):
    return jnp.dot(a, b, preferred_element_type=F32)


def _dot_nt(a, b):
    return lax.dot_general(a, b, (((1,), (1,)), ((), ())), preferred_element_type=F32)


def _dot_tn(a, b):
    return lax.dot_general(a, b, (((0,), (0,)), ((), ())), preferred_element_type=F32)


def _split_bf16(x, n):
    parts = []
    r = x
    for _ in range(n):
        p = r.astype(BF16)
        parts.append(p)
        r = r - p.astype(F32)
    return parts


def _exact_left(m, x, n=3):
    acc = None
    for p in _split_bf16(x, n):
        t = _dot(m, p)
        acc = t if acc is None else acc + t
    return acc


def _exact_left_many(m, xs, n=3):
    parts = [_split_bf16(x, n) for x in xs]
    accs = [None] * len(xs)
    for i in range(n):
        for j in range(len(xs)):
            t = _dot(m, parts[j][i])
            accs[j] = t if accs[j] is None else accs[j] + t
    return accs


def _group_mean_many(xs, gmat, n=2):
    parts = [_split_bf16(x, n) for x in xs]
    accs = [None] * len(xs)
    for i in range(n):
        for j in range(len(xs)):
            t = _dot(parts[j][i], gmat)
            accs[j] = t if accs[j] is None else accs[j] + t
    return accs


def _group_mean(x, gmat, n=2):
    w = gmat.shape[0]
    outs = []
    for c0 in range(0, x.shape[1], w):
        acc = None
        for p in _split_bf16(x[:, c0:c0 + w], n):
            t = _dot(p, gmat)
            acc = t if acc is None else acc + t
        outs.append(acc)
    return jnp.concatenate(outs, axis=1)


def _sigmoid(x):
    return 1.0 / (1.0 + jnp.exp(-x))


def _lower_bound(lbl):
    l0 = lbl[0:1, :]
    l1 = lbl[1:2, :]
    m = jnp.maximum(l0, l1)
    e0 = jnp.exp(l0 - m)
    e1 = jnp.exp(l1 - m)
    return e0 / (e0 + e1)


def _tri(lower):
    r = lax.broadcasted_iota(jnp.int32, (CHUNK, CHUNK), 0)
    c = lax.broadcasted_iota(jnp.int32, (CHUNK, CHUNK), 1)
    return jnp.where((c <= r) if lower else (c >= r), 1.0, 0.0).astype(BF16)


def _causal():
    r = lax.broadcasted_iota(jnp.int32, (CHUNK, CHUNK), 0)
    c = lax.broadcasted_iota(jnp.int32, (CHUNK, CHUNK), 1)
    return c <= r


def _shift_down(x, sh, prev_tail):
    r = pltpu.roll(x, sh, 0)
    pt = pltpu.roll(prev_tail, sh, 0)
    rows = lax.broadcasted_iota(jnp.int32, prev_tail.shape, 0)
    top = jnp.where(rows < sh, pt, r[0:8])
    return jnp.concatenate([top, r[8:]], axis=0)


def _shift_up(x, sh, next_head):
    n = x.shape[0]
    r = pltpu.roll(x, n - sh, 0)
    nh = pltpu.roll(next_head, 8 - sh, 0)
    rows = lax.broadcasted_iota(jnp.int32, next_head.shape, 0)
    bot = jnp.where(rows >= 8 - sh, nh, r[n - 8:])
    return jnp.concatenate([r[:n - 8], bot], axis=0)


def _group_matrix(width, group):
    r = np.arange(width)[:, None] // group
    c = np.arange(width)[None, :] // group
    return jnp.asarray(np.where(r == c, 1.0 / group, 0.0), dtype=BF16)


TG = 1024
SEM_W, SEM_CW, SEM_W_FWD, N_SEM = 0, 4, 7, 11


def _gather_proj(kidx, x2d, g1, w_in, conv_w):
    half_w = D_MODEL // 2
    half_c = SHARD_COLS // 2
    nt = SEQ // TG
    n_steps = 2 * N_SHARD

    def body(k_ref, x_ref, g_ref, w_ref, cw_ref, h_ref, p_ref, wg_out, cwg_out,
             wg_v, cwg_v, send_sems, recv_sems, out_sems):
        s, t = pl.program_id(0), pl.program_id(1)
        x, y, c = lax.axis_index("x"), lax.axis_index("y"), lax.axis_index("c")
        k = 2 * x + y
        sibling = (x, y, 1 - c)
        chips = [(1 - x, y), (x, 1 - y), (1 - x, 1 - y)]
        kjs = [2 * cx + cy for cx, cy in chips]
        diag = (*chips[2], c)

        def w_half(kk, cc):
            return wg_v.at[kk, pl.ds(cc * half_w, half_w), :]

        def w_quarter(kk, cc, piece):
            return wg_v.at[kk, pl.ds(cc * half_w, half_w), piece * half_c:(piece + 1) * half_c]

        def cw_of(kk):
            return cwg_v.at[:, pl.ds(pl.multiple_of(kk * HEAD, HEAD), HEAD)]

        def copy(sem, ref, to):
            return pltpu.make_async_remote_copy(
                src_ref=ref, dst_ref=ref, send_sem=send_sems.at[sem], recv_sem=recv_sems.at[sem],
                device_id=to, device_id_type=MESH)

        def at_step(sv, tv):
            return pl.when((s == sv) & (t == tv))

        w_direct = ([copy(SEM_W + j, w_half(k, c), (*chips[j], c)) for j in range(2)]
                    + [copy(SEM_W + 2 + p, w_quarter(k, c, p), diag) for p in range(2)])
        cw_direct = [copy(SEM_CW + j, cw_of(k), (*chip, c)) for j, chip in enumerate(chips)]
        w_passed = ([copy(SEM_W_FWD + j, w_half(kjs[j], c), sibling) for j in range(2)]
                    + [copy(SEM_W_FWD + 2 + p, w_quarter(kjs[2], c, p), sibling) for p in range(2)])
        stores = ([pltpu.make_async_copy(wg_v.at[kk], wg_out.at[kk], out_sems.at[i])
                   for i, kk in enumerate([k] + kjs)]
                  + [pltpu.make_async_copy(cwg_v, cwg_out, out_sems.at[4])])

        @at_step(0, 0)
        def _():
            barrier = pltpu.get_barrier_semaphore()
            for peer in [sibling] + [(*chip, c) for chip in chips]:
                pl.semaphore_signal(barrier, inc=1, device_id=peer, device_id_type=MESH)
            wg_v[k] = w_ref[0].astype(BF16)
            mine = pl.ds(pl.multiple_of(k * HEAD, HEAD), HEAD)
            cwg_v[:, mine] = jnp.zeros((8, HEAD), F32)
            for tap in range(3):
                cwg_v[tap:tap + 1, mine] = cw_ref[:, tap * HEAD:(tap + 1) * HEAD]
            pl.semaphore_wait(barrier, 4)
            for cp in w_direct + cw_direct:
                cp.start()
            stores[0].start()

        @at_step(2, 0)
        def _():
            for j in range(2):
                copy(SEM_W + j, w_half(kjs[j], c), sibling).wait_recv()
                w_passed[j].start()
            copy(SEM_W_FWD, w_half(kjs[0], 1 - c), sibling).wait_recv()
            stores[1].start()

        @at_step(4, 0)
        def _():
            copy(SEM_W_FWD + 1, w_half(kjs[1], 1 - c), sibling).wait_recv()
            stores[2].start()

        for p in range(2):
            @at_step(6 + p, 0)
            def _(p=p):
                copy(SEM_W + 2 + p, w_quarter(kjs[2], c, p), sibling).wait_recv()
                w_passed[2 + p].start()
                copy(SEM_W_FWD + 2 + p, w_quarter(kjs[2], 1 - c, p), sibling).wait_recv()

        rows = pl.ds(pl.multiple_of(t * TG, TG), TG)

        @pl.when(s == 0)
        def _():
            xv = x_ref[...]
            r = lax.rsqrt(jnp.mean(xv * xv, axis=-1, keepdims=True) + EPS)
            h_ref[rows, :] = (xv * r * g_ref[...]).astype(BF16)

        sh = s >> 1
        js = k ^ (((sh & 1) << 1) | (sh >> 1))
        for piece in range(2):
            @pl.when((s & 1) == piece)
            def _(piece=piece):
                p_ref[...] = _dot(h_ref[rows, :], wg_v[js, :, piece * half_c:(piece + 1) * half_c])

        @at_step(n_steps - 1, nt - 1)
        def _():
            stores[3].start()
            for j in range(3):
                copy(SEM_CW + j, cw_of(kjs[j]), sibling).wait_recv()
            stores[4].start()
            for cp in w_direct + cw_direct + w_passed:
                cp.wait_send()
            for st in stores:
                st.wait()

    def x_map(s, t, kr):
        return (jnp.where(s == 0, t, nt - 1), 0)

    def p_map(s, t, kr):
        sh = s >> 1
        return (t, 2 * (kr[0] ^ (((sh & 1) << 1) | (sh >> 1))) + (s & 1))

    hbm = pl.BlockSpec(memory_space=pl.ANY)
    grid_spec = pltpu.PrefetchScalarGridSpec(
        num_scalar_prefetch=1, grid=(n_steps, nt),
        in_specs=[pl.BlockSpec((TG, D_MODEL), x_map),
                  pl.BlockSpec((1, D_MODEL), lambda s, t, kr: (0, 0)),
                  pl.BlockSpec((1, D_MODEL, SHARD_COLS), lambda s, t, kr: (0, 0, 0)),
                  pl.BlockSpec((1, 3 * HEAD), lambda s, t, kr: (0, 0))],
        out_specs=(pl.BlockSpec((SEQ, D_MODEL), lambda s, t, kr: (0, 0)),
                   pl.BlockSpec((TG, half_c), p_map), hbm, hbm),
        scratch_shapes=[pltpu.VMEM((N_SHARD, D_MODEL, SHARD_COLS), BF16),
                        pltpu.VMEM((8, D_CONV), F32),
                        pltpu.SemaphoreType.DMA((N_SEM,)), pltpu.SemaphoreType.DMA((N_SEM,)),
                        pltpu.SemaphoreType.DMA((5,))])
    return pl.pallas_call(
        body, name="gather_proj", grid_spec=grid_spec,
        out_shape=(jax.ShapeDtypeStruct((SEQ, D_MODEL), BF16),
                   jax.ShapeDtypeStruct((SEQ, N_SHARD * SHARD_COLS), F32),
                   jax.ShapeDtypeStruct((N_SHARD, D_MODEL, SHARD_COLS), BF16),
                   jax.ShapeDtypeStruct((8, D_CONV), F32)),
        compiler_params=pltpu.CompilerParams(dimension_semantics=("arbitrary", "arbitrary"),
                                             vmem_limit_bytes=VMEM_LIMIT, collective_id=COLLECTIVE_GATHER),
    )(kidx, x2d, g1, w_in, conv_w)


LAG = 6


def _mix_out(proj, lb_logits, cw, ga, gcn, g64, w_out, x2d, gf, tgt):
    half_o = WO_ROWS // 2
    nblk = SEQ // TB
    n_steps = nblk + LAG

    def body(p_ref, lbl_ref, cw_ref, ga_ref, gcn_ref, g64_ref, wo_ref, x_ref, gf_ref, t_ref,
             aux_ref, sto_ref, dx2_ref, dm_ref, gwo_ref, part_ref,
             st_ref, tail_ref, wog_v, stage, ring, acc_ref, send_sems, recv_sems):
        i = pl.program_id(0)
        x, y, c = lax.axis_index("x"), lax.axis_index("y"), lax.axis_index("c")
        k = 2 * x + y
        sibling = (x, y, 1 - c)
        chips = [(1 - x, y), (x, 1 - y), (1 - x, 1 - y)]
        kjs = [2 * cx + cy for cx, cy in chips]

        def wo_half(kk, cc):
            return wog_v.at[pl.ds(pl.multiple_of(kk * WO_ROWS + cc * half_o, half_o), half_o), :]

        def copy(sem, ref, to):
            return pltpu.make_async_remote_copy(
                src_ref=ref, dst_ref=ref, send_sem=send_sems.at[sem], recv_sem=recv_sems.at[sem],
                device_id=to, device_id_type=MESH)

        wo_direct = [copy(j, wo_half(k, c), (*chip, c)) for j, chip in enumerate(chips)]
        wo_passed = [copy(3 + j, wo_half(kj, c), sibling) for j, kj in enumerate(kjs)]

        @pl.when(i == 0)
        def _():
            barrier = pltpu.get_barrier_semaphore()
            for peer in [sibling] + [(*chip, c) for chip in chips]:
                pl.semaphore_signal(barrier, inc=1, device_id=peer, device_id_type=MESH)
            st_ref[...] = jnp.zeros_like(st_ref)
            tail_ref[...] = jnp.zeros_like(tail_ref)
            acc_ref[...] = jnp.zeros_like(acc_ref)
            part_ref[...] = jnp.zeros_like(part_ref)
            wog_v[pl.ds(pl.multiple_of(k * WO_ROWS, WO_ROWS), WO_ROWS), :] = wo_ref[0].astype(BF16)
            pl.semaphore_wait(barrier, 4)
            for cp in wo_direct:
                cp.start()

        @pl.when(i == LAG - 1)
        def _():
            for j in range(3):
                copy(j, wo_half(kjs[j], c), sibling).wait_recv()
                wo_passed[j].start()

        @pl.when(i == LAG)
        def _():
            for j in range(3):
                copy(3 + j, wo_half(kjs[j], 1 - c), sibling).wait_recv()

        lb = _lower_bound(lbl_ref[...])
        tri = _tri(True)
        causal = _causal()
        g64m = g64_ref[...]
        heads = range(N_HEADS)
        cs = [slice(hd * HEAD, (hd + 1) * HEAD) for hd in heads]
        col = lambda base, hd: slice(base + hd * HEAD, base + (hd + 1) * HEAD)

        def mix_chunk(n):
            sl = pl.ds(n * CHUNK, CHUNK)
            sg = [_sigmoid(p_ref[sl, col(512, hd)]) for hd in heads]
            f = [lb[:, cs[hd]] + (1.0 - lb[:, cs[hd]]) * sg[hd] for hd in heads]
            bc = _exact_left_many(tri, [jnp.log(f[hd]) for hd in heads])
            for hd in heads:
                aux_ref[sl, col(AUX_B, hd)] = bc[hd]
            g = [bc[hd][CHUNK - 1:CHUNK, :] for hd in heads]
            qd = [(p_ref[sl, col(0, hd)] * jnp.exp(bc[hd])).astype(BF16) for hd in heads]
            kk = [1.0 - f[hd] for hd in heads]
            ki = [(kk[hd] * jnp.exp(-bc[hd])).astype(BF16) for hd in heads]
            ke = [(kk[hd] * jnp.exp(g[hd] - bc[hd])).astype(BF16) for hd in heads]
            vb = [p_ref[sl, col(1024, hd)].astype(BF16) for hd in heads]
            st = [st_ref[hd] for hd in heads]
            st_b = [a.astype(BF16) for a in st]
            for hd in heads:
                sto_ref[n, hd] = st_b[hd]
            scm = [_dot_nt(qd[hd], ki[hd]) for hd in heads]
            inter = [_dot_nt(qd[hd], st_b[hd]) for hd in heads]
            upd = [_dot_tn(vb[hd], ke[hd]) for hd in heads]
            intra = [_dot(jnp.where(causal, scm[hd], 0.0).astype(BF16), vb[hd]) for hd in heads]
            for hd in heads:
                st_ref[hd] = st[hd] * jnp.exp(g[hd]) + upd[hd]
                o = intra[hd] + inter[hd]
                aux_ref[sl, col(AUX_O, hd)] = o
                ra = lax.rsqrt(jnp.mean(o * o, axis=-1, keepdims=True) + EPS)
                za = p_ref[sl, col(1536, hd)]
                stage[sl, cs[hd]] = (o * ra * ga_ref[:, cs[hd]] * (za * _sigmoid(za))).astype(BF16)
            yb = []
            for hd in heads:
                cu = p_ref[sl, col(3072, hd)] * p_ref[sl, col(2048, hd)]
                tail = tail_ref[:, cs[hd]]
                cv = (cw_ref[0:1, cs[hd]] * _shift_down(cu, 2, tail) + cw_ref[1:2, cs[hd]] * _shift_down(cu, 1, tail)
                      + cw_ref[2:3, cs[hd]] * cu)
                tail_ref[:, cs[hd]] = cu[CHUNK - 8:, :]
                aux_ref[sl, col(AUX_CV, hd)] = cv
                yb.append(p_ref[sl, col(2560, hd)] * cv)
            ms = _group_mean_many([y * y for y in yb], g64m)
            for hd in heads:
                rb = lax.rsqrt(ms[hd] + EPS)
                zb = p_ref[sl, col(3584, hd)]
                stage[sl, col(512, hd)] = (yb[hd] * rb * gcn_ref[:, cs[hd]] * (zb * _sigmoid(zb))).astype(BF16)

        def step(mix, project):
            if project:
                mixed_b = ring[pl.ds(pl.multiple_of((i - LAG) * TB, TB), TB), :]
                y = _dot(mixed_b, wog_v[...])
            if mix:
                mix_chunk(0)
            if project:
                x2 = x_ref[...] + y
                r2 = lax.rsqrt(jnp.mean(x2 * x2, axis=-1, keepdims=True) + EPS)
                n2 = x2 * r2
                gfv = gf_ref[...]
                err = n2 * gfv - t_ref[...]
                loss = 0.5 * jnp.sum(jnp.mean(err * err, axis=-1, keepdims=True), axis=0, keepdims=True)
                dy = err * (1.0 / D_MODEL)
                part_ref[1:2, :] += jnp.sum(dy * n2, axis=0, keepdims=True)
                part_ref[7:8, :] += jnp.broadcast_to(loss, (1, D_MODEL))
                dn = dy * gfv
                dx2 = r2 * (dn - n2 * jnp.mean(dn * n2, axis=-1, keepdims=True))
                dx2_ref[...] = dx2
                dx2_b = dx2.astype(BF16)
            if mix:
                mix_chunk(1)
            if project:
                dm_ref[...] = _dot_nt(dx2_b, wog_v[...])
            if mix:
                mix_chunk(2)
            if project:
                acc_ref[...] += _dot_tn(mixed_b, dx2_b)
            if mix:
                mix_chunk(3)
                ring[pl.ds(pl.multiple_of(i * TB, TB), TB), :] = stage[...]

        @pl.when(i < LAG)
        def _():
            step(True, False)

        @pl.when((i >= LAG) & (i < nblk))
        def _():
            step(True, True)

        @pl.when(i >= nblk)
        def _():
            step(False, True)

        @pl.when(i == n_steps - 1)
        def _():
            gwo_ref[...] = acc_ref[...].astype(BF16)
            for cp in wo_direct + wo_passed:
                cp.wait_send()

    assert NCB == 4
    row = lambda w: pl.BlockSpec((1, w), lambda i: (0, 0))
    mix_blk = lambda i: jnp.minimum(i, nblk - 1)
    out_blk = lambda i: jnp.clip(i - LAG, 0, nblk - 1)
    tok = lambda: pl.BlockSpec((TB, D_MODEL), lambda i: (out_blk(i), 0))
    return pl.pallas_call(
        body, name="mix_out", grid=(n_steps,),
        out_shape=(jax.ShapeDtypeStruct((SEQ, AUX_COLS), F32),
                   jax.ShapeDtypeStruct((N_CHUNKS, N_HEADS, HEAD, HEAD), BF16),
                   jax.ShapeDtypeStruct((SEQ, D_MODEL), F32),
                   jax.ShapeDtypeStruct((SEQ, D_MODEL), F32),
                   jax.ShapeDtypeStruct((D_MODEL, D_MODEL), BF16),
                   jax.ShapeDtypeStruct((8, D_MODEL), F32)),
        in_specs=[pl.BlockSpec((TB, 4096), lambda i: (jnp.minimum(i, nblk - 1), 0)),
                  pl.BlockSpec((2, D_HGRN), lambda i: (0, 0)),
                  pl.BlockSpec((8, D_CONV), lambda i: (0, 0)),
                  row(D_HGRN), row(D_CONV),
                  pl.BlockSpec((HEAD, HEAD), lambda i: (0, 0)),
                  pl.BlockSpec((1, WO_ROWS, D_MODEL), lambda i: (0, 0, 0)),
                  tok(), row(D_MODEL), tok()],
        out_specs=(pl.BlockSpec((TB, AUX_COLS), lambda i: (mix_blk(i), 0)),
                   pl.BlockSpec((NCB, N_HEADS, HEAD, HEAD), lambda i: (mix_blk(i), 0, 0, 0)),
                   tok(), tok(),
                   pl.BlockSpec((D_MODEL, D_MODEL), lambda i: (0, 0)),
                   pl.BlockSpec((8, D_MODEL), lambda i: (0, 0))),
        scratch_shapes=[pltpu.VMEM((N_HEADS, HEAD, HEAD), F32), pltpu.VMEM((8, D_CONV), F32),
                        pltpu.VMEM((D_MODEL, D_MODEL), BF16), pltpu.VMEM((TB, D_MODEL), BF16),
                        pltpu.VMEM((SEQ, D_MODEL), BF16), pltpu.VMEM((D_MODEL, D_MODEL), F32),
                        pltpu.SemaphoreType.DMA((6,)), pltpu.SemaphoreType.DMA((6,))],
        compiler_params=pltpu.CompilerParams(dimension_semantics=("arbitrary",), vmem_limit_bytes=VMEM_LIMIT,
                                             collective_id=COLLECTIVE_MIX_OUT),
    )(proj, lb_logits, cw, ga, gcn, g64, w_out, x2d, gf, tgt)


def _mix_bwd(proj, aux, states, dmixed, lb_logits, cw, ga, gcn, g64):
    nblk = SEQ // TB

    def body(p_ref, aux_ref, st_ref, dm_ref, lbl_ref, cw_ref, ga_ref, gcn_ref, g64_ref,
             dp_ref, part_ref, dst_ref, head_ref, dlb_ref):
        i = pl.program_id(0)

        @pl.when(i == 0)
        def _():
            dst_ref[...] = jnp.zeros_like(dst_ref)
            head_ref[...] = jnp.zeros_like(head_ref)
            part_ref[...] = jnp.zeros_like(part_ref)
            dlb_ref[...] = jnp.zeros_like(dlb_ref)

        lb = _lower_bound(lbl_ref[...])
        triu = _tri(False)
        causal = _causal()
        g64m = g64_ref[...]
        rowsum = lambda a: jnp.sum(a, axis=0, keepdims=True)
        heads = range(N_HEADS)
        cs = [slice(hd * HEAD, (hd + 1) * HEAD) for hd in heads]
        col = lambda base, hd: slice(base + hd * HEAD, base + (hd + 1) * HEAD)
        for n in reversed(range(NCB)):
            sl = pl.ds(n * CHUNK, CHUNK)
            cvv = [aux_ref[sl, col(AUX_CV, hd)] for hd in heads]
            gb = [p_ref[sl, col(2560, hd)] for hd in heads]
            yb = [gb[hd] * cvv[hd] for hd in heads]
            ms = _group_mean_many([y * y for y in yb], g64m)
            rb, nb, dnb = [], [], []
            for hd in heads:
                rb.append(lax.rsqrt(ms[hd] + EPS))
                nb.append(yb[hd] * rb[hd])
                zb = p_ref[sl, col(3584, hd)]
                sgb = _sigmoid(zb)
                dmb = dm_ref[sl, col(512, hd)]
                silu = zb * sgb
                dgate = dmb * gcn_ref[:, cs[hd]]
                part_ref[2:3, col(512, hd)] += rowsum(dmb * nb[hd] * silu)
                dp_ref[sl, col(3584, hd)] = (dgate * nb[hd] * (sgb + silu * (1.0 - sgb))).astype(BF16)
                dnb.append(dgate * silu)
            mdn = _group_mean_many([dnb[hd] * nb[hd] for hd in heads], g64m)
            for hd in heads:
                dyb = rb[hd] * (dnb[hd] - nb[hd] * mdn[hd])
                dp_ref[sl, col(2560, hd)] = (dyb * cvv[hd]).astype(BF16)
                dcv = dyb * gb[hd]
                head = head_ref[:, cs[hd]]
                dcv1 = _shift_up(dcv, 1, head)
                dcv2 = _shift_up(dcv, 2, head)
                head_ref[:, cs[hd]] = dcv[0:8, :]
                u = p_ref[sl, col(2048, hd)]
                gc = p_ref[sl, col(3072, hd)]
                cu = gc * u
                part_ref[4:5, cs[hd]] += rowsum(dcv2 * cu)
                part_ref[5:6, cs[hd]] += rowsum(dcv1 * cu)
                part_ref[6:7, cs[hd]] += rowsum(dcv * cu)
                dcu = cw_ref[2:3, cs[hd]] * dcv + cw_ref[1:2, cs[hd]] * dcv1 + cw_ref[0:1, cs[hd]] * dcv2
                dp_ref[sl, col(3072, hd)] = (dcu * u).astype(BF16)
                dp_ref[sl, col(2048, hd)] = (dcu * gc).astype(BF16)
            do_b = []
            for hd in heads:
                ov = aux_ref[sl, col(AUX_O, hd)]
                ra = lax.rsqrt(jnp.mean(ov * ov, axis=-1, keepdims=True) + EPS)
                na = ov * ra
                za = p_ref[sl, col(1536, hd)]
                sga = _sigmoid(za)
                dma = dm_ref[sl, cs[hd]]
                silu = za * sga
                dgate = dma * ga_ref[:, cs[hd]]
                part_ref[2:3, cs[hd]] += rowsum(dma * na * silu)
                dp_ref[sl, col(1536, hd)] = (dgate * na * (sga + silu * (1.0 - sga))).astype(BF16)
                dna = dgate * silu
                do_b.append((ra * (dna - na * jnp.mean(dna * na, axis=-1, keepdims=True))).astype(BF16))
            s = [_sigmoid(p_ref[sl, col(512, hd)]) for hd in heads]
            f = [lb[:, cs[hd]] + (1.0 - lb[:, cs[hd]]) * s[hd] for hd in heads]
            bc = [aux_ref[sl, col(AUX_B, hd)] for hd in heads]
            g = [bc[hd][CHUNK - 1:CHUNK, :] for hd in heads]
            eb = [jnp.exp(bc[hd]) for hd in heads]
            enb = [jnp.exp(-bc[hd]) for hd in heads]
            eg = [jnp.exp(g[hd] - bc[hd]) for hd in heads]
            dec = [jnp.exp(g[hd]) for hd in heads]
            qd = [p_ref[sl, cs[hd]] * eb[hd] for hd in heads]
            kk = [1.0 - f[hd] for hd in heads]
            ki = [kk[hd] * enb[hd] for hd in heads]
            ke = [kk[hd] * eg[hd] for hd in heads]
            qd_b = [a.astype(BF16) for a in qd]
            ki_b = [a.astype(BF16) for a in ki]
            ke_b = [a.astype(BF16) for a in ke]
            vb = [p_ref[sl, col(1024, hd)].astype(BF16) for hd in heads]
            st_b = [st_ref[n, hd] for hd in heads]
            dst = [dst_ref[hd] for hd in heads]
            dst_b = [a.astype(BF16) for a in dst]
            scm = [_dot_nt(qd_b[hd], ki_b[hd]) for hd in heads]
            amm = [_dot_nt(do_b[hd], vb[hd]) for hd in heads]
            dqd2 = [_dot(do_b[hd], st_b[hd]) for hd in heads]
            dke = [_dot(vb[hd], dst_b[hd]) for hd in heads]
            dv2 = [_dot_nt(ke_b[hd], dst_b[hd]) for hd in heads]
            dsu = [_dot_tn(do_b[hd], qd_b[hd]) for hd in heads]
            sc = [jnp.where(causal, scm[hd], 0.0).astype(BF16) for hd in heads]
            am = [jnp.where(causal, amm[hd], 0.0).astype(BF16) for hd in heads]
            dqd1 = [_dot(am[hd], ki_b[hd]) for hd in heads]
            dki = [_dot_tn(am[hd], qd_b[hd]) for hd in heads]
            dv1 = [_dot_tn(sc[hd], do_b[hd]) for hd in heads]
            db, dgv, dkk = [], [], []
            for hd in heads:
                dqd = dqd1[hd] + dqd2[hd]
                ddec = rowsum(dst[hd] * st_b[hd].astype(F32))
                dst_ref[hd] = dst[hd] * dec[hd] + dsu[hd]
                dp_ref[sl, cs[hd]] = (dqd * eb[hd]).astype(BF16)
                dp_ref[sl, col(1024, hd)] = (dv1[hd] + dv2[hd]).astype(BF16)
                dke_eg = dke[hd] * eg[hd]
                dkk.append(dki[hd] * enb[hd] + dke_eg)
                db.append(dqd * qd[hd] - kk[hd] * dkk[hd])
                dgv.append(rowsum(kk[hd] * dke_eg) + ddec * dec[hd])
            rc = _exact_left_many(triu, db, 2)
            for hd in heads:
                df = (rc[hd] + dgv[hd]) / f[hd] - dkk[hd]
                one_s = 1.0 - s[hd]
                dlb_ref[:, cs[hd]] += rowsum(df * one_s)
                dp_ref[sl, col(512, hd)] = (df * (1.0 - lb[:, cs[hd]]) * s[hd] * one_s).astype(BF16)

        @pl.when(i == nblk - 1)
        def _():
            row = dlb_ref[...] * lb * (1.0 - lb)
            part_ref[3:4, 0:D_HGRN] = row
            part_ref[3:4, D_HGRN:] = -row

    rev = lambda w: pl.BlockSpec((TB, w), lambda i: (nblk - 1 - i, 0))
    row = lambda w: pl.BlockSpec((1, w), lambda i: (0, 0))
    return pl.pallas_call(
        body, name="mix_bwd", grid=(nblk,),
        out_shape=(jax.ShapeDtypeStruct((SEQ, 4096), BF16),
                   jax.ShapeDtypeStruct((8, D_MODEL), F32)),
        in_specs=[rev(4096), rev(AUX_COLS),
                  pl.BlockSpec((NCB, N_HEADS, HEAD, HEAD), lambda i: (nblk - 1 - i, 0, 0, 0)),
                  rev(D_MODEL),
                  pl.BlockSpec((2, D_HGRN), lambda i: (0, 0)),
                  pl.BlockSpec((8, D_CONV), lambda i: (0, 0)),
                  row(D_HGRN), row(D_CONV),
                  pl.BlockSpec((HEAD, HEAD), lambda i: (0, 0))],
        out_specs=(rev(4096), pl.BlockSpec((8, D_MODEL), lambda i: (0, 0))),
        scratch_shapes=[pltpu.VMEM((N_HEADS, HEAD, HEAD), F32), pltpu.VMEM((8, D_CONV), F32),
                        pltpu.VMEM((1, D_HGRN), F32)],
        compiler_params=pltpu.CompilerParams(dimension_semantics=("arbitrary",), vmem_limit_bytes=VMEM_LIMIT),
    )(proj, aux, states, dmixed, lb_logits, cw, ga, gcn, g64)


TT = 1024
TX = 512
(SEM_D2D, SEM_D2D_O, SEM_ICI, SEM_ICI_O, SEM_FIN, SEM_FIN_O, SEM_SMALL, SEM_VIA, SEM_NORM, N_SEM_TAIL) = (
    0, 4, 5, 8, 11, 12, 12, 20, 22, 30)


def _bwd_tail(kidx, h, dproj, wg, gwo, x2d, dx2, g1, small_a, small_b):
    hw = D_MODEL // 2
    ho = WO_ROWS // 2
    nt = SEQ // TT
    norm_step = 2 * N_SHARD
    n_steps = norm_step + SEQ // TX // nt

    def body(k_ref, h_ref, dp_ref, w_ref, gwo_ref, x_ref, dx2_ref, g_ref, sm_ref, smb_ref,
             gx_ref, gw_out, gwo_out, osm_ref,
             acc, dh, sendbuf, keep, sibrcv, rcv, merge, sib_o, p_o, rcv_o, res_o, sm_buf, dng_buf, dng,
             send_sems, recv_sems, out_sems):
        s, t = pl.program_id(0), pl.program_id(1)
        x, y, c = lax.axis_index("x"), lax.axis_index("y"), lax.axis_index("c")
        k = 2 * x + y
        me = 4 * x + 2 * y + c
        sibling = (x, y, 1 - c)
        chips = [(1 - x, 1 - y), (1 - x, y), (x, 1 - y)]
        kjs = [2 * cx + cy for cx, cy in chips]
        mine = pl.ds(pl.multiple_of(c * hw, hw), hw)
        other = pl.ds(pl.multiple_of((1 - c) * hw, hw), hw)
        mine_o = pl.ds(pl.multiple_of(c * ho, ho), ho)
        other_o = pl.ds(pl.multiple_of((1 - c) * ho, ho), ho)

        def copy(sem, src, dst, to):
            return pltpu.make_async_remote_copy(
                src_ref=src, dst_ref=dst, send_sem=send_sems.at[sem], recv_sem=recv_sems.at[sem],
                device_id=to, device_id_type=MESH)

        def at_step(sv, tv):
            return pl.when((s == sv) & (t == tv))

        def at_norm_block(b):
            return at_step(norm_step + b // nt, b % nt)

        d2d = [copy(SEM_D2D + sv, sendbuf.at[sv], sibrcv.at[sv], sibling) for sv in range(N_SHARD)]
        d2d_o = copy(SEM_D2D_O, gwo_ref.at[:, other_o, :], sib_o, sibling)
        ici = {sv: copy(SEM_ICI + sv, keep.at[sv], rcv.at[sv - 1], (*chips[sv], c)) for sv in (1, 2)}
        qh = hw // 2
        via = [copy(SEM_VIA, keep.at[0, 0:qh, :], merge.at[1], (*chips[1], c)),
               copy(SEM_VIA + 1, keep.at[0, qh:hw, :], merge.at[0], (*chips[2], c))]
        merged_rows = [slice(qh, hw), slice(0, qh)]
        ici_o = [copy(SEM_ICI_O + sv, p_o.at[kjs[sv]], rcv_o.at[sv], (*chips[sv], c)) for sv in range(3)]
        fin = copy(SEM_FIN, acc.at[mine, :], gw_out.at[mine, :], sibling)
        fin_o = copy(SEM_FIN_O, res_o.at[mine_o, :], res_o.at[mine_o, :], sibling)
        peers = [(x ^ (m >> 2), y ^ ((m >> 1) & 1), c ^ (m & 1)) for m in range(1, N_DEV)]
        smalls = [copy(SEM_SMALL + 1 + j, sm_buf.at[me], sm_buf.at[me], to) for j, to in enumerate(peers)]
        dngs = [copy(SEM_NORM + 1 + j, dng_buf.at[me], dng_buf.at[me], to) for j, to in enumerate(peers)]
        store_w = pltpu.make_async_copy(acc.at[mine, :], gw_out.at[mine, :], out_sems.at[0])
        store_o = pltpu.make_async_copy(res_o, gwo_out, out_sems.at[1])

        @at_step(0, 0)
        def _():
            barrier = pltpu.get_barrier_semaphore()
            for to in peers:
                pl.semaphore_signal(barrier, inc=1, device_id=to, device_id_type=MESH)
            sm_buf[me] = sm_ref[...] + smb_ref[...]
            pl.semaphore_wait(barrier, N_DEV - 1)
            d2d_o.start()
            for cp in smalls:
                cp.start()

        @at_step(0, 1)
        def _():
            d2d_o.wait_recv()
            for j in range(N_SHARD):
                p_o[j] = (gwo_ref[j, mine_o, :].astype(F32) + sib_o[j].astype(F32)).astype(BF16)
            res_o[mine_o, :] = gwo_ref[k, mine_o, :].astype(F32) + sib_o[k].astype(F32)
            for cp in ici_o:
                cp.start()

        rows = pl.ds(pl.multiple_of(t * TT, TT), TT)

        @pl.when((s < N_SHARD) & (t == 0))
        def _():
            acc[...] = _dot_tn(h_ref[...], dp_ref[...])

        @pl.when((s < N_SHARD) & (t > 0))
        def _():
            acc[...] += _dot_tn(h_ref[...], dp_ref[...])

        for sv in range(N_SHARD):
            @at_step(sv, nt - 1)
            def _(sv=sv):
                sendbuf[sv] = acc[other, :].astype(BF16)
                if sv < 3:
                    keep[sv] = acc[mine, :].astype(BF16)
                d2d[sv].start()

        @at_step(1, 0)
        def _():
            d2d[0].wait_recv()
            keep[0] = (keep[0].astype(F32) + sibrcv[0].astype(F32)).astype(BF16)
            for cp in via:
                cp.start()

        for sv in (1, 2):
            @at_step(sv + 2, 0)
            def _(sv=sv):
                d2d[sv].wait_recv()
                keep[sv] = (keep[sv].astype(F32) + sibrcv[sv].astype(F32)).astype(BF16)
                via[2 - sv].wait_recv()
                rows_m = merged_rows[sv - 1]
                keep[sv, rows_m, :] = (keep[sv, rows_m, :].astype(F32) + merge[sv - 1].astype(F32)).astype(BF16)
                ici[sv].start()

        @pl.when(s == N_SHARD)
        def _():
            dh[rows, :] = _dot_nt(dp_ref[...], w_ref[0])

        @pl.when((s > N_SHARD) & (s < norm_step))
        def _():
            dh[rows, :] += _dot_nt(dp_ref[...], w_ref[0])

        @at_norm_block(0)
        def _():
            d2d[3].wait_recv()
            acc[mine, :] += sibrcv[3].astype(F32)

        @at_norm_block(1)
        def _():
            tot = res_o[mine_o, :]
            for sv in range(3):
                ici_o[sv].wait_recv()
                tot = tot + rcv_o[sv].astype(F32)
            res_o[mine_o, :] = tot
            fin_o.start()

        @at_norm_block(2)
        def _():
            ici[1].wait_recv()
            acc[mine, :] += rcv[0].astype(F32)

        @at_norm_block(SEQ // TX - 2)
        def _():
            ici[2].wait_recv()
            acc[mine, :] += rcv[1].astype(F32)
            fin.start()
            store_w.start()
            fin_o.wait_recv()
            store_o.start()

        @at_norm_block(0)
        def _():
            dng[...] = jnp.zeros_like(dng)

        @pl.when(s >= norm_step)
        def _():
            blk = (s - norm_step) * nt + t
            dhv = dh[pl.ds(pl.multiple_of(blk * TX, TX), TX), :]
            xv = x_ref[...]
            r = lax.rsqrt(jnp.mean(xv * xv, axis=-1, keepdims=True) + EPS)
            xn = xv * r
            dng[...] += jnp.sum(dhv * xn, axis=0, keepdims=True)
            dxn = dhv * g_ref[...]
            gx_ref[...] = dx2_ref[...] + r * (dxn - xn * jnp.mean(dxn * xn, axis=-1, keepdims=True))

        @at_step(n_steps - 1, nt - 1)
        def _():
            dng_buf[me] = dng[...]
            for cp in dngs:
                cp.start()
            for m in range(1, N_DEV):
                copy(SEM_SMALL + m, sm_buf.at[0], sm_buf.at[0], sibling).wait_recv()
            tot = sm_buf[0]
            for d in range(1, N_DEV):
                tot = tot + sm_buf[d]
            osm_ref[...] = tot
            for m in range(1, N_DEV):
                copy(SEM_NORM + m, dng_buf.at[0], dng_buf.at[0], sibling).wait_recv()
            tot = dng_buf[0]
            for d in range(1, N_DEV):
                tot = tot + dng_buf[d]
            osm_ref[0:1, :] = tot
            fin.wait_recv()
            for cp in d2d + [d2d_o] + via + list(ici.values()) + ici_o + [fin, fin_o] + smalls + dngs:
                cp.wait_send()
            store_o.wait()
            store_w.wait()

    def shard_of(s, kr):
        order = jnp.where(s < N_SHARD, s, jnp.where(s < norm_step, s - N_SHARD, 3))
        return kr[0] ^ (3 - order)

    def h_map(s, t, kr):
        return (jnp.where(s < N_SHARD, t, nt - 1), 0)

    def dp_map(s, t, kr):
        return (jnp.where(s < norm_step, t, nt - 1), shard_of(s, kr))

    def w_map(s, t, kr):
        return (shard_of(jnp.maximum(s, N_SHARD), kr), 0, 0)

    def blk_map(s, t, kr):
        return (jnp.where(s < norm_step, 0, (s - norm_step) * nt + t), 0)

    hbm = pl.BlockSpec(memory_space=pl.ANY)
    grid_spec = pltpu.PrefetchScalarGridSpec(
        num_scalar_prefetch=1, grid=(n_steps, nt),
        in_specs=[pl.BlockSpec((TT, D_MODEL), h_map),
                  pl.BlockSpec((TT, SHARD_COLS), dp_map),
                  pl.BlockSpec((1, D_MODEL, SHARD_COLS), w_map),
                  pl.BlockSpec((N_SHARD, WO_ROWS, D_MODEL), lambda s, t, kr: (0, 0, 0),
                               pipeline_mode=pl.Buffered(1)),
                  pl.BlockSpec((TX, D_MODEL), blk_map),
                  pl.BlockSpec((TX, D_MODEL), blk_map),
                  pl.BlockSpec((1, D_MODEL), lambda s, t, kr: (0, 0)),
                  pl.BlockSpec((8, D_MODEL), lambda s, t, kr: (0, 0)),
                  pl.BlockSpec((8, D_MODEL), lambda s, t, kr: (0, 0))],
        out_specs=(pl.BlockSpec((TX, D_MODEL), blk_map), hbm, hbm,
                   pl.BlockSpec((8, D_MODEL), lambda s, t, kr: (0, 0))),
        scratch_shapes=[pltpu.VMEM((D_MODEL, SHARD_COLS), F32), pltpu.VMEM((SEQ, D_MODEL), F32),
                        pltpu.VMEM((N_SHARD, hw, SHARD_COLS), BF16), pltpu.VMEM((3, hw, SHARD_COLS), BF16),
                        pltpu.VMEM((N_SHARD, hw, SHARD_COLS), BF16), pltpu.VMEM((2, hw, SHARD_COLS), BF16),
                        pltpu.VMEM((2, hw // 2, SHARD_COLS), BF16),
                        pltpu.VMEM((N_SHARD, ho, D_MODEL), BF16), pltpu.VMEM((N_SHARD, ho, D_MODEL), BF16),
                        pltpu.VMEM((3, ho, D_MODEL), BF16), pltpu.VMEM((WO_ROWS, D_MODEL), F32),
                        pltpu.VMEM((N_DEV, 8, D_MODEL), F32), pltpu.VMEM((N_DEV, 1, D_MODEL), F32),
                        pltpu.VMEM((1, D_MODEL), F32),
                        pltpu.SemaphoreType.DMA((N_SEM_TAIL,)), pltpu.SemaphoreType.DMA((N_SEM_TAIL,)),
                        pltpu.SemaphoreType.DMA((2,))])
    return pl.pallas_call(
        body, name="bwd_tail", grid_spec=grid_spec,
        out_shape=(jax.ShapeDtypeStruct((SEQ, D_MODEL), F32),
                   jax.ShapeDtypeStruct((D_MODEL, SHARD_COLS), F32),
                   jax.ShapeDtypeStruct((WO_ROWS, D_MODEL), F32),
                   jax.ShapeDtypeStruct((8, D_MODEL), F32)),
        compiler_params=pltpu.CompilerParams(dimension_semantics=("arbitrary", "arbitrary"),
                                             vmem_limit_bytes=61 * 1024 * 1024, collective_id=COLLECTIVE_TAIL),
    )(kidx, h, dproj, wg, gwo, x2d, dx2, g1, small_a, small_b)


def _adam_update(w, g, m, v):
    nm = ADAM_B1 * m + (1.0 - ADAM_B1) * g
    nv = ADAM_B2 * v + (1.0 - ADAM_B2) * (g * g)
    m_hat = nm / (1.0 - ADAM_B1 ** ADAM_STEP)
    v_hat = nv / (1.0 - ADAM_B2 ** ADAM_STEP)
    return -ADAM_LR * (m_hat / (jnp.sqrt(v_hat) + ADAM_EPS) + ADAM_WD * w), nm, nv


def _adamw_all(tot, g_w_in, g_w_out, big, small, grad_x):
    n = len(small)
    rows = WO_ROWS
    steps = D_MODEL // rows

    def body(tot_ref, *refs):
        gx_ref, gx_out = refs[2 + 3 * (2 + n)], refs[-1]
        gx_out[...] = gx_ref[...]
        ins, outs = refs[:2 + 3 * (2 + n)], refs[3 + 3 * (2 + n):-1]
        g_refs, wmv = ins[:2], ins[2:]
        loss_ref, quads = outs[0], outs[1:]

        def update(j, g):
            w_ref, m_ref, v_ref = wmv[3 * j:3 * j + 3]
            g_ref, d_ref, nm_ref, nv_ref = quads[4 * j:4 * j + 4]
            g_ref[...] = g
            d_ref[...], nm_ref[...], nv_ref[...] = _adam_update(w_ref[...], g, m_ref[...], v_ref[...])

        update(0, g_refs[0][...])

        @pl.when(pl.program_id(0) == 0)
        def _():
            update(1, g_refs[1][...])
            k = 2 * lax.axis_index("x") + lax.axis_index("y")
            mine = pl.ds(pl.multiple_of(k * HEAD, HEAD), HEAD)
            loss_ref[...] = tot_ref[7:8, 0:1]
            grads = [tot_ref[0:1, :], tot_ref[1:2, :], tot_ref[2:3, 0:D_HGRN], tot_ref[2:3, D_HGRN:],
                     jnp.concatenate([tot_ref[3:4, 0:D_HGRN], tot_ref[3:4, D_HGRN:]], axis=0),
                     jnp.concatenate([tot_ref[4 + tap:5 + tap, mine] for tap in range(3)], axis=1)]
            for j, g in enumerate(grads):
                update(2 + j, g)

    whole = lambda a: pl.BlockSpec(a.shape, lambda i: (0, 0))
    blk = pl.BlockSpec((rows, SHARD_COLS), lambda i: (i, 0))
    arrays = [a for triple in big + small for a in triple]
    in_specs = ([whole(tot), blk, whole(g_w_out)] + [blk] * 3 + [whole(a) for a in arrays[3:]])
    shapes = [big[0][0], big[1][0]] + [w for w, _, _ in small]
    out_shape = (jax.ShapeDtypeStruct((1, 1), F32),) + tuple(
        jax.ShapeDtypeStruct(w.shape, F32) for w in shapes for _ in range(4))
    out_specs = (pl.BlockSpec((1, 1), lambda i: (0, 0)),) + (blk,) * 4 + tuple(
        whole(w) for w in shapes[1:] for _ in range(4))
    gx_blk = pl.BlockSpec((SEQ // steps, D_MODEL), lambda i: (i, 0))
    outs = pl.pallas_call(
        body, name="adamw_all", grid=(steps,),
        out_shape=out_shape + (jax.ShapeDtypeStruct(grad_x.shape, F32),),
        in_specs=in_specs + [gx_blk], out_specs=out_specs + (gx_blk,),
        compiler_params=pltpu.CompilerParams(dimension_semantics=("arbitrary",), vmem_limit_bytes=VMEM_LIMIT),
    )(tot, g_w_in, g_w_out, *arrays, grad_x)
    return [outs[0]] + [outs[1 + 4 * j:5 + 4 * j] for j in range(2 + n)] + [outs[-1]]


def _local_step(x2d, tgt, proj, lb_logits, cw, ga, gcn, w_out, gf):
    g64 = _group_matrix(HEAD, CONV_GROUP)
    aux, states, dx2, dmixed, gwo, part_out = _mix_out(proj, lb_logits, cw, ga, gcn, g64, w_out, x2d, gf, tgt)
    dproj, part_mix = _mix_bwd(proj, aux, states, dmixed, lb_logits, cw, ga, gcn, g64)
    return dproj, dx2, gwo.reshape(N_SHARD, WO_ROWS, D_MODEL), part_out, part_mix


def kernel(x, norm_gain, w_in, lb_logits, conv_w, hgrn_norm_gain, conv_norm_gain, w_out, final_norm_gain, loss_target, m_norm_gain, m_w_in, m_lb_logits, m_conv_w, m_hgrn_norm_gain, m_conv_norm_gain, m_w_out, m_final_norm_gain, v_norm_gain, v_w_in, v_lb_logits, v_conv_w, v_hgrn_norm_gain, v_conv_norm_gain, v_w_out, v_final_norm_gain):
    k = 2 * lax.axis_index("x") + lax.axis_index("y")
    kidx = jnp.reshape(k, (1,)).astype(jnp.int32)
    row = lambda a: a.reshape(1, D_MODEL)
    taps = lambda a: a.reshape(1, 3 * HEAD)
    h, proj, wg, cw = _gather_proj(kidx, x[0], norm_gain, w_in, taps(conv_w))
    dproj, dx2, gwo, part_out, part_mix = _local_step(
        x[0], loss_target[0], proj, lb_logits, cw, hgrn_norm_gain, conv_norm_gain, w_out, row(final_norm_gain))
    rgrad_x, rg_w_in, rg_w_out, tot = _bwd_tail(kidx, h, dproj, wg, gwo, x[0], dx2, norm_gain, part_out, part_mix)

    (loss, (g_w_in, d_w_in, nm_w_in, nv_w_in), (g_w_out, d_w_out, nm_w_out, nv_w_out),
     (g_norm_gain, d_ng, nm_ng, nv_ng), (g_final, d_fg, nm_fg, nv_fg), (g_hgrn, d_hg, nm_hg, nv_hg),
     (g_convn, d_cg, nm_cg, nv_cg), (g_lb, d_lb, nm_lb, nv_lb), (g_conv_w, d_cw, nm_cw, nv_cw),
     grad_x) = _adamw_all(
        tot, rg_w_in, rg_w_out,
        [(w_in[0], m_w_in[0], v_w_in[0]), (w_out[0], m_w_out[0], v_w_out[0])],
        [(norm_gain, m_norm_gain, v_norm_gain),
         (row(final_norm_gain), row(m_final_norm_gain), row(v_final_norm_gain)),
         (hgrn_norm_gain, m_hgrn_norm_gain, v_hgrn_norm_gain),
         (conv_norm_gain, m_conv_norm_gain, v_conv_norm_gain),
         (lb_logits, m_lb_logits, v_lb_logits),
         (taps(conv_w), taps(m_conv_w), taps(v_conv_w))],
        rgrad_x)
    flat = lambda a: a.reshape(D_MODEL)
    untap = lambda a: a.reshape(1, 3, HEAD)
    return (loss.reshape(()), grad_x[None],
            g_norm_gain, g_w_in[None], g_lb, untap(g_conv_w), g_hgrn, g_convn, g_w_out[None], flat(g_final),
            d_ng, d_w_in[None], d_lb, untap(d_cw), d_hg, d_cg, d_w_out[None], flat(d_fg),
            nm_ng, nm_w_in[None], nm_lb, untap(nm_cw), nm_hg, nm_cg, nm_w_out[None], flat(nm_fg),
            nv_ng, nv_w_in[None], nv_lb, untap(nv_cw), nv_hg, nv_cg, nv_w_out[None], flat(nv_fg))
```

```python
import jax
import jax.numpy as jnp
import numpy as np
from jax import lax
from jax.experimental import pallas as pl
from jax.experimental.pallas import tpu as pltpu

F32 = jnp.float32
BF16 = jnp.bfloat16
MESH = pl.DeviceIdType.MESH

SEQ = 2048
D_MODEL = 1024
D_HGRN = 512
D_CONV = 512
HEAD = 128
N_HEADS = 4
CHUNK = 64
CONV_GROUP = 64
N_SHARD = 4
SHARD_COLS = 1024
WO_ROWS = 256
EPS = 1e-6
TB = 256
NCB = TB // CHUNK
N_CHUNKS = SEQ // CHUNK
N_DEV = 8
COLLECTIVE_GATHER, COLLECTIVE_MIX_OUT, COLLECTIVE_TAIL = 1, 0, 2
AUX_O, AUX_CV, AUX_B, AUX_COLS = 0, 512, 1024, 1536

ADAM_LR = 0.001
ADAM_B1 = 0.9
ADAM_B2 = 0.999
ADAM_EPS = 1e-08
ADAM_WD = 0.01
ADAM_STEP = 10

VMEM_LIMIT = 56 * 1024 * 1024


def _dot(a, b):
    return jnp.dot(a, b, preferred_element_type=F32)


def _dot_nt(a, b):
    return lax.dot_general(a, b, (((1,), (1,)), ((), ())), preferred_element_type=F32)


def _dot_tn(a, b):
    return lax.dot_general(a, b, (((0,), (0,)), ((), ())), preferred_element_type=F32)


def _split_bf16(x, n):
    parts = []
    r = x
    for _ in range(n):
        p = r.astype(BF16)
        parts.append(p)
        r = r - p.astype(F32)
    return parts


def _exact_left(m, x, n=3):
    acc = None
    for p in _split_bf16(x, n):
        t = _dot(m, p)
        acc = t if acc is None else acc + t
    return acc


def _exact_left_many(m, xs, n=3):
    parts = [_split_bf16(x, n) for x in xs]
    accs = [None] * len(xs)
    for i in range(n):
        for j in range(len(xs)):
            t = _dot(m, parts[j][i])
            accs[j] = t if accs[j] is None else accs[j] + t
    return accs


def _group_mean_many(xs, gmat, n=2):
    parts = [_split_bf16(x, n) for x in xs]
    accs = [None] * len(xs)
    for i in range(n):
        for j in range(len(xs)):
            t = _dot(parts[j][i], gmat)
            accs[j] = t if accs[j] is None else accs[j] + t
    return accs


def _group_mean(x, gmat, n=2):
    w = gmat.shape[0]
    outs = []
    for c0 in range(0, x.shape[1], w):
        acc = None
        for p in _split_bf16(x[:, c0:c0 + w], n):
            t = _dot(p, gmat)
            acc = t if acc is None else acc + t
        outs.append(acc)
    return jnp.concatenate(outs, axis=1)


def _sigmoid(x):
    return 1.0 / (1.0 + jnp.exp(-x))


def _lower_bound(lbl):
    l0 = lbl[0:1, :]
    l1 = lbl[1:2, :]
    m = jnp.maximum(l0, l1)
    e0 = jnp.exp(l0 - m)
    e1 = jnp.exp(l1 - m)
    return e0 / (e0 + e1)


def _tri(lower):
    r = lax.broadcasted_iota(jnp.int32, (CHUNK, CHUNK), 0)
    c = lax.broadcasted_iota(jnp.int32, (CHUNK, CHUNK), 1)
    return jnp.where((c <= r) if lower else (c >= r), 1.0, 0.0).astype(BF16)


def _causal():
    r = lax.broadcasted_iota(jnp.int32, (CHUNK, CHUNK), 0)
    c = lax.broadcasted_iota(jnp.int32, (CHUNK, CHUNK), 1)
    return c <= r


def _shift_down(x, sh, prev_tail):
    r = pltpu.roll(x, sh, 0)
    pt = pltpu.roll(prev_tail, sh, 0)
    rows = lax.broadcasted_iota(jnp.int32, prev_tail.shape, 0)
    top = jnp.where(rows < sh, pt, r[0:8])
    return jnp.concatenate([top, r[8:]], axis=0)


def _shift_up(x, sh, next_head):
    n = x.shape[0]
    r = pltpu.roll(x, n - sh, 0)
    nh = pltpu.roll(next_head, 8 - sh, 0)
    rows = lax.broadcasted_iota(jnp.int32, next_head.shape, 0)
    bot = jnp.where(rows >= 8 - sh, nh, r[n - 8:])
    return jnp.concatenate([r[:n - 8], bot], axis=0)


def _group_matrix(width, group):
    r = np.arange(width)[:, None] // group
    c = np.arange(width)[None, :] // group
    return jnp.asarray(np.where(r == c, 1.0 / group, 0.0), dtype=BF16)


TG = 1024
SEM_W, SEM_CW, SEM_W_FWD, N_SEM = 0, 4, 7, 11


def _gather_proj(kidx, x2d, g1, w_in, conv_w):
    half_w = D_MODEL // 2
    half_c = SHARD_COLS // 2
    nt = SEQ // TG
    n_steps = 2 * N_SHARD

    def body(k_ref, x_ref, g_ref, w_ref, cw_ref, h_ref, p_ref, wg_out, cwg_out,
             wg_v, cwg_v, send_sems, recv_sems, out_sems):
        s, t = pl.program_id(0), pl.program_id(1)
        x, y, c = lax.axis_index("x"), lax.axis_index("y"), lax.axis_index("c")
        k = 2 * x + y
        sibling = (x, y, 1 - c)
        chips = [(1 - x, y), (x, 1 - y), (1 - x, 1 - y)]
        kjs = [2 * cx + cy for cx, cy in chips]
        diag = (*chips[2], c)

        def w_half(kk, cc):
            return wg_v.at[kk, pl.ds(cc * half_w, half_w), :]

        def w_quarter(kk, cc, piece):
            return wg_v.at[kk, pl.ds(cc * half_w, half_w), piece * half_c:(piece + 1) * half_c]

        def cw_of(kk):
            return cwg_v.at[:, pl.ds(pl.multiple_of(kk * HEAD, HEAD), HEAD)]

        def copy(sem, ref, to):
            return pltpu.make_async_remote_copy(
                src_ref=ref, dst_ref=ref, send_sem=send_sems.at[sem], recv_sem=recv_sems.at[sem],
                device_id=to, device_id_type=MESH)

        def at_step(sv, tv):
            return pl.when((s == sv) & (t == tv))

        w_direct = ([copy(SEM_W + j, w_half(k, c), (*chips[j], c)) for j in range(2)]
                    + [copy(SEM_W + 2 + p, w_quarter(k, c, p), diag) for p in range(2)])
        cw_direct = [copy(SEM_CW + j, cw_of(k), (*chip, c)) for j, chip in enumerate(chips)]
        w_passed = ([copy(SEM_W_FWD + j, w_half(kjs[j], c), sibling) for j in range(2)]
                    + [copy(SEM_W_FWD + 2 + p, w_quarter(kjs[2], c, p), sibling) for p in range(2)])
        stores = ([pltpu.make_async_copy(wg_v.at[kk], wg_out.at[kk], out_sems.at[i])
                   for i, kk in enumerate([k] + kjs)]
                  + [pltpu.make_async_copy(cwg_v, cwg_out, out_sems.at[4])])

        @at_step(0, 0)
        def _():
            barrier = pltpu.get_barrier_semaphore()
            for peer in [sibling] + [(*chip, c) for chip in chips]:
                pl.semaphore_signal(barrier, inc=1, device_id=peer, device_id_type=MESH)
            wg_v[k] = w_ref[0].astype(BF16)
            mine = pl.ds(pl.multiple_of(k * HEAD, HEAD), HEAD)
            cwg_v[:, mine] = jnp.zeros((8, HEAD), F32)
            for tap in range(3):
                cwg_v[tap:tap + 1, mine] = cw_ref[:, tap * HEAD:(tap + 1) * HEAD]
            pl.semaphore_wait(barrier, 4)
            for cp in w_direct + cw_direct:
                cp.start()
            stores[0].start()

        @at_step(2, 0)
        def _():
            for j in range(2):
                copy(SEM_W + j, w_half(kjs[j], c), sibling).wait_recv()
                w_passed[j].start()
            copy(SEM_W_FWD, w_half(kjs[0], 1 - c), sibling).wait_recv()
            stores[1].start()

        @at_step(4, 0)
        def _():
            copy(SEM_W_FWD + 1, w_half(kjs[1], 1 - c), sibling).wait_recv()
            stores[2].start()

        for p in range(2):
            @at_step(6 + p, 0)
            def _(p=p):
                copy(SEM_W + 2 + p, w_quarter(kjs[2], c, p), sibling).wait_recv()
                w_passed[2 + p].start()
                copy(SEM_W_FWD + 2 + p, w_quarter(kjs[2], 1 - c, p), sibling).wait_recv()

        rows = pl.ds(pl.multiple_of(t * TG, TG), TG)

        @pl.when(s == 0)
        def _():
            xv = x_ref[...]
            r = lax.rsqrt(jnp.mean(xv * xv, axis=-1, keepdims=True) + EPS)
            h_ref[rows, :] = (xv * r * g_ref[...]).astype(BF16)

        sh = s >> 1
        js = k ^ (((sh & 1) << 1) | (sh >> 1))
        for piece in range(2):
            @pl.when((s & 1) == piece)
            def _(piece=piece):
                p_ref[...] = _dot(h_ref[rows, :], wg_v[js, :, piece * half_c:(piece + 1) * half_c])

        @at_step(n_steps - 1, nt - 1)
        def _():
            stores[3].start()
            for j in range(3):
                copy(SEM_CW + j, cw_of(kjs[j]), sibling).wait_recv()
            stores[4].start()
            for cp in w_direct + cw_direct + w_passed:
                cp.wait_send()
            for st in stores:
                st.wait()

    def x_map(s, t, kr):
        return (jnp.where(s == 0, t, nt - 1), 0)

    def p_map(s, t, kr):
        sh = s >> 1
        return (t, 2 * (kr[0] ^ (((sh & 1) << 1) | (sh >> 1))) + (s & 1))

    hbm = pl.BlockSpec(memory_space=pl.ANY)
    grid_spec = pltpu.PrefetchScalarGridSpec(
        num_scalar_prefetch=1, grid=(n_steps, nt),
        in_specs=[pl.BlockSpec((TG, D_MODEL), x_map),
                  pl.BlockSpec((1, D_MODEL), lambda s, t, kr: (0, 0)),
                  pl.BlockSpec((1, D_MODEL, SHARD_COLS), lambda s, t, kr: (0, 0, 0)),
                  pl.BlockSpec((1, 3 * HEAD), lambda s, t, kr: (0, 0))],
        out_specs=(pl.BlockSpec((SEQ, D_MODEL), lambda s, t, kr: (0, 0)),
                   pl.BlockSpec((TG, half_c), p_map), hbm, hbm),
        scratch_shapes=[pltpu.VMEM((N_SHARD, D_MODEL, SHARD_COLS), BF16),
                        pltpu.VMEM((8, D_CONV), F32),
                        pltpu.SemaphoreType.DMA((N_SEM,)), pltpu.SemaphoreType.DMA((N_SEM,)),
                        pltpu.SemaphoreType.DMA((5,))])
    return pl.pallas_call(
        body, name="gather_proj", grid_spec=grid_spec,
        out_shape=(jax.ShapeDtypeStruct((SEQ, D_MODEL), BF16),
                   jax.ShapeDtypeStruct((SEQ, N_SHARD * SHARD_COLS), F32),
                   jax.ShapeDtypeStruct((N_SHARD, D_MODEL, SHARD_COLS), BF16),
                   jax.ShapeDtypeStruct((8, D_CONV), F32)),
        compiler_params=pltpu.CompilerParams(dimension_semantics=("arbitrary", "arbitrary"),
                                             vmem_limit_bytes=VMEM_LIMIT, collective_id=COLLECTIVE_GATHER),
    )(kidx, x2d, g1, w_in, conv_w)


LAG = 6


def _mix_out(proj, lb_logits, cw, ga, gcn, g64, w_out, x2d, gf, tgt):
    half_o = WO_ROWS // 2
    nblk = SEQ // TB
    n_steps = nblk + LAG

    def body(p_ref, lbl_ref, cw_ref, ga_ref, gcn_ref, g64_ref, wo_ref, x_ref, gf_ref, t_ref,
             aux_ref, sto_ref, dx2_ref, dm_ref, gwo_ref, part_ref,
             st_ref, tail_ref, wog_v, stage, ring, acc_ref, send_sems, recv_sems):
        i = pl.program_id(0)
        x, y, c = lax.axis_index("x"), lax.axis_index("y"), lax.axis_index("c")
        k = 2 * x + y
        sibling = (x, y, 1 - c)
        chips = [(1 - x, y), (x, 1 - y), (1 - x, 1 - y)]
        kjs = [2 * cx + cy for cx, cy in chips]

        def wo_half(kk, cc):
            return wog_v.at[pl.ds(pl.multiple_of(kk * WO_ROWS + cc * half_o, half_o), half_o), :]

        def copy(sem, ref, to):
            return pltpu.make_async_remote_copy(
                src_ref=ref, dst_ref=ref, send_sem=send_sems.at[sem], recv_sem=recv_sems.at[sem],
                device_id=to, device_id_type=MESH)

        wo_direct = [copy(j, wo_half(k, c), (*chip, c)) for j, chip in enumerate(chips)]
        wo_passed = [copy(3 + j, wo_half(kj, c), sibling) for j, kj in enumerate(kjs)]

        @pl.when(i == 0)
        def _():
            barrier = pltpu.get_barrier_semaphore()
            for peer in [sibling] + [(*chip, c) for chip in chips]:
                pl.semaphore_signal(barrier, inc=1, device_id=peer, device_id_type=MESH)
            st_ref[...] = jnp.zeros_like(st_ref)
            tail_ref[...] = jnp.zeros_like(tail_ref)
            acc_ref[...] = jnp.zeros_like(acc_ref)
            part_ref[...] = jnp.zeros_like(part_ref)
            wog_v[pl.ds(pl.multiple_of(k * WO_ROWS, WO_ROWS), WO_ROWS), :] = wo_ref[0].astype(BF16)
            pl.semaphore_wait(barrier, 4)
            for cp in wo_direct:
                cp.start()

        @pl.when(i == LAG - 1)
        def _():
            for j in range(3):
                copy(j, wo_half(kjs[j], c), sibling).wait_recv()
                wo_passed[j].start()

        @pl.when(i == LAG)
        def _():
            for j in range(3):
                copy(3 + j, wo_half(kjs[j], 1 - c), sibling).wait_recv()

        lb = _lower_bound(lbl_ref[...])
        tri = _tri(True)
        causal = _causal()
        g64m = g64_ref[...]
        heads = range(N_HEADS)
        cs = [slice(hd * HEAD, (hd + 1) * HEAD) for hd in heads]
        col = lambda base, hd: slice(base + hd * HEAD, base + (hd + 1) * HEAD)

        def mix_chunk(n):
            sl = pl.ds(n * CHUNK, CHUNK)
            sg = [_sigmoid(p_ref[sl, col(512, hd)]) for hd in heads]
            f = [lb[:, cs[hd]] + (1.0 - lb[:, cs[hd]]) * sg[hd] for hd in heads]
            bc = _exact_left_many(tri, [jnp.log(f[hd]) for hd in heads])
            for hd in heads:
                aux_ref[sl, col(AUX_B, hd)] = bc[hd]
            g = [bc[hd][CHUNK - 1:CHUNK, :] for hd in heads]
            qd = [(p_ref[sl, col(0, hd)] * jnp.exp(bc[hd])).astype(BF16) for hd in heads]
            kk = [1.0 - f[hd] for hd in heads]
            ki = [(kk[hd] * jnp.exp(-bc[hd])).astype(BF16) for hd in heads]
            ke = [(kk[hd] * jnp.exp(g[hd] - bc[hd])).astype(BF16) for hd in heads]
            vb = [p_ref[sl, col(1024, hd)].astype(BF16) for hd in heads]
            st = [st_ref[hd] for hd in heads]
            st_b = [a.astype(BF16) for a in st]
            for hd in heads:
                sto_ref[n, hd] = st_b[hd]
            scm = [_dot_nt(qd[hd], ki[hd]) for hd in heads]
            inter = [_dot_nt(qd[hd], st_b[hd]) for hd in heads]
            upd = [_dot_tn(vb[hd], ke[hd]) for hd in heads]
            intra = [_dot(jnp.where(causal, scm[hd], 0.0).astype(BF16), vb[hd]) for hd in heads]
            for hd in heads:
                st_ref[hd] = st[hd] * jnp.exp(g[hd]) + upd[hd]
                o = intra[hd] + inter[hd]
                aux_ref[sl, col(AUX_O, hd)] = o
                ra = lax.rsqrt(jnp.mean(o * o, axis=-1, keepdims=True) + EPS)
                za = p_ref[sl, col(1536, hd)]
                stage[sl, cs[hd]] = (o * ra * ga_ref[:, cs[hd]] * (za * _sigmoid(za))).astype(BF16)
            yb = []
            for hd in heads:
                cu = p_ref[sl, col(3072, hd)] * p_ref[sl, col(2048, hd)]
                tail = tail_ref[:, cs[hd]]
                cv = (cw_ref[0:1, cs[hd]] * _shift_down(cu, 2, tail) + cw_ref[1:2, cs[hd]] * _shift_down(cu, 1, tail)
                      + cw_ref[2:3, cs[hd]] * cu)
                tail_ref[:, cs[hd]] = cu[CHUNK - 8:, :]
                aux_ref[sl, col(AUX_CV, hd)] = cv
                yb.append(p_ref[sl, col(2560, hd)] * cv)
            ms = _group_mean_many([y * y for y in yb], g64m)
            for hd in heads:
                rb = lax.rsqrt(ms[hd] + EPS)
                zb = p_ref[sl, col(3584, hd)]
                stage[sl, col(512, hd)] = (yb[hd] * rb * gcn_ref[:, cs[hd]] * (zb * _sigmoid(zb))).astype(BF16)

        def step(mix, project):
            if project:
                mixed_b = ring[pl.ds(pl.multiple_of((i - LAG) * TB, TB), TB), :]
                y = _dot(mixed_b, wog_v[...])
            if mix:
                mix_chunk(0)
            if project:
                x2 = x_ref[...] + y
                r2 = lax.rsqrt(jnp.mean(x2 * x2, axis=-1, keepdims=True) + EPS)
                n2 = x2 * r2
                gfv = gf_ref[...]
                err = n2 * gfv - t_ref[...]
                loss = 0.5 * jnp.sum(jnp.mean(err * err, axis=-1, keepdims=True), axis=0, keepdims=True)
                dy = err * (1.0 / D_MODEL)
                part_ref[1:2, :] += jnp.sum(dy * n2, axis=0, keepdims=True)
                part_ref[7:8, :] += jnp.broadcast_to(loss, (1, D_MODEL))
                dn = dy * gfv
                dx2 = r2 * (dn - n2 * jnp.mean(dn * n2, axis=-1, keepdims=True))
                dx2_ref[...] = dx2
                dx2_b = dx2.astype(BF16)
            if mix:
                mix_chunk(1)
            if project:
                dm_ref[...] = _dot_nt(dx2_b, wog_v[...])
            if mix:
                mix_chunk(2)
            if project:
                acc_ref[...] += _dot_tn(mixed_b, dx2_b)
            if mix:
                mix_chunk(3)
                ring[pl.ds(pl.multiple_of(i * TB, TB), TB), :] = stage[...]

        @pl.when(i < LAG)
        def _():
            step(True, False)

        @pl.when((i >= LAG) & (i < nblk))
        def _():
            step(True, True)

        @pl.when(i >= nblk)
        def _():
            step(False, True)

        @pl.when(i == n_steps - 1)
        def _():
            gwo_ref[...] = acc_ref[...].astype(BF16)
            for cp in wo_direct + wo_passed:
                cp.wait_send()

    assert NCB == 4
    row = lambda w: pl.BlockSpec((1, w), lambda i: (0, 0))
    mix_blk = lambda i: jnp.minimum(i, nblk - 1)
    out_blk = lambda i: jnp.clip(i - LAG, 0, nblk - 1)
    tok = lambda: pl.BlockSpec((TB, D_MODEL), lambda i: (out_blk(i), 0))
    return pl.pallas_call(
        body, name="mix_out", grid=(n_steps,),
        out_shape=(jax.ShapeDtypeStruct((SEQ, AUX_COLS), F32),
                   jax.ShapeDtypeStruct((N_CHUNKS, N_HEADS, HEAD, HEAD), BF16),
                   jax.ShapeDtypeStruct((SEQ, D_MODEL), F32),
                   jax.ShapeDtypeStruct((SEQ, D_MODEL), F32),
                   jax.ShapeDtypeStruct((D_MODEL, D_MODEL), BF16),
                   jax.ShapeDtypeStruct((8, D_MODEL), F32)),
        in_specs=[pl.BlockSpec((TB, 4096), lambda i: (jnp.minimum(i, nblk - 1), 0)),
                  pl.BlockSpec((2, D_HGRN), lambda i: (0, 0)),
                  pl.BlockSpec((8, D_CONV), lambda i: (0, 0)),
                  row(D_HGRN), row(D_CONV),
                  pl.BlockSpec((HEAD, HEAD), lambda i: (0, 0)),
                  pl.BlockSpec((1, WO_ROWS, D_MODEL), lambda i: (0, 0, 0)),
                  tok(), row(D_MODEL), tok()],
        out_specs=(pl.BlockSpec((TB, AUX_COLS), lambda i: (mix_blk(i), 0)),
                   pl.BlockSpec((NCB, N_HEADS, HEAD, HEAD), lambda i: (mix_blk(i), 0, 0, 0)),
                   tok(), tok(),
                   pl.BlockSpec((D_MODEL, D_MODEL), lambda i: (0, 0)),
                   pl.BlockSpec((8, D_MODEL), lambda i: (0, 0))),
        scratch_shapes=[pltpu.VMEM((N_HEADS, HEAD, HEAD), F32), pltpu.VMEM((8, D_CONV), F32),
                        pltpu.VMEM((D_MODEL, D_MODEL), BF16), pltpu.VMEM((TB, D_MODEL), BF16),
                        pltpu.VMEM((SEQ, D_MODEL), BF16), pltpu.VMEM((D_MODEL, D_MODEL), F32),
                        pltpu.SemaphoreType.DMA((6,)), pltpu.SemaphoreType.DMA((6,))],
        compiler_params=pltpu.CompilerParams(dimension_semantics=("arbitrary",), vmem_limit_bytes=VMEM_LIMIT,
                                             collective_id=COLLECTIVE_MIX_OUT),
    )(proj, lb_logits, cw, ga, gcn, g64, w_out, x2d, gf, tgt)


def _mix_bwd(proj, aux, states, dmixed, lb_logits, cw, ga, gcn, g64):
    nblk = SEQ // TB

    def body(p_ref, aux_ref, st_ref, dm_ref, lbl_ref, cw_ref, ga_ref, gcn_ref, g64_ref,
             dp_ref, part_ref, dst_ref, head_ref, dlb_ref):
        i = pl.program_id(0)

        @pl.when(i == 0)
        def _():
            dst_ref[...] = jnp.zeros_like(dst_ref)
            head_ref[...] = jnp.zeros_like(head_ref)
            part_ref[...] = jnp.zeros_like(part_ref)
            dlb_ref[...] = jnp.zeros_like(dlb_ref)

        lb = _lower_bound(lbl_ref[...])
        triu = _tri(False)
        causal = _causal()
        g64m = g64_ref[...]
        rowsum = lambda a: jnp.sum(a, axis=0, keepdims=True)
        heads = range(N_HEADS)
        cs = [slice(hd * HEAD, (hd + 1) * HEAD) for hd in heads]
        col = lambda base, hd: slice(base + hd * HEAD, base + (hd + 1) * HEAD)
        for n in reversed(range(NCB)):
            sl = pl.ds(n * CHUNK, CHUNK)
            cvv = [aux_ref[sl, col(AUX_CV, hd)] for hd in heads]
            gb = [p_ref[sl, col(2560, hd)] for hd in heads]
            yb = [gb[hd] * cvv[hd] for hd in heads]
            ms = _group_mean_many([y * y for y in yb], g64m)
            rb, nb, dnb = [], [], []
            for hd in heads:
                rb.append(lax.rsqrt(ms[hd] + EPS))
                nb.append(yb[hd] * rb[hd])
                zb = p_ref[sl, col(3584, hd)]
                sgb = _sigmoid(zb)
                dmb = dm_ref[sl, col(512, hd)]
                silu = zb * sgb
                dgate = dmb * gcn_ref[:, cs[hd]]
                part_ref[2:3, col(512, hd)] += rowsum(dmb * nb[hd] * silu)
                dp_ref[sl, col(3584, hd)] = (dgate * nb[hd] * (sgb + silu * (1.0 - sgb))).astype(BF16)
                dnb.append(dgate * silu)
            mdn = _group_mean_many([dnb[hd] * nb[hd] for hd in heads], g64m)
            for hd in heads:
                dyb = rb[hd] * (dnb[hd] - nb[hd] * mdn[hd])
                dp_ref[sl, col(2560, hd)] = (dyb * cvv[hd]).astype(BF16)
                dcv = dyb * gb[hd]
                head = head_ref[:, cs[hd]]
                dcv1 = _shift_up(dcv, 1, head)
                dcv2 = _shift_up(dcv, 2, head)
                head_ref[:, cs[hd]] = dcv[0:8, :]
                u = p_ref[sl, col(2048, hd)]
                gc = p_ref[sl, col(3072, hd)]
                cu = gc * u
                part_ref[4:5, cs[hd]] += rowsum(dcv2 * cu)
                part_ref[5:6, cs[hd]] += rowsum(dcv1 * cu)
                part_ref[6:7, cs[hd]] += rowsum(dcv * cu)
                dcu = cw_ref[2:3, cs[hd]] * dcv + cw_ref[1:2, cs[hd]] * dcv1 + cw_ref[0:1, cs[hd]] * dcv2
                dp_ref[sl, col(3072, hd)] = (dcu * u).astype(BF16)
                dp_ref[sl, col(2048, hd)] = (dcu * gc).astype(BF16)
            do_b = []
            for hd in heads:
                ov = aux_ref[sl, col(AUX_O, hd)]
                ra = lax.rsqrt(jnp.mean(ov * ov, axis=-1, keepdims=True) + EPS)
                na = ov * ra
                za = p_ref[sl, col(1536, hd)]
                sga = _sigmoid(za)
                dma = dm_ref[sl, cs[hd]]
                silu = za * sga
                dgate = dma * ga_ref[:, cs[hd]]
                part_ref[2:3, cs[hd]] += rowsum(dma * na * silu)
                dp_ref[sl, col(1536, hd)] = (dgate * na * (sga + silu * (1.0 - sga))).astype(BF16)
                dna = dgate * silu
                do_b.append((ra * (dna - na * jnp.mean(dna * na, axis=-1, keepdims=True))).astype(BF16))
            s = [_sigmoid(p_ref[sl, col(512, hd)]) for hd in heads]
            f = [lb[:, cs[hd]] + (1.0 - lb[:, cs[hd]]) * s[hd] for hd in heads]
            bc = [aux_ref[sl, col(AUX_B, hd)] for hd in heads]
            g = [bc[hd][CHUNK - 1:CHUNK, :] for hd in heads]
            eb = [jnp.exp(bc[hd]) for hd in heads]
            enb = [jnp.exp(-bc[hd]) for hd in heads]
            eg = [jnp.exp(g[hd] - bc[hd]) for hd in heads]
            dec = [jnp.exp(g[hd]) for hd in heads]
            qd = [p_ref[sl, cs[hd]] * eb[hd] for hd in heads]
            kk = [1.0 - f[hd] for hd in heads]
            ki = [kk[hd] * enb[hd] for hd in heads]
            ke = [kk[hd] * eg[hd] for hd in heads]
            qd_b = [a.astype(BF16) for a in qd]
            ki_b = [a.astype(BF16) for a in ki]
            ke_b = [a.astype(BF16) for a in ke]
            vb = [p_ref[sl, col(1024, hd)].astype(BF16) for hd in heads]
            st_b = [st_ref[n, hd] for hd in heads]
            dst = [dst_ref[hd] for hd in heads]
            dst_b = [a.astype(BF16) for a in dst]
            scm = [_dot_nt(qd_b[hd], ki_b[hd]) for hd in heads]
            amm = [_dot_nt(do_b[hd], vb[hd]) for hd in heads]
            dqd2 = [_dot(do_b[hd], st_b[hd]) for hd in heads]
            dke = [_dot(vb[hd], dst_b[hd]) for hd in heads]
            dv2 = [_dot_nt(ke_b[hd], dst_b[hd]) for hd in heads]
            dsu = [_dot_tn(do_b[hd], qd_b[hd]) for hd in heads]
            sc = [jnp.where(causal, scm[hd], 0.0).astype(BF16) for hd in heads]
            am = [jnp.where(causal, amm[hd], 0.0).astype(BF16) for hd in heads]
            dqd1 = [_dot(am[hd], ki_b[hd]) for hd in heads]
            dki = [_dot_tn(am[hd], qd_b[hd]) for hd in heads]
            dv1 = [_dot_tn(sc[hd], do_b[hd]) for hd in heads]
            db, dgv, dkk = [], [], []
            for hd in heads:
                dqd = dqd1[hd] + dqd2[hd]
                ddec = rowsum(dst[hd] * st_b[hd].astype(F32))
                dst_ref[hd] = dst[hd] * dec[hd] + dsu[hd]
                dp_ref[sl, cs[hd]] = (dqd * eb[hd]).astype(BF16)
                dp_ref[sl, col(1024, hd)] = (dv1[hd] + dv2[hd]).astype(BF16)
                dke_eg = dke[hd] * eg[hd]
                dkk.append(dki[hd] * enb[hd] + dke_eg)
                db.append(dqd * qd[hd] - kk[hd] * dkk[hd])
                dgv.append(rowsum(kk[hd] * dke_eg) + ddec * dec[hd])
            rc = _exact_left_many(triu, db, 2)
            for hd in heads:
                df = (rc[hd] + dgv[hd]) / f[hd] - dkk[hd]
                one_s = 1.0 - s[hd]
                dlb_ref[:, cs[hd]] += rowsum(df * one_s)
                dp_ref[sl, col(512, hd)] = (df * (1.0 - lb[:, cs[hd]]) * s[hd] * one_s).astype(BF16)

        @pl.when(i == nblk - 1)
        def _():
            row = dlb_ref[...] * lb * (1.0 - lb)
            part_ref[3:4, 0:D_HGRN] = row
            part_ref[3:4, D_HGRN:] = -row

    rev = lambda w: pl.BlockSpec((TB, w), lambda i: (nblk - 1 - i, 0))
    row = lambda w: pl.BlockSpec((1, w), lambda i: (0, 0))
    return pl.pallas_call(
        body, name="mix_bwd", grid=(nblk,),
        out_shape=(jax.ShapeDtypeStruct((SEQ, 4096), BF16),
                   jax.ShapeDtypeStruct((8, D_MODEL), F32)),
        in_specs=[rev(4096), rev(AUX_COLS),
                  pl.BlockSpec((NCB, N_HEADS, HEAD, HEAD), lambda i: (nblk - 1 - i, 0, 0, 0)),
                  rev(D_MODEL),
                  pl.BlockSpec((2, D_HGRN), lambda i: (0, 0)),
                  pl.BlockSpec((8, D_CONV), lambda i: (0, 0)),
                  row(D_HGRN), row(D_CONV),
                  pl.BlockSpec((HEAD, HEAD), lambda i: (0, 0))],
        out_specs=(rev(4096), pl.BlockSpec((8, D_MODEL), lambda i: (0, 0))),
        scratch_shapes=[pltpu.VMEM((N_HEADS, HEAD, HEAD), F32), pltpu.VMEM((8, D_CONV), F32),
                        pltpu.VMEM((1, D_HGRN), F32)],
        compiler_params=pltpu.CompilerParams(dimension_semantics=("arbitrary",), vmem_limit_bytes=VMEM_LIMIT),
    )(proj, aux, states, dmixed, lb_logits, cw, ga, gcn, g64)


TT = 1024
TX = 512
(SEM_D2D, SEM_D2D_O, SEM_ICI, SEM_ICI_O, SEM_FIN, SEM_FIN_O, SEM_SMALL, SEM_VIA, SEM_NORM, N_SEM_TAIL) = (
    0, 4, 5, 8, 11, 12, 12, 20, 22, 30)


def _bwd_tail(kidx, h, dproj, wg, gwo, x2d, dx2, g1, small_a, small_b):
    hw = D_MODEL // 2
    ho = WO_ROWS // 2
    nt = SEQ // TT
    norm_step = 2 * N_SHARD
    n_steps = norm_step + SEQ // TX // nt

    def body(k_ref, h_ref, dp_ref, w_ref, gwo_ref, x_ref, dx2_ref, g_ref, sm_ref, smb_ref,
             gx_ref, gw_out, gwo_out, osm_ref,
             acc, dh, sendbuf, keep, sibrcv, rcv, merge, sib_o, p_o, rcv_o, res_o, sm_buf, dng_buf, dng,
             send_sems, recv_sems, out_sems):
        s, t = pl.program_id(0), pl.program_id(1)
        x, y, c = lax.axis_index("x"), lax.axis_index("y"), lax.axis_index("c")
        k = 2 * x + y
        me = 4 * x + 2 * y + c
        sibling = (x, y, 1 - c)
        chips = [(1 - x, 1 - y), (1 - x, y), (x, 1 - y)]
        kjs = [2 * cx + cy for cx, cy in chips]
        mine = pl.ds(pl.multiple_of(c * hw, hw), hw)
        other = pl.ds(pl.multiple_of((1 - c) * hw, hw), hw)
        mine_o = pl.ds(pl.multiple_of(c * ho, ho), ho)
        other_o = pl.ds(pl.multiple_of((1 - c) * ho, ho), ho)

        def copy(sem, src, dst, to):
            return pltpu.make_async_remote_copy(
                src_ref=src, dst_ref=dst, send_sem=send_sems.at[sem], recv_sem=recv_sems.at[sem],
                device_id=to, device_id_type=MESH)

        def at_step(sv, tv):
            return pl.when((s == sv) & (t == tv))

        def at_norm_block(b):
            return at_step(norm_step + b // nt, b % nt)

        d2d = [copy(SEM_D2D + sv, sendbuf.at[sv], sibrcv.at[sv], sibling) for sv in range(N_SHARD)]
        d2d_o = copy(SEM_D2D_O, gwo_ref.at[:, other_o, :], sib_o, sibling)
        ici = {sv: copy(SEM_ICI + sv, keep.at[sv], rcv.at[sv - 1], (*chips[sv], c)) for sv in (1, 2)}
        qh = hw // 2
        via = [copy(SEM_VIA, keep.at[0, 0:qh, :], merge.at[1], (*chips[1], c)),
               copy(SEM_VIA + 1, keep.at[0, qh:hw, :], merge.at[0], (*chips[2], c))]
        merged_rows = [slice(qh, hw), slice(0, qh)]
        ici_o = [copy(SEM_ICI_O + sv, p_o.at[kjs[sv]], rcv_o.at[sv], (*chips[sv], c)) for sv in range(3)]
        fin = copy(SEM_FIN, acc.at[mine, :], gw_out.at[mine, :], sibling)
        fin_o = copy(SEM_FIN_O, res_o.at[mine_o, :], res_o.at[mine_o, :], sibling)
        peers = [(x ^ (m >> 2), y ^ ((m >> 1) & 1), c ^ (m & 1)) for m in range(1, N_DEV)]
        smalls = [copy(SEM_SMALL + 1 + j, sm_buf.at[me], sm_buf.at[me], to) for j, to in enumerate(peers)]
        dngs = [copy(SEM_NORM + 1 + j, dng_buf.at[me], dng_buf.at[me], to) for j, to in enumerate(peers)]
        store_w = pltpu.make_async_copy(acc.at[mine, :], gw_out.at[mine, :], out_sems.at[0])
        store_o = pltpu.make_async_copy(res_o, gwo_out, out_sems.at[1])

        @at_step(0, 0)
        def _():
            barrier = pltpu.get_barrier_semaphore()
            for to in peers:
                pl.semaphore_signal(barrier, inc=1, device_id=to, device_id_type=MESH)
            sm_buf[me] = sm_ref[...] + smb_ref[...]

        rows = pl.ds(pl.multiple_of(t * TT, TT), TT)

        @pl.when((s < N_SHARD) & (t == 0))
        def _():
            acc[...] = _dot_tn(h_ref[...], dp_ref[...])

        @pl.when((s < N_SHARD) & (t > 0))
        def _():
            acc[...] += _dot_tn(h_ref[...], dp_ref[...])

        @at_step(0, 0)
        def _():
            pl.semaphore_wait(pltpu.get_barrier_semaphore(), N_DEV - 1)
            d2d_o.start()
            for cp in smalls:
                cp.start()

        @at_step(0, 1)
        def _():
            d2d_o.wait_recv()
            for j in range(N_SHARD):
                p_o[j] = (gwo_ref[j, mine_o, :].astype(F32) + sib_o[j].astype(F32)).astype(BF16)
            res_o[mine_o, :] = gwo_ref[k, mine_o, :].astype(F32) + sib_o[k].astype(F32)
            for cp in ici_o:
                cp.start()

        for sv in range(N_SHARD):
            @at_step(sv, nt - 1)
            def _(sv=sv):
                sendbuf[sv] = acc[other, :].astype(BF16)
                if sv < 3:
                    keep[sv] = acc[mine, :].astype(BF16)
                d2d[sv].start()

        @at_step(1, 0)
        def _():
            d2d[0].wait_recv()
            keep[0] = (keep[0].astype(F32) + sibrcv[0].astype(F32)).astype(BF16)
            for cp in via:
                cp.start()

        for sv in (1, 2):
            @at_step(sv + 2, 0)
            def _(sv=sv):
                d2d[sv].wait_recv()
                keep[sv] = (keep[sv].astype(F32) + sibrcv[sv].astype(F32)).astype(BF16)
                via[2 - sv].wait_recv()
                rows_m = merged_rows[sv - 1]
                keep[sv, rows_m, :] = (keep[sv, rows_m, :].astype(F32) + merge[sv - 1].astype(F32)).astype(BF16)
                ici[sv].start()

        @pl.when(s == N_SHARD)
        def _():
            dh[rows, :] = _dot_nt(dp_ref[...], w_ref[0])

        @pl.when((s > N_SHARD) & (s < norm_step))
        def _():
            dh[rows, :] += _dot_nt(dp_ref[...], w_ref[0])

        @at_norm_block(0)
        def _():
            d2d[3].wait_recv()
            acc[mine, :] += sibrcv[3].astype(F32)

        @at_norm_block(1)
        def _():
            tot = res_o[mine_o, :]
            for sv in range(3):
                ici_o[sv].wait_recv()
                tot = tot + rcv_o[sv].astype(F32)
            res_o[mine_o, :] = tot
            fin_o.start()

        @at_norm_block(2)
        def _():
            ici[1].wait_recv()
            acc[mine, :] += rcv[0].astype(F32)

        @at_norm_block(SEQ // TX - 2)
        def _():
            ici[2].wait_recv()
            acc[mine, :] += rcv[1].astype(F32)
            fin.start()
            store_w.start()
            fin_o.wait_recv()
            store_o.start()

        @at_norm_block(0)
        def _():
            dng[...] = jnp.zeros_like(dng)

        @pl.when(s >= norm_step)
        def _():
            blk = (s - norm_step) * nt + t
            dhv = dh[pl.ds(pl.multiple_of(blk * TX, TX), TX), :]
            xv = x_ref[...]
            r = lax.rsqrt(jnp.mean(xv * xv, axis=-1, keepdims=True) + EPS)
            xn = xv * r
            dng[...] += jnp.sum(dhv * xn, axis=0, keepdims=True)
            dxn = dhv * g_ref[...]
            gx_ref[...] = dx2_ref[...] + r * (dxn - xn * jnp.mean(dxn * xn, axis=-1, keepdims=True))

        @at_step(n_steps - 1, nt - 1)
        def _():
            dng_buf[me] = dng[...]
            for cp in dngs:
                cp.start()
            for m in range(1, N_DEV):
                copy(SEM_SMALL + m, sm_buf.at[0], sm_buf.at[0], sibling).wait_recv()
            tot = sm_buf[0]
            for d in range(1, N_DEV):
                tot = tot + sm_buf[d]
            osm_ref[...] = tot
            for m in range(1, N_DEV):
                copy(SEM_NORM + m, dng_buf.at[0], dng_buf.at[0], sibling).wait_recv()
            tot = dng_buf[0]
            for d in range(1, N_DEV):
                tot = tot + dng_buf[d]
            osm_ref[0:1, :] = tot
            fin.wait_recv()
            for cp in d2d + [d2d_o] + via + list(ici.values()) + ici_o + [fin, fin_o] + smalls + dngs:
                cp.wait_send()
            store_o.wait()
            store_w.wait()

    def shard_of(s, kr):
        order = jnp.where(s < N_SHARD, s, jnp.where(s < norm_step, s - N_SHARD, 3))
        return kr[0] ^ (3 - order)

    def h_map(s, t, kr):
        return (jnp.where(s < N_SHARD, t, nt - 1), 0)

    def dp_map(s, t, kr):
        return (jnp.where(s < norm_step, t, nt - 1), shard_of(s, kr))

    def w_map(s, t, kr):
        return (shard_of(jnp.maximum(s, N_SHARD), kr), 0, 0)

    def blk_map(s, t, kr):
        return (jnp.where(s < norm_step, 0, (s - norm_step) * nt + t), 0)

    hbm = pl.BlockSpec(memory_space=pl.ANY)
    grid_spec = pltpu.PrefetchScalarGridSpec(
        num_scalar_prefetch=1, grid=(n_steps, nt),
        in_specs=[pl.BlockSpec((TT, D_MODEL), h_map),
                  pl.BlockSpec((TT, SHARD_COLS), dp_map),
                  pl.BlockSpec((1, D_MODEL, SHARD_COLS), w_map),
                  pl.BlockSpec((N_SHARD, WO_ROWS, D_MODEL), lambda s, t, kr: (0, 0, 0),
                               pipeline_mode=pl.Buffered(1)),
                  pl.BlockSpec((TX, D_MODEL), blk_map),
                  pl.BlockSpec((TX, D_MODEL), blk_map),
                  pl.BlockSpec((1, D_MODEL), lambda s, t, kr: (0, 0)),
                  pl.BlockSpec((8, D_MODEL), lambda s, t, kr: (0, 0)),
                  pl.BlockSpec((8, D_MODEL), lambda s, t, kr: (0, 0))],
        out_specs=(pl.BlockSpec((TX, D_MODEL), blk_map), hbm, hbm,
                   pl.BlockSpec((8, D_MODEL), lambda s, t, kr: (0, 0))),
        scratch_shapes=[pltpu.VMEM((D_MODEL, SHARD_COLS), F32), pltpu.VMEM((SEQ, D_MODEL), F32),
                        pltpu.VMEM((N_SHARD, hw, SHARD_COLS), BF16), pltpu.VMEM((3, hw, SHARD_COLS), BF16),
                        pltpu.VMEM((N_SHARD, hw, SHARD_COLS), BF16), pltpu.VMEM((2, hw, SHARD_COLS), BF16),
                        pltpu.VMEM((2, hw // 2, SHARD_COLS), BF16),
                        pltpu.VMEM((N_SHARD, ho, D_MODEL), BF16), pltpu.VMEM((N_SHARD, ho, D_MODEL), BF16),
                        pltpu.VMEM((3, ho, D_MODEL), BF16), pltpu.VMEM((WO_ROWS, D_MODEL), F32),
                        pltpu.VMEM((N_DEV, 8, D_MODEL), F32), pltpu.VMEM((N_DEV, 1, D_MODEL), F32),
                        pltpu.VMEM((1, D_MODEL), F32),
                        pltpu.SemaphoreType.DMA((N_SEM_TAIL,)), pltpu.SemaphoreType.DMA((N_SEM_TAIL,)),
                        pltpu.SemaphoreType.DMA((2,))])
    return pl.pallas_call(
        body, name="bwd_tail", grid_spec=grid_spec,
        out_shape=(jax.ShapeDtypeStruct((SEQ, D_MODEL), F32),
                   jax.ShapeDtypeStruct((D_MODEL, SHARD_COLS), F32),
                   jax.ShapeDtypeStruct((WO_ROWS, D_MODEL), F32),
                   jax.ShapeDtypeStruct((8, D_MODEL), F32)),
        compiler_params=pltpu.CompilerParams(dimension_semantics=("arbitrary", "arbitrary"),
                                             vmem_limit_bytes=61 * 1024 * 1024, collective_id=COLLECTIVE_TAIL),
    )(kidx, h, dproj, wg, gwo, x2d, dx2, g1, small_a, small_b)


def _adam_update(w, g, m, v):
    nm = ADAM_B1 * m + (1.0 - ADAM_B1) * g
    nv = ADAM_B2 * v + (1.0 - ADAM_B2) * (g * g)
    m_hat = nm / (1.0 - ADAM_B1 ** ADAM_STEP)
    v_hat = nv / (1.0 - ADAM_B2 ** ADAM_STEP)
    return -ADAM_LR * (m_hat / (jnp.sqrt(v_hat) + ADAM_EPS) + ADAM_WD * w), nm, nv


def _adamw_all(tot, g_w_in, g_w_out, big, small, grad_x):
    n = len(small)
    rows = WO_ROWS
    steps = D_MODEL // rows

    def body(tot_ref, *refs):
        gx_ref, gx_out = refs[2 + 3 * (2 + n)], refs[-1]
        gx_out[...] = gx_ref[...]
        ins, outs = refs[:2 + 3 * (2 + n)], refs[3 + 3 * (2 + n):-1]
        g_refs, wmv = ins[:2], ins[2:]
        loss_ref, quads = outs[0], outs[1:]

        def update(j, g):
            w_ref, m_ref, v_ref = wmv[3 * j:3 * j + 3]
            g_ref, d_ref, nm_ref, nv_ref = quads[4 * j:4 * j + 4]
            g_ref[...] = g
            d_ref[...], nm_ref[...], nv_ref[...] = _adam_update(w_ref[...], g, m_ref[...], v_ref[...])

        update(0, g_refs[0][...])

        @pl.when(pl.program_id(0) == 0)
        def _():
            update(1, g_refs[1][...])
            k = 2 * lax.axis_index("x") + lax.axis_index("y")
            mine = pl.ds(pl.multiple_of(k * HEAD, HEAD), HEAD)
            loss_ref[...] = tot_ref[7:8, 0:1]
            grads = [tot_ref[0:1, :], tot_ref[1:2, :], tot_ref[2:3, 0:D_HGRN], tot_ref[2:3, D_HGRN:],
                     jnp.concatenate([tot_ref[3:4, 0:D_HGRN], tot_ref[3:4, D_HGRN:]], axis=0),
                     jnp.concatenate([tot_ref[4 + tap:5 + tap, mine] for tap in range(3)], axis=1)]
            for j, g in enumerate(grads):
                update(2 + j, g)

    whole = lambda a: pl.BlockSpec(a.shape, lambda i: (0, 0))
    blk = pl.BlockSpec((rows, SHARD_COLS), lambda i: (i, 0))
    arrays = [a for triple in big + small for a in triple]
    in_specs = ([whole(tot), blk, whole(g_w_out)] + [blk] * 3 + [whole(a) for a in arrays[3:]])
    shapes = [big[0][0], big[1][0]] + [w for w, _, _ in small]
    out_shape = (jax.ShapeDtypeStruct((1, 1), F32),) + tuple(
        jax.ShapeDtypeStruct(w.shape, F32) for w in shapes for _ in range(4))
    out_specs = (pl.BlockSpec((1, 1), lambda i: (0, 0)),) + (blk,) * 4 + tuple(
        whole(w) for w in shapes[1:] for _ in range(4))
    gx_blk = pl.BlockSpec((SEQ // steps, D_MODEL), lambda i: (i, 0))
    outs = pl.pallas_call(
        body, name="adamw_all", grid=(steps,),
        out_shape=out_shape + (jax.ShapeDtypeStruct(grad_x.shape, F32),),
        in_specs=in_specs + [gx_blk], out_specs=out_specs + (gx_blk,),
        compiler_params=pltpu.CompilerParams(dimension_semantics=("arbitrary",), vmem_limit_bytes=VMEM_LIMIT),
    )(tot, g_w_in, g_w_out, *arrays, grad_x)
    return [outs[0]] + [outs[1 + 4 * j:5 + 4 * j] for j in range(2 + n)] + [outs[-1]]


def _local_step(x2d, tgt, proj, lb_logits, cw, ga, gcn, w_out, gf):
    g64 = _group_matrix(HEAD, CONV_GROUP)
    aux, states, dx2, dmixed, gwo, part_out = _mix_out(proj, lb_logits, cw, ga, gcn, g64, w_out, x2d, gf, tgt)
    dproj, part_mix = _mix_bwd(proj, aux, states, dmixed, lb_logits, cw, ga, gcn, g64)
    return dproj, dx2, gwo.reshape(N_SHARD, WO_ROWS, D_MODEL), part_out, part_mix


def kernel(x, norm_gain, w_in, lb_logits, conv_w, hgrn_norm_gain, conv_norm_gain, w_out, final_norm_gain, loss_target, m_norm_gain, m_w_in, m_lb_logits, m_conv_w, m_hgrn_norm_gain, m_conv_norm_gain, m_w_out, m_final_norm_gain, v_norm_gain, v_w_in, v_lb_logits, v_conv_w, v_hgrn_norm_gain, v_conv_norm_gain, v_w_out, v_final_norm_gain):
    k = 2 * lax.axis_index("x") + lax.axis_index("y")
    kidx = jnp.reshape(k, (1,)).astype(jnp.int32)
    row = lambda a: a.reshape(1, D_MODEL)
    taps = lambda a: a.reshape(1, 3 * HEAD)
    h, proj, wg, cw = _gather_proj(kidx, x[0], norm_gain, w_in, taps(conv_w))
    dproj, dx2, gwo, part_out, part_mix = _local_step(
        x[0], loss_target[0], proj, lb_logits, cw, hgrn_norm_gain, conv_norm_gain, w_out, row(final_norm_gain))
    rgrad_x, rg_w_in, rg_w_out, tot = _bwd_tail(kidx, h, dproj, wg, gwo, x[0], dx2, norm_gain, part_out, part_mix)

    (loss, (g_w_in, d_w_in, nm_w_in, nv_w_in), (g_w_out, d_w_out, nm_w_out, nv_w_out),
     (g_norm_gain, d_ng, nm_ng, nv_ng), (g_final, d_fg, nm_fg, nv_fg), (g_hgrn, d_hg, nm_hg, nv_hg),
     (g_convn, d_cg, nm_cg, nv_cg), (g_lb, d_lb, nm_lb, nv_lb), (g_conv_w, d_cw, nm_cw, nv_cw),
     grad_x) = _adamw_all(
        tot, rg_w_in, rg_w_out,
        [(w_in[0], m_w_in[0], v_w_in[0]), (w_out[0], m_w_out[0], v_w_out[0])],
        [(norm_gain, m_norm_gain, v_norm_gain),
         (row(final_norm_gain), row(m_final_norm_gain), row(v_final_norm_gain)),
         (hgrn_norm_gain, m_hgrn_norm_gain, v_hgrn_norm_gain),
         (conv_norm_gain, m_conv_norm_gain, v_conv_norm_gain),
         (lb_logits, m_lb_logits, v_lb_logits),
         (taps(conv_w), taps(m_conv_w), taps(v_conv_w))],
        rgrad_x)
    flat = lambda a: a.reshape(D_MODEL)
    untap = lambda a: a.reshape(1, 3, HEAD)
    return (loss.reshape(()), grad_x[None],
            g_norm_gain, g_w_in[None], g_lb, untap(g_conv_w), g_hgrn, g_convn, g_w_out[None], flat(g_final),
            d_ng, d_w_in[None], d_lb, untap(d_cw), d_hg, d_cg, d_w_out[None], flat(d_fg),
            nm_ng, nm_w_in[None], nm_lb, untap(nm_cw), nm_hg, nm_cg, nm_w_out[None], flat(nm_fg),
            nv_ng, nv_w_in[None], nv_lb, untap(nv_cw), nv_hg, nv_cg, nv_w_out[None], flat(nv_fg))
```

```python
import jax
import jax.numpy as jnp
import numpy as np
from jax import lax
from jax.experimental import pallas as pl
from jax.experimental.pallas import tpu as pltpu

F32 = jnp.float32
BF16 = jnp.bfloat16
MESH = pl.DeviceIdType.MESH

SEQ = 2048
D_MODEL = 1024
D_HGRN = 512
D_CONV = 512
HEAD = 128
N_HEADS = 4
CHUNK = 64
CONV_GROUP = 64
N_SHARD = 4
SHARD_COLS = 1024
WO_ROWS = 256
EPS = 1e-6
TB = 256
NCB = TB // CHUNK
N_CHUNKS = SEQ // CHUNK
N_DEV = 8
COLLECTIVE_GATHER, COLLECTIVE_MIX_OUT, COLLECTIVE_TAIL = 1, 0, 2
AUX_O, AUX_CV, AUX_B, AUX_COLS = 0, 512, 1024, 1536

ADAM_LR = 0.001
ADAM_B1 = 0.9
ADAM_B2 = 0.999
ADAM_EPS = 1e-08
ADAM_WD = 0.01
ADAM_STEP = 10

VMEM_LIMIT = 56 * 1024 * 1024


def _dot(a, b):
    return jnp.dot(a, b, preferred_element_type=F32)


def _dot_nt(a, b):
    return lax.dot_general(a, b, (((1,), (1,)), ((), ())), preferred_element_type=F32)


def _dot_tn(a, b):
    return lax.dot_general(a, b, (((0,), (0,)), ((), ())), preferred_element_type=F32)


def _split_bf16(x, n):
    parts = []
    r = x
    for _ in range(n):
        p = r.astype(BF16)
        parts.append(p)
        r = r - p.astype(F32)
    return parts


def _exact_left(m, x, n=3):
    acc = None
    for p in _split_bf16(x, n):
        t = _dot(m, p)
        acc = t if acc is None else acc + t
    return acc


def _exact_left_many(m, xs, n=3):
    parts = [_split_bf16(x, n) for x in xs]
    accs = [None] * len(xs)
    for i in range(n):
        for j in range(len(xs)):
            t = _dot(m, parts[j][i])
            accs[j] = t if accs[j] is None else accs[j] + t
    return accs


def _group_mean_many(xs, gmat, n=2):
    parts = [_split_bf16(x, n) for x in xs]
    accs = [None] * len(xs)
    for i in range(n):
        for j in range(len(xs)):
            t = _dot(parts[j][i], gmat)
            accs[j] = t if accs[j] is None else accs[j] + t
    return accs


def _group_mean(x, gmat, n=2):
    w = gmat.shape[0]
    outs = []
    for c0 in range(0, x.shape[1], w):
        acc = None
        for p in _split_bf16(x[:, c0:c0 + w], n):
            t = _dot(p, gmat)
            acc = t if acc is None else acc + t
        outs.append(acc)
    return jnp.concatenate(outs, axis=1)


def _sigmoid(x):
    return 1.0 / (1.0 + jnp.exp(-x))


def _lower_bound(lbl):
    l0 = lbl[0:1, :]
    l1 = lbl[1:2, :]
    m = jnp.maximum(l0, l1)
    e0 = jnp.exp(l0 - m)
    e1 = jnp.exp(l1 - m)
    return e0 / (e0 + e1)


def _tri(lower):
    r = lax.broadcasted_iota(jnp.int32, (CHUNK, CHUNK), 0)
    c = lax.broadcasted_iota(jnp.int32, (CHUNK, CHUNK), 1)
    return jnp.where((c <= r) if lower else (c >= r), 1.0, 0.0).astype(BF16)


def _causal():
    r = lax.broadcasted_iota(jnp.int32, (CHUNK, CHUNK), 0)
    c = lax.broadcasted_iota(jnp.int32, (CHUNK, CHUNK), 1)
    return c <= r


def _shift_down(x, sh, prev_tail):
    r = pltpu.roll(x, sh, 0)
    pt = pltpu.roll(prev_tail, sh, 0)
    rows = lax.broadcasted_iota(jnp.int32, prev_tail.shape, 0)
    top = jnp.where(rows < sh, pt, r[0:8])
    return jnp.concatenate([top, r[8:]], axis=0)


def _shift_up(x, sh, next_head):
    n = x.shape[0]
    r = pltpu.roll(x, n - sh, 0)
    nh = pltpu.roll(next_head, 8 - sh, 0)
    rows = lax.broadcasted_iota(jnp.int32, next_head.shape, 0)
    bot = jnp.where(rows >= 8 - sh, nh, r[n - 8:])
    return jnp.concatenate([r[:n - 8], bot], axis=0)


def _group_matrix(width, group):
    r = np.arange(width)[:, None] // group
    c = np.arange(width)[None, :] // group
    return jnp.asarray(np.where(r == c, 1.0 / group, 0.0), dtype=BF16)


TG = 1024
SEM_W, SEM_CW, SEM_W_FWD, N_SEM = 0, 4, 7, 11


def _gather_proj(kidx, x2d, g1, w_in, conv_w):
    half_w = D_MODEL // 2
    half_c = SHARD_COLS // 2
    nt = SEQ // TG
    n_steps = 2 * N_SHARD

    def body(k_ref, x_ref, g_ref, w_ref, cw_ref, h_ref, p_ref, wg_out, cwg_out,
             wg_v, cwg_v, send_sems, recv_sems, out_sems):
        s, t = pl.program_id(0), pl.program_id(1)
        x, y, c = lax.axis_index("x"), lax.axis_index("y"), lax.axis_index("c")
        k = 2 * x + y
        sibling = (x, y, 1 - c)
        chips = [(1 - x, y), (x, 1 - y), (1 - x, 1 - y)]
        kjs = [2 * cx + cy for cx, cy in chips]
        diag = (*chips[2], c)

        def w_half(kk, cc):
            return wg_v.at[kk, pl.ds(cc * half_w, half_w), :]

        def w_quarter(kk, cc, piece):
            return wg_v.at[kk, pl.ds(cc * half_w, half_w), piece * half_c:(piece + 1) * half_c]

        def cw_of(kk):
            return cwg_v.at[:, pl.ds(pl.multiple_of(kk * HEAD, HEAD), HEAD)]

        def copy(sem, ref, to):
            return pltpu.make_async_remote_copy(
                src_ref=ref, dst_ref=ref, send_sem=send_sems.at[sem], recv_sem=recv_sems.at[sem],
                device_id=to, device_id_type=MESH)

        def at_step(sv, tv):
            return pl.when((s == sv) & (t == tv))

        w_direct = ([copy(SEM_W + j, w_half(k, c), (*chips[j], c)) for j in range(2)]
                    + [copy(SEM_W + 2 + p, w_quarter(k, c, p), diag) for p in range(2)])
        cw_direct = [copy(SEM_CW + j, cw_of(k), (*chip, c)) for j, chip in enumerate(chips)]
        w_passed = ([copy(SEM_W_FWD + j, w_half(kjs[j], c), sibling) for j in range(2)]
                    + [copy(SEM_W_FWD + 2 + p, w_quarter(kjs[2], c, p), sibling) for p in range(2)])
        stores = ([pltpu.make_async_copy(wg_v.at[kk], wg_out.at[kk], out_sems.at[i])
                   for i, kk in enumerate([k] + kjs)]
                  + [pltpu.make_async_copy(cwg_v, cwg_out, out_sems.at[4])])

        @at_step(0, 0)
        def _():
            barrier = pltpu.get_barrier_semaphore()
            for peer in [sibling] + [(*chip, c) for chip in chips]:
                pl.semaphore_signal(barrier, inc=1, device_id=peer, device_id_type=MESH)
            wg_v[k] = w_ref[0].astype(BF16)
            mine = pl.ds(pl.multiple_of(k * HEAD, HEAD), HEAD)
            cwg_v[:, mine] = jnp.zeros((8, HEAD), F32)
            for tap in range(3):
                cwg_v[tap:tap + 1, mine] = cw_ref[:, tap * HEAD:(tap + 1) * HEAD]
            pl.semaphore_wait(barrier, 4)
            for cp in w_direct + cw_direct:
                cp.start()
            stores[0].start()

        @at_step(2, 0)
        def _():
            for j in range(2):
                copy(SEM_W + j, w_half(kjs[j], c), sibling).wait_recv()
                w_passed[j].start()
            copy(SEM_W_FWD, w_half(kjs[0], 1 - c), sibling).wait_recv()
            stores[1].start()

        @at_step(4, 0)
        def _():
            copy(SEM_W_FWD + 1, w_half(kjs[1], 1 - c), sibling).wait_recv()
            stores[2].start()

        for p in range(2):
            @at_step(6 + p, 0)
            def _(p=p):
                copy(SEM_W + 2 + p, w_quarter(kjs[2], c, p), sibling).wait_recv()
                w_passed[2 + p].start()
                copy(SEM_W_FWD + 2 + p, w_quarter(kjs[2], 1 - c, p), sibling).wait_recv()
                if p == 1:
                    stores[3].start()

        rows = pl.ds(pl.multiple_of(t * TG, TG), TG)

        @pl.when(s == 0)
        def _():
            xv = x_ref[...]
            r = lax.rsqrt(jnp.mean(xv * xv, axis=-1, keepdims=True) + EPS)
            h_ref[rows, :] = (xv * r * g_ref[...]).astype(BF16)

        sh = s >> 1
        js = k ^ (((sh & 1) << 1) | (sh >> 1))
        for piece in range(2):
            @pl.when((s & 1) == piece)
            def _(piece=piece):
                p_ref[...] = _dot(h_ref[rows, :], wg_v[js, :, piece * half_c:(piece + 1) * half_c])

        @at_step(n_steps - 1, nt - 1)
        def _():
            for j in range(3):
                copy(SEM_CW + j, cw_of(kjs[j]), sibling).wait_recv()
            stores[4].start()
            for cp in w_direct + cw_direct + w_passed:
                cp.wait_send()
            for st in stores:
                st.wait()

    def x_map(s, t, kr):
        return (jnp.where(s == 0, t, nt - 1), 0)

    def p_map(s, t, kr):
        sh = s >> 1
        return (t, 2 * (kr[0] ^ (((sh & 1) << 1) | (sh >> 1))) + (s & 1))

    hbm = pl.BlockSpec(memory_space=pl.ANY)
    grid_spec = pltpu.PrefetchScalarGridSpec(
        num_scalar_prefetch=1, grid=(n_steps, nt),
        in_specs=[pl.BlockSpec((TG, D_MODEL), x_map),
                  pl.BlockSpec((1, D_MODEL), lambda s, t, kr: (0, 0)),
                  pl.BlockSpec((1, D_MODEL, SHARD_COLS), lambda s, t, kr: (0, 0, 0)),
                  pl.BlockSpec((1, 3 * HEAD), lambda s, t, kr: (0, 0))],
        out_specs=(pl.BlockSpec((SEQ, D_MODEL), lambda s, t, kr: (0, 0)),
                   pl.BlockSpec((TG, half_c), p_map), hbm, hbm),
        scratch_shapes=[pltpu.VMEM((N_SHARD, D_MODEL, SHARD_COLS), BF16),
                        pltpu.VMEM((8, D_CONV), F32),
                        pltpu.SemaphoreType.DMA((N_SEM,)), pltpu.SemaphoreType.DMA((N_SEM,)),
                        pltpu.SemaphoreType.DMA((5,))])
    return pl.pallas_call(
        body, name="gather_proj", grid_spec=grid_spec,
        out_shape=(jax.ShapeDtypeStruct((SEQ, D_MODEL), BF16),
                   jax.ShapeDtypeStruct((SEQ, N_SHARD * SHARD_COLS), F32),
                   jax.ShapeDtypeStruct((N_SHARD, D_MODEL, SHARD_COLS), BF16),
                   jax.ShapeDtypeStruct((8, D_CONV), F32)),
        compiler_params=pltpu.CompilerParams(dimension_semantics=("arbitrary", "arbitrary"),
                                             vmem_limit_bytes=VMEM_LIMIT, collective_id=COLLECTIVE_GATHER),
    )(kidx, x2d, g1, w_in, conv_w)


LAG = 6


def _mix_out(proj, lb_logits, cw, ga, gcn, g64, w_out, x2d, gf, tgt):
    half_o = WO_ROWS // 2
    nblk = SEQ // TB
    n_steps = nblk + LAG

    def body(p_ref, lbl_ref, cw_ref, ga_ref, gcn_ref, g64_ref, wo_ref, x_ref, gf_ref, t_ref,
             aux_ref, sto_ref, dx2_ref, dm_ref, gwo_ref, part_ref,
             st_ref, tail_ref, wog_v, stage, ring, acc_ref, send_sems, recv_sems):
        i = pl.program_id(0)
        x, y, c = lax.axis_index("x"), lax.axis_index("y"), lax.axis_index("c")
        k = 2 * x + y
        sibling = (x, y, 1 - c)
        chips = [(1 - x, y), (x, 1 - y), (1 - x, 1 - y)]
        kjs = [2 * cx + cy for cx, cy in chips]

        def wo_half(kk, cc):
            return wog_v.at[pl.ds(pl.multiple_of(kk * WO_ROWS + cc * half_o, half_o), half_o), :]

        def copy(sem, ref, to):
            return pltpu.make_async_remote_copy(
                src_ref=ref, dst_ref=ref, send_sem=send_sems.at[sem], recv_sem=recv_sems.at[sem],
                device_id=to, device_id_type=MESH)

        wo_direct = [copy(j, wo_half(k, c), (*chip, c)) for j, chip in enumerate(chips)]
        wo_passed = [copy(3 + j, wo_half(kj, c), sibling) for j, kj in enumerate(kjs)]

        @pl.when(i == 0)
        def _():
            barrier = pltpu.get_barrier_semaphore()
            for peer in [sibling] + [(*chip, c) for chip in chips]:
                pl.semaphore_signal(barrier, inc=1, device_id=peer, device_id_type=MESH)
            st_ref[...] = jnp.zeros_like(st_ref)
            tail_ref[...] = jnp.zeros_like(tail_ref)
            acc_ref[...] = jnp.zeros_like(acc_ref)
            part_ref[...] = jnp.zeros_like(part_ref)
            wog_v[pl.ds(pl.multiple_of(k * WO_ROWS, WO_ROWS), WO_ROWS), :] = wo_ref[0].astype(BF16)
            pl.semaphore_wait(barrier, 4)
            for cp in wo_direct:
                cp.start()

        @pl.when(i == LAG - 1)
        def _():
            for j in range(3):
                copy(j, wo_half(kjs[j], c), sibling).wait_recv()
                wo_passed[j].start()

        @pl.when(i == LAG)
        def _():
            for j in range(3):
                copy(3 + j, wo_half(kjs[j], 1 - c), sibling).wait_recv()

        lb = _lower_bound(lbl_ref[...])
        tri = _tri(True)
        causal = _causal()
        g64m = g64_ref[...]
        heads = range(N_HEADS)
        cs = [slice(hd * HEAD, (hd + 1) * HEAD) for hd in heads]
        col = lambda base, hd: slice(base + hd * HEAD, base + (hd + 1) * HEAD)

        def mix_chunk(n):
            sl = pl.ds(n * CHUNK, CHUNK)
            sg = [_sigmoid(p_ref[sl, col(512, hd)]) for hd in heads]
            f = [lb[:, cs[hd]] + (1.0 - lb[:, cs[hd]]) * sg[hd] for hd in heads]
            bc = _exact_left_many(tri, [jnp.log(f[hd]) for hd in heads])
            for hd in heads:
                aux_ref[sl, col(AUX_B, hd)] = bc[hd]
            g = [bc[hd][CHUNK - 1:CHUNK, :] for hd in heads]
            qd = [(p_ref[sl, col(0, hd)] * jnp.exp(bc[hd])).astype(BF16) for hd in heads]
            kk = [1.0 - f[hd] for hd in heads]
            ki = [(kk[hd] * jnp.exp(-bc[hd])).astype(BF16) for hd in heads]
            ke = [(kk[hd] * jnp.exp(g[hd] - bc[hd])).astype(BF16) for hd in heads]
            vb = [p_ref[sl, col(1024, hd)].astype(BF16) for hd in heads]
            st = [st_ref[hd] for hd in heads]
            st_b = [a.astype(BF16) for a in st]
            for hd in heads:
                sto_ref[n, hd] = st_b[hd]
            scm = [_dot_nt(qd[hd], ki[hd]) for hd in heads]
            inter = [_dot_nt(qd[hd], st_b[hd]) for hd in heads]
            upd = [_dot_tn(vb[hd], ke[hd]) for hd in heads]
            intra = [_dot(jnp.where(causal, scm[hd], 0.0).astype(BF16), vb[hd]) for hd in heads]
            for hd in heads:
                st_ref[hd] = st[hd] * jnp.exp(g[hd]) + upd[hd]
                o = intra[hd] + inter[hd]
                aux_ref[sl, col(AUX_O, hd)] = o
                ra = lax.rsqrt(jnp.mean(o * o, axis=-1, keepdims=True) + EPS)
                za = p_ref[sl, col(1536, hd)]
                stage[sl, cs[hd]] = (o * ra * ga_ref[:, cs[hd]] * (za * _sigmoid(za))).astype(BF16)
            yb = []
            for hd in heads:
                cu = p_ref[sl, col(3072, hd)] * p_ref[sl, col(2048, hd)]
                tail = tail_ref[:, cs[hd]]
                cv = (cw_ref[0:1, cs[hd]] * _shift_down(cu, 2, tail) + cw_ref[1:2, cs[hd]] * _shift_down(cu, 1, tail)
                      + cw_ref[2:3, cs[hd]] * cu)
                tail_ref[:, cs[hd]] = cu[CHUNK - 8:, :]
                aux_ref[sl, col(AUX_CV, hd)] = cv
                yb.append(p_ref[sl, col(2560, hd)] * cv)
            ms = _group_mean_many([y * y for y in yb], g64m)
            for hd in heads:
                rb = lax.rsqrt(ms[hd] + EPS)
                zb = p_ref[sl, col(3584, hd)]
                stage[sl, col(512, hd)] = (yb[hd] * rb * gcn_ref[:, cs[hd]] * (zb * _sigmoid(zb))).astype(BF16)

        def step(mix, project):
            if project:
                mixed_b = ring[pl.ds(pl.multiple_of((i - LAG) * TB, TB), TB), :]
                y = _dot(mixed_b, wog_v[...])
            if mix:
                mix_chunk(0)
            if project:
                x2 = x_ref[...] + y
                r2 = lax.rsqrt(jnp.mean(x2 * x2, axis=-1, keepdims=True) + EPS)
                n2 = x2 * r2
                gfv = gf_ref[...]
                err = n2 * gfv - t_ref[...]
                loss = 0.5 * jnp.sum(jnp.mean(err * err, axis=-1, keepdims=True), axis=0, keepdims=True)
                dy = err * (1.0 / D_MODEL)
                part_ref[1:2, :] += jnp.sum(dy * n2, axis=0, keepdims=True)
                part_ref[7:8, :] += jnp.broadcast_to(loss, (1, D_MODEL))
                dn = dy * gfv
                dx2 = r2 * (dn - n2 * jnp.mean(dn * n2, axis=-1, keepdims=True))
                dx2_ref[...] = dx2
                dx2_b = dx2.astype(BF16)
            if mix:
                mix_chunk(1)
            if project:
                dm_ref[...] = _dot_nt(dx2_b, wog_v[...])
            if mix:
                mix_chunk(2)
            if project:
                acc_ref[...] += _dot_tn(mixed_b, dx2_b)
            if mix:
                mix_chunk(3)
                ring[pl.ds(pl.multiple_of(i * TB, TB), TB), :] = stage[...]

        @pl.when(i < LAG)
        def _():
            step(True, False)

        @pl.when((i >= LAG) & (i < nblk))
        def _():
            step(True, True)

        @pl.when(i >= nblk)
        def _():
            step(False, True)

        @pl.when(i == n_steps - 1)
        def _():
            gwo_ref[...] = acc_ref[...].astype(BF16)
            for cp in wo_direct + wo_passed:
                cp.wait_send()

    assert NCB == 4
    row = lambda w: pl.BlockSpec((1, w), lambda i: (0, 0))
    mix_blk = lambda i: jnp.minimum(i, nblk - 1)
    out_blk = lambda i: jnp.clip(i - LAG, 0, nblk - 1)
    tok = lambda: pl.BlockSpec((TB, D_MODEL), lambda i: (out_blk(i), 0))
    return pl.pallas_call(
        body, name="mix_out", grid=(n_steps,),
        out_shape=(jax.ShapeDtypeStruct((SEQ, AUX_COLS), F32),
                   jax.ShapeDtypeStruct((N_CHUNKS, N_HEADS, HEAD, HEAD), BF16),
                   jax.ShapeDtypeStruct((SEQ, D_MODEL), F32),
                   jax.ShapeDtypeStruct((SEQ, D_MODEL), F32),
                   jax.ShapeDtypeStruct((D_MODEL, D_MODEL), BF16),
                   jax.ShapeDtypeStruct((8, D_MODEL), F32)),
        in_specs=[pl.BlockSpec((TB, 4096), lambda i: (jnp.minimum(i, nblk - 1), 0)),
                  pl.BlockSpec((2, D_HGRN), lambda i: (0, 0)),
                  pl.BlockSpec((8, D_CONV), lambda i: (0, 0)),
                  row(D_HGRN), row(D_CONV),
                  pl.BlockSpec((HEAD, HEAD), lambda i: (0, 0)),
                  pl.BlockSpec((1, WO_ROWS, D_MODEL), lambda i: (0, 0, 0)),
                  tok(), row(D_MODEL), tok()],
        out_specs=(pl.BlockSpec((TB, AUX_COLS), lambda i: (mix_blk(i), 0)),
                   pl.BlockSpec((NCB, N_HEADS, HEAD, HEAD), lambda i: (mix_blk(i), 0, 0, 0)),
                   tok(), tok(),
                   pl.BlockSpec((D_MODEL, D_MODEL), lambda i: (0, 0)),
                   pl.BlockSpec((8, D_MODEL), lambda i: (0, 0))),
        scratch_shapes=[pltpu.VMEM((N_HEADS, HEAD, HEAD), F32), pltpu.VMEM((8, D_CONV), F32),
                        pltpu.VMEM((D_MODEL, D_MODEL), BF16), pltpu.VMEM((TB, D_MODEL), BF16),
                        pltpu.VMEM((SEQ, D_MODEL), BF16), pltpu.VMEM((D_MODEL, D_MODEL), F32),
                        pltpu.SemaphoreType.DMA((6,)), pltpu.SemaphoreType.DMA((6,))],
        compiler_params=pltpu.CompilerParams(dimension_semantics=("arbitrary",), vmem_limit_bytes=VMEM_LIMIT,
                                             collective_id=COLLECTIVE_MIX_OUT),
    )(proj, lb_logits, cw, ga, gcn, g64, w_out, x2d, gf, tgt)


def _mix_bwd(proj, aux, states, dmixed, lb_logits, cw, ga, gcn, g64):
    nblk = SEQ // TB

    def body(p_ref, aux_ref, st_ref, dm_ref, lbl_ref, cw_ref, ga_ref, gcn_ref, g64_ref,
             dp_ref, part_ref, dst_ref, head_ref, dlb_ref):
        i = pl.program_id(0)

        @pl.when(i == 0)
        def _():
            dst_ref[...] = jnp.zeros_like(dst_ref)
            head_ref[...] = jnp.zeros_like(head_ref)
            part_ref[...] = jnp.zeros_like(part_ref)
            dlb_ref[...] = jnp.zeros_like(dlb_ref)

        lb = _lower_bound(lbl_ref[...])
        triu = _tri(False)
        causal = _causal()
        g64m = g64_ref[...]
        rowsum = lambda a: jnp.sum(a, axis=0, keepdims=True)
        heads = range(N_HEADS)
        cs = [slice(hd * HEAD, (hd + 1) * HEAD) for hd in heads]
        col = lambda base, hd: slice(base + hd * HEAD, base + (hd + 1) * HEAD)
        for n in reversed(range(NCB)):
            sl = pl.ds(n * CHUNK, CHUNK)
            cvv = [aux_ref[sl, col(AUX_CV, hd)] for hd in heads]
            gb = [p_ref[sl, col(2560, hd)] for hd in heads]
            yb = [gb[hd] * cvv[hd] for hd in heads]
            ms = _group_mean_many([y * y for y in yb], g64m)
            rb, nb, dnb = [], [], []
            for hd in heads:
                rb.append(lax.rsqrt(ms[hd] + EPS))
                nb.append(yb[hd] * rb[hd])
                zb = p_ref[sl, col(3584, hd)]
                sgb = _sigmoid(zb)
                dmb = dm_ref[sl, col(512, hd)]
                silu = zb * sgb
                dgate = dmb * gcn_ref[:, cs[hd]]
                part_ref[2:3, col(512, hd)] += rowsum(dmb * nb[hd] * silu)
                dp_ref[sl, col(3584, hd)] = (dgate * nb[hd] * (sgb + silu * (1.0 - sgb))).astype(BF16)
                dnb.append(dgate * silu)
            mdn = _group_mean_many([dnb[hd] * nb[hd] for hd in heads], g64m)
            for hd in heads:
                dyb = rb[hd] * (dnb[hd] - nb[hd] * mdn[hd])
                dp_ref[sl, col(2560, hd)] = (dyb * cvv[hd]).astype(BF16)
                dcv = dyb * gb[hd]
                head = head_ref[:, cs[hd]]
                dcv1 = _shift_up(dcv, 1, head)
                dcv2 = _shift_up(dcv, 2, head)
                head_ref[:, cs[hd]] = dcv[0:8, :]
                u = p_ref[sl, col(2048, hd)]
                gc = p_ref[sl, col(3072, hd)]
                cu = gc * u
                part_ref[4:5, cs[hd]] += rowsum(dcv2 * cu)
                part_ref[5:6, cs[hd]] += rowsum(dcv1 * cu)
                part_ref[6:7, cs[hd]] += rowsum(dcv * cu)
                dcu = cw_ref[2:3, cs[hd]] * dcv + cw_ref[1:2, cs[hd]] * dcv1 + cw_ref[0:1, cs[hd]] * dcv2
                dp_ref[sl, col(3072, hd)] = (dcu * u).astype(BF16)
                dp_ref[sl, col(2048, hd)] = (dcu * gc).astype(BF16)
            do_b = []
            for hd in heads:
                ov = aux_ref[sl, col(AUX_O, hd)]
                ra = lax.rsqrt(jnp.mean(ov * ov, axis=-1, keepdims=True) + EPS)
                na = ov * ra
                za = p_ref[sl, col(1536, hd)]
                sga = _sigmoid(za)
                dma = dm_ref[sl, cs[hd]]
                silu = za * sga
                dgate = dma * ga_ref[:, cs[hd]]
                part_ref[2:3, cs[hd]] += rowsum(dma * na * silu)
                dp_ref[sl, col(1536, hd)] = (dgate * na * (sga + silu * (1.0 - sga))).astype(BF16)
                dna = dgate * silu
                do_b.append((ra * (dna - na * jnp.mean(dna * na, axis=-1, keepdims=True))).astype(BF16))
            s = [_sigmoid(p_ref[sl, col(512, hd)]) for hd in heads]
            f = [lb[:, cs[hd]] + (1.0 - lb[:, cs[hd]]) * s[hd] for hd in heads]
            bc = [aux_ref[sl, col(AUX_B, hd)] for hd in heads]
            g = [bc[hd][CHUNK - 1:CHUNK, :] for hd in heads]
            eb = [jnp.exp(bc[hd]) for hd in heads]
            enb = [jnp.exp(-bc[hd]) for hd in heads]
            eg = [jnp.exp(g[hd] - bc[hd]) for hd in heads]
            dec = [jnp.exp(g[hd]) for hd in heads]
            qd = [p_ref[sl, cs[hd]] * eb[hd] for hd in heads]
            kk = [1.0 - f[hd] for hd in heads]
            ki = [kk[hd] * enb[hd] for hd in heads]
            ke = [kk[hd] * eg[hd] for hd in heads]
            qd_b = [a.astype(BF16) for a in qd]
            ki_b = [a.astype(BF16) for a in ki]
            ke_b = [a.astype(BF16) for a in ke]
            vb = [p_ref[sl, col(1024, hd)].astype(BF16) for hd in heads]
            st_b = [st_ref[n, hd] for hd in heads]
            dst = [dst_ref[hd] for hd in heads]
            dst_b = [a.astype(BF16) for a in dst]
            scm = [_dot_nt(qd_b[hd], ki_b[hd]) for hd in heads]
            amm = [_dot_nt(do_b[hd], vb[hd]) for hd in heads]
            dqd2 = [_dot(do_b[hd], st_b[hd]) for hd in heads]
            dke = [_dot(vb[hd], dst_b[hd]) for hd in heads]
            dv2 = [_dot_nt(ke_b[hd], dst_b[hd]) for hd in heads]
            dsu = [_dot_tn(do_b[hd], qd_b[hd]) for hd in heads]
            sc = [jnp.where(causal, scm[hd], 0.0).astype(BF16) for hd in heads]
            am = [jnp.where(causal, amm[hd], 0.0).astype(BF16) for hd in heads]
            dqd1 = [_dot(am[hd], ki_b[hd]) for hd in heads]
            dki = [_dot_tn(am[hd], qd_b[hd]) for hd in heads]
            dv1 = [_dot_tn(sc[hd], do_b[hd]) for hd in heads]
            db, dgv, dkk = [], [], []
            for hd in heads:
                dqd = dqd1[hd] + dqd2[hd]
                ddec = rowsum(dst[hd] * st_b[hd].astype(F32))
                dst_ref[hd] = dst[hd] * dec[hd] + dsu[hd]
                dp_ref[sl, cs[hd]] = (dqd * eb[hd]).astype(BF16)
                dp_ref[sl, col(1024, hd)] = (dv1[hd] + dv2[hd]).astype(BF16)
                dke_eg = dke[hd] * eg[hd]
                dkk.append(dki[hd] * enb[hd] + dke_eg)
                db.append(dqd * qd[hd] - kk[hd] * dkk[hd])
                dgv.append(rowsum(kk[hd] * dke_eg) + ddec * dec[hd])
            rc = _exact_left_many(triu, db, 2)
            for hd in heads:
                df = (rc[hd] + dgv[hd]) / f[hd] - dkk[hd]
                one_s = 1.0 - s[hd]
                dlb_ref[:, cs[hd]] += rowsum(df * one_s)
                dp_ref[sl, col(512, hd)] = (df * (1.0 - lb[:, cs[hd]]) * s[hd] * one_s).astype(BF16)

        @pl.when(i == nblk - 1)
        def _():
            row = dlb_ref[...] * lb * (1.0 - lb)
            part_ref[3:4, 0:D_HGRN] = row
            part_ref[3:4, D_HGRN:] = -row

    rev = lambda w: pl.BlockSpec((TB, w), lambda i: (nblk - 1 - i, 0))
    row = lambda w: pl.BlockSpec((1, w), lambda i: (0, 0))
    return pl.pallas_call(
        body, name="mix_bwd", grid=(nblk,),
        out_shape=(jax.ShapeDtypeStruct((SEQ, 4096), BF16),
                   jax.ShapeDtypeStruct((8, D_MODEL), F32)),
        in_specs=[rev(4096), rev(AUX_COLS),
                  pl.BlockSpec((NCB, N_HEADS, HEAD, HEAD), lambda i: (nblk - 1 - i, 0, 0, 0)),
                  rev(D_MODEL),
                  pl.BlockSpec((2, D_HGRN), lambda i: (0, 0)),
                  pl.BlockSpec((8, D_CONV), lambda i: (0, 0)),
                  row(D_HGRN), row(D_CONV),
                  pl.BlockSpec((HEAD, HEAD), lambda i: (0, 0))],
        out_specs=(rev(4096), pl.BlockSpec((8, D_MODEL), lambda i: (0, 0))),
        scratch_shapes=[pltpu.VMEM((N_HEADS, HEAD, HEAD), F32), pltpu.VMEM((8, D_CONV), F32),
                        pltpu.VMEM((1, D_HGRN), F32)],
        compiler_params=pltpu.CompilerParams(dimension_semantics=("arbitrary",), vmem_limit_bytes=VMEM_LIMIT),
    )(proj, aux, states, dmixed, lb_logits, cw, ga, gcn, g64)


TT = 1024
TX = 512
(SEM_D2D, SEM_D2D_O, SEM_ICI, SEM_ICI_O, SEM_FIN, SEM_FIN_O, SEM_SMALL, SEM_VIA, SEM_NORM, N_SEM_TAIL) = (
    0, 4, 5, 8, 11, 12, 12, 20, 22, 30)


def _bwd_tail(kidx, h, dproj, wg, gwo, x2d, dx2, g1, small_a, small_b):
    hw = D_MODEL // 2
    ho = WO_ROWS // 2
    nt = SEQ // TT
    norm_step = 2 * N_SHARD
    n_steps = norm_step + SEQ // TX // nt

    def body(k_ref, h_ref, dp_ref, w_ref, gwo_ref, x_ref, dx2_ref, g_ref, sm_ref, smb_ref,
             gx_ref, gw_out, gwo_out, osm_ref,
             acc, dh, sendbuf, keep, sibrcv, rcv, merge, sib_o, p_o, rcv_o, res_o, sm_buf, dng_buf, dng,
             send_sems, recv_sems, out_sems):
        s, t = pl.program_id(0), pl.program_id(1)
        x, y, c = lax.axis_index("x"), lax.axis_index("y"), lax.axis_index("c")
        k = 2 * x + y
        me = 4 * x + 2 * y + c
        sibling = (x, y, 1 - c)
        chips = [(1 - x, 1 - y), (1 - x, y), (x, 1 - y)]
        kjs = [2 * cx + cy for cx, cy in chips]
        mine = pl.ds(pl.multiple_of(c * hw, hw), hw)
        other = pl.ds(pl.multiple_of((1 - c) * hw, hw), hw)
        mine_o = pl.ds(pl.multiple_of(c * ho, ho), ho)
        other_o = pl.ds(pl.multiple_of((1 - c) * ho, ho), ho)

        def copy(sem, src, dst, to):
            return pltpu.make_async_remote_copy(
                src_ref=src, dst_ref=dst, send_sem=send_sems.at[sem], recv_sem=recv_sems.at[sem],
                device_id=to, device_id_type=MESH)

        def at_step(sv, tv):
            return pl.when((s == sv) & (t == tv))

        def at_norm_block(b):
            return at_step(norm_step + b // nt, b % nt)

        d2d = [copy(SEM_D2D + sv, sendbuf.at[sv], sibrcv.at[sv], sibling) for sv in range(N_SHARD)]
        d2d_o = copy(SEM_D2D_O, gwo_ref.at[:, other_o, :], sib_o, sibling)
        ici = {sv: copy(SEM_ICI + sv, keep.at[sv], rcv.at[sv - 1], (*chips[sv], c)) for sv in (1, 2)}
        qh = hw // 2
        via = [copy(SEM_VIA, keep.at[0, 0:qh, :], merge.at[1], (*chips[1], c)),
               copy(SEM_VIA + 1, keep.at[0, qh:hw, :], merge.at[0], (*chips[2], c))]
        merged_rows = [slice(qh, hw), slice(0, qh)]
        ici_o = [copy(SEM_ICI_O + sv, p_o.at[kjs[sv]], rcv_o.at[sv], (*chips[sv], c)) for sv in range(3)]
        fin = copy(SEM_FIN, acc.at[mine, :], gw_out.at[mine, :], sibling)
        fin_o = copy(SEM_FIN_O, res_o.at[mine_o, :], res_o.at[mine_o, :], sibling)
        peers = [(x ^ (m >> 2), y ^ ((m >> 1) & 1), c ^ (m & 1)) for m in range(1, N_DEV)]
        smalls = [copy(SEM_SMALL + 1 + j, sm_buf.at[me], sm_buf.at[me], to) for j, to in enumerate(peers)]
        dngs = [copy(SEM_NORM + 1 + j, dng_buf.at[me], dng_buf.at[me], to) for j, to in enumerate(peers)]
        store_w = pltpu.make_async_copy(acc.at[mine, :], gw_out.at[mine, :], out_sems.at[0])
        store_o = pltpu.make_async_copy(res_o, gwo_out, out_sems.at[1])

        @at_step(0, 0)
        def _():
            barrier = pltpu.get_barrier_semaphore()
            for to in peers:
                pl.semaphore_signal(barrier, inc=1, device_id=to, device_id_type=MESH)
            sm_buf[me] = sm_ref[...] + smb_ref[...]
            pl.semaphore_wait(barrier, N_DEV - 1)
            d2d_o.start()
            for cp in smalls:
                cp.start()

        @at_step(0, 1)
        def _():
            d2d_o.wait_recv()
            for j in range(N_SHARD):
                p_o[j] = (gwo_ref[j, mine_o, :].astype(F32) + sib_o[j].astype(F32)).astype(BF16)
            res_o[mine_o, :] = gwo_ref[k, mine_o, :].astype(F32) + sib_o[k].astype(F32)
            for cp in ici_o:
                cp.start()

        rows = pl.ds(pl.multiple_of(t * TT, TT), TT)

        @pl.when((s < N_SHARD) & (t == 0))
        def _():
            acc[...] = _dot_tn(h_ref[...], dp_ref[...])

        @pl.when((s < N_SHARD) & (t > 0))
        def _():
            acc[...] += _dot_tn(h_ref[...], dp_ref[...])

        for sv in range(N_SHARD):
            @at_step(sv, nt - 1)
            def _(sv=sv):
                sendbuf[sv] = acc[other, :].astype(BF16)
                if sv < 3:
                    keep[sv] = acc[mine, :].astype(BF16)
                d2d[sv].start()

        @at_step(1, 0)
        def _():
            d2d[0].wait_recv()
            keep[0] = (keep[0].astype(F32) + sibrcv[0].astype(F32)).astype(BF16)
            for cp in via:
                cp.start()

        for sv in (1, 2):
            @at_step(sv + 2, 0)
            def _(sv=sv):
                d2d[sv].wait_recv()
                keep[sv] = (keep[sv].astype(F32) + sibrcv[sv].astype(F32)).astype(BF16)
                via[2 - sv].wait_recv()
                rows_m = merged_rows[sv - 1]
                keep[sv, rows_m, :] = (keep[sv, rows_m, :].astype(F32) + merge[sv - 1].astype(F32)).astype(BF16)
                ici[sv].start()

        @pl.when(s == N_SHARD)
        def _():
            dh[rows, :] = _dot_nt(dp_ref[...], w_ref[0])

        @pl.when((s > N_SHARD) & (s < norm_step))
        def _():
            dh[rows, :] += _dot_nt(dp_ref[...], w_ref[0])

        @at_norm_block(0)
        def _():
            d2d[3].wait_recv()
            acc[mine, :] += sibrcv[3].astype(F32)

        @at_norm_block(1)
        def _():
            tot = res_o[mine_o, :]
            for sv in range(3):
                ici_o[sv].wait_recv()
                tot = tot + rcv_o[sv].astype(F32)
            res_o[mine_o, :] = tot
            fin_o.start()

        @at_norm_block(2)
        def _():
            ici[1].wait_recv()
            acc[mine, :] += rcv[0].astype(F32)

        @at_norm_block(SEQ // TX - 2)
        def _():
            ici[2].wait_recv()
            acc[mine, :] += rcv[1].astype(F32)
            fin.start()
            store_w.start()
            fin_o.wait_recv()
            store_o.start()

        @at_norm_block(0)
        def _():
            dng[...] = jnp.zeros_like(dng)

        @pl.when(s >= norm_step)
        def _():
            blk = (s - norm_step) * nt + t
            dhv = dh[pl.ds(pl.multiple_of(blk * TX, TX), TX), :]
            xv = x_ref[...]
            r = lax.rsqrt(jnp.mean(xv * xv, axis=-1, keepdims=True) + EPS)
            xn = xv * r
            dng[...] += jnp.sum(dhv * xn, axis=0, keepdims=True)
            dxn = dhv * g_ref[...]
            gx_ref[...] = dx2_ref[...] + r * (dxn - xn * jnp.mean(dxn * xn, axis=-1, keepdims=True))

        @at_step(n_steps - 1, nt - 1)
        def _():
            dng_buf[me] = dng[...]
            for cp in dngs:
                cp.start()
            for m in range(1, N_DEV):
                copy(SEM_SMALL + m, sm_buf.at[0], sm_buf.at[0], sibling).wait_recv()
            tot = sm_buf[0]
            for d in range(1, N_DEV):
                tot = tot + sm_buf[d]
            osm_ref[...] = tot
            for m in range(1, N_DEV):
                copy(SEM_NORM + m, dng_buf.at[0], dng_buf.at[0], sibling).wait_recv()
            tot = dng_buf[0]
            for d in range(1, N_DEV):
                tot = tot + dng_buf[d]
            osm_ref[0:1, :] = tot
            fin.wait_recv()
            for cp in d2d + [d2d_o] + via + list(ici.values()) + ici_o + [fin, fin_o] + smalls + dngs:
                cp.wait_send()
            store_o.wait()
            store_w.wait()

    def shard_of(s, kr):
        order = jnp.where(s < N_SHARD, s, jnp.where(s < norm_step, s - N_SHARD, 3))
        return kr[0] ^ (3 - order)

    def h_map(s, t, kr):
        return (jnp.where(s < N_SHARD, t, nt - 1), 0)

    def dp_map(s, t, kr):
        return (jnp.where(s < norm_step, t, nt - 1), shard_of(s, kr))

    def w_map(s, t, kr):
        return (shard_of(jnp.maximum(s, N_SHARD), kr), 0, 0)

    def blk_map(s, t, kr):
        return (jnp.where(s < norm_step, 0, (s - norm_step) * nt + t), 0)

    hbm = pl.BlockSpec(memory_space=pl.ANY)
    grid_spec = pltpu.PrefetchScalarGridSpec(
        num_scalar_prefetch=1, grid=(n_steps, nt),
        in_specs=[pl.BlockSpec((TT, D_MODEL), h_map),
                  pl.BlockSpec((TT, SHARD_COLS), dp_map),
                  pl.BlockSpec((1, D_MODEL, SHARD_COLS), w_map),
                  pl.BlockSpec((N_SHARD, WO_ROWS, D_MODEL), lambda s, t, kr: (0, 0, 0),
                               pipeline_mode=pl.Buffered(1)),
                  pl.BlockSpec((TX, D_MODEL), blk_map),
                  pl.BlockSpec((TX, D_MODEL), blk_map),
                  pl.BlockSpec((1, D_MODEL), lambda s, t, kr: (0, 0)),
                  pl.BlockSpec((8, D_MODEL), lambda s, t, kr: (0, 0)),
                  pl.BlockSpec((8, D_MODEL), lambda s, t, kr: (0, 0))],
        out_specs=(pl.BlockSpec((TX, D_MODEL), blk_map), hbm, hbm,
                   pl.BlockSpec((8, D_MODEL), lambda s, t, kr: (0, 0))),
        scratch_shapes=[pltpu.VMEM((D_MODEL, SHARD_COLS), F32), pltpu.VMEM((SEQ, D_MODEL), F32),
                        pltpu.VMEM((N_SHARD, hw, SHARD_COLS), BF16), pltpu.VMEM((3, hw, SHARD_COLS), BF16),
                        pltpu.VMEM((N_SHARD, hw, SHARD_COLS), BF16), pltpu.VMEM((2, hw, SHARD_COLS), BF16),
                        pltpu.VMEM((2, hw // 2, SHARD_COLS), BF16),
                        pltpu.VMEM((N_SHARD, ho, D_MODEL), BF16), pltpu.VMEM((N_SHARD, ho, D_MODEL), BF16),
                        pltpu.VMEM((3, ho, D_MODEL), BF16), pltpu.VMEM((WO_ROWS, D_MODEL), F32),
                        pltpu.VMEM((N_DEV, 8, D_MODEL), F32), pltpu.VMEM((N_DEV, 1, D_MODEL), F32),
                        pltpu.VMEM((1, D_MODEL), F32),
                        pltpu.SemaphoreType.DMA((N_SEM_TAIL,)), pltpu.SemaphoreType.DMA((N_SEM_TAIL,)),
                        pltpu.SemaphoreType.DMA((2,))])
    return pl.pallas_call(
        body, name="bwd_tail", grid_spec=grid_spec,
        out_shape=(jax.ShapeDtypeStruct((SEQ, D_MODEL), F32),
                   jax.ShapeDtypeStruct((D_MODEL, SHARD_COLS), F32),
                   jax.ShapeDtypeStruct((WO_ROWS, D_MODEL), F32),
                   jax.ShapeDtypeStruct((8, D_MODEL), F32)),
        compiler_params=pltpu.CompilerParams(dimension_semantics=("arbitrary", "arbitrary"),
                                             vmem_limit_bytes=61 * 1024 * 1024, collective_id=COLLECTIVE_TAIL),
    )(kidx, h, dproj, wg, gwo, x2d, dx2, g1, small_a, small_b)


def _adam_update(w, g, m, v):
    nm = ADAM_B1 * m + (1.0 - ADAM_B1) * g
    nv = ADAM_B2 * v + (1.0 - ADAM_B2) * (g * g)
    m_hat = nm / (1.0 - ADAM_B1 ** ADAM_STEP)
    v_hat = nv / (1.0 - ADAM_B2 ** ADAM_STEP)
    return -ADAM_LR * (m_hat / (jnp.sqrt(v_hat) + ADAM_EPS) + ADAM_WD * w), nm, nv


def _adamw_all(tot, g_w_in, g_w_out, big, small, grad_x):
    n = len(small)
    rows = WO_ROWS
    steps = D_MODEL // rows

    def body(tot_ref, *refs):
        gx_ref, gx_out = refs[2 + 3 * (2 + n)], refs[-1]
        gx_out[...] = gx_ref[...]
        ins, outs = refs[:2 + 3 * (2 + n)], refs[3 + 3 * (2 + n):-1]
        g_refs, wmv = ins[:2], ins[2:]
        loss_ref, quads = outs[0], outs[1:]

        def update(j, g):
            w_ref, m_ref, v_ref = wmv[3 * j:3 * j + 3]
            g_ref, d_ref, nm_ref, nv_ref = quads[4 * j:4 * j + 4]
            g_ref[...] = g
            d_ref[...], nm_ref[...], nv_ref[...] = _adam_update(w_ref[...], g, m_ref[...], v_ref[...])

        update(0, g_refs[0][...])

        @pl.when(pl.program_id(0) == 0)
        def _():
            update(1, g_refs[1][...])
            k = 2 * lax.axis_index("x") + lax.axis_index("y")
            mine = pl.ds(pl.multiple_of(k * HEAD, HEAD), HEAD)
            loss_ref[...] = tot_ref[7:8, 0:1]
            grads = [tot_ref[0:1, :], tot_ref[1:2, :], tot_ref[2:3, 0:D_HGRN], tot_ref[2:3, D_HGRN:],
                     jnp.concatenate([tot_ref[3:4, 0:D_HGRN], tot_ref[3:4, D_HGRN:]], axis=0),
                     jnp.concatenate([tot_ref[4 + tap:5 + tap, mine] for tap in range(3)], axis=1)]
            for j, g in enumerate(grads):
                update(2 + j, g)

    whole = lambda a: pl.BlockSpec(a.shape, lambda i: (0, 0))
    blk = pl.BlockSpec((rows, SHARD_COLS), lambda i: (i, 0))
    arrays = [a for triple in big + small for a in triple]
    in_specs = ([whole(tot), blk, whole(g_w_out)] + [blk] * 3 + [whole(a) for a in arrays[3:]])
    shapes = [big[0][0], big[1][0]] + [w for w, _, _ in small]
    out_shape = (jax.ShapeDtypeStruct((1, 1), F32),) + tuple(
        jax.ShapeDtypeStruct(w.shape, F32) for w in shapes for _ in range(4))
    out_specs = (pl.BlockSpec((1, 1), lambda i: (0, 0)),) + (blk,) * 4 + tuple(
        whole(w) for w in shapes[1:] for _ in range(4))
    gx_blk = pl.BlockSpec((SEQ // steps, D_MODEL), lambda i: (i, 0))
    outs = pl.pallas_call(
        body, name="adamw_all", grid=(steps,),
        out_shape=out_shape + (jax.ShapeDtypeStruct(grad_x.shape, F32),),
        in_specs=in_specs + [gx_blk], out_specs=out_specs + (gx_blk,),
        compiler_params=pltpu.CompilerParams(dimension_semantics=("arbitrary",), vmem_limit_bytes=VMEM_LIMIT),
    )(tot, g_w_in, g_w_out, *arrays, grad_x)
    return [outs[0]] + [outs[1 + 4 * j:5 + 4 * j] for j in range(2 + n)] + [outs[-1]]


def _local_step(x2d, tgt, proj, lb_logits, cw, ga, gcn, w_out, gf):
    g64 = _group_matrix(HEAD, CONV_GROUP)
    aux, states, dx2, dmixed, gwo, part_out = _mix_out(proj, lb_logits, cw, ga, gcn, g64, w_out, x2d, gf, tgt)
    dproj, part_mix = _mix_bwd(proj, aux, states, dmixed, lb_logits, cw, ga, gcn, g64)
    return dproj, dx2, gwo.reshape(N_SHARD, WO_ROWS, D_MODEL), part_out, part_mix


def kernel(x, norm_gain, w_in, lb_logits, conv_w, hgrn_norm_gain, conv_norm_gain, w_out, final_norm_gain, loss_target, m_norm_gain, m_w_in, m_lb_logits, m_conv_w, m_hgrn_norm_gain, m_conv_norm_gain, m_w_out, m_final_norm_gain, v_norm_gain, v_w_in, v_lb_logits, v_conv_w, v_hgrn_norm_gain, v_conv_norm_gain, v_w_out, v_final_norm_gain):
    k = 2 * lax.axis_index("x") + lax.axis_index("y")
    kidx = jnp.reshape(k, (1,)).astype(jnp.int32)
    row = lambda a: a.reshape(1, D_MODEL)
    taps = lambda a: a.reshape(1, 3 * HEAD)
    h, proj, wg, cw = _gather_proj(kidx, x[0], norm_gain, w_in, taps(conv_w))
    dproj, dx2, gwo, part_out, part_mix = _local_step(
        x[0], loss_target[0], proj, lb_logits, cw, hgrn_norm_gain, conv_norm_gain, w_out, row(final_norm_gain))
    rgrad_x, rg_w_in, rg_w_out, tot = _bwd_tail(kidx, h, dproj, wg, gwo, x[0], dx2, norm_gain, part_out, part_mix)

    (loss, (g_w_in, d_w_in, nm_w_in, nv_w_in), (g_w_out, d_w_out, nm_w_out, nv_w_out),
     (g_norm_gain, d_ng, nm_ng, nv_ng), (g_final, d_fg, nm_fg, nv_fg), (g_hgrn, d_hg, nm_hg, nv_hg),
     (g_convn, d_cg, nm_cg, nv_cg), (g_lb, d_lb, nm_lb, nv_lb), (g_conv_w, d_cw, nm_cw, nv_cw),
     grad_x) = _adamw_all(
        tot, rg_w_in, rg_w_out,
        [(w_in[0], m_w_in[0], v_w_in[0]), (w_out[0], m_w_out[0], v_w_out[0])],
        [(norm_gain, m_norm_gain, v_norm_gain),
         (row(final_norm_gain), row(m_final_norm_gain), row(v_final_norm_gain)),
         (hgrn_norm_gain, m_hgrn_norm_gain, v_hgrn_norm_gain),
         (conv_norm_gain, m_conv_norm_gain, v_conv_norm_gain),
         (lb_logits, m_lb_logits, v_lb_logits),
         (taps(conv_w), taps(m_conv_w), taps(v_conv_w))],
        rgrad_x)
    flat = lambda a: a.reshape(D_MODEL)
    untap = lambda a: a.reshape(1, 3, HEAD)
    return (loss.reshape(()), grad_x[None],
            g_norm_gain, g_w_in[None], g_lb, untap(g_conv_w), g_hgrn, g_convn, g_w_out[None], flat(g_final),
            d_ng, d_w_in[None], d_lb, untap(d_cw), d_hg, d_cg, d_w_out[None], flat(d_fg),
            nm_ng, nm_w_in[None], nm_lb, untap(nm_cw), nm_hg, nm_cg, nm_w_out[None], flat(nm_fg),
            nv_ng, nv_w_in[None], nv_lb, untap(nv_cw), nv_hg, nv_cg, nv_w_out[None], flat(nv_fg))
```

```python
import jax
import jax.numpy as jnp
import numpy as np
from jax import lax
from jax.experimental import pallas as pl
from jax.experimental.pallas import tpu as pltpu

F32 = jnp.float32
BF16 = jnp.bfloat16
MESH = pl.DeviceIdType.MESH

SEQ = 2048
D_MODEL = 1024
D_HGRN = 512
D_CONV = 512
HEAD = 128
N_HEADS = 4
CHUNK = 64
CONV_GROUP = 64
N_SHARD = 4
SHARD_COLS = 1024
WO_ROWS = 256
EPS = 1e-6
TB = 256
NCB = TB // CHUNK
N_CHUNKS = SEQ // CHUNK
N_DEV = 8
COLLECTIVE_GATHER, COLLECTIVE_MIX_OUT, COLLECTIVE_TAIL = 1, 0, 2
AUX_O, AUX_CV, AUX_B, AUX_COLS = 0, 512, 1024, 1536

ADAM_LR = 0.001
ADAM_B1 = 0.9
ADAM_B2 = 0.999
ADAM_EPS = 1e-08
ADAM_WD = 0.01
ADAM_STEP = 10

VMEM_LIMIT = 56 * 1024 * 1024


def _dot(a, b):
    return jnp.dot(a, b, preferred_element_type=F32)


def _dot_nt(a, b):
    return lax.dot_general(a, b, (((1,), (1,)), ((), ())), preferred_element_type=F32)


def _dot_tn(a, b):
    return lax.dot_general(a, b, (((0,), (0,)), ((), ())), preferred_element_type=F32)


def _split_bf16(x, n):
    parts = []
    r = x
    for _ in range(n):
        p = r.astype(BF16)
        parts.append(p)
        r = r - p.astype(F32)
    return parts


def _exact_left(m, x, n=3):
    acc = None
    for p in _split_bf16(x, n):
        t = _dot(m, p)
        acc = t if acc is None else acc + t
    return acc


def _exact_left_many(m, xs, n=3):
    parts = [_split_bf16(x, n) for x in xs]
    accs = [None] * len(xs)
    for i in range(n):
        for j in range(len(xs)):
            t = _dot(m, parts[j][i])
            accs[j] = t if accs[j] is None else accs[j] + t
    return accs


def _group_mean_many(xs, gmat, n=2):
    parts = [_split_bf16(x, n) for x in xs]
    accs = [None] * len(xs)
    for i in range(n):
        for j in range(len(xs)):
            t = _dot(parts[j][i], gmat)
            accs[j] = t if accs[j] is None else accs[j] + t
    return accs


def _group_mean(x, gmat, n=2):
    w = gmat.shape[0]
    outs = []
    for c0 in range(0, x.shape[1], w):
        acc = None
        for p in _split_bf16(x[:, c0:c0 + w], n):
            t = _dot(p, gmat)
            acc = t if acc is None else acc + t
        outs.append(acc)
    return jnp.concatenate(outs, axis=1)


def _sigmoid(x):
    return 1.0 / (1.0 + jnp.exp(-x))


def _lower_bound(lbl):
    l0 = lbl[0:1, :]
    l1 = lbl[1:2, :]
    m = jnp.maximum(l0, l1)
    e0 = jnp.exp(l0 - m)
    e1 = jnp.exp(l1 - m)
    return e0 / (e0 + e1)


def _tri(lower):
    r = lax.broadcasted_iota(jnp.int32, (CHUNK, CHUNK), 0)
    c = lax.broadcasted_iota(jnp.int32, (CHUNK, CHUNK), 1)
    return jnp.where((c <= r) if lower else (c >= r), 1.0, 0.0).astype(BF16)


def _causal():
    r = lax.broadcasted_iota(jnp.int32, (CHUNK, CHUNK), 0)
    c = lax.broadcasted_iota(jnp.int32, (CHUNK, CHUNK), 1)
    return c <= r


def _shift_down(x, sh, prev_tail):
    r = pltpu.roll(x, sh, 0)
    pt = pltpu.roll(prev_tail, sh, 0)
    rows = lax.broadcasted_iota(jnp.int32, prev_tail.shape, 0)
    top = jnp.where(rows < sh, pt, r[0:8])
    return jnp.concatenate([top, r[8:]], axis=0)


def _shift_up(x, sh, next_head):
    n = x.shape[0]
    r = pltpu.roll(x, n - sh, 0)
    nh = pltpu.roll(next_head, 8 - sh, 0)
    rows = lax.broadcasted_iota(jnp.int32, next_head.shape, 0)
    bot = jnp.where(rows >= 8 - sh, nh, r[n - 8:])
    return jnp.concatenate([r[:n - 8], bot], axis=0)


def _group_matrix(width, group):
    r = np.arange(width)[:, None] // group
    c = np.arange(width)[None, :] // group
    return jnp.asarray(np.where(r == c, 1.0 / group, 0.0), dtype=BF16)


TG = 1024
SEM_W, SEM_CW, SEM_W_FWD, N_SEM = 0, 4, 7, 11


def _gather_proj(kidx, x2d, g1, w_in, conv_w):
    half_w = D_MODEL // 2
    half_c = SHARD_COLS // 2
    nt = SEQ // TG
    n_steps = 2 * N_SHARD

    def body(k_ref, x_ref, g_ref, w_ref, cw_ref, h_ref, p_ref, wg_out, cwg_out,
             wg_v, cwg_v, send_sems, recv_sems, out_sems):
        s, t = pl.program_id(0), pl.program_id(1)
        x, y, c = lax.axis_index("x"), lax.axis_index("y"), lax.axis_index("c")
        k = 2 * x + y
        sibling = (x, y, 1 - c)
        chips = [(1 - x, y), (x, 1 - y), (1 - x, 1 - y)]
        kjs = [2 * cx + cy for cx, cy in chips]
        diag = (*chips[2], c)

        def w_half(kk, cc):
            return wg_v.at[kk, pl.ds(cc * half_w, half_w), :]

        def w_quarter(kk, cc, piece):
            return wg_v.at[kk, pl.ds(cc * half_w, half_w), piece * half_c:(piece + 1) * half_c]

        def cw_of(kk):
            return cwg_v.at[:, pl.ds(pl.multiple_of(kk * HEAD, HEAD), HEAD)]

        def copy(sem, ref, to):
            return pltpu.make_async_remote_copy(
                src_ref=ref, dst_ref=ref, send_sem=send_sems.at[sem], recv_sem=recv_sems.at[sem],
                device_id=to, device_id_type=MESH)

        def at_step(sv, tv):
            return pl.when((s == sv) & (t == tv))

        w_direct = ([copy(SEM_W + j, w_half(k, c), (*chips[j], c)) for j in range(2)]
                    + [copy(SEM_W + 2 + p, w_quarter(k, c, p), diag) for p in range(2)])
        cw_direct = [copy(SEM_CW + j, cw_of(k), (*chip, c)) for j, chip in enumerate(chips)]
        w_passed = ([copy(SEM_W_FWD + j, w_half(kjs[j], c), sibling) for j in range(2)]
                    + [copy(SEM_W_FWD + 2 + p, w_quarter(kjs[2], c, p), sibling) for p in range(2)])
        stores = ([pltpu.make_async_copy(wg_v.at[kk], wg_out.at[kk], out_sems.at[i])
                   for i, kk in enumerate([k] + kjs)]
                  + [pltpu.make_async_copy(cwg_v, cwg_out, out_sems.at[4])])

        @at_step(0, 0)
        def _():
            barrier = pltpu.get_barrier_semaphore()
            for peer in [sibling] + [(*chip, c) for chip in chips]:
                pl.semaphore_signal(barrier, inc=1, device_id=peer, device_id_type=MESH)
            wg_v[k] = w_ref[0].astype(BF16)
            mine = pl.ds(pl.multiple_of(k * HEAD, HEAD), HEAD)
            cwg_v[:, mine] = jnp.zeros((8, HEAD), F32)
            for tap in range(3):
                cwg_v[tap:tap + 1, mine] = cw_ref[:, tap * HEAD:(tap + 1) * HEAD]
            pl.semaphore_wait(barrier, 4)
            for cp in w_direct + cw_direct:
                cp.start()
            stores[0].start()

        @at_step(2, 0)
        def _():
            for j in range(2):
                copy(SEM_W + j, w_half(kjs[j], c), sibling).wait_recv()
                w_passed[j].start()
            copy(SEM_W_FWD, w_half(kjs[0], 1 - c), sibling).wait_recv()
            stores[1].start()

        @at_step(4, 0)
        def _():
            copy(SEM_W_FWD + 1, w_half(kjs[1], 1 - c), sibling).wait_recv()
            stores[2].start()

        for p in range(2):
            @at_step(6 + p, 0)
            def _(p=p):
                copy(SEM_W + 2 + p, w_quarter(kjs[2], c, p), sibling).wait_recv()
                w_passed[2 + p].start()
                copy(SEM_W_FWD + 2 + p, w_quarter(kjs[2], 1 - c, p), sibling).wait_recv()
                if p == 1:
                    stores[3].start()

        rows = pl.ds(pl.multiple_of(t * TG, TG), TG)

        @pl.when(s == 0)
        def _():
            xv = x_ref[...]
            r = lax.rsqrt(jnp.mean(xv * xv, axis=-1, keepdims=True) + EPS)
            h_ref[rows, :] = (xv * r * g_ref[...]).astype(BF16)

        sh = s >> 1
        js = k ^ (((sh & 1) << 1) | (sh >> 1))
        for piece in range(2):
            @pl.when((s & 1) == piece)
            def _(piece=piece):
                p_ref[...] = _dot(h_ref[rows, :], wg_v[js, :, piece * half_c:(piece + 1) * half_c])

        @at_step(n_steps - 1, nt - 1)
        def _():
            for j in range(3):
                copy(SEM_CW + j, cw_of(kjs[j]), sibling).wait_recv()
            stores[4].start()
            for cp in w_direct + cw_direct + w_passed:
                cp.wait_send()
            for st in stores:
                st.wait()

    def x_map(s, t, kr):
        return (jnp.where(s == 0, t, nt - 1), 0)

    def p_map(s, t, kr):
        sh = s >> 1
        return (t, 2 * (kr[0] ^ (((sh & 1) << 1) | (sh >> 1))) + (s & 1))

    hbm = pl.BlockSpec(memory_space=pl.ANY)
    grid_spec = pltpu.PrefetchScalarGridSpec(
        num_scalar_prefetch=1, grid=(n_steps, nt),
        in_specs=[pl.BlockSpec((TG, D_MODEL), x_map),
                  pl.BlockSpec((1, D_MODEL), lambda s, t, kr: (0, 0)),
                  pl.BlockSpec((1, D_MODEL, SHARD_COLS), lambda s, t, kr: (0, 0, 0)),
                  pl.BlockSpec((1, 3 * HEAD), lambda s, t, kr: (0, 0))],
        out_specs=(pl.BlockSpec((SEQ, D_MODEL), lambda s, t, kr: (0, 0)),
                   pl.BlockSpec((TG, half_c), p_map), hbm, hbm),
        scratch_shapes=[pltpu.VMEM((N_SHARD, D_MODEL, SHARD_COLS), BF16),
                        pltpu.VMEM((8, D_CONV), F32),
                        pltpu.SemaphoreType.DMA((N_SEM,)), pltpu.SemaphoreType.DMA((N_SEM,)),
                        pltpu.SemaphoreType.DMA((5,))])
    return pl.pallas_call(
        body, name="gather_proj", grid_spec=grid_spec,
        out_shape=(jax.ShapeDtypeStruct((SEQ, D_MODEL), BF16),
                   jax.ShapeDtypeStruct((SEQ, N_SHARD * SHARD_COLS), F32),
                   jax.ShapeDtypeStruct((N_SHARD, D_MODEL, SHARD_COLS), BF16),
                   jax.ShapeDtypeStruct((8, D_CONV), F32)),
        compiler_params=pltpu.CompilerParams(dimension_semantics=("arbitrary", "arbitrary"),
                                             vmem_limit_bytes=VMEM_LIMIT, collective_id=COLLECTIVE_GATHER),
    )(kidx, x2d, g1, w_in, conv_w)


LAG = 6


def _mix_out(proj, lb_logits, cw, ga, gcn, g64, w_out, x2d, gf, tgt):
    half_o = WO_ROWS // 2
    nblk = SEQ // TB
    n_steps = nblk + LAG

    def body(p_ref, lbl_ref, cw_ref, ga_ref, gcn_ref, g64_ref, wo_ref, x_ref, gf_ref, t_ref,
             aux_ref, sto_ref, dx2_ref, dm_ref, gwo_ref, part_ref,
             st_ref, tail_ref, wog_v, stage, ring, acc_ref, send_sems, recv_sems):
        i = pl.program_id(0)
        x, y, c = lax.axis_index("x"), lax.axis_index("y"), lax.axis_index("c")
        k = 2 * x + y
        sibling = (x, y, 1 - c)
        chips = [(1 - x, y), (x, 1 - y), (1 - x, 1 - y)]
        kjs = [2 * cx + cy for cx, cy in chips]

        def wo_half(kk, cc):
            return wog_v.at[pl.ds(pl.multiple_of(kk * WO_ROWS + cc * half_o, half_o), half_o), :]

        def copy(sem, ref, to):
            return pltpu.make_async_remote_copy(
                src_ref=ref, dst_ref=ref, send_sem=send_sems.at[sem], recv_sem=recv_sems.at[sem],
                device_id=to, device_id_type=MESH)

        wo_direct = [copy(j, wo_half(k, c), (*chip, c)) for j, chip in enumerate(chips)]
        wo_passed = [copy(3 + j, wo_half(kj, c), sibling) for j, kj in enumerate(kjs)]

        @pl.when(i == 0)
        def _():
            barrier = pltpu.get_barrier_semaphore()
            for peer in [sibling] + [(*chip, c) for chip in chips]:
                pl.semaphore_signal(barrier, inc=1, device_id=peer, device_id_type=MESH)
            st_ref[...] = jnp.zeros_like(st_ref)
            tail_ref[...] = jnp.zeros_like(tail_ref)
            acc_ref[...] = jnp.zeros_like(acc_ref)
            part_ref[...] = jnp.zeros_like(part_ref)
            wog_v[pl.ds(pl.multiple_of(k * WO_ROWS, WO_ROWS), WO_ROWS), :] = wo_ref[0].astype(BF16)
            pl.semaphore_wait(barrier, 4)
            for cp in wo_direct:
                cp.start()

        @pl.when(i == LAG - 1)
        def _():
            for j in range(3):
                copy(j, wo_half(kjs[j], c), sibling).wait_recv()
                wo_passed[j].start()

        @pl.when(i == LAG)
        def _():
            for j in range(3):
                copy(3 + j, wo_half(kjs[j], 1 - c), sibling).wait_recv()

        lb = _lower_bound(lbl_ref[...])
        tri = _tri(True)
        causal = _causal()
        g64m = g64_ref[...]
        heads = range(N_HEADS)
        cs = [slice(hd * HEAD, (hd + 1) * HEAD) for hd in heads]
        col = lambda base, hd: slice(base + hd * HEAD, base + (hd + 1) * HEAD)

        def mix_chunk(n):
            sl = pl.ds(n * CHUNK, CHUNK)
            sg = [_sigmoid(p_ref[sl, col(512, hd)]) for hd in heads]
            f = [lb[:, cs[hd]] + (1.0 - lb[:, cs[hd]]) * sg[hd] for hd in heads]
            bc = _exact_left_many(tri, [jnp.log(f[hd]) for hd in heads])
            for hd in heads:
                aux_ref[sl, col(AUX_B, hd)] = bc[hd]
            g = [bc[hd][CHUNK - 1:CHUNK, :] for hd in heads]
            qd = [(p_ref[sl, col(0, hd)] * jnp.exp(bc[hd])).astype(BF16) for hd in heads]
            kk = [1.0 - f[hd] for hd in heads]
            ki = [(kk[hd] * jnp.exp(-bc[hd])).astype(BF16) for hd in heads]
            ke = [(kk[hd] * jnp.exp(g[hd] - bc[hd])).astype(BF16) for hd in heads]
            vb = [p_ref[sl, col(1024, hd)].astype(BF16) for hd in heads]
            st = [st_ref[hd] for hd in heads]
            st_b = [a.astype(BF16) for a in st]
            for hd in heads:
                sto_ref[n, hd] = st_b[hd]
            scm = [_dot_nt(qd[hd], ki[hd]) for hd in heads]
            inter = [_dot_nt(qd[hd], st_b[hd]) for hd in heads]
            upd = [_dot_tn(vb[hd], ke[hd]) for hd in heads]
            intra = [_dot(jnp.where(causal, scm[hd], 0.0).astype(BF16), vb[hd]) for hd in heads]
            for hd in heads:
                st_ref[hd] = st[hd] * jnp.exp(g[hd]) + upd[hd]
                o = intra[hd] + inter[hd]
                aux_ref[sl, col(AUX_O, hd)] = o
                ra = lax.rsqrt(jnp.mean(o * o, axis=-1, keepdims=True) + EPS)
                za = p_ref[sl, col(1536, hd)]
                stage[sl, cs[hd]] = (o * ra * ga_ref[:, cs[hd]] * (za * _sigmoid(za))).astype(BF16)
            yb = []
            for hd in heads:
                cu = p_ref[sl, col(3072, hd)] * p_ref[sl, col(2048, hd)]
                tail = tail_ref[:, cs[hd]]
                cv = (cw_ref[0:1, cs[hd]] * _shift_down(cu, 2, tail) + cw_ref[1:2, cs[hd]] * _shift_down(cu, 1, tail)
                      + cw_ref[2:3, cs[hd]] * cu)
                tail_ref[:, cs[hd]] = cu[CHUNK - 8:, :]
                aux_ref[sl, col(AUX_CV, hd)] = cv
                yb.append(p_ref[sl, col(2560, hd)] * cv)
            ms = _group_mean_many([y * y for y in yb], g64m)
            for hd in heads:
                rb = lax.rsqrt(ms[hd] + EPS)
                zb = p_ref[sl, col(3584, hd)]
                stage[sl, col(512, hd)] = (yb[hd] * rb * gcn_ref[:, cs[hd]] * (zb * _sigmoid(zb))).astype(BF16)

        def step(mix, project, last=False):
            if project:
                mixed_b = ring[pl.ds(pl.multiple_of((i - LAG) * TB, TB), TB), :]
                y = _dot(mixed_b, wog_v[...])
            if mix:
                mix_chunk(0)
            if project:
                x2 = x_ref[...] + y
                r2 = lax.rsqrt(jnp.mean(x2 * x2, axis=-1, keepdims=True) + EPS)
                n2 = x2 * r2
                gfv = gf_ref[...]
                err = n2 * gfv - t_ref[...]
                loss = 0.5 * jnp.sum(jnp.mean(err * err, axis=-1, keepdims=True), axis=0, keepdims=True)
                dy = err * (1.0 / D_MODEL)
                part_ref[1:2, :] += jnp.sum(dy * n2, axis=0, keepdims=True)
                part_ref[7:8, :] += jnp.broadcast_to(loss, (1, D_MODEL))
                dn = dy * gfv
                dx2 = r2 * (dn - n2 * jnp.mean(dn * n2, axis=-1, keepdims=True))
                dx2_ref[...] = dx2
                dx2_b = dx2.astype(BF16)
            if mix:
                mix_chunk(1)
            if project:
                dm_ref[...] = _dot_nt(dx2_b, wog_v[...])
            if mix:
                mix_chunk(2)
            if project and last:
                gwo_ref[...] = (acc_ref[...] + _dot_tn(mixed_b, dx2_b)).astype(BF16)
            elif project:
                acc_ref[...] += _dot_tn(mixed_b, dx2_b)
            if mix:
                mix_chunk(3)
                ring[pl.ds(pl.multiple_of(i * TB, TB), TB), :] = stage[...]

        @pl.when(i < LAG)
        def _():
            step(True, False)

        @pl.when((i >= LAG) & (i < nblk))
        def _():
            step(True, True)

        @pl.when((i >= nblk) & (i < n_steps - 1))
        def _():
            step(False, True)

        @pl.when(i == n_steps - 1)
        def _():
            step(False, True, last=True)
            for cp in wo_direct + wo_passed:
                cp.wait_send()

    assert NCB == 4
    row = lambda w: pl.BlockSpec((1, w), lambda i: (0, 0))
    mix_blk = lambda i: jnp.minimum(i, nblk - 1)
    out_blk = lambda i: jnp.clip(i - LAG, 0, nblk - 1)
    tok = lambda: pl.BlockSpec((TB, D_MODEL), lambda i: (out_blk(i), 0))
    return pl.pallas_call(
        body, name="mix_out", grid=(n_steps,),
        out_shape=(jax.ShapeDtypeStruct((SEQ, AUX_COLS), F32),
                   jax.ShapeDtypeStruct((N_CHUNKS, N_HEADS, HEAD, HEAD), BF16),
                   jax.ShapeDtypeStruct((SEQ, D_MODEL), F32),
                   jax.ShapeDtypeStruct((SEQ, D_MODEL), F32),
                   jax.ShapeDtypeStruct((D_MODEL, D_MODEL), BF16),
                   jax.ShapeDtypeStruct((8, D_MODEL), F32)),
        in_specs=[pl.BlockSpec((TB, 4096), lambda i: (jnp.minimum(i, nblk - 1), 0)),
                  pl.BlockSpec((2, D_HGRN), lambda i: (0, 0)),
                  pl.BlockSpec((8, D_CONV), lambda i: (0, 0)),
                  row(D_HGRN), row(D_CONV),
                  pl.BlockSpec((HEAD, HEAD), lambda i: (0, 0)),
                  pl.BlockSpec((1, WO_ROWS, D_MODEL), lambda i: (0, 0, 0)),
                  tok(), row(D_MODEL), tok()],
        out_specs=(pl.BlockSpec((TB, AUX_COLS), lambda i: (mix_blk(i), 0)),
                   pl.BlockSpec((NCB, N_HEADS, HEAD, HEAD), lambda i: (mix_blk(i), 0, 0, 0)),
                   tok(), tok(),
                   pl.BlockSpec((D_MODEL, D_MODEL), lambda i: (0, 0)),
                   pl.BlockSpec((8, D_MODEL), lambda i: (0, 0))),
        scratch_shapes=[pltpu.VMEM((N_HEADS, HEAD, HEAD), F32), pltpu.VMEM((8, D_CONV), F32),
                        pltpu.VMEM((D_MODEL, D_MODEL), BF16), pltpu.VMEM((TB, D_MODEL), BF16),
                        pltpu.VMEM((SEQ, D_MODEL), BF16), pltpu.VMEM((D_MODEL, D_MODEL), F32),
                        pltpu.SemaphoreType.DMA((6,)), pltpu.SemaphoreType.DMA((6,))],
        compiler_params=pltpu.CompilerParams(dimension_semantics=("arbitrary",), vmem_limit_bytes=VMEM_LIMIT,
                                             collective_id=COLLECTIVE_MIX_OUT),
    )(proj, lb_logits, cw, ga, gcn, g64, w_out, x2d, gf, tgt)


def _mix_bwd(proj, aux, states, dmixed, lb_logits, cw, ga, gcn, g64):
    nblk = SEQ // TB

    def body(p_ref, aux_ref, st_ref, dm_ref, lbl_ref, cw_ref, ga_ref, gcn_ref, g64_ref,
             dp_ref, part_ref, dst_ref, head_ref, dlb_ref):
        i = pl.program_id(0)

        @pl.when(i == 0)
        def _():
            dst_ref[...] = jnp.zeros_like(dst_ref)
            head_ref[...] = jnp.zeros_like(head_ref)
            part_ref[...] = jnp.zeros_like(part_ref)
            dlb_ref[...] = jnp.zeros_like(dlb_ref)

        lb = _lower_bound(lbl_ref[...])
        triu = _tri(False)
        causal = _causal()
        g64m = g64_ref[...]
        rowsum = lambda a: jnp.sum(a, axis=0, keepdims=True)
        heads = range(N_HEADS)
        cs = [slice(hd * HEAD, (hd + 1) * HEAD) for hd in heads]
        col = lambda base, hd: slice(base + hd * HEAD, base + (hd + 1) * HEAD)
        for n in reversed(range(NCB)):
            sl = pl.ds(n * CHUNK, CHUNK)
            cvv = [aux_ref[sl, col(AUX_CV, hd)] for hd in heads]
            gb = [p_ref[sl, col(2560, hd)] for hd in heads]
            yb = [gb[hd] * cvv[hd] for hd in heads]
            ms = _group_mean_many([y * y for y in yb], g64m)
            rb, nb, dnb = [], [], []
            for hd in heads:
                rb.append(lax.rsqrt(ms[hd] + EPS))
                nb.append(yb[hd] * rb[hd])
                zb = p_ref[sl, col(3584, hd)]
                sgb = _sigmoid(zb)
                dmb = dm_ref[sl, col(512, hd)]
                silu = zb * sgb
                dgate = dmb * gcn_ref[:, cs[hd]]
                part_ref[2:3, col(512, hd)] += rowsum(dmb * nb[hd] * silu)
                dp_ref[sl, col(3584, hd)] = (dgate * nb[hd] * (sgb + silu * (1.0 - sgb))).astype(BF16)
                dnb.append(dgate * silu)
            mdn = _group_mean_many([dnb[hd] * nb[hd] for hd in heads], g64m)
            for hd in heads:
                dyb = rb[hd] * (dnb[hd] - nb[hd] * mdn[hd])
                dp_ref[sl, col(2560, hd)] = (dyb * cvv[hd]).astype(BF16)
                dcv = dyb * gb[hd]
                head = head_ref[:, cs[hd]]
                dcv1 = _shift_up(dcv, 1, head)
                dcv2 = _shift_up(dcv, 2, head)
                head_ref[:, cs[hd]] = dcv[0:8, :]
                u = p_ref[sl, col(2048, hd)]
                gc = p_ref[sl, col(3072, hd)]
                cu = gc * u
                part_ref[4:5, cs[hd]] += rowsum(dcv2 * cu)
                part_ref[5:6, cs[hd]] += rowsum(dcv1 * cu)
                part_ref[6:7, cs[hd]] += rowsum(dcv * cu)
                dcu = cw_ref[2:3, cs[hd]] * dcv + cw_ref[1:2, cs[hd]] * dcv1 + cw_ref[0:1, cs[hd]] * dcv2
                dp_ref[sl, col(3072, hd)] = (dcu * u).astype(BF16)
                dp_ref[sl, col(2048, hd)] = (dcu * gc).astype(BF16)
            do_b = []
            for hd in heads:
                ov = aux_ref[sl, col(AUX_O, hd)]
                ra = lax.rsqrt(jnp.mean(ov * ov, axis=-1, keepdims=True) + EPS)
                na = ov * ra
                za = p_ref[sl, col(1536, hd)]
                sga = _sigmoid(za)
                dma = dm_ref[sl, cs[hd]]
                silu = za * sga
                dgate = dma * ga_ref[:, cs[hd]]
                part_ref[2:3, cs[hd]] += rowsum(dma * na * silu)
                dp_ref[sl, col(1536, hd)] = (dgate * na * (sga + silu * (1.0 - sga))).astype(BF16)
                dna = dgate * silu
                do_b.append((ra * (dna - na * jnp.mean(dna * na, axis=-1, keepdims=True))).astype(BF16))
            s = [_sigmoid(p_ref[sl, col(512, hd)]) for hd in heads]
            f = [lb[:, cs[hd]] + (1.0 - lb[:, cs[hd]]) * s[hd] for hd in heads]
            bc = [aux_ref[sl, col(AUX_B, hd)] for hd in heads]
            g = [bc[hd][CHUNK - 1:CHUNK, :] for hd in heads]
            eb = [jnp.exp(bc[hd]) for hd in heads]
            enb = [jnp.exp(-bc[hd]) for hd in heads]
            eg = [jnp.exp(g[hd] - bc[hd]) for hd in heads]
            dec = [jnp.exp(g[hd]) for hd in heads]
            qd = [p_ref[sl, cs[hd]] * eb[hd] for hd in heads]
            kk = [1.0 - f[hd] for hd in heads]
            ki = [kk[hd] * enb[hd] for hd in heads]
            ke = [kk[hd] * eg[hd] for hd in heads]
            qd_b = [a.astype(BF16) for a in qd]
            ki_b = [a.astype(BF16) for a in ki]
            ke_b = [a.astype(BF16) for a in ke]
            vb = [p_ref[sl, col(1024, hd)].astype(BF16) for hd in heads]
            st_b = [st_ref[n, hd] for hd in heads]
            dst = [dst_ref[hd] for hd in heads]
            dst_b = [a.astype(BF16) for a in dst]
            scm = [_dot_nt(qd_b[hd], ki_b[hd]) for hd in heads]
            amm = [_dot_nt(do_b[hd], vb[hd]) for hd in heads]
            dqd2 = [_dot(do_b[hd], st_b[hd]) for hd in heads]
            dke = [_dot(vb[hd], dst_b[hd]) for hd in heads]
            dv2 = [_dot_nt(ke_b[hd], dst_b[hd]) for hd in heads]
            dsu = [_dot_tn(do_b[hd], qd_b[hd]) for hd in heads]
            sc = [jnp.where(causal, scm[hd], 0.0).astype(BF16) for hd in heads]
            am = [jnp.where(causal, amm[hd], 0.0).astype(BF16) for hd in heads]
            dqd1 = [_dot(am[hd], ki_b[hd]) for hd in heads]
            dki = [_dot_tn(am[hd], qd_b[hd]) for hd in heads]
            dv1 = [_dot_tn(sc[hd], do_b[hd]) for hd in heads]
            db, dgv, dkk = [], [], []
            for hd in heads:
                dqd = dqd1[hd] + dqd2[hd]
                ddec = rowsum(dst[hd] * st_b[hd].astype(F32))
                dst_ref[hd] = dst[hd] * dec[hd] + dsu[hd]
                dp_ref[sl, cs[hd]] = (dqd * eb[hd]).astype(BF16)
                dp_ref[sl, col(1024, hd)] = (dv1[hd] + dv2[hd]).astype(BF16)
                dke_eg = dke[hd] * eg[hd]
                dkk.append(dki[hd] * enb[hd] + dke_eg)
                db.append(dqd * qd[hd] - kk[hd] * dkk[hd])
                dgv.append(rowsum(kk[hd] * dke_eg) + ddec * dec[hd])
            rc = _exact_left_many(triu, db, 2)
            for hd in heads:
                df = (rc[hd] + dgv[hd]) / f[hd] - dkk[hd]
                one_s = 1.0 - s[hd]
                dlb_ref[:, cs[hd]] += rowsum(df * one_s)
                dp_ref[sl, col(512, hd)] = (df * (1.0 - lb[:, cs[hd]]) * s[hd] * one_s).astype(BF16)

        @pl.when(i == nblk - 1)
        def _():
            row = dlb_ref[...] * lb * (1.0 - lb)
            part_ref[3:4, 0:D_HGRN] = row
            part_ref[3:4, D_HGRN:] = -row

    rev = lambda w: pl.BlockSpec((TB, w), lambda i: (nblk - 1 - i, 0))
    row = lambda w: pl.BlockSpec((1, w), lambda i: (0, 0))
    return pl.pallas_call(
        body, name="mix_bwd", grid=(nblk,),
        out_shape=(jax.ShapeDtypeStruct((SEQ, 4096), BF16),
                   jax.ShapeDtypeStruct((8, D_MODEL), F32)),
        in_specs=[rev(4096), rev(AUX_COLS),
                  pl.BlockSpec((NCB, N_HEADS, HEAD, HEAD), lambda i: (nblk - 1 - i, 0, 0, 0)),
                  rev(D_MODEL),
                  pl.BlockSpec((2, D_HGRN), lambda i: (0, 0)),
                  pl.BlockSpec((8, D_CONV), lambda i: (0, 0)),
                  row(D_HGRN), row(D_CONV),
                  pl.BlockSpec((HEAD, HEAD), lambda i: (0, 0))],
        out_specs=(rev(4096), pl.BlockSpec((8, D_MODEL), lambda i: (0, 0))),
        scratch_shapes=[pltpu.VMEM((N_HEADS, HEAD, HEAD), F32), pltpu.VMEM((8, D_CONV), F32),
                        pltpu.VMEM((1, D_HGRN), F32)],
        compiler_params=pltpu.CompilerParams(dimension_semantics=("arbitrary",), vmem_limit_bytes=VMEM_LIMIT),
    )(proj, aux, states, dmixed, lb_logits, cw, ga, gcn, g64)


TT = 1024
TX = 512
(SEM_D2D, SEM_D2D_O, SEM_ICI, SEM_ICI_O, SEM_FIN, SEM_FIN_O, SEM_SMALL, SEM_VIA, SEM_NORM, N_SEM_TAIL) = (
    0, 4, 5, 8, 11, 12, 12, 20, 22, 30)


def _bwd_tail(kidx, h, dproj, wg, gwo, x2d, dx2, g1, small_a, small_b):
    hw = D_MODEL // 2
    ho = WO_ROWS // 2
    nt = SEQ // TT
    norm_step = 2 * N_SHARD
    n_steps = norm_step + SEQ // TX // nt

    def body(k_ref, h_ref, dp_ref, w_ref, gwo_ref, x_ref, dx2_ref, g_ref, sm_ref, smb_ref,
             gx_ref, gw_out, gwo_out, osm_ref,
             acc, dh, sendbuf, keep, sibrcv, rcv, merge, sib_o, p_o, rcv_o, res_o, sm_buf, dng_buf, dng,
             send_sems, recv_sems, out_sems):
        s, t = pl.program_id(0), pl.program_id(1)
        x, y, c = lax.axis_index("x"), lax.axis_index("y"), lax.axis_index("c")
        k = 2 * x + y
        me = 4 * x + 2 * y + c
        sibling = (x, y, 1 - c)
        chips = [(1 - x, 1 - y), (1 - x, y), (x, 1 - y)]
        kjs = [2 * cx + cy for cx, cy in chips]
        mine = pl.ds(pl.multiple_of(c * hw, hw), hw)
        other = pl.ds(pl.multiple_of((1 - c) * hw, hw), hw)
        mine_o = pl.ds(pl.multiple_of(c * ho, ho), ho)
        other_o = pl.ds(pl.multiple_of((1 - c) * ho, ho), ho)

        def copy(sem, src, dst, to):
            return pltpu.make_async_remote_copy(
                src_ref=src, dst_ref=dst, send_sem=send_sems.at[sem], recv_sem=recv_sems.at[sem],
                device_id=to, device_id_type=MESH)

        def at_step(sv, tv):
            return pl.when((s == sv) & (t == tv))

        def at_norm_block(b):
            return at_step(norm_step + b // nt, b % nt)

        d2d = [copy(SEM_D2D + sv, sendbuf.at[sv], sibrcv.at[sv], sibling) for sv in range(N_SHARD)]
        d2d_o = copy(SEM_D2D_O, gwo_ref.at[:, other_o, :], sib_o, sibling)
        ici = {sv: copy(SEM_ICI + sv, keep.at[sv], rcv.at[sv - 1], (*chips[sv], c)) for sv in (1, 2)}
        qh = hw // 2
        via = [copy(SEM_VIA, keep.at[0, 0:qh, :], merge.at[1], (*chips[1], c)),
               copy(SEM_VIA + 1, keep.at[0, qh:hw, :], merge.at[0], (*chips[2], c))]
        merged_rows = [slice(qh, hw), slice(0, qh)]
        ici_o = [copy(SEM_ICI_O + sv, p_o.at[kjs[sv]], rcv_o.at[sv], (*chips[sv], c)) for sv in range(3)]
        fin = copy(SEM_FIN, acc.at[mine, :], gw_out.at[mine, :], sibling)
        fin_o = copy(SEM_FIN_O, res_o.at[mine_o, :], res_o.at[mine_o, :], sibling)
        peers = [(x ^ (m >> 2), y ^ ((m >> 1) & 1), c ^ (m & 1)) for m in range(1, N_DEV)]
        smalls = [copy(SEM_SMALL + 1 + j, sm_buf.at[me], sm_buf.at[me], to) for j, to in enumerate(peers)]
        dngs = [copy(SEM_NORM + 1 + j, dng_buf.at[me], dng_buf.at[me], to) for j, to in enumerate(peers)]
        store_w = pltpu.make_async_copy(acc.at[mine, :], gw_out.at[mine, :], out_sems.at[0])
        store_o = pltpu.make_async_copy(res_o, gwo_out, out_sems.at[1])

        @at_step(0, 0)
        def _():
            barrier = pltpu.get_barrier_semaphore()
            for to in peers:
                pl.semaphore_signal(barrier, inc=1, device_id=to, device_id_type=MESH)
            sm_buf[me] = sm_ref[...] + smb_ref[...]
            pl.semaphore_wait(barrier, N_DEV - 1)
            d2d_o.start()
            for cp in smalls:
                cp.start()

        @at_step(0, 1)
        def _():
            d2d_o.wait_recv()
            for j in range(N_SHARD):
                p_o[j] = (gwo_ref[j, mine_o, :].astype(F32) + sib_o[j].astype(F32)).astype(BF16)
            res_o[mine_o, :] = gwo_ref[k, mine_o, :].astype(F32) + sib_o[k].astype(F32)
            for cp in ici_o:
                cp.start()

        rows = pl.ds(pl.multiple_of(t * TT, TT), TT)

        @pl.when((s < N_SHARD) & (t == 0))
        def _():
            acc[...] = _dot_tn(h_ref[...], dp_ref[...])

        @pl.when((s < N_SHARD) & (t > 0))
        def _():
            acc[...] += _dot_tn(h_ref[...], dp_ref[...])

        for sv in range(N_SHARD):
            @at_step(sv, nt - 1)
            def _(sv=sv):
                sendbuf[sv] = acc[other, :].astype(BF16)
                if sv < 3:
                    keep[sv] = acc[mine, :].astype(BF16)
                d2d[sv].start()

        @at_step(1, 0)
        def _():
            d2d[0].wait_recv()
            keep[0] = (keep[0].astype(F32) + sibrcv[0].astype(F32)).astype(BF16)
            for cp in via:
                cp.start()

        for sv in (1, 2):
            @at_step(sv + 2, 0)
            def _(sv=sv):
                d2d[sv].wait_recv()
                keep[sv] = (keep[sv].astype(F32) + sibrcv[sv].astype(F32)).astype(BF16)
                via[2 - sv].wait_recv()
                rows_m = merged_rows[sv - 1]
                keep[sv, rows_m, :] = (keep[sv, rows_m, :].astype(F32) + merge[sv - 1].astype(F32)).astype(BF16)
                ici[sv].start()

        @pl.when(s == N_SHARD)
        def _():
            dh[rows, :] = _dot_nt(dp_ref[...], w_ref[0])

        @pl.when((s > N_SHARD) & (s < norm_step))
        def _():
            dh[rows, :] += _dot_nt(dp_ref[...], w_ref[0])

        @at_norm_block(0)
        def _():
            d2d[3].wait_recv()
            acc[mine, :] += sibrcv[3].astype(F32)

        @at_norm_block(1)
        def _():
            tot = res_o[mine_o, :]
            for sv in range(3):
                ici_o[sv].wait_recv()
                tot = tot + rcv_o[sv].astype(F32)
            res_o[mine_o, :] = tot
            fin_o.start()

        @at_norm_block(2)
        def _():
            ici[1].wait_recv()
            acc[mine, :] += rcv[0].astype(F32)

        @at_norm_block(SEQ // TX - 2)
        def _():
            ici[2].wait_recv()
            acc[mine, :] += rcv[1].astype(F32)
            fin.start()
            store_w.start()
            fin_o.wait_recv()
            store_o.start()

        @at_norm_block(0)
        def _():
            dng[...] = jnp.zeros_like(dng)

        @pl.when(s >= norm_step)
        def _():
            blk = (s - norm_step) * nt + t
            dhv = dh[pl.ds(pl.multiple_of(blk * TX, TX), TX), :]
            xv = x_ref[...]
            r = lax.rsqrt(jnp.mean(xv * xv, axis=-1, keepdims=True) + EPS)
            xn = xv * r
            dng[...] += jnp.sum(dhv * xn, axis=0, keepdims=True)
            dxn = dhv * g_ref[...]
            gx_ref[...] = dx2_ref[...] + r * (dxn - xn * jnp.mean(dxn * xn, axis=-1, keepdims=True))

        @at_step(n_steps - 1, nt - 1)
        def _():
            dng_buf[me] = dng[...]
            for cp in dngs:
                cp.start()
            for m in range(1, N_DEV):
                copy(SEM_SMALL + m, sm_buf.at[0], sm_buf.at[0], sibling).wait_recv()
            tot = sm_buf[0]
            for d in range(1, N_DEV):
                tot = tot + sm_buf[d]
            osm_ref[...] = tot
            for m in range(1, N_DEV):
                copy(SEM_NORM + m, dng_buf.at[0], dng_buf.at[0], sibling).wait_recv()
            tot = dng_buf[0]
            for d in range(1, N_DEV):
                tot = tot + dng_buf[d]
            osm_ref[0:1, :] = tot
            fin.wait_recv()
            for cp in d2d + [d2d_o] + via + list(ici.values()) + ici_o + [fin, fin_o] + smalls + dngs:
                cp.wait_send()
            store_o.wait()
            store_w.wait()

    def shard_of(s, kr):
        order = jnp.where(s < N_SHARD, s, jnp.where(s < norm_step, s - N_SHARD, 3))
        return kr[0] ^ (3 - order)

    def h_map(s, t, kr):
        return (jnp.where(s < N_SHARD, t, nt - 1), 0)

    def dp_map(s, t, kr):
        return (jnp.where(s < norm_step, t, nt - 1), shard_of(s, kr))

    def w_map(s, t, kr):
        return (shard_of(jnp.maximum(s, N_SHARD), kr), 0, 0)

    def blk_map(s, t, kr):
        return (jnp.where(s < norm_step, 0, (s - norm_step) * nt + t), 0)

    hbm = pl.BlockSpec(memory_space=pl.ANY)
    grid_spec = pltpu.PrefetchScalarGridSpec(
        num_scalar_prefetch=1, grid=(n_steps, nt),
        in_specs=[pl.BlockSpec((TT, D_MODEL), h_map),
                  pl.BlockSpec((TT, SHARD_COLS), dp_map),
                  pl.BlockSpec((1, D_MODEL, SHARD_COLS), w_map),
                  pl.BlockSpec((N_SHARD, WO_ROWS, D_MODEL), lambda s, t, kr: (0, 0, 0),
                               pipeline_mode=pl.Buffered(1)),
                  pl.BlockSpec((TX, D_MODEL), blk_map),
                  pl.BlockSpec((TX, D_MODEL), blk_map),
                  pl.BlockSpec((1, D_MODEL), lambda s, t, kr: (0, 0)),
                  pl.BlockSpec((8, D_MODEL), lambda s, t, kr: (0, 0)),
                  pl.BlockSpec((8, D_MODEL), lambda s, t, kr: (0, 0))],
        out_specs=(pl.BlockSpec((TX, D_MODEL), blk_map), hbm, hbm,
                   pl.BlockSpec((8, D_MODEL), lambda s, t, kr: (0, 0))),
        scratch_shapes=[pltpu.VMEM((D_MODEL, SHARD_COLS), F32), pltpu.VMEM((SEQ, D_MODEL), F32),
                        pltpu.VMEM((N_SHARD, hw, SHARD_COLS), BF16), pltpu.VMEM((3, hw, SHARD_COLS), BF16),
                        pltpu.VMEM((N_SHARD, hw, SHARD_COLS), BF16), pltpu.VMEM((2, hw, SHARD_COLS), BF16),
                        pltpu.VMEM((2, hw // 2, SHARD_COLS), BF16),
                        pltpu.VMEM((N_SHARD, ho, D_MODEL), BF16), pltpu.VMEM((N_SHARD, ho, D_MODEL), BF16),
                        pltpu.VMEM((3, ho, D_MODEL), BF16), pltpu.VMEM((WO_ROWS, D_MODEL), F32),
                        pltpu.VMEM((N_DEV, 8, D_MODEL), F32), pltpu.VMEM((N_DEV, 1, D_MODEL), F32),
                        pltpu.VMEM((1, D_MODEL), F32),
                        pltpu.SemaphoreType.DMA((N_SEM_TAIL,)), pltpu.SemaphoreType.DMA((N_SEM_TAIL,)),
                        pltpu.SemaphoreType.DMA((2,))])
    return pl.pallas_call(
        body, name="bwd_tail", grid_spec=grid_spec,
        out_shape=(jax.ShapeDtypeStruct((SEQ, D_MODEL), F32),
                   jax.ShapeDtypeStruct((D_MODEL, SHARD_COLS), F32),
                   jax.ShapeDtypeStruct((WO_ROWS, D_MODEL), F32),
                   jax.ShapeDtypeStruct((8, D_MODEL), F32)),
        compiler_params=pltpu.CompilerParams(dimension_semantics=("arbitrary", "arbitrary"),
                                             vmem_limit_bytes=61 * 1024 * 1024, collective_id=COLLECTIVE_TAIL),
    )(kidx, h, dproj, wg, gwo, x2d, dx2, g1, small_a, small_b)


def _adam_update(w, g, m, v):
    nm = ADAM_B1 * m + (1.0 - ADAM_B1) * g
    nv = ADAM_B2 * v + (1.0 - ADAM_B2) * (g * g)
    m_hat = nm / (1.0 - ADAM_B1 ** ADAM_STEP)
    v_hat = nv / (1.0 - ADAM_B2 ** ADAM_STEP)
    return -ADAM_LR * (m_hat / (jnp.sqrt(v_hat) + ADAM_EPS) + ADAM_WD * w), nm, nv


def _adamw_all(tot, g_w_in, g_w_out, big, small, grad_x):
    n = len(small)
    rows = WO_ROWS
    steps = D_MODEL // rows

    def body(tot_ref, *refs):
        gx_ref, gx_out = refs[2 + 3 * (2 + n)], refs[-1]
        gx_out[...] = gx_ref[...]
        ins, outs = refs[:2 + 3 * (2 + n)], refs[3 + 3 * (2 + n):-1]
        g_refs, wmv = ins[:2], ins[2:]
        loss_ref, quads = outs[0], outs[1:]

        def update(j, g):
            w_ref, m_ref, v_ref = wmv[3 * j:3 * j + 3]
            g_ref, d_ref, nm_ref, nv_ref = quads[4 * j:4 * j + 4]
            g_ref[...] = g
            d_ref[...], nm_ref[...], nv_ref[...] = _adam_update(w_ref[...], g, m_ref[...], v_ref[...])

        update(0, g_refs[0][...])

        @pl.when(pl.program_id(0) == 0)
        def _():
            update(1, g_refs[1][...])
            k = 2 * lax.axis_index("x") + lax.axis_index("y")
            mine = pl.ds(pl.multiple_of(k * HEAD, HEAD), HEAD)
            loss_ref[...] = tot_ref[7:8, 0:1]
            grads = [tot_ref[0:1, :], tot_ref[1:2, :], tot_ref[2:3, 0:D_HGRN], tot_ref[2:3, D_HGRN:],
                     jnp.concatenate([tot_ref[3:4, 0:D_HGRN], tot_ref[3:4, D_HGRN:]], axis=0),
                     jnp.concatenate([tot_ref[4 + tap:5 + tap, mine] for tap in range(3)], axis=1)]
            for j, g in enumerate(grads):
                update(2 + j, g)

    whole = lambda a: pl.BlockSpec(a.shape, lambda i: (0, 0))
    blk = pl.BlockSpec((rows, SHARD_COLS), lambda i: (i, 0))
    arrays = [a for triple in big + small for a in triple]
    in_specs = ([whole(tot), blk, whole(g_w_out)] + [blk] * 3 + [whole(a) for a in arrays[3:]])
    shapes = [big[0][0], big[1][0]] + [w for w, _, _ in small]
    out_shape = (jax.ShapeDtypeStruct((1, 1), F32),) + tuple(
        jax.ShapeDtypeStruct(w.shape, F32) for w in shapes for _ in range(4))
    out_specs = (pl.BlockSpec((1, 1), lambda i: (0, 0)),) + (blk,) * 4 + tuple(
        whole(w) for w in shapes[1:] for _ in range(4))
    gx_blk = pl.BlockSpec((SEQ // steps, D_MODEL), lambda i: (i, 0))
    outs = pl.pallas_call(
        body, name="adamw_all", grid=(steps,),
        out_shape=out_shape + (jax.ShapeDtypeStruct(grad_x.shape, F32),),
        in_specs=in_specs + [gx_blk], out_specs=out_specs + (gx_blk,),
        compiler_params=pltpu.CompilerParams(dimension_semantics=("arbitrary",), vmem_limit_bytes=VMEM_LIMIT),
    )(tot, g_w_in, g_w_out, *arrays, grad_x)
    return [outs[0]] + [outs[1 + 4 * j:5 + 4 * j] for j in range(2 + n)] + [outs[-1]]


def _local_step(x2d, tgt, proj, lb_logits, cw, ga, gcn, w_out, gf):
    g64 = _group_matrix(HEAD, CONV_GROUP)
    aux, states, dx2, dmixed, gwo, part_out = _mix_out(proj, lb_logits, cw, ga, gcn, g64, w_out, x2d, gf, tgt)
    dproj, part_mix = _mix_bwd(proj, aux, states, dmixed, lb_logits, cw, ga, gcn, g64)
    return dproj, dx2, gwo.reshape(N_SHARD, WO_ROWS, D_MODEL), part_out, part_mix


def kernel(x, norm_gain, w_in, lb_logits, conv_w, hgrn_norm_gain, conv_norm_gain, w_out, final_norm_gain, loss_target, m_norm_gain, m_w_in, m_lb_logits, m_conv_w, m_hgrn_norm_gain, m_conv_norm_gain, m_w_out, m_final_norm_gain, v_norm_gain, v_w_in, v_lb_logits, v_conv_w, v_hgrn_norm_gain, v_conv_norm_gain, v_w_out, v_final_norm_gain):
    k = 2 * lax.axis_index("x") + lax.axis_index("y")
    kidx = jnp.reshape(k, (1,)).astype(jnp.int32)
    row = lambda a: a.reshape(1, D_MODEL)
    taps = lambda a: a.reshape(1, 3 * HEAD)
    h, proj, wg, cw = _gather_proj(kidx, x[0], norm_gain, w_in, taps(conv_w))
    dproj, dx2, gwo, part_out, part_mix = _local_step(
        x[0], loss_target[0], proj, lb_logits, cw, hgrn_norm_gain, conv_norm_gain, w_out, row(final_norm_gain))
    rgrad_x, rg_w_in, rg_w_out, tot = _bwd_tail(kidx, h, dproj, wg, gwo, x[0], dx2, norm_gain, part_out, part_mix)

    (loss, (g_w_in, d_w_in, nm_w_in, nv_w_in), (g_w_out, d_w_out, nm_w_out, nv_w_out),
     (g_norm_gain, d_ng, nm_ng, nv_ng), (g_final, d_fg, nm_fg, nv_fg), (g_hgrn, d_hg, nm_hg, nv_hg),
     (g_convn, d_cg, nm_cg, nv_cg), (g_lb, d_lb, nm_lb, nv_lb), (g_conv_w, d_cw, nm_cw, nv_cw),
     grad_x) = _adamw_all(
        tot, rg_w_in, rg_w_out,
        [(w_in[0], m_w_in[0], v_w_in[0]), (w_out[0], m_w_out[0], v_w_out[0])],
        [(norm_gain, m_norm_gain, v_norm_gain),
         (row(final_norm_gain), row(m_final_norm_gain), row(v_final_norm_gain)),
         (hgrn_norm_gain, m_hgrn_norm_gain, v_hgrn_norm_gain),
         (conv_norm_gain, m_conv_norm_gain, v_conv_norm_gain),
         (lb_logits, m_lb_logits, v_lb_logits),
         (taps(conv_w), taps(m_conv_w), taps(v_conv_w))],
        rgrad_x)
    flat = lambda a: a.reshape(D_MODEL)
    untap = lambda a: a.reshape(1, 3, HEAD)
    return (loss.reshape(()), grad_x[None],
            g_norm_gain, g_w_in[None], g_lb, untap(g_conv_w), g_hgrn, g_convn, g_w_out[None], flat(g_final),
            d_ng, d_w_in[None], d_lb, untap(d_cw), d_hg, d_cg, d_w_out[None], flat(d_fg),
            nm_ng, nm_w_in[None], nm_lb, untap(nm_cw), nm_hg, nm_cg, nm_w_out[None], flat(nm_fg),
            nv_ng, nv_w_in[None], nv_lb, untap(nv_cw), nv_hg, nv_cg, nv_w_out[None], flat(nv_fg))
```

```python
import jax
import jax.numpy as jnp
import numpy as np
from jax import lax
from jax.experimental import pallas as pl
from jax.experimental.pallas import tpu as pltpu

F32 = jnp.float32
BF16 = jnp.bfloat16
MESH = pl.DeviceIdType.MESH

SEQ = 2048
D_MODEL = 1024
D_HGRN = 512
D_CONV = 512
HEAD = 128
N_HEADS = 4
CHUNK = 64
CONV_GROUP = 64
N_SHARD = 4
SHARD_COLS = 1024
WO_ROWS = 256
EPS = 1e-6
TB = 256
NCB = TB // CHUNK
N_CHUNKS = SEQ // CHUNK
N_DEV = 8
COLLECTIVE_GATHER, COLLECTIVE_MIX_OUT, COLLECTIVE_TAIL = 1, 0, 2
AUX_O, AUX_CV, AUX_B, AUX_COLS = 0, 512, 1024, 1536

ADAM_LR = 0.001
ADAM_B1 = 0.9
ADAM_B2 = 0.999
ADAM_EPS = 1e-08
ADAM_WD = 0.01
ADAM_STEP = 10

VMEM_LIMIT = 56 * 1024 * 1024


def _dot(a, b):
    return jnp.dot(a, b, preferred_element_type=F32)


def _dot_nt(a, b):
    return lax.dot_general(a, b, (((1,), (1,)), ((), ())), preferred_element_type=F32)


def _dot_tn(a, b):
    return lax.dot_general(a, b, (((0,), (0,)), ((), ())), preferred_element_type=F32)


def _split_bf16(x, n):
    parts = []
    r = x
    for _ in range(n):
        p = r.astype(BF16)
        parts.append(p)
        r = r - p.astype(F32)
    return parts


def _exact_left(m, x, n=3):
    acc = None
    for p in _split_bf16(x, n):
        t = _dot(m, p)
        acc = t if acc is None else acc + t
    return acc


def _exact_left_many(m, xs, n=3):
    parts = [_split_bf16(x, n) for x in xs]
    accs = [None] * len(xs)
    for i in range(n):
        for j in range(len(xs)):
            t = _dot(m, parts[j][i])
            accs[j] = t if accs[j] is None else accs[j] + t
    return accs


def _group_mean_many(xs, gmat, n=2):
    parts = [_split_bf16(x, n) for x in xs]
    accs = [None] * len(xs)
    for i in range(n):
        for j in range(len(xs)):
            t = _dot(parts[j][i], gmat)
            accs[j] = t if accs[j] is None else accs[j] + t
    return accs


def _group_mean(x, gmat, n=2):
    w = gmat.shape[0]
    outs = []
    for c0 in range(0, x.shape[1], w):
        acc = None
        for p in _split_bf16(x[:, c0:c0 + w], n):
            t = _dot(p, gmat)
            acc = t if acc is None else acc + t
        outs.append(acc)
    return jnp.concatenate(outs, axis=1)


def _sigmoid(x):
    return 1.0 / (1.0 + jnp.exp(-x))


def _lower_bound(lbl):
    l0 = lbl[0:1, :]
    l1 = lbl[1:2, :]
    m = jnp.maximum(l0, l1)
    e0 = jnp.exp(l0 - m)
    e1 = jnp.exp(l1 - m)
    return e0 / (e0 + e1)


def _tri(lower):
    r = lax.broadcasted_iota(jnp.int32, (CHUNK, CHUNK), 0)
    c = lax.broadcasted_iota(jnp.int32, (CHUNK, CHUNK), 1)
    return jnp.where((c <= r) if lower else (c >= r), 1.0, 0.0).astype(BF16)


def _causal():
    r = lax.broadcasted_iota(jnp.int32, (CHUNK, CHUNK), 0)
    c = lax.broadcasted_iota(jnp.int32, (CHUNK, CHUNK), 1)
    return c <= r


def _shift_down(x, sh, prev_tail):
    r = pltpu.roll(x, sh, 0)
    pt = pltpu.roll(prev_tail, sh, 0)
    rows = lax.broadcasted_iota(jnp.int32, prev_tail.shape, 0)
    top = jnp.where(rows < sh, pt, r[0:8])
    return jnp.concatenate([top, r[8:]], axis=0)


def _shift_up(x, sh, next_head):
    n = x.shape[0]
    r = pltpu.roll(x, n - sh, 0)
    nh = pltpu.roll(next_head, 8 - sh, 0)
    rows = lax.broadcasted_iota(jnp.int32, next_head.shape, 0)
    bot = jnp.where(rows >= 8 - sh, nh, r[n - 8:])
    return jnp.concatenate([r[:n - 8], bot], axis=0)


def _group_matrix(width, group):
    r = np.arange(width)[:, None] // group
    c = np.arange(width)[None, :] // group
    return jnp.asarray(np.where(r == c, 1.0 / group, 0.0), dtype=BF16)


TG = 1024
SEM_W, SEM_CW, SEM_W_FWD, N_SEM = 0, 4, 7, 11


def _gather_proj(kidx, x2d, g1, w_in, conv_w):
    half_w = D_MODEL // 2
    half_c = SHARD_COLS // 2
    nt = SEQ // TG
    n_steps = 2 * N_SHARD

    def body(k_ref, x_ref, g_ref, w_ref, cw_ref, h_out, p_ref, wg_out, cwg_out,
             wg_v, cwg_v, h_ref, send_sems, recv_sems, out_sems):
        s, t = pl.program_id(0), pl.program_id(1)
        x, y, c = lax.axis_index("x"), lax.axis_index("y"), lax.axis_index("c")
        k = 2 * x + y
        sibling = (x, y, 1 - c)
        chips = [(1 - x, y), (x, 1 - y), (1 - x, 1 - y)]
        kjs = [2 * cx + cy for cx, cy in chips]
        diag = (*chips[2], c)

        def w_half(kk, cc):
            return wg_v.at[kk, pl.ds(cc * half_w, half_w), :]

        def w_quarter(kk, cc, piece):
            return wg_v.at[kk, pl.ds(cc * half_w, half_w), piece * half_c:(piece + 1) * half_c]

        def cw_of(kk):
            return cwg_v.at[:, pl.ds(pl.multiple_of(kk * HEAD, HEAD), HEAD)]

        def copy(sem, ref, to):
            return pltpu.make_async_remote_copy(
                src_ref=ref, dst_ref=ref, send_sem=send_sems.at[sem], recv_sem=recv_sems.at[sem],
                device_id=to, device_id_type=MESH)

        def at_step(sv, tv):
            return pl.when((s == sv) & (t == tv))

        w_direct = ([copy(SEM_W + j, w_half(k, c), (*chips[j], c)) for j in range(2)]
                    + [copy(SEM_W + 2 + p, w_quarter(k, c, p), diag) for p in range(2)])
        cw_direct = [copy(SEM_CW + j, cw_of(k), (*chip, c)) for j, chip in enumerate(chips)]
        w_passed = ([copy(SEM_W_FWD + j, w_half(kjs[j], c), sibling) for j in range(2)]
                    + [copy(SEM_W_FWD + 2 + p, w_quarter(kjs[2], c, p), sibling) for p in range(2)])
        stores = ([pltpu.make_async_copy(wg_v.at[kk], wg_out.at[kk], out_sems.at[i])
                   for i, kk in enumerate([k] + kjs)]
                  + [pltpu.make_async_copy(cwg_v, cwg_out, out_sems.at[4]),
                     pltpu.make_async_copy(h_ref, h_out, out_sems.at[5])])

        @at_step(0, 0)
        def _():
            barrier = pltpu.get_barrier_semaphore()
            for peer in [sibling] + [(*chip, c) for chip in chips]:
                pl.semaphore_signal(barrier, inc=1, device_id=peer, device_id_type=MESH)
            wg_v[k] = w_ref[0].astype(BF16)
            mine = pl.ds(pl.multiple_of(k * HEAD, HEAD), HEAD)
            cwg_v[:, mine] = jnp.zeros((8, HEAD), F32)
            for tap in range(3):
                cwg_v[tap:tap + 1, mine] = cw_ref[:, tap * HEAD:(tap + 1) * HEAD]
            pl.semaphore_wait(barrier, 4)
            for cp in w_direct + cw_direct:
                cp.start()
            stores[0].start()

        @at_step(1, 0)
        def _():
            stores[5].start()

        @at_step(2, 0)
        def _():
            for j in range(2):
                copy(SEM_W + j, w_half(kjs[j], c), sibling).wait_recv()
                w_passed[j].start()
            copy(SEM_W_FWD, w_half(kjs[0], 1 - c), sibling).wait_recv()
            stores[1].start()

        @at_step(4, 0)
        def _():
            copy(SEM_W_FWD + 1, w_half(kjs[1], 1 - c), sibling).wait_recv()
            stores[2].start()

        for p in range(2):
            @at_step(6 + p, 0)
            def _(p=p):
                copy(SEM_W + 2 + p, w_quarter(kjs[2], c, p), sibling).wait_recv()
                w_passed[2 + p].start()
                copy(SEM_W_FWD + 2 + p, w_quarter(kjs[2], 1 - c, p), sibling).wait_recv()
                if p == 1:
                    stores[3].start()

        rows = pl.ds(pl.multiple_of(t * TG, TG), TG)

        @pl.when(s == 0)
        def _():
            xv = x_ref[...]
            r = lax.rsqrt(jnp.mean(xv * xv, axis=-1, keepdims=True) + EPS)
            h_ref[rows, :] = (xv * r * g_ref[...]).astype(BF16)

        sh = s >> 1
        js = k ^ (((sh & 1) << 1) | (sh >> 1))
        for piece in range(2):
            @pl.when((s & 1) == piece)
            def _(piece=piece):
                p_ref[...] = _dot(h_ref[rows, :], wg_v[js, :, piece * half_c:(piece + 1) * half_c])

        @at_step(n_steps - 1, nt - 1)
        def _():
            for j in range(3):
                copy(SEM_CW + j, cw_of(kjs[j]), sibling).wait_recv()
            stores[4].start()
            for cp in w_direct + cw_direct + w_passed:
                cp.wait_send()
            for st in stores:
                st.wait()

    def x_map(s, t, kr):
        return (jnp.where(s == 0, t, nt - 1), 0)

    def p_map(s, t, kr):
        sh = s >> 1
        return (t, 2 * (kr[0] ^ (((sh & 1) << 1) | (sh >> 1))) + (s & 1))

    hbm = pl.BlockSpec(memory_space=pl.ANY)
    grid_spec = pltpu.PrefetchScalarGridSpec(
        num_scalar_prefetch=1, grid=(n_steps, nt),
        in_specs=[pl.BlockSpec((TG, D_MODEL), x_map),
                  pl.BlockSpec((1, D_MODEL), lambda s, t, kr: (0, 0)),
                  pl.BlockSpec((1, D_MODEL, SHARD_COLS), lambda s, t, kr: (0, 0, 0)),
                  pl.BlockSpec((1, 3 * HEAD), lambda s, t, kr: (0, 0))],
        out_specs=(hbm, pl.BlockSpec((TG, half_c), p_map), hbm, hbm),
        scratch_shapes=[pltpu.VMEM((N_SHARD, D_MODEL, SHARD_COLS), BF16),
                        pltpu.VMEM((8, D_CONV), F32), pltpu.VMEM((SEQ, D_MODEL), BF16),
                        pltpu.SemaphoreType.DMA((N_SEM,)), pltpu.SemaphoreType.DMA((N_SEM,)),
                        pltpu.SemaphoreType.DMA((6,))])
    return pl.pallas_call(
        body, name="gather_proj", grid_spec=grid_spec,
        out_shape=(jax.ShapeDtypeStruct((SEQ, D_MODEL), BF16),
                   jax.ShapeDtypeStruct((SEQ, N_SHARD * SHARD_COLS), F32),
                   jax.ShapeDtypeStruct((N_SHARD, D_MODEL, SHARD_COLS), BF16),
                   jax.ShapeDtypeStruct((8, D_CONV), F32)),
        compiler_params=pltpu.CompilerParams(dimension_semantics=("arbitrary", "arbitrary"),
                                             vmem_limit_bytes=VMEM_LIMIT, collective_id=COLLECTIVE_GATHER),
    )(kidx, x2d, g1, w_in, conv_w)


LAG = 6


def _mix_out(proj, lb_logits, cw, ga, gcn, g64, w_out, x2d, gf, tgt):
    half_o = WO_ROWS // 2
    nblk = SEQ // TB
    n_steps = nblk + LAG

    def body(p_ref, lbl_ref, cw_ref, ga_ref, gcn_ref, g64_ref, wo_ref, x_ref, gf_ref, t_ref,
             aux_ref, sto_ref, dx2_ref, dm_ref, gwo_ref, part_ref,
             st_ref, tail_ref, wog_v, stage, ring, acc_ref, send_sems, recv_sems):
        i = pl.program_id(0)
        x, y, c = lax.axis_index("x"), lax.axis_index("y"), lax.axis_index("c")
        k = 2 * x + y
        sibling = (x, y, 1 - c)
        chips = [(1 - x, y), (x, 1 - y), (1 - x, 1 - y)]
        kjs = [2 * cx + cy for cx, cy in chips]

        def wo_half(kk, cc):
            return wog_v.at[pl.ds(pl.multiple_of(kk * WO_ROWS + cc * half_o, half_o), half_o), :]

        def copy(sem, ref, to):
            return pltpu.make_async_remote_copy(
                src_ref=ref, dst_ref=ref, send_sem=send_sems.at[sem], recv_sem=recv_sems.at[sem],
                device_id=to, device_id_type=MESH)

        wo_direct = [copy(j, wo_half(k, c), (*chip, c)) for j, chip in enumerate(chips)]
        wo_passed = [copy(3 + j, wo_half(kj, c), sibling) for j, kj in enumerate(kjs)]

        @pl.when(i == 0)
        def _():
            barrier = pltpu.get_barrier_semaphore()
            for peer in [sibling] + [(*chip, c) for chip in chips]:
                pl.semaphore_signal(barrier, inc=1, device_id=peer, device_id_type=MESH)
            st_ref[...] = jnp.zeros_like(st_ref)
            tail_ref[...] = jnp.zeros_like(tail_ref)
            acc_ref[...] = jnp.zeros_like(acc_ref)
            part_ref[...] = jnp.zeros_like(part_ref)
            wog_v[pl.ds(pl.multiple_of(k * WO_ROWS, WO_ROWS), WO_ROWS), :] = wo_ref[0].astype(BF16)
            pl.semaphore_wait(barrier, 4)
            for cp in wo_direct:
                cp.start()

        @pl.when(i == LAG - 1)
        def _():
            for j in range(3):
                copy(j, wo_half(kjs[j], c), sibling).wait_recv()
                wo_passed[j].start()

        @pl.when(i == LAG)
        def _():
            for j in range(3):
                copy(3 + j, wo_half(kjs[j], 1 - c), sibling).wait_recv()

        lb = _lower_bound(lbl_ref[...])
        tri = _tri(True)
        causal = _causal()
        g64m = g64_ref[...]
        heads = range(N_HEADS)
        cs = [slice(hd * HEAD, (hd + 1) * HEAD) for hd in heads]
        col = lambda base, hd: slice(base + hd * HEAD, base + (hd + 1) * HEAD)

        def mix_chunk(n):
            sl = pl.ds(n * CHUNK, CHUNK)
            sg = [_sigmoid(p_ref[sl, col(512, hd)]) for hd in heads]
            f = [lb[:, cs[hd]] + (1.0 - lb[:, cs[hd]]) * sg[hd] for hd in heads]
            bc = _exact_left_many(tri, [jnp.log(f[hd]) for hd in heads])
            for hd in heads:
                aux_ref[sl, col(AUX_B, hd)] = bc[hd]
            g = [bc[hd][CHUNK - 1:CHUNK, :] for hd in heads]
            qd = [(p_ref[sl, col(0, hd)] * jnp.exp(bc[hd])).astype(BF16) for hd in heads]
            kk = [1.0 - f[hd] for hd in heads]
            ki = [(kk[hd] * jnp.exp(-bc[hd])).astype(BF16) for hd in heads]
            ke = [(kk[hd] * jnp.exp(g[hd] - bc[hd])).astype(BF16) for hd in heads]
            vb = [p_ref[sl, col(1024, hd)].astype(BF16) for hd in heads]
            st = [st_ref[hd] for hd in heads]
            st_b = [a.astype(BF16) for a in st]
            for hd in heads:
                sto_ref[n, hd] = st_b[hd]
            scm = [_dot_nt(qd[hd], ki[hd]) for hd in heads]
            inter = [_dot_nt(qd[hd], st_b[hd]) for hd in heads]
            upd = [_dot_tn(vb[hd], ke[hd]) for hd in heads]
            intra = [_dot(jnp.where(causal, scm[hd], 0.0).astype(BF16), vb[hd]) for hd in heads]
            for hd in heads:
                st_ref[hd] = st[hd] * jnp.exp(g[hd]) + upd[hd]
                o = intra[hd] + inter[hd]
                aux_ref[sl, col(AUX_O, hd)] = o
                ra = lax.rsqrt(jnp.mean(o * o, axis=-1, keepdims=True) + EPS)
                za = p_ref[sl, col(1536, hd)]
                stage[sl, cs[hd]] = (o * ra * ga_ref[:, cs[hd]] * (za * _sigmoid(za))).astype(BF16)
            yb = []
            for hd in heads:
                cu = p_ref[sl, col(3072, hd)] * p_ref[sl, col(2048, hd)]
                tail = tail_ref[:, cs[hd]]
                cv = (cw_ref[0:1, cs[hd]] * _shift_down(cu, 2, tail) + cw_ref[1:2, cs[hd]] * _shift_down(cu, 1, tail)
                      + cw_ref[2:3, cs[hd]] * cu)
                tail_ref[:, cs[hd]] = cu[CHUNK - 8:, :]
                aux_ref[sl, col(AUX_CV, hd)] = cv
                yb.append(p_ref[sl, col(2560, hd)] * cv)
            ms = _group_mean_many([y * y for y in yb], g64m)
            for hd in heads:
                rb = lax.rsqrt(ms[hd] + EPS)
                zb = p_ref[sl, col(3584, hd)]
                stage[sl, col(512, hd)] = (yb[hd] * rb * gcn_ref[:, cs[hd]] * (zb * _sigmoid(zb))).astype(BF16)

        def step(mix, project):
            if project:
                mixed_b = ring[pl.ds(pl.multiple_of((i - LAG) * TB, TB), TB), :]
                y = _dot(mixed_b, wog_v[...])
            if mix:
                mix_chunk(0)
            if project:
                x2 = x_ref[...] + y
                r2 = lax.rsqrt(jnp.mean(x2 * x2, axis=-1, keepdims=True) + EPS)
                n2 = x2 * r2
                gfv = gf_ref[...]
                err = n2 * gfv - t_ref[...]
                loss = 0.5 * jnp.sum(jnp.mean(err * err, axis=-1, keepdims=True), axis=0, keepdims=True)
                dy = err * (1.0 / D_MODEL)
                part_ref[1:2, :] += jnp.sum(dy * n2, axis=0, keepdims=True)
                part_ref[7:8, :] += jnp.broadcast_to(loss, (1, D_MODEL))
                dn = dy * gfv
                dx2 = r2 * (dn - n2 * jnp.mean(dn * n2, axis=-1, keepdims=True))
                dx2_ref[...] = dx2
                dx2_b = dx2.astype(BF16)
            if mix:
                mix_chunk(1)
            if project:
                dm_ref[...] = _dot_nt(dx2_b, wog_v[...])
            if mix:
                mix_chunk(2)
            if project:
                acc_ref[...] += _dot_tn(mixed_b, dx2_b)
            if mix:
                mix_chunk(3)
                ring[pl.ds(pl.multiple_of(i * TB, TB), TB), :] = stage[...]

        @pl.when(i < LAG)
        def _():
            step(True, False)

        @pl.when((i >= LAG) & (i < nblk))
        def _():
            step(True, True)

        @pl.when(i >= nblk)
        def _():
            step(False, True)

        @pl.when(i == n_steps - 1)
        def _():
            gwo_ref[...] = acc_ref[...].astype(BF16)
            for cp in wo_direct + wo_passed:
                cp.wait_send()

    assert NCB == 4
    row = lambda w: pl.BlockSpec((1, w), lambda i: (0, 0))
    mix_blk = lambda i: jnp.minimum(i, nblk - 1)
    out_blk = lambda i: jnp.clip(i - LAG, 0, nblk - 1)
    tok = lambda: pl.BlockSpec((TB, D_MODEL), lambda i: (out_blk(i), 0))
    return pl.pallas_call(
        body, name="mix_out", grid=(n_steps,),
        out_shape=(jax.ShapeDtypeStruct((SEQ, AUX_COLS), F32),
                   jax.ShapeDtypeStruct((N_CHUNKS, N_HEADS, HEAD, HEAD), BF16),
                   jax.ShapeDtypeStruct((SEQ, D_MODEL), F32),
                   jax.ShapeDtypeStruct((SEQ, D_MODEL), F32),
                   jax.ShapeDtypeStruct((D_MODEL, D_MODEL), BF16),
                   jax.ShapeDtypeStruct((8, D_MODEL), F32)),
        in_specs=[pl.BlockSpec((TB, 4096), lambda i: (jnp.minimum(i, nblk - 1), 0)),
                  pl.BlockSpec((2, D_HGRN), lambda i: (0, 0)),
                  pl.BlockSpec((8, D_CONV), lambda i: (0, 0)),
                  row(D_HGRN), row(D_CONV),
                  pl.BlockSpec((HEAD, HEAD), lambda i: (0, 0)),
                  pl.BlockSpec((1, WO_ROWS, D_MODEL), lambda i: (0, 0, 0)),
                  tok(), row(D_MODEL), tok()],
        out_specs=(pl.BlockSpec((TB, AUX_COLS), lambda i: (mix_blk(i), 0)),
                   pl.BlockSpec((NCB, N_HEADS, HEAD, HEAD), lambda i: (mix_blk(i), 0, 0, 0)),
                   tok(), tok(),
                   pl.BlockSpec((D_MODEL, D_MODEL), lambda i: (0, 0)),
                   pl.BlockSpec((8, D_MODEL), lambda i: (0, 0))),
        scratch_shapes=[pltpu.VMEM((N_HEADS, HEAD, HEAD), F32), pltpu.VMEM((8, D_CONV), F32),
                        pltpu.VMEM((D_MODEL, D_MODEL), BF16), pltpu.VMEM((TB, D_MODEL), BF16),
                        pltpu.VMEM((SEQ, D_MODEL), BF16), pltpu.VMEM((D_MODEL, D_MODEL), F32),
                        pltpu.SemaphoreType.DMA((6,)), pltpu.SemaphoreType.DMA((6,))],
        compiler_params=pltpu.CompilerParams(dimension_semantics=("arbitrary",), vmem_limit_bytes=VMEM_LIMIT,
                                             collective_id=COLLECTIVE_MIX_OUT),
    )(proj, lb_logits, cw, ga, gcn, g64, w_out, x2d, gf, tgt)


def _mix_bwd(proj, aux, states, dmixed, lb_logits, cw, ga, gcn, g64):
    nblk = SEQ // TB

    def body(p_ref, aux_ref, st_ref, dm_ref, lbl_ref, cw_ref, ga_ref, gcn_ref, g64_ref,
             dp_ref, part_ref, dst_ref, head_ref, dlb_ref):
        i = pl.program_id(0)

        @pl.when(i == 0)
        def _():
            dst_ref[...] = jnp.zeros_like(dst_ref)
            head_ref[...] = jnp.zeros_like(head_ref)
            part_ref[...] = jnp.zeros_like(part_ref)
            dlb_ref[...] = jnp.zeros_like(dlb_ref)

        lb = _lower_bound(lbl_ref[...])
        triu = _tri(False)
        causal = _causal()
        g64m = g64_ref[...]
        rowsum = lambda a: jnp.sum(a, axis=0, keepdims=True)
        heads = range(N_HEADS)
        cs = [slice(hd * HEAD, (hd + 1) * HEAD) for hd in heads]
        col = lambda base, hd: slice(base + hd * HEAD, base + (hd + 1) * HEAD)
        for n in reversed(range(NCB)):
            sl = pl.ds(n * CHUNK, CHUNK)
            cvv = [aux_ref[sl, col(AUX_CV, hd)] for hd in heads]
            gb = [p_ref[sl, col(2560, hd)] for hd in heads]
            yb = [gb[hd] * cvv[hd] for hd in heads]
            ms = _group_mean_many([y * y for y in yb], g64m)
            rb, nb, dnb = [], [], []
            for hd in heads:
                rb.append(lax.rsqrt(ms[hd] + EPS))
                nb.append(yb[hd] * rb[hd])
                zb = p_ref[sl, col(3584, hd)]
                sgb = _sigmoid(zb)
                dmb = dm_ref[sl, col(512, hd)]
                silu = zb * sgb
                dgate = dmb * gcn_ref[:, cs[hd]]
                part_ref[2:3, col(512, hd)] += rowsum(dmb * nb[hd] * silu)
                dp_ref[sl, col(3584, hd)] = (dgate * nb[hd] * (sgb + silu * (1.0 - sgb))).astype(BF16)
                dnb.append(dgate * silu)
            mdn = _group_mean_many([dnb[hd] * nb[hd] for hd in heads], g64m)
            for hd in heads:
                dyb = rb[hd] * (dnb[hd] - nb[hd] * mdn[hd])
                dp_ref[sl, col(2560, hd)] = (dyb * cvv[hd]).astype(BF16)
                dcv = dyb * gb[hd]
                head = head_ref[:, cs[hd]]
                dcv1 = _shift_up(dcv, 1, head)
                dcv2 = _shift_up(dcv, 2, head)
                head_ref[:, cs[hd]] = dcv[0:8, :]
                u = p_ref[sl, col(2048, hd)]
                gc = p_ref[sl, col(3072, hd)]
                cu = gc * u
                part_ref[4:5, cs[hd]] += rowsum(dcv2 * cu)
                part_ref[5:6, cs[hd]] += rowsum(dcv1 * cu)
                part_ref[6:7, cs[hd]] += rowsum(dcv * cu)
                dcu = cw_ref[2:3, cs[hd]] * dcv + cw_ref[1:2, cs[hd]] * dcv1 + cw_ref[0:1, cs[hd]] * dcv2
                dp_ref[sl, col(3072, hd)] = (dcu * u).astype(BF16)
                dp_ref[sl, col(2048, hd)] = (dcu * gc).astype(BF16)
            do_b = []
            for hd in heads:
                ov = aux_ref[sl, col(AUX_O, hd)]
                ra = lax.rsqrt(jnp.mean(ov * ov, axis=-1, keepdims=True) + EPS)
                na = ov * ra
                za = p_ref[sl, col(1536, hd)]
                sga = _sigmoid(za)
                dma = dm_ref[sl, cs[hd]]
                silu = za * sga
                dgate = dma * ga_ref[:, cs[hd]]
                part_ref[2:3, cs[hd]] += rowsum(dma * na * silu)
                dp_ref[sl, col(1536, hd)] = (dgate * na * (sga + silu * (1.0 - sga))).astype(BF16)
                dna = dgate * silu
                do_b.append((ra * (dna - na * jnp.mean(dna * na, axis=-1, keepdims=True))).astype(BF16))
            s = [_sigmoid(p_ref[sl, col(512, hd)]) for hd in heads]
            f = [lb[:, cs[hd]] + (1.0 - lb[:, cs[hd]]) * s[hd] for hd in heads]
            bc = [aux_ref[sl, col(AUX_B, hd)] for hd in heads]
            g = [bc[hd][CHUNK - 1:CHUNK, :] for hd in heads]
            eb = [jnp.exp(bc[hd]) for hd in heads]
            enb = [jnp.exp(-bc[hd]) for hd in heads]
            eg = [jnp.exp(g[hd] - bc[hd]) for hd in heads]
            dec = [jnp.exp(g[hd]) for hd in heads]
            qd = [p_ref[sl, cs[hd]] * eb[hd] for hd in heads]
            kk = [1.0 - f[hd] for hd in heads]
            ki = [kk[hd] * enb[hd] for hd in heads]
            ke = [kk[hd] * eg[hd] for hd in heads]
            qd_b = [a.astype(BF16) for a in qd]
            ki_b = [a.astype(BF16) for a in ki]
            ke_b = [a.astype(BF16) for a in ke]
            vb = [p_ref[sl, col(1024, hd)].astype(BF16) for hd in heads]
            st_b = [st_ref[n, hd] for hd in heads]
            dst = [dst_ref[hd] for hd in heads]
            dst_b = [a.astype(BF16) for a in dst]
            scm = [_dot_nt(qd_b[hd], ki_b[hd]) for hd in heads]
            amm = [_dot_nt(do_b[hd], vb[hd]) for hd in heads]
            dqd2 = [_dot(do_b[hd], st_b[hd]) for hd in heads]
            dke = [_dot(vb[hd], dst_b[hd]) for hd in heads]
            dv2 = [_dot_nt(ke_b[hd], dst_b[hd]) for hd in heads]
            dsu = [_dot_tn(do_b[hd], qd_b[hd]) for hd in heads]
            sc = [jnp.where(causal, scm[hd], 0.0).astype(BF16) for hd in heads]
            am = [jnp.where(causal, amm[hd], 0.0).astype(BF16) for hd in heads]
            dqd1 = [_dot(am[hd], ki_b[hd]) for hd in heads]
            dki = [_dot_tn(am[hd], qd_b[hd]) for hd in heads]
            dv1 = [_dot_tn(sc[hd], do_b[hd]) for hd in heads]
            db, dgv, dkk = [], [], []
            for hd in heads:
                dqd = dqd1[hd] + dqd2[hd]
                ddec = rowsum(dst[hd] * st_b[hd].astype(F32))
                dst_ref[hd] = dst[hd] * dec[hd] + dsu[hd]
                dp_ref[sl, cs[hd]] = (dqd * eb[hd]).astype(BF16)
                dp_ref[sl, col(1024, hd)] = (dv1[hd] + dv2[hd]).astype(BF16)
                dke_eg = dke[hd] * eg[hd]
                dkk.append(dki[hd] * enb[hd] + dke_eg)
                db.append(dqd * qd[hd] - kk[hd] * dkk[hd])
                dgv.append(rowsum(kk[hd] * dke_eg) + ddec * dec[hd])
            rc = _exact_left_many(triu, db, 2)
            for hd in heads:
                df = (rc[hd] + dgv[hd]) / f[hd] - dkk[hd]
                one_s = 1.0 - s[hd]
                dlb_ref[:, cs[hd]] += rowsum(df * one_s)
                dp_ref[sl, col(512, hd)] = (df * (1.0 - lb[:, cs[hd]]) * s[hd] * one_s).astype(BF16)

        @pl.when(i == nblk - 1)
        def _():
            row = dlb_ref[...] * lb * (1.0 - lb)
            part_ref[3:4, 0:D_HGRN] = row
            part_ref[3:4, D_HGRN:] = -row

    rev = lambda w: pl.BlockSpec((TB, w), lambda i: (nblk - 1 - i, 0))
    row = lambda w: pl.BlockSpec((1, w), lambda i: (0, 0))
    return pl.pallas_call(
        body, name="mix_bwd", grid=(nblk,),
        out_shape=(jax.ShapeDtypeStruct((SEQ, 4096), BF16),
                   jax.ShapeDtypeStruct((8, D_MODEL), F32)),
        in_specs=[rev(4096), rev(AUX_COLS),
                  pl.BlockSpec((NCB, N_HEADS, HEAD, HEAD), lambda i: (nblk - 1 - i, 0, 0, 0)),
                  rev(D_MODEL),
                  pl.BlockSpec((2, D_HGRN), lambda i: (0, 0)),
                  pl.BlockSpec((8, D_CONV), lambda i: (0, 0)),
                  row(D_HGRN), row(D_CONV),
                  pl.BlockSpec((HEAD, HEAD), lambda i: (0, 0))],
        out_specs=(rev(4096), pl.BlockSpec((8, D_MODEL), lambda i: (0, 0))),
        scratch_shapes=[pltpu.VMEM((N_HEADS, HEAD, HEAD), F32), pltpu.VMEM((8, D_CONV), F32),
                        pltpu.VMEM((1, D_HGRN), F32)],
        compiler_params=pltpu.CompilerParams(dimension_semantics=("arbitrary",), vmem_limit_bytes=VMEM_LIMIT),
    )(proj, aux, states, dmixed, lb_logits, cw, ga, gcn, g64)


TT = 1024
TX = 512
(SEM_D2D, SEM_D2D_O, SEM_ICI, SEM_ICI_O, SEM_FIN, SEM_FIN_O, SEM_SMALL, SEM_VIA, SEM_NORM, N_SEM_TAIL) = (
    0, 4, 5, 8, 11, 12, 12, 20, 22, 30)


def _bwd_tail(kidx, h, dproj, wg, gwo, x2d, dx2, g1, small_a, small_b):
    hw = D_MODEL // 2
    ho = WO_ROWS // 2
    nt = SEQ // TT
    norm_step = 2 * N_SHARD
    n_steps = norm_step + SEQ // TX // nt

    def body(k_ref, h_ref, dp_ref, w_ref, gwo_ref, x_ref, dx2_ref, g_ref, sm_ref, smb_ref,
             gx_ref, gw_out, gwo_out, osm_ref,
             acc, dh, sendbuf, keep, sibrcv, rcv, merge, sib_o, p_o, rcv_o, res_o, sm_buf, dng_buf, dng,
             send_sems, recv_sems, out_sems):
        s, t = pl.program_id(0), pl.program_id(1)
        x, y, c = lax.axis_index("x"), lax.axis_index("y"), lax.axis_index("c")
        k = 2 * x + y
        me = 4 * x + 2 * y + c
        sibling = (x, y, 1 - c)
        chips = [(1 - x, 1 - y), (1 - x, y), (x, 1 - y)]
        kjs = [2 * cx + cy for cx, cy in chips]
        mine = pl.ds(pl.multiple_of(c * hw, hw), hw)
        other = pl.ds(pl.multiple_of((1 - c) * hw, hw), hw)
        mine_o = pl.ds(pl.multiple_of(c * ho, ho), ho)
        other_o = pl.ds(pl.multiple_of((1 - c) * ho, ho), ho)

        def copy(sem, src, dst, to):
            return pltpu.make_async_remote_copy(
                src_ref=src, dst_ref=dst, send_sem=send_sems.at[sem], recv_sem=recv_sems.at[sem],
                device_id=to, device_id_type=MESH)

        def at_step(sv, tv):
            return pl.when((s == sv) & (t == tv))

        def at_norm_block(b):
            return at_step(norm_step + b // nt, b % nt)

        d2d = [copy(SEM_D2D + sv, sendbuf.at[sv], sibrcv.at[sv], sibling) for sv in range(N_SHARD)]
        d2d_o = copy(SEM_D2D_O, gwo_ref.at[:, other_o, :], sib_o, sibling)
        ici = {sv: copy(SEM_ICI + sv, keep.at[sv], rcv.at[sv - 1], (*chips[sv], c)) for sv in (1, 2)}
        qh = hw // 2
        via = [copy(SEM_VIA, keep.at[0, 0:qh, :], merge.at[1], (*chips[1], c)),
               copy(SEM_VIA + 1, keep.at[0, qh:hw, :], merge.at[0], (*chips[2], c))]
        merged_rows = [slice(qh, hw), slice(0, qh)]
        ici_o = [copy(SEM_ICI_O + sv, p_o.at[kjs[sv]], rcv_o.at[sv], (*chips[sv], c)) for sv in range(3)]
        fin = copy(SEM_FIN, acc.at[mine, :], gw_out.at[mine, :], sibling)
        fin_o = copy(SEM_FIN_O, res_o.at[mine_o, :], res_o.at[mine_o, :], sibling)
        peers = [(x ^ (m >> 2), y ^ ((m >> 1) & 1), c ^ (m & 1)) for m in range(1, N_DEV)]
        smalls = [copy(SEM_SMALL + 1 + j, sm_buf.at[me], sm_buf.at[me], to) for j, to in enumerate(peers)]
        dngs = [copy(SEM_NORM + 1 + j, dng_buf.at[me], dng_buf.at[me], to) for j, to in enumerate(peers)]
        store_w = pltpu.make_async_copy(acc.at[mine, :], gw_out.at[mine, :], out_sems.at[0])
        store_o = pltpu.make_async_copy(res_o, gwo_out, out_sems.at[1])

        @at_step(0, 0)
        def _():
            barrier = pltpu.get_barrier_semaphore()
            for to in peers:
                pl.semaphore_signal(barrier, inc=1, device_id=to, device_id_type=MESH)
            sm_buf[me] = sm_ref[...] + smb_ref[...]
            pl.semaphore_wait(barrier, N_DEV - 1)
            d2d_o.start()
            for cp in smalls:
                cp.start()

        @at_step(0, 1)
        def _():
            d2d_o.wait_recv()
            for j in range(N_SHARD):
                p_o[j] = (gwo_ref[j, mine_o, :].astype(F32) + sib_o[j].astype(F32)).astype(BF16)
            res_o[mine_o, :] = gwo_ref[k, mine_o, :].astype(F32) + sib_o[k].astype(F32)
            for cp in ici_o:
                cp.start()

        rows = pl.ds(pl.multiple_of(t * TT, TT), TT)

        @pl.when((s < N_SHARD) & (t == 0))
        def _():
            acc[...] = _dot_tn(h_ref[...], dp_ref[...])

        @pl.when((s < N_SHARD) & (t > 0))
        def _():
            acc[...] += _dot_tn(h_ref[...], dp_ref[...])

        for sv in range(N_SHARD):
            @at_step(sv, nt - 1)
            def _(sv=sv):
                sendbuf[sv] = acc[other, :].astype(BF16)
                if sv < 3:
                    keep[sv] = acc[mine, :].astype(BF16)
                d2d[sv].start()

        @at_step(1, 0)
        def _():
            d2d[0].wait_recv()
            keep[0] = (keep[0].astype(F32) + sibrcv[0].astype(F32)).astype(BF16)
            for cp in via:
                cp.start()

        for sv in (1, 2):
            @at_step(sv + 2, 0)
            def _(sv=sv):
                d2d[sv].wait_recv()
                keep[sv] = (keep[sv].astype(F32) + sibrcv[sv].astype(F32)).astype(BF16)
                via[2 - sv].wait_recv()
                rows_m = merged_rows[sv - 1]
                keep[sv, rows_m, :] = (keep[sv, rows_m, :].astype(F32) + merge[sv - 1].astype(F32)).astype(BF16)
                ici[sv].start()

        @pl.when(s == N_SHARD)
        def _():
            dh[rows, :] = _dot_nt(dp_ref[...], w_ref[0])

        @pl.when((s > N_SHARD) & (s < norm_step))
        def _():
            dh[rows, :] += _dot_nt(dp_ref[...], w_ref[0])

        @at_norm_block(0)
        def _():
            d2d[3].wait_recv()
            acc[mine, :] += sibrcv[3].astype(F32)

        @at_norm_block(1)
        def _():
            tot = res_o[mine_o, :]
            for sv in range(3):
                ici_o[sv].wait_recv()
                tot = tot + rcv_o[sv].astype(F32)
            res_o[mine_o, :] = tot
            fin_o.start()

        @at_norm_block(2)
        def _():
            ici[1].wait_recv()
            acc[mine, :] += rcv[0].astype(F32)

        @at_norm_block(SEQ // TX - 2)
        def _():
            ici[2].wait_recv()
            acc[mine, :] += rcv[1].astype(F32)
            fin.start()
            store_w.start()
            fin_o.wait_recv()
            store_o.start()

        @at_norm_block(0)
        def _():
            dng[...] = jnp.zeros_like(dng)

        @pl.when(s >= norm_step)
        def _():
            blk = (s - norm_step) * nt + t
            dhv = dh[pl.ds(pl.multiple_of(blk * TX, TX), TX), :]
            xv = x_ref[...]
            r = lax.rsqrt(jnp.mean(xv * xv, axis=-1, keepdims=True) + EPS)
            xn = xv * r
            dng[...] += jnp.sum(dhv * xn, axis=0, keepdims=True)
            dxn = dhv * g_ref[...]
            gx_ref[...] = dx2_ref[...] + r * (dxn - xn * jnp.mean(dxn * xn, axis=-1, keepdims=True))

        @at_step(n_steps - 1, nt - 1)
        def _():
            dng_buf[me] = dng[...]
            for cp in dngs:
                cp.start()
            for m in range(1, N_DEV):
                copy(SEM_SMALL + m, sm_buf.at[0], sm_buf.at[0], sibling).wait_recv()
            tot = sm_buf[0]
            for d in range(1, N_DEV):
                tot = tot + sm_buf[d]
            osm_ref[...] = tot
            for m in range(1, N_DEV):
                copy(SEM_NORM + m, dng_buf.at[0], dng_buf.at[0], sibling).wait_recv()
            tot = dng_buf[0]
            for d in range(1, N_DEV):
                tot = tot + dng_buf[d]
            osm_ref[0:1, :] = tot
            fin.wait_recv()
            for cp in d2d + [d2d_o] + via + list(ici.values()) + ici_o + [fin, fin_o] + smalls + dngs:
                cp.wait_send()
            store_o.wait()
            store_w.wait()

    def shard_of(s, kr):
        order = jnp.where(s < N_SHARD, s, jnp.where(s < norm_step, s - N_SHARD, 3))
        return kr[0] ^ (3 - order)

    def h_map(s, t, kr):
        return (jnp.where(s < N_SHARD, t, nt - 1), 0)

    def dp_map(s, t, kr):
        return (jnp.where(s < norm_step, t, nt - 1), shard_of(s, kr))

    def w_map(s, t, kr):
        return (shard_of(jnp.maximum(s, N_SHARD), kr), 0, 0)

    def blk_map(s, t, kr):
        return (jnp.where(s < norm_step, 0, (s - norm_step) * nt + t), 0)

    hbm = pl.BlockSpec(memory_space=pl.ANY)
    grid_spec = pltpu.PrefetchScalarGridSpec(
        num_scalar_prefetch=1, grid=(n_steps, nt),
        in_specs=[pl.BlockSpec((TT, D_MODEL), h_map),
                  pl.BlockSpec((TT, SHARD_COLS), dp_map),
                  pl.BlockSpec((1, D_MODEL, SHARD_COLS), w_map),
                  pl.BlockSpec((N_SHARD, WO_ROWS, D_MODEL), lambda s, t, kr: (0, 0, 0),
                               pipeline_mode=pl.Buffered(1)),
                  pl.BlockSpec((TX, D_MODEL), blk_map),
                  pl.BlockSpec((TX, D_MODEL), blk_map),
                  pl.BlockSpec((1, D_MODEL), lambda s, t, kr: (0, 0)),
                  pl.BlockSpec((8, D_MODEL), lambda s, t, kr: (0, 0)),
                  pl.BlockSpec((8, D_MODEL), lambda s, t, kr: (0, 0))],
        out_specs=(pl.BlockSpec((TX, D_MODEL), blk_map), hbm, hbm,
                   pl.BlockSpec((8, D_MODEL), lambda s, t, kr: (0, 0))),
        scratch_shapes=[pltpu.VMEM((D_MODEL, SHARD_COLS), F32), pltpu.VMEM((SEQ, D_MODEL), F32),
                        pltpu.VMEM((N_SHARD, hw, SHARD_COLS), BF16), pltpu.VMEM((3, hw, SHARD_COLS), BF16),
                        pltpu.VMEM((N_SHARD, hw, SHARD_COLS), BF16), pltpu.VMEM((2, hw, SHARD_COLS), BF16),
                        pltpu.VMEM((2, hw // 2, SHARD_COLS), BF16),
                        pltpu.VMEM((N_SHARD, ho, D_MODEL), BF16), pltpu.VMEM((N_SHARD, ho, D_MODEL), BF16),
                        pltpu.VMEM((3, ho, D_MODEL), BF16), pltpu.VMEM((WO_ROWS, D_MODEL), F32),
                        pltpu.VMEM((N_DEV, 8, D_MODEL), F32), pltpu.VMEM((N_DEV, 1, D_MODEL), F32),
                        pltpu.VMEM((1, D_MODEL), F32),
                        pltpu.SemaphoreType.DMA((N_SEM_TAIL,)), pltpu.SemaphoreType.DMA((N_SEM_TAIL,)),
                        pltpu.SemaphoreType.DMA((2,))])
    return pl.pallas_call(
        body, name="bwd_tail", grid_spec=grid_spec,
        out_shape=(jax.ShapeDtypeStruct((SEQ, D_MODEL), F32),
                   jax.ShapeDtypeStruct((D_MODEL, SHARD_COLS), F32),
                   jax.ShapeDtypeStruct((WO_ROWS, D_MODEL), F32),
                   jax.ShapeDtypeStruct((8, D_MODEL), F32)),
        compiler_params=pltpu.CompilerParams(dimension_semantics=("arbitrary", "arbitrary"),
                                             vmem_limit_bytes=61 * 1024 * 1024, collective_id=COLLECTIVE_TAIL),
    )(kidx, h, dproj, wg, gwo, x2d, dx2, g1, small_a, small_b)


def _adam_update(w, g, m, v):
    nm = ADAM_B1 * m + (1.0 - ADAM_B1) * g
    nv = ADAM_B2 * v + (1.0 - ADAM_B2) * (g * g)
    m_hat = nm / (1.0 - ADAM_B1 ** ADAM_STEP)
    v_hat = nv / (1.0 - ADAM_B2 ** ADAM_STEP)
    return -ADAM_LR * (m_hat / (jnp.sqrt(v_hat) + ADAM_EPS) + ADAM_WD * w), nm, nv


def _adamw_all(tot, g_w_in, g_w_out, big, small, grad_x):
    n = len(small)
    rows = WO_ROWS
    steps = D_MODEL // rows

    def body(tot_ref, *refs):
        gx_ref, gx_out = refs[2 + 3 * (2 + n)], refs[-1]
        gx_out[...] = gx_ref[...]
        ins, outs = refs[:2 + 3 * (2 + n)], refs[3 + 3 * (2 + n):-1]
        g_refs, wmv = ins[:2], ins[2:]
        loss_ref, quads = outs[0], outs[1:]

        def update(j, g):
            w_ref, m_ref, v_ref = wmv[3 * j:3 * j + 3]
            g_ref, d_ref, nm_ref, nv_ref = quads[4 * j:4 * j + 4]
            g_ref[...] = g
            d_ref[...], nm_ref[...], nv_ref[...] = _adam_update(w_ref[...], g, m_ref[...], v_ref[...])

        update(0, g_refs[0][...])

        @pl.when(pl.program_id(0) == 0)
        def _():
            update(1, g_refs[1][...])
            k = 2 * lax.axis_index("x") + lax.axis_index("y")
            mine = pl.ds(pl.multiple_of(k * HEAD, HEAD), HEAD)
            loss_ref[...] = tot_ref[7:8, 0:1]
            grads = [tot_ref[0:1, :], tot_ref[1:2, :], tot_ref[2:3, 0:D_HGRN], tot_ref[2:3, D_HGRN:],
                     jnp.concatenate([tot_ref[3:4, 0:D_HGRN], tot_ref[3:4, D_HGRN:]], axis=0),
                     jnp.concatenate([tot_ref[4 + tap:5 + tap, mine] for tap in range(3)], axis=1)]
            for j, g in enumerate(grads):
                update(2 + j, g)

    whole = lambda a: pl.BlockSpec(a.shape, lambda i: (0, 0))
    blk = pl.BlockSpec((rows, SHARD_COLS), lambda i: (i, 0))
    arrays = [a for triple in big + small for a in triple]
    in_specs = ([whole(tot), blk, whole(g_w_out)] + [blk] * 3 + [whole(a) for a in arrays[3:]])
    shapes = [big[0][0], big[1][0]] + [w for w, _, _ in small]
    out_shape = (jax.ShapeDtypeStruct((1, 1), F32),) + tuple(
        jax.ShapeDtypeStruct(w.shape, F32) for w in shapes for _ in range(4))
    out_specs = (pl.BlockSpec((1, 1), lambda i: (0, 0)),) + (blk,) * 4 + tuple(
        whole(w) for w in shapes[1:] for _ in range(4))
    gx_blk = pl.BlockSpec((SEQ // steps, D_MODEL), lambda i: (i, 0))
    outs = pl.pallas_call(
        body, name="adamw_all", grid=(steps,),
        out_shape=out_shape + (jax.ShapeDtypeStruct(grad_x.shape, F32),),
        in_specs=in_specs + [gx_blk], out_specs=out_specs + (gx_blk,),
        compiler_params=pltpu.CompilerParams(dimension_semantics=("arbitrary",), vmem_limit_bytes=VMEM_LIMIT),
    )(tot, g_w_in, g_w_out, *arrays, grad_x)
    return [outs[0]] + [outs[1 + 4 * j:5 + 4 * j] for j in range(2 + n)] + [outs[-1]]


def _local_step(x2d, tgt, proj, lb_logits, cw, ga, gcn, w_out, gf):
    g64 = _group_matrix(HEAD, CONV_GROUP)
    aux, states, dx2, dmixed, gwo, part_out = _mix_out(proj, lb_logits, cw, ga, gcn, g64, w_out, x2d, gf, tgt)
    dproj, part_mix = _mix_bwd(proj, aux, states, dmixed, lb_logits, cw, ga, gcn, g64)
    return dproj, dx2, gwo.reshape(N_SHARD, WO_ROWS, D_MODEL), part_out, part_mix


def kernel(x, norm_gain, w_in, lb_logits, conv_w, hgrn_norm_gain, conv_norm_gain, w_out, final_norm_gain, loss_target, m_norm_gain, m_w_in, m_lb_logits, m_conv_w, m_hgrn_norm_gain, m_conv_norm_gain, m_w_out, m_final_norm_gain, v_norm_gain, v_w_in, v_lb_logits, v_conv_w, v_hgrn_norm_gain, v_conv_norm_gain, v_w_out, v_final_norm_gain):
    k = 2 * lax.axis_index("x") + lax.axis_index("y")
    kidx = jnp.reshape(k, (1,)).astype(jnp.int32)
    row = lambda a: a.reshape(1, D_MODEL)
    taps = lambda a: a.reshape(1, 3 * HEAD)
    h, proj, wg, cw = _gather_proj(kidx, x[0], norm_gain, w_in, taps(conv_w))
    dproj, dx2, gwo, part_out, part_mix = _local_step(
        x[0], loss_target[0], proj, lb_logits, cw, hgrn_norm_gain, conv_norm_gain, w_out, row(final_norm_gain))
    rgrad_x, rg_w_in, rg_w_out, tot = _bwd_tail(kidx, h, dproj, wg, gwo, x[0], dx2, norm_gain, part_out, part_mix)

    (loss, (g_w_in, d_w_in, nm_w_in, nv_w_in), (g_w_out, d_w_out, nm_w_out, nv_w_out),
     (g_norm_gain, d_ng, nm_ng, nv_ng), (g_final, d_fg, nm_fg, nv_fg), (g_hgrn, d_hg, nm_hg, nv_hg),
     (g_convn, d_cg, nm_cg, nv_cg), (g_lb, d_lb, nm_lb, nv_lb), (g_conv_w, d_cw, nm_cw, nv_cw),
     grad_x) = _adamw_all(
        tot, rg_w_in, rg_w_out,
        [(w_in[0], m_w_in[0], v_w_in[0]), (w_out[0], m_w_out[0], v_w_out[0])],
        [(norm_gain, m_norm_gain, v_norm_gain),
         (row(final_norm_gain), row(m_final_norm_gain), row(v_final_norm_gain)),
         (hgrn_norm_gain, m_hgrn_norm_gain, v_hgrn_norm_gain),
         (conv_norm_gain, m_conv_norm_gain, v_conv_norm_gain),
         (lb_logits, m_lb_logits, v_lb_logits),
         (taps(conv_w), taps(m_conv_w), taps(v_conv_w))],
        rgrad_x)
    flat = lambda a: a.reshape(D_MODEL)
    untap = lambda a: a.reshape(1, 3, HEAD)
    return (loss.reshape(()), grad_x[None],
            g_norm_gain, g_w_in[None], g_lb, untap(g_conv_w), g_hgrn, g_convn, g_w_out[None], flat(g_final),
            d_ng, d_w_in[None], d_lb, untap(d_cw), d_hg, d_cg, d_w_out[None], flat(d_fg),
            nm_ng, nm_w_in[None], nm_lb, untap(nm_cw), nm_hg, nm_cg, nm_w_out[None], flat(nm_fg),
            nv_ng, nv_w_in[None], nv_lb, untap(nv_cw), nv_hg, nv_cg, nv_w_out[None], flat(nv_fg))
```

```python
import jax
import jax.numpy as jnp
import numpy as np
from jax import lax
from jax.experimental import pallas as pl
from jax.experimental.pallas import tpu as pltpu

F32 = jnp.float32
BF16 = jnp.bfloat16
MESH = pl.DeviceIdType.MESH

SEQ = 2048
D_MODEL = 1024
D_HGRN = 512
D_CONV = 512
HEAD = 128
N_HEADS = 4
CHUNK = 64
CONV_GROUP = 64
N_SHARD = 4
SHARD_COLS = 1024
WO_ROWS = 256
EPS = 1e-6
TB = 256
NCB = TB // CHUNK
N_CHUNKS = SEQ // CHUNK
N_DEV = 8
COLLECTIVE_GATHER, COLLECTIVE_MIX_OUT, COLLECTIVE_TAIL = 1, 0, 2
AUX_O, AUX_CV, AUX_B, AUX_COLS = 0, 512, 1024, 1536

ADAM_LR = 0.001
ADAM_B1 = 0.9
ADAM_B2 = 0.999
ADAM_EPS = 1e-08
ADAM_WD = 0.01
ADAM_STEP = 10

VMEM_LIMIT = 56 * 1024 * 1024


def _dot(a, b):
    return jnp.dot(a, b, preferred_element_type=F32)


def _dot_nt(a, b):
    return lax.dot_general(a, b, (((1,), (1,)), ((), ())), preferred_element_type=F32)


def _dot_tn(a, b):
    return lax.dot_general(a, b, (((0,), (0,)), ((), ())), preferred_element_type=F32)


def _split_bf16(x, n):
    parts = []
    r = x
    for _ in range(n):
        p = r.astype(BF16)
        parts.append(p)
        r = r - p.astype(F32)
    return parts


def _exact_left(m, x, n=3):
    acc = None
    for p in _split_bf16(x, n):
        t = _dot(m, p)
        acc = t if acc is None else acc + t
    return acc


def _exact_left_many(m, xs, n=3):
    parts = [_split_bf16(x, n) for x in xs]
    accs = [None] * len(xs)
    for i in range(n):
        for j in range(len(xs)):
            t = _dot(m, parts[j][i])
            accs[j] = t if accs[j] is None else accs[j] + t
    return accs


def _group_mean_many(xs, gmat, n=2):
    parts = [_split_bf16(x, n) for x in xs]
    accs = [None] * len(xs)
    for i in range(n):
        for j in range(len(xs)):
            t = _dot(parts[j][i], gmat)
            accs[j] = t if accs[j] is None else accs[j] + t
    return accs


def _group_mean(x, gmat, n=2):
    w = gmat.shape[0]
    outs = []
    for c0 in range(0, x.shape[1], w):
        acc = None
        for p in _split_bf16(x[:, c0:c0 + w], n):
            t = _dot(p, gmat)
            acc = t if acc is None else acc + t
        outs.append(acc)
    return jnp.concatenate(outs, axis=1)


def _sigmoid(x):
    return 1.0 / (1.0 + jnp.exp(-x))


def _lower_bound(lbl):
    l0 = lbl[0:1, :]
    l1 = lbl[1:2, :]
    m = jnp.maximum(l0, l1)
    e0 = jnp.exp(l0 - m)
    e1 = jnp.exp(l1 - m)
    return e0 / (e0 + e1)


def _tri(lower):
    r = lax.broadcasted_iota(jnp.int32, (CHUNK, CHUNK), 0)
    c = lax.broadcasted_iota(jnp.int32, (CHUNK, CHUNK), 1)
    return jnp.where((c <= r) if lower else (c >= r), 1.0, 0.0).astype(BF16)


def _causal():
    r = lax.broadcasted_iota(jnp.int32, (CHUNK, CHUNK), 0)
    c = lax.broadcasted_iota(jnp.int32, (CHUNK, CHUNK), 1)
    return c <= r


def _shift_down(x, sh, prev_tail):
    r = pltpu.roll(x, sh, 0)
    pt = pltpu.roll(prev_tail, sh, 0)
    rows = lax.broadcasted_iota(jnp.int32, prev_tail.shape, 0)
    top = jnp.where(rows < sh, pt, r[0:8])
    return jnp.concatenate([top, r[8:]], axis=0)


def _shift_up(x, sh, next_head):
    n = x.shape[0]
    r = pltpu.roll(x, n - sh, 0)
    nh = pltpu.roll(next_head, 8 - sh, 0)
    rows = lax.broadcasted_iota(jnp.int32, next_head.shape, 0)
    bot = jnp.where(rows >= 8 - sh, nh, r[n - 8:])
    return jnp.concatenate([r[:n - 8], bot], axis=0)


def _group_matrix(width, group):
    r = np.arange(width)[:, None] // group
    c = np.arange(width)[None, :] // group
    return jnp.asarray(np.where(r == c, 1.0 / group, 0.0), dtype=BF16)


TG = 1024
SEM_W, SEM_CW, SEM_W_FWD, N_SEM = 0, 4, 7, 11


def _gather_proj(kidx, x2d, g1, w_in, conv_w):
    half_w = D_MODEL // 2
    half_c = SHARD_COLS // 2
    nt = SEQ // TG
    n_steps = 2 * N_SHARD

    def body(k_ref, x_ref, g_ref, w_ref, cw_ref, h_out, p_ref, wg_out, cwg_out,
             wg_v, cwg_v, h_ref, send_sems, recv_sems, out_sems):
        s, t = pl.program_id(0), pl.program_id(1)
        x, y, c = lax.axis_index("x"), lax.axis_index("y"), lax.axis_index("c")
        k = 2 * x + y
        sibling = (x, y, 1 - c)
        chips = [(1 - x, y), (x, 1 - y), (1 - x, 1 - y)]
        kjs = [2 * cx + cy for cx, cy in chips]
        diag = (*chips[2], c)

        def w_half(kk, cc):
            return wg_v.at[kk, pl.ds(cc * half_w, half_w), :]

        def w_quarter(kk, cc, piece):
            return wg_v.at[kk, pl.ds(cc * half_w, half_w), piece * half_c:(piece + 1) * half_c]

        def cw_of(kk):
            return cwg_v.at[:, pl.ds(pl.multiple_of(kk * HEAD, HEAD), HEAD)]

        def copy(sem, ref, to):
            return pltpu.make_async_remote_copy(
                src_ref=ref, dst_ref=ref, send_sem=send_sems.at[sem], recv_sem=recv_sems.at[sem],
                device_id=to, device_id_type=MESH)

        def at_step(sv, tv):
            return pl.when((s == sv) & (t == tv))

        w_direct = ([copy(SEM_W + j, w_half(k, c), (*chips[j], c)) for j in range(2)]
                    + [copy(SEM_W + 2 + p, w_quarter(k, c, p), diag) for p in range(2)])
        cw_direct = [copy(SEM_CW + j, cw_of(k), (*chip, c)) for j, chip in enumerate(chips)]
        w_passed = ([copy(SEM_W_FWD + j, w_half(kjs[j], c), sibling) for j in range(2)]
                    + [copy(SEM_W_FWD + 2 + p, w_quarter(kjs[2], c, p), sibling) for p in range(2)])
        stores = ([pltpu.make_async_copy(wg_v.at[kk], wg_out.at[kk], out_sems.at[i])
                   for i, kk in enumerate([k] + kjs)]
                  + [pltpu.make_async_copy(cwg_v, cwg_out, out_sems.at[4]),
                     pltpu.make_async_copy(h_ref, h_out, out_sems.at[5])])

        @at_step(0, 0)
        def _():
            barrier = pltpu.get_barrier_semaphore()
            for peer in [sibling] + [(*chip, c) for chip in chips]:
                pl.semaphore_signal(barrier, inc=1, device_id=peer, device_id_type=MESH)
            wg_v[k] = w_ref[0].astype(BF16)
            mine = pl.ds(pl.multiple_of(k * HEAD, HEAD), HEAD)
            cwg_v[:, mine] = jnp.zeros((8, HEAD), F32)
            for tap in range(3):
                cwg_v[tap:tap + 1, mine] = cw_ref[:, tap * HEAD:(tap + 1) * HEAD]
            pl.semaphore_wait(barrier, 4)
            for cp in w_direct + cw_direct:
                cp.start()
            stores[0].start()

        @at_step(1, 0)
        def _():
            stores[5].start()

        @at_step(2, 0)
        def _():
            for j in range(2):
                copy(SEM_W + j, w_half(kjs[j], c), sibling).wait_recv()
                w_passed[j].start()
            copy(SEM_W_FWD, w_half(kjs[0], 1 - c), sibling).wait_recv()
            stores[1].start()

        @at_step(4, 0)
        def _():
            copy(SEM_W_FWD + 1, w_half(kjs[1], 1 - c), sibling).wait_recv()
            stores[2].start()

        for p in range(2):
            @at_step(6 + p, 0)
            def _(p=p):
                copy(SEM_W + 2 + p, w_quarter(kjs[2], c, p), sibling).wait_recv()
                w_passed[2 + p].start()
                copy(SEM_W_FWD + 2 + p, w_quarter(kjs[2], 1 - c, p), sibling).wait_recv()
                if p == 1:
                    stores[3].start()

        rows = pl.ds(pl.multiple_of(t * TG, TG), TG)

        @pl.when(s == 0)
        def _():
            xv = x_ref[...]
            r = lax.rsqrt(jnp.mean(xv * xv, axis=-1, keepdims=True) + EPS)
            h_ref[rows, :] = (xv * r * g_ref[...]).astype(BF16)

        sh = s >> 1
        js = k ^ (((sh & 1) << 1) | (sh >> 1))
        for piece in range(2):
            @pl.when((s & 1) == piece)
            def _(piece=piece):
                p_ref[...] = _dot(h_ref[rows, :], wg_v[js, :, piece * half_c:(piece + 1) * half_c])

        @at_step(n_steps - 1, nt - 1)
        def _():
            for j in range(3):
                copy(SEM_CW + j, cw_of(kjs[j]), sibling).wait_recv()
            stores[4].start()
            for cp in w_direct + cw_direct + w_passed:
                cp.wait_send()
            for st in stores:
                st.wait()

    def x_map(s, t, kr):
        return (jnp.where(s == 0, t, nt - 1), 0)

    def p_map(s, t, kr):
        sh = s >> 1
        return (t, 2 * (kr[0] ^ (((sh & 1) << 1) | (sh >> 1))) + (s & 1))

    hbm = pl.BlockSpec(memory_space=pl.ANY)
    grid_spec = pltpu.PrefetchScalarGridSpec(
        num_scalar_prefetch=1, grid=(n_steps, nt),
        in_specs=[pl.BlockSpec((TG, D_MODEL), x_map),
                  pl.BlockSpec((1, D_MODEL), lambda s, t, kr: (0, 0)),
                  pl.BlockSpec((1, D_MODEL, SHARD_COLS), lambda s, t, kr: (0, 0, 0)),
                  pl.BlockSpec((1, 3 * HEAD), lambda s, t, kr: (0, 0))],
        out_specs=(hbm, pl.BlockSpec((TG, half_c), p_map), hbm, hbm),
        scratch_shapes=[pltpu.VMEM((N_SHARD, D_MODEL, SHARD_COLS), BF16),
                        pltpu.VMEM((8, D_CONV), F32), pltpu.VMEM((SEQ, D_MODEL), BF16),
                        pltpu.SemaphoreType.DMA((N_SEM,)), pltpu.SemaphoreType.DMA((N_SEM,)),
                        pltpu.SemaphoreType.DMA((6,))])
    return pl.pallas_call(
        body, name="gather_proj", grid_spec=grid_spec,
        out_shape=(jax.ShapeDtypeStruct((SEQ, D_MODEL), BF16),
                   jax.ShapeDtypeStruct((SEQ, N_SHARD * SHARD_COLS), F32),
                   jax.ShapeDtypeStruct((N_SHARD, D_MODEL, SHARD_COLS), BF16),
                   jax.ShapeDtypeStruct((8, D_CONV), F32)),
        compiler_params=pltpu.CompilerParams(dimension_semantics=("arbitrary", "arbitrary"),
                                             vmem_limit_bytes=VMEM_LIMIT, collective_id=COLLECTIVE_GATHER),
    )(kidx, x2d, g1, w_in, conv_w)


LAG = 6


def _mix_out(proj, lb_logits, cw, ga, gcn, g64, w_out, x2d, gf, tgt):
    half_o = WO_ROWS // 2
    nblk = SEQ // TB
    n_steps = nblk + LAG

    def body(p_ref, lbl_ref, cw_ref, ga_ref, gcn_ref, g64_ref, wo_ref, x_ref, gf_ref, t_ref,
             aux_ref, sto_ref, dx2_ref, dm_ref, gwo_ref, part_ref,
             st_ref, tail_ref, wog_v, stage, ring, acc_ref, send_sems, recv_sems):
        i = pl.program_id(0)
        x, y, c = lax.axis_index("x"), lax.axis_index("y"), lax.axis_index("c")
        k = 2 * x + y
        sibling = (x, y, 1 - c)
        chips = [(1 - x, y), (x, 1 - y), (1 - x, 1 - y)]
        kjs = [2 * cx + cy for cx, cy in chips]

        def wo_half(kk, cc):
            return wog_v.at[pl.ds(pl.multiple_of(kk * WO_ROWS + cc * half_o, half_o), half_o), :]

        def copy(sem, ref, to):
            return pltpu.make_async_remote_copy(
                src_ref=ref, dst_ref=ref, send_sem=send_sems.at[sem], recv_sem=recv_sems.at[sem],
                device_id=to, device_id_type=MESH)

        wo_direct = [copy(j, wo_half(k, c), (*chip, c)) for j, chip in enumerate(chips)]
        wo_passed = [copy(3 + j, wo_half(kj, c), sibling) for j, kj in enumerate(kjs)]

        @pl.when(i == 0)
        def _():
            barrier = pltpu.get_barrier_semaphore()
            for peer in [sibling] + [(*chip, c) for chip in chips]:
                pl.semaphore_signal(barrier, inc=1, device_id=peer, device_id_type=MESH)
            st_ref[...] = jnp.zeros_like(st_ref)
            tail_ref[...] = jnp.zeros_like(tail_ref)
            acc_ref[...] = jnp.zeros_like(acc_ref)
            part_ref[...] = jnp.zeros_like(part_ref)
            wog_v[pl.ds(pl.multiple_of(k * WO_ROWS, WO_ROWS), WO_ROWS), :] = wo_ref[0].astype(BF16)
            pl.semaphore_wait(barrier, 4)
            for cp in wo_direct:
                cp.start()

        @pl.when(i == LAG - 1)
        def _():
            for j in range(3):
                copy(j, wo_half(kjs[j], c), sibling).wait_recv()
                wo_passed[j].start()

        @pl.when(i == LAG)
        def _():
            for j in range(3):
                copy(3 + j, wo_half(kjs[j], 1 - c), sibling).wait_recv()

        lb = _lower_bound(lbl_ref[...])
        tri = _tri(True)
        causal = _causal()
        g64m = g64_ref[...]
        heads = range(N_HEADS)
        cs = [slice(hd * HEAD, (hd + 1) * HEAD) for hd in heads]
        col = lambda base, hd: slice(base + hd * HEAD, base + (hd + 1) * HEAD)

        def mix_chunk(n):
            sl = pl.ds(n * CHUNK, CHUNK)
            sg = [_sigmoid(p_ref[sl, col(512, hd)]) for hd in heads]
            f = [lb[:, cs[hd]] + (1.0 - lb[:, cs[hd]]) * sg[hd] for hd in heads]
            bc = _exact_left_many(tri, [jnp.log(f[hd]) for hd in heads])
            for hd in heads:
                aux_ref[sl, col(AUX_B, hd)] = bc[hd]
            g = [bc[hd][CHUNK - 1:CHUNK, :] for hd in heads]
            qd = [(p_ref[sl, col(0, hd)] * jnp.exp(bc[hd])).astype(BF16) for hd in heads]
            kk = [1.0 - f[hd] for hd in heads]
            ki = [(kk[hd] * jnp.exp(-bc[hd])).astype(BF16) for hd in heads]
            ke = [(kk[hd] * jnp.exp(g[hd] - bc[hd])).astype(BF16) for hd in heads]
            vb = [p_ref[sl, col(1024, hd)].astype(BF16) for hd in heads]
            st = [st_ref[hd] for hd in heads]
            st_b = [a.astype(BF16) for a in st]
            for hd in heads:
                sto_ref[n, hd] = st_b[hd]
            scm = [_dot_nt(qd[hd], ki[hd]) for hd in heads]
            inter = [_dot_nt(qd[hd], st_b[hd]) for hd in heads]
            upd = [_dot_tn(vb[hd], ke[hd]) for hd in heads]
            intra = [_dot(jnp.where(causal, scm[hd], 0.0).astype(BF16), vb[hd]) for hd in heads]
            for hd in heads:
                st_ref[hd] = st[hd] * jnp.exp(g[hd]) + upd[hd]
                o = intra[hd] + inter[hd]
                aux_ref[sl, col(AUX_O, hd)] = o
                ra = lax.rsqrt(jnp.mean(o * o, axis=-1, keepdims=True) + EPS)
                za = p_ref[sl, col(1536, hd)]
                stage[sl, cs[hd]] = (o * ra * ga_ref[:, cs[hd]] * (za * _sigmoid(za))).astype(BF16)
            yb = []
            for hd in heads:
                cu = p_ref[sl, col(3072, hd)] * p_ref[sl, col(2048, hd)]
                tail = tail_ref[:, cs[hd]]
                cv = (cw_ref[0:1, cs[hd]] * _shift_down(cu, 2, tail) + cw_ref[1:2, cs[hd]] * _shift_down(cu, 1, tail)
                      + cw_ref[2:3, cs[hd]] * cu)
                tail_ref[:, cs[hd]] = cu[CHUNK - 8:, :]
                aux_ref[sl, col(AUX_CV, hd)] = cv
                yb.append(p_ref[sl, col(2560, hd)] * cv)
            ms = _group_mean_many([y * y for y in yb], g64m)
            for hd in heads:
                rb = lax.rsqrt(ms[hd] + EPS)
                zb = p_ref[sl, col(3584, hd)]
                stage[sl, col(512, hd)] = (yb[hd] * rb * gcn_ref[:, cs[hd]] * (zb * _sigmoid(zb))).astype(BF16)

        def step(mix, project):
            if project:
                mixed_b = ring[pl.ds(pl.multiple_of((i - LAG) * TB, TB), TB), :]
                y = _dot(mixed_b, wog_v[...])
            if mix:
                mix_chunk(0)
            if project:
                x2 = x_ref[...] + y
                r2 = lax.rsqrt(jnp.mean(x2 * x2, axis=-1, keepdims=True) + EPS)
                n2 = x2 * r2
                gfv = gf_ref[...]
                err = n2 * gfv - t_ref[...]
                loss = 0.5 * jnp.sum(jnp.mean(err * err, axis=-1, keepdims=True), axis=0, keepdims=True)
                dy = err * (1.0 / D_MODEL)
                part_ref[1:2, :] += jnp.sum(dy * n2, axis=0, keepdims=True)
                part_ref[7:8, :] += jnp.broadcast_to(loss, (1, D_MODEL))
                dn = dy * gfv
                dx2 = r2 * (dn - n2 * jnp.mean(dn * n2, axis=-1, keepdims=True))
                dx2_ref[...] = dx2
                dx2_b = dx2.astype(BF16)
            if mix:
                mix_chunk(1)
            if project:
                dm_ref[...] = _dot_nt(dx2_b, wog_v[...])
            if mix:
                mix_chunk(2)
            if project:
                acc_ref[...] += _dot_tn(mixed_b, dx2_b)
            if mix:
                mix_chunk(3)
                ring[pl.ds(pl.multiple_of(i * TB, TB), TB), :] = stage[...]

        @pl.when(i < LAG)
        def _():
            step(True, False)

        @pl.when((i >= LAG) & (i < nblk))
        def _():
            step(True, True)

        @pl.when(i >= nblk)
        def _():
            step(False, True)

        @pl.when(i == n_steps - 1)
        def _():
            gwo_ref[...] = acc_ref[...].astype(BF16)
            for cp in wo_direct + wo_passed:
                cp.wait_send()

    assert NCB == 4
    row = lambda w: pl.BlockSpec((1, w), lambda i: (0, 0))
    mix_blk = lambda i: jnp.minimum(i, nblk - 1)
    out_blk = lambda i: jnp.clip(i - LAG, 0, nblk - 1)
    tok = lambda: pl.BlockSpec((TB, D_MODEL), lambda i: (out_blk(i), 0))
    return pl.pallas_call(
        body, name="mix_out", grid=(n_steps,),
        out_shape=(jax.ShapeDtypeStruct((SEQ, AUX_COLS), F32),
                   jax.ShapeDtypeStruct((N_CHUNKS, N_HEADS, HEAD, HEAD), BF16),
                   jax.ShapeDtypeStruct((SEQ, D_MODEL), F32),
                   jax.ShapeDtypeStruct((SEQ, D_MODEL), F32),
                   jax.ShapeDtypeStruct((D_MODEL, D_MODEL), BF16),
                   jax.ShapeDtypeStruct((8, D_MODEL), F32)),
        in_specs=[pl.BlockSpec((TB, 4096), lambda i: (jnp.minimum(i, nblk - 1), 0)),
                  pl.BlockSpec((2, D_HGRN), lambda i: (0, 0)),
                  pl.BlockSpec((8, D_CONV), lambda i: (0, 0)),
                  row(D_HGRN), row(D_CONV),
                  pl.BlockSpec((HEAD, HEAD), lambda i: (0, 0)),
                  pl.BlockSpec((1, WO_ROWS, D_MODEL), lambda i: (0, 0, 0)),
                  tok(), row(D_MODEL), tok()],
        out_specs=(pl.BlockSpec((TB, AUX_COLS), lambda i: (mix_blk(i), 0)),
                   pl.BlockSpec((NCB, N_HEADS, HEAD, HEAD), lambda i: (mix_blk(i), 0, 0, 0)),
                   tok(), tok(),
                   pl.BlockSpec((D_MODEL, D_MODEL), lambda i: (0, 0)),
                   pl.BlockSpec((8, D_MODEL), lambda i: (0, 0))),
        scratch_shapes=[pltpu.VMEM((N_HEADS, HEAD, HEAD), F32), pltpu.VMEM((8, D_CONV), F32),
                        pltpu.VMEM((D_MODEL, D_MODEL), BF16), pltpu.VMEM((TB, D_MODEL), BF16),
                        pltpu.VMEM((SEQ, D_MODEL), BF16), pltpu.VMEM((D_MODEL, D_MODEL), F32),
                        pltpu.SemaphoreType.DMA((6,)), pltpu.SemaphoreType.DMA((6,))],
        compiler_params=pltpu.CompilerParams(dimension_semantics=("arbitrary",), vmem_limit_bytes=VMEM_LIMIT,
                                             collective_id=COLLECTIVE_MIX_OUT),
    )(proj, lb_logits, cw, ga, gcn, g64, w_out, x2d, gf, tgt)


def _mix_bwd(proj, aux, states, dmixed, lb_logits, cw, ga, gcn, g64):
    nblk = SEQ // TB

    def body(p_ref, aux_ref, st_ref, dm_ref, lbl_ref, cw_ref, ga_ref, gcn_ref, g64_ref,
             dp_ref, part_ref, dst_ref, head_ref, dlb_ref):
        i = pl.program_id(0)

        @pl.when(i == 0)
        def _():
            dst_ref[...] = jnp.zeros_like(dst_ref)
            head_ref[...] = jnp.zeros_like(head_ref)
            part_ref[...] = jnp.zeros_like(part_ref)
            dlb_ref[...] = jnp.zeros_like(dlb_ref)

        lb = _lower_bound(lbl_ref[...])
        triu = _tri(False)
        causal = _causal()
        g64m = g64_ref[...]
        rowsum = lambda a: jnp.sum(a, axis=0, keepdims=True)
        heads = range(N_HEADS)
        cs = [slice(hd * HEAD, (hd + 1) * HEAD) for hd in heads]
        col = lambda base, hd: slice(base + hd * HEAD, base + (hd + 1) * HEAD)
        for n in reversed(range(NCB)):
            sl = pl.ds(n * CHUNK, CHUNK)
            cvv = [aux_ref[sl, col(AUX_CV, hd)] for hd in heads]
            gb = [p_ref[sl, col(2560, hd)] for hd in heads]
            yb = [gb[hd] * cvv[hd] for hd in heads]
            ms = _group_mean_many([y * y for y in yb], g64m)
            rb, nb, dnb = [], [], []
            for hd in heads:
                rb.append(lax.rsqrt(ms[hd] + EPS))
                nb.append(yb[hd] * rb[hd])
                zb = p_ref[sl, col(3584, hd)]
                sgb = _sigmoid(zb)
                dmb = dm_ref[sl, col(512, hd)]
                silu = zb * sgb
                dgate = dmb * gcn_ref[:, cs[hd]]
                part_ref[2:3, col(512, hd)] += rowsum(dmb * nb[hd] * silu)
                dp_ref[sl, col(3584, hd)] = (dgate * nb[hd] * (sgb + silu * (1.0 - sgb))).astype(BF16)
                dnb.append(dgate * silu)
            mdn = _group_mean_many([dnb[hd] * nb[hd] for hd in heads], g64m)
            for hd in heads:
                dyb = rb[hd] * (dnb[hd] - nb[hd] * mdn[hd])
                dp_ref[sl, col(2560, hd)] = (dyb * cvv[hd]).astype(BF16)
                dcv = dyb * gb[hd]
                head = head_ref[:, cs[hd]]
                dcv1 = _shift_up(dcv, 1, head)
                dcv2 = _shift_up(dcv, 2, head)
                head_ref[:, cs[hd]] = dcv[0:8, :]
                u = p_ref[sl, col(2048, hd)]
                gc = p_ref[sl, col(3072, hd)]
                cu = gc * u
                part_ref[4:5, cs[hd]] += rowsum(dcv2 * cu)
                part_ref[5:6, cs[hd]] += rowsum(dcv1 * cu)
                part_ref[6:7, cs[hd]] += rowsum(dcv * cu)
                dcu = cw_ref[2:3, cs[hd]] * dcv + cw_ref[1:2, cs[hd]] * dcv1 + cw_ref[0:1, cs[hd]] * dcv2
                dp_ref[sl, col(3072, hd)] = (dcu * u).astype(BF16)
                dp_ref[sl, col(2048, hd)] = (dcu * gc).astype(BF16)
            do_b = []
            for hd in heads:
                ov = aux_ref[sl, col(AUX_O, hd)]
                ra = lax.rsqrt(jnp.mean(ov * ov, axis=-1, keepdims=True) + EPS)
                na = ov * ra
                za = p_ref[sl, col(1536, hd)]
                sga = _sigmoid(za)
                dma = dm_ref[sl, cs[hd]]
                silu = za * sga
                dgate = dma * ga_ref[:, cs[hd]]
                part_ref[2:3, cs[hd]] += rowsum(dma * na * silu)
                dp_ref[sl, col(1536, hd)] = (dgate * na * (sga + silu * (1.0 - sga))).astype(BF16)
                dna = dgate * silu
                do_b.append((ra * (dna - na * jnp.mean(dna * na, axis=-1, keepdims=True))).astype(BF16))
            s = [_sigmoid(p_ref[sl, col(512, hd)]) for hd in heads]
            f = [lb[:, cs[hd]] + (1.0 - lb[:, cs[hd]]) * s[hd] for hd in heads]
            bc = [aux_ref[sl, col(AUX_B, hd)] for hd in heads]
            g = [bc[hd][CHUNK - 1:CHUNK, :] for hd in heads]
            eb = [jnp.exp(bc[hd]) for hd in heads]
            enb = [jnp.exp(-bc[hd]) for hd in heads]
            eg = [jnp.exp(g[hd] - bc[hd]) for hd in heads]
            dec = [jnp.exp(g[hd]) for hd in heads]
            qd = [p_ref[sl, cs[hd]] * eb[hd] for hd in heads]
            kk = [1.0 - f[hd] for hd in heads]
            ki = [kk[hd] * enb[hd] for hd in heads]
            ke = [kk[hd] * eg[hd] for hd in heads]
            qd_b = [a.astype(BF16) for a in qd]
            ki_b = [a.astype(BF16) for a in ki]
            ke_b = [a.astype(BF16) for a in ke]
            vb = [p_ref[sl, col(1024, hd)].astype(BF16) for hd in heads]
            st_b = [st_ref[n, hd] for hd in heads]
            dst = [dst_ref[hd] for hd in heads]
            dst_b = [a.astype(BF16) for a in dst]
            scm = [_dot_nt(qd_b[hd], ki_b[hd]) for hd in heads]
            amm = [_dot_nt(do_b[hd], vb[hd]) for hd in heads]
            dqd2 = [_dot(do_b[hd], st_b[hd]) for hd in heads]
            dke = [_dot(vb[hd], dst_b[hd]) for hd in heads]
            dv2 = [_dot_nt(ke_b[hd], dst_b[hd]) for hd in heads]
            dsu = [_dot_tn(do_b[hd], qd_b[hd]) for hd in heads]
            sc = [jnp.where(causal, scm[hd], 0.0).astype(BF16) for hd in heads]
            am = [jnp.where(causal, amm[hd], 0.0).astype(BF16) for hd in heads]
            dqd1 = [_dot(am[hd], ki_b[hd]) for hd in heads]
            dki = [_dot_tn(am[hd], qd_b[hd]) for hd in heads]
            dv1 = [_dot_tn(sc[hd], do_b[hd]) for hd in heads]
            db, dgv, dkk = [], [], []
            for hd in heads:
                dqd = dqd1[hd] + dqd2[hd]
                ddec = rowsum(dst[hd] * st_b[hd].astype(F32))
                dst_ref[hd] = dst[hd] * dec[hd] + dsu[hd]
                dp_ref[sl, cs[hd]] = (dqd * eb[hd]).astype(BF16)
                dp_ref[sl, col(1024, hd)] = (dv1[hd] + dv2[hd]).astype(BF16)
                dke_eg = dke[hd] * eg[hd]
                dkk.append(dki[hd] * enb[hd] + dke_eg)
                db.append(dqd * qd[hd] - kk[hd] * dkk[hd])
                dgv.append(rowsum(kk[hd] * dke_eg) + ddec * dec[hd])
            rc = _exact_left_many(triu, db, 2)
            for hd in heads:
                df = (rc[hd] + dgv[hd]) / f[hd] - dkk[hd]
                one_s = 1.0 - s[hd]
                dlb_ref[:, cs[hd]] += rowsum(df * one_s)
                dp_ref[sl, col(512, hd)] = (df * (1.0 - lb[:, cs[hd]]) * s[hd] * one_s).astype(BF16)

        @pl.when(i == nblk - 1)
        def _():
            row = dlb_ref[...] * lb * (1.0 - lb)
            part_ref[3:4, 0:D_HGRN] = row
            part_ref[3:4, D_HGRN:] = -row

    rev = lambda w: pl.BlockSpec((TB, w), lambda i: (nblk - 1 - i, 0))
    row = lambda w: pl.BlockSpec((1, w), lambda i: (0, 0))
    return pl.pallas_call(
        body, name="mix_bwd", grid=(nblk,),
        out_shape=(jax.ShapeDtypeStruct((SEQ, 4096), BF16),
                   jax.ShapeDtypeStruct((8, D_MODEL), F32)),
        in_specs=[rev(4096), rev(AUX_COLS),
                  pl.BlockSpec((NCB, N_HEADS, HEAD, HEAD), lambda i: (nblk - 1 - i, 0, 0, 0)),
                  rev(D_MODEL),
                  pl.BlockSpec((2, D_HGRN), lambda i: (0, 0)),
                  pl.BlockSpec((8, D_CONV), lambda i: (0, 0)),
                  row(D_HGRN), row(D_CONV),
                  pl.BlockSpec((HEAD, HEAD), lambda i: (0, 0))],
        out_specs=(rev(4096), pl.BlockSpec((8, D_MODEL), lambda i: (0, 0))),
        scratch_shapes=[pltpu.VMEM((N_HEADS, HEAD, HEAD), F32), pltpu.VMEM((8, D_CONV), F32),
                        pltpu.VMEM((1, D_HGRN), F32)],
        compiler_params=pltpu.CompilerParams(dimension_semantics=("arbitrary",), vmem_limit_bytes=VMEM_LIMIT),
    )(proj, aux, states, dmixed, lb_logits, cw, ga, gcn, g64)


TT = 1024
TX = 512
(SEM_D2D, SEM_D2D_O, SEM_ICI, SEM_ICI_O, SEM_FIN, SEM_FIN_O, SEM_SMALL, SEM_VIA, SEM_NORM, N_SEM_TAIL) = (
    0, 4, 5, 8, 11, 12, 12, 20, 22, 30)


def _bwd_tail(kidx, h, dproj, wg, gwo, x2d, dx2, g1, small_a, small_b):
    hw = D_MODEL // 2
    ho = WO_ROWS // 2
    nt = SEQ // TT
    norm_step = 2 * N_SHARD
    n_steps = norm_step + SEQ // TX // nt

    def body(k_ref, h_ref, dp_ref, w_ref, gwo_ref, x_ref, dx2_ref, g_ref, sm_ref, smb_ref,
             gx_ref, gw_out, gwo_out, osm_ref,
             acc, dh, sendbuf, keep, sibrcv, rcv, merge, sib_o, p_o, rcv_o, res_o, sm_buf, dng_buf, dng,
             x_buf, dx2_buf, send_sems, recv_sems, out_sems, in_sems):
        s, t = pl.program_id(0), pl.program_id(1)
        x, y, c = lax.axis_index("x"), lax.axis_index("y"), lax.axis_index("c")
        k = 2 * x + y
        me = 4 * x + 2 * y + c
        sibling = (x, y, 1 - c)
        chips = [(1 - x, 1 - y), (1 - x, y), (x, 1 - y)]
        kjs = [2 * cx + cy for cx, cy in chips]
        mine = pl.ds(pl.multiple_of(c * hw, hw), hw)
        other = pl.ds(pl.multiple_of((1 - c) * hw, hw), hw)
        mine_o = pl.ds(pl.multiple_of(c * ho, ho), ho)
        other_o = pl.ds(pl.multiple_of((1 - c) * ho, ho), ho)

        def copy(sem, src, dst, to):
            return pltpu.make_async_remote_copy(
                src_ref=src, dst_ref=dst, send_sem=send_sems.at[sem], recv_sem=recv_sems.at[sem],
                device_id=to, device_id_type=MESH)

        def at_step(sv, tv):
            return pl.when((s == sv) & (t == tv))

        def at_norm_block(b):
            return at_step(norm_step + b // nt, b % nt)

        d2d = [copy(SEM_D2D + sv, sendbuf.at[sv], sibrcv.at[sv], sibling) for sv in range(N_SHARD)]
        d2d_o = copy(SEM_D2D_O, gwo_ref.at[:, other_o, :], sib_o, sibling)
        ici = {sv: copy(SEM_ICI + sv, keep.at[sv], rcv.at[sv - 1], (*chips[sv], c)) for sv in (1, 2)}
        qh = hw // 2
        via = [copy(SEM_VIA, keep.at[0, 0:qh, :], merge.at[1], (*chips[1], c)),
               copy(SEM_VIA + 1, keep.at[0, qh:hw, :], merge.at[0], (*chips[2], c))]
        merged_rows = [slice(qh, hw), slice(0, qh)]
        ici_o = [copy(SEM_ICI_O + sv, p_o.at[kjs[sv]], rcv_o.at[sv], (*chips[sv], c)) for sv in range(3)]
        fin = copy(SEM_FIN, acc.at[mine, :], gw_out.at[mine, :], sibling)
        fin_o = copy(SEM_FIN_O, res_o.at[mine_o, :], res_o.at[mine_o, :], sibling)
        peers = [(x ^ (m >> 2), y ^ ((m >> 1) & 1), c ^ (m & 1)) for m in range(1, N_DEV)]
        smalls = [copy(SEM_SMALL + 1 + j, sm_buf.at[me], sm_buf.at[me], to) for j, to in enumerate(peers)]
        dngs = [copy(SEM_NORM + 1 + j, dng_buf.at[me], dng_buf.at[me], to) for j, to in enumerate(peers)]
        store_w = pltpu.make_async_copy(acc.at[mine, :], gw_out.at[mine, :], out_sems.at[0])
        store_o = pltpu.make_async_copy(res_o, gwo_out, out_sems.at[1])

        @at_step(0, 0)
        def _():
            barrier = pltpu.get_barrier_semaphore()
            for to in peers:
                pl.semaphore_signal(barrier, inc=1, device_id=to, device_id_type=MESH)
            sm_buf[me] = sm_ref[...] + smb_ref[...]
            pl.semaphore_wait(barrier, N_DEV - 1)
            d2d_o.start()
            for cp in smalls:
                cp.start()

        @at_step(0, 1)
        def _():
            d2d_o.wait_recv()
            for j in range(N_SHARD):
                p_o[j] = (gwo_ref[j, mine_o, :].astype(F32) + sib_o[j].astype(F32)).astype(BF16)
            res_o[mine_o, :] = gwo_ref[k, mine_o, :].astype(F32) + sib_o[k].astype(F32)
            for cp in ici_o:
                cp.start()

        rows = pl.ds(pl.multiple_of(t * TT, TT), TT)

        @pl.when((s < N_SHARD) & (t == 0))
        def _():
            acc[...] = _dot_tn(h_ref[...], dp_ref[...])

        @pl.when((s < N_SHARD) & (t > 0))
        def _():
            acc[...] += _dot_tn(h_ref[...], dp_ref[...])

        for sv in range(N_SHARD):
            @at_step(sv, nt - 1)
            def _(sv=sv):
                sendbuf[sv] = acc[other, :].astype(BF16)
                if sv < 3:
                    keep[sv] = acc[mine, :].astype(BF16)
                d2d[sv].start()

        @at_step(1, 0)
        def _():
            d2d[0].wait_recv()
            keep[0] = (keep[0].astype(F32) + sibrcv[0].astype(F32)).astype(BF16)
            for cp in via:
                cp.start()

        for sv in (1, 2):
            @at_step(sv + 2, 0)
            def _(sv=sv):
                d2d[sv].wait_recv()
                keep[sv] = (keep[sv].astype(F32) + sibrcv[sv].astype(F32)).astype(BF16)
                via[2 - sv].wait_recv()
                rows_m = merged_rows[sv - 1]
                keep[sv, rows_m, :] = (keep[sv, rows_m, :].astype(F32) + merge[sv - 1].astype(F32)).astype(BF16)
                ici[sv].start()

        @pl.when(s == N_SHARD)
        def _():
            dh[rows, :] = _dot_nt(dp_ref[...], w_ref[0])

        @pl.when((s > N_SHARD) & (s < norm_step))
        def _():
            dh[rows, :] += _dot_nt(dp_ref[...], w_ref[0])

        @at_norm_block(0)
        def _():
            d2d[3].wait_recv()
            acc[mine, :] += sibrcv[3].astype(F32)

        @at_norm_block(1)
        def _():
            tot = res_o[mine_o, :]
            for sv in range(3):
                ici_o[sv].wait_recv()
                tot = tot + rcv_o[sv].astype(F32)
            res_o[mine_o, :] = tot
            fin_o.start()

        @at_norm_block(2)
        def _():
            ici[1].wait_recv()
            acc[mine, :] += rcv[0].astype(F32)

        @at_norm_block(SEQ // TX - 2)
        def _():
            ici[2].wait_recv()
            acc[mine, :] += rcv[1].astype(F32)
            fin.start()
            store_w.start()
            fin_o.wait_recv()
            store_o.start()

        @at_norm_block(0)
        def _():
            dng[...] = jnp.zeros_like(dng)

        def fetch(blk):
            at = blk * TX if isinstance(blk, int) else pl.multiple_of(blk * TX, TX)
            return [pltpu.make_async_copy(src.at[pl.ds(at, TX), :], buf.at[blk % 2], in_sems.at[2 * j + blk % 2])
                    for j, (src, buf) in enumerate([(x_ref, x_buf), (dx2_ref, dx2_buf)])]

        @at_step(norm_step - 1, 0)
        def _():
            for cp in fetch(0):
                cp.start()

        @pl.when(s >= norm_step)
        def _():
            blk = (s - norm_step) * nt + t
            for cp in fetch(blk):
                cp.wait()

            @pl.when(blk + 1 < SEQ // TX)
            def _():
                for cp in fetch(blk + 1):
                    cp.start()

            dhv = dh[pl.ds(pl.multiple_of(blk * TX, TX), TX), :]
            xv = x_buf[blk % 2]
            r = lax.rsqrt(jnp.mean(xv * xv, axis=-1, keepdims=True) + EPS)
            xn = xv * r
            dng[...] += jnp.sum(dhv * xn, axis=0, keepdims=True)
            dxn = dhv * g_ref[...]
            gx_ref[...] = dx2_buf[blk % 2] + r * (dxn - xn * jnp.mean(dxn * xn, axis=-1, keepdims=True))

        @at_step(n_steps - 1, nt - 1)
        def _():
            dng_buf[me] = dng[...]
            for cp in dngs:
                cp.start()
            for m in range(1, N_DEV):
                copy(SEM_SMALL + m, sm_buf.at[0], sm_buf.at[0], sibling).wait_recv()
            tot = sm_buf[0]
            for d in range(1, N_DEV):
                tot = tot + sm_buf[d]
            osm_ref[...] = tot
            for m in range(1, N_DEV):
                copy(SEM_NORM + m, dng_buf.at[0], dng_buf.at[0], sibling).wait_recv()
            tot = dng_buf[0]
            for d in range(1, N_DEV):
                tot = tot + dng_buf[d]
            osm_ref[0:1, :] = tot
            fin.wait_recv()
            for cp in d2d + [d2d_o] + via + list(ici.values()) + ici_o + [fin, fin_o] + smalls + dngs:
                cp.wait_send()
            store_o.wait()
            store_w.wait()

    def shard_of(s, kr):
        order = jnp.where(s < N_SHARD, s, jnp.where(s < norm_step, s - N_SHARD, 3))
        return kr[0] ^ (3 - order)

    def h_map(s, t, kr):
        return (jnp.where(s < N_SHARD, t, nt - 1), 0)

    def dp_map(s, t, kr):
        return (jnp.where(s < norm_step, t, nt - 1), shard_of(s, kr))

    def w_map(s, t, kr):
        return (shard_of(jnp.maximum(s, N_SHARD), kr), 0, 0)

    def blk_map(s, t, kr):
        return (jnp.where(s < norm_step, 0, (s - norm_step) * nt + t), 0)

    hbm = pl.BlockSpec(memory_space=pl.ANY)
    grid_spec = pltpu.PrefetchScalarGridSpec(
        num_scalar_prefetch=1, grid=(n_steps, nt),
        in_specs=[pl.BlockSpec((TT, D_MODEL), h_map),
                  pl.BlockSpec((TT, SHARD_COLS), dp_map),
                  pl.BlockSpec((1, D_MODEL, SHARD_COLS), w_map),
                  pl.BlockSpec((N_SHARD, WO_ROWS, D_MODEL), lambda s, t, kr: (0, 0, 0),
                               pipeline_mode=pl.Buffered(1)),
                  hbm, hbm,
                  pl.BlockSpec((1, D_MODEL), lambda s, t, kr: (0, 0)),
                  pl.BlockSpec((8, D_MODEL), lambda s, t, kr: (0, 0)),
                  pl.BlockSpec((8, D_MODEL), lambda s, t, kr: (0, 0))],
        out_specs=(pl.BlockSpec((TX, D_MODEL), blk_map), hbm, hbm,
                   pl.BlockSpec((8, D_MODEL), lambda s, t, kr: (0, 0))),
        scratch_shapes=[pltpu.VMEM((D_MODEL, SHARD_COLS), F32), pltpu.VMEM((SEQ, D_MODEL), F32),
                        pltpu.VMEM((N_SHARD, hw, SHARD_COLS), BF16), pltpu.VMEM((3, hw, SHARD_COLS), BF16),
                        pltpu.VMEM((N_SHARD, hw, SHARD_COLS), BF16), pltpu.VMEM((2, hw, SHARD_COLS), BF16),
                        pltpu.VMEM((2, hw // 2, SHARD_COLS), BF16),
                        pltpu.VMEM((N_SHARD, ho, D_MODEL), BF16), pltpu.VMEM((N_SHARD, ho, D_MODEL), BF16),
                        pltpu.VMEM((3, ho, D_MODEL), BF16), pltpu.VMEM((WO_ROWS, D_MODEL), F32),
                        pltpu.VMEM((N_DEV, 8, D_MODEL), F32), pltpu.VMEM((N_DEV, 1, D_MODEL), F32),
                        pltpu.VMEM((1, D_MODEL), F32),
                        pltpu.VMEM((2, TX, D_MODEL), F32), pltpu.VMEM((2, TX, D_MODEL), F32),
                        pltpu.SemaphoreType.DMA((N_SEM_TAIL,)), pltpu.SemaphoreType.DMA((N_SEM_TAIL,)),
                        pltpu.SemaphoreType.DMA((2,)), pltpu.SemaphoreType.DMA((4,))])
    return pl.pallas_call(
        body, name="bwd_tail", grid_spec=grid_spec,
        out_shape=(jax.ShapeDtypeStruct((SEQ, D_MODEL), F32),
                   jax.ShapeDtypeStruct((D_MODEL, SHARD_COLS), F32),
                   jax.ShapeDtypeStruct((WO_ROWS, D_MODEL), F32),
                   jax.ShapeDtypeStruct((8, D_MODEL), F32)),
        compiler_params=pltpu.CompilerParams(dimension_semantics=("arbitrary", "arbitrary"),
                                             vmem_limit_bytes=61 * 1024 * 1024, collective_id=COLLECTIVE_TAIL),
    )(kidx, h, dproj, wg, gwo, x2d, dx2, g1, small_a, small_b)


def _adam_update(w, g, m, v):
    nm = ADAM_B1 * m + (1.0 - ADAM_B1) * g
    nv = ADAM_B2 * v + (1.0 - ADAM_B2) * (g * g)
    m_hat = nm / (1.0 - ADAM_B1 ** ADAM_STEP)
    v_hat = nv / (1.0 - ADAM_B2 ** ADAM_STEP)
    return -ADAM_LR * (m_hat / (jnp.sqrt(v_hat) + ADAM_EPS) + ADAM_WD * w), nm, nv


def _adamw_all(tot, g_w_in, g_w_out, big, small, grad_x):
    n = len(small)
    rows = WO_ROWS
    steps = D_MODEL // rows

    def body(tot_ref, *refs):
        gx_ref, gx_out = refs[2 + 3 * (2 + n)], refs[-1]
        gx_out[...] = gx_ref[...]
        ins, outs = refs[:2 + 3 * (2 + n)], refs[3 + 3 * (2 + n):-1]
        g_refs, wmv = ins[:2], ins[2:]
        loss_ref, quads = outs[0], outs[1:]

        def update(j, g):
            w_ref, m_ref, v_ref = wmv[3 * j:3 * j + 3]
            g_ref, d_ref, nm_ref, nv_ref = quads[4 * j:4 * j + 4]
            g_ref[...] = g
            d_ref[...], nm_ref[...], nv_ref[...] = _adam_update(w_ref[...], g, m_ref[...], v_ref[...])

        update(0, g_refs[0][...])

        @pl.when(pl.program_id(0) == 0)
        def _():
            update(1, g_refs[1][...])
            k = 2 * lax.axis_index("x") + lax.axis_index("y")
            mine = pl.ds(pl.multiple_of(k * HEAD, HEAD), HEAD)
            loss_ref[...] = tot_ref[7:8, 0:1]
            grads = [tot_ref[0:1, :], tot_ref[1:2, :], tot_ref[2:3, 0:D_HGRN], tot_ref[2:3, D_HGRN:],
                     jnp.concatenate([tot_ref[3:4, 0:D_HGRN], tot_ref[3:4, D_HGRN:]], axis=0),
                     jnp.concatenate([tot_ref[4 + tap:5 + tap, mine] for tap in range(3)], axis=1)]
            for j, g in enumerate(grads):
                update(2 + j, g)

    whole = lambda a: pl.BlockSpec(a.shape, lambda i: (0, 0))
    blk = pl.BlockSpec((rows, SHARD_COLS), lambda i: (i, 0))
    arrays = [a for triple in big + small for a in triple]
    in_specs = ([whole(tot), blk, whole(g_w_out)] + [blk] * 3 + [whole(a) for a in arrays[3:]])
    shapes = [big[0][0], big[1][0]] + [w for w, _, _ in small]
    out_shape = (jax.ShapeDtypeStruct((1, 1), F32),) + tuple(
        jax.ShapeDtypeStruct(w.shape, F32) for w in shapes for _ in range(4))
    out_specs = (pl.BlockSpec((1, 1), lambda i: (0, 0)),) + (blk,) * 4 + tuple(
        whole(w) for w in shapes[1:] for _ in range(4))
    gx_blk = pl.BlockSpec((SEQ // steps, D_MODEL), lambda i: (i, 0))
    outs = pl.pallas_call(
        body, name="adamw_all", grid=(steps,),
        out_shape=out_shape + (jax.ShapeDtypeStruct(grad_x.shape, F32),),
        in_specs=in_specs + [gx_blk], out_specs=out_specs + (gx_blk,),
        compiler_params=pltpu.CompilerParams(dimension_semantics=("arbitrary",), vmem_limit_bytes=VMEM_LIMIT),
    )(tot, g_w_in, g_w_out, *arrays, grad_x)
    return [outs[0]] + [outs[1 + 4 * j:5 + 4 * j] for j in range(2 + n)] + [outs[-1]]


def _local_step(x2d, tgt, proj, lb_logits, cw, ga, gcn, w_out, gf):
    g64 = _group_matrix(HEAD, CONV_GROUP)
    aux, states, dx2, dmixed, gwo, part_out = _mix_out(proj, lb_logits, cw, ga, gcn, g64, w_out, x2d, gf, tgt)
    dproj, part_mix = _mix_bwd(proj, aux, states, dmixed, lb_logits, cw, ga, gcn, g64)
    return dproj, dx2, gwo.reshape(N_SHARD, WO_ROWS, D_MODEL), part_out, part_mix


def kernel(x, norm_gain, w_in, lb_logits, conv_w, hgrn_norm_gain, conv_norm_gain, w_out, final_norm_gain, loss_target, m_norm_gain, m_w_in, m_lb_logits, m_conv_w, m_hgrn_norm_gain, m_conv_norm_gain, m_w_out, m_final_norm_gain, v_norm_gain, v_w_in, v_lb_logits, v_conv_w, v_hgrn_norm_gain, v_conv_norm_gain, v_w_out, v_final_norm_gain):
    k = 2 * lax.axis_index("x") + lax.axis_index("y")
    kidx = jnp.reshape(k, (1,)).astype(jnp.int32)
    row = lambda a: a.reshape(1, D_MODEL)
    taps = lambda a: a.reshape(1, 3 * HEAD)
    h, proj, wg, cw = _gather_proj(kidx, x[0], norm_gain, w_in, taps(conv_w))
    dproj, dx2, gwo, part_out, part_mix = _local_step(
        x[0], loss_target[0], proj, lb_logits, cw, hgrn_norm_gain, conv_norm_gain, w_out, row(final_norm_gain))
    rgrad_x, rg_w_in, rg_w_out, tot = _bwd_tail(kidx, h, dproj, wg, gwo, x[0], dx2, norm_gain, part_out, part_mix)

    (loss, (g_w_in, d_w_in, nm_w_in, nv_w_in), (g_w_out, d_w_out, nm_w_out, nv_w_out),
     (g_norm_gain, d_ng, nm_ng, nv_ng), (g_final, d_fg, nm_fg, nv_fg), (g_hgrn, d_hg, nm_hg, nv_hg),
     (g_convn, d_cg, nm_cg, nv_cg), (g_lb, d_lb, nm_lb, nv_lb), (g_conv_w, d_cw, nm_cw, nv_cw),
     grad_x) = _adamw_all(
        tot, rg_w_in, rg_w_out,
        [(w_in[0], m_w_in[0], v_w_in[0]), (w_out[0], m_w_out[0], v_w_out[0])],
        [(norm_gain, m_norm_gain, v_norm_gain),
         (row(final_norm_gain), row(m_final_norm_gain), row(v_final_norm_gain)),
         (hgrn_norm_gain, m_hgrn_norm_gain, v_hgrn_norm_gain),
         (conv_norm_gain, m_conv_norm_gain, v_conv_norm_gain),
         (lb_logits, m_lb_logits, v_lb_logits),
         (taps(conv_w), taps(m_conv_w), taps(v_conv_w))],
        rgrad_x)
    flat = lambda a: a.reshape(D_MODEL)
    untap = lambda a: a.reshape(1, 3, HEAD)
    return (loss.reshape(()), grad_x[None],
            g_norm_gain, g_w_in[None], g_lb, untap(g_conv_w), g_hgrn, g_convn, g_w_out[None], flat(g_final),
            d_ng, d_w_in[None], d_lb, untap(d_cw), d_hg, d_cg, d_w_out[None], flat(d_fg),
            nm_ng, nm_w_in[None], nm_lb, untap(nm_cw), nm_hg, nm_cg, nm_w_out[None], flat(nm_fg),
            nv_ng, nv_w_in[None], nv_lb, untap(nv_cw), nv_hg, nv_cg, nv_w_out[None], flat(nv_fg))
```

```python
import jax
import jax.numpy as jnp
import numpy as np
from jax import lax
from jax.experimental import pallas as pl
from jax.experimental.pallas import tpu as pltpu

F32 = jnp.float32
BF16 = jnp.bfloat16
MESH = pl.DeviceIdType.MESH

SEQ = 2048
D_MODEL = 1024
D_HGRN = 512
D_CONV = 512
HEAD = 128
N_HEADS = 4
CHUNK = 64
CONV_GROUP = 64
N_SHARD = 4
SHARD_COLS = 1024
WO_ROWS = 256
EPS = 1e-6
TB = 256
NCB = TB // CHUNK
N_CHUNKS = SEQ // CHUNK
N_DEV = 8
COLLECTIVE_GATHER, COLLECTIVE_MIX_OUT, COLLECTIVE_TAIL = 1, 0, 2
AUX_O, AUX_CV, AUX_B, AUX_COLS = 0, 512, 1024, 1536

ADAM_LR = 0.001
ADAM_B1 = 0.9
ADAM_B2 = 0.999
ADAM_EPS = 1e-08
ADAM_WD = 0.01
ADAM_STEP = 10

VMEM_LIMIT = 56 * 1024 * 1024


def _dot(a, b):
    return jnp.dot(a, b, preferred_element_type=F32)


def _dot_nt(a, b):
    return lax.dot_general(a, b, (((1,), (1,)), ((), ())), preferred_element_type=F32)


def _dot_tn(a, b):
    return lax.dot_general(a, b, (((0,), (0,)), ((), ())), preferred_element_type=F32)


def _split_bf16(x, n):
    parts = []
    r = x
    for _ in range(n):
        p = r.astype(BF16)
        parts.append(p)
        r = r - p.astype(F32)
    return parts


def _exact_left(m, x, n=3):
    acc = None
    for p in _split_bf16(x, n):
        t = _dot(m, p)
        acc = t if acc is None else acc + t
    return acc


def _exact_left_many(m, xs, n=3):
    parts = [_split_bf16(x, n) for x in xs]
    accs = [None] * len(xs)
    for i in range(n):
        for j in range(len(xs)):
            t = _dot(m, parts[j][i])
            accs[j] = t if accs[j] is None else accs[j] + t
    return accs


def _group_mean_many(xs, gmat, n=2):
    parts = [_split_bf16(x, n) for x in xs]
    accs = [None] * len(xs)
    for i in range(n):
        for j in range(len(xs)):
            t = _dot(parts[j][i], gmat)
            accs[j] = t if accs[j] is None else accs[j] + t
    return accs


def _group_mean(x, gmat, n=2):
    w = gmat.shape[0]
    outs = []
    for c0 in range(0, x.shape[1], w):
        acc = None
        for p in _split_bf16(x[:, c0:c0 + w], n):
            t = _dot(p, gmat)
            acc = t if acc is None else acc + t
        outs.append(acc)
    return jnp.concatenate(outs, axis=1)


def _sigmoid(x):
    return 1.0 / (1.0 + jnp.exp(-x))


def _lower_bound(lbl):
    l0 = lbl[0:1, :]
    l1 = lbl[1:2, :]
    m = jnp.maximum(l0, l1)
    e0 = jnp.exp(l0 - m)
    e1 = jnp.exp(l1 - m)
    return e0 / (e0 + e1)


def _tri(lower):
    r = lax.broadcasted_iota(jnp.int32, (CHUNK, CHUNK), 0)
    c = lax.broadcasted_iota(jnp.int32, (CHUNK, CHUNK), 1)
    return jnp.where((c <= r) if lower else (c >= r), 1.0, 0.0).astype(BF16)


def _causal():
    r = lax.broadcasted_iota(jnp.int32, (CHUNK, CHUNK), 0)
    c = lax.broadcasted_iota(jnp.int32, (CHUNK, CHUNK), 1)
    return c <= r


def _shift_down(x, sh, prev_tail):
    r = pltpu.roll(x, sh, 0)
    pt = pltpu.roll(prev_tail, sh, 0)
    rows = lax.broadcasted_iota(jnp.int32, prev_tail.shape, 0)
    top = jnp.where(rows < sh, pt, r[0:8])
    return jnp.concatenate([top, r[8:]], axis=0)


def _shift_up(x, sh, next_head):
    n = x.shape[0]
    r = pltpu.roll(x, n - sh, 0)
    nh = pltpu.roll(next_head, 8 - sh, 0)
    rows = lax.broadcasted_iota(jnp.int32, next_head.shape, 0)
    bot = jnp.where(rows >= 8 - sh, nh, r[n - 8:])
    return jnp.concatenate([r[:n - 8], bot], axis=0)


def _group_matrix(width, group):
    r = np.arange(width)[:, None] // group
    c = np.arange(width)[None, :] // group
    return jnp.asarray(np.where(r == c, 1.0 / group, 0.0), dtype=BF16)


TG = 1024
SEM_W, SEM_CW, SEM_W_FWD, N_SEM = 0, 4, 7, 11


def _gather_proj(kidx, x2d, g1, w_in, conv_w):
    half_w = D_MODEL // 2
    half_c = SHARD_COLS // 2
    nt = SEQ // TG
    n_steps = 2 * N_SHARD

    def body(k_ref, x_ref, g_ref, w_ref, cw_ref, h_out, p_ref, wg_out, cwg_out,
             wg_v, cwg_v, h_ref, send_sems, recv_sems, out_sems):
        s, t = pl.program_id(0), pl.program_id(1)
        x, y, c = lax.axis_index("x"), lax.axis_index("y"), lax.axis_index("c")
        k = 2 * x + y
        sibling = (x, y, 1 - c)
        chips = [(1 - x, y), (x, 1 - y), (1 - x, 1 - y)]
        kjs = [2 * cx + cy for cx, cy in chips]
        diag = (*chips[2], c)

        def w_half(kk, cc):
            return wg_v.at[kk, pl.ds(cc * half_w, half_w), :]

        def w_quarter(kk, cc, piece):
            return wg_v.at[kk, pl.ds(cc * half_w, half_w), piece * half_c:(piece + 1) * half_c]

        def cw_of(kk):
            return cwg_v.at[:, pl.ds(pl.multiple_of(kk * HEAD, HEAD), HEAD)]

        def copy(sem, ref, to):
            return pltpu.make_async_remote_copy(
                src_ref=ref, dst_ref=ref, send_sem=send_sems.at[sem], recv_sem=recv_sems.at[sem],
                device_id=to, device_id_type=MESH)

        def at_step(sv, tv):
            return pl.when((s == sv) & (t == tv))

        w_direct = ([copy(SEM_W + j, w_half(k, c), (*chips[j], c)) for j in range(2)]
                    + [copy(SEM_W + 2 + p, w_quarter(k, c, p), diag) for p in range(2)])
        cw_direct = [copy(SEM_CW + j, cw_of(k), (*chip, c)) for j, chip in enumerate(chips)]
        w_passed = ([copy(SEM_W_FWD + j, w_half(kjs[j], c), sibling) for j in range(2)]
                    + [copy(SEM_W_FWD + 2 + p, w_quarter(kjs[2], c, p), sibling) for p in range(2)])
        stores = ([pltpu.make_async_copy(wg_v.at[kk], wg_out.at[kk], out_sems.at[i])
                   for i, kk in enumerate([k] + kjs)]
                  + [pltpu.make_async_copy(cwg_v, cwg_out, out_sems.at[4]),
                     pltpu.make_async_copy(h_ref, h_out, out_sems.at[5])])

        @at_step(0, 0)
        def _():
            barrier = pltpu.get_barrier_semaphore()
            for peer in [sibling] + [(*chip, c) for chip in chips]:
                pl.semaphore_signal(barrier, inc=1, device_id=peer, device_id_type=MESH)
            wg_v[k] = w_ref[0].astype(BF16)
            mine = pl.ds(pl.multiple_of(k * HEAD, HEAD), HEAD)
            cwg_v[:, mine] = jnp.zeros((8, HEAD), F32)
            for tap in range(3):
                cwg_v[tap:tap + 1, mine] = cw_ref[:, tap * HEAD:(tap + 1) * HEAD]
            pl.semaphore_wait(barrier, 4)
            for cp in w_direct + cw_direct:
                cp.start()
            stores[0].start()

        @at_step(1, 0)
        def _():
            stores[5].start()

        @at_step(2, 0)
        def _():
            for j in range(2):
                copy(SEM_W + j, w_half(kjs[j], c), sibling).wait_recv()
                w_passed[j].start()
            copy(SEM_W_FWD, w_half(kjs[0], 1 - c), sibling).wait_recv()
            stores[1].start()

        @at_step(4, 0)
        def _():
            copy(SEM_W_FWD + 1, w_half(kjs[1], 1 - c), sibling).wait_recv()
            stores[2].start()

        for p in range(2):
            @at_step(6 + p, 0)
            def _(p=p):
                copy(SEM_W + 2 + p, w_quarter(kjs[2], c, p), sibling).wait_recv()
                w_passed[2 + p].start()
                copy(SEM_W_FWD + 2 + p, w_quarter(kjs[2], 1 - c, p), sibling).wait_recv()
                if p == 1:
                    stores[3].start()
                    for j in range(3):
                        copy(SEM_CW + j, cw_of(kjs[j]), sibling).wait_recv()
                    stores[4].start()

        rows = pl.ds(pl.multiple_of(t * TG, TG), TG)

        @pl.when(s == 0)
        def _():
            xv = x_ref[...]
            r = lax.rsqrt(jnp.mean(xv * xv, axis=-1, keepdims=True) + EPS)
            h_ref[rows, :] = (xv * r * g_ref[...]).astype(BF16)

        sh = s >> 1
        js = k ^ (((sh & 1) << 1) | (sh >> 1))
        for piece in range(2):
            @pl.when((s & 1) == piece)
            def _(piece=piece):
                p_ref[...] = _dot(h_ref[rows, :], wg_v[js, :, piece * half_c:(piece + 1) * half_c])

        @at_step(n_steps - 1, nt - 1)
        def _():
            for cp in w_direct + cw_direct + w_passed:
                cp.wait_send()
            for st in stores:
                st.wait()

    def x_map(s, t, kr):
        return (jnp.where(s == 0, t, nt - 1), 0)

    def p_map(s, t, kr):
        sh = s >> 1
        return (t, 2 * (kr[0] ^ (((sh & 1) << 1) | (sh >> 1))) + (s & 1))

    hbm = pl.BlockSpec(memory_space=pl.ANY)
    grid_spec = pltpu.PrefetchScalarGridSpec(
        num_scalar_prefetch=1, grid=(n_steps, nt),
        in_specs=[pl.BlockSpec((TG, D_MODEL), x_map),
                  pl.BlockSpec((1, D_MODEL), lambda s, t, kr: (0, 0)),
                  pl.BlockSpec((1, D_MODEL, SHARD_COLS), lambda s, t, kr: (0, 0, 0)),
                  pl.BlockSpec((1, 3 * HEAD), lambda s, t, kr: (0, 0))],
        out_specs=(hbm, pl.BlockSpec((TG, half_c), p_map), hbm, hbm),
        scratch_shapes=[pltpu.VMEM((N_SHARD, D_MODEL, SHARD_COLS), BF16),
                        pltpu.VMEM((8, D_CONV), F32), pltpu.VMEM((SEQ, D_MODEL), BF16),
                        pltpu.SemaphoreType.DMA((N_SEM,)), pltpu.SemaphoreType.DMA((N_SEM,)),
                        pltpu.SemaphoreType.DMA((6,))])
    return pl.pallas_call(
        body, name="gather_proj", grid_spec=grid_spec,
        out_shape=(jax.ShapeDtypeStruct((SEQ, D_MODEL), BF16),
                   jax.ShapeDtypeStruct((SEQ, N_SHARD * SHARD_COLS), F32),
                   jax.ShapeDtypeStruct((N_SHARD, D_MODEL, SHARD_COLS), BF16),
                   jax.ShapeDtypeStruct((8, D_CONV), F32)),
        compiler_params=pltpu.CompilerParams(dimension_semantics=("arbitrary", "arbitrary"),
                                             vmem_limit_bytes=VMEM_LIMIT, collective_id=COLLECTIVE_GATHER),
    )(kidx, x2d, g1, w_in, conv_w)


LAG = 6


def _mix_out(proj, lb_logits, cw, ga, gcn, g64, w_out, x2d, gf, tgt):
    half_o = WO_ROWS // 2
    nblk = SEQ // TB
    n_steps = nblk + LAG

    def body(p_ref, lbl_ref, cw_ref, ga_ref, gcn_ref, g64_ref, wo_ref, x_ref, gf_ref, t_ref,
             aux_ref, sto_ref, dx2_ref, dm_ref, gwo_ref, part_ref,
             st_ref, tail_ref, wog_v, stage, ring, acc_ref, send_sems, recv_sems):
        i = pl.program_id(0)
        x, y, c = lax.axis_index("x"), lax.axis_index("y"), lax.axis_index("c")
        k = 2 * x + y
        sibling = (x, y, 1 - c)
        chips = [(1 - x, y), (x, 1 - y), (1 - x, 1 - y)]
        kjs = [2 * cx + cy for cx, cy in chips]

        def wo_half(kk, cc):
            return wog_v.at[pl.ds(pl.multiple_of(kk * WO_ROWS + cc * half_o, half_o), half_o), :]

        def copy(sem, ref, to):
            return pltpu.make_async_remote_copy(
                src_ref=ref, dst_ref=ref, send_sem=send_sems.at[sem], recv_sem=recv_sems.at[sem],
                device_id=to, device_id_type=MESH)

        wo_direct = [copy(j, wo_half(k, c), (*chip, c)) for j, chip in enumerate(chips)]
        wo_passed = [copy(3 + j, wo_half(kj, c), sibling) for j, kj in enumerate(kjs)]

        @pl.when(i == 0)
        def _():
            barrier = pltpu.get_barrier_semaphore()
            for peer in [sibling] + [(*chip, c) for chip in chips]:
                pl.semaphore_signal(barrier, inc=1, device_id=peer, device_id_type=MESH)
            st_ref[...] = jnp.zeros_like(st_ref)
            tail_ref[...] = jnp.zeros_like(tail_ref)
            acc_ref[...] = jnp.zeros_like(acc_ref)
            part_ref[...] = jnp.zeros_like(part_ref)
            wog_v[pl.ds(pl.multiple_of(k * WO_ROWS, WO_ROWS), WO_ROWS), :] = wo_ref[0].astype(BF16)
            pl.semaphore_wait(barrier, 4)
            for cp in wo_direct:
                cp.start()

        @pl.when(i == LAG - 1)
        def _():
            for j in range(3):
                copy(j, wo_half(kjs[j], c), sibling).wait_recv()
                wo_passed[j].start()

        @pl.when(i == LAG)
        def _():
            for j in range(3):
                copy(3 + j, wo_half(kjs[j], 1 - c), sibling).wait_recv()

        lb = _lower_bound(lbl_ref[...])
        tri = _tri(True)
        causal = _causal()
        g64m = g64_ref[...]
        heads = range(N_HEADS)
        cs = [slice(hd * HEAD, (hd + 1) * HEAD) for hd in heads]
        col = lambda base, hd: slice(base + hd * HEAD, base + (hd + 1) * HEAD)

        def mix_chunk(n):
            sl = pl.ds(n * CHUNK, CHUNK)
            sg = [_sigmoid(p_ref[sl, col(512, hd)]) for hd in heads]
            f = [lb[:, cs[hd]] + (1.0 - lb[:, cs[hd]]) * sg[hd] for hd in heads]
            bc = _exact_left_many(tri, [jnp.log(f[hd]) for hd in heads])
            for hd in heads:
                aux_ref[sl, col(AUX_B, hd)] = bc[hd]
            g = [bc[hd][CHUNK - 1:CHUNK, :] for hd in heads]
            qd = [(p_ref[sl, col(0, hd)] * jnp.exp(bc[hd])).astype(BF16) for hd in heads]
            kk = [1.0 - f[hd] for hd in heads]
            ki = [(kk[hd] * jnp.exp(-bc[hd])).astype(BF16) for hd in heads]
            ke = [(kk[hd] * jnp.exp(g[hd] - bc[hd])).astype(BF16) for hd in heads]
            vb = [p_ref[sl, col(1024, hd)].astype(BF16) for hd in heads]
            st = [st_ref[hd] for hd in heads]
            st_b = [a.astype(BF16) for a in st]
            for hd in heads:
                sto_ref[n, hd] = st_b[hd]
            scm = [_dot_nt(qd[hd], ki[hd]) for hd in heads]
            inter = [_dot_nt(qd[hd], st_b[hd]) for hd in heads]
            upd = [_dot_tn(vb[hd], ke[hd]) for hd in heads]
            intra = [_dot(jnp.where(causal, scm[hd], 0.0).astype(BF16), vb[hd]) for hd in heads]
            for hd in heads:
                st_ref[hd] = st[hd] * jnp.exp(g[hd]) + upd[hd]
                o = intra[hd] + inter[hd]
                aux_ref[sl, col(AUX_O, hd)] = o
                ra = lax.rsqrt(jnp.mean(o * o, axis=-1, keepdims=True) + EPS)
                za = p_ref[sl, col(1536, hd)]
                stage[sl, cs[hd]] = (o * ra * ga_ref[:, cs[hd]] * (za * _sigmoid(za))).astype(BF16)
            yb = []
            for hd in heads:
                cu = p_ref[sl, col(3072, hd)] * p_ref[sl, col(2048, hd)]
                tail = tail_ref[:, cs[hd]]
                cv = (cw_ref[0:1, cs[hd]] * _shift_down(cu, 2, tail) + cw_ref[1:2, cs[hd]] * _shift_down(cu, 1, tail)
                      + cw_ref[2:3, cs[hd]] * cu)
                tail_ref[:, cs[hd]] = cu[CHUNK - 8:, :]
                aux_ref[sl, col(AUX_CV, hd)] = cv
                yb.append(p_ref[sl, col(2560, hd)] * cv)
            ms = _group_mean_many([y * y for y in yb], g64m)
            for hd in heads:
                rb = lax.rsqrt(ms[hd] + EPS)
                zb = p_ref[sl, col(3584, hd)]
                stage[sl, col(512, hd)] = (yb[hd] * rb * gcn_ref[:, cs[hd]] * (zb * _sigmoid(zb))).astype(BF16)

        def step(mix, project):
            if project:
                mixed_b = ring[pl.ds(pl.multiple_of((i - LAG) * TB, TB), TB), :]
                y = _dot(mixed_b, wog_v[...])
            if mix:
                mix_chunk(0)
            if project:
                x2 = x_ref[...] + y
                r2 = lax.rsqrt(jnp.mean(x2 * x2, axis=-1, keepdims=True) + EPS)
                n2 = x2 * r2
                gfv = gf_ref[...]
                err = n2 * gfv - t_ref[...]
                loss = 0.5 * jnp.sum(jnp.mean(err * err, axis=-1, keepdims=True), axis=0, keepdims=True)
                dy = err * (1.0 / D_MODEL)
                part_ref[1:2, :] += jnp.sum(dy * n2, axis=0, keepdims=True)
                part_ref[7:8, :] += jnp.broadcast_to(loss, (1, D_MODEL))
                dn = dy * gfv
                dx2 = r2 * (dn - n2 * jnp.mean(dn * n2, axis=-1, keepdims=True))
                dx2_ref[...] = dx2
                dx2_b = dx2.astype(BF16)
            if mix:
                mix_chunk(1)
            if project:
                dm_ref[...] = _dot_nt(dx2_b, wog_v[...])
            if mix:
                mix_chunk(2)
            if project:
                acc_ref[...] += _dot_tn(mixed_b, dx2_b)
            if mix:
                mix_chunk(3)
                ring[pl.ds(pl.multiple_of(i * TB, TB), TB), :] = stage[...]

        @pl.when(i < LAG)
        def _():
            step(True, False)

        @pl.when((i >= LAG) & (i < nblk))
        def _():
            step(True, True)

        @pl.when(i >= nblk)
        def _():
            step(False, True)

        @pl.when(i == n_steps - 1)
        def _():
            gwo_ref[...] = acc_ref[...].astype(BF16)
            for cp in wo_direct + wo_passed:
                cp.wait_send()

    assert NCB == 4
    row = lambda w: pl.BlockSpec((1, w), lambda i: (0, 0))
    mix_blk = lambda i: jnp.minimum(i, nblk - 1)
    out_blk = lambda i: jnp.clip(i - LAG, 0, nblk - 1)
    tok = lambda: pl.BlockSpec((TB, D_MODEL), lambda i: (out_blk(i), 0))
    return pl.pallas_call(
        body, name="mix_out", grid=(n_steps,),
        out_shape=(jax.ShapeDtypeStruct((SEQ, AUX_COLS), F32),
                   jax.ShapeDtypeStruct((N_CHUNKS, N_HEADS, HEAD, HEAD), BF16),
                   jax.ShapeDtypeStruct((SEQ, D_MODEL), F32),
                   jax.ShapeDtypeStruct((SEQ, D_MODEL), F32),
                   jax.ShapeDtypeStruct((D_MODEL, D_MODEL), BF16),
                   jax.ShapeDtypeStruct((8, D_MODEL), F32)),
        in_specs=[pl.BlockSpec((TB, 4096), lambda i: (jnp.minimum(i, nblk - 1), 0)),
                  pl.BlockSpec((2, D_HGRN), lambda i: (0, 0)),
                  pl.BlockSpec((8, D_CONV), lambda i: (0, 0)),
                  row(D_HGRN), row(D_CONV),
                  pl.BlockSpec((HEAD, HEAD), lambda i: (0, 0)),
                  pl.BlockSpec((1, WO_ROWS, D_MODEL), lambda i: (0, 0, 0)),
                  tok(), row(D_MODEL), tok()],
        out_specs=(pl.BlockSpec((TB, AUX_COLS), lambda i: (mix_blk(i), 0)),
                   pl.BlockSpec((NCB, N_HEADS, HEAD, HEAD), lambda i: (mix_blk(i), 0, 0, 0)),
                   tok(), tok(),
                   pl.BlockSpec((D_MODEL, D_MODEL), lambda i: (0, 0)),
                   pl.BlockSpec((8, D_MODEL), lambda i: (0, 0))),
        scratch_shapes=[pltpu.VMEM((N_HEADS, HEAD, HEAD), F32), pltpu.VMEM((8, D_CONV), F32),
                        pltpu.VMEM((D_MODEL, D_MODEL), BF16), pltpu.VMEM((TB, D_MODEL), BF16),
                        pltpu.VMEM((SEQ, D_MODEL), BF16), pltpu.VMEM((D_MODEL, D_MODEL), F32),
                        pltpu.SemaphoreType.DMA((6,)), pltpu.SemaphoreType.DMA((6,))],
        compiler_params=pltpu.CompilerParams(dimension_semantics=("arbitrary",), vmem_limit_bytes=VMEM_LIMIT,
                                             collective_id=COLLECTIVE_MIX_OUT),
    )(proj, lb_logits, cw, ga, gcn, g64, w_out, x2d, gf, tgt)


def _mix_bwd(proj, aux, states, dmixed, lb_logits, cw, ga, gcn, g64):
    nblk = SEQ // TB

    def body(p_ref, aux_ref, st_ref, dm_ref, lbl_ref, cw_ref, ga_ref, gcn_ref, g64_ref,
             dp_ref, part_ref, dst_ref, head_ref, dlb_ref):
        i = pl.program_id(0)

        @pl.when(i == 0)
        def _():
            dst_ref[...] = jnp.zeros_like(dst_ref)
            head_ref[...] = jnp.zeros_like(head_ref)
            part_ref[...] = jnp.zeros_like(part_ref)
            dlb_ref[...] = jnp.zeros_like(dlb_ref)

        lb = _lower_bound(lbl_ref[...])
        triu = _tri(False)
        causal = _causal()
        g64m = g64_ref[...]
        rowsum = lambda a: jnp.sum(a, axis=0, keepdims=True)
        heads = range(N_HEADS)
        cs = [slice(hd * HEAD, (hd + 1) * HEAD) for hd in heads]
        col = lambda base, hd: slice(base + hd * HEAD, base + (hd + 1) * HEAD)
        for n in reversed(range(NCB)):
            sl = pl.ds(n * CHUNK, CHUNK)
            cvv = [aux_ref[sl, col(AUX_CV, hd)] for hd in heads]
            gb = [p_ref[sl, col(2560, hd)] for hd in heads]
            yb = [gb[hd] * cvv[hd] for hd in heads]
            ms = _group_mean_many([y * y for y in yb], g64m)
            rb, nb, dnb = [], [], []
            for hd in heads:
                rb.append(lax.rsqrt(ms[hd] + EPS))
                nb.append(yb[hd] * rb[hd])
                zb = p_ref[sl, col(3584, hd)]
                sgb = _sigmoid(zb)
                dmb = dm_ref[sl, col(512, hd)]
                silu = zb * sgb
                dgate = dmb * gcn_ref[:, cs[hd]]
                part_ref[2:3, col(512, hd)] += rowsum(dmb * nb[hd] * silu)
                dp_ref[sl, col(3584, hd)] = (dgate * nb[hd] * (sgb + silu * (1.0 - sgb))).astype(BF16)
                dnb.append(dgate * silu)
            mdn = _group_mean_many([dnb[hd] * nb[hd] for hd in heads], g64m)
            for hd in heads:
                dyb = rb[hd] * (dnb[hd] - nb[hd] * mdn[hd])
                dp_ref[sl, col(2560, hd)] = (dyb * cvv[hd]).astype(BF16)
                dcv = dyb * gb[hd]
                head = head_ref[:, cs[hd]]
                dcv1 = _shift_up(dcv, 1, head)
                dcv2 = _shift_up(dcv, 2, head)
                head_ref[:, cs[hd]] = dcv[0:8, :]
                u = p_ref[sl, col(2048, hd)]
                gc = p_ref[sl, col(3072, hd)]
                cu = gc * u
                part_ref[4:5, cs[hd]] += rowsum(dcv2 * cu)
                part_ref[5:6, cs[hd]] += rowsum(dcv1 * cu)
                part_ref[6:7, cs[hd]] += rowsum(dcv * cu)
                dcu = cw_ref[2:3, cs[hd]] * dcv + cw_ref[1:2, cs[hd]] * dcv1 + cw_ref[0:1, cs[hd]] * dcv2
                dp_ref[sl, col(3072, hd)] = (dcu * u).astype(BF16)
                dp_ref[sl, col(2048, hd)] = (dcu * gc).astype(BF16)
            do_b = []
            for hd in heads:
                ov = aux_ref[sl, col(AUX_O, hd)]
                ra = lax.rsqrt(jnp.mean(ov * ov, axis=-1, keepdims=True) + EPS)
                na = ov * ra
                za = p_ref[sl, col(1536, hd)]
                sga = _sigmoid(za)
                dma = dm_ref[sl, cs[hd]]
                silu = za * sga
                dgate = dma * ga_ref[:, cs[hd]]
                part_ref[2:3, cs[hd]] += rowsum(dma * na * silu)
                dp_ref[sl, col(1536, hd)] = (dgate * na * (sga + silu * (1.0 - sga))).astype(BF16)
                dna = dgate * silu
                do_b.append((ra * (dna - na * jnp.mean(dna * na, axis=-1, keepdims=True))).astype(BF16))
            s = [_sigmoid(p_ref[sl, col(512, hd)]) for hd in heads]
            f = [lb[:, cs[hd]] + (1.0 - lb[:, cs[hd]]) * s[hd] for hd in heads]
            bc = [aux_ref[sl, col(AUX_B, hd)] for hd in heads]
            g = [bc[hd][CHUNK - 1:CHUNK, :] for hd in heads]
            eb = [jnp.exp(bc[hd]) for hd in heads]
            enb = [jnp.exp(-bc[hd]) for hd in heads]
            eg = [jnp.exp(g[hd] - bc[hd]) for hd in heads]
            dec = [jnp.exp(g[hd]) for hd in heads]
            qd = [p_ref[sl, cs[hd]] * eb[hd] for hd in heads]
            kk = [1.0 - f[hd] for hd in heads]
            ki = [kk[hd] * enb[hd] for hd in heads]
            ke = [kk[hd] * eg[hd] for hd in heads]
            qd_b = [a.astype(BF16) for a in qd]
            ki_b = [a.astype(BF16) for a in ki]
            ke_b = [a.astype(BF16) for a in ke]
            vb = [p_ref[sl, col(1024, hd)].astype(BF16) for hd in heads]
            st_b = [st_ref[n, hd] for hd in heads]
            dst = [dst_ref[hd] for hd in heads]
            dst_b = [a.astype(BF16) for a in dst]
            scm = [_dot_nt(qd_b[hd], ki_b[hd]) for hd in heads]
            amm = [_dot_nt(do_b[hd], vb[hd]) for hd in heads]
            dqd2 = [_dot(do_b[hd], st_b[hd]) for hd in heads]
            dke = [_dot(vb[hd], dst_b[hd]) for hd in heads]
            dv2 = [_dot_nt(ke_b[hd], dst_b[hd]) for hd in heads]
            dsu = [_dot_tn(do_b[hd], qd_b[hd]) for hd in heads]
            sc = [jnp.where(causal, scm[hd], 0.0).astype(BF16) for hd in heads]
            am = [jnp.where(causal, amm[hd], 0.0).astype(BF16) for hd in heads]
            dqd1 = [_dot(am[hd], ki_b[hd]) for hd in heads]
            dki = [_dot_tn(am[hd], qd_b[hd]) for hd in heads]
            dv1 = [_dot_tn(sc[hd], do_b[hd]) for hd in heads]
            db, dgv, dkk = [], [], []
            for hd in heads:
                dqd = dqd1[hd] + dqd2[hd]
                ddec = rowsum(dst[hd] * st_b[hd].astype(F32))
                dst_ref[hd] = dst[hd] * dec[hd] + dsu[hd]
                dp_ref[sl, cs[hd]] = (dqd * eb[hd]).astype(BF16)
                dp_ref[sl, col(1024, hd)] = (dv1[hd] + dv2[hd]).astype(BF16)
                dke_eg = dke[hd] * eg[hd]
                dkk.append(dki[hd] * enb[hd] + dke_eg)
                db.append(dqd * qd[hd] - kk[hd] * dkk[hd])
                dgv.append(rowsum(kk[hd] * dke_eg) + ddec * dec[hd])
            rc = _exact_left_many(triu, db, 2)
            for hd in heads:
                df = (rc[hd] + dgv[hd]) / f[hd] - dkk[hd]
                one_s = 1.0 - s[hd]
                dlb_ref[:, cs[hd]] += rowsum(df * one_s)
                dp_ref[sl, col(512, hd)] = (df * (1.0 - lb[:, cs[hd]]) * s[hd] * one_s).astype(BF16)

        @pl.when(i == nblk - 1)
        def _():
            row = dlb_ref[...] * lb * (1.0 - lb)
            part_ref[3:4, 0:D_HGRN] = row
            part_ref[3:4, D_HGRN:] = -row

    rev = lambda w: pl.BlockSpec((TB, w), lambda i: (nblk - 1 - i, 0))
    row = lambda w: pl.BlockSpec((1, w), lambda i: (0, 0))
    return pl.pallas_call(
        body, name="mix_bwd", grid=(nblk,),
        out_shape=(jax.ShapeDtypeStruct((SEQ, 4096), BF16),
                   jax.ShapeDtypeStruct((8, D_MODEL), F32)),
        in_specs=[rev(4096), rev(AUX_COLS),
                  pl.BlockSpec((NCB, N_HEADS, HEAD, HEAD), lambda i: (nblk - 1 - i, 0, 0, 0)),
                  rev(D_MODEL),
                  pl.BlockSpec((2, D_HGRN), lambda i: (0, 0)),
                  pl.BlockSpec((8, D_CONV), lambda i: (0, 0)),
                  row(D_HGRN), row(D_CONV),
                  pl.BlockSpec((HEAD, HEAD), lambda i: (0, 0))],
        out_specs=(rev(4096), pl.BlockSpec((8, D_MODEL), lambda i: (0, 0))),
        scratch_shapes=[pltpu.VMEM((N_HEADS, HEAD, HEAD), F32), pltpu.VMEM((8, D_CONV), F32),
                        pltpu.VMEM((1, D_HGRN), F32)],
        compiler_params=pltpu.CompilerParams(dimension_semantics=("arbitrary",), vmem_limit_bytes=VMEM_LIMIT),
    )(proj, aux, states, dmixed, lb_logits, cw, ga, gcn, g64)


TT = 1024
TX = 512
(SEM_D2D, SEM_D2D_O, SEM_ICI, SEM_ICI_O, SEM_FIN, SEM_FIN_O, SEM_SMALL, SEM_VIA, SEM_NORM, N_SEM_TAIL) = (
    0, 4, 5, 8, 11, 12, 12, 20, 22, 30)


def _bwd_tail(kidx, h, dproj, wg, gwo, x2d, dx2, g1, small_a, small_b):
    hw = D_MODEL // 2
    ho = WO_ROWS // 2
    nt = SEQ // TT
    norm_step = 2 * N_SHARD
    n_steps = norm_step + SEQ // TX // nt

    def body(k_ref, h_ref, dp_ref, w_ref, gwo_ref, x_ref, dx2_ref, g_ref, sm_ref, smb_ref,
             gx_ref, gw_out, gwo_out, osm_ref,
             acc, dh, sendbuf, keep, sibrcv, rcv, merge, sib_o, p_o, rcv_o, res_o, sm_buf, dng_buf, dng,
             send_sems, recv_sems, out_sems):
        s, t = pl.program_id(0), pl.program_id(1)
        x, y, c = lax.axis_index("x"), lax.axis_index("y"), lax.axis_index("c")
        k = 2 * x + y
        me = 4 * x + 2 * y + c
        sibling = (x, y, 1 - c)
        chips = [(1 - x, 1 - y), (1 - x, y), (x, 1 - y)]
        kjs = [2 * cx + cy for cx, cy in chips]
        mine = pl.ds(pl.multiple_of(c * hw, hw), hw)
        other = pl.ds(pl.multiple_of((1 - c) * hw, hw), hw)
        mine_o = pl.ds(pl.multiple_of(c * ho, ho), ho)
        other_o = pl.ds(pl.multiple_of((1 - c) * ho, ho), ho)

        def copy(sem, src, dst, to):
            return pltpu.make_async_remote_copy(
                src_ref=src, dst_ref=dst, send_sem=send_sems.at[sem], recv_sem=recv_sems.at[sem],
                device_id=to, device_id_type=MESH)

        def at_step(sv, tv):
            return pl.when((s == sv) & (t == tv))

        def at_norm_block(b):
            return at_step(norm_step + b // nt, b % nt)

        d2d = [copy(SEM_D2D + sv, sendbuf.at[sv], sibrcv.at[sv], sibling) for sv in range(N_SHARD)]
        d2d_o = copy(SEM_D2D_O, gwo_ref.at[:, other_o, :], sib_o, sibling)
        ici = {sv: copy(SEM_ICI + sv, keep.at[sv], rcv.at[sv - 1], (*chips[sv], c)) for sv in (1, 2)}
        qh = hw // 2
        via = [copy(SEM_VIA, keep.at[0, 0:qh, :], merge.at[1], (*chips[1], c)),
               copy(SEM_VIA + 1, keep.at[0, qh:hw, :], merge.at[0], (*chips[2], c))]
        merged_rows = [slice(qh, hw), slice(0, qh)]
        ici_o = [copy(SEM_ICI_O + sv, p_o.at[kjs[sv]], rcv_o.at[sv], (*chips[sv], c)) for sv in range(3)]
        fin = copy(SEM_FIN, acc.at[mine, :], gw_out.at[mine, :], sibling)
        fin_o = copy(SEM_FIN_O, res_o.at[mine_o, :], res_o.at[mine_o, :], sibling)
        peers = [(x ^ (m >> 2), y ^ ((m >> 1) & 1), c ^ (m & 1)) for m in range(1, N_DEV)]
        smalls = [copy(SEM_SMALL + 1 + j, sm_buf.at[me], sm_buf.at[me], to) for j, to in enumerate(peers)]
        dngs = [copy(SEM_NORM + 1 + j, dng_buf.at[me], dng_buf.at[me], to) for j, to in enumerate(peers)]
        store_w = pltpu.make_async_copy(acc.at[mine, :], gw_out.at[mine, :], out_sems.at[0])
        store_o = pltpu.make_async_copy(res_o, gwo_out, out_sems.at[1])

        @at_step(0, 0)
        def _():
            barrier = pltpu.get_barrier_semaphore()
            for to in peers:
                pl.semaphore_signal(barrier, inc=1, device_id=to, device_id_type=MESH)
            sm_buf[me] = sm_ref[...] + smb_ref[...]
            pl.semaphore_wait(barrier, N_DEV - 1)
            d2d_o.start()
            for cp in smalls:
                cp.start()

        @at_step(0, 1)
        def _():
            d2d_o.wait_recv()
            for j in range(N_SHARD):
                p_o[j] = (gwo_ref[j, mine_o, :].astype(F32) + sib_o[j].astype(F32)).astype(BF16)
            res_o[mine_o, :] = gwo_ref[k, mine_o, :].astype(F32) + sib_o[k].astype(F32)
            for cp in ici_o:
                cp.start()

        rows = pl.ds(pl.multiple_of(t * TT, TT), TT)

        @pl.when((s < N_SHARD) & (t == 0))
        def _():
            acc[...] = _dot_tn(h_ref[...], dp_ref[...])

        @pl.when((s < N_SHARD) & (t > 0))
        def _():
            acc[...] += _dot_tn(h_ref[...], dp_ref[...])

        for sv in range(N_SHARD):
            @at_step(sv, nt - 1)
            def _(sv=sv):
                sendbuf[sv] = acc[other, :].astype(BF16)
                if sv < 3:
                    keep[sv] = acc[mine, :].astype(BF16)
                d2d[sv].start()

        @at_step(1, 0)
        def _():
            d2d[0].wait_recv()
            keep[0] = (keep[0].astype(F32) + sibrcv[0].astype(F32)).astype(BF16)
            for cp in via:
                cp.start()

        for sv in (1, 2):
            @at_step(sv + 2, 0)
            def _(sv=sv):
                d2d[sv].wait_recv()
                keep[sv] = (keep[sv].astype(F32) + sibrcv[sv].astype(F32)).astype(BF16)
                via[2 - sv].wait_recv()
                rows_m = merged_rows[sv - 1]
                keep[sv, rows_m, :] = (keep[sv, rows_m, :].astype(F32) + merge[sv - 1].astype(F32)).astype(BF16)
                ici[sv].start()

        @pl.when(s == N_SHARD)
        def _():
            dh[rows, :] = _dot_nt(dp_ref[...], w_ref[0])

        @pl.when((s > N_SHARD) & (s < norm_step))
        def _():
            dh[rows, :] += _dot_nt(dp_ref[...], w_ref[0])

        @at_norm_block(0)
        def _():
            d2d[3].wait_recv()
            acc[mine, :] += sibrcv[3].astype(F32)

        @at_norm_block(1)
        def _():
            tot = res_o[mine_o, :]
            for sv in range(3):
                ici_o[sv].wait_recv()
                tot = tot + rcv_o[sv].astype(F32)
            res_o[mine_o, :] = tot
            fin_o.start()

        @at_norm_block(2)
        def _():
            ici[1].wait_recv()
            acc[mine, :] += rcv[0].astype(F32)

        @at_norm_block(SEQ // TX - 2)
        def _():
            ici[2].wait_recv()
            acc[mine, :] += rcv[1].astype(F32)
            fin.start()
            store_w.start()
            fin_o.wait_recv()
            store_o.start()

        @at_norm_block(0)
        def _():
            dng[...] = jnp.zeros_like(dng)

        @pl.when(s >= norm_step)
        def _():
            blk = (s - norm_step) * nt + t
            dhv = dh[pl.ds(pl.multiple_of(blk * TX, TX), TX), :]
            xv = x_ref[...]
            r = lax.rsqrt(jnp.mean(xv * xv, axis=-1, keepdims=True) + EPS)
            xn = xv * r
            dng[...] += jnp.sum(dhv * xn, axis=0, keepdims=True)
            dxn = dhv * g_ref[...]
            gx_ref[...] = dx2_ref[...] + r * (dxn - xn * jnp.mean(dxn * xn, axis=-1, keepdims=True))

        @at_step(n_steps - 1, nt - 1)
        def _():
            dng_buf[me] = dng[...]
            for cp in dngs:
                cp.start()
            for m in range(1, N_DEV):
                copy(SEM_SMALL + m, sm_buf.at[0], sm_buf.at[0], sibling).wait_recv()
            tot = sm_buf[0]
            for d in range(1, N_DEV):
                tot = tot + sm_buf[d]
            osm_ref[...] = tot
            for m in range(1, N_DEV):
                copy(SEM_NORM + m, dng_buf.at[0], dng_buf.at[0], sibling).wait_recv()
            tot = dng_buf[0]
            for d in range(1, N_DEV):
                tot = tot + dng_buf[d]
            osm_ref[0:1, :] = tot
            fin.wait_recv()
            for cp in d2d + [d2d_o] + via + list(ici.values()) + ici_o + [fin, fin_o] + smalls + dngs:
                cp.wait_send()
            store_o.wait()
            store_w.wait()

    def shard_of(s, kr):
        order = jnp.where(s < N_SHARD, s, jnp.where(s < norm_step, s - N_SHARD, 3))
        return kr[0] ^ (3 - order)

    def h_map(s, t, kr):
        return (jnp.where(s < N_SHARD, t, nt - 1), 0)

    def dp_map(s, t, kr):
        return (jnp.where(s < norm_step, t, nt - 1), shard_of(s, kr))

    def w_map(s, t, kr):
        return (shard_of(jnp.maximum(s, N_SHARD), kr), 0, 0)

    def blk_map(s, t, kr):
        return (jnp.where(s < norm_step, 0, (s - norm_step) * nt + t), 0)

    hbm = pl.BlockSpec(memory_space=pl.ANY)
    grid_spec = pltpu.PrefetchScalarGridSpec(
        num_scalar_prefetch=1, grid=(n_steps, nt),
        in_specs=[pl.BlockSpec((TT, D_MODEL), h_map),
                  pl.BlockSpec((TT, SHARD_COLS), dp_map),
                  pl.BlockSpec((1, D_MODEL, SHARD_COLS), w_map),
                  pl.BlockSpec((N_SHARD, WO_ROWS, D_MODEL), lambda s, t, kr: (0, 0, 0),
                               pipeline_mode=pl.Buffered(1)),
                  pl.BlockSpec((TX, D_MODEL), blk_map),
                  pl.BlockSpec((TX, D_MODEL), blk_map),
                  pl.BlockSpec((1, D_MODEL), lambda s, t, kr: (0, 0)),
                  pl.BlockSpec((8, D_MODEL), lambda s, t, kr: (0, 0)),
                  pl.BlockSpec((8, D_MODEL), lambda s, t, kr: (0, 0))],
        out_specs=(pl.BlockSpec((TX, D_MODEL), blk_map), hbm, hbm,
                   pl.BlockSpec((8, D_MODEL), lambda s, t, kr: (0, 0))),
        scratch_shapes=[pltpu.VMEM((D_MODEL, SHARD_COLS), F32), pltpu.VMEM((SEQ, D_MODEL), F32),
                        pltpu.VMEM((N_SHARD, hw, SHARD_COLS), BF16), pltpu.VMEM((3, hw, SHARD_COLS), BF16),
                        pltpu.VMEM((N_SHARD, hw, SHARD_COLS), BF16), pltpu.VMEM((2, hw, SHARD_COLS), BF16),
                        pltpu.VMEM((2, hw // 2, SHARD_COLS), BF16),
                        pltpu.VMEM((N_SHARD, ho, D_MODEL), BF16), pltpu.VMEM((N_SHARD, ho, D_MODEL), BF16),
                        pltpu.VMEM((3, ho, D_MODEL), BF16), pltpu.VMEM((WO_ROWS, D_MODEL), F32),
                        pltpu.VMEM((N_DEV, 8, D_MODEL), F32), pltpu.VMEM((N_DEV, 1, D_MODEL), F32),
                        pltpu.VMEM((1, D_MODEL), F32),
                        pltpu.SemaphoreType.DMA((N_SEM_TAIL,)), pltpu.SemaphoreType.DMA((N_SEM_TAIL,)),
                        pltpu.SemaphoreType.DMA((2,))])
    return pl.pallas_call(
        body, name="bwd_tail", grid_spec=grid_spec,
        out_shape=(jax.ShapeDtypeStruct((SEQ, D_MODEL), F32),
                   jax.ShapeDtypeStruct((D_MODEL, SHARD_COLS), F32),
                   jax.ShapeDtypeStruct((WO_ROWS, D_MODEL), F32),
                   jax.ShapeDtypeStruct((8, D_MODEL), F32)),
        compiler_params=pltpu.CompilerParams(dimension_semantics=("arbitrary", "arbitrary"),
                                             vmem_limit_bytes=61 * 1024 * 1024, collective_id=COLLECTIVE_TAIL),
    )(kidx, h, dproj, wg, gwo, x2d, dx2, g1, small_a, small_b)


def _adam_update(w, g, m, v):
    nm = ADAM_B1 * m + (1.0 - ADAM_B1) * g
    nv = ADAM_B2 * v + (1.0 - ADAM_B2) * (g * g)
    m_hat = nm / (1.0 - ADAM_B1 ** ADAM_STEP)
    v_hat = nv / (1.0 - ADAM_B2 ** ADAM_STEP)
    return -ADAM_LR * (m_hat / (jnp.sqrt(v_hat) + ADAM_EPS) + ADAM_WD * w), nm, nv


def _adamw_all(tot, g_w_in, g_w_out, big, small, grad_x):
    n = len(small)
    rows = WO_ROWS
    steps = D_MODEL // rows

    def body(tot_ref, *refs):
        gx_ref, gx_out = refs[2 + 3 * (2 + n)], refs[-1]
        gx_out[...] = gx_ref[...]
        ins, outs = refs[:2 + 3 * (2 + n)], refs[3 + 3 * (2 + n):-1]
        g_refs, wmv = ins[:2], ins[2:]
        loss_ref, quads = outs[0], outs[1:]

        def update(j, g):
            w_ref, m_ref, v_ref = wmv[3 * j:3 * j + 3]
            g_ref, d_ref, nm_ref, nv_ref = quads[4 * j:4 * j + 4]
            g_ref[...] = g
            d_ref[...], nm_ref[...], nv_ref[...] = _adam_update(w_ref[...], g, m_ref[...], v_ref[...])

        update(0, g_refs[0][...])

        @pl.when(pl.program_id(0) == 0)
        def _():
            update(1, g_refs[1][...])
            k = 2 * lax.axis_index("x") + lax.axis_index("y")
            mine = pl.ds(pl.multiple_of(k * HEAD, HEAD), HEAD)
            loss_ref[...] = tot_ref[7:8, 0:1]
            grads = [tot_ref[0:1, :], tot_ref[1:2, :], tot_ref[2:3, 0:D_HGRN], tot_ref[2:3, D_HGRN:],
                     jnp.concatenate([tot_ref[3:4, 0:D_HGRN], tot_ref[3:4, D_HGRN:]], axis=0),
                     jnp.concatenate([tot_ref[4 + tap:5 + tap, mine] for tap in range(3)], axis=1)]
            for j, g in enumerate(grads):
                update(2 + j, g)

    whole = lambda a: pl.BlockSpec(a.shape, lambda i: (0, 0))
    blk = pl.BlockSpec((rows, SHARD_COLS), lambda i: (i, 0))
    arrays = [a for triple in big + small for a in triple]
    in_specs = ([whole(tot), blk, whole(g_w_out)] + [blk] * 3 + [whole(a) for a in arrays[3:]])
    shapes = [big[0][0], big[1][0]] + [w for w, _, _ in small]
    out_shape = (jax.ShapeDtypeStruct((1, 1), F32),) + tuple(
        jax.ShapeDtypeStruct(w.shape, F32) for w in shapes for _ in range(4))
    out_specs = (pl.BlockSpec((1, 1), lambda i: (0, 0)),) + (blk,) * 4 + tuple(
        whole(w) for w in shapes[1:] for _ in range(4))
    gx_blk = pl.BlockSpec((SEQ // steps, D_MODEL), lambda i: (i, 0))
    outs = pl.pallas_call(
        body, name="adamw_all", grid=(steps,),
        out_shape=out_shape + (jax.ShapeDtypeStruct(grad_x.shape, F32),),
        in_specs=in_specs + [gx_blk], out_specs=out_specs + (gx_blk,),
        compiler_params=pltpu.CompilerParams(dimension_semantics=("arbitrary",), vmem_limit_bytes=VMEM_LIMIT),
    )(tot, g_w_in, g_w_out, *arrays, grad_x)
    return [outs[0]] + [outs[1 + 4 * j:5 + 4 * j] for j in range(2 + n)] + [outs[-1]]


def _local_step(x2d, tgt, proj, lb_logits, cw, ga, gcn, w_out, gf):
    g64 = _group_matrix(HEAD, CONV_GROUP)
    aux, states, dx2, dmixed, gwo, part_out = _mix_out(proj, lb_logits, cw, ga, gcn, g64, w_out, x2d, gf, tgt)
    dproj, part_mix = _mix_bwd(proj, aux, states, dmixed, lb_logits, cw, ga, gcn, g64)
    return dproj, dx2, gwo.reshape(N_SHARD, WO_ROWS, D_MODEL), part_out, part_mix


def kernel(x, norm_gain, w_in, lb_logits, conv_w, hgrn_norm_gain, conv_norm_gain, w_out, final_norm_gain, loss_target, m_norm_gain, m_w_in, m_lb_logits, m_conv_w, m_hgrn_norm_gain, m_conv_norm_gain, m_w_out, m_final_norm_gain, v_norm_gain, v_w_in, v_lb_logits, v_conv_w, v_hgrn_norm_gain, v_conv_norm_gain, v_w_out, v_final_norm_gain):
    k = 2 * lax.axis_index("x") + lax.axis_index("y")
    kidx = jnp.reshape(k, (1,)).astype(jnp.int32)
    row = lambda a: a.reshape(1, D_MODEL)
    taps = lambda a: a.reshape(1, 3 * HEAD)
    h, proj, wg, cw = _gather_proj(kidx, x[0], norm_gain, w_in, taps(conv_w))
    dproj, dx2, gwo, part_out, part_mix = _local_step(
        x[0], loss_target[0], proj, lb_logits, cw, hgrn_norm_gain, conv_norm_gain, w_out, row(final_norm_gain))
    rgrad_x, rg_w_in, rg_w_out, tot = _bwd_tail(kidx, h, dproj, wg, gwo, x[0], dx2, norm_gain, part_out, part_mix)

    (loss, (g_w_in, d_w_in, nm_w_in, nv_w_in), (g_w_out, d_w_out, nm_w_out, nv_w_out),
     (g_norm_gain, d_ng, nm_ng, nv_ng), (g_final, d_fg, nm_fg, nv_fg), (g_hgrn, d_hg, nm_hg, nv_hg),
     (g_convn, d_cg, nm_cg, nv_cg), (g_lb, d_lb, nm_lb, nv_lb), (g_conv_w, d_cw, nm_cw, nv_cw),
     grad_x) = _adamw_all(
        tot, rg_w_in, rg_w_out,
        [(w_in[0], m_w_in[0], v_w_in[0]), (w_out[0], m_w_out[0], v_w_out[0])],
        [(norm_gain, m_norm_gain, v_norm_gain),
         (row(final_norm_gain), row(m_final_norm_gain), row(v_final_norm_gain)),
         (hgrn_norm_gain, m_hgrn_norm_gain, v_hgrn_norm_gain),
         (conv_norm_gain, m_conv_norm_gain, v_conv_norm_gain),
         (lb_logits, m_lb_logits, v_lb_logits),
         (taps(conv_w), taps(m_conv_w), taps(v_conv_w))],
        rgrad_x)
    flat = lambda a: a.reshape(D_MODEL)
    untap = lambda a: a.reshape(1, 3, HEAD)
    return (loss.reshape(()), grad_x[None],
            g_norm_gain, g_w_in[None], g_lb, untap(g_conv_w), g_hgrn, g_convn, g_w_out[None], flat(g_final),
            d_ng, d_w_in[None], d_lb, untap(d_cw), d_hg, d_cg, d_w_out[None], flat(d_fg),
            nm_ng, nm_w_in[None], nm_lb, untap(nm_cw), nm_hg, nm_cg, nm_w_out[None], flat(nm_fg),
            nv_ng, nv_w_in[None], nv_lb, untap(nv_cw), nv_hg, nv_cg, nv_w_out[None], flat(nv_fg))
```

```python
import jax
import jax.numpy as jnp
import numpy as np
from jax import lax
from jax.experimental import pallas as pl
from jax.experimental.pallas import tpu as pltpu

F32 = jnp.float32
BF16 = jnp.bfloat16
MESH = pl.DeviceIdType.MESH

SEQ = 2048
D_MODEL = 1024
D_HGRN = 512
D_CONV = 512
HEAD = 128
N_HEADS = 4
CHUNK = 64
CONV_GROUP = 64
N_SHARD = 4
SHARD_COLS = 1024
WO_ROWS = 256
EPS = 1e-6
TB = 256
NCB = TB // CHUNK
N_CHUNKS = SEQ // CHUNK
N_DEV = 8
COLLECTIVE_GATHER, COLLECTIVE_MIX_OUT, COLLECTIVE_TAIL = 1, 0, 2
AUX_O, AUX_CV, AUX_B, AUX_COLS = 0, 512, 1024, 1536

ADAM_LR = 0.001
ADAM_B1 = 0.9
ADAM_B2 = 0.999
ADAM_EPS = 1e-08
ADAM_WD = 0.01
ADAM_STEP = 10

VMEM_LIMIT = 56 * 1024 * 1024


def _dot(a, b):
    return jnp.dot(a, b, preferred_element_type=F32)


def _dot_nt(a, b):
    return lax.dot_general(a, b, (((1,), (1,)), ((), ())), preferred_element_type=F32)


def _dot_tn(a, b):
    return lax.dot_general(a, b, (((0,), (0,)), ((), ())), preferred_element_type=F32)


def _split_bf16(x, n):
    parts = []
    r = x
    for _ in range(n):
        p = r.astype(BF16)
        parts.append(p)
        r = r - p.astype(F32)
    return parts


def _exact_left(m, x, n=3):
    acc = None
    for p in _split_bf16(x, n):
        t = _dot(m, p)
        acc = t if acc is None else acc + t
    return acc


def _exact_left_many(m, xs, n=3):
    parts = [_split_bf16(x, n) for x in xs]
    accs = [None] * len(xs)
    for i in range(n):
        for j in range(len(xs)):
            t = _dot(m, parts[j][i])
            accs[j] = t if accs[j] is None else accs[j] + t
    return accs


def _group_mean_many(xs, gmat, n=2):
    parts = [_split_bf16(x, n) for x in xs]
    accs = [None] * len(xs)
    for i in range(n):
        for j in range(len(xs)):
            t = _dot(parts[j][i], gmat)
            accs[j] = t if accs[j] is None else accs[j] + t
    return accs


def _group_mean(x, gmat, n=2):
    w = gmat.shape[0]
    outs = []
    for c0 in range(0, x.shape[1], w):
        acc = None
        for p in _split_bf16(x[:, c0:c0 + w], n):
            t = _dot(p, gmat)
            acc = t if acc is None else acc + t
        outs.append(acc)
    return jnp.concatenate(outs, axis=1)


def _sigmoid(x):
    return 1.0 / (1.0 + jnp.exp(-x))


def _lower_bound(lbl):
    l0 = lbl[0:1, :]
    l1 = lbl[1:2, :]
    m = jnp.maximum(l0, l1)
    e0 = jnp.exp(l0 - m)
    e1 = jnp.exp(l1 - m)
    return e0 / (e0 + e1)


def _tri(lower):
    r = lax.broadcasted_iota(jnp.int32, (CHUNK, CHUNK), 0)
    c = lax.broadcasted_iota(jnp.int32, (CHUNK, CHUNK), 1)
    return jnp.where((c <= r) if lower else (c >= r), 1.0, 0.0).astype(BF16)


def _causal():
    r = lax.broadcasted_iota(jnp.int32, (CHUNK, CHUNK), 0)
    c = lax.broadcasted_iota(jnp.int32, (CHUNK, CHUNK), 1)
    return c <= r


def _shift_down(x, sh, prev_tail):
    r = pltpu.roll(x, sh, 0)
    pt = pltpu.roll(prev_tail, sh, 0)
    rows = lax.broadcasted_iota(jnp.int32, prev_tail.shape, 0)
    top = jnp.where(rows < sh, pt, r[0:8])
    return jnp.concatenate([top, r[8:]], axis=0)


def _shift_up(x, sh, next_head):
    n = x.shape[0]
    r = pltpu.roll(x, n - sh, 0)
    nh = pltpu.roll(next_head, 8 - sh, 0)
    rows = lax.broadcasted_iota(jnp.int32, next_head.shape, 0)
    bot = jnp.where(rows >= 8 - sh, nh, r[n - 8:])
    return jnp.concatenate([r[:n - 8], bot], axis=0)


def _group_matrix(width, group):
    r = np.arange(width)[:, None] // group
    c = np.arange(width)[None, :] // group
    return jnp.asarray(np.where(r == c, 1.0 / group, 0.0), dtype=BF16)


TG = 1024
SEM_W, SEM_CW, SEM_W_FWD, N_SEM = 0, 4, 7, 11


def _gather_proj(kidx, x2d, g1, w_in, conv_w):
    half_w = D_MODEL // 2
    half_c = SHARD_COLS // 2
    nt = SEQ // TG
    n_steps = 2 * N_SHARD

    def body(k_ref, x_ref, g_ref, w_ref, cw_ref, h_out, p_ref, wg_out, cwg_out,
             wg_v, cwg_v, h_ref, send_sems, recv_sems, out_sems):
        s, t = pl.program_id(0), pl.program_id(1)
        x, y, c = lax.axis_index("x"), lax.axis_index("y"), lax.axis_index("c")
        k = 2 * x + y
        sibling = (x, y, 1 - c)
        chips = [(1 - x, y), (x, 1 - y), (1 - x, 1 - y)]
        kjs = [2 * cx + cy for cx, cy in chips]
        diag = (*chips[2], c)

        def w_half(kk, cc):
            return wg_v.at[kk, pl.ds(cc * half_w, half_w), :]

        def w_quarter(kk, cc, piece):
            return wg_v.at[kk, pl.ds(cc * half_w, half_w), piece * half_c:(piece + 1) * half_c]

        def cw_of(kk):
            return cwg_v.at[:, pl.ds(pl.multiple_of(kk * HEAD, HEAD), HEAD)]

        def copy(sem, ref, to):
            return pltpu.make_async_remote_copy(
                src_ref=ref, dst_ref=ref, send_sem=send_sems.at[sem], recv_sem=recv_sems.at[sem],
                device_id=to, device_id_type=MESH)

        def at_step(sv, tv):
            return pl.when((s == sv) & (t == tv))

        w_direct = ([copy(SEM_W + j, w_half(k, c), (*chips[j], c)) for j in range(2)]
                    + [copy(SEM_W + 2 + p, w_quarter(k, c, p), diag) for p in range(2)])
        cw_direct = [copy(SEM_CW + j, cw_of(k), (*chip, c)) for j, chip in enumerate(chips)]
        w_passed = ([copy(SEM_W_FWD + j, w_half(kjs[j], c), sibling) for j in range(2)]
                    + [copy(SEM_W_FWD + 2 + p, w_quarter(kjs[2], c, p), sibling) for p in range(2)])
        stores = ([pltpu.make_async_copy(wg_v.at[kk], wg_out.at[kk], out_sems.at[i])
                   for i, kk in enumerate([k] + kjs)]
                  + [pltpu.make_async_copy(cwg_v, cwg_out, out_sems.at[4]),
                     pltpu.make_async_copy(h_ref, h_out, out_sems.at[5])])

        @at_step(0, 0)
        def _():
            barrier = pltpu.get_barrier_semaphore()
            for peer in [sibling] + [(*chip, c) for chip in chips]:
                pl.semaphore_signal(barrier, inc=1, device_id=peer, device_id_type=MESH)
            wg_v[k] = w_ref[0].astype(BF16)
            mine = pl.ds(pl.multiple_of(k * HEAD, HEAD), HEAD)
            cwg_v[:, mine] = jnp.zeros((8, HEAD), F32)
            for tap in range(3):
                cwg_v[tap:tap + 1, mine] = cw_ref[:, tap * HEAD:(tap + 1) * HEAD]
            pl.semaphore_wait(barrier, 4)
            for cp in w_direct + cw_direct:
                cp.start()
            stores[0].start()

        @at_step(1, 0)
        def _():
            stores[5].start()

        @at_step(2, 0)
        def _():
            for j in range(2):
                copy(SEM_W + j, w_half(kjs[j], c), sibling).wait_recv()
                w_passed[j].start()
            copy(SEM_W_FWD, w_half(kjs[0], 1 - c), sibling).wait_recv()
            stores[1].start()

        @at_step(4, 0)
        def _():
            copy(SEM_W_FWD + 1, w_half(kjs[1], 1 - c), sibling).wait_recv()
            stores[2].start()

        for p in range(2):
            @at_step(6 + p, 0)
            def _(p=p):
                copy(SEM_W + 2 + p, w_quarter(kjs[2], c, p), sibling).wait_recv()
                w_passed[2 + p].start()
                copy(SEM_W_FWD + 2 + p, w_quarter(kjs[2], 1 - c, p), sibling).wait_recv()
                if p == 1:
                    stores[3].start()
                    for j in range(3):
                        copy(SEM_CW + j, cw_of(kjs[j]), sibling).wait_recv()
                    stores[4].start()

        rows = pl.ds(pl.multiple_of(t * TG, TG), TG)

        @pl.when(s == 0)
        def _():
            xv = x_ref[...]
            r = lax.rsqrt(jnp.mean(xv * xv, axis=-1, keepdims=True) + EPS)
            h_ref[rows, :] = (xv * r * g_ref[...]).astype(BF16)

        sh = s >> 1
        js = k ^ (((sh & 1) << 1) | (sh >> 1))
        for piece in range(2):
            @pl.when((s & 1) == piece)
            def _(piece=piece):
                p_ref[...] = _dot(h_ref[rows, :], wg_v[js, :, piece * half_c:(piece + 1) * half_c])

        @at_step(n_steps - 1, nt - 1)
        def _():
            for cp in w_direct + cw_direct + w_passed:
                cp.wait_send()
            for st in stores:
                st.wait()

    def x_map(s, t, kr):
        return (jnp.where(s == 0, t, nt - 1), 0)

    def p_map(s, t, kr):
        sh = s >> 1
        return (t, 2 * (kr[0] ^ (((sh & 1) << 1) | (sh >> 1))) + (s & 1))

    hbm = pl.BlockSpec(memory_space=pl.ANY)
    grid_spec = pltpu.PrefetchScalarGridSpec(
        num_scalar_prefetch=1, grid=(n_steps, nt),
        in_specs=[pl.BlockSpec((TG, D_MODEL), x_map),
                  pl.BlockSpec((1, D_MODEL), lambda s, t, kr: (0, 0)),
                  pl.BlockSpec((1, D_MODEL, SHARD_COLS), lambda s, t, kr: (0, 0, 0)),
                  pl.BlockSpec((1, 3 * HEAD), lambda s, t, kr: (0, 0))],
        out_specs=(hbm, pl.BlockSpec((TG, half_c), p_map), hbm, hbm),
        scratch_shapes=[pltpu.VMEM((N_SHARD, D_MODEL, SHARD_COLS), BF16),
                        pltpu.VMEM((8, D_CONV), F32), pltpu.VMEM((SEQ, D_MODEL), BF16),
                        pltpu.SemaphoreType.DMA((N_SEM,)), pltpu.SemaphoreType.DMA((N_SEM,)),
                        pltpu.SemaphoreType.DMA((6,))])
    return pl.pallas_call(
        body, name="gather_proj", grid_spec=grid_spec,
        out_shape=(jax.ShapeDtypeStruct((SEQ, D_MODEL), BF16),
                   jax.ShapeDtypeStruct((SEQ, N_SHARD * SHARD_COLS), F32),
                   jax.ShapeDtypeStruct((N_SHARD, D_MODEL, SHARD_COLS), BF16),
                   jax.ShapeDtypeStruct((8, D_CONV), F32)),
        compiler_params=pltpu.CompilerParams(dimension_semantics=("arbitrary", "arbitrary"),
                                             vmem_limit_bytes=VMEM_LIMIT, collective_id=COLLECTIVE_GATHER),
    )(kidx, x2d, g1, w_in, conv_w)


LAG = 6


def _mix_out(proj, lb_logits, cw, ga, gcn, g64, w_out, x2d, gf, tgt):
    half_o = WO_ROWS // 2
    nblk = SEQ // TB
    n_steps = nblk + LAG

    def body(p_ref, lbl_ref, cw_ref, ga_ref, gcn_ref, g64_ref, wo_ref, x_ref, gf_ref, t_ref,
             aux_ref, sto_ref, dx2_ref, dm_ref, gwo_ref, part_ref,
             st_ref, tail_ref, wog_v, stage, ring, acc_ref, send_sems, recv_sems):
        i = pl.program_id(0)
        x, y, c = lax.axis_index("x"), lax.axis_index("y"), lax.axis_index("c")
        k = 2 * x + y
        sibling = (x, y, 1 - c)
        chips = [(1 - x, y), (x, 1 - y), (1 - x, 1 - y)]
        kjs = [2 * cx + cy for cx, cy in chips]

        def wo_half(kk, cc):
            return wog_v.at[pl.ds(pl.multiple_of(kk * WO_ROWS + cc * half_o, half_o), half_o), :]

        def copy(sem, ref, to):
            return pltpu.make_async_remote_copy(
                src_ref=ref, dst_ref=ref, send_sem=send_sems.at[sem], recv_sem=recv_sems.at[sem],
                device_id=to, device_id_type=MESH)

        wo_direct = [copy(j, wo_half(k, c), (*chip, c)) for j, chip in enumerate(chips)]
        wo_passed = [copy(3 + j, wo_half(kj, c), sibling) for j, kj in enumerate(kjs)]

        @pl.when(i == 0)
        def _():
            barrier = pltpu.get_barrier_semaphore()
            for peer in [sibling] + [(*chip, c) for chip in chips]:
                pl.semaphore_signal(barrier, inc=1, device_id=peer, device_id_type=MESH)
            st_ref[...] = jnp.zeros_like(st_ref)
            tail_ref[...] = jnp.zeros_like(tail_ref)
            acc_ref[...] = jnp.zeros_like(acc_ref)
            part_ref[...] = jnp.zeros_like(part_ref)
            wog_v[pl.ds(pl.multiple_of(k * WO_ROWS, WO_ROWS), WO_ROWS), :] = wo_ref[0].astype(BF16)

        @pl.when(i == LAG - 1)
        def _():
            for j in range(3):
                copy(j, wo_half(kjs[j], c), sibling).wait_recv()
                wo_passed[j].start()

        @pl.when(i == LAG)
        def _():
            for j in range(3):
                copy(3 + j, wo_half(kjs[j], 1 - c), sibling).wait_recv()

        lb = _lower_bound(lbl_ref[...])
        tri = _tri(True)
        causal = _causal()
        g64m = g64_ref[...]
        heads = range(N_HEADS)
        cs = [slice(hd * HEAD, (hd + 1) * HEAD) for hd in heads]
        col = lambda base, hd: slice(base + hd * HEAD, base + (hd + 1) * HEAD)

        def mix_chunk(n):
            sl = pl.ds(n * CHUNK, CHUNK)
            sg = [_sigmoid(p_ref[sl, col(512, hd)]) for hd in heads]
            f = [lb[:, cs[hd]] + (1.0 - lb[:, cs[hd]]) * sg[hd] for hd in heads]
            bc = _exact_left_many(tri, [jnp.log(f[hd]) for hd in heads])
            for hd in heads:
                aux_ref[sl, col(AUX_B, hd)] = bc[hd]
            g = [bc[hd][CHUNK - 1:CHUNK, :] for hd in heads]
            qd = [(p_ref[sl, col(0, hd)] * jnp.exp(bc[hd])).astype(BF16) for hd in heads]
            kk = [1.0 - f[hd] for hd in heads]
            ki = [(kk[hd] * jnp.exp(-bc[hd])).astype(BF16) for hd in heads]
            ke = [(kk[hd] * jnp.exp(g[hd] - bc[hd])).astype(BF16) for hd in heads]
            vb = [p_ref[sl, col(1024, hd)].astype(BF16) for hd in heads]
            st = [st_ref[hd] for hd in heads]
            st_b = [a.astype(BF16) for a in st]
            for hd in heads:
                sto_ref[n, hd] = st_b[hd]
            scm = [_dot_nt(qd[hd], ki[hd]) for hd in heads]
            inter = [_dot_nt(qd[hd], st_b[hd]) for hd in heads]
            upd = [_dot_tn(vb[hd], ke[hd]) for hd in heads]
            intra = [_dot(jnp.where(causal, scm[hd], 0.0).astype(BF16), vb[hd]) for hd in heads]
            for hd in heads:
                st_ref[hd] = st[hd] * jnp.exp(g[hd]) + upd[hd]
                o = intra[hd] + inter[hd]
                aux_ref[sl, col(AUX_O, hd)] = o
                ra = lax.rsqrt(jnp.mean(o * o, axis=-1, keepdims=True) + EPS)
                za = p_ref[sl, col(1536, hd)]
                stage[sl, cs[hd]] = (o * ra * ga_ref[:, cs[hd]] * (za * _sigmoid(za))).astype(BF16)
            yb = []
            for hd in heads:
                cu = p_ref[sl, col(3072, hd)] * p_ref[sl, col(2048, hd)]
                tail = tail_ref[:, cs[hd]]
                cv = (cw_ref[0:1, cs[hd]] * _shift_down(cu, 2, tail) + cw_ref[1:2, cs[hd]] * _shift_down(cu, 1, tail)
                      + cw_ref[2:3, cs[hd]] * cu)
                tail_ref[:, cs[hd]] = cu[CHUNK - 8:, :]
                aux_ref[sl, col(AUX_CV, hd)] = cv
                yb.append(p_ref[sl, col(2560, hd)] * cv)
            ms = _group_mean_many([y * y for y in yb], g64m)
            for hd in heads:
                rb = lax.rsqrt(ms[hd] + EPS)
                zb = p_ref[sl, col(3584, hd)]
                stage[sl, col(512, hd)] = (yb[hd] * rb * gcn_ref[:, cs[hd]] * (zb * _sigmoid(zb))).astype(BF16)

        def step(mix, project, entry=False):
            if project:
                mixed_b = ring[pl.ds(pl.multiple_of((i - LAG) * TB, TB), TB), :]
                y = _dot(mixed_b, wog_v[...])
            if mix:
                mix_chunk(0)
            if entry:
                pl.semaphore_wait(pltpu.get_barrier_semaphore(), 4)
                for cp in wo_direct:
                    cp.start()
            if project:
                x2 = x_ref[...] + y
                r2 = lax.rsqrt(jnp.mean(x2 * x2, axis=-1, keepdims=True) + EPS)
                n2 = x2 * r2
                gfv = gf_ref[...]
                err = n2 * gfv - t_ref[...]
                loss = 0.5 * jnp.sum(jnp.mean(err * err, axis=-1, keepdims=True), axis=0, keepdims=True)
                dy = err * (1.0 / D_MODEL)
                part_ref[1:2, :] += jnp.sum(dy * n2, axis=0, keepdims=True)
                part_ref[7:8, :] += jnp.broadcast_to(loss, (1, D_MODEL))
                dn = dy * gfv
                dx2 = r2 * (dn - n2 * jnp.mean(dn * n2, axis=-1, keepdims=True))
                dx2_ref[...] = dx2
                dx2_b = dx2.astype(BF16)
            if mix:
                mix_chunk(1)
            if project:
                dm_ref[...] = _dot_nt(dx2_b, wog_v[...])
            if mix:
                mix_chunk(2)
            if project:
                acc_ref[...] += _dot_tn(mixed_b, dx2_b)
            if mix:
                mix_chunk(3)
                ring[pl.ds(pl.multiple_of(i * TB, TB), TB), :] = stage[...]

        @pl.when(i == 0)
        def _():
            step(True, False, entry=True)

        @pl.when((i > 0) & (i < LAG))
        def _():
            step(True, False)

        @pl.when((i >= LAG) & (i < nblk))
        def _():
            step(True, True)

        @pl.when(i >= nblk)
        def _():
            step(False, True)

        @pl.when(i == n_steps - 1)
        def _():
            gwo_ref[...] = acc_ref[...].astype(BF16)
            for cp in wo_direct + wo_passed:
                cp.wait_send()

    assert NCB == 4
    row = lambda w: pl.BlockSpec((1, w), lambda i: (0, 0))
    mix_blk = lambda i: jnp.minimum(i, nblk - 1)
    out_blk = lambda i: jnp.clip(i - LAG, 0, nblk - 1)
    tok = lambda: pl.BlockSpec((TB, D_MODEL), lambda i: (out_blk(i), 0))
    return pl.pallas_call(
        body, name="mix_out", grid=(n_steps,),
        out_shape=(jax.ShapeDtypeStruct((SEQ, AUX_COLS), F32),
                   jax.ShapeDtypeStruct((N_CHUNKS, N_HEADS, HEAD, HEAD), BF16),
                   jax.ShapeDtypeStruct((SEQ, D_MODEL), F32),
                   jax.ShapeDtypeStruct((SEQ, D_MODEL), F32),
                   jax.ShapeDtypeStruct((D_MODEL, D_MODEL), BF16),
                   jax.ShapeDtypeStruct((8, D_MODEL), F32)),
        in_specs=[pl.BlockSpec((TB, 4096), lambda i: (jnp.minimum(i, nblk - 1), 0)),
                  pl.BlockSpec((2, D_HGRN), lambda i: (0, 0)),
                  pl.BlockSpec((8, D_CONV), lambda i: (0, 0)),
                  row(D_HGRN), row(D_CONV),
                  pl.BlockSpec((HEAD, HEAD), lambda i: (0, 0)),
                  pl.BlockSpec((1, WO_ROWS, D_MODEL), lambda i: (0, 0, 0)),
                  tok(), row(D_MODEL), tok()],
        out_specs=(pl.BlockSpec((TB, AUX_COLS), lambda i: (mix_blk(i), 0)),
                   pl.BlockSpec((NCB, N_HEADS, HEAD, HEAD), lambda i: (mix_blk(i), 0, 0, 0)),
                   tok(), tok(),
                   pl.BlockSpec((D_MODEL, D_MODEL), lambda i: (0, 0)),
                   pl.BlockSpec((8, D_MODEL), lambda i: (0, 0))),
        scratch_shapes=[pltpu.VMEM((N_HEADS, HEAD, HEAD), F32), pltpu.VMEM((8, D_CONV), F32),
                        pltpu.VMEM((D_MODEL, D_MODEL), BF16), pltpu.VMEM((TB, D_MODEL), BF16),
                        pltpu.VMEM((SEQ, D_MODEL), BF16), pltpu.VMEM((D_MODEL, D_MODEL), F32),
                        pltpu.SemaphoreType.DMA((6,)), pltpu.SemaphoreType.DMA((6,))],
        compiler_params=pltpu.CompilerParams(dimension_semantics=("arbitrary",), vmem_limit_bytes=VMEM_LIMIT,
                                             collective_id=COLLECTIVE_MIX_OUT),
    )(proj, lb_logits, cw, ga, gcn, g64, w_out, x2d, gf, tgt)


def _mix_bwd(proj, aux, states, dmixed, lb_logits, cw, ga, gcn, g64):
    nblk = SEQ // TB

    def body(p_ref, aux_ref, st_ref, dm_ref, lbl_ref, cw_ref, ga_ref, gcn_ref, g64_ref,
             dp_ref, part_ref, dst_ref, head_ref, dlb_ref):
        i = pl.program_id(0)

        @pl.when(i == 0)
        def _():
            dst_ref[...] = jnp.zeros_like(dst_ref)
            head_ref[...] = jnp.zeros_like(head_ref)
            part_ref[...] = jnp.zeros_like(part_ref)
            dlb_ref[...] = jnp.zeros_like(dlb_ref)

        lb = _lower_bound(lbl_ref[...])
        triu = _tri(False)
        causal = _causal()
        g64m = g64_ref[...]
        rowsum = lambda a: jnp.sum(a, axis=0, keepdims=True)
        heads = range(N_HEADS)
        cs = [slice(hd * HEAD, (hd + 1) * HEAD) for hd in heads]
        col = lambda base, hd: slice(base + hd * HEAD, base + (hd + 1) * HEAD)
        for n in reversed(range(NCB)):
            sl = pl.ds(n * CHUNK, CHUNK)
            cvv = [aux_ref[sl, col(AUX_CV, hd)] for hd in heads]
            gb = [p_ref[sl, col(2560, hd)] for hd in heads]
            yb = [gb[hd] * cvv[hd] for hd in heads]
            ms = _group_mean_many([y * y for y in yb], g64m)
            rb, nb, dnb = [], [], []
            for hd in heads:
                rb.append(lax.rsqrt(ms[hd] + EPS))
                nb.append(yb[hd] * rb[hd])
                zb = p_ref[sl, col(3584, hd)]
                sgb = _sigmoid(zb)
                dmb = dm_ref[sl, col(512, hd)]
                silu = zb * sgb
                dgate = dmb * gcn_ref[:, cs[hd]]
                part_ref[2:3, col(512, hd)] += rowsum(dmb * nb[hd] * silu)
                dp_ref[sl, col(3584, hd)] = (dgate * nb[hd] * (sgb + silu * (1.0 - sgb))).astype(BF16)
                dnb.append(dgate * silu)
            mdn = _group_mean_many([dnb[hd] * nb[hd] for hd in heads], g64m)
            for hd in heads:
                dyb = rb[hd] * (dnb[hd] - nb[hd] * mdn[hd])
                dp_ref[sl, col(2560, hd)] = (dyb * cvv[hd]).astype(BF16)
                dcv = dyb * gb[hd]
                head = head_ref[:, cs[hd]]
                dcv1 = _shift_up(dcv, 1, head)
                dcv2 = _shift_up(dcv, 2, head)
                head_ref[:, cs[hd]] = dcv[0:8, :]
                u = p_ref[sl, col(2048, hd)]
                gc = p_ref[sl, col(3072, hd)]
                cu = gc * u
                part_ref[4:5, cs[hd]] += rowsum(dcv2 * cu)
                part_ref[5:6, cs[hd]] += rowsum(dcv1 * cu)
                part_ref[6:7, cs[hd]] += rowsum(dcv * cu)
                dcu = cw_ref[2:3, cs[hd]] * dcv + cw_ref[1:2, cs[hd]] * dcv1 + cw_ref[0:1, cs[hd]] * dcv2
                dp_ref[sl, col(3072, hd)] = (dcu * u).astype(BF16)
                dp_ref[sl, col(2048, hd)] = (dcu * gc).astype(BF16)
            do_b = []
            for hd in heads:
                ov = aux_ref[sl, col(AUX_O, hd)]
                ra = lax.rsqrt(jnp.mean(ov * ov, axis=-1, keepdims=True) + EPS)
                na = ov * ra
                za = p_ref[sl, col(1536, hd)]
                sga = _sigmoid(za)
                dma = dm_ref[sl, cs[hd]]
                silu = za * sga
                dgate = dma * ga_ref[:, cs[hd]]
                part_ref[2:3, cs[hd]] += rowsum(dma * na * silu)
                dp_ref[sl, col(1536, hd)] = (dgate * na * (sga + silu * (1.0 - sga))).astype(BF16)
                dna = dgate * silu
                do_b.append((ra * (dna - na * jnp.mean(dna * na, axis=-1, keepdims=True))).astype(BF16))
            s = [_sigmoid(p_ref[sl, col(512, hd)]) for hd in heads]
            f = [lb[:, cs[hd]] + (1.0 - lb[:, cs[hd]]) * s[hd] for hd in heads]
            bc = [aux_ref[sl, col(AUX_B, hd)] for hd in heads]
            g = [bc[hd][CHUNK - 1:CHUNK, :] for hd in heads]
            eb = [jnp.exp(bc[hd]) for hd in heads]
            enb = [jnp.exp(-bc[hd]) for hd in heads]
            eg = [jnp.exp(g[hd] - bc[hd]) for hd in heads]
            dec = [jnp.exp(g[hd]) for hd in heads]
            qd = [p_ref[sl, cs[hd]] * eb[hd] for hd in heads]
            kk = [1.0 - f[hd] for hd in heads]
            ki = [kk[hd] * enb[hd] for hd in heads]
            ke = [kk[hd] * eg[hd] for hd in heads]
            qd_b = [a.astype(BF16) for a in qd]
            ki_b = [a.astype(BF16) for a in ki]
            ke_b = [a.astype(BF16) for a in ke]
            vb = [p_ref[sl, col(1024, hd)].astype(BF16) for hd in heads]
            st_b = [st_ref[n, hd] for hd in heads]
            dst = [dst_ref[hd] for hd in heads]
            dst_b = [a.astype(BF16) for a in dst]
            scm = [_dot_nt(qd_b[hd], ki_b[hd]) for hd in heads]
            amm = [_dot_nt(do_b[hd], vb[hd]) for hd in heads]
            dqd2 = [_dot(do_b[hd], st_b[hd]) for hd in heads]
            dke = [_dot(vb[hd], dst_b[hd]) for hd in heads]
            dv2 = [_dot_nt(ke_b[hd], dst_b[hd]) for hd in heads]
            dsu = [_dot_tn(do_b[hd], qd_b[hd]) for hd in heads]
            sc = [jnp.where(causal, scm[hd], 0.0).astype(BF16) for hd in heads]
            am = [jnp.where(causal, amm[hd], 0.0).astype(BF16) for hd in heads]
            dqd1 = [_dot(am[hd], ki_b[hd]) for hd in heads]
            dki = [_dot_tn(am[hd], qd_b[hd]) for hd in heads]
            dv1 = [_dot_tn(sc[hd], do_b[hd]) for hd in heads]
            db, dgv, dkk = [], [], []
            for hd in heads:
                dqd = dqd1[hd] + dqd2[hd]
                ddec = rowsum(dst[hd] * st_b[hd].astype(F32))
                dst_ref[hd] = dst[hd] * dec[hd] + dsu[hd]
                dp_ref[sl, cs[hd]] = (dqd * eb[hd]).astype(BF16)
                dp_ref[sl, col(1024, hd)] = (dv1[hd] + dv2[hd]).astype(BF16)
                dke_eg = dke[hd] * eg[hd]
                dkk.append(dki[hd] * enb[hd] + dke_eg)
                db.append(dqd * qd[hd] - kk[hd] * dkk[hd])
                dgv.append(rowsum(kk[hd] * dke_eg) + ddec * dec[hd])
            rc = _exact_left_many(triu, db, 2)
            for hd in heads:
                df = (rc[hd] + dgv[hd]) / f[hd] - dkk[hd]
                one_s = 1.0 - s[hd]
                dlb_ref[:, cs[hd]] += rowsum(df * one_s)
                dp_ref[sl, col(512, hd)] = (df * (1.0 - lb[:, cs[hd]]) * s[hd] * one_s).astype(BF16)

        @pl.when(i == nblk - 1)
        def _():
            row = dlb_ref[...] * lb * (1.0 - lb)
            part_ref[3:4, 0:D_HGRN] = row
            part_ref[3:4, D_HGRN:] = -row

    rev = lambda w: pl.BlockSpec((TB, w), lambda i: (nblk - 1 - i, 0))
    row = lambda w: pl.BlockSpec((1, w), lambda i: (0, 0))
    return pl.pallas_call(
        body, name="mix_bwd", grid=(nblk,),
        out_shape=(jax.ShapeDtypeStruct((SEQ, 4096), BF16),
                   jax.ShapeDtypeStruct((8, D_MODEL), F32)),
        in_specs=[rev(4096), rev(AUX_COLS),
                  pl.BlockSpec((NCB, N_HEADS, HEAD, HEAD), lambda i: (nblk - 1 - i, 0, 0, 0)),
                  rev(D_MODEL),
                  pl.BlockSpec((2, D_HGRN), lambda i: (0, 0)),
                  pl.BlockSpec((8, D_CONV), lambda i: (0, 0)),
                  row(D_HGRN), row(D_CONV),
                  pl.BlockSpec((HEAD, HEAD), lambda i: (0, 0))],
        out_specs=(rev(4096), pl.BlockSpec((8, D_MODEL), lambda i: (0, 0))),
        scratch_shapes=[pltpu.VMEM((N_HEADS, HEAD, HEAD), F32), pltpu.VMEM((8, D_CONV), F32),
                        pltpu.VMEM((1, D_HGRN), F32)],
        compiler_params=pltpu.CompilerParams(dimension_semantics=("arbitrary",), vmem_limit_bytes=VMEM_LIMIT),
    )(proj, aux, states, dmixed, lb_logits, cw, ga, gcn, g64)


TT = 1024
TX = 512
(SEM_D2D, SEM_D2D_O, SEM_ICI, SEM_ICI_O, SEM_FIN, SEM_FIN_O, SEM_SMALL, SEM_VIA, SEM_NORM, N_SEM_TAIL) = (
    0, 4, 5, 8, 11, 12, 12, 20, 22, 30)


def _bwd_tail(kidx, h, dproj, wg, gwo, x2d, dx2, g1, small_a, small_b):
    hw = D_MODEL // 2
    ho = WO_ROWS // 2
    nt = SEQ // TT
    norm_step = 2 * N_SHARD
    n_steps = norm_step + SEQ // TX // nt

    def body(k_ref, h_ref, dp_ref, w_ref, gwo_ref, x_ref, dx2_ref, g_ref, sm_ref, smb_ref,
             gx_ref, gw_out, gwo_out, osm_ref,
             acc, dh, sendbuf, keep, sibrcv, rcv, merge, sib_o, p_o, rcv_o, res_o, sm_buf, dng_buf, dng,
             send_sems, recv_sems, out_sems):
        s, t = pl.program_id(0), pl.program_id(1)
        x, y, c = lax.axis_index("x"), lax.axis_index("y"), lax.axis_index("c")
        k = 2 * x + y
        me = 4 * x + 2 * y + c
        sibling = (x, y, 1 - c)
        chips = [(1 - x, 1 - y), (1 - x, y), (x, 1 - y)]
        kjs = [2 * cx + cy for cx, cy in chips]
        mine = pl.ds(pl.multiple_of(c * hw, hw), hw)
        other = pl.ds(pl.multiple_of((1 - c) * hw, hw), hw)
        mine_o = pl.ds(pl.multiple_of(c * ho, ho), ho)
        other_o = pl.ds(pl.multiple_of((1 - c) * ho, ho), ho)

        def copy(sem, src, dst, to):
            return pltpu.make_async_remote_copy(
                src_ref=src, dst_ref=dst, send_sem=send_sems.at[sem], recv_sem=recv_sems.at[sem],
                device_id=to, device_id_type=MESH)

        def at_step(sv, tv):
            return pl.when((s == sv) & (t == tv))

        def at_norm_block(b):
            return at_step(norm_step + b // nt, b % nt)

        d2d = [copy(SEM_D2D + sv, sendbuf.at[sv], sibrcv.at[sv], sibling) for sv in range(N_SHARD)]
        d2d_o = copy(SEM_D2D_O, gwo_ref.at[:, other_o, :], sib_o, sibling)
        ici = {sv: copy(SEM_ICI + sv, keep.at[sv], rcv.at[sv - 1], (*chips[sv], c)) for sv in (1, 2)}
        qh = hw // 2
        via = [copy(SEM_VIA, keep.at[0, 0:qh, :], merge.at[1], (*chips[1], c)),
               copy(SEM_VIA + 1, keep.at[0, qh:hw, :], merge.at[0], (*chips[2], c))]
        merged_rows = [slice(qh, hw), slice(0, qh)]
        ici_o = [copy(SEM_ICI_O + sv, p_o.at[kjs[sv]], rcv_o.at[sv], (*chips[sv], c)) for sv in range(3)]
        fin = copy(SEM_FIN, acc.at[mine, :], gw_out.at[mine, :], sibling)
        fin_o = copy(SEM_FIN_O, res_o.at[mine_o, :], res_o.at[mine_o, :], sibling)
        peers = [(x ^ (m >> 2), y ^ ((m >> 1) & 1), c ^ (m & 1)) for m in range(1, N_DEV)]
        smalls = [copy(SEM_SMALL + 1 + j, sm_buf.at[me], sm_buf.at[me], to) for j, to in enumerate(peers)]
        dngs = [copy(SEM_NORM + 1 + j, dng_buf.at[me], dng_buf.at[me], to) for j, to in enumerate(peers)]
        store_w = pltpu.make_async_copy(acc.at[mine, :], gw_out.at[mine, :], out_sems.at[0])
        store_o = pltpu.make_async_copy(res_o, gwo_out, out_sems.at[1])

        @at_step(0, 0)
        def _():
            barrier = pltpu.get_barrier_semaphore()
            for to in peers:
                pl.semaphore_signal(barrier, inc=1, device_id=to, device_id_type=MESH)
            sm_buf[me] = sm_ref[...] + smb_ref[...]
            pl.semaphore_wait(barrier, N_DEV - 1)
            d2d_o.start()
            for cp in smalls:
                cp.start()

        @at_step(0, 1)
        def _():
            d2d_o.wait_recv()
            for j in range(N_SHARD):
                p_o[j] = (gwo_ref[j, mine_o, :].astype(F32) + sib_o[j].astype(F32)).astype(BF16)
            res_o[mine_o, :] = gwo_ref[k, mine_o, :].astype(F32) + sib_o[k].astype(F32)
            for cp in ici_o:
                cp.start()

        rows = pl.ds(pl.multiple_of(t * TT, TT), TT)

        @pl.when((s < N_SHARD) & (t == 0))
        def _():
            acc[...] = _dot_tn(h_ref[...], dp_ref[...])

        @pl.when((s < N_SHARD) & (t > 0))
        def _():
            acc[...] += _dot_tn(h_ref[...], dp_ref[...])

        for sv in range(N_SHARD):
            @at_step(sv, nt - 1)
            def _(sv=sv):
                sendbuf[sv] = acc[other, :].astype(BF16)
                if sv < 3:
                    keep[sv] = acc[mine, :].astype(BF16)
                d2d[sv].start()

        @at_step(1, 0)
        def _():
            d2d[0].wait_recv()
            keep[0] = (keep[0].astype(F32) + sibrcv[0].astype(F32)).astype(BF16)
            for cp in via:
                cp.start()

        for sv in (1, 2):
            @at_step(sv + 2, 0)
            def _(sv=sv):
                d2d[sv].wait_recv()
                keep[sv] = (keep[sv].astype(F32) + sibrcv[sv].astype(F32)).astype(BF16)
                via[2 - sv].wait_recv()
                rows_m = merged_rows[sv - 1]
                keep[sv, rows_m, :] = (keep[sv, rows_m, :].astype(F32) + merge[sv - 1].astype(F32)).astype(BF16)
                ici[sv].start()

        @pl.when(s == N_SHARD)
        def _():
            dh[rows, :] = _dot_nt(dp_ref[...], w_ref[0])

        @pl.when((s > N_SHARD) & (s < norm_step))
        def _():
            dh[rows, :] += _dot_nt(dp_ref[...], w_ref[0])

        @at_norm_block(0)
        def _():
            d2d[3].wait_recv()
            acc[mine, :] += sibrcv[3].astype(F32)

        @at_norm_block(1)
        def _():
            tot = res_o[mine_o, :]
            for sv in range(3):
                ici_o[sv].wait_recv()
                tot = tot + rcv_o[sv].astype(F32)
            res_o[mine_o, :] = tot
            fin_o.start()

        @at_norm_block(2)
        def _():
            ici[1].wait_recv()
            acc[mine, :] += rcv[0].astype(F32)

        @at_norm_block(SEQ // TX - 2)
        def _():
            ici[2].wait_recv()
            acc[mine, :] += rcv[1].astype(F32)
            fin.start()
            store_w.start()
            fin_o.wait_recv()
            store_o.start()

        @at_norm_block(0)
        def _():
            dng[...] = jnp.zeros_like(dng)

        @pl.when(s >= norm_step)
        def _():
            blk = (s - norm_step) * nt + t
            dhv = dh[pl.ds(pl.multiple_of(blk * TX, TX), TX), :]
            xv = x_ref[...]
            r = lax.rsqrt(jnp.mean(xv * xv, axis=-1, keepdims=True) + EPS)
            xn = xv * r
            dng[...] += jnp.sum(dhv * xn, axis=0, keepdims=True)
            dxn = dhv * g_ref[...]
            gx_ref[...] = dx2_ref[...] + r * (dxn - xn * jnp.mean(dxn * xn, axis=-1, keepdims=True))

        @at_step(n_steps - 1, nt - 1)
        def _():
            dng_buf[me] = dng[...]
            for cp in dngs:
                cp.start()
            for m in range(1, N_DEV):
                copy(SEM_SMALL + m, sm_buf.at[0], sm_buf.at[0], sibling).wait_recv()
            tot = sm_buf[0]
            for d in range(1, N_DEV):
                tot = tot + sm_buf[d]
            osm_ref[...] = tot
            for m in range(1, N_DEV):
                copy(SEM_NORM + m, dng_buf.at[0], dng_buf.at[0], sibling).wait_recv()
            tot = dng_buf[0]
            for d in range(1, N_DEV):
                tot = tot + dng_buf[d]
            osm_ref[0:1, :] = tot
            fin.wait_recv()
            for cp in d2d + [d2d_o] + via + list(ici.values()) + ici_o + [fin, fin_o] + smalls + dngs:
                cp.wait_send()
            store_o.wait()
            store_w.wait()

    def shard_of(s, kr):
        order = jnp.where(s < N_SHARD, s, jnp.where(s < norm_step, s - N_SHARD, 3))
        return kr[0] ^ (3 - order)

    def h_map(s, t, kr):
        return (jnp.where(s < N_SHARD, t, nt - 1), 0)

    def dp_map(s, t, kr):
        return (jnp.where(s < norm_step, t, nt - 1), shard_of(s, kr))

    def w_map(s, t, kr):
        return (shard_of(jnp.maximum(s, N_SHARD), kr), 0, 0)

    def blk_map(s, t, kr):
        return (jnp.where(s < norm_step, 0, (s - norm_step) * nt + t), 0)

    hbm = pl.BlockSpec(memory_space=pl.ANY)
    grid_spec = pltpu.PrefetchScalarGridSpec(
        num_scalar_prefetch=1, grid=(n_steps, nt),
        in_specs=[pl.BlockSpec((TT, D_MODEL), h_map),
                  pl.BlockSpec((TT, SHARD_COLS), dp_map),
                  pl.BlockSpec((1, D_MODEL, SHARD_COLS), w_map),
                  pl.BlockSpec((N_SHARD, WO_ROWS, D_MODEL), lambda s, t, kr: (0, 0, 0),
                               pipeline_mode=pl.Buffered(1)),
                  pl.BlockSpec((TX, D_MODEL), blk_map),
                  pl.BlockSpec((TX, D_MODEL), blk_map),
                  pl.BlockSpec((1, D_MODEL), lambda s, t, kr: (0, 0)),
                  pl.BlockSpec((8, D_MODEL), lambda s, t, kr: (0, 0)),
                  pl.BlockSpec((8, D_MODEL), lambda s, t, kr: (0, 0))],
        out_specs=(pl.BlockSpec((TX, D_MODEL), blk_map), hbm, hbm,
                   pl.BlockSpec((8, D_MODEL), lambda s, t, kr: (0, 0))),
        scratch_shapes=[pltpu.VMEM((D_MODEL, SHARD_COLS), F32), pltpu.VMEM((SEQ, D_MODEL), F32),
                        pltpu.VMEM((N_SHARD, hw, SHARD_COLS), BF16), pltpu.VMEM((3, hw, SHARD_COLS), BF16),
                        pltpu.VMEM((N_SHARD, hw, SHARD_COLS), BF16), pltpu.VMEM((2, hw, SHARD_COLS), BF16),
                        pltpu.VMEM((2, hw // 2, SHARD_COLS), BF16),
                        pltpu.VMEM((N_SHARD, ho, D_MODEL), BF16), pltpu.VMEM((N_SHARD, ho, D_MODEL), BF16),
                        pltpu.VMEM((3, ho, D_MODEL), BF16), pltpu.VMEM((WO_ROWS, D_MODEL), F32),
                        pltpu.VMEM((N_DEV, 8, D_MODEL), F32), pltpu.VMEM((N_DEV, 1, D_MODEL), F32),
                        pltpu.VMEM((1, D_MODEL), F32),
                        pltpu.SemaphoreType.DMA((N_SEM_TAIL,)), pltpu.SemaphoreType.DMA((N_SEM_TAIL,)),
                        pltpu.SemaphoreType.DMA((2,))])
    return pl.pallas_call(
        body, name="bwd_tail", grid_spec=grid_spec,
        out_shape=(jax.ShapeDtypeStruct((SEQ, D_MODEL), F32),
                   jax.ShapeDtypeStruct((D_MODEL, SHARD_COLS), F32),
                   jax.ShapeDtypeStruct((WO_ROWS, D_MODEL), F32),
                   jax.ShapeDtypeStruct((8, D_MODEL), F32)),
        compiler_params=pltpu.CompilerParams(dimension_semantics=("arbitrary", "arbitrary"),
                                             vmem_limit_bytes=61 * 1024 * 1024, collective_id=COLLECTIVE_TAIL),
    )(kidx, h, dproj, wg, gwo, x2d, dx2, g1, small_a, small_b)


def _adam_update(w, g, m, v):
    nm = ADAM_B1 * m + (1.0 - ADAM_B1) * g
    nv = ADAM_B2 * v + (1.0 - ADAM_B2) * (g * g)
    m_hat = nm / (1.0 - ADAM_B1 ** ADAM_STEP)
    v_hat = nv / (1.0 - ADAM_B2 ** ADAM_STEP)
    return -ADAM_LR * (m_hat / (jnp.sqrt(v_hat) + ADAM_EPS) + ADAM_WD * w), nm, nv


def _adamw_all(tot, g_w_in, g_w_out, big, small, grad_x):
    n = len(small)
    rows = WO_ROWS
    steps = D_MODEL // rows

    def body(tot_ref, *refs):
        gx_ref, gx_out = refs[2 + 3 * (2 + n)], refs[-1]
        gx_out[...] = gx_ref[...]
        ins, outs = refs[:2 + 3 * (2 + n)], refs[3 + 3 * (2 + n):-1]
        g_refs, wmv = ins[:2], ins[2:]
        loss_ref, quads = outs[0], outs[1:]

        def update(j, g):
            w_ref, m_ref, v_ref = wmv[3 * j:3 * j + 3]
            g_ref, d_ref, nm_ref, nv_ref = quads[4 * j:4 * j + 4]
            g_ref[...] = g
            d_ref[...], nm_ref[...], nv_ref[...] = _adam_update(w_ref[...], g, m_ref[...], v_ref[...])

        update(0, g_refs[0][...])

        @pl.when(pl.program_id(0) == 0)
        def _():
            update(1, g_refs[1][...])
            k = 2 * lax.axis_index("x") + lax.axis_index("y")
            mine = pl.ds(pl.multiple_of(k * HEAD, HEAD), HEAD)
            loss_ref[...] = tot_ref[7:8, 0:1]
            grads = [tot_ref[0:1, :], tot_ref[1:2, :], tot_ref[2:3, 0:D_HGRN], tot_ref[2:3, D_HGRN:],
                     jnp.concatenate([tot_ref[3:4, 0:D_HGRN], tot_ref[3:4, D_HGRN:]], axis=0),
                     jnp.concatenate([tot_ref[4 + tap:5 + tap, mine] for tap in range(3)], axis=1)]
            for j, g in enumerate(grads):
                update(2 + j, g)

    whole = lambda a: pl.BlockSpec(a.shape, lambda i: (0, 0))
    blk = pl.BlockSpec((rows, SHARD_COLS), lambda i: (i, 0))
    arrays = [a for triple in big + small for a in triple]
    in_specs = ([whole(tot), blk, whole(g_w_out)] + [blk] * 3 + [whole(a) for a in arrays[3:]])
    shapes = [big[0][0], big[1][0]] + [w for w, _, _ in small]
    out_shape = (jax.ShapeDtypeStruct((1, 1), F32),) + tuple(
        jax.ShapeDtypeStruct(w.shape, F32) for w in shapes for _ in range(4))
    out_specs = (pl.BlockSpec((1, 1), lambda i: (0, 0)),) + (blk,) * 4 + tuple(
        whole(w) for w in shapes[1:] for _ in range(4))
    gx_blk = pl.BlockSpec((SEQ // steps, D_MODEL), lambda i: (i, 0))
    outs = pl.pallas_call(
        body, name="adamw_all", grid=(steps,),
        out_shape=out_shape + (jax.ShapeDtypeStruct(grad_x.shape, F32),),
        in_specs=in_specs + [gx_blk], out_specs=out_specs + (gx_blk,),
        compiler_params=pltpu.CompilerParams(dimension_semantics=("arbitrary",), vmem_limit_bytes=VMEM_LIMIT),
    )(tot, g_w_in, g_w_out, *arrays, grad_x)
    return [outs[0]] + [outs[1 + 4 * j:5 + 4 * j] for j in range(2 + n)] + [outs[-1]]


def _local_step(x2d, tgt, proj, lb_logits, cw, ga, gcn, w_out, gf):
    g64 = _group_matrix(HEAD, CONV_GROUP)
    aux, states, dx2, dmixed, gwo, part_out = _mix_out(proj, lb_logits, cw, ga, gcn, g64, w_out, x2d, gf, tgt)
    dproj, part_mix = _mix_bwd(proj, aux, states, dmixed, lb_logits, cw, ga, gcn, g64)
    return dproj, dx2, gwo.reshape(N_SHARD, WO_ROWS, D_MODEL), part_out, part_mix


def kernel(x, norm_gain, w_in, lb_logits, conv_w, hgrn_norm_gain, conv_norm_gain, w_out, final_norm_gain, loss_target, m_norm_gain, m_w_in, m_lb_logits, m_conv_w, m_hgrn_norm_gain, m_conv_norm_gain, m_w_out, m_final_norm_gain, v_norm_gain, v_w_in, v_lb_logits, v_conv_w, v_hgrn_norm_gain, v_conv_norm_gain, v_w_out, v_final_norm_gain):
    k = 2 * lax.axis_index("x") + lax.axis_index("y")
    kidx = jnp.reshape(k, (1,)).astype(jnp.int32)
    row = lambda a: a.reshape(1, D_MODEL)
    taps = lambda a: a.reshape(1, 3 * HEAD)
    h, proj, wg, cw = _gather_proj(kidx, x[0], norm_gain, w_in, taps(conv_w))
    dproj, dx2, gwo, part_out, part_mix = _local_step(
        x[0], loss_target[0], proj, lb_logits, cw, hgrn_norm_gain, conv_norm_gain, w_out, row(final_norm_gain))
    rgrad_x, rg_w_in, rg_w_out, tot = _bwd_tail(kidx, h, dproj, wg, gwo, x[0], dx2, norm_gain, part_out, part_mix)

    (loss, (g_w_in, d_w_in, nm_w_in, nv_w_in), (g_w_out, d_w_out, nm_w_out, nv_w_out),
     (g_norm_gain, d_ng, nm_ng, nv_ng), (g_final, d_fg, nm_fg, nv_fg), (g_hgrn, d_hg, nm_hg, nv_hg),
     (g_convn, d_cg, nm_cg, nv_cg), (g_lb, d_lb, nm_lb, nv_lb), (g_conv_w, d_cw, nm_cw, nv_cw),
     grad_x) = _adamw_all(
        tot, rg_w_in, rg_w_out,
        [(w_in[0], m_w_in[0], v_w_in[0]), (w_out[0], m_w_out[0], v_w_out[0])],
        [(norm_gain, m_norm_gain, v_norm_gain),
         (row(final_norm_gain), row(m_final_norm_gain), row(v_final_norm_gain)),
         (hgrn_norm_gain, m_hgrn_norm_gain, v_hgrn_norm_gain),
         (conv_norm_gain, m_conv_norm_gain, v_conv_norm_gain),
         (lb_logits, m_lb_logits, v_lb_logits),
         (taps(conv_w), taps(m_conv_w), taps(v_conv_w))],
        rgrad_x)
    flat = lambda a: a.reshape(D_MODEL)
    untap = lambda a: a.reshape(1, 3, HEAD)
    return (loss.reshape(()), grad_x[None],
            g_norm_gain, g_w_in[None], g_lb, untap(g_conv_w), g_hgrn, g_convn, g_w_out[None], flat(g_final),
            d_ng, d_w_in[None], d_lb, untap(d_cw), d_hg, d_cg, d_w_out[None], flat(d_fg),
            nm_ng, nm_w_in[None], nm_lb, untap(nm_cw), nm_hg, nm_cg, nm_w_out[None], flat(nm_fg),
            nv_ng, nv_w_in[None], nv_lb, untap(nv_cw), nv_hg, nv_cg, nv_w_out[None], flat(nv_fg))
```

```python
import jax
import jax.numpy as jnp
import numpy as np
from jax import lax
from jax.experimental import pallas as pl
from jax.experimental.pallas import tpu as pltpu

F32 = jnp.float32
BF16 = jnp.bfloat16
MESH = pl.DeviceIdType.MESH

SEQ = 2048
D_MODEL = 1024
D_HGRN = 512
D_CONV = 512
HEAD = 128
N_HEADS = 4
CHUNK = 64
CONV_GROUP = 64
N_SHARD = 4
SHARD_COLS = 1024
WO_ROWS = 256
EPS = 1e-6
TB = 256
NCB = TB // CHUNK
N_CHUNKS = SEQ // CHUNK
N_DEV = 8
COLLECTIVE_GATHER, COLLECTIVE_MIX_OUT, COLLECTIVE_TAIL = 1, 0, 2
AUX_O, AUX_CV, AUX_B, AUX_COLS = 0, 512, 1024, 1536

ADAM_LR = 0.001
ADAM_B1 = 0.9
ADAM_B2 = 0.999
ADAM_EPS = 1e-08
ADAM_WD = 0.01
ADAM_STEP = 10

VMEM_LIMIT = 56 * 1024 * 1024


def _dot(a, b):
    return jnp.dot(a, b, preferred_element_type=F32)


def _dot_nt(a, b):
    return lax.dot_general(a, b, (((1,), (1,)), ((), ())), preferred_element_type=F32)


def _dot_tn(a, b):
    return lax.dot_general(a, b, (((0,), (0,)), ((), ())), preferred_element_type=F32)


def _split_bf16(x, n):
    parts = []
    r = x
    for _ in range(n):
        p = r.astype(BF16)
        parts.append(p)
        r = r - p.astype(F32)
    return parts


def _exact_left(m, x, n=3):
    acc = None
    for p in _split_bf16(x, n):
        t = _dot(m, p)
        acc = t if acc is None else acc + t
    return acc


def _exact_left_many(m, xs, n=3):
    parts = [_split_bf16(x, n) for x in xs]
    accs = [None] * len(xs)
    for i in range(n):
        for j in range(len(xs)):
            t = _dot(m, parts[j][i])
            accs[j] = t if accs[j] is None else accs[j] + t
    return accs


def _group_mean_many(xs, gmat, n=2):
    parts = [_split_bf16(x, n) for x in xs]
    accs = [None] * len(xs)
    for i in range(n):
        for j in range(len(xs)):
            t = _dot(parts[j][i], gmat)
            accs[j] = t if accs[j] is None else accs[j] + t
    return accs


def _group_mean(x, gmat, n=2):
    w = gmat.shape[0]
    outs = []
    for c0 in range(0, x.shape[1], w):
        acc = None
        for p in _split_bf16(x[:, c0:c0 + w], n):
            t = _dot(p, gmat)
            acc = t if acc is None else acc + t
        outs.append(acc)
    return jnp.concatenate(outs, axis=1)


def _sigmoid(x):
    return 1.0 / (1.0 + jnp.exp(-x))


def _lower_bound(lbl):
    l0 = lbl[0:1, :]
    l1 = lbl[1:2, :]
    m = jnp.maximum(l0, l1)
    e0 = jnp.exp(l0 - m)
    e1 = jnp.exp(l1 - m)
    return e0 / (e0 + e1)


def _tri(lower):
    r = lax.broadcasted_iota(jnp.int32, (CHUNK, CHUNK), 0)
    c = lax.broadcasted_iota(jnp.int32, (CHUNK, CHUNK), 1)
    return jnp.where((c <= r) if lower else (c >= r), 1.0, 0.0).astype(BF16)


def _causal():
    r = lax.broadcasted_iota(jnp.int32, (CHUNK, CHUNK), 0)
    c = lax.broadcasted_iota(jnp.int32, (CHUNK, CHUNK), 1)
    return c <= r


def _shift_down(x, sh, prev_tail):
    r = pltpu.roll(x, sh, 0)
    pt = pltpu.roll(prev_tail, sh, 0)
    rows = lax.broadcasted_iota(jnp.int32, prev_tail.shape, 0)
    top = jnp.where(rows < sh, pt, r[0:8])
    return jnp.concatenate([top, r[8:]], axis=0)


def _shift_up(x, sh, next_head):
    n = x.shape[0]
    r = pltpu.roll(x, n - sh, 0)
    nh = pltpu.roll(next_head, 8 - sh, 0)
    rows = lax.broadcasted_iota(jnp.int32, next_head.shape, 0)
    bot = jnp.where(rows >= 8 - sh, nh, r[n - 8:])
    return jnp.concatenate([r[:n - 8], bot], axis=0)


def _group_matrix(width, group):
    r = np.arange(width)[:, None] // group
    c = np.arange(width)[None, :] // group
    return jnp.asarray(np.where(r == c, 1.0 / group, 0.0), dtype=BF16)


TG = 1024
SEM_W, SEM_CW, SEM_W_FWD, N_SEM = 0, 4, 7, 11


def _gather_proj(kidx, x2d, g1, w_in, conv_w):
    half_w = D_MODEL // 2
    half_c = SHARD_COLS // 2
    nt = SEQ // TG
    n_steps = 2 * N_SHARD

    def body(k_ref, x_ref, g_ref, w_ref, cw_ref, h_out, p_ref, wg_out, cwg_out,
             wg_v, cwg_v, h_ref, send_sems, recv_sems, out_sems):
        s, t = pl.program_id(0), pl.program_id(1)
        x, y, c = lax.axis_index("x"), lax.axis_index("y"), lax.axis_index("c")
        k = 2 * x + y
        sibling = (x, y, 1 - c)
        chips = [(1 - x, y), (x, 1 - y), (1 - x, 1 - y)]
        kjs = [2 * cx + cy for cx, cy in chips]
        diag = (*chips[2], c)

        def w_half(kk, cc):
            return wg_v.at[kk, pl.ds(cc * half_w, half_w), :]

        def w_quarter(kk, cc, piece):
            return wg_v.at[kk, pl.ds(cc * half_w, half_w), piece * half_c:(piece + 1) * half_c]

        def cw_of(kk):
            return cwg_v.at[:, pl.ds(pl.multiple_of(kk * HEAD, HEAD), HEAD)]

        def copy(sem, ref, to):
            return pltpu.make_async_remote_copy(
                src_ref=ref, dst_ref=ref, send_sem=send_sems.at[sem], recv_sem=recv_sems.at[sem],
                device_id=to, device_id_type=MESH)

        def at_step(sv, tv):
            return pl.when((s == sv) & (t == tv))

        w_direct = ([copy(SEM_W + j, w_half(k, c), (*chips[j], c)) for j in range(2)]
                    + [copy(SEM_W + 2 + p, w_quarter(k, c, p), diag) for p in range(2)])
        cw_direct = [copy(SEM_CW + j, cw_of(k), (*chip, c)) for j, chip in enumerate(chips)]
        w_passed = ([copy(SEM_W_FWD + j, w_half(kjs[j], c), sibling) for j in range(2)]
                    + [copy(SEM_W_FWD + 2 + p, w_quarter(kjs[2], c, p), sibling) for p in range(2)])
        stores = ([pltpu.make_async_copy(wg_v.at[kk], wg_out.at[kk], out_sems.at[i])
                   for i, kk in enumerate([k] + kjs)]
                  + [pltpu.make_async_copy(cwg_v, cwg_out, out_sems.at[4]),
                     pltpu.make_async_copy(h_ref, h_out, out_sems.at[5])])

        @at_step(0, 0)
        def _():
            barrier = pltpu.get_barrier_semaphore()
            for peer in [sibling] + [(*chip, c) for chip in chips]:
                pl.semaphore_signal(barrier, inc=1, device_id=peer, device_id_type=MESH)
            wg_v[k] = w_ref[0].astype(BF16)
            mine = pl.ds(pl.multiple_of(k * HEAD, HEAD), HEAD)
            cwg_v[:, mine] = jnp.zeros((8, HEAD), F32)
            for tap in range(3):
                cwg_v[tap:tap + 1, mine] = cw_ref[:, tap * HEAD:(tap + 1) * HEAD]
            pl.semaphore_wait(barrier, 4)
            for cp in w_direct + cw_direct:
                cp.start()
            stores[0].start()

        @at_step(1, 0)
        def _():
            stores[5].start()

        @at_step(2, 0)
        def _():
            for j in range(2):
                copy(SEM_W + j, w_half(kjs[j], c), sibling).wait_recv()
                w_passed[j].start()
            copy(SEM_W_FWD, w_half(kjs[0], 1 - c), sibling).wait_recv()
            stores[1].start()

        @at_step(4, 0)
        def _():
            copy(SEM_W_FWD + 1, w_half(kjs[1], 1 - c), sibling).wait_recv()
            stores[2].start()

        for p in range(2):
            @at_step(6 + p, 0)
            def _(p=p):
                copy(SEM_W + 2 + p, w_quarter(kjs[2], c, p), sibling).wait_recv()
                w_passed[2 + p].start()
                copy(SEM_W_FWD + 2 + p, w_quarter(kjs[2], 1 - c, p), sibling).wait_recv()
                if p == 1:
                    stores[3].start()
                    for j in range(3):
                        copy(SEM_CW + j, cw_of(kjs[j]), sibling).wait_recv()
                    stores[4].start()

        rows = pl.ds(pl.multiple_of(t * TG, TG), TG)

        @pl.when(s == 0)
        def _():
            xv = x_ref[...]
            r = lax.rsqrt(jnp.mean(xv * xv, axis=-1, keepdims=True) + EPS)
            h_ref[rows, :] = (xv * r * g_ref[...]).astype(BF16)

        sh = s >> 1
        js = k ^ (((sh & 1) << 1) | (sh >> 1))
        for piece in range(2):
            @pl.when((s & 1) == piece)
            def _(piece=piece):
                p_ref[...] = _dot(h_ref[rows, :], wg_v[js, :, piece * half_c:(piece + 1) * half_c])

        @at_step(n_steps - 1, nt - 1)
        def _():
            for cp in w_direct + cw_direct + w_passed:
                cp.wait_send()
            for st in stores:
                st.wait()

    def x_map(s, t, kr):
        return (jnp.where(s == 0, t, nt - 1), 0)

    def p_map(s, t, kr):
        sh = s >> 1
        return (t, 2 * (kr[0] ^ (((sh & 1) << 1) | (sh >> 1))) + (s & 1))

    hbm = pl.BlockSpec(memory_space=pl.ANY)
    grid_spec = pltpu.PrefetchScalarGridSpec(
        num_scalar_prefetch=1, grid=(n_steps, nt),
        in_specs=[pl.BlockSpec((TG, D_MODEL), x_map),
                  pl.BlockSpec((1, D_MODEL), lambda s, t, kr: (0, 0)),
                  pl.BlockSpec((1, D_MODEL, SHARD_COLS), lambda s, t, kr: (0, 0, 0)),
                  pl.BlockSpec((1, 3 * HEAD), lambda s, t, kr: (0, 0))],
        out_specs=(hbm, pl.BlockSpec((TG, half_c), p_map), hbm, hbm),
        scratch_shapes=[pltpu.VMEM((N_SHARD, D_MODEL, SHARD_COLS), BF16),
                        pltpu.VMEM((8, D_CONV), F32), pltpu.VMEM((SEQ, D_MODEL), BF16),
                        pltpu.SemaphoreType.DMA((N_SEM,)), pltpu.SemaphoreType.DMA((N_SEM,)),
                        pltpu.SemaphoreType.DMA((6,))])
    return pl.pallas_call(
        body, name="gather_proj", grid_spec=grid_spec,
        out_shape=(jax.ShapeDtypeStruct((SEQ, D_MODEL), BF16),
                   jax.ShapeDtypeStruct((SEQ, N_SHARD * SHARD_COLS), F32),
                   jax.ShapeDtypeStruct((N_SHARD, D_MODEL, SHARD_COLS), BF16),
                   jax.ShapeDtypeStruct((8, D_CONV), F32)),
        compiler_params=pltpu.CompilerParams(dimension_semantics=("arbitrary", "arbitrary"),
                                             vmem_limit_bytes=VMEM_LIMIT, collective_id=COLLECTIVE_GATHER),
    )(kidx, x2d, g1, w_in, conv_w)


LAG = 6


def _mix_out(proj, lb_logits, cw, ga, gcn, g64, w_out, x2d, gf, tgt):
    half_o = WO_ROWS // 2
    nblk = SEQ // TB
    n_steps = nblk + LAG

    def body(p_ref, lbl_ref, cw_ref, ga_ref, gcn_ref, g64_ref, wo_ref, x_ref, gf_ref, t_ref,
             aux_ref, sto_ref, dx2_ref, dm_ref, gwo_ref, part_ref,
             st_ref, tail_ref, wog_v, stage, ring, acc_ref, send_sems, recv_sems):
        i = pl.program_id(0)
        x, y, c = lax.axis_index("x"), lax.axis_index("y"), lax.axis_index("c")
        k = 2 * x + y
        sibling = (x, y, 1 - c)
        chips = [(1 - x, y), (x, 1 - y), (1 - x, 1 - y)]
        kjs = [2 * cx + cy for cx, cy in chips]

        def wo_half(kk, cc):
            return wog_v.at[pl.ds(pl.multiple_of(kk * WO_ROWS + cc * half_o, half_o), half_o), :]

        def copy(sem, ref, to):
            return pltpu.make_async_remote_copy(
                src_ref=ref, dst_ref=ref, send_sem=send_sems.at[sem], recv_sem=recv_sems.at[sem],
                device_id=to, device_id_type=MESH)

        wo_direct = [copy(j, wo_half(k, c), (*chip, c)) for j, chip in enumerate(chips)]
        wo_passed = [copy(3 + j, wo_half(kj, c), sibling) for j, kj in enumerate(kjs)]

        @pl.when(i == 0)
        def _():
            barrier = pltpu.get_barrier_semaphore()
            for peer in [sibling] + [(*chip, c) for chip in chips]:
                pl.semaphore_signal(barrier, inc=1, device_id=peer, device_id_type=MESH)
            st_ref[...] = jnp.zeros_like(st_ref)
            tail_ref[...] = jnp.zeros_like(tail_ref)
            acc_ref[...] = jnp.zeros_like(acc_ref)
            part_ref[...] = jnp.zeros_like(part_ref)
            wog_v[pl.ds(pl.multiple_of(k * WO_ROWS, WO_ROWS), WO_ROWS), :] = wo_ref[0].astype(BF16)

        @pl.when(i == LAG - 1)
        def _():
            for j in range(3):
                copy(j, wo_half(kjs[j], c), sibling).wait_recv()
                wo_passed[j].start()

        @pl.when(i == LAG)
        def _():
            for j in range(3):
                copy(3 + j, wo_half(kjs[j], 1 - c), sibling).wait_recv()

        lb = _lower_bound(lbl_ref[...])
        tri = _tri(True)
        causal = _causal()
        g64m = g64_ref[...]
        heads = range(N_HEADS)
        cs = [slice(hd * HEAD, (hd + 1) * HEAD) for hd in heads]
        col = lambda base, hd: slice(base + hd * HEAD, base + (hd + 1) * HEAD)

        def mix_chunk(n):
            sl = pl.ds(n * CHUNK, CHUNK)
            sg = [_sigmoid(p_ref[sl, col(512, hd)]) for hd in heads]
            f = [lb[:, cs[hd]] + (1.0 - lb[:, cs[hd]]) * sg[hd] for hd in heads]
            bc = _exact_left_many(tri, [jnp.log(f[hd]) for hd in heads])
            for hd in heads:
                aux_ref[sl, col(AUX_B, hd)] = bc[hd]
            g = [bc[hd][CHUNK - 1:CHUNK, :] for hd in heads]
            qd = [(p_ref[sl, col(0, hd)] * jnp.exp(bc[hd])).astype(BF16) for hd in heads]
            kk = [1.0 - f[hd] for hd in heads]
            ki = [(kk[hd] * jnp.exp(-bc[hd])).astype(BF16) for hd in heads]
            ke = [(kk[hd] * jnp.exp(g[hd] - bc[hd])).astype(BF16) for hd in heads]
            vb = [p_ref[sl, col(1024, hd)].astype(BF16) for hd in heads]
            st = [st_ref[hd] for hd in heads]
            st_b = [a.astype(BF16) for a in st]
            for hd in heads:
                sto_ref[n, hd] = st_b[hd]
            scm = [_dot_nt(qd[hd], ki[hd]) for hd in heads]
            inter = [_dot_nt(qd[hd], st_b[hd]) for hd in heads]
            upd = [_dot_tn(vb[hd], ke[hd]) for hd in heads]
            intra = [_dot(jnp.where(causal, scm[hd], 0.0).astype(BF16), vb[hd]) for hd in heads]
            for hd in heads:
                st_ref[hd] = st[hd] * jnp.exp(g[hd]) + upd[hd]
                o = intra[hd] + inter[hd]
                aux_ref[sl, col(AUX_O, hd)] = o
                ra = lax.rsqrt(jnp.mean(o * o, axis=-1, keepdims=True) + EPS)
                za = p_ref[sl, col(1536, hd)]
                stage[sl, cs[hd]] = (o * ra * ga_ref[:, cs[hd]] * (za * _sigmoid(za))).astype(BF16)
            yb = []
            for hd in heads:
                cu = p_ref[sl, col(3072, hd)] * p_ref[sl, col(2048, hd)]
                tail = tail_ref[:, cs[hd]]
                cv = (cw_ref[0:1, cs[hd]] * _shift_down(cu, 2, tail) + cw_ref[1:2, cs[hd]] * _shift_down(cu, 1, tail)
                      + cw_ref[2:3, cs[hd]] * cu)
                tail_ref[:, cs[hd]] = cu[CHUNK - 8:, :]
                aux_ref[sl, col(AUX_CV, hd)] = cv
                yb.append(p_ref[sl, col(2560, hd)] * cv)
            ms = _group_mean_many([y * y for y in yb], g64m)
            for hd in heads:
                rb = lax.rsqrt(ms[hd] + EPS)
                zb = p_ref[sl, col(3584, hd)]
                stage[sl, col(512, hd)] = (yb[hd] * rb * gcn_ref[:, cs[hd]] * (zb * _sigmoid(zb))).astype(BF16)

        def step(mix, project, entry=False):
            if project:
                mixed_b = ring[pl.ds(pl.multiple_of((i - LAG) * TB, TB), TB), :]
                y = _dot(mixed_b, wog_v[...])
            if mix:
                mix_chunk(0)
            if project:
                x2 = x_ref[...] + y
                r2 = lax.rsqrt(jnp.mean(x2 * x2, axis=-1, keepdims=True) + EPS)
                n2 = x2 * r2
                gfv = gf_ref[...]
                err = n2 * gfv - t_ref[...]
                loss = 0.5 * jnp.sum(jnp.mean(err * err, axis=-1, keepdims=True), axis=0, keepdims=True)
                dy = err * (1.0 / D_MODEL)
                part_ref[1:2, :] += jnp.sum(dy * n2, axis=0, keepdims=True)
                part_ref[7:8, :] += jnp.broadcast_to(loss, (1, D_MODEL))
                dn = dy * gfv
                dx2 = r2 * (dn - n2 * jnp.mean(dn * n2, axis=-1, keepdims=True))
                dx2_ref[...] = dx2
                dx2_b = dx2.astype(BF16)
            if mix:
                mix_chunk(1)
            if entry:
                pl.semaphore_wait(pltpu.get_barrier_semaphore(), 4)
                for cp in wo_direct:
                    cp.start()
            if project:
                dm_ref[...] = _dot_nt(dx2_b, wog_v[...])
            if mix:
                mix_chunk(2)
            if project:
                acc_ref[...] += _dot_tn(mixed_b, dx2_b)
            if mix:
                mix_chunk(3)
                ring[pl.ds(pl.multiple_of(i * TB, TB), TB), :] = stage[...]

        @pl.when(i == 0)
        def _():
            step(True, False, entry=True)

        @pl.when((i > 0) & (i < LAG))
        def _():
            step(True, False)

        @pl.when((i >= LAG) & (i < nblk))
        def _():
            step(True, True)

        @pl.when(i >= nblk)
        def _():
            step(False, True)

        @pl.when(i == n_steps - 1)
        def _():
            gwo_ref[...] = acc_ref[...].astype(BF16)
            for cp in wo_direct + wo_passed:
                cp.wait_send()

    assert NCB == 4
    row = lambda w: pl.BlockSpec((1, w), lambda i: (0, 0))
    mix_blk = lambda i: jnp.minimum(i, nblk - 1)
    out_blk = lambda i: jnp.clip(i - LAG, 0, nblk - 1)
    tok = lambda: pl.BlockSpec((TB, D_MODEL), lambda i: (out_blk(i), 0))
    return pl.pallas_call(
        body, name="mix_out", grid=(n_steps,),
        out_shape=(jax.ShapeDtypeStruct((SEQ, AUX_COLS), F32),
                   jax.ShapeDtypeStruct((N_CHUNKS, N_HEADS, HEAD, HEAD), BF16),
                   jax.ShapeDtypeStruct((SEQ, D_MODEL), F32),
                   jax.ShapeDtypeStruct((SEQ, D_MODEL), F32),
                   jax.ShapeDtypeStruct((D_MODEL, D_MODEL), BF16),
                   jax.ShapeDtypeStruct((8, D_MODEL), F32)),
        in_specs=[pl.BlockSpec((TB, 4096), lambda i: (jnp.minimum(i, nblk - 1), 0)),
                  pl.BlockSpec((2, D_HGRN), lambda i: (0, 0)),
                  pl.BlockSpec((8, D_CONV), lambda i: (0, 0)),
                  row(D_HGRN), row(D_CONV),
                  pl.BlockSpec((HEAD, HEAD), lambda i: (0, 0)),
                  pl.BlockSpec((1, WO_ROWS, D_MODEL), lambda i: (0, 0, 0)),
                  tok(), row(D_MODEL), tok()],
        out_specs=(pl.BlockSpec((TB, AUX_COLS), lambda i: (mix_blk(i), 0)),
                   pl.BlockSpec((NCB, N_HEADS, HEAD, HEAD), lambda i: (mix_blk(i), 0, 0, 0)),
                   tok(), tok(),
                   pl.BlockSpec((D_MODEL, D_MODEL), lambda i: (0, 0)),
                   pl.BlockSpec((8, D_MODEL), lambda i: (0, 0))),
        scratch_shapes=[pltpu.VMEM((N_HEADS, HEAD, HEAD), F32), pltpu.VMEM((8, D_CONV), F32),
                        pltpu.VMEM((D_MODEL, D_MODEL), BF16), pltpu.VMEM((TB, D_MODEL), BF16),
                        pltpu.VMEM((SEQ, D_MODEL), BF16), pltpu.VMEM((D_MODEL, D_MODEL), F32),
                        pltpu.SemaphoreType.DMA((6,)), pltpu.SemaphoreType.DMA((6,))],
        compiler_params=pltpu.CompilerParams(dimension_semantics=("arbitrary",), vmem_limit_bytes=VMEM_LIMIT,
                                             collective_id=COLLECTIVE_MIX_OUT),
    )(proj, lb_logits, cw, ga, gcn, g64, w_out, x2d, gf, tgt)


def _mix_bwd(proj, aux, states, dmixed, lb_logits, cw, ga, gcn, g64):
    nblk = SEQ // TB

    def body(p_ref, aux_ref, st_ref, dm_ref, lbl_ref, cw_ref, ga_ref, gcn_ref, g64_ref,
             dp_ref, part_ref, dst_ref, head_ref, dlb_ref):
        i = pl.program_id(0)

        @pl.when(i == 0)
        def _():
            dst_ref[...] = jnp.zeros_like(dst_ref)
            head_ref[...] = jnp.zeros_like(head_ref)
            part_ref[...] = jnp.zeros_like(part_ref)
            dlb_ref[...] = jnp.zeros_like(dlb_ref)

        lb = _lower_bound(lbl_ref[...])
        triu = _tri(False)
        causal = _causal()
        g64m = g64_ref[...]
        rowsum = lambda a: jnp.sum(a, axis=0, keepdims=True)
        heads = range(N_HEADS)
        cs = [slice(hd * HEAD, (hd + 1) * HEAD) for hd in heads]
        col = lambda base, hd: slice(base + hd * HEAD, base + (hd + 1) * HEAD)
        for n in reversed(range(NCB)):
            sl = pl.ds(n * CHUNK, CHUNK)
            cvv = [aux_ref[sl, col(AUX_CV, hd)] for hd in heads]
            gb = [p_ref[sl, col(2560, hd)] for hd in heads]
            yb = [gb[hd] * cvv[hd] for hd in heads]
            ms = _group_mean_many([y * y for y in yb], g64m)
            rb, nb, dnb = [], [], []
            for hd in heads:
                rb.append(lax.rsqrt(ms[hd] + EPS))
                nb.append(yb[hd] * rb[hd])
                zb = p_ref[sl, col(3584, hd)]
                sgb = _sigmoid(zb)
                dmb = dm_ref[sl, col(512, hd)]
                silu = zb * sgb
                dgate = dmb * gcn_ref[:, cs[hd]]
                part_ref[2:3, col(512, hd)] += rowsum(dmb * nb[hd] * silu)
                dp_ref[sl, col(3584, hd)] = (dgate * nb[hd] * (sgb + silu * (1.0 - sgb))).astype(BF16)
                dnb.append(dgate * silu)
            mdn = _group_mean_many([dnb[hd] * nb[hd] for hd in heads], g64m)
            for hd in heads:
                dyb = rb[hd] * (dnb[hd] - nb[hd] * mdn[hd])
                dp_ref[sl, col(2560, hd)] = (dyb * cvv[hd]).astype(BF16)
                dcv = dyb * gb[hd]
                head = head_ref[:, cs[hd]]
                dcv1 = _shift_up(dcv, 1, head)
                dcv2 = _shift_up(dcv, 2, head)
                head_ref[:, cs[hd]] = dcv[0:8, :]
                u = p_ref[sl, col(2048, hd)]
                gc = p_ref[sl, col(3072, hd)]
                cu = gc * u
                part_ref[4:5, cs[hd]] += rowsum(dcv2 * cu)
                part_ref[5:6, cs[hd]] += rowsum(dcv1 * cu)
                part_ref[6:7, cs[hd]] += rowsum(dcv * cu)
                dcu = cw_ref[2:3, cs[hd]] * dcv + cw_ref[1:2, cs[hd]] * dcv1 + cw_ref[0:1, cs[hd]] * dcv2
                dp_ref[sl, col(3072, hd)] = (dcu * u).astype(BF16)
                dp_ref[sl, col(2048, hd)] = (dcu * gc).astype(BF16)
            do_b = []
            for hd in heads:
                ov = aux_ref[sl, col(AUX_O, hd)]
                ra = lax.rsqrt(jnp.mean(ov * ov, axis=-1, keepdims=True) + EPS)
                na = ov * ra
                za = p_ref[sl, col(1536, hd)]
                sga = _sigmoid(za)
                dma = dm_ref[sl, cs[hd]]
                silu = za * sga
                dgate = dma * ga_ref[:, cs[hd]]
                part_ref[2:3, cs[hd]] += rowsum(dma * na * silu)
                dp_ref[sl, col(1536, hd)] = (dgate * na * (sga + silu * (1.0 - sga))).astype(BF16)
                dna = dgate * silu
                do_b.append((ra * (dna - na * jnp.mean(dna * na, axis=-1, keepdims=True))).astype(BF16))
            s = [_sigmoid(p_ref[sl, col(512, hd)]) for hd in heads]
            f = [lb[:, cs[hd]] + (1.0 - lb[:, cs[hd]]) * s[hd] for hd in heads]
            bc = [aux_ref[sl, col(AUX_B, hd)] for hd in heads]
            g = [bc[hd][CHUNK - 1:CHUNK, :] for hd in heads]
            eb = [jnp.exp(bc[hd]) for hd in heads]
            enb = [jnp.exp(-bc[hd]) for hd in heads]
            eg = [jnp.exp(g[hd] - bc[hd]) for hd in heads]
            dec = [jnp.exp(g[hd]) for hd in heads]
            qd = [p_ref[sl, cs[hd]] * eb[hd] for hd in heads]
            kk = [1.0 - f[hd] for hd in heads]
            ki = [kk[hd] * enb[hd] for hd in heads]
            ke = [kk[hd] * eg[hd] for hd in heads]
            qd_b = [a.astype(BF16) for a in qd]
            ki_b = [a.astype(BF16) for a in ki]
            ke_b = [a.astype(BF16) for a in ke]
            vb = [p_ref[sl, col(1024, hd)].astype(BF16) for hd in heads]
            st_b = [st_ref[n, hd] for hd in heads]
            dst = [dst_ref[hd] for hd in heads]
            dst_b = [a.astype(BF16) for a in dst]
            scm = [_dot_nt(qd_b[hd], ki_b[hd]) for hd in heads]
            amm = [_dot_nt(do_b[hd], vb[hd]) for hd in heads]
            dqd2 = [_dot(do_b[hd], st_b[hd]) for hd in heads]
            dke = [_dot(vb[hd], dst_b[hd]) for hd in heads]
            dv2 = [_dot_nt(ke_b[hd], dst_b[hd]) for hd in heads]
            dsu = [_dot_tn(do_b[hd], qd_b[hd]) for hd in heads]
            sc = [jnp.where(causal, scm[hd], 0.0).astype(BF16) for hd in heads]
            am = [jnp.where(causal, amm[hd], 0.0).astype(BF16) for hd in heads]
            dqd1 = [_dot(am[hd], ki_b[hd]) for hd in heads]
            dki = [_dot_tn(am[hd], qd_b[hd]) for hd in heads]
            dv1 = [_dot_tn(sc[hd], do_b[hd]) for hd in heads]
            db, dgv, dkk = [], [], []
            for hd in heads:
                dqd = dqd1[hd] + dqd2[hd]
                ddec = rowsum(dst[hd] * st_b[hd].astype(F32))
                dst_ref[hd] = dst[hd] * dec[hd] + dsu[hd]
                dp_ref[sl, cs[hd]] = (dqd * eb[hd]).astype(BF16)
                dp_ref[sl, col(1024, hd)] = (dv1[hd] + dv2[hd]).astype(BF16)
                dke_eg = dke[hd] * eg[hd]
                dkk.append(dki[hd] * enb[hd] + dke_eg)
                db.append(dqd * qd[hd] - kk[hd] * dkk[hd])
                dgv.append(rowsum(kk[hd] * dke_eg) + ddec * dec[hd])
            rc = _exact_left_many(triu, db, 2)
            for hd in heads:
                df = (rc[hd] + dgv[hd]) / f[hd] - dkk[hd]
                one_s = 1.0 - s[hd]
                dlb_ref[:, cs[hd]] += rowsum(df * one_s)
                dp_ref[sl, col(512, hd)] = (df * (1.0 - lb[:, cs[hd]]) * s[hd] * one_s).astype(BF16)

        @pl.when(i == nblk - 1)
        def _():
            row = dlb_ref[...] * lb * (1.0 - lb)
            part_ref[3:4, 0:D_HGRN] = row
            part_ref[3:4, D_HGRN:] = -row

    rev = lambda w: pl.BlockSpec((TB, w), lambda i: (nblk - 1 - i, 0))
    row = lambda w: pl.BlockSpec((1, w), lambda i: (0, 0))
    return pl.pallas_call(
        body, name="mix_bwd", grid=(nblk,),
        out_shape=(jax.ShapeDtypeStruct((SEQ, 4096), BF16),
                   jax.ShapeDtypeStruct((8, D_MODEL), F32)),
        in_specs=[rev(4096), rev(AUX_COLS),
                  pl.BlockSpec((NCB, N_HEADS, HEAD, HEAD), lambda i: (nblk - 1 - i, 0, 0, 0)),
                  rev(D_MODEL),
                  pl.BlockSpec((2, D_HGRN), lambda i: (0, 0)),
                  pl.BlockSpec((8, D_CONV), lambda i: (0, 0)),
                  row(D_HGRN), row(D_CONV),
                  pl.BlockSpec((HEAD, HEAD), lambda i: (0, 0))],
        out_specs=(rev(4096), pl.BlockSpec((8, D_MODEL), lambda i: (0, 0))),
        scratch_shapes=[pltpu.VMEM((N_HEADS, HEAD, HEAD), F32), pltpu.VMEM((8, D_CONV), F32),
                        pltpu.VMEM((1, D_HGRN), F32)],
        compiler_params=pltpu.CompilerParams(dimension_semantics=("arbitrary",), vmem_limit_bytes=VMEM_LIMIT),
    )(proj, aux, states, dmixed, lb_logits, cw, ga, gcn, g64)


TT = 1024
TX = 512
(SEM_D2D, SEM_D2D_O, SEM_ICI, SEM_ICI_O, SEM_FIN, SEM_FIN_O, SEM_SMALL, SEM_VIA, SEM_NORM, N_SEM_TAIL) = (
    0, 4, 5, 8, 11, 12, 12, 20, 22, 30)


def _bwd_tail(kidx, h, dproj, wg, gwo, x2d, dx2, g1, small_a, small_b):
    hw = D_MODEL // 2
    ho = WO_ROWS // 2
    nt = SEQ // TT
    norm_step = 2 * N_SHARD
    n_steps = norm_step + SEQ // TX // nt

    def body(k_ref, h_ref, dp_ref, w_ref, gwo_ref, x_ref, dx2_ref, g_ref, sm_ref, smb_ref,
             gx_ref, gw_out, gwo_out, osm_ref,
             acc, dh, sendbuf, keep, sibrcv, rcv, merge, sib_o, p_o, rcv_o, res_o, sm_buf, dng_buf, dng,
             send_sems, recv_sems, out_sems):
        s, t = pl.program_id(0), pl.program_id(1)
        x, y, c = lax.axis_index("x"), lax.axis_index("y"), lax.axis_index("c")
        k = 2 * x + y
        me = 4 * x + 2 * y + c
        sibling = (x, y, 1 - c)
        chips = [(1 - x, 1 - y), (1 - x, y), (x, 1 - y)]
        kjs = [2 * cx + cy for cx, cy in chips]
        mine = pl.ds(pl.multiple_of(c * hw, hw), hw)
        other = pl.ds(pl.multiple_of((1 - c) * hw, hw), hw)
        mine_o = pl.ds(pl.multiple_of(c * ho, ho), ho)
        other_o = pl.ds(pl.multiple_of((1 - c) * ho, ho), ho)

        def copy(sem, src, dst, to):
            return pltpu.make_async_remote_copy(
                src_ref=src, dst_ref=dst, send_sem=send_sems.at[sem], recv_sem=recv_sems.at[sem],
                device_id=to, device_id_type=MESH)

        def at_step(sv, tv):
            return pl.when((s == sv) & (t == tv))

        def at_norm_block(b):
            return at_step(norm_step + b // nt, b % nt)

        d2d = [copy(SEM_D2D + sv, sendbuf.at[sv], sibrcv.at[sv], sibling) for sv in range(N_SHARD)]
        d2d_o = copy(SEM_D2D_O, gwo_ref.at[:, other_o, :], sib_o, sibling)
        ici = {sv: copy(SEM_ICI + sv, keep.at[sv], rcv.at[sv - 1], (*chips[sv], c)) for sv in (1, 2)}
        qh = hw // 2
        via = [copy(SEM_VIA, keep.at[0, 0:qh, :], merge.at[1], (*chips[1], c)),
               copy(SEM_VIA + 1, keep.at[0, qh:hw, :], merge.at[0], (*chips[2], c))]
        merged_rows = [slice(qh, hw), slice(0, qh)]
        ici_o = [copy(SEM_ICI_O + sv, p_o.at[kjs[sv]], rcv_o.at[sv], (*chips[sv], c)) for sv in range(3)]
        fin = copy(SEM_FIN, acc.at[mine, :], gw_out.at[mine, :], sibling)
        fin_o = copy(SEM_FIN_O, res_o.at[mine_o, :], res_o.at[mine_o, :], sibling)
        peers = [(x ^ (m >> 2), y ^ ((m >> 1) & 1), c ^ (m & 1)) for m in range(1, N_DEV)]
        smalls = [copy(SEM_SMALL + 1 + j, sm_buf.at[me], sm_buf.at[me], to) for j, to in enumerate(peers)]
        dngs = [copy(SEM_NORM + 1 + j, dng_buf.at[me], dng_buf.at[me], to) for j, to in enumerate(peers)]
        store_w = pltpu.make_async_copy(acc.at[mine, :], gw_out.at[mine, :], out_sems.at[0])
        store_o = pltpu.make_async_copy(res_o, gwo_out, out_sems.at[1])

        @at_step(0, 0)
        def _():
            barrier = pltpu.get_barrier_semaphore()
            for to in peers:
                pl.semaphore_signal(barrier, inc=1, device_id=to, device_id_type=MESH)
            sm_buf[me] = sm_ref[...] + smb_ref[...]
            pl.semaphore_wait(barrier, N_DEV - 1)
            d2d_o.start()
            for cp in smalls:
                cp.start()

        @at_step(0, 1)
        def _():
            d2d_o.wait_recv()
            for j in range(N_SHARD):
                p_o[j] = (gwo_ref[j, mine_o, :].astype(F32) + sib_o[j].astype(F32)).astype(BF16)
            res_o[mine_o, :] = gwo_ref[k, mine_o, :].astype(F32) + sib_o[k].astype(F32)
            for cp in ici_o:
                cp.start()

        rows = pl.ds(pl.multiple_of(t * TT, TT), TT)

        @pl.when((s < N_SHARD) & (t == 0))
        def _():
            acc[...] = _dot_tn(h_ref[...], dp_ref[...])

        @pl.when((s < N_SHARD) & (t > 0))
        def _():
            acc[...] += _dot_tn(h_ref[...], dp_ref[...])

        for sv in range(N_SHARD):
            @at_step(sv, nt - 1)
            def _(sv=sv):
                sendbuf[sv] = acc[other, :].astype(BF16)
                if sv < 3:
                    keep[sv] = acc[mine, :].astype(BF16)
                d2d[sv].start()

        @at_step(1, 0)
        def _():
            d2d[0].wait_recv()
            keep[0] = (keep[0].astype(F32) + sibrcv[0].astype(F32)).astype(BF16)
            for cp in via:
                cp.start()

        for sv in (1, 2):
            @at_step(sv + 2, 0)
            def _(sv=sv):
                d2d[sv].wait_recv()
                keep[sv] = (keep[sv].astype(F32) + sibrcv[sv].astype(F32)).astype(BF16)
                via[2 - sv].wait_recv()
                rows_m = merged_rows[sv - 1]
                keep[sv, rows_m, :] = (keep[sv, rows_m, :].astype(F32) + merge[sv - 1].astype(F32)).astype(BF16)
                ici[sv].start()

        @pl.when(s == N_SHARD)
        def _():
            dh[rows, :] = _dot_nt(dp_ref[...], w_ref[0])

        @pl.when((s > N_SHARD) & (s < norm_step))
        def _():
            dh[rows, :] += _dot_nt(dp_ref[...], w_ref[0])

        @at_norm_block(0)
        def _():
            d2d[3].wait_recv()
            acc[mine, :] += sibrcv[3].astype(F32)

        @at_norm_block(1)
        def _():
            tot = res_o[mine_o, :]
            for sv in range(3):
                ici_o[sv].wait_recv()
                tot = tot + rcv_o[sv].astype(F32)
            res_o[mine_o, :] = tot
            fin_o.start()

        @at_norm_block(2)
        def _():
            ici[1].wait_recv()
            acc[mine, :] += rcv[0].astype(F32)

        @at_norm_block(SEQ // TX - 2)
        def _():
            ici[2].wait_recv()
            acc[mine, :] += rcv[1].astype(F32)
            fin.start()
            store_w.start()
            fin_o.wait_recv()
            store_o.start()

        @at_norm_block(0)
        def _():
            dng[...] = jnp.zeros_like(dng)

        @pl.when(s >= norm_step)
        def _():
            blk = (s - norm_step) * nt + t
            dhv = dh[pl.ds(pl.multiple_of(blk * TX, TX), TX), :]
            xv = x_ref[...]
            r = lax.rsqrt(jnp.mean(xv * xv, axis=-1, keepdims=True) + EPS)
            xn = xv * r
            dng[...] += jnp.sum(dhv * xn, axis=0, keepdims=True)
            dxn = dhv * g_ref[...]
            gx_ref[...] = dx2_ref[...] + r * (dxn - xn * jnp.mean(dxn * xn, axis=-1, keepdims=True))

        @at_step(n_steps - 1, nt - 1)
        def _():
            dng_buf[me] = dng[...]
            for cp in dngs:
                cp.start()
            for m in range(1, N_DEV):
                copy(SEM_SMALL + m, sm_buf.at[0], sm_buf.at[0], sibling).wait_recv()
            tot = sm_buf[0]
            for d in range(1, N_DEV):
                tot = tot + sm_buf[d]
            osm_ref[...] = tot
            for m in range(1, N_DEV):
                copy(SEM_NORM + m, dng_buf.at[0], dng_buf.at[0], sibling).wait_recv()
            tot = dng_buf[0]
            for d in range(1, N_DEV):
                tot = tot + dng_buf[d]
            osm_ref[0:1, :] = tot
            fin.wait_recv()
            for cp in d2d + [d2d_o] + via + list(ici.values()) + ici_o + [fin, fin_o] + smalls + dngs:
                cp.wait_send()
            store_o.wait()
            store_w.wait()

    def shard_of(s, kr):
        order = jnp.where(s < N_SHARD, s, jnp.where(s < norm_step, s - N_SHARD, 3))
        return kr[0] ^ (3 - order)

    def h_map(s, t, kr):
        return (jnp.where(s < N_SHARD, t, nt - 1), 0)

    def dp_map(s, t, kr):
        return (jnp.where(s < norm_step, t, nt - 1), shard_of(s, kr))

    def w_map(s, t, kr):
        return (shard_of(jnp.maximum(s, N_SHARD), kr), 0, 0)

    def blk_map(s, t, kr):
        return (jnp.where(s < norm_step, 0, (s - norm_step) * nt + t), 0)

    hbm = pl.BlockSpec(memory_space=pl.ANY)
    grid_spec = pltpu.PrefetchScalarGridSpec(
        num_scalar_prefetch=1, grid=(n_steps, nt),
        in_specs=[pl.BlockSpec((TT, D_MODEL), h_map),
                  pl.BlockSpec((TT, SHARD_COLS), dp_map),
                  pl.BlockSpec((1, D_MODEL, SHARD_COLS), w_map),
                  pl.BlockSpec((N_SHARD, WO_ROWS, D_MODEL), lambda s, t, kr: (0, 0, 0),
                               pipeline_mode=pl.Buffered(1)),
                  pl.BlockSpec((TX, D_MODEL), blk_map),
                  pl.BlockSpec((TX, D_MODEL), blk_map),
                  pl.BlockSpec((1, D_MODEL), lambda s, t, kr: (0, 0)),
                  pl.BlockSpec((8, D_MODEL), lambda s, t, kr: (0, 0)),
                  pl.BlockSpec((8, D_MODEL), lambda s, t, kr: (0, 0))],
        out_specs=(pl.BlockSpec((TX, D_MODEL), blk_map), hbm, hbm,
                   pl.BlockSpec((8, D_MODEL), lambda s, t, kr: (0, 0))),
        scratch_shapes=[pltpu.VMEM((D_MODEL, SHARD_COLS), F32), pltpu.VMEM((SEQ, D_MODEL), F32),
                        pltpu.VMEM((N_SHARD, hw, SHARD_COLS), BF16), pltpu.VMEM((3, hw, SHARD_COLS), BF16),
                        pltpu.VMEM((N_SHARD, hw, SHARD_COLS), BF16), pltpu.VMEM((2, hw, SHARD_COLS), BF16),
                        pltpu.VMEM((2, hw // 2, SHARD_COLS), BF16),
                        pltpu.VMEM((N_SHARD, ho, D_MODEL), BF16), pltpu.VMEM((N_SHARD, ho, D_MODEL), BF16),
                        pltpu.VMEM((3, ho, D_MODEL), BF16), pltpu.VMEM((WO_ROWS, D_MODEL), F32),
                        pltpu.VMEM((N_DEV, 8, D_MODEL), F32), pltpu.VMEM((N_DEV, 1, D_MODEL), F32),
                        pltpu.VMEM((1, D_MODEL), F32),
                        pltpu.SemaphoreType.DMA((N_SEM_TAIL,)), pltpu.SemaphoreType.DMA((N_SEM_TAIL,)),
                        pltpu.SemaphoreType.DMA((2,))])
    return pl.pallas_call(
        body, name="bwd_tail", grid_spec=grid_spec,
        out_shape=(jax.ShapeDtypeStruct((SEQ, D_MODEL), F32),
                   jax.ShapeDtypeStruct((D_MODEL, SHARD_COLS), F32),
                   jax.ShapeDtypeStruct((WO_ROWS, D_MODEL), F32),
                   jax.ShapeDtypeStruct((8, D_MODEL), F32)),
        compiler_params=pltpu.CompilerParams(dimension_semantics=("arbitrary", "arbitrary"),
                                             vmem_limit_bytes=61 * 1024 * 1024, collective_id=COLLECTIVE_TAIL),
    )(kidx, h, dproj, wg, gwo, x2d, dx2, g1, small_a, small_b)


def _adam_update(w, g, m, v):
    nm = ADAM_B1 * m + (1.0 - ADAM_B1) * g
    nv = ADAM_B2 * v + (1.0 - ADAM_B2) * (g * g)
    m_hat = nm / (1.0 - ADAM_B1 ** ADAM_STEP)
    v_hat = nv / (1.0 - ADAM_B2 ** ADAM_STEP)
    return -ADAM_LR * (m_hat / (jnp.sqrt(v_hat) + ADAM_EPS) + ADAM_WD * w), nm, nv


def _adamw_all(tot, g_w_in, g_w_out, big, small, grad_x):
    n = len(small)
    rows = WO_ROWS
    steps = D_MODEL // rows

    def body(tot_ref, *refs):
        gx_ref, gx_out = refs[2 + 3 * (2 + n)], refs[-1]
        gx_out[...] = gx_ref[...]
        ins, outs = refs[:2 + 3 * (2 + n)], refs[3 + 3 * (2 + n):-1]
        g_refs, wmv = ins[:2], ins[2:]
        loss_ref, quads = outs[0], outs[1:]

        def update(j, g):
            w_ref, m_ref, v_ref = wmv[3 * j:3 * j + 3]
            g_ref, d_ref, nm_ref, nv_ref = quads[4 * j:4 * j + 4]
            g_ref[...] = g
            d_ref[...], nm_ref[...], nv_ref[...] = _adam_update(w_ref[...], g, m_ref[...], v_ref[...])

        update(0, g_refs[0][...])

        @pl.when(pl.program_id(0) == 0)
        def _():
            update(1, g_refs[1][...])
            k = 2 * lax.axis_index("x") + lax.axis_index("y")
            mine = pl.ds(pl.multiple_of(k * HEAD, HEAD), HEAD)
            loss_ref[...] = tot_ref[7:8, 0:1]
            grads = [tot_ref[0:1, :], tot_ref[1:2, :], tot_ref[2:3, 0:D_HGRN], tot_ref[2:3, D_HGRN:],
                     jnp.concatenate([tot_ref[3:4, 0:D_HGRN], tot_ref[3:4, D_HGRN:]], axis=0),
                     jnp.concatenate([tot_ref[4 + tap:5 + tap, mine] for tap in range(3)], axis=1)]
            for j, g in enumerate(grads):
                update(2 + j, g)

    whole = lambda a: pl.BlockSpec(a.shape, lambda i: (0, 0))
    blk = pl.BlockSpec((rows, SHARD_COLS), lambda i: (i, 0))
    arrays = [a for triple in big + small for a in triple]
    in_specs = ([whole(tot), blk, whole(g_w_out)] + [blk] * 3 + [whole(a) for a in arrays[3:]])
    shapes = [big[0][0], big[1][0]] + [w for w, _, _ in small]
    out_shape = (jax.ShapeDtypeStruct((1, 1), F32),) + tuple(
        jax.ShapeDtypeStruct(w.shape, F32) for w in shapes for _ in range(4))
    out_specs = (pl.BlockSpec((1, 1), lambda i: (0, 0)),) + (blk,) * 4 + tuple(
        whole(w) for w in shapes[1:] for _ in range(4))
    gx_blk = pl.BlockSpec((SEQ // steps, D_MODEL), lambda i: (i, 0))
    outs = pl.pallas_call(
        body, name="adamw_all", grid=(steps,),
        out_shape=out_shape + (jax.ShapeDtypeStruct(grad_x.shape, F32),),
        in_specs=in_specs + [gx_blk], out_specs=out_specs + (gx_blk,),
        compiler_params=pltpu.CompilerParams(dimension_semantics=("arbitrary",), vmem_limit_bytes=VMEM_LIMIT),
    )(tot, g_w_in, g_w_out, *arrays, grad_x)
    return [outs[0]] + [outs[1 + 4 * j:5 + 4 * j] for j in range(2 + n)] + [outs[-1]]


def _local_step(x2d, tgt, proj, lb_logits, cw, ga, gcn, w_out, gf):
    g64 = _group_matrix(HEAD, CONV_GROUP)
    aux, states, dx2, dmixed, gwo, part_out = _mix_out(proj, lb_logits, cw, ga, gcn, g64, w_out, x2d, gf, tgt)
    dproj, part_mix = _mix_bwd(proj, aux, states, dmixed, lb_logits, cw, ga, gcn, g64)
    return dproj, dx2, gwo.reshape(N_SHARD, WO_ROWS, D_MODEL), part_out, part_mix


def kernel(x, norm_gain, w_in, lb_logits, conv_w, hgrn_norm_gain, conv_norm_gain, w_out, final_norm_gain, loss_target, m_norm_gain, m_w_in, m_lb_logits, m_conv_w, m_hgrn_norm_gain, m_conv_norm_gain, m_w_out, m_final_norm_gain, v_norm_gain, v_w_in, v_lb_logits, v_conv_w, v_hgrn_norm_gain, v_conv_norm_gain, v_w_out, v_final_norm_gain):
    k = 2 * lax.axis_index("x") + lax.axis_index("y")
    kidx = jnp.reshape(k, (1,)).astype(jnp.int32)
    row = lambda a: a.reshape(1, D_MODEL)
    taps = lambda a: a.reshape(1, 3 * HEAD)
    h, proj, wg, cw = _gather_proj(kidx, x[0], norm_gain, w_in, taps(conv_w))
    dproj, dx2, gwo, part_out, part_mix = _local_step(
        x[0], loss_target[0], proj, lb_logits, cw, hgrn_norm_gain, conv_norm_gain, w_out, row(final_norm_gain))
    rgrad_x, rg_w_in, rg_w_out, tot = _bwd_tail(kidx, h, dproj, wg, gwo, x[0], dx2, norm_gain, part_out, part_mix)

    (loss, (g_w_in, d_w_in, nm_w_in, nv_w_in), (g_w_out, d_w_out, nm_w_out, nv_w_out),
     (g_norm_gain, d_ng, nm_ng, nv_ng), (g_final, d_fg, nm_fg, nv_fg), (g_hgrn, d_hg, nm_hg, nv_hg),
     (g_convn, d_cg, nm_cg, nv_cg), (g_lb, d_lb, nm_lb, nv_lb), (g_conv_w, d_cw, nm_cw, nv_cw),
     grad_x) = _adamw_all(
        tot, rg_w_in, rg_w_out,
        [(w_in[0], m_w_in[0], v_w_in[0]), (w_out[0], m_w_out[0], v_w_out[0])],
        [(norm_gain, m_norm_gain, v_norm_gain),
         (row(final_norm_gain), row(m_final_norm_gain), row(v_final_norm_gain)),
         (hgrn_norm_gain, m_hgrn_norm_gain, v_hgrn_norm_gain),
         (conv_norm_gain, m_conv_norm_gain, v_conv_norm_gain),
         (lb_logits, m_lb_logits, v_lb_logits),
         (taps(conv_w), taps(m_conv_w), taps(v_conv_w))],
        rgrad_x)
    flat = lambda a: a.reshape(D_MODEL)
    untap = lambda a: a.reshape(1, 3, HEAD)
    return (loss.reshape(()), grad_x[None],
            g_norm_gain, g_w_in[None], g_lb, untap(g_conv_w), g_hgrn, g_convn, g_w_out[None], flat(g_final),
            d_ng, d_w_in[None], d_lb, untap(d_cw), d_hg, d_cg, d_w_out[None], flat(d_fg),
            nm_ng, nm_w_in[None], nm_lb, untap(nm_cw), nm_hg, nm_cg, nm_w_out[None], flat(nm_fg),
            nv_ng, nv_w_in[None], nv_lb, untap(nv_cw), nv_hg, nv_cg, nv_w_out[None], flat(nv_fg))
```

```python
import jax
import jax.numpy as jnp
import numpy as np
from jax import lax
from jax.experimental import pallas as pl
from jax.experimental.pallas import tpu as pltpu

F32 = jnp.float32
BF16 = jnp.bfloat16
MESH = pl.DeviceIdType.MESH

SEQ = 2048
D_MODEL = 1024
D_HGRN = 512
D_CONV = 512
HEAD = 128
N_HEADS = 4
CHUNK = 64
CONV_GROUP = 64
N_SHARD = 4
SHARD_COLS = 1024
WO_ROWS = 256
EPS = 1e-6
TB = 256
NCB = TB // CHUNK
N_CHUNKS = SEQ // CHUNK
N_DEV = 8
COLLECTIVE_GATHER, COLLECTIVE_MIX_OUT, COLLECTIVE_TAIL = 1, 0, 2
AUX_O, AUX_CV, AUX_B, AUX_COLS = 0, 512, 1024, 1536

ADAM_LR = 0.001
ADAM_B1 = 0.9
ADAM_B2 = 0.999
ADAM_EPS = 1e-08
ADAM_WD = 0.01
ADAM_STEP = 10

VMEM_LIMIT = 56 * 1024 * 1024


def _dot(a, b):
    return jnp.dot(a, b, preferred_element_type=F32)


def _dot_nt(a, b):
    return lax.dot_general(a, b, (((1,), (1,)), ((), ())), preferred_element_type=F32)


def _dot_tn(a, b):
    return lax.dot_general(a, b, (((0,), (0,)), ((), ())), preferred_element_type=F32)


def _split_bf16(x, n):
    parts = []
    r = x
    for _ in range(n):
        p = r.astype(BF16)
        parts.append(p)
        r = r - p.astype(F32)
    return parts


def _exact_left(m, x, n=3):
    acc = None
    for p in _split_bf16(x, n):
        t = _dot(m, p)
        acc = t if acc is None else acc + t
    return acc


def _exact_left_many(m, xs, n=3):
    parts = [_split_bf16(x, n) for x in xs]
    accs = [None] * len(xs)
    for i in range(n):
        for j in range(len(xs)):
            t = _dot(m, parts[j][i])
            accs[j] = t if accs[j] is None else accs[j] + t
    return accs


def _group_mean_many(xs, gmat, n=2):
    parts = [_split_bf16(x, n) for x in xs]
    accs = [None] * len(xs)
    for i in range(n):
        for j in range(len(xs)):
            t = _dot(parts[j][i], gmat)
            accs[j] = t if accs[j] is None else accs[j] + t
    return accs


def _group_mean(x, gmat, n=2):
    w = gmat.shape[0]
    outs = []
    for c0 in range(0, x.shape[1], w):
        acc = None
        for p in _split_bf16(x[:, c0:c0 + w], n):
            t = _dot(p, gmat)
            acc = t if acc is None else acc + t
        outs.append(acc)
    return jnp.concatenate(outs, axis=1)


def _sigmoid(x):
    return 1.0 / (1.0 + jnp.exp(-x))


def _lower_bound(lbl):
    l0 = lbl[0:1, :]
    l1 = lbl[1:2, :]
    m = jnp.maximum(l0, l1)
    e0 = jnp.exp(l0 - m)
    e1 = jnp.exp(l1 - m)
    return e0 / (e0 + e1)


def _tri(lower):
    r = lax.broadcasted_iota(jnp.int32, (CHUNK, CHUNK), 0)
    c = lax.broadcasted_iota(jnp.int32, (CHUNK, CHUNK), 1)
    return jnp.where((c <= r) if lower else (c >= r), 1.0, 0.0).astype(BF16)


def _causal():
    r = lax.broadcasted_iota(jnp.int32, (CHUNK, CHUNK), 0)
    c = lax.broadcasted_iota(jnp.int32, (CHUNK, CHUNK), 1)
    return c <= r


def _shift_down(x, sh, prev_tail):
    r = pltpu.roll(x, sh, 0)
    pt = pltpu.roll(prev_tail, sh, 0)
    rows = lax.broadcasted_iota(jnp.int32, prev_tail.shape, 0)
    top = jnp.where(rows < sh, pt, r[0:8])
    return jnp.concatenate([top, r[8:]], axis=0)


def _shift_up(x, sh, next_head):
    n = x.shape[0]
    r = pltpu.roll(x, n - sh, 0)
    nh = pltpu.roll(next_head, 8 - sh, 0)
    rows = lax.broadcasted_iota(jnp.int32, next_head.shape, 0)
    bot = jnp.where(rows >= 8 - sh, nh, r[n - 8:])
    return jnp.concatenate([r[:n - 8], bot], axis=0)


def _group_matrix(width, group):
    r = np.arange(width)[:, None] // group
    c = np.arange(width)[None, :] // group
    return jnp.asarray(np.where(r == c, 1.0 / group, 0.0), dtype=BF16)


TG = 1024
SEM_W, SEM_CW, SEM_W_FWD, N_SEM = 0, 4, 7, 11


def _gather_proj(kidx, x2d, g1, w_in, conv_w):
    half_w = D_MODEL // 2
    half_c = SHARD_COLS // 2
    nt = SEQ // TG
    n_steps = 2 * N_SHARD

    def body(k_ref, x_ref, g_ref, w_ref, cw_ref, h_out, p_ref, wg_out, cwg_out,
             wg_v, cwg_v, h_ref, send_sems, recv_sems, out_sems):
        s, t = pl.program_id(0), pl.program_id(1)
        x, y, c = lax.axis_index("x"), lax.axis_index("y"), lax.axis_index("c")
        k = 2 * x + y
        sibling = (x, y, 1 - c)
        chips = [(1 - x, y), (x, 1 - y), (1 - x, 1 - y)]
        kjs = [2 * cx + cy for cx, cy in chips]
        diag = (*chips[2], c)

        def w_half(kk, cc):
            return wg_v.at[kk, pl.ds(cc * half_w, half_w), :]

        def w_quarter(kk, cc, piece):
            return wg_v.at[kk, pl.ds(cc * half_w, half_w), piece * half_c:(piece + 1) * half_c]

        def cw_of(kk):
            return cwg_v.at[:, pl.ds(pl.multiple_of(kk * HEAD, HEAD), HEAD)]

        def copy(sem, ref, to):
            return pltpu.make_async_remote_copy(
                src_ref=ref, dst_ref=ref, send_sem=send_sems.at[sem], recv_sem=recv_sems.at[sem],
                device_id=to, device_id_type=MESH)

        def at_step(sv, tv):
            return pl.when((s == sv) & (t == tv))

        w_direct = ([copy(SEM_W + j, w_half(k, c), (*chips[j], c)) for j in range(2)]
                    + [copy(SEM_W + 2 + p, w_quarter(k, c, p), diag) for p in range(2)])
        cw_direct = [copy(SEM_CW + j, cw_of(k), (*chip, c)) for j, chip in enumerate(chips)]
        w_passed = ([copy(SEM_W_FWD + j, w_half(kjs[j], c), sibling) for j in range(2)]
                    + [copy(SEM_W_FWD + 2 + p, w_quarter(kjs[2], c, p), sibling) for p in range(2)])
        stores = ([pltpu.make_async_copy(wg_v.at[kk], wg_out.at[kk], out_sems.at[i])
                   for i, kk in enumerate([k] + kjs)]
                  + [pltpu.make_async_copy(cwg_v, cwg_out, out_sems.at[4]),
                     pltpu.make_async_copy(h_ref, h_out, out_sems.at[5])])

        @at_step(0, 0)
        def _():
            barrier = pltpu.get_barrier_semaphore()
            for peer in [sibling] + [(*chip, c) for chip in chips]:
                pl.semaphore_signal(barrier, inc=1, device_id=peer, device_id_type=MESH)
            wg_v[k] = w_ref[0].astype(BF16)
            mine = pl.ds(pl.multiple_of(k * HEAD, HEAD), HEAD)
            cwg_v[:, mine] = jnp.zeros((8, HEAD), F32)
            for tap in range(3):
                cwg_v[tap:tap + 1, mine] = cw_ref[:, tap * HEAD:(tap + 1) * HEAD]
            pl.semaphore_wait(barrier, 4)
            for cp in w_direct + cw_direct:
                cp.start()
            stores[0].start()

        @at_step(1, 0)
        def _():
            stores[5].start()

        @at_step(2, 0)
        def _():
            for j in range(2):
                copy(SEM_W + j, w_half(kjs[j], c), sibling).wait_recv()
                w_passed[j].start()
            copy(SEM_W_FWD, w_half(kjs[0], 1 - c), sibling).wait_recv()
            stores[1].start()

        @at_step(4, 0)
        def _():
            copy(SEM_W_FWD + 1, w_half(kjs[1], 1 - c), sibling).wait_recv()
            stores[2].start()

        for p in range(2):
            @at_step(6 + p, 0)
            def _(p=p):
                copy(SEM_W + 2 + p, w_quarter(kjs[2], c, p), sibling).wait_recv()
                w_passed[2 + p].start()
                copy(SEM_W_FWD + 2 + p, w_quarter(kjs[2], 1 - c, p), sibling).wait_recv()
                if p == 1:
                    stores[3].start()
                    for j in range(3):
                        copy(SEM_CW + j, cw_of(kjs[j]), sibling).wait_recv()
                    stores[4].start()

        rows = pl.ds(pl.multiple_of(t * TG, TG), TG)

        @pl.when(s == 0)
        def _():
            xv = x_ref[...]
            r = lax.rsqrt(jnp.mean(xv * xv, axis=-1, keepdims=True) + EPS)
            h_ref[rows, :] = (xv * r * g_ref[...]).astype(BF16)

        sh = s >> 1
        js = k ^ (((sh & 1) << 1) | (sh >> 1))
        for piece in range(2):
            @pl.when((s & 1) == piece)
            def _(piece=piece):
                p_ref[...] = _dot(h_ref[rows, :], wg_v[js, :, piece * half_c:(piece + 1) * half_c])

        @at_step(n_steps - 1, nt - 1)
        def _():
            for cp in w_direct + cw_direct + w_passed:
                cp.wait_send()
            for st in stores:
                st.wait()

    def x_map(s, t, kr):
        return (jnp.where(s == 0, t, nt - 1), 0)

    def p_map(s, t, kr):
        sh = s >> 1
        return (t, 2 * (kr[0] ^ (((sh & 1) << 1) | (sh >> 1))) + (s & 1))

    hbm = pl.BlockSpec(memory_space=pl.ANY)
    grid_spec = pltpu.PrefetchScalarGridSpec(
        num_scalar_prefetch=1, grid=(n_steps, nt),
        in_specs=[pl.BlockSpec((TG, D_MODEL), x_map),
                  pl.BlockSpec((1, D_MODEL), lambda s, t, kr: (0, 0)),
                  pl.BlockSpec((1, D_MODEL, SHARD_COLS), lambda s, t, kr: (0, 0, 0)),
                  pl.BlockSpec((1, 3 * HEAD), lambda s, t, kr: (0, 0))],
        out_specs=(hbm, pl.BlockSpec((TG, half_c), p_map), hbm, hbm),
        scratch_shapes=[pltpu.VMEM((N_SHARD, D_MODEL, SHARD_COLS), BF16),
                        pltpu.VMEM((8, D_CONV), F32), pltpu.VMEM((SEQ, D_MODEL), BF16),
                        pltpu.SemaphoreType.DMA((N_SEM,)), pltpu.SemaphoreType.DMA((N_SEM,)),
                        pltpu.SemaphoreType.DMA((6,))])
    return pl.pallas_call(
        body, name="gather_proj", grid_spec=grid_spec,
        out_shape=(jax.ShapeDtypeStruct((SEQ, D_MODEL), BF16),
                   jax.ShapeDtypeStruct((SEQ, N_SHARD * SHARD_COLS), F32),
                   jax.ShapeDtypeStruct((N_SHARD, D_MODEL, SHARD_COLS), BF16),
                   jax.ShapeDtypeStruct((8, D_CONV), F32)),
        compiler_params=pltpu.CompilerParams(dimension_semantics=("arbitrary", "arbitrary"),
                                             vmem_limit_bytes=VMEM_LIMIT, collective_id=COLLECTIVE_GATHER),
    )(kidx, x2d, g1, w_in, conv_w)


LAG = 6


def _mix_out(proj, lb_logits, cw, ga, gcn, g64, w_out, x2d, gf, tgt):
    half_o = WO_ROWS // 2
    nblk = SEQ // TB
    n_steps = nblk + LAG

    def body(p_ref, lbl_ref, cw_ref, ga_ref, gcn_ref, g64_ref, wo_ref, x_ref, gf_ref, t_ref,
             aux_ref, sto_ref, dx2_ref, dm_ref, gwo_ref, part_ref,
             st_ref, tail_ref, wog_v, stage, ring, acc_ref, send_sems, recv_sems):
        i = pl.program_id(0)
        x, y, c = lax.axis_index("x"), lax.axis_index("y"), lax.axis_index("c")
        k = 2 * x + y
        sibling = (x, y, 1 - c)
        chips = [(1 - x, y), (x, 1 - y), (1 - x, 1 - y)]
        kjs = [2 * cx + cy for cx, cy in chips]

        def wo_half(kk, cc):
            return wog_v.at[pl.ds(pl.multiple_of(kk * WO_ROWS + cc * half_o, half_o), half_o), :]

        def copy(sem, ref, to):
            return pltpu.make_async_remote_copy(
                src_ref=ref, dst_ref=ref, send_sem=send_sems.at[sem], recv_sem=recv_sems.at[sem],
                device_id=to, device_id_type=MESH)

        wo_direct = [copy(j, wo_half(k, c), (*chip, c)) for j, chip in enumerate(chips)]
        wo_passed = [copy(3 + j, wo_half(kj, c), sibling) for j, kj in enumerate(kjs)]

        @pl.when(i == 0)
        def _():
            barrier = pltpu.get_barrier_semaphore()
            for peer in [sibling] + [(*chip, c) for chip in chips]:
                pl.semaphore_signal(barrier, inc=1, device_id=peer, device_id_type=MESH)
            st_ref[...] = jnp.zeros_like(st_ref)
            tail_ref[...] = jnp.zeros_like(tail_ref)
            acc_ref[...] = jnp.zeros_like(acc_ref)
            part_ref[...] = jnp.zeros_like(part_ref)
            wog_v[pl.ds(pl.multiple_of(k * WO_ROWS, WO_ROWS), WO_ROWS), :] = wo_ref[0].astype(BF16)

        @pl.when(i == LAG - 1)
        def _():
            for j in range(3):
                copy(j, wo_half(kjs[j], c), sibling).wait_recv()
                wo_passed[j].start()

        @pl.when(i == LAG)
        def _():
            for j in range(3):
                copy(3 + j, wo_half(kjs[j], 1 - c), sibling).wait_recv()

        lb = _lower_bound(lbl_ref[...])
        tri = _tri(True)
        causal = _causal()
        g64m = g64_ref[...]
        heads = range(N_HEADS)
        cs = [slice(hd * HEAD, (hd + 1) * HEAD) for hd in heads]
        col = lambda base, hd: slice(base + hd * HEAD, base + (hd + 1) * HEAD)

        def mix_chunk(n):
            sl = pl.ds(n * CHUNK, CHUNK)
            sg = [_sigmoid(p_ref[sl, col(512, hd)]) for hd in heads]
            f = [lb[:, cs[hd]] + (1.0 - lb[:, cs[hd]]) * sg[hd] for hd in heads]
            bc = _exact_left_many(tri, [jnp.log(f[hd]) for hd in heads])
            for hd in heads:
                aux_ref[sl, col(AUX_B, hd)] = bc[hd]
            g = [bc[hd][CHUNK - 1:CHUNK, :] for hd in heads]
            qd = [(p_ref[sl, col(0, hd)] * jnp.exp(bc[hd])).astype(BF16) for hd in heads]
            kk = [1.0 - f[hd] for hd in heads]
            ki = [(kk[hd] * jnp.exp(-bc[hd])).astype(BF16) for hd in heads]
            ke = [(kk[hd] * jnp.exp(g[hd] - bc[hd])).astype(BF16) for hd in heads]
            vb = [p_ref[sl, col(1024, hd)].astype(BF16) for hd in heads]
            st = [st_ref[hd] for hd in heads]
            st_b = [a.astype(BF16) for a in st]
            for hd in heads:
                sto_ref[n, hd] = st_b[hd]
            scm = [_dot_nt(qd[hd], ki[hd]) for hd in heads]
            inter = [_dot_nt(qd[hd], st_b[hd]) for hd in heads]
            upd = [_dot_tn(vb[hd], ke[hd]) for hd in heads]
            intra = [_dot(jnp.where(causal, scm[hd], 0.0).astype(BF16), vb[hd]) for hd in heads]
            for hd in heads:
                st_ref[hd] = st[hd] * jnp.exp(g[hd]) + upd[hd]
                o = intra[hd] + inter[hd]
                aux_ref[sl, col(AUX_O, hd)] = o
                ra = lax.rsqrt(jnp.mean(o * o, axis=-1, keepdims=True) + EPS)
                za = p_ref[sl, col(1536, hd)]
                stage[sl, cs[hd]] = (o * ra * ga_ref[:, cs[hd]] * (za * _sigmoid(za))).astype(BF16)
            yb = []
            for hd in heads:
                cu = p_ref[sl, col(3072, hd)] * p_ref[sl, col(2048, hd)]
                tail = tail_ref[:, cs[hd]]
                cv = (cw_ref[0:1, cs[hd]] * _shift_down(cu, 2, tail) + cw_ref[1:2, cs[hd]] * _shift_down(cu, 1, tail)
                      + cw_ref[2:3, cs[hd]] * cu)
                tail_ref[:, cs[hd]] = cu[CHUNK - 8:, :]
                aux_ref[sl, col(AUX_CV, hd)] = cv
                yb.append(p_ref[sl, col(2560, hd)] * cv)
            ms = _group_mean_many([y * y for y in yb], g64m)
            for hd in heads:
                rb = lax.rsqrt(ms[hd] + EPS)
                zb = p_ref[sl, col(3584, hd)]
                stage[sl, col(512, hd)] = (yb[hd] * rb * gcn_ref[:, cs[hd]] * (zb * _sigmoid(zb))).astype(BF16)

        def step(mix, project, entry=False):
            if project:
                mixed_b = ring[pl.ds(pl.multiple_of((i - LAG) * TB, TB), TB), :]
                y = _dot(mixed_b, wog_v[...])
            if mix:
                mix_chunk(0)
            if project:
                x2 = x_ref[...] + y
                r2 = lax.rsqrt(jnp.mean(x2 * x2, axis=-1, keepdims=True) + EPS)
                n2 = x2 * r2
                gfv = gf_ref[...]
                err = n2 * gfv - t_ref[...]
                loss = 0.5 * jnp.sum(jnp.mean(err * err, axis=-1, keepdims=True), axis=0, keepdims=True)
                dy = err * (1.0 / D_MODEL)
                part_ref[1:2, :] += jnp.sum(dy * n2, axis=0, keepdims=True)
                part_ref[7:8, :] += jnp.broadcast_to(loss, (1, D_MODEL))
                dn = dy * gfv
                dx2 = r2 * (dn - n2 * jnp.mean(dn * n2, axis=-1, keepdims=True))
                dx2_ref[...] = dx2
                dx2_b = dx2.astype(BF16)
            if mix:
                mix_chunk(1)
            if project:
                dm_ref[...] = _dot_nt(dx2_b, wog_v[...])
            if mix:
                mix_chunk(2)
            if project:
                acc_ref[...] += _dot_tn(mixed_b, dx2_b)
            if mix:
                mix_chunk(3)
                ring[pl.ds(pl.multiple_of(i * TB, TB), TB), :] = stage[...]
            if entry:
                pl.semaphore_wait(pltpu.get_barrier_semaphore(), 4)
                for cp in wo_direct:
                    cp.start()

        @pl.when(i == 0)
        def _():
            step(True, False, entry=True)

        @pl.when((i > 0) & (i < LAG))
        def _():
            step(True, False)

        @pl.when((i >= LAG) & (i < nblk))
        def _():
            step(True, True)

        @pl.when(i >= nblk)
        def _():
            step(False, True)

        @pl.when(i == n_steps - 1)
        def _():
            gwo_ref[...] = acc_ref[...].astype(BF16)
            for cp in wo_direct + wo_passed:
                cp.wait_send()

    assert NCB == 4
    row = lambda w: pl.BlockSpec((1, w), lambda i: (0, 0))
    mix_blk = lambda i: jnp.minimum(i, nblk - 1)
    out_blk = lambda i: jnp.clip(i - LAG, 0, nblk - 1)
    tok = lambda: pl.BlockSpec((TB, D_MODEL), lambda i: (out_blk(i), 0))
    return pl.pallas_call(
        body, name="mix_out", grid=(n_steps,),
        out_shape=(jax.ShapeDtypeStruct((SEQ, AUX_COLS), F32),
                   jax.ShapeDtypeStruct((N_CHUNKS, N_HEADS, HEAD, HEAD), BF16),
                   jax.ShapeDtypeStruct((SEQ, D_MODEL), F32),
                   jax.ShapeDtypeStruct((SEQ, D_MODEL), F32),
                   jax.ShapeDtypeStruct((D_MODEL, D_MODEL), BF16),
                   jax.ShapeDtypeStruct((8, D_MODEL), F32)),
        in_specs=[pl.BlockSpec((TB, 4096), lambda i: (jnp.minimum(i, nblk - 1), 0)),
                  pl.BlockSpec((2, D_HGRN), lambda i: (0, 0)),
                  pl.BlockSpec((8, D_CONV), lambda i: (0, 0)),
                  row(D_HGRN), row(D_CONV),
                  pl.BlockSpec((HEAD, HEAD), lambda i: (0, 0)),
                  pl.BlockSpec((1, WO_ROWS, D_MODEL), lambda i: (0, 0, 0)),
                  tok(), row(D_MODEL), tok()],
        out_specs=(pl.BlockSpec((TB, AUX_COLS), lambda i: (mix_blk(i), 0)),
                   pl.BlockSpec((NCB, N_HEADS, HEAD, HEAD), lambda i: (mix_blk(i), 0, 0, 0)),
                   tok(), tok(),
                   pl.BlockSpec((D_MODEL, D_MODEL), lambda i: (0, 0)),
                   pl.BlockSpec((8, D_MODEL), lambda i: (0, 0))),
        scratch_shapes=[pltpu.VMEM((N_HEADS, HEAD, HEAD), F32), pltpu.VMEM((8, D_CONV), F32),
                        pltpu.VMEM((D_MODEL, D_MODEL), BF16), pltpu.VMEM((TB, D_MODEL), BF16),
                        pltpu.VMEM((SEQ, D_MODEL), BF16), pltpu.VMEM((D_MODEL, D_MODEL), F32),
                        pltpu.SemaphoreType.DMA((6,)), pltpu.SemaphoreType.DMA((6,))],
        compiler_params=pltpu.CompilerParams(dimension_semantics=("arbitrary",), vmem_limit_bytes=VMEM_LIMIT,
                                             collective_id=COLLECTIVE_MIX_OUT),
    )(proj, lb_logits, cw, ga, gcn, g64, w_out, x2d, gf, tgt)


def _mix_bwd(proj, aux, states, dmixed, lb_logits, cw, ga, gcn, g64):
    nblk = SEQ // TB

    def body(p_ref, aux_ref, st_ref, dm_ref, lbl_ref, cw_ref, ga_ref, gcn_ref, g64_ref,
             dp_ref, part_ref, dst_ref, head_ref, dlb_ref):
        i = pl.program_id(0)

        @pl.when(i == 0)
        def _():
            dst_ref[...] = jnp.zeros_like(dst_ref)
            head_ref[...] = jnp.zeros_like(head_ref)
            part_ref[...] = jnp.zeros_like(part_ref)
            dlb_ref[...] = jnp.zeros_like(dlb_ref)

        lb = _lower_bound(lbl_ref[...])
        triu = _tri(False)
        causal = _causal()
        g64m = g64_ref[...]
        rowsum = lambda a: jnp.sum(a, axis=0, keepdims=True)
        heads = range(N_HEADS)
        cs = [slice(hd * HEAD, (hd + 1) * HEAD) for hd in heads]
        col = lambda base, hd: slice(base + hd * HEAD, base + (hd + 1) * HEAD)
        for n in reversed(range(NCB)):
            sl = pl.ds(n * CHUNK, CHUNK)
            cvv = [aux_ref[sl, col(AUX_CV, hd)] for hd in heads]
            gb = [p_ref[sl, col(2560, hd)] for hd in heads]
            yb = [gb[hd] * cvv[hd] for hd in heads]
            ms = _group_mean_many([y * y for y in yb], g64m)
            rb, nb, dnb = [], [], []
            for hd in heads:
                rb.append(lax.rsqrt(ms[hd] + EPS))
                nb.append(yb[hd] * rb[hd])
                zb = p_ref[sl, col(3584, hd)]
                sgb = _sigmoid(zb)
                dmb = dm_ref[sl, col(512, hd)]
                silu = zb * sgb
                dgate = dmb * gcn_ref[:, cs[hd]]
                part_ref[2:3, col(512, hd)] += rowsum(dmb * nb[hd] * silu)
                dp_ref[sl, col(3584, hd)] = (dgate * nb[hd] * (sgb + silu * (1.0 - sgb))).astype(BF16)
                dnb.append(dgate * silu)
            mdn = _group_mean_many([dnb[hd] * nb[hd] for hd in heads], g64m)
            for hd in heads:
                dyb = rb[hd] * (dnb[hd] - nb[hd] * mdn[hd])
                dp_ref[sl, col(2560, hd)] = (dyb * cvv[hd]).astype(BF16)
                dcv = dyb * gb[hd]
                head = head_ref[:, cs[hd]]
                dcv1 = _shift_up(dcv, 1, head)
                dcv2 = _shift_up(dcv, 2, head)
                head_ref[:, cs[hd]] = dcv[0:8, :]
                u = p_ref[sl, col(2048, hd)]
                gc = p_ref[sl, col(3072, hd)]
                cu = gc * u
                part_ref[4:5, cs[hd]] += rowsum(dcv2 * cu)
                part_ref[5:6, cs[hd]] += rowsum(dcv1 * cu)
                part_ref[6:7, cs[hd]] += rowsum(dcv * cu)
                dcu = cw_ref[2:3, cs[hd]] * dcv + cw_ref[1:2, cs[hd]] * dcv1 + cw_ref[0:1, cs[hd]] * dcv2
                dp_ref[sl, col(3072, hd)] = (dcu * u).astype(BF16)
                dp_ref[sl, col(2048, hd)] = (dcu * gc).astype(BF16)
            do_b = []
            for hd in heads:
                ov = aux_ref[sl, col(AUX_O, hd)]
                ra = lax.rsqrt(jnp.mean(ov * ov, axis=-1, keepdims=True) + EPS)
                na = ov * ra
                za = p_ref[sl, col(1536, hd)]
                sga = _sigmoid(za)
                dma = dm_ref[sl, cs[hd]]
                silu = za * sga
                dgate = dma * ga_ref[:, cs[hd]]
                part_ref[2:3, cs[hd]] += rowsum(dma * na * silu)
                dp_ref[sl, col(1536, hd)] = (dgate * na * (sga + silu * (1.0 - sga))).astype(BF16)
                dna = dgate * silu
                do_b.append((ra * (dna - na * jnp.mean(dna * na, axis=-1, keepdims=True))).astype(BF16))
            s = [_sigmoid(p_ref[sl, col(512, hd)]) for hd in heads]
            f = [lb[:, cs[hd]] + (1.0 - lb[:, cs[hd]]) * s[hd] for hd in heads]
            bc = [aux_ref[sl, col(AUX_B, hd)] for hd in heads]
            g = [bc[hd][CHUNK - 1:CHUNK, :] for hd in heads]
            eb = [jnp.exp(bc[hd]) for hd in heads]
            enb = [jnp.exp(-bc[hd]) for hd in heads]
            eg = [jnp.exp(g[hd] - bc[hd]) for hd in heads]
            dec = [jnp.exp(g[hd]) for hd in heads]
            qd = [p_ref[sl, cs[hd]] * eb[hd] for hd in heads]
            kk = [1.0 - f[hd] for hd in heads]
            ki = [kk[hd] * enb[hd] for hd in heads]
            ke = [kk[hd] * eg[hd] for hd in heads]
            qd_b = [a.astype(BF16) for a in qd]
            ki_b = [a.astype(BF16) for a in ki]
            ke_b = [a.astype(BF16) for a in ke]
            vb = [p_ref[sl, col(1024, hd)].astype(BF16) for hd in heads]
            st_b = [st_ref[n, hd] for hd in heads]
            dst = [dst_ref[hd] for hd in heads]
            dst_b = [a.astype(BF16) for a in dst]
            scm = [_dot_nt(qd_b[hd], ki_b[hd]) for hd in heads]
            amm = [_dot_nt(do_b[hd], vb[hd]) for hd in heads]
            dqd2 = [_dot(do_b[hd], st_b[hd]) for hd in heads]
            dke = [_dot(vb[hd], dst_b[hd]) for hd in heads]
            dv2 = [_dot_nt(ke_b[hd], dst_b[hd]) for hd in heads]
            dsu = [_dot_tn(do_b[hd], qd_b[hd]) for hd in heads]
            sc = [jnp.where(causal, scm[hd], 0.0).astype(BF16) for hd in heads]
            am = [jnp.where(causal, amm[hd], 0.0).astype(BF16) for hd in heads]
            dqd1 = [_dot(am[hd], ki_b[hd]) for hd in heads]
            dki = [_dot_tn(am[hd], qd_b[hd]) for hd in heads]
            dv1 = [_dot_tn(sc[hd], do_b[hd]) for hd in heads]
            db, dgv, dkk = [], [], []
            for hd in heads:
                dqd = dqd1[hd] + dqd2[hd]
                ddec = rowsum(dst[hd] * st_b[hd].astype(F32))
                dst_ref[hd] = dst[hd] * dec[hd] + dsu[hd]
                dp_ref[sl, cs[hd]] = (dqd * eb[hd]).astype(BF16)
                dp_ref[sl, col(1024, hd)] = (dv1[hd] + dv2[hd]).astype(BF16)
                dke_eg = dke[hd] * eg[hd]
                dkk.append(dki[hd] * enb[hd] + dke_eg)
                db.append(dqd * qd[hd] - kk[hd] * dkk[hd])
                dgv.append(rowsum(kk[hd] * dke_eg) + ddec * dec[hd])
            rc = _exact_left_many(triu, db, 2)
            for hd in heads:
                df = (rc[hd] + dgv[hd]) / f[hd] - dkk[hd]
                one_s = 1.0 - s[hd]
                dlb_ref[:, cs[hd]] += rowsum(df * one_s)
                dp_ref[sl, col(512, hd)] = (df * (1.0 - lb[:, cs[hd]]) * s[hd] * one_s).astype(BF16)

        @pl.when(i == nblk - 1)
        def _():
            row = dlb_ref[...] * lb * (1.0 - lb)
            part_ref[3:4, 0:D_HGRN] = row
            part_ref[3:4, D_HGRN:] = -row

    rev = lambda w: pl.BlockSpec((TB, w), lambda i: (nblk - 1 - i, 0))
    row = lambda w: pl.BlockSpec((1, w), lambda i: (0, 0))
    return pl.pallas_call(
        body, name="mix_bwd", grid=(nblk,),
        out_shape=(jax.ShapeDtypeStruct((SEQ, 4096), BF16),
                   jax.ShapeDtypeStruct((8, D_MODEL), F32)),
        in_specs=[rev(4096), rev(AUX_COLS),
                  pl.BlockSpec((NCB, N_HEADS, HEAD, HEAD), lambda i: (nblk - 1 - i, 0, 0, 0)),
                  rev(D_MODEL),
                  pl.BlockSpec((2, D_HGRN), lambda i: (0, 0)),
                  pl.BlockSpec((8, D_CONV), lambda i: (0, 0)),
                  row(D_HGRN), row(D_CONV),
                  pl.BlockSpec((HEAD, HEAD), lambda i: (0, 0))],
        out_specs=(rev(4096), pl.BlockSpec((8, D_MODEL), lambda i: (0, 0))),
        scratch_shapes=[pltpu.VMEM((N_HEADS, HEAD, HEAD), F32), pltpu.VMEM((8, D_CONV), F32),
                        pltpu.VMEM((1, D_HGRN), F32)],
        compiler_params=pltpu.CompilerParams(dimension_semantics=("arbitrary",), vmem_limit_bytes=VMEM_LIMIT),
    )(proj, aux, states, dmixed, lb_logits, cw, ga, gcn, g64)


TT = 1024
TX = 512
(SEM_D2D, SEM_D2D_O, SEM_ICI, SEM_ICI_O, SEM_FIN, SEM_FIN_O, SEM_SMALL, SEM_VIA, SEM_NORM, N_SEM_TAIL) = (
    0, 4, 5, 8, 11, 12, 12, 20, 22, 30)


def _bwd_tail(kidx, h, dproj, wg, gwo, x2d, dx2, g1, small_a, small_b):
    hw = D_MODEL // 2
    ho = WO_ROWS // 2
    nt = SEQ // TT
    norm_step = 2 * N_SHARD
    n_steps = norm_step + SEQ // TX // nt

    def body(k_ref, h_ref, dp_ref, w_ref, gwo_ref, x_ref, dx2_ref, g_ref, sm_ref, smb_ref,
             gx_ref, gw_out, gwo_out, osm_ref,
             acc, dh, sendbuf, keep, sibrcv, rcv, merge, sib_o, p_o, rcv_o, res_o, sm_buf, dng_buf, dng,
             send_sems, recv_sems, out_sems):
        s, t = pl.program_id(0), pl.program_id(1)
        x, y, c = lax.axis_index("x"), lax.axis_index("y"), lax.axis_index("c")
        k = 2 * x + y
        me = 4 * x + 2 * y + c
        sibling = (x, y, 1 - c)
        chips = [(1 - x, 1 - y), (1 - x, y), (x, 1 - y)]
        kjs = [2 * cx + cy for cx, cy in chips]
        mine = pl.ds(pl.multiple_of(c * hw, hw), hw)
        other = pl.ds(pl.multiple_of((1 - c) * hw, hw), hw)
        mine_o = pl.ds(pl.multiple_of(c * ho, ho), ho)
        other_o = pl.ds(pl.multiple_of((1 - c) * ho, ho), ho)

        def copy(sem, src, dst, to):
            return pltpu.make_async_remote_copy(
                src_ref=src, dst_ref=dst, send_sem=send_sems.at[sem], recv_sem=recv_sems.at[sem],
                device_id=to, device_id_type=MESH)

        def at_step(sv, tv):
            return pl.when((s == sv) & (t == tv))

        def at_norm_block(b):
            return at_step(norm_step + b // nt, b % nt)

        d2d = [copy(SEM_D2D + sv, sendbuf.at[sv], sibrcv.at[sv], sibling) for sv in range(N_SHARD)]
        d2d_o = copy(SEM_D2D_O, gwo_ref.at[:, other_o, :], sib_o, sibling)
        ici = {sv: copy(SEM_ICI + sv, keep.at[sv], rcv.at[sv - 1], (*chips[sv], c)) for sv in (1, 2)}
        qh = hw // 2
        via = [copy(SEM_VIA, keep.at[0, 0:qh, :], merge.at[1], (*chips[1], c)),
               copy(SEM_VIA + 1, keep.at[0, qh:hw, :], merge.at[0], (*chips[2], c))]
        merged_rows = [slice(qh, hw), slice(0, qh)]
        ici_o = [copy(SEM_ICI_O + sv, p_o.at[kjs[sv]], rcv_o.at[sv], (*chips[sv], c)) for sv in range(3)]
        fin = copy(SEM_FIN, acc.at[mine, :], gw_out.at[mine, :], sibling)
        fin_o = copy(SEM_FIN_O, res_o.at[mine_o, :], res_o.at[mine_o, :], sibling)
        peers = [(x ^ (m >> 2), y ^ ((m >> 1) & 1), c ^ (m & 1)) for m in range(1, N_DEV)]
        smalls = [copy(SEM_SMALL + 1 + j, sm_buf.at[me], sm_buf.at[me], to) for j, to in enumerate(peers)]
        dngs = [copy(SEM_NORM + 1 + j, dng_buf.at[me], dng_buf.at[me], to) for j, to in enumerate(peers)]
        store_w = pltpu.make_async_copy(acc.at[mine, :], gw_out.at[mine, :], out_sems.at[0])
        store_o = pltpu.make_async_copy(res_o, gwo_out, out_sems.at[1])

        @at_step(0, 0)
        def _():
            barrier = pltpu.get_barrier_semaphore()
            for to in peers:
                pl.semaphore_signal(barrier, inc=1, device_id=to, device_id_type=MESH)
            sm_buf[me] = sm_ref[...] + smb_ref[...]
            pl.semaphore_wait(barrier, N_DEV - 1)
            d2d_o.start()
            for cp in smalls:
                cp.start()

        @at_step(0, 1)
        def _():
            d2d_o.wait_recv()
            for j in range(N_SHARD):
                p_o[j] = (gwo_ref[j, mine_o, :].astype(F32) + sib_o[j].astype(F32)).astype(BF16)
            res_o[mine_o, :] = gwo_ref[k, mine_o, :].astype(F32) + sib_o[k].astype(F32)
            for cp in ici_o:
                cp.start()

        rows = pl.ds(pl.multiple_of(t * TT, TT), TT)

        @pl.when((s < N_SHARD) & (t == 0))
        def _():
            acc[...] = _dot_tn(h_ref[...], dp_ref[...])

        @pl.when((s < N_SHARD) & (t > 0))
        def _():
            acc[...] += _dot_tn(h_ref[...], dp_ref[...])

        for sv in range(N_SHARD):
            @at_step(sv, nt - 1)
            def _(sv=sv):
                sendbuf[sv] = acc[other, :].astype(BF16)
                if sv < 3:
                    keep[sv] = acc[mine, :].astype(BF16)
                d2d[sv].start()

        @at_step(1, 0)
        def _():
            d2d[0].wait_recv()
            keep[0] = (keep[0].astype(F32) + sibrcv[0].astype(F32)).astype(BF16)
            for cp in via:
                cp.start()

        for sv in (1, 2):
            @at_step(sv + 2, 0)
            def _(sv=sv):
                d2d[sv].wait_recv()
                keep[sv] = (keep[sv].astype(F32) + sibrcv[sv].astype(F32)).astype(BF16)
                via[2 - sv].wait_recv()
                rows_m = merged_rows[sv - 1]
                keep[sv, rows_m, :] = (keep[sv, rows_m, :].astype(F32) + merge[sv - 1].astype(F32)).astype(BF16)
                ici[sv].start()

        @pl.when(s == N_SHARD)
        def _():
            dh[rows, :] = _dot_nt(dp_ref[...], w_ref[0])

        @pl.when((s > N_SHARD) & (s < norm_step))
        def _():
            dh[rows, :] += _dot_nt(dp_ref[...], w_ref[0])

        @at_norm_block(0)
        def _():
            d2d[3].wait_recv()
            acc[mine, :] += sibrcv[3].astype(F32)

        @at_norm_block(1)
        def _():
            tot = res_o[mine_o, :]
            for sv in range(3):
                ici_o[sv].wait_recv()
                tot = tot + rcv_o[sv].astype(F32)
            res_o[mine_o, :] = tot
            fin_o.start()

        @at_norm_block(2)
        def _():
            ici[1].wait_recv()
            acc[mine, :] += rcv[0].astype(F32)

        @at_norm_block(SEQ // TX - 2)
        def _():
            ici[2].wait_recv()
            acc[mine, :] += rcv[1].astype(F32)
            fin.start()
            store_w.start()
            fin_o.wait_recv()
            store_o.start()

        @at_norm_block(0)
        def _():
            dng[...] = jnp.zeros_like(dng)

        @pl.when(s >= norm_step)
        def _():
            blk = (s - norm_step) * nt + t
            dhv = dh[pl.ds(pl.multiple_of(blk * TX, TX), TX), :]
            xv = x_ref[...]
            r = lax.rsqrt(jnp.mean(xv * xv, axis=-1, keepdims=True) + EPS)
            xn = xv * r
            dng[...] += jnp.sum(dhv * xn, axis=0, keepdims=True)
            dxn = dhv * g_ref[...]
            gx_ref[...] = dx2_ref[...] + r * (dxn - xn * jnp.mean(dxn * xn, axis=-1, keepdims=True))

        @at_step(n_steps - 1, nt - 1)
        def _():
            dng_buf[me] = dng[...]
            for cp in dngs:
                cp.start()
            for m in range(1, N_DEV):
                copy(SEM_SMALL + m, sm_buf.at[0], sm_buf.at[0], sibling).wait_recv()
            tot = sm_buf[0]
            for d in range(1, N_DEV):
                tot = tot + sm_buf[d]
            osm_ref[...] = tot
            for m in range(1, N_DEV):
                copy(SEM_NORM + m, dng_buf.at[0], dng_buf.at[0], sibling).wait_recv()
            tot = dng_buf[0]
            for d in range(1, N_DEV):
                tot = tot + dng_buf[d]
            osm_ref[0:1, :] = tot
            fin.wait_recv()
            for cp in d2d + [d2d_o] + via + list(ici.values()) + ici_o + [fin, fin_o] + smalls + dngs:
                cp.wait_send()
            store_o.wait()
            store_w.wait()

    def shard_of(s, kr):
        order = jnp.where(s < N_SHARD, s, jnp.where(s < norm_step, s - N_SHARD, 3))
        return kr[0] ^ (3 - order)

    def h_map(s, t, kr):
        return (jnp.where(s < N_SHARD, t, nt - 1), 0)

    def dp_map(s, t, kr):
        return (jnp.where(s < norm_step, t, nt - 1), shard_of(s, kr))

    def w_map(s, t, kr):
        return (shard_of(jnp.maximum(s, N_SHARD), kr), 0, 0)

    def blk_map(s, t, kr):
        return (jnp.where(s < norm_step, 0, (s - norm_step) * nt + t), 0)

    hbm = pl.BlockSpec(memory_space=pl.ANY)
    grid_spec = pltpu.PrefetchScalarGridSpec(
        num_scalar_prefetch=1, grid=(n_steps, nt),
        in_specs=[pl.BlockSpec((TT, D_MODEL), h_map),
                  pl.BlockSpec((TT, SHARD_COLS), dp_map),
                  pl.BlockSpec((1, D_MODEL, SHARD_COLS), w_map),
                  pl.BlockSpec((N_SHARD, WO_ROWS, D_MODEL), lambda s, t, kr: (0, 0, 0),
                               pipeline_mode=pl.Buffered(1)),
                  pl.BlockSpec((TX, D_MODEL), blk_map),
                  pl.BlockSpec((TX, D_MODEL), blk_map),
                  pl.BlockSpec((1, D_MODEL), lambda s, t, kr: (0, 0)),
                  pl.BlockSpec((8, D_MODEL), lambda s, t, kr: (0, 0)),
                  pl.BlockSpec((8, D_MODEL), lambda s, t, kr: (0, 0))],
        out_specs=(pl.BlockSpec((TX, D_MODEL), blk_map), hbm, hbm,
                   pl.BlockSpec((8, D_MODEL), lambda s, t, kr: (0, 0))),
        scratch_shapes=[pltpu.VMEM((D_MODEL, SHARD_COLS), F32), pltpu.VMEM((SEQ, D_MODEL), F32),
                        pltpu.VMEM((N_SHARD, hw, SHARD_COLS), BF16), pltpu.VMEM((3, hw, SHARD_COLS), BF16),
                        pltpu.VMEM((N_SHARD, hw, SHARD_COLS), BF16), pltpu.VMEM((2, hw, SHARD_COLS), BF16),
                        pltpu.VMEM((2, hw // 2, SHARD_COLS), BF16),
                        pltpu.VMEM((N_SHARD, ho, D_MODEL), BF16), pltpu.VMEM((N_SHARD, ho, D_MODEL), BF16),
                        pltpu.VMEM((3, ho, D_MODEL), BF16), pltpu.VMEM((WO_ROWS, D_MODEL), F32),
                        pltpu.VMEM((N_DEV, 8, D_MODEL), F32), pltpu.VMEM((N_DEV, 1, D_MODEL), F32),
                        pltpu.VMEM((1, D_MODEL), F32),
                        pltpu.SemaphoreType.DMA((N_SEM_TAIL,)), pltpu.SemaphoreType.DMA((N_SEM_TAIL,)),
                        pltpu.SemaphoreType.DMA((2,))])
    return pl.pallas_call(
        body, name="bwd_tail", grid_spec=grid_spec,
        out_shape=(jax.ShapeDtypeStruct((SEQ, D_MODEL), F32),
                   jax.ShapeDtypeStruct((D_MODEL, SHARD_COLS), F32),
                   jax.ShapeDtypeStruct((WO_ROWS, D_MODEL), F32),
                   jax.ShapeDtypeStruct((8, D_MODEL), F32)),
        compiler_params=pltpu.CompilerParams(dimension_semantics=("arbitrary", "arbitrary"),
                                             vmem_limit_bytes=61 * 1024 * 1024, collective_id=COLLECTIVE_TAIL),
    )(kidx, h, dproj, wg, gwo, x2d, dx2, g1, small_a, small_b)


def _adam_update(w, g, m, v):
    nm = ADAM_B1 * m + (1.0 - ADAM_B1) * g
    nv = ADAM_B2 * v + (1.0 - ADAM_B2) * (g * g)
    m_hat = nm / (1.0 - ADAM_B1 ** ADAM_STEP)
    v_hat = nv / (1.0 - ADAM_B2 ** ADAM_STEP)
    return -ADAM_LR * (m_hat / (jnp.sqrt(v_hat) + ADAM_EPS) + ADAM_WD * w), nm, nv


def _adamw_all(tot, g_w_in, g_w_out, big, small, grad_x):
    n = len(small)
    rows = WO_ROWS
    steps = D_MODEL // rows

    def body(tot_ref, *refs):
        gx_ref, gx_out = refs[2 + 3 * (2 + n)], refs[-1]
        gx_out[...] = gx_ref[...]
        ins, outs = refs[:2 + 3 * (2 + n)], refs[3 + 3 * (2 + n):-1]
        g_refs, wmv = ins[:2], ins[2:]
        loss_ref, quads = outs[0], outs[1:]

        def update(j, g):
            w_ref, m_ref, v_ref = wmv[3 * j:3 * j + 3]
            g_ref, d_ref, nm_ref, nv_ref = quads[4 * j:4 * j + 4]
            g_ref[...] = g
            d_ref[...], nm_ref[...], nv_ref[...] = _adam_update(w_ref[...], g, m_ref[...], v_ref[...])

        update(0, g_refs[0][...])

        @pl.when(pl.program_id(0) == 0)
        def _():
            update(1, g_refs[1][...])
            k = 2 * lax.axis_index("x") + lax.axis_index("y")
            mine = pl.ds(pl.multiple_of(k * HEAD, HEAD), HEAD)
            loss_ref[...] = tot_ref[7:8, 0:1]
            grads = [tot_ref[0:1, :], tot_ref[1:2, :], tot_ref[2:3, 0:D_HGRN], tot_ref[2:3, D_HGRN:],
                     jnp.concatenate([tot_ref[3:4, 0:D_HGRN], tot_ref[3:4, D_HGRN:]], axis=0),
                     jnp.concatenate([tot_ref[4 + tap:5 + tap, mine] for tap in range(3)], axis=1)]
            for j, g in enumerate(grads):
                update(2 + j, g)

    whole = lambda a: pl.BlockSpec(a.shape, lambda i: (0, 0))
    blk = pl.BlockSpec((rows, SHARD_COLS), lambda i: (i, 0))
    arrays = [a for triple in big + small for a in triple]
    in_specs = ([whole(tot), blk, whole(g_w_out)] + [blk] * 3 + [whole(a) for a in arrays[3:]])
    shapes = [big[0][0], big[1][0]] + [w for w, _, _ in small]
    out_shape = (jax.ShapeDtypeStruct((1, 1), F32),) + tuple(
        jax.ShapeDtypeStruct(w.shape, F32) for w in shapes for _ in range(4))
    out_specs = (pl.BlockSpec((1, 1), lambda i: (0, 0)),) + (blk,) * 4 + tuple(
        whole(w) for w in shapes[1:] for _ in range(4))
    gx_blk = pl.BlockSpec((SEQ // steps, D_MODEL), lambda i: (i, 0))
    outs = pl.pallas_call(
        body, name="adamw_all", grid=(steps,),
        out_shape=out_shape + (jax.ShapeDtypeStruct(grad_x.shape, F32),),
        in_specs=in_specs + [gx_blk], out_specs=out_specs + (gx_blk,),
        compiler_params=pltpu.CompilerParams(dimension_semantics=("arbitrary",), vmem_limit_bytes=VMEM_LIMIT),
    )(tot, g_w_in, g_w_out, *arrays, grad_x)
    return [outs[0]] + [outs[1 + 4 * j:5 + 4 * j] for j in range(2 + n)] + [outs[-1]]


def _local_step(x2d, tgt, proj, lb_logits, cw, ga, gcn, w_out, gf):
    g64 = _group_matrix(HEAD, CONV_GROUP)
    aux, states, dx2, dmixed, gwo, part_out = _mix_out(proj, lb_logits, cw, ga, gcn, g64, w_out, x2d, gf, tgt)
    dproj, part_mix = _mix_bwd(proj, aux, states, dmixed, lb_logits, cw, ga, gcn, g64)
    return dproj, dx2, gwo.reshape(N_SHARD, WO_ROWS, D_MODEL), part_out, part_mix


def kernel(x, norm_gain, w_in, lb_logits, conv_w, hgrn_norm_gain, conv_norm_gain, w_out, final_norm_gain, loss_target, m_norm_gain, m_w_in, m_lb_logits, m_conv_w, m_hgrn_norm_gain, m_conv_norm_gain, m_w_out, m_final_norm_gain, v_norm_gain, v_w_in, v_lb_logits, v_conv_w, v_hgrn_norm_gain, v_conv_norm_gain, v_w_out, v_final_norm_gain):
    k = 2 * lax.axis_index("x") + lax.axis_index("y")
    kidx = jnp.reshape(k, (1,)).astype(jnp.int32)
    row = lambda a: a.reshape(1, D_MODEL)
    taps = lambda a: a.reshape(1, 3 * HEAD)
    h, proj, wg, cw = _gather_proj(kidx, x[0], norm_gain, w_in, taps(conv_w))
    dproj, dx2, gwo, part_out, part_mix = _local_step(
        x[0], loss_target[0], proj, lb_logits, cw, hgrn_norm_gain, conv_norm_gain, w_out, row(final_norm_gain))
    rgrad_x, rg_w_in, rg_w_out, tot = _bwd_tail(kidx, h, dproj, wg, gwo, x[0], dx2, norm_gain, part_out, part_mix)

    (loss, (g_w_in, d_w_in, nm_w_in, nv_w_in), (g_w_out, d_w_out, nm_w_out, nv_w_out),
     (g_norm_gain, d_ng, nm_ng, nv_ng), (g_final, d_fg, nm_fg, nv_fg), (g_hgrn, d_hg, nm_hg, nv_hg),
     (g_convn, d_cg, nm_cg, nv_cg), (g_lb, d_lb, nm_lb, nv_lb), (g_conv_w, d_cw, nm_cw, nv_cw),
     grad_x) = _adamw_all(
        tot, rg_w_in, rg_w_out,
        [(w_in[0], m_w_in[0], v_w_in[0]), (w_out[0], m_w_out[0], v_w_out[0])],
        [(norm_gain, m_norm_gain, v_norm_gain),
         (row(final_norm_gain), row(m_final_norm_gain), row(v_final_norm_gain)),
         (hgrn_norm_gain, m_hgrn_norm_gain, v_hgrn_norm_gain),
         (conv_norm_gain, m_conv_norm_gain, v_conv_norm_gain),
         (lb_logits, m_lb_logits, v_lb_logits),
         (taps(conv_w), taps(m_conv_w), taps(v_conv_w))],
        rgrad_x)
    flat = lambda a: a.reshape(D_MODEL)
    untap = lambda a: a.reshape(1, 3, HEAD)
    return (loss.reshape(()), grad_x[None],
            g_norm_gain, g_w_in[None], g_lb, untap(g_conv_w), g_hgrn, g_convn, g_w_out[None], flat(g_final),
            d_ng, d_w_in[None], d_lb, untap(d_cw), d_hg, d_cg, d_w_out[None], flat(d_fg),
            nm_ng, nm_w_in[None], nm_lb, untap(nm_cw), nm_hg, nm_cg, nm_w_out[None], flat(nm_fg),
            nv_ng, nv_w_in[None], nv_lb, untap(nv_cw), nv_hg, nv_cg, nv_w_out[None], flat(nv_fg))
```

```python
import jax
import jax.numpy as jnp
import numpy as np
from jax import lax
from jax.experimental import pallas as pl
from jax.experimental.pallas import tpu as pltpu

F32 = jnp.float32
BF16 = jnp.bfloat16
MESH = pl.DeviceIdType.MESH

SEQ = 2048
D_MODEL = 1024
D_HGRN = 512
D_CONV = 512
HEAD = 128
N_HEADS = 4
CHUNK = 64
CONV_GROUP = 64
N_SHARD = 4
SHARD_COLS = 1024
WO_ROWS = 256
EPS = 1e-6
TB = 256
NCB = TB // CHUNK
N_CHUNKS = SEQ // CHUNK
N_DEV = 8
COLLECTIVE_GATHER, COLLECTIVE_MIX_OUT, COLLECTIVE_TAIL = 1, 0, 2
AUX_O, AUX_CV, AUX_B, AUX_COLS = 0, 512, 1024, 1536

ADAM_LR = 0.001
ADAM_B1 = 0.9
ADAM_B2 = 0.999
ADAM_EPS = 1e-08
ADAM_WD = 0.01
ADAM_STEP = 10

VMEM_LIMIT = 56 * 1024 * 1024


def _dot(a, b):
    return jnp.dot(a, b, preferred_element_type=F32)


def _dot_nt(a, b):
    return lax.dot_general(a, b, (((1,), (1,)), ((), ())), preferred_element_type=F32)


def _dot_tn(a, b):
    return lax.dot_general(a, b, (((0,), (0,)), ((), ())), preferred_element_type=F32)


def _split_bf16(x, n):
    parts = []
    r = x
    for _ in range(n):
        p = r.astype(BF16)
        parts.append(p)
        r = r - p.astype(F32)
    return parts


def _exact_left(m, x, n=3):
    acc = None
    for p in _split_bf16(x, n):
        t = _dot(m, p)
        acc = t if acc is None else acc + t
    return acc


def _exact_left_many(m, xs, n=3):
    parts = [_split_bf16(x, n) for x in xs]
    accs = [None] * len(xs)
    for i in range(n):
        for j in range(len(xs)):
            t = _dot(m, parts[j][i])
            accs[j] = t if accs[j] is None else accs[j] + t
    return accs


def _group_mean_many(xs, gmat, n=2):
    parts = [_split_bf16(x, n) for x in xs]
    accs = [None] * len(xs)
    for i in range(n):
        for j in range(len(xs)):
            t = _dot(parts[j][i], gmat)
            accs[j] = t if accs[j] is None else accs[j] + t
    return accs


def _group_mean(x, gmat, n=2):
    w = gmat.shape[0]
    outs = []
    for c0 in range(0, x.shape[1], w):
        acc = None
        for p in _split_bf16(x[:, c0:c0 + w], n):
            t = _dot(p, gmat)
            acc = t if acc is None else acc + t
        outs.append(acc)
    return jnp.concatenate(outs, axis=1)


def _sigmoid(x):
    return 1.0 / (1.0 + jnp.exp(-x))


def _lower_bound(lbl):
    l0 = lbl[0:1, :]
    l1 = lbl[1:2, :]
    m = jnp.maximum(l0, l1)
    e0 = jnp.exp(l0 - m)
    e1 = jnp.exp(l1 - m)
    return e0 / (e0 + e1)


def _tri(lower):
    r = lax.broadcasted_iota(jnp.int32, (CHUNK, CHUNK), 0)
    c = lax.broadcasted_iota(jnp.int32, (CHUNK, CHUNK), 1)
    return jnp.where((c <= r) if lower else (c >= r), 1.0, 0.0).astype(BF16)


def _causal():
    r = lax.broadcasted_iota(jnp.int32, (CHUNK, CHUNK), 0)
    c = lax.broadcasted_iota(jnp.int32, (CHUNK, CHUNK), 1)
    return c <= r


def _shift_down(x, sh, prev_tail):
    r = pltpu.roll(x, sh, 0)
    pt = pltpu.roll(prev_tail, sh, 0)
    rows = lax.broadcasted_iota(jnp.int32, prev_tail.shape, 0)
    top = jnp.where(rows < sh, pt, r[0:8])
    return jnp.concatenate([top, r[8:]], axis=0)


def _shift_up(x, sh, next_head):
    n = x.shape[0]
    r = pltpu.roll(x, n - sh, 0)
    nh = pltpu.roll(next_head, 8 - sh, 0)
    rows = lax.broadcasted_iota(jnp.int32, next_head.shape, 0)
    bot = jnp.where(rows >= 8 - sh, nh, r[n - 8:])
    return jnp.concatenate([r[:n - 8], bot], axis=0)


def _group_matrix(width, group):
    r = np.arange(width)[:, None] // group
    c = np.arange(width)[None, :] // group
    return jnp.asarray(np.where(r == c, 1.0 / group, 0.0), dtype=BF16)


TG = 1024
SEM_W, SEM_CW, SEM_W_FWD, N_SEM = 0, 4, 7, 11


def _gather_proj(kidx, x2d, g1, w_in, conv_w):
    half_w = D_MODEL // 2
    half_c = SHARD_COLS // 2
    nt = SEQ // TG
    n_steps = 2 * N_SHARD

    def body(k_ref, x_ref, g_ref, w_ref, cw_ref, h_out, p_ref, wg_out, cwg_out,
             wg_v, cwg_v, h_ref, send_sems, recv_sems, out_sems):
        s, t = pl.program_id(0), pl.program_id(1)
        x, y, c = lax.axis_index("x"), lax.axis_index("y"), lax.axis_index("c")
        k = 2 * x + y
        sibling = (x, y, 1 - c)
        chips = [(1 - x, y), (x, 1 - y), (1 - x, 1 - y)]
        kjs = [2 * cx + cy for cx, cy in chips]
        diag = (*chips[2], c)

        def w_half(kk, cc):
            return wg_v.at[kk, pl.ds(cc * half_w, half_w), :]

        def w_quarter(kk, cc, piece):
            return wg_v.at[kk, pl.ds(cc * half_w, half_w), piece * half_c:(piece + 1) * half_c]

        def cw_of(kk):
            return cwg_v.at[:, pl.ds(pl.multiple_of(kk * HEAD, HEAD), HEAD)]

        def copy(sem, ref, to):
            return pltpu.make_async_remote_copy(
                src_ref=ref, dst_ref=ref, send_sem=send_sems.at[sem], recv_sem=recv_sems.at[sem],
                device_id=to, device_id_type=MESH)

        def at_step(sv, tv):
            return pl.when((s == sv) & (t == tv))

        w_direct = ([copy(SEM_W + j, w_half(k, c), (*chips[j], c)) for j in range(2)]
                    + [copy(SEM_W + 2 + p, w_quarter(k, c, p), diag) for p in range(2)])
        cw_direct = [copy(SEM_CW + j, cw_of(k), (*chip, c)) for j, chip in enumerate(chips)]
        w_passed = ([copy(SEM_W_FWD + j, w_half(kjs[j], c), sibling) for j in range(2)]
                    + [copy(SEM_W_FWD + 2 + p, w_quarter(kjs[2], c, p), sibling) for p in range(2)])
        stores = ([pltpu.make_async_copy(wg_v.at[kk], wg_out.at[kk], out_sems.at[i])
                   for i, kk in enumerate([k] + kjs)]
                  + [pltpu.make_async_copy(cwg_v, cwg_out, out_sems.at[4]),
                     pltpu.make_async_copy(h_ref, h_out, out_sems.at[5])])

        @at_step(0, 0)
        def _():
            barrier = pltpu.get_barrier_semaphore()
            for peer in [sibling] + [(*chip, c) for chip in chips]:
                pl.semaphore_signal(barrier, inc=1, device_id=peer, device_id_type=MESH)
            wg_v[k] = w_ref[0].astype(BF16)
            mine = pl.ds(pl.multiple_of(k * HEAD, HEAD), HEAD)
            cwg_v[:, mine] = jnp.zeros((8, HEAD), F32)
            for tap in range(3):
                cwg_v[tap:tap + 1, mine] = cw_ref[:, tap * HEAD:(tap + 1) * HEAD]
            pl.semaphore_wait(barrier, 4)
            for cp in w_direct + cw_direct:
                cp.start()
            stores[0].start()

        @at_step(1, 0)
        def _():
            stores[5].start()

        @at_step(2, 0)
        def _():
            for j in range(2):
                copy(SEM_W + j, w_half(kjs[j], c), sibling).wait_recv()
                w_passed[j].start()
            copy(SEM_W_FWD, w_half(kjs[0], 1 - c), sibling).wait_recv()
            stores[1].start()

        @at_step(4, 0)
        def _():
            copy(SEM_W_FWD + 1, w_half(kjs[1], 1 - c), sibling).wait_recv()
            stores[2].start()

        for p in range(2):
            @at_step(6 + p, 0)
            def _(p=p):
                copy(SEM_W + 2 + p, w_quarter(kjs[2], c, p), sibling).wait_recv()
                w_passed[2 + p].start()
                copy(SEM_W_FWD + 2 + p, w_quarter(kjs[2], 1 - c, p), sibling).wait_recv()
                if p == 1:
                    stores[3].start()
                    for j in range(3):
                        copy(SEM_CW + j, cw_of(kjs[j]), sibling).wait_recv()
                    stores[4].start()

        rows = pl.ds(pl.multiple_of(t * TG, TG), TG)

        @pl.when(s == 0)
        def _():
            xv = x_ref[...]
            r = lax.rsqrt(jnp.mean(xv * xv, axis=-1, keepdims=True) + EPS)
            h_ref[rows, :] = (xv * r * g_ref[...]).astype(BF16)

        sh = s >> 1
        js = k ^ (((sh & 1) << 1) | (sh >> 1))
        for piece in range(2):
            @pl.when((s & 1) == piece)
            def _(piece=piece):
                p_ref[...] = _dot(h_ref[rows, :], wg_v[js, :, piece * half_c:(piece + 1) * half_c])

        @at_step(n_steps - 1, nt - 1)
        def _():
            for cp in w_direct + cw_direct + w_passed:
                cp.wait_send()
            for st in stores:
                st.wait()

    def x_map(s, t, kr):
        return (jnp.where(s == 0, t, nt - 1), 0)

    def p_map(s, t, kr):
        sh = s >> 1
        return (t, 2 * (kr[0] ^ (((sh & 1) << 1) | (sh >> 1))) + (s & 1))

    hbm = pl.BlockSpec(memory_space=pl.ANY)
    grid_spec = pltpu.PrefetchScalarGridSpec(
        num_scalar_prefetch=1, grid=(n_steps, nt),
        in_specs=[pl.BlockSpec((TG, D_MODEL), x_map),
                  pl.BlockSpec((1, D_MODEL), lambda s, t, kr: (0, 0)),
                  pl.BlockSpec((1, D_MODEL, SHARD_COLS), lambda s, t, kr: (0, 0, 0)),
                  pl.BlockSpec((1, 3 * HEAD), lambda s, t, kr: (0, 0))],
        out_specs=(hbm, pl.BlockSpec((TG, half_c), p_map), hbm, hbm),
        scratch_shapes=[pltpu.VMEM((N_SHARD, D_MODEL, SHARD_COLS), BF16),
                        pltpu.VMEM((8, D_CONV), F32), pltpu.VMEM((SEQ, D_MODEL), BF16),
                        pltpu.SemaphoreType.DMA((N_SEM,)), pltpu.SemaphoreType.DMA((N_SEM,)),
                        pltpu.SemaphoreType.DMA((6,))])
    return pl.pallas_call(
        body, name="gather_proj", grid_spec=grid_spec,
        out_shape=(jax.ShapeDtypeStruct((SEQ, D_MODEL), BF16),
                   jax.ShapeDtypeStruct((SEQ, N_SHARD * SHARD_COLS), F32),
                   jax.ShapeDtypeStruct((N_SHARD, D_MODEL, SHARD_COLS), BF16),
                   jax.ShapeDtypeStruct((8, D_CONV), F32)),
        compiler_params=pltpu.CompilerParams(dimension_semantics=("arbitrary", "arbitrary"),
                                             vmem_limit_bytes=VMEM_LIMIT, collective_id=COLLECTIVE_GATHER),
    )(kidx, x2d, g1, w_in, conv_w)


LAG = 6


def _mix_out(proj, lb_logits, cw, ga, gcn, g64, w_out, x2d, gf, tgt):
    half_o = WO_ROWS // 2
    nblk = SEQ // TB
    n_steps = nblk + LAG

    def body(p_ref, lbl_ref, cw_ref, ga_ref, gcn_ref, g64_ref, wo_ref, x_ref, gf_ref, t_ref,
             aux_ref, sto_ref, dx2_ref, dm_ref, gwo_ref, part_ref,
             st_ref, tail_ref, wog_v, stage, ring, acc_ref, send_sems, recv_sems):
        i = pl.program_id(0)
        x, y, c = lax.axis_index("x"), lax.axis_index("y"), lax.axis_index("c")
        k = 2 * x + y
        sibling = (x, y, 1 - c)
        chips = [(1 - x, y), (x, 1 - y), (1 - x, 1 - y)]
        kjs = [2 * cx + cy for cx, cy in chips]

        def wo_half(kk, cc):
            return wog_v.at[pl.ds(pl.multiple_of(kk * WO_ROWS + cc * half_o, half_o), half_o), :]

        def copy(sem, ref, to):
            return pltpu.make_async_remote_copy(
                src_ref=ref, dst_ref=ref, send_sem=send_sems.at[sem], recv_sem=recv_sems.at[sem],
                device_id=to, device_id_type=MESH)

        wo_direct = [copy(j, wo_half(k, c), (*chip, c)) for j, chip in enumerate(chips)]
        wo_passed = [copy(3 + j, wo_half(kj, c), sibling) for j, kj in enumerate(kjs)]

        @pl.when(i == 0)
        def _():
            barrier = pltpu.get_barrier_semaphore()
            for peer in [sibling] + [(*chip, c) for chip in chips]:
                pl.semaphore_signal(barrier, inc=1, device_id=peer, device_id_type=MESH)
            st_ref[...] = jnp.zeros_like(st_ref)
            tail_ref[...] = jnp.zeros_like(tail_ref)
            acc_ref[...] = jnp.zeros_like(acc_ref)
            part_ref[...] = jnp.zeros_like(part_ref)
            wog_v[pl.ds(pl.multiple_of(k * WO_ROWS, WO_ROWS), WO_ROWS), :] = wo_ref[0].astype(BF16)

        @pl.when(i == LAG - 1)
        def _():
            for j in range(3):
                copy(j, wo_half(kjs[j], c), sibling).wait_recv()
                wo_passed[j].start()

        @pl.when(i == LAG)
        def _():
            for j in range(3):
                copy(3 + j, wo_half(kjs[j], 1 - c), sibling).wait_recv()

        lb = _lower_bound(lbl_ref[...])
        tri = _tri(True)
        causal = _causal()
        g64m = g64_ref[...]
        heads = range(N_HEADS)
        cs = [slice(hd * HEAD, (hd + 1) * HEAD) for hd in heads]
        col = lambda base, hd: slice(base + hd * HEAD, base + (hd + 1) * HEAD)

        def mix_chunk(n):
            sl = pl.ds(n * CHUNK, CHUNK)
            sg = [_sigmoid(p_ref[sl, col(512, hd)]) for hd in heads]
            f = [lb[:, cs[hd]] + (1.0 - lb[:, cs[hd]]) * sg[hd] for hd in heads]
            bc = _exact_left_many(tri, [jnp.log(f[hd]) for hd in heads])
            for hd in heads:
                aux_ref[sl, col(AUX_B, hd)] = bc[hd]
            g = [bc[hd][CHUNK - 1:CHUNK, :] for hd in heads]
            qd = [(p_ref[sl, col(0, hd)] * jnp.exp(bc[hd])).astype(BF16) for hd in heads]
            kk = [1.0 - f[hd] for hd in heads]
            ki = [(kk[hd] * jnp.exp(-bc[hd])).astype(BF16) for hd in heads]
            ke = [(kk[hd] * jnp.exp(g[hd] - bc[hd])).astype(BF16) for hd in heads]
            vb = [p_ref[sl, col(1024, hd)].astype(BF16) for hd in heads]
            st = [st_ref[hd] for hd in heads]
            st_b = [a.astype(BF16) for a in st]
            for hd in heads:
                sto_ref[n, hd] = st_b[hd]
            scm = [_dot_nt(qd[hd], ki[hd]) for hd in heads]
            inter = [_dot_nt(qd[hd], st_b[hd]) for hd in heads]
            upd = [_dot_tn(vb[hd], ke[hd]) for hd in heads]
            intra = [_dot(jnp.where(causal, scm[hd], 0.0).astype(BF16), vb[hd]) for hd in heads]
            for hd in heads:
                st_ref[hd] = st[hd] * jnp.exp(g[hd]) + upd[hd]
                o = intra[hd] + inter[hd]
                aux_ref[sl, col(AUX_O, hd)] = o
                ra = lax.rsqrt(jnp.mean(o * o, axis=-1, keepdims=True) + EPS)
                za = p_ref[sl, col(1536, hd)]
                stage[sl, cs[hd]] = (o * ra * ga_ref[:, cs[hd]] * (za * _sigmoid(za))).astype(BF16)
            yb = []
            for hd in heads:
                cu = p_ref[sl, col(3072, hd)] * p_ref[sl, col(2048, hd)]
                tail = tail_ref[:, cs[hd]]
                cv = (cw_ref[0:1, cs[hd]] * _shift_down(cu, 2, tail) + cw_ref[1:2, cs[hd]] * _shift_down(cu, 1, tail)
                      + cw_ref[2:3, cs[hd]] * cu)
                tail_ref[:, cs[hd]] = cu[CHUNK - 8:, :]
                aux_ref[sl, col(AUX_CV, hd)] = cv
                yb.append(p_ref[sl, col(2560, hd)] * cv)
            ms = _group_mean_many([y * y for y in yb], g64m)
            for hd in heads:
                rb = lax.rsqrt(ms[hd] + EPS)
                zb = p_ref[sl, col(3584, hd)]
                stage[sl, col(512, hd)] = (yb[hd] * rb * gcn_ref[:, cs[hd]] * (zb * _sigmoid(zb))).astype(BF16)

        def step(mix, project, entry=False):
            if project:
                mixed_b = ring[pl.ds(pl.multiple_of((i - LAG) * TB, TB), TB), :]
                y = _dot(mixed_b, wog_v[...])
            if mix:
                mix_chunk(0)
            if project:
                x2 = x_ref[...] + y
                r2 = lax.rsqrt(jnp.mean(x2 * x2, axis=-1, keepdims=True) + EPS)
                n2 = x2 * r2
                gfv = gf_ref[...]
                err = n2 * gfv - t_ref[...]
                loss = 0.5 * jnp.sum(jnp.mean(err * err, axis=-1, keepdims=True), axis=0, keepdims=True)
                dy = err * (1.0 / D_MODEL)
                part_ref[1:2, :] += jnp.sum(dy * n2, axis=0, keepdims=True)
                part_ref[7:8, :] += jnp.broadcast_to(loss, (1, D_MODEL))
                dn = dy * gfv
                dx2 = r2 * (dn - n2 * jnp.mean(dn * n2, axis=-1, keepdims=True))
                dx2_ref[...] = dx2
                dx2_b = dx2.astype(BF16)
            if mix:
                mix_chunk(1)
            if project:
                dm_ref[...] = _dot_nt(dx2_b, wog_v[...])
            if mix:
                mix_chunk(2)
            if entry:
                pl.semaphore_wait(pltpu.get_barrier_semaphore(), 4)
                for cp in wo_direct:
                    cp.start()
            if project:
                acc_ref[...] += _dot_tn(mixed_b, dx2_b)
            if mix:
                mix_chunk(3)
                ring[pl.ds(pl.multiple_of(i * TB, TB), TB), :] = stage[...]

        @pl.when(i == 0)
        def _():
            step(True, False, entry=True)

        @pl.when((i > 0) & (i < LAG))
        def _():
            step(True, False)

        @pl.when((i >= LAG) & (i < nblk))
        def _():
            step(True, True)

        @pl.when(i >= nblk)
        def _():
            step(False, True)

        @pl.when(i == n_steps - 1)
        def _():
            gwo_ref[...] = acc_ref[...].astype(BF16)
            for cp in wo_direct + wo_passed:
                cp.wait_send()

    assert NCB == 4
    row = lambda w: pl.BlockSpec((1, w), lambda i: (0, 0))
    mix_blk = lambda i: jnp.minimum(i, nblk - 1)
    out_blk = lambda i: jnp.clip(i - LAG, 0, nblk - 1)
    tok = lambda: pl.BlockSpec((TB, D_MODEL), lambda i: (out_blk(i), 0))
    return pl.pallas_call(
        body, name="mix_out", grid=(n_steps,),
        out_shape=(jax.ShapeDtypeStruct((SEQ, AUX_COLS), F32),
                   jax.ShapeDtypeStruct((N_CHUNKS, N_HEADS, HEAD, HEAD), BF16),
                   jax.ShapeDtypeStruct((SEQ, D_MODEL), F32),
                   jax.ShapeDtypeStruct((SEQ, D_MODEL), F32),
                   jax.ShapeDtypeStruct((D_MODEL, D_MODEL), BF16),
                   jax.ShapeDtypeStruct((8, D_MODEL), F32)),
        in_specs=[pl.BlockSpec((TB, 4096), lambda i: (jnp.minimum(i, nblk - 1), 0)),
                  pl.BlockSpec((2, D_HGRN), lambda i: (0, 0)),
                  pl.BlockSpec((8, D_CONV), lambda i: (0, 0)),
                  row(D_HGRN), row(D_CONV),
                  pl.BlockSpec((HEAD, HEAD), lambda i: (0, 0)),
                  pl.BlockSpec((1, WO_ROWS, D_MODEL), lambda i: (0, 0, 0)),
                  tok(), row(D_MODEL), tok()],
        out_specs=(pl.BlockSpec((TB, AUX_COLS), lambda i: (mix_blk(i), 0)),
                   pl.BlockSpec((NCB, N_HEADS, HEAD, HEAD), lambda i: (mix_blk(i), 0, 0, 0)),
                   tok(), tok(),
                   pl.BlockSpec((D_MODEL, D_MODEL), lambda i: (0, 0)),
                   pl.BlockSpec((8, D_MODEL), lambda i: (0, 0))),
        scratch_shapes=[pltpu.VMEM((N_HEADS, HEAD, HEAD), F32), pltpu.VMEM((8, D_CONV), F32),
                        pltpu.VMEM((D_MODEL, D_MODEL), BF16), pltpu.VMEM((TB, D_MODEL), BF16),
                        pltpu.VMEM((SEQ, D_MODEL), BF16), pltpu.VMEM((D_MODEL, D_MODEL), F32),
                        pltpu.SemaphoreType.DMA((6,)), pltpu.SemaphoreType.DMA((6,))],
        compiler_params=pltpu.CompilerParams(dimension_semantics=("arbitrary",), vmem_limit_bytes=VMEM_LIMIT,
                                             collective_id=COLLECTIVE_MIX_OUT),
    )(proj, lb_logits, cw, ga, gcn, g64, w_out, x2d, gf, tgt)


def _mix_bwd(proj, aux, states, dmixed, lb_logits, cw, ga, gcn, g64):
    nblk = SEQ // TB

    def body(p_ref, aux_ref, st_ref, dm_ref, lbl_ref, cw_ref, ga_ref, gcn_ref, g64_ref,
             dp_ref, part_ref, dst_ref, head_ref, dlb_ref):
        i = pl.program_id(0)

        @pl.when(i == 0)
        def _():
            dst_ref[...] = jnp.zeros_like(dst_ref)
            head_ref[...] = jnp.zeros_like(head_ref)
            part_ref[...] = jnp.zeros_like(part_ref)
            dlb_ref[...] = jnp.zeros_like(dlb_ref)

        lb = _lower_bound(lbl_ref[...])
        triu = _tri(False)
        causal = _causal()
        g64m = g64_ref[...]
        rowsum = lambda a: jnp.sum(a, axis=0, keepdims=True)
        heads = range(N_HEADS)
        cs = [slice(hd * HEAD, (hd + 1) * HEAD) for hd in heads]
        col = lambda base, hd: slice(base + hd * HEAD, base + (hd + 1) * HEAD)
        for n in reversed(range(NCB)):
            sl = pl.ds(n * CHUNK, CHUNK)
            cvv = [aux_ref[sl, col(AUX_CV, hd)] for hd in heads]
            gb = [p_ref[sl, col(2560, hd)] for hd in heads]
            yb = [gb[hd] * cvv[hd] for hd in heads]
            ms = _group_mean_many([y * y for y in yb], g64m)
            rb, nb, dnb = [], [], []
            for hd in heads:
                rb.append(lax.rsqrt(ms[hd] + EPS))
                nb.append(yb[hd] * rb[hd])
                zb = p_ref[sl, col(3584, hd)]
                sgb = _sigmoid(zb)
                dmb = dm_ref[sl, col(512, hd)]
                silu = zb * sgb
                dgate = dmb * gcn_ref[:, cs[hd]]
                part_ref[2:3, col(512, hd)] += rowsum(dmb * nb[hd] * silu)
                dp_ref[sl, col(3584, hd)] = (dgate * nb[hd] * (sgb + silu * (1.0 - sgb))).astype(BF16)
                dnb.append(dgate * silu)
            mdn = _group_mean_many([dnb[hd] * nb[hd] for hd in heads], g64m)
            for hd in heads:
                dyb = rb[hd] * (dnb[hd] - nb[hd] * mdn[hd])
                dp_ref[sl, col(2560, hd)] = (dyb * cvv[hd]).astype(BF16)
                dcv = dyb * gb[hd]
                head = head_ref[:, cs[hd]]
                dcv1 = _shift_up(dcv, 1, head)
                dcv2 = _shift_up(dcv, 2, head)
                head_ref[:, cs[hd]] = dcv[0:8, :]
                u = p_ref[sl, col(2048, hd)]
                gc = p_ref[sl, col(3072, hd)]
                cu = gc * u
                part_ref[4:5, cs[hd]] += rowsum(dcv2 * cu)
                part_ref[5:6, cs[hd]] += rowsum(dcv1 * cu)
                part_ref[6:7, cs[hd]] += rowsum(dcv * cu)
                dcu = cw_ref[2:3, cs[hd]] * dcv + cw_ref[1:2, cs[hd]] * dcv1 + cw_ref[0:1, cs[hd]] * dcv2
                dp_ref[sl, col(3072, hd)] = (dcu * u).astype(BF16)
                dp_ref[sl, col(2048, hd)] = (dcu * gc).astype(BF16)
            do_b = []
            for hd in heads:
                ov = aux_ref[sl, col(AUX_O, hd)]
                ra = lax.rsqrt(jnp.mean(ov * ov, axis=-1, keepdims=True) + EPS)
                na = ov * ra
                za = p_ref[sl, col(1536, hd)]
                sga = _sigmoid(za)
                dma = dm_ref[sl, cs[hd]]
                silu = za * sga
                dgate = dma * ga_ref[:, cs[hd]]
                part_ref[2:3, cs[hd]] += rowsum(dma * na * silu)
                dp_ref[sl, col(1536, hd)] = (dgate * na * (sga + silu * (1.0 - sga))).astype(BF16)
                dna = dgate * silu
                do_b.append((ra * (dna - na * jnp.mean(dna * na, axis=-1, keepdims=True))).astype(BF16))
            s = [_sigmoid(p_ref[sl, col(512, hd)]) for hd in heads]
            f = [lb[:, cs[hd]] + (1.0 - lb[:, cs[hd]]) * s[hd] for hd in heads]
            bc = [aux_ref[sl, col(AUX_B, hd)] for hd in heads]
            g = [bc[hd][CHUNK - 1:CHUNK, :] for hd in heads]
            eb = [jnp.exp(bc[hd]) for hd in heads]
            enb = [jnp.exp(-bc[hd]) for hd in heads]
            eg = [jnp.exp(g[hd] - bc[hd]) for hd in heads]
            dec = [jnp.exp(g[hd]) for hd in heads]
            qd = [p_ref[sl, cs[hd]] * eb[hd] for hd in heads]
            kk = [1.0 - f[hd] for hd in heads]
            ki = [kk[hd] * enb[hd] for hd in heads]
            ke = [kk[hd] * eg[hd] for hd in heads]
            qd_b = [a.astype(BF16) for a in qd]
            ki_b = [a.astype(BF16) for a in ki]
            ke_b = [a.astype(BF16) for a in ke]
            vb = [p_ref[sl, col(1024, hd)].astype(BF16) for hd in heads]
            st_b = [st_ref[n, hd] for hd in heads]
            dst = [dst_ref[hd] for hd in heads]
            dst_b = [a.astype(BF16) for a in dst]
            scm = [_dot_nt(qd_b[hd], ki_b[hd]) for hd in heads]
            amm = [_dot_nt(do_b[hd], vb[hd]) for hd in heads]
            dqd2 = [_dot(do_b[hd], st_b[hd]) for hd in heads]
            dke = [_dot(vb[hd], dst_b[hd]) for hd in heads]
            dv2 = [_dot_nt(ke_b[hd], dst_b[hd]) for hd in heads]
            dsu = [_dot_tn(do_b[hd], qd_b[hd]) for hd in heads]
            sc = [jnp.where(causal, scm[hd], 0.0).astype(BF16) for hd in heads]
            am = [jnp.where(causal, amm[hd], 0.0).astype(BF16) for hd in heads]
            dqd1 = [_dot(am[hd], ki_b[hd]) for hd in heads]
            dki = [_dot_tn(am[hd], qd_b[hd]) for hd in heads]
            dv1 = [_dot_tn(sc[hd], do_b[hd]) for hd in heads]
            db, dgv, dkk = [], [], []
            for hd in heads:
                dqd = dqd1[hd] + dqd2[hd]
                ddec = rowsum(dst[hd] * st_b[hd].astype(F32))
                dst_ref[hd] = dst[hd] * dec[hd] + dsu[hd]
                dp_ref[sl, cs[hd]] = (dqd * eb[hd]).astype(BF16)
                dp_ref[sl, col(1024, hd)] = (dv1[hd] + dv2[hd]).astype(BF16)
                dke_eg = dke[hd] * eg[hd]
                dkk.append(dki[hd] * enb[hd] + dke_eg)
                db.append(dqd * qd[hd] - kk[hd] * dkk[hd])
                dgv.append(rowsum(kk[hd] * dke_eg) + ddec * dec[hd])
            rc = _exact_left_many(triu, db, 2)
            for hd in heads:
                df = (rc[hd] + dgv[hd]) / f[hd] - dkk[hd]
                one_s = 1.0 - s[hd]
                dlb_ref[:, cs[hd]] += rowsum(df * one_s)
                dp_ref[sl, col(512, hd)] = (df * (1.0 - lb[:, cs[hd]]) * s[hd] * one_s).astype(BF16)

        @pl.when(i == nblk - 1)
        def _():
            row = dlb_ref[...] * lb * (1.0 - lb)
            part_ref[3:4, 0:D_HGRN] = row
            part_ref[3:4, D_HGRN:] = -row

    rev = lambda w: pl.BlockSpec((TB, w), lambda i: (nblk - 1 - i, 0))
    row = lambda w: pl.BlockSpec((1, w), lambda i: (0, 0))
    return pl.pallas_call(
        body, name="mix_bwd", grid=(nblk,),
        out_shape=(jax.ShapeDtypeStruct((SEQ, 4096), BF16),
                   jax.ShapeDtypeStruct((8, D_MODEL), F32)),
        in_specs=[rev(4096), rev(AUX_COLS),
                  pl.BlockSpec((NCB, N_HEADS, HEAD, HEAD), lambda i: (nblk - 1 - i, 0, 0, 0)),
                  rev(D_MODEL),
                  pl.BlockSpec((2, D_HGRN), lambda i: (0, 0)),
                  pl.BlockSpec((8, D_CONV), lambda i: (0, 0)),
                  row(D_HGRN), row(D_CONV),
                  pl.BlockSpec((HEAD, HEAD), lambda i: (0, 0))],
        out_specs=(rev(4096), pl.BlockSpec((8, D_MODEL), lambda i: (0, 0))),
        scratch_shapes=[pltpu.VMEM((N_HEADS, HEAD, HEAD), F32), pltpu.VMEM((8, D_CONV), F32),
                        pltpu.VMEM((1, D_HGRN), F32)],
        compiler_params=pltpu.CompilerParams(dimension_semantics=("arbitrary",), vmem_limit_bytes=VMEM_LIMIT),
    )(proj, aux, states, dmixed, lb_logits, cw, ga, gcn, g64)


TT = 1024
TX = 512
(SEM_D2D, SEM_D2D_O, SEM_ICI, SEM_ICI_O, SEM_FIN, SEM_FIN_O, SEM_SMALL, SEM_VIA, SEM_NORM, N_SEM_TAIL) = (
    0, 4, 5, 8, 11, 12, 12, 20, 22, 30)


def _bwd_tail(kidx, h, dproj, wg, gwo, x2d, dx2, g1, small_a, small_b):
    hw = D_MODEL // 2
    ho = WO_ROWS // 2
    nt = SEQ // TT
    norm_step = 2 * N_SHARD
    n_steps = norm_step + SEQ // TX // nt

    def body(k_ref, h_ref, dp_ref, w_ref, gwo_ref, x_ref, dx2_ref, g_ref, sm_ref, smb_ref,
             gx_ref, gw_out, gwo_out, osm_ref,
             acc, dh, sendbuf, keep, sibrcv, rcv, merge, sib_o, p_o, rcv_o, res_o, sm_buf, dng_buf, dng,
             send_sems, recv_sems, out_sems):
        s, t = pl.program_id(0), pl.program_id(1)
        x, y, c = lax.axis_index("x"), lax.axis_index("y"), lax.axis_index("c")
        k = 2 * x + y
        me = 4 * x + 2 * y + c
        sibling = (x, y, 1 - c)
        chips = [(1 - x, 1 - y), (1 - x, y), (x, 1 - y)]
        kjs = [2 * cx + cy for cx, cy in chips]
        mine = pl.ds(pl.multiple_of(c * hw, hw), hw)
        other = pl.ds(pl.multiple_of((1 - c) * hw, hw), hw)
        mine_o = pl.ds(pl.multiple_of(c * ho, ho), ho)
        other_o = pl.ds(pl.multiple_of((1 - c) * ho, ho), ho)

        def copy(sem, src, dst, to):
            return pltpu.make_async_remote_copy(
                src_ref=src, dst_ref=dst, send_sem=send_sems.at[sem], recv_sem=recv_sems.at[sem],
                device_id=to, device_id_type=MESH)

        def at_step(sv, tv):
            return pl.when((s == sv) & (t == tv))

        def at_norm_block(b):
            return at_step(norm_step + b // nt, b % nt)

        d2d = [copy(SEM_D2D + sv, sendbuf.at[sv], sibrcv.at[sv], sibling) for sv in range(N_SHARD)]
        d2d_o = copy(SEM_D2D_O, gwo_ref.at[:, other_o, :], sib_o, sibling)
        ici = {sv: copy(SEM_ICI + sv, keep.at[sv], rcv.at[sv - 1], (*chips[sv], c)) for sv in (1, 2)}
        qh = hw // 2
        via = [copy(SEM_VIA, keep.at[0, 0:qh, :], merge.at[1], (*chips[1], c)),
               copy(SEM_VIA + 1, keep.at[0, qh:hw, :], merge.at[0], (*chips[2], c))]
        merged_rows = [slice(qh, hw), slice(0, qh)]
        ici_o = [copy(SEM_ICI_O + sv, p_o.at[kjs[sv]], rcv_o.at[sv], (*chips[sv], c)) for sv in range(3)]
        fin = copy(SEM_FIN, acc.at[mine, :], gw_out.at[mine, :], sibling)
        fin_o = copy(SEM_FIN_O, res_o.at[mine_o, :], res_o.at[mine_o, :], sibling)
        peers = [(x ^ (m >> 2), y ^ ((m >> 1) & 1), c ^ (m & 1)) for m in range(1, N_DEV)]
        smalls = [copy(SEM_SMALL + 1 + j, sm_buf.at[me], sm_buf.at[me], to) for j, to in enumerate(peers)]
        dngs = [copy(SEM_NORM + 1 + j, dng_buf.at[me], dng_buf.at[me], to) for j, to in enumerate(peers)]
        store_w = pltpu.make_async_copy(acc.at[mine, :], gw_out.at[mine, :], out_sems.at[0])
        store_o = pltpu.make_async_copy(res_o, gwo_out, out_sems.at[1])

        @at_step(0, 0)
        def _():
            barrier = pltpu.get_barrier_semaphore()
            for to in peers:
                pl.semaphore_signal(barrier, inc=1, device_id=to, device_id_type=MESH)
            sm_buf[me] = sm_ref[...] + smb_ref[...]
            pl.semaphore_wait(barrier, N_DEV - 1)
            d2d_o.start()
            for cp in smalls:
                cp.start()

        @at_step(0, 1)
        def _():
            d2d_o.wait_recv()
            for j in range(N_SHARD):
                p_o[j] = (gwo_ref[j, mine_o, :].astype(F32) + sib_o[j].astype(F32)).astype(BF16)
            res_o[mine_o, :] = gwo_ref[k, mine_o, :].astype(F32) + sib_o[k].astype(F32)
            for cp in ici_o:
                cp.start()

        rows = pl.ds(pl.multiple_of(t * TT, TT), TT)

        @pl.when((s < N_SHARD) & (t == 0))
        def _():
            acc[...] = _dot_tn(h_ref[...], dp_ref[...])

        @pl.when((s < N_SHARD) & (t > 0))
        def _():
            acc[...] += _dot_tn(h_ref[...], dp_ref[...])

        for sv in range(N_SHARD):
            @at_step(sv, nt - 1)
            def _(sv=sv):
                sendbuf[sv] = acc[other, :].astype(BF16)
                if sv < 3:
                    keep[sv] = acc[mine, :].astype(BF16)
                d2d[sv].start()

        @at_step(1, 0)
        def _():
            d2d[0].wait_recv()
            keep[0] = (keep[0].astype(F32) + sibrcv[0].astype(F32)).astype(BF16)
            for cp in via:
                cp.start()

        for sv in (1, 2):
            @at_step(sv + 2, 0)
            def _(sv=sv):
                d2d[sv].wait_recv()
                keep[sv] = (keep[sv].astype(F32) + sibrcv[sv].astype(F32)).astype(BF16)
                via[2 - sv].wait_recv()
                rows_m = merged_rows[sv - 1]
                keep[sv, rows_m, :] = (keep[sv, rows_m, :].astype(F32) + merge[sv - 1].astype(F32)).astype(BF16)
                ici[sv].start()

        @pl.when(s == N_SHARD)
        def _():
            dh[rows, :] = _dot_nt(dp_ref[...], w_ref[0])

        @pl.when((s > N_SHARD) & (s < norm_step))
        def _():
            dh[rows, :] += _dot_nt(dp_ref[...], w_ref[0])

        @at_norm_block(0)
        def _():
            d2d[3].wait_recv()
            acc[mine, :] += sibrcv[3].astype(F32)

        @at_norm_block(1)
        def _():
            tot = res_o[mine_o, :]
            for sv in range(3):
                ici_o[sv].wait_recv()
                tot = tot + rcv_o[sv].astype(F32)
            res_o[mine_o, :] = tot
            fin_o.start()

        @at_norm_block(2)
        def _():
            ici[1].wait_recv()
            acc[mine, :] += rcv[0].astype(F32)

        @at_norm_block(SEQ // TX - 2)
        def _():
            ici[2].wait_recv()
            acc[mine, :] += rcv[1].astype(F32)
            fin.start()
            store_w.start()
            fin_o.wait_recv()
            store_o.start()

        @at_norm_block(0)
        def _():
            dng[...] = jnp.zeros_like(dng)

        @pl.when(s >= norm_step)
        def _():
            blk = (s - norm_step) * nt + t
            dhv = dh[pl.ds(pl.multiple_of(blk * TX, TX), TX), :]
            xv = x_ref[...]
            r = lax.rsqrt(jnp.mean(xv * xv, axis=-1, keepdims=True) + EPS)
            xn = xv * r
            dng[...] += jnp.sum(dhv * xn, axis=0, keepdims=True)
            dxn = dhv * g_ref[...]
            gx_ref[...] = dx2_ref[...] + r * (dxn - xn * jnp.mean(dxn * xn, axis=-1, keepdims=True))

        @at_step(n_steps - 1, nt - 1)
        def _():
            dng_buf[me] = dng[...]
            for cp in dngs:
                cp.start()
            for m in range(1, N_DEV):
                copy(SEM_SMALL + m, sm_buf.at[0], sm_buf.at[0], sibling).wait_recv()
            tot = sm_buf[0]
            for d in range(1, N_DEV):
                tot = tot + sm_buf[d]
            osm_ref[...] = tot
            for m in range(1, N_DEV):
                copy(SEM_NORM + m, dng_buf.at[0], dng_buf.at[0], sibling).wait_recv()
            tot = dng_buf[0]
            for d in range(1, N_DEV):
                tot = tot + dng_buf[d]
            osm_ref[0:1, :] = tot
            fin.wait_recv()
            for cp in d2d + [d2d_o] + via + list(ici.values()) + ici_o + [fin, fin_o] + smalls + dngs:
                cp.wait_send()
            store_o.wait()
            store_w.wait()

    def shard_of(s, kr):
        order = jnp.where(s < N_SHARD, s, jnp.where(s < norm_step, s - N_SHARD, 3))
        return kr[0] ^ (3 - order)

    def h_map(s, t, kr):
        return (jnp.where(s < N_SHARD, t, nt - 1), 0)

    def dp_map(s, t, kr):
        return (jnp.where(s < norm_step, t, nt - 1), shard_of(s, kr))

    def w_map(s, t, kr):
        return (shard_of(jnp.maximum(s, N_SHARD), kr), 0, 0)

    def blk_map(s, t, kr):
        return (jnp.where(s < norm_step, 0, (s - norm_step) * nt + t), 0)

    hbm = pl.BlockSpec(memory_space=pl.ANY)
    grid_spec = pltpu.PrefetchScalarGridSpec(
        num_scalar_prefetch=1, grid=(n_steps, nt),
        in_specs=[pl.BlockSpec((TT, D_MODEL), h_map),
                  pl.BlockSpec((TT, SHARD_COLS), dp_map),
                  pl.BlockSpec((1, D_MODEL, SHARD_COLS), w_map),
                  pl.BlockSpec((N_SHARD, WO_ROWS, D_MODEL), lambda s, t, kr: (0, 0, 0),
                               pipeline_mode=pl.Buffered(1)),
                  pl.BlockSpec((TX, D_MODEL), blk_map),
                  pl.BlockSpec((TX, D_MODEL), blk_map),
                  pl.BlockSpec((1, D_MODEL), lambda s, t, kr: (0, 0)),
                  pl.BlockSpec((8, D_MODEL), lambda s, t, kr: (0, 0)),
                  pl.BlockSpec((8, D_MODEL), lambda s, t, kr: (0, 0))],
        out_specs=(pl.BlockSpec((TX, D_MODEL), blk_map), hbm, hbm,
                   pl.BlockSpec((8, D_MODEL), lambda s, t, kr: (0, 0))),
        scratch_shapes=[pltpu.VMEM((D_MODEL, SHARD_COLS), F32), pltpu.VMEM((SEQ, D_MODEL), F32),
                        pltpu.VMEM((N_SHARD, hw, SHARD_COLS), BF16), pltpu.VMEM((3, hw, SHARD_COLS), BF16),
                        pltpu.VMEM((N_SHARD, hw, SHARD_COLS), BF16), pltpu.VMEM((2, hw, SHARD_COLS), BF16),
                        pltpu.VMEM((2, hw // 2, SHARD_COLS), BF16),
                        pltpu.VMEM((N_SHARD, ho, D_MODEL), BF16), pltpu.VMEM((N_SHARD, ho, D_MODEL), BF16),
                        pltpu.VMEM((3, ho, D_MODEL), BF16), pltpu.VMEM((WO_ROWS, D_MODEL), F32),
                        pltpu.VMEM((N_DEV, 8, D_MODEL), F32), pltpu.VMEM((N_DEV, 1, D_MODEL), F32),
                        pltpu.VMEM((1, D_MODEL), F32),
                        pltpu.SemaphoreType.DMA((N_SEM_TAIL,)), pltpu.SemaphoreType.DMA((N_SEM_TAIL,)),
                        pltpu.SemaphoreType.DMA((2,))])
    return pl.pallas_call(
        body, name="bwd_tail", grid_spec=grid_spec,
        out_shape=(jax.ShapeDtypeStruct((SEQ, D_MODEL), F32),
                   jax.ShapeDtypeStruct((D_MODEL, SHARD_COLS), F32),
                   jax.ShapeDtypeStruct((WO_ROWS, D_MODEL), F32),
                   jax.ShapeDtypeStruct((8, D_MODEL), F32)),
        compiler_params=pltpu.CompilerParams(dimension_semantics=("arbitrary", "arbitrary"),
                                             vmem_limit_bytes=61 * 1024 * 1024, collective_id=COLLECTIVE_TAIL),
    )(kidx, h, dproj, wg, gwo, x2d, dx2, g1, small_a, small_b)


def _adam_update(w, g, m, v):
    nm = ADAM_B1 * m + (1.0 - ADAM_B1) * g
    nv = ADAM_B2 * v + (1.0 - ADAM_B2) * (g * g)
    m_hat = nm / (1.0 - ADAM_B1 ** ADAM_STEP)
    v_hat = nv / (1.0 - ADAM_B2 ** ADAM_STEP)
    return -ADAM_LR * (m_hat / (jnp.sqrt(v_hat) + ADAM_EPS) + ADAM_WD * w), nm, nv


def _adamw_all(tot, g_w_in, g_w_out, big, small, grad_x):
    n = len(small)
    rows = WO_ROWS
    steps = D_MODEL // rows

    def body(tot_ref, *refs):
        gx_ref, gx_out = refs[2 + 3 * (2 + n)], refs[-1]
        gx_out[...] = gx_ref[...]
        ins, outs = refs[:2 + 3 * (2 + n)], refs[3 + 3 * (2 + n):-1]
        g_refs, wmv = ins[:2], ins[2:]
        loss_ref, quads = outs[0], outs[1:]

        def update(j, g):
            w_ref, m_ref, v_ref = wmv[3 * j:3 * j + 3]
            g_ref, d_ref, nm_ref, nv_ref = quads[4 * j:4 * j + 4]
            g_ref[...] = g
            d_ref[...], nm_ref[...], nv_ref[...] = _adam_update(w_ref[...], g, m_ref[...], v_ref[...])

        update(0, g_refs[0][...])

        @pl.when(pl.program_id(0) == 0)
        def _():
            update(1, g_refs[1][...])
            k = 2 * lax.axis_index("x") + lax.axis_index("y")
            mine = pl.ds(pl.multiple_of(k * HEAD, HEAD), HEAD)
            loss_ref[...] = tot_ref[7:8, 0:1]
            grads = [tot_ref[0:1, :], tot_ref[1:2, :], tot_ref[2:3, 0:D_HGRN], tot_ref[2:3, D_HGRN:],
                     jnp.concatenate([tot_ref[3:4, 0:D_HGRN], tot_ref[3:4, D_HGRN:]], axis=0),
                     jnp.concatenate([tot_ref[4 + tap:5 + tap, mine] for tap in range(3)], axis=1)]
            for j, g in enumerate(grads):
                update(2 + j, g)

    whole = lambda a: pl.BlockSpec(a.shape, lambda i: (0, 0))
    blk = pl.BlockSpec((rows, SHARD_COLS), lambda i: (i, 0))
    arrays = [a for triple in big + small for a in triple]
    in_specs = ([whole(tot), blk, whole(g_w_out)] + [blk] * 3 + [whole(a) for a in arrays[3:]])
    shapes = [big[0][0], big[1][0]] + [w for w, _, _ in small]
    out_shape = (jax.ShapeDtypeStruct((1, 1), F32),) + tuple(
        jax.ShapeDtypeStruct(w.shape, F32) for w in shapes for _ in range(4))
    out_specs = (pl.BlockSpec((1, 1), lambda i: (0, 0)),) + (blk,) * 4 + tuple(
        whole(w) for w in shapes[1:] for _ in range(4))
    gx_blk = pl.BlockSpec((SEQ // steps, D_MODEL), lambda i: (i, 0))
    outs = pl.pallas_call(
        body, name="adamw_all", grid=(steps,),
        out_shape=out_shape + (jax.ShapeDtypeStruct(grad_x.shape, F32),),
        in_specs=in_specs + [gx_blk], out_specs=out_specs + (gx_blk,),
        compiler_params=pltpu.CompilerParams(dimension_semantics=("arbitrary",), vmem_limit_bytes=VMEM_LIMIT),
    )(tot, g_w_in, g_w_out, *arrays, grad_x)
    return [outs[0]] + [outs[1 + 4 * j:5 + 4 * j] for j in range(2 + n)] + [outs[-1]]


def _local_step(x2d, tgt, proj, lb_logits, cw, ga, gcn, w_out, gf):
    g64 = _group_matrix(HEAD, CONV_GROUP)
    aux, states, dx2, dmixed, gwo, part_out = _mix_out(proj, lb_logits, cw, ga, gcn, g64, w_out, x2d, gf, tgt)
    dproj, part_mix = _mix_bwd(proj, aux, states, dmixed, lb_logits, cw, ga, gcn, g64)
    return dproj, dx2, gwo.reshape(N_SHARD, WO_ROWS, D_MODEL), part_out, part_mix


def kernel(x, norm_gain, w_in, lb_logits, conv_w, hgrn_norm_gain, conv_norm_gain, w_out, final_norm_gain, loss_target, m_norm_gain, m_w_in, m_lb_logits, m_conv_w, m_hgrn_norm_gain, m_conv_norm_gain, m_w_out, m_final_norm_gain, v_norm_gain, v_w_in, v_lb_logits, v_conv_w, v_hgrn_norm_gain, v_conv_norm_gain, v_w_out, v_final_norm_gain):
    k = 2 * lax.axis_index("x") + lax.axis_index("y")
    kidx = jnp.reshape(k, (1,)).astype(jnp.int32)
    row = lambda a: a.reshape(1, D_MODEL)
    taps = lambda a: a.reshape(1, 3 * HEAD)
    h, proj, wg, cw = _gather_proj(kidx, x[0], norm_gain, w_in, taps(conv_w))
    dproj, dx2, gwo, part_out, part_mix = _local_step(
        x[0], loss_target[0], proj, lb_logits, cw, hgrn_norm_gain, conv_norm_gain, w_out, row(final_norm_gain))
    rgrad_x, rg_w_in, rg_w_out, tot = _bwd_tail(kidx, h, dproj, wg, gwo, x[0], dx2, norm_gain, part_out, part_mix)

    (loss, (g_w_in, d_w_in, nm_w_in, nv_w_in), (g_w_out, d_w_out, nm_w_out, nv_w_out),
     (g_norm_gain, d_ng, nm_ng, nv_ng), (g_final, d_fg, nm_fg, nv_fg), (g_hgrn, d_hg, nm_hg, nv_hg),
     (g_convn, d_cg, nm_cg, nv_cg), (g_lb, d_lb, nm_lb, nv_lb), (g_conv_w, d_cw, nm_cw, nv_cw),
     grad_x) = _adamw_all(
        tot, rg_w_in, rg_w_out,
        [(w_in[0], m_w_in[0], v_w_in[0]), (w_out[0], m_w_out[0], v_w_out[0])],
        [(norm_gain, m_norm_gain, v_norm_gain),
         (row(final_norm_gain), row(m_final_norm_gain), row(v_final_norm_gain)),
         (hgrn_norm_gain, m_hgrn_norm_gain, v_hgrn_norm_gain),
         (conv_norm_gain, m_conv_norm_gain, v_conv_norm_gain),
         (lb_logits, m_lb_logits, v_lb_logits),
         (taps(conv_w), taps(m_conv_w), taps(v_conv_w))],
        rgrad_x)
    flat = lambda a: a.reshape(D_MODEL)
    untap = lambda a: a.reshape(1, 3, HEAD)
    return (loss.reshape(()), grad_x[None],
            g_norm_gain, g_w_in[None], g_lb, untap(g_conv_w), g_hgrn, g_convn, g_w_out[None], flat(g_final),
            d_ng, d_w_in[None], d_lb, untap(d_cw), d_hg, d_cg, d_w_out[None], flat(d_fg),
            nm_ng, nm_w_in[None], nm_lb, untap(nm_cw), nm_hg, nm_cg, nm_w_out[None], flat(nm_fg),
            nv_ng, nv_w_in[None], nv_lb, untap(nv_cw), nv_hg, nv_cg, nv_w_out[None], flat(nv_fg))
```

```python
import jax
import jax.numpy as jnp
import numpy as np
from jax import lax
from jax.experimental import pallas as pl
from jax.experimental.pallas import tpu as pltpu

F32 = jnp.float32
BF16 = jnp.bfloat16
MESH = pl.DeviceIdType.MESH

SEQ = 2048
D_MODEL = 1024
D_HGRN = 512
D_CONV = 512
HEAD = 128
N_HEADS = 4
CHUNK = 64
CONV_GROUP = 64
N_SHARD = 4
SHARD_COLS = 1024
WO_ROWS = 256
EPS = 1e-6
TB = 256
NCB = TB // CHUNK
N_CHUNKS = SEQ // CHUNK
N_DEV = 8
COLLECTIVE_GATHER, COLLECTIVE_MIX_OUT, COLLECTIVE_TAIL = 1, 0, 2
AUX_O, AUX_CV, AUX_B, AUX_COLS = 0, 512, 1024, 1536

ADAM_LR = 0.001
ADAM_B1 = 0.9
ADAM_B2 = 0.999
ADAM_EPS = 1e-08
ADAM_WD = 0.01
ADAM_STEP = 10

VMEM_LIMIT = 56 * 1024 * 1024


def _dot(a, b):
    return jnp.dot(a, b, preferred_element_type=F32)


def _dot_nt(a, b):
    return lax.dot_general(a, b, (((1,), (1,)), ((), ())), preferred_element_type=F32)


def _dot_tn(a, b):
    return lax.dot_general(a, b, (((0,), (0,)), ((), ())), preferred_element_type=F32)


def _split_bf16(x, n):
    parts = []
    r = x
    for _ in range(n):
        p = r.astype(BF16)
        parts.append(p)
        r = r - p.astype(F32)
    return parts


def _exact_left(m, x, n=3):
    acc = None
    for p in _split_bf16(x, n):
        t = _dot(m, p)
        acc = t if acc is None else acc + t
    return acc


def _exact_left_many(m, xs, n=3):
    parts = [_split_bf16(x, n) for x in xs]
    accs = [None] * len(xs)
    for i in range(n):
        for j in range(len(xs)):
            t = _dot(m, parts[j][i])
            accs[j] = t if accs[j] is None else accs[j] + t
    return accs


def _group_mean_many(xs, gmat, n=2):
    parts = [_split_bf16(x, n) for x in xs]
    accs = [None] * len(xs)
    for i in range(n):
        for j in range(len(xs)):
            t = _dot(parts[j][i], gmat)
            accs[j] = t if accs[j] is None else accs[j] + t
    return accs


def _group_mean(x, gmat, n=2):
    w = gmat.shape[0]
    outs = []
    for c0 in range(0, x.shape[1], w):
        acc = None
        for p in _split_bf16(x[:, c0:c0 + w], n):
            t = _dot(p, gmat)
            acc = t if acc is None else acc + t
        outs.append(acc)
    return jnp.concatenate(outs, axis=1)


def _sigmoid(x):
    return 1.0 / (1.0 + jnp.exp(-x))


def _lower_bound(lbl):
    l0 = lbl[0:1, :]
    l1 = lbl[1:2, :]
    m = jnp.maximum(l0, l1)
    e0 = jnp.exp(l0 - m)
    e1 = jnp.exp(l1 - m)
    return e0 / (e0 + e1)


def _tri(lower):
    r = lax.broadcasted_iota(jnp.int32, (CHUNK, CHUNK), 0)
    c = lax.broadcasted_iota(jnp.int32, (CHUNK, CHUNK), 1)
    return jnp.where((c <= r) if lower else (c >= r), 1.0, 0.0).astype(BF16)


def _causal():
    r = lax.broadcasted_iota(jnp.int32, (CHUNK, CHUNK), 0)
    c = lax.broadcasted_iota(jnp.int32, (CHUNK, CHUNK), 1)
    return c <= r


def _shift_down(x, sh, prev_tail):
    r = pltpu.roll(x, sh, 0)
    pt = pltpu.roll(prev_tail, sh, 0)
    rows = lax.broadcasted_iota(jnp.int32, prev_tail.shape, 0)
    top = jnp.where(rows < sh, pt, r[0:8])
    return jnp.concatenate([top, r[8:]], axis=0)


def _shift_up(x, sh, next_head):
    n = x.shape[0]
    r = pltpu.roll(x, n - sh, 0)
    nh = pltpu.roll(next_head, 8 - sh, 0)
    rows = lax.broadcasted_iota(jnp.int32, next_head.shape, 0)
    bot = jnp.where(rows >= 8 - sh, nh, r[n - 8:])
    return jnp.concatenate([r[:n - 8], bot], axis=0)


def _group_matrix(width, group):
    r = np.arange(width)[:, None] // group
    c = np.arange(width)[None, :] // group
    return jnp.asarray(np.where(r == c, 1.0 / group, 0.0), dtype=BF16)


TG = 1024
SEM_W, SEM_CW, SEM_W_FWD, N_SEM = 0, 4, 7, 11


def _gather_proj(kidx, x2d, g1, w_in, conv_w):
    half_w = D_MODEL // 2
    half_c = SHARD_COLS // 2
    nt = SEQ // TG
    n_steps = 2 * N_SHARD

    def body(k_ref, x_ref, g_ref, w_ref, cw_ref, h_out, p_ref, wg_out, cwg_out,
             wg_v, cwg_v, h_ref, send_sems, recv_sems, out_sems):
        s, t = pl.program_id(0), pl.program_id(1)
        x, y, c = lax.axis_index("x"), lax.axis_index("y"), lax.axis_index("c")
        k = 2 * x + y
        sibling = (x, y, 1 - c)
        chips = [(1 - x, y), (x, 1 - y), (1 - x, 1 - y)]
        kjs = [2 * cx + cy for cx, cy in chips]
        diag = (*chips[2], c)

        def w_half(kk, cc):
            return wg_v.at[kk, pl.ds(cc * half_w, half_w), :]

        def w_quarter(kk, cc, piece):
            return wg_v.at[kk, pl.ds(cc * half_w, half_w), piece * half_c:(piece + 1) * half_c]

        def cw_of(kk):
            return cwg_v.at[:, pl.ds(pl.multiple_of(kk * HEAD, HEAD), HEAD)]

        def copy(sem, ref, to):
            return pltpu.make_async_remote_copy(
                src_ref=ref, dst_ref=ref, send_sem=send_sems.at[sem], recv_sem=recv_sems.at[sem],
                device_id=to, device_id_type=MESH)

        def at_step(sv, tv):
            return pl.when((s == sv) & (t == tv))

        w_direct = ([copy(SEM_W + j, w_half(k, c), (*chips[j], c)) for j in range(2)]
                    + [copy(SEM_W + 2 + p, w_quarter(k, c, p), diag) for p in range(2)])
        cw_direct = [copy(SEM_CW + j, cw_of(k), (*chip, c)) for j, chip in enumerate(chips)]
        w_passed = ([copy(SEM_W_FWD + j, w_half(kjs[j], c), sibling) for j in range(2)]
                    + [copy(SEM_W_FWD + 2 + p, w_quarter(kjs[2], c, p), sibling) for p in range(2)])
        stores = ([pltpu.make_async_copy(wg_v.at[kk], wg_out.at[kk], out_sems.at[i])
                   for i, kk in enumerate([k] + kjs)]
                  + [pltpu.make_async_copy(cwg_v, cwg_out, out_sems.at[4]),
                     pltpu.make_async_copy(h_ref, h_out, out_sems.at[5])])

        @at_step(0, 0)
        def _():
            barrier = pltpu.get_barrier_semaphore()
            for peer in [sibling] + [(*chip, c) for chip in chips]:
                pl.semaphore_signal(barrier, inc=1, device_id=peer, device_id_type=MESH)
            wg_v[k] = w_ref[0].astype(BF16)
            mine = pl.ds(pl.multiple_of(k * HEAD, HEAD), HEAD)
            cwg_v[:, mine] = jnp.zeros((8, HEAD), F32)
            for tap in range(3):
                cwg_v[tap:tap + 1, mine] = cw_ref[:, tap * HEAD:(tap + 1) * HEAD]
            pl.semaphore_wait(barrier, 4)
            for cp in w_direct + cw_direct:
                cp.start()
            stores[0].start()

        @at_step(1, 0)
        def _():
            stores[5].start()

        @at_step(2, 0)
        def _():
            for j in range(2):
                copy(SEM_W + j, w_half(kjs[j], c), sibling).wait_recv()
                w_passed[j].start()
            copy(SEM_W_FWD, w_half(kjs[0], 1 - c), sibling).wait_recv()
            stores[1].start()

        @at_step(4, 0)
        def _():
            copy(SEM_W_FWD + 1, w_half(kjs[1], 1 - c), sibling).wait_recv()
            stores[2].start()

        for p in range(2):
            @at_step(6 + p, 0)
            def _(p=p):
                copy(SEM_W + 2 + p, w_quarter(kjs[2], c, p), sibling).wait_recv()
                w_passed[2 + p].start()
                copy(SEM_W_FWD + 2 + p, w_quarter(kjs[2], 1 - c, p), sibling).wait_recv()
                if p == 1:
                    stores[3].start()
                    for j in range(3):
                        copy(SEM_CW + j, cw_of(kjs[j]), sibling).wait_recv()
                    stores[4].start()

        rows = pl.ds(pl.multiple_of(t * TG, TG), TG)

        @pl.when(s == 0)
        def _():
            xv = x_ref[...]
            r = lax.rsqrt(jnp.mean(xv * xv, axis=-1, keepdims=True) + EPS)
            h_ref[rows, :] = (xv * r * g_ref[...]).astype(BF16)

        sh = s >> 1
        js = k ^ (((sh & 1) << 1) | (sh >> 1))
        for piece in range(2):
            @pl.when((s & 1) == piece)
            def _(piece=piece):
                p_ref[...] = _dot(h_ref[rows, :], wg_v[js, :, piece * half_c:(piece + 1) * half_c])

        @at_step(n_steps - 1, nt - 1)
        def _():
            for cp in w_direct + cw_direct + w_passed:
                cp.wait_send()
            for st in stores:
                st.wait()

    def x_map(s, t, kr):
        return (jnp.where(s == 0, t, nt - 1), 0)

    def p_map(s, t, kr):
        sh = s >> 1
        return (t, 2 * (kr[0] ^ (((sh & 1) << 1) | (sh >> 1))) + (s & 1))

    hbm = pl.BlockSpec(memory_space=pl.ANY)
    grid_spec = pltpu.PrefetchScalarGridSpec(
        num_scalar_prefetch=1, grid=(n_steps, nt),
        in_specs=[pl.BlockSpec((TG, D_MODEL), x_map),
                  pl.BlockSpec((1, D_MODEL), lambda s, t, kr: (0, 0)),
                  pl.BlockSpec((1, D_MODEL, SHARD_COLS), lambda s, t, kr: (0, 0, 0)),
                  pl.BlockSpec((1, 3 * HEAD), lambda s, t, kr: (0, 0))],
        out_specs=(hbm, pl.BlockSpec((TG, half_c), p_map), hbm, hbm),
        scratch_shapes=[pltpu.VMEM((N_SHARD, D_MODEL, SHARD_COLS), BF16),
                        pltpu.VMEM((8, D_CONV), F32), pltpu.VMEM((SEQ, D_MODEL), BF16),
                        pltpu.SemaphoreType.DMA((N_SEM,)), pltpu.SemaphoreType.DMA((N_SEM,)),
                        pltpu.SemaphoreType.DMA((6,))])
    return pl.pallas_call(
        body, name="gather_proj", grid_spec=grid_spec,
        out_shape=(jax.ShapeDtypeStruct((SEQ, D_MODEL), BF16),
                   jax.ShapeDtypeStruct((SEQ, N_SHARD * SHARD_COLS), F32),
                   jax.ShapeDtypeStruct((N_SHARD, D_MODEL, SHARD_COLS), BF16),
                   jax.ShapeDtypeStruct((8, D_CONV), F32)),
        compiler_params=pltpu.CompilerParams(dimension_semantics=("arbitrary", "arbitrary"),
                                             vmem_limit_bytes=VMEM_LIMIT, collective_id=COLLECTIVE_GATHER),
    )(kidx, x2d, g1, w_in, conv_w)


LAG = 6


def _mix_out(proj, lb_logits, cw, ga, gcn, g64, w_out, x2d, gf, tgt):
    half_o = WO_ROWS // 2
    nblk = SEQ // TB
    n_steps = nblk + LAG

    def body(p_ref, lbl_ref, cw_ref, ga_ref, gcn_ref, g64_ref, wo_ref, x_ref, gf_ref, t_ref,
             aux_ref, sto_ref, dx2_ref, dm_ref, gwo_ref, part_ref,
             st_ref, tail_ref, wog_v, stage, ring, acc_ref, send_sems, recv_sems):
        i = pl.program_id(0)
        x, y, c = lax.axis_index("x"), lax.axis_index("y"), lax.axis_index("c")
        k = 2 * x + y
        sibling = (x, y, 1 - c)
        chips = [(1 - x, y), (x, 1 - y), (1 - x, 1 - y)]
        kjs = [2 * cx + cy for cx, cy in chips]

        def wo_half(kk, cc):
            return wog_v.at[pl.ds(pl.multiple_of(kk * WO_ROWS + cc * half_o, half_o), half_o), :]

        def copy(sem, ref, to):
            return pltpu.make_async_remote_copy(
                src_ref=ref, dst_ref=ref, send_sem=send_sems.at[sem], recv_sem=recv_sems.at[sem],
                device_id=to, device_id_type=MESH)

        wo_direct = [copy(j, wo_half(k, c), (*chip, c)) for j, chip in enumerate(chips)]
        wo_passed = [copy(3 + j, wo_half(kj, c), sibling) for j, kj in enumerate(kjs)]

        @pl.when(i == 0)
        def _():
            barrier = pltpu.get_barrier_semaphore()
            for peer in [sibling] + [(*chip, c) for chip in chips]:
                pl.semaphore_signal(barrier, inc=1, device_id=peer, device_id_type=MESH)
            st_ref[...] = jnp.zeros_like(st_ref)
            tail_ref[...] = jnp.zeros_like(tail_ref)
            acc_ref[...] = jnp.zeros_like(acc_ref)
            part_ref[...] = jnp.zeros_like(part_ref)
            wog_v[pl.ds(pl.multiple_of(k * WO_ROWS, WO_ROWS), WO_ROWS), :] = wo_ref[0].astype(BF16)

        @pl.when(i == LAG - 1)
        def _():
            for j in range(3):
                copy(j, wo_half(kjs[j], c), sibling).wait_recv()
                wo_passed[j].start()

        @pl.when(i == LAG)
        def _():
            for j in range(3):
                copy(3 + j, wo_half(kjs[j], 1 - c), sibling).wait_recv()

        lb = _lower_bound(lbl_ref[...])
        tri = _tri(True)
        causal = _causal()
        g64m = g64_ref[...]
        heads = range(N_HEADS)
        cs = [slice(hd * HEAD, (hd + 1) * HEAD) for hd in heads]
        col = lambda base, hd: slice(base + hd * HEAD, base + (hd + 1) * HEAD)

        def mix_chunk(n):
            sl = pl.ds(n * CHUNK, CHUNK)
            sg = [_sigmoid(p_ref[sl, col(512, hd)]) for hd in heads]
            f = [lb[:, cs[hd]] + (1.0 - lb[:, cs[hd]]) * sg[hd] for hd in heads]
            bc = _exact_left_many(tri, [jnp.log(f[hd]) for hd in heads])
            for hd in heads:
                aux_ref[sl, col(AUX_B, hd)] = bc[hd]
            g = [bc[hd][CHUNK - 1:CHUNK, :] for hd in heads]
            qd = [(p_ref[sl, col(0, hd)] * jnp.exp(bc[hd])).astype(BF16) for hd in heads]
            kk = [1.0 - f[hd] for hd in heads]
            ki = [(kk[hd] * jnp.exp(-bc[hd])).astype(BF16) for hd in heads]
            ke = [(kk[hd] * jnp.exp(g[hd] - bc[hd])).astype(BF16) for hd in heads]
            vb = [p_ref[sl, col(1024, hd)].astype(BF16) for hd in heads]
            st = [st_ref[hd] for hd in heads]
            st_b = [a.astype(BF16) for a in st]
            for hd in heads:
                sto_ref[n, hd] = st_b[hd]
            scm = [_dot_nt(qd[hd], ki[hd]) for hd in heads]
            inter = [_dot_nt(qd[hd], st_b[hd]) for hd in heads]
            upd = [_dot_tn(vb[hd], ke[hd]) for hd in heads]
            intra = [_dot(jnp.where(causal, scm[hd], 0.0).astype(BF16), vb[hd]) for hd in heads]
            for hd in heads:
                st_ref[hd] = st[hd] * jnp.exp(g[hd]) + upd[hd]
                o = intra[hd] + inter[hd]
                aux_ref[sl, col(AUX_O, hd)] = o
                ra = lax.rsqrt(jnp.mean(o * o, axis=-1, keepdims=True) + EPS)
                za = p_ref[sl, col(1536, hd)]
                stage[sl, cs[hd]] = (o * ra * ga_ref[:, cs[hd]] * (za * _sigmoid(za))).astype(BF16)
            yb = []
            for hd in heads:
                cu = p_ref[sl, col(3072, hd)] * p_ref[sl, col(2048, hd)]
                tail = tail_ref[:, cs[hd]]
                cv = (cw_ref[0:1, cs[hd]] * _shift_down(cu, 2, tail) + cw_ref[1:2, cs[hd]] * _shift_down(cu, 1, tail)
                      + cw_ref[2:3, cs[hd]] * cu)
                tail_ref[:, cs[hd]] = cu[CHUNK - 8:, :]
                aux_ref[sl, col(AUX_CV, hd)] = cv
                yb.append(p_ref[sl, col(2560, hd)] * cv)
            ms = _group_mean_many([y * y for y in yb], g64m)
            for hd in heads:
                rb = lax.rsqrt(ms[hd] + EPS)
                zb = p_ref[sl, col(3584, hd)]
                stage[sl, col(512, hd)] = (yb[hd] * rb * gcn_ref[:, cs[hd]] * (zb * _sigmoid(zb))).astype(BF16)

        def step(mix, project, entry=False):
            if project:
                mixed_b = ring[pl.ds(pl.multiple_of((i - LAG) * TB, TB), TB), :]
                y = _dot(mixed_b, wog_v[...])
            if mix:
                mix_chunk(0)
            if project:
                x2 = x_ref[...] + y
                r2 = lax.rsqrt(jnp.mean(x2 * x2, axis=-1, keepdims=True) + EPS)
                n2 = x2 * r2
                gfv = gf_ref[...]
                err = n2 * gfv - t_ref[...]
                loss = 0.5 * jnp.sum(jnp.mean(err * err, axis=-1, keepdims=True), axis=0, keepdims=True)
                dy = err * (1.0 / D_MODEL)
                part_ref[1:2, :] += jnp.sum(dy * n2, axis=0, keepdims=True)
                part_ref[7:8, :] += jnp.broadcast_to(loss, (1, D_MODEL))
                dn = dy * gfv
                dx2 = r2 * (dn - n2 * jnp.mean(dn * n2, axis=-1, keepdims=True))
                dx2_ref[...] = dx2
                dx2_b = dx2.astype(BF16)
            if mix:
                mix_chunk(1)
            if entry:
                pl.semaphore_wait(pltpu.get_barrier_semaphore(), 4)
                for cp in wo_direct:
                    cp.start()
            if project:
                dm_ref[...] = _dot_nt(dx2_b, wog_v[...])
            if mix:
                mix_chunk(2)
            if project:
                acc_ref[...] += _dot_tn(mixed_b, dx2_b)
            if mix:
                mix_chunk(3)
                ring[pl.ds(pl.multiple_of(i * TB, TB), TB), :] = stage[...]

        @pl.when(i == 0)
        def _():
            step(True, False, entry=True)

        @pl.when((i > 0) & (i < LAG))
        def _():
            step(True, False)

        @pl.when((i >= LAG) & (i < nblk))
        def _():
            step(True, True)

        @pl.when(i >= nblk)
        def _():
            step(False, True)

        @pl.when(i == n_steps - 1)
        def _():
            gwo_ref[...] = acc_ref[...].astype(BF16)
            for cp in wo_direct + wo_passed:
                cp.wait_send()

    assert NCB == 4
    row = lambda w: pl.BlockSpec((1, w), lambda i: (0, 0))
    mix_blk = lambda i: jnp.minimum(i, nblk - 1)
    out_blk = lambda i: jnp.clip(i - LAG, 0, nblk - 1)
    tok = lambda: pl.BlockSpec((TB, D_MODEL), lambda i: (out_blk(i), 0))
    return pl.pallas_call(
        body, name="mix_out", grid=(n_steps,),
        out_shape=(jax.ShapeDtypeStruct((SEQ, AUX_COLS), F32),
                   jax.ShapeDtypeStruct((N_CHUNKS, N_HEADS, HEAD, HEAD), BF16),
                   jax.ShapeDtypeStruct((SEQ, D_MODEL), F32),
                   jax.ShapeDtypeStruct((SEQ, D_MODEL), F32),
                   jax.ShapeDtypeStruct((D_MODEL, D_MODEL), BF16),
                   jax.ShapeDtypeStruct((8, D_MODEL), F32)),
        in_specs=[pl.BlockSpec((TB, 4096), lambda i: (jnp.minimum(i, nblk - 1), 0)),
                  pl.BlockSpec((2, D_HGRN), lambda i: (0, 0)),
                  pl.BlockSpec((8, D_CONV), lambda i: (0, 0)),
                  row(D_HGRN), row(D_CONV),
                  pl.BlockSpec((HEAD, HEAD), lambda i: (0, 0)),
                  pl.BlockSpec((1, WO_ROWS, D_MODEL), lambda i: (0, 0, 0)),
                  tok(), row(D_MODEL), tok()],
        out_specs=(pl.BlockSpec((TB, AUX_COLS), lambda i: (mix_blk(i), 0)),
                   pl.BlockSpec((NCB, N_HEADS, HEAD, HEAD), lambda i: (mix_blk(i), 0, 0, 0)),
                   tok(), tok(),
                   pl.BlockSpec((D_MODEL, D_MODEL), lambda i: (0, 0)),
                   pl.BlockSpec((8, D_MODEL), lambda i: (0, 0))),
        scratch_shapes=[pltpu.VMEM((N_HEADS, HEAD, HEAD), F32), pltpu.VMEM((8, D_CONV), F32),
                        pltpu.VMEM((D_MODEL, D_MODEL), BF16), pltpu.VMEM((TB, D_MODEL), BF16),
                        pltpu.VMEM((SEQ, D_MODEL), BF16), pltpu.VMEM((D_MODEL, D_MODEL), F32),
                        pltpu.SemaphoreType.DMA((6,)), pltpu.SemaphoreType.DMA((6,))],
        compiler_params=pltpu.CompilerParams(dimension_semantics=("arbitrary",), vmem_limit_bytes=VMEM_LIMIT,
                                             collective_id=COLLECTIVE_MIX_OUT),
    )(proj, lb_logits, cw, ga, gcn, g64, w_out, x2d, gf, tgt)


def _mix_bwd(proj, aux, states, dmixed, lb_logits, cw, ga, gcn, g64):
    nblk = SEQ // TB

    def body(p_ref, aux_ref, st_ref, dm_ref, lbl_ref, cw_ref, ga_ref, gcn_ref, g64_ref,
             dp_ref, part_ref, dst_ref, head_ref, dlb_ref):
        i = pl.program_id(0)

        @pl.when(i == 0)
        def _():
            dst_ref[...] = jnp.zeros_like(dst_ref)
            head_ref[...] = jnp.zeros_like(head_ref)
            part_ref[...] = jnp.zeros_like(part_ref)
            dlb_ref[...] = jnp.zeros_like(dlb_ref)

        lb = _lower_bound(lbl_ref[...])
        triu = _tri(False)
        causal = _causal()
        g64m = g64_ref[...]
        rowsum = lambda a: jnp.sum(a, axis=0, keepdims=True)
        heads = range(N_HEADS)
        cs = [slice(hd * HEAD, (hd + 1) * HEAD) for hd in heads]
        col = lambda base, hd: slice(base + hd * HEAD, base + (hd + 1) * HEAD)
        for n in reversed(range(NCB)):
            sl = pl.ds(n * CHUNK, CHUNK)
            cvv = [aux_ref[sl, col(AUX_CV, hd)] for hd in heads]
            gb = [p_ref[sl, col(2560, hd)] for hd in heads]
            yb = [gb[hd] * cvv[hd] for hd in heads]
            ms = _group_mean_many([y * y for y in yb], g64m)
            rb, nb, dnb = [], [], []
            for hd in heads:
                rb.append(lax.rsqrt(ms[hd] + EPS))
                nb.append(yb[hd] * rb[hd])
                zb = p_ref[sl, col(3584, hd)]
                sgb = _sigmoid(zb)
                dmb = dm_ref[sl, col(512, hd)]
                silu = zb * sgb
                dgate = dmb * gcn_ref[:, cs[hd]]
                part_ref[2:3, col(512, hd)] += rowsum(dmb * nb[hd] * silu)
                dp_ref[sl, col(3584, hd)] = (dgate * nb[hd] * (sgb + silu * (1.0 - sgb))).astype(BF16)
                dnb.append(dgate * silu)
            mdn = _group_mean_many([dnb[hd] * nb[hd] for hd in heads], g64m)
            for hd in heads:
                dyb = rb[hd] * (dnb[hd] - nb[hd] * mdn[hd])
                dp_ref[sl, col(2560, hd)] = (dyb * cvv[hd]).astype(BF16)
                dcv = dyb * gb[hd]
                head = head_ref[:, cs[hd]]
                dcv1 = _shift_up(dcv, 1, head)
                dcv2 = _shift_up(dcv, 2, head)
                head_ref[:, cs[hd]] = dcv[0:8, :]
                u = p_ref[sl, col(2048, hd)]
                gc = p_ref[sl, col(3072, hd)]
                cu = gc * u
                part_ref[4:5, cs[hd]] += rowsum(dcv2 * cu)
                part_ref[5:6, cs[hd]] += rowsum(dcv1 * cu)
                part_ref[6:7, cs[hd]] += rowsum(dcv * cu)
                dcu = cw_ref[2:3, cs[hd]] * dcv + cw_ref[1:2, cs[hd]] * dcv1 + cw_ref[0:1, cs[hd]] * dcv2
                dp_ref[sl, col(3072, hd)] = (dcu * u).astype(BF16)
                dp_ref[sl, col(2048, hd)] = (dcu * gc).astype(BF16)
            do_b = []
            for hd in heads:
                ov = aux_ref[sl, col(AUX_O, hd)]
                ra = lax.rsqrt(jnp.mean(ov * ov, axis=-1, keepdims=True) + EPS)
                na = ov * ra
                za = p_ref[sl, col(1536, hd)]
                sga = _sigmoid(za)
                dma = dm_ref[sl, cs[hd]]
                silu = za * sga
                dgate = dma * ga_ref[:, cs[hd]]
                part_ref[2:3, cs[hd]] += rowsum(dma * na * silu)
                dp_ref[sl, col(1536, hd)] = (dgate * na * (sga + silu * (1.0 - sga))).astype(BF16)
                dna = dgate * silu
                do_b.append((ra * (dna - na * jnp.mean(dna * na, axis=-1, keepdims=True))).astype(BF16))
            s = [_sigmoid(p_ref[sl, col(512, hd)]) for hd in heads]
            f = [lb[:, cs[hd]] + (1.0 - lb[:, cs[hd]]) * s[hd] for hd in heads]
            bc = [aux_ref[sl, col(AUX_B, hd)] for hd in heads]
            g = [bc[hd][CHUNK - 1:CHUNK, :] for hd in heads]
            eb = [jnp.exp(bc[hd]) for hd in heads]
            enb = [jnp.exp(-bc[hd]) for hd in heads]
            eg = [jnp.exp(g[hd] - bc[hd]) for hd in heads]
            dec = [jnp.exp(g[hd]) for hd in heads]
            qd = [p_ref[sl, cs[hd]] * eb[hd] for hd in heads]
            kk = [1.0 - f[hd] for hd in heads]
            ki = [kk[hd] * enb[hd] for hd in heads]
            ke = [kk[hd] * eg[hd] for hd in heads]
            qd_b = [a.astype(BF16) for a in qd]
            ki_b = [a.astype(BF16) for a in ki]
            ke_b = [a.astype(BF16) for a in ke]
            vb = [p_ref[sl, col(1024, hd)].astype(BF16) for hd in heads]
            st_b = [st_ref[n, hd] for hd in heads]
            dst = [dst_ref[hd] for hd in heads]
            dst_b = [a.astype(BF16) for a in dst]
            scm = [_dot_nt(qd_b[hd], ki_b[hd]) for hd in heads]
            amm = [_dot_nt(do_b[hd], vb[hd]) for hd in heads]
            dqd2 = [_dot(do_b[hd], st_b[hd]) for hd in heads]
            dke = [_dot(vb[hd], dst_b[hd]) for hd in heads]
            dv2 = [_dot_nt(ke_b[hd], dst_b[hd]) for hd in heads]
            dsu = [_dot_tn(do_b[hd], qd_b[hd]) for hd in heads]
            sc = [jnp.where(causal, scm[hd], 0.0).astype(BF16) for hd in heads]
            am = [jnp.where(causal, amm[hd], 0.0).astype(BF16) for hd in heads]
            dqd1 = [_dot(am[hd], ki_b[hd]) for hd in heads]
            dki = [_dot_tn(am[hd], qd_b[hd]) for hd in heads]
            dv1 = [_dot_tn(sc[hd], do_b[hd]) for hd in heads]
            db, dgv, dkk = [], [], []
            for hd in heads:
                dqd = dqd1[hd] + dqd2[hd]
                ddec = rowsum(dst[hd] * st_b[hd].astype(F32))
                dst_ref[hd] = dst[hd] * dec[hd] + dsu[hd]
                dp_ref[sl, cs[hd]] = (dqd * eb[hd]).astype(BF16)
                dp_ref[sl, col(1024, hd)] = (dv1[hd] + dv2[hd]).astype(BF16)
                dke_eg = dke[hd] * eg[hd]
                dkk.append(dki[hd] * enb[hd] + dke_eg)
                db.append(dqd * qd[hd] - kk[hd] * dkk[hd])
                dgv.append(rowsum(kk[hd] * dke_eg) + ddec * dec[hd])
            rc = _exact_left_many(triu, db, 2)
            for hd in heads:
                df = (rc[hd] + dgv[hd]) / f[hd] - dkk[hd]
                one_s = 1.0 - s[hd]
                dlb_ref[:, cs[hd]] += rowsum(df * one_s)
                dp_ref[sl, col(512, hd)] = (df * (1.0 - lb[:, cs[hd]]) * s[hd] * one_s).astype(BF16)

        @pl.when(i == nblk - 1)
        def _():
            row = dlb_ref[...] * lb * (1.0 - lb)
            part_ref[3:4, 0:D_HGRN] = row
            part_ref[3:4, D_HGRN:] = -row

    rev = lambda w: pl.BlockSpec((TB, w), lambda i: (nblk - 1 - i, 0))
    row = lambda w: pl.BlockSpec((1, w), lambda i: (0, 0))
    return pl.pallas_call(
        body, name="mix_bwd", grid=(nblk,),
        out_shape=(jax.ShapeDtypeStruct((SEQ, 4096), BF16),
                   jax.ShapeDtypeStruct((8, D_MODEL), F32)),
        in_specs=[rev(4096), rev(AUX_COLS),
                  pl.BlockSpec((NCB, N_HEADS, HEAD, HEAD), lambda i: (nblk - 1 - i, 0, 0, 0)),
                  rev(D_MODEL),
                  pl.BlockSpec((2, D_HGRN), lambda i: (0, 0)),
                  pl.BlockSpec((8, D_CONV), lambda i: (0, 0)),
                  row(D_HGRN), row(D_CONV),
                  pl.BlockSpec((HEAD, HEAD), lambda i: (0, 0))],
        out_specs=(rev(4096), pl.BlockSpec((8, D_MODEL), lambda i: (0, 0))),
        scratch_shapes=[pltpu.VMEM((N_HEADS, HEAD, HEAD), F32), pltpu.VMEM((8, D_CONV), F32),
                        pltpu.VMEM((1, D_HGRN), F32)],
        compiler_params=pltpu.CompilerParams(dimension_semantics=("arbitrary",), vmem_limit_bytes=VMEM_LIMIT),
    )(proj, aux, states, dmixed, lb_logits, cw, ga, gcn, g64)


TT = 1024
TX = 512
(SEM_D2D, SEM_D2D_O, SEM_ICI, SEM_ICI_O, SEM_FIN, SEM_FIN_O, SEM_SMALL, SEM_VIA, SEM_NORM, N_SEM_TAIL) = (
    0, 4, 5, 8, 11, 12, 12, 20, 22, 30)


def _bwd_tail(kidx, h, dproj, wg, gwo, x2d, dx2, g1, small_a, small_b):
    hw = D_MODEL // 2
    ho = WO_ROWS // 2
    nt = SEQ // TT
    norm_step = 2 * N_SHARD
    n_steps = norm_step + SEQ // TX // nt

    def body(k_ref, h_ref, dp_ref, w_ref, gwo_ref, x_ref, dx2_ref, g_ref, sm_ref, smb_ref,
             gx_ref, gw_out, gwo_out, osm_ref,
             acc, dh, sendbuf, keep, sibrcv, rcv, merge, sib_o, p_o, rcv_o, res_o, sm_buf, dng_buf, dng,
             send_sems, recv_sems, out_sems):
        s, t = pl.program_id(0), pl.program_id(1)
        x, y, c = lax.axis_index("x"), lax.axis_index("y"), lax.axis_index("c")
        k = 2 * x + y
        me = 4 * x + 2 * y + c
        sibling = (x, y, 1 - c)
        chips = [(1 - x, 1 - y), (1 - x, y), (x, 1 - y)]
        kjs = [2 * cx + cy for cx, cy in chips]
        mine = pl.ds(pl.multiple_of(c * hw, hw), hw)
        other = pl.ds(pl.multiple_of((1 - c) * hw, hw), hw)
        mine_o = pl.ds(pl.multiple_of(c * ho, ho), ho)
        other_o = pl.ds(pl.multiple_of((1 - c) * ho, ho), ho)

        def copy(sem, src, dst, to):
            return pltpu.make_async_remote_copy(
                src_ref=src, dst_ref=dst, send_sem=send_sems.at[sem], recv_sem=recv_sems.at[sem],
                device_id=to, device_id_type=MESH)

        def at_step(sv, tv):
            return pl.when((s == sv) & (t == tv))

        def at_norm_block(b):
            return at_step(norm_step + b // nt, b % nt)

        d2d = [copy(SEM_D2D + sv, sendbuf.at[sv], sibrcv.at[sv], sibling) for sv in range(N_SHARD)]
        d2d_o = copy(SEM_D2D_O, gwo_ref.at[:, other_o, :], sib_o, sibling)
        ici = {sv: copy(SEM_ICI + sv, keep.at[sv], rcv.at[sv - 1], (*chips[sv], c)) for sv in (1, 2)}
        qh = hw // 2
        via = [copy(SEM_VIA, keep.at[0, 0:qh, :], merge.at[1], (*chips[1], c)),
               copy(SEM_VIA + 1, keep.at[0, qh:hw, :], merge.at[0], (*chips[2], c))]
        merged_rows = [slice(qh, hw), slice(0, qh)]
        ici_o = [copy(SEM_ICI_O + sv, p_o.at[kjs[sv]], rcv_o.at[sv], (*chips[sv], c)) for sv in range(3)]
        fin = copy(SEM_FIN, acc.at[mine, :], gw_out.at[mine, :], sibling)
        fin_o = copy(SEM_FIN_O, res_o.at[mine_o, :], res_o.at[mine_o, :], sibling)
        peers = [(x ^ (m >> 2), y ^ ((m >> 1) & 1), c ^ (m & 1)) for m in range(1, N_DEV)]
        smalls = [copy(SEM_SMALL + 1 + j, sm_buf.at[me], sm_buf.at[me], to) for j, to in enumerate(peers)]
        dngs = [copy(SEM_NORM + 1 + j, dng_buf.at[me], dng_buf.at[me], to) for j, to in enumerate(peers)]
        store_w = pltpu.make_async_copy(acc.at[mine, :], gw_out.at[mine, :], out_sems.at[0])
        store_o = pltpu.make_async_copy(res_o, gwo_out, out_sems.at[1])

        @at_step(0, 0)
        def _():
            barrier = pltpu.get_barrier_semaphore()
            for to in peers:
                pl.semaphore_signal(barrier, inc=1, device_id=to, device_id_type=MESH)
            sm_buf[me] = sm_ref[...] + smb_ref[...]
            pl.semaphore_wait(barrier, N_DEV - 1)
            d2d_o.start()
            for cp in smalls:
                cp.start()

        @at_step(0, 1)
        def _():
            d2d_o.wait_recv()
            for j in range(N_SHARD):
                p_o[j] = (gwo_ref[j, mine_o, :].astype(F32) + sib_o[j].astype(F32)).astype(BF16)
            res_o[mine_o, :] = gwo_ref[k, mine_o, :].astype(F32) + sib_o[k].astype(F32)
            for cp in ici_o:
                cp.start()

        rows = pl.ds(pl.multiple_of(t * TT, TT), TT)

        @pl.when((s < N_SHARD) & (t == 0))
        def _():
            acc[...] = _dot_tn(h_ref[...], dp_ref[...])

        @pl.when((s < N_SHARD) & (t > 0))
        def _():
            acc[...] += _dot_tn(h_ref[...], dp_ref[...])

        for sv in range(N_SHARD):
            @at_step(sv, nt - 1)
            def _(sv=sv):
                sendbuf[sv] = acc[other, :].astype(BF16)
                if sv < 3:
                    keep[sv] = acc[mine, :].astype(BF16)
                d2d[sv].start()

        @at_step(1, 0)
        def _():
            d2d[0].wait_recv()
            keep[0] = (keep[0].astype(F32) + sibrcv[0].astype(F32)).astype(BF16)
            for cp in via:
                cp.start()

        for sv in (1, 2):
            @at_step(sv + 2, 0)
            def _(sv=sv):
                d2d[sv].wait_recv()
                keep[sv] = (keep[sv].astype(F32) + sibrcv[sv].astype(F32)).astype(BF16)
                via[2 - sv].wait_recv()
                rows_m = merged_rows[sv - 1]
                keep[sv, rows_m, :] = (keep[sv, rows_m, :].astype(F32) + merge[sv - 1].astype(F32)).astype(BF16)
                ici[sv].start()

        @pl.when(s == N_SHARD)
        def _():
            dh[rows, :] = _dot_nt(dp_ref[...], w_ref[0])

        @pl.when((s > N_SHARD) & (s < norm_step))
        def _():
            dh[rows, :] += _dot_nt(dp_ref[...], w_ref[0])

        @at_norm_block(0)
        def _():
            d2d[3].wait_recv()
            acc[mine, :] += sibrcv[3].astype(F32)

        @at_norm_block(1)
        def _():
            tot = res_o[mine_o, :]
            for sv in range(3):
                ici_o[sv].wait_recv()
                tot = tot + rcv_o[sv].astype(F32)
            res_o[mine_o, :] = tot
            fin_o.start()

        @at_norm_block(2)
        def _():
            ici[1].wait_recv()
            acc[mine, :] += rcv[0].astype(F32)

        @at_norm_block(SEQ // TX - 2)
        def _():
            ici[2].wait_recv()
            acc[mine, :] += rcv[1].astype(F32)
            fin.start()
            store_w.start()
            fin_o.wait_recv()
            store_o.start()

        @at_norm_block(0)
        def _():
            dng[...] = jnp.zeros_like(dng)

        @pl.when(s >= norm_step)
        def _():
            blk = (s - norm_step) * nt + t
            dhv = dh[pl.ds(pl.multiple_of(blk * TX, TX), TX), :]
            xv = x_ref[...]
            r = lax.rsqrt(jnp.mean(xv * xv, axis=-1, keepdims=True) + EPS)
            xn = xv * r
            dng[...] += jnp.sum(dhv * xn, axis=0, keepdims=True)
            dxn = dhv * g_ref[...]
            gx_ref[...] = dx2_ref[...] + r * (dxn - xn * jnp.mean(dxn * xn, axis=-1, keepdims=True))

        @at_step(n_steps - 1, nt - 1)
        def _():
            dng_buf[me] = dng[...]
            for cp in dngs:
                cp.start()
            for m in range(1, N_DEV):
                copy(SEM_SMALL + m, sm_buf.at[0], sm_buf.at[0], sibling).wait_recv()
            tot = sm_buf[0]
            for d in range(1, N_DEV):
                tot = tot + sm_buf[d]
            osm_ref[...] = tot
            for m in range(1, N_DEV):
                copy(SEM_NORM + m, dng_buf.at[0], dng_buf.at[0], sibling).wait_recv()
            tot = dng_buf[0]
            for d in range(1, N_DEV):
                tot = tot + dng_buf[d]
            osm_ref[0:1, :] = tot
            fin.wait_recv()
            for cp in d2d + [d2d_o] + via + list(ici.values()) + ici_o + [fin, fin_o] + smalls + dngs:
                cp.wait_send()
            store_o.wait()
            store_w.wait()

    def shard_of(s, kr):
        order = jnp.where(s < N_SHARD, s, jnp.where(s < norm_step, s - N_SHARD, 3))
        return kr[0] ^ (3 - order)

    def h_map(s, t, kr):
        return (jnp.where(s < N_SHARD, t, nt - 1), 0)

    def dp_map(s, t, kr):
        return (jnp.where(s < norm_step, t, nt - 1), shard_of(s, kr))

    def w_map(s, t, kr):
        return (shard_of(jnp.maximum(s, N_SHARD), kr), 0, 0)

    def blk_map(s, t, kr):
        return (jnp.where(s < norm_step, 0, (s - norm_step) * nt + t), 0)

    hbm = pl.BlockSpec(memory_space=pl.ANY)
    grid_spec = pltpu.PrefetchScalarGridSpec(
        num_scalar_prefetch=1, grid=(n_steps, nt),
        in_specs=[pl.BlockSpec((TT, D_MODEL), h_map),
                  pl.BlockSpec((TT, SHARD_COLS), dp_map),
                  pl.BlockSpec((1, D_MODEL, SHARD_COLS), w_map),
                  pl.BlockSpec((N_SHARD, WO_ROWS, D_MODEL), lambda s, t, kr: (0, 0, 0),
                               pipeline_mode=pl.Buffered(1)),
                  pl.BlockSpec((TX, D_MODEL), blk_map),
                  pl.BlockSpec((TX, D_MODEL), blk_map),
                  pl.BlockSpec((1, D_MODEL), lambda s, t, kr: (0, 0)),
                  pl.BlockSpec((8, D_MODEL), lambda s, t, kr: (0, 0)),
                  pl.BlockSpec((8, D_MODEL), lambda s, t, kr: (0, 0))],
        out_specs=(pl.BlockSpec((TX, D_MODEL), blk_map), hbm, hbm,
                   pl.BlockSpec((8, D_MODEL), lambda s, t, kr: (0, 0))),
        scratch_shapes=[pltpu.VMEM((D_MODEL, SHARD_COLS), F32), pltpu.VMEM((SEQ, D_MODEL), F32),
                        pltpu.VMEM((N_SHARD, hw, SHARD_COLS), BF16), pltpu.VMEM((3, hw, SHARD_COLS), BF16),
                        pltpu.VMEM((N_SHARD, hw, SHARD_COLS), BF16), pltpu.VMEM((2, hw, SHARD_COLS), BF16),
                        pltpu.VMEM((2, hw // 2, SHARD_COLS), BF16),
                        pltpu.VMEM((N_SHARD, ho, D_MODEL), BF16), pltpu.VMEM((N_SHARD, ho, D_MODEL), BF16),
                        pltpu.VMEM((3, ho, D_MODEL), BF16), pltpu.VMEM((WO_ROWS, D_MODEL), F32),
                        pltpu.VMEM((N_DEV, 8, D_MODEL), F32), pltpu.VMEM((N_DEV, 1, D_MODEL), F32),
                        pltpu.VMEM((1, D_MODEL), F32),
                        pltpu.SemaphoreType.DMA((N_SEM_TAIL,)), pltpu.SemaphoreType.DMA((N_SEM_TAIL,)),
                        pltpu.SemaphoreType.DMA((2,))])
    return pl.pallas_call(
        body, name="bwd_tail", grid_spec=grid_spec,
        out_shape=(jax.ShapeDtypeStruct((SEQ, D_MODEL), F32),
                   jax.ShapeDtypeStruct((D_MODEL, SHARD_COLS), F32),
                   jax.ShapeDtypeStruct((WO_ROWS, D_MODEL), F32),
                   jax.ShapeDtypeStruct((8, D_MODEL), F32)),
        compiler_params=pltpu.CompilerParams(dimension_semantics=("arbitrary", "arbitrary"),
                                             vmem_limit_bytes=61 * 1024 * 1024, collective_id=COLLECTIVE_TAIL),
    )(kidx, h, dproj, wg, gwo, x2d, dx2, g1, small_a, small_b)


def _adam_update(w, g, m, v):
    nm = ADAM_B1 * m + (1.0 - ADAM_B1) * g
    nv = ADAM_B2 * v + (1.0 - ADAM_B2) * (g * g)
    m_hat = nm / (1.0 - ADAM_B1 ** ADAM_STEP)
    v_hat = nv / (1.0 - ADAM_B2 ** ADAM_STEP)
    return -ADAM_LR * (m_hat / (jnp.sqrt(v_hat) + ADAM_EPS) + ADAM_WD * w), nm, nv


def _adamw_all(tot, g_w_in, g_w_out, big, small, grad_x):
    n = len(small)
    rows = WO_ROWS
    steps = D_MODEL // rows

    def body(tot_ref, *refs):
        gx_ref, gx_out, gx_sems = refs[2 + 3 * (2 + n)], refs[-2], refs[-1]
        pieces = [pl.ds(j * (SEQ // steps), SEQ // steps) for j in range(steps)]
        gx_copies = [pltpu.make_async_copy(gx_ref.at[rows, :], gx_out.at[rows, :], gx_sems.at[j])
                     for j, rows in enumerate(pieces)]

        @pl.when(pl.program_id(0) == 0)
        def _():
            for cp in gx_copies:
                cp.start()

        @pl.when(pl.program_id(0) == steps - 1)
        def _():
            for cp in gx_copies:
                cp.wait()

        ins, outs = refs[:2 + 3 * (2 + n)], refs[3 + 3 * (2 + n):-2]
        g_refs, wmv = ins[:2], ins[2:]
        loss_ref, quads = outs[0], outs[1:]

        def update(j, g):
            w_ref, m_ref, v_ref = wmv[3 * j:3 * j + 3]
            g_ref, d_ref, nm_ref, nv_ref = quads[4 * j:4 * j + 4]
            g_ref[...] = g
            d_ref[...], nm_ref[...], nv_ref[...] = _adam_update(w_ref[...], g, m_ref[...], v_ref[...])

        update(0, g_refs[0][...])

        @pl.when(pl.program_id(0) == 0)
        def _():
            update(1, g_refs[1][...])
            k = 2 * lax.axis_index("x") + lax.axis_index("y")
            mine = pl.ds(pl.multiple_of(k * HEAD, HEAD), HEAD)
            loss_ref[...] = tot_ref[7:8, 0:1]
            grads = [tot_ref[0:1, :], tot_ref[1:2, :], tot_ref[2:3, 0:D_HGRN], tot_ref[2:3, D_HGRN:],
                     jnp.concatenate([tot_ref[3:4, 0:D_HGRN], tot_ref[3:4, D_HGRN:]], axis=0),
                     jnp.concatenate([tot_ref[4 + tap:5 + tap, mine] for tap in range(3)], axis=1)]
            for j, g in enumerate(grads):
                update(2 + j, g)

    whole = lambda a: pl.BlockSpec(a.shape, lambda i: (0, 0))
    blk = pl.BlockSpec((rows, SHARD_COLS), lambda i: (i, 0))
    arrays = [a for triple in big + small for a in triple]
    in_specs = ([whole(tot), blk, whole(g_w_out)] + [blk] * 3 + [whole(a) for a in arrays[3:]])
    shapes = [big[0][0], big[1][0]] + [w for w, _, _ in small]
    out_shape = (jax.ShapeDtypeStruct((1, 1), F32),) + tuple(
        jax.ShapeDtypeStruct(w.shape, F32) for w in shapes for _ in range(4))
    out_specs = (pl.BlockSpec((1, 1), lambda i: (0, 0)),) + (blk,) * 4 + tuple(
        whole(w) for w in shapes[1:] for _ in range(4))
    hbm = pl.BlockSpec(memory_space=pl.ANY)
    outs = pl.pallas_call(
        body, name="adamw_all", grid=(steps,),
        out_shape=out_shape + (jax.ShapeDtypeStruct(grad_x.shape, F32),),
        in_specs=in_specs + [hbm], out_specs=out_specs + (hbm,),
        scratch_shapes=[pltpu.SemaphoreType.DMA((steps,))],
        compiler_params=pltpu.CompilerParams(dimension_semantics=("arbitrary",), vmem_limit_bytes=VMEM_LIMIT),
    )(tot, g_w_in, g_w_out, *arrays, grad_x)
    return [outs[0]] + [outs[1 + 4 * j:5 + 4 * j] for j in range(2 + n)] + [outs[-1]]


def _local_step(x2d, tgt, proj, lb_logits, cw, ga, gcn, w_out, gf):
    g64 = _group_matrix(HEAD, CONV_GROUP)
    aux, states, dx2, dmixed, gwo, part_out = _mix_out(proj, lb_logits, cw, ga, gcn, g64, w_out, x2d, gf, tgt)
    dproj, part_mix = _mix_bwd(proj, aux, states, dmixed, lb_logits, cw, ga, gcn, g64)
    return dproj, dx2, gwo.reshape(N_SHARD, WO_ROWS, D_MODEL), part_out, part_mix


def kernel(x, norm_gain, w_in, lb_logits, conv_w, hgrn_norm_gain, conv_norm_gain, w_out, final_norm_gain, loss_target, m_norm_gain, m_w_in, m_lb_logits, m_conv_w, m_hgrn_norm_gain, m_conv_norm_gain, m_w_out, m_final_norm_gain, v_norm_gain, v_w_in, v_lb_logits, v_conv_w, v_hgrn_norm_gain, v_conv_norm_gain, v_w_out, v_final_norm_gain):
    k = 2 * lax.axis_index("x") + lax.axis_index("y")
    kidx = jnp.reshape(k, (1,)).astype(jnp.int32)
    row = lambda a: a.reshape(1, D_MODEL)
    taps = lambda a: a.reshape(1, 3 * HEAD)
    h, proj, wg, cw = _gather_proj(kidx, x[0], norm_gain, w_in, taps(conv_w))
    dproj, dx2, gwo, part_out, part_mix = _local_step(
        x[0], loss_target[0], proj, lb_logits, cw, hgrn_norm_gain, conv_norm_gain, w_out, row(final_norm_gain))
    rgrad_x, rg_w_in, rg_w_out, tot = _bwd_tail(kidx, h, dproj, wg, gwo, x[0], dx2, norm_gain, part_out, part_mix)

    (loss, (g_w_in, d_w_in, nm_w_in, nv_w_in), (g_w_out, d_w_out, nm_w_out, nv_w_out),
     (g_norm_gain, d_ng, nm_ng, nv_ng), (g_final, d_fg, nm_fg, nv_fg), (g_hgrn, d_hg, nm_hg, nv_hg),
     (g_convn, d_cg, nm_cg, nv_cg), (g_lb, d_lb, nm_lb, nv_lb), (g_conv_w, d_cw, nm_cw, nv_cw),
     grad_x) = _adamw_all(
        tot, rg_w_in, rg_w_out,
        [(w_in[0], m_w_in[0], v_w_in[0]), (w_out[0], m_w_out[0], v_w_out[0])],
        [(norm_gain, m_norm_gain, v_norm_gain),
         (row(final_norm_gain), row(m_final_norm_gain), row(v_final_norm_gain)),
         (hgrn_norm_gain, m_hgrn_norm_gain, v_hgrn_norm_gain),
         (conv_norm_gain, m_conv_norm_gain, v_conv_norm_gain),
         (lb_logits, m_lb_logits, v_lb_logits),
         (taps(conv_w), taps(m_conv_w), taps(v_conv_w))],
        rgrad_x)
    flat = lambda a: a.reshape(D_MODEL)
    untap = lambda a: a.reshape(1, 3, HEAD)
    return (loss.reshape(()), grad_x[None],
            g_norm_gain, g_w_in[None], g_lb, untap(g_conv_w), g_hgrn, g_convn, g_w_out[None], flat(g_final),
            d_ng, d_w_in[None], d_lb, untap(d_cw), d_hg, d_cg, d_w_out[None], flat(d_fg),
            nm_ng, nm_w_in[None], nm_lb, untap(nm_cw), nm_hg, nm_cg, nm_w_out[None], flat(nm_fg),
            nv_ng, nv_w_in[None], nv_lb, untap(nv_cw), nv_hg, nv_cg, nv_w_out[None], flat(nv_fg))
```

```python
import jax
import jax.numpy as jnp
import numpy as np
from jax import lax
from jax.experimental import pallas as pl
from jax.experimental.pallas import tpu as pltpu

F32 = jnp.float32
BF16 = jnp.bfloat16
MESH = pl.DeviceIdType.MESH

SEQ = 2048
D_MODEL = 1024
D_HGRN = 512
D_CONV = 512
HEAD = 128
N_HEADS = 4
CHUNK = 64
CONV_GROUP = 64
N_SHARD = 4
SHARD_COLS = 1024
WO_ROWS = 256
EPS = 1e-6
TB = 256
NCB = TB // CHUNK
N_CHUNKS = SEQ // CHUNK
N_DEV = 8
COLLECTIVE_GATHER, COLLECTIVE_MIX_OUT, COLLECTIVE_TAIL = 1, 0, 2
AUX_O, AUX_CV, AUX_B, AUX_COLS = 0, 512, 1024, 1536

ADAM_LR = 0.001
ADAM_B1 = 0.9
ADAM_B2 = 0.999
ADAM_EPS = 1e-08
ADAM_WD = 0.01
ADAM_STEP = 10

VMEM_LIMIT = 56 * 1024 * 1024


def _dot(a, b):
    return jnp.dot(a, b, preferred_element_type=F32)


def _dot_nt(a, b):
    return lax.dot_general(a, b, (((1,), (1,)), ((), ())), preferred_element_type=F32)


def _dot_tn(a, b):
    return lax.dot_general(a, b, (((0,), (0,)), ((), ())), preferred_element_type=F32)


def _split_bf16(x, n):
    parts = []
    r = x
    for _ in range(n):
        p = r.astype(BF16)
        parts.append(p)
        r = r - p.astype(F32)
    return parts


def _exact_left(m, x, n=3):
    acc = None
    for p in _split_bf16(x, n):
        t = _dot(m, p)
        acc = t if acc is None else acc + t
    return acc


def _exact_left_many(m, xs, n=3):
    parts = [_split_bf16(x, n) for x in xs]
    accs = [None] * len(xs)
    for i in range(n):
        for j in range(len(xs)):
            t = _dot(m, parts[j][i])
            accs[j] = t if accs[j] is None else accs[j] + t
    return accs


def _group_mean_many(xs, gmat, n=2):
    parts = [_split_bf16(x, n) for x in xs]
    accs = [None] * len(xs)
    for i in range(n):
        for j in range(len(xs)):
            t = _dot(parts[j][i], gmat)
            accs[j] = t if accs[j] is None else accs[j] + t
    return accs


def _group_mean(x, gmat, n=2):
    w = gmat.shape[0]
    outs = []
    for c0 in range(0, x.shape[1], w):
        acc = None
        for p in _split_bf16(x[:, c0:c0 + w], n):
            t = _dot(p, gmat)
            acc = t if acc is None else acc + t
        outs.append(acc)
    return jnp.concatenate(outs, axis=1)


def _sigmoid(x):
    return 1.0 / (1.0 + jnp.exp(-x))


def _lower_bound(lbl):
    l0 = lbl[0:1, :]
    l1 = lbl[1:2, :]
    m = jnp.maximum(l0, l1)
    e0 = jnp.exp(l0 - m)
    e1 = jnp.exp(l1 - m)
    return e0 / (e0 + e1)


def _tri(lower):
    r = lax.broadcasted_iota(jnp.int32, (CHUNK, CHUNK), 0)
    c = lax.broadcasted_iota(jnp.int32, (CHUNK, CHUNK), 1)
    return jnp.where((c <= r) if lower else (c >= r), 1.0, 0.0).astype(BF16)


def _causal():
    r = lax.broadcasted_iota(jnp.int32, (CHUNK, CHUNK), 0)
    c = lax.broadcasted_iota(jnp.int32, (CHUNK, CHUNK), 1)
    return c <= r


def _shift_down(x, sh, prev_tail):
    r = pltpu.roll(x, sh, 0)
    pt = pltpu.roll(prev_tail, sh, 0)
    rows = lax.broadcasted_iota(jnp.int32, prev_tail.shape, 0)
    top = jnp.where(rows < sh, pt, r[0:8])
    return jnp.concatenate([top, r[8:]], axis=0)


def _shift_up(x, sh, next_head):
    n = x.shape[0]
    r = pltpu.roll(x, n - sh, 0)
    nh = pltpu.roll(next_head, 8 - sh, 0)
    rows = lax.broadcasted_iota(jnp.int32, next_head.shape, 0)
    bot = jnp.where(rows >= 8 - sh, nh, r[n - 8:])
    return jnp.concatenate([r[:n - 8], bot], axis=0)


def _group_matrix(width, group):
    r = np.arange(width)[:, None] // group
    c = np.arange(width)[None, :] // group
    return jnp.asarray(np.where(r == c, 1.0 / group, 0.0), dtype=BF16)


TG = 1024
SEM_W, SEM_CW, SEM_W_FWD, N_SEM = 0, 4, 7, 11


def _gather_proj(kidx, x2d, g1, w_in, conv_w):
    half_w = D_MODEL // 2
    half_c = SHARD_COLS // 2
    nt = SEQ // TG
    n_steps = 2 * N_SHARD

    def body(k_ref, x_ref, g_ref, w_ref, cw_ref, h_out, p_ref, wg_out, cwg_out,
             wg_v, cwg_v, h_ref, send_sems, recv_sems, out_sems):
        s, t = pl.program_id(0), pl.program_id(1)
        x, y, c = lax.axis_index("x"), lax.axis_index("y"), lax.axis_index("c")
        k = 2 * x + y
        sibling = (x, y, 1 - c)
        chips = [(1 - x, y), (x, 1 - y), (1 - x, 1 - y)]
        kjs = [2 * cx + cy for cx, cy in chips]
        diag = (*chips[2], c)

        def w_half(kk, cc):
            return wg_v.at[kk, pl.ds(cc * half_w, half_w), :]

        def w_quarter(kk, cc, piece):
            return wg_v.at[kk, pl.ds(cc * half_w, half_w), piece * half_c:(piece + 1) * half_c]

        def cw_of(kk):
            return cwg_v.at[:, pl.ds(pl.multiple_of(kk * HEAD, HEAD), HEAD)]

        def copy(sem, ref, to):
            return pltpu.make_async_remote_copy(
                src_ref=ref, dst_ref=ref, send_sem=send_sems.at[sem], recv_sem=recv_sems.at[sem],
                device_id=to, device_id_type=MESH)

        def at_step(sv, tv):
            return pl.when((s == sv) & (t == tv))

        w_direct = ([copy(SEM_W + j, w_half(k, c), (*chips[j], c)) for j in range(2)]
                    + [copy(SEM_W + 2 + p, w_quarter(k, c, p), diag) for p in range(2)])
        cw_direct = [copy(SEM_CW + j, cw_of(k), (*chip, c)) for j, chip in enumerate(chips)]
        w_passed = ([copy(SEM_W_FWD + j, w_half(kjs[j], c), sibling) for j in range(2)]
                    + [copy(SEM_W_FWD + 2 + p, w_quarter(kjs[2], c, p), sibling) for p in range(2)])
        stores = ([pltpu.make_async_copy(wg_v.at[kk], wg_out.at[kk], out_sems.at[i])
                   for i, kk in enumerate([k] + kjs)]
                  + [pltpu.make_async_copy(cwg_v, cwg_out, out_sems.at[4]),
                     pltpu.make_async_copy(h_ref, h_out, out_sems.at[5])])

        @at_step(0, 0)
        def _():
            barrier = pltpu.get_barrier_semaphore()
            for peer in [sibling] + [(*chip, c) for chip in chips]:
                pl.semaphore_signal(barrier, inc=1, device_id=peer, device_id_type=MESH)
            wg_v[k] = w_ref[0].astype(BF16)
            mine = pl.ds(pl.multiple_of(k * HEAD, HEAD), HEAD)
            cwg_v[:, mine] = jnp.zeros((8, HEAD), F32)
            for tap in range(3):
                cwg_v[tap:tap + 1, mine] = cw_ref[:, tap * HEAD:(tap + 1) * HEAD]
            pl.semaphore_wait(barrier, 4)
            for cp in w_direct + cw_direct:
                cp.start()
            stores[0].start()

        @at_step(1, 0)
        def _():
            stores[5].start()

        @at_step(2, 0)
        def _():
            for j in range(2):
                copy(SEM_W + j, w_half(kjs[j], c), sibling).wait_recv()
                w_passed[j].start()
            copy(SEM_W_FWD, w_half(kjs[0], 1 - c), sibling).wait_recv()
            stores[1].start()

        @at_step(4, 0)
        def _():
            copy(SEM_W_FWD + 1, w_half(kjs[1], 1 - c), sibling).wait_recv()
            stores[2].start()

        for p in range(2):
            @at_step(6 + p, 0)
            def _(p=p):
                copy(SEM_W + 2 + p, w_quarter(kjs[2], c, p), sibling).wait_recv()
                w_passed[2 + p].start()
                copy(SEM_W_FWD + 2 + p, w_quarter(kjs[2], 1 - c, p), sibling).wait_recv()
                if p == 1:
                    stores[3].start()
                    for j in range(3):
                        copy(SEM_CW + j, cw_of(kjs[j]), sibling).wait_recv()
                    stores[4].start()

        rows = pl.ds(pl.multiple_of(t * TG, TG), TG)

        @pl.when(s == 0)
        def _():
            xv = x_ref[...]
            r = lax.rsqrt(jnp.mean(xv * xv, axis=-1, keepdims=True) + EPS)
            h_ref[rows, :] = (xv * r * g_ref[...]).astype(BF16)

        sh = s >> 1
        js = k ^ (((sh & 1) << 1) | (sh >> 1))
        for piece in range(2):
            @pl.when((s & 1) == piece)
            def _(piece=piece):
                p_ref[...] = _dot(h_ref[rows, :], wg_v[js, :, piece * half_c:(piece + 1) * half_c])

        @at_step(n_steps - 1, nt - 1)
        def _():
            for cp in w_direct + cw_direct + w_passed:
                cp.wait_send()
            for st in stores:
                st.wait()

    def x_map(s, t, kr):
        return (jnp.where(s == 0, t, nt - 1), 0)

    def p_map(s, t, kr):
        sh = s >> 1
        return (t, 2 * (kr[0] ^ (((sh & 1) << 1) | (sh >> 1))) + (s & 1))

    hbm = pl.BlockSpec(memory_space=pl.ANY)
    grid_spec = pltpu.PrefetchScalarGridSpec(
        num_scalar_prefetch=1, grid=(n_steps, nt),
        in_specs=[pl.BlockSpec((TG, D_MODEL), x_map),
                  pl.BlockSpec((1, D_MODEL), lambda s, t, kr: (0, 0)),
                  pl.BlockSpec((1, D_MODEL, SHARD_COLS), lambda s, t, kr: (0, 0, 0)),
                  pl.BlockSpec((1, 3 * HEAD), lambda s, t, kr: (0, 0))],
        out_specs=(hbm, pl.BlockSpec((TG, half_c), p_map), hbm, hbm),
        scratch_shapes=[pltpu.VMEM((N_SHARD, D_MODEL, SHARD_COLS), BF16),
                        pltpu.VMEM((8, D_CONV), F32), pltpu.VMEM((SEQ, D_MODEL), BF16),
                        pltpu.SemaphoreType.DMA((N_SEM,)), pltpu.SemaphoreType.DMA((N_SEM,)),
                        pltpu.SemaphoreType.DMA((6,))])
    return pl.pallas_call(
        body, name="gather_proj", grid_spec=grid_spec,
        out_shape=(jax.ShapeDtypeStruct((SEQ, D_MODEL), BF16),
                   jax.ShapeDtypeStruct((SEQ, N_SHARD * SHARD_COLS), F32),
                   jax.ShapeDtypeStruct((N_SHARD, D_MODEL, SHARD_COLS), BF16),
                   jax.ShapeDtypeStruct((8, D_CONV), F32)),
        compiler_params=pltpu.CompilerParams(dimension_semantics=("arbitrary", "arbitrary"),
                                             vmem_limit_bytes=VMEM_LIMIT, collective_id=COLLECTIVE_GATHER),
    )(kidx, x2d, g1, w_in, conv_w)


LAG = 6


def _mix_out(proj, lb_logits, cw, ga, gcn, g64, w_out, x2d, gf, tgt):
    half_o = WO_ROWS // 2
    nblk = SEQ // TB
    n_steps = nblk + LAG

    def body(p_ref, lbl_ref, cw_ref, ga_ref, gcn_ref, g64_ref, wo_ref, x_ref, gf_ref, t_ref,
             aux_ref, sto_ref, dx2_ref, dm_ref, gwo_ref, part_ref,
             st_ref, tail_ref, wog_v, stage, ring, acc_ref, send_sems, recv_sems):
        i = pl.program_id(0)
        x, y, c = lax.axis_index("x"), lax.axis_index("y"), lax.axis_index("c")
        k = 2 * x + y
        sibling = (x, y, 1 - c)
        chips = [(1 - x, y), (x, 1 - y), (1 - x, 1 - y)]
        kjs = [2 * cx + cy for cx, cy in chips]

        def wo_half(kk, cc):
            return wog_v.at[pl.ds(pl.multiple_of(kk * WO_ROWS + cc * half_o, half_o), half_o), :]

        def copy(sem, ref, to):
            return pltpu.make_async_remote_copy(
                src_ref=ref, dst_ref=ref, send_sem=send_sems.at[sem], recv_sem=recv_sems.at[sem],
                device_id=to, device_id_type=MESH)

        wo_direct = [copy(j, wo_half(k, c), (*chip, c)) for j, chip in enumerate(chips)]
        wo_passed = [copy(3 + j, wo_half(kj, c), sibling) for j, kj in enumerate(kjs)]

        @pl.when(i == 0)
        def _():
            barrier = pltpu.get_barrier_semaphore()
            for peer in [sibling] + [(*chip, c) for chip in chips]:
                pl.semaphore_signal(barrier, inc=1, device_id=peer, device_id_type=MESH)
            st_ref[...] = jnp.zeros_like(st_ref)
            tail_ref[...] = jnp.zeros_like(tail_ref)
            acc_ref[...] = jnp.zeros_like(acc_ref)
            part_ref[...] = jnp.zeros_like(part_ref)
            wog_v[pl.ds(pl.multiple_of(k * WO_ROWS, WO_ROWS), WO_ROWS), :] = wo_ref[0].astype(BF16)

        @pl.when(i == LAG - 1)
        def _():
            for j in range(3):
                copy(j, wo_half(kjs[j], c), sibling).wait_recv()
                wo_passed[j].start()

        @pl.when(i == LAG)
        def _():
            for j in range(3):
                copy(3 + j, wo_half(kjs[j], 1 - c), sibling).wait_recv()

        lb = _lower_bound(lbl_ref[...])
        tri = _tri(True)
        causal = _causal()
        g64m = g64_ref[...]
        heads = range(N_HEADS)
        cs = [slice(hd * HEAD, (hd + 1) * HEAD) for hd in heads]
        col = lambda base, hd: slice(base + hd * HEAD, base + (hd + 1) * HEAD)

        def mix_chunk(n):
            sl = pl.ds(n * CHUNK, CHUNK)
            sg = [_sigmoid(p_ref[sl, col(512, hd)]) for hd in heads]
            f = [lb[:, cs[hd]] + (1.0 - lb[:, cs[hd]]) * sg[hd] for hd in heads]
            bc = _exact_left_many(tri, [jnp.log(f[hd]) for hd in heads])
            for hd in heads:
                aux_ref[sl, col(AUX_B, hd)] = bc[hd]
            g = [bc[hd][CHUNK - 1:CHUNK, :] for hd in heads]
            qd = [(p_ref[sl, col(0, hd)] * jnp.exp(bc[hd])).astype(BF16) for hd in heads]
            kk = [1.0 - f[hd] for hd in heads]
            ki = [(kk[hd] * jnp.exp(-bc[hd])).astype(BF16) for hd in heads]
            ke = [(kk[hd] * jnp.exp(g[hd] - bc[hd])).astype(BF16) for hd in heads]
            vb = [p_ref[sl, col(1024, hd)].astype(BF16) for hd in heads]
            st = [st_ref[hd] for hd in heads]
            st_b = [a.astype(BF16) for a in st]
            for hd in heads:
                sto_ref[n, hd] = st_b[hd]
            scm = [_dot_nt(qd[hd], ki[hd]) for hd in heads]
            inter = [_dot_nt(qd[hd], st_b[hd]) for hd in heads]
            upd = [_dot_tn(vb[hd], ke[hd]) for hd in heads]
            intra = [_dot(jnp.where(causal, scm[hd], 0.0).astype(BF16), vb[hd]) for hd in heads]
            for hd in heads:
                st_ref[hd] = st[hd] * jnp.exp(g[hd]) + upd[hd]
                o = intra[hd] + inter[hd]
                aux_ref[sl, col(AUX_O, hd)] = o
                ra = lax.rsqrt(jnp.mean(o * o, axis=-1, keepdims=True) + EPS)
                za = p_ref[sl, col(1536, hd)]
                stage[sl, cs[hd]] = (o * ra * ga_ref[:, cs[hd]] * (za * _sigmoid(za))).astype(BF16)
            yb = []
            for hd in heads:
                cu = p_ref[sl, col(3072, hd)] * p_ref[sl, col(2048, hd)]
                tail = tail_ref[:, cs[hd]]
                cv = (cw_ref[0:1, cs[hd]] * _shift_down(cu, 2, tail) + cw_ref[1:2, cs[hd]] * _shift_down(cu, 1, tail)
                      + cw_ref[2:3, cs[hd]] * cu)
                tail_ref[:, cs[hd]] = cu[CHUNK - 8:, :]
                aux_ref[sl, col(AUX_CV, hd)] = cv
                yb.append(p_ref[sl, col(2560, hd)] * cv)
            ms = _group_mean_many([y * y for y in yb], g64m)
            for hd in heads:
                rb = lax.rsqrt(ms[hd] + EPS)
                zb = p_ref[sl, col(3584, hd)]
                stage[sl, col(512, hd)] = (yb[hd] * rb * gcn_ref[:, cs[hd]] * (zb * _sigmoid(zb))).astype(BF16)

        def step(mix, project, entry=False):
            if project:
                mixed_b = ring[pl.ds(pl.multiple_of((i - LAG) * TB, TB), TB), :]
                y = _dot(mixed_b, wog_v[...])
            if mix:
                mix_chunk(0)
            if project:
                x2 = x_ref[...] + y
                r2 = lax.rsqrt(jnp.mean(x2 * x2, axis=-1, keepdims=True) + EPS)
                n2 = x2 * r2
                gfv = gf_ref[...]
                err = n2 * gfv - t_ref[...]
                loss = 0.5 * jnp.sum(jnp.mean(err * err, axis=-1, keepdims=True), axis=0, keepdims=True)
                dy = err * (1.0 / D_MODEL)
                part_ref[1:2, :] += jnp.sum(dy * n2, axis=0, keepdims=True)
                part_ref[7:8, :] += jnp.broadcast_to(loss, (1, D_MODEL))
                dn = dy * gfv
                dx2 = r2 * (dn - n2 * jnp.mean(dn * n2, axis=-1, keepdims=True))
                dx2_ref[...] = dx2
                dx2_b = dx2.astype(BF16)
            if mix:
                mix_chunk(1)
            if entry:
                pl.semaphore_wait(pltpu.get_barrier_semaphore(), 4)
                for cp in wo_direct:
                    cp.start()
            if project:
                dm_ref[...] = _dot_nt(dx2_b, wog_v[...])
            if mix:
                mix_chunk(2)
            if project:
                acc_ref[...] += _dot_tn(mixed_b, dx2_b)
            if mix:
                mix_chunk(3)
                ring[pl.ds(pl.multiple_of(i * TB, TB), TB), :] = stage[...]

        @pl.when(i == 0)
        def _():
            step(True, False, entry=True)

        @pl.when((i > 0) & (i < LAG))
        def _():
            step(True, False)

        @pl.when((i >= LAG) & (i < nblk))
        def _():
            step(True, True)

        @pl.when(i >= nblk)
        def _():
            step(False, True)

        @pl.when(i == n_steps - 1)
        def _():
            gwo_ref[...] = acc_ref[...].astype(BF16)
            for cp in wo_direct + wo_passed:
                cp.wait_send()

    assert NCB == 4
    row = lambda w: pl.BlockSpec((1, w), lambda i: (0, 0))
    mix_blk = lambda i: jnp.minimum(i, nblk - 1)
    out_blk = lambda i: jnp.clip(i - LAG, 0, nblk - 1)
    tok = lambda: pl.BlockSpec((TB, D_MODEL), lambda i: (out_blk(i), 0))
    return pl.pallas_call(
        body, name="mix_out", grid=(n_steps,),
        out_shape=(jax.ShapeDtypeStruct((SEQ, AUX_COLS), F32),
                   jax.ShapeDtypeStruct((N_CHUNKS, N_HEADS, HEAD, HEAD), BF16),
                   jax.ShapeDtypeStruct((SEQ, D_MODEL), F32),
                   jax.ShapeDtypeStruct((SEQ, D_MODEL), F32),
                   jax.ShapeDtypeStruct((D_MODEL, D_MODEL), BF16),
                   jax.ShapeDtypeStruct((8, D_MODEL), F32)),
        in_specs=[pl.BlockSpec((TB, 4096), lambda i: (jnp.minimum(i, nblk - 1), 0)),
                  pl.BlockSpec((2, D_HGRN), lambda i: (0, 0)),
                  pl.BlockSpec((8, D_CONV), lambda i: (0, 0)),
                  row(D_HGRN), row(D_CONV),
                  pl.BlockSpec((HEAD, HEAD), lambda i: (0, 0)),
                  pl.BlockSpec((1, WO_ROWS, D_MODEL), lambda i: (0, 0, 0)),
                  tok(), row(D_MODEL), tok()],
        out_specs=(pl.BlockSpec((TB, AUX_COLS), lambda i: (mix_blk(i), 0)),
                   pl.BlockSpec((NCB, N_HEADS, HEAD, HEAD), lambda i: (mix_blk(i), 0, 0, 0)),
                   tok(), tok(),
                   pl.BlockSpec((D_MODEL, D_MODEL), lambda i: (0, 0)),
                   pl.BlockSpec((8, D_MODEL), lambda i: (0, 0))),
        scratch_shapes=[pltpu.VMEM((N_HEADS, HEAD, HEAD), F32), pltpu.VMEM((8, D_CONV), F32),
                        pltpu.VMEM((D_MODEL, D_MODEL), BF16), pltpu.VMEM((TB, D_MODEL), BF16),
                        pltpu.VMEM((SEQ, D_MODEL), BF16), pltpu.VMEM((D_MODEL, D_MODEL), F32),
                        pltpu.SemaphoreType.DMA((6,)), pltpu.SemaphoreType.DMA((6,))],
        compiler_params=pltpu.CompilerParams(dimension_semantics=("arbitrary",), vmem_limit_bytes=VMEM_LIMIT,
                                             collective_id=COLLECTIVE_MIX_OUT),
    )(proj, lb_logits, cw, ga, gcn, g64, w_out, x2d, gf, tgt)


def _mix_bwd(proj, aux, states, dmixed, lb_logits, cw, ga, gcn, g64):
    nblk = SEQ // TB

    def body(p_ref, aux_ref, st_ref, dm_ref, lbl_ref, cw_ref, ga_ref, gcn_ref, g64_ref,
             dp_ref, part_ref, dst_ref, head_ref, dlb_ref):
        i = pl.program_id(0)

        @pl.when(i == 0)
        def _():
            dst_ref[...] = jnp.zeros_like(dst_ref)
            head_ref[...] = jnp.zeros_like(head_ref)
            part_ref[...] = jnp.zeros_like(part_ref)
            dlb_ref[...] = jnp.zeros_like(dlb_ref)

        lb = _lower_bound(lbl_ref[...])
        triu = _tri(False)
        causal = _causal()
        g64m = g64_ref[...]
        rowsum = lambda a: jnp.sum(a, axis=0, keepdims=True)
        heads = range(N_HEADS)
        cs = [slice(hd * HEAD, (hd + 1) * HEAD) for hd in heads]
        col = lambda base, hd: slice(base + hd * HEAD, base + (hd + 1) * HEAD)
        for n in reversed(range(NCB)):
            sl = pl.ds(n * CHUNK, CHUNK)
            cvv = [aux_ref[sl, col(AUX_CV, hd)] for hd in heads]
            gb = [p_ref[sl, col(2560, hd)] for hd in heads]
            yb = [gb[hd] * cvv[hd] for hd in heads]
            ms = _group_mean_many([y * y for y in yb], g64m)
            rb, nb, dnb = [], [], []
            for hd in heads:
                rb.append(lax.rsqrt(ms[hd] + EPS))
                nb.append(yb[hd] * rb[hd])
                zb = p_ref[sl, col(3584, hd)]
                sgb = _sigmoid(zb)
                dmb = dm_ref[sl, col(512, hd)]
                silu = zb * sgb
                dgate = dmb * gcn_ref[:, cs[hd]]
                part_ref[2:3, col(512, hd)] += rowsum(dmb * nb[hd] * silu)
                dp_ref[sl, col(3584, hd)] = (dgate * nb[hd] * (sgb + silu * (1.0 - sgb))).astype(BF16)
                dnb.append(dgate * silu)
            mdn = _group_mean_many([dnb[hd] * nb[hd] for hd in heads], g64m)
            for hd in heads:
                dyb = rb[hd] * (dnb[hd] - nb[hd] * mdn[hd])
                dp_ref[sl, col(2560, hd)] = (dyb * cvv[hd]).astype(BF16)
                dcv = dyb * gb[hd]
                head = head_ref[:, cs[hd]]
                dcv1 = _shift_up(dcv, 1, head)
                dcv2 = _shift_up(dcv, 2, head)
                head_ref[:, cs[hd]] = dcv[0:8, :]
                u = p_ref[sl, col(2048, hd)]
                gc = p_ref[sl, col(3072, hd)]
                cu = gc * u
                part_ref[4:5, cs[hd]] += rowsum(dcv2 * cu)
                part_ref[5:6, cs[hd]] += rowsum(dcv1 * cu)
                part_ref[6:7, cs[hd]] += rowsum(dcv * cu)
                dcu = cw_ref[2:3, cs[hd]] * dcv + cw_ref[1:2, cs[hd]] * dcv1 + cw_ref[0:1, cs[hd]] * dcv2
                dp_ref[sl, col(3072, hd)] = (dcu * u).astype(BF16)
                dp_ref[sl, col(2048, hd)] = (dcu * gc).astype(BF16)
            do_b = []
            for hd in heads:
                ov = aux_ref[sl, col(AUX_O, hd)]
                ra = lax.rsqrt(jnp.mean(ov * ov, axis=-1, keepdims=True) + EPS)
                na = ov * ra
                za = p_ref[sl, col(1536, hd)]
                sga = _sigmoid(za)
                dma = dm_ref[sl, cs[hd]]
                silu = za * sga
                dgate = dma * ga_ref[:, cs[hd]]
                part_ref[2:3, cs[hd]] += rowsum(dma * na * silu)
                dp_ref[sl, col(1536, hd)] = (dgate * na * (sga + silu * (1.0 - sga))).astype(BF16)
                dna = dgate * silu
                do_b.append((ra * (dna - na * jnp.mean(dna * na, axis=-1, keepdims=True))).astype(BF16))
            s = [_sigmoid(p_ref[sl, col(512, hd)]) for hd in heads]
            f = [lb[:, cs[hd]] + (1.0 - lb[:, cs[hd]]) * s[hd] for hd in heads]
            bc = [aux_ref[sl, col(AUX_B, hd)] for hd in heads]
            g = [bc[hd][CHUNK - 1:CHUNK, :] for hd in heads]
            eb = [jnp.exp(bc[hd]) for hd in heads]
            enb = [jnp.exp(-bc[hd]) for hd in heads]
            eg = [jnp.exp(g[hd] - bc[hd]) for hd in heads]
            dec = [jnp.exp(g[hd]) for hd in heads]
            qd = [p_ref[sl, cs[hd]] * eb[hd] for hd in heads]
            kk = [1.0 - f[hd] for hd in heads]
            ki = [kk[hd] * enb[hd] for hd in heads]
            ke = [kk[hd] * eg[hd] for hd in heads]
            qd_b = [a.astype(BF16) for a in qd]
            ki_b = [a.astype(BF16) for a in ki]
            ke_b = [a.astype(BF16) for a in ke]
            vb = [p_ref[sl, col(1024, hd)].astype(BF16) for hd in heads]
            st_b = [st_ref[n, hd] for hd in heads]
            dst = [dst_ref[hd] for hd in heads]
            dst_b = [a.astype(BF16) for a in dst]
            scm = [_dot_nt(qd_b[hd], ki_b[hd]) for hd in heads]
            amm = [_dot_nt(do_b[hd], vb[hd]) for hd in heads]
            dqd2 = [_dot(do_b[hd], st_b[hd]) for hd in heads]
            dke = [_dot(vb[hd], dst_b[hd]) for hd in heads]
            dv2 = [_dot_nt(ke_b[hd], dst_b[hd]) for hd in heads]
            dsu = [_dot_tn(do_b[hd], qd_b[hd]) for hd in heads]
            sc = [jnp.where(causal, scm[hd], 0.0).astype(BF16) for hd in heads]
            am = [jnp.where(causal, amm[hd], 0.0).astype(BF16) for hd in heads]
            dqd1 = [_dot(am[hd], ki_b[hd]) for hd in heads]
            dki = [_dot_tn(am[hd], qd_b[hd]) for hd in heads]
            dv1 = [_dot_tn(sc[hd], do_b[hd]) for hd in heads]
            db, dgv, dkk = [], [], []
            for hd in heads:
                dqd = dqd1[hd] + dqd2[hd]
                ddec = rowsum(dst[hd] * st_b[hd].astype(F32))
                dst_ref[hd] = dst[hd] * dec[hd] + dsu[hd]
                dp_ref[sl, cs[hd]] = (dqd * eb[hd]).astype(BF16)
                dp_ref[sl, col(1024, hd)] = (dv1[hd] + dv2[hd]).astype(BF16)
                dke_eg = dke[hd] * eg[hd]
                dkk.append(dki[hd] * enb[hd] + dke_eg)
                db.append(dqd * qd[hd] - kk[hd] * dkk[hd])
                dgv.append(rowsum(kk[hd] * dke_eg) + ddec * dec[hd])
            rc = _exact_left_many(triu, db, 2)
            for hd in heads:
                df = (rc[hd] + dgv[hd]) / f[hd] - dkk[hd]
                one_s = 1.0 - s[hd]
                dlb_ref[:, cs[hd]] += rowsum(df * one_s)
                dp_ref[sl, col(512, hd)] = (df * (1.0 - lb[:, cs[hd]]) * s[hd] * one_s).astype(BF16)

        @pl.when(i == nblk - 1)
        def _():
            row = dlb_ref[...] * lb * (1.0 - lb)
            part_ref[3:4, 0:D_HGRN] = row
            part_ref[3:4, D_HGRN:] = -row

    rev = lambda w: pl.BlockSpec((TB, w), lambda i: (nblk - 1 - i, 0))
    row = lambda w: pl.BlockSpec((1, w), lambda i: (0, 0))
    return pl.pallas_call(
        body, name="mix_bwd", grid=(nblk,),
        out_shape=(jax.ShapeDtypeStruct((SEQ, 4096), BF16),
                   jax.ShapeDtypeStruct((8, D_MODEL), F32)),
        in_specs=[rev(4096), rev(AUX_COLS),
                  pl.BlockSpec((NCB, N_HEADS, HEAD, HEAD), lambda i: (nblk - 1 - i, 0, 0, 0)),
                  rev(D_MODEL),
                  pl.BlockSpec((2, D_HGRN), lambda i: (0, 0)),
                  pl.BlockSpec((8, D_CONV), lambda i: (0, 0)),
                  row(D_HGRN), row(D_CONV),
                  pl.BlockSpec((HEAD, HEAD), lambda i: (0, 0))],
        out_specs=(rev(4096), pl.BlockSpec((8, D_MODEL), lambda i: (0, 0))),
        scratch_shapes=[pltpu.VMEM((N_HEADS, HEAD, HEAD), F32), pltpu.VMEM((8, D_CONV), F32),
                        pltpu.VMEM((1, D_HGRN), F32)],
        compiler_params=pltpu.CompilerParams(dimension_semantics=("arbitrary",), vmem_limit_bytes=VMEM_LIMIT),
    )(proj, aux, states, dmixed, lb_logits, cw, ga, gcn, g64)


TT = 1024
TX = 512
(SEM_D2D, SEM_D2D_O, SEM_ICI, SEM_ICI_O, SEM_FIN, SEM_FIN_O, SEM_SMALL, SEM_VIA, SEM_NORM, N_SEM_TAIL) = (
    0, 4, 5, 8, 11, 12, 12, 20, 22, 30)


def _bwd_tail(kidx, h, dproj, wg, gwo, x2d, dx2, g1, small_a, small_b):
    hw = D_MODEL // 2
    ho = WO_ROWS // 2
    nt = SEQ // TT
    norm_step = 2 * N_SHARD
    n_steps = norm_step + SEQ // TX // nt

    def body(k_ref, h_ref, dp_ref, w_ref, gwo_ref, x_ref, dx2_ref, g_ref, sm_ref, smb_ref,
             gx_ref, gw_out, gwo_out, osm_ref,
             acc, dh, sendbuf, keep, sibrcv, rcv, merge, sib_o, p_o, rcv_o, res_o, sm_buf, dng_buf, dng,
             send_sems, recv_sems, out_sems):
        s, t = pl.program_id(0), pl.program_id(1)
        x, y, c = lax.axis_index("x"), lax.axis_index("y"), lax.axis_index("c")
        k = 2 * x + y
        me = 4 * x + 2 * y + c
        sibling = (x, y, 1 - c)
        chips = [(1 - x, 1 - y), (1 - x, y), (x, 1 - y)]
        kjs = [2 * cx + cy for cx, cy in chips]
        mine = pl.ds(pl.multiple_of(c * hw, hw), hw)
        other = pl.ds(pl.multiple_of((1 - c) * hw, hw), hw)
        mine_o = pl.ds(pl.multiple_of(c * ho, ho), ho)
        other_o = pl.ds(pl.multiple_of((1 - c) * ho, ho), ho)

        def copy(sem, src, dst, to):
            return pltpu.make_async_remote_copy(
                src_ref=src, dst_ref=dst, send_sem=send_sems.at[sem], recv_sem=recv_sems.at[sem],
                device_id=to, device_id_type=MESH)

        def at_step(sv, tv):
            return pl.when((s == sv) & (t == tv))

        def at_norm_block(b):
            return at_step(norm_step + b // nt, b % nt)

        d2d = [copy(SEM_D2D + sv, sendbuf.at[sv], sibrcv.at[sv], sibling) for sv in range(N_SHARD)]
        d2d_o = copy(SEM_D2D_O, gwo_ref.at[:, other_o, :], sib_o, sibling)
        ici = {sv: copy(SEM_ICI + sv, keep.at[sv], rcv.at[sv - 1], (*chips[sv], c)) for sv in (1, 2)}
        qh = hw // 2
        via = [copy(SEM_VIA, keep.at[0, 0:qh, :], merge.at[1], (*chips[1], c)),
               copy(SEM_VIA + 1, keep.at[0, qh:hw, :], merge.at[0], (*chips[2], c))]
        merged_rows = [slice(qh, hw), slice(0, qh)]
        ici_o = [copy(SEM_ICI_O + sv, p_o.at[kjs[sv]], rcv_o.at[sv], (*chips[sv], c)) for sv in range(3)]
        fin = copy(SEM_FIN, acc.at[mine, :], gw_out.at[mine, :], sibling)
        fin_o = copy(SEM_FIN_O, res_o.at[mine_o, :], res_o.at[mine_o, :], sibling)
        peers = [(x ^ (m >> 2), y ^ ((m >> 1) & 1), c ^ (m & 1)) for m in range(1, N_DEV)]
        smalls = [copy(SEM_SMALL + 1 + j, sm_buf.at[me], sm_buf.at[me], to) for j, to in enumerate(peers)]
        dngs = [copy(SEM_NORM + 1 + j, dng_buf.at[me], dng_buf.at[me], to) for j, to in enumerate(peers)]
        store_w = pltpu.make_async_copy(acc.at[mine, :], gw_out.at[mine, :], out_sems.at[0])
        store_o = pltpu.make_async_copy(res_o, gwo_out, out_sems.at[1])

        @at_step(0, 0)
        def _():
            barrier = pltpu.get_barrier_semaphore()
            for to in peers:
                pl.semaphore_signal(barrier, inc=1, device_id=to, device_id_type=MESH)
            sm_buf[me] = sm_ref[...] + smb_ref[...]
            pl.semaphore_wait(barrier, N_DEV - 1)
            d2d_o.start()
            for cp in smalls:
                cp.start()

        @at_step(0, 1)
        def _():
            d2d_o.wait_recv()
            for j in range(N_SHARD):
                p_o[j] = (gwo_ref[j, mine_o, :].astype(F32) + sib_o[j].astype(F32)).astype(BF16)
            for cp in ici_o:
                cp.start()
            res_o[mine_o, :] = gwo_ref[k, mine_o, :].astype(F32) + sib_o[k].astype(F32)

        rows = pl.ds(pl.multiple_of(t * TT, TT), TT)

        @pl.when((s < N_SHARD) & (t == 0))
        def _():
            acc[...] = _dot_tn(h_ref[...], dp_ref[...])

        @pl.when((s < N_SHARD) & (t > 0))
        def _():
            acc[...] += _dot_tn(h_ref[...], dp_ref[...])

        for sv in range(N_SHARD):
            @at_step(sv, nt - 1)
            def _(sv=sv):
                sendbuf[sv] = acc[other, :].astype(BF16)
                d2d[sv].start()
                if sv < 3:
                    keep[sv] = acc[mine, :].astype(BF16)

        @at_step(1, 0)
        def _():
            d2d[0].wait_recv()
            keep[0] = (keep[0].astype(F32) + sibrcv[0].astype(F32)).astype(BF16)
            for cp in via:
                cp.start()

        for sv in (1, 2):
            @at_step(sv + 2, 0)
            def _(sv=sv):
                d2d[sv].wait_recv()
                keep[sv] = (keep[sv].astype(F32) + sibrcv[sv].astype(F32)).astype(BF16)
                via[2 - sv].wait_recv()
                rows_m = merged_rows[sv - 1]
                keep[sv, rows_m, :] = (keep[sv, rows_m, :].astype(F32) + merge[sv - 1].astype(F32)).astype(BF16)
                ici[sv].start()

        @pl.when(s == N_SHARD)
        def _():
            dh[rows, :] = _dot_nt(dp_ref[...], w_ref[0])

        @pl.when((s > N_SHARD) & (s < norm_step))
        def _():
            dh[rows, :] += _dot_nt(dp_ref[...], w_ref[0])

        @at_norm_block(0)
        def _():
            d2d[3].wait_recv()
            acc[mine, :] += sibrcv[3].astype(F32)

        @at_norm_block(1)
        def _():
            tot = res_o[mine_o, :]
            for sv in range(3):
                ici_o[sv].wait_recv()
                tot = tot + rcv_o[sv].astype(F32)
            res_o[mine_o, :] = tot
            fin_o.start()

        @at_norm_block(2)
        def _():
            ici[1].wait_recv()
            acc[mine, :] += rcv[0].astype(F32)

        @at_norm_block(SEQ // TX - 2)
        def _():
            ici[2].wait_recv()
            acc[mine, :] += rcv[1].astype(F32)
            fin.start()
            store_w.start()
            fin_o.wait_recv()
            store_o.start()

        @at_norm_block(0)
        def _():
            dng[...] = jnp.zeros_like(dng)

        @pl.when(s >= norm_step)
        def _():
            blk = (s - norm_step) * nt + t
            dhv = dh[pl.ds(pl.multiple_of(blk * TX, TX), TX), :]
            xv = x_ref[...]
            r = lax.rsqrt(jnp.mean(xv * xv, axis=-1, keepdims=True) + EPS)
            xn = xv * r
            dng[...] += jnp.sum(dhv * xn, axis=0, keepdims=True)
            dxn = dhv * g_ref[...]
            gx_ref[...] = dx2_ref[...] + r * (dxn - xn * jnp.mean(dxn * xn, axis=-1, keepdims=True))

        @at_step(n_steps - 1, nt - 1)
        def _():
            dng_buf[me] = dng[...]
            for cp in dngs:
                cp.start()
            for m in range(1, N_DEV):
                copy(SEM_SMALL + m, sm_buf.at[0], sm_buf.at[0], sibling).wait_recv()
            tot = sm_buf[0]
            for d in range(1, N_DEV):
                tot = tot + sm_buf[d]
            osm_ref[...] = tot
            for m in range(1, N_DEV):
                copy(SEM_NORM + m, dng_buf.at[0], dng_buf.at[0], sibling).wait_recv()
            tot = dng_buf[0]
            for d in range(1, N_DEV):
                tot = tot + dng_buf[d]
            osm_ref[0:1, :] = tot
            fin.wait_recv()
            for cp in d2d + [d2d_o] + via + list(ici.values()) + ici_o + [fin, fin_o] + smalls + dngs:
                cp.wait_send()
            store_o.wait()
            store_w.wait()

    def shard_of(s, kr):
        order = jnp.where(s < N_SHARD, s, jnp.where(s < norm_step, s - N_SHARD, 3))
        return kr[0] ^ (3 - order)

    def h_map(s, t, kr):
        return (jnp.where(s < N_SHARD, t, nt - 1), 0)

    def dp_map(s, t, kr):
        return (jnp.where(s < norm_step, t, nt - 1), shard_of(s, kr))

    def w_map(s, t, kr):
        return (shard_of(jnp.maximum(s, N_SHARD), kr), 0, 0)

    def blk_map(s, t, kr):
        return (jnp.where(s < norm_step, 0, (s - norm_step) * nt + t), 0)

    hbm = pl.BlockSpec(memory_space=pl.ANY)
    grid_spec = pltpu.PrefetchScalarGridSpec(
        num_scalar_prefetch=1, grid=(n_steps, nt),
        in_specs=[pl.BlockSpec((TT, D_MODEL), h_map),
                  pl.BlockSpec((TT, SHARD_COLS), dp_map),
                  pl.BlockSpec((1, D_MODEL, SHARD_COLS), w_map),
                  pl.BlockSpec((N_SHARD, WO_ROWS, D_MODEL), lambda s, t, kr: (0, 0, 0),
                               pipeline_mode=pl.Buffered(1)),
                  pl.BlockSpec((TX, D_MODEL), blk_map),
                  pl.BlockSpec((TX, D_MODEL), blk_map),
                  pl.BlockSpec((1, D_MODEL), lambda s, t, kr: (0, 0)),
                  pl.BlockSpec((8, D_MODEL), lambda s, t, kr: (0, 0)),
                  pl.BlockSpec((8, D_MODEL), lambda s, t, kr: (0, 0))],
        out_specs=(pl.BlockSpec((TX, D_MODEL), blk_map), hbm, hbm,
                   pl.BlockSpec((8, D_MODEL), lambda s, t, kr: (0, 0))),
        scratch_shapes=[pltpu.VMEM((D_MODEL, SHARD_COLS), F32), pltpu.VMEM((SEQ, D_MODEL), F32),
                        pltpu.VMEM((N_SHARD, hw, SHARD_COLS), BF16), pltpu.VMEM((3, hw, SHARD_COLS), BF16),
                        pltpu.VMEM((N_SHARD, hw, SHARD_COLS), BF16), pltpu.VMEM((2, hw, SHARD_COLS), BF16),
                        pltpu.VMEM((2, hw // 2, SHARD_COLS), BF16),
                        pltpu.VMEM((N_SHARD, ho, D_MODEL), BF16), pltpu.VMEM((N_SHARD, ho, D_MODEL), BF16),
                        pltpu.VMEM((3, ho, D_MODEL), BF16), pltpu.VMEM((WO_ROWS, D_MODEL), F32),
                        pltpu.VMEM((N_DEV, 8, D_MODEL), F32), pltpu.VMEM((N_DEV, 1, D_MODEL), F32),
                        pltpu.VMEM((1, D_MODEL), F32),
                        pltpu.SemaphoreType.DMA((N_SEM_TAIL,)), pltpu.SemaphoreType.DMA((N_SEM_TAIL,)),
                        pltpu.SemaphoreType.DMA((2,))])
    return pl.pallas_call(
        body, name="bwd_tail", grid_spec=grid_spec,
        out_shape=(jax.ShapeDtypeStruct((SEQ, D_MODEL), F32),
                   jax.ShapeDtypeStruct((D_MODEL, SHARD_COLS), F32),
                   jax.ShapeDtypeStruct((WO_ROWS, D_MODEL), F32),
                   jax.ShapeDtypeStruct((8, D_MODEL), F32)),
        compiler_params=pltpu.CompilerParams(dimension_semantics=("arbitrary", "arbitrary"),
                                             vmem_limit_bytes=61 * 1024 * 1024, collective_id=COLLECTIVE_TAIL),
    )(kidx, h, dproj, wg, gwo, x2d, dx2, g1, small_a, small_b)


def _adam_update(w, g, m, v):
    nm = ADAM_B1 * m + (1.0 - ADAM_B1) * g
    nv = ADAM_B2 * v + (1.0 - ADAM_B2) * (g * g)
    m_hat = nm / (1.0 - ADAM_B1 ** ADAM_STEP)
    v_hat = nv / (1.0 - ADAM_B2 ** ADAM_STEP)
    return -ADAM_LR * (m_hat / (jnp.sqrt(v_hat) + ADAM_EPS) + ADAM_WD * w), nm, nv


def _adamw_all(tot, g_w_in, g_w_out, big, small, grad_x):
    n = len(small)
    rows = WO_ROWS
    steps = D_MODEL // rows

    def body(tot_ref, *refs):
        gx_ref, gx_out = refs[2 + 3 * (2 + n)], refs[-1]
        gx_out[...] = gx_ref[...]
        ins, outs = refs[:2 + 3 * (2 + n)], refs[3 + 3 * (2 + n):-1]
        g_refs, wmv = ins[:2], ins[2:]
        loss_ref, quads = outs[0], outs[1:]

        def update(j, g):
            w_ref, m_ref, v_ref = wmv[3 * j:3 * j + 3]
            g_ref, d_ref, nm_ref, nv_ref = quads[4 * j:4 * j + 4]
            g_ref[...] = g
            d_ref[...], nm_ref[...], nv_ref[...] = _adam_update(w_ref[...], g, m_ref[...], v_ref[...])

        update(0, g_refs[0][...])

        @pl.when(pl.program_id(0) == 0)
        def _():
            update(1, g_refs[1][...])
            k = 2 * lax.axis_index("x") + lax.axis_index("y")
            mine = pl.ds(pl.multiple_of(k * HEAD, HEAD), HEAD)
            loss_ref[...] = tot_ref[7:8, 0:1]
            grads = [tot_ref[0:1, :], tot_ref[1:2, :], tot_ref[2:3, 0:D_HGRN], tot_ref[2:3, D_HGRN:],
                     jnp.concatenate([tot_ref[3:4, 0:D_HGRN], tot_ref[3:4, D_HGRN:]], axis=0),
                     jnp.concatenate([tot_ref[4 + tap:5 + tap, mine] for tap in range(3)], axis=1)]
            for j, g in enumerate(grads):
                update(2 + j, g)

    whole = lambda a: pl.BlockSpec(a.shape, lambda i: (0, 0))
    blk = pl.BlockSpec((rows, SHARD_COLS), lambda i: (i, 0))
    arrays = [a for triple in big + small for a in triple]
    in_specs = ([whole(tot), blk, whole(g_w_out)] + [blk] * 3 + [whole(a) for a in arrays[3:]])
    shapes = [big[0][0], big[1][0]] + [w for w, _, _ in small]
    out_shape = (jax.ShapeDtypeStruct((1, 1), F32),) + tuple(
        jax.ShapeDtypeStruct(w.shape, F32) for w in shapes for _ in range(4))
    out_specs = (pl.BlockSpec((1, 1), lambda i: (0, 0)),) + (blk,) * 4 + tuple(
        whole(w) for w in shapes[1:] for _ in range(4))
    gx_blk = pl.BlockSpec((SEQ // steps, D_MODEL), lambda i: (i, 0))
    outs = pl.pallas_call(
        body, name="adamw_all", grid=(steps,),
        out_shape=out_shape + (jax.ShapeDtypeStruct(grad_x.shape, F32),),
        in_specs=in_specs + [gx_blk], out_specs=out_specs + (gx_blk,),
        compiler_params=pltpu.CompilerParams(dimension_semantics=("arbitrary",), vmem_limit_bytes=VMEM_LIMIT),
    )(tot, g_w_in, g_w_out, *arrays, grad_x)
    return [outs[0]] + [outs[1 + 4 * j:5 + 4 * j] for j in range(2 + n)] + [outs[-1]]


def _local_step(x2d, tgt, proj, lb_logits, cw, ga, gcn, w_out, gf):
    g64 = _group_matrix(HEAD, CONV_GROUP)
    aux, states, dx2, dmixed, gwo, part_out = _mix_out(proj, lb_logits, cw, ga, gcn, g64, w_out, x2d, gf, tgt)
    dproj, part_mix = _mix_bwd(proj, aux, states, dmixed, lb_logits, cw, ga, gcn, g64)
    return dproj, dx2, gwo.reshape(N_SHARD, WO_ROWS, D_MODEL), part_out, part_mix


def kernel(x, norm_gain, w_in, lb_logits, conv_w, hgrn_norm_gain, conv_norm_gain, w_out, final_norm_gain, loss_target, m_norm_gain, m_w_in, m_lb_logits, m_conv_w, m_hgrn_norm_gain, m_conv_norm_gain, m_w_out, m_final_norm_gain, v_norm_gain, v_w_in, v_lb_logits, v_conv_w, v_hgrn_norm_gain, v_conv_norm_gain, v_w_out, v_final_norm_gain):
    k = 2 * lax.axis_index("x") + lax.axis_index("y")
    kidx = jnp.reshape(k, (1,)).astype(jnp.int32)
    row = lambda a: a.reshape(1, D_MODEL)
    taps = lambda a: a.reshape(1, 3 * HEAD)
    h, proj, wg, cw = _gather_proj(kidx, x[0], norm_gain, w_in, taps(conv_w))
    dproj, dx2, gwo, part_out, part_mix = _local_step(
        x[0], loss_target[0], proj, lb_logits, cw, hgrn_norm_gain, conv_norm_gain, w_out, row(final_norm_gain))
    rgrad_x, rg_w_in, rg_w_out, tot = _bwd_tail(kidx, h, dproj, wg, gwo, x[0], dx2, norm_gain, part_out, part_mix)

    (loss, (g_w_in, d_w_in, nm_w_in, nv_w_in), (g_w_out, d_w_out, nm_w_out, nv_w_out),
     (g_norm_gain, d_ng, nm_ng, nv_ng), (g_final, d_fg, nm_fg, nv_fg), (g_hgrn, d_hg, nm_hg, nv_hg),
     (g_convn, d_cg, nm_cg, nv_cg), (g_lb, d_lb, nm_lb, nv_lb), (g_conv_w, d_cw, nm_cw, nv_cw),
     grad_x) = _adamw_all(
        tot, rg_w_in, rg_w_out,
        [(w_in[0], m_w_in[0], v_w_in[0]), (w_out[0], m_w_out[0], v_w_out[0])],
        [(norm_gain, m_norm_gain, v_norm_gain),
         (row(final_norm_gain), row(m_final_norm_gain), row(v_final_norm_gain)),
         (hgrn_norm_gain, m_hgrn_norm_gain, v_hgrn_norm_gain),
         (conv_norm_gain, m_conv_norm_gain, v_conv_norm_gain),
         (lb_logits, m_lb_logits, v_lb_logits),
         (taps(conv_w), taps(m_conv_w), taps(v_conv_w))],
        rgrad_x)
    flat = lambda a: a.reshape(D_MODEL)
    untap = lambda a: a.reshape(1, 3, HEAD)
    return (loss.reshape(()), grad_x[None],
            g_norm_gain, g_w_in[None], g_lb, untap(g_conv_w), g_hgrn, g_convn, g_w_out[None], flat(g_final),
            d_ng, d_w_in[None], d_lb, untap(d_cw), d_hg, d_cg, d_w_out[None], flat(d_fg),
            nm_ng, nm_w_in[None], nm_lb, untap(nm_cw), nm_hg, nm_cg, nm_w_out[None], flat(nm_fg),
            nv_ng, nv_w_in[None], nv_lb, untap(nv_cw), nv_hg, nv_cg, nv_w_out[None], flat(nv_fg))
```

```python
import jax
import jax.numpy as jnp
import numpy as np
from jax import lax
from jax.experimental import pallas as pl
from jax.experimental.pallas import tpu as pltpu

F32 = jnp.float32
BF16 = jnp.bfloat16
MESH = pl.DeviceIdType.MESH

SEQ = 2048
D_MODEL = 1024
D_HGRN = 512
D_CONV = 512
HEAD = 128
N_HEADS = 4
CHUNK = 64
CONV_GROUP = 64
N_SHARD = 4
SHARD_COLS = 1024
WO_ROWS = 256
EPS = 1e-6
TB = 256
NCB = TB // CHUNK
N_CHUNKS = SEQ // CHUNK
N_DEV = 8
COLLECTIVE_GATHER, COLLECTIVE_MIX_OUT, COLLECTIVE_TAIL = 1, 0, 2
AUX_O, AUX_CV, AUX_B, AUX_COLS = 0, 512, 1024, 1536

ADAM_LR = 0.001
ADAM_B1 = 0.9
ADAM_B2 = 0.999
ADAM_EPS = 1e-08
ADAM_WD = 0.01
ADAM_STEP = 10

VMEM_LIMIT = 56 * 1024 * 1024


def _dot(a, b):
    return jnp.dot(a, b, preferred_element_type=F32)


def _dot_nt(a, b):
    return lax.dot_general(a, b, (((1,), (1,)), ((), ())), preferred_element_type=F32)


def _dot_tn(a, b):
    return lax.dot_general(a, b, (((0,), (0,)), ((), ())), preferred_element_type=F32)


def _split_bf16(x, n):
    parts = []
    r = x
    for _ in range(n):
        p = r.astype(BF16)
        parts.append(p)
        r = r - p.astype(F32)
    return parts


def _exact_left(m, x, n=3):
    acc = None
    for p in _split_bf16(x, n):
        t = _dot(m, p)
        acc = t if acc is None else acc + t
    return acc


def _exact_left_many(m, xs, n=3):
    parts = [_split_bf16(x, n) for x in xs]
    accs = [None] * len(xs)
    for i in range(n):
        for j in range(len(xs)):
            t = _dot(m, parts[j][i])
            accs[j] = t if accs[j] is None else accs[j] + t
    return accs


def _group_mean_many(xs, gmat, n=2):
    parts = [_split_bf16(x, n) for x in xs]
    accs = [None] * len(xs)
    for i in range(n):
        for j in range(len(xs)):
            t = _dot(parts[j][i], gmat)
            accs[j] = t if accs[j] is None else accs[j] + t
    return accs


def _group_mean(x, gmat, n=2):
    w = gmat.shape[0]
    outs = []
    for c0 in range(0, x.shape[1], w):
        acc = None
        for p in _split_bf16(x[:, c0:c0 + w], n):
            t = _dot(p, gmat)
            acc = t if acc is None else acc + t
        outs.append(acc)
    return jnp.concatenate(outs, axis=1)


def _sigmoid(x):
    return 1.0 / (1.0 + jnp.exp(-x))


def _lower_bound(lbl):
    l0 = lbl[0:1, :]
    l1 = lbl[1:2, :]
    m = jnp.maximum(l0, l1)
    e0 = jnp.exp(l0 - m)
    e1 = jnp.exp(l1 - m)
    return e0 / (e0 + e1)


def _tri(lower):
    r = lax.broadcasted_iota(jnp.int32, (CHUNK, CHUNK), 0)
    c = lax.broadcasted_iota(jnp.int32, (CHUNK, CHUNK), 1)
    return jnp.where((c <= r) if lower else (c >= r), 1.0, 0.0).astype(BF16)


def _causal():
    r = lax.broadcasted_iota(jnp.int32, (CHUNK, CHUNK), 0)
    c = lax.broadcasted_iota(jnp.int32, (CHUNK, CHUNK), 1)
    return c <= r


def _shift_down(x, sh, prev_tail):
    r = pltpu.roll(x, sh, 0)
    pt = pltpu.roll(prev_tail, sh, 0)
    rows = lax.broadcasted_iota(jnp.int32, prev_tail.shape, 0)
    top = jnp.where(rows < sh, pt, r[0:8])
    return jnp.concatenate([top, r[8:]], axis=0)


def _shift_up(x, sh, next_head):
    n = x.shape[0]
    r = pltpu.roll(x, n - sh, 0)
    nh = pltpu.roll(next_head, 8 - sh, 0)
    rows = lax.broadcasted_iota(jnp.int32, next_head.shape, 0)
    bot = jnp.where(rows >= 8 - sh, nh, r[n - 8:])
    return jnp.concatenate([r[:n - 8], bot], axis=0)


def _group_matrix(width, group):
    r = np.arange(width)[:, None] // group
    c = np.arange(width)[None, :] // group
    return jnp.asarray(np.where(r == c, 1.0 / group, 0.0), dtype=BF16)


TG = 1024
SEM_W, SEM_CW, SEM_W_FWD, N_SEM = 0, 4, 7, 11


def _gather_proj(kidx, x2d, g1, w_in, conv_w):
    half_w = D_MODEL // 2
    half_c = SHARD_COLS // 2
    nt = SEQ // TG
    n_steps = 2 * N_SHARD

    def body(k_ref, x_ref, g_ref, w_ref, cw_ref, h_out, p_ref, wg_out, cwg_out,
             wg_v, cwg_v, h_ref, send_sems, recv_sems, out_sems):
        s, t = pl.program_id(0), pl.program_id(1)
        x, y, c = lax.axis_index("x"), lax.axis_index("y"), lax.axis_index("c")
        k = 2 * x + y
        sibling = (x, y, 1 - c)
        chips = [(1 - x, y), (x, 1 - y), (1 - x, 1 - y)]
        kjs = [2 * cx + cy for cx, cy in chips]
        diag = (*chips[2], c)

        def w_half(kk, cc):
            return wg_v.at[kk, pl.ds(cc * half_w, half_w), :]

        def w_quarter(kk, cc, piece):
            return wg_v.at[kk, pl.ds(cc * half_w, half_w), piece * half_c:(piece + 1) * half_c]

        def cw_of(kk):
            return cwg_v.at[:, pl.ds(pl.multiple_of(kk * HEAD, HEAD), HEAD)]

        def copy(sem, ref, to):
            return pltpu.make_async_remote_copy(
                src_ref=ref, dst_ref=ref, send_sem=send_sems.at[sem], recv_sem=recv_sems.at[sem],
                device_id=to, device_id_type=MESH)

        def at_step(sv, tv):
            return pl.when((s == sv) & (t == tv))

        w_direct = ([copy(SEM_W + j, w_half(k, c), (*chips[j], c)) for j in range(2)]
                    + [copy(SEM_W + 2 + p, w_quarter(k, c, p), diag) for p in range(2)])
        cw_direct = [copy(SEM_CW + j, cw_of(k), (*chip, c)) for j, chip in enumerate(chips)]
        w_passed = ([copy(SEM_W_FWD + j, w_half(kjs[j], c), sibling) for j in range(2)]
                    + [copy(SEM_W_FWD + 2 + p, w_quarter(kjs[2], c, p), sibling) for p in range(2)])
        stores = ([pltpu.make_async_copy(wg_v.at[kk], wg_out.at[kk], out_sems.at[i])
                   for i, kk in enumerate([k] + kjs)]
                  + [pltpu.make_async_copy(cwg_v, cwg_out, out_sems.at[4]),
                     pltpu.make_async_copy(h_ref, h_out, out_sems.at[5])])

        @at_step(0, 0)
        def _():
            barrier = pltpu.get_barrier_semaphore()
            for peer in [sibling] + [(*chip, c) for chip in chips]:
                pl.semaphore_signal(barrier, inc=1, device_id=peer, device_id_type=MESH)
            wg_v[k] = w_ref[0].astype(BF16)
            mine = pl.ds(pl.multiple_of(k * HEAD, HEAD), HEAD)
            cwg_v[:, mine] = jnp.zeros((8, HEAD), F32)
            for tap in range(3):
                cwg_v[tap:tap + 1, mine] = cw_ref[:, tap * HEAD:(tap + 1) * HEAD]
            pl.semaphore_wait(barrier, 4)
            for cp in w_direct + cw_direct:
                cp.start()
            stores[0].start()

        @at_step(1, 0)
        def _():
            stores[5].start()

        @at_step(2, 0)
        def _():
            for j in range(2):
                copy(SEM_W + j, w_half(kjs[j], c), sibling).wait_recv()
                w_passed[j].start()
            copy(SEM_W_FWD, w_half(kjs[0], 1 - c), sibling).wait_recv()
            stores[1].start()

        @at_step(4, 0)
        def _():
            copy(SEM_W_FWD + 1, w_half(kjs[1], 1 - c), sibling).wait_recv()
            stores[2].start()

        for p in range(2):
            @at_step(6 + p, 0)
            def _(p=p):
                copy(SEM_W + 2 + p, w_quarter(kjs[2], c, p), sibling).wait_recv()
                w_passed[2 + p].start()
                copy(SEM_W_FWD + 2 + p, w_quarter(kjs[2], 1 - c, p), sibling).wait_recv()
                if p == 1:
                    stores[3].start()
                    for j in range(3):
                        copy(SEM_CW + j, cw_of(kjs[j]), sibling).wait_recv()
                    stores[4].start()

        rows = pl.ds(pl.multiple_of(t * TG, TG), TG)

        @pl.when(s == 0)
        def _():
            xv = x_ref[...]
            r = lax.rsqrt(jnp.mean(xv * xv, axis=-1, keepdims=True) + EPS)
            h_ref[rows, :] = (xv * r * g_ref[...]).astype(BF16)

        sh = s >> 1
        js = k ^ (((sh & 1) << 1) | (sh >> 1))
        for piece in range(2):
            @pl.when((s & 1) == piece)
            def _(piece=piece):
                p_ref[...] = _dot(h_ref[rows, :], wg_v[js, :, piece * half_c:(piece + 1) * half_c])

        @at_step(n_steps - 1, nt - 1)
        def _():
            for cp in w_direct + cw_direct + w_passed:
                cp.wait_send()
            for st in stores:
                st.wait()

    def x_map(s, t, kr):
        return (jnp.where(s == 0, t, nt - 1), 0)

    def p_map(s, t, kr):
        sh = s >> 1
        return (t, 2 * (kr[0] ^ (((sh & 1) << 1) | (sh >> 1))) + (s & 1))

    hbm = pl.BlockSpec(memory_space=pl.ANY)
    grid_spec = pltpu.PrefetchScalarGridSpec(
        num_scalar_prefetch=1, grid=(n_steps, nt),
        in_specs=[pl.BlockSpec((TG, D_MODEL), x_map),
                  pl.BlockSpec((1, D_MODEL), lambda s, t, kr: (0, 0)),
                  pl.BlockSpec((1, D_MODEL, SHARD_COLS), lambda s, t, kr: (0, 0, 0)),
                  pl.BlockSpec((1, 3 * HEAD), lambda s, t, kr: (0, 0))],
        out_specs=(hbm, pl.BlockSpec((TG, half_c), p_map), hbm, hbm),
        scratch_shapes=[pltpu.VMEM((N_SHARD, D_MODEL, SHARD_COLS), BF16),
                        pltpu.VMEM((8, D_CONV), F32), pltpu.VMEM((SEQ, D_MODEL), BF16),
                        pltpu.SemaphoreType.DMA((N_SEM,)), pltpu.SemaphoreType.DMA((N_SEM,)),
                        pltpu.SemaphoreType.DMA((6,))])
    return pl.pallas_call(
        body, name="gather_proj", grid_spec=grid_spec,
        out_shape=(jax.ShapeDtypeStruct((SEQ, D_MODEL), BF16),
                   jax.ShapeDtypeStruct((SEQ, N_SHARD * SHARD_COLS), F32),
                   jax.ShapeDtypeStruct((N_SHARD, D_MODEL, SHARD_COLS), BF16),
                   jax.ShapeDtypeStruct((8, D_CONV), F32)),
        compiler_params=pltpu.CompilerParams(dimension_semantics=("arbitrary", "arbitrary"),
                                             vmem_limit_bytes=VMEM_LIMIT, collective_id=COLLECTIVE_GATHER),
    )(kidx, x2d, g1, w_in, conv_w)


LAG = 6


def _mix_out(proj, lb_logits, cw, ga, gcn, g64, w_out, x2d, gf, tgt):
    half_o = WO_ROWS // 2
    nblk = SEQ // TB
    n_steps = nblk + LAG

    def body(p_ref, lbl_ref, cw_ref, ga_ref, gcn_ref, g64_ref, wo_ref, x_ref, gf_ref, t_ref,
             aux_ref, sto_ref, dx2_ref, dm_ref, gwo_ref, part_ref,
             st_ref, tail_ref, wog_v, stage, ring, acc_ref, send_sems, recv_sems):
        i = pl.program_id(0)
        x, y, c = lax.axis_index("x"), lax.axis_index("y"), lax.axis_index("c")
        k = 2 * x + y
        sibling = (x, y, 1 - c)
        chips = [(1 - x, y), (x, 1 - y), (1 - x, 1 - y)]
        kjs = [2 * cx + cy for cx, cy in chips]

        def wo_half(kk, cc):
            return wog_v.at[pl.ds(pl.multiple_of(kk * WO_ROWS + cc * half_o, half_o), half_o), :]

        def copy(sem, ref, to):
            return pltpu.make_async_remote_copy(
                src_ref=ref, dst_ref=ref, send_sem=send_sems.at[sem], recv_sem=recv_sems.at[sem],
                device_id=to, device_id_type=MESH)

        wo_direct = [copy(j, wo_half(k, c), (*chip, c)) for j, chip in enumerate(chips)]
        wo_passed = [copy(3 + j, wo_half(kj, c), sibling) for j, kj in enumerate(kjs)]

        @pl.when(i == 0)
        def _():
            barrier = pltpu.get_barrier_semaphore()
            for peer in [sibling] + [(*chip, c) for chip in chips]:
                pl.semaphore_signal(barrier, inc=1, device_id=peer, device_id_type=MESH)
            st_ref[...] = jnp.zeros_like(st_ref)
            tail_ref[...] = jnp.zeros_like(tail_ref)
            part_ref[...] = jnp.zeros_like(part_ref)
            wog_v[pl.ds(pl.multiple_of(k * WO_ROWS, WO_ROWS), WO_ROWS), :] = wo_ref[0].astype(BF16)

        @pl.when(i == LAG - 1)
        def _():
            for j in range(3):
                copy(j, wo_half(kjs[j], c), sibling).wait_recv()
                wo_passed[j].start()

        @pl.when(i == LAG)
        def _():
            for j in range(3):
                copy(3 + j, wo_half(kjs[j], 1 - c), sibling).wait_recv()

        lb = _lower_bound(lbl_ref[...])
        tri = _tri(True)
        causal = _causal()
        g64m = g64_ref[...]
        heads = range(N_HEADS)
        cs = [slice(hd * HEAD, (hd + 1) * HEAD) for hd in heads]
        col = lambda base, hd: slice(base + hd * HEAD, base + (hd + 1) * HEAD)

        def mix_chunk(n):
            sl = pl.ds(n * CHUNK, CHUNK)
            sg = [_sigmoid(p_ref[sl, col(512, hd)]) for hd in heads]
            f = [lb[:, cs[hd]] + (1.0 - lb[:, cs[hd]]) * sg[hd] for hd in heads]
            bc = _exact_left_many(tri, [jnp.log(f[hd]) for hd in heads])
            for hd in heads:
                aux_ref[sl, col(AUX_B, hd)] = bc[hd]
            g = [bc[hd][CHUNK - 1:CHUNK, :] for hd in heads]
            qd = [(p_ref[sl, col(0, hd)] * jnp.exp(bc[hd])).astype(BF16) for hd in heads]
            kk = [1.0 - f[hd] for hd in heads]
            ki = [(kk[hd] * jnp.exp(-bc[hd])).astype(BF16) for hd in heads]
            ke = [(kk[hd] * jnp.exp(g[hd] - bc[hd])).astype(BF16) for hd in heads]
            vb = [p_ref[sl, col(1024, hd)].astype(BF16) for hd in heads]
            st = [st_ref[hd] for hd in heads]
            st_b = [a.astype(BF16) for a in st]
            for hd in heads:
                sto_ref[n, hd] = st_b[hd]
            scm = [_dot_nt(qd[hd], ki[hd]) for hd in heads]
            inter = [_dot_nt(qd[hd], st_b[hd]) for hd in heads]
            upd = [_dot_tn(vb[hd], ke[hd]) for hd in heads]
            intra = [_dot(jnp.where(causal, scm[hd], 0.0).astype(BF16), vb[hd]) for hd in heads]
            for hd in heads:
                st_ref[hd] = st[hd] * jnp.exp(g[hd]) + upd[hd]
                o = intra[hd] + inter[hd]
                aux_ref[sl, col(AUX_O, hd)] = o
                ra = lax.rsqrt(jnp.mean(o * o, axis=-1, keepdims=True) + EPS)
                za = p_ref[sl, col(1536, hd)]
                stage[sl, cs[hd]] = (o * ra * ga_ref[:, cs[hd]] * (za * _sigmoid(za))).astype(BF16)
            yb = []
            for hd in heads:
                cu = p_ref[sl, col(3072, hd)] * p_ref[sl, col(2048, hd)]
                tail = tail_ref[:, cs[hd]]
                cv = (cw_ref[0:1, cs[hd]] * _shift_down(cu, 2, tail) + cw_ref[1:2, cs[hd]] * _shift_down(cu, 1, tail)
                      + cw_ref[2:3, cs[hd]] * cu)
                tail_ref[:, cs[hd]] = cu[CHUNK - 8:, :]
                aux_ref[sl, col(AUX_CV, hd)] = cv
                yb.append(p_ref[sl, col(2560, hd)] * cv)
            ms = _group_mean_many([y * y for y in yb], g64m)
            for hd in heads:
                rb = lax.rsqrt(ms[hd] + EPS)
                zb = p_ref[sl, col(3584, hd)]
                stage[sl, col(512, hd)] = (yb[hd] * rb * gcn_ref[:, cs[hd]] * (zb * _sigmoid(zb))).astype(BF16)

        def step(mix, project, entry=False, first=False):
            if project:
                mixed_b = ring[pl.ds(pl.multiple_of((i - LAG) * TB, TB), TB), :]
                y = _dot(mixed_b, wog_v[...])
            if mix:
                mix_chunk(0)
            if project:
                x2 = x_ref[...] + y
                r2 = lax.rsqrt(jnp.mean(x2 * x2, axis=-1, keepdims=True) + EPS)
                n2 = x2 * r2
                gfv = gf_ref[...]
                err = n2 * gfv - t_ref[...]
                loss = 0.5 * jnp.sum(jnp.mean(err * err, axis=-1, keepdims=True), axis=0, keepdims=True)
                dy = err * (1.0 / D_MODEL)
                part_ref[1:2, :] += jnp.sum(dy * n2, axis=0, keepdims=True)
                part_ref[7:8, :] += jnp.broadcast_to(loss, (1, D_MODEL))
                dn = dy * gfv
                dx2 = r2 * (dn - n2 * jnp.mean(dn * n2, axis=-1, keepdims=True))
                dx2_ref[...] = dx2
                dx2_b = dx2.astype(BF16)
            if mix:
                mix_chunk(1)
            if entry:
                pl.semaphore_wait(pltpu.get_barrier_semaphore(), 4)
                for cp in wo_direct:
                    cp.start()
            if project:
                dm_ref[...] = _dot_nt(dx2_b, wog_v[...])
            if mix:
                mix_chunk(2)
            if project and first:
                acc_ref[...] = _dot_tn(mixed_b, dx2_b)
            elif project:
                acc_ref[...] += _dot_tn(mixed_b, dx2_b)
            if mix:
                mix_chunk(3)
                ring[pl.ds(pl.multiple_of(i * TB, TB), TB), :] = stage[...]

        @pl.when(i == 0)
        def _():
            step(True, False, entry=True)

        @pl.when((i > 0) & (i < LAG))
        def _():
            step(True, False)

        @pl.when(i == LAG)
        def _():
            step(True, True, first=True)

        @pl.when((i > LAG) & (i < nblk))
        def _():
            step(True, True)

        @pl.when(i >= nblk)
        def _():
            step(False, True)

        @pl.when(i == n_steps - 1)
        def _():
            gwo_ref[...] = acc_ref[...].astype(BF16)
            for cp in wo_direct + wo_passed:
                cp.wait_send()

    assert NCB == 4
    row = lambda w: pl.BlockSpec((1, w), lambda i: (0, 0))
    mix_blk = lambda i: jnp.minimum(i, nblk - 1)
    out_blk = lambda i: jnp.clip(i - LAG, 0, nblk - 1)
    tok = lambda: pl.BlockSpec((TB, D_MODEL), lambda i: (out_blk(i), 0))
    return pl.pallas_call(
        body, name="mix_out", grid=(n_steps,),
        out_shape=(jax.ShapeDtypeStruct((SEQ, AUX_COLS), F32),
                   jax.ShapeDtypeStruct((N_CHUNKS, N_HEADS, HEAD, HEAD), BF16),
                   jax.ShapeDtypeStruct((SEQ, D_MODEL), F32),
                   jax.ShapeDtypeStruct((SEQ, D_MODEL), F32),
                   jax.ShapeDtypeStruct((D_MODEL, D_MODEL), BF16),
                   jax.ShapeDtypeStruct((8, D_MODEL), F32)),
        in_specs=[pl.BlockSpec((TB, 4096), lambda i: (jnp.minimum(i, nblk - 1), 0)),
                  pl.BlockSpec((2, D_HGRN), lambda i: (0, 0)),
                  pl.BlockSpec((8, D_CONV), lambda i: (0, 0)),
                  row(D_HGRN), row(D_CONV),
                  pl.BlockSpec((HEAD, HEAD), lambda i: (0, 0)),
                  pl.BlockSpec((1, WO_ROWS, D_MODEL), lambda i: (0, 0, 0)),
                  tok(), row(D_MODEL), tok()],
        out_specs=(pl.BlockSpec((TB, AUX_COLS), lambda i: (mix_blk(i), 0)),
                   pl.BlockSpec((NCB, N_HEADS, HEAD, HEAD), lambda i: (mix_blk(i), 0, 0, 0)),
                   tok(), tok(),
                   pl.BlockSpec((D_MODEL, D_MODEL), lambda i: (0, 0)),
                   pl.BlockSpec((8, D_MODEL), lambda i: (0, 0))),
        scratch_shapes=[pltpu.VMEM((N_HEADS, HEAD, HEAD), F32), pltpu.VMEM((8, D_CONV), F32),
                        pltpu.VMEM((D_MODEL, D_MODEL), BF16), pltpu.VMEM((TB, D_MODEL), BF16),
                        pltpu.VMEM((SEQ, D_MODEL), BF16), pltpu.VMEM((D_MODEL, D_MODEL), F32),
                        pltpu.SemaphoreType.DMA((6,)), pltpu.SemaphoreType.DMA((6,))],
        compiler_params=pltpu.CompilerParams(dimension_semantics=("arbitrary",), vmem_limit_bytes=VMEM_LIMIT,
                                             collective_id=COLLECTIVE_MIX_OUT),
    )(proj, lb_logits, cw, ga, gcn, g64, w_out, x2d, gf, tgt)


def _mix_bwd(proj, aux, states, dmixed, lb_logits, cw, ga, gcn, g64):
    nblk = SEQ // TB

    def body(p_ref, aux_ref, st_ref, dm_ref, lbl_ref, cw_ref, ga_ref, gcn_ref, g64_ref,
             dp_ref, part_ref, dst_ref, head_ref, dlb_ref):
        i = pl.program_id(0)

        @pl.when(i == 0)
        def _():
            dst_ref[...] = jnp.zeros_like(dst_ref)
            head_ref[...] = jnp.zeros_like(head_ref)
            part_ref[...] = jnp.zeros_like(part_ref)
            dlb_ref[...] = jnp.zeros_like(dlb_ref)

        lb = _lower_bound(lbl_ref[...])
        triu = _tri(False)
        causal = _causal()
        g64m = g64_ref[...]
        rowsum = lambda a: jnp.sum(a, axis=0, keepdims=True)
        heads = range(N_HEADS)
        cs = [slice(hd * HEAD, (hd + 1) * HEAD) for hd in heads]
        col = lambda base, hd: slice(base + hd * HEAD, base + (hd + 1) * HEAD)
        for n in reversed(range(NCB)):
            sl = pl.ds(n * CHUNK, CHUNK)
            cvv = [aux_ref[sl, col(AUX_CV, hd)] for hd in heads]
            gb = [p_ref[sl, col(2560, hd)] for hd in heads]
            yb = [gb[hd] * cvv[hd] for hd in heads]
            ms = _group_mean_many([y * y for y in yb], g64m)
            rb, nb, dnb = [], [], []
            for hd in heads:
                rb.append(lax.rsqrt(ms[hd] + EPS))
                nb.append(yb[hd] * rb[hd])
                zb = p_ref[sl, col(3584, hd)]
                sgb = _sigmoid(zb)
                dmb = dm_ref[sl, col(512, hd)]
                silu = zb * sgb
                dgate = dmb * gcn_ref[:, cs[hd]]
                part_ref[2:3, col(512, hd)] += rowsum(dmb * nb[hd] * silu)
                dp_ref[sl, col(3584, hd)] = (dgate * nb[hd] * (sgb + silu * (1.0 - sgb))).astype(BF16)
                dnb.append(dgate * silu)
            mdn = _group_mean_many([dnb[hd] * nb[hd] for hd in heads], g64m)
            for hd in heads:
                dyb = rb[hd] * (dnb[hd] - nb[hd] * mdn[hd])
                dp_ref[sl, col(2560, hd)] = (dyb * cvv[hd]).astype(BF16)
                dcv = dyb * gb[hd]
                head = head_ref[:, cs[hd]]
                dcv1 = _shift_up(dcv, 1, head)
                dcv2 = _shift_up(dcv, 2, head)
                head_ref[:, cs[hd]] = dcv[0:8, :]
                u = p_ref[sl, col(2048, hd)]
                gc = p_ref[sl, col(3072, hd)]
                cu = gc * u
                part_ref[4:5, cs[hd]] += rowsum(dcv2 * cu)
                part_ref[5:6, cs[hd]] += rowsum(dcv1 * cu)
                part_ref[6:7, cs[hd]] += rowsum(dcv * cu)
                dcu = cw_ref[2:3, cs[hd]] * dcv + cw_ref[1:2, cs[hd]] * dcv1 + cw_ref[0:1, cs[hd]] * dcv2
                dp_ref[sl, col(3072, hd)] = (dcu * u).astype(BF16)
                dp_ref[sl, col(2048, hd)] = (dcu * gc).astype(BF16)
            do_b = []
            for hd in heads:
                ov = aux_ref[sl, col(AUX_O, hd)]
                ra = lax.rsqrt(jnp.mean(ov * ov, axis=-1, keepdims=True) + EPS)
                na = ov * ra
                za = p_ref[sl, col(1536, hd)]
                sga = _sigmoid(za)
                dma = dm_ref[sl, cs[hd]]
                silu = za * sga
                dgate = dma * ga_ref[:, cs[hd]]
                part_ref[2:3, cs[hd]] += rowsum(dma * na * silu)
                dp_ref[sl, col(1536, hd)] = (dgate * na * (sga + silu * (1.0 - sga))).astype(BF16)
                dna = dgate * silu
                do_b.append((ra * (dna - na * jnp.mean(dna * na, axis=-1, keepdims=True))).astype(BF16))
            s = [_sigmoid(p_ref[sl, col(512, hd)]) for hd in heads]
            f = [lb[:, cs[hd]] + (1.0 - lb[:, cs[hd]]) * s[hd] for hd in heads]
            bc = [aux_ref[sl, col(AUX_B, hd)] for hd in heads]
            g = [bc[hd][CHUNK - 1:CHUNK, :] for hd in heads]
            eb = [jnp.exp(bc[hd]) for hd in heads]
            enb = [jnp.exp(-bc[hd]) for hd in heads]
            eg = [jnp.exp(g[hd] - bc[hd]) for hd in heads]
            dec = [jnp.exp(g[hd]) for hd in heads]
            qd = [p_ref[sl, cs[hd]] * eb[hd] for hd in heads]
            kk = [1.0 - f[hd] for hd in heads]
            ki = [kk[hd] * enb[hd] for hd in heads]
            ke = [kk[hd] * eg[hd] for hd in heads]
            qd_b = [a.astype(BF16) for a in qd]
            ki_b = [a.astype(BF16) for a in ki]
            ke_b = [a.astype(BF16) for a in ke]
            vb = [p_ref[sl, col(1024, hd)].astype(BF16) for hd in heads]
            st_b = [st_ref[n, hd] for hd in heads]
            dst = [dst_ref[hd] for hd in heads]
            dst_b = [a.astype(BF16) for a in dst]
            scm = [_dot_nt(qd_b[hd], ki_b[hd]) for hd in heads]
            amm = [_dot_nt(do_b[hd], vb[hd]) for hd in heads]
            dqd2 = [_dot(do_b[hd], st_b[hd]) for hd in heads]
            dke = [_dot(vb[hd], dst_b[hd]) for hd in heads]
            dv2 = [_dot_nt(ke_b[hd], dst_b[hd]) for hd in heads]
            dsu = [_dot_tn(do_b[hd], qd_b[hd]) for hd in heads]
            sc = [jnp.where(causal, scm[hd], 0.0).astype(BF16) for hd in heads]
            am = [jnp.where(causal, amm[hd], 0.0).astype(BF16) for hd in heads]
            dqd1 = [_dot(am[hd], ki_b[hd]) for hd in heads]
            dki = [_dot_tn(am[hd], qd_b[hd]) for hd in heads]
            dv1 = [_dot_tn(sc[hd], do_b[hd]) for hd in heads]
            db, dgv, dkk = [], [], []
            for hd in heads:
                dqd = dqd1[hd] + dqd2[hd]
                ddec = rowsum(dst[hd] * st_b[hd].astype(F32))
                dst_ref[hd] = dst[hd] * dec[hd] + dsu[hd]
                dp_ref[sl, cs[hd]] = (dqd * eb[hd]).astype(BF16)
                dp_ref[sl, col(1024, hd)] = (dv1[hd] + dv2[hd]).astype(BF16)
                dke_eg = dke[hd] * eg[hd]
                dkk.append(dki[hd] * enb[hd] + dke_eg)
                db.append(dqd * qd[hd] - kk[hd] * dkk[hd])
                dgv.append(rowsum(kk[hd] * dke_eg) + ddec * dec[hd])
            rc = _exact_left_many(triu, db, 2)
            for hd in heads:
                df = (rc[hd] + dgv[hd]) / f[hd] - dkk[hd]
                one_s = 1.0 - s[hd]
                dlb_ref[:, cs[hd]] += rowsum(df * one_s)
                dp_ref[sl, col(512, hd)] = (df * (1.0 - lb[:, cs[hd]]) * s[hd] * one_s).astype(BF16)

        @pl.when(i == nblk - 1)
        def _():
            row = dlb_ref[...] * lb * (1.0 - lb)
            part_ref[3:4, 0:D_HGRN] = row
            part_ref[3:4, D_HGRN:] = -row

    rev = lambda w: pl.BlockSpec((TB, w), lambda i: (nblk - 1 - i, 0))
    row = lambda w: pl.BlockSpec((1, w), lambda i: (0, 0))
    return pl.pallas_call(
        body, name="mix_bwd", grid=(nblk,),
        out_shape=(jax.ShapeDtypeStruct((SEQ, 4096), BF16),
                   jax.ShapeDtypeStruct((8, D_MODEL), F32)),
        in_specs=[rev(4096), rev(AUX_COLS),
                  pl.BlockSpec((NCB, N_HEADS, HEAD, HEAD), lambda i: (nblk - 1 - i, 0, 0, 0)),
                  rev(D_MODEL),
                  pl.BlockSpec((2, D_HGRN), lambda i: (0, 0)),
                  pl.BlockSpec((8, D_CONV), lambda i: (0, 0)),
                  row(D_HGRN), row(D_CONV),
                  pl.BlockSpec((HEAD, HEAD), lambda i: (0, 0))],
        out_specs=(rev(4096), pl.BlockSpec((8, D_MODEL), lambda i: (0, 0))),
        scratch_shapes=[pltpu.VMEM((N_HEADS, HEAD, HEAD), F32), pltpu.VMEM((8, D_CONV), F32),
                        pltpu.VMEM((1, D_HGRN), F32)],
        compiler_params=pltpu.CompilerParams(dimension_semantics=("arbitrary",), vmem_limit_bytes=VMEM_LIMIT),
    )(proj, aux, states, dmixed, lb_logits, cw, ga, gcn, g64)


TT = 1024
TX = 512
(SEM_D2D, SEM_D2D_O, SEM_ICI, SEM_ICI_O, SEM_FIN, SEM_FIN_O, SEM_SMALL, SEM_VIA, SEM_NORM, N_SEM_TAIL) = (
    0, 4, 5, 8, 11, 12, 12, 20, 22, 30)


def _bwd_tail(kidx, h, dproj, wg, gwo, x2d, dx2, g1, small_a, small_b):
    hw = D_MODEL // 2
    ho = WO_ROWS // 2
    nt = SEQ // TT
    norm_step = 2 * N_SHARD
    n_steps = norm_step + SEQ // TX // nt

    def body(k_ref, h_ref, dp_ref, w_ref, gwo_ref, x_ref, dx2_ref, g_ref, sm_ref, smb_ref,
             gx_ref, gw_out, gwo_out, osm_ref,
             acc, dh, sendbuf, keep, sibrcv, rcv, merge, sib_o, p_o, rcv_o, res_o, sm_buf, dng_buf, dng,
             send_sems, recv_sems, out_sems):
        s, t = pl.program_id(0), pl.program_id(1)
        x, y, c = lax.axis_index("x"), lax.axis_index("y"), lax.axis_index("c")
        k = 2 * x + y
        me = 4 * x + 2 * y + c
        sibling = (x, y, 1 - c)
        chips = [(1 - x, 1 - y), (1 - x, y), (x, 1 - y)]
        kjs = [2 * cx + cy for cx, cy in chips]
        mine = pl.ds(pl.multiple_of(c * hw, hw), hw)
        other = pl.ds(pl.multiple_of((1 - c) * hw, hw), hw)
        mine_o = pl.ds(pl.multiple_of(c * ho, ho), ho)
        other_o = pl.ds(pl.multiple_of((1 - c) * ho, ho), ho)

        def copy(sem, src, dst, to):
            return pltpu.make_async_remote_copy(
                src_ref=src, dst_ref=dst, send_sem=send_sems.at[sem], recv_sem=recv_sems.at[sem],
                device_id=to, device_id_type=MESH)

        def at_step(sv, tv):
            return pl.when((s == sv) & (t == tv))

        def at_norm_block(b):
            return at_step(norm_step + b // nt, b % nt)

        d2d = [copy(SEM_D2D + sv, sendbuf.at[sv], sibrcv.at[sv], sibling) for sv in range(N_SHARD)]
        d2d_o = copy(SEM_D2D_O, gwo_ref.at[:, other_o, :], sib_o, sibling)
        ici = {sv: copy(SEM_ICI + sv, keep.at[sv], rcv.at[sv - 1], (*chips[sv], c)) for sv in (1, 2)}
        qh = hw // 2
        via = [copy(SEM_VIA, keep.at[0, 0:qh, :], merge.at[1], (*chips[1], c)),
               copy(SEM_VIA + 1, keep.at[0, qh:hw, :], merge.at[0], (*chips[2], c))]
        merged_rows = [slice(qh, hw), slice(0, qh)]
        ici_o = [copy(SEM_ICI_O + sv, p_o.at[kjs[sv]], rcv_o.at[sv], (*chips[sv], c)) for sv in range(3)]
        fin = copy(SEM_FIN, acc.at[mine, :], gw_out.at[mine, :], sibling)
        fin_o = copy(SEM_FIN_O, res_o.at[mine_o, :], res_o.at[mine_o, :], sibling)
        peers = [(x ^ (m >> 2), y ^ ((m >> 1) & 1), c ^ (m & 1)) for m in range(1, N_DEV)]
        smalls = [copy(SEM_SMALL + 1 + j, sm_buf.at[me], sm_buf.at[me], to) for j, to in enumerate(peers)]
        dngs = [copy(SEM_NORM + 1 + j, dng_buf.at[me], dng_buf.at[me], to) for j, to in enumerate(peers)]
        store_w = pltpu.make_async_copy(acc.at[mine, :], gw_out.at[mine, :], out_sems.at[0])
        store_o = pltpu.make_async_copy(res_o, gwo_out, out_sems.at[1])

        @at_step(0, 0)
        def _():
            barrier = pltpu.get_barrier_semaphore()
            for to in peers:
                pl.semaphore_signal(barrier, inc=1, device_id=to, device_id_type=MESH)
            sm_buf[me] = sm_ref[...] + smb_ref[...]
            pl.semaphore_wait(barrier, N_DEV - 1)
            d2d_o.start()
            for cp in smalls:
                cp.start()

        @at_step(0, 1)
        def _():
            d2d_o.wait_recv()
            for j in range(N_SHARD):
                p_o[j] = (gwo_ref[j, mine_o, :].astype(F32) + sib_o[j].astype(F32)).astype(BF16)
            res_o[mine_o, :] = gwo_ref[k, mine_o, :].astype(F32) + sib_o[k].astype(F32)
            for cp in ici_o:
                cp.start()

        rows = pl.ds(pl.multiple_of(t * TT, TT), TT)

        @pl.when((s < N_SHARD) & (t == 0))
        def _():
            acc[...] = _dot_tn(h_ref[...], dp_ref[...])

        @pl.when((s < N_SHARD) & (t > 0))
        def _():
            acc[...] += _dot_tn(h_ref[...], dp_ref[...])

        for sv in range(N_SHARD):
            @at_step(sv, nt - 1)
            def _(sv=sv):
                sendbuf[sv] = acc[other, :].astype(BF16)
                if sv < 3:
                    keep[sv] = acc[mine, :].astype(BF16)
                d2d[sv].start()

        @at_step(1, 0)
        def _():
            d2d[0].wait_recv()
            keep[0] = (keep[0].astype(F32) + sibrcv[0].astype(F32)).astype(BF16)
            for cp in via:
                cp.start()

        for sv in (1, 2):
            @at_step(sv + 2, 0)
            def _(sv=sv):
                d2d[sv].wait_recv()
                keep[sv] = (keep[sv].astype(F32) + sibrcv[sv].astype(F32)).astype(BF16)
                via[2 - sv].wait_recv()
                rows_m = merged_rows[sv - 1]
                keep[sv, rows_m, :] = (keep[sv, rows_m, :].astype(F32) + merge[sv - 1].astype(F32)).astype(BF16)
                ici[sv].start()

        @pl.when(s == N_SHARD)
        def _():
            dh[rows, :] = _dot_nt(dp_ref[...], w_ref[0])

        @pl.when((s > N_SHARD) & (s < norm_step))
        def _():
            dh[rows, :] += _dot_nt(dp_ref[...], w_ref[0])

        @at_norm_block(0)
        def _():
            d2d[3].wait_recv()
            acc[mine, :] += sibrcv[3].astype(F32)

        @at_norm_block(1)
        def _():
            tot = res_o[mine_o, :]
            for sv in range(3):
                ici_o[sv].wait_recv()
                tot = tot + rcv_o[sv].astype(F32)
            res_o[mine_o, :] = tot
            fin_o.start()

        @at_norm_block(2)
        def _():
            ici[1].wait_recv()
            acc[mine, :] += rcv[0].astype(F32)

        @at_norm_block(SEQ // TX - 2)
        def _():
            ici[2].wait_recv()
            acc[mine, :] += rcv[1].astype(F32)
            fin.start()
            store_w.start()
            fin_o.wait_recv()
            store_o.start()

        @at_norm_block(0)
        def _():
            dng[...] = jnp.zeros_like(dng)

        @pl.when(s >= norm_step)
        def _():
            blk = (s - norm_step) * nt + t
            dhv = dh[pl.ds(pl.multiple_of(blk * TX, TX), TX), :]
            xv = x_ref[...]
            r = lax.rsqrt(jnp.mean(xv * xv, axis=-1, keepdims=True) + EPS)
            xn = xv * r
            dng[...] += jnp.sum(dhv * xn, axis=0, keepdims=True)
            dxn = dhv * g_ref[...]
            gx_ref[...] = dx2_ref[...] + r * (dxn - xn * jnp.mean(dxn * xn, axis=-1, keepdims=True))

        @at_step(n_steps - 1, nt - 1)
        def _():
            dng_buf[me] = dng[...]
            for cp in dngs:
                cp.start()
            for m in range(1, N_DEV):
                copy(SEM_SMALL + m, sm_buf.at[0], sm_buf.at[0], sibling).wait_recv()
            tot = sm_buf[0]
            for d in range(1, N_DEV):
                tot = tot + sm_buf[d]
            osm_ref[...] = tot
            for m in range(1, N_DEV):
                copy(SEM_NORM + m, dng_buf.at[0], dng_buf.at[0], sibling).wait_recv()
            tot = dng_buf[0]
            for d in range(1, N_DEV):
                tot = tot + dng_buf[d]
            osm_ref[0:1, :] = tot
            fin.wait_recv()
            for cp in d2d + [d2d_o] + via + list(ici.values()) + ici_o + [fin, fin_o] + smalls + dngs:
                cp.wait_send()
            store_o.wait()
            store_w.wait()

    def shard_of(s, kr):
        order = jnp.where(s < N_SHARD, s, jnp.where(s < norm_step, s - N_SHARD, 3))
        return kr[0] ^ (3 - order)

    def h_map(s, t, kr):
        return (jnp.where(s < N_SHARD, t, nt - 1), 0)

    def dp_map(s, t, kr):
        return (jnp.where(s < norm_step, t, nt - 1), shard_of(s, kr))

    def w_map(s, t, kr):
        return (shard_of(jnp.maximum(s, N_SHARD), kr), 0, 0)

    def blk_map(s, t, kr):
        return (jnp.where(s < norm_step, 0, (s - norm_step) * nt + t), 0)

    hbm = pl.BlockSpec(memory_space=pl.ANY)
    grid_spec = pltpu.PrefetchScalarGridSpec(
        num_scalar_prefetch=1, grid=(n_steps, nt),
        in_specs=[pl.BlockSpec((TT, D_MODEL), h_map),
                  pl.BlockSpec((TT, SHARD_COLS), dp_map),
                  pl.BlockSpec((1, D_MODEL, SHARD_COLS), w_map),
                  pl.BlockSpec((N_SHARD, WO_ROWS, D_MODEL), lambda s, t, kr: (0, 0, 0),
                               pipeline_mode=pl.Buffered(1)),
                  pl.BlockSpec((TX, D_MODEL), blk_map),
                  pl.BlockSpec((TX, D_MODEL), blk_map),
                  pl.BlockSpec((1, D_MODEL), lambda s, t, kr: (0, 0)),
                  pl.BlockSpec((8, D_MODEL), lambda s, t, kr: (0, 0)),
                  pl.BlockSpec((8, D_MODEL), lambda s, t, kr: (0, 0))],
        out_specs=(pl.BlockSpec((TX, D_MODEL), blk_map), hbm, hbm,
                   pl.BlockSpec((8, D_MODEL), lambda s, t, kr: (0, 0))),
        scratch_shapes=[pltpu.VMEM((D_MODEL, SHARD_COLS), F32), pltpu.VMEM((SEQ, D_MODEL), F32),
                        pltpu.VMEM((N_SHARD, hw, SHARD_COLS), BF16), pltpu.VMEM((3, hw, SHARD_COLS), BF16),
                        pltpu.VMEM((N_SHARD, hw, SHARD_COLS), BF16), pltpu.VMEM((2, hw, SHARD_COLS), BF16),
                        pltpu.VMEM((2, hw // 2, SHARD_COLS), BF16),
                        pltpu.VMEM((N_SHARD, ho, D_MODEL), BF16), pltpu.VMEM((N_SHARD, ho, D_MODEL), BF16),
                        pltpu.VMEM((3, ho, D_MODEL), BF16), pltpu.VMEM((WO_ROWS, D_MODEL), F32),
                        pltpu.VMEM((N_DEV, 8, D_MODEL), F32), pltpu.VMEM((N_DEV, 1, D_MODEL), F32),
                        pltpu.VMEM((1, D_MODEL), F32),
                        pltpu.SemaphoreType.DMA((N_SEM_TAIL,)), pltpu.SemaphoreType.DMA((N_SEM_TAIL,)),
                        pltpu.SemaphoreType.DMA((2,))])
    return pl.pallas_call(
        body, name="bwd_tail", grid_spec=grid_spec,
        out_shape=(jax.ShapeDtypeStruct((SEQ, D_MODEL), F32),
                   jax.ShapeDtypeStruct((D_MODEL, SHARD_COLS), F32),
                   jax.ShapeDtypeStruct((WO_ROWS, D_MODEL), F32),
                   jax.ShapeDtypeStruct((8, D_MODEL), F32)),
        compiler_params=pltpu.CompilerParams(dimension_semantics=("arbitrary", "arbitrary"),
                                             vmem_limit_bytes=61 * 1024 * 1024, collective_id=COLLECTIVE_TAIL),
    )(kidx, h, dproj, wg, gwo, x2d, dx2, g1, small_a, small_b)


def _adam_update(w, g, m, v):
    nm = ADAM_B1 * m + (1.0 - ADAM_B1) * g
    nv = ADAM_B2 * v + (1.0 - ADAM_B2) * (g * g)
    m_hat = nm / (1.0 - ADAM_B1 ** ADAM_STEP)
    v_hat = nv / (1.0 - ADAM_B2 ** ADAM_STEP)
    return -ADAM_LR * (m_hat / (jnp.sqrt(v_hat) + ADAM_EPS) + ADAM_WD * w), nm, nv


def _adamw_all(tot, g_w_in, g_w_out, big, small, grad_x):
    n = len(small)
    rows = WO_ROWS
    steps = D_MODEL // rows

    def body(tot_ref, *refs):
        gx_ref, gx_out = refs[2 + 3 * (2 + n)], refs[-1]
        gx_out[...] = gx_ref[...]
        ins, outs = refs[:2 + 3 * (2 + n)], refs[3 + 3 * (2 + n):-1]
        g_refs, wmv = ins[:2], ins[2:]
        loss_ref, quads = outs[0], outs[1:]

        def update(j, g):
            w_ref, m_ref, v_ref = wmv[3 * j:3 * j + 3]
            g_ref, d_ref, nm_ref, nv_ref = quads[4 * j:4 * j + 4]
            g_ref[...] = g
            d_ref[...], nm_ref[...], nv_ref[...] = _adam_update(w_ref[...], g, m_ref[...], v_ref[...])

        update(0, g_refs[0][...])

        @pl.when(pl.program_id(0) == 0)
        def _():
            update(1, g_refs[1][...])
            k = 2 * lax.axis_index("x") + lax.axis_index("y")
            mine = pl.ds(pl.multiple_of(k * HEAD, HEAD), HEAD)
            loss_ref[...] = tot_ref[7:8, 0:1]
            grads = [tot_ref[0:1, :], tot_ref[1:2, :], tot_ref[2:3, 0:D_HGRN], tot_ref[2:3, D_HGRN:],
                     jnp.concatenate([tot_ref[3:4, 0:D_HGRN], tot_ref[3:4, D_HGRN:]], axis=0),
                     jnp.concatenate([tot_ref[4 + tap:5 + tap, mine] for tap in range(3)], axis=1)]
            for j, g in enumerate(grads):
                update(2 + j, g)

    whole = lambda a: pl.BlockSpec(a.shape, lambda i: (0, 0))
    blk = pl.BlockSpec((rows, SHARD_COLS), lambda i: (i, 0))
    arrays = [a for triple in big + small for a in triple]
    in_specs = ([whole(tot), blk, whole(g_w_out)] + [blk] * 3 + [whole(a) for a in arrays[3:]])
    shapes = [big[0][0], big[1][0]] + [w for w, _, _ in small]
    out_shape = (jax.ShapeDtypeStruct((1, 1), F32),) + tuple(
        jax.ShapeDtypeStruct(w.shape, F32) for w in shapes for _ in range(4))
    out_specs = (pl.BlockSpec((1, 1), lambda i: (0, 0)),) + (blk,) * 4 + tuple(
        whole(w) for w in shapes[1:] for _ in range(4))
    gx_blk = pl.BlockSpec((SEQ // steps, D_MODEL), lambda i: (i, 0))
    outs = pl.pallas_call(
        body, name="adamw_all", grid=(steps,),
        out_shape=out_shape + (jax.ShapeDtypeStruct(grad_x.shape, F32),),
        in_specs=in_specs + [gx_blk], out_specs=out_specs + (gx_blk,),
        compiler_params=pltpu.CompilerParams(dimension_semantics=("arbitrary",), vmem_limit_bytes=VMEM_LIMIT),
    )(tot, g_w_in, g_w_out, *arrays, grad_x)
    return [outs[0]] + [outs[1 + 4 * j:5 + 4 * j] for j in range(2 + n)] + [outs[-1]]


def _local_step(x2d, tgt, proj, lb_logits, cw, ga, gcn, w_out, gf):
    g64 = _group_matrix(HEAD, CONV_GROUP)
    aux, states, dx2, dmixed, gwo, part_out = _mix_out(proj, lb_logits, cw, ga, gcn, g64, w_out, x2d, gf, tgt)
    dproj, part_mix = _mix_bwd(proj, aux, states, dmixed, lb_logits, cw, ga, gcn, g64)
    return dproj, dx2, gwo.reshape(N_SHARD, WO_ROWS, D_MODEL), part_out, part_mix


def kernel(x, norm_gain, w_in, lb_logits, conv_w, hgrn_norm_gain, conv_norm_gain, w_out, final_norm_gain, loss_target, m_norm_gain, m_w_in, m_lb_logits, m_conv_w, m_hgrn_norm_gain, m_conv_norm_gain, m_w_out, m_final_norm_gain, v_norm_gain, v_w_in, v_lb_logits, v_conv_w, v_hgrn_norm_gain, v_conv_norm_gain, v_w_out, v_final_norm_gain):
    k = 2 * lax.axis_index("x") + lax.axis_index("y")
    kidx = jnp.reshape(k, (1,)).astype(jnp.int32)
    row = lambda a: a.reshape(1, D_MODEL)
    taps = lambda a: a.reshape(1, 3 * HEAD)
    h, proj, wg, cw = _gather_proj(kidx, x[0], norm_gain, w_in, taps(conv_w))
    dproj, dx2, gwo, part_out, part_mix = _local_step(
        x[0], loss_target[0], proj, lb_logits, cw, hgrn_norm_gain, conv_norm_gain, w_out, row(final_norm_gain))
    rgrad_x, rg_w_in, rg_w_out, tot = _bwd_tail(kidx, h, dproj, wg, gwo, x[0], dx2, norm_gain, part_out, part_mix)

    (loss, (g_w_in, d_w_in, nm_w_in, nv_w_in), (g_w_out, d_w_out, nm_w_out, nv_w_out),
     (g_norm_gain, d_ng, nm_ng, nv_ng), (g_final, d_fg, nm_fg, nv_fg), (g_hgrn, d_hg, nm_hg, nv_hg),
     (g_convn, d_cg, nm_cg, nv_cg), (g_lb, d_lb, nm_lb, nv_lb), (g_conv_w, d_cw, nm_cw, nv_cw),
     grad_x) = _adamw_all(
        tot, rg_w_in, rg_w_out,
        [(w_in[0], m_w_in[0], v_w_in[0]), (w_out[0], m_w_out[0], v_w_out[0])],
        [(norm_gain, m_norm_gain, v_norm_gain),
         (row(final_norm_gain), row(m_final_norm_gain), row(v_final_norm_gain)),
         (hgrn_norm_gain, m_hgrn_norm_gain, v_hgrn_norm_gain),
         (conv_norm_gain, m_conv_norm_gain, v_conv_norm_gain),
         (lb_logits, m_lb_logits, v_lb_logits),
         (taps(conv_w), taps(m_conv_w), taps(v_conv_w))],
        rgrad_x)
    flat = lambda a: a.reshape(D_MODEL)
    untap = lambda a: a.reshape(1, 3, HEAD)
    return (loss.reshape(()), grad_x[None],
            g_norm_gain, g_w_in[None], g_lb, untap(g_conv_w), g_hgrn, g_convn, g_w_out[None], flat(g_final),
            d_ng, d_w_in[None], d_lb, untap(d_cw), d_hg, d_cg, d_w_out[None], flat(d_fg),
            nm_ng, nm_w_in[None], nm_lb, untap(nm_cw), nm_hg, nm_cg, nm_w_out[None], flat(nm_fg),
            nv_ng, nv_w_in[None], nv_lb, untap(nv_cw), nv_hg, nv_cg, nv_w_out[None], flat(nv_fg))
```
